```python
import jax, jax.numpy as jnp
from jax import lax
import numpy as np

D_MODEL = 1024
BATCH = 8
SEQ = 4096
DEPTH = 2

N_A_LAYERS = DEPTH // 2
N_B_LAYERS = DEPTH - N_A_LAYERS

CHUNK = 128
SGU_WIDTH = D_MODEL
SGU_GROUPS = 8
SGU_GROUP_DIM = SGU_WIDTH // SGU_GROUPS

HEAD_DIM = 64
N_Q_HEADS = D_MODEL // HEAD_DIM
N_KV_HEADS = 4
Q_PER_KV = N_Q_HEADS // N_KV_HEADS
WINDOW = 128
BLOCK = WINDOW

D_FF = ((8 * D_MODEL // 3 + 255) // 256) * 256
CONV_WIDTH = 3

EPS = 1e-6

kernel_name = "yoco_gmlp_swa_sink_convffn"


def rms_norm(x, g):
    xf = x.astype(jnp.float32)
    y = xf * lax.rsqrt(jnp.mean(xf * xf, axis=-1, keepdims=True) + EPS)
    return (y * g.astype(jnp.float32)).astype(x.dtype)


def alibi_slopes(n_heads):
    h = jnp.arange(1, n_heads + 1, dtype=jnp.float32)
    return jnp.exp2(-8.0 * h / n_heads)


def causal_depthwise_conv(h, w, b):
    c = h.shape[-1]
    y = lax.conv_general_dilated(
        h, w[:, None, :].astype(h.dtype), window_strides=(1,),
        padding=[(CONV_WIDTH - 1, 0)], dimension_numbers=("NWC", "WIO", "NWC"),
        feature_group_count=c)
    return y + b.astype(h.dtype)


def conv_ffn(x, g_norm, w_in, conv_w, conv_b, w_out):
    h = rms_norm(x, g_norm)
    hu = causal_depthwise_conv(h @ w_in, conv_w, conv_b)
    gate, up = jnp.split(hu, 2, axis=-1)
    return (jax.nn.silu(gate) * up) @ w_out


def chunked_sgu(x, g_norm, w_in, g_v, w_s, b_s, w_out):
    b, s, _ = x.shape
    nc = s // CHUNK
    h = rms_norm(x, g_norm)
    z = jax.nn.gelu(h @ w_in, approximate=False)
    u, v = jnp.split(z, 2, axis=-1)
    v = rms_norm(v, g_v).reshape(b, nc, CHUNK, SGU_GROUPS, SGU_GROUP_DIM)
    tril = jnp.tril(jnp.ones((CHUNK, CHUNK), dtype=bool))
    w_causal = jnp.where(tril[None], w_s, 0).astype(v.dtype)
    sv = jnp.einsum("gts,bnsgc->bntgc", w_causal, v) + b_s.T.astype(v.dtype)[None, None, :, :, None]
    y = u.reshape(b, nc, CHUNK, SGU_GROUPS, SGU_GROUP_DIM) * sv
    return y.reshape(b, s, SGU_WIDTH) @ w_out


def to_band(t):
    b, s, hk, hd = t.shape
    blk = t.reshape(b, s // BLOCK, BLOCK, hk, hd)
    prev = jnp.pad(blk[:, :-1], ((0, 0), (1, 0), (0, 0), (0, 0), (0, 0)))
    return jnp.concatenate([prev, blk], axis=2)


def shared_kv(x, g_kv, w_kv, g_k):
    b, s, _ = x.shape
    kv = rms_norm(x, g_kv) @ w_kv
    k, v = jnp.split(kv, 2, axis=-1)
    k = rms_norm(k.reshape(b, s, N_KV_HEADS, HEAD_DIM), g_k)
    v = v.reshape(b, s, N_KV_HEADS, HEAD_DIM)
    return to_band(k), to_band(v)


def swa_sink_attention(x, k_band, v_band, g_norm, w_q, g_q, sinks, w_o):
    b, s, _ = x.shape
    nb = s // BLOCK
    q = (rms_norm(x, g_norm) @ w_q).reshape(b, s, N_Q_HEADS, HEAD_DIM)
    q = rms_norm(q, g_q).reshape(b, nb, BLOCK, N_KV_HEADS, Q_PER_KV, HEAD_DIM)
    logits = jnp.einsum("bntkgd,bnskd->bnkgts", q, k_band).astype(jnp.float32)
    logits = logits * (HEAD_DIM ** -0.5)
    t_idx = jnp.arange(BLOCK)[:, None]
    j_idx = jnp.arange(2 * BLOCK)[None, :]
    dist = t_idx + BLOCK - j_idx
    in_window = (dist >= 0) & (dist < WINDOW)
    first_blk = (jnp.arange(nb) == 0)[:, None, None] & (j_idx < BLOCK)[None]
    mask = in_window[None] & ~first_blk
    slopes = alibi_slopes(N_Q_HEADS).reshape(N_KV_HEADS, Q_PER_KV)
    bias = -slopes[:, :, None, None] * dist.astype(jnp.float32)[None, None]
    logits = jnp.where(mask[None, :, None, None], logits + bias[None, None], -jnp.inf)
    sink = sinks.astype(jnp.float32).reshape(N_KV_HEADS, Q_PER_KV)[None, None, :, :, None, None]
    m = jnp.maximum(jnp.max(logits, axis=-1, keepdims=True), sink)
    p = jnp.exp(logits - m)
    denom = jnp.sum(p, axis=-1, keepdims=True) + jnp.exp(sink - m)
    p = (p / denom).astype(v_band.dtype)
    o = jnp.einsum("bnkgts,bnskd->bntkgd", p, v_band).reshape(b, s, N_Q_HEADS * HEAD_DIM)
    return o @ w_o


def _fwd_setup_inputs(seed: int = 0) -> dict:
    key = jax.random.key(seed)
    ks = jax.random.split(key, 20)

    def nrm(k, shape, scale):
        return jax.random.normal(k, shape, jnp.float32) * scale

    def gain(k, shape):
        return 1.0 + 0.1 * jax.random.normal(k, shape, jnp.float32)

    qd = N_Q_HEADS * HEAD_DIM
    return {
        "x": nrm(ks[0], (BATCH, SEQ, D_MODEL), 1.0),
        "a_norm": gain(ks[1], (N_A_LAYERS, D_MODEL)),
        "a_w_in": nrm(ks[2], (N_A_LAYERS, D_MODEL, 2 * SGU_WIDTH), D_MODEL ** -0.5),
        "a_v_norm": gain(ks[3], (N_A_LAYERS, SGU_WIDTH)),
        "a_w_s": nrm(ks[4], (N_A_LAYERS, SGU_GROUPS, CHUNK, CHUNK), CHUNK ** -0.5),
        "a_b_s": gain(ks[5], (N_A_LAYERS, SGU_GROUPS, CHUNK)),
        "a_w_out": nrm(ks[6], (N_A_LAYERS, SGU_WIDTH, D_MODEL), SGU_WIDTH ** -0.5),
        "f_norm": gain(ks[7], (DEPTH, D_MODEL)),
        "f_w_in": nrm(ks[8], (DEPTH, D_MODEL, 2 * D_FF), D_MODEL ** -0.5),
        "f_conv_w": nrm(ks[9], (DEPTH, CONV_WIDTH, 2 * D_FF), CONV_WIDTH ** -0.5),
        "f_conv_b": nrm(ks[10], (DEPTH, 2 * D_FF), 0.02),
        "f_w_out": nrm(ks[11], (DEPTH, D_FF, D_MODEL), D_FF ** -0.5),
        "kv_norm": gain(ks[12], (D_MODEL,)),
        "w_kv": nrm(ks[13], (D_MODEL, 2 * N_KV_HEADS * HEAD_DIM), D_MODEL ** -0.5),
        "k_norm": gain(ks[14], (HEAD_DIM,)),
        "b_norm": gain(ks[15], (N_B_LAYERS, D_MODEL)),
        "b_w_q": nrm(ks[16], (N_B_LAYERS, D_MODEL, qd), D_MODEL ** -0.5),
        "b_q_norm": gain(ks[17], (N_B_LAYERS, HEAD_DIM)),
        "b_sinks": nrm(ks[18], (N_B_LAYERS, N_Q_HEADS), 1.0),
        "b_w_o": nrm(ks[19], (N_B_LAYERS, qd, D_MODEL), qd ** -0.5),
    }


def _fwd_reference(x, a_norm, a_w_in, a_v_norm, a_w_s, a_b_s, a_w_out,
              f_norm, f_w_in, f_conv_w, f_conv_b, f_w_out,
              kv_norm, w_kv, k_norm,
              b_norm, b_w_q, b_q_norm, b_sinks, b_w_o):
    k_band = None
    v_band = None
    for layer in range(DEPTH):
        if layer < N_A_LAYERS:
            i = layer
            x = x + chunked_sgu(x, a_norm[i], a_w_in[i], a_v_norm[i], a_w_s[i], a_b_s[i], a_w_out[i])
        else:
            j = layer - N_A_LAYERS
            x = x + swa_sink_attention(x, k_band, v_band, b_norm[j], b_w_q[j], b_q_norm[j],
                                       b_sinks[j], b_w_o[j])
        x = x + conv_ffn(x, f_norm[layer], f_w_in[layer], f_conv_w[layer], f_conv_b[layer],
                         f_w_out[layer])
        if layer == N_A_LAYERS - 1:
            k_band, v_band = shared_kv(x, kv_norm, w_kv, k_norm)
    return x


import jax as _jax
import jax.numpy as _jnp

TWIN_FORMAT = 'train_step'
FWD_PARAMS = ['x', 'a_norm', 'a_w_in', 'a_v_norm', 'a_w_s', 'a_b_s', 'a_w_out', 'f_norm', 'f_w_in', 'f_conv_w', 'f_conv_b', 'f_w_out', 'kv_norm', 'w_kv', 'k_norm', 'b_norm', 'b_w_q', 'b_q_norm', 'b_sinks', 'b_w_o']
TWIN_WEIGHTS = ['a_norm', 'a_w_in', 'a_v_norm', 'a_w_s', 'a_b_s', 'a_w_out', 'f_norm', 'f_w_in', 'f_conv_w', 'f_conv_b', 'f_w_out', 'kv_norm', 'w_kv', 'k_norm', 'b_norm', 'b_w_q', 'b_q_norm', 'b_sinks', 'b_w_o']
TWIN_DIFF_INPUT = 'x'
TWIN_INPUTS = ['x', 'a_norm', 'a_w_in', 'a_v_norm', 'a_w_s', 'a_b_s', 'a_w_out', 'f_norm', 'f_w_in', 'f_conv_w', 'f_conv_b', 'f_w_out', 'kv_norm', 'w_kv', 'k_norm', 'b_norm', 'b_w_q', 'b_q_norm', 'b_sinks', 'b_w_o', 'loss_target', 'm_a_norm', 'm_a_w_in', 'm_a_v_norm', 'm_a_w_s', 'm_a_b_s', 'm_a_w_out', 'm_f_norm', 'm_f_w_in', 'm_f_conv_w', 'm_f_conv_b', 'm_f_w_out', 'm_kv_norm', 'm_w_kv', 'm_k_norm', 'm_b_norm', 'm_b_w_q', 'm_b_q_norm', 'm_b_sinks', 'm_b_w_o', 'v_a_norm', 'v_a_w_in', 'v_a_v_norm', 'v_a_w_s', 'v_a_b_s', 'v_a_w_out', 'v_f_norm', 'v_f_w_in', 'v_f_conv_w', 'v_f_conv_b', 'v_f_w_out', 'v_kv_norm', 'v_w_kv', 'v_k_norm', 'v_b_norm', 'v_b_w_q', 'v_b_q_norm', 'v_b_sinks', 'v_b_w_o']
TWIN_OUTPUTS = ['loss', 'grad_x', 'grad_a_norm', 'grad_a_w_in', 'grad_a_v_norm', 'grad_a_w_s', 'grad_a_b_s', 'grad_a_w_out', 'grad_f_norm', 'grad_f_w_in', 'grad_f_conv_w', 'grad_f_conv_b', 'grad_f_w_out', 'grad_kv_norm', 'grad_w_kv', 'grad_k_norm', 'grad_b_norm', 'grad_b_w_q', 'grad_b_q_norm', 'grad_b_sinks', 'grad_b_w_o', 'delta_a_norm', 'delta_a_w_in', 'delta_a_v_norm', 'delta_a_w_s', 'delta_a_b_s', 'delta_a_w_out', 'delta_f_norm', 'delta_f_w_in', 'delta_f_conv_w', 'delta_f_conv_b', 'delta_f_w_out', 'delta_kv_norm', 'delta_w_kv', 'delta_k_norm', 'delta_b_norm', 'delta_b_w_q', 'delta_b_q_norm', 'delta_b_sinks', 'delta_b_w_o', 'new_m_a_norm', 'new_m_a_w_in', 'new_m_a_v_norm', 'new_m_a_w_s', 'new_m_a_b_s', 'new_m_a_w_out', 'new_m_f_norm', 'new_m_f_w_in', 'new_m_f_conv_w', 'new_m_f_conv_b', 'new_m_f_w_out', 'new_m_kv_norm', 'new_m_w_kv', 'new_m_k_norm', 'new_m_b_norm', 'new_m_b_w_q', 'new_m_b_q_norm', 'new_m_b_sinks', 'new_m_b_w_o', 'new_v_a_norm', 'new_v_a_w_in', 'new_v_a_v_norm', 'new_v_a_w_s', 'new_v_a_b_s', 'new_v_a_w_out', 'new_v_f_norm', 'new_v_f_w_in', 'new_v_f_conv_w', 'new_v_f_conv_b', 'new_v_f_w_out', 'new_v_kv_norm', 'new_v_w_kv', 'new_v_k_norm', 'new_v_b_norm', 'new_v_b_w_q', 'new_v_b_q_norm', 'new_v_b_sinks', 'new_v_b_w_o']
TWIN_LEAF_KINDS = {'loss': 'loss', 'grad_x': 'grad_x', 'grad_a_norm': 'grad_w', 'grad_a_w_in': 'grad_w', 'grad_a_v_norm': 'grad_w', 'grad_a_w_s': 'grad_w', 'grad_a_b_s': 'grad_w', 'grad_a_w_out': 'grad_w', 'grad_f_norm': 'grad_w', 'grad_f_w_in': 'grad_w', 'grad_f_conv_w': 'grad_w', 'grad_f_conv_b': 'grad_w', 'grad_f_w_out': 'grad_w', 'grad_kv_norm': 'grad_w', 'grad_w_kv': 'grad_w', 'grad_k_norm': 'grad_w', 'grad_b_norm': 'grad_w', 'grad_b_w_q': 'grad_w', 'grad_b_q_norm': 'grad_w', 'grad_b_sinks': 'grad_w', 'grad_b_w_o': 'grad_w', 'delta_a_norm': 'delta_w', 'delta_a_w_in': 'delta_w', 'delta_a_v_norm': 'delta_w', 'delta_a_w_s': 'delta_w', 'delta_a_b_s': 'delta_w', 'delta_a_w_out': 'delta_w', 'delta_f_norm': 'delta_w', 'delta_f_w_in': 'delta_w', 'delta_f_conv_w': 'delta_w', 'delta_f_conv_b': 'delta_w', 'delta_f_w_out': 'delta_w', 'delta_kv_norm': 'delta_w', 'delta_w_kv': 'delta_w', 'delta_k_norm': 'delta_w', 'delta_b_norm': 'delta_w', 'delta_b_w_q': 'delta_w', 'delta_b_q_norm': 'delta_w', 'delta_b_sinks': 'delta_w', 'delta_b_w_o': 'delta_w', 'new_m_a_norm': 'new_m', 'new_m_a_w_in': 'new_m', 'new_m_a_v_norm': 'new_m', 'new_m_a_w_s': 'new_m', 'new_m_a_b_s': 'new_m', 'new_m_a_w_out': 'new_m', 'new_m_f_norm': 'new_m', 'new_m_f_w_in': 'new_m', 'new_m_f_conv_w': 'new_m', 'new_m_f_conv_b': 'new_m', 'new_m_f_w_out': 'new_m', 'new_m_kv_norm': 'new_m', 'new_m_w_kv': 'new_m', 'new_m_k_norm': 'new_m', 'new_m_b_norm': 'new_m', 'new_m_b_w_q': 'new_m', 'new_m_b_q_norm': 'new_m', 'new_m_b_sinks': 'new_m', 'new_m_b_w_o': 'new_m', 'new_v_a_norm': 'new_v', 'new_v_a_w_in': 'new_v', 'new_v_a_v_norm': 'new_v', 'new_v_a_w_s': 'new_v', 'new_v_a_b_s': 'new_v', 'new_v_a_w_out': 'new_v', 'new_v_f_norm': 'new_v', 'new_v_f_w_in': 'new_v', 'new_v_f_conv_w': 'new_v', 'new_v_f_conv_b': 'new_v', 'new_v_f_w_out': 'new_v', 'new_v_kv_norm': 'new_v', 'new_v_w_kv': 'new_v', 'new_v_k_norm': 'new_v', 'new_v_b_norm': 'new_v', 'new_v_b_w_q': 'new_v', 'new_v_b_q_norm': 'new_v', 'new_v_b_sinks': 'new_v', 'new_v_b_w_o': 'new_v'}


def _forward(args):
    return _fwd_reference(*[args[k] for k in FWD_PARAMS])


def _output_shape():
    def fwd():
        inp = _fwd_setup_inputs(0)
        return _fwd_reference(*[inp[k] for k in FWD_PARAMS])
    out = _jax.eval_shape(fwd)
    return out.shape, out.dtype

N_MICROBATCH = 1
ADAM_LR = 0.001
ADAM_B1 = 0.9
ADAM_B2 = 0.999
ADAM_EPS = 1e-08
ADAM_WD = 0.01
ADAM_STEP = 10
PER_EXAMPLE_BATCH_AXIS = {'x': 0, 'loss_target': 0}
SHARED_INPUTS = []
_WEIGHT_DTYPES = {'a_norm': _jnp.float32, 'a_w_in': _jnp.float32, 'a_v_norm': _jnp.float32, 'a_w_s': _jnp.float32, 'a_b_s': _jnp.float32, 'a_w_out': _jnp.float32, 'f_norm': _jnp.float32, 'f_w_in': _jnp.float32, 'f_conv_w': _jnp.float32, 'f_conv_b': _jnp.float32, 'f_w_out': _jnp.float32, 'kv_norm': _jnp.float32, 'w_kv': _jnp.float32, 'k_norm': _jnp.float32, 'b_norm': _jnp.float32, 'b_w_q': _jnp.float32, 'b_q_norm': _jnp.float32, 'b_sinks': _jnp.float32, 'b_w_o': _jnp.float32}
MOMENT_SCALE = {'a_norm': 2.610177e+01, 'a_w_in': 7.593213e-01, 'a_v_norm': 7.132930e+00, 'a_w_s': 4.906190e+00, 'a_b_s': 1.569850e+01, 'a_w_out': 6.980365e+00, 'f_norm': 2.620109e+01, 'f_w_in': 1.424032e+00, 'f_conv_w': 4.188754e+00, 'f_conv_b': 4.282254e+00, 'f_w_out': 1.147220e+00, 'kv_norm': 5.723793e+00, 'w_kv': 5.726270e+00, 'k_norm': 2.013821e+01, 'b_norm': 2.900734e-01, 'b_w_q': 2.888369e-01, 'b_q_norm': 2.042196e+01, 'b_sinks': 4.170547e+01, 'b_w_o': 3.160540e+00}


def _to_microbatches(a, axis):
    t = _jnp.moveaxis(a, axis, 0)
    t = t.reshape((N_MICROBATCH, t.shape[0] // N_MICROBATCH) + t.shape[1:])
    return _jnp.moveaxis(t, 1, axis + 1)


def setup_inputs(seed: int = 0) -> dict:
    inp = _fwd_setup_inputs(seed)
    key = _jax.random.fold_in(_jax.random.key(seed), 7919)
    shape, _ = _output_shape()
    out = dict(inp)
    out["loss_target"] = _jax.random.normal(_jax.random.fold_in(key, 0), shape, _jnp.float32)
    for i, name in enumerate(TWIN_WEIGHTS):
        w = inp[name].astype(_jnp.float32)
        if MOMENT_SCALE is None:
            s = _jnp.sqrt(_jnp.mean(_jnp.square(w)) + 1e-30)
        else:
            s = MOMENT_SCALE[name]
        km, kv = _jax.random.split(_jax.random.fold_in(key, i + 1))
        out[name] = w
        out["m_" + name] = s * _jax.random.normal(km, w.shape, _jnp.float32)
        out["v_" + name] = (s * s) * _jax.random.uniform(kv, w.shape, _jnp.float32, 0.5, 1.5)
    if N_MICROBATCH > 1:
        for name, axis in PER_EXAMPLE_BATCH_AXIS.items():
            out[name] = _to_microbatches(out[name], axis)
    return {'x': out['x'], 'a_norm': out['a_norm'], 'a_w_in': out['a_w_in'], 'a_v_norm': out['a_v_norm'], 'a_w_s': out['a_w_s'], 'a_b_s': out['a_b_s'], 'a_w_out': out['a_w_out'], 'f_norm': out['f_norm'], 'f_w_in': out['f_w_in'], 'f_conv_w': out['f_conv_w'], 'f_conv_b': out['f_conv_b'], 'f_w_out': out['f_w_out'], 'kv_norm': out['kv_norm'], 'w_kv': out['w_kv'], 'k_norm': out['k_norm'], 'b_norm': out['b_norm'], 'b_w_q': out['b_w_q'], 'b_q_norm': out['b_q_norm'], 'b_sinks': out['b_sinks'], 'b_w_o': out['b_w_o'], 'loss_target': out['loss_target'], 'm_a_norm': out['m_a_norm'], 'm_a_w_in': out['m_a_w_in'], 'm_a_v_norm': out['m_a_v_norm'], 'm_a_w_s': out['m_a_w_s'], 'm_a_b_s': out['m_a_b_s'], 'm_a_w_out': out['m_a_w_out'], 'm_f_norm': out['m_f_norm'], 'm_f_w_in': out['m_f_w_in'], 'm_f_conv_w': out['m_f_conv_w'], 'm_f_conv_b': out['m_f_conv_b'], 'm_f_w_out': out['m_f_w_out'], 'm_kv_norm': out['m_kv_norm'], 'm_w_kv': out['m_w_kv'], 'm_k_norm': out['m_k_norm'], 'm_b_norm': out['m_b_norm'], 'm_b_w_q': out['m_b_w_q'], 'm_b_q_norm': out['m_b_q_norm'], 'm_b_sinks': out['m_b_sinks'], 'm_b_w_o': out['m_b_w_o'], 'v_a_norm': out['v_a_norm'], 'v_a_w_in': out['v_a_w_in'], 'v_a_v_norm': out['v_a_v_norm'], 'v_a_w_s': out['v_a_w_s'], 'v_a_b_s': out['v_a_b_s'], 'v_a_w_out': out['v_a_w_out'], 'v_f_norm': out['v_f_norm'], 'v_f_w_in': out['v_f_w_in'], 'v_f_conv_w': out['v_f_conv_w'], 'v_f_conv_b': out['v_f_conv_b'], 'v_f_w_out': out['v_f_w_out'], 'v_kv_norm': out['v_kv_norm'], 'v_w_kv': out['v_w_kv'], 'v_k_norm': out['v_k_norm'], 'v_b_norm': out['v_b_norm'], 'v_b_w_q': out['v_b_w_q'], 'v_b_q_norm': out['v_b_q_norm'], 'v_b_sinks': out['v_b_sinks'], 'v_b_w_o': out['v_b_w_o']}


def _loss(weights, diff, rest, loss_target):
    with _jax.named_scope("forward"):
        args = {**rest, TWIN_DIFF_INPUT: diff, **{k: w.astype(_WEIGHT_DTYPES[k]) for k, w in weights.items()}}
        y = _forward(args)
    with _jax.named_scope("loss_head"):
        err = _jnp.square(y.astype(_jnp.float32) - loss_target)
        return 0.5 * _jnp.sum(_jnp.mean(err, axis=-1)) if err.ndim else 0.5 * err


def _adamw(w, g, m, v):
    m = ADAM_B1 * m + (1.0 - ADAM_B1) * g
    v = ADAM_B2 * v + (1.0 - ADAM_B2) * _jnp.square(g)
    m_hat = m / (1.0 - ADAM_B1 ** ADAM_STEP)
    v_hat = v / (1.0 - ADAM_B2 ** ADAM_STEP)
    delta = -ADAM_LR * (m_hat / (_jnp.sqrt(v_hat) + ADAM_EPS) + ADAM_WD * w)
    return delta, m, v


def reference(x, a_norm, a_w_in, a_v_norm, a_w_s, a_b_s, a_w_out, f_norm, f_w_in, f_conv_w, f_conv_b, f_w_out, kv_norm, w_kv, k_norm, b_norm, b_w_q, b_q_norm, b_sinks, b_w_o, loss_target, m_a_norm, m_a_w_in, m_a_v_norm, m_a_w_s, m_a_b_s, m_a_w_out, m_f_norm, m_f_w_in, m_f_conv_w, m_f_conv_b, m_f_w_out, m_kv_norm, m_w_kv, m_k_norm, m_b_norm, m_b_w_q, m_b_q_norm, m_b_sinks, m_b_w_o, v_a_norm, v_a_w_in, v_a_v_norm, v_a_w_s, v_a_b_s, v_a_w_out, v_f_norm, v_f_w_in, v_f_conv_w, v_f_conv_b, v_f_w_out, v_kv_norm, v_w_kv, v_k_norm, v_b_norm, v_b_w_q, v_b_q_norm, v_b_sinks, v_b_w_o):
    given = dict(x=x, a_norm=a_norm, a_w_in=a_w_in, a_v_norm=a_v_norm, a_w_s=a_w_s, a_b_s=a_b_s, a_w_out=a_w_out, f_norm=f_norm, f_w_in=f_w_in, f_conv_w=f_conv_w, f_conv_b=f_conv_b, f_w_out=f_w_out, kv_norm=kv_norm, w_kv=w_kv, k_norm=k_norm, b_norm=b_norm, b_w_q=b_w_q, b_q_norm=b_q_norm, b_sinks=b_sinks, b_w_o=b_w_o, loss_target=loss_target, m_a_norm=m_a_norm, m_a_w_in=m_a_w_in, m_a_v_norm=m_a_v_norm, m_a_w_s=m_a_w_s, m_a_b_s=m_a_b_s, m_a_w_out=m_a_w_out, m_f_norm=m_f_norm, m_f_w_in=m_f_w_in, m_f_conv_w=m_f_conv_w, m_f_conv_b=m_f_conv_b, m_f_w_out=m_f_w_out, m_kv_norm=m_kv_norm, m_w_kv=m_w_kv, m_k_norm=m_k_norm, m_b_norm=m_b_norm, m_b_w_q=m_b_w_q, m_b_q_norm=m_b_q_norm, m_b_sinks=m_b_sinks, m_b_w_o=m_b_w_o, v_a_norm=v_a_norm, v_a_w_in=v_a_w_in, v_a_v_norm=v_a_v_norm, v_a_w_s=v_a_w_s, v_a_b_s=v_a_b_s, v_a_w_out=v_a_w_out, v_f_norm=v_f_norm, v_f_w_in=v_f_w_in, v_f_conv_w=v_f_conv_w, v_f_conv_b=v_f_conv_b, v_f_w_out=v_f_w_out, v_kv_norm=v_kv_norm, v_w_kv=v_w_kv, v_k_norm=v_k_norm, v_b_norm=v_b_norm, v_b_w_q=v_b_w_q, v_b_q_norm=v_b_q_norm, v_b_sinks=v_b_sinks, v_b_w_o=v_b_w_o)
    weights = {n: given[n] for n in TWIN_WEIGHTS}
    shared = {n: given[n] for n in SHARED_INPUTS}
    per_example = {n: given[n] for n in ['x']}
    grad_fn = _jax.value_and_grad(_loss, argnums=(0, 1))

    def one_microbatch(ex, loss_target):
        ex = dict(ex)
        diff = ex.pop(TWIN_DIFF_INPUT)
        return grad_fn(weights, diff, {**shared, **ex}, loss_target)

    if N_MICROBATCH == 1:
        loss, (grad_w, grad_x) = one_microbatch(per_example, given["loss_target"])
    else:
        def body(carry, xs):
            loss_sum, grad_sum = carry
            l_k, (gw_k, gx_k) = one_microbatch(xs[0], xs[1])
            with _jax.named_scope("update"):
                return (loss_sum + l_k, _jax.tree.map(_jnp.add, grad_sum, gw_k)), gx_k

        init = (_jnp.zeros((), _jnp.float32), _jax.tree.map(_jnp.zeros_like, weights))
        (loss, grad_w), grad_x = _jax.lax.scan(body, init, (per_example, given["loss_target"]))
    with _jax.named_scope("update"):
        delta_w, new_m, new_v = {}, {}, {}
        for n in TWIN_WEIGHTS:
            delta_w[n], new_m[n], new_v[n] = _adamw(weights[n], grad_w[n], given["m_" + n], given["v_" + n])
    return (loss, grad_x, *[grad_w[n] for n in TWIN_WEIGHTS], *[delta_w[n] for n in TWIN_WEIGHTS],
            *[new_m[n] for n in TWIN_WEIGHTS], *[new_v[n] for n in TWIN_WEIGHTS])
```

```python
import functools

import jax
import jax.numpy as jnp
from jax import lax
from jax.experimental import pallas as pl
from jax.experimental.pallas import tpu as pltpu

F32 = jnp.float32
BF16 = jnp.bfloat16
EPS = 1e-6
D_MODEL = 1024
CHUNK = 128
N_GROUPS = 8
N_SHARDS = 8
HEAD_DIM = 64
N_Q_HEADS = 16
N_KV_HEADS = 4
D_FF = 2816
FF_SHARD = 2 * D_FF // N_SHARDS
LANES = 128
NEG_BIG = -1e30
ADAM_LR = 0.001
ADAM_B1 = 0.9
ADAM_B2 = 0.999
ADAM_EPS = 1e-08
ADAM_WD = 0.01
ADAM_STEP = 10
VMEM_LIMIT_BYTES = 56 * 1024 * 1024
MESH = pl.DeviceIdType.MESH

NN = (((1,), (0,)), ((), ()))
NT = (((1,), (1,)), ((), ()))
TN = (((0,), (0,)), ((), ()))
SLOPES = tuple(2.0 ** (-8.0 * (h + 1) / N_Q_HEADS) for h in range(N_Q_HEADS))


def _params(sem=None):
    return pltpu.CompilerParams(dimension_semantics=sem, vmem_limit_bytes=VMEM_LIMIT_BYTES)


def _dot(a, b, dims=NN):
    return lax.dot_general(a, b, dims, preferred_element_type=F32)


def _sigmoid(x):
    return 1.0 / (1.0 + jnp.exp(-x))


def _gelu_parts(z):
    cdf = 0.5 * (1.0 + lax.erf(z * (2.0 ** -0.5)))
    pdf = jnp.exp(-0.5 * z * z) * 0.3989422804014327
    return cdf, pdf


def _rms_fwd(x, gains, name, tm=512):
    t, d = x.shape
    n = len(gains)

    def body(*refs):
        x_ref, g_refs, h_refs = refs[0], refs[1:1 + n], refs[1 + n:]
        xf = x_ref[...]
        xhat = xf * lax.rsqrt(jnp.mean(xf * xf, axis=-1, keepdims=True) + EPS)
        for g_ref, h_ref in zip(g_refs, h_refs):
            h_ref[...] = (xhat * g_ref[...]).astype(BF16)

    row = pl.BlockSpec((tm, d), lambda i: (i, 0))
    vec = pl.BlockSpec((1, d), lambda i: (0, 0))
    return pl.pallas_call(
        body, grid=(t // tm,), in_specs=[row] + [vec] * n, out_specs=[row] * n,
        out_shape=[jax.ShapeDtypeStruct((t, d), BF16)] * n, name=name, compiler_params=_params(),
    )(x, *gains)


def _rms_bwd(x, gains, dhs, dres, name, tm=256):
    t, d = x.shape
    n = len(gains)

    def body(*refs):
        x_ref, dres_ref = refs[0], refs[1]
        g_refs, dh_refs = refs[2:2 + n], refs[2 + n:2 + 2 * n]
        dx_ref, dg_ref = refs[2 + 2 * n], refs[3 + 2 * n]
        i = pl.program_id(0)

        @pl.when(i == 0)
        def _():
            dg_ref[...] = jnp.zeros_like(dg_ref)

        xf = x_ref[...]
        r = lax.rsqrt(jnp.mean(xf * xf, axis=-1, keepdims=True) + EPS)
        xhat = xf * r
        dx = dres_ref[...]
        for j in range(n):
            dh = dh_refs[j][...]
            dg_ref[j:j + 1, :] += jnp.sum(dh * xhat, axis=0, keepdims=True)
            gy = dh * g_refs[j][...]
            dx = dx + r * (gy - xhat * jnp.mean(gy * xhat, axis=-1, keepdims=True))
        dx_ref[...] = dx

    row = pl.BlockSpec((tm, d), lambda i: (i, 0))
    vec = pl.BlockSpec((1, d), lambda i: (0, 0))
    return pl.pallas_call(
        body, grid=(t // tm,), in_specs=[row, row] + [vec] * n + [row] * n,
        out_specs=[row, pl.BlockSpec((8, d), lambda i: (0, 0))],
        out_shape=[jax.ShapeDtypeStruct((t, d), F32), jax.ShapeDtypeStruct((8, d), F32)],
        name=name, compiler_params=_params(("arbitrary",)),
    )(x, dres, *gains, *dhs)


def _mm(a, b, a_spec, b_spec, o_spec, out_shape, grid, dims, name, res=None, res_spec=None):
    nk = grid[2]
    acc_shape = tuple(s for s in o_spec.block_shape if s is not None)

    def body(*refs):
        if res is None:
            a_ref, b_ref, o_ref, acc_ref = refs
        else:
            a_ref, b_ref, r_ref, o_ref, acc_ref = refs
        k = pl.program_id(2)
        p = _dot(a_ref[...].astype(BF16), b_ref[...].astype(BF16), dims)

        @pl.when(k == 0)
        def _():
            acc_ref[...] = p

        @pl.when(k > 0)
        def _():
            acc_ref[...] += p

        @pl.when(k == nk - 1)
        def _():
            out = acc_ref[...]
            if res is not None:
                out = out + r_ref[...]
            o_ref[...] = out.astype(o_ref.dtype)

    ins = [a, b] + ([res] if res is not None else [])
    specs = [a_spec, b_spec] + ([res_spec] if res is not None else [])
    return pl.pallas_call(
        body, grid=grid, in_specs=specs, out_specs=o_spec, out_shape=out_shape,
        scratch_shapes=[pltpu.VMEM(acc_shape, F32)], name=name,
        compiler_params=_params(("parallel", "parallel", "arbitrary")),
    )(*ins)


def _mm_rows(a, w, out_dtype, name, trans_w=False, res=None, tm=512):
    t, k = a.shape
    n = w.shape[0] if trans_w else w.shape[1]
    return _mm(
        a, w, pl.BlockSpec((tm, k), lambda i, j, kk: (i, 0)), pl.BlockSpec(w.shape, lambda i, j, kk: (0, 0)),
        pl.BlockSpec((tm, n), lambda i, j, kk: (i, 0)), jax.ShapeDtypeStruct((t, n), out_dtype), (t // tm, 1, 1),
        NT if trans_w else NN, name, res=res,
        res_spec=None if res is None else pl.BlockSpec((tm, n), lambda i, j, kk: (i, 0)))


def _mm_wgrad(a, b, name, tt=512):
    t, m = a.shape
    n = b.shape[1]
    return _mm(
        a, b, pl.BlockSpec((tt, m), lambda i, j, kk: (kk, 0)), pl.BlockSpec((tt, n), lambda i, j, kk: (kk, 0)),
        pl.BlockSpec((m, n), lambda i, j, kk: (0, 0)), jax.ShapeDtypeStruct((m, n), F32), (1, 1, t // tt), TN, name)


def _sgu_fwd(x0, h1, w_in, g_v, w_c, b_sb, w_out, tm=256):
    t, d = x0.shape
    nsub = w_in.shape[2]

    def body(x_ref, h_ref, win_ref, gv_ref, wc_ref, bsb_ref, wout_ref, zpre_ref, x1_ref, u_s, v_s, vn_s, y_s):
        h = h_ref[...]
        for k in range(N_SHARDS):
            zk = _dot(h, win_ref[k])
            zpre_ref[:, k * nsub:(k + 1) * nsub] = zk
            cdf, _ = _gelu_parts(zk)
            if k < N_SHARDS // 2:
                u_s[:, k * nsub:(k + 1) * nsub] = zk * cdf
            else:
                v_s[:, (k - 4) * nsub:(k - 3) * nsub] = zk * cdf
        v = v_s[...]
        rv = lax.rsqrt(jnp.mean(v * v, axis=-1, keepdims=True) + EPS)
        vn_s[...] = (v * rv * gv_ref[...]).astype(BF16)
        for ci in range(tm // CHUNK):
            rows = slice(ci * CHUNK, (ci + 1) * CHUNK)
            for g in range(N_GROUPS):
                cols = slice(g * LANES, (g + 1) * LANES)
                sv = _dot(wc_ref[g], vn_s[rows, cols]) + bsb_ref[g]
                y_s[rows, cols] = (u_s[rows, cols] * sv).astype(BF16)
        x1_ref[...] = x_ref[...] + _dot(y_s[...], wout_ref[...])

    row = pl.BlockSpec((tm, d), lambda i: (i, 0))
    full = lambda a: pl.BlockSpec(a.shape, lambda i: (0,) * a.ndim)
    return pl.pallas_call(
        body, grid=(t // tm,),
        in_specs=[row, row, full(w_in), full(g_v), full(w_c), full(b_sb), full(w_out)],
        out_specs=[pl.BlockSpec((tm, 2 * d), lambda i: (i, 0)), row],
        out_shape=[jax.ShapeDtypeStruct((t, 2 * d), F32), jax.ShapeDtypeStruct((t, d), F32)],
        scratch_shapes=[pltpu.VMEM((tm, d), F32), pltpu.VMEM((tm, d), F32), pltpu.VMEM((tm, d), BF16),
                        pltpu.VMEM((tm, d), BF16)],
        name="sgu_fwd", compiler_params=_params(),
    )(x0, h1, w_in, g_v, w_c, b_sb, w_out)


def _sgu_bwd(dx1, zpre, w_out, g_v, w_c, w_ct, b_sb, tm=256):
    t, d = dx1.shape

    def body(dx_ref, zpre_ref, wout_ref, gv_ref, wc_ref, wct_ref, bsb_ref,
             dz_ref, y_ref, dwc_ref, dbs_ref, dgv_ref, u_s, vn_s, dy_s, du_s, dvn_s):
        i = pl.program_id(0)

        @pl.when(i == 0)
        def _():
            dwc_ref[...] = jnp.zeros_like(dwc_ref)
            dbs_ref[...] = jnp.zeros_like(dbs_ref)
            dgv_ref[...] = jnp.zeros_like(dgv_ref)

        dy_s[...] = _dot(dx_ref[...].astype(BF16), wout_ref[...], NT)
        zu = zpre_ref[:, :d]
        zv = zpre_ref[:, d:]
        cdf_u, pdf_u = _gelu_parts(zu)
        cdf_v, pdf_v = _gelu_parts(zv)
        u_s[...] = zu * cdf_u
        v = zv * cdf_v
        rv = lax.rsqrt(jnp.mean(v * v, axis=-1, keepdims=True) + EPS)
        vhat = v * rv
        gv = gv_ref[...]
        vn_s[...] = (vhat * gv).astype(BF16)
        for ci in range(tm // CHUNK):
            rows = slice(ci * CHUNK, (ci + 1) * CHUNK)
            for g in range(N_GROUPS):
                cols = slice(g * LANES, (g + 1) * LANES)
                vnb = vn_s[rows, cols]
                sv = _dot(wc_ref[g], vnb) + bsb_ref[g]
                dyb = dy_s[rows, cols]
                ub = u_s[rows, cols]
                dsv = dyb * ub
                du_s[rows, cols] = dyb * sv
                y_ref[rows, cols] = (ub * sv).astype(BF16)
                dsvb = dsv.astype(BF16)
                dbs_ref[g] += dsv
                dwc_ref[g] += _dot(dsvb, vnb, NT)
                dvn_s[rows, cols] = _dot(wct_ref[g], dsvb)
        dvn = dvn_s[...]
        dgv_ref[0:1, :] += jnp.sum(dvn * vhat, axis=0, keepdims=True)
        gy = dvn * gv
        dv = rv * (gy - vhat * jnp.mean(gy * vhat, axis=-1, keepdims=True))
        dz_ref[:, :d] = (du_s[...] * (cdf_u + zu * pdf_u)).astype(BF16)
        dz_ref[:, d:] = (dv * (cdf_v + zv * pdf_v)).astype(BF16)

        @pl.when(i == t // tm - 1)
        def _():
            tri = (lax.broadcasted_iota(jnp.int32, (CHUNK, CHUNK), 0)
                   >= lax.broadcasted_iota(jnp.int32, (CHUNK, CHUNK), 1))
            for g in range(N_GROUPS):
                dwc_ref[g] = jnp.where(tri, dwc_ref[g], 0.0)
                dbs_ref[g] = jnp.broadcast_to(jnp.sum(dbs_ref[g], axis=1, keepdims=True), (CHUNK, CHUNK))

    row = pl.BlockSpec((tm, d), lambda i: (i, 0))
    row2 = pl.BlockSpec((tm, 2 * d), lambda i: (i, 0))
    full = lambda a: pl.BlockSpec(a.shape, lambda i: (0,) * a.ndim)
    grp = pl.BlockSpec((N_GROUPS, CHUNK, CHUNK), lambda i: (0, 0, 0))
    return pl.pallas_call(
        body, grid=(t // tm,),
        in_specs=[row, row2, full(w_out), full(g_v), full(w_c), full(w_ct), full(b_sb)],
        out_specs=[row2, row, grp, grp, pl.BlockSpec((8, d), lambda i: (0, 0))],
        out_shape=[jax.ShapeDtypeStruct((t, 2 * d), BF16), jax.ShapeDtypeStruct((t, d), BF16),
                   jax.ShapeDtypeStruct((N_GROUPS, CHUNK, CHUNK), F32),
                   jax.ShapeDtypeStruct((N_GROUPS, CHUNK, CHUNK), F32), jax.ShapeDtypeStruct((8, d), F32)],
        scratch_shapes=[pltpu.VMEM((tm, d), F32), pltpu.VMEM((tm, d), BF16), pltpu.VMEM((tm, d), F32),
                        pltpu.VMEM((tm, d), F32), pltpu.VMEM((tm, d), F32)],
        name="sgu_bwd", compiler_params=_params(("arbitrary",)),
    )(dx1, zpre, w_out, g_v, w_c, w_ct, b_sb)


def _causal_conv(a_ref, prev_ref, cw, cb, first, tm):
    af = a_ref[...].astype(F32)
    keep = jnp.where(first, 0.0, 1.0)
    pv = prev_ref[...].astype(F32)
    p1 = pv[15:16, :] * keep
    p2 = pv[14:15, :] * keep
    row = lax.broadcasted_iota(jnp.int32, af.shape, 0)
    a1 = jnp.where(row == 0, p1, pltpu.roll(af, 1, 0))
    a2 = jnp.where(row == 0, p2, jnp.where(row == 1, p1, pltpu.roll(af, 2, 0)))
    hu = cw[2:3, :] * af + cw[1:2, :] * a1 + cw[0:1, :] * a2 + cb
    return hu, af, a1, a2


def _ffn_in(hf, w_in_all, layer, tm=512):
    t, d = hf.shape
    return _mm(
        hf, w_in_all, pl.BlockSpec((tm, d), lambda s, i, kk: (i, 0)),
        pl.BlockSpec((None, None, d, FF_SHARD), lambda s, i, kk: (layer, s, 0, 0)),
        pl.BlockSpec((None, tm, FF_SHARD), lambda s, i, kk: (s, i, 0)),
        jax.ShapeDtypeStruct((N_SHARDS, t, FF_SHARD), BF16), (N_SHARDS, t // tm, 1), NN, f"ffn{layer}_in")


def _ffn_conv_specs(tm, gate_of, tile_of):
    def specs(shard_of):
        return [
            pl.BlockSpec((None, tm, FF_SHARD), lambda *g: (shard_of(*g), tile_of(*g), 0)),
            pl.BlockSpec((None, 16, FF_SHARD),
                         lambda *g: (shard_of(*g), jnp.maximum(tile_of(*g) * (tm // 16) - 1, 0), 0)),
            pl.BlockSpec((None, 8, FF_SHARD), lambda *g: (shard_of(*g), 0, 0)),
            pl.BlockSpec((None, 1, FF_SHARD), lambda *g: (shard_of(*g), 0, 0)),
        ]
    return specs(gate_of) + specs(lambda *g: gate_of(*g) + N_SHARDS // 2)


def _ffn_out(a, cw, cb, w_out, x, layer, tm=512):
    t, d = x.shape
    nc = N_SHARDS // 2

    def body(ag_ref, pg_ref, cwg_ref, cbg_ref, au_ref, pu_ref, cwu_ref, cbu_ref, wout_ref, x_ref, o_ref, acc_ref):
        i, c = pl.program_id(0), pl.program_id(1)
        hg = _causal_conv(ag_ref, pg_ref, cwg_ref[...], cbg_ref[...], i == 0, tm)[0]
        hu = _causal_conv(au_ref, pu_ref, cwu_ref[...], cbu_ref[...], i == 0, tm)[0]
        act = (hg * _sigmoid(hg) * hu).astype(BF16)
        p = _dot(act, wout_ref[...])

        @pl.when(c == 0)
        def _():
            acc_ref[...] = x_ref[...] + p

        @pl.when(c > 0)
        def _():
            acc_ref[...] += p

        @pl.when(c == nc - 1)
        def _():
            o_ref[...] = acc_ref[...]

    row = pl.BlockSpec((tm, d), lambda i, c: (i, 0))
    return pl.pallas_call(
        body, grid=(t // tm, nc),
        in_specs=_ffn_conv_specs(tm, lambda i, c: c, lambda i, c: i)
        + [pl.BlockSpec((FF_SHARD, d), lambda i, c: (c, 0)), row],
        out_specs=row, out_shape=jax.ShapeDtypeStruct((t, d), F32),
        scratch_shapes=[pltpu.VMEM((tm, d), F32)], name=f"ffn{layer}_out",
        compiler_params=_params(("parallel", "arbitrary")),
    )(a, a, cw, cb, a, a, cw, cb, w_out, x)


def _ffn_bwd_act(a, cw, cb, w_out, dxn, layer, tm=512):
    t, d = dxn.shape
    nc = N_SHARDS // 2

    def body(ag_ref, pg_ref, cwg_ref, cbg_ref, au_ref, pu_ref, cwu_ref, cbu_ref, wout_ref, dx_ref,
             dhu_ref, dw_ref, dconv_ref):
        i = pl.program_id(1)

        @pl.when(i == 0)
        def _():
            dw_ref[...] = jnp.zeros_like(dw_ref)
            dconv_ref[...] = jnp.zeros_like(dconv_ref)

        hg, ag0, ag1, ag2 = _causal_conv(ag_ref, pg_ref, cwg_ref[...], cbg_ref[...], i == 0, tm)
        hu, au0, au1, au2 = _causal_conv(au_ref, pu_ref, cwu_ref[...], cbu_ref[...], i == 0, tm)
        sg = _sigmoid(hg)
        sl = hg * sg
        dxb = dx_ref[...].astype(BF16)
        dact = _dot(dxb, wout_ref[...], NT)
        dw_ref[...] += _dot((sl * hu).astype(BF16), dxb, TN)
        d_up = dact * sl
        d_gate = dact * hu * (sg * (1.0 + hg * (1.0 - sg)))
        for j, (dv, taps) in enumerate(((d_gate, (ag2, ag1, ag0)), (d_up, (au2, au1, au0)))):
            dvb = dv.astype(BF16)
            dhu_ref[j] = dvb
            dvr = dvb.astype(F32)
            for k in range(3):
                dconv_ref[j, k:k + 1, :] += jnp.sum(dvr * taps[k], axis=0, keepdims=True)
            dconv_ref[j, 3:4, :] += jnp.sum(dv, axis=0, keepdims=True)

    return pl.pallas_call(
        body, grid=(nc, t // tm),
        in_specs=_ffn_conv_specs(tm, lambda c, i: c, lambda c, i: i)
        + [pl.BlockSpec((FF_SHARD, d), lambda c, i: (c, 0)), pl.BlockSpec((tm, d), lambda c, i: (i, 0))],
        out_specs=[pl.BlockSpec((None, 2, tm, FF_SHARD), lambda c, i: (c, 0, i, 0)),
                   pl.BlockSpec((FF_SHARD, d), lambda c, i: (c, 0)),
                   pl.BlockSpec((None, 2, 8, FF_SHARD), lambda c, i: (c, 0, 0, 0))],
        out_shape=[jax.ShapeDtypeStruct((nc, 2, t, FF_SHARD), BF16), jax.ShapeDtypeStruct((D_FF, d), F32),
                   jax.ShapeDtypeStruct((nc, 2, 8, FF_SHARD), F32)],
        name=f"ffn{layer}_bwd_act", compiler_params=_params(("parallel", "arbitrary")),
    )(a, a, cw, cb, a, a, cw, cb, w_out, dxn)


def _ffn_bwd_in(dhu, cw, w_in_all, layer, tm=512):
    nc, _, t, _ = dhu.shape
    d = D_MODEL
    last_blk = t // 16 - 1

    def body(dh_ref, nx_ref, cw_ref, win_ref, da_ref, o_ref):
        i, s = pl.program_id(0), pl.program_id(1)
        df = dh_ref[...].astype(F32)
        keep = jnp.where(i == t // tm - 1, 0.0, 1.0)
        nx = nx_ref[...].astype(F32)
        n0 = nx[0:1, :] * keep
        n1 = nx[1:2, :] * keep
        row = lax.broadcasted_iota(jnp.int32, df.shape, 0)
        d1 = jnp.where(row == tm - 1, n0, pltpu.roll(df, tm - 1, 0))
        d2 = jnp.where(row == tm - 1, n1, jnp.where(row == tm - 2, n0, pltpu.roll(df, tm - 2, 0)))
        cw = cw_ref[...]
        da = (cw[2:3, :] * df + cw[1:2, :] * d1 + cw[0:1, :] * d2).astype(BF16)
        da_ref[...] = da
        p = _dot(da, win_ref[...], NT)

        @pl.when(s == 0)
        def _():
            o_ref[...] = p

        @pl.when(s > 0)
        def _():
            o_ref[...] += p

    return pl.pallas_call(
        body, grid=(t // tm, N_SHARDS),
        in_specs=[pl.BlockSpec((None, None, tm, FF_SHARD), lambda i, s: (s % nc, s // nc, i, 0)),
                  pl.BlockSpec((None, None, 16, FF_SHARD),
                               lambda i, s: (s % nc, s // nc, jnp.minimum((i + 1) * (tm // 16), last_blk), 0)),
                  pl.BlockSpec((None, 8, FF_SHARD), lambda i, s: (s, 0, 0)),
                  pl.BlockSpec((None, None, d, FF_SHARD), lambda i, s: (layer, s, 0, 0))],
        out_specs=[pl.BlockSpec((None, tm, FF_SHARD), lambda i, s: (s, i, 0)),
                   pl.BlockSpec((tm, d), lambda i, s: (i, 0))],
        out_shape=[jax.ShapeDtypeStruct((N_SHARDS, t, FF_SHARD), BF16), jax.ShapeDtypeStruct((t, d), F32)],
        name=f"ffn{layer}_bwd_in", compiler_params=_params(("parallel", "arbitrary")),
    )(dhu, dhu, cw, w_in_all)


def _ffn_wgrad_in(hf, da, layer, tt=512):
    t, d = hf.shape
    return _mm(
        hf, da, pl.BlockSpec((tt, d), lambda s, j, kk: (kk, 0)),
        pl.BlockSpec((None, tt, FF_SHARD), lambda s, j, kk: (s, kk, 0)),
        pl.BlockSpec((None, d, FF_SHARD), lambda s, j, kk: (s, 0, 0)),
        jax.ShapeDtypeStruct((N_SHARDS, d, FF_SHARD), F32), (N_SHARDS, 1, t // tt), TN, f"ffn{layer}_wgrad_in")


def _attn_masks(n):
    lane = lax.broadcasted_iota(jnp.int32, (CHUNK, LANES), 1)
    lo = lane < HEAD_DIM
    tq = lax.broadcasted_iota(jnp.int32, (CHUNK, 2 * CHUNK), 0)
    jk = lax.broadcasted_iota(jnp.int32, (CHUNK, 2 * CHUNK), 1)
    dist = tq + CHUNK - jk
    mask = (dist >= 0) & (dist < CHUNK) & (jk >= jnp.where(n == 0, CHUNK, 0))
    return lo, mask, dist.astype(F32)


def _half_sum(x, lo):
    s_lo = jnp.sum(jnp.where(lo, x, 0.0), axis=-1, keepdims=True)
    s_hi = jnp.sum(jnp.where(lo, 0.0, x), axis=-1, keepdims=True)
    return jnp.where(lo, s_lo, s_hi)


def _attn_probs(qh, kn, mask, distf, slope, sink):
    s = _dot(qh, kn, NT) * (HEAD_DIM ** -0.5)
    s = jnp.where(mask, s - slope * distf, NEG_BIG)
    m = jnp.maximum(jnp.max(s, axis=-1, keepdims=True), sink)
    e = jnp.exp(s - m)
    den = jnp.sum(e, axis=-1, keepdims=True) + jnp.exp(sink - m)
    return e / den, m, den


def _attn_fwd(qraw, kvd, gq, gk, sinks):
    t, d = qraw.shape
    nb = t // CHUNK

    def body(sink_ref, q_ref, cur_ref, prev_ref, gq_ref, gk_ref, o_ref):
        n = pl.program_id(0)
        lo, mask, distf = _attn_masks(n)
        gq_v, gk_v = gq_ref[...], gk_ref[...]
        for kvh in range(N_KV_HEADS):
            ks = slice(kvh * LANES, (kvh + 1) * LANES)
            vs = slice(4 * LANES + kvh * LANES, 4 * LANES + (kvh + 1) * LANES)
            kraw = jnp.concatenate([prev_ref[:, ks], cur_ref[:, ks]], axis=0)
            rk = lax.rsqrt(jnp.mean(kraw * kraw, axis=-1, keepdims=True) + EPS)
            kn = (kraw * rk * gk_v).astype(BF16)
            vv = jnp.concatenate([prev_ref[:, vs], cur_ref[:, vs]], axis=0).astype(BF16)
            for p in range(2):
                jq = 2 * kvh + p
                qp = q_ref[:, jq * LANES:(jq + 1) * LANES]
                r = lax.rsqrt(_half_sum(qp * qp, lo) * (1.0 / HEAD_DIM) + EPS)
                qn = qp * r * gq_v
                acc = None
                for half in range(2):
                    h = 4 * kvh + 2 * p + half
                    sel = lo if half == 0 else jnp.logical_not(lo)
                    qh = jnp.where(sel, qn, 0.0).astype(BF16)
                    pf, _, _ = _attn_probs(qh, kn, mask, distf, SLOPES[h], sink_ref[h])
                    oh = _dot(pf.astype(BF16), vv)
                    acc = oh if half == 0 else jnp.where(lo, acc, oh)
                o_ref[:, jq * LANES:(jq + 1) * LANES] = acc.astype(BF16)

    blk = lambda f: pl.BlockSpec((CHUNK, d), f)
    vec = pl.BlockSpec((1, LANES), lambda n: (0, 0))
    return pl.pallas_call(
        body, grid=(nb,),
        in_specs=[pl.BlockSpec(memory_space=pltpu.SMEM), blk(lambda n: (n, 0)), blk(lambda n: (n, 0)),
                  blk(lambda n: (jnp.maximum(n - 1, 0), 0)), vec, vec],
        out_specs=blk(lambda n: (n, 0)), out_shape=jax.ShapeDtypeStruct((t, d), BF16),
        name="attn_fwd", compiler_params=_params(),
    )(sinks, qraw, kvd, kvd, gq, gk)


def _attn_bwd(qraw, kvd, d_o, gq, gk, sinks):
    t, d = qraw.shape
    nb = t // CHUNK

    def body(sink_ref, q_ref, cur_ref, prev_ref, do_ref, gq_ref, gk_ref,
             dq_ref, dkv_ref, dsink_ref, dgq_ref, dgk_ref, carry_s, pp_s, cp_s):
        n = pl.program_id(0)

        @pl.when(n == 0)
        def _():
            carry_s[...] = jnp.zeros_like(carry_s)
            dsink_ref[...] = jnp.zeros_like(dsink_ref)
            dgq_ref[...] = jnp.zeros_like(dgq_ref)
            dgk_ref[...] = jnp.zeros_like(dgk_ref)

        @pl.when(n < nb)
        def _():
            lo, mask, distf = _attn_masks(n)
            gq_v, gk_v = gq_ref[...], gk_ref[...]
            for kvh in range(N_KV_HEADS):
                ks = slice(kvh * LANES, (kvh + 1) * LANES)
                vs = slice(4 * LANES + kvh * LANES, 4 * LANES + (kvh + 1) * LANES)
                kraw = jnp.concatenate([prev_ref[:, ks], cur_ref[:, ks]], axis=0)
                rk = lax.rsqrt(jnp.mean(kraw * kraw, axis=-1, keepdims=True) + EPS)
                khat = kraw * rk
                kn = (khat * gk_v).astype(BF16)
                vv = jnp.concatenate([prev_ref[:, vs], cur_ref[:, vs]], axis=0).astype(BF16)
                dkn = jnp.zeros((2 * CHUNK, LANES), F32)
                dvb = jnp.zeros((2 * CHUNK, LANES), F32)
                for p in range(2):
                    jq = 2 * kvh + p
                    cols = slice(jq * LANES, (jq + 1) * LANES)
                    qp = q_ref[:, cols]
                    r = lax.rsqrt(_half_sum(qp * qp, lo) * (1.0 / HEAD_DIM) + EPS)
                    qhat = qp * r
                    qn = qhat * gq_v
                    dop = do_ref[:, cols]
                    dqn = None
                    for half in range(2):
                        h = 4 * kvh + 2 * p + half
                        sel = lo if half == 0 else jnp.logical_not(lo)
                        qh = jnp.where(sel, qn, 0.0).astype(BF16)
                        doh = jnp.where(sel, dop, jnp.zeros_like(dop))
                        sink = sink_ref[h]
                        pf, m, den = _attn_probs(qh, kn, mask, distf, SLOPES[h], sink)
                        dp = _dot(doh, vv, NT)
                        delta = jnp.sum(pf * dp, axis=-1, keepdims=True)
                        p_sink = jnp.exp(sink - m) / den
                        dsink_ref[h:h + 1, :] -= jnp.broadcast_to(
                            jnp.sum(p_sink * delta, axis=0, keepdims=True), (1, LANES))
                        ds = (pf * (dp - delta) * (HEAD_DIM ** -0.5)).astype(BF16)
                        dqh = _dot(ds, kn)
                        dqn = dqh if half == 0 else jnp.where(lo, dqn, dqh)
                        dkn = dkn + _dot(ds, qh, TN)
                        dvb = dvb + _dot(pf.astype(BF16), doh, TN)
                    dgq_ref[0:1, :] += jnp.sum(dqn * qhat, axis=0, keepdims=True)
                    gy = dqn * gq_v
                    mq = _half_sum(gy * qhat, lo) * (1.0 / HEAD_DIM)
                    dq_ref[:, cols] = (r * (gy - qhat * mq)).astype(BF16)
                dgk_ref[0:1, :] += jnp.sum(dkn * khat, axis=0, keepdims=True)
                gyk = dkn * gk_v
                dkraw = rk * (gyk - khat * jnp.mean(gyk * khat, axis=-1, keepdims=True))
                pp_s[:, ks] = dkraw[:CHUNK]
                cp_s[:, ks] = dkraw[CHUNK:]
                pp_s[:, vs] = dvb[:CHUNK]
                cp_s[:, vs] = dvb[CHUNK:]
            dkv_ref[...] = (carry_s[...] + pp_s[...]).astype(BF16)
            carry_s[...] = cp_s[...]

        @pl.when(n == nb)
        def _():
            dkv_ref[...] = carry_s[...].astype(BF16)

    blk = lambda f: pl.BlockSpec((CHUNK, d), f)
    vec = pl.BlockSpec((1, LANES), lambda n: (0, 0))
    cur = lambda n: (jnp.minimum(n, nb - 1), 0)
    prev = lambda n: (jnp.maximum(jnp.minimum(n, nb - 1) - 1, 0), 0)
    small = lambda r: pl.BlockSpec((r, LANES), lambda n: (0, 0))
    return pl.pallas_call(
        body, grid=(nb + 1,),
        in_specs=[pl.BlockSpec(memory_space=pltpu.SMEM), blk(cur), blk(cur), blk(prev), blk(cur), vec, vec],
        out_specs=[blk(cur), blk(lambda n: (jnp.maximum(n - 1, 0), 0)), small(N_Q_HEADS), small(8), small(8)],
        out_shape=[jax.ShapeDtypeStruct((t, d), BF16), jax.ShapeDtypeStruct((t, d), BF16),
                   jax.ShapeDtypeStruct((N_Q_HEADS, LANES), F32), jax.ShapeDtypeStruct((8, LANES), F32),
                   jax.ShapeDtypeStruct((8, LANES), F32)],
        scratch_shapes=[pltpu.VMEM((CHUNK, d), F32)] * 3,
        name="attn_bwd", compiler_params=_params(("arbitrary",)),
    )(sinks, qraw, kvd, kvd, d_o, gq, gk)


def _loss_head(y, target, tm=512):
    t, d = y.shape

    def body(y_ref, t_ref, dy_ref, loss_ref, acc_ref):
        i = pl.program_id(0)

        @pl.when(i == 0)
        def _():
            acc_ref[...] = jnp.zeros_like(acc_ref)

        err = y_ref[...] - t_ref[...]
        dy_ref[...] = err * (1.0 / d)
        acc_ref[...] += jnp.sum(err * err, axis=0, keepdims=True)

        @pl.when(i == t // tm - 1)
        def _():
            loss_ref[...] = jnp.broadcast_to(0.5 / d * jnp.sum(acc_ref[...], axis=1, keepdims=True), loss_ref.shape)

    row = pl.BlockSpec((tm, d), lambda i: (i, 0))
    return pl.pallas_call(
        body, grid=(t // tm,), in_specs=[row, row], out_specs=[row, pl.BlockSpec((8, LANES), lambda i: (0, 0))],
        out_shape=[jax.ShapeDtypeStruct((t, d), F32), jax.ShapeDtypeStruct((8, LANES), F32)],
        scratch_shapes=[pltpu.VMEM((1, d), F32)], name="loss_head", compiler_params=_params(("arbitrary",)),
    )(y, target)


def _adamw_math(g, w, m, v):
    m = ADAM_B1 * m + (1.0 - ADAM_B1) * g
    v = ADAM_B2 * v + (1.0 - ADAM_B2) * (g * g)
    m_hat = m / (1.0 - ADAM_B1 ** ADAM_STEP)
    v_hat = v / (1.0 - ADAM_B2 ** ADAM_STEP)
    delta = -ADAM_LR * (m_hat / (jnp.sqrt(v_hat) + ADAM_EPS) + ADAM_WD * w)
    return delta, m, v


def _row_tile(r, cap=128):
    for tr in range(min(r, cap), 0, -1):
        if r % tr == 0 and (tr % 8 == 0 or tr == r):
            return tr
    return r


def _adamw_sharded(own, others, w, m, v, name):
    r, c = w.shape
    tr = _row_tile(r)

    def body(own_ref, oth_ref, w_ref, m_ref, v_ref, g_ref, d_ref, nm_ref, nv_ref):
        g = ((own_ref[...] + oth_ref[0]) + oth_ref[1]) + oth_ref[2]
        delta, nm, nv = _adamw_math(g, w_ref[...], m_ref[...], v_ref[...])
        g_ref[...] = g
        d_ref[...] = delta
        nm_ref[...] = nm
        nv_ref[...] = nv

    row = pl.BlockSpec((tr, c), lambda i: (i, 0))
    return pl.pallas_call(
        body, grid=(r // tr,), in_specs=[row, pl.BlockSpec((3, tr, c), lambda i: (0, i, 0)), row, row, row],
        out_specs=[row] * 4, out_shape=[jax.ShapeDtypeStruct((r, c), F32)] * 4, name=name, compiler_params=_params(),
    )(own, others, w, m, v)


def _adamw_replicated(parts, w, m, v, name):
    r, c = w.shape
    tr = _row_tile(r)

    def body(p_ref, w_ref, m_ref, v_ref, g_ref, d_ref, nm_ref, nv_ref):
        g = p_ref[0]
        for k in range(1, N_SHARDS):
            g = g + p_ref[k]
        delta, nm, nv = _adamw_math(g, w_ref[...], m_ref[...], v_ref[...])
        g_ref[...] = g
        d_ref[...] = delta
        nm_ref[...] = nm
        nv_ref[...] = nv

    row = pl.BlockSpec((tr, c), lambda i: (i, 0))
    return pl.pallas_call(
        body, grid=(r // tr,), in_specs=[pl.BlockSpec((N_SHARDS, tr, c), lambda i: (0, i, 0)), row, row, row],
        out_specs=[row] * 4, out_shape=[jax.ShapeDtypeStruct((r, c), F32)] * 4, name=name, compiler_params=_params(),
    )(parts, w, m, v)


def _coords():
    return lax.axis_index("x"), lax.axis_index("y"), lax.axis_index("c")


def _all_gather(srcs, layers, name):
    n = len(srcs)
    entries = []
    out_shapes = []
    for i, (s, nl) in enumerate(zip(srcs, layers)):
        if nl is None:
            entries.append((i, None))
            out_shapes.append(jax.ShapeDtypeStruct((N_SHARDS,) + s.shape, s.dtype))
        else:
            entries += [(i, l) for l in range(nl)]
            out_shapes.append(jax.ShapeDtypeStruct((nl, N_SHARDS) + s.shape[1:], s.dtype))
    ne = len(entries)

    def body(*refs):
        src, dst = refs[:n], refs[n:2 * n]
        send_sems, recv_sems, local_sems = refs[2 * n:]
        x, y, c = _coords()
        me, sibling = (x, y, c), (x, y, 1 - c)
        chips = [(1 - x, y), (x, 1 - y), (1 - x, 1 - y)]

        def own(e):
            i, l = entries[e]
            return src[i] if l is None else src[i].at[l]

        def rows(e, dev):
            i, l = entries[e]
            k = 4 * dev[0] + 2 * dev[1] + dev[2]
            return dst[i].at[k] if l is None else dst[i].at[l, k]

        def copy(e, slot, block, to, from_own=False):
            return pltpu.make_async_remote_copy(
                src_ref=own(e) if from_own else rows(e, block), dst_ref=rows(e, block),
                send_sem=send_sems.at[e, slot], recv_sem=recv_sems.at[e, slot], device_id=to, device_id_type=MESH)

        mine = [pltpu.make_async_copy(own(e), rows(e, me), local_sems.at[e]) for e in range(ne)]
        for cp in mine:
            cp.start()
        first = []
        for e in range(ne):
            first.append(copy(e, 0, me, sibling, from_own=True))
            first += [copy(e, 1 + j, me, (*chip, c), from_own=True) for j, chip in enumerate(chips)]
        for cp in first:
            cp.start()
        passed = []
        for j, chip in enumerate(chips):
            for e in range(ne):
                copy(e, 1 + j, (*chip, c), me).wait_recv()
                cp = copy(e, 4 + j, (*chip, c), sibling)
                cp.start()
                passed.append(cp)
        for e in range(ne):
            copy(e, 0, sibling, me).wait_recv()
            for j, chip in enumerate(chips):
                copy(e, 4 + j, (*chip, 1 - c), me).wait_recv()
        for cp in first + passed:
            cp.wait_send()
        for cp in mine:
            cp.wait()

    hbm = pl.BlockSpec(memory_space=pl.ANY)
    return pl.pallas_call(
        body, in_specs=[hbm] * n, out_specs=[hbm] * n, out_shape=out_shapes,
        scratch_shapes=[pltpu.SemaphoreType.DMA((ne, 7)), pltpu.SemaphoreType.DMA((ne, 7)),
                        pltpu.SemaphoreType.DMA((ne,))],
        name=name,
    )(*srcs)


def _exchange_core(grads):
    n = len(grads)

    def body(*refs):
        g, a = refs[:n], refs[n:2 * n]
        send_sems, recv_sems = refs[2 * n:]
        x, y, c = _coords()
        copies = [
            pltpu.make_async_remote_copy(
                src_ref=g[i].at[2 * q + (1 - c)], dst_ref=a[i].at[q], send_sem=send_sems.at[i, q],
                recv_sem=recv_sems.at[i, q], device_id=(x, y, 1 - c), device_id_type=MESH)
            for i in range(n) for q in range(4)]
        for cp in copies:
            cp.start()
        for cp in copies:
            cp.wait()

    hbm = pl.BlockSpec(memory_space=pl.ANY)
    return pl.pallas_call(
        body, in_specs=[hbm] * n, out_specs=[hbm] * n,
        out_shape=[jax.ShapeDtypeStruct((4,) + g.shape[1:], F32) for g in grads],
        scratch_shapes=[pltpu.SemaphoreType.DMA((n, 4)), pltpu.SemaphoreType.DMA((n, 4))],
        name="exchange_core",
    )(*grads)


def _chip_sum(grad, recv, c_idx, name):
    _, r, c = grad.shape
    tr = _row_tile(r, 256)

    def body(c_ref, g_ref, a_ref, p_ref):
        p_ref[...] = g_ref[...] + a_ref[...]

    return pl.pallas_call(
        body,
        grid_spec=pltpu.PrefetchScalarGridSpec(
            num_scalar_prefetch=1, grid=(4, r // tr),
            in_specs=[pl.BlockSpec((None, None, tr, c), lambda q, i, cr: (q, cr[0], i, 0)),
                      pl.BlockSpec((None, tr, c), lambda q, i, cr: (q, i, 0))],
            out_specs=pl.BlockSpec((None, tr, c), lambda q, i, cr: (q, i, 0))),
        out_shape=jax.ShapeDtypeStruct((4, r, c), F32), name=name, compiler_params=_params(),
    )(c_idx, grad.reshape(4, 2, r, c), recv)


def _exchange_chips(psums):
    n = len(psums)

    def body(*refs):
        p, b = refs[:n], refs[n:2 * n]
        send_sems, recv_sems = refs[2 * n:]
        x, y, c = _coords()
        peers = [(x, 1 - y), (1 - x, y), (1 - x, 1 - y)]
        copies = [
            pltpu.make_async_remote_copy(
                src_ref=p[i].at[2 * px + py], dst_ref=b[i].at[r], send_sem=send_sems.at[i, r],
                recv_sem=recv_sems.at[i, r], device_id=(px, py, c), device_id_type=MESH)
            for i in range(n) for r, (px, py) in enumerate(peers)]
        for cp in copies:
            cp.start()
        for cp in copies:
            cp.wait()

    hbm = pl.BlockSpec(memory_space=pl.ANY)
    return pl.pallas_call(
        body, in_specs=[hbm] * n, out_specs=[hbm] * n,
        out_shape=[jax.ShapeDtypeStruct((3,) + p.shape[1:], F32) for p in psums],
        scratch_shapes=[pltpu.SemaphoreType.DMA((n, 3)), pltpu.SemaphoreType.DMA((n, 3))],
        name="exchange_chips",
    )(*psums)


def _as_rows(a):
    flat = a.reshape(-1).astype(F32)
    pad = (-flat.shape[0]) % LANES
    if pad:
        flat = jnp.concatenate([flat, jnp.zeros((pad,), F32)])
    return flat.reshape(-1, LANES)


def _pack(arrays):
    rows = jnp.concatenate([_as_rows(a) for a in arrays], axis=0)
    pad = (-rows.shape[0]) % 8
    if pad:
        rows = jnp.concatenate([rows, jnp.zeros((pad, LANES), F32)], axis=0)
    return rows


def _unpack(rows, shapes):
    out, r0 = [], 0
    for shp in shapes:
        size = 1
        for s in shp:
            size *= s
        nr = -(-size // LANES)
        out.append(rows[r0:r0 + nr].reshape(-1)[:size].reshape(shp))
        r0 += nr
    return out


def _dup_heads(w):
    lead = w.shape[:-1]
    w4 = w.reshape(lead + (N_KV_HEADS, 1, HEAD_DIM))
    return jnp.broadcast_to(w4, lead + (N_KV_HEADS, 2, HEAD_DIM)).reshape(lead + (N_KV_HEADS * LANES,))


def _fold_heads(g):
    lead = g.shape[:-1]
    return g.reshape(lead + (N_KV_HEADS, 2, HEAD_DIM)).sum(axis=-2).reshape(lead + (N_KV_HEADS * HEAD_DIM,))


def _derive_weights(a_in, a_out, f_in, f_out, kv_full, w_q, w_o, a_norm, a_v_norm, conv_w, a_w_s, a_b_s, f_norm,
                    f_conv_b, kv_norm, k_norm, b_norm, b_q_norm, b_sinks):
    conv_w = lax.reduce_precision(conv_w, 8, 7)
    tri = jnp.tril(jnp.ones((CHUNK, CHUNK), dtype=bool))
    w_causal = jnp.where(tri[None], a_w_s[0], 0.0).astype(BF16)
    half = N_KV_HEADS * HEAD_DIM
    return {
        "a_norm": a_norm, "a_v_norm": a_v_norm, "a_w_in": a_in, "a_w_out": a_out,
        "sgu_wc": w_causal, "sgu_wct": jnp.transpose(w_causal, (0, 2, 1)),
        "sgu_bsb": jnp.broadcast_to(a_b_s[0][:, :, None], (N_GROUPS, CHUNK, CHUNK)),
        "f_norm": f_norm, "f_w_in": f_in,
        "f_cw": jnp.pad(jnp.transpose(conv_w, (1, 0, 2, 3)), ((0, 0), (0, 0), (0, 5), (0, 0))),
        "f_cb": f_conv_b.reshape(2, N_SHARDS, 1, FF_SHARD), "f_w_out": f_out,
        "kv_norm": kv_norm, "b_norm": b_norm,
        "w_kv_dup": jnp.concatenate([_dup_heads(kv_full[:, :half]), _dup_heads(kv_full[:, half:])], axis=1),
        "b_w_q": w_q, "b_w_o": w_o,
        "gq": jnp.tile(b_q_norm.reshape(1, HEAD_DIM), (1, 2)), "gk": jnp.tile(k_norm.reshape(1, HEAD_DIM), (1, 2)),
        "sinks": b_sinks.reshape(N_Q_HEADS),
    }


def _local_step(x, target, wts):
    d = D_MODEL
    row = lambda v: v.reshape(1, -1)
    sgu_wc, sgu_wct, sgu_bsb = wts["sgu_wc"], wts["sgu_wct"], wts["sgu_bsb"]

    (h1,) = _rms_fwd(x, [row(wts["a_norm"])], "a_norm_fwd")
    zpre, x1 = _sgu_fwd(x, h1, wts["a_w_in"], row(wts["a_v_norm"]), sgu_wc, sgu_bsb, wts["a_w_out"])
    (hf0,) = _rms_fwd(x1, [wts["f_norm"][0:1]], "f0_norm_fwd")
    a0 = _ffn_in(hf0, wts["f_w_in"], 0)
    x2 = _ffn_out(a0, wts["f_cw"][0], wts["f_cb"][0], wts["f_w_out"][0], x1, 0)
    hk, hq = _rms_fwd(x2, [row(wts["kv_norm"]), wts["b_norm"]], "kvq_norm_fwd")
    kvd = _mm_rows(hk, wts["w_kv_dup"], F32, "kv_proj")
    qraw = _mm_rows(hq, wts["b_w_q"], F32, "q_proj")
    o = _attn_fwd(qraw, kvd, wts["gq"], wts["gk"], wts["sinks"])
    x3 = _mm_rows(o, wts["b_w_o"], F32, "o_proj", res=x2)
    (hf1,) = _rms_fwd(x3, [wts["f_norm"][1:2]], "f1_norm_fwd")
    a1 = _ffn_in(hf1, wts["f_w_in"], 1)
    x4 = _ffn_out(a1, wts["f_cw"][1], wts["f_cb"][1], wts["f_w_out"][1], x3, 1)
    dy, loss_lanes = _loss_head(x4, target)

    grads = {}

    def ffn_bwd(layer, a, hf, x_in, dxn):
        dhu, dw_out, dconv = _ffn_bwd_act(a, wts["f_cw"][layer], wts["f_cb"][layer], wts["f_w_out"][layer], dxn, layer)
        da, dh = _ffn_bwd_in(dhu, wts["f_cw"][layer], wts["f_w_in"], layer)
        dw_in = _ffn_wgrad_in(hf, da, layer)
        dx, dg = _rms_bwd(x_in, [wts["f_norm"][layer:layer + 1]], [dh], dxn, f"f{layer}_norm_bwd")
        grads[f"f_w_in{layer}"] = dw_in
        grads[f"f_w_out{layer}"] = dw_out.reshape(N_SHARDS, D_FF // N_SHARDS, d)
        grads[f"f_norm{layer}"] = dg[0]
        dconv = jnp.transpose(dconv, (1, 0, 2, 3)).reshape(N_SHARDS, 8, FF_SHARD)
        grads[f"f_conv_w{layer}"] = dconv[:, 0:3, :]
        grads[f"f_conv_b{layer}"] = dconv[:, 3, :].reshape(-1)
        return dx

    dx3 = ffn_bwd(1, a1, hf1, x3, dy)
    d_o = _mm_rows(dx3, wts["b_w_o"], BF16, "o_proj_bwd", trans_w=True)
    grads["b_w_o"] = _mm_wgrad(o, dx3, "o_wgrad").reshape(N_SHARDS, d // N_SHARDS, d)
    dq, dkv, dsink, dgq, dgk = _attn_bwd(qraw, kvd, d_o, wts["gq"], wts["gk"], wts["sinks"])
    grads["b_sinks"] = dsink[:, 0].reshape(1, N_Q_HEADS)
    grads["b_q_norm"] = (dgq[0, :HEAD_DIM] + dgq[0, HEAD_DIM:]).reshape(1, HEAD_DIM)
    grads["k_norm"] = dgk[0, :HEAD_DIM] + dgk[0, HEAD_DIM:]
    grads["b_w_q"] = _mm_wgrad(hq, dq, "q_wgrad").reshape(N_SHARDS, d // N_SHARDS, d)
    dw_kv_dup = _mm_wgrad(hk, dkv, "kv_wgrad")
    grads["w_kv"] = jnp.concatenate(
        [_fold_heads(dw_kv_dup[:, :4 * LANES]), _fold_heads(dw_kv_dup[:, 4 * LANES:])], axis=1
    ).reshape(N_SHARDS, d // N_SHARDS, 2 * N_KV_HEADS * HEAD_DIM)
    dhq = _mm_rows(dq, wts["b_w_q"], F32, "q_proj_bwd", trans_w=True)
    dhk = _mm_rows(dkv, wts["w_kv_dup"], F32, "kv_proj_bwd", trans_w=True)
    dx2, dg2 = _rms_bwd(x2, [row(wts["kv_norm"]), wts["b_norm"]], [dhk, dhq], dx3, "kvq_norm_bwd")
    grads["kv_norm"] = dg2[0]
    grads["b_norm"] = dg2[1:2]
    dx1 = ffn_bwd(0, a0, hf0, x1, dx2)
    dz, y, dwc, dbs, dgv = _sgu_bwd(dx1, zpre, wts["a_w_out"], row(wts["a_v_norm"]), sgu_wc, sgu_wct, sgu_bsb)
    grads["a_w_s"] = dwc.reshape(1, N_GROUPS, CHUNK, CHUNK)
    grads["a_b_s"] = dbs[:, :, 0].reshape(1, N_GROUPS, CHUNK)
    grads["a_v_norm"] = dgv[0]
    grads["a_w_out"] = _mm_wgrad(y, dx1, "a_out_wgrad").reshape(N_SHARDS, d // N_SHARDS, d)
    tt = 512
    t = x.shape[0]
    nsub = wts["a_w_in"].shape[2]
    grads["a_w_in"] = _mm(
        h1, dz, pl.BlockSpec((tt, d), lambda s, j, kk: (kk, 0)), pl.BlockSpec((tt, nsub), lambda s, j, kk: (kk, s)),
        pl.BlockSpec((None, d, nsub), lambda s, j, kk: (s, 0, 0)), jax.ShapeDtypeStruct((N_SHARDS, d, nsub), F32),
        (N_SHARDS, 1, t // tt), TN, "a_in_wgrad")
    dh1 = _mm(
        dz, wts["a_w_in"], pl.BlockSpec((tt, nsub), lambda i, j, kk: (i, kk)),
        pl.BlockSpec((None, d, nsub), lambda i, j, kk: (kk, 0, 0)), pl.BlockSpec((tt, d), lambda i, j, kk: (i, 0)),
        jax.ShapeDtypeStruct((t, d), F32), (t // tt, 1, N_SHARDS), NT, "a_in_bwd")
    dx0, dg0 = _rms_bwd(x, [row(wts["a_norm"])], [dh1], dx1, "a_norm_bwd")
    grads["a_norm"] = dg0[0]
    return loss_lanes, dx0, grads


def kernel(x, a_norm, a_w_in, a_v_norm, a_w_s, a_b_s, a_w_out, f_norm, f_w_in, f_conv_w, f_conv_b, f_w_out, kv_norm, w_kv, k_norm, b_norm, b_w_q, b_q_norm, b_sinks, b_w_o, loss_target, m_a_norm, m_a_w_in, m_a_v_norm, m_a_w_s, m_a_b_s, m_a_w_out, m_f_norm, m_f_w_in, m_f_conv_w, m_f_conv_b, m_f_w_out, m_kv_norm, m_w_kv, m_k_norm, m_b_norm, m_b_w_q, m_b_q_norm, m_b_sinks, m_b_w_o, v_a_norm, v_a_w_in, v_a_v_norm, v_a_w_s, v_a_b_s, v_a_w_out, v_f_norm, v_f_w_in, v_f_conv_w, v_f_conv_b, v_f_w_out, v_kv_norm, v_w_kv, v_k_norm, v_b_norm, v_b_w_q, v_b_q_norm, v_b_sinks, v_b_w_o):
    d = D_MODEL
    xi, yi, ci = _coords()
    chip_idx = 2 * xi + yi

    small_sharded_shapes = [(1, LANES), (1, LANES), (2, 3, FF_SHARD)]
    small_w = _pack([a_norm, a_v_norm, f_conv_w])
    gathered = _all_gather(
        [a_w_in[0].astype(BF16), a_w_out[0].astype(BF16), f_w_in.astype(BF16), f_w_out.astype(BF16),
         w_kv.astype(BF16), b_w_q[0].astype(BF16), b_w_o[0].astype(BF16), small_w],
        [None, None, 2, 2, None, None, None, None], "gather_weights")
    g_a_in, g_a_out, g_f_in, g_f_out, g_kv, g_q, g_o, g_small = gathered
    wts = _derive_weights(
        g_a_in, g_a_out.reshape(d, d), g_f_in, g_f_out.reshape(2, D_FF, d), g_kv.reshape(d, -1), g_q.reshape(d, d),
        g_o.reshape(d, d), g_small[:, 0, :].reshape(d), g_small[:, 1, :].reshape(d),
        g_small[:, 2:35, :].reshape(N_SHARDS, 2, 3, FF_SHARD), a_w_s, a_b_s, f_norm, f_conv_b, kv_norm, k_norm,
        b_norm, b_q_norm, b_sinks)

    loss_lanes, grad_x, grads = _local_step(x[0], loss_target[0], wts)
    loss = lax.psum(loss_lanes[0, 0], ("x", "y", "c"))

    conv_g = jnp.stack([grads["f_conv_w0"], grads["f_conv_w1"]], axis=1)
    small_g = jnp.concatenate(
        [grads["a_norm"].reshape(N_SHARDS, 1, LANES), grads["a_v_norm"].reshape(N_SHARDS, 1, LANES),
         conv_g.reshape(N_SHARDS, 33, LANES), jnp.zeros((N_SHARDS, 5, LANES), F32)], axis=1)
    names = ["a_w_in", "a_w_out", "f_w_in0", "f_w_in1", "f_w_out0", "f_w_out1", "w_kv", "b_w_q", "b_w_o"]
    big = [grads[k] for k in names] + [small_g]
    recv_core = _exchange_core(big)
    c_arr = ci.reshape(1).astype(jnp.int32)
    psums = [_chip_sum(g, a, c_arr, f"chip_sum_{k}") for g, a, k in zip(big, recv_core, names + ["small"])]
    recv_chips = _exchange_chips(psums)
    owns = [lax.dynamic_index_in_dim(p, chip_idx, axis=0, keepdims=False) for p in psums]

    def update(i, w, m, v, name):
        shp = w.shape
        r2 = lambda t_: t_.reshape(-1, shp[-1])
        oth = recv_chips[i].reshape(3, -1, shp[-1])
        outs = _adamw_sharded(r2(owns[i]), oth, r2(w), r2(m), r2(v), name)
        return [o_.reshape(shp) for o_ in outs]

    res = {}
    res["a_w_in"] = update(0, a_w_in, m_a_w_in, v_a_w_in, "adamw_a_w_in")
    res["a_w_out"] = update(1, a_w_out, m_a_w_out, v_a_w_out, "adamw_a_w_out")
    f_in = [update(2 + l, f_w_in[l], m_f_w_in[l], v_f_w_in[l], f"adamw_f_w_in{l}") for l in range(2)]
    res["f_w_in"] = [jnp.stack([f_in[0][j], f_in[1][j]]) for j in range(4)]
    f_out = [update(4 + l, f_w_out[l], m_f_w_out[l], v_f_w_out[l], f"adamw_f_w_out{l}") for l in range(2)]
    res["f_w_out"] = [jnp.stack([f_out[0][j], f_out[1][j]]) for j in range(4)]
    res["w_kv"] = update(6, w_kv, m_w_kv, v_w_kv, "adamw_w_kv")
    res["b_w_q"] = update(7, b_w_q, m_b_w_q, v_b_w_q, "adamw_b_w_q")
    res["b_w_o"] = update(8, b_w_o, m_b_w_o, v_b_w_o, "adamw_b_w_o")
    small_outs = _adamw_sharded(
        owns[9], recv_chips[9], _pack([a_norm, a_v_norm, f_conv_w]), _pack([m_a_norm, m_a_v_norm, m_f_conv_w]),
        _pack([v_a_norm, v_a_v_norm, v_f_conv_w]), "adamw_small_sharded")
    for j, key in enumerate(["a_norm", "a_v_norm", "f_conv_w"]):
        res[key] = [_unpack(o_, small_sharded_shapes)[j] for o_ in small_outs]

    rep = ["a_w_s", "a_b_s", "f_norm", "f_conv_b", "kv_norm", "k_norm", "b_norm", "b_q_norm", "b_sinks"]
    rep_w = dict(a_w_s=a_w_s, a_b_s=a_b_s, f_norm=f_norm, f_conv_b=f_conv_b, kv_norm=kv_norm, k_norm=k_norm,
                 b_norm=b_norm, b_q_norm=b_q_norm, b_sinks=b_sinks)
    rep_m = dict(a_w_s=m_a_w_s, a_b_s=m_a_b_s, f_norm=m_f_norm, f_conv_b=m_f_conv_b, kv_norm=m_kv_norm,
                 k_norm=m_k_norm, b_norm=m_b_norm, b_q_norm=m_b_q_norm, b_sinks=m_b_sinks)
    rep_v = dict(a_w_s=v_a_w_s, a_b_s=v_a_b_s, f_norm=v_f_norm, f_conv_b=v_f_conv_b, kv_norm=v_kv_norm,
                 k_norm=v_k_norm, b_norm=v_b_norm, b_q_norm=v_b_q_norm, b_sinks=v_b_sinks)
    grads["f_norm"] = jnp.stack([grads["f_norm0"], grads["f_norm1"]])
    grads["f_conv_b"] = jnp.stack([grads["f_conv_b0"], grads["f_conv_b1"]])
    rep_parts = _all_gather([_pack([grads[k] for k in rep])], [None], "gather_replicated_grads")[0]
    rep_outs = _adamw_replicated(rep_parts, _pack([rep_w[k] for k in rep]), _pack([rep_m[k] for k in rep]),
                                 _pack([rep_v[k] for k in rep]), "adamw_replicated")
    rep_shapes = [rep_w[k].shape for k in rep]
    for j, key in enumerate(rep):
        res[key] = [_unpack(o_, rep_shapes)[j] for o_ in rep_outs]

    order = ["a_norm", "a_w_in", "a_v_norm", "a_w_s", "a_b_s", "a_w_out", "f_norm", "f_w_in", "f_conv_w", "f_conv_b",
             "f_w_out", "kv_norm", "w_kv", "k_norm", "b_norm", "b_w_q", "b_q_norm", "b_sinks", "b_w_o"]
    outs = [loss, grad_x[None]]
    for j in range(4):
        outs += [res[k][j] for k in order]
    return tuple(outs)
```

```python
import jax
import jax.numpy as jnp
from jax import lax
from jax.experimental import pallas as pl
from jax.experimental.pallas import tpu as pltpu

F32 = jnp.float32
BF16 = jnp.bfloat16
EPS = 1e-6
D_MODEL = 1024
CHUNK = 128
N_GROUPS = 8
N_SHARDS = 8
HEAD_DIM = 64
N_Q_HEADS = 16
N_KV_HEADS = 4
D_FF = 2816
FF_SHARD = 2 * D_FF // N_SHARDS
LANES = 128
NEG_BIG = -1e30
ADAM_LR = 0.001
ADAM_B1 = 0.9
ADAM_B2 = 0.999
ADAM_EPS = 1e-08
ADAM_WD = 0.01
ADAM_STEP = 10
VMEM_LIMIT_BYTES = 56 * 1024 * 1024
MESH = pl.DeviceIdType.MESH

NN = (((1,), (0,)), ((), ()))
NT = (((1,), (1,)), ((), ()))
TN = (((0,), (0,)), ((), ()))
SLOPES = tuple(2.0 ** (-8.0 * (h + 1) / N_Q_HEADS) for h in range(N_Q_HEADS))


def _params(sem=None):
    return pltpu.CompilerParams(dimension_semantics=sem, vmem_limit_bytes=VMEM_LIMIT_BYTES)


def _dot(a, b, dims=NN):
    return lax.dot_general(a, b, dims, preferred_element_type=F32)


def _sigmoid(x):
    return 1.0 / (1.0 + jnp.exp(-x))


def _gelu_parts(z):
    cdf = 0.5 * (1.0 + lax.erf(z * (2.0 ** -0.5)))
    pdf = jnp.exp(-0.5 * z * z) * 0.3989422804014327
    return cdf, pdf


def _coords():
    return lax.axis_index("x"), lax.axis_index("y"), lax.axis_index("c")


class _Gather:
    def __init__(self, srcs):
        self.srcs = list(srcs)
        n = len(self.srcs)
        self.out_shapes = [jax.ShapeDtypeStruct((N_SHARDS,) + s.shape, s.dtype) for s in self.srcs]
        self.sems = [pltpu.SemaphoreType.DMA((n, 7)), pltpu.SemaphoreType.DMA((n, 7)), pltpu.SemaphoreType.DMA((n,))]

    def _plan(self, src, dst, sems):
        send_sems, recv_sems, local_sems = sems
        x, y, c = _coords()
        me, sibling = (x, y, c), (x, y, 1 - c)
        chips = [(1 - x, y), (x, 1 - y), (1 - x, 1 - y)]
        n = len(src)

        def rows(e, dev):
            return dst[e].at[4 * dev[0] + 2 * dev[1] + dev[2]]

        def copy(e, slot, block, to, from_own=False):
            return pltpu.make_async_remote_copy(
                src_ref=src[e] if from_own else rows(e, block), dst_ref=rows(e, block),
                send_sem=send_sems.at[e, slot], recv_sem=recv_sems.at[e, slot], device_id=to, device_id_type=MESH)

        mine = [pltpu.make_async_copy(src[e], rows(e, me), local_sems.at[e]) for e in range(n)]
        first = []
        for e in range(n):
            first.append(copy(e, 0, me, sibling, from_own=True))
            first += [copy(e, 1 + j, me, (*chip, c), from_own=True) for j, chip in enumerate(chips)]
        return n, me, sibling, chips, c, copy, mine, first

    def start(self, src, dst, sems):
        _, _, _, _, _, _, mine, first = self._plan(src, dst, sems)
        for cp in mine + first:
            cp.start()

    def finish(self, src, dst, sems):
        n, me, sibling, chips, c, copy, mine, first = self._plan(src, dst, sems)
        passed = []
        for j, chip in enumerate(chips):
            for e in range(n):
                copy(e, 1 + j, (*chip, c), me).wait_recv()
                cp = copy(e, 4 + j, (*chip, c), sibling)
                cp.start()
                passed.append(cp)
        for e in range(n):
            copy(e, 0, sibling, me).wait_recv()
            for j, chip in enumerate(chips):
                copy(e, 4 + j, (*chip, 1 - c), me).wait_recv()
        for cp in first + passed:
            cp.wait_send()
        for cp in mine:
            cp.wait()


class _ToSibling:
    def __init__(self, grads):
        self.srcs = list(grads)
        n = len(self.srcs)
        self.out_shapes = [jax.ShapeDtypeStruct((4,) + g.shape[1:], g.dtype) for g in self.srcs]
        self.sems = [pltpu.SemaphoreType.DMA((n, 4)), pltpu.SemaphoreType.DMA((n, 4))]

    def _copies(self, src, dst, sems):
        send_sems, recv_sems = sems
        x, y, c = _coords()
        return [
            pltpu.make_async_remote_copy(
                src_ref=src[i].at[2 * q + (1 - c)], dst_ref=dst[i].at[q], send_sem=send_sems.at[i, q],
                recv_sem=recv_sems.at[i, q], device_id=(x, y, 1 - c), device_id_type=MESH)
            for i in range(len(src)) for q in range(4)]

    def start(self, src, dst, sems):
        for cp in self._copies(src, dst, sems):
            cp.start()

    def finish(self, src, dst, sems):
        for cp in self._copies(src, dst, sems):
            cp.wait()


class _ToChips:
    def __init__(self, psums, rows=None):
        self.srcs = list(psums)
        n = len(self.srcs)
        self.rows = rows
        self.out_shapes = [
            jax.ShapeDtypeStruct((3, p.shape[1] if rows is None else rows[1]) + p.shape[2:], p.dtype)
            for p in self.srcs]
        self.sems = [pltpu.SemaphoreType.DMA((n, 3)), pltpu.SemaphoreType.DMA((n, 3))]

    def _copies(self, src, dst, sems):
        send_sems, recv_sems = sems
        x, y, c = _coords()
        peers = [(x, 1 - y), (1 - x, y), (1 - x, 1 - y)]

        def part(i, q):
            if self.rows is None:
                return src[i].at[q]
            return src[i].at[q, pl.ds(self.rows[0], self.rows[1])]

        return [
            pltpu.make_async_remote_copy(
                src_ref=part(i, 2 * px + py), dst_ref=dst[i].at[r], send_sem=send_sems.at[i, r],
                recv_sem=recv_sems.at[i, r], device_id=(px, py, c), device_id_type=MESH)
            for i in range(len(src)) for r, (px, py) in enumerate(peers)]

    def start(self, src, dst, sems):
        for cp in self._copies(src, dst, sems):
            cp.start()

    def finish(self, src, dst, sems):
        for cp in self._copies(src, dst, sems):
            cp.wait()


class _ToOwners:
    def __init__(self, grads):
        self.srcs = list(grads)
        n = len(self.srcs)
        self.out_shapes = [jax.ShapeDtypeStruct(g.shape, g.dtype) for g in self.srcs]
        self.sems = [pltpu.SemaphoreType.DMA((n, 7)), pltpu.SemaphoreType.DMA((n, 7)), pltpu.SemaphoreType.DMA((n,))]

    def _copies(self, src, dst, sems):
        send_sems, recv_sems, local_sems = sems
        x, y, c = _coords()
        me = 4 * x + 2 * y + c
        copies = [pltpu.make_async_copy(src[i].at[me], dst[i].at[me], local_sems.at[i]) for i in range(len(src))]
        for i in range(len(src)):
            for rel in range(1, N_SHARDS):
                px = x ^ (rel >> 2) if rel >> 2 else x
                py = y ^ ((rel >> 1) & 1) if (rel >> 1) & 1 else y
                pc = c ^ (rel & 1) if rel & 1 else c
                copies.append(pltpu.make_async_remote_copy(
                    src_ref=src[i].at[4 * px + 2 * py + pc], dst_ref=dst[i].at[me], send_sem=send_sems.at[i, rel - 1],
                    recv_sem=recv_sems.at[i, rel - 1], device_id=(px, py, pc), device_id_type=MESH))
        return copies

    def start(self, src, dst, sems):
        for cp in self._copies(src, dst, sems):
            cp.start()

    def finish(self, src, dst, sems):
        for cp in self._copies(src, dst, sems):
            cp.wait()


class _Together:
    def __init__(self, parts):
        self.parts = list(parts)
        self.srcs = [s for p in self.parts for s in p.srcs]
        self.out_shapes = [s for p in self.parts for s in p.out_shapes]
        self.sems = [s for p in self.parts for s in p.sems]

    def _split(self, src, dst, sems):
        a = b = c = 0
        for p in self.parts:
            na, nc = len(p.srcs), len(p.sems)
            yield p, src[a:a + na], dst[b:b + na], sems[c:c + nc]
            a, b, c = a + na, b + na, c + nc

    def start(self, src, dst, sems):
        for p, s, d, m in self._split(src, dst, sems):
            p.start(s, d, m)

    def finish(self, src, dst, sems):
        for p, s, d, m in self._split(src, dst, sems):
            p.finish(s, d, m)

    def spread(self):
        b = 0
        for p in self.parts:
            p.results = self.results[b:b + len(p.srcs)]
            b += len(p.srcs)


def _call(body, args, *, grid, in_specs, out_specs, out_shape, name, scratch=(), sem=None, carry=None):
    out_shape, out_specs = list(out_shape), list(out_specs)
    if carry is None:
        return pl.pallas_call(
            body, grid=grid, in_specs=list(in_specs), out_specs=out_specs, out_shape=out_shape,
            scratch_shapes=list(scratch), name=name, compiler_params=_params(sem))(*args)
    n_in, n_out, n_scr, n_c = len(args), len(out_shape), len(scratch), len(carry.srcs)
    steps = tuple(grid)

    def carried(*refs):
        ins, rest = refs[:n_in], refs[n_in:]
        c_src, rest = rest[:n_c], rest[n_c:]
        outs, rest = rest[:n_out], rest[n_out:]
        c_dst, rest = rest[:n_c], rest[n_c:]
        scr, sems = rest[:n_scr], rest[n_scr:]
        first = pl.program_id(0) == 0
        last = pl.program_id(0) == steps[0] - 1
        for ax in range(1, len(steps)):
            first = first & (pl.program_id(ax) == 0)
            last = last & (pl.program_id(ax) == steps[ax] - 1)

        @pl.when(first)
        def _():
            carry.start(c_src, c_dst, sems)

        body(*ins, *outs, *scr)

        @pl.when(last)
        def _():
            carry.finish(c_src, c_dst, sems)

    hbm = pl.BlockSpec(memory_space=pl.ANY)
    res = pl.pallas_call(
        carried, grid=grid, in_specs=list(in_specs) + [hbm] * n_c, out_specs=out_specs + [hbm] * n_c,
        out_shape=out_shape + carry.out_shapes, scratch_shapes=list(scratch) + carry.sems, name=name,
        compiler_params=_params(("arbitrary",) * len(steps)))(*args, *carry.srcs)
    carry.results = list(res[n_out:])
    return list(res[:n_out])


def _exchange_alone(ex, name):
    n = len(ex.srcs)

    def body(*refs):
        src, dst, sems = refs[:n], refs[n:2 * n], refs[2 * n:]
        ex.start(src, dst, sems)
        ex.finish(src, dst, sems)

    hbm = pl.BlockSpec(memory_space=pl.ANY)
    res = pl.pallas_call(body, in_specs=[hbm] * n, out_specs=[hbm] * n, out_shape=ex.out_shapes,
                         scratch_shapes=ex.sems, name=name)(*ex.srcs)
    ex.results = list(res)
    return ex.results


def _rms_fwd(x, gains, name, tm=512, carry=None):
    t, d = x.shape
    n = len(gains)

    def body(*refs):
        x_ref, g_refs, h_refs = refs[0], refs[1:1 + n], refs[1 + n:]
        xf = x_ref[...]
        xhat = xf * lax.rsqrt(jnp.mean(xf * xf, axis=-1, keepdims=True) + EPS)
        for g_ref, h_ref in zip(g_refs, h_refs):
            h_ref[...] = (xhat * g_ref[...]).astype(BF16)

    row = pl.BlockSpec((tm, d), lambda i: (i, 0))
    vec = pl.BlockSpec((1, d), lambda i: (0, 0))
    return _call(body, [x, *gains], grid=(t // tm,), in_specs=[row] + [vec] * n, out_specs=[row] * n,
                 out_shape=[jax.ShapeDtypeStruct((t, d), BF16)] * n, name=name, carry=carry)


def _rms_bwd(x, gains, dhs, dres, name, tm=256, carry=None):
    t, d = x.shape
    n = len(gains)

    def body(*refs):
        x_ref, dres_ref = refs[0], refs[1]
        g_refs, dh_refs = refs[2:2 + n], refs[2 + n:2 + 2 * n]
        dx_ref, dg_ref = refs[2 + 2 * n], refs[3 + 2 * n]
        i = pl.program_id(0)

        @pl.when(i == 0)
        def _():
            dg_ref[...] = jnp.zeros_like(dg_ref)

        xf = x_ref[...]
        r = lax.rsqrt(jnp.mean(xf * xf, axis=-1, keepdims=True) + EPS)
        xhat = xf * r
        dx = dres_ref[...]
        for j in range(n):
            dh = dh_refs[j][...]
            dg_ref[j:j + 1, :] += jnp.sum(dh * xhat, axis=0, keepdims=True)
            gy = dh * g_refs[j][...]
            dx = dx + r * (gy - xhat * jnp.mean(gy * xhat, axis=-1, keepdims=True))
        dx_ref[...] = dx

    row = pl.BlockSpec((tm, d), lambda i: (i, 0))
    vec = pl.BlockSpec((1, d), lambda i: (0, 0))
    return _call(body, [x, dres, *gains, *dhs], grid=(t // tm,), in_specs=[row, row] + [vec] * n + [row] * n,
                 out_specs=[row, pl.BlockSpec((8, d), lambda i: (0, 0))],
                 out_shape=[jax.ShapeDtypeStruct((t, d), F32), jax.ShapeDtypeStruct((8, d), F32)],
                 name=name, sem=("arbitrary",), carry=carry)


def _mm(a, b, a_spec, b_spec, o_spec, out_shape, grid, dims, name, res=None, res_spec=None, carry=None):
    nk = grid[2]
    acc_shape = tuple(s for s in o_spec.block_shape if s is not None)

    def body(*refs):
        if res is None:
            a_ref, b_ref, o_ref, acc_ref = refs
        else:
            a_ref, b_ref, r_ref, o_ref, acc_ref = refs
        k = pl.program_id(2)
        p = _dot(a_ref[...].astype(BF16), b_ref[...].astype(BF16), dims)

        @pl.when(k == 0)
        def _():
            acc_ref[...] = p

        @pl.when(k > 0)
        def _():
            acc_ref[...] += p

        @pl.when(k == nk - 1)
        def _():
            out = acc_ref[...]
            if res is not None:
                out = out + r_ref[...]
            o_ref[...] = out.astype(o_ref.dtype)

    ins = [a, b] + ([res] if res is not None else [])
    specs = [a_spec, b_spec] + ([res_spec] if res is not None else [])
    return _call(body, ins, grid=grid, in_specs=specs, out_specs=[o_spec], out_shape=[out_shape],
                 scratch=[pltpu.VMEM(acc_shape, F32)], name=name, sem=("parallel", "parallel", "arbitrary"),
                 carry=carry)[0]


def _mm_rows(a, w, out_dtype, name, trans_w=False, res=None, tm=512, carry=None):
    t, k = a.shape
    n = w.shape[0] if trans_w else w.shape[1]
    return _mm(
        a, w, pl.BlockSpec((tm, k), lambda i, j, kk: (i, 0)), pl.BlockSpec(w.shape, lambda i, j, kk: (0, 0)),
        pl.BlockSpec((tm, n), lambda i, j, kk: (i, 0)), jax.ShapeDtypeStruct((t, n), out_dtype), (t // tm, 1, 1),
        NT if trans_w else NN, name, res=res,
        res_spec=None if res is None else pl.BlockSpec((tm, n), lambda i, j, kk: (i, 0)), carry=carry)


def _mm_wgrad(a, b, name, tt=512, carry=None):
    t, m = a.shape
    n = b.shape[1]
    return _mm(
        a, b, pl.BlockSpec((tt, m), lambda i, j, kk: (kk, 0)), pl.BlockSpec((tt, n), lambda i, j, kk: (kk, 0)),
        pl.BlockSpec((m, n), lambda i, j, kk: (0, 0)), jax.ShapeDtypeStruct((m, n), F32), (1, 1, t // tt), TN, name,
        carry=carry)


def _sgu_fwd(x0, h1, w_in, g_v, w_c, b_sb, w_out, tm=256, carry=None):
    t, d = x0.shape
    nsub = w_in.shape[2]

    def body(x_ref, h_ref, win_ref, gv_ref, wc_ref, bsb_ref, wout_ref, zpre_ref, x1_ref, u_s, v_s, vn_s, y_s):
        h = h_ref[...]
        for k in range(N_SHARDS):
            zk = _dot(h, win_ref[k])
            zpre_ref[:, k * nsub:(k + 1) * nsub] = zk
            cdf, _ = _gelu_parts(zk)
            if k < N_SHARDS // 2:
                u_s[:, k * nsub:(k + 1) * nsub] = zk * cdf
            else:
                v_s[:, (k - 4) * nsub:(k - 3) * nsub] = zk * cdf
        v = v_s[...]
        rv = lax.rsqrt(jnp.mean(v * v, axis=-1, keepdims=True) + EPS)
        vn_s[...] = (v * rv * gv_ref[...]).astype(BF16)
        for ci in range(tm // CHUNK):
            rows = slice(ci * CHUNK, (ci + 1) * CHUNK)
            for g in range(N_GROUPS):
                cols = slice(g * LANES, (g + 1) * LANES)
                sv = _dot(wc_ref[g], vn_s[rows, cols]) + bsb_ref[g]
                y_s[rows, cols] = (u_s[rows, cols] * sv).astype(BF16)
        x1_ref[...] = x_ref[...] + _dot(y_s[...], wout_ref[...])

    row = pl.BlockSpec((tm, d), lambda i: (i, 0))
    full = lambda a: pl.BlockSpec(a.shape, lambda i: (0,) * a.ndim)
    return _call(
        body, [x0, h1, w_in, g_v, w_c, b_sb, w_out], grid=(t // tm,),
        in_specs=[row, row, full(w_in), full(g_v), full(w_c), full(b_sb), full(w_out)],
        out_specs=[pl.BlockSpec((tm, 2 * d), lambda i: (i, 0)), row],
        out_shape=[jax.ShapeDtypeStruct((t, 2 * d), F32), jax.ShapeDtypeStruct((t, d), F32)],
        scratch=[pltpu.VMEM((tm, d), F32), pltpu.VMEM((tm, d), F32), pltpu.VMEM((tm, d), BF16),
                 pltpu.VMEM((tm, d), BF16)],
        name="sgu_fwd", carry=carry)


def _sgu_bwd(dx1, zpre, w_out, g_v, w_c, w_ct, b_sb, tm=256, carry=None):
    t, d = dx1.shape

    def body(dx_ref, zpre_ref, wout_ref, gv_ref, wc_ref, wct_ref, bsb_ref,
             dz_ref, y_ref, dwc_ref, dbs_ref, dgv_ref, u_s, vn_s, dy_s, du_s, dvn_s):
        i = pl.program_id(0)

        @pl.when(i == 0)
        def _():
            dwc_ref[...] = jnp.zeros_like(dwc_ref)
            dbs_ref[...] = jnp.zeros_like(dbs_ref)
            dgv_ref[...] = jnp.zeros_like(dgv_ref)

        dy_s[...] = _dot(dx_ref[...].astype(BF16), wout_ref[...], NT)
        zu = zpre_ref[:, :d]
        zv = zpre_ref[:, d:]
        cdf_u, pdf_u = _gelu_parts(zu)
        cdf_v, pdf_v = _gelu_parts(zv)
        u_s[...] = zu * cdf_u
        v = zv * cdf_v
        rv = lax.rsqrt(jnp.mean(v * v, axis=-1, keepdims=True) + EPS)
        vhat = v * rv
        gv = gv_ref[...]
        vn_s[...] = (vhat * gv).astype(BF16)
        for ci in range(tm // CHUNK):
            rows = slice(ci * CHUNK, (ci + 1) * CHUNK)
            for g in range(N_GROUPS):
                cols = slice(g * LANES, (g + 1) * LANES)
                vnb = vn_s[rows, cols]
                sv = _dot(wc_ref[g], vnb) + bsb_ref[g]
                dyb = dy_s[rows, cols]
                ub = u_s[rows, cols]
                dsv = dyb * ub
                du_s[rows, cols] = dyb * sv
                y_ref[rows, cols] = (ub * sv).astype(BF16)
                dsvb = dsv.astype(BF16)
                dbs_ref[g] += dsv
                dwc_ref[g] += _dot(dsvb, vnb, NT)
                dvn_s[rows, cols] = _dot(wct_ref[g], dsvb)
        dvn = dvn_s[...]
        dgv_ref[0:1, :] += jnp.sum(dvn * vhat, axis=0, keepdims=True)
        gy = dvn * gv
        dv = rv * (gy - vhat * jnp.mean(gy * vhat, axis=-1, keepdims=True))
        dz_ref[:, :d] = (du_s[...] * (cdf_u + zu * pdf_u)).astype(BF16)
        dz_ref[:, d:] = (dv * (cdf_v + zv * pdf_v)).astype(BF16)

        @pl.when(i == t // tm - 1)
        def _():
            tri = (lax.broadcasted_iota(jnp.int32, (CHUNK, CHUNK), 0)
                   >= lax.broadcasted_iota(jnp.int32, (CHUNK, CHUNK), 1))
            for g in range(N_GROUPS):
                dwc_ref[g] = jnp.where(tri, dwc_ref[g], 0.0)
                dbs_ref[g] = jnp.broadcast_to(jnp.sum(dbs_ref[g], axis=1, keepdims=True), (CHUNK, CHUNK))

    row = pl.BlockSpec((tm, d), lambda i: (i, 0))
    row2 = pl.BlockSpec((tm, 2 * d), lambda i: (i, 0))
    full = lambda a: pl.BlockSpec(a.shape, lambda i: (0,) * a.ndim)
    grp = pl.BlockSpec((N_GROUPS, CHUNK, CHUNK), lambda i: (0, 0, 0))
    return _call(
        body, [dx1, zpre, w_out, g_v, w_c, w_ct, b_sb], grid=(t // tm,),
        in_specs=[row, row2, full(w_out), full(g_v), full(w_c), full(w_ct), full(b_sb)],
        out_specs=[row2, row, grp, grp, pl.BlockSpec((8, d), lambda i: (0, 0))],
        out_shape=[jax.ShapeDtypeStruct((t, 2 * d), BF16), jax.ShapeDtypeStruct((t, d), BF16),
                   jax.ShapeDtypeStruct((N_GROUPS, CHUNK, CHUNK), F32),
                   jax.ShapeDtypeStruct((N_GROUPS, CHUNK, CHUNK), F32), jax.ShapeDtypeStruct((8, d), F32)],
        scratch=[pltpu.VMEM((tm, d), F32), pltpu.VMEM((tm, d), BF16), pltpu.VMEM((tm, d), F32),
                 pltpu.VMEM((tm, d), F32), pltpu.VMEM((tm, d), F32)],
        name="sgu_bwd", sem=("arbitrary",), carry=carry)


def _causal_conv(a_ref, prev_ref, cw, cb, first, tm):
    af = a_ref[...].astype(F32)
    keep = jnp.where(first, 0.0, 1.0)
    pv = prev_ref[...].astype(F32)
    p1 = pv[15:16, :] * keep
    p2 = pv[14:15, :] * keep
    row = lax.broadcasted_iota(jnp.int32, af.shape, 0)
    a1 = jnp.where(row == 0, p1, pltpu.roll(af, 1, 0))
    a2 = jnp.where(row == 0, p2, jnp.where(row == 1, p1, pltpu.roll(af, 2, 0)))
    hu = cw[2:3, :] * af + cw[1:2, :] * a1 + cw[0:1, :] * a2 + cb
    return hu, af, a1, a2


def _ffn_in(hf, w_in, layer, tm=512, carry=None):
    t, d = hf.shape
    return _mm(
        hf, w_in, pl.BlockSpec((tm, d), lambda s, i, kk: (i, 0)),
        pl.BlockSpec((None, d, FF_SHARD), lambda s, i, kk: (s, 0, 0)),
        pl.BlockSpec((None, tm, FF_SHARD), lambda s, i, kk: (s, i, 0)),
        jax.ShapeDtypeStruct((N_SHARDS, t, FF_SHARD), BF16), (N_SHARDS, t // tm, 1), NN, f"ffn{layer}_in", carry=carry)


def _ffn_conv_specs(tm, gate_of, tile_of):
    def specs(shard_of):
        return [
            pl.BlockSpec((None, tm, FF_SHARD), lambda *g: (shard_of(*g), tile_of(*g), 0)),
            pl.BlockSpec((None, 16, FF_SHARD),
                         lambda *g: (shard_of(*g), jnp.maximum(tile_of(*g) * (tm // 16) - 1, 0), 0)),
            pl.BlockSpec((None, 8, FF_SHARD), lambda *g: (shard_of(*g), 0, 0)),
            pl.BlockSpec((None, 1, FF_SHARD), lambda *g: (shard_of(*g), 0, 0)),
        ]
    return specs(gate_of) + specs(lambda *g: gate_of(*g) + N_SHARDS // 2)


def _ffn_out(a, cw, cb, w_out, x, layer, tm=512, carry=None):
    t, d = x.shape
    nc = N_SHARDS // 2

    def body(ag_ref, pg_ref, cwg_ref, cbg_ref, au_ref, pu_ref, cwu_ref, cbu_ref, wout_ref, x_ref, o_ref, acc_ref):
        i, c = pl.program_id(0), pl.program_id(1)
        hg = _causal_conv(ag_ref, pg_ref, cwg_ref[...], cbg_ref[...], i == 0, tm)[0]
        hu = _causal_conv(au_ref, pu_ref, cwu_ref[...], cbu_ref[...], i == 0, tm)[0]
        act = (hg * _sigmoid(hg) * hu).astype(BF16)
        p = _dot(act, wout_ref[...])

        @pl.when(c == 0)
        def _():
            acc_ref[...] = x_ref[...] + p

        @pl.when(c > 0)
        def _():
            acc_ref[...] += p

        @pl.when(c == nc - 1)
        def _():
            o_ref[...] = acc_ref[...]

    row = pl.BlockSpec((tm, d), lambda i, c: (i, 0))
    return _call(
        body, [a, a, cw, cb, a, a, cw, cb, w_out, x], grid=(t // tm, nc),
        in_specs=_ffn_conv_specs(tm, lambda i, c: c, lambda i, c: i)
        + [pl.BlockSpec((FF_SHARD, d), lambda i, c: (c, 0)), row],
        out_specs=[row], out_shape=[jax.ShapeDtypeStruct((t, d), F32)],
        scratch=[pltpu.VMEM((tm, d), F32)], name=f"ffn{layer}_out", sem=("parallel", "arbitrary"), carry=carry)[0]


def _ffn_bwd_act(a, cw, cb, w_out, dxn, layer, tm=512, carry=None):
    t, d = dxn.shape
    nc = N_SHARDS // 2

    def body(ag_ref, pg_ref, cwg_ref, cbg_ref, au_ref, pu_ref, cwu_ref, cbu_ref, wout_ref, dx_ref,
             dhu_ref, dw_ref, dconv_ref):
        i = pl.program_id(1)

        @pl.when(i == 0)
        def _():
            dw_ref[...] = jnp.zeros_like(dw_ref)
            dconv_ref[...] = jnp.zeros_like(dconv_ref)

        hg, ag0, ag1, ag2 = _causal_conv(ag_ref, pg_ref, cwg_ref[...], cbg_ref[...], i == 0, tm)
        hu, au0, au1, au2 = _causal_conv(au_ref, pu_ref, cwu_ref[...], cbu_ref[...], i == 0, tm)
        sg = _sigmoid(hg)
        sl = hg * sg
        dxb = dx_ref[...].astype(BF16)
        dact = _dot(dxb, wout_ref[...], NT)
        dw_ref[...] += _dot((sl * hu).astype(BF16), dxb, TN)
        d_up = dact * sl
        d_gate = dact * hu * (sg * (1.0 + hg * (1.0 - sg)))
        for j, (dv, taps) in enumerate(((d_gate, (ag2, ag1, ag0)), (d_up, (au2, au1, au0)))):
            dvb = dv.astype(BF16)
            dhu_ref[j] = dvb
            dvr = dvb.astype(F32)
            for k in range(3):
                dconv_ref[j, k:k + 1, :] += jnp.sum(dvr * taps[k], axis=0, keepdims=True)
            dconv_ref[j, 3:4, :] += jnp.sum(dv, axis=0, keepdims=True)

    return _call(
        body, [a, a, cw, cb, a, a, cw, cb, w_out, dxn], grid=(nc, t // tm),
        in_specs=_ffn_conv_specs(tm, lambda c, i: c, lambda c, i: i)
        + [pl.BlockSpec((FF_SHARD, d), lambda c, i: (c, 0)), pl.BlockSpec((tm, d), lambda c, i: (i, 0))],
        out_specs=[pl.BlockSpec((None, 2, tm, FF_SHARD), lambda c, i: (c, 0, i, 0)),
                   pl.BlockSpec((FF_SHARD, d), lambda c, i: (c, 0)),
                   pl.BlockSpec((None, 2, 8, FF_SHARD), lambda c, i: (c, 0, 0, 0))],
        out_shape=[jax.ShapeDtypeStruct((nc, 2, t, FF_SHARD), BF16), jax.ShapeDtypeStruct((D_FF, d), F32),
                   jax.ShapeDtypeStruct((nc, 2, 8, FF_SHARD), F32)],
        name=f"ffn{layer}_bwd_act", sem=("parallel", "arbitrary"), carry=carry)


def _ffn_bwd_in(dhu, cw, w_in, layer, tm=512, carry=None):
    nc, _, t, _ = dhu.shape
    d = D_MODEL
    last_blk = t // 16 - 1

    def body(dh_ref, nx_ref, cw_ref, win_ref, da_ref, o_ref):
        i, s = pl.program_id(0), pl.program_id(1)
        df = dh_ref[...].astype(F32)
        keep = jnp.where(i == t // tm - 1, 0.0, 1.0)
        nx = nx_ref[...].astype(F32)
        n0 = nx[0:1, :] * keep
        n1 = nx[1:2, :] * keep
        row = lax.broadcasted_iota(jnp.int32, df.shape, 0)
        d1 = jnp.where(row == tm - 1, n0, pltpu.roll(df, tm - 1, 0))
        d2 = jnp.where(row == tm - 1, n1, jnp.where(row == tm - 2, n0, pltpu.roll(df, tm - 2, 0)))
        cw = cw_ref[...]
        da = (cw[2:3, :] * df + cw[1:2, :] * d1 + cw[0:1, :] * d2).astype(BF16)
        da_ref[...] = da
        p = _dot(da, win_ref[...], NT)

        @pl.when(s == 0)
        def _():
            o_ref[...] = p

        @pl.when(s > 0)
        def _():
            o_ref[...] += p

    return _call(
        body, [dhu, dhu, cw, w_in], grid=(t // tm, N_SHARDS),
        in_specs=[pl.BlockSpec((None, None, tm, FF_SHARD), lambda i, s: (s % nc, s // nc, i, 0)),
                  pl.BlockSpec((None, None, 16, FF_SHARD),
                               lambda i, s: (s % nc, s // nc, jnp.minimum((i + 1) * (tm // 16), last_blk), 0)),
                  pl.BlockSpec((None, 8, FF_SHARD), lambda i, s: (s, 0, 0)),
                  pl.BlockSpec((None, d, FF_SHARD), lambda i, s: (s, 0, 0))],
        out_specs=[pl.BlockSpec((None, tm, FF_SHARD), lambda i, s: (s, i, 0)),
                   pl.BlockSpec((tm, d), lambda i, s: (i, 0))],
        out_shape=[jax.ShapeDtypeStruct((N_SHARDS, t, FF_SHARD), BF16), jax.ShapeDtypeStruct((t, d), F32)],
        name=f"ffn{layer}_bwd_in", sem=("parallel", "arbitrary"), carry=carry)


def _ffn_wgrad_in(hf, da, layer, tt=512, carry=None):
    t, d = hf.shape
    return _mm(
        hf, da, pl.BlockSpec((tt, d), lambda s, j, kk: (kk, 0)),
        pl.BlockSpec((None, tt, FF_SHARD), lambda s, j, kk: (s, kk, 0)),
        pl.BlockSpec((None, d, FF_SHARD), lambda s, j, kk: (s, 0, 0)),
        jax.ShapeDtypeStruct((N_SHARDS, d, FF_SHARD), F32), (N_SHARDS, 1, t // tt), TN, f"ffn{layer}_wgrad_in",
        carry=carry)


def _attn_masks(n):
    lane = lax.broadcasted_iota(jnp.int32, (CHUNK, LANES), 1)
    lo = lane < HEAD_DIM
    tq = lax.broadcasted_iota(jnp.int32, (CHUNK, 2 * CHUNK), 0)
    jk = lax.broadcasted_iota(jnp.int32, (CHUNK, 2 * CHUNK), 1)
    dist = tq + CHUNK - jk
    mask = (dist >= 0) & (dist < CHUNK) & (jk >= jnp.where(n == 0, CHUNK, 0))
    return lo, mask, dist.astype(F32)


def _half_sum(x, lo):
    s_lo = jnp.sum(jnp.where(lo, x, 0.0), axis=-1, keepdims=True)
    s_hi = jnp.sum(jnp.where(lo, 0.0, x), axis=-1, keepdims=True)
    return jnp.where(lo, s_lo, s_hi)


def _attn_probs(qh, kn, mask, distf, slope, sink):
    s = _dot(qh, kn, NT) * (HEAD_DIM ** -0.5)
    s = jnp.where(mask, s - slope * distf, NEG_BIG)
    m = jnp.maximum(jnp.max(s, axis=-1, keepdims=True), sink)
    e = jnp.exp(s - m)
    den = jnp.sum(e, axis=-1, keepdims=True) + jnp.exp(sink - m)
    return e / den, m, den


def _attn_fwd(qraw, kvd, gq, gk, sinks, carry=None):
    t, d = qraw.shape
    nb = t // CHUNK

    def body(sink_ref, q_ref, cur_ref, prev_ref, gq_ref, gk_ref, o_ref):
        n = pl.program_id(0)
        lo, mask, distf = _attn_masks(n)
        gq_v, gk_v = gq_ref[...], gk_ref[...]
        for kvh in range(N_KV_HEADS):
            ks = slice(kvh * LANES, (kvh + 1) * LANES)
            vs = slice(4 * LANES + kvh * LANES, 4 * LANES + (kvh + 1) * LANES)
            kraw = jnp.concatenate([prev_ref[:, ks], cur_ref[:, ks]], axis=0)
            rk = lax.rsqrt(jnp.mean(kraw * kraw, axis=-1, keepdims=True) + EPS)
            kn = (kraw * rk * gk_v).astype(BF16)
            vv = jnp.concatenate([prev_ref[:, vs], cur_ref[:, vs]], axis=0).astype(BF16)
            for p in range(2):
                jq = 2 * kvh + p
                qp = q_ref[:, jq * LANES:(jq + 1) * LANES]
                r = lax.rsqrt(_half_sum(qp * qp, lo) * (1.0 / HEAD_DIM) + EPS)
                qn = qp * r * gq_v
                acc = None
                for half in range(2):
                    h = 4 * kvh + 2 * p + half
                    sel = lo if half == 0 else jnp.logical_not(lo)
                    qh = jnp.where(sel, qn, 0.0).astype(BF16)
                    pf, _, _ = _attn_probs(qh, kn, mask, distf, SLOPES[h], sink_ref[h])
                    oh = _dot(pf.astype(BF16), vv)
                    acc = oh if half == 0 else jnp.where(lo, acc, oh)
                o_ref[:, jq * LANES:(jq + 1) * LANES] = acc.astype(BF16)

    blk = lambda f: pl.BlockSpec((CHUNK, d), f)
    vec = pl.BlockSpec((1, LANES), lambda n: (0, 0))
    return _call(
        body, [sinks, qraw, kvd, kvd, gq, gk], grid=(nb,),
        in_specs=[pl.BlockSpec(memory_space=pltpu.SMEM), blk(lambda n: (n, 0)), blk(lambda n: (n, 0)),
                  blk(lambda n: (jnp.maximum(n - 1, 0), 0)), vec, vec],
        out_specs=[blk(lambda n: (n, 0))], out_shape=[jax.ShapeDtypeStruct((t, d), BF16)],
        name="attn_fwd", carry=carry)[0]


def _attn_bwd(qraw, kvd, d_o, gq, gk, sinks, carry=None):
    t, d = qraw.shape
    nb = t // CHUNK

    def body(sink_ref, q_ref, cur_ref, prev_ref, do_ref, gq_ref, gk_ref,
             dq_ref, dkv_ref, dsink_ref, dgq_ref, dgk_ref, carry_s, pp_s, cp_s):
        n = pl.program_id(0)

        @pl.when(n == 0)
        def _():
            carry_s[...] = jnp.zeros_like(carry_s)
            dsink_ref[...] = jnp.zeros_like(dsink_ref)
            dgq_ref[...] = jnp.zeros_like(dgq_ref)
            dgk_ref[...] = jnp.zeros_like(dgk_ref)

        @pl.when(n < nb)
        def _():
            lo, mask, distf = _attn_masks(n)
            gq_v, gk_v = gq_ref[...], gk_ref[...]
            for kvh in range(N_KV_HEADS):
                ks = slice(kvh * LANES, (kvh + 1) * LANES)
                vs = slice(4 * LANES + kvh * LANES, 4 * LANES + (kvh + 1) * LANES)
                kraw = jnp.concatenate([prev_ref[:, ks], cur_ref[:, ks]], axis=0)
                rk = lax.rsqrt(jnp.mean(kraw * kraw, axis=-1, keepdims=True) + EPS)
                khat = kraw * rk
                kn = (khat * gk_v).astype(BF16)
                vv = jnp.concatenate([prev_ref[:, vs], cur_ref[:, vs]], axis=0).astype(BF16)
                dkn = jnp.zeros((2 * CHUNK, LANES), F32)
                dvb = jnp.zeros((2 * CHUNK, LANES), F32)
                for p in range(2):
                    jq = 2 * kvh + p
                    cols = slice(jq * LANES, (jq + 1) * LANES)
                    qp = q_ref[:, cols]
                    r = lax.rsqrt(_half_sum(qp * qp, lo) * (1.0 / HEAD_DIM) + EPS)
                    qhat = qp * r
                    qn = qhat * gq_v
                    dop = do_ref[:, cols]
                    dqn = None
                    for half in range(2):
                        h = 4 * kvh + 2 * p + half
                        sel = lo if half == 0 else jnp.logical_not(lo)
                        qh = jnp.where(sel, qn, 0.0).astype(BF16)
                        doh = jnp.where(sel, dop, jnp.zeros_like(dop))
                        sink = sink_ref[h]
                        pf, m, den = _attn_probs(qh, kn, mask, distf, SLOPES[h], sink)
                        dp = _dot(doh, vv, NT)
                        delta = jnp.sum(pf * dp, axis=-1, keepdims=True)
                        p_sink = jnp.exp(sink - m) / den
                        dsink_ref[h:h + 1, :] -= jnp.broadcast_to(
                            jnp.sum(p_sink * delta, axis=0, keepdims=True), (1, LANES))
                        ds = (pf * (dp - delta) * (HEAD_DIM ** -0.5)).astype(BF16)
                        dqh = _dot(ds, kn)
                        dqn = dqh if half == 0 else jnp.where(lo, dqn, dqh)
                        dkn = dkn + _dot(ds, qh, TN)
                        dvb = dvb + _dot(pf.astype(BF16), doh, TN)
                    dgq_ref[0:1, :] += jnp.sum(dqn * qhat, axis=0, keepdims=True)
                    gy = dqn * gq_v
                    mq = _half_sum(gy * qhat, lo) * (1.0 / HEAD_DIM)
                    dq_ref[:, cols] = (r * (gy - qhat * mq)).astype(BF16)
                dgk_ref[0:1, :] += jnp.sum(dkn * khat, axis=0, keepdims=True)
                gyk = dkn * gk_v
                dkraw = rk * (gyk - khat * jnp.mean(gyk * khat, axis=-1, keepdims=True))
                pp_s[:, ks] = dkraw[:CHUNK]
                cp_s[:, ks] = dkraw[CHUNK:]
                pp_s[:, vs] = dvb[:CHUNK]
                cp_s[:, vs] = dvb[CHUNK:]
            dkv_ref[...] = (carry_s[...] + pp_s[...]).astype(BF16)
            carry_s[...] = cp_s[...]

        @pl.when(n == nb)
        def _():
            dkv_ref[...] = carry_s[...].astype(BF16)

    blk = lambda f: pl.BlockSpec((CHUNK, d), f)
    vec = pl.BlockSpec((1, LANES), lambda n: (0, 0))
    cur = lambda n: (jnp.minimum(n, nb - 1), 0)
    prev = lambda n: (jnp.maximum(jnp.minimum(n, nb - 1) - 1, 0), 0)
    small = lambda r: pl.BlockSpec((r, LANES), lambda n: (0, 0))
    return _call(
        body, [sinks, qraw, kvd, kvd, d_o, gq, gk], grid=(nb + 1,),
        in_specs=[pl.BlockSpec(memory_space=pltpu.SMEM), blk(cur), blk(cur), blk(prev), blk(cur), vec, vec],
        out_specs=[blk(cur), blk(lambda n: (jnp.maximum(n - 1, 0), 0)), small(N_Q_HEADS), small(8), small(8)],
        out_shape=[jax.ShapeDtypeStruct((t, d), BF16), jax.ShapeDtypeStruct((t, d), BF16),
                   jax.ShapeDtypeStruct((N_Q_HEADS, LANES), F32), jax.ShapeDtypeStruct((8, LANES), F32),
                   jax.ShapeDtypeStruct((8, LANES), F32)],
        scratch=[pltpu.VMEM((CHUNK, d), F32)] * 3, name="attn_bwd", sem=("arbitrary",), carry=carry)


def _loss_head(y, target, tm=512):
    t, d = y.shape

    def body(y_ref, t_ref, dy_ref, loss_ref, acc_ref):
        i = pl.program_id(0)

        @pl.when(i == 0)
        def _():
            acc_ref[...] = jnp.zeros_like(acc_ref)

        err = y_ref[...] - t_ref[...]
        dy_ref[...] = err * (1.0 / d)
        acc_ref[...] += jnp.sum(err * err, axis=0, keepdims=True)

        @pl.when(i == t // tm - 1)
        def _():
            loss_ref[...] = jnp.broadcast_to(0.5 / d * jnp.sum(acc_ref[...], axis=1, keepdims=True), loss_ref.shape)

    row = pl.BlockSpec((tm, d), lambda i: (i, 0))
    return _call(
        body, [y, target], grid=(t // tm,), in_specs=[row, row],
        out_specs=[row, pl.BlockSpec((8, LANES), lambda i: (0, 0))],
        out_shape=[jax.ShapeDtypeStruct((t, d), F32), jax.ShapeDtypeStruct((8, LANES), F32)],
        scratch=[pltpu.VMEM((1, d), F32)], name="loss_head", sem=("arbitrary",))


def _adamw_math(g, w, m, v):
    m = ADAM_B1 * m + (1.0 - ADAM_B1) * g
    v = ADAM_B2 * v + (1.0 - ADAM_B2) * (g * g)
    m_hat = m / (1.0 - ADAM_B1 ** ADAM_STEP)
    v_hat = v / (1.0 - ADAM_B2 ** ADAM_STEP)
    delta = -ADAM_LR * (m_hat / (jnp.sqrt(v_hat) + ADAM_EPS) + ADAM_WD * w)
    return delta, m, v


def _row_tile(r, cap=128):
    for tr in range(min(r, cap), 0, -1):
        if r % tr == 0 and (tr % 8 == 0 or tr == r):
            return tr
    return r


def _chip_sum(grad, recv, core_idx, name):
    _, r, c = grad.shape
    tr = _row_tile(r, 256)

    def body(c_ref, g_ref, a_ref, p_ref):
        p_ref[...] = g_ref[...] + a_ref[...]

    return pl.pallas_call(
        body,
        grid_spec=pltpu.PrefetchScalarGridSpec(
            num_scalar_prefetch=1, grid=(4, r // tr),
            in_specs=[pl.BlockSpec((None, None, tr, c), lambda q, i, cr: (q, cr[0], i, 0)),
                      pl.BlockSpec((None, tr, c), lambda q, i, cr: (q, i, 0))],
            out_specs=pl.BlockSpec((None, tr, c), lambda q, i, cr: (q, i, 0))),
        out_shape=jax.ShapeDtypeStruct((4, r, c), F32), name=name, compiler_params=_params(),
    )(core_idx, grad.reshape(4, 2, r, c), recv)


def _adamw_sharded(psum, others, chip_idx, w, m, v, name):
    r, c = w.shape
    tr = _row_tile(min(o.shape[1] for o in others))
    starts, r0 = [], 0
    for o in others:
        starts.append(r0 // tr)
        r0 += o.shape[1]
    n_oth = len(others)

    def body(q_ref, own_ref, *refs):
        oth_refs = refs[:n_oth]
        w_ref, m_ref, v_ref, g_ref, d_ref, nm_ref, nv_ref = refs[n_oth:]
        i = pl.program_id(0)
        oth = oth_refs[0][...]
        for k in range(1, n_oth):
            oth = jnp.where(i >= starts[k], oth_refs[k][...], oth)
        g = ((own_ref[...] + oth[0]) + oth[1]) + oth[2]
        delta, nm, nv = _adamw_math(g, w_ref[...], m_ref[...], v_ref[...])
        g_ref[...] = g
        d_ref[...] = delta
        nm_ref[...] = nm
        nv_ref[...] = nv

    def oth_spec(k):
        nblk = others[k].shape[1] // tr
        return pl.BlockSpec((3, tr, c), lambda i, q: (0, jnp.clip(i - starts[k], 0, nblk - 1), 0))

    row = pl.BlockSpec((tr, c), lambda i, q: (i, 0))
    return pl.pallas_call(
        body,
        grid_spec=pltpu.PrefetchScalarGridSpec(
            num_scalar_prefetch=1, grid=(r // tr,),
            in_specs=[pl.BlockSpec((None, tr, c), lambda i, q: (q[0], i, 0))]
            + [oth_spec(k) for k in range(n_oth)] + [row, row, row],
            out_specs=[row] * 4),
        out_shape=[jax.ShapeDtypeStruct((r, c), F32)] * 4, name=name, compiler_params=_params(),
    )(chip_idx, psum, *others, w, m, v)


def _adamw_replicated(parts, w, m, v, name):
    r, c = w.shape
    tr = _row_tile(r)

    def body(p_ref, w_ref, m_ref, v_ref, g_ref, d_ref, nm_ref, nv_ref):
        g = p_ref[0]
        for k in range(1, N_SHARDS):
            g = g + p_ref[k]
        delta, nm, nv = _adamw_math(g, w_ref[...], m_ref[...], v_ref[...])
        g_ref[...] = g
        d_ref[...] = delta
        nm_ref[...] = nm
        nv_ref[...] = nv

    row = pl.BlockSpec((tr, c), lambda i: (i, 0))
    return _call(body, [parts, w, m, v], grid=(r // tr,),
                 in_specs=[pl.BlockSpec((N_SHARDS, tr, c), lambda i: (0, i, 0)), row, row, row],
                 out_specs=[row] * 4, out_shape=[jax.ShapeDtypeStruct((r, c), F32)] * 4, name=name)


def _as_rows(a):
    flat = a.reshape(-1).astype(F32)
    pad = (-flat.shape[0]) % LANES
    if pad:
        flat = jnp.concatenate([flat, jnp.zeros((pad,), F32)])
    return flat.reshape(-1, LANES)


def _pack(arrays):
    rows = jnp.concatenate([_as_rows(a) for a in arrays], axis=0)
    pad = (-rows.shape[0]) % 8
    if pad:
        rows = jnp.concatenate([rows, jnp.zeros((pad, LANES), F32)], axis=0)
    return rows


def _unpack(rows, shapes):
    out, r0 = [], 0
    for shp in shapes:
        size = 1
        for s in shp:
            size *= s
        nr = -(-size // LANES)
        out.append(rows[r0:r0 + nr].reshape(-1)[:size].reshape(shp))
        r0 += nr
    return out


def _dup_heads(w):
    lead = w.shape[:-1]
    w4 = w.reshape(lead + (N_KV_HEADS, 1, HEAD_DIM))
    return jnp.broadcast_to(w4, lead + (N_KV_HEADS, 2, HEAD_DIM)).reshape(lead + (N_KV_HEADS * LANES,))


def _fold_heads(g):
    lead = g.shape[:-1]
    return g.reshape(lead + (N_KV_HEADS, 2, HEAD_DIM)).sum(axis=-2).reshape(lead + (N_KV_HEADS * HEAD_DIM,))


def kernel(x, a_norm, a_w_in, a_v_norm, a_w_s, a_b_s, a_w_out, f_norm, f_w_in, f_conv_w, f_conv_b, f_w_out, kv_norm, w_kv, k_norm, b_norm, b_w_q, b_q_norm, b_sinks, b_w_o, loss_target, m_a_norm, m_a_w_in, m_a_v_norm, m_a_w_s, m_a_b_s, m_a_w_out, m_f_norm, m_f_w_in, m_f_conv_w, m_f_conv_b, m_f_w_out, m_kv_norm, m_w_kv, m_k_norm, m_b_norm, m_b_w_q, m_b_q_norm, m_b_sinks, m_b_w_o, v_a_norm, v_a_w_in, v_a_v_norm, v_a_w_s, v_a_b_s, v_a_w_out, v_f_norm, v_f_w_in, v_f_conv_w, v_f_conv_b, v_f_w_out, v_kv_norm, v_w_kv, v_k_norm, v_b_norm, v_b_w_q, v_b_q_norm, v_b_sinks, v_b_w_o):
    d = D_MODEL
    xi, yi, ci = _coords()
    chip_idx = (2 * xi + yi).reshape(1).astype(jnp.int32)
    core_idx = ci.reshape(1).astype(jnp.int32)
    bf = lambda a: a.astype(BF16)
    row = lambda v_: v_.reshape(1, -1)
    x0, target = x[0], loss_target[0]
    t = x0.shape[0]
    res = {}

    def reduce_first(grads, names):
        return [_chip_sum(g, a, core_idx, f"chip_sum_{k}") for g, a, k in zip(grads, reduce_first.ex.results, names)]

    def update(psum, others, w, m, v, name):
        shp = w.shape
        r2 = lambda t_: t_.reshape(-1, shp[-1])
        outs = _adamw_sharded(psum, others, chip_idx, r2(w), r2(m), r2(v), name)
        return [o_.reshape(shp) for o_ in outs]

    small_w = _pack([a_v_norm, f_conv_w, a_norm])
    g_a_in, g_a_out, g_small = _exchange_alone(_Gather([bf(a_w_in[0]), bf(a_w_out[0]), small_w]), "gather_first")
    a_v_norm_full = g_small[:, 0, :].reshape(1, d)
    a_norm_full = g_small[:, 34, :].reshape(1, d)
    conv_w = lax.reduce_precision(g_small[:, 1:34, :].reshape(N_SHARDS, 2, 3, FF_SHARD), 8, 7)
    cw = jnp.pad(jnp.transpose(conv_w, (1, 0, 2, 3)), ((0, 0), (0, 0), (0, 5), (0, 0)))
    cb = f_conv_b.reshape(2, N_SHARDS, 1, FF_SHARD)
    tri = jnp.tril(jnp.ones((CHUNK, CHUNK), dtype=bool))
    w_causal = jnp.where(tri[None], a_w_s[0], 0.0).astype(BF16)
    w_causal_t = jnp.transpose(w_causal, (0, 2, 1))
    b_sb = jnp.broadcast_to(a_b_s[0][:, :, None], (N_GROUPS, CHUNK, CHUNK))
    w_a_out = g_a_out.reshape(d, d)
    gq = jnp.tile(b_q_norm.reshape(1, HEAD_DIM), (1, 2))
    gk = jnp.tile(k_norm.reshape(1, HEAD_DIM), (1, 2))
    sinks = b_sinks.reshape(N_Q_HEADS)

    (h1,) = _rms_fwd(x0, [a_norm_full], "a_norm_fwd")
    ex = _Gather([bf(f_w_in[0])])
    zpre, x1 = _sgu_fwd(x0, h1, g_a_in, a_v_norm_full, w_causal, b_sb, w_a_out, carry=ex)
    w_in0 = ex.results[0]
    (hf0,) = _rms_fwd(x1, [f_norm[0:1]], "f0_norm_fwd")
    ex = _Gather([bf(f_w_out[0]), bf(w_kv), bf(b_w_q[0]), bf(b_w_o[0])])
    a0 = _ffn_in(hf0, w_in0, 0, carry=ex)
    w_out0 = ex.results[0].reshape(D_FF, d)
    kv_full = ex.results[1].reshape(d, 2 * N_KV_HEADS * HEAD_DIM)
    w_q, w_o = ex.results[2].reshape(d, d), ex.results[3].reshape(d, d)
    half = N_KV_HEADS * HEAD_DIM
    w_kv_dup = jnp.concatenate([_dup_heads(kv_full[:, :half]), _dup_heads(kv_full[:, half:])], axis=1)
    ex = _Gather([bf(f_w_in[1])])
    x2 = _ffn_out(a0, cw[0], cb[0], w_out0, x1, 0, carry=ex)
    w_in1 = ex.results[0]
    hk, hq = _rms_fwd(x2, [row(kv_norm), b_norm], "kvq_norm_fwd")
    kvd = _mm_rows(hk, w_kv_dup, F32, "kv_proj")
    qraw = _mm_rows(hq, w_q, F32, "q_proj")
    ex = _Gather([bf(f_w_out[1])])
    o = _attn_fwd(qraw, kvd, gq, gk, sinks, carry=ex)
    w_out1 = ex.results[0].reshape(D_FF, d)
    x3 = _mm_rows(o, w_o, F32, "o_proj", res=x2)
    (hf1,) = _rms_fwd(x3, [f_norm[1:2]], "f1_norm_fwd")
    a1 = _ffn_in(hf1, w_in1, 1)
    x4 = _ffn_out(a1, cw[1], cb[1], w_out1, x3, 1)
    dy, loss_lanes = _loss_head(x4, target)
    loss = lax.psum(loss_lanes[0, 0], ("x", "y", "c"))

    dhu1, dw_out1, dconv1 = _ffn_bwd_act(a1, cw[1], cb[1], w_out1, dy, 1)
    dw_out1 = dw_out1.reshape(N_SHARDS, D_FF // N_SHARDS, d)
    reduce_first.ex = _ToSibling([dw_out1])
    da1, dhf1 = _ffn_bwd_in(dhu1, cw[1], w_in1, 1, carry=reduce_first.ex)
    (p_out1,) = reduce_first([dw_out1], ["f_w_out1"])
    ex_out1 = _ToChips([p_out1])
    dw_in1 = _ffn_wgrad_in(hf1, da1, 1, carry=ex_out1)
    reduce_first.ex = _ToSibling([dw_in1])
    dx3, dgf1 = _rms_bwd(x3, [f_norm[1:2]], [dhf1], dy, "f1_norm_bwd", carry=reduce_first.ex)
    (p_in1,) = reduce_first([dw_in1], ["f_w_in1"])
    d_o = _mm_rows(dx3, w_o, BF16, "o_proj_bwd", trans_w=True)
    dw_o = _mm_wgrad(o, dx3, "o_wgrad").reshape(N_SHARDS, d // N_SHARDS, d)
    ex_in1 = _ToChips([p_in1])
    dq, dkv, dsink, dgq, dgk = _attn_bwd(qraw, kvd, d_o, gq, gk, sinks, carry=ex_in1)
    dw_q = _mm_wgrad(hq, dq, "q_wgrad").reshape(N_SHARDS, d // N_SHARDS, d)
    dw_kv_dup = _mm_wgrad(hk, dkv, "kv_wgrad")
    dw_kv = jnp.concatenate(
        [_fold_heads(dw_kv_dup[:, :4 * LANES]), _fold_heads(dw_kv_dup[:, 4 * LANES:])], axis=1
    ).reshape(N_SHARDS, d // N_SHARDS, 2 * N_KV_HEADS * HEAD_DIM)
    reduce_first.ex = _ToSibling([dw_o, dw_q, dw_kv])
    dhq = _mm_rows(dq, w_q, F32, "q_proj_bwd", trans_w=True, carry=reduce_first.ex)
    dhk = _mm_rows(dkv, w_kv_dup, F32, "kv_proj_bwd", trans_w=True)
    p_o, p_q, p_kv = reduce_first([dw_o, dw_q, dw_kv], ["b_w_o", "b_w_q", "w_kv"])
    dx2, dg2 = _rms_bwd(x2, [row(kv_norm), b_norm], [dhk, dhq], dx3, "kvq_norm_bwd")
    ex_attn = _ToChips([p_o, p_q, p_kv])
    dhu0, dw_out0, dconv0 = _ffn_bwd_act(a0, cw[0], cb[0], w_out0, dx2, 0, carry=ex_attn)
    dw_out0 = dw_out0.reshape(N_SHARDS, D_FF // N_SHARDS, d)
    reduce_first.ex = _ToSibling([dw_out0])
    da0, dhf0 = _ffn_bwd_in(dhu0, cw[0], w_in0, 0, carry=reduce_first.ex)
    (p_out0,) = reduce_first([dw_out0], ["f_w_out0"])
    ex_out0 = _ToChips([p_out0])
    dw_in0 = _ffn_wgrad_in(hf0, da0, 0, carry=ex_out0)
    reduce_first.ex = _ToSibling([dw_in0])
    dx1, dgf0 = _rms_bwd(x1, [f_norm[0:1]], [dhf0], dx2, "f0_norm_bwd", carry=reduce_first.ex)
    (p_in0,) = reduce_first([dw_in0], ["f_w_in0"])
    dz, y, dwc, dbs, dgv = _sgu_bwd(dx1, zpre, w_a_out, a_v_norm_full, w_causal, w_causal_t, b_sb)
    dw_a_out = _mm_wgrad(y, dx1, "a_out_wgrad").reshape(N_SHARDS, d // N_SHARDS, d)
    tt = 512
    nsub = g_a_in.shape[2]
    ex_in0a = _ToChips([p_in0], rows=(0, d // 2))
    dw_a_in = _mm(
        h1, dz, pl.BlockSpec((tt, d), lambda s, j, kk: (kk, 0)), pl.BlockSpec((tt, nsub), lambda s, j, kk: (kk, s)),
        pl.BlockSpec((None, d, nsub), lambda s, j, kk: (s, 0, 0)), jax.ShapeDtypeStruct((N_SHARDS, d, nsub), F32),
        (N_SHARDS, 1, t // tt), TN, "a_in_wgrad", carry=ex_in0a)

    def conv_grads(dconv):
        return jnp.transpose(dconv, (1, 0, 2, 3)).reshape(N_SHARDS, 8, FF_SHARD)

    dconv0, dconv1 = conv_grads(dconv0), conv_grads(dconv1)
    conv_g = jnp.stack([dconv0[:, 0:3, :], dconv1[:, 0:3, :]], axis=1)
    small_g = jnp.concatenate(
        [dgv[0].reshape(N_SHARDS, 1, LANES), conv_g.reshape(N_SHARDS, 33, LANES),
         jnp.zeros((N_SHARDS, 6, LANES), F32)], axis=1)
    rep = ["a_w_s", "a_b_s", "f_norm", "f_conv_b", "kv_norm", "k_norm", "b_norm", "b_q_norm", "b_sinks"]
    rep_g = dict(
        a_w_s=dwc, a_b_s=dbs[:, :, 0], f_norm=jnp.stack([dgf0[0], dgf1[0]]),
        f_conv_b=jnp.stack([dconv0[:, 3, :].reshape(-1), dconv1[:, 3, :].reshape(-1)]), kv_norm=dg2[0],
        k_norm=dgk[0, :HEAD_DIM] + dgk[0, HEAD_DIM:], b_norm=dg2[1], b_q_norm=dgq[0, :HEAD_DIM] + dgq[0, HEAD_DIM:],
        b_sinks=dsink[:, 0])
    reduce_first.ex = _ToSibling([dw_a_out, dw_a_in, small_g])
    ex_in0b = _ToChips([p_in0], rows=(d // 2, d // 2))
    ex_rep = _Gather([_pack([rep_g[k] for k in rep])])
    together = _Together([reduce_first.ex, ex_in0b, ex_rep])
    dh1 = _mm(
        dz, g_a_in, pl.BlockSpec((tt, nsub), lambda i, j, kk: (i, kk)),
        pl.BlockSpec((None, d, nsub), lambda i, j, kk: (kk, 0, 0)), pl.BlockSpec((tt, d), lambda i, j, kk: (i, 0)),
        jax.ShapeDtypeStruct((t, d), F32), (t // tt, 1, N_SHARDS), NT, "a_in_bwd", carry=together)
    together.spread()
    p_a_out, p_a_in, p_small = reduce_first([dw_a_out, dw_a_in, small_g], ["a_w_out", "a_w_in", "small"])
    ex_last = _ToChips([p_a_out, p_a_in, p_small])
    grad_x, dg0 = _rms_bwd(x0, [a_norm_full], [dh1], dx1, "a_norm_bwd", carry=ex_last)
    g_a_norm = jnp.pad(dg0[0].reshape(N_SHARDS, 1, LANES), ((0, 0), (0, 7), (0, 0)))
    (a_norm_parts,) = _exchange_alone(_ToOwners([g_a_norm]), "a_norm_to_owners")

    res["f_w_out"] = [update(p_out0, ex_out0.results, f_w_out[0], m_f_w_out[0], v_f_w_out[0], "adamw_f_w_out0"),
                      update(p_out1, ex_out1.results, f_w_out[1], m_f_w_out[1], v_f_w_out[1], "adamw_f_w_out1")]
    res["f_w_in"] = [update(p_in0, ex_in0a.results + ex_in0b.results, f_w_in[0], m_f_w_in[0], v_f_w_in[0],
                            "adamw_f_w_in0"),
                     update(p_in1, ex_in1.results, f_w_in[1], m_f_w_in[1], v_f_w_in[1], "adamw_f_w_in1")]
    for key in ("f_w_out", "f_w_in"):
        res[key] = [jnp.stack([res[key][0][j], res[key][1][j]]) for j in range(4)]
    res["b_w_o"] = update(p_o, ex_attn.results[0:1], b_w_o, m_b_w_o, v_b_w_o, "adamw_b_w_o")
    res["b_w_q"] = update(p_q, ex_attn.results[1:2], b_w_q, m_b_w_q, v_b_w_q, "adamw_b_w_q")
    res["w_kv"] = update(p_kv, ex_attn.results[2:3], w_kv, m_w_kv, v_w_kv, "adamw_w_kv")
    res["a_w_out"] = update(p_a_out, ex_last.results[0:1], a_w_out, m_a_w_out, v_a_w_out, "adamw_a_w_out")
    res["a_w_in"] = update(p_a_in, ex_last.results[1:2], a_w_in, m_a_w_in, v_a_w_in, "adamw_a_w_in")
    small_shapes = [(1, LANES), (2, 3, FF_SHARD)]
    small_outs = _adamw_sharded(
        p_small, ex_last.results[2:3], chip_idx, _pack([a_v_norm, f_conv_w]), _pack([m_a_v_norm, m_f_conv_w]),
        _pack([v_a_v_norm, v_f_conv_w]), "adamw_small_sharded")
    for j, key in enumerate(["a_v_norm", "f_conv_w"]):
        res[key] = [_unpack(o_, small_shapes)[j] for o_ in small_outs]
    a_norm_outs = _adamw_replicated(a_norm_parts, _pack([a_norm]), _pack([m_a_norm]), _pack([v_a_norm]),
                                    "adamw_a_norm")
    res["a_norm"] = [o_[0:1] for o_ in a_norm_outs]

    rep_w = dict(a_w_s=a_w_s, a_b_s=a_b_s, f_norm=f_norm, f_conv_b=f_conv_b, kv_norm=kv_norm, k_norm=k_norm,
                 b_norm=b_norm, b_q_norm=b_q_norm, b_sinks=b_sinks)
    rep_m = dict(a_w_s=m_a_w_s, a_b_s=m_a_b_s, f_norm=m_f_norm, f_conv_b=m_f_conv_b, kv_norm=m_kv_norm,
                 k_norm=m_k_norm, b_norm=m_b_norm, b_q_norm=m_b_q_norm, b_sinks=m_b_sinks)
    rep_v = dict(a_w_s=v_a_w_s, a_b_s=v_a_b_s, f_norm=v_f_norm, f_conv_b=v_f_conv_b, kv_norm=v_kv_norm,
                 k_norm=v_k_norm, b_norm=v_b_norm, b_q_norm=v_b_q_norm, b_sinks=v_b_sinks)
    rep_outs = _adamw_replicated(ex_rep.results[0], _pack([rep_w[k] for k in rep]), _pack([rep_m[k] for k in rep]),
                                 _pack([rep_v[k] for k in rep]), "adamw_replicated")
    rep_shapes = [rep_w[k].shape for k in rep]
    for j, key in enumerate(rep):
        res[key] = [_unpack(o_, rep_shapes)[j] for o_ in rep_outs]

    order = ["a_norm", "a_w_in", "a_v_norm", "a_w_s", "a_b_s", "a_w_out", "f_norm", "f_w_in", "f_conv_w", "f_conv_b",
             "f_w_out", "kv_norm", "w_kv", "k_norm", "b_norm", "b_w_q", "b_q_norm", "b_sinks", "b_w_o"]
    outs = [loss, grad_x[None]]
    for j in range(4):
        outs += [res[k][j] for k in order]
    return tuple(outs)
```

```python
import jax
import jax.numpy as jnp
from jax import lax
from jax.experimental import pallas as pl
from jax.experimental.pallas import tpu as pltpu

F32 = jnp.float32
BF16 = jnp.bfloat16
EPS = 1e-6
D_MODEL = 1024
CHUNK = 128
N_GROUPS = 8
N_SHARDS = 8
HEAD_DIM = 64
N_Q_HEADS = 16
N_KV_HEADS = 4
D_FF = 2816
FF_SHARD = 2 * D_FF // N_SHARDS
LANES = 128
NEG_BIG = -1e30
ADAM_LR = 0.001
ADAM_B1 = 0.9
ADAM_B2 = 0.999
ADAM_EPS = 1e-08
ADAM_WD = 0.01
ADAM_STEP = 10
VMEM_LIMIT_BYTES = 56 * 1024 * 1024
MESH = pl.DeviceIdType.MESH

NN = (((1,), (0,)), ((), ()))
NT = (((1,), (1,)), ((), ()))
TN = (((0,), (0,)), ((), ()))
SLOPES = tuple(2.0 ** (-8.0 * (h + 1) / N_Q_HEADS) for h in range(N_Q_HEADS))


def _params(sem=None):
    return pltpu.CompilerParams(dimension_semantics=sem, vmem_limit_bytes=VMEM_LIMIT_BYTES)


def _dot(a, b, dims=NN):
    return lax.dot_general(a, b, dims, preferred_element_type=F32)


def _sigmoid(x):
    return 1.0 / (1.0 + jnp.exp(-x))


def _gelu_parts(z):
    cdf = 0.5 * (1.0 + lax.erf(z * (2.0 ** -0.5)))
    pdf = jnp.exp(-0.5 * z * z) * 0.3989422804014327
    return cdf, pdf


def _coords():
    return lax.axis_index("x"), lax.axis_index("y"), lax.axis_index("c")


class _Gather:
    def __init__(self, srcs):
        self.srcs = list(srcs)
        n = len(self.srcs)
        self.out_shapes = [jax.ShapeDtypeStruct((N_SHARDS,) + s.shape, s.dtype) for s in self.srcs]
        self.sems = [pltpu.SemaphoreType.DMA((n, 7)), pltpu.SemaphoreType.DMA((n, 7)), pltpu.SemaphoreType.DMA((n,))]

    def _plan(self, src, dst, sems):
        send_sems, recv_sems, local_sems = sems
        x, y, c = _coords()
        me, sibling = (x, y, c), (x, y, 1 - c)
        chips = [(1 - x, y), (x, 1 - y), (1 - x, 1 - y)]
        n = len(src)

        def rows(e, dev):
            return dst[e].at[4 * dev[0] + 2 * dev[1] + dev[2]]

        def copy(e, slot, block, to, from_own=False):
            return pltpu.make_async_remote_copy(
                src_ref=src[e] if from_own else rows(e, block), dst_ref=rows(e, block),
                send_sem=send_sems.at[e, slot], recv_sem=recv_sems.at[e, slot], device_id=to, device_id_type=MESH)

        mine = [pltpu.make_async_copy(src[e], rows(e, me), local_sems.at[e]) for e in range(n)]
        first = []
        for e in range(n):
            first.append(copy(e, 0, me, sibling, from_own=True))
            first += [copy(e, 1 + j, me, (*chip, c), from_own=True) for j, chip in enumerate(chips)]
        return n, me, sibling, chips, c, copy, mine, first

    def start(self, src, dst, sems):
        _, _, _, _, _, _, mine, first = self._plan(src, dst, sems)
        for cp in mine + first:
            cp.start()

    def finish(self, src, dst, sems):
        n, me, sibling, chips, c, copy, mine, first = self._plan(src, dst, sems)
        passed = []
        for j, chip in enumerate(chips):
            for e in range(n):
                copy(e, 1 + j, (*chip, c), me).wait_recv()
                cp = copy(e, 4 + j, (*chip, c), sibling)
                cp.start()
                passed.append(cp)
        for e in range(n):
            copy(e, 0, sibling, me).wait_recv()
            for j, chip in enumerate(chips):
                copy(e, 4 + j, (*chip, 1 - c), me).wait_recv()
        for cp in first + passed:
            cp.wait_send()
        for cp in mine:
            cp.wait()


class _ToSibling:
    def __init__(self, grads):
        self.srcs = list(grads)
        n = len(self.srcs)
        self.out_shapes = [jax.ShapeDtypeStruct((4,) + g.shape[1:], g.dtype) for g in self.srcs]
        self.sems = [pltpu.SemaphoreType.DMA((n, 4)), pltpu.SemaphoreType.DMA((n, 4))]

    def _copies(self, src, dst, sems):
        send_sems, recv_sems = sems
        x, y, c = _coords()
        return [
            pltpu.make_async_remote_copy(
                src_ref=src[i].at[2 * q + (1 - c)], dst_ref=dst[i].at[q], send_sem=send_sems.at[i, q],
                recv_sem=recv_sems.at[i, q], device_id=(x, y, 1 - c), device_id_type=MESH)
            for i in range(len(src)) for q in range(4)]

    def start(self, src, dst, sems):
        for cp in self._copies(src, dst, sems):
            cp.start()

    def finish(self, src, dst, sems):
        for cp in self._copies(src, dst, sems):
            cp.wait()


class _ToChips:
    def __init__(self, psums, rows=None):
        self.srcs = list(psums)
        n = len(self.srcs)
        self.rows = rows
        self.out_shapes = [
            jax.ShapeDtypeStruct((3, p.shape[1] if rows is None else rows[1]) + p.shape[2:], p.dtype)
            for p in self.srcs]
        self.sems = [pltpu.SemaphoreType.DMA((n, 3)), pltpu.SemaphoreType.DMA((n, 3))]

    def _copies(self, src, dst, sems):
        send_sems, recv_sems = sems
        x, y, c = _coords()
        peers = [(x, 1 - y), (1 - x, y), (1 - x, 1 - y)]

        def part(i, q):
            if self.rows is None:
                return src[i].at[q]
            return src[i].at[q, pl.ds(self.rows[0], self.rows[1])]

        return [
            pltpu.make_async_remote_copy(
                src_ref=part(i, 2 * px + py), dst_ref=dst[i].at[r], send_sem=send_sems.at[i, r],
                recv_sem=recv_sems.at[i, r], device_id=(px, py, c), device_id_type=MESH)
            for i in range(len(src)) for r, (px, py) in enumerate(peers)]

    def start(self, src, dst, sems):
        for cp in self._copies(src, dst, sems):
            cp.start()

    def finish(self, src, dst, sems):
        for cp in self._copies(src, dst, sems):
            cp.wait()


class _ToOwners:
    def __init__(self, grads):
        self.srcs = list(grads)
        n = len(self.srcs)
        self.out_shapes = [jax.ShapeDtypeStruct(g.shape, g.dtype) for g in self.srcs]
        self.sems = [pltpu.SemaphoreType.DMA((n, 7)), pltpu.SemaphoreType.DMA((n, 7)), pltpu.SemaphoreType.DMA((n,))]

    def _copies(self, src, dst, sems):
        send_sems, recv_sems, local_sems = sems
        x, y, c = _coords()
        me = 4 * x + 2 * y + c
        copies = [pltpu.make_async_copy(src[i].at[me], dst[i].at[me], local_sems.at[i]) for i in range(len(src))]
        for i in range(len(src)):
            for rel in range(1, N_SHARDS):
                px = x ^ (rel >> 2) if rel >> 2 else x
                py = y ^ ((rel >> 1) & 1) if (rel >> 1) & 1 else y
                pc = c ^ (rel & 1) if rel & 1 else c
                copies.append(pltpu.make_async_remote_copy(
                    src_ref=src[i].at[4 * px + 2 * py + pc], dst_ref=dst[i].at[me], send_sem=send_sems.at[i, rel - 1],
                    recv_sem=recv_sems.at[i, rel - 1], device_id=(px, py, pc), device_id_type=MESH))
        return copies

    def start(self, src, dst, sems):
        for cp in self._copies(src, dst, sems):
            cp.start()

    def finish(self, src, dst, sems):
        for cp in self._copies(src, dst, sems):
            cp.wait()


class _Together:
    def __init__(self, parts):
        self.parts = list(parts)
        self.srcs = [s for p in self.parts for s in p.srcs]
        self.out_shapes = [s for p in self.parts for s in p.out_shapes]
        self.sems = [s for p in self.parts for s in p.sems]

    def _split(self, src, dst, sems):
        a = b = c = 0
        for p in self.parts:
            na, nc = len(p.srcs), len(p.sems)
            yield p, src[a:a + na], dst[b:b + na], sems[c:c + nc]
            a, b, c = a + na, b + na, c + nc

    def start(self, src, dst, sems):
        for p, s, d, m in self._split(src, dst, sems):
            p.start(s, d, m)

    def finish(self, src, dst, sems):
        for p, s, d, m in self._split(src, dst, sems):
            p.finish(s, d, m)

    def spread(self):
        b = 0
        for p in self.parts:
            p.results = self.results[b:b + len(p.srcs)]
            b += len(p.srcs)


def _call(body, args, *, grid, in_specs, out_specs, out_shape, name, scratch=(), sem=None, carry=None):
    out_shape, out_specs = list(out_shape), list(out_specs)
    if carry is None:
        return pl.pallas_call(
            body, grid=grid, in_specs=list(in_specs), out_specs=out_specs, out_shape=out_shape,
            scratch_shapes=list(scratch), name=name, compiler_params=_params(sem))(*args)
    n_in, n_out, n_scr, n_c = len(args), len(out_shape), len(scratch), len(carry.srcs)
    steps = tuple(grid)

    def carried(*refs):
        ins, rest = refs[:n_in], refs[n_in:]
        c_src, rest = rest[:n_c], rest[n_c:]
        outs, rest = rest[:n_out], rest[n_out:]
        c_dst, rest = rest[:n_c], rest[n_c:]
        scr, sems = rest[:n_scr], rest[n_scr:]
        first = pl.program_id(0) == 0
        last = pl.program_id(0) == steps[0] - 1
        for ax in range(1, len(steps)):
            first = first & (pl.program_id(ax) == 0)
            last = last & (pl.program_id(ax) == steps[ax] - 1)

        @pl.when(first)
        def _():
            carry.start(c_src, c_dst, sems)

        body(*ins, *outs, *scr)

        @pl.when(last)
        def _():
            carry.finish(c_src, c_dst, sems)

    hbm = pl.BlockSpec(memory_space=pl.ANY)
    res = pl.pallas_call(
        carried, grid=grid, in_specs=list(in_specs) + [hbm] * n_c, out_specs=out_specs + [hbm] * n_c,
        out_shape=out_shape + carry.out_shapes, scratch_shapes=list(scratch) + carry.sems, name=name,
        compiler_params=_params(("arbitrary",) * len(steps)))(*args, *carry.srcs)
    carry.results = list(res[n_out:])
    return list(res[:n_out])


def _exchange_alone(ex, name):
    n = len(ex.srcs)

    def body(*refs):
        src, dst, sems = refs[:n], refs[n:2 * n], refs[2 * n:]
        ex.start(src, dst, sems)
        ex.finish(src, dst, sems)

    hbm = pl.BlockSpec(memory_space=pl.ANY)
    res = pl.pallas_call(body, in_specs=[hbm] * n, out_specs=[hbm] * n, out_shape=ex.out_shapes,
                         scratch_shapes=ex.sems, name=name)(*ex.srcs)
    ex.results = list(res)
    return ex.results


def _rms_fwd(x, gains, name, tm=512, carry=None):
    t, d = x.shape
    n = len(gains)

    def body(*refs):
        x_ref, g_refs, h_refs = refs[0], refs[1:1 + n], refs[1 + n:]
        xf = x_ref[...]
        xhat = xf * lax.rsqrt(jnp.mean(xf * xf, axis=-1, keepdims=True) + EPS)
        for g_ref, h_ref in zip(g_refs, h_refs):
            h_ref[...] = (xhat * g_ref[...]).astype(BF16)

    row = pl.BlockSpec((tm, d), lambda i: (i, 0))
    vec = pl.BlockSpec((1, d), lambda i: (0, 0))
    return _call(body, [x, *gains], grid=(t // tm,), in_specs=[row] + [vec] * n, out_specs=[row] * n,
                 out_shape=[jax.ShapeDtypeStruct((t, d), BF16)] * n, name=name, carry=carry)


def _rms_bwd(x, gains, dhs, dres, name, tm=256, carry=None):
    t, d = x.shape
    n = len(gains)

    def body(*refs):
        x_ref, dres_ref = refs[0], refs[1]
        g_refs, dh_refs = refs[2:2 + n], refs[2 + n:2 + 2 * n]
        dx_ref, dg_ref = refs[2 + 2 * n], refs[3 + 2 * n]
        i = pl.program_id(0)

        @pl.when(i == 0)
        def _():
            dg_ref[...] = jnp.zeros_like(dg_ref)

        xf = x_ref[...]
        r = lax.rsqrt(jnp.mean(xf * xf, axis=-1, keepdims=True) + EPS)
        xhat = xf * r
        dx = dres_ref[...]
        for j in range(n):
            dh = dh_refs[j][...]
            dg_ref[j:j + 1, :] += jnp.sum(dh * xhat, axis=0, keepdims=True)
            gy = dh * g_refs[j][...]
            dx = dx + r * (gy - xhat * jnp.mean(gy * xhat, axis=-1, keepdims=True))
        dx_ref[...] = dx

    row = pl.BlockSpec((tm, d), lambda i: (i, 0))
    vec = pl.BlockSpec((1, d), lambda i: (0, 0))
    return _call(body, [x, dres, *gains, *dhs], grid=(t // tm,), in_specs=[row, row] + [vec] * n + [row] * n,
                 out_specs=[row, pl.BlockSpec((8, d), lambda i: (0, 0))],
                 out_shape=[jax.ShapeDtypeStruct((t, d), F32), jax.ShapeDtypeStruct((8, d), F32)],
                 name=name, sem=("arbitrary",), carry=carry)


def _mm(a, b, a_spec, b_spec, o_spec, out_shape, grid, dims, name, res=None, res_spec=None, carry=None):
    nk = grid[2]
    acc_shape = tuple(s for s in o_spec.block_shape if s is not None)

    def body(*refs):
        a_ref, b_ref = refs[0], refs[1]
        r_ref = refs[2] if res is not None else None
        o_ref = refs[3] if res is not None else refs[2]
        p = _dot(a_ref[...].astype(BF16), b_ref[...].astype(BF16), dims)
        if nk == 1:
            if res is not None:
                p = p + r_ref[...]
            o_ref[...] = p.astype(o_ref.dtype)
            return
        acc_ref = refs[-1]
        k = pl.program_id(2)

        @pl.when(k == 0)
        def _():
            acc_ref[...] = p

        @pl.when(k > 0)
        def _():
            acc_ref[...] += p

        @pl.when(k == nk - 1)
        def _():
            out = acc_ref[...]
            if res is not None:
                out = out + r_ref[...]
            o_ref[...] = out.astype(o_ref.dtype)

    ins = [a, b] + ([res] if res is not None else [])
    specs = [a_spec, b_spec] + ([res_spec] if res is not None else [])
    return _call(body, ins, grid=grid, in_specs=specs, out_specs=[o_spec], out_shape=[out_shape],
                 scratch=[pltpu.VMEM(acc_shape, F32)] if nk > 1 else [], name=name,
                 sem=("parallel", "parallel", "arbitrary"), carry=carry)[0]


def _mm_rows(a, w, out_dtype, name, trans_w=False, res=None, tm=512, carry=None):
    t, k = a.shape
    n = w.shape[0] if trans_w else w.shape[1]
    return _mm(
        a, w, pl.BlockSpec((tm, k), lambda i, j, kk: (i, 0)), pl.BlockSpec(w.shape, lambda i, j, kk: (0, 0)),
        pl.BlockSpec((tm, n), lambda i, j, kk: (i, 0)), jax.ShapeDtypeStruct((t, n), out_dtype), (t // tm, 1, 1),
        NT if trans_w else NN, name, res=res,
        res_spec=None if res is None else pl.BlockSpec((tm, n), lambda i, j, kk: (i, 0)), carry=carry)


def _mm_wgrad(a, b, name, carry=None):
    t, m = a.shape
    n = b.shape[1]
    tn = n // (4 if b.dtype == F32 else 2)
    return _mm(
        a, b, pl.BlockSpec((t, m), lambda i, j, kk: (0, 0)), pl.BlockSpec((t, tn), lambda i, j, kk: (0, j)),
        pl.BlockSpec((m, tn), lambda i, j, kk: (0, j)), jax.ShapeDtypeStruct((m, n), F32), (1, n // tn, 1), TN, name,
        carry=carry)


def _sgu_fwd(x0, h1, w_in, g_v, w_c, b_sb, w_out, tm=256, carry=None):
    t, d = x0.shape
    nsub = w_in.shape[2]

    def body(x_ref, h_ref, win_ref, gv_ref, wc_ref, bsb_ref, wout_ref, zpre_ref, x1_ref, u_s, v_s, vn_s, y_s):
        h = h_ref[...]
        for k in range(N_SHARDS):
            zk = _dot(h, win_ref[k])
            zpre_ref[:, k * nsub:(k + 1) * nsub] = zk
            cdf, _ = _gelu_parts(zk)
            if k < N_SHARDS // 2:
                u_s[:, k * nsub:(k + 1) * nsub] = zk * cdf
            else:
                v_s[:, (k - 4) * nsub:(k - 3) * nsub] = zk * cdf
        v = v_s[...]
        rv = lax.rsqrt(jnp.mean(v * v, axis=-1, keepdims=True) + EPS)
        vn_s[...] = (v * rv * gv_ref[...]).astype(BF16)
        for ci in range(tm // CHUNK):
            rows = slice(ci * CHUNK, (ci + 1) * CHUNK)
            for g in range(N_GROUPS):
                cols = slice(g * LANES, (g + 1) * LANES)
                sv = _dot(wc_ref[g], vn_s[rows, cols]) + bsb_ref[g]
                y_s[rows, cols] = (u_s[rows, cols] * sv).astype(BF16)
        x1_ref[...] = x_ref[...] + _dot(y_s[...], wout_ref[...])

    row = pl.BlockSpec((tm, d), lambda i: (i, 0))
    full = lambda a: pl.BlockSpec(a.shape, lambda i: (0,) * a.ndim)
    return _call(
        body, [x0, h1, w_in, g_v, w_c, b_sb, w_out], grid=(t // tm,),
        in_specs=[row, row, full(w_in), full(g_v), full(w_c), full(b_sb), full(w_out)],
        out_specs=[pl.BlockSpec((tm, 2 * d), lambda i: (i, 0)), row],
        out_shape=[jax.ShapeDtypeStruct((t, 2 * d), F32), jax.ShapeDtypeStruct((t, d), F32)],
        scratch=[pltpu.VMEM((tm, d), F32), pltpu.VMEM((tm, d), F32), pltpu.VMEM((tm, d), BF16),
                 pltpu.VMEM((tm, d), BF16)],
        name="sgu_fwd", carry=carry)


def _sgu_bwd(dx1, zpre, w_out, g_v, w_c, w_ct, b_sb, tm=256, carry=None):
    t, d = dx1.shape

    def body(dx_ref, zpre_ref, wout_ref, gv_ref, wc_ref, wct_ref, bsb_ref,
             dz_ref, y_ref, dwc_ref, dbs_ref, dgv_ref, u_s, vn_s, dy_s, du_s, dvn_s):
        i = pl.program_id(0)

        @pl.when(i == 0)
        def _():
            dwc_ref[...] = jnp.zeros_like(dwc_ref)
            dbs_ref[...] = jnp.zeros_like(dbs_ref)
            dgv_ref[...] = jnp.zeros_like(dgv_ref)

        dy_s[...] = _dot(dx_ref[...].astype(BF16), wout_ref[...], NT)
        zu = zpre_ref[:, :d]
        zv = zpre_ref[:, d:]
        cdf_u, pdf_u = _gelu_parts(zu)
        cdf_v, pdf_v = _gelu_parts(zv)
        u_s[...] = zu * cdf_u
        v = zv * cdf_v
        rv = lax.rsqrt(jnp.mean(v * v, axis=-1, keepdims=True) + EPS)
        vhat = v * rv
        gv = gv_ref[...]
        vn_s[...] = (vhat * gv).astype(BF16)
        for ci in range(tm // CHUNK):
            rows = slice(ci * CHUNK, (ci + 1) * CHUNK)
            for g in range(N_GROUPS):
                cols = slice(g * LANES, (g + 1) * LANES)
                vnb = vn_s[rows, cols]
                sv = _dot(wc_ref[g], vnb) + bsb_ref[g]
                dyb = dy_s[rows, cols]
                ub = u_s[rows, cols]
                dsv = dyb * ub
                du_s[rows, cols] = dyb * sv
                y_ref[rows, cols] = (ub * sv).astype(BF16)
                dsvb = dsv.astype(BF16)
                dbs_ref[g] += dsv
                dwc_ref[g] += _dot(dsvb, vnb, NT)
                dvn_s[rows, cols] = _dot(wct_ref[g], dsvb)
        dvn = dvn_s[...]
        dgv_ref[0:1, :] += jnp.sum(dvn * vhat, axis=0, keepdims=True)
        gy = dvn * gv
        dv = rv * (gy - vhat * jnp.mean(gy * vhat, axis=-1, keepdims=True))
        dz_ref[:, :d] = (du_s[...] * (cdf_u + zu * pdf_u)).astype(BF16)
        dz_ref[:, d:] = (dv * (cdf_v + zv * pdf_v)).astype(BF16)

        @pl.when(i == t // tm - 1)
        def _():
            tri = (lax.broadcasted_iota(jnp.int32, (CHUNK, CHUNK), 0)
                   >= lax.broadcasted_iota(jnp.int32, (CHUNK, CHUNK), 1))
            for g in range(N_GROUPS):
                dwc_ref[g] = jnp.where(tri, dwc_ref[g], 0.0)
                dbs_ref[g] = jnp.broadcast_to(jnp.sum(dbs_ref[g], axis=1, keepdims=True), (CHUNK, CHUNK))

    row = pl.BlockSpec((tm, d), lambda i: (i, 0))
    row2 = pl.BlockSpec((tm, 2 * d), lambda i: (i, 0))
    full = lambda a: pl.BlockSpec(a.shape, lambda i: (0,) * a.ndim)
    grp = pl.BlockSpec((N_GROUPS, CHUNK, CHUNK), lambda i: (0, 0, 0))
    return _call(
        body, [dx1, zpre, w_out, g_v, w_c, w_ct, b_sb], grid=(t // tm,),
        in_specs=[row, row2, full(w_out), full(g_v), full(w_c), full(w_ct), full(b_sb)],
        out_specs=[row2, row, grp, grp, pl.BlockSpec((8, d), lambda i: (0, 0))],
        out_shape=[jax.ShapeDtypeStruct((t, 2 * d), BF16), jax.ShapeDtypeStruct((t, d), BF16),
                   jax.ShapeDtypeStruct((N_GROUPS, CHUNK, CHUNK), F32),
                   jax.ShapeDtypeStruct((N_GROUPS, CHUNK, CHUNK), F32), jax.ShapeDtypeStruct((8, d), F32)],
        scratch=[pltpu.VMEM((tm, d), F32), pltpu.VMEM((tm, d), BF16), pltpu.VMEM((tm, d), F32),
                 pltpu.VMEM((tm, d), F32), pltpu.VMEM((tm, d), F32)],
        name="sgu_bwd", sem=("arbitrary",), carry=carry)


def _causal_conv(a_ref, prev_ref, cw, cb, first, tm):
    af = a_ref[...].astype(F32)
    keep = jnp.where(first, 0.0, 1.0)
    pv = prev_ref[...].astype(F32)
    p1 = pv[15:16, :] * keep
    p2 = pv[14:15, :] * keep
    row = lax.broadcasted_iota(jnp.int32, af.shape, 0)
    a1 = jnp.where(row == 0, p1, pltpu.roll(af, 1, 0))
    a2 = jnp.where(row == 0, p2, jnp.where(row == 1, p1, pltpu.roll(af, 2, 0)))
    hu = cw[2:3, :] * af + cw[1:2, :] * a1 + cw[0:1, :] * a2 + cb
    return hu, af, a1, a2


def _ffn_in(hf, w_in, layer, tm=1024, carry=None):
    t, d = hf.shape
    tm = min(tm, t)
    return _mm(
        hf, w_in, pl.BlockSpec((tm, d), lambda s, i, kk: (i, 0)),
        pl.BlockSpec((None, d, FF_SHARD), lambda s, i, kk: (s, 0, 0)),
        pl.BlockSpec((None, tm, FF_SHARD), lambda s, i, kk: (s, i, 0)),
        jax.ShapeDtypeStruct((N_SHARDS, t, FF_SHARD), BF16), (N_SHARDS, t // tm, 1), NN, f"ffn{layer}_in", carry=carry)


def _ffn_conv_specs(tm, gate_of, tile_of):
    def specs(shard_of):
        return [
            pl.BlockSpec((None, tm, FF_SHARD), lambda *g: (shard_of(*g), tile_of(*g), 0)),
            pl.BlockSpec((None, 16, FF_SHARD),
                         lambda *g: (shard_of(*g), jnp.maximum(tile_of(*g) * (tm // 16) - 1, 0), 0)),
            pl.BlockSpec((None, 8, FF_SHARD), lambda *g: (shard_of(*g), 0, 0)),
            pl.BlockSpec((None, 1, FF_SHARD), lambda *g: (shard_of(*g), 0, 0)),
        ]
    return specs(gate_of) + specs(lambda *g: gate_of(*g) + N_SHARDS // 2)


def _ffn_out(a, cw, cb, w_out, x, layer, tm=512, carry=None):
    t, d = x.shape
    nc = N_SHARDS // 2

    def body(ag_ref, pg_ref, cwg_ref, cbg_ref, au_ref, pu_ref, cwu_ref, cbu_ref, wout_ref, x_ref, o_ref, acc_ref):
        i, c = pl.program_id(0), pl.program_id(1)
        hg = _causal_conv(ag_ref, pg_ref, cwg_ref[...], cbg_ref[...], i == 0, tm)[0]
        hu = _causal_conv(au_ref, pu_ref, cwu_ref[...], cbu_ref[...], i == 0, tm)[0]
        act = (hg * _sigmoid(hg) * hu).astype(BF16)
        p = _dot(act, wout_ref[...])

        @pl.when(c == 0)
        def _():
            acc_ref[...] = x_ref[...] + p

        @pl.when(c > 0)
        def _():
            acc_ref[...] += p

        @pl.when(c == nc - 1)
        def _():
            o_ref[...] = acc_ref[...]

    row = pl.BlockSpec((tm, d), lambda i, c: (i, 0))
    return _call(
        body, [a, a, cw, cb, a, a, cw, cb, w_out, x], grid=(t // tm, nc),
        in_specs=_ffn_conv_specs(tm, lambda i, c: c, lambda i, c: i)
        + [pl.BlockSpec((FF_SHARD, d), lambda i, c: (c, 0)), row],
        out_specs=[row], out_shape=[jax.ShapeDtypeStruct((t, d), F32)],
        scratch=[pltpu.VMEM((tm, d), F32)], name=f"ffn{layer}_out", sem=("parallel", "arbitrary"), carry=carry)[0]


def _ffn_bwd_act(a, cw, cb, w_out, dxn, layer, tm=512, carry=None):
    t, d = dxn.shape
    nc = N_SHARDS // 2

    def body(ag_ref, pg_ref, cwg_ref, cbg_ref, au_ref, pu_ref, cwu_ref, cbu_ref, wout_ref, dx_ref,
             dhu_ref, dw_ref, dconv_ref):
        i = pl.program_id(1)

        @pl.when(i == 0)
        def _():
            dw_ref[...] = jnp.zeros_like(dw_ref)
            dconv_ref[...] = jnp.zeros_like(dconv_ref)

        hg, ag0, ag1, ag2 = _causal_conv(ag_ref, pg_ref, cwg_ref[...], cbg_ref[...], i == 0, tm)
        hu, au0, au1, au2 = _causal_conv(au_ref, pu_ref, cwu_ref[...], cbu_ref[...], i == 0, tm)
        sg = _sigmoid(hg)
        sl = hg * sg
        dxb = dx_ref[...].astype(BF16)
        dact = _dot(dxb, wout_ref[...], NT)
        dw_ref[...] += _dot((sl * hu).astype(BF16), dxb, TN)
        d_up = dact * sl
        d_gate = dact * hu * (sg * (1.0 + hg * (1.0 - sg)))
        for j, (dv, taps) in enumerate(((d_gate, (ag2, ag1, ag0)), (d_up, (au2, au1, au0)))):
            dvb = dv.astype(BF16)
            dhu_ref[j] = dvb
            dvr = dvb.astype(F32)
            for k in range(3):
                dconv_ref[j, k:k + 1, :] += jnp.sum(dvr * taps[k], axis=0, keepdims=True)
            dconv_ref[j, 3:4, :] += jnp.sum(dv, axis=0, keepdims=True)

    return _call(
        body, [a, a, cw, cb, a, a, cw, cb, w_out, dxn], grid=(nc, t // tm),
        in_specs=_ffn_conv_specs(tm, lambda c, i: c, lambda c, i: i)
        + [pl.BlockSpec((FF_SHARD, d), lambda c, i: (c, 0)), pl.BlockSpec((tm, d), lambda c, i: (i, 0))],
        out_specs=[pl.BlockSpec((None, 2, tm, FF_SHARD), lambda c, i: (c, 0, i, 0)),
                   pl.BlockSpec((FF_SHARD, d), lambda c, i: (c, 0)),
                   pl.BlockSpec((None, 2, 8, FF_SHARD), lambda c, i: (c, 0, 0, 0))],
        out_shape=[jax.ShapeDtypeStruct((nc, 2, t, FF_SHARD), BF16), jax.ShapeDtypeStruct((D_FF, d), F32),
                   jax.ShapeDtypeStruct((nc, 2, 8, FF_SHARD), F32)],
        name=f"ffn{layer}_bwd_act", sem=("parallel", "arbitrary"), carry=carry)


def _ffn_bwd_in(dhu, cw, w_in, layer, tm=1024, carry=None):
    nc, _, t, _ = dhu.shape
    d = D_MODEL
    tm = min(tm, t)
    last_blk = t // 16 - 1

    def body(dh_ref, nx_ref, cw_ref, win_ref, da_ref, o_ref):
        i, s = pl.program_id(0), pl.program_id(1)
        df = dh_ref[...].astype(F32)
        keep = jnp.where(i == t // tm - 1, 0.0, 1.0)
        nx = nx_ref[...].astype(F32)
        n0 = nx[0:1, :] * keep
        n1 = nx[1:2, :] * keep
        row = lax.broadcasted_iota(jnp.int32, df.shape, 0)
        d1 = jnp.where(row == tm - 1, n0, pltpu.roll(df, tm - 1, 0))
        d2 = jnp.where(row == tm - 1, n1, jnp.where(row == tm - 2, n0, pltpu.roll(df, tm - 2, 0)))
        cw = cw_ref[...]
        da = (cw[2:3, :] * df + cw[1:2, :] * d1 + cw[0:1, :] * d2).astype(BF16)
        da_ref[...] = da
        p = _dot(da, win_ref[...], NT)

        @pl.when(s == 0)
        def _():
            o_ref[...] = p

        @pl.when(s > 0)
        def _():
            o_ref[...] += p

    return _call(
        body, [dhu, dhu, cw, w_in], grid=(t // tm, N_SHARDS),
        in_specs=[pl.BlockSpec((None, None, tm, FF_SHARD), lambda i, s: (s % nc, s // nc, i, 0)),
                  pl.BlockSpec((None, None, 16, FF_SHARD),
                               lambda i, s: (s % nc, s // nc, jnp.minimum((i + 1) * (tm // 16), last_blk), 0)),
                  pl.BlockSpec((None, 8, FF_SHARD), lambda i, s: (s, 0, 0)),
                  pl.BlockSpec((None, d, FF_SHARD), lambda i, s: (s, 0, 0))],
        out_specs=[pl.BlockSpec((None, tm, FF_SHARD), lambda i, s: (s, i, 0)),
                   pl.BlockSpec((tm, d), lambda i, s: (i, 0))],
        out_shape=[jax.ShapeDtypeStruct((N_SHARDS, t, FF_SHARD), BF16), jax.ShapeDtypeStruct((t, d), F32)],
        name=f"ffn{layer}_bwd_in", sem=("parallel", "arbitrary"), carry=carry)


def _ffn_wgrad_in(hf, da, layer, carry=None):
    t, d = hf.shape
    return _mm(
        hf, da, pl.BlockSpec((t, d), lambda s, j, kk: (0, 0)),
        pl.BlockSpec((None, t, FF_SHARD), lambda s, j, kk: (s, 0, 0)),
        pl.BlockSpec((None, d, FF_SHARD), lambda s, j, kk: (s, 0, 0)),
        jax.ShapeDtypeStruct((N_SHARDS, d, FF_SHARD), F32), (N_SHARDS, 1, 1), TN, f"ffn{layer}_wgrad_in",
        carry=carry)


def _attn_masks(n):
    lane = lax.broadcasted_iota(jnp.int32, (CHUNK, LANES), 1)
    lo = lane < HEAD_DIM
    tq = lax.broadcasted_iota(jnp.int32, (CHUNK, 2 * CHUNK), 0)
    jk = lax.broadcasted_iota(jnp.int32, (CHUNK, 2 * CHUNK), 1)
    dist = tq + CHUNK - jk
    mask = (dist >= 0) & (dist < CHUNK) & (jk >= jnp.where(n == 0, CHUNK, 0))
    return lo, mask, dist.astype(F32)


def _half_sum(x, lo):
    s_lo = jnp.sum(jnp.where(lo, x, 0.0), axis=-1, keepdims=True)
    s_hi = jnp.sum(jnp.where(lo, 0.0, x), axis=-1, keepdims=True)
    return jnp.where(lo, s_lo, s_hi)


def _attn_probs(qh, kn, mask, distf, slope, sink):
    s = _dot(qh, kn, NT) * (HEAD_DIM ** -0.5)
    s = jnp.where(mask, s - slope * distf, NEG_BIG)
    m = jnp.maximum(jnp.max(s, axis=-1, keepdims=True), sink)
    e = jnp.exp(s - m)
    den = jnp.sum(e, axis=-1, keepdims=True) + jnp.exp(sink - m)
    return e / den, m, den


def _attn_fwd(qraw, kvd, gq, gk, sinks, carry=None):
    t, d = qraw.shape
    nb = t // CHUNK

    def body(sink_ref, q_ref, cur_ref, prev_ref, gq_ref, gk_ref, o_ref):
        n = pl.program_id(0)
        lo, mask, distf = _attn_masks(n)
        gq_v, gk_v = gq_ref[...], gk_ref[...]
        for kvh in range(N_KV_HEADS):
            ks = slice(kvh * LANES, (kvh + 1) * LANES)
            vs = slice(4 * LANES + kvh * LANES, 4 * LANES + (kvh + 1) * LANES)
            kraw = jnp.concatenate([prev_ref[:, ks], cur_ref[:, ks]], axis=0)
            rk = lax.rsqrt(jnp.mean(kraw * kraw, axis=-1, keepdims=True) + EPS)
            kn = (kraw * rk * gk_v).astype(BF16)
            vv = jnp.concatenate([prev_ref[:, vs], cur_ref[:, vs]], axis=0).astype(BF16)
            for p in range(2):
                jq = 2 * kvh + p
                qp = q_ref[:, jq * LANES:(jq + 1) * LANES]
                r = lax.rsqrt(_half_sum(qp * qp, lo) * (1.0 / HEAD_DIM) + EPS)
                qn = qp * r * gq_v
                acc = None
                for half in range(2):
                    h = 4 * kvh + 2 * p + half
                    sel = lo if half == 0 else jnp.logical_not(lo)
                    qh = jnp.where(sel, qn, 0.0).astype(BF16)
                    pf, _, _ = _attn_probs(qh, kn, mask, distf, SLOPES[h], sink_ref[h])
                    oh = _dot(pf.astype(BF16), vv)
                    acc = oh if half == 0 else jnp.where(lo, acc, oh)
                o_ref[:, jq * LANES:(jq + 1) * LANES] = acc.astype(BF16)

    blk = lambda f: pl.BlockSpec((CHUNK, d), f)
    vec = pl.BlockSpec((1, LANES), lambda n: (0, 0))
    return _call(
        body, [sinks, qraw, kvd, kvd, gq, gk], grid=(nb,),
        in_specs=[pl.BlockSpec(memory_space=pltpu.SMEM), blk(lambda n: (n, 0)), blk(lambda n: (n, 0)),
                  blk(lambda n: (jnp.maximum(n - 1, 0), 0)), vec, vec],
        out_specs=[blk(lambda n: (n, 0))], out_shape=[jax.ShapeDtypeStruct((t, d), BF16)],
        name="attn_fwd", carry=carry)[0]


def _attn_bwd(qraw, kvd, d_o, gq, gk, sinks, carry=None):
    t, d = qraw.shape
    nb = t // CHUNK

    def body(sink_ref, q_ref, cur_ref, prev_ref, do_ref, gq_ref, gk_ref,
             dq_ref, dkv_ref, dsink_ref, dgq_ref, dgk_ref, carry_s, pp_s, cp_s):
        n = pl.program_id(0)

        @pl.when(n == 0)
        def _():
            carry_s[...] = jnp.zeros_like(carry_s)
            dsink_ref[...] = jnp.zeros_like(dsink_ref)
            dgq_ref[...] = jnp.zeros_like(dgq_ref)
            dgk_ref[...] = jnp.zeros_like(dgk_ref)

        @pl.when(n < nb)
        def _():
            lo, mask, distf = _attn_masks(n)
            gq_v, gk_v = gq_ref[...], gk_ref[...]
            for kvh in range(N_KV_HEADS):
                ks = slice(kvh * LANES, (kvh + 1) * LANES)
                vs = slice(4 * LANES + kvh * LANES, 4 * LANES + (kvh + 1) * LANES)
                kraw = jnp.concatenate([prev_ref[:, ks], cur_ref[:, ks]], axis=0)
                rk = lax.rsqrt(jnp.mean(kraw * kraw, axis=-1, keepdims=True) + EPS)
                khat = kraw * rk
                kn = (khat * gk_v).astype(BF16)
                vv = jnp.concatenate([prev_ref[:, vs], cur_ref[:, vs]], axis=0).astype(BF16)
                dkn = jnp.zeros((2 * CHUNK, LANES), F32)
                dvb = jnp.zeros((2 * CHUNK, LANES), F32)
                for p in range(2):
                    jq = 2 * kvh + p
                    cols = slice(jq * LANES, (jq + 1) * LANES)
                    qp = q_ref[:, cols]
                    r = lax.rsqrt(_half_sum(qp * qp, lo) * (1.0 / HEAD_DIM) + EPS)
                    qhat = qp * r
                    qn = qhat * gq_v
                    dop = do_ref[:, cols]
                    dqn = None
                    for half in range(2):
                        h = 4 * kvh + 2 * p + half
                        sel = lo if half == 0 else jnp.logical_not(lo)
                        qh = jnp.where(sel, qn, 0.0).astype(BF16)
                        doh = jnp.where(sel, dop, jnp.zeros_like(dop))
                        sink = sink_ref[h]
                        pf, m, den = _attn_probs(qh, kn, mask, distf, SLOPES[h], sink)
                        dp = _dot(doh, vv, NT)
                        delta = jnp.sum(pf * dp, axis=-1, keepdims=True)
                        p_sink = jnp.exp(sink - m) / den
                        dsink_ref[h:h + 1, :] -= jnp.broadcast_to(
                            jnp.sum(p_sink * delta, axis=0, keepdims=True), (1, LANES))
                        ds = (pf * (dp - delta) * (HEAD_DIM ** -0.5)).astype(BF16)
                        dqh = _dot(ds, kn)
                        dqn = dqh if half == 0 else jnp.where(lo, dqn, dqh)
                        dkn = dkn + _dot(ds, qh, TN)
                        dvb = dvb + _dot(pf.astype(BF16), doh, TN)
                    dgq_ref[0:1, :] += jnp.sum(dqn * qhat, axis=0, keepdims=True)
                    gy = dqn * gq_v
                    mq = _half_sum(gy * qhat, lo) * (1.0 / HEAD_DIM)
                    dq_ref[:, cols] = (r * (gy - qhat * mq)).astype(BF16)
                dgk_ref[0:1, :] += jnp.sum(dkn * khat, axis=0, keepdims=True)
                gyk = dkn * gk_v
                dkraw = rk * (gyk - khat * jnp.mean(gyk * khat, axis=-1, keepdims=True))
                pp_s[:, ks] = dkraw[:CHUNK]
                cp_s[:, ks] = dkraw[CHUNK:]
                pp_s[:, vs] = dvb[:CHUNK]
                cp_s[:, vs] = dvb[CHUNK:]
            dkv_ref[...] = (carry_s[...] + pp_s[...]).astype(BF16)
            carry_s[...] = cp_s[...]

        @pl.when(n == nb)
        def _():
            dkv_ref[...] = carry_s[...].astype(BF16)

    blk = lambda f: pl.BlockSpec((CHUNK, d), f)
    vec = pl.BlockSpec((1, LANES), lambda n: (0, 0))
    cur = lambda n: (jnp.minimum(n, nb - 1), 0)
    prev = lambda n: (jnp.maximum(jnp.minimum(n, nb - 1) - 1, 0), 0)
    small = lambda r: pl.BlockSpec((r, LANES), lambda n: (0, 0))
    return _call(
        body, [sinks, qraw, kvd, kvd, d_o, gq, gk], grid=(nb + 1,),
        in_specs=[pl.BlockSpec(memory_space=pltpu.SMEM), blk(cur), blk(cur), blk(prev), blk(cur), vec, vec],
        out_specs=[blk(cur), blk(lambda n: (jnp.maximum(n - 1, 0), 0)), small(N_Q_HEADS), small(8), small(8)],
        out_shape=[jax.ShapeDtypeStruct((t, d), BF16), jax.ShapeDtypeStruct((t, d), BF16),
                   jax.ShapeDtypeStruct((N_Q_HEADS, LANES), F32), jax.ShapeDtypeStruct((8, LANES), F32),
                   jax.ShapeDtypeStruct((8, LANES), F32)],
        scratch=[pltpu.VMEM((CHUNK, d), F32)] * 3, name="attn_bwd", sem=("arbitrary",), carry=carry)


def _loss_head(y, target, tm=512):
    t, d = y.shape

    def body(y_ref, t_ref, dy_ref, loss_ref, acc_ref):
        i = pl.program_id(0)

        @pl.when(i == 0)
        def _():
            acc_ref[...] = jnp.zeros_like(acc_ref)

        err = y_ref[...] - t_ref[...]
        dy_ref[...] = err * (1.0 / d)
        acc_ref[...] += jnp.sum(err * err, axis=0, keepdims=True)

        @pl.when(i == t // tm - 1)
        def _():
            loss_ref[...] = jnp.broadcast_to(0.5 / d * jnp.sum(acc_ref[...], axis=1, keepdims=True), loss_ref.shape)

    row = pl.BlockSpec((tm, d), lambda i: (i, 0))
    return _call(
        body, [y, target], grid=(t // tm,), in_specs=[row, row],
        out_specs=[row, pl.BlockSpec((8, LANES), lambda i: (0, 0))],
        out_shape=[jax.ShapeDtypeStruct((t, d), F32), jax.ShapeDtypeStruct((8, LANES), F32)],
        scratch=[pltpu.VMEM((1, d), F32)], name="loss_head", sem=("arbitrary",))


def _adamw_math(g, w, m, v):
    m = ADAM_B1 * m + (1.0 - ADAM_B1) * g
    v = ADAM_B2 * v + (1.0 - ADAM_B2) * (g * g)
    m_hat = m / (1.0 - ADAM_B1 ** ADAM_STEP)
    v_hat = v / (1.0 - ADAM_B2 ** ADAM_STEP)
    delta = -ADAM_LR * (m_hat / (jnp.sqrt(v_hat) + ADAM_EPS) + ADAM_WD * w)
    return delta, m, v


def _row_tile(r, cap=128):
    for tr in range(min(r, cap), 0, -1):
        if r % tr == 0 and (tr % 8 == 0 or tr == r):
            return tr
    return r


def _chip_sum(grad, recv, core_idx, name):
    _, r, c = grad.shape
    tr = _row_tile(r, 256)

    def body(c_ref, g_ref, a_ref, p_ref):
        p_ref[...] = g_ref[...] + a_ref[...]

    return pl.pallas_call(
        body,
        grid_spec=pltpu.PrefetchScalarGridSpec(
            num_scalar_prefetch=1, grid=(4, r // tr),
            in_specs=[pl.BlockSpec((None, None, tr, c), lambda q, i, cr: (q, cr[0], i, 0)),
                      pl.BlockSpec((None, tr, c), lambda q, i, cr: (q, i, 0))],
            out_specs=pl.BlockSpec((None, tr, c), lambda q, i, cr: (q, i, 0))),
        out_shape=jax.ShapeDtypeStruct((4, r, c), F32), name=name, compiler_params=_params(),
    )(core_idx, grad.reshape(4, 2, r, c), recv)


def _adamw_sharded(psum, others, chip_idx, w, m, v, name):
    r, c = w.shape
    tr = _row_tile(min(o.shape[1] for o in others))
    starts, r0 = [], 0
    for o in others:
        starts.append(r0 // tr)
        r0 += o.shape[1]
    n_oth = len(others)

    def body(q_ref, own_ref, *refs):
        oth_refs = refs[:n_oth]
        w_ref, m_ref, v_ref, g_ref, d_ref, nm_ref, nv_ref = refs[n_oth:]
        i = pl.program_id(0)
        oth = oth_refs[0][...]
        for k in range(1, n_oth):
            oth = jnp.where(i >= starts[k], oth_refs[k][...], oth)
        g = ((own_ref[...] + oth[0]) + oth[1]) + oth[2]
        delta, nm, nv = _adamw_math(g, w_ref[...], m_ref[...], v_ref[...])
        g_ref[...] = g
        d_ref[...] = delta
        nm_ref[...] = nm
        nv_ref[...] = nv

    def oth_spec(k):
        nblk = others[k].shape[1] // tr
        return pl.BlockSpec((3, tr, c), lambda i, q: (0, jnp.clip(i - starts[k], 0, nblk - 1), 0))

    row = pl.BlockSpec((tr, c), lambda i, q: (i, 0))
    return pl.pallas_call(
        body,
        grid_spec=pltpu.PrefetchScalarGridSpec(
            num_scalar_prefetch=1, grid=(r // tr,),
            in_specs=[pl.BlockSpec((None, tr, c), lambda i, q: (q[0], i, 0))]
            + [oth_spec(k) for k in range(n_oth)] + [row, row, row],
            out_specs=[row] * 4),
        out_shape=[jax.ShapeDtypeStruct((r, c), F32)] * 4, name=name, compiler_params=_params(),
    )(chip_idx, psum, *others, w, m, v)


def _adamw_summed(parts, ws, ms, vs, name):
    n = len(parts)

    def body(*refs):
        p_refs, w_refs, m_refs, v_refs = refs[:n], refs[n:2 * n], refs[2 * n:3 * n], refs[3 * n:4 * n]
        o_refs = refs[4 * n:]
        for i in range(n):
            g = p_refs[i][0]
            for k in range(1, N_SHARDS):
                g = g + p_refs[i][k]
            delta, nm, nv = _adamw_math(g, w_refs[i][...], m_refs[i][...], v_refs[i][...])
            o_refs[4 * i][...] = g
            o_refs[4 * i + 1][...] = delta
            o_refs[4 * i + 2][...] = nm
            o_refs[4 * i + 3][...] = nv

    shapes = [jax.ShapeDtypeStruct(w.shape, F32) for w in ws for _ in range(4)]
    outs = pl.pallas_call(body, out_shape=shapes, name=name, compiler_params=_params())(*parts, *ws, *ms, *vs)
    return [outs[4 * i:4 * i + 4] for i in range(n)]


def _dup_heads(w):
    lead = w.shape[:-1]
    w4 = w.reshape(lead + (N_KV_HEADS, 1, HEAD_DIM))
    return jnp.broadcast_to(w4, lead + (N_KV_HEADS, 2, HEAD_DIM)).reshape(lead + (N_KV_HEADS * LANES,))


def _fold_heads(g):
    lead = g.shape[:-1]
    return g.reshape(lead + (N_KV_HEADS, 2, HEAD_DIM)).sum(axis=-2).reshape(lead + (N_KV_HEADS * HEAD_DIM,))


def kernel(x, a_norm, a_w_in, a_v_norm, a_w_s, a_b_s, a_w_out, f_norm, f_w_in, f_conv_w, f_conv_b, f_w_out, kv_norm, w_kv, k_norm, b_norm, b_w_q, b_q_norm, b_sinks, b_w_o, loss_target, m_a_norm, m_a_w_in, m_a_v_norm, m_a_w_s, m_a_b_s, m_a_w_out, m_f_norm, m_f_w_in, m_f_conv_w, m_f_conv_b, m_f_w_out, m_kv_norm, m_w_kv, m_k_norm, m_b_norm, m_b_w_q, m_b_q_norm, m_b_sinks, m_b_w_o, v_a_norm, v_a_w_in, v_a_v_norm, v_a_w_s, v_a_b_s, v_a_w_out, v_f_norm, v_f_w_in, v_f_conv_w, v_f_conv_b, v_f_w_out, v_kv_norm, v_w_kv, v_k_norm, v_b_norm, v_b_w_q, v_b_q_norm, v_b_sinks, v_b_w_o):
    d = D_MODEL
    xi, yi, ci = _coords()
    chip_idx = (2 * xi + yi).reshape(1).astype(jnp.int32)
    core_idx = ci.reshape(1).astype(jnp.int32)
    bf = lambda a: a.astype(BF16)
    row = lambda v_: v_.reshape(1, -1)
    x0, target = x[0], loss_target[0]
    t = x0.shape[0]
    res = {}

    def reduce_first(grads, names):
        return [_chip_sum(g, a, core_idx, f"chip_sum_{k}") for g, a, k in zip(grads, reduce_first.ex.results, names)]

    def update(psum, others, w, m, v, name):
        shp = w.shape
        r2 = lambda t_: t_.reshape(-1, shp[-1])
        outs = _adamw_sharded(psum, others, chip_idx, r2(w), r2(m), r2(v), name)
        return [o_.reshape(shp) for o_ in outs]

    g_a_in, g_a_out, g_a_norm, g_a_v_norm, g_conv = _exchange_alone(
        _Gather([bf(a_w_in[0]), bf(a_w_out[0]), a_norm, a_v_norm, f_conv_w.reshape(6, FF_SHARD)]), "gather_first")
    a_norm_full, a_v_norm_full = g_a_norm.reshape(1, d), g_a_v_norm.reshape(1, d)
    conv_w = lax.reduce_precision(g_conv.reshape(N_SHARDS, 2, 3, FF_SHARD), 8, 7)
    cw = jnp.pad(jnp.transpose(conv_w, (1, 0, 2, 3)), ((0, 0), (0, 0), (0, 5), (0, 0)))
    w_a_in_flat = jnp.transpose(g_a_in, (1, 0, 2)).reshape(d, 2 * d)
    cb = f_conv_b.reshape(2, N_SHARDS, 1, FF_SHARD)
    tri = jnp.tril(jnp.ones((CHUNK, CHUNK), dtype=bool))
    w_causal = jnp.where(tri[None], a_w_s[0], 0.0).astype(BF16)
    w_causal_t = jnp.transpose(w_causal, (0, 2, 1))
    b_sb = jnp.broadcast_to(a_b_s[0][:, :, None], (N_GROUPS, CHUNK, CHUNK))
    w_a_out = g_a_out.reshape(d, d)
    gq = jnp.tile(b_q_norm.reshape(1, HEAD_DIM), (1, 2))
    gk = jnp.tile(k_norm.reshape(1, HEAD_DIM), (1, 2))
    sinks = b_sinks.reshape(N_Q_HEADS)

    (h1,) = _rms_fwd(x0, [a_norm_full], "a_norm_fwd")
    ex = _Gather([bf(f_w_in[0])])
    zpre, x1 = _sgu_fwd(x0, h1, g_a_in, a_v_norm_full, w_causal, b_sb, w_a_out, carry=ex)
    w_in0 = ex.results[0]
    (hf0,) = _rms_fwd(x1, [f_norm[0:1]], "f0_norm_fwd")
    ex = _Gather([bf(f_w_out[0]), bf(w_kv), bf(b_w_q[0]), bf(b_w_o[0])])
    a0 = _ffn_in(hf0, w_in0, 0, carry=ex)
    w_out0 = ex.results[0].reshape(D_FF, d)
    kv_full = ex.results[1].reshape(d, 2 * N_KV_HEADS * HEAD_DIM)
    w_q, w_o = ex.results[2].reshape(d, d), ex.results[3].reshape(d, d)
    half = N_KV_HEADS * HEAD_DIM
    w_kv_dup = jnp.concatenate([_dup_heads(kv_full[:, :half]), _dup_heads(kv_full[:, half:])], axis=1)
    ex = _Gather([bf(f_w_in[1])])
    x2 = _ffn_out(a0, cw[0], cb[0], w_out0, x1, 0, carry=ex)
    w_in1 = ex.results[0]
    hk, hq = _rms_fwd(x2, [row(kv_norm), b_norm], "kvq_norm_fwd")
    kvd = _mm_rows(hk, w_kv_dup, F32, "kv_proj")
    qraw = _mm_rows(hq, w_q, F32, "q_proj")
    ex = _Gather([bf(f_w_out[1])])
    o = _attn_fwd(qraw, kvd, gq, gk, sinks, carry=ex)
    w_out1 = ex.results[0].reshape(D_FF, d)
    x3 = _mm_rows(o, w_o, F32, "o_proj", res=x2)
    (hf1,) = _rms_fwd(x3, [f_norm[1:2]], "f1_norm_fwd")
    a1 = _ffn_in(hf1, w_in1, 1)
    x4 = _ffn_out(a1, cw[1], cb[1], w_out1, x3, 1)
    dy, loss_lanes = _loss_head(x4, target)
    loss = lax.psum(loss_lanes[0, 0], ("x", "y", "c"))

    dhu1, dw_out1, dconv1 = _ffn_bwd_act(a1, cw[1], cb[1], w_out1, dy, 1)
    dw_out1 = dw_out1.reshape(N_SHARDS, D_FF // N_SHARDS, d)
    reduce_first.ex = _ToSibling([dw_out1])
    da1, dhf1 = _ffn_bwd_in(dhu1, cw[1], w_in1, 1, carry=reduce_first.ex)
    (p_out1,) = reduce_first([dw_out1], ["f_w_out1"])
    ex_out1 = _ToChips([p_out1])
    dw_in1 = _ffn_wgrad_in(hf1, da1, 1, carry=ex_out1)
    reduce_first.ex = _ToSibling([dw_in1])
    dx3, dgf1 = _rms_bwd(x3, [f_norm[1:2]], [dhf1], dy, "f1_norm_bwd", carry=reduce_first.ex)
    (p_in1,) = reduce_first([dw_in1], ["f_w_in1"])
    d_o = _mm_rows(dx3, w_o, BF16, "o_proj_bwd", trans_w=True)
    dw_o = _mm_wgrad(o, dx3, "o_wgrad").reshape(N_SHARDS, d // N_SHARDS, d)
    ex_in1 = _ToChips([p_in1])
    dq, dkv, dsink, dgq, dgk = _attn_bwd(qraw, kvd, d_o, gq, gk, sinks, carry=ex_in1)
    dw_q = _mm_wgrad(hq, dq, "q_wgrad").reshape(N_SHARDS, d // N_SHARDS, d)
    dw_kv_dup = _mm_wgrad(hk, dkv, "kv_wgrad")
    dw_kv = jnp.concatenate(
        [_fold_heads(dw_kv_dup[:, :4 * LANES]), _fold_heads(dw_kv_dup[:, 4 * LANES:])], axis=1
    ).reshape(N_SHARDS, d // N_SHARDS, 2 * N_KV_HEADS * HEAD_DIM)
    reduce_first.ex = _ToSibling([dw_o, dw_q, dw_kv])
    dhq = _mm_rows(dq, w_q, F32, "q_proj_bwd", trans_w=True, carry=reduce_first.ex)
    dhk = _mm_rows(dkv, w_kv_dup, F32, "kv_proj_bwd", trans_w=True)
    p_o, p_q, p_kv = reduce_first([dw_o, dw_q, dw_kv], ["b_w_o", "b_w_q", "w_kv"])
    dx2, dg2 = _rms_bwd(x2, [row(kv_norm), b_norm], [dhk, dhq], dx3, "kvq_norm_bwd")
    ex_attn = _ToChips([p_o, p_q, p_kv])
    dhu0, dw_out0, dconv0 = _ffn_bwd_act(a0, cw[0], cb[0], w_out0, dx2, 0, carry=ex_attn)
    dw_out0 = dw_out0.reshape(N_SHARDS, D_FF // N_SHARDS, d)
    reduce_first.ex = _ToSibling([dw_out0])
    da0, dhf0 = _ffn_bwd_in(dhu0, cw[0], w_in0, 0, carry=reduce_first.ex)
    (p_out0,) = reduce_first([dw_out0], ["f_w_out0"])
    ex_out0 = _ToChips([p_out0])
    dw_in0 = _ffn_wgrad_in(hf0, da0, 0, carry=ex_out0)
    reduce_first.ex = _ToSibling([dw_in0])
    dx1, dgf0 = _rms_bwd(x1, [f_norm[0:1]], [dhf0], dx2, "f0_norm_bwd", carry=reduce_first.ex)
    (p_in0,) = reduce_first([dw_in0], ["f_w_in0"])
    dz, y, dwc, dbs, dgv = _sgu_bwd(dx1, zpre, w_a_out, a_v_norm_full, w_causal, w_causal_t, b_sb)
    dw_a_out = _mm_wgrad(y, dx1, "a_out_wgrad").reshape(N_SHARDS, d // N_SHARDS, d)
    nsub = g_a_in.shape[2]
    ex_in0a = _ToChips([p_in0], rows=(0, d // 2))
    dw_a_in = _mm(
        h1, dz, pl.BlockSpec((t, d), lambda s, j, kk: (0, 0)), pl.BlockSpec((t, nsub), lambda s, j, kk: (0, s)),
        pl.BlockSpec((None, d, nsub), lambda s, j, kk: (s, 0, 0)), jax.ShapeDtypeStruct((N_SHARDS, d, nsub), F32),
        (N_SHARDS, 1, 1), TN, "a_in_wgrad", carry=ex_in0a)

    def conv_grads(dconv):
        return jnp.transpose(dconv, (1, 0, 2, 3)).reshape(N_SHARDS, 8, FF_SHARD)

    dconv0, dconv1 = conv_grads(dconv0), conv_grads(dconv1)
    g_conv_w = jnp.concatenate([dconv0[:, 0:3, :], dconv1[:, 0:3, :]], axis=1)
    g_a_v_norm = dgv[0].reshape(N_SHARDS, 1, LANES)
    rep = ["a_w_s", "a_b_s", "f_norm", "f_conv_b", "kv_norm", "k_norm", "b_norm", "b_q_norm", "b_sinks"]
    rep_g = dict(
        a_w_s=dwc.reshape(N_GROUPS * CHUNK, CHUNK), a_b_s=dbs[:, :, 0], f_norm=jnp.stack([dgf0[0], dgf1[0]]),
        f_conv_b=jnp.stack([dconv0[:, 3, :].reshape(-1), dconv1[:, 3, :].reshape(-1)]), kv_norm=dg2[0:1],
        k_norm=(dgk[0, :HEAD_DIM] + dgk[0, HEAD_DIM:])[None], b_norm=dg2[1:2],
        b_q_norm=(dgq[0, :HEAD_DIM] + dgq[0, HEAD_DIM:])[None], b_sinks=dsink[:, 0][None])
    reduce_first.ex = _ToSibling([dw_a_out, dw_a_in, g_a_v_norm, g_conv_w])
    ex_in0b = _ToChips([p_in0], rows=(d // 2, d // 2))
    ex_rep = _Gather([rep_g[k] for k in rep])
    together = _Together([reduce_first.ex, ex_in0b, ex_rep])
    dh1 = _mm_rows(dz, w_a_in_flat, F32, "a_in_bwd", trans_w=True, carry=together)
    together.spread()
    p_a_out, p_a_in, p_a_v_norm, p_conv_w = reduce_first(
        [dw_a_out, dw_a_in, g_a_v_norm, g_conv_w], ["a_w_out", "a_w_in", "a_v_norm", "f_conv_w"])
    ex_last = _ToChips([p_a_out, p_a_in, p_a_v_norm, p_conv_w])
    grad_x, dg0 = _rms_bwd(x0, [a_norm_full], [dh1], dx1, "a_norm_bwd", carry=ex_last)
    (a_norm_parts,) = _exchange_alone(_ToOwners([dg0[0].reshape(N_SHARDS, 1, LANES)]), "a_norm_to_owners")

    res["f_w_out"] = [update(p_out0, ex_out0.results, f_w_out[0], m_f_w_out[0], v_f_w_out[0], "adamw_f_w_out0"),
                      update(p_out1, ex_out1.results, f_w_out[1], m_f_w_out[1], v_f_w_out[1], "adamw_f_w_out1")]
    res["f_w_in"] = [update(p_in0, ex_in0a.results + ex_in0b.results, f_w_in[0], m_f_w_in[0], v_f_w_in[0],
                            "adamw_f_w_in0"),
                     update(p_in1, ex_in1.results, f_w_in[1], m_f_w_in[1], v_f_w_in[1], "adamw_f_w_in1")]
    for key in ("f_w_out", "f_w_in"):
        res[key] = [jnp.stack([res[key][0][j], res[key][1][j]]) for j in range(4)]
    res["b_w_o"] = update(p_o, ex_attn.results[0:1], b_w_o, m_b_w_o, v_b_w_o, "adamw_b_w_o")
    res["b_w_q"] = update(p_q, ex_attn.results[1:2], b_w_q, m_b_w_q, v_b_w_q, "adamw_b_w_q")
    res["w_kv"] = update(p_kv, ex_attn.results[2:3], w_kv, m_w_kv, v_w_kv, "adamw_w_kv")
    res["a_w_out"] = update(p_a_out, ex_last.results[0:1], a_w_out, m_a_w_out, v_a_w_out, "adamw_a_w_out")
    res["a_w_in"] = update(p_a_in, ex_last.results[1:2], a_w_in, m_a_w_in, v_a_w_in, "adamw_a_w_in")
    res["a_v_norm"] = update(p_a_v_norm, ex_last.results[2:3], a_v_norm, m_a_v_norm, v_a_v_norm, "adamw_a_v_norm")
    res["f_conv_w"] = [o_.reshape(f_conv_w.shape) for o_ in update(
        p_conv_w, ex_last.results[3:4], f_conv_w.reshape(6, FF_SHARD), m_f_conv_w.reshape(6, FF_SHARD),
        v_f_conv_w.reshape(6, FF_SHARD), "adamw_f_conv_w")]

    rep_w = dict(a_w_s=a_w_s, a_b_s=a_b_s, f_norm=f_norm, f_conv_b=f_conv_b, kv_norm=kv_norm, k_norm=k_norm,
                 b_norm=b_norm, b_q_norm=b_q_norm, b_sinks=b_sinks, a_norm=a_norm)
    rep_m = dict(a_w_s=m_a_w_s, a_b_s=m_a_b_s, f_norm=m_f_norm, f_conv_b=m_f_conv_b, kv_norm=m_kv_norm,
                 k_norm=m_k_norm, b_norm=m_b_norm, b_q_norm=m_b_q_norm, b_sinks=m_b_sinks, a_norm=m_a_norm)
    rep_v = dict(a_w_s=v_a_w_s, a_b_s=v_a_b_s, f_norm=v_f_norm, f_conv_b=v_f_conv_b, kv_norm=v_kv_norm,
                 k_norm=v_k_norm, b_norm=v_b_norm, b_q_norm=v_b_q_norm, b_sinks=v_b_sinks, a_norm=v_a_norm)
    keys = rep + ["a_norm"]
    parts = ex_rep.results + [a_norm_parts]
    as2d = lambda a, p: a.reshape(p.shape[1:])
    rep_outs = _adamw_summed(parts, [as2d(rep_w[k], p) for k, p in zip(keys, parts)],
                             [as2d(rep_m[k], p) for k, p in zip(keys, parts)],
                             [as2d(rep_v[k], p) for k, p in zip(keys, parts)], "adamw_replicated")
    for j, key in enumerate(keys):
        res[key] = [o_.reshape(rep_w[key].shape) for o_ in rep_outs[j]]

    order = ["a_norm", "a_w_in", "a_v_norm", "a_w_s", "a_b_s", "a_w_out", "f_norm", "f_w_in", "f_conv_w", "f_conv_b",
             "f_w_out", "kv_norm", "w_kv", "k_norm", "b_norm", "b_w_q", "b_q_norm", "b_sinks", "b_w_o"]
    outs = [loss, grad_x[None]]
    for j in range(4):
        outs += [res[k][j] for k in order]
    return tuple(outs)
```

```python
import jax
import jax.numpy as jnp
from jax import lax
from jax.experimental import pallas as pl
from jax.experimental.pallas import tpu as pltpu

F32 = jnp.float32
BF16 = jnp.bfloat16
EPS = 1e-6
D_MODEL = 1024
CHUNK = 128
N_GROUPS = 8
N_SHARDS = 8
HEAD_DIM = 64
N_Q_HEADS = 16
N_KV_HEADS = 4
D_FF = 2816
FF_SHARD = 2 * D_FF // N_SHARDS
LANES = 128
NEG_BIG = -1e30
ADAM_LR = 0.001
ADAM_B1 = 0.9
ADAM_B2 = 0.999
ADAM_EPS = 1e-08
ADAM_WD = 0.01
ADAM_STEP = 10
VMEM_LIMIT_BYTES = 56 * 1024 * 1024
MESH = pl.DeviceIdType.MESH

NN = (((1,), (0,)), ((), ()))
NT = (((1,), (1,)), ((), ()))
TN = (((0,), (0,)), ((), ()))
SLOPES = tuple(2.0 ** (-8.0 * (h + 1) / N_Q_HEADS) for h in range(N_Q_HEADS))


def _params(sem=None):
    return pltpu.CompilerParams(dimension_semantics=sem, vmem_limit_bytes=VMEM_LIMIT_BYTES)


def _dot(a, b, dims=NN):
    return lax.dot_general(a, b, dims, preferred_element_type=F32)


def _sigmoid(x):
    return 1.0 / (1.0 + jnp.exp(-x))


def _gelu_parts(z):
    cdf = 0.5 * (1.0 + lax.erf(z * (2.0 ** -0.5)))
    pdf = jnp.exp(-0.5 * z * z) * 0.3989422804014327
    return cdf, pdf


def _coords():
    return lax.axis_index("x"), lax.axis_index("y"), lax.axis_index("c")


class _Gather:
    def __init__(self, srcs):
        self.srcs = list(srcs)
        n = len(self.srcs)
        self.out_shapes = [jax.ShapeDtypeStruct((N_SHARDS,) + s.shape, s.dtype) for s in self.srcs]
        self.sems = [pltpu.SemaphoreType.DMA((n, 7)), pltpu.SemaphoreType.DMA((n, 7)), pltpu.SemaphoreType.DMA((n,))]

    def _plan(self, src, dst, sems):
        send_sems, recv_sems, local_sems = sems
        x, y, c = _coords()
        me, sibling = (x, y, c), (x, y, 1 - c)
        chips = [(1 - x, y), (x, 1 - y), (1 - x, 1 - y)]
        n = len(src)

        def rows(e, dev):
            return dst[e].at[4 * dev[0] + 2 * dev[1] + dev[2]]

        def copy(e, slot, block, to, from_own=False):
            return pltpu.make_async_remote_copy(
                src_ref=src[e] if from_own else rows(e, block), dst_ref=rows(e, block),
                send_sem=send_sems.at[e, slot], recv_sem=recv_sems.at[e, slot], device_id=to, device_id_type=MESH)

        mine = [pltpu.make_async_copy(src[e], rows(e, me), local_sems.at[e]) for e in range(n)]
        first = []
        for e in range(n):
            first.append(copy(e, 0, me, sibling, from_own=True))
            first += [copy(e, 1 + j, me, (*chip, c), from_own=True) for j, chip in enumerate(chips)]
        return n, me, sibling, chips, c, copy, mine, first

    def start(self, src, dst, sems):
        _, _, _, _, _, _, mine, first = self._plan(src, dst, sems)
        for cp in mine + first:
            cp.start()

    def finish(self, src, dst, sems):
        n, me, sibling, chips, c, copy, mine, first = self._plan(src, dst, sems)
        passed = []
        for j, chip in enumerate(chips):
            for e in range(n):
                copy(e, 1 + j, (*chip, c), me).wait_recv()
                cp = copy(e, 4 + j, (*chip, c), sibling)
                cp.start()
                passed.append(cp)
        for e in range(n):
            copy(e, 0, sibling, me).wait_recv()
            for j, chip in enumerate(chips):
                copy(e, 4 + j, (*chip, 1 - c), me).wait_recv()
        for cp in first + passed:
            cp.wait_send()
        for cp in mine:
            cp.wait()


class _ToSibling:
    def __init__(self, grads):
        self.srcs = list(grads)
        n = len(self.srcs)
        self.out_shapes = [jax.ShapeDtypeStruct((4,) + g.shape[1:], g.dtype) for g in self.srcs]
        self.sems = [pltpu.SemaphoreType.DMA((n, 4)), pltpu.SemaphoreType.DMA((n, 4))]

    def _copies(self, src, dst, sems):
        send_sems, recv_sems = sems
        x, y, c = _coords()
        return [
            pltpu.make_async_remote_copy(
                src_ref=src[i].at[2 * q + (1 - c)], dst_ref=dst[i].at[q], send_sem=send_sems.at[i, q],
                recv_sem=recv_sems.at[i, q], device_id=(x, y, 1 - c), device_id_type=MESH)
            for i in range(len(src)) for q in range(4)]

    def start(self, src, dst, sems):
        for cp in self._copies(src, dst, sems):
            cp.start()

    def finish(self, src, dst, sems):
        for cp in self._copies(src, dst, sems):
            cp.wait()


class _ToChips:
    def __init__(self, psums, rows=None):
        self.srcs = list(psums)
        n = len(self.srcs)
        self.rows = rows
        self.out_shapes = [
            jax.ShapeDtypeStruct((3, p.shape[1] if rows is None else rows[1]) + p.shape[2:], p.dtype)
            for p in self.srcs]
        self.sems = [pltpu.SemaphoreType.DMA((n, 3)), pltpu.SemaphoreType.DMA((n, 3))]

    def _copies(self, src, dst, sems):
        send_sems, recv_sems = sems
        x, y, c = _coords()
        peers = [(x, 1 - y), (1 - x, y), (1 - x, 1 - y)]

        def part(i, q):
            if self.rows is None:
                return src[i].at[q]
            return src[i].at[q, pl.ds(self.rows[0], self.rows[1])]

        return [
            pltpu.make_async_remote_copy(
                src_ref=part(i, 2 * px + py), dst_ref=dst[i].at[r], send_sem=send_sems.at[i, r],
                recv_sem=recv_sems.at[i, r], device_id=(px, py, c), device_id_type=MESH)
            for i in range(len(src)) for r, (px, py) in enumerate(peers)]

    def start(self, src, dst, sems):
        for cp in self._copies(src, dst, sems):
            cp.start()

    def finish(self, src, dst, sems):
        for cp in self._copies(src, dst, sems):
            cp.wait()


class _ToOwners:
    def __init__(self, grads):
        self.srcs = list(grads)
        n = len(self.srcs)
        self.out_shapes = [jax.ShapeDtypeStruct(g.shape, g.dtype) for g in self.srcs]
        self.sems = [pltpu.SemaphoreType.DMA((n, 7)), pltpu.SemaphoreType.DMA((n, 7)), pltpu.SemaphoreType.DMA((n,))]

    def _copies(self, src, dst, sems):
        send_sems, recv_sems, local_sems = sems
        x, y, c = _coords()
        me = 4 * x + 2 * y + c
        copies = [pltpu.make_async_copy(src[i].at[me], dst[i].at[me], local_sems.at[i]) for i in range(len(src))]
        for i in range(len(src)):
            for rel in range(1, N_SHARDS):
                px = x ^ (rel >> 2) if rel >> 2 else x
                py = y ^ ((rel >> 1) & 1) if (rel >> 1) & 1 else y
                pc = c ^ (rel & 1) if rel & 1 else c
                copies.append(pltpu.make_async_remote_copy(
                    src_ref=src[i].at[4 * px + 2 * py + pc], dst_ref=dst[i].at[me], send_sem=send_sems.at[i, rel - 1],
                    recv_sem=recv_sems.at[i, rel - 1], device_id=(px, py, pc), device_id_type=MESH))
        return copies

    def start(self, src, dst, sems):
        for cp in self._copies(src, dst, sems):
            cp.start()

    def finish(self, src, dst, sems):
        for cp in self._copies(src, dst, sems):
            cp.wait()


class _Together:
    def __init__(self, parts):
        self.parts = list(parts)
        self.srcs = [s for p in self.parts for s in p.srcs]
        self.out_shapes = [s for p in self.parts for s in p.out_shapes]
        self.sems = [s for p in self.parts for s in p.sems]

    def _split(self, src, dst, sems):
        a = b = c = 0
        for p in self.parts:
            na, nc = len(p.srcs), len(p.sems)
            yield p, src[a:a + na], dst[b:b + na], sems[c:c + nc]
            a, b, c = a + na, b + na, c + nc

    def start(self, src, dst, sems):
        for p, s, d, m in self._split(src, dst, sems):
            p.start(s, d, m)

    def finish(self, src, dst, sems):
        for p, s, d, m in self._split(src, dst, sems):
            p.finish(s, d, m)

    def spread(self):
        b = 0
        for p in self.parts:
            p.results = self.results[b:b + len(p.srcs)]
            b += len(p.srcs)


def _call(body, args, *, grid, in_specs, out_specs, out_shape, name, scratch=(), sem=None, carry=None):
    out_shape, out_specs = list(out_shape), list(out_specs)
    if carry is None:
        return pl.pallas_call(
            body, grid=grid, in_specs=list(in_specs), out_specs=out_specs, out_shape=out_shape,
            scratch_shapes=list(scratch), name=name, compiler_params=_params(sem))(*args)
    n_in, n_out, n_scr, n_c = len(args), len(out_shape), len(scratch), len(carry.srcs)
    steps = tuple(grid)

    def carried(*refs):
        ins, rest = refs[:n_in], refs[n_in:]
        c_src, rest = rest[:n_c], rest[n_c:]
        outs, rest = rest[:n_out], rest[n_out:]
        c_dst, rest = rest[:n_c], rest[n_c:]
        scr, sems = rest[:n_scr], rest[n_scr:]
        first = pl.program_id(0) == 0
        last = pl.program_id(0) == steps[0] - 1
        for ax in range(1, len(steps)):
            first = first & (pl.program_id(ax) == 0)
            last = last & (pl.program_id(ax) == steps[ax] - 1)

        @pl.when(first)
        def _():
            carry.start(c_src, c_dst, sems)

        body(*ins, *outs, *scr)

        @pl.when(last)
        def _():
            carry.finish(c_src, c_dst, sems)

    hbm = pl.BlockSpec(memory_space=pl.ANY)
    res = pl.pallas_call(
        carried, grid=grid, in_specs=list(in_specs) + [hbm] * n_c, out_specs=out_specs + [hbm] * n_c,
        out_shape=out_shape + carry.out_shapes, scratch_shapes=list(scratch) + carry.sems, name=name,
        compiler_params=_params(("arbitrary",) * len(steps)))(*args, *carry.srcs)
    carry.results = list(res[n_out:])
    return list(res[:n_out])


def _exchange_alone(ex, name):
    n = len(ex.srcs)

    def body(*refs):
        src, dst, sems = refs[:n], refs[n:2 * n], refs[2 * n:]
        ex.start(src, dst, sems)
        ex.finish(src, dst, sems)

    hbm = pl.BlockSpec(memory_space=pl.ANY)
    res = pl.pallas_call(body, in_specs=[hbm] * n, out_specs=[hbm] * n, out_shape=ex.out_shapes,
                         scratch_shapes=ex.sems, name=name)(*ex.srcs)
    ex.results = list(res)
    return ex.results


def _rms_fwd(x, gains, name, tm=512, carry=None):
    t, d = x.shape
    n = len(gains)

    def body(*refs):
        x_ref, g_refs, h_refs = refs[0], refs[1:1 + n], refs[1 + n:]
        xf = x_ref[...]
        xhat = xf * lax.rsqrt(jnp.mean(xf * xf, axis=-1, keepdims=True) + EPS)
        for g_ref, h_ref in zip(g_refs, h_refs):
            h_ref[...] = (xhat * g_ref[...]).astype(BF16)

    row = pl.BlockSpec((tm, d), lambda i: (i, 0))
    vec = pl.BlockSpec((1, d), lambda i: (0, 0))
    return _call(body, [x, *gains], grid=(t // tm,), in_specs=[row] + [vec] * n, out_specs=[row] * n,
                 out_shape=[jax.ShapeDtypeStruct((t, d), BF16)] * n, name=name, carry=carry)


def _rms_bwd(x, gains, dhs, dres, name, tm=256, carry=None):
    t, d = x.shape
    n = len(gains)

    def body(*refs):
        x_ref, dres_ref = refs[0], refs[1]
        g_refs, dh_refs = refs[2:2 + n], refs[2 + n:2 + 2 * n]
        dx_ref, dg_ref = refs[2 + 2 * n], refs[3 + 2 * n]
        i = pl.program_id(0)

        @pl.when(i == 0)
        def _():
            dg_ref[...] = jnp.zeros_like(dg_ref)

        xf = x_ref[...]
        r = lax.rsqrt(jnp.mean(xf * xf, axis=-1, keepdims=True) + EPS)
        xhat = xf * r
        dx = dres_ref[...]
        for j in range(n):
            dh = dh_refs[j][...]
            dg_ref[j:j + 1, :] += jnp.sum(dh * xhat, axis=0, keepdims=True)
            gy = dh * g_refs[j][...]
            dx = dx + r * (gy - xhat * jnp.mean(gy * xhat, axis=-1, keepdims=True))
        dx_ref[...] = dx

    row = pl.BlockSpec((tm, d), lambda i: (i, 0))
    vec = pl.BlockSpec((1, d), lambda i: (0, 0))
    return _call(body, [x, dres, *gains, *dhs], grid=(t // tm,), in_specs=[row, row] + [vec] * n + [row] * n,
                 out_specs=[row, pl.BlockSpec((8, d), lambda i: (0, 0))],
                 out_shape=[jax.ShapeDtypeStruct((t, d), F32), jax.ShapeDtypeStruct((8, d), F32)],
                 name=name, sem=("arbitrary",), carry=carry)


def _mm(a, b, a_spec, b_spec, o_spec, out_shape, grid, dims, name, res=None, res_spec=None, carry=None):
    nk = grid[2]
    acc_shape = tuple(s for s in o_spec.block_shape if s is not None)

    def body(*refs):
        a_ref, b_ref = refs[0], refs[1]
        r_ref = refs[2] if res is not None else None
        o_ref = refs[3] if res is not None else refs[2]
        p = _dot(a_ref[...].astype(BF16), b_ref[...].astype(BF16), dims)
        if nk == 1:
            if res is not None:
                p = p + r_ref[...]
            o_ref[...] = p.astype(o_ref.dtype)
            return
        acc_ref = refs[-1]
        k = pl.program_id(2)

        @pl.when(k == 0)
        def _():
            acc_ref[...] = p

        @pl.when(k > 0)
        def _():
            acc_ref[...] += p

        @pl.when(k == nk - 1)
        def _():
            out = acc_ref[...]
            if res is not None:
                out = out + r_ref[...]
            o_ref[...] = out.astype(o_ref.dtype)

    ins = [a, b] + ([res] if res is not None else [])
    specs = [a_spec, b_spec] + ([res_spec] if res is not None else [])
    return _call(body, ins, grid=grid, in_specs=specs, out_specs=[o_spec], out_shape=[out_shape],
                 scratch=[pltpu.VMEM(acc_shape, F32)] if nk > 1 else [], name=name,
                 sem=("parallel", "parallel", "arbitrary"), carry=carry)[0]


def _mm_rows(a, w, out_dtype, name, trans_w=False, res=None, tm=512, carry=None):
    t, k = a.shape
    n = w.shape[0] if trans_w else w.shape[1]
    return _mm(
        a, w, pl.BlockSpec((tm, k), lambda i, j, kk: (i, 0)), pl.BlockSpec(w.shape, lambda i, j, kk: (0, 0)),
        pl.BlockSpec((tm, n), lambda i, j, kk: (i, 0)), jax.ShapeDtypeStruct((t, n), out_dtype), (t // tm, 1, 1),
        NT if trans_w else NN, name, res=res,
        res_spec=None if res is None else pl.BlockSpec((tm, n), lambda i, j, kk: (i, 0)), carry=carry)


def _mm_wgrad(a, b, name, carry=None):
    t, m = a.shape
    n = b.shape[1]
    tn = n // (4 if b.dtype == F32 else 2)
    return _mm(
        a, b, pl.BlockSpec((t, m), lambda i, j, kk: (0, 0)), pl.BlockSpec((t, tn), lambda i, j, kk: (0, j)),
        pl.BlockSpec((m, tn), lambda i, j, kk: (0, j)), jax.ShapeDtypeStruct((m, n), F32), (1, n // tn, 1), TN, name,
        carry=carry)


def _sgu_fwd(x0, h1, w_in, g_v, w_c, b_sb, w_out, tm=256, carry=None):
    t, d = x0.shape
    nsub = w_in.shape[2]

    def body(x_ref, h_ref, win_ref, gv_ref, wc_ref, bsb_ref, wout_ref, zpre_ref, x1_ref, u_s, v_s, vn_s, y_s):
        h = h_ref[...]
        for k in range(N_SHARDS):
            zk = _dot(h, win_ref[k])
            zpre_ref[:, k * nsub:(k + 1) * nsub] = zk
            cdf, _ = _gelu_parts(zk)
            if k < N_SHARDS // 2:
                u_s[:, k * nsub:(k + 1) * nsub] = zk * cdf
            else:
                v_s[:, (k - 4) * nsub:(k - 3) * nsub] = zk * cdf
        v = v_s[...]
        rv = lax.rsqrt(jnp.mean(v * v, axis=-1, keepdims=True) + EPS)
        vn_s[...] = (v * rv * gv_ref[...]).astype(BF16)
        for ci in range(tm // CHUNK):
            rows = slice(ci * CHUNK, (ci + 1) * CHUNK)
            for g in range(N_GROUPS):
                cols = slice(g * LANES, (g + 1) * LANES)
                sv = _dot(wc_ref[g], vn_s[rows, cols]) + bsb_ref[g]
                y_s[rows, cols] = (u_s[rows, cols] * sv).astype(BF16)
        x1_ref[...] = x_ref[...] + _dot(y_s[...], wout_ref[...])

    row = pl.BlockSpec((tm, d), lambda i: (i, 0))
    full = lambda a: pl.BlockSpec(a.shape, lambda i: (0,) * a.ndim)
    return _call(
        body, [x0, h1, w_in, g_v, w_c, b_sb, w_out], grid=(t // tm,),
        in_specs=[row, row, full(w_in), full(g_v), full(w_c), full(b_sb), full(w_out)],
        out_specs=[pl.BlockSpec((tm, 2 * d), lambda i: (i, 0)), row],
        out_shape=[jax.ShapeDtypeStruct((t, 2 * d), F32), jax.ShapeDtypeStruct((t, d), F32)],
        scratch=[pltpu.VMEM((tm, d), F32), pltpu.VMEM((tm, d), F32), pltpu.VMEM((tm, d), BF16),
                 pltpu.VMEM((tm, d), BF16)],
        name="sgu_fwd", carry=carry)


def _sgu_bwd(dx1, zpre, w_out, g_v, w_c, w_ct, b_sb, tm=256, carry=None):
    t, d = dx1.shape

    def body(dx_ref, zpre_ref, wout_ref, gv_ref, wc_ref, wct_ref, bsb_ref,
             dz_ref, y_ref, dwc_ref, dbs_ref, dgv_ref, u_s, vn_s, dy_s, du_s, dvn_s):
        i = pl.program_id(0)

        @pl.when(i == 0)
        def _():
            dwc_ref[...] = jnp.zeros_like(dwc_ref)
            dbs_ref[...] = jnp.zeros_like(dbs_ref)
            dgv_ref[...] = jnp.zeros_like(dgv_ref)

        dy_s[...] = _dot(dx_ref[...].astype(BF16), wout_ref[...], NT)
        zu = zpre_ref[:, :d]
        zv = zpre_ref[:, d:]
        cdf_u, pdf_u = _gelu_parts(zu)
        cdf_v, pdf_v = _gelu_parts(zv)
        u_s[...] = zu * cdf_u
        v = zv * cdf_v
        rv = lax.rsqrt(jnp.mean(v * v, axis=-1, keepdims=True) + EPS)
        vhat = v * rv
        gv = gv_ref[...]
        vn_s[...] = (vhat * gv).astype(BF16)
        for ci in range(tm // CHUNK):
            rows = slice(ci * CHUNK, (ci + 1) * CHUNK)
            for g in range(N_GROUPS):
                cols = slice(g * LANES, (g + 1) * LANES)
                vnb = vn_s[rows, cols]
                sv = _dot(wc_ref[g], vnb) + bsb_ref[g]
                dyb = dy_s[rows, cols]
                ub = u_s[rows, cols]
                dsv = dyb * ub
                du_s[rows, cols] = dyb * sv
                y_ref[rows, cols] = (ub * sv).astype(BF16)
                dsvb = dsv.astype(BF16)
                dbs_ref[g] += dsv
                dwc_ref[g] += _dot(dsvb, vnb, NT)
                dvn_s[rows, cols] = _dot(wct_ref[g], dsvb)
        dvn = dvn_s[...]
        dgv_ref[0:1, :] += jnp.sum(dvn * vhat, axis=0, keepdims=True)
        gy = dvn * gv
        dv = rv * (gy - vhat * jnp.mean(gy * vhat, axis=-1, keepdims=True))
        dz_ref[:, :d] = (du_s[...] * (cdf_u + zu * pdf_u)).astype(BF16)
        dz_ref[:, d:] = (dv * (cdf_v + zv * pdf_v)).astype(BF16)

        @pl.when(i == t // tm - 1)
        def _():
            tri = (lax.broadcasted_iota(jnp.int32, (CHUNK, CHUNK), 0)
                   >= lax.broadcasted_iota(jnp.int32, (CHUNK, CHUNK), 1))
            for g in range(N_GROUPS):
                dwc_ref[g] = jnp.where(tri, dwc_ref[g], 0.0)
                dbs_ref[g] = jnp.broadcast_to(jnp.sum(dbs_ref[g], axis=1, keepdims=True), (CHUNK, CHUNK))

    row = pl.BlockSpec((tm, d), lambda i: (i, 0))
    row2 = pl.BlockSpec((tm, 2 * d), lambda i: (i, 0))
    full = lambda a: pl.BlockSpec(a.shape, lambda i: (0,) * a.ndim)
    grp = pl.BlockSpec((N_GROUPS, CHUNK, CHUNK), lambda i: (0, 0, 0))
    return _call(
        body, [dx1, zpre, w_out, g_v, w_c, w_ct, b_sb], grid=(t // tm,),
        in_specs=[row, row2, full(w_out), full(g_v), full(w_c), full(w_ct), full(b_sb)],
        out_specs=[row2, row, grp, grp, pl.BlockSpec((8, d), lambda i: (0, 0))],
        out_shape=[jax.ShapeDtypeStruct((t, 2 * d), BF16), jax.ShapeDtypeStruct((t, d), BF16),
                   jax.ShapeDtypeStruct((N_GROUPS, CHUNK, CHUNK), F32),
                   jax.ShapeDtypeStruct((N_GROUPS, CHUNK, CHUNK), F32), jax.ShapeDtypeStruct((8, d), F32)],
        scratch=[pltpu.VMEM((tm, d), F32), pltpu.VMEM((tm, d), BF16), pltpu.VMEM((tm, d), F32),
                 pltpu.VMEM((tm, d), F32), pltpu.VMEM((tm, d), F32)],
        name="sgu_bwd", sem=("arbitrary",), carry=carry)


def _causal_conv(a_ref, prev_ref, cw, cb, first, tm):
    af = a_ref[...].astype(F32)
    keep = jnp.where(first, 0.0, 1.0)
    pv = prev_ref[...].astype(F32)
    p1 = pv[15:16, :] * keep
    p2 = pv[14:15, :] * keep
    row = lax.broadcasted_iota(jnp.int32, af.shape, 0)
    a1 = jnp.where(row == 0, p1, pltpu.roll(af, 1, 0))
    a2 = jnp.where(row == 0, p2, jnp.where(row == 1, p1, pltpu.roll(af, 2, 0)))
    hu = cw[2:3, :] * af + cw[1:2, :] * a1 + cw[0:1, :] * a2 + cb
    return hu, af, a1, a2


def _ffn_in(hf, w_in, layer, tm=1024, carry=None):
    t, d = hf.shape
    tm = min(tm, t)
    return _mm(
        hf, w_in, pl.BlockSpec((tm, d), lambda s, i, kk: (i, 0)),
        pl.BlockSpec((None, d, FF_SHARD), lambda s, i, kk: (s, 0, 0)),
        pl.BlockSpec((None, tm, FF_SHARD), lambda s, i, kk: (s, i, 0)),
        jax.ShapeDtypeStruct((N_SHARDS, t, FF_SHARD), BF16), (N_SHARDS, t // tm, 1), NN, f"ffn{layer}_in", carry=carry)


def _ffn_conv_specs(tm, gate_of, tile_of):
    def specs(shard_of):
        return [
            pl.BlockSpec((None, tm, FF_SHARD), lambda *g: (shard_of(*g), tile_of(*g), 0)),
            pl.BlockSpec((None, 16, FF_SHARD),
                         lambda *g: (shard_of(*g), jnp.maximum(tile_of(*g) * (tm // 16) - 1, 0), 0)),
            pl.BlockSpec((None, 8, FF_SHARD), lambda *g: (shard_of(*g), 0, 0)),
            pl.BlockSpec((None, 1, FF_SHARD), lambda *g: (shard_of(*g), 0, 0)),
        ]
    return specs(gate_of) + specs(lambda *g: gate_of(*g) + N_SHARDS // 2)


def _ffn_out(a, cw, cb, w_out, x, layer, tm=512, carry=None):
    t, d = x.shape
    nc = N_SHARDS // 2

    def body(ag_ref, pg_ref, cwg_ref, cbg_ref, au_ref, pu_ref, cwu_ref, cbu_ref, wout_ref, x_ref, o_ref, acc_ref):
        i, c = pl.program_id(0), pl.program_id(1)
        hg = _causal_conv(ag_ref, pg_ref, cwg_ref[...], cbg_ref[...], i == 0, tm)[0]
        hu = _causal_conv(au_ref, pu_ref, cwu_ref[...], cbu_ref[...], i == 0, tm)[0]
        act = (hg * _sigmoid(hg) * hu).astype(BF16)
        p = _dot(act, wout_ref[...])

        @pl.when(c == 0)
        def _():
            acc_ref[...] = x_ref[...] + p

        @pl.when(c > 0)
        def _():
            acc_ref[...] += p

        @pl.when(c == nc - 1)
        def _():
            o_ref[...] = acc_ref[...]

    row = pl.BlockSpec((tm, d), lambda i, c: (i, 0))
    return _call(
        body, [a, a, cw, cb, a, a, cw, cb, w_out, x], grid=(t // tm, nc),
        in_specs=_ffn_conv_specs(tm, lambda i, c: c, lambda i, c: i)
        + [pl.BlockSpec((FF_SHARD, d), lambda i, c: (c, 0)), row],
        out_specs=[row], out_shape=[jax.ShapeDtypeStruct((t, d), F32)],
        scratch=[pltpu.VMEM((tm, d), F32)], name=f"ffn{layer}_out", sem=("parallel", "arbitrary"), carry=carry)[0]


def _ffn_bwd_act(a, cw, cb, w_out, dxn, layer, tm=512, carry=None):
    t, d = dxn.shape
    nc = N_SHARDS // 2

    def body(ag_ref, pg_ref, cwg_ref, cbg_ref, au_ref, pu_ref, cwu_ref, cbu_ref, wout_ref, dx_ref,
             dhu_ref, dw_ref, dconv_ref):
        i = pl.program_id(1)

        @pl.when(i == 0)
        def _():
            dw_ref[...] = jnp.zeros_like(dw_ref)
            dconv_ref[...] = jnp.zeros_like(dconv_ref)

        hg, ag0, ag1, ag2 = _causal_conv(ag_ref, pg_ref, cwg_ref[...], cbg_ref[...], i == 0, tm)
        hu, au0, au1, au2 = _causal_conv(au_ref, pu_ref, cwu_ref[...], cbu_ref[...], i == 0, tm)
        sg = _sigmoid(hg)
        sl = hg * sg
        dxb = dx_ref[...].astype(BF16)
        dact = _dot(dxb, wout_ref[...], NT)
        dw_ref[...] += _dot((sl * hu).astype(BF16), dxb, TN)
        d_up = dact * sl
        d_gate = dact * hu * (sg * (1.0 + hg * (1.0 - sg)))
        for j, (dv, taps) in enumerate(((d_gate, (ag2, ag1, ag0)), (d_up, (au2, au1, au0)))):
            dvb = dv.astype(BF16)
            dhu_ref[j] = dvb
            dvr = dvb.astype(F32)
            for k in range(3):
                dconv_ref[j, k:k + 1, :] += jnp.sum(dvr * taps[k], axis=0, keepdims=True)
            dconv_ref[j, 3:4, :] += jnp.sum(dv, axis=0, keepdims=True)

    return _call(
        body, [a, a, cw, cb, a, a, cw, cb, w_out, dxn], grid=(nc, t // tm),
        in_specs=_ffn_conv_specs(tm, lambda c, i: c, lambda c, i: i)
        + [pl.BlockSpec((FF_SHARD, d), lambda c, i: (c, 0)), pl.BlockSpec((tm, d), lambda c, i: (i, 0))],
        out_specs=[pl.BlockSpec((None, 2, tm, FF_SHARD), lambda c, i: (c, 0, i, 0)),
                   pl.BlockSpec((FF_SHARD, d), lambda c, i: (c, 0)),
                   pl.BlockSpec((None, 2, 8, FF_SHARD), lambda c, i: (c, 0, 0, 0))],
        out_shape=[jax.ShapeDtypeStruct((nc, 2, t, FF_SHARD), BF16), jax.ShapeDtypeStruct((D_FF, d), F32),
                   jax.ShapeDtypeStruct((nc, 2, 8, FF_SHARD), F32)],
        name=f"ffn{layer}_bwd_act", sem=("parallel", "arbitrary"), carry=carry)


def _ffn_bwd_in(dhu, cw, w_in, layer, tm=1024, carry=None):
    nc, _, t, _ = dhu.shape
    d = D_MODEL
    tm = min(tm, t)
    last_blk = t // 16 - 1

    def body(dh_ref, nx_ref, cw_ref, win_ref, da_ref, o_ref):
        i, s = pl.program_id(0), pl.program_id(1)
        df = dh_ref[...].astype(F32)
        keep = jnp.where(i == t // tm - 1, 0.0, 1.0)
        nx = nx_ref[...].astype(F32)
        n0 = nx[0:1, :] * keep
        n1 = nx[1:2, :] * keep
        row = lax.broadcasted_iota(jnp.int32, df.shape, 0)
        d1 = jnp.where(row == tm - 1, n0, pltpu.roll(df, tm - 1, 0))
        d2 = jnp.where(row == tm - 1, n1, jnp.where(row == tm - 2, n0, pltpu.roll(df, tm - 2, 0)))
        cw = cw_ref[...]
        da = (cw[2:3, :] * df + cw[1:2, :] * d1 + cw[0:1, :] * d2).astype(BF16)
        da_ref[...] = da
        p = _dot(da, win_ref[...], NT)

        @pl.when(s == 0)
        def _():
            o_ref[...] = p

        @pl.when(s > 0)
        def _():
            o_ref[...] += p

    return _call(
        body, [dhu, dhu, cw, w_in], grid=(t // tm, N_SHARDS),
        in_specs=[pl.BlockSpec((None, None, tm, FF_SHARD), lambda i, s: (s % nc, s // nc, i, 0)),
                  pl.BlockSpec((None, None, 16, FF_SHARD),
                               lambda i, s: (s % nc, s // nc, jnp.minimum((i + 1) * (tm // 16), last_blk), 0)),
                  pl.BlockSpec((None, 8, FF_SHARD), lambda i, s: (s, 0, 0)),
                  pl.BlockSpec((None, d, FF_SHARD), lambda i, s: (s, 0, 0))],
        out_specs=[pl.BlockSpec((None, tm, FF_SHARD), lambda i, s: (s, i, 0)),
                   pl.BlockSpec((tm, d), lambda i, s: (i, 0))],
        out_shape=[jax.ShapeDtypeStruct((N_SHARDS, t, FF_SHARD), BF16), jax.ShapeDtypeStruct((t, d), F32)],
        name=f"ffn{layer}_bwd_in", sem=("parallel", "arbitrary"), carry=carry)


def _ffn_wgrad_in(hf, da, layer, carry=None):
    t, d = hf.shape
    return _mm(
        hf, da, pl.BlockSpec((t, d), lambda s, j, kk: (0, 0)),
        pl.BlockSpec((None, t, FF_SHARD), lambda s, j, kk: (s, 0, 0)),
        pl.BlockSpec((None, d, FF_SHARD), lambda s, j, kk: (s, 0, 0)),
        jax.ShapeDtypeStruct((N_SHARDS, d, FF_SHARD), F32), (N_SHARDS, 1, 1), TN, f"ffn{layer}_wgrad_in",
        carry=carry)


def _attn_masks(n):
    lane = lax.broadcasted_iota(jnp.int32, (CHUNK, LANES), 1)
    lo = lane < HEAD_DIM
    tq = lax.broadcasted_iota(jnp.int32, (CHUNK, 2 * CHUNK), 0)
    jk = lax.broadcasted_iota(jnp.int32, (CHUNK, 2 * CHUNK), 1)
    dist = tq + CHUNK - jk
    mask = (dist >= 0) & (dist < CHUNK) & (jk >= jnp.where(n == 0, CHUNK, 0))
    return lo, mask, dist.astype(F32)


def _half_sum(x, lo):
    s_lo = jnp.sum(jnp.where(lo, x, 0.0), axis=-1, keepdims=True)
    s_hi = jnp.sum(jnp.where(lo, 0.0, x), axis=-1, keepdims=True)
    return jnp.where(lo, s_lo, s_hi)


def _attn_probs(qh, kn, mask, distf, slope, sink):
    s = _dot(qh, kn, NT) * (HEAD_DIM ** -0.5)
    s = jnp.where(mask, s - slope * distf, NEG_BIG)
    m = jnp.maximum(jnp.max(s, axis=-1, keepdims=True), sink)
    e = jnp.exp(s - m)
    den = jnp.sum(e, axis=-1, keepdims=True) + jnp.exp(sink - m)
    return e / den, m, den


def _attn_fwd(qraw, kvd, gq, gk, sinks, carry=None):
    t, d = qraw.shape
    nb = t // CHUNK

    def body(sink_ref, q_ref, cur_ref, prev_ref, gq_ref, gk_ref, o_ref):
        n = pl.program_id(0)
        lo, mask, distf = _attn_masks(n)
        gq_v, gk_v = gq_ref[...], gk_ref[...]
        for kvh in range(N_KV_HEADS):
            ks = slice(kvh * LANES, (kvh + 1) * LANES)
            vs = slice(4 * LANES + kvh * LANES, 4 * LANES + (kvh + 1) * LANES)
            kraw = jnp.concatenate([prev_ref[:, ks], cur_ref[:, ks]], axis=0)
            rk = lax.rsqrt(jnp.mean(kraw * kraw, axis=-1, keepdims=True) + EPS)
            kn = (kraw * rk * gk_v).astype(BF16)
            vv = jnp.concatenate([prev_ref[:, vs], cur_ref[:, vs]], axis=0).astype(BF16)
            for p in range(2):
                jq = 2 * kvh + p
                qp = q_ref[:, jq * LANES:(jq + 1) * LANES]
                r = lax.rsqrt(_half_sum(qp * qp, lo) * (1.0 / HEAD_DIM) + EPS)
                qn = qp * r * gq_v
                acc = None
                for half in range(2):
                    h = 4 * kvh + 2 * p + half
                    sel = lo if half == 0 else jnp.logical_not(lo)
                    qh = jnp.where(sel, qn, 0.0).astype(BF16)
                    pf, _, _ = _attn_probs(qh, kn, mask, distf, SLOPES[h], sink_ref[h])
                    oh = _dot(pf.astype(BF16), vv)
                    acc = oh if half == 0 else jnp.where(lo, acc, oh)
                o_ref[:, jq * LANES:(jq + 1) * LANES] = acc.astype(BF16)

    blk = lambda f: pl.BlockSpec((CHUNK, d), f)
    vec = pl.BlockSpec((1, LANES), lambda n: (0, 0))
    return _call(
        body, [sinks, qraw, kvd, kvd, gq, gk], grid=(nb,),
        in_specs=[pl.BlockSpec(memory_space=pltpu.SMEM), blk(lambda n: (n, 0)), blk(lambda n: (n, 0)),
                  blk(lambda n: (jnp.maximum(n - 1, 0), 0)), vec, vec],
        out_specs=[blk(lambda n: (n, 0))], out_shape=[jax.ShapeDtypeStruct((t, d), BF16)],
        name="attn_fwd", carry=carry)[0]


def _attn_bwd(qraw, kvd, d_o, gq, gk, sinks, carry=None):
    t, d = qraw.shape
    nb = t // CHUNK

    def body(sink_ref, q_ref, cur_ref, prev_ref, do_ref, gq_ref, gk_ref,
             dq_ref, dkv_ref, dsink_ref, dgq_ref, dgk_ref, carry_s, pp_s, cp_s):
        n = pl.program_id(0)

        @pl.when(n == 0)
        def _():
            carry_s[...] = jnp.zeros_like(carry_s)
            dsink_ref[...] = jnp.zeros_like(dsink_ref)
            dgq_ref[...] = jnp.zeros_like(dgq_ref)
            dgk_ref[...] = jnp.zeros_like(dgk_ref)

        @pl.when(n < nb)
        def _():
            lo, mask, distf = _attn_masks(n)
            gq_v, gk_v = gq_ref[...], gk_ref[...]
            for kvh in range(N_KV_HEADS):
                ks = slice(kvh * LANES, (kvh + 1) * LANES)
                vs = slice(4 * LANES + kvh * LANES, 4 * LANES + (kvh + 1) * LANES)
                kraw = jnp.concatenate([prev_ref[:, ks], cur_ref[:, ks]], axis=0)
                rk = lax.rsqrt(jnp.mean(kraw * kraw, axis=-1, keepdims=True) + EPS)
                khat = kraw * rk
                kn = (khat * gk_v).astype(BF16)
                vv = jnp.concatenate([prev_ref[:, vs], cur_ref[:, vs]], axis=0).astype(BF16)
                dkn = jnp.zeros((2 * CHUNK, LANES), F32)
                dvb = jnp.zeros((2 * CHUNK, LANES), F32)
                for p in range(2):
                    jq = 2 * kvh + p
                    cols = slice(jq * LANES, (jq + 1) * LANES)
                    qp = q_ref[:, cols]
                    r = lax.rsqrt(_half_sum(qp * qp, lo) * (1.0 / HEAD_DIM) + EPS)
                    qhat = qp * r
                    qn = qhat * gq_v
                    dop = do_ref[:, cols]
                    dqn = None
                    for half in range(2):
                        h = 4 * kvh + 2 * p + half
                        sel = lo if half == 0 else jnp.logical_not(lo)
                        qh = jnp.where(sel, qn, 0.0).astype(BF16)
                        doh = jnp.where(sel, dop, jnp.zeros_like(dop))
                        sink = sink_ref[h]
                        pf, m, den = _attn_probs(qh, kn, mask, distf, SLOPES[h], sink)
                        dp = _dot(doh, vv, NT)
                        delta = jnp.sum(pf * dp, axis=-1, keepdims=True)
                        p_sink = jnp.exp(sink - m) / den
                        dsink_ref[h:h + 1, :] -= jnp.broadcast_to(
                            jnp.sum(p_sink * delta, axis=0, keepdims=True), (1, LANES))
                        ds = (pf * (dp - delta) * (HEAD_DIM ** -0.5)).astype(BF16)
                        dqh = _dot(ds, kn)
                        dqn = dqh if half == 0 else jnp.where(lo, dqn, dqh)
                        dkn = dkn + _dot(ds, qh, TN)
                        dvb = dvb + _dot(pf.astype(BF16), doh, TN)
                    dgq_ref[0:1, :] += jnp.sum(dqn * qhat, axis=0, keepdims=True)
                    gy = dqn * gq_v
                    mq = _half_sum(gy * qhat, lo) * (1.0 / HEAD_DIM)
                    dq_ref[:, cols] = (r * (gy - qhat * mq)).astype(BF16)
                dgk_ref[0:1, :] += jnp.sum(dkn * khat, axis=0, keepdims=True)
                gyk = dkn * gk_v
                dkraw = rk * (gyk - khat * jnp.mean(gyk * khat, axis=-1, keepdims=True))
                pp_s[:, ks] = dkraw[:CHUNK]
                cp_s[:, ks] = dkraw[CHUNK:]
                pp_s[:, vs] = dvb[:CHUNK]
                cp_s[:, vs] = dvb[CHUNK:]
            dkv_ref[...] = (carry_s[...] + pp_s[...]).astype(BF16)
            carry_s[...] = cp_s[...]

        @pl.when(n == nb)
        def _():
            dkv_ref[...] = carry_s[...].astype(BF16)

    blk = lambda f: pl.BlockSpec((CHUNK, d), f)
    vec = pl.BlockSpec((1, LANES), lambda n: (0, 0))
    cur = lambda n: (jnp.minimum(n, nb - 1), 0)
    prev = lambda n: (jnp.maximum(jnp.minimum(n, nb - 1) - 1, 0), 0)
    small = lambda r: pl.BlockSpec((r, LANES), lambda n: (0, 0))
    return _call(
        body, [sinks, qraw, kvd, kvd, d_o, gq, gk], grid=(nb + 1,),
        in_specs=[pl.BlockSpec(memory_space=pltpu.SMEM), blk(cur), blk(cur), blk(prev), blk(cur), vec, vec],
        out_specs=[blk(cur), blk(lambda n: (jnp.maximum(n - 1, 0), 0)), small(N_Q_HEADS), small(8), small(8)],
        out_shape=[jax.ShapeDtypeStruct((t, d), BF16), jax.ShapeDtypeStruct((t, d), BF16),
                   jax.ShapeDtypeStruct((N_Q_HEADS, LANES), F32), jax.ShapeDtypeStruct((8, LANES), F32),
                   jax.ShapeDtypeStruct((8, LANES), F32)],
        scratch=[pltpu.VMEM((CHUNK, d), F32)] * 3, name="attn_bwd", sem=("arbitrary",), carry=carry)


def _loss_head(y, target, tm=512):
    t, d = y.shape

    def body(y_ref, t_ref, dy_ref, loss_ref, acc_ref):
        i = pl.program_id(0)

        @pl.when(i == 0)
        def _():
            acc_ref[...] = jnp.zeros_like(acc_ref)

        err = y_ref[...] - t_ref[...]
        dy_ref[...] = err * (1.0 / d)
        acc_ref[...] += jnp.sum(err * err, axis=0, keepdims=True)

        @pl.when(i == t // tm - 1)
        def _():
            loss_ref[...] = jnp.broadcast_to(0.5 / d * jnp.sum(acc_ref[...], axis=1, keepdims=True), loss_ref.shape)

    row = pl.BlockSpec((tm, d), lambda i: (i, 0))
    return _call(
        body, [y, target], grid=(t // tm,), in_specs=[row, row],
        out_specs=[row, pl.BlockSpec((8, LANES), lambda i: (0, 0))],
        out_shape=[jax.ShapeDtypeStruct((t, d), F32), jax.ShapeDtypeStruct((8, LANES), F32)],
        scratch=[pltpu.VMEM((1, d), F32)], name="loss_head", sem=("arbitrary",))


def _adamw_math(g, w, m, v):
    m = ADAM_B1 * m + (1.0 - ADAM_B1) * g
    v = ADAM_B2 * v + (1.0 - ADAM_B2) * (g * g)
    m_hat = m / (1.0 - ADAM_B1 ** ADAM_STEP)
    v_hat = v / (1.0 - ADAM_B2 ** ADAM_STEP)
    delta = -ADAM_LR * (m_hat / (jnp.sqrt(v_hat) + ADAM_EPS) + ADAM_WD * w)
    return delta, m, v


def _row_tile(r, cap=128):
    for tr in range(min(r, cap), 0, -1):
        if r % tr == 0 and (tr % 8 == 0 or tr == r):
            return tr
    return r


def _chip_sum(grad, recv, place, name, wire_dtype):
    _, r, c = grad.shape
    tr = _row_tile(r, 256)

    def body(pl_ref, g_ref, a_ref, p_ref):
        p_ref[...] = (g_ref[...] + a_ref[...]).astype(p_ref.dtype)

    return pl.pallas_call(
        body,
        grid_spec=pltpu.PrefetchScalarGridSpec(
            num_scalar_prefetch=1, grid=(4, r // tr),
            in_specs=[pl.BlockSpec((None, None, tr, c), lambda q, i, pr: (q, pr[1], i, 0)),
                      pl.BlockSpec((None, tr, c), lambda q, i, pr: (q, i, 0))],
            out_specs=pl.BlockSpec((None, tr, c), lambda q, i, pr: (q, i, 0))),
        out_shape=jax.ShapeDtypeStruct((4, r, c), wire_dtype), name=name, compiler_params=_params(),
    )(place, grad.reshape(4, 2, r, c), recv)


def _adamw_sharded(grad, recv, others, place, w, m, v, name, layer=None, fill=None):
    r, c = w.shape[-2:]
    tr = _row_tile(r)

    def body(pl_ref, g_ref, a_ref, oth_ref, w_ref, m_ref, v_ref, *rest):
        g_out, d_out, nm_out, nv_out = rest[-4:]
        g = g_ref[...] + a_ref[...]
        for k in range(3):
            g = g + oth_ref[k].astype(F32)
        delta, nm, nv = _adamw_math(g, w_ref[...], m_ref[...], v_ref[...])
        g_out[...] = g
        d_out[...] = delta
        nm_out[...] = nm
        nv_out[...] = nv

    if layer is None:
        row = pl.BlockSpec((tr, c), lambda i, pr: (i, 0))
    else:
        row = pl.BlockSpec((None, tr, c), lambda i, pr: (layer, i, 0))
    n_fill = 0 if fill is None else 4
    in_specs = [pl.BlockSpec((None, None, tr, c), lambda i, pr: (pr[0], pr[1], i, 0)),
                pl.BlockSpec((None, tr, c), lambda i, pr: (pr[0], i, 0)),
                pl.BlockSpec((3, tr, c), lambda i, pr: (0, i, 0)), row, row, row]
    in_specs += [pl.BlockSpec(memory_space=pl.ANY)] * n_fill
    return pl.pallas_call(
        body,
        grid_spec=pltpu.PrefetchScalarGridSpec(
            num_scalar_prefetch=1, grid=(r // tr,), in_specs=in_specs, out_specs=[row] * 4),
        out_shape=[jax.ShapeDtypeStruct(w.shape, F32)] * 4, name=name, compiler_params=_params(),
        input_output_aliases={7 + j: j for j in range(n_fill)},
    )(place, grad.reshape(4, 2, r, c), recv, others, w, m, v, *([] if fill is None else fill))


def _adamw_summed(parts, ws, ms, vs, name):
    n = len(parts)

    def body(*refs):
        p_refs, w_refs, m_refs, v_refs = refs[:n], refs[n:2 * n], refs[2 * n:3 * n], refs[3 * n:4 * n]
        o_refs = refs[4 * n:]
        for i in range(n):
            g = p_refs[i][0]
            for k in range(1, N_SHARDS):
                g = g + p_refs[i][k]
            delta, nm, nv = _adamw_math(g, w_refs[i][...], m_refs[i][...], v_refs[i][...])
            o_refs[4 * i][...] = g
            o_refs[4 * i + 1][...] = delta
            o_refs[4 * i + 2][...] = nm
            o_refs[4 * i + 3][...] = nv

    shapes = [jax.ShapeDtypeStruct(w.shape, F32) for w in ws for _ in range(4)]
    outs = pl.pallas_call(body, out_shape=shapes, name=name, compiler_params=_params())(*parts, *ws, *ms, *vs)
    return [outs[4 * i:4 * i + 4] for i in range(n)]


def _dup_heads(w):
    lead = w.shape[:-1]
    w4 = w.reshape(lead + (N_KV_HEADS, 1, HEAD_DIM))
    return jnp.broadcast_to(w4, lead + (N_KV_HEADS, 2, HEAD_DIM)).reshape(lead + (N_KV_HEADS * LANES,))


def _fold_heads(g):
    lead = g.shape[:-1]
    return g.reshape(lead + (N_KV_HEADS, 2, HEAD_DIM)).sum(axis=-2).reshape(lead + (N_KV_HEADS * HEAD_DIM,))


def kernel(x, a_norm, a_w_in, a_v_norm, a_w_s, a_b_s, a_w_out, f_norm, f_w_in, f_conv_w, f_conv_b, f_w_out, kv_norm, w_kv, k_norm, b_norm, b_w_q, b_q_norm, b_sinks, b_w_o, loss_target, m_a_norm, m_a_w_in, m_a_v_norm, m_a_w_s, m_a_b_s, m_a_w_out, m_f_norm, m_f_w_in, m_f_conv_w, m_f_conv_b, m_f_w_out, m_kv_norm, m_w_kv, m_k_norm, m_b_norm, m_b_w_q, m_b_q_norm, m_b_sinks, m_b_w_o, v_a_norm, v_a_w_in, v_a_v_norm, v_a_w_s, v_a_b_s, v_a_w_out, v_f_norm, v_f_w_in, v_f_conv_w, v_f_conv_b, v_f_w_out, v_kv_norm, v_w_kv, v_k_norm, v_b_norm, v_b_w_q, v_b_q_norm, v_b_sinks, v_b_w_o):
    d = D_MODEL
    xi, yi, ci = _coords()
    place = jnp.stack([2 * xi + yi, ci]).astype(jnp.int32)
    bf = lambda a: a.astype(BF16)
    row = lambda v_: v_.reshape(1, -1)
    x0, target = x[0], loss_target[0]
    t = x0.shape[0]
    res = {}

    red = {}

    def to_sibling(grads, wire=BF16):
        for k, g in grads.items():
            red[k] = dict(grad=g, wire=wire)
        ex = _ToSibling(list(grads.values()))
        ex.names = list(grads)
        return ex

    def to_chips(ex):
        for k, a in zip(ex.names, ex.results):
            red[k]["recv"] = a
            red[k]["psum"] = _chip_sum(red[k]["grad"], a, place, f"chip_sum_{k}", red[k]["wire"])
        nxt = _ToChips([red[k]["psum"] for k in ex.names])
        nxt.names = ex.names
        return nxt

    def landed(ex):
        for k, b in zip(ex.names, ex.results):
            red[k]["others"] = b

    def update(k, w, m, v, layer=None, fill=None):
        r = red[k]
        return _adamw_sharded(r["grad"], r["recv"], r["others"], place, w, m, v,
                              f"adamw_{k}", layer=layer, fill=fill)

    g_a_in, g_a_out, g_a_norm, g_a_v_norm, g_conv = _exchange_alone(
        _Gather([bf(a_w_in[0]), bf(a_w_out[0]), a_norm, a_v_norm, f_conv_w.reshape(6, FF_SHARD)]), "gather_first")
    a_norm_full, a_v_norm_full = g_a_norm.reshape(1, d), g_a_v_norm.reshape(1, d)
    conv_w = lax.reduce_precision(g_conv.reshape(N_SHARDS, 2, 3, FF_SHARD), 8, 7)
    cw = jnp.pad(jnp.transpose(conv_w, (1, 0, 2, 3)), ((0, 0), (0, 0), (0, 5), (0, 0)))
    w_a_in_flat = jnp.transpose(g_a_in, (1, 0, 2)).reshape(d, 2 * d)
    cb = f_conv_b.reshape(2, N_SHARDS, 1, FF_SHARD)
    tri = jnp.tril(jnp.ones((CHUNK, CHUNK), dtype=bool))
    w_causal = jnp.where(tri[None], a_w_s[0], 0.0).astype(BF16)
    w_causal_t = jnp.transpose(w_causal, (0, 2, 1))
    b_sb = jnp.broadcast_to(a_b_s[0][:, :, None], (N_GROUPS, CHUNK, CHUNK))
    w_a_out = g_a_out.reshape(d, d)
    gq = jnp.tile(b_q_norm.reshape(1, HEAD_DIM), (1, 2))
    gk = jnp.tile(k_norm.reshape(1, HEAD_DIM), (1, 2))
    sinks = b_sinks.reshape(N_Q_HEADS)

    (h1,) = _rms_fwd(x0, [a_norm_full], "a_norm_fwd")
    ex = _Gather([bf(f_w_in[0])])
    zpre, x1 = _sgu_fwd(x0, h1, g_a_in, a_v_norm_full, w_causal, b_sb, w_a_out, carry=ex)
    w_in0 = ex.results[0]
    (hf0,) = _rms_fwd(x1, [f_norm[0:1]], "f0_norm_fwd")
    ex = _Gather([bf(f_w_out[0]), bf(w_kv), bf(b_w_q[0]), bf(b_w_o[0])])
    a0 = _ffn_in(hf0, w_in0, 0, carry=ex)
    w_out0 = ex.results[0].reshape(D_FF, d)
    kv_full = ex.results[1].reshape(d, 2 * N_KV_HEADS * HEAD_DIM)
    w_q, w_o = ex.results[2].reshape(d, d), ex.results[3].reshape(d, d)
    half = N_KV_HEADS * HEAD_DIM
    w_kv_dup = jnp.concatenate([_dup_heads(kv_full[:, :half]), _dup_heads(kv_full[:, half:])], axis=1)
    ex = _Gather([bf(f_w_in[1])])
    x2 = _ffn_out(a0, cw[0], cb[0], w_out0, x1, 0, carry=ex)
    w_in1 = ex.results[0]
    hk, hq = _rms_fwd(x2, [row(kv_norm), b_norm], "kvq_norm_fwd")
    kvd = _mm_rows(hk, w_kv_dup, F32, "kv_proj")
    qraw = _mm_rows(hq, w_q, F32, "q_proj")
    ex = _Gather([bf(f_w_out[1])])
    o = _attn_fwd(qraw, kvd, gq, gk, sinks, carry=ex)
    w_out1 = ex.results[0].reshape(D_FF, d)
    x3 = _mm_rows(o, w_o, F32, "o_proj", res=x2)
    (hf1,) = _rms_fwd(x3, [f_norm[1:2]], "f1_norm_fwd")
    a1 = _ffn_in(hf1, w_in1, 1)
    x4 = _ffn_out(a1, cw[1], cb[1], w_out1, x3, 1)
    dy, loss_lanes = _loss_head(x4, target)
    loss = lax.psum(loss_lanes[0, 0], ("x", "y", "c"))

    dhu1, dw_out1, dconv1 = _ffn_bwd_act(a1, cw[1], cb[1], w_out1, dy, 1)
    ex = to_sibling({"f_w_out1": dw_out1.reshape(N_SHARDS, D_FF // N_SHARDS, d)})
    da1, dhf1 = _ffn_bwd_in(dhu1, cw[1], w_in1, 1, carry=ex)
    ex = to_chips(ex)
    dw_in1 = _ffn_wgrad_in(hf1, da1, 1, carry=ex)
    landed(ex)
    ex = to_sibling({"f_w_in1": dw_in1})
    dx3, dgf1 = _rms_bwd(x3, [f_norm[1:2]], [dhf1], dy, "f1_norm_bwd", carry=ex)
    ex = to_chips(ex)
    d_o = _mm_rows(dx3, w_o, BF16, "o_proj_bwd", trans_w=True)
    dw_o = _mm_wgrad(o, dx3, "o_wgrad").reshape(N_SHARDS, d // N_SHARDS, d)
    dq, dkv, dsink, dgq, dgk = _attn_bwd(qraw, kvd, d_o, gq, gk, sinks, carry=ex)
    landed(ex)
    dw_q = _mm_wgrad(hq, dq, "q_wgrad").reshape(N_SHARDS, d // N_SHARDS, d)
    dw_kv_dup = _mm_wgrad(hk, dkv, "kv_wgrad")
    dw_kv = jnp.concatenate(
        [_fold_heads(dw_kv_dup[:, :4 * LANES]), _fold_heads(dw_kv_dup[:, 4 * LANES:])], axis=1
    ).reshape(N_SHARDS, d // N_SHARDS, 2 * N_KV_HEADS * HEAD_DIM)
    ex = to_sibling({"b_w_o": dw_o, "b_w_q": dw_q, "w_kv": dw_kv})
    dhq = _mm_rows(dq, w_q, F32, "q_proj_bwd", trans_w=True, carry=ex)
    dhk = _mm_rows(dkv, w_kv_dup, F32, "kv_proj_bwd", trans_w=True)
    ex = to_chips(ex)
    dx2, dg2 = _rms_bwd(x2, [row(kv_norm), b_norm], [dhk, dhq], dx3, "kvq_norm_bwd")
    dhu0, dw_out0, dconv0 = _ffn_bwd_act(a0, cw[0], cb[0], w_out0, dx2, 0, carry=ex)
    landed(ex)
    ex = to_sibling({"f_w_out0": dw_out0.reshape(N_SHARDS, D_FF // N_SHARDS, d)})
    da0, dhf0 = _ffn_bwd_in(dhu0, cw[0], w_in0, 0, carry=ex)
    ex = to_chips(ex)
    dw_in0 = _ffn_wgrad_in(hf0, da0, 0, carry=ex)
    landed(ex)
    ex = to_sibling({"f_w_in0": dw_in0})
    dx1, dgf0 = _rms_bwd(x1, [f_norm[0:1]], [dhf0], dx2, "f0_norm_bwd", carry=ex)
    ex = to_chips(ex)
    dz, y, dwc, dbs, dgv = _sgu_bwd(dx1, zpre, w_a_out, a_v_norm_full, w_causal, w_causal_t, b_sb, carry=ex)
    landed(ex)
    dw_a_out = _mm_wgrad(y, dx1, "a_out_wgrad").reshape(N_SHARDS, d // N_SHARDS, d)
    nsub = g_a_in.shape[2]
    dw_a_in = _mm(
        h1, dz, pl.BlockSpec((t, d), lambda s, j, kk: (0, 0)), pl.BlockSpec((t, nsub), lambda s, j, kk: (0, s)),
        pl.BlockSpec((None, d, nsub), lambda s, j, kk: (s, 0, 0)), jax.ShapeDtypeStruct((N_SHARDS, d, nsub), F32),
        (N_SHARDS, 1, 1), TN, "a_in_wgrad")

    def conv_grads(dconv):
        return jnp.transpose(dconv, (1, 0, 2, 3)).reshape(N_SHARDS, 8, FF_SHARD)

    dconv0, dconv1 = conv_grads(dconv0), conv_grads(dconv1)
    g_conv_w = jnp.concatenate([dconv0[:, 0:3, :], dconv1[:, 0:3, :]], axis=1)
    g_a_v_norm = dgv[0].reshape(N_SHARDS, 1, LANES)
    rep = ["a_w_s", "a_b_s", "f_norm", "f_conv_b", "kv_norm", "k_norm", "b_norm", "b_q_norm", "b_sinks"]
    rep_g = dict(
        a_w_s=dwc.reshape(N_GROUPS * CHUNK, CHUNK), a_b_s=dbs[:, :, 0], f_norm=jnp.stack([dgf0[0], dgf1[0]]),
        f_conv_b=jnp.stack([dconv0[:, 3, :].reshape(-1), dconv1[:, 3, :].reshape(-1)]), kv_norm=dg2[0:1],
        k_norm=(dgk[0, :HEAD_DIM] + dgk[0, HEAD_DIM:])[None], b_norm=dg2[1:2],
        b_q_norm=(dgq[0, :HEAD_DIM] + dgq[0, HEAD_DIM:])[None], b_sinks=dsink[:, 0][None])
    ex_big = to_sibling({"a_w_out": dw_a_out, "a_w_in": dw_a_in})
    ex_small = to_sibling({"a_v_norm": g_a_v_norm, "f_conv_w": g_conv_w}, wire=F32)
    ex_rep = _Gather([rep_g[k] for k in rep])
    together = _Together([ex_big, ex_small, ex_rep])
    dh1 = _mm_rows(dz, w_a_in_flat, F32, "a_in_bwd", trans_w=True, carry=together)
    together.spread()
    ex_big, ex_small = to_chips(ex_big), to_chips(ex_small)
    together = _Together([ex_big, ex_small])
    grad_x, dg0 = _rms_bwd(x0, [a_norm_full], [dh1], dx1, "a_norm_bwd", carry=together)
    together.spread()
    landed(ex_big)
    landed(ex_small)
    (a_norm_parts,) = _exchange_alone(_ToOwners([dg0[0].reshape(N_SHARDS, 1, LANES)]), "a_norm_to_owners")

    res["f_w_out"] = update("f_w_out1", f_w_out, m_f_w_out, v_f_w_out, layer=1)
    res["f_w_in"] = update("f_w_in1", f_w_in, m_f_w_in, v_f_w_in, layer=1)
    res["b_w_o"] = update("b_w_o", b_w_o, m_b_w_o, v_b_w_o, layer=0)
    res["b_w_q"] = update("b_w_q", b_w_q, m_b_w_q, v_b_w_q, layer=0)
    res["w_kv"] = update("w_kv", w_kv, m_w_kv, v_w_kv)
    res["f_w_out"] = update("f_w_out0", f_w_out, m_f_w_out, v_f_w_out, layer=0, fill=res["f_w_out"])
    res["f_w_in"] = update("f_w_in0", f_w_in, m_f_w_in, v_f_w_in, layer=0, fill=res["f_w_in"])
    res["a_w_out"] = update("a_w_out", a_w_out, m_a_w_out, v_a_w_out, layer=0)
    res["a_w_in"] = update("a_w_in", a_w_in, m_a_w_in, v_a_w_in, layer=0)
    res["a_v_norm"] = update("a_v_norm", a_v_norm, m_a_v_norm, v_a_v_norm)
    res["f_conv_w"] = [o_.reshape(f_conv_w.shape) for o_ in update(
        "f_conv_w", f_conv_w.reshape(6, FF_SHARD), m_f_conv_w.reshape(6, FF_SHARD), v_f_conv_w.reshape(6, FF_SHARD))]

    rep_w = dict(a_w_s=a_w_s, a_b_s=a_b_s, f_norm=f_norm, f_conv_b=f_conv_b, kv_norm=kv_norm, k_norm=k_norm,
                 b_norm=b_norm, b_q_norm=b_q_norm, b_sinks=b_sinks, a_norm=a_norm)
    rep_m = dict(a_w_s=m_a_w_s, a_b_s=m_a_b_s, f_norm=m_f_norm, f_conv_b=m_f_conv_b, kv_norm=m_kv_norm,
                 k_norm=m_k_norm, b_norm=m_b_norm, b_q_norm=m_b_q_norm, b_sinks=m_b_sinks, a_norm=m_a_norm)
    rep_v = dict(a_w_s=v_a_w_s, a_b_s=v_a_b_s, f_norm=v_f_norm, f_conv_b=v_f_conv_b, kv_norm=v_kv_norm,
                 k_norm=v_k_norm, b_norm=v_b_norm, b_q_norm=v_b_q_norm, b_sinks=v_b_sinks, a_norm=v_a_norm)
    keys = rep + ["a_norm"]
    parts = ex_rep.results + [a_norm_parts]
    as2d = lambda a, p: a.reshape(p.shape[1:])
    rep_outs = _adamw_summed(parts, [as2d(rep_w[k], p) for k, p in zip(keys, parts)],
                             [as2d(rep_m[k], p) for k, p in zip(keys, parts)],
                             [as2d(rep_v[k], p) for k, p in zip(keys, parts)], "adamw_replicated")
    for j, key in enumerate(keys):
        res[key] = [o_.reshape(rep_w[key].shape) for o_ in rep_outs[j]]

    order = ["a_norm", "a_w_in", "a_v_norm", "a_w_s", "a_b_s", "a_w_out", "f_norm", "f_w_in", "f_conv_w", "f_conv_b",
             "f_w_out", "kv_norm", "w_kv", "k_norm", "b_norm", "b_w_q", "b_q_norm", "b_sinks", "b_w_o"]
    outs = [loss, grad_x[None]]
    for j in range(4):
        outs += [res[k][j] for k in order]
    return tuple(outs)
```

```python
import jax
import jax.numpy as jnp
from jax import lax
from jax.experimental import pallas as pl
from jax.experimental.pallas import tpu as pltpu

F32 = jnp.float32
BF16 = jnp.bfloat16
EPS = 1e-6
D_MODEL = 1024
CHUNK = 128
N_GROUPS = 8
N_SHARDS = 8
HEAD_DIM = 64
N_Q_HEADS = 16
N_KV_HEADS = 4
D_FF = 2816
FF_SHARD = 2 * D_FF // N_SHARDS
LANES = 128
NEG_BIG = -1e30
ADAM_LR = 0.001
ADAM_B1 = 0.9
ADAM_B2 = 0.999
ADAM_EPS = 1e-08
ADAM_WD = 0.01
ADAM_STEP = 10
VMEM_LIMIT_BYTES = 56 * 1024 * 1024
MESH = pl.DeviceIdType.MESH

NN = (((1,), (0,)), ((), ()))
NT = (((1,), (1,)), ((), ()))
TN = (((0,), (0,)), ((), ()))
SLOPES = tuple(2.0 ** (-8.0 * (h + 1) / N_Q_HEADS) for h in range(N_Q_HEADS))


def _params(sem=None):
    return pltpu.CompilerParams(dimension_semantics=sem, vmem_limit_bytes=VMEM_LIMIT_BYTES)


def _dot(a, b, dims=NN):
    return lax.dot_general(a, b, dims, preferred_element_type=F32)


def _sigmoid(x):
    return 1.0 / (1.0 + jnp.exp(-x))


def _gelu_parts(z):
    cdf = 0.5 * (1.0 + lax.erf(z * (2.0 ** -0.5)))
    pdf = jnp.exp(-0.5 * z * z) * 0.3989422804014327
    return cdf, pdf


def _coords():
    return lax.axis_index("x"), lax.axis_index("y"), lax.axis_index("c")


class _Gather:
    def __init__(self, srcs):
        self.srcs = list(srcs)
        n = len(self.srcs)
        self.out_shapes = [jax.ShapeDtypeStruct((N_SHARDS,) + s.shape, s.dtype) for s in self.srcs]
        self.sems = [pltpu.SemaphoreType.DMA((n, 7)), pltpu.SemaphoreType.DMA((n, 7)), pltpu.SemaphoreType.DMA((n,))]

    def _plan(self, src, dst, sems):
        send_sems, recv_sems, local_sems = sems
        x, y, c = _coords()
        me, sibling = (x, y, c), (x, y, 1 - c)
        chips = [(1 - x, y), (x, 1 - y), (1 - x, 1 - y)]
        n = len(src)

        def rows(e, dev):
            return dst[e].at[4 * dev[0] + 2 * dev[1] + dev[2]]

        def copy(e, slot, block, to, from_own=False):
            return pltpu.make_async_remote_copy(
                src_ref=src[e] if from_own else rows(e, block), dst_ref=rows(e, block),
                send_sem=send_sems.at[e, slot], recv_sem=recv_sems.at[e, slot], device_id=to, device_id_type=MESH)

        mine = [pltpu.make_async_copy(src[e], rows(e, me), local_sems.at[e]) for e in range(n)]
        first = []
        for e in range(n):
            first.append(copy(e, 0, me, sibling, from_own=True))
            first += [copy(e, 1 + j, me, (*chip, c), from_own=True) for j, chip in enumerate(chips)]
        return n, me, sibling, chips, c, copy, mine, first

    def start(self, src, dst, sems):
        _, _, _, _, _, _, mine, first = self._plan(src, dst, sems)
        for cp in mine + first:
            cp.start()

    def finish(self, src, dst, sems):
        n, me, sibling, chips, c, copy, mine, first = self._plan(src, dst, sems)
        passed = []
        for j, chip in enumerate(chips):
            for e in range(n):
                copy(e, 1 + j, (*chip, c), me).wait_recv()
                cp = copy(e, 4 + j, (*chip, c), sibling)
                cp.start()
                passed.append(cp)
        for e in range(n):
            copy(e, 0, sibling, me).wait_recv()
            for j, chip in enumerate(chips):
                copy(e, 4 + j, (*chip, 1 - c), me).wait_recv()
        for cp in first + passed:
            cp.wait_send()
        for cp in mine:
            cp.wait()


class _ToSibling:
    def __init__(self, grads):
        self.srcs = list(grads)
        n = len(self.srcs)
        self.out_shapes = [jax.ShapeDtypeStruct((4,) + g.shape[1:], g.dtype) for g in self.srcs]
        self.sems = [pltpu.SemaphoreType.DMA((n, 4)), pltpu.SemaphoreType.DMA((n, 4))]

    def _copies(self, src, dst, sems):
        send_sems, recv_sems = sems
        x, y, c = _coords()
        return [
            pltpu.make_async_remote_copy(
                src_ref=src[i].at[2 * q + (1 - c)], dst_ref=dst[i].at[q], send_sem=send_sems.at[i, q],
                recv_sem=recv_sems.at[i, q], device_id=(x, y, 1 - c), device_id_type=MESH)
            for i in range(len(src)) for q in range(4)]

    def start(self, src, dst, sems):
        for cp in self._copies(src, dst, sems):
            cp.start()

    def finish(self, src, dst, sems):
        for cp in self._copies(src, dst, sems):
            cp.wait()


class _ToChips:
    def __init__(self, psums, rows=None):
        self.srcs = list(psums)
        n = len(self.srcs)
        self.rows = rows
        self.out_shapes = [
            jax.ShapeDtypeStruct((3, p.shape[1] if rows is None else rows[1]) + p.shape[2:], p.dtype)
            for p in self.srcs]
        self.sems = [pltpu.SemaphoreType.DMA((n, 3)), pltpu.SemaphoreType.DMA((n, 3))]

    def _copies(self, src, dst, sems):
        send_sems, recv_sems = sems
        x, y, c = _coords()
        peers = [(x, 1 - y), (1 - x, y), (1 - x, 1 - y)]

        def part(i, q):
            if self.rows is None:
                return src[i].at[q]
            return src[i].at[q, pl.ds(self.rows[0], self.rows[1])]

        return [
            pltpu.make_async_remote_copy(
                src_ref=part(i, 2 * px + py), dst_ref=dst[i].at[r], send_sem=send_sems.at[i, r],
                recv_sem=recv_sems.at[i, r], device_id=(px, py, c), device_id_type=MESH)
            for i in range(len(src)) for r, (px, py) in enumerate(peers)]

    def start(self, src, dst, sems):
        for cp in self._copies(src, dst, sems):
            cp.start()

    def finish(self, src, dst, sems):
        for cp in self._copies(src, dst, sems):
            cp.wait()


class _ToOwners:
    def __init__(self, grads):
        self.srcs = list(grads)
        n = len(self.srcs)
        self.out_shapes = [jax.ShapeDtypeStruct(g.shape, g.dtype) for g in self.srcs]
        self.sems = [pltpu.SemaphoreType.DMA((n, 7)), pltpu.SemaphoreType.DMA((n, 7)), pltpu.SemaphoreType.DMA((n,))]

    def _copies(self, src, dst, sems):
        send_sems, recv_sems, local_sems = sems
        x, y, c = _coords()
        me = 4 * x + 2 * y + c
        copies = [pltpu.make_async_copy(src[i].at[me], dst[i].at[me], local_sems.at[i]) for i in range(len(src))]
        for i in range(len(src)):
            for rel in range(1, N_SHARDS):
                px = x ^ (rel >> 2) if rel >> 2 else x
                py = y ^ ((rel >> 1) & 1) if (rel >> 1) & 1 else y
                pc = c ^ (rel & 1) if rel & 1 else c
                copies.append(pltpu.make_async_remote_copy(
                    src_ref=src[i].at[4 * px + 2 * py + pc], dst_ref=dst[i].at[me], send_sem=send_sems.at[i, rel - 1],
                    recv_sem=recv_sems.at[i, rel - 1], device_id=(px, py, pc), device_id_type=MESH))
        return copies

    def start(self, src, dst, sems):
        for cp in self._copies(src, dst, sems):
            cp.start()

    def finish(self, src, dst, sems):
        for cp in self._copies(src, dst, sems):
            cp.wait()


class _Together:
    def __init__(self, parts):
        self.parts = list(parts)
        self.srcs = [s for p in self.parts for s in p.srcs]
        self.out_shapes = [s for p in self.parts for s in p.out_shapes]
        self.sems = [s for p in self.parts for s in p.sems]

    def _split(self, src, dst, sems):
        a = b = c = 0
        for p in self.parts:
            na, nc = len(p.srcs), len(p.sems)
            yield p, src[a:a + na], dst[b:b + na], sems[c:c + nc]
            a, b, c = a + na, b + na, c + nc

    def start(self, src, dst, sems):
        for p, s, d, m in self._split(src, dst, sems):
            p.start(s, d, m)

    def finish(self, src, dst, sems):
        for p, s, d, m in self._split(src, dst, sems):
            p.finish(s, d, m)

    def spread(self):
        b = 0
        for p in self.parts:
            p.results = self.results[b:b + len(p.srcs)]
            b += len(p.srcs)


def _call(body, args, *, grid, in_specs, out_specs, out_shape, name, scratch=(), sem=None, carry=None):
    out_shape, out_specs = list(out_shape), list(out_specs)
    if carry is None:
        return pl.pallas_call(
            body, grid=grid, in_specs=list(in_specs), out_specs=out_specs, out_shape=out_shape,
            scratch_shapes=list(scratch), name=name, compiler_params=_params(sem))(*args)
    n_in, n_out, n_scr, n_c = len(args), len(out_shape), len(scratch), len(carry.srcs)
    steps = tuple(grid)

    def carried(*refs):
        ins, rest = refs[:n_in], refs[n_in:]
        c_src, rest = rest[:n_c], rest[n_c:]
        outs, rest = rest[:n_out], rest[n_out:]
        c_dst, rest = rest[:n_c], rest[n_c:]
        scr, sems = rest[:n_scr], rest[n_scr:]
        first = pl.program_id(0) == 0
        last = pl.program_id(0) == steps[0] - 1
        for ax in range(1, len(steps)):
            first = first & (pl.program_id(ax) == 0)
            last = last & (pl.program_id(ax) == steps[ax] - 1)

        @pl.when(first)
        def _():
            carry.start(c_src, c_dst, sems)

        body(*ins, *outs, *scr)

        @pl.when(last)
        def _():
            carry.finish(c_src, c_dst, sems)

    hbm = pl.BlockSpec(memory_space=pl.ANY)
    res = pl.pallas_call(
        carried, grid=grid, in_specs=list(in_specs) + [hbm] * n_c, out_specs=out_specs + [hbm] * n_c,
        out_shape=out_shape + carry.out_shapes, scratch_shapes=list(scratch) + carry.sems, name=name,
        compiler_params=_params(("arbitrary",) * len(steps)))(*args, *carry.srcs)
    carry.results = list(res[n_out:])
    return list(res[:n_out])


def _exchange_alone(ex, name):
    n = len(ex.srcs)

    def body(*refs):
        src, dst, sems = refs[:n], refs[n:2 * n], refs[2 * n:]
        ex.start(src, dst, sems)
        ex.finish(src, dst, sems)

    hbm = pl.BlockSpec(memory_space=pl.ANY)
    res = pl.pallas_call(body, in_specs=[hbm] * n, out_specs=[hbm] * n, out_shape=ex.out_shapes,
                         scratch_shapes=ex.sems, name=name)(*ex.srcs)
    ex.results = list(res)
    return ex.results


def _rms_fwd(x, gains, name, tm=512, carry=None):
    t, d = x.shape
    n = len(gains)

    def body(*refs):
        x_ref, g_refs, h_refs = refs[0], refs[1:1 + n], refs[1 + n:]
        xf = x_ref[...]
        xhat = xf * lax.rsqrt(jnp.mean(xf * xf, axis=-1, keepdims=True) + EPS)
        for g_ref, h_ref in zip(g_refs, h_refs):
            h_ref[...] = (xhat * g_ref[...]).astype(BF16)

    row = pl.BlockSpec((tm, d), lambda i: (i, 0))
    vec = pl.BlockSpec((1, d), lambda i: (0, 0))
    return _call(body, [x, *gains], grid=(t // tm,), in_specs=[row] + [vec] * n, out_specs=[row] * n,
                 out_shape=[jax.ShapeDtypeStruct((t, d), BF16)] * n, name=name, carry=carry)


def _rms_bwd(x, gains, dhs, dres, name, tm=256, carry=None):
    t, d = x.shape
    n = len(gains)

    def body(*refs):
        x_ref, dres_ref = refs[0], refs[1]
        g_refs, dh_refs = refs[2:2 + n], refs[2 + n:2 + 2 * n]
        dx_ref, dg_ref = refs[2 + 2 * n], refs[3 + 2 * n]
        i = pl.program_id(0)

        @pl.when(i == 0)
        def _():
            dg_ref[...] = jnp.zeros_like(dg_ref)

        xf = x_ref[...]
        r = lax.rsqrt(jnp.mean(xf * xf, axis=-1, keepdims=True) + EPS)
        xhat = xf * r
        dx = dres_ref[...]
        for j in range(n):
            dh = dh_refs[j][...]
            dg_ref[j:j + 1, :] += jnp.sum(dh * xhat, axis=0, keepdims=True)
            gy = dh * g_refs[j][...]
            dx = dx + r * (gy - xhat * jnp.mean(gy * xhat, axis=-1, keepdims=True))
        dx_ref[...] = dx

    row = pl.BlockSpec((tm, d), lambda i: (i, 0))
    vec = pl.BlockSpec((1, d), lambda i: (0, 0))
    return _call(body, [x, dres, *gains, *dhs], grid=(t // tm,), in_specs=[row, row] + [vec] * n + [row] * n,
                 out_specs=[row, pl.BlockSpec((8, d), lambda i: (0, 0))],
                 out_shape=[jax.ShapeDtypeStruct((t, d), F32), jax.ShapeDtypeStruct((8, d), F32)],
                 name=name, sem=("arbitrary",), carry=carry)


def _mm(a, b, a_spec, b_spec, o_spec, out_shape, grid, dims, name, res=None, res_spec=None, carry=None):
    nk = grid[2]
    acc_shape = tuple(s for s in o_spec.block_shape if s is not None)

    def body(*refs):
        a_ref, b_ref = refs[0], refs[1]
        r_ref = refs[2] if res is not None else None
        o_ref = refs[3] if res is not None else refs[2]
        p = _dot(a_ref[...].astype(BF16), b_ref[...].astype(BF16), dims)
        if nk == 1:
            if res is not None:
                p = p + r_ref[...]
            o_ref[...] = p.astype(o_ref.dtype)
            return
        acc_ref = refs[-1]
        k = pl.program_id(2)

        @pl.when(k == 0)
        def _():
            acc_ref[...] = p

        @pl.when(k > 0)
        def _():
            acc_ref[...] += p

        @pl.when(k == nk - 1)
        def _():
            out = acc_ref[...]
            if res is not None:
                out = out + r_ref[...]
            o_ref[...] = out.astype(o_ref.dtype)

    ins = [a, b] + ([res] if res is not None else [])
    specs = [a_spec, b_spec] + ([res_spec] if res is not None else [])
    return _call(body, ins, grid=grid, in_specs=specs, out_specs=[o_spec], out_shape=[out_shape],
                 scratch=[pltpu.VMEM(acc_shape, F32)] if nk > 1 else [], name=name,
                 sem=("parallel", "parallel", "arbitrary"), carry=carry)[0]


def _mm_rows(a, w, out_dtype, name, trans_w=False, res=None, tm=512, carry=None):
    t, k = a.shape
    n = w.shape[0] if trans_w else w.shape[1]
    return _mm(
        a, w, pl.BlockSpec((tm, k), lambda i, j, kk: (i, 0)), pl.BlockSpec(w.shape, lambda i, j, kk: (0, 0)),
        pl.BlockSpec((tm, n), lambda i, j, kk: (i, 0)), jax.ShapeDtypeStruct((t, n), out_dtype), (t // tm, 1, 1),
        NT if trans_w else NN, name, res=res,
        res_spec=None if res is None else pl.BlockSpec((tm, n), lambda i, j, kk: (i, 0)), carry=carry)


def _mm_wgrad(a, b, name, carry=None):
    t, m = a.shape
    n = b.shape[1]
    tn = n // (4 if b.dtype == F32 else 2)
    return _mm(
        a, b, pl.BlockSpec((t, m), lambda i, j, kk: (0, 0)), pl.BlockSpec((t, tn), lambda i, j, kk: (0, j)),
        pl.BlockSpec((m, tn), lambda i, j, kk: (0, j)), jax.ShapeDtypeStruct((m, n), F32), (1, n // tn, 1), TN, name,
        carry=carry)


def _sgu_fwd(x0, h1, w_in, g_v, w_c, b_sb, w_out, tm=256, carry=None):
    t, d = x0.shape
    nsub = w_in.shape[2]

    def body(x_ref, h_ref, win_ref, gv_ref, wc_ref, bsb_ref, wout_ref, zpre_ref, x1_ref, u_s, v_s, vn_s, y_s):
        h = h_ref[...]
        for k in range(N_SHARDS):
            zk = _dot(h, win_ref[k])
            zpre_ref[:, k * nsub:(k + 1) * nsub] = zk
            cdf, _ = _gelu_parts(zk)
            if k < N_SHARDS // 2:
                u_s[:, k * nsub:(k + 1) * nsub] = zk * cdf
            else:
                v_s[:, (k - 4) * nsub:(k - 3) * nsub] = zk * cdf
        v = v_s[...]
        rv = lax.rsqrt(jnp.mean(v * v, axis=-1, keepdims=True) + EPS)
        vn_s[...] = (v * rv * gv_ref[...]).astype(BF16)
        for ci in range(tm // CHUNK):
            rows = slice(ci * CHUNK, (ci + 1) * CHUNK)
            for g in range(N_GROUPS):
                cols = slice(g * LANES, (g + 1) * LANES)
                sv = _dot(wc_ref[g], vn_s[rows, cols]) + bsb_ref[g]
                y_s[rows, cols] = (u_s[rows, cols] * sv).astype(BF16)
        x1_ref[...] = x_ref[...] + _dot(y_s[...], wout_ref[...])

    row = pl.BlockSpec((tm, d), lambda i: (i, 0))
    full = lambda a: pl.BlockSpec(a.shape, lambda i: (0,) * a.ndim)
    return _call(
        body, [x0, h1, w_in, g_v, w_c, b_sb, w_out], grid=(t // tm,),
        in_specs=[row, row, full(w_in), full(g_v), full(w_c), full(b_sb), full(w_out)],
        out_specs=[pl.BlockSpec((tm, 2 * d), lambda i: (i, 0)), row],
        out_shape=[jax.ShapeDtypeStruct((t, 2 * d), F32), jax.ShapeDtypeStruct((t, d), F32)],
        scratch=[pltpu.VMEM((tm, d), F32), pltpu.VMEM((tm, d), F32), pltpu.VMEM((tm, d), BF16),
                 pltpu.VMEM((tm, d), BF16)],
        name="sgu_fwd", carry=carry)


def _sgu_bwd(dx1, zpre, w_out, g_v, w_c, w_ct, b_sb, tm=256, carry=None):
    t, d = dx1.shape

    def body(dx_ref, zpre_ref, wout_ref, gv_ref, wc_ref, wct_ref, bsb_ref,
             dz_ref, y_ref, dwc_ref, dbs_ref, dgv_ref, u_s, vn_s, dy_s, du_s, dvn_s):
        i = pl.program_id(0)

        @pl.when(i == 0)
        def _():
            dwc_ref[...] = jnp.zeros_like(dwc_ref)
            dbs_ref[...] = jnp.zeros_like(dbs_ref)
            dgv_ref[...] = jnp.zeros_like(dgv_ref)

        dy_s[...] = _dot(dx_ref[...].astype(BF16), wout_ref[...], NT)
        zu = zpre_ref[:, :d]
        zv = zpre_ref[:, d:]
        cdf_u, pdf_u = _gelu_parts(zu)
        cdf_v, pdf_v = _gelu_parts(zv)
        u_s[...] = zu * cdf_u
        v = zv * cdf_v
        rv = lax.rsqrt(jnp.mean(v * v, axis=-1, keepdims=True) + EPS)
        vhat = v * rv
        gv = gv_ref[...]
        vn_s[...] = (vhat * gv).astype(BF16)
        for ci in range(tm // CHUNK):
            rows = slice(ci * CHUNK, (ci + 1) * CHUNK)
            for g in range(N_GROUPS):
                cols = slice(g * LANES, (g + 1) * LANES)
                vnb = vn_s[rows, cols]
                sv = _dot(wc_ref[g], vnb) + bsb_ref[g]
                dyb = dy_s[rows, cols]
                ub = u_s[rows, cols]
                dsv = dyb * ub
                du_s[rows, cols] = dyb * sv
                y_ref[rows, cols] = (ub * sv).astype(BF16)
                dsvb = dsv.astype(BF16)
                dbs_ref[g] += dsv
                dwc_ref[g] += _dot(dsvb, vnb, NT)
                dvn_s[rows, cols] = _dot(wct_ref[g], dsvb)
        dvn = dvn_s[...]
        dgv_ref[0:1, :] += jnp.sum(dvn * vhat, axis=0, keepdims=True)
        gy = dvn * gv
        dv = rv * (gy - vhat * jnp.mean(gy * vhat, axis=-1, keepdims=True))
        dz_ref[:, :d] = (du_s[...] * (cdf_u + zu * pdf_u)).astype(BF16)
        dz_ref[:, d:] = (dv * (cdf_v + zv * pdf_v)).astype(BF16)

        @pl.when(i == t // tm - 1)
        def _():
            tri = (lax.broadcasted_iota(jnp.int32, (CHUNK, CHUNK), 0)
                   >= lax.broadcasted_iota(jnp.int32, (CHUNK, CHUNK), 1))
            for g in range(N_GROUPS):
                dwc_ref[g] = jnp.where(tri, dwc_ref[g], 0.0)
                dbs_ref[g] = jnp.broadcast_to(jnp.sum(dbs_ref[g], axis=1, keepdims=True), (CHUNK, CHUNK))

    row = pl.BlockSpec((tm, d), lambda i: (i, 0))
    row2 = pl.BlockSpec((tm, 2 * d), lambda i: (i, 0))
    full = lambda a: pl.BlockSpec(a.shape, lambda i: (0,) * a.ndim)
    grp = pl.BlockSpec((N_GROUPS, CHUNK, CHUNK), lambda i: (0, 0, 0))
    return _call(
        body, [dx1, zpre, w_out, g_v, w_c, w_ct, b_sb], grid=(t // tm,),
        in_specs=[row, row2, full(w_out), full(g_v), full(w_c), full(w_ct), full(b_sb)],
        out_specs=[row2, row, grp, grp, pl.BlockSpec((8, d), lambda i: (0, 0))],
        out_shape=[jax.ShapeDtypeStruct((t, 2 * d), BF16), jax.ShapeDtypeStruct((t, d), BF16),
                   jax.ShapeDtypeStruct((N_GROUPS, CHUNK, CHUNK), F32),
                   jax.ShapeDtypeStruct((N_GROUPS, CHUNK, CHUNK), F32), jax.ShapeDtypeStruct((8, d), F32)],
        scratch=[pltpu.VMEM((tm, d), F32), pltpu.VMEM((tm, d), BF16), pltpu.VMEM((tm, d), F32),
                 pltpu.VMEM((tm, d), F32), pltpu.VMEM((tm, d), F32)],
        name="sgu_bwd", sem=("arbitrary",), carry=carry)


def _causal_conv(a_ref, prev_ref, cw, cb, first, tm):
    af = a_ref[...].astype(F32)
    keep = jnp.where(first, 0.0, 1.0)
    pv = prev_ref[...].astype(F32)
    p1 = pv[15:16, :] * keep
    p2 = pv[14:15, :] * keep
    row = lax.broadcasted_iota(jnp.int32, af.shape, 0)
    a1 = jnp.where(row == 0, p1, pltpu.roll(af, 1, 0))
    a2 = jnp.where(row == 0, p2, jnp.where(row == 1, p1, pltpu.roll(af, 2, 0)))
    hu = cw[2:3, :] * af + cw[1:2, :] * a1 + cw[0:1, :] * a2 + cb
    return hu, af, a1, a2


def _ffn_in(hf, w_in, layer, tm=1024, carry=None):
    t, d = hf.shape
    tm = min(tm, t)
    return _mm(
        hf, w_in, pl.BlockSpec((tm, d), lambda s, i, kk: (i, 0)),
        pl.BlockSpec((None, d, FF_SHARD), lambda s, i, kk: (s, 0, 0)),
        pl.BlockSpec((None, tm, FF_SHARD), lambda s, i, kk: (s, i, 0)),
        jax.ShapeDtypeStruct((N_SHARDS, t, FF_SHARD), BF16), (N_SHARDS, t // tm, 1), NN, f"ffn{layer}_in", carry=carry)


def _ffn_conv_specs(tm, gate_of, tile_of):
    def specs(shard_of):
        return [
            pl.BlockSpec((None, tm, FF_SHARD), lambda *g: (shard_of(*g), tile_of(*g), 0)),
            pl.BlockSpec((None, 16, FF_SHARD),
                         lambda *g: (shard_of(*g), jnp.maximum(tile_of(*g) * (tm // 16) - 1, 0), 0)),
            pl.BlockSpec((None, 8, FF_SHARD), lambda *g: (shard_of(*g), 0, 0)),
            pl.BlockSpec((None, 1, FF_SHARD), lambda *g: (shard_of(*g), 0, 0)),
        ]
    return specs(gate_of) + specs(lambda *g: gate_of(*g) + N_SHARDS // 2)


def _ffn_out(a, cw, cb, w_out, x, layer, tm=512, carry=None):
    t, d = x.shape
    nc = N_SHARDS // 2

    def body(ag_ref, pg_ref, cwg_ref, cbg_ref, au_ref, pu_ref, cwu_ref, cbu_ref, wout_ref, x_ref, o_ref, acc_ref):
        i, c = pl.program_id(0), pl.program_id(1)
        hg = _causal_conv(ag_ref, pg_ref, cwg_ref[...], cbg_ref[...], i == 0, tm)[0]
        hu = _causal_conv(au_ref, pu_ref, cwu_ref[...], cbu_ref[...], i == 0, tm)[0]
        act = (hg * _sigmoid(hg) * hu).astype(BF16)
        p = _dot(act, wout_ref[...])

        @pl.when(c == 0)
        def _():
            acc_ref[...] = x_ref[...] + p

        @pl.when(c > 0)
        def _():
            acc_ref[...] += p

        @pl.when(c == nc - 1)
        def _():
            o_ref[...] = acc_ref[...]

    row = pl.BlockSpec((tm, d), lambda i, c: (i, 0))
    return _call(
        body, [a, a, cw, cb, a, a, cw, cb, w_out, x], grid=(t // tm, nc),
        in_specs=_ffn_conv_specs(tm, lambda i, c: c, lambda i, c: i)
        + [pl.BlockSpec((FF_SHARD, d), lambda i, c: (c, 0)), row],
        out_specs=[row], out_shape=[jax.ShapeDtypeStruct((t, d), F32)],
        scratch=[pltpu.VMEM((tm, d), F32)], name=f"ffn{layer}_out", sem=("parallel", "arbitrary"), carry=carry)[0]


def _ffn_bwd_act(a, cw, cb, w_out, dxn, layer, tm=512, carry=None):
    t, d = dxn.shape
    nc = N_SHARDS // 2

    def body(ag_ref, pg_ref, cwg_ref, cbg_ref, au_ref, pu_ref, cwu_ref, cbu_ref, wout_ref, dx_ref,
             dhu_ref, dw_ref, dconv_ref):
        i = pl.program_id(1)

        @pl.when(i == 0)
        def _():
            dw_ref[...] = jnp.zeros_like(dw_ref)
            dconv_ref[...] = jnp.zeros_like(dconv_ref)

        hg, ag0, ag1, ag2 = _causal_conv(ag_ref, pg_ref, cwg_ref[...], cbg_ref[...], i == 0, tm)
        hu, au0, au1, au2 = _causal_conv(au_ref, pu_ref, cwu_ref[...], cbu_ref[...], i == 0, tm)
        sg = _sigmoid(hg)
        sl = hg * sg
        dxb = dx_ref[...].astype(BF16)
        dact = _dot(dxb, wout_ref[...], NT)
        dw_ref[...] += _dot((sl * hu).astype(BF16), dxb, TN)
        d_up = dact * sl
        d_gate = dact * hu * (sg * (1.0 + hg * (1.0 - sg)))
        for j, (dv, taps) in enumerate(((d_gate, (ag2, ag1, ag0)), (d_up, (au2, au1, au0)))):
            dvb = dv.astype(BF16)
            dhu_ref[j] = dvb
            dvr = dvb.astype(F32)
            for k in range(3):
                dconv_ref[j, k:k + 1, :] += jnp.sum(dvr * taps[k], axis=0, keepdims=True)
            dconv_ref[j, 3:4, :] += jnp.sum(dv, axis=0, keepdims=True)

    return _call(
        body, [a, a, cw, cb, a, a, cw, cb, w_out, dxn], grid=(nc, t // tm),
        in_specs=_ffn_conv_specs(tm, lambda c, i: c, lambda c, i: i)
        + [pl.BlockSpec((FF_SHARD, d), lambda c, i: (c, 0)), pl.BlockSpec((tm, d), lambda c, i: (i, 0))],
        out_specs=[pl.BlockSpec((None, 2, tm, FF_SHARD), lambda c, i: (c, 0, i, 0)),
                   pl.BlockSpec((FF_SHARD, d), lambda c, i: (c, 0)),
                   pl.BlockSpec((None, 2, 8, FF_SHARD), lambda c, i: (c, 0, 0, 0))],
        out_shape=[jax.ShapeDtypeStruct((nc, 2, t, FF_SHARD), BF16), jax.ShapeDtypeStruct((D_FF, d), F32),
                   jax.ShapeDtypeStruct((nc, 2, 8, FF_SHARD), F32)],
        name=f"ffn{layer}_bwd_act", sem=("parallel", "arbitrary"), carry=carry)


def _ffn_bwd_in(dhu, cw, w_in, layer, tm=1024, carry=None):
    nc, _, t, _ = dhu.shape
    d = D_MODEL
    tm = min(tm, t)
    last_blk = t // 16 - 1

    def body(dh_ref, nx_ref, cw_ref, win_ref, da_ref, o_ref):
        i, s = pl.program_id(0), pl.program_id(1)
        df = dh_ref[...].astype(F32)
        keep = jnp.where(i == t // tm - 1, 0.0, 1.0)
        nx = nx_ref[...].astype(F32)
        n0 = nx[0:1, :] * keep
        n1 = nx[1:2, :] * keep
        row = lax.broadcasted_iota(jnp.int32, df.shape, 0)
        d1 = jnp.where(row == tm - 1, n0, pltpu.roll(df, tm - 1, 0))
        d2 = jnp.where(row == tm - 1, n1, jnp.where(row == tm - 2, n0, pltpu.roll(df, tm - 2, 0)))
        cw = cw_ref[...]
        da = (cw[2:3, :] * df + cw[1:2, :] * d1 + cw[0:1, :] * d2).astype(BF16)
        da_ref[...] = da
        p = _dot(da, win_ref[...], NT)

        @pl.when(s == 0)
        def _():
            o_ref[...] = p

        @pl.when(s > 0)
        def _():
            o_ref[...] += p

    return _call(
        body, [dhu, dhu, cw, w_in], grid=(t // tm, N_SHARDS),
        in_specs=[pl.BlockSpec((None, None, tm, FF_SHARD), lambda i, s: (s % nc, s // nc, i, 0)),
                  pl.BlockSpec((None, None, 16, FF_SHARD),
                               lambda i, s: (s % nc, s // nc, jnp.minimum((i + 1) * (tm // 16), last_blk), 0)),
                  pl.BlockSpec((None, 8, FF_SHARD), lambda i, s: (s, 0, 0)),
                  pl.BlockSpec((None, d, FF_SHARD), lambda i, s: (s, 0, 0))],
        out_specs=[pl.BlockSpec((None, tm, FF_SHARD), lambda i, s: (s, i, 0)),
                   pl.BlockSpec((tm, d), lambda i, s: (i, 0))],
        out_shape=[jax.ShapeDtypeStruct((N_SHARDS, t, FF_SHARD), BF16), jax.ShapeDtypeStruct((t, d), F32)],
        name=f"ffn{layer}_bwd_in", sem=("parallel", "arbitrary"), carry=carry)


def _ffn_wgrad_in(hf, da, layer, carry=None):
    t, d = hf.shape
    return _mm(
        hf, da, pl.BlockSpec((t, d), lambda s, j, kk: (0, 0)),
        pl.BlockSpec((None, t, FF_SHARD), lambda s, j, kk: (s, 0, 0)),
        pl.BlockSpec((None, d, FF_SHARD), lambda s, j, kk: (s, 0, 0)),
        jax.ShapeDtypeStruct((N_SHARDS, d, FF_SHARD), F32), (N_SHARDS, 1, 1), TN, f"ffn{layer}_wgrad_in",
        carry=carry)


Q_PER_KV = N_Q_HEADS // N_KV_HEADS
GROUP_ROWS = Q_PER_KV * CHUNK


def _attn_masks(n):
    lane = lax.broadcasted_iota(jnp.int32, (CHUNK, LANES), 1)
    lo = lane < HEAD_DIM
    tq = lax.broadcasted_iota(jnp.int32, (GROUP_ROWS, 2 * CHUNK), 0) & (CHUNK - 1)
    jk = lax.broadcasted_iota(jnp.int32, (GROUP_ROWS, 2 * CHUNK), 1)
    dist = tq + CHUNK - jk
    mask = (dist >= 0) & (dist < CHUNK) & (jk >= jnp.where(n == 0, CHUNK, 0))
    return lo, mask, dist.astype(F32)


def _per_head_column(values):
    r = lax.broadcasted_iota(jnp.int32, (GROUP_ROWS, 1), 0)
    col = jnp.full((GROUP_ROWS, 1), values[Q_PER_KV - 1], F32)
    for j in range(Q_PER_KV - 2, -1, -1):
        col = jnp.where(r < (j + 1) * CHUNK, values[j], col)
    return col


def _half_sum(x, lo):
    s_lo = jnp.sum(jnp.where(lo, x, 0.0), axis=-1, keepdims=True)
    s_hi = jnp.sum(jnp.where(lo, 0.0, x), axis=-1, keepdims=True)
    return jnp.where(lo, s_lo, s_hi)


def _stack_heads(pairs, lo):
    zero = jnp.zeros_like(pairs[0])
    return jnp.concatenate([jnp.where(lo, pairs[0], zero), jnp.where(lo, zero, pairs[0]),
                            jnp.where(lo, pairs[1], zero), jnp.where(lo, zero, pairs[1])], axis=0)


def _unstack_heads(stacked, lo):
    return (jnp.where(lo, stacked[0:CHUNK], stacked[CHUNK:2 * CHUNK]),
            jnp.where(lo, stacked[2 * CHUNK:3 * CHUNK], stacked[3 * CHUNK:]))


def _attn_probs(qs, kn, mask, distf, slope_col, sink_col):
    s = _dot(qs, kn, NT) * (HEAD_DIM ** -0.5)
    s = jnp.where(mask, s - slope_col * distf, NEG_BIG)
    m = jnp.maximum(jnp.max(s, axis=-1, keepdims=True), sink_col)
    e = jnp.exp(s - m)
    den = jnp.sum(e, axis=-1, keepdims=True) + jnp.exp(sink_col - m)
    return e * (1.0 / den), m, den


def _attn_fwd(qraw, kvd, gq, gk, sinks, carry=None):
    t, d = qraw.shape
    nb = t // CHUNK

    def body(sink_ref, q_ref, cur_ref, prev_ref, gq_ref, gk_ref, o_ref):
        n = pl.program_id(0)
        lo, mask, distf = _attn_masks(n)
        gq_v, gk_v = gq_ref[...], gk_ref[...]
        for kvh in range(N_KV_HEADS):
            ks = slice(kvh * LANES, (kvh + 1) * LANES)
            vs = slice(4 * LANES + kvh * LANES, 4 * LANES + (kvh + 1) * LANES)
            kraw = jnp.concatenate([prev_ref[:, ks], cur_ref[:, ks]], axis=0)
            rk = lax.rsqrt(jnp.mean(kraw * kraw, axis=-1, keepdims=True) + EPS)
            kn = (kraw * rk * gk_v).astype(BF16)
            vv = jnp.concatenate([prev_ref[:, vs], cur_ref[:, vs]], axis=0).astype(BF16)
            qn = []
            for p in range(2):
                qp = q_ref[:, (2 * kvh + p) * LANES:(2 * kvh + p + 1) * LANES]
                r = lax.rsqrt(_half_sum(qp * qp, lo) * (1.0 / HEAD_DIM) + EPS)
                qn.append(qp * r * gq_v)
            heads = range(Q_PER_KV * kvh, Q_PER_KV * (kvh + 1))
            pf, _, _ = _attn_probs(_stack_heads(qn, lo).astype(BF16), kn, mask, distf,
                                   _per_head_column([SLOPES[h] for h in heads]),
                                   _per_head_column([sink_ref[h] for h in heads]))
            for p, o_pair in enumerate(_unstack_heads(_dot(pf.astype(BF16), vv), lo)):
                o_ref[:, (2 * kvh + p) * LANES:(2 * kvh + p + 1) * LANES] = o_pair.astype(BF16)

    blk = lambda f: pl.BlockSpec((CHUNK, d), f)
    vec = pl.BlockSpec((1, LANES), lambda n: (0, 0))
    return _call(
        body, [sinks, qraw, kvd, kvd, gq, gk], grid=(nb,),
        in_specs=[pl.BlockSpec(memory_space=pltpu.SMEM), blk(lambda n: (n, 0)), blk(lambda n: (n, 0)),
                  blk(lambda n: (jnp.maximum(n - 1, 0), 0)), vec, vec],
        out_specs=[blk(lambda n: (n, 0))], out_shape=[jax.ShapeDtypeStruct((t, d), BF16)],
        name="attn_fwd", carry=carry)[0]


def _attn_bwd(qraw, kvd, d_o, gq, gk, sinks, carry=None):
    t, d = qraw.shape
    nb = t // CHUNK

    def body(sink_ref, q_ref, cur_ref, prev_ref, do_ref, gq_ref, gk_ref,
             dq_ref, dkv_ref, dsink_ref, dgq_ref, dgk_ref, carry_s, pp_s, cp_s):
        n = pl.program_id(0)

        @pl.when(n == 0)
        def _():
            carry_s[...] = jnp.zeros_like(carry_s)
            dsink_ref[...] = jnp.zeros_like(dsink_ref)
            dgq_ref[...] = jnp.zeros_like(dgq_ref)
            dgk_ref[...] = jnp.zeros_like(dgk_ref)

        @pl.when(n < nb)
        def _():
            lo, mask, distf = _attn_masks(n)
            gq_v, gk_v = gq_ref[...], gk_ref[...]
            for kvh in range(N_KV_HEADS):
                ks = slice(kvh * LANES, (kvh + 1) * LANES)
                vs = slice(4 * LANES + kvh * LANES, 4 * LANES + (kvh + 1) * LANES)
                kraw = jnp.concatenate([prev_ref[:, ks], cur_ref[:, ks]], axis=0)
                rk = lax.rsqrt(jnp.mean(kraw * kraw, axis=-1, keepdims=True) + EPS)
                khat = kraw * rk
                kn = (khat * gk_v).astype(BF16)
                vv = jnp.concatenate([prev_ref[:, vs], cur_ref[:, vs]], axis=0).astype(BF16)
                cols = [slice((2 * kvh + p) * LANES, (2 * kvh + p + 1) * LANES) for p in range(2)]
                rq, qhat = [], []
                for p in range(2):
                    qp = q_ref[:, cols[p]]
                    rq.append(lax.rsqrt(_half_sum(qp * qp, lo) * (1.0 / HEAD_DIM) + EPS))
                    qhat.append(qp * rq[p])
                heads = range(Q_PER_KV * kvh, Q_PER_KV * (kvh + 1))
                qs = _stack_heads([qhat[p] * gq_v for p in range(2)], lo).astype(BF16)
                dos = _stack_heads([do_ref[:, cols[p]] for p in range(2)], lo)
                sink_col = _per_head_column([sink_ref[h] for h in heads])
                pf, m, den = _attn_probs(qs, kn, mask, distf, _per_head_column([SLOPES[h] for h in heads]), sink_col)
                dp = _dot(dos, vv, NT)
                delta = jnp.sum(pf * dp, axis=-1, keepdims=True)
                sink_delta = jnp.exp(sink_col - m) / den * delta
                for j, h in enumerate(heads):
                    dsink_ref[h:h + 1, :] -= jnp.broadcast_to(
                        jnp.sum(sink_delta[j * CHUNK:(j + 1) * CHUNK], axis=0, keepdims=True), (1, LANES))
                ds = (pf * (dp - delta) * (HEAD_DIM ** -0.5)).astype(BF16)
                dkn = _dot(ds, qs, TN)
                dvb = _dot(pf.astype(BF16), dos, TN)
                for p, dqn in enumerate(_unstack_heads(_dot(ds, kn), lo)):
                    dgq_ref[0:1, :] += jnp.sum(dqn * qhat[p], axis=0, keepdims=True)
                    gy = dqn * gq_v
                    mq = _half_sum(gy * qhat[p], lo) * (1.0 / HEAD_DIM)
                    dq_ref[:, cols[p]] = (rq[p] * (gy - qhat[p] * mq)).astype(BF16)
                dgk_ref[0:1, :] += jnp.sum(dkn * khat, axis=0, keepdims=True)
                gyk = dkn * gk_v
                dkraw = rk * (gyk - khat * jnp.mean(gyk * khat, axis=-1, keepdims=True))
                pp_s[:, ks] = dkraw[:CHUNK]
                cp_s[:, ks] = dkraw[CHUNK:]
                pp_s[:, vs] = dvb[:CHUNK]
                cp_s[:, vs] = dvb[CHUNK:]
            dkv_ref[...] = (carry_s[...] + pp_s[...]).astype(BF16)
            carry_s[...] = cp_s[...]

        @pl.when(n == nb)
        def _():
            dkv_ref[...] = carry_s[...].astype(BF16)

    blk = lambda f: pl.BlockSpec((CHUNK, d), f)
    vec = pl.BlockSpec((1, LANES), lambda n: (0, 0))
    cur = lambda n: (jnp.minimum(n, nb - 1), 0)
    prev = lambda n: (jnp.maximum(jnp.minimum(n, nb - 1) - 1, 0), 0)
    small = lambda r: pl.BlockSpec((r, LANES), lambda n: (0, 0))
    return _call(
        body, [sinks, qraw, kvd, kvd, d_o, gq, gk], grid=(nb + 1,),
        in_specs=[pl.BlockSpec(memory_space=pltpu.SMEM), blk(cur), blk(cur), blk(prev), blk(cur), vec, vec],
        out_specs=[blk(cur), blk(lambda n: (jnp.maximum(n - 1, 0), 0)), small(N_Q_HEADS), small(8), small(8)],
        out_shape=[jax.ShapeDtypeStruct((t, d), BF16), jax.ShapeDtypeStruct((t, d), BF16),
                   jax.ShapeDtypeStruct((N_Q_HEADS, LANES), F32), jax.ShapeDtypeStruct((8, LANES), F32),
                   jax.ShapeDtypeStruct((8, LANES), F32)],
        scratch=[pltpu.VMEM((CHUNK, d), F32)] * 3, name="attn_bwd", sem=("arbitrary",), carry=carry)


def _loss_head(y, target, tm=512):
    t, d = y.shape

    def body(y_ref, t_ref, dy_ref, loss_ref, acc_ref):
        i = pl.program_id(0)

        @pl.when(i == 0)
        def _():
            acc_ref[...] = jnp.zeros_like(acc_ref)

        err = y_ref[...] - t_ref[...]
        dy_ref[...] = err * (1.0 / d)
        acc_ref[...] += jnp.sum(err * err, axis=0, keepdims=True)

        @pl.when(i == t // tm - 1)
        def _():
            loss_ref[...] = jnp.broadcast_to(0.5 / d * jnp.sum(acc_ref[...], axis=1, keepdims=True), loss_ref.shape)

    row = pl.BlockSpec((tm, d), lambda i: (i, 0))
    return _call(
        body, [y, target], grid=(t // tm,), in_specs=[row, row],
        out_specs=[row, pl.BlockSpec((8, LANES), lambda i: (0, 0))],
        out_shape=[jax.ShapeDtypeStruct((t, d), F32), jax.ShapeDtypeStruct((8, LANES), F32)],
        scratch=[pltpu.VMEM((1, d), F32)], name="loss_head", sem=("arbitrary",))


def _adamw_math(g, w, m, v):
    m = ADAM_B1 * m + (1.0 - ADAM_B1) * g
    v = ADAM_B2 * v + (1.0 - ADAM_B2) * (g * g)
    m_hat = m / (1.0 - ADAM_B1 ** ADAM_STEP)
    v_hat = v / (1.0 - ADAM_B2 ** ADAM_STEP)
    delta = -ADAM_LR * (m_hat / (jnp.sqrt(v_hat) + ADAM_EPS) + ADAM_WD * w)
    return delta, m, v


def _row_tile(r, cap=128):
    for tr in range(min(r, cap), 0, -1):
        if r % tr == 0 and (tr % 8 == 0 or tr == r):
            return tr
    return r


def _chip_sum(grad, recv, place, name, wire_dtype):
    _, r, c = grad.shape
    tr = _row_tile(r, 256)

    def body(pl_ref, g_ref, a_ref, p_ref):
        p_ref[...] = (g_ref[...] + a_ref[...]).astype(p_ref.dtype)

    return pl.pallas_call(
        body,
        grid_spec=pltpu.PrefetchScalarGridSpec(
            num_scalar_prefetch=1, grid=(4, r // tr),
            in_specs=[pl.BlockSpec((None, None, tr, c), lambda q, i, pr: (q, pr[1], i, 0)),
                      pl.BlockSpec((None, tr, c), lambda q, i, pr: (q, i, 0))],
            out_specs=pl.BlockSpec((None, tr, c), lambda q, i, pr: (q, i, 0))),
        out_shape=jax.ShapeDtypeStruct((4, r, c), wire_dtype), name=name, compiler_params=_params(),
    )(place, grad.reshape(4, 2, r, c), recv)


def _adamw_sharded(grad, recv, others, place, w, m, v, name, layer=None, fill=None):
    r, c = w.shape[-2:]
    tr = _row_tile(r)

    def body(pl_ref, g_ref, a_ref, oth_ref, w_ref, m_ref, v_ref, *rest):
        g_out, d_out, nm_out, nv_out = rest[-4:]
        g = g_ref[...] + a_ref[...]
        for k in range(3):
            g = g + oth_ref[k].astype(F32)
        delta, nm, nv = _adamw_math(g, w_ref[...], m_ref[...], v_ref[...])
        g_out[...] = g
        d_out[...] = delta
        nm_out[...] = nm
        nv_out[...] = nv

    if layer is None:
        row = pl.BlockSpec((tr, c), lambda i, pr: (i, 0))
    else:
        row = pl.BlockSpec((None, tr, c), lambda i, pr: (layer, i, 0))
    n_fill = 0 if fill is None else 4
    in_specs = [pl.BlockSpec((None, None, tr, c), lambda i, pr: (pr[0], pr[1], i, 0)),
                pl.BlockSpec((None, tr, c), lambda i, pr: (pr[0], i, 0)),
                pl.BlockSpec((3, tr, c), lambda i, pr: (0, i, 0)), row, row, row]
    in_specs += [pl.BlockSpec(memory_space=pl.ANY)] * n_fill
    return pl.pallas_call(
        body,
        grid_spec=pltpu.PrefetchScalarGridSpec(
            num_scalar_prefetch=1, grid=(r // tr,), in_specs=in_specs, out_specs=[row] * 4),
        out_shape=[jax.ShapeDtypeStruct(w.shape, F32)] * 4, name=name, compiler_params=_params(),
        input_output_aliases={7 + j: j for j in range(n_fill)},
    )(place, grad.reshape(4, 2, r, c), recv, others, w, m, v, *([] if fill is None else fill))


def _adamw_summed(parts, ws, ms, vs, name):
    n = len(parts)

    def body(*refs):
        p_refs, w_refs, m_refs, v_refs = refs[:n], refs[n:2 * n], refs[2 * n:3 * n], refs[3 * n:4 * n]
        o_refs = refs[4 * n:]
        for i in range(n):
            g = p_refs[i][0]
            for k in range(1, N_SHARDS):
                g = g + p_refs[i][k]
            delta, nm, nv = _adamw_math(g, w_refs[i][...], m_refs[i][...], v_refs[i][...])
            o_refs[4 * i][...] = g
            o_refs[4 * i + 1][...] = delta
            o_refs[4 * i + 2][...] = nm
            o_refs[4 * i + 3][...] = nv

    shapes = [jax.ShapeDtypeStruct(w.shape, F32) for w in ws for _ in range(4)]
    outs = pl.pallas_call(body, out_shape=shapes, name=name, compiler_params=_params())(*parts, *ws, *ms, *vs)
    return [outs[4 * i:4 * i + 4] for i in range(n)]


def _dup_heads(w):
    lead = w.shape[:-1]
    w4 = w.reshape(lead + (N_KV_HEADS, 1, HEAD_DIM))
    return jnp.broadcast_to(w4, lead + (N_KV_HEADS, 2, HEAD_DIM)).reshape(lead + (N_KV_HEADS * LANES,))


def _fold_heads(g):
    lead = g.shape[:-1]
    return g.reshape(lead + (N_KV_HEADS, 2, HEAD_DIM)).sum(axis=-2).reshape(lead + (N_KV_HEADS * HEAD_DIM,))


def kernel(x, a_norm, a_w_in, a_v_norm, a_w_s, a_b_s, a_w_out, f_norm, f_w_in, f_conv_w, f_conv_b, f_w_out, kv_norm, w_kv, k_norm, b_norm, b_w_q, b_q_norm, b_sinks, b_w_o, loss_target, m_a_norm, m_a_w_in, m_a_v_norm, m_a_w_s, m_a_b_s, m_a_w_out, m_f_norm, m_f_w_in, m_f_conv_w, m_f_conv_b, m_f_w_out, m_kv_norm, m_w_kv, m_k_norm, m_b_norm, m_b_w_q, m_b_q_norm, m_b_sinks, m_b_w_o, v_a_norm, v_a_w_in, v_a_v_norm, v_a_w_s, v_a_b_s, v_a_w_out, v_f_norm, v_f_w_in, v_f_conv_w, v_f_conv_b, v_f_w_out, v_kv_norm, v_w_kv, v_k_norm, v_b_norm, v_b_w_q, v_b_q_norm, v_b_sinks, v_b_w_o):
    d = D_MODEL
    xi, yi, ci = _coords()
    place = jnp.stack([2 * xi + yi, ci]).astype(jnp.int32)
    bf = lambda a: a.astype(BF16)
    row = lambda v_: v_.reshape(1, -1)
    x0, target = x[0], loss_target[0]
    t = x0.shape[0]
    res = {}

    red = {}

    def to_sibling(grads, wire=BF16):
        for k, g in grads.items():
            red[k] = dict(grad=g, wire=wire)
        ex = _ToSibling(list(grads.values()))
        ex.names = list(grads)
        return ex

    def to_chips(ex):
        for k, a in zip(ex.names, ex.results):
            red[k]["recv"] = a
            red[k]["psum"] = _chip_sum(red[k]["grad"], a, place, f"chip_sum_{k}", red[k]["wire"])
        nxt = _ToChips([red[k]["psum"] for k in ex.names])
        nxt.names = ex.names
        return nxt

    def landed(ex):
        for k, b in zip(ex.names, ex.results):
            red[k]["others"] = b

    def update(k, w, m, v, layer=None, fill=None):
        r = red[k]
        return _adamw_sharded(r["grad"], r["recv"], r["others"], place, w, m, v,
                              f"adamw_{k}", layer=layer, fill=fill)

    g_a_in, g_a_out, g_a_norm, g_a_v_norm, g_conv = _exchange_alone(
        _Gather([bf(a_w_in[0]), bf(a_w_out[0]), a_norm, a_v_norm, f_conv_w.reshape(6, FF_SHARD)]), "gather_first")
    a_norm_full, a_v_norm_full = g_a_norm.reshape(1, d), g_a_v_norm.reshape(1, d)
    conv_w = lax.reduce_precision(g_conv.reshape(N_SHARDS, 2, 3, FF_SHARD), 8, 7)
    cw = jnp.pad(jnp.transpose(conv_w, (1, 0, 2, 3)), ((0, 0), (0, 0), (0, 5), (0, 0)))
    w_a_in_flat = jnp.transpose(g_a_in, (1, 0, 2)).reshape(d, 2 * d)
    cb = f_conv_b.reshape(2, N_SHARDS, 1, FF_SHARD)
    tri = jnp.tril(jnp.ones((CHUNK, CHUNK), dtype=bool))
    w_causal = jnp.where(tri[None], a_w_s[0], 0.0).astype(BF16)
    w_causal_t = jnp.transpose(w_causal, (0, 2, 1))
    b_sb = jnp.broadcast_to(a_b_s[0][:, :, None], (N_GROUPS, CHUNK, CHUNK))
    w_a_out = g_a_out.reshape(d, d)
    gq = jnp.tile(b_q_norm.reshape(1, HEAD_DIM), (1, 2))
    gk = jnp.tile(k_norm.reshape(1, HEAD_DIM), (1, 2))
    sinks = b_sinks.reshape(N_Q_HEADS)

    (h1,) = _rms_fwd(x0, [a_norm_full], "a_norm_fwd")
    ex = _Gather([bf(f_w_in[0])])
    zpre, x1 = _sgu_fwd(x0, h1, g_a_in, a_v_norm_full, w_causal, b_sb, w_a_out, carry=ex)
    w_in0 = ex.results[0]
    (hf0,) = _rms_fwd(x1, [f_norm[0:1]], "f0_norm_fwd")
    ex = _Gather([bf(f_w_out[0]), bf(w_kv), bf(b_w_q[0]), bf(b_w_o[0])])
    a0 = _ffn_in(hf0, w_in0, 0, carry=ex)
    w_out0 = ex.results[0].reshape(D_FF, d)
    kv_full = ex.results[1].reshape(d, 2 * N_KV_HEADS * HEAD_DIM)
    w_q, w_o = ex.results[2].reshape(d, d), ex.results[3].reshape(d, d)
    half = N_KV_HEADS * HEAD_DIM
    w_kv_dup = jnp.concatenate([_dup_heads(kv_full[:, :half]), _dup_heads(kv_full[:, half:])], axis=1)
    ex = _Gather([bf(f_w_in[1])])
    x2 = _ffn_out(a0, cw[0], cb[0], w_out0, x1, 0, carry=ex)
    w_in1 = ex.results[0]
    hk, hq = _rms_fwd(x2, [row(kv_norm), b_norm], "kvq_norm_fwd")
    kvd = _mm_rows(hk, w_kv_dup, F32, "kv_proj")
    qraw = _mm_rows(hq, w_q, F32, "q_proj")
    ex = _Gather([bf(f_w_out[1])])
    o = _attn_fwd(qraw, kvd, gq, gk, sinks, carry=ex)
    w_out1 = ex.results[0].reshape(D_FF, d)
    x3 = _mm_rows(o, w_o, F32, "o_proj", res=x2)
    (hf1,) = _rms_fwd(x3, [f_norm[1:2]], "f1_norm_fwd")
    a1 = _ffn_in(hf1, w_in1, 1)
    x4 = _ffn_out(a1, cw[1], cb[1], w_out1, x3, 1)
    dy, loss_lanes = _loss_head(x4, target)
    loss = lax.psum(loss_lanes[0, 0], ("x", "y", "c"))

    dhu1, dw_out1, dconv1 = _ffn_bwd_act(a1, cw[1], cb[1], w_out1, dy, 1)
    ex = to_sibling({"f_w_out1": dw_out1.reshape(N_SHARDS, D_FF // N_SHARDS, d)})
    da1, dhf1 = _ffn_bwd_in(dhu1, cw[1], w_in1, 1, carry=ex)
    ex = to_chips(ex)
    dw_in1 = _ffn_wgrad_in(hf1, da1, 1, carry=ex)
    landed(ex)
    ex = to_sibling({"f_w_in1": dw_in1})
    dx3, dgf1 = _rms_bwd(x3, [f_norm[1:2]], [dhf1], dy, "f1_norm_bwd", carry=ex)
    ex = to_chips(ex)
    d_o = _mm_rows(dx3, w_o, BF16, "o_proj_bwd", trans_w=True)
    dw_o = _mm_wgrad(o, dx3, "o_wgrad").reshape(N_SHARDS, d // N_SHARDS, d)
    dq, dkv, dsink, dgq, dgk = _attn_bwd(qraw, kvd, d_o, gq, gk, sinks, carry=ex)
    landed(ex)
    dw_q = _mm_wgrad(hq, dq, "q_wgrad").reshape(N_SHARDS, d // N_SHARDS, d)
    dw_kv_dup = _mm_wgrad(hk, dkv, "kv_wgrad")
    dw_kv = jnp.concatenate(
        [_fold_heads(dw_kv_dup[:, :4 * LANES]), _fold_heads(dw_kv_dup[:, 4 * LANES:])], axis=1
    ).reshape(N_SHARDS, d // N_SHARDS, 2 * N_KV_HEADS * HEAD_DIM)
    ex = to_sibling({"b_w_o": dw_o, "b_w_q": dw_q, "w_kv": dw_kv})
    dhq = _mm_rows(dq, w_q, F32, "q_proj_bwd", trans_w=True, carry=ex)
    dhk = _mm_rows(dkv, w_kv_dup, F32, "kv_proj_bwd", trans_w=True)
    ex = to_chips(ex)
    dx2, dg2 = _rms_bwd(x2, [row(kv_norm), b_norm], [dhk, dhq], dx3, "kvq_norm_bwd")
    dhu0, dw_out0, dconv0 = _ffn_bwd_act(a0, cw[0], cb[0], w_out0, dx2, 0, carry=ex)
    landed(ex)
    ex = to_sibling({"f_w_out0": dw_out0.reshape(N_SHARDS, D_FF // N_SHARDS, d)})
    da0, dhf0 = _ffn_bwd_in(dhu0, cw[0], w_in0, 0, carry=ex)
    ex = to_chips(ex)
    dw_in0 = _ffn_wgrad_in(hf0, da0, 0, carry=ex)
    landed(ex)
    ex = to_sibling({"f_w_in0": dw_in0})
    dx1, dgf0 = _rms_bwd(x1, [f_norm[0:1]], [dhf0], dx2, "f0_norm_bwd", carry=ex)
    ex = to_chips(ex)
    dz, y, dwc, dbs, dgv = _sgu_bwd(dx1, zpre, w_a_out, a_v_norm_full, w_causal, w_causal_t, b_sb, carry=ex)
    landed(ex)
    dw_a_out = _mm_wgrad(y, dx1, "a_out_wgrad").reshape(N_SHARDS, d // N_SHARDS, d)
    nsub = g_a_in.shape[2]
    dw_a_in = _mm(
        h1, dz, pl.BlockSpec((t, d), lambda s, j, kk: (0, 0)), pl.BlockSpec((t, nsub), lambda s, j, kk: (0, s)),
        pl.BlockSpec((None, d, nsub), lambda s, j, kk: (s, 0, 0)), jax.ShapeDtypeStruct((N_SHARDS, d, nsub), F32),
        (N_SHARDS, 1, 1), TN, "a_in_wgrad")

    def conv_grads(dconv):
        return jnp.transpose(dconv, (1, 0, 2, 3)).reshape(N_SHARDS, 8, FF_SHARD)

    dconv0, dconv1 = conv_grads(dconv0), conv_grads(dconv1)
    g_conv_w = jnp.concatenate([dconv0[:, 0:3, :], dconv1[:, 0:3, :]], axis=1)
    g_a_v_norm = dgv[0].reshape(N_SHARDS, 1, LANES)
    rep = ["a_w_s", "a_b_s", "f_norm", "f_conv_b", "kv_norm", "k_norm", "b_norm", "b_q_norm", "b_sinks"]
    rep_g = dict(
        a_w_s=dwc.reshape(N_GROUPS * CHUNK, CHUNK), a_b_s=dbs[:, :, 0], f_norm=jnp.stack([dgf0[0], dgf1[0]]),
        f_conv_b=jnp.stack([dconv0[:, 3, :].reshape(-1), dconv1[:, 3, :].reshape(-1)]), kv_norm=dg2[0:1],
        k_norm=(dgk[0, :HEAD_DIM] + dgk[0, HEAD_DIM:])[None], b_norm=dg2[1:2],
        b_q_norm=(dgq[0, :HEAD_DIM] + dgq[0, HEAD_DIM:])[None], b_sinks=dsink[:, 0][None])
    ex_big = to_sibling({"a_w_out": dw_a_out, "a_w_in": dw_a_in})
    ex_small = to_sibling({"a_v_norm": g_a_v_norm, "f_conv_w": g_conv_w}, wire=F32)
    ex_rep = _Gather([rep_g[k] for k in rep])
    together = _Together([ex_big, ex_small, ex_rep])
    dh1 = _mm_rows(dz, w_a_in_flat, F32, "a_in_bwd", trans_w=True, carry=together)
    together.spread()
    ex_big, ex_small = to_chips(ex_big), to_chips(ex_small)
    together = _Together([ex_big, ex_small])
    grad_x, dg0 = _rms_bwd(x0, [a_norm_full], [dh1], dx1, "a_norm_bwd", carry=together)
    together.spread()
    landed(ex_big)
    landed(ex_small)
    (a_norm_parts,) = _exchange_alone(_ToOwners([dg0[0].reshape(N_SHARDS, 1, LANES)]), "a_norm_to_owners")

    res["f_w_out"] = update("f_w_out1", f_w_out, m_f_w_out, v_f_w_out, layer=1)
    res["f_w_in"] = update("f_w_in1", f_w_in, m_f_w_in, v_f_w_in, layer=1)
    res["b_w_o"] = update("b_w_o", b_w_o, m_b_w_o, v_b_w_o, layer=0)
    res["b_w_q"] = update("b_w_q", b_w_q, m_b_w_q, v_b_w_q, layer=0)
    res["w_kv"] = update("w_kv", w_kv, m_w_kv, v_w_kv)
    res["f_w_out"] = update("f_w_out0", f_w_out, m_f_w_out, v_f_w_out, layer=0, fill=res["f_w_out"])
    res["f_w_in"] = update("f_w_in0", f_w_in, m_f_w_in, v_f_w_in, layer=0, fill=res["f_w_in"])
    res["a_w_out"] = update("a_w_out", a_w_out, m_a_w_out, v_a_w_out, layer=0)
    res["a_w_in"] = update("a_w_in", a_w_in, m_a_w_in, v_a_w_in, layer=0)
    res["a_v_norm"] = update("a_v_norm", a_v_norm, m_a_v_norm, v_a_v_norm)
    res["f_conv_w"] = [o_.reshape(f_conv_w.shape) for o_ in update(
        "f_conv_w", f_conv_w.reshape(6, FF_SHARD), m_f_conv_w.reshape(6, FF_SHARD), v_f_conv_w.reshape(6, FF_SHARD))]

    rep_w = dict(a_w_s=a_w_s, a_b_s=a_b_s, f_norm=f_norm, f_conv_b=f_conv_b, kv_norm=kv_norm, k_norm=k_norm,
                 b_norm=b_norm, b_q_norm=b_q_norm, b_sinks=b_sinks, a_norm=a_norm)
    rep_m = dict(a_w_s=m_a_w_s, a_b_s=m_a_b_s, f_norm=m_f_norm, f_conv_b=m_f_conv_b, kv_norm=m_kv_norm,
                 k_norm=m_k_norm, b_norm=m_b_norm, b_q_norm=m_b_q_norm, b_sinks=m_b_sinks, a_norm=m_a_norm)
    rep_v = dict(a_w_s=v_a_w_s, a_b_s=v_a_b_s, f_norm=v_f_norm, f_conv_b=v_f_conv_b, kv_norm=v_kv_norm,
                 k_norm=v_k_norm, b_norm=v_b_norm, b_q_norm=v_b_q_norm, b_sinks=v_b_sinks, a_norm=v_a_norm)
    keys = rep + ["a_norm"]
    parts = ex_rep.results + [a_norm_parts]
    as2d = lambda a, p: a.reshape(p.shape[1:])
    rep_outs = _adamw_summed(parts, [as2d(rep_w[k], p) for k, p in zip(keys, parts)],
                             [as2d(rep_m[k], p) for k, p in zip(keys, parts)],
                             [as2d(rep_v[k], p) for k, p in zip(keys, parts)], "adamw_replicated")
    for j, key in enumerate(keys):
        res[key] = [o_.reshape(rep_w[key].shape) for o_ in rep_outs[j]]

    order = ["a_norm", "a_w_in", "a_v_norm", "a_w_s", "a_b_s", "a_w_out", "f_norm", "f_w_in", "f_conv_w", "f_conv_b",
             "f_w_out", "kv_norm", "w_kv", "k_norm", "b_norm", "b_w_q", "b_q_norm", "b_sinks", "b_w_o"]
    outs = [loss, grad_x[None]]
    for j in range(4):
        outs += [res[k][j] for k in order]
    return tuple(outs)
```

```python
import jax
import jax.numpy as jnp
from jax import lax
from jax.experimental import pallas as pl
from jax.experimental.pallas import tpu as pltpu

F32 = jnp.float32
BF16 = jnp.bfloat16
EPS = 1e-6
D_MODEL = 1024
CHUNK = 128
N_GROUPS = 8
N_SHARDS = 8
HEAD_DIM = 64
N_Q_HEADS = 16
N_KV_HEADS = 4
D_FF = 2816
FF_SHARD = 2 * D_FF // N_SHARDS
LANES = 128
NEG_BIG = -1e30
ADAM_LR = 0.001
ADAM_B1 = 0.9
ADAM_B2 = 0.999
ADAM_EPS = 1e-08
ADAM_WD = 0.01
ADAM_STEP = 10
VMEM_LIMIT_BYTES = 56 * 1024 * 1024
MESH = pl.DeviceIdType.MESH

NN = (((1,), (0,)), ((), ()))
NT = (((1,), (1,)), ((), ()))
TN = (((0,), (0,)), ((), ()))
SLOPES = tuple(2.0 ** (-8.0 * (h + 1) / N_Q_HEADS) for h in range(N_Q_HEADS))


def _params(sem=None):
    return pltpu.CompilerParams(dimension_semantics=sem, vmem_limit_bytes=VMEM_LIMIT_BYTES)


def _dot(a, b, dims=NN):
    return lax.dot_general(a, b, dims, preferred_element_type=F32)


def _sigmoid(x):
    return 1.0 / (1.0 + jnp.exp(-x))


def _gelu_parts(z):
    cdf = 0.5 * (1.0 + lax.erf(z * (2.0 ** -0.5)))
    pdf = jnp.exp(-0.5 * z * z) * 0.3989422804014327
    return cdf, pdf


def _coords():
    return lax.axis_index("x"), lax.axis_index("y"), lax.axis_index("c")


class _Gather:
    def __init__(self, srcs):
        self.srcs = list(srcs)
        n = len(self.srcs)
        self.out_shapes = [jax.ShapeDtypeStruct((N_SHARDS,) + s.shape, s.dtype) for s in self.srcs]
        self.sems = [pltpu.SemaphoreType.DMA((n, 7)), pltpu.SemaphoreType.DMA((n, 7)), pltpu.SemaphoreType.DMA((n,))]

    def _plan(self, src, dst, sems):
        send_sems, recv_sems, local_sems = sems
        x, y, c = _coords()
        me, sibling = (x, y, c), (x, y, 1 - c)
        chips = [(1 - x, y), (x, 1 - y), (1 - x, 1 - y)]
        n = len(src)

        def rows(e, dev):
            return dst[e].at[4 * dev[0] + 2 * dev[1] + dev[2]]

        def copy(e, slot, block, to, from_own=False):
            return pltpu.make_async_remote_copy(
                src_ref=src[e] if from_own else rows(e, block), dst_ref=rows(e, block),
                send_sem=send_sems.at[e, slot], recv_sem=recv_sems.at[e, slot], device_id=to, device_id_type=MESH)

        mine = [pltpu.make_async_copy(src[e], rows(e, me), local_sems.at[e]) for e in range(n)]
        first = []
        for e in range(n):
            first.append(copy(e, 0, me, sibling, from_own=True))
            first += [copy(e, 1 + j, me, (*chip, c), from_own=True) for j, chip in enumerate(chips)]
        return n, me, sibling, chips, c, copy, mine, first

    def start(self, src, dst, sems):
        _, _, _, _, _, _, mine, first = self._plan(src, dst, sems)
        for cp in mine + first:
            cp.start()

    def finish(self, src, dst, sems):
        n, me, sibling, chips, c, copy, mine, first = self._plan(src, dst, sems)
        passed = []
        for j, chip in enumerate(chips):
            for e in range(n):
                copy(e, 1 + j, (*chip, c), me).wait_recv()
                cp = copy(e, 4 + j, (*chip, c), sibling)
                cp.start()
                passed.append(cp)
        for e in range(n):
            copy(e, 0, sibling, me).wait_recv()
            for j, chip in enumerate(chips):
                copy(e, 4 + j, (*chip, 1 - c), me).wait_recv()
        for cp in first + passed:
            cp.wait_send()
        for cp in mine:
            cp.wait()


class _ToSibling:
    def __init__(self, grads):
        self.srcs = list(grads)
        n = len(self.srcs)
        self.out_shapes = [jax.ShapeDtypeStruct((4,) + g.shape[1:], g.dtype) for g in self.srcs]
        self.sems = [pltpu.SemaphoreType.DMA((n, 4)), pltpu.SemaphoreType.DMA((n, 4))]

    def _copies(self, src, dst, sems):
        send_sems, recv_sems = sems
        x, y, c = _coords()
        return [
            pltpu.make_async_remote_copy(
                src_ref=src[i].at[2 * q + (1 - c)], dst_ref=dst[i].at[q], send_sem=send_sems.at[i, q],
                recv_sem=recv_sems.at[i, q], device_id=(x, y, 1 - c), device_id_type=MESH)
            for i in range(len(src)) for q in range(4)]

    def start(self, src, dst, sems):
        for cp in self._copies(src, dst, sems):
            cp.start()

    def finish(self, src, dst, sems):
        for cp in self._copies(src, dst, sems):
            cp.wait()


class _ToChips:
    def __init__(self, psums, rows=None):
        self.srcs = list(psums)
        n = len(self.srcs)
        self.rows = rows
        self.out_shapes = [
            jax.ShapeDtypeStruct((3, p.shape[1] if rows is None else rows[1]) + p.shape[2:], p.dtype)
            for p in self.srcs]
        self.sems = [pltpu.SemaphoreType.DMA((n, 3)), pltpu.SemaphoreType.DMA((n, 3))]

    def _copies(self, src, dst, sems):
        send_sems, recv_sems = sems
        x, y, c = _coords()
        peers = [(x, 1 - y), (1 - x, y), (1 - x, 1 - y)]

        def part(i, q):
            if self.rows is None:
                return src[i].at[q]
            return src[i].at[q, pl.ds(self.rows[0], self.rows[1])]

        return [
            pltpu.make_async_remote_copy(
                src_ref=part(i, 2 * px + py), dst_ref=dst[i].at[r], send_sem=send_sems.at[i, r],
                recv_sem=recv_sems.at[i, r], device_id=(px, py, c), device_id_type=MESH)
            for i in range(len(src)) for r, (px, py) in enumerate(peers)]

    def start(self, src, dst, sems):
        for cp in self._copies(src, dst, sems):
            cp.start()

    def finish(self, src, dst, sems):
        for cp in self._copies(src, dst, sems):
            cp.wait()


class _ToOwners:
    def __init__(self, grads):
        self.srcs = list(grads)
        n = len(self.srcs)
        self.out_shapes = [jax.ShapeDtypeStruct(g.shape, g.dtype) for g in self.srcs]
        self.sems = [pltpu.SemaphoreType.DMA((n, 7)), pltpu.SemaphoreType.DMA((n, 7)), pltpu.SemaphoreType.DMA((n,))]

    def _copies(self, src, dst, sems):
        send_sems, recv_sems, local_sems = sems
        x, y, c = _coords()
        me = 4 * x + 2 * y + c
        copies = [pltpu.make_async_copy(src[i].at[me], dst[i].at[me], local_sems.at[i]) for i in range(len(src))]
        for i in range(len(src)):
            for rel in range(1, N_SHARDS):
                px = x ^ (rel >> 2) if rel >> 2 else x
                py = y ^ ((rel >> 1) & 1) if (rel >> 1) & 1 else y
                pc = c ^ (rel & 1) if rel & 1 else c
                copies.append(pltpu.make_async_remote_copy(
                    src_ref=src[i].at[4 * px + 2 * py + pc], dst_ref=dst[i].at[me], send_sem=send_sems.at[i, rel - 1],
                    recv_sem=recv_sems.at[i, rel - 1], device_id=(px, py, pc), device_id_type=MESH))
        return copies

    def start(self, src, dst, sems):
        for cp in self._copies(src, dst, sems):
            cp.start()

    def finish(self, src, dst, sems):
        for cp in self._copies(src, dst, sems):
            cp.wait()


class _Together:
    def __init__(self, parts):
        self.parts = list(parts)
        self.srcs = [s for p in self.parts for s in p.srcs]
        self.out_shapes = [s for p in self.parts for s in p.out_shapes]
        self.sems = [s for p in self.parts for s in p.sems]

    def _split(self, src, dst, sems):
        a = b = c = 0
        for p in self.parts:
            na, nc = len(p.srcs), len(p.sems)
            yield p, src[a:a + na], dst[b:b + na], sems[c:c + nc]
            a, b, c = a + na, b + na, c + nc

    def start(self, src, dst, sems):
        for p, s, d, m in self._split(src, dst, sems):
            p.start(s, d, m)

    def finish(self, src, dst, sems):
        for p, s, d, m in self._split(src, dst, sems):
            p.finish(s, d, m)

    def spread(self):
        b = 0
        for p in self.parts:
            p.results = self.results[b:b + len(p.srcs)]
            b += len(p.srcs)


def _call(body, args, *, grid, in_specs, out_specs, out_shape, name, scratch=(), sem=None, carry=None):
    out_shape, out_specs = list(out_shape), list(out_specs)
    if carry is None:
        return pl.pallas_call(
            body, grid=grid, in_specs=list(in_specs), out_specs=out_specs, out_shape=out_shape,
            scratch_shapes=list(scratch), name=name, compiler_params=_params(sem))(*args)
    n_in, n_out, n_scr, n_c = len(args), len(out_shape), len(scratch), len(carry.srcs)
    steps = tuple(grid)

    def carried(*refs):
        ins, rest = refs[:n_in], refs[n_in:]
        c_src, rest = rest[:n_c], rest[n_c:]
        outs, rest = rest[:n_out], rest[n_out:]
        c_dst, rest = rest[:n_c], rest[n_c:]
        scr, sems = rest[:n_scr], rest[n_scr:]
        first = pl.program_id(0) == 0
        last = pl.program_id(0) == steps[0] - 1
        for ax in range(1, len(steps)):
            first = first & (pl.program_id(ax) == 0)
            last = last & (pl.program_id(ax) == steps[ax] - 1)

        @pl.when(first)
        def _():
            carry.start(c_src, c_dst, sems)

        body(*ins, *outs, *scr)

        @pl.when(last)
        def _():
            carry.finish(c_src, c_dst, sems)

    hbm = pl.BlockSpec(memory_space=pl.ANY)
    res = pl.pallas_call(
        carried, grid=grid, in_specs=list(in_specs) + [hbm] * n_c, out_specs=out_specs + [hbm] * n_c,
        out_shape=out_shape + carry.out_shapes, scratch_shapes=list(scratch) + carry.sems, name=name,
        compiler_params=_params(("arbitrary",) * len(steps)))(*args, *carry.srcs)
    carry.results = list(res[n_out:])
    return list(res[:n_out])


def _exchange_alone(ex, name):
    n = len(ex.srcs)

    def body(*refs):
        src, dst, sems = refs[:n], refs[n:2 * n], refs[2 * n:]
        ex.start(src, dst, sems)
        ex.finish(src, dst, sems)

    hbm = pl.BlockSpec(memory_space=pl.ANY)
    res = pl.pallas_call(body, in_specs=[hbm] * n, out_specs=[hbm] * n, out_shape=ex.out_shapes,
                         scratch_shapes=ex.sems, name=name)(*ex.srcs)
    ex.results = list(res)
    return ex.results


def _rms_fwd(x, gains, name, tm=512, carry=None):
    t, d = x.shape
    n = len(gains)

    def body(*refs):
        x_ref, g_refs, h_refs = refs[0], refs[1:1 + n], refs[1 + n:]
        xf = x_ref[...]
        xhat = xf * lax.rsqrt(jnp.mean(xf * xf, axis=-1, keepdims=True) + EPS)
        for g_ref, h_ref in zip(g_refs, h_refs):
            h_ref[...] = (xhat * g_ref[...]).astype(BF16)

    row = pl.BlockSpec((tm, d), lambda i: (i, 0))
    vec = pl.BlockSpec((1, d), lambda i: (0, 0))
    return _call(body, [x, *gains], grid=(t // tm,), in_specs=[row] + [vec] * n, out_specs=[row] * n,
                 out_shape=[jax.ShapeDtypeStruct((t, d), BF16)] * n, name=name, carry=carry)


def _rms_bwd(x, gains, dhs, dres, name, tm=256, carry=None):
    t, d = x.shape
    n = len(gains)

    def body(*refs):
        x_ref, dres_ref = refs[0], refs[1]
        g_refs, dh_refs = refs[2:2 + n], refs[2 + n:2 + 2 * n]
        dx_ref, dg_ref = refs[2 + 2 * n], refs[3 + 2 * n]
        i = pl.program_id(0)

        @pl.when(i == 0)
        def _():
            dg_ref[...] = jnp.zeros_like(dg_ref)

        xf = x_ref[...]
        r = lax.rsqrt(jnp.mean(xf * xf, axis=-1, keepdims=True) + EPS)
        xhat = xf * r
        dx = dres_ref[...]
        for j in range(n):
            dh = dh_refs[j][...]
            dg_ref[j:j + 1, :] += jnp.sum(dh * xhat, axis=0, keepdims=True)
            gy = dh * g_refs[j][...]
            dx = dx + r * (gy - xhat * jnp.mean(gy * xhat, axis=-1, keepdims=True))
        dx_ref[...] = dx

    row = pl.BlockSpec((tm, d), lambda i: (i, 0))
    vec = pl.BlockSpec((1, d), lambda i: (0, 0))
    return _call(body, [x, dres, *gains, *dhs], grid=(t // tm,), in_specs=[row, row] + [vec] * n + [row] * n,
                 out_specs=[row, pl.BlockSpec((8, d), lambda i: (0, 0))],
                 out_shape=[jax.ShapeDtypeStruct((t, d), F32), jax.ShapeDtypeStruct((8, d), F32)],
                 name=name, sem=("arbitrary",), carry=carry)


def _mm(a, b, a_spec, b_spec, o_spec, out_shape, grid, dims, name, res=None, res_spec=None, carry=None):
    nk = grid[2]
    acc_shape = tuple(s for s in o_spec.block_shape if s is not None)

    def body(*refs):
        a_ref, b_ref = refs[0], refs[1]
        r_ref = refs[2] if res is not None else None
        o_ref = refs[3] if res is not None else refs[2]
        p = _dot(a_ref[...].astype(BF16), b_ref[...].astype(BF16), dims)
        if nk == 1:
            if res is not None:
                p = p + r_ref[...]
            o_ref[...] = p.astype(o_ref.dtype)
            return
        acc_ref = refs[-1]
        k = pl.program_id(2)

        @pl.when(k == 0)
        def _():
            acc_ref[...] = p

        @pl.when(k > 0)
        def _():
            acc_ref[...] += p

        @pl.when(k == nk - 1)
        def _():
            out = acc_ref[...]
            if res is not None:
                out = out + r_ref[...]
            o_ref[...] = out.astype(o_ref.dtype)

    ins = [a, b] + ([res] if res is not None else [])
    specs = [a_spec, b_spec] + ([res_spec] if res is not None else [])
    return _call(body, ins, grid=grid, in_specs=specs, out_specs=[o_spec], out_shape=[out_shape],
                 scratch=[pltpu.VMEM(acc_shape, F32)] if nk > 1 else [], name=name,
                 sem=("parallel", "parallel", "arbitrary"), carry=carry)[0]


def _mm_rows(a, w, out_dtype, name, trans_w=False, res=None, tm=512, carry=None):
    t, k = a.shape
    n = w.shape[0] if trans_w else w.shape[1]
    return _mm(
        a, w, pl.BlockSpec((tm, k), lambda i, j, kk: (i, 0)), pl.BlockSpec(w.shape, lambda i, j, kk: (0, 0)),
        pl.BlockSpec((tm, n), lambda i, j, kk: (i, 0)), jax.ShapeDtypeStruct((t, n), out_dtype), (t // tm, 1, 1),
        NT if trans_w else NN, name, res=res,
        res_spec=None if res is None else pl.BlockSpec((tm, n), lambda i, j, kk: (i, 0)), carry=carry)


def _mm_wgrad(a, b, name, carry=None):
    t, m = a.shape
    n = b.shape[1]
    tn = n // (4 if b.dtype == F32 else 2)
    return _mm(
        a, b, pl.BlockSpec((t, m), lambda i, j, kk: (0, 0)), pl.BlockSpec((t, tn), lambda i, j, kk: (0, j)),
        pl.BlockSpec((m, tn), lambda i, j, kk: (0, j)), jax.ShapeDtypeStruct((m, n), F32), (1, n // tn, 1), TN, name,
        carry=carry)


def _sgu_fwd(x0, h1, w_in, g_v, w_c, b_sb, w_out, tm=256, carry=None):
    t, d = x0.shape
    nsub = w_in.shape[2]

    def body(x_ref, h_ref, win_ref, gv_ref, wc_ref, bsb_ref, wout_ref, zpre_ref, x1_ref, u_s, v_s, vn_s, y_s):
        h = h_ref[...]
        for k in range(N_SHARDS):
            zk = _dot(h, win_ref[k])
            zpre_ref[:, k * nsub:(k + 1) * nsub] = zk
            cdf, _ = _gelu_parts(zk)
            if k < N_SHARDS // 2:
                u_s[:, k * nsub:(k + 1) * nsub] = zk * cdf
            else:
                v_s[:, (k - 4) * nsub:(k - 3) * nsub] = zk * cdf
        v = v_s[...]
        rv = lax.rsqrt(jnp.mean(v * v, axis=-1, keepdims=True) + EPS)
        vn_s[...] = (v * rv * gv_ref[...]).astype(BF16)
        for ci in range(tm // CHUNK):
            rows = slice(ci * CHUNK, (ci + 1) * CHUNK)
            for g in range(N_GROUPS):
                cols = slice(g * LANES, (g + 1) * LANES)
                sv = _dot(wc_ref[g], vn_s[rows, cols]) + bsb_ref[g]
                y_s[rows, cols] = (u_s[rows, cols] * sv).astype(BF16)
        x1_ref[...] = x_ref[...] + _dot(y_s[...], wout_ref[...])

    row = pl.BlockSpec((tm, d), lambda i: (i, 0))
    full = lambda a: pl.BlockSpec(a.shape, lambda i: (0,) * a.ndim)
    return _call(
        body, [x0, h1, w_in, g_v, w_c, b_sb, w_out], grid=(t // tm,),
        in_specs=[row, row, full(w_in), full(g_v), full(w_c), full(b_sb), full(w_out)],
        out_specs=[pl.BlockSpec((tm, 2 * d), lambda i: (i, 0)), row],
        out_shape=[jax.ShapeDtypeStruct((t, 2 * d), F32), jax.ShapeDtypeStruct((t, d), F32)],
        scratch=[pltpu.VMEM((tm, d), F32), pltpu.VMEM((tm, d), F32), pltpu.VMEM((tm, d), BF16),
                 pltpu.VMEM((tm, d), BF16)],
        name="sgu_fwd", carry=carry)


def _sgu_bwd(dx1, zpre, w_out, g_v, w_c, w_ct, b_sb, tm=256, carry=None):
    t, d = dx1.shape

    def body(dx_ref, zpre_ref, wout_ref, gv_ref, wc_ref, wct_ref, bsb_ref,
             dz_ref, y_ref, dwc_ref, dbs_ref, dgv_ref, u_s, vn_s, dy_s, du_s, dvn_s):
        i = pl.program_id(0)

        @pl.when(i == 0)
        def _():
            dwc_ref[...] = jnp.zeros_like(dwc_ref)
            dbs_ref[...] = jnp.zeros_like(dbs_ref)
            dgv_ref[...] = jnp.zeros_like(dgv_ref)

        dy_s[...] = _dot(dx_ref[...].astype(BF16), wout_ref[...], NT)
        zu = zpre_ref[:, :d]
        zv = zpre_ref[:, d:]
        cdf_u, pdf_u = _gelu_parts(zu)
        cdf_v, pdf_v = _gelu_parts(zv)
        u_s[...] = zu * cdf_u
        v = zv * cdf_v
        rv = lax.rsqrt(jnp.mean(v * v, axis=-1, keepdims=True) + EPS)
        vhat = v * rv
        gv = gv_ref[...]
        vn_s[...] = (vhat * gv).astype(BF16)
        for ci in range(tm // CHUNK):
            rows = slice(ci * CHUNK, (ci + 1) * CHUNK)
            for g in range(N_GROUPS):
                cols = slice(g * LANES, (g + 1) * LANES)
                vnb = vn_s[rows, cols]
                sv = _dot(wc_ref[g], vnb) + bsb_ref[g]
                dyb = dy_s[rows, cols]
                ub = u_s[rows, cols]
                dsv = dyb * ub
                du_s[rows, cols] = dyb * sv
                y_ref[rows, cols] = (ub * sv).astype(BF16)
                dsvb = dsv.astype(BF16)
                dbs_ref[g] += dsv
                dwc_ref[g] += _dot(dsvb, vnb, NT)
                dvn_s[rows, cols] = _dot(wct_ref[g], dsvb)
        dvn = dvn_s[...]
        dgv_ref[0:1, :] += jnp.sum(dvn * vhat, axis=0, keepdims=True)
        gy = dvn * gv
        dv = rv * (gy - vhat * jnp.mean(gy * vhat, axis=-1, keepdims=True))
        dz_ref[:, :d] = (du_s[...] * (cdf_u + zu * pdf_u)).astype(BF16)
        dz_ref[:, d:] = (dv * (cdf_v + zv * pdf_v)).astype(BF16)

        @pl.when(i == t // tm - 1)
        def _():
            tri = (lax.broadcasted_iota(jnp.int32, (CHUNK, CHUNK), 0)
                   >= lax.broadcasted_iota(jnp.int32, (CHUNK, CHUNK), 1))
            for g in range(N_GROUPS):
                dwc_ref[g] = jnp.where(tri, dwc_ref[g], 0.0)
                dbs_ref[g] = jnp.broadcast_to(jnp.sum(dbs_ref[g], axis=1, keepdims=True), (CHUNK, CHUNK))

    row = pl.BlockSpec((tm, d), lambda i: (i, 0))
    row2 = pl.BlockSpec((tm, 2 * d), lambda i: (i, 0))
    full = lambda a: pl.BlockSpec(a.shape, lambda i: (0,) * a.ndim)
    grp = pl.BlockSpec((N_GROUPS, CHUNK, CHUNK), lambda i: (0, 0, 0))
    return _call(
        body, [dx1, zpre, w_out, g_v, w_c, w_ct, b_sb], grid=(t // tm,),
        in_specs=[row, row2, full(w_out), full(g_v), full(w_c), full(w_ct), full(b_sb)],
        out_specs=[row2, row, grp, grp, pl.BlockSpec((8, d), lambda i: (0, 0))],
        out_shape=[jax.ShapeDtypeStruct((t, 2 * d), BF16), jax.ShapeDtypeStruct((t, d), BF16),
                   jax.ShapeDtypeStruct((N_GROUPS, CHUNK, CHUNK), F32),
                   jax.ShapeDtypeStruct((N_GROUPS, CHUNK, CHUNK), F32), jax.ShapeDtypeStruct((8, d), F32)],
        scratch=[pltpu.VMEM((tm, d), F32), pltpu.VMEM((tm, d), BF16), pltpu.VMEM((tm, d), F32),
                 pltpu.VMEM((tm, d), F32), pltpu.VMEM((tm, d), F32)],
        name="sgu_bwd", sem=("arbitrary",), carry=carry)


ROW_CHUNK = 256
HALO = 16


def _causal_conv(a_ref, prev_ref, cw, cb, r0, keep, nrows=ROW_CHUNK):
    if r0 == 0:
        win = jnp.concatenate([prev_ref[...].astype(F32) * keep, a_ref[0:nrows, :].astype(F32)], axis=0)
    else:
        win = a_ref[r0 - HALO:r0 + nrows, :].astype(F32)
    a0 = win[HALO:]
    a1 = pltpu.roll(win, 1, 0)[HALO:]
    a2 = pltpu.roll(win, 2, 0)[HALO:]
    hu = cw[2:3, :] * a0 + cw[1:2, :] * a1 + cw[0:1, :] * a2 + cb
    return hu, a0, a1, a2


def _ffn_in(hf, w_in, layer, tm=1024, carry=None):
    t, d = hf.shape
    tm = min(tm, t)
    return _mm(
        hf, w_in, pl.BlockSpec((tm, d), lambda s, i, kk: (i, 0)),
        pl.BlockSpec((None, d, FF_SHARD), lambda s, i, kk: (s, 0, 0)),
        pl.BlockSpec((None, tm, FF_SHARD), lambda s, i, kk: (s, i, 0)),
        jax.ShapeDtypeStruct((N_SHARDS, t, FF_SHARD), BF16), (N_SHARDS, t // tm, 1), NN, f"ffn{layer}_in", carry=carry)


def _ffn_conv_specs(tm, gate_of, tile_of):
    def specs(shard_of):
        return [
            pl.BlockSpec((None, tm, FF_SHARD), lambda *g: (shard_of(*g), tile_of(*g), 0)),
            pl.BlockSpec((None, 16, FF_SHARD),
                         lambda *g: (shard_of(*g), jnp.maximum(tile_of(*g) * (tm // 16) - 1, 0), 0)),
            pl.BlockSpec((None, 8, FF_SHARD), lambda *g: (shard_of(*g), 0, 0)),
            pl.BlockSpec((None, 1, FF_SHARD), lambda *g: (shard_of(*g), 0, 0)),
        ]
    return specs(gate_of) + specs(lambda *g: gate_of(*g) + N_SHARDS // 2)


def _ffn_out(a, cw, cb, w_out, x, layer, tm=512, carry=None):
    t, d = x.shape
    nc = N_SHARDS // 2

    def body(ag_ref, pg_ref, cwg_ref, cbg_ref, au_ref, pu_ref, cwu_ref, cbu_ref, wout_ref, x_ref, o_ref):
        i, c = pl.program_id(0), pl.program_id(1)

        @pl.when(c == 0)
        def _():
            o_ref[...] = x_ref[...]

        keep = jnp.where(i == 0, 0.0, 1.0)
        cwg, cbg, cwu, cbu = cwg_ref[...], cbg_ref[...], cwu_ref[...], cbu_ref[...]
        for r0 in range(0, tm, ROW_CHUNK):
            hg = _causal_conv(ag_ref, pg_ref, cwg, cbg, r0, keep)[0]
            hu = _causal_conv(au_ref, pu_ref, cwu, cbu, r0, keep)[0]
            act = (hg * _sigmoid(hg) * hu).astype(BF16)
            o_ref[r0:r0 + ROW_CHUNK, :] += _dot(act, wout_ref[...])

    row = pl.BlockSpec((tm, d), lambda i, c: (i, 0))
    return _call(
        body, [a, a, cw, cb, a, a, cw, cb, w_out, x], grid=(t // tm, nc),
        in_specs=_ffn_conv_specs(tm, lambda i, c: c, lambda i, c: i)
        + [pl.BlockSpec((FF_SHARD, d), lambda i, c: (c, 0)), row],
        out_specs=[row], out_shape=[jax.ShapeDtypeStruct((t, d), F32)],
        name=f"ffn{layer}_out", sem=("parallel", "arbitrary"), carry=carry)[0]


def _ffn_bwd_act(a, cw, cb, w_out, dxn, layer, tm=512, carry=None):
    t, d = dxn.shape
    nc = N_SHARDS // 2

    def body(ag_ref, pg_ref, cwg_ref, cbg_ref, au_ref, pu_ref, cwu_ref, cbu_ref, wout_ref, dx_ref,
             dhu_ref, dw_ref, dconv_ref):
        i = pl.program_id(1)

        @pl.when(i == 0)
        def _():
            dw_ref[...] = jnp.zeros_like(dw_ref)
            dconv_ref[...] = jnp.zeros_like(dconv_ref)

        keep = jnp.where(i == 0, 0.0, 1.0)
        cwg, cbg, cwu, cbu = cwg_ref[...], cbg_ref[...], cwu_ref[...], cbu_ref[...]
        hg, ag0, ag1, ag2 = _causal_conv(ag_ref, pg_ref, cwg, cbg, 0, keep, tm)
        hu, au0, au1, au2 = _causal_conv(au_ref, pu_ref, cwu, cbu, 0, keep, tm)
        sg = _sigmoid(hg)
        sl = hg * sg
        dxb = dx_ref[...].astype(BF16)
        dact = _dot(dxb, wout_ref[...], NT)
        dw_ref[...] += _dot((sl * hu).astype(BF16), dxb, TN)
        d_up = dact * sl
        d_gate = dact * hu * (sg * (1.0 + hg * (1.0 - sg)))
        for j, (dv, taps) in enumerate(((d_gate, (ag2, ag1, ag0)), (d_up, (au2, au1, au0)))):
            dvb = dv.astype(BF16)
            dhu_ref[j] = dvb
            dvr = dvb.astype(F32)
            for k in range(3):
                dconv_ref[j, k:k + 1, :] += jnp.sum(dvr * taps[k], axis=0, keepdims=True)
            dconv_ref[j, 3:4, :] += jnp.sum(dv, axis=0, keepdims=True)

    return _call(
        body, [a, a, cw, cb, a, a, cw, cb, w_out, dxn], grid=(nc, t // tm),
        in_specs=_ffn_conv_specs(tm, lambda c, i: c, lambda c, i: i)
        + [pl.BlockSpec((FF_SHARD, d), lambda c, i: (c, 0)), pl.BlockSpec((tm, d), lambda c, i: (i, 0))],
        out_specs=[pl.BlockSpec((None, 2, tm, FF_SHARD), lambda c, i: (c, 0, i, 0)),
                   pl.BlockSpec((FF_SHARD, d), lambda c, i: (c, 0)),
                   pl.BlockSpec((None, 2, 8, FF_SHARD), lambda c, i: (c, 0, 0, 0))],
        out_shape=[jax.ShapeDtypeStruct((nc, 2, t, FF_SHARD), BF16), jax.ShapeDtypeStruct((D_FF, d), F32),
                   jax.ShapeDtypeStruct((nc, 2, 8, FF_SHARD), F32)],
        name=f"ffn{layer}_bwd_act", sem=("parallel", "arbitrary"), carry=carry)


def _ffn_bwd_in(dhu, cw, w_in, layer, tm=1024, carry=None):
    nc, _, t, _ = dhu.shape
    d = D_MODEL
    tm = min(tm, t)
    last_blk = t // 16 - 1

    def body(dh_ref, nx_ref, cw_ref, win_ref, da_ref, o_ref):
        i, s = pl.program_id(0), pl.program_id(1)

        @pl.when(s == 0)
        def _():
            o_ref[...] = jnp.zeros_like(o_ref)

        keep = jnp.where(i == t // tm - 1, 0.0, 1.0)
        cw = cw_ref[...]
        for r0 in range(0, tm, ROW_CHUNK):
            rows = slice(r0, r0 + ROW_CHUNK)
            if r0 + ROW_CHUNK == tm:
                win = jnp.concatenate([dh_ref[rows, :].astype(F32), nx_ref[...].astype(F32) * keep], axis=0)
            else:
                win = dh_ref[r0:r0 + ROW_CHUNK + HALO, :].astype(F32)
            n = ROW_CHUNK + HALO
            d1 = pltpu.roll(win, n - 1, 0)[:ROW_CHUNK]
            d2 = pltpu.roll(win, n - 2, 0)[:ROW_CHUNK]
            da = (cw[2:3, :] * win[:ROW_CHUNK] + cw[1:2, :] * d1 + cw[0:1, :] * d2).astype(BF16)
            da_ref[rows, :] = da
            o_ref[rows, :] += _dot(da, win_ref[...], NT)

    return _call(
        body, [dhu, dhu, cw, w_in], grid=(t // tm, N_SHARDS),
        in_specs=[pl.BlockSpec((None, None, tm, FF_SHARD), lambda i, s: (s % nc, s // nc, i, 0)),
                  pl.BlockSpec((None, None, 16, FF_SHARD),
                               lambda i, s: (s % nc, s // nc, jnp.minimum((i + 1) * (tm // 16), last_blk), 0)),
                  pl.BlockSpec((None, 8, FF_SHARD), lambda i, s: (s, 0, 0)),
                  pl.BlockSpec((None, d, FF_SHARD), lambda i, s: (s, 0, 0))],
        out_specs=[pl.BlockSpec((None, tm, FF_SHARD), lambda i, s: (s, i, 0)),
                   pl.BlockSpec((tm, d), lambda i, s: (i, 0))],
        out_shape=[jax.ShapeDtypeStruct((N_SHARDS, t, FF_SHARD), BF16), jax.ShapeDtypeStruct((t, d), F32)],
        name=f"ffn{layer}_bwd_in", sem=("parallel", "arbitrary"), carry=carry)


def _ffn_wgrad_in(hf, da, layer, carry=None):
    t, d = hf.shape
    return _mm(
        hf, da, pl.BlockSpec((t, d), lambda s, j, kk: (0, 0)),
        pl.BlockSpec((None, t, FF_SHARD), lambda s, j, kk: (s, 0, 0)),
        pl.BlockSpec((None, d, FF_SHARD), lambda s, j, kk: (s, 0, 0)),
        jax.ShapeDtypeStruct((N_SHARDS, d, FF_SHARD), F32), (N_SHARDS, 1, 1), TN, f"ffn{layer}_wgrad_in",
        carry=carry)


Q_PER_KV = N_Q_HEADS // N_KV_HEADS
GROUP_ROWS = Q_PER_KV * CHUNK


def _attn_masks(n):
    lane = lax.broadcasted_iota(jnp.int32, (CHUNK, LANES), 1)
    lo = lane < HEAD_DIM
    tq = lax.broadcasted_iota(jnp.int32, (GROUP_ROWS, 2 * CHUNK), 0) & (CHUNK - 1)
    jk = lax.broadcasted_iota(jnp.int32, (GROUP_ROWS, 2 * CHUNK), 1)
    dist = tq + CHUNK - jk
    mask = (dist >= 0) & (dist < CHUNK) & (jk >= jnp.where(n == 0, CHUNK, 0))
    return lo, mask, dist.astype(F32)


def _per_head_column(values):
    r = lax.broadcasted_iota(jnp.int32, (GROUP_ROWS, 1), 0)
    col = jnp.full((GROUP_ROWS, 1), values[Q_PER_KV - 1], F32)
    for j in range(Q_PER_KV - 2, -1, -1):
        col = jnp.where(r < (j + 1) * CHUNK, values[j], col)
    return col


def _half_sum(x, lo):
    s_lo = jnp.sum(jnp.where(lo, x, 0.0), axis=-1, keepdims=True)
    s_hi = jnp.sum(jnp.where(lo, 0.0, x), axis=-1, keepdims=True)
    return jnp.where(lo, s_lo, s_hi)


def _stack_heads(pairs, lo):
    zero = jnp.zeros_like(pairs[0])
    return jnp.concatenate([jnp.where(lo, pairs[0], zero), jnp.where(lo, zero, pairs[0]),
                            jnp.where(lo, pairs[1], zero), jnp.where(lo, zero, pairs[1])], axis=0)


def _unstack_heads(stacked, lo):
    return (jnp.where(lo, stacked[0:CHUNK], stacked[CHUNK:2 * CHUNK]),
            jnp.where(lo, stacked[2 * CHUNK:3 * CHUNK], stacked[3 * CHUNK:]))


def _attn_probs(qs, kn, mask, distf, slope_col, sink_col):
    s = _dot(qs, kn, NT) * (HEAD_DIM ** -0.5)
    s = jnp.where(mask, s - slope_col * distf, NEG_BIG)
    m = jnp.maximum(jnp.max(s, axis=-1, keepdims=True), sink_col)
    e = jnp.exp(s - m)
    den = jnp.sum(e, axis=-1, keepdims=True) + jnp.exp(sink_col - m)
    return e * (1.0 / den), m, den


def _attn_fwd(qraw, kvd, gq, gk, sinks, carry=None):
    t, d = qraw.shape
    nb = t // CHUNK

    def body(sink_ref, q_ref, cur_ref, prev_ref, gq_ref, gk_ref, o_ref):
        n = pl.program_id(0)
        lo, mask, distf = _attn_masks(n)
        gq_v, gk_v = gq_ref[...], gk_ref[...]
        for kvh in range(N_KV_HEADS):
            ks = slice(kvh * LANES, (kvh + 1) * LANES)
            vs = slice(4 * LANES + kvh * LANES, 4 * LANES + (kvh + 1) * LANES)
            kraw = jnp.concatenate([prev_ref[:, ks], cur_ref[:, ks]], axis=0)
            rk = lax.rsqrt(jnp.mean(kraw * kraw, axis=-1, keepdims=True) + EPS)
            kn = (kraw * rk * gk_v).astype(BF16)
            vv = jnp.concatenate([prev_ref[:, vs], cur_ref[:, vs]], axis=0).astype(BF16)
            qn = []
            for p in range(2):
                qp = q_ref[:, (2 * kvh + p) * LANES:(2 * kvh + p + 1) * LANES]
                r = lax.rsqrt(_half_sum(qp * qp, lo) * (1.0 / HEAD_DIM) + EPS)
                qn.append(qp * r * gq_v)
            heads = range(Q_PER_KV * kvh, Q_PER_KV * (kvh + 1))
            pf, _, _ = _attn_probs(_stack_heads(qn, lo).astype(BF16), kn, mask, distf,
                                   _per_head_column([SLOPES[h] for h in heads]),
                                   _per_head_column([sink_ref[h] for h in heads]))
            for p, o_pair in enumerate(_unstack_heads(_dot(pf.astype(BF16), vv), lo)):
                o_ref[:, (2 * kvh + p) * LANES:(2 * kvh + p + 1) * LANES] = o_pair.astype(BF16)

    blk = lambda f: pl.BlockSpec((CHUNK, d), f)
    vec = pl.BlockSpec((1, LANES), lambda n: (0, 0))
    return _call(
        body, [sinks, qraw, kvd, kvd, gq, gk], grid=(nb,),
        in_specs=[pl.BlockSpec(memory_space=pltpu.SMEM), blk(lambda n: (n, 0)), blk(lambda n: (n, 0)),
                  blk(lambda n: (jnp.maximum(n - 1, 0), 0)), vec, vec],
        out_specs=[blk(lambda n: (n, 0))], out_shape=[jax.ShapeDtypeStruct((t, d), BF16)],
        name="attn_fwd", carry=carry)[0]


def _attn_bwd(qraw, kvd, d_o, gq, gk, sinks, carry=None):
    t, d = qraw.shape
    nb = t // CHUNK

    def body(sink_ref, q_ref, cur_ref, prev_ref, do_ref, gq_ref, gk_ref,
             dq_ref, dkv_ref, dsink_ref, dgq_ref, dgk_ref, carry_s, pp_s, cp_s):
        n = pl.program_id(0)

        @pl.when(n == 0)
        def _():
            carry_s[...] = jnp.zeros_like(carry_s)
            dsink_ref[...] = jnp.zeros_like(dsink_ref)
            dgq_ref[...] = jnp.zeros_like(dgq_ref)
            dgk_ref[...] = jnp.zeros_like(dgk_ref)

        @pl.when(n < nb)
        def _():
            lo, mask, distf = _attn_masks(n)
            gq_v, gk_v = gq_ref[...], gk_ref[...]
            for kvh in range(N_KV_HEADS):
                ks = slice(kvh * LANES, (kvh + 1) * LANES)
                vs = slice(4 * LANES + kvh * LANES, 4 * LANES + (kvh + 1) * LANES)
                kraw = jnp.concatenate([prev_ref[:, ks], cur_ref[:, ks]], axis=0)
                rk = lax.rsqrt(jnp.mean(kraw * kraw, axis=-1, keepdims=True) + EPS)
                khat = kraw * rk
                kn = (khat * gk_v).astype(BF16)
                vv = jnp.concatenate([prev_ref[:, vs], cur_ref[:, vs]], axis=0).astype(BF16)
                cols = [slice((2 * kvh + p) * LANES, (2 * kvh + p + 1) * LANES) for p in range(2)]
                rq, qhat = [], []
                for p in range(2):
                    qp = q_ref[:, cols[p]]
                    rq.append(lax.rsqrt(_half_sum(qp * qp, lo) * (1.0 / HEAD_DIM) + EPS))
                    qhat.append(qp * rq[p])
                heads = range(Q_PER_KV * kvh, Q_PER_KV * (kvh + 1))
                qs = _stack_heads([qhat[p] * gq_v for p in range(2)], lo).astype(BF16)
                dos = _stack_heads([do_ref[:, cols[p]] for p in range(2)], lo)
                sink_col = _per_head_column([sink_ref[h] for h in heads])
                pf, m, den = _attn_probs(qs, kn, mask, distf, _per_head_column([SLOPES[h] for h in heads]), sink_col)
                dp = _dot(dos, vv, NT)
                delta = jnp.sum(pf * dp, axis=-1, keepdims=True)
                sink_delta = jnp.exp(sink_col - m) / den * delta
                for j, h in enumerate(heads):
                    dsink_ref[h:h + 1, :] -= jnp.broadcast_to(
                        jnp.sum(sink_delta[j * CHUNK:(j + 1) * CHUNK], axis=0, keepdims=True), (1, LANES))
                ds = (pf * (dp - delta) * (HEAD_DIM ** -0.5)).astype(BF16)
                dkn = _dot(ds, qs, TN)
                dvb = _dot(pf.astype(BF16), dos, TN)
                for p, dqn in enumerate(_unstack_heads(_dot(ds, kn), lo)):
                    dgq_ref[0:1, :] += jnp.sum(dqn * qhat[p], axis=0, keepdims=True)
                    gy = dqn * gq_v
                    mq = _half_sum(gy * qhat[p], lo) * (1.0 / HEAD_DIM)
                    dq_ref[:, cols[p]] = (rq[p] * (gy - qhat[p] * mq)).astype(BF16)
                dgk_ref[0:1, :] += jnp.sum(dkn * khat, axis=0, keepdims=True)
                gyk = dkn * gk_v
                dkraw = rk * (gyk - khat * jnp.mean(gyk * khat, axis=-1, keepdims=True))
                pp_s[:, ks] = dkraw[:CHUNK]
                cp_s[:, ks] = dkraw[CHUNK:]
                pp_s[:, vs] = dvb[:CHUNK]
                cp_s[:, vs] = dvb[CHUNK:]
            dkv_ref[...] = (carry_s[...] + pp_s[...]).astype(BF16)
            carry_s[...] = cp_s[...]

        @pl.when(n == nb)
        def _():
            dkv_ref[...] = carry_s[...].astype(BF16)

    blk = lambda f: pl.BlockSpec((CHUNK, d), f)
    vec = pl.BlockSpec((1, LANES), lambda n: (0, 0))
    cur = lambda n: (jnp.minimum(n, nb - 1), 0)
    prev = lambda n: (jnp.maximum(jnp.minimum(n, nb - 1) - 1, 0), 0)
    small = lambda r: pl.BlockSpec((r, LANES), lambda n: (0, 0))
    return _call(
        body, [sinks, qraw, kvd, kvd, d_o, gq, gk], grid=(nb + 1,),
        in_specs=[pl.BlockSpec(memory_space=pltpu.SMEM), blk(cur), blk(cur), blk(prev), blk(cur), vec, vec],
        out_specs=[blk(cur), blk(lambda n: (jnp.maximum(n - 1, 0), 0)), small(N_Q_HEADS), small(8), small(8)],
        out_shape=[jax.ShapeDtypeStruct((t, d), BF16), jax.ShapeDtypeStruct((t, d), BF16),
                   jax.ShapeDtypeStruct((N_Q_HEADS, LANES), F32), jax.ShapeDtypeStruct((8, LANES), F32),
                   jax.ShapeDtypeStruct((8, LANES), F32)],
        scratch=[pltpu.VMEM((CHUNK, d), F32)] * 3, name="attn_bwd", sem=("arbitrary",), carry=carry)


def _loss_head(y, target, tm=512):
    t, d = y.shape

    def body(y_ref, t_ref, dy_ref, loss_ref, acc_ref):
        i = pl.program_id(0)

        @pl.when(i == 0)
        def _():
            acc_ref[...] = jnp.zeros_like(acc_ref)

        err = y_ref[...] - t_ref[...]
        dy_ref[...] = err * (1.0 / d)
        acc_ref[...] += jnp.sum(err * err, axis=0, keepdims=True)

        @pl.when(i == t // tm - 1)
        def _():
            loss_ref[...] = jnp.broadcast_to(0.5 / d * jnp.sum(acc_ref[...], axis=1, keepdims=True), loss_ref.shape)

    row = pl.BlockSpec((tm, d), lambda i: (i, 0))
    return _call(
        body, [y, target], grid=(t // tm,), in_specs=[row, row],
        out_specs=[row, pl.BlockSpec((8, LANES), lambda i: (0, 0))],
        out_shape=[jax.ShapeDtypeStruct((t, d), F32), jax.ShapeDtypeStruct((8, LANES), F32)],
        scratch=[pltpu.VMEM((1, d), F32)], name="loss_head", sem=("arbitrary",))


def _adamw_math(g, w, m, v):
    m = ADAM_B1 * m + (1.0 - ADAM_B1) * g
    v = ADAM_B2 * v + (1.0 - ADAM_B2) * (g * g)
    m_hat = m / (1.0 - ADAM_B1 ** ADAM_STEP)
    v_hat = v / (1.0 - ADAM_B2 ** ADAM_STEP)
    delta = -ADAM_LR * (m_hat / (jnp.sqrt(v_hat) + ADAM_EPS) + ADAM_WD * w)
    return delta, m, v


def _row_tile(r, cap=128):
    for tr in range(min(r, cap), 0, -1):
        if r % tr == 0 and (tr % 8 == 0 or tr == r):
            return tr
    return r


def _chip_sum(grad, recv, place, name, wire_dtype):
    _, r, c = grad.shape
    tr = _row_tile(r, 256)

    def body(pl_ref, g_ref, a_ref, p_ref):
        p_ref[...] = (g_ref[...] + a_ref[...]).astype(p_ref.dtype)

    return pl.pallas_call(
        body,
        grid_spec=pltpu.PrefetchScalarGridSpec(
            num_scalar_prefetch=1, grid=(4, r // tr),
            in_specs=[pl.BlockSpec((None, None, tr, c), lambda q, i, pr: (q, pr[1], i, 0)),
                      pl.BlockSpec((None, tr, c), lambda q, i, pr: (q, i, 0))],
            out_specs=pl.BlockSpec((None, tr, c), lambda q, i, pr: (q, i, 0))),
        out_shape=jax.ShapeDtypeStruct((4, r, c), wire_dtype), name=name, compiler_params=_params(),
    )(place, grad.reshape(4, 2, r, c), recv)


def _adamw_sharded(grad, recv, others, place, w, m, v, name, layer=None, fill=None):
    r, c = w.shape[-2:]
    tr = _row_tile(r)

    def body(pl_ref, g_ref, a_ref, oth_ref, w_ref, m_ref, v_ref, *rest):
        g_out, d_out, nm_out, nv_out = rest[-4:]
        g = g_ref[...] + a_ref[...]
        for k in range(3):
            g = g + oth_ref[k].astype(F32)
        delta, nm, nv = _adamw_math(g, w_ref[...], m_ref[...], v_ref[...])
        g_out[...] = g
        d_out[...] = delta
        nm_out[...] = nm
        nv_out[...] = nv

    if layer is None:
        row = pl.BlockSpec((tr, c), lambda i, pr: (i, 0))
    else:
        row = pl.BlockSpec((None, tr, c), lambda i, pr: (layer, i, 0))
    n_fill = 0 if fill is None else 4
    in_specs = [pl.BlockSpec((None, None, tr, c), lambda i, pr: (pr[0], pr[1], i, 0)),
                pl.BlockSpec((None, tr, c), lambda i, pr: (pr[0], i, 0)),
                pl.BlockSpec((3, tr, c), lambda i, pr: (0, i, 0)), row, row, row]
    in_specs += [pl.BlockSpec(memory_space=pl.ANY)] * n_fill
    return pl.pallas_call(
        body,
        grid_spec=pltpu.PrefetchScalarGridSpec(
            num_scalar_prefetch=1, grid=(r // tr,), in_specs=in_specs, out_specs=[row] * 4),
        out_shape=[jax.ShapeDtypeStruct(w.shape, F32)] * 4, name=name, compiler_params=_params(),
        input_output_aliases={7 + j: j for j in range(n_fill)},
    )(place, grad.reshape(4, 2, r, c), recv, others, w, m, v, *([] if fill is None else fill))


def _adamw_summed(parts, ws, ms, vs, name):
    n = len(parts)

    def body(*refs):
        p_refs, w_refs, m_refs, v_refs = refs[:n], refs[n:2 * n], refs[2 * n:3 * n], refs[3 * n:4 * n]
        o_refs = refs[4 * n:]
        for i in range(n):
            g = p_refs[i][0]
            for k in range(1, N_SHARDS):
                g = g + p_refs[i][k]
            delta, nm, nv = _adamw_math(g, w_refs[i][...], m_refs[i][...], v_refs[i][...])
            o_refs[4 * i][...] = g
            o_refs[4 * i + 1][...] = delta
            o_refs[4 * i + 2][...] = nm
            o_refs[4 * i + 3][...] = nv

    shapes = [jax.ShapeDtypeStruct(w.shape, F32) for w in ws for _ in range(4)]
    outs = pl.pallas_call(body, out_shape=shapes, name=name, compiler_params=_params())(*parts, *ws, *ms, *vs)
    return [outs[4 * i:4 * i + 4] for i in range(n)]


def _dup_heads(w):
    lead = w.shape[:-1]
    w4 = w.reshape(lead + (N_KV_HEADS, 1, HEAD_DIM))
    return jnp.broadcast_to(w4, lead + (N_KV_HEADS, 2, HEAD_DIM)).reshape(lead + (N_KV_HEADS * LANES,))


def _fold_heads(g):
    lead = g.shape[:-1]
    return g.reshape(lead + (N_KV_HEADS, 2, HEAD_DIM)).sum(axis=-2).reshape(lead + (N_KV_HEADS * HEAD_DIM,))


def kernel(x, a_norm, a_w_in, a_v_norm, a_w_s, a_b_s, a_w_out, f_norm, f_w_in, f_conv_w, f_conv_b, f_w_out, kv_norm, w_kv, k_norm, b_norm, b_w_q, b_q_norm, b_sinks, b_w_o, loss_target, m_a_norm, m_a_w_in, m_a_v_norm, m_a_w_s, m_a_b_s, m_a_w_out, m_f_norm, m_f_w_in, m_f_conv_w, m_f_conv_b, m_f_w_out, m_kv_norm, m_w_kv, m_k_norm, m_b_norm, m_b_w_q, m_b_q_norm, m_b_sinks, m_b_w_o, v_a_norm, v_a_w_in, v_a_v_norm, v_a_w_s, v_a_b_s, v_a_w_out, v_f_norm, v_f_w_in, v_f_conv_w, v_f_conv_b, v_f_w_out, v_kv_norm, v_w_kv, v_k_norm, v_b_norm, v_b_w_q, v_b_q_norm, v_b_sinks, v_b_w_o):
    d = D_MODEL
    xi, yi, ci = _coords()
    place = jnp.stack([2 * xi + yi, ci]).astype(jnp.int32)
    bf = lambda a: a.astype(BF16)
    row = lambda v_: v_.reshape(1, -1)
    x0, target = x[0], loss_target[0]
    t = x0.shape[0]
    res = {}

    red = {}

    def to_sibling(grads, wire=BF16):
        for k, g in grads.items():
            red[k] = dict(grad=g, wire=wire)
        ex = _ToSibling(list(grads.values()))
        ex.names = list(grads)
        return ex

    def to_chips(ex):
        for k, a in zip(ex.names, ex.results):
            red[k]["recv"] = a
            red[k]["psum"] = _chip_sum(red[k]["grad"], a, place, f"chip_sum_{k}", red[k]["wire"])
        nxt = _ToChips([red[k]["psum"] for k in ex.names])
        nxt.names = ex.names
        return nxt

    def landed(ex):
        for k, b in zip(ex.names, ex.results):
            red[k]["others"] = b

    def halves(ex):
        parts = []
        for h in range(2):
            nr = ex.srcs[0].shape[1] // 2
            part = _ToChips(ex.srcs, rows=(h * nr, nr))
            part.names = ex.names
            parts.append(part)
        return parts

    def landed_halves(parts):
        for j, k in enumerate(parts[0].names):
            red[k]["others"] = jnp.concatenate([p.results[j] for p in parts], axis=1)

    def update(k, w, m, v, layer=None, fill=None):
        r = red[k]
        return _adamw_sharded(r["grad"], r["recv"], r["others"], place, w, m, v,
                              f"adamw_{k}", layer=layer, fill=fill)

    g_a_in, g_a_out, g_a_norm, g_a_v_norm, g_conv = _exchange_alone(
        _Gather([bf(a_w_in[0]), bf(a_w_out[0]), a_norm, a_v_norm, f_conv_w.reshape(6, FF_SHARD)]), "gather_first")
    a_norm_full, a_v_norm_full = g_a_norm.reshape(1, d), g_a_v_norm.reshape(1, d)
    conv_w = lax.reduce_precision(g_conv.reshape(N_SHARDS, 2, 3, FF_SHARD), 8, 7)
    cw = jnp.pad(jnp.transpose(conv_w, (1, 0, 2, 3)), ((0, 0), (0, 0), (0, 5), (0, 0)))
    w_a_in_flat = jnp.transpose(g_a_in, (1, 0, 2)).reshape(d, 2 * d)
    cb = f_conv_b.reshape(2, N_SHARDS, 1, FF_SHARD)
    tri = jnp.tril(jnp.ones((CHUNK, CHUNK), dtype=bool))
    w_causal = jnp.where(tri[None], a_w_s[0], 0.0).astype(BF16)
    w_causal_t = jnp.transpose(w_causal, (0, 2, 1))
    b_sb = jnp.broadcast_to(a_b_s[0][:, :, None], (N_GROUPS, CHUNK, CHUNK))
    w_a_out = g_a_out.reshape(d, d)
    gq = jnp.tile(b_q_norm.reshape(1, HEAD_DIM), (1, 2))
    gk = jnp.tile(k_norm.reshape(1, HEAD_DIM), (1, 2))
    sinks = b_sinks.reshape(N_Q_HEADS)

    (h1,) = _rms_fwd(x0, [a_norm_full], "a_norm_fwd")
    ex = _Gather([bf(f_w_in[0])])
    zpre, x1 = _sgu_fwd(x0, h1, g_a_in, a_v_norm_full, w_causal, b_sb, w_a_out, carry=ex)
    w_in0 = ex.results[0]
    (hf0,) = _rms_fwd(x1, [f_norm[0:1]], "f0_norm_fwd")
    ex = _Gather([bf(f_w_out[0]), bf(w_kv), bf(b_w_q[0]), bf(b_w_o[0])])
    a0 = _ffn_in(hf0, w_in0, 0, carry=ex)
    w_out0 = ex.results[0].reshape(D_FF, d)
    kv_full = ex.results[1].reshape(d, 2 * N_KV_HEADS * HEAD_DIM)
    w_q, w_o = ex.results[2].reshape(d, d), ex.results[3].reshape(d, d)
    half = N_KV_HEADS * HEAD_DIM
    w_kv_dup = jnp.concatenate([_dup_heads(kv_full[:, :half]), _dup_heads(kv_full[:, half:])], axis=1)
    ex = _Gather([bf(f_w_in[1])])
    x2 = _ffn_out(a0, cw[0], cb[0], w_out0, x1, 0, carry=ex)
    w_in1 = ex.results[0]
    hk, hq = _rms_fwd(x2, [row(kv_norm), b_norm], "kvq_norm_fwd")
    kvd = _mm_rows(hk, w_kv_dup, F32, "kv_proj")
    qraw = _mm_rows(hq, w_q, F32, "q_proj")
    ex = _Gather([bf(f_w_out[1])])
    o = _attn_fwd(qraw, kvd, gq, gk, sinks, carry=ex)
    w_out1 = ex.results[0].reshape(D_FF, d)
    x3 = _mm_rows(o, w_o, F32, "o_proj", res=x2)
    (hf1,) = _rms_fwd(x3, [f_norm[1:2]], "f1_norm_fwd")
    a1 = _ffn_in(hf1, w_in1, 1)
    x4 = _ffn_out(a1, cw[1], cb[1], w_out1, x3, 1)
    dy, loss_lanes = _loss_head(x4, target)
    loss = lax.psum(loss_lanes[0, 0], ("x", "y", "c"))

    dhu1, dw_out1, dconv1 = _ffn_bwd_act(a1, cw[1], cb[1], w_out1, dy, 1)
    ex = to_sibling({"f_w_out1": dw_out1.reshape(N_SHARDS, D_FF // N_SHARDS, d)})
    da1, dhf1 = _ffn_bwd_in(dhu1, cw[1], w_in1, 1, carry=ex)
    ex = to_chips(ex)
    dw_in1 = _ffn_wgrad_in(hf1, da1, 1, carry=ex)
    landed(ex)
    ex = to_sibling({"f_w_in1": dw_in1})
    dx3, dgf1 = _rms_bwd(x3, [f_norm[1:2]], [dhf1], dy, "f1_norm_bwd", carry=ex)
    ex = to_chips(ex)
    d_o = _mm_rows(dx3, w_o, BF16, "o_proj_bwd", trans_w=True)
    dw_o = _mm_wgrad(o, dx3, "o_wgrad").reshape(N_SHARDS, d // N_SHARDS, d)
    dq, dkv, dsink, dgq, dgk = _attn_bwd(qraw, kvd, d_o, gq, gk, sinks, carry=ex)
    landed(ex)
    dw_q = _mm_wgrad(hq, dq, "q_wgrad").reshape(N_SHARDS, d // N_SHARDS, d)
    dw_kv_dup = _mm_wgrad(hk, dkv, "kv_wgrad")
    dw_kv = jnp.concatenate(
        [_fold_heads(dw_kv_dup[:, :4 * LANES]), _fold_heads(dw_kv_dup[:, 4 * LANES:])], axis=1
    ).reshape(N_SHARDS, d // N_SHARDS, 2 * N_KV_HEADS * HEAD_DIM)
    ex = to_sibling({"b_w_o": dw_o, "b_w_q": dw_q, "w_kv": dw_kv})
    dhq = _mm_rows(dq, w_q, F32, "q_proj_bwd", trans_w=True, carry=ex)
    dhk = _mm_rows(dkv, w_kv_dup, F32, "kv_proj_bwd", trans_w=True)
    ex = to_chips(ex)
    dx2, dg2 = _rms_bwd(x2, [row(kv_norm), b_norm], [dhk, dhq], dx3, "kvq_norm_bwd")
    dhu0, dw_out0, dconv0 = _ffn_bwd_act(a0, cw[0], cb[0], w_out0, dx2, 0, carry=ex)
    landed(ex)
    ex = to_sibling({"f_w_out0": dw_out0.reshape(N_SHARDS, D_FF // N_SHARDS, d)})
    da0, dhf0 = _ffn_bwd_in(dhu0, cw[0], w_in0, 0, carry=ex)
    ex = to_chips(ex)
    dw_in0 = _ffn_wgrad_in(hf0, da0, 0, carry=ex)
    landed(ex)
    ex = to_sibling({"f_w_in0": dw_in0})
    dx1, dgf0 = _rms_bwd(x1, [f_norm[0:1]], [dhf0], dx2, "f0_norm_bwd", carry=ex)
    ex_lo, ex_hi = halves(to_chips(ex))
    dz, y, dwc, dbs, dgv = _sgu_bwd(dx1, zpre, w_a_out, a_v_norm_full, w_causal, w_causal_t, b_sb, carry=ex_lo)
    dw_a_out = _mm_wgrad(y, dx1, "a_out_wgrad").reshape(N_SHARDS, d // N_SHARDS, d)
    nsub = g_a_in.shape[2]
    dw_a_in = _mm(
        h1, dz, pl.BlockSpec((t, d), lambda s, j, kk: (0, 0)), pl.BlockSpec((t, nsub), lambda s, j, kk: (0, s)),
        pl.BlockSpec((None, d, nsub), lambda s, j, kk: (s, 0, 0)), jax.ShapeDtypeStruct((N_SHARDS, d, nsub), F32),
        (N_SHARDS, 1, 1), TN, "a_in_wgrad", carry=ex_hi)
    landed_halves([ex_lo, ex_hi])

    def conv_grads(dconv):
        return jnp.transpose(dconv, (1, 0, 2, 3)).reshape(N_SHARDS, 8, FF_SHARD)

    dconv0, dconv1 = conv_grads(dconv0), conv_grads(dconv1)
    g_conv_w = jnp.concatenate([dconv0[:, 0:3, :], dconv1[:, 0:3, :]], axis=1)
    g_a_v_norm = dgv[0].reshape(N_SHARDS, 1, LANES)
    rep = ["a_w_s", "a_b_s", "f_norm", "f_conv_b", "kv_norm", "k_norm", "b_norm", "b_q_norm", "b_sinks"]
    rep_g = dict(
        a_w_s=dwc.reshape(N_GROUPS * CHUNK, CHUNK), a_b_s=dbs[:, :, 0], f_norm=jnp.stack([dgf0[0], dgf1[0]]),
        f_conv_b=jnp.stack([dconv0[:, 3, :].reshape(-1), dconv1[:, 3, :].reshape(-1)]), kv_norm=dg2[0:1],
        k_norm=(dgk[0, :HEAD_DIM] + dgk[0, HEAD_DIM:])[None], b_norm=dg2[1:2],
        b_q_norm=(dgq[0, :HEAD_DIM] + dgq[0, HEAD_DIM:])[None], b_sinks=dsink[:, 0][None])
    ex_big = to_sibling({"a_w_out": dw_a_out, "a_w_in": dw_a_in})
    ex_small = to_sibling({"a_v_norm": g_a_v_norm, "f_conv_w": g_conv_w}, wire=F32)
    ex_rep = _Gather([rep_g[k] for k in rep])
    together = _Together([ex_big, ex_small, ex_rep])
    dh1 = _mm_rows(dz, w_a_in_flat, F32, "a_in_bwd", trans_w=True, carry=together)
    together.spread()
    ex_big, ex_small = to_chips(ex_big), to_chips(ex_small)
    together = _Together([ex_big, ex_small])
    grad_x, dg0 = _rms_bwd(x0, [a_norm_full], [dh1], dx1, "a_norm_bwd", carry=together)
    together.spread()
    landed(ex_big)
    landed(ex_small)
    (a_norm_parts,) = _exchange_alone(_ToOwners([dg0[0].reshape(N_SHARDS, 1, LANES)]), "a_norm_to_owners")

    res["f_w_out"] = update("f_w_out1", f_w_out, m_f_w_out, v_f_w_out, layer=1)
    res["f_w_in"] = update("f_w_in1", f_w_in, m_f_w_in, v_f_w_in, layer=1)
    res["b_w_o"] = update("b_w_o", b_w_o, m_b_w_o, v_b_w_o, layer=0)
    res["b_w_q"] = update("b_w_q", b_w_q, m_b_w_q, v_b_w_q, layer=0)
    res["w_kv"] = update("w_kv", w_kv, m_w_kv, v_w_kv)
    res["f_w_out"] = update("f_w_out0", f_w_out, m_f_w_out, v_f_w_out, layer=0, fill=res["f_w_out"])
    res["f_w_in"] = update("f_w_in0", f_w_in, m_f_w_in, v_f_w_in, layer=0, fill=res["f_w_in"])
    res["a_w_out"] = update("a_w_out", a_w_out, m_a_w_out, v_a_w_out, layer=0)
    res["a_w_in"] = update("a_w_in", a_w_in, m_a_w_in, v_a_w_in, layer=0)
    res["a_v_norm"] = update("a_v_norm", a_v_norm, m_a_v_norm, v_a_v_norm)
    res["f_conv_w"] = [o_.reshape(f_conv_w.shape) for o_ in update(
        "f_conv_w", f_conv_w.reshape(6, FF_SHARD), m_f_conv_w.reshape(6, FF_SHARD), v_f_conv_w.reshape(6, FF_SHARD))]

    rep_w = dict(a_w_s=a_w_s, a_b_s=a_b_s, f_norm=f_norm, f_conv_b=f_conv_b, kv_norm=kv_norm, k_norm=k_norm,
                 b_norm=b_norm, b_q_norm=b_q_norm, b_sinks=b_sinks, a_norm=a_norm)
    rep_m = dict(a_w_s=m_a_w_s, a_b_s=m_a_b_s, f_norm=m_f_norm, f_conv_b=m_f_conv_b, kv_norm=m_kv_norm,
                 k_norm=m_k_norm, b_norm=m_b_norm, b_q_norm=m_b_q_norm, b_sinks=m_b_sinks, a_norm=m_a_norm)
    rep_v = dict(a_w_s=v_a_w_s, a_b_s=v_a_b_s, f_norm=v_f_norm, f_conv_b=v_f_conv_b, kv_norm=v_kv_norm,
                 k_norm=v_k_norm, b_norm=v_b_norm, b_q_norm=v_b_q_norm, b_sinks=v_b_sinks, a_norm=v_a_norm)
    keys = rep + ["a_norm"]
    parts = ex_rep.results + [a_norm_parts]
    as2d = lambda a, p: a.reshape(p.shape[1:])
    rep_outs = _adamw_summed(parts, [as2d(rep_w[k], p) for k, p in zip(keys, parts)],
                             [as2d(rep_m[k], p) for k, p in zip(keys, parts)],
                             [as2d(rep_v[k], p) for k, p in zip(keys, parts)], "adamw_replicated")
    for j, key in enumerate(keys):
        res[key] = [o_.reshape(rep_w[key].shape) for o_ in rep_outs[j]]

    order = ["a_norm", "a_w_in", "a_v_norm", "a_w_s", "a_b_s", "a_w_out", "f_norm", "f_w_in", "f_conv_w", "f_conv_b",
             "f_w_out", "kv_norm", "w_kv", "k_norm", "b_norm", "b_w_q", "b_q_norm", "b_sinks", "b_w_o"]
    outs = [loss, grad_x[None]]
    for j in range(4):
        outs += [res[k][j] for k in order]
    return tuple(outs)
```

```python
import jax
import jax.numpy as jnp
from jax import lax
from jax.experimental import pallas as pl
from jax.experimental.pallas import tpu as pltpu

F32 = jnp.float32
BF16 = jnp.bfloat16
EPS = 1e-6
D_MODEL = 1024
CHUNK = 128
N_GROUPS = 8
N_SHARDS = 8
HEAD_DIM = 64
N_Q_HEADS = 16
N_KV_HEADS = 4
D_FF = 2816
FF_SHARD = 2 * D_FF // N_SHARDS
LANES = 128
NEG_BIG = -1e30
ADAM_LR = 0.001
ADAM_B1 = 0.9
ADAM_B2 = 0.999
ADAM_EPS = 1e-08
ADAM_WD = 0.01
ADAM_STEP = 10
VMEM_LIMIT_BYTES = 56 * 1024 * 1024
MESH = pl.DeviceIdType.MESH

NN = (((1,), (0,)), ((), ()))
NT = (((1,), (1,)), ((), ()))
TN = (((0,), (0,)), ((), ()))
SLOPES = tuple(2.0 ** (-8.0 * (h + 1) / N_Q_HEADS) for h in range(N_Q_HEADS))


def _params(sem=None):
    return pltpu.CompilerParams(dimension_semantics=sem, vmem_limit_bytes=VMEM_LIMIT_BYTES)


def _dot(a, b, dims=NN):
    return lax.dot_general(a, b, dims, preferred_element_type=F32)


def _sigmoid(x):
    return 1.0 / (1.0 + jnp.exp(-x))


def _gelu_parts(z):
    cdf = 0.5 * (1.0 + lax.erf(z * (2.0 ** -0.5)))
    pdf = jnp.exp(-0.5 * z * z) * 0.3989422804014327
    return cdf, pdf


def _coords():
    return lax.axis_index("x"), lax.axis_index("y"), lax.axis_index("c")


class _Gather:
    def __init__(self, srcs):
        self.srcs = list(srcs)
        n = len(self.srcs)
        self.out_shapes = [jax.ShapeDtypeStruct((N_SHARDS,) + s.shape, s.dtype) for s in self.srcs]
        self.sems = [pltpu.SemaphoreType.DMA((n, 7)), pltpu.SemaphoreType.DMA((n, 7)), pltpu.SemaphoreType.DMA((n,))]

    def _plan(self, src, dst, sems):
        send_sems, recv_sems, local_sems = sems
        x, y, c = _coords()
        me, sibling = (x, y, c), (x, y, 1 - c)
        chips = [(1 - x, y), (x, 1 - y), (1 - x, 1 - y)]
        n = len(src)

        def rows(e, dev):
            return dst[e].at[4 * dev[0] + 2 * dev[1] + dev[2]]

        def copy(e, slot, block, to, from_own=False):
            return pltpu.make_async_remote_copy(
                src_ref=src[e] if from_own else rows(e, block), dst_ref=rows(e, block),
                send_sem=send_sems.at[e, slot], recv_sem=recv_sems.at[e, slot], device_id=to, device_id_type=MESH)

        mine = [pltpu.make_async_copy(src[e], rows(e, me), local_sems.at[e]) for e in range(n)]
        first = []
        for e in range(n):
            first.append(copy(e, 0, me, sibling, from_own=True))
            first += [copy(e, 1 + j, me, (*chip, c), from_own=True) for j, chip in enumerate(chips)]
        return n, me, sibling, chips, c, copy, mine, first

    def start(self, src, dst, sems):
        _, _, _, _, _, _, mine, first = self._plan(src, dst, sems)
        for cp in mine + first:
            cp.start()

    def finish(self, src, dst, sems):
        n, me, sibling, chips, c, copy, mine, first = self._plan(src, dst, sems)
        passed = []
        for j, chip in enumerate(chips):
            for e in range(n):
                copy(e, 1 + j, (*chip, c), me).wait_recv()
                cp = copy(e, 4 + j, (*chip, c), sibling)
                cp.start()
                passed.append(cp)
        for e in range(n):
            copy(e, 0, sibling, me).wait_recv()
            for j, chip in enumerate(chips):
                copy(e, 4 + j, (*chip, 1 - c), me).wait_recv()
        for cp in first + passed:
            cp.wait_send()
        for cp in mine:
            cp.wait()


class _ToSibling:
    def __init__(self, grads):
        self.srcs = list(grads)
        n = len(self.srcs)
        self.out_shapes = [jax.ShapeDtypeStruct((4,) + g.shape[1:], g.dtype) for g in self.srcs]
        self.sems = [pltpu.SemaphoreType.DMA((n, 4)), pltpu.SemaphoreType.DMA((n, 4))]

    def _copies(self, src, dst, sems):
        send_sems, recv_sems = sems
        x, y, c = _coords()
        return [
            pltpu.make_async_remote_copy(
                src_ref=src[i].at[2 * q + (1 - c)], dst_ref=dst[i].at[q], send_sem=send_sems.at[i, q],
                recv_sem=recv_sems.at[i, q], device_id=(x, y, 1 - c), device_id_type=MESH)
            for i in range(len(src)) for q in range(4)]

    def start(self, src, dst, sems):
        for cp in self._copies(src, dst, sems):
            cp.start()

    def finish(self, src, dst, sems):
        for cp in self._copies(src, dst, sems):
            cp.wait()


class _ToChips:
    def __init__(self, psums, rows=None):
        self.srcs = list(psums)
        n = len(self.srcs)
        self.rows = rows
        self.out_shapes = [
            jax.ShapeDtypeStruct((3, p.shape[1] if rows is None else rows[1]) + p.shape[2:], p.dtype)
            for p in self.srcs]
        self.sems = [pltpu.SemaphoreType.DMA((n, 3)), pltpu.SemaphoreType.DMA((n, 3))]

    def _copies(self, src, dst, sems):
        send_sems, recv_sems = sems
        x, y, c = _coords()
        peers = [(x, 1 - y), (1 - x, y), (1 - x, 1 - y)]

        def part(i, q):
            if self.rows is None:
                return src[i].at[q]
            return src[i].at[q, pl.ds(self.rows[0], self.rows[1])]

        return [
            pltpu.make_async_remote_copy(
                src_ref=part(i, 2 * px + py), dst_ref=dst[i].at[r], send_sem=send_sems.at[i, r],
                recv_sem=recv_sems.at[i, r], device_id=(px, py, c), device_id_type=MESH)
            for i in range(len(src)) for r, (px, py) in enumerate(peers)]

    def start(self, src, dst, sems):
        for cp in self._copies(src, dst, sems):
            cp.start()

    def finish(self, src, dst, sems):
        for cp in self._copies(src, dst, sems):
            cp.wait()


class _ToOwners:
    def __init__(self, grads):
        self.srcs = list(grads)
        n = len(self.srcs)
        self.out_shapes = [jax.ShapeDtypeStruct(g.shape, g.dtype) for g in self.srcs]
        self.sems = [pltpu.SemaphoreType.DMA((n, 7)), pltpu.SemaphoreType.DMA((n, 7)), pltpu.SemaphoreType.DMA((n,))]

    def _copies(self, src, dst, sems):
        send_sems, recv_sems, local_sems = sems
        x, y, c = _coords()
        me = 4 * x + 2 * y + c
        copies = [pltpu.make_async_copy(src[i].at[me], dst[i].at[me], local_sems.at[i]) for i in range(len(src))]
        for i in range(len(src)):
            for rel in range(1, N_SHARDS):
                px = x ^ (rel >> 2) if rel >> 2 else x
                py = y ^ ((rel >> 1) & 1) if (rel >> 1) & 1 else y
                pc = c ^ (rel & 1) if rel & 1 else c
                copies.append(pltpu.make_async_remote_copy(
                    src_ref=src[i].at[4 * px + 2 * py + pc], dst_ref=dst[i].at[me], send_sem=send_sems.at[i, rel - 1],
                    recv_sem=recv_sems.at[i, rel - 1], device_id=(px, py, pc), device_id_type=MESH))
        return copies

    def start(self, src, dst, sems):
        for cp in self._copies(src, dst, sems):
            cp.start()

    def finish(self, src, dst, sems):
        for cp in self._copies(src, dst, sems):
            cp.wait()


class _Together:
    def __init__(self, parts):
        self.parts = list(parts)
        self.srcs = [s for p in self.parts for s in p.srcs]
        self.out_shapes = [s for p in self.parts for s in p.out_shapes]
        self.sems = [s for p in self.parts for s in p.sems]

    def _split(self, src, dst, sems):
        a = b = c = 0
        for p in self.parts:
            na, nc = len(p.srcs), len(p.sems)
            yield p, src[a:a + na], dst[b:b + na], sems[c:c + nc]
            a, b, c = a + na, b + na, c + nc

    def start(self, src, dst, sems):
        for p, s, d, m in self._split(src, dst, sems):
            p.start(s, d, m)

    def finish(self, src, dst, sems):
        for p, s, d, m in self._split(src, dst, sems):
            p.finish(s, d, m)

    def spread(self):
        b = 0
        for p in self.parts:
            p.results = self.results[b:b + len(p.srcs)]
            b += len(p.srcs)


def _call(body, args, *, grid, in_specs, out_specs, out_shape, name, scratch=(), sem=None, carry=None):
    out_shape, out_specs = list(out_shape), list(out_specs)
    if carry is None:
        return pl.pallas_call(
            body, grid=grid, in_specs=list(in_specs), out_specs=out_specs, out_shape=out_shape,
            scratch_shapes=list(scratch), name=name, compiler_params=_params(sem))(*args)
    n_in, n_out, n_scr, n_c = len(args), len(out_shape), len(scratch), len(carry.srcs)
    steps = tuple(grid)

    def carried(*refs):
        ins, rest = refs[:n_in], refs[n_in:]
        c_src, rest = rest[:n_c], rest[n_c:]
        outs, rest = rest[:n_out], rest[n_out:]
        c_dst, rest = rest[:n_c], rest[n_c:]
        scr, sems = rest[:n_scr], rest[n_scr:]
        first = pl.program_id(0) == 0
        last = pl.program_id(0) == steps[0] - 1
        for ax in range(1, len(steps)):
            first = first & (pl.program_id(ax) == 0)
            last = last & (pl.program_id(ax) == steps[ax] - 1)

        @pl.when(first)
        def _():
            carry.start(c_src, c_dst, sems)

        body(*ins, *outs, *scr)

        @pl.when(last)
        def _():
            carry.finish(c_src, c_dst, sems)

    hbm = pl.BlockSpec(memory_space=pl.ANY)
    res = pl.pallas_call(
        carried, grid=grid, in_specs=list(in_specs) + [hbm] * n_c, out_specs=out_specs + [hbm] * n_c,
        out_shape=out_shape + carry.out_shapes, scratch_shapes=list(scratch) + carry.sems, name=name,
        compiler_params=_params(("arbitrary",) * len(steps)))(*args, *carry.srcs)
    carry.results = list(res[n_out:])
    return list(res[:n_out])


def _exchange_alone(ex, name):
    n = len(ex.srcs)

    def body(*refs):
        src, dst, sems = refs[:n], refs[n:2 * n], refs[2 * n:]
        ex.start(src, dst, sems)
        ex.finish(src, dst, sems)

    hbm = pl.BlockSpec(memory_space=pl.ANY)
    res = pl.pallas_call(body, in_specs=[hbm] * n, out_specs=[hbm] * n, out_shape=ex.out_shapes,
                         scratch_shapes=ex.sems, name=name)(*ex.srcs)
    ex.results = list(res)
    return ex.results


def _rms_fwd(x, gains, name, tm=512, carry=None):
    t, d = x.shape
    n = len(gains)

    def body(*refs):
        x_ref, g_refs, h_refs = refs[0], refs[1:1 + n], refs[1 + n:]
        xf = x_ref[...]
        xhat = xf * lax.rsqrt(jnp.mean(xf * xf, axis=-1, keepdims=True) + EPS)
        for g_ref, h_ref in zip(g_refs, h_refs):
            h_ref[...] = (xhat * g_ref[...]).astype(BF16)

    row = pl.BlockSpec((tm, d), lambda i: (i, 0))
    vec = pl.BlockSpec((1, d), lambda i: (0, 0))
    return _call(body, [x, *gains], grid=(t // tm,), in_specs=[row] + [vec] * n, out_specs=[row] * n,
                 out_shape=[jax.ShapeDtypeStruct((t, d), BF16)] * n, name=name, carry=carry)


def _rms_bwd(x, gains, dhs, dres, name, tm=256, carry=None):
    t, d = x.shape
    n = len(gains)

    def body(*refs):
        x_ref, dres_ref = refs[0], refs[1]
        g_refs, dh_refs = refs[2:2 + n], refs[2 + n:2 + 2 * n]
        dx_ref, dg_ref = refs[2 + 2 * n], refs[3 + 2 * n]
        i = pl.program_id(0)

        @pl.when(i == 0)
        def _():
            dg_ref[...] = jnp.zeros_like(dg_ref)

        xf = x_ref[...]
        r = lax.rsqrt(jnp.mean(xf * xf, axis=-1, keepdims=True) + EPS)
        xhat = xf * r
        dx = dres_ref[...]
        for j in range(n):
            dh = dh_refs[j][...]
            dg_ref[j:j + 1, :] += jnp.sum(dh * xhat, axis=0, keepdims=True)
            gy = dh * g_refs[j][...]
            dx = dx + r * (gy - xhat * jnp.mean(gy * xhat, axis=-1, keepdims=True))
        dx_ref[...] = dx

    row = pl.BlockSpec((tm, d), lambda i: (i, 0))
    vec = pl.BlockSpec((1, d), lambda i: (0, 0))
    return _call(body, [x, dres, *gains, *dhs], grid=(t // tm,), in_specs=[row, row] + [vec] * n + [row] * n,
                 out_specs=[row, pl.BlockSpec((8, d), lambda i: (0, 0))],
                 out_shape=[jax.ShapeDtypeStruct((t, d), F32), jax.ShapeDtypeStruct((8, d), F32)],
                 name=name, sem=("arbitrary",), carry=carry)


def _mm(a, b, a_spec, b_spec, o_spec, out_shape, grid, dims, name, res=None, res_spec=None, carry=None):
    nk = grid[2]
    acc_shape = tuple(s for s in o_spec.block_shape if s is not None)

    def body(*refs):
        a_ref, b_ref = refs[0], refs[1]
        r_ref = refs[2] if res is not None else None
        o_ref = refs[3] if res is not None else refs[2]
        p = _dot(a_ref[...].astype(BF16), b_ref[...].astype(BF16), dims)
        if nk == 1:
            if res is not None:
                p = p + r_ref[...]
            o_ref[...] = p.astype(o_ref.dtype)
            return
        acc_ref = refs[-1]
        k = pl.program_id(2)

        @pl.when(k == 0)
        def _():
            acc_ref[...] = p

        @pl.when(k > 0)
        def _():
            acc_ref[...] += p

        @pl.when(k == nk - 1)
        def _():
            out = acc_ref[...]
            if res is not None:
                out = out + r_ref[...]
            o_ref[...] = out.astype(o_ref.dtype)

    ins = [a, b] + ([res] if res is not None else [])
    specs = [a_spec, b_spec] + ([res_spec] if res is not None else [])
    return _call(body, ins, grid=grid, in_specs=specs, out_specs=[o_spec], out_shape=[out_shape],
                 scratch=[pltpu.VMEM(acc_shape, F32)] if nk > 1 else [], name=name,
                 sem=("parallel", "parallel", "arbitrary"), carry=carry)[0]


def _mm_rows(a, w, out_dtype, name, trans_w=False, res=None, tm=512, carry=None):
    t, k = a.shape
    n = w.shape[0] if trans_w else w.shape[1]
    return _mm(
        a, w, pl.BlockSpec((tm, k), lambda i, j, kk: (i, 0)), pl.BlockSpec(w.shape, lambda i, j, kk: (0, 0)),
        pl.BlockSpec((tm, n), lambda i, j, kk: (i, 0)), jax.ShapeDtypeStruct((t, n), out_dtype), (t // tm, 1, 1),
        NT if trans_w else NN, name, res=res,
        res_spec=None if res is None else pl.BlockSpec((tm, n), lambda i, j, kk: (i, 0)), carry=carry)


def _mm_wgrad(a, b, name, carry=None):
    t, m = a.shape
    n = b.shape[1]
    tn = n // (4 if b.dtype == F32 else 2)
    return _mm(
        a, b, pl.BlockSpec((t, m), lambda i, j, kk: (0, 0)), pl.BlockSpec((t, tn), lambda i, j, kk: (0, j)),
        pl.BlockSpec((m, tn), lambda i, j, kk: (0, j)), jax.ShapeDtypeStruct((m, n), F32), (1, n // tn, 1), TN, name,
        carry=carry)


def _sgu_fwd(x0, h1, w_in, g_v, w_c, b_sb, w_out, tm=256, carry=None):
    t, d = x0.shape
    nsub = w_in.shape[2]

    def body(x_ref, h_ref, win_ref, gv_ref, wc_ref, bsb_ref, wout_ref, zpre_ref, x1_ref, u_s, v_s, vn_s, y_s):
        h = h_ref[...]
        for k in range(N_SHARDS):
            zk = _dot(h, win_ref[k])
            zpre_ref[:, k * nsub:(k + 1) * nsub] = zk
            cdf, _ = _gelu_parts(zk)
            if k < N_SHARDS // 2:
                u_s[:, k * nsub:(k + 1) * nsub] = zk * cdf
            else:
                v_s[:, (k - 4) * nsub:(k - 3) * nsub] = zk * cdf
        v = v_s[...]
        rv = lax.rsqrt(jnp.mean(v * v, axis=-1, keepdims=True) + EPS)
        vn_s[...] = (v * rv * gv_ref[...]).astype(BF16)
        for ci in range(tm // CHUNK):
            rows = slice(ci * CHUNK, (ci + 1) * CHUNK)
            for g in range(N_GROUPS):
                cols = slice(g * LANES, (g + 1) * LANES)
                sv = _dot(wc_ref[g], vn_s[rows, cols]) + bsb_ref[g]
                y_s[rows, cols] = (u_s[rows, cols] * sv).astype(BF16)
        x1_ref[...] = x_ref[...] + _dot(y_s[...], wout_ref[...])

    row = pl.BlockSpec((tm, d), lambda i: (i, 0))
    full = lambda a: pl.BlockSpec(a.shape, lambda i: (0,) * a.ndim)
    return _call(
        body, [x0, h1, w_in, g_v, w_c, b_sb, w_out], grid=(t // tm,),
        in_specs=[row, row, full(w_in), full(g_v), full(w_c), full(b_sb), full(w_out)],
        out_specs=[pl.BlockSpec((tm, 2 * d), lambda i: (i, 0)), row],
        out_shape=[jax.ShapeDtypeStruct((t, 2 * d), F32), jax.ShapeDtypeStruct((t, d), F32)],
        scratch=[pltpu.VMEM((tm, d), F32), pltpu.VMEM((tm, d), F32), pltpu.VMEM((tm, d), BF16),
                 pltpu.VMEM((tm, d), BF16)],
        name="sgu_fwd", carry=carry)


def _sgu_bwd(dx1, zpre, w_out, g_v, w_c, w_ct, b_sb, tm=256, carry=None):
    t, d = dx1.shape

    def body(dx_ref, zpre_ref, wout_ref, gv_ref, wc_ref, wct_ref, bsb_ref,
             dz_ref, y_ref, dwc_ref, dbs_ref, dgv_ref, u_s, vn_s, dy_s, du_s, dvn_s):
        i = pl.program_id(0)

        @pl.when(i == 0)
        def _():
            dwc_ref[...] = jnp.zeros_like(dwc_ref)
            dbs_ref[...] = jnp.zeros_like(dbs_ref)
            dgv_ref[...] = jnp.zeros_like(dgv_ref)

        dy_s[...] = _dot(dx_ref[...].astype(BF16), wout_ref[...], NT)
        zu = zpre_ref[:, :d]
        zv = zpre_ref[:, d:]
        cdf_u, pdf_u = _gelu_parts(zu)
        cdf_v, pdf_v = _gelu_parts(zv)
        u_s[...] = zu * cdf_u
        v = zv * cdf_v
        rv = lax.rsqrt(jnp.mean(v * v, axis=-1, keepdims=True) + EPS)
        vhat = v * rv
        gv = gv_ref[...]
        vn_s[...] = (vhat * gv).astype(BF16)
        for ci in range(tm // CHUNK):
            rows = slice(ci * CHUNK, (ci + 1) * CHUNK)
            for g in range(N_GROUPS):
                cols = slice(g * LANES, (g + 1) * LANES)
                vnb = vn_s[rows, cols]
                sv = _dot(wc_ref[g], vnb) + bsb_ref[g]
                dyb = dy_s[rows, cols]
                ub = u_s[rows, cols]
                dsv = dyb * ub
                du_s[rows, cols] = dyb * sv
                y_ref[rows, cols] = (ub * sv).astype(BF16)
                dsvb = dsv.astype(BF16)
                dbs_ref[g] += dsv
                dwc_ref[g] += _dot(dsvb, vnb, NT)
                dvn_s[rows, cols] = _dot(wct_ref[g], dsvb)
        dvn = dvn_s[...]
        dgv_ref[0:1, :] += jnp.sum(dvn * vhat, axis=0, keepdims=True)
        gy = dvn * gv
        dv = rv * (gy - vhat * jnp.mean(gy * vhat, axis=-1, keepdims=True))
        dz_ref[:, :d] = (du_s[...] * (cdf_u + zu * pdf_u)).astype(BF16)
        dz_ref[:, d:] = (dv * (cdf_v + zv * pdf_v)).astype(BF16)

        @pl.when(i == t // tm - 1)
        def _():
            tri = (lax.broadcasted_iota(jnp.int32, (CHUNK, CHUNK), 0)
                   >= lax.broadcasted_iota(jnp.int32, (CHUNK, CHUNK), 1))
            for g in range(N_GROUPS):
                dwc_ref[g] = jnp.where(tri, dwc_ref[g], 0.0)
                dbs_ref[g] = jnp.broadcast_to(jnp.sum(dbs_ref[g], axis=1, keepdims=True), (CHUNK, CHUNK))

    row = pl.BlockSpec((tm, d), lambda i: (i, 0))
    row2 = pl.BlockSpec((tm, 2 * d), lambda i: (i, 0))
    full = lambda a: pl.BlockSpec(a.shape, lambda i: (0,) * a.ndim)
    grp = pl.BlockSpec((N_GROUPS, CHUNK, CHUNK), lambda i: (0, 0, 0))
    return _call(
        body, [dx1, zpre, w_out, g_v, w_c, w_ct, b_sb], grid=(t // tm,),
        in_specs=[row, row2, full(w_out), full(g_v), full(w_c), full(w_ct), full(b_sb)],
        out_specs=[row2, row, grp, grp, pl.BlockSpec((8, d), lambda i: (0, 0))],
        out_shape=[jax.ShapeDtypeStruct((t, 2 * d), BF16), jax.ShapeDtypeStruct((t, d), BF16),
                   jax.ShapeDtypeStruct((N_GROUPS, CHUNK, CHUNK), F32),
                   jax.ShapeDtypeStruct((N_GROUPS, CHUNK, CHUNK), F32), jax.ShapeDtypeStruct((8, d), F32)],
        scratch=[pltpu.VMEM((tm, d), F32), pltpu.VMEM((tm, d), BF16), pltpu.VMEM((tm, d), F32),
                 pltpu.VMEM((tm, d), F32), pltpu.VMEM((tm, d), F32)],
        name="sgu_bwd", sem=("arbitrary",), carry=carry)


ROW_CHUNK = 256
HALO = 16


def _causal_conv(a_ref, prev_ref, cw, cb, r0, keep, nrows=ROW_CHUNK):
    if r0 == 0:
        win = jnp.concatenate([prev_ref[...].astype(F32) * keep, a_ref[0:nrows, :].astype(F32)], axis=0)
    else:
        win = a_ref[r0 - HALO:r0 + nrows, :].astype(F32)
    a0 = win[HALO:]
    a1 = pltpu.roll(win, 1, 0)[HALO:]
    a2 = pltpu.roll(win, 2, 0)[HALO:]
    hu = cw[2:3, :] * a0 + cw[1:2, :] * a1 + cw[0:1, :] * a2 + cb
    return hu, a0, a1, a2


def _ffn_conv_specs(tm, gate_of, tile_of):
    def specs(shard_of):
        return [
            pl.BlockSpec((None, tm, FF_SHARD), lambda *g: (shard_of(*g), tile_of(*g), 0)),
            pl.BlockSpec((None, 16, FF_SHARD),
                         lambda *g: (shard_of(*g), jnp.maximum(tile_of(*g) * (tm // 16) - 1, 0), 0)),
            pl.BlockSpec((None, 8, FF_SHARD), lambda *g: (shard_of(*g), 0, 0)),
            pl.BlockSpec((None, 1, FF_SHARD), lambda *g: (shard_of(*g), 0, 0)),
        ]
    return specs(gate_of) + specs(lambda *g: gate_of(*g) + N_SHARDS // 2)


def _ffn_fwd(x, g, w_in, cw, cb, w_out, layer, tm=512, carry=None):
    t, d = x.shape
    nc = N_SHARDS // 2

    def body(x_ref, xp_ref, g_ref, wg_ref, wu_ref, cwg_ref, cbg_ref, cwu_ref, cbu_ref, wout_ref,
             o_ref, hf_ref, a_ref, hw_s):
        i, c = pl.program_id(0), pl.program_id(1)

        @pl.when(c == 0)
        def _():
            keep = jnp.where(i == 0, 0.0, 1.0)
            xw = jnp.concatenate([xp_ref[...] * keep, x_ref[...]], axis=0)
            xhat = xw * lax.rsqrt(jnp.mean(xw * xw, axis=-1, keepdims=True) + EPS)
            hw_s[...] = (xhat * g_ref[...]).astype(BF16)
            hf_ref[...] = hw_s[HALO:, :]
            o_ref[...] = x_ref[...]

        hw = hw_s[...]
        pre = []
        for j, (w_ref, cw_ref, cb_ref) in enumerate(((wg_ref, cwg_ref, cbg_ref), (wu_ref, cwu_ref, cbu_ref))):
            ab = _dot(hw, w_ref[...]).astype(BF16)
            a_ref[j] = ab[HALO:]
            win = ab.astype(F32)
            cw_v = cw_ref[...]
            pre.append(cw_v[2:3, :] * win[HALO:] + cw_v[1:2, :] * pltpu.roll(win, 1, 0)[HALO:]
                       + cw_v[0:1, :] * pltpu.roll(win, 2, 0)[HALO:] + cb_ref[...])
        act = (pre[0] * _sigmoid(pre[0]) * pre[1]).astype(BF16)
        o_ref[...] += _dot(act, wout_ref[...])

    row = pl.BlockSpec((tm, d), lambda i, c: (i, 0))
    shard = lambda rows, up: pl.BlockSpec((None, rows, FF_SHARD), lambda i, c: (c + up * nc, 0, 0))
    outs = _call(
        body, [x, x, g, w_in, w_in, cw, cb, cw, cb, w_out], grid=(t // tm, nc),
        in_specs=[row, pl.BlockSpec((HALO, d), lambda i, c: (jnp.maximum(i * (tm // HALO) - 1, 0), 0)),
                  pl.BlockSpec((1, d), lambda i, c: (0, 0)), shard(d, 0), shard(d, 1),
                  shard(8, 0), shard(1, 0), shard(8, 1), shard(1, 1), pl.BlockSpec((FF_SHARD, d), lambda i, c: (c, 0))],
        out_specs=[row, row, pl.BlockSpec((2, None, tm, FF_SHARD), lambda i, c: (0, c, i, 0))],
        out_shape=[jax.ShapeDtypeStruct((t, d), F32), jax.ShapeDtypeStruct((t, d), BF16),
                   jax.ShapeDtypeStruct((2, nc, t, FF_SHARD), BF16)],
        scratch=[pltpu.VMEM((tm + HALO, d), BF16)], name=f"ffn{layer}_fwd", sem=("parallel", "arbitrary"), carry=carry)
    return outs[0], outs[1], outs[2].reshape(N_SHARDS, t, FF_SHARD)


def _ffn_bwd_act(a, cw, cb, w_out, dxn, layer, tm=512, carry=None):
    t, d = dxn.shape
    nc = N_SHARDS // 2

    def body(ag_ref, pg_ref, cwg_ref, cbg_ref, au_ref, pu_ref, cwu_ref, cbu_ref, wout_ref, dx_ref,
             dhu_ref, dw_ref, dconv_ref):
        i = pl.program_id(1)

        @pl.when(i == 0)
        def _():
            dw_ref[...] = jnp.zeros_like(dw_ref)
            dconv_ref[...] = jnp.zeros_like(dconv_ref)

        keep = jnp.where(i == 0, 0.0, 1.0)
        cwg, cbg, cwu, cbu = cwg_ref[...], cbg_ref[...], cwu_ref[...], cbu_ref[...]
        hg, ag0, ag1, ag2 = _causal_conv(ag_ref, pg_ref, cwg, cbg, 0, keep, tm)
        hu, au0, au1, au2 = _causal_conv(au_ref, pu_ref, cwu, cbu, 0, keep, tm)
        sg = _sigmoid(hg)
        sl = hg * sg
        dxb = dx_ref[...].astype(BF16)
        dact = _dot(dxb, wout_ref[...], NT)
        dw_ref[...] += _dot((sl * hu).astype(BF16), dxb, TN)
        d_up = dact * sl
        d_gate = dact * hu * (sg * (1.0 + hg * (1.0 - sg)))
        for j, (dv, taps) in enumerate(((d_gate, (ag2, ag1, ag0)), (d_up, (au2, au1, au0)))):
            dvb = dv.astype(BF16)
            dhu_ref[j] = dvb
            dvr = dvb.astype(F32)
            for k in range(3):
                dconv_ref[j, k:k + 1, :] += jnp.sum(dvr * taps[k], axis=0, keepdims=True)
            dconv_ref[j, 3:4, :] += jnp.sum(dv, axis=0, keepdims=True)

    return _call(
        body, [a, a, cw, cb, a, a, cw, cb, w_out, dxn], grid=(nc, t // tm),
        in_specs=_ffn_conv_specs(tm, lambda c, i: c, lambda c, i: i)
        + [pl.BlockSpec((FF_SHARD, d), lambda c, i: (c, 0)), pl.BlockSpec((tm, d), lambda c, i: (i, 0))],
        out_specs=[pl.BlockSpec((None, 2, tm, FF_SHARD), lambda c, i: (c, 0, i, 0)),
                   pl.BlockSpec((FF_SHARD, d), lambda c, i: (c, 0)),
                   pl.BlockSpec((None, 2, 8, FF_SHARD), lambda c, i: (c, 0, 0, 0))],
        out_shape=[jax.ShapeDtypeStruct((nc, 2, t, FF_SHARD), BF16), jax.ShapeDtypeStruct((D_FF, d), F32),
                   jax.ShapeDtypeStruct((nc, 2, 8, FF_SHARD), F32)],
        name=f"ffn{layer}_bwd_act", sem=("parallel", "arbitrary"), carry=carry)


def _ffn_bwd_in(dhu, cw, w_in, layer, tm=1024, carry=None):
    nc, _, t, _ = dhu.shape
    d = D_MODEL
    tm = min(tm, t)
    last_blk = t // 16 - 1

    def body(dh_ref, nx_ref, cw_ref, win_ref, da_ref, o_ref):
        i, s = pl.program_id(0), pl.program_id(1)

        @pl.when(s == 0)
        def _():
            o_ref[...] = jnp.zeros_like(o_ref)

        keep = jnp.where(i == t // tm - 1, 0.0, 1.0)
        cw = cw_ref[...]
        for r0 in range(0, tm, ROW_CHUNK):
            rows = slice(r0, r0 + ROW_CHUNK)
            if r0 + ROW_CHUNK == tm:
                win = jnp.concatenate([dh_ref[rows, :].astype(F32), nx_ref[...].astype(F32) * keep], axis=0)
            else:
                win = dh_ref[r0:r0 + ROW_CHUNK + HALO, :].astype(F32)
            n = ROW_CHUNK + HALO
            d1 = pltpu.roll(win, n - 1, 0)[:ROW_CHUNK]
            d2 = pltpu.roll(win, n - 2, 0)[:ROW_CHUNK]
            da = (cw[2:3, :] * win[:ROW_CHUNK] + cw[1:2, :] * d1 + cw[0:1, :] * d2).astype(BF16)
            da_ref[rows, :] = da
            o_ref[rows, :] += _dot(da, win_ref[...], NT)

    return _call(
        body, [dhu, dhu, cw, w_in], grid=(t // tm, N_SHARDS),
        in_specs=[pl.BlockSpec((None, None, tm, FF_SHARD), lambda i, s: (s % nc, s // nc, i, 0)),
                  pl.BlockSpec((None, None, 16, FF_SHARD),
                               lambda i, s: (s % nc, s // nc, jnp.minimum((i + 1) * (tm // 16), last_blk), 0)),
                  pl.BlockSpec((None, 8, FF_SHARD), lambda i, s: (s, 0, 0)),
                  pl.BlockSpec((None, d, FF_SHARD), lambda i, s: (s, 0, 0))],
        out_specs=[pl.BlockSpec((None, tm, FF_SHARD), lambda i, s: (s, i, 0)),
                   pl.BlockSpec((tm, d), lambda i, s: (i, 0))],
        out_shape=[jax.ShapeDtypeStruct((N_SHARDS, t, FF_SHARD), BF16), jax.ShapeDtypeStruct((t, d), F32)],
        name=f"ffn{layer}_bwd_in", sem=("parallel", "arbitrary"), carry=carry)


def _ffn_wgrad_in(hf, da, layer, carry=None):
    t, d = hf.shape
    return _mm(
        hf, da, pl.BlockSpec((t, d), lambda s, j, kk: (0, 0)),
        pl.BlockSpec((None, t, FF_SHARD), lambda s, j, kk: (s, 0, 0)),
        pl.BlockSpec((None, d, FF_SHARD), lambda s, j, kk: (s, 0, 0)),
        jax.ShapeDtypeStruct((N_SHARDS, d, FF_SHARD), F32), (N_SHARDS, 1, 1), TN, f"ffn{layer}_wgrad_in",
        carry=carry)


Q_PER_KV = N_Q_HEADS // N_KV_HEADS
GROUP_ROWS = Q_PER_KV * CHUNK


def _attn_masks(n):
    lane = lax.broadcasted_iota(jnp.int32, (CHUNK, LANES), 1)
    lo = lane < HEAD_DIM
    tq = lax.broadcasted_iota(jnp.int32, (GROUP_ROWS, 2 * CHUNK), 0) & (CHUNK - 1)
    jk = lax.broadcasted_iota(jnp.int32, (GROUP_ROWS, 2 * CHUNK), 1)
    dist = tq + CHUNK - jk
    mask = (dist >= 0) & (dist < CHUNK) & (jk >= jnp.where(n == 0, CHUNK, 0))
    return lo, mask, dist.astype(F32)


def _per_head_column(values):
    r = lax.broadcasted_iota(jnp.int32, (GROUP_ROWS, 1), 0)
    col = jnp.full((GROUP_ROWS, 1), values[Q_PER_KV - 1], F32)
    for j in range(Q_PER_KV - 2, -1, -1):
        col = jnp.where(r < (j + 1) * CHUNK, values[j], col)
    return col


def _half_sum(x, lo):
    s_lo = jnp.sum(jnp.where(lo, x, 0.0), axis=-1, keepdims=True)
    s_hi = jnp.sum(jnp.where(lo, 0.0, x), axis=-1, keepdims=True)
    return jnp.where(lo, s_lo, s_hi)


def _stack_heads(pairs, lo):
    zero = jnp.zeros_like(pairs[0])
    return jnp.concatenate([jnp.where(lo, pairs[0], zero), jnp.where(lo, zero, pairs[0]),
                            jnp.where(lo, pairs[1], zero), jnp.where(lo, zero, pairs[1])], axis=0)


def _unstack_heads(stacked, lo):
    return (jnp.where(lo, stacked[0:CHUNK], stacked[CHUNK:2 * CHUNK]),
            jnp.where(lo, stacked[2 * CHUNK:3 * CHUNK], stacked[3 * CHUNK:]))


def _attn_probs(qs, kn, mask, distf, slope_col, sink_col):
    s = _dot(qs, kn, NT) * (HEAD_DIM ** -0.5)
    s = jnp.where(mask, s - slope_col * distf, NEG_BIG)
    m = jnp.maximum(jnp.max(s, axis=-1, keepdims=True), sink_col)
    e = jnp.exp(s - m)
    den = jnp.sum(e, axis=-1, keepdims=True) + jnp.exp(sink_col - m)
    return e * (1.0 / den), m, den


def _attn_fwd(qraw, kvd, gq, gk, sinks, carry=None):
    t, d = qraw.shape
    nb = t // CHUNK

    def body(sink_ref, q_ref, cur_ref, prev_ref, gq_ref, gk_ref, o_ref):
        n = pl.program_id(0)
        lo, mask, distf = _attn_masks(n)
        gq_v, gk_v = gq_ref[...], gk_ref[...]
        for kvh in range(N_KV_HEADS):
            ks = slice(kvh * LANES, (kvh + 1) * LANES)
            vs = slice(4 * LANES + kvh * LANES, 4 * LANES + (kvh + 1) * LANES)
            kraw = jnp.concatenate([prev_ref[:, ks], cur_ref[:, ks]], axis=0)
            rk = lax.rsqrt(jnp.mean(kraw * kraw, axis=-1, keepdims=True) + EPS)
            kn = (kraw * rk * gk_v).astype(BF16)
            vv = jnp.concatenate([prev_ref[:, vs], cur_ref[:, vs]], axis=0).astype(BF16)
            qn = []
            for p in range(2):
                qp = q_ref[:, (2 * kvh + p) * LANES:(2 * kvh + p + 1) * LANES]
                r = lax.rsqrt(_half_sum(qp * qp, lo) * (1.0 / HEAD_DIM) + EPS)
                qn.append(qp * r * gq_v)
            heads = range(Q_PER_KV * kvh, Q_PER_KV * (kvh + 1))
            pf, _, _ = _attn_probs(_stack_heads(qn, lo).astype(BF16), kn, mask, distf,
                                   _per_head_column([SLOPES[h] for h in heads]),
                                   _per_head_column([sink_ref[h] for h in heads]))
            for p, o_pair in enumerate(_unstack_heads(_dot(pf.astype(BF16), vv), lo)):
                o_ref[:, (2 * kvh + p) * LANES:(2 * kvh + p + 1) * LANES] = o_pair.astype(BF16)

    blk = lambda f: pl.BlockSpec((CHUNK, d), f)
    vec = pl.BlockSpec((1, LANES), lambda n: (0, 0))
    return _call(
        body, [sinks, qraw, kvd, kvd, gq, gk], grid=(nb,),
        in_specs=[pl.BlockSpec(memory_space=pltpu.SMEM), blk(lambda n: (n, 0)), blk(lambda n: (n, 0)),
                  blk(lambda n: (jnp.maximum(n - 1, 0), 0)), vec, vec],
        out_specs=[blk(lambda n: (n, 0))], out_shape=[jax.ShapeDtypeStruct((t, d), BF16)],
        name="attn_fwd", carry=carry)[0]


def _attn_bwd(qraw, kvd, d_o, gq, gk, sinks, carry=None):
    t, d = qraw.shape
    nb = t // CHUNK

    def body(sink_ref, q_ref, cur_ref, prev_ref, do_ref, gq_ref, gk_ref,
             dq_ref, dkv_ref, dsink_ref, dgq_ref, dgk_ref, carry_s, pp_s, cp_s):
        n = pl.program_id(0)

        @pl.when(n == 0)
        def _():
            carry_s[...] = jnp.zeros_like(carry_s)
            dsink_ref[...] = jnp.zeros_like(dsink_ref)
            dgq_ref[...] = jnp.zeros_like(dgq_ref)
            dgk_ref[...] = jnp.zeros_like(dgk_ref)

        @pl.when(n < nb)
        def _():
            lo, mask, distf = _attn_masks(n)
            gq_v, gk_v = gq_ref[...], gk_ref[...]
            for kvh in range(N_KV_HEADS):
                ks = slice(kvh * LANES, (kvh + 1) * LANES)
                vs = slice(4 * LANES + kvh * LANES, 4 * LANES + (kvh + 1) * LANES)
                kraw = jnp.concatenate([prev_ref[:, ks], cur_ref[:, ks]], axis=0)
                rk = lax.rsqrt(jnp.mean(kraw * kraw, axis=-1, keepdims=True) + EPS)
                khat = kraw * rk
                kn = (khat * gk_v).astype(BF16)
                vv = jnp.concatenate([prev_ref[:, vs], cur_ref[:, vs]], axis=0).astype(BF16)
                cols = [slice((2 * kvh + p) * LANES, (2 * kvh + p + 1) * LANES) for p in range(2)]
                rq, qhat = [], []
                for p in range(2):
                    qp = q_ref[:, cols[p]]
                    rq.append(lax.rsqrt(_half_sum(qp * qp, lo) * (1.0 / HEAD_DIM) + EPS))
                    qhat.append(qp * rq[p])
                heads = range(Q_PER_KV * kvh, Q_PER_KV * (kvh + 1))
                qs = _stack_heads([qhat[p] * gq_v for p in range(2)], lo).astype(BF16)
                dos = _stack_heads([do_ref[:, cols[p]] for p in range(2)], lo)
                sink_col = _per_head_column([sink_ref[h] for h in heads])
                pf, m, den = _attn_probs(qs, kn, mask, distf, _per_head_column([SLOPES[h] for h in heads]), sink_col)
                dp = _dot(dos, vv, NT)
                delta = jnp.sum(pf * dp, axis=-1, keepdims=True)
                sink_delta = jnp.exp(sink_col - m) / den * delta
                for j, h in enumerate(heads):
                    dsink_ref[h:h + 1, :] -= jnp.broadcast_to(
                        jnp.sum(sink_delta[j * CHUNK:(j + 1) * CHUNK], axis=0, keepdims=True), (1, LANES))
                ds = (pf * (dp - delta) * (HEAD_DIM ** -0.5)).astype(BF16)
                dkn = _dot(ds, qs, TN)
                dvb = _dot(pf.astype(BF16), dos, TN)
                for p, dqn in enumerate(_unstack_heads(_dot(ds, kn), lo)):
                    dgq_ref[0:1, :] += jnp.sum(dqn * qhat[p], axis=0, keepdims=True)
                    gy = dqn * gq_v
                    mq = _half_sum(gy * qhat[p], lo) * (1.0 / HEAD_DIM)
                    dq_ref[:, cols[p]] = (rq[p] * (gy - qhat[p] * mq)).astype(BF16)
                dgk_ref[0:1, :] += jnp.sum(dkn * khat, axis=0, keepdims=True)
                gyk = dkn * gk_v
                dkraw = rk * (gyk - khat * jnp.mean(gyk * khat, axis=-1, keepdims=True))
                pp_s[:, ks] = dkraw[:CHUNK]
                cp_s[:, ks] = dkraw[CHUNK:]
                pp_s[:, vs] = dvb[:CHUNK]
                cp_s[:, vs] = dvb[CHUNK:]
            dkv_ref[...] = (carry_s[...] + pp_s[...]).astype(BF16)
            carry_s[...] = cp_s[...]

        @pl.when(n == nb)
        def _():
            dkv_ref[...] = carry_s[...].astype(BF16)

    blk = lambda f: pl.BlockSpec((CHUNK, d), f)
    vec = pl.BlockSpec((1, LANES), lambda n: (0, 0))
    cur = lambda n: (jnp.minimum(n, nb - 1), 0)
    prev = lambda n: (jnp.maximum(jnp.minimum(n, nb - 1) - 1, 0), 0)
    small = lambda r: pl.BlockSpec((r, LANES), lambda n: (0, 0))
    return _call(
        body, [sinks, qraw, kvd, kvd, d_o, gq, gk], grid=(nb + 1,),
        in_specs=[pl.BlockSpec(memory_space=pltpu.SMEM), blk(cur), blk(cur), blk(prev), blk(cur), vec, vec],
        out_specs=[blk(cur), blk(lambda n: (jnp.maximum(n - 1, 0), 0)), small(N_Q_HEADS), small(8), small(8)],
        out_shape=[jax.ShapeDtypeStruct((t, d), BF16), jax.ShapeDtypeStruct((t, d), BF16),
                   jax.ShapeDtypeStruct((N_Q_HEADS, LANES), F32), jax.ShapeDtypeStruct((8, LANES), F32),
                   jax.ShapeDtypeStruct((8, LANES), F32)],
        scratch=[pltpu.VMEM((CHUNK, d), F32)] * 3, name="attn_bwd", sem=("arbitrary",), carry=carry)


def _loss_head(y, target, tm=512):
    t, d = y.shape

    def body(y_ref, t_ref, dy_ref, loss_ref, acc_ref):
        i = pl.program_id(0)

        @pl.when(i == 0)
        def _():
            acc_ref[...] = jnp.zeros_like(acc_ref)

        err = y_ref[...] - t_ref[...]
        dy_ref[...] = err * (1.0 / d)
        acc_ref[...] += jnp.sum(err * err, axis=0, keepdims=True)

        @pl.when(i == t // tm - 1)
        def _():
            loss_ref[...] = jnp.broadcast_to(0.5 / d * jnp.sum(acc_ref[...], axis=1, keepdims=True), loss_ref.shape)

    row = pl.BlockSpec((tm, d), lambda i: (i, 0))
    return _call(
        body, [y, target], grid=(t // tm,), in_specs=[row, row],
        out_specs=[row, pl.BlockSpec((8, LANES), lambda i: (0, 0))],
        out_shape=[jax.ShapeDtypeStruct((t, d), F32), jax.ShapeDtypeStruct((8, LANES), F32)],
        scratch=[pltpu.VMEM((1, d), F32)], name="loss_head", sem=("arbitrary",))


def _adamw_math(g, w, m, v):
    m = ADAM_B1 * m + (1.0 - ADAM_B1) * g
    v = ADAM_B2 * v + (1.0 - ADAM_B2) * (g * g)
    m_hat = m / (1.0 - ADAM_B1 ** ADAM_STEP)
    v_hat = v / (1.0 - ADAM_B2 ** ADAM_STEP)
    delta = -ADAM_LR * (m_hat / (jnp.sqrt(v_hat) + ADAM_EPS) + ADAM_WD * w)
    return delta, m, v


def _row_tile(r, cap=128):
    for tr in range(min(r, cap), 0, -1):
        if r % tr == 0 and (tr % 8 == 0 or tr == r):
            return tr
    return r


def _chip_sum(grad, recv, place, name, wire_dtype):
    _, r, c = grad.shape
    tr = _row_tile(r, 256)

    def body(pl_ref, g_ref, a_ref, p_ref):
        p_ref[...] = (g_ref[...] + a_ref[...]).astype(p_ref.dtype)

    return pl.pallas_call(
        body,
        grid_spec=pltpu.PrefetchScalarGridSpec(
            num_scalar_prefetch=1, grid=(4, r // tr),
            in_specs=[pl.BlockSpec((None, None, tr, c), lambda q, i, pr: (q, pr[1], i, 0)),
                      pl.BlockSpec((None, tr, c), lambda q, i, pr: (q, i, 0))],
            out_specs=pl.BlockSpec((None, tr, c), lambda q, i, pr: (q, i, 0))),
        out_shape=jax.ShapeDtypeStruct((4, r, c), wire_dtype), name=name, compiler_params=_params(),
    )(place, grad.reshape(4, 2, r, c), recv)


def _adamw_sharded(grad, recv, others, place, w, m, v, name, layer=None, fill=None):
    r, c = w.shape[-2:]
    tr = _row_tile(r)

    def body(pl_ref, g_ref, a_ref, oth_ref, w_ref, m_ref, v_ref, *rest):
        g_out, d_out, nm_out, nv_out = rest[-4:]
        g = g_ref[...] + a_ref[...]
        for k in range(3):
            g = g + oth_ref[k].astype(F32)
        delta, nm, nv = _adamw_math(g, w_ref[...], m_ref[...], v_ref[...])
        g_out[...] = g
        d_out[...] = delta
        nm_out[...] = nm
        nv_out[...] = nv

    if layer is None:
        row = pl.BlockSpec((tr, c), lambda i, pr: (i, 0))
    else:
        row = pl.BlockSpec((None, tr, c), lambda i, pr: (layer, i, 0))
    n_fill = 0 if fill is None else 4
    in_specs = [pl.BlockSpec((None, None, tr, c), lambda i, pr: (pr[0], pr[1], i, 0)),
                pl.BlockSpec((None, tr, c), lambda i, pr: (pr[0], i, 0)),
                pl.BlockSpec((3, tr, c), lambda i, pr: (0, i, 0)), row, row, row]
    in_specs += [pl.BlockSpec(memory_space=pl.ANY)] * n_fill
    return pl.pallas_call(
        body,
        grid_spec=pltpu.PrefetchScalarGridSpec(
            num_scalar_prefetch=1, grid=(r // tr,), in_specs=in_specs, out_specs=[row] * 4),
        out_shape=[jax.ShapeDtypeStruct(w.shape, F32)] * 4, name=name, compiler_params=_params(),
        input_output_aliases={7 + j: j for j in range(n_fill)},
    )(place, grad.reshape(4, 2, r, c), recv, others, w, m, v, *([] if fill is None else fill))


def _adamw_summed(parts, ws, ms, vs, name):
    n = len(parts)

    def body(*refs):
        p_refs, w_refs, m_refs, v_refs = refs[:n], refs[n:2 * n], refs[2 * n:3 * n], refs[3 * n:4 * n]
        o_refs = refs[4 * n:]
        for i in range(n):
            g = p_refs[i][0]
            for k in range(1, N_SHARDS):
                g = g + p_refs[i][k]
            delta, nm, nv = _adamw_math(g, w_refs[i][...], m_refs[i][...], v_refs[i][...])
            o_refs[4 * i][...] = g
            o_refs[4 * i + 1][...] = delta
            o_refs[4 * i + 2][...] = nm
            o_refs[4 * i + 3][...] = nv

    shapes = [jax.ShapeDtypeStruct(w.shape, F32) for w in ws for _ in range(4)]
    outs = pl.pallas_call(body, out_shape=shapes, name=name, compiler_params=_params())(*parts, *ws, *ms, *vs)
    return [outs[4 * i:4 * i + 4] for i in range(n)]


def _dup_heads(w):
    lead = w.shape[:-1]
    w4 = w.reshape(lead + (N_KV_HEADS, 1, HEAD_DIM))
    return jnp.broadcast_to(w4, lead + (N_KV_HEADS, 2, HEAD_DIM)).reshape(lead + (N_KV_HEADS * LANES,))


def _fold_heads(g):
    lead = g.shape[:-1]
    return g.reshape(lead + (N_KV_HEADS, 2, HEAD_DIM)).sum(axis=-2).reshape(lead + (N_KV_HEADS * HEAD_DIM,))


def kernel(x, a_norm, a_w_in, a_v_norm, a_w_s, a_b_s, a_w_out, f_norm, f_w_in, f_conv_w, f_conv_b, f_w_out, kv_norm, w_kv, k_norm, b_norm, b_w_q, b_q_norm, b_sinks, b_w_o, loss_target, m_a_norm, m_a_w_in, m_a_v_norm, m_a_w_s, m_a_b_s, m_a_w_out, m_f_norm, m_f_w_in, m_f_conv_w, m_f_conv_b, m_f_w_out, m_kv_norm, m_w_kv, m_k_norm, m_b_norm, m_b_w_q, m_b_q_norm, m_b_sinks, m_b_w_o, v_a_norm, v_a_w_in, v_a_v_norm, v_a_w_s, v_a_b_s, v_a_w_out, v_f_norm, v_f_w_in, v_f_conv_w, v_f_conv_b, v_f_w_out, v_kv_norm, v_w_kv, v_k_norm, v_b_norm, v_b_w_q, v_b_q_norm, v_b_sinks, v_b_w_o):
    d = D_MODEL
    xi, yi, ci = _coords()
    place = jnp.stack([2 * xi + yi, ci]).astype(jnp.int32)
    bf = lambda a: a.astype(BF16)
    row = lambda v_: v_.reshape(1, -1)
    x0, target = x[0], loss_target[0]
    t = x0.shape[0]
    res = {}

    red = {}

    def to_sibling(grads, wire=BF16):
        for k, g in grads.items():
            red[k] = dict(grad=g, wire=wire)
        ex = _ToSibling(list(grads.values()))
        ex.names = list(grads)
        return ex

    def to_chips(ex):
        for k, a in zip(ex.names, ex.results):
            red[k]["recv"] = a
            red[k]["psum"] = _chip_sum(red[k]["grad"], a, place, f"chip_sum_{k}", red[k]["wire"])
        nxt = _ToChips([red[k]["psum"] for k in ex.names])
        nxt.names = ex.names
        return nxt

    def landed(ex):
        for k, b in zip(ex.names, ex.results):
            red[k]["others"] = b

    def halves(ex):
        parts = []
        for h in range(2):
            nr = ex.srcs[0].shape[1] // 2
            part = _ToChips(ex.srcs, rows=(h * nr, nr))
            part.names = ex.names
            parts.append(part)
        return parts

    def landed_halves(parts):
        for j, k in enumerate(parts[0].names):
            red[k]["others"] = jnp.concatenate([p.results[j] for p in parts], axis=1)

    def update(k, w, m, v, layer=None, fill=None):
        r = red[k]
        return _adamw_sharded(r["grad"], r["recv"], r["others"], place, w, m, v,
                              f"adamw_{k}", layer=layer, fill=fill)

    g_a_in, g_a_out, g_a_norm, g_a_v_norm, g_conv = _exchange_alone(
        _Gather([bf(a_w_in[0]), bf(a_w_out[0]), a_norm, a_v_norm, f_conv_w.reshape(6, FF_SHARD)]), "gather_first")
    a_norm_full, a_v_norm_full = g_a_norm.reshape(1, d), g_a_v_norm.reshape(1, d)
    conv_w = lax.reduce_precision(g_conv.reshape(N_SHARDS, 2, 3, FF_SHARD), 8, 7)
    cw = jnp.pad(jnp.transpose(conv_w, (1, 0, 2, 3)), ((0, 0), (0, 0), (0, 5), (0, 0)))
    w_a_in_flat = jnp.transpose(g_a_in, (1, 0, 2)).reshape(d, 2 * d)
    cb = f_conv_b.reshape(2, N_SHARDS, 1, FF_SHARD)
    tri = jnp.tril(jnp.ones((CHUNK, CHUNK), dtype=bool))
    w_causal = jnp.where(tri[None], a_w_s[0], 0.0).astype(BF16)
    w_causal_t = jnp.transpose(w_causal, (0, 2, 1))
    b_sb = jnp.broadcast_to(a_b_s[0][:, :, None], (N_GROUPS, CHUNK, CHUNK))
    w_a_out = g_a_out.reshape(d, d)
    gq = jnp.tile(b_q_norm.reshape(1, HEAD_DIM), (1, 2))
    gk = jnp.tile(k_norm.reshape(1, HEAD_DIM), (1, 2))
    sinks = b_sinks.reshape(N_Q_HEADS)

    (h1,) = _rms_fwd(x0, [a_norm_full], "a_norm_fwd")
    ex = _Gather([bf(f_w_in[0]), bf(f_w_out[0])])
    zpre, x1 = _sgu_fwd(x0, h1, g_a_in, a_v_norm_full, w_causal, b_sb, w_a_out, carry=ex)
    w_in0, w_out0 = ex.results[0], ex.results[1].reshape(D_FF, d)
    ex = _Gather([bf(w_kv), bf(b_w_q[0]), bf(b_w_o[0]), bf(f_w_out[1])])
    x2, hf0, a0 = _ffn_fwd(x1, f_norm[0:1], w_in0, cw[0], cb[0], w_out0, 0, carry=ex)
    kv_full = ex.results[0].reshape(d, 2 * N_KV_HEADS * HEAD_DIM)
    w_q, w_o = ex.results[1].reshape(d, d), ex.results[2].reshape(d, d)
    w_out1 = ex.results[3].reshape(D_FF, d)
    half = N_KV_HEADS * HEAD_DIM
    w_kv_dup = jnp.concatenate([_dup_heads(kv_full[:, :half]), _dup_heads(kv_full[:, half:])], axis=1)
    hk, hq = _rms_fwd(x2, [row(kv_norm), b_norm], "kvq_norm_fwd")
    kvd = _mm_rows(hk, w_kv_dup, F32, "kv_proj")
    qraw = _mm_rows(hq, w_q, F32, "q_proj")
    ex = _Gather([bf(f_w_in[1])])
    o = _attn_fwd(qraw, kvd, gq, gk, sinks, carry=ex)
    w_in1 = ex.results[0]
    x3 = _mm_rows(o, w_o, F32, "o_proj", res=x2)
    x4, hf1, a1 = _ffn_fwd(x3, f_norm[1:2], w_in1, cw[1], cb[1], w_out1, 1)
    dy, loss_lanes = _loss_head(x4, target)
    loss = lax.psum(loss_lanes[0, 0], ("x", "y", "c"))

    dhu1, dw_out1, dconv1 = _ffn_bwd_act(a1, cw[1], cb[1], w_out1, dy, 1)
    ex = to_sibling({"f_w_out1": dw_out1.reshape(N_SHARDS, D_FF // N_SHARDS, d)})
    da1, dhf1 = _ffn_bwd_in(dhu1, cw[1], w_in1, 1, carry=ex)
    ex = to_chips(ex)
    dw_in1 = _ffn_wgrad_in(hf1, da1, 1, carry=ex)
    landed(ex)
    ex = to_sibling({"f_w_in1": dw_in1})
    dx3, dgf1 = _rms_bwd(x3, [f_norm[1:2]], [dhf1], dy, "f1_norm_bwd", carry=ex)
    ex = to_chips(ex)
    d_o = _mm_rows(dx3, w_o, BF16, "o_proj_bwd", trans_w=True)
    dw_o = _mm_wgrad(o, dx3, "o_wgrad").reshape(N_SHARDS, d // N_SHARDS, d)
    dq, dkv, dsink, dgq, dgk = _attn_bwd(qraw, kvd, d_o, gq, gk, sinks, carry=ex)
    landed(ex)
    dw_q = _mm_wgrad(hq, dq, "q_wgrad").reshape(N_SHARDS, d // N_SHARDS, d)
    dw_kv_dup = _mm_wgrad(hk, dkv, "kv_wgrad")
    dw_kv = jnp.concatenate(
        [_fold_heads(dw_kv_dup[:, :4 * LANES]), _fold_heads(dw_kv_dup[:, 4 * LANES:])], axis=1
    ).reshape(N_SHARDS, d // N_SHARDS, 2 * N_KV_HEADS * HEAD_DIM)
    ex = to_sibling({"b_w_o": dw_o, "b_w_q": dw_q, "w_kv": dw_kv})
    dhq = _mm_rows(dq, w_q, F32, "q_proj_bwd", trans_w=True, carry=ex)
    dhk = _mm_rows(dkv, w_kv_dup, F32, "kv_proj_bwd", trans_w=True)
    ex = to_chips(ex)
    dx2, dg2 = _rms_bwd(x2, [row(kv_norm), b_norm], [dhk, dhq], dx3, "kvq_norm_bwd")
    dhu0, dw_out0, dconv0 = _ffn_bwd_act(a0, cw[0], cb[0], w_out0, dx2, 0, carry=ex)
    landed(ex)
    ex = to_sibling({"f_w_out0": dw_out0.reshape(N_SHARDS, D_FF // N_SHARDS, d)})
    da0, dhf0 = _ffn_bwd_in(dhu0, cw[0], w_in0, 0, carry=ex)
    ex = to_chips(ex)
    dw_in0 = _ffn_wgrad_in(hf0, da0, 0, carry=ex)
    landed(ex)
    ex = to_sibling({"f_w_in0": dw_in0})
    dx1, dgf0 = _rms_bwd(x1, [f_norm[0:1]], [dhf0], dx2, "f0_norm_bwd", carry=ex)
    ex_lo, ex_hi = halves(to_chips(ex))
    dz, y, dwc, dbs, dgv = _sgu_bwd(dx1, zpre, w_a_out, a_v_norm_full, w_causal, w_causal_t, b_sb, carry=ex_lo)
    dw_a_out = _mm_wgrad(y, dx1, "a_out_wgrad").reshape(N_SHARDS, d // N_SHARDS, d)
    nsub = g_a_in.shape[2]
    dw_a_in = _mm(
        h1, dz, pl.BlockSpec((t, d), lambda s, j, kk: (0, 0)), pl.BlockSpec((t, nsub), lambda s, j, kk: (0, s)),
        pl.BlockSpec((None, d, nsub), lambda s, j, kk: (s, 0, 0)), jax.ShapeDtypeStruct((N_SHARDS, d, nsub), F32),
        (N_SHARDS, 1, 1), TN, "a_in_wgrad", carry=ex_hi)
    landed_halves([ex_lo, ex_hi])

    def conv_grads(dconv):
        return jnp.transpose(dconv, (1, 0, 2, 3)).reshape(N_SHARDS, 8, FF_SHARD)

    dconv0, dconv1 = conv_grads(dconv0), conv_grads(dconv1)
    g_conv_w = jnp.concatenate([dconv0[:, 0:3, :], dconv1[:, 0:3, :]], axis=1)
    g_a_v_norm = dgv[0].reshape(N_SHARDS, 1, LANES)
    rep = ["a_w_s", "a_b_s", "f_norm", "f_conv_b", "kv_norm", "k_norm", "b_norm", "b_q_norm", "b_sinks"]
    rep_g = dict(
        a_w_s=dwc.reshape(N_GROUPS * CHUNK, CHUNK), a_b_s=dbs[:, :, 0], f_norm=jnp.stack([dgf0[0], dgf1[0]]),
        f_conv_b=jnp.stack([dconv0[:, 3, :].reshape(-1), dconv1[:, 3, :].reshape(-1)]), kv_norm=dg2[0:1],
        k_norm=(dgk[0, :HEAD_DIM] + dgk[0, HEAD_DIM:])[None], b_norm=dg2[1:2],
        b_q_norm=(dgq[0, :HEAD_DIM] + dgq[0, HEAD_DIM:])[None], b_sinks=dsink[:, 0][None])
    ex_big = to_sibling({"a_w_out": dw_a_out, "a_w_in": dw_a_in})
    ex_small = to_sibling({"a_v_norm": g_a_v_norm, "f_conv_w": g_conv_w}, wire=F32)
    ex_rep = _Gather([rep_g[k] for k in rep])
    together = _Together([ex_big, ex_small, ex_rep])
    dh1 = _mm_rows(dz, w_a_in_flat, F32, "a_in_bwd", trans_w=True, carry=together)
    together.spread()
    ex_big, ex_small = to_chips(ex_big), to_chips(ex_small)
    together = _Together([ex_big, ex_small])
    grad_x, dg0 = _rms_bwd(x0, [a_norm_full], [dh1], dx1, "a_norm_bwd", carry=together)
    together.spread()
    landed(ex_big)
    landed(ex_small)
    (a_norm_parts,) = _exchange_alone(_ToOwners([dg0[0].reshape(N_SHARDS, 1, LANES)]), "a_norm_to_owners")

    res["f_w_out"] = update("f_w_out1", f_w_out, m_f_w_out, v_f_w_out, layer=1)
    res["f_w_in"] = update("f_w_in1", f_w_in, m_f_w_in, v_f_w_in, layer=1)
    res["b_w_o"] = update("b_w_o", b_w_o, m_b_w_o, v_b_w_o, layer=0)
    res["b_w_q"] = update("b_w_q", b_w_q, m_b_w_q, v_b_w_q, layer=0)
    res["w_kv"] = update("w_kv", w_kv, m_w_kv, v_w_kv)
    res["f_w_out"] = update("f_w_out0", f_w_out, m_f_w_out, v_f_w_out, layer=0, fill=res["f_w_out"])
    res["f_w_in"] = update("f_w_in0", f_w_in, m_f_w_in, v_f_w_in, layer=0, fill=res["f_w_in"])
    res["a_w_out"] = update("a_w_out", a_w_out, m_a_w_out, v_a_w_out, layer=0)
    res["a_w_in"] = update("a_w_in", a_w_in, m_a_w_in, v_a_w_in, layer=0)
    res["a_v_norm"] = update("a_v_norm", a_v_norm, m_a_v_norm, v_a_v_norm)
    res["f_conv_w"] = [o_.reshape(f_conv_w.shape) for o_ in update(
        "f_conv_w", f_conv_w.reshape(6, FF_SHARD), m_f_conv_w.reshape(6, FF_SHARD), v_f_conv_w.reshape(6, FF_SHARD))]

    rep_w = dict(a_w_s=a_w_s, a_b_s=a_b_s, f_norm=f_norm, f_conv_b=f_conv_b, kv_norm=kv_norm, k_norm=k_norm,
                 b_norm=b_norm, b_q_norm=b_q_norm, b_sinks=b_sinks, a_norm=a_norm)
    rep_m = dict(a_w_s=m_a_w_s, a_b_s=m_a_b_s, f_norm=m_f_norm, f_conv_b=m_f_conv_b, kv_norm=m_kv_norm,
                 k_norm=m_k_norm, b_norm=m_b_norm, b_q_norm=m_b_q_norm, b_sinks=m_b_sinks, a_norm=m_a_norm)
    rep_v = dict(a_w_s=v_a_w_s, a_b_s=v_a_b_s, f_norm=v_f_norm, f_conv_b=v_f_conv_b, kv_norm=v_kv_norm,
                 k_norm=v_k_norm, b_norm=v_b_norm, b_q_norm=v_b_q_norm, b_sinks=v_b_sinks, a_norm=v_a_norm)
    keys = rep + ["a_norm"]
    parts = ex_rep.results + [a_norm_parts]
    as2d = lambda a, p: a.reshape(p.shape[1:])
    rep_outs = _adamw_summed(parts, [as2d(rep_w[k], p) for k, p in zip(keys, parts)],
                             [as2d(rep_m[k], p) for k, p in zip(keys, parts)],
                             [as2d(rep_v[k], p) for k, p in zip(keys, parts)], "adamw_replicated")
    for j, key in enumerate(keys):
        res[key] = [o_.reshape(rep_w[key].shape) for o_ in rep_outs[j]]

    order = ["a_norm", "a_w_in", "a_v_norm", "a_w_s", "a_b_s", "a_w_out", "f_norm", "f_w_in", "f_conv_w", "f_conv_b",
             "f_w_out", "kv_norm", "w_kv", "k_norm", "b_norm", "b_w_q", "b_q_norm", "b_sinks", "b_w_o"]
    outs = [loss, grad_x[None]]
    for j in range(4):
        outs += [res[k][j] for k in order]
    return tuple(outs)
```

```python
import jax
import jax.numpy as jnp
from jax import lax
from jax.experimental import pallas as pl
from jax.experimental.pallas import tpu as pltpu

F32 = jnp.float32
BF16 = jnp.bfloat16
EPS = 1e-6
D_MODEL = 1024
CHUNK = 128
N_GROUPS = 8
N_SHARDS = 8
HEAD_DIM = 64
N_Q_HEADS = 16
N_KV_HEADS = 4
D_FF = 2816
FF_SHARD = 2 * D_FF // N_SHARDS
LANES = 128
NEG_BIG = -1e30
ADAM_LR = 0.001
ADAM_B1 = 0.9
ADAM_B2 = 0.999
ADAM_EPS = 1e-08
ADAM_WD = 0.01
ADAM_STEP = 10
VMEM_LIMIT_BYTES = 56 * 1024 * 1024
MESH = pl.DeviceIdType.MESH

NN = (((1,), (0,)), ((), ()))
NT = (((1,), (1,)), ((), ()))
TN = (((0,), (0,)), ((), ()))
SLOPES = tuple(2.0 ** (-8.0 * (h + 1) / N_Q_HEADS) for h in range(N_Q_HEADS))


def _params(sem=None):
    return pltpu.CompilerParams(dimension_semantics=sem, vmem_limit_bytes=VMEM_LIMIT_BYTES)


def _dot(a, b, dims=NN):
    return lax.dot_general(a, b, dims, preferred_element_type=F32)


def _sigmoid(x):
    return 1.0 / (1.0 + jnp.exp(-x))


def _gelu_parts(z):
    cdf = 0.5 * (1.0 + lax.erf(z * (2.0 ** -0.5)))
    pdf = jnp.exp(-0.5 * z * z) * 0.3989422804014327
    return cdf, pdf


def _coords():
    return lax.axis_index("x"), lax.axis_index("y"), lax.axis_index("c")


class _Gather:
    def __init__(self, srcs):
        self.srcs = list(srcs)
        n = len(self.srcs)
        self.out_shapes = [jax.ShapeDtypeStruct((N_SHARDS,) + s.shape, s.dtype) for s in self.srcs]
        self.sems = [pltpu.SemaphoreType.DMA((n, 7)), pltpu.SemaphoreType.DMA((n, 7)), pltpu.SemaphoreType.DMA((n,))]

    def _plan(self, src, dst, sems):
        send_sems, recv_sems, local_sems = sems
        x, y, c = _coords()
        me, sibling = (x, y, c), (x, y, 1 - c)
        chips = [(1 - x, y), (x, 1 - y), (1 - x, 1 - y)]
        n = len(src)

        def rows(e, dev):
            return dst[e].at[4 * dev[0] + 2 * dev[1] + dev[2]]

        def copy(e, slot, block, to, from_own=False):
            return pltpu.make_async_remote_copy(
                src_ref=src[e] if from_own else rows(e, block), dst_ref=rows(e, block),
                send_sem=send_sems.at[e, slot], recv_sem=recv_sems.at[e, slot], device_id=to, device_id_type=MESH)

        mine = [pltpu.make_async_copy(src[e], rows(e, me), local_sems.at[e]) for e in range(n)]
        first = []
        for e in range(n):
            first.append(copy(e, 0, me, sibling, from_own=True))
            first += [copy(e, 1 + j, me, (*chip, c), from_own=True) for j, chip in enumerate(chips)]
        return n, me, sibling, chips, c, copy, mine, first

    def start(self, src, dst, sems):
        _, _, _, _, _, _, mine, first = self._plan(src, dst, sems)
        for cp in mine + first:
            cp.start()

    def finish(self, src, dst, sems):
        n, me, sibling, chips, c, copy, mine, first = self._plan(src, dst, sems)
        passed = []
        for j, chip in enumerate(chips):
            for e in range(n):
                copy(e, 1 + j, (*chip, c), me).wait_recv()
                cp = copy(e, 4 + j, (*chip, c), sibling)
                cp.start()
                passed.append(cp)
        for e in range(n):
            copy(e, 0, sibling, me).wait_recv()
            for j, chip in enumerate(chips):
                copy(e, 4 + j, (*chip, 1 - c), me).wait_recv()
        for cp in first + passed:
            cp.wait_send()
        for cp in mine:
            cp.wait()


class _ToSibling:
    def __init__(self, grads):
        self.srcs = list(grads)
        n = len(self.srcs)
        self.out_shapes = [jax.ShapeDtypeStruct((4,) + g.shape[1:], g.dtype) for g in self.srcs]
        self.sems = [pltpu.SemaphoreType.DMA((n, 4)), pltpu.SemaphoreType.DMA((n, 4))]

    def _copies(self, src, dst, sems):
        send_sems, recv_sems = sems
        x, y, c = _coords()
        return [
            pltpu.make_async_remote_copy(
                src_ref=src[i].at[2 * q + (1 - c)], dst_ref=dst[i].at[q], send_sem=send_sems.at[i, q],
                recv_sem=recv_sems.at[i, q], device_id=(x, y, 1 - c), device_id_type=MESH)
            for i in range(len(src)) for q in range(4)]

    def start(self, src, dst, sems):
        for cp in self._copies(src, dst, sems):
            cp.start()

    def finish(self, src, dst, sems):
        for cp in self._copies(src, dst, sems):
            cp.wait()


class _ToChips:
    def __init__(self, psums, rows=None):
        self.srcs = list(psums)
        n = len(self.srcs)
        self.rows = rows
        self.out_shapes = [
            jax.ShapeDtypeStruct((3, p.shape[1] if rows is None else rows[1]) + p.shape[2:], p.dtype)
            for p in self.srcs]
        self.sems = [pltpu.SemaphoreType.DMA((n, 3)), pltpu.SemaphoreType.DMA((n, 3))]

    def _copies(self, src, dst, sems):
        send_sems, recv_sems = sems
        x, y, c = _coords()
        peers = [(x, 1 - y), (1 - x, y), (1 - x, 1 - y)]

        def part(i, q):
            if self.rows is None:
                return src[i].at[q]
            return src[i].at[q, pl.ds(self.rows[0], self.rows[1])]

        return [
            pltpu.make_async_remote_copy(
                src_ref=part(i, 2 * px + py), dst_ref=dst[i].at[r], send_sem=send_sems.at[i, r],
                recv_sem=recv_sems.at[i, r], device_id=(px, py, c), device_id_type=MESH)
            for i in range(len(src)) for r, (px, py) in enumerate(peers)]

    def start(self, src, dst, sems):
        for cp in self._copies(src, dst, sems):
            cp.start()

    def finish(self, src, dst, sems):
        for cp in self._copies(src, dst, sems):
            cp.wait()


class _ToOwners:
    def __init__(self, grads):
        self.srcs = list(grads)
        n = len(self.srcs)
        self.out_shapes = [jax.ShapeDtypeStruct(g.shape, g.dtype) for g in self.srcs]
        self.sems = [pltpu.SemaphoreType.DMA((n, 7)), pltpu.SemaphoreType.DMA((n, 7)), pltpu.SemaphoreType.DMA((n,))]

    def _copies(self, src, dst, sems):
        send_sems, recv_sems, local_sems = sems
        x, y, c = _coords()
        me = 4 * x + 2 * y + c
        copies = [pltpu.make_async_copy(src[i].at[me], dst[i].at[me], local_sems.at[i]) for i in range(len(src))]
        for i in range(len(src)):
            for rel in range(1, N_SHARDS):
                px = x ^ (rel >> 2) if rel >> 2 else x
                py = y ^ ((rel >> 1) & 1) if (rel >> 1) & 1 else y
                pc = c ^ (rel & 1) if rel & 1 else c
                copies.append(pltpu.make_async_remote_copy(
                    src_ref=src[i].at[4 * px + 2 * py + pc], dst_ref=dst[i].at[me], send_sem=send_sems.at[i, rel - 1],
                    recv_sem=recv_sems.at[i, rel - 1], device_id=(px, py, pc), device_id_type=MESH))
        return copies

    def start(self, src, dst, sems):
        for cp in self._copies(src, dst, sems):
            cp.start()

    def finish(self, src, dst, sems):
        for cp in self._copies(src, dst, sems):
            cp.wait()


class _Together:
    def __init__(self, parts):
        self.parts = list(parts)
        self.srcs = [s for p in self.parts for s in p.srcs]
        self.out_shapes = [s for p in self.parts for s in p.out_shapes]
        self.sems = [s for p in self.parts for s in p.sems]

    def _split(self, src, dst, sems):
        a = b = c = 0
        for p in self.parts:
            na, nc = len(p.srcs), len(p.sems)
            yield p, src[a:a + na], dst[b:b + na], sems[c:c + nc]
            a, b, c = a + na, b + na, c + nc

    def start(self, src, dst, sems):
        for p, s, d, m in self._split(src, dst, sems):
            p.start(s, d, m)

    def finish(self, src, dst, sems):
        for p, s, d, m in self._split(src, dst, sems):
            p.finish(s, d, m)

    def spread(self):
        b = 0
        for p in self.parts:
            p.results = self.results[b:b + len(p.srcs)]
            b += len(p.srcs)


def _call(body, args, *, grid, in_specs, out_specs, out_shape, name, scratch=(), sem=None, carry=None):
    out_shape, out_specs = list(out_shape), list(out_specs)
    if carry is None:
        return pl.pallas_call(
            body, grid=grid, in_specs=list(in_specs), out_specs=out_specs, out_shape=out_shape,
            scratch_shapes=list(scratch), name=name, compiler_params=_params(sem))(*args)
    n_in, n_out, n_scr, n_c = len(args), len(out_shape), len(scratch), len(carry.srcs)
    steps = tuple(grid)

    def carried(*refs):
        ins, rest = refs[:n_in], refs[n_in:]
        c_src, rest = rest[:n_c], rest[n_c:]
        outs, rest = rest[:n_out], rest[n_out:]
        c_dst, rest = rest[:n_c], rest[n_c:]
        scr, sems = rest[:n_scr], rest[n_scr:]
        first = pl.program_id(0) == 0
        last = pl.program_id(0) == steps[0] - 1
        for ax in range(1, len(steps)):
            first = first & (pl.program_id(ax) == 0)
            last = last & (pl.program_id(ax) == steps[ax] - 1)

        @pl.when(first)
        def _():
            carry.start(c_src, c_dst, sems)

        body(*ins, *outs, *scr)

        @pl.when(last)
        def _():
            carry.finish(c_src, c_dst, sems)

    hbm = pl.BlockSpec(memory_space=pl.ANY)
    res = pl.pallas_call(
        carried, grid=grid, in_specs=list(in_specs) + [hbm] * n_c, out_specs=out_specs + [hbm] * n_c,
        out_shape=out_shape + carry.out_shapes, scratch_shapes=list(scratch) + carry.sems, name=name,
        compiler_params=_params(("arbitrary",) * len(steps)))(*args, *carry.srcs)
    carry.results = list(res[n_out:])
    return list(res[:n_out])


def _exchange_alone(ex, name):
    n = len(ex.srcs)

    def body(*refs):
        src, dst, sems = refs[:n], refs[n:2 * n], refs[2 * n:]
        ex.start(src, dst, sems)
        ex.finish(src, dst, sems)

    hbm = pl.BlockSpec(memory_space=pl.ANY)
    res = pl.pallas_call(body, in_specs=[hbm] * n, out_specs=[hbm] * n, out_shape=ex.out_shapes,
                         scratch_shapes=ex.sems, name=name)(*ex.srcs)
    ex.results = list(res)
    return ex.results


def _rms_fwd(x, gains, name, tm=512, carry=None):
    t, d = x.shape
    n = len(gains)

    def body(*refs):
        x_ref, g_refs, h_refs = refs[0], refs[1:1 + n], refs[1 + n:]
        xf = x_ref[...]
        xhat = xf * lax.rsqrt(jnp.mean(xf * xf, axis=-1, keepdims=True) + EPS)
        for g_ref, h_ref in zip(g_refs, h_refs):
            h_ref[...] = (xhat * g_ref[...]).astype(BF16)

    row = pl.BlockSpec((tm, d), lambda i: (i, 0))
    vec = pl.BlockSpec((1, d), lambda i: (0, 0))
    return _call(body, [x, *gains], grid=(t // tm,), in_specs=[row] + [vec] * n, out_specs=[row] * n,
                 out_shape=[jax.ShapeDtypeStruct((t, d), BF16)] * n, name=name, carry=carry)


def _rms_bwd(x, gains, dhs, dres, name, tm=256, carry=None):
    t, d = x.shape
    n = len(gains)

    def body(*refs):
        x_ref, dres_ref = refs[0], refs[1]
        g_refs, dh_refs = refs[2:2 + n], refs[2 + n:2 + 2 * n]
        dx_ref, dg_ref = refs[2 + 2 * n], refs[3 + 2 * n]
        i = pl.program_id(0)

        @pl.when(i == 0)
        def _():
            dg_ref[...] = jnp.zeros_like(dg_ref)

        xf = x_ref[...]
        r = lax.rsqrt(jnp.mean(xf * xf, axis=-1, keepdims=True) + EPS)
        xhat = xf * r
        dx = dres_ref[...]
        for j in range(n):
            dh = dh_refs[j][...]
            dg_ref[j:j + 1, :] += jnp.sum(dh * xhat, axis=0, keepdims=True)
            gy = dh * g_refs[j][...]
            dx = dx + r * (gy - xhat * jnp.mean(gy * xhat, axis=-1, keepdims=True))
        dx_ref[...] = dx

    row = pl.BlockSpec((tm, d), lambda i: (i, 0))
    vec = pl.BlockSpec((1, d), lambda i: (0, 0))
    return _call(body, [x, dres, *gains, *dhs], grid=(t // tm,), in_specs=[row, row] + [vec] * n + [row] * n,
                 out_specs=[row, pl.BlockSpec((8, d), lambda i: (0, 0))],
                 out_shape=[jax.ShapeDtypeStruct((t, d), F32), jax.ShapeDtypeStruct((8, d), F32)],
                 name=name, sem=("arbitrary",), carry=carry)


def _mm(a, b, a_spec, b_spec, o_spec, out_shape, grid, dims, name, res=None, res_spec=None, carry=None):
    nk = grid[2]
    acc_shape = tuple(s for s in o_spec.block_shape if s is not None)

    def body(*refs):
        a_ref, b_ref = refs[0], refs[1]
        r_ref = refs[2] if res is not None else None
        o_ref = refs[3] if res is not None else refs[2]
        p = _dot(a_ref[...].astype(BF16), b_ref[...].astype(BF16), dims)
        if nk == 1:
            if res is not None:
                p = p + r_ref[...]
            o_ref[...] = p.astype(o_ref.dtype)
            return
        acc_ref = refs[-1]
        k = pl.program_id(2)

        @pl.when(k == 0)
        def _():
            acc_ref[...] = p

        @pl.when(k > 0)
        def _():
            acc_ref[...] += p

        @pl.when(k == nk - 1)
        def _():
            out = acc_ref[...]
            if res is not None:
                out = out + r_ref[...]
            o_ref[...] = out.astype(o_ref.dtype)

    ins = [a, b] + ([res] if res is not None else [])
    specs = [a_spec, b_spec] + ([res_spec] if res is not None else [])
    return _call(body, ins, grid=grid, in_specs=specs, out_specs=[o_spec], out_shape=[out_shape],
                 scratch=[pltpu.VMEM(acc_shape, F32)] if nk > 1 else [], name=name,
                 sem=("parallel", "parallel", "arbitrary"), carry=carry)[0]


def _mm_rows(a, w, out_dtype, name, trans_w=False, res=None, tm=512, carry=None):
    t, k = a.shape
    n = w.shape[0] if trans_w else w.shape[1]
    return _mm(
        a, w, pl.BlockSpec((tm, k), lambda i, j, kk: (i, 0)), pl.BlockSpec(w.shape, lambda i, j, kk: (0, 0)),
        pl.BlockSpec((tm, n), lambda i, j, kk: (i, 0)), jax.ShapeDtypeStruct((t, n), out_dtype), (t // tm, 1, 1),
        NT if trans_w else NN, name, res=res,
        res_spec=None if res is None else pl.BlockSpec((tm, n), lambda i, j, kk: (i, 0)), carry=carry)


def _mm_wgrad(a, b, name, carry=None):
    t, m = a.shape
    n = b.shape[1]
    tn = n // (4 if b.dtype == F32 else 2)
    return _mm(
        a, b, pl.BlockSpec((t, m), lambda i, j, kk: (0, 0)), pl.BlockSpec((t, tn), lambda i, j, kk: (0, j)),
        pl.BlockSpec((m, tn), lambda i, j, kk: (0, j)), jax.ShapeDtypeStruct((m, n), F32), (1, n // tn, 1), TN, name,
        carry=carry)


def _sgu_fwd(x0, h1, w_in, g_v, w_c, b_sb, w_out, tm=256, carry=None):
    t, d = x0.shape
    nsub = w_in.shape[2]

    def body(x_ref, h_ref, win_ref, gv_ref, wc_ref, bsb_ref, wout_ref, zpre_ref, x1_ref, u_s, v_s, vn_s, y_s):
        h = h_ref[...]
        for k in range(N_SHARDS):
            zk = _dot(h, win_ref[k])
            zpre_ref[:, k * nsub:(k + 1) * nsub] = zk
            cdf, _ = _gelu_parts(zk)
            if k < N_SHARDS // 2:
                u_s[:, k * nsub:(k + 1) * nsub] = zk * cdf
            else:
                v_s[:, (k - 4) * nsub:(k - 3) * nsub] = zk * cdf
        v = v_s[...]
        rv = lax.rsqrt(jnp.mean(v * v, axis=-1, keepdims=True) + EPS)
        vn_s[...] = (v * rv * gv_ref[...]).astype(BF16)
        for ci in range(tm // CHUNK):
            rows = slice(ci * CHUNK, (ci + 1) * CHUNK)
            for g in range(N_GROUPS):
                cols = slice(g * LANES, (g + 1) * LANES)
                sv = _dot(wc_ref[g], vn_s[rows, cols]) + bsb_ref[g]
                y_s[rows, cols] = (u_s[rows, cols] * sv).astype(BF16)
        x1_ref[...] = x_ref[...] + _dot(y_s[...], wout_ref[...])

    row = pl.BlockSpec((tm, d), lambda i: (i, 0))
    full = lambda a: pl.BlockSpec(a.shape, lambda i: (0,) * a.ndim)
    return _call(
        body, [x0, h1, w_in, g_v, w_c, b_sb, w_out], grid=(t // tm,),
        in_specs=[row, row, full(w_in), full(g_v), full(w_c), full(b_sb), full(w_out)],
        out_specs=[pl.BlockSpec((tm, 2 * d), lambda i: (i, 0)), row],
        out_shape=[jax.ShapeDtypeStruct((t, 2 * d), F32), jax.ShapeDtypeStruct((t, d), F32)],
        scratch=[pltpu.VMEM((tm, d), F32), pltpu.VMEM((tm, d), F32), pltpu.VMEM((tm, d), BF16),
                 pltpu.VMEM((tm, d), BF16)],
        name="sgu_fwd", carry=carry)


def _sgu_bwd(dx1, zpre, w_out, g_v, w_c, w_ct, b_sb, tm=256, carry=None):
    t, d = dx1.shape

    def body(dx_ref, zpre_ref, wout_ref, gv_ref, wc_ref, wct_ref, bsb_ref,
             dz_ref, y_ref, dwc_ref, dbs_ref, dgv_ref, u_s, vn_s, dy_s, du_s, dvn_s):
        i = pl.program_id(0)

        @pl.when(i == 0)
        def _():
            dwc_ref[...] = jnp.zeros_like(dwc_ref)
            dbs_ref[...] = jnp.zeros_like(dbs_ref)
            dgv_ref[...] = jnp.zeros_like(dgv_ref)

        dy_s[...] = _dot(dx_ref[...].astype(BF16), wout_ref[...], NT)
        zu = zpre_ref[:, :d]
        zv = zpre_ref[:, d:]
        cdf_u, pdf_u = _gelu_parts(zu)
        cdf_v, pdf_v = _gelu_parts(zv)
        u_s[...] = zu * cdf_u
        v = zv * cdf_v
        rv = lax.rsqrt(jnp.mean(v * v, axis=-1, keepdims=True) + EPS)
        vhat = v * rv
        gv = gv_ref[...]
        vn_s[...] = (vhat * gv).astype(BF16)
        for ci in range(tm // CHUNK):
            rows = slice(ci * CHUNK, (ci + 1) * CHUNK)
            for g in range(N_GROUPS):
                cols = slice(g * LANES, (g + 1) * LANES)
                vnb = vn_s[rows, cols]
                sv = _dot(wc_ref[g], vnb) + bsb_ref[g]
                dyb = dy_s[rows, cols]
                ub = u_s[rows, cols]
                dsv = dyb * ub
                du_s[rows, cols] = dyb * sv
                y_ref[rows, cols] = (ub * sv).astype(BF16)
                dsvb = dsv.astype(BF16)
                dbs_ref[g] += dsv
                dwc_ref[g] += _dot(dsvb, vnb, NT)
                dvn_s[rows, cols] = _dot(wct_ref[g], dsvb)
        dvn = dvn_s[...]
        dgv_ref[0:1, :] += jnp.sum(dvn * vhat, axis=0, keepdims=True)
        gy = dvn * gv
        dv = rv * (gy - vhat * jnp.mean(gy * vhat, axis=-1, keepdims=True))
        dz_ref[:, :d] = (du_s[...] * (cdf_u + zu * pdf_u)).astype(BF16)
        dz_ref[:, d:] = (dv * (cdf_v + zv * pdf_v)).astype(BF16)

        @pl.when(i == t // tm - 1)
        def _():
            tri = (lax.broadcasted_iota(jnp.int32, (CHUNK, CHUNK), 0)
                   >= lax.broadcasted_iota(jnp.int32, (CHUNK, CHUNK), 1))
            for g in range(N_GROUPS):
                dwc_ref[g] = jnp.where(tri, dwc_ref[g], 0.0)
                dbs_ref[g] = jnp.broadcast_to(jnp.sum(dbs_ref[g], axis=1, keepdims=True), (CHUNK, CHUNK))

    row = pl.BlockSpec((tm, d), lambda i: (i, 0))
    row2 = pl.BlockSpec((tm, 2 * d), lambda i: (i, 0))
    full = lambda a: pl.BlockSpec(a.shape, lambda i: (0,) * a.ndim)
    grp = pl.BlockSpec((N_GROUPS, CHUNK, CHUNK), lambda i: (0, 0, 0))
    return _call(
        body, [dx1, zpre, w_out, g_v, w_c, w_ct, b_sb], grid=(t // tm,),
        in_specs=[row, row2, full(w_out), full(g_v), full(w_c), full(w_ct), full(b_sb)],
        out_specs=[row2, row, grp, grp, pl.BlockSpec((8, d), lambda i: (0, 0))],
        out_shape=[jax.ShapeDtypeStruct((t, 2 * d), BF16), jax.ShapeDtypeStruct((t, d), BF16),
                   jax.ShapeDtypeStruct((N_GROUPS, CHUNK, CHUNK), F32),
                   jax.ShapeDtypeStruct((N_GROUPS, CHUNK, CHUNK), F32), jax.ShapeDtypeStruct((8, d), F32)],
        scratch=[pltpu.VMEM((tm, d), F32), pltpu.VMEM((tm, d), BF16), pltpu.VMEM((tm, d), F32),
                 pltpu.VMEM((tm, d), F32), pltpu.VMEM((tm, d), F32)],
        name="sgu_bwd", sem=("arbitrary",), carry=carry)


ROW_CHUNK = 256
HALO = 16


def _ffn_fwd(x, g, w_in, cw, cb, w_out, layer, tm=512, carry=None):
    t, d = x.shape
    nc = N_SHARDS // 2

    def body(x_ref, xp_ref, g_ref, wg_ref, wu_ref, cwg_ref, cbg_ref, cwu_ref, cbu_ref, wout_ref,
             o_ref, hf_ref, a_ref, pre_ref, hw_s):
        i, c = pl.program_id(0), pl.program_id(1)

        @pl.when(c == 0)
        def _():
            keep = jnp.where(i == 0, 0.0, 1.0)
            xw = jnp.concatenate([xp_ref[...] * keep, x_ref[...]], axis=0)
            xhat = xw * lax.rsqrt(jnp.mean(xw * xw, axis=-1, keepdims=True) + EPS)
            hw_s[...] = (xhat * g_ref[...]).astype(BF16)
            hf_ref[...] = hw_s[HALO:, :]
            o_ref[...] = x_ref[...]

        hw = hw_s[...]
        pre = []
        for j, (w_ref, cw_ref, cb_ref) in enumerate(((wg_ref, cwg_ref, cbg_ref), (wu_ref, cwu_ref, cbu_ref))):
            ab = _dot(hw, w_ref[...]).astype(BF16)
            a_ref[j] = ab[HALO:]
            win = ab.astype(F32)
            cw_v = cw_ref[...]
            pre.append(cw_v[2:3, :] * win[HALO:] + cw_v[1:2, :] * pltpu.roll(win, 1, 0)[HALO:]
                       + cw_v[0:1, :] * pltpu.roll(win, 2, 0)[HALO:] + cb_ref[...])
            pre_ref[j] = pre[j]
        act = (pre[0] * _sigmoid(pre[0]) * pre[1]).astype(BF16)
        o_ref[...] += _dot(act, wout_ref[...])

    row = pl.BlockSpec((tm, d), lambda i, c: (i, 0))
    shard = lambda rows, up: pl.BlockSpec((None, rows, FF_SHARD), lambda i, c: (c + up * nc, 0, 0))
    pair = pl.BlockSpec((2, None, tm, FF_SHARD), lambda i, c: (0, c, i, 0))
    outs = _call(
        body, [x, x, g, w_in, w_in, cw, cb, cw, cb, w_out], grid=(t // tm, nc),
        in_specs=[row, pl.BlockSpec((HALO, d), lambda i, c: (jnp.maximum(i * (tm // HALO) - 1, 0), 0)),
                  pl.BlockSpec((1, d), lambda i, c: (0, 0)), shard(d, 0), shard(d, 1),
                  shard(8, 0), shard(1, 0), shard(8, 1), shard(1, 1), pl.BlockSpec((FF_SHARD, d), lambda i, c: (c, 0))],
        out_specs=[row, row, pair, pair],
        out_shape=[jax.ShapeDtypeStruct((t, d), F32), jax.ShapeDtypeStruct((t, d), BF16),
                   jax.ShapeDtypeStruct((2, nc, t, FF_SHARD), BF16), jax.ShapeDtypeStruct((2, nc, t, FF_SHARD), F32)],
        scratch=[pltpu.VMEM((tm + HALO, d), BF16)], name=f"ffn{layer}_fwd", sem=("parallel", "arbitrary"), carry=carry)
    return outs[0], outs[1], outs[2].reshape(N_SHARDS, t, FF_SHARD), outs[3]


def _ffn_bwd_act(pre, w_out, dxn, layer, tm=512, carry=None):
    t, d = dxn.shape
    nc = N_SHARDS // 2

    def body(pre_ref, wout_ref, dx_ref, dhu_ref, dw_ref, dcb_ref):
        i = pl.program_id(1)

        @pl.when(i == 0)
        def _():
            dw_ref[...] = jnp.zeros_like(dw_ref)
            dcb_ref[...] = jnp.zeros_like(dcb_ref)

        hg, hu = pre_ref[0], pre_ref[1]
        sg = _sigmoid(hg)
        sl = hg * sg
        dxb = dx_ref[...].astype(BF16)
        dact = _dot(dxb, wout_ref[...], NT)
        dw_ref[...] += _dot((sl * hu).astype(BF16), dxb, TN)
        d_up = dact * sl
        d_gate = dact * hu * (sg * (1.0 + hg * (1.0 - sg)))
        for j, dv in enumerate((d_gate, d_up)):
            dhu_ref[j] = dv.astype(BF16)
            dcb_ref[j, 0:1, :] += jnp.sum(dv, axis=0, keepdims=True)

    return _call(
        body, [pre, w_out, dxn], grid=(nc, t // tm),
        in_specs=[pl.BlockSpec((2, None, tm, FF_SHARD), lambda c, i: (0, c, i, 0)),
                  pl.BlockSpec((FF_SHARD, d), lambda c, i: (c, 0)), pl.BlockSpec((tm, d), lambda c, i: (i, 0))],
        out_specs=[pl.BlockSpec((None, 2, tm, FF_SHARD), lambda c, i: (c, 0, i, 0)),
                   pl.BlockSpec((FF_SHARD, d), lambda c, i: (c, 0)),
                   pl.BlockSpec((None, 2, 8, FF_SHARD), lambda c, i: (c, 0, 0, 0))],
        out_shape=[jax.ShapeDtypeStruct((nc, 2, t, FF_SHARD), BF16), jax.ShapeDtypeStruct((D_FF, d), F32),
                   jax.ShapeDtypeStruct((nc, 2, 8, FF_SHARD), F32)],
        name=f"ffn{layer}_bwd_act", sem=("parallel", "arbitrary"), carry=carry)


def _ffn_bwd_in(dhu, a, cw, w_in, layer, tm=1024, carry=None):
    nc, _, t, _ = dhu.shape
    d = D_MODEL
    tm = min(tm, t)
    last_blk = t // 16 - 1

    def body(dh_ref, nx_ref, a_ref, cw_ref, win_ref, da_ref, o_ref, dcw_ref):
        i, s = pl.program_id(0), pl.program_id(1)

        @pl.when(s == 0)
        def _():
            o_ref[...] = jnp.zeros_like(o_ref)

        @pl.when((s == 0) & (i == 0))
        def _():
            dcw_ref[...] = jnp.zeros_like(dcw_ref)

        keep = jnp.where(i == t // tm - 1, 0.0, 1.0)
        cw = cw_ref[...]
        sums = [None] * 3
        for r0 in range(0, tm, ROW_CHUNK):
            rows = slice(r0, r0 + ROW_CHUNK)
            if r0 + ROW_CHUNK == tm:
                win = jnp.concatenate([dh_ref[rows, :].astype(F32), nx_ref[...].astype(F32) * keep], axis=0)
            else:
                win = dh_ref[r0:r0 + ROW_CHUNK + HALO, :].astype(F32)
            n = ROW_CHUNK + HALO
            taps = (pltpu.roll(win, n - 2, 0)[:ROW_CHUNK],
                    pltpu.roll(win, n - 1, 0)[:ROW_CHUNK],
                    win[:ROW_CHUNK])
            da = (cw[0:1, :] * taps[0] + cw[1:2, :] * taps[1] + cw[2:3, :] * taps[2]).astype(BF16)
            da_ref[rows, :] = da
            o_ref[rows, :] += _dot(da, win_ref[...], NT)
            af = a_ref[rows, :].astype(F32)
            parts = [jnp.sum(taps[k] * af, axis=0, keepdims=True) for k in range(3)]
            sums = [p if q is None else q + p for q, p in zip(sums, parts)]
        for k in range(3):
            dcw_ref[pl.ds(s, 1), k:k + 1, :] += sums[k][None]

    return _call(
        body, [dhu, dhu, a, cw, w_in], grid=(t // tm, N_SHARDS),
        in_specs=[pl.BlockSpec((None, None, tm, FF_SHARD), lambda i, s: (s % nc, s // nc, i, 0)),
                  pl.BlockSpec((None, None, 16, FF_SHARD),
                               lambda i, s: (s % nc, s // nc, jnp.minimum((i + 1) * (tm // 16), last_blk), 0)),
                  pl.BlockSpec((None, tm, FF_SHARD), lambda i, s: (s, i, 0)),
                  pl.BlockSpec((None, 8, FF_SHARD), lambda i, s: (s, 0, 0)),
                  pl.BlockSpec((None, d, FF_SHARD), lambda i, s: (s, 0, 0))],
        out_specs=[pl.BlockSpec((None, tm, FF_SHARD), lambda i, s: (s, i, 0)),
                   pl.BlockSpec((tm, d), lambda i, s: (i, 0)),
                   pl.BlockSpec((N_SHARDS, 8, FF_SHARD), lambda i, s: (0, 0, 0))],
        out_shape=[jax.ShapeDtypeStruct((N_SHARDS, t, FF_SHARD), BF16), jax.ShapeDtypeStruct((t, d), F32),
                   jax.ShapeDtypeStruct((N_SHARDS, 8, FF_SHARD), F32)],
        name=f"ffn{layer}_bwd_in", sem=("arbitrary", "arbitrary"), carry=carry)


def _ffn_wgrad_in(hf, da, layer, carry=None):
    t, d = hf.shape
    return _mm(
        hf, da, pl.BlockSpec((t, d), lambda s, j, kk: (0, 0)),
        pl.BlockSpec((None, t, FF_SHARD), lambda s, j, kk: (s, 0, 0)),
        pl.BlockSpec((None, d, FF_SHARD), lambda s, j, kk: (s, 0, 0)),
        jax.ShapeDtypeStruct((N_SHARDS, d, FF_SHARD), F32), (N_SHARDS, 1, 1), TN, f"ffn{layer}_wgrad_in",
        carry=carry)


Q_PER_KV = N_Q_HEADS // N_KV_HEADS
GROUP_ROWS = Q_PER_KV * CHUNK


def _attn_masks(n):
    lane = lax.broadcasted_iota(jnp.int32, (CHUNK, LANES), 1)
    lo = lane < HEAD_DIM
    tq = lax.broadcasted_iota(jnp.int32, (GROUP_ROWS, 2 * CHUNK), 0) & (CHUNK - 1)
    jk = lax.broadcasted_iota(jnp.int32, (GROUP_ROWS, 2 * CHUNK), 1)
    dist = tq + CHUNK - jk
    mask = (dist >= 0) & (dist < CHUNK) & (jk >= jnp.where(n == 0, CHUNK, 0))
    return lo, mask, dist.astype(F32)


def _per_head_column(values):
    r = lax.broadcasted_iota(jnp.int32, (GROUP_ROWS, 1), 0)
    col = jnp.full((GROUP_ROWS, 1), values[Q_PER_KV - 1], F32)
    for j in range(Q_PER_KV - 2, -1, -1):
        col = jnp.where(r < (j + 1) * CHUNK, values[j], col)
    return col


def _half_sum(x, lo):
    s_lo = jnp.sum(jnp.where(lo, x, 0.0), axis=-1, keepdims=True)
    s_hi = jnp.sum(jnp.where(lo, 0.0, x), axis=-1, keepdims=True)
    return jnp.where(lo, s_lo, s_hi)


def _stack_heads(pairs, lo):
    zero = jnp.zeros_like(pairs[0])
    return jnp.concatenate([jnp.where(lo, pairs[0], zero), jnp.where(lo, zero, pairs[0]),
                            jnp.where(lo, pairs[1], zero), jnp.where(lo, zero, pairs[1])], axis=0)


def _unstack_heads(stacked, lo):
    return (jnp.where(lo, stacked[0:CHUNK], stacked[CHUNK:2 * CHUNK]),
            jnp.where(lo, stacked[2 * CHUNK:3 * CHUNK], stacked[3 * CHUNK:]))


def _attn_probs(qs, kn, mask, distf, slope_col, sink_col):
    s = _dot(qs, kn, NT) * (HEAD_DIM ** -0.5)
    s = jnp.where(mask, s - slope_col * distf, NEG_BIG)
    m = jnp.maximum(jnp.max(s, axis=-1, keepdims=True), sink_col)
    e = jnp.exp(s - m)
    den = jnp.sum(e, axis=-1, keepdims=True) + jnp.exp(sink_col - m)
    return e * (1.0 / den), m, den


def _attn_fwd(qraw, kvd, gq, gk, sinks, carry=None):
    t, d = qraw.shape
    nb = t // CHUNK

    def body(sink_ref, q_ref, cur_ref, prev_ref, gq_ref, gk_ref, o_ref):
        n = pl.program_id(0)
        lo, mask, distf = _attn_masks(n)
        gq_v, gk_v = gq_ref[...], gk_ref[...]
        for kvh in range(N_KV_HEADS):
            ks = slice(kvh * LANES, (kvh + 1) * LANES)
            vs = slice(4 * LANES + kvh * LANES, 4 * LANES + (kvh + 1) * LANES)
            kraw = jnp.concatenate([prev_ref[:, ks], cur_ref[:, ks]], axis=0)
            rk = lax.rsqrt(jnp.mean(kraw * kraw, axis=-1, keepdims=True) + EPS)
            kn = (kraw * rk * gk_v).astype(BF16)
            vv = jnp.concatenate([prev_ref[:, vs], cur_ref[:, vs]], axis=0).astype(BF16)
            qn = []
            for p in range(2):
                qp = q_ref[:, (2 * kvh + p) * LANES:(2 * kvh + p + 1) * LANES]
                r = lax.rsqrt(_half_sum(qp * qp, lo) * (1.0 / HEAD_DIM) + EPS)
                qn.append(qp * r * gq_v)
            heads = range(Q_PER_KV * kvh, Q_PER_KV * (kvh + 1))
            pf, _, _ = _attn_probs(_stack_heads(qn, lo).astype(BF16), kn, mask, distf,
                                   _per_head_column([SLOPES[h] for h in heads]),
                                   _per_head_column([sink_ref[h] for h in heads]))
            for p, o_pair in enumerate(_unstack_heads(_dot(pf.astype(BF16), vv), lo)):
                o_ref[:, (2 * kvh + p) * LANES:(2 * kvh + p + 1) * LANES] = o_pair.astype(BF16)

    blk = lambda f: pl.BlockSpec((CHUNK, d), f)
    vec = pl.BlockSpec((1, LANES), lambda n: (0, 0))
    return _call(
        body, [sinks, qraw, kvd, kvd, gq, gk], grid=(nb,),
        in_specs=[pl.BlockSpec(memory_space=pltpu.SMEM), blk(lambda n: (n, 0)), blk(lambda n: (n, 0)),
                  blk(lambda n: (jnp.maximum(n - 1, 0), 0)), vec, vec],
        out_specs=[blk(lambda n: (n, 0))], out_shape=[jax.ShapeDtypeStruct((t, d), BF16)],
        name="attn_fwd", carry=carry)[0]


def _attn_bwd(qraw, kvd, d_o, gq, gk, sinks, carry=None):
    t, d = qraw.shape
    nb = t // CHUNK

    def body(sink_ref, q_ref, cur_ref, prev_ref, do_ref, gq_ref, gk_ref,
             dq_ref, dkv_ref, dsink_ref, dgq_ref, dgk_ref, carry_s, pp_s, cp_s):
        n = pl.program_id(0)

        @pl.when(n == 0)
        def _():
            carry_s[...] = jnp.zeros_like(carry_s)
            dsink_ref[...] = jnp.zeros_like(dsink_ref)
            dgq_ref[...] = jnp.zeros_like(dgq_ref)
            dgk_ref[...] = jnp.zeros_like(dgk_ref)

        @pl.when(n < nb)
        def _():
            lo, mask, distf = _attn_masks(n)
            gq_v, gk_v = gq_ref[...], gk_ref[...]
            for kvh in range(N_KV_HEADS):
                ks = slice(kvh * LANES, (kvh + 1) * LANES)
                vs = slice(4 * LANES + kvh * LANES, 4 * LANES + (kvh + 1) * LANES)
                kraw = jnp.concatenate([prev_ref[:, ks], cur_ref[:, ks]], axis=0)
                rk = lax.rsqrt(jnp.mean(kraw * kraw, axis=-1, keepdims=True) + EPS)
                khat = kraw * rk
                kn = (khat * gk_v).astype(BF16)
                vv = jnp.concatenate([prev_ref[:, vs], cur_ref[:, vs]], axis=0).astype(BF16)
                cols = [slice((2 * kvh + p) * LANES, (2 * kvh + p + 1) * LANES) for p in range(2)]
                rq, qhat = [], []
                for p in range(2):
                    qp = q_ref[:, cols[p]]
                    rq.append(lax.rsqrt(_half_sum(qp * qp, lo) * (1.0 / HEAD_DIM) + EPS))
                    qhat.append(qp * rq[p])
                heads = range(Q_PER_KV * kvh, Q_PER_KV * (kvh + 1))
                qs = _stack_heads([qhat[p] * gq_v for p in range(2)], lo).astype(BF16)
                dos = _stack_heads([do_ref[:, cols[p]] for p in range(2)], lo)
                sink_col = _per_head_column([sink_ref[h] for h in heads])
                pf, m, den = _attn_probs(qs, kn, mask, distf, _per_head_column([SLOPES[h] for h in heads]), sink_col)
                dp = _dot(dos, vv, NT)
                delta = jnp.sum(pf * dp, axis=-1, keepdims=True)
                sink_delta = jnp.exp(sink_col - m) / den * delta
                for j, h in enumerate(heads):
                    dsink_ref[h:h + 1, :] -= jnp.broadcast_to(
                        jnp.sum(sink_delta[j * CHUNK:(j + 1) * CHUNK], axis=0, keepdims=True), (1, LANES))
                ds = (pf * (dp - delta) * (HEAD_DIM ** -0.5)).astype(BF16)
                dkn = _dot(ds, qs, TN)
                dvb = _dot(pf.astype(BF16), dos, TN)
                for p, dqn in enumerate(_unstack_heads(_dot(ds, kn), lo)):
                    dgq_ref[0:1, :] += jnp.sum(dqn * qhat[p], axis=0, keepdims=True)
                    gy = dqn * gq_v
                    mq = _half_sum(gy * qhat[p], lo) * (1.0 / HEAD_DIM)
                    dq_ref[:, cols[p]] = (rq[p] * (gy - qhat[p] * mq)).astype(BF16)
                dgk_ref[0:1, :] += jnp.sum(dkn * khat, axis=0, keepdims=True)
                gyk = dkn * gk_v
                dkraw = rk * (gyk - khat * jnp.mean(gyk * khat, axis=-1, keepdims=True))
                pp_s[:, ks] = dkraw[:CHUNK]
                cp_s[:, ks] = dkraw[CHUNK:]
                pp_s[:, vs] = dvb[:CHUNK]
                cp_s[:, vs] = dvb[CHUNK:]
            dkv_ref[...] = (carry_s[...] + pp_s[...]).astype(BF16)
            carry_s[...] = cp_s[...]

        @pl.when(n == nb)
        def _():
            dkv_ref[...] = carry_s[...].astype(BF16)

    blk = lambda f: pl.BlockSpec((CHUNK, d), f)
    vec = pl.BlockSpec((1, LANES), lambda n: (0, 0))
    cur = lambda n: (jnp.minimum(n, nb - 1), 0)
    prev = lambda n: (jnp.maximum(jnp.minimum(n, nb - 1) - 1, 0), 0)
    small = lambda r: pl.BlockSpec((r, LANES), lambda n: (0, 0))
    return _call(
        body, [sinks, qraw, kvd, kvd, d_o, gq, gk], grid=(nb + 1,),
        in_specs=[pl.BlockSpec(memory_space=pltpu.SMEM), blk(cur), blk(cur), blk(prev), blk(cur), vec, vec],
        out_specs=[blk(cur), blk(lambda n: (jnp.maximum(n - 1, 0), 0)), small(N_Q_HEADS), small(8), small(8)],
        out_shape=[jax.ShapeDtypeStruct((t, d), BF16), jax.ShapeDtypeStruct((t, d), BF16),
                   jax.ShapeDtypeStruct((N_Q_HEADS, LANES), F32), jax.ShapeDtypeStruct((8, LANES), F32),
                   jax.ShapeDtypeStruct((8, LANES), F32)],
        scratch=[pltpu.VMEM((CHUNK, d), F32)] * 3, name="attn_bwd", sem=("arbitrary",), carry=carry)


def _loss_head(y, target, tm=512):
    t, d = y.shape

    def body(y_ref, t_ref, dy_ref, loss_ref, acc_ref):
        i = pl.program_id(0)

        @pl.when(i == 0)
        def _():
            acc_ref[...] = jnp.zeros_like(acc_ref)

        err = y_ref[...] - t_ref[...]
        dy_ref[...] = err * (1.0 / d)
        acc_ref[...] += jnp.sum(err * err, axis=0, keepdims=True)

        @pl.when(i == t // tm - 1)
        def _():
            loss_ref[...] = jnp.broadcast_to(0.5 / d * jnp.sum(acc_ref[...], axis=1, keepdims=True), loss_ref.shape)

    row = pl.BlockSpec((tm, d), lambda i: (i, 0))
    return _call(
        body, [y, target], grid=(t // tm,), in_specs=[row, row],
        out_specs=[row, pl.BlockSpec((8, LANES), lambda i: (0, 0))],
        out_shape=[jax.ShapeDtypeStruct((t, d), F32), jax.ShapeDtypeStruct((8, LANES), F32)],
        scratch=[pltpu.VMEM((1, d), F32)], name="loss_head", sem=("arbitrary",))


def _adamw_math(g, w, m, v):
    m = ADAM_B1 * m + (1.0 - ADAM_B1) * g
    v = ADAM_B2 * v + (1.0 - ADAM_B2) * (g * g)
    m_hat = m / (1.0 - ADAM_B1 ** ADAM_STEP)
    v_hat = v / (1.0 - ADAM_B2 ** ADAM_STEP)
    delta = -ADAM_LR * (m_hat / (jnp.sqrt(v_hat) + ADAM_EPS) + ADAM_WD * w)
    return delta, m, v


def _row_tile(r, cap=128):
    for tr in range(min(r, cap), 0, -1):
        if r % tr == 0 and (tr % 8 == 0 or tr == r):
            return tr
    return r


def _chip_sum(grad, recv, place, name, wire_dtype):
    _, r, c = grad.shape
    tr = _row_tile(r, 256)

    def body(pl_ref, g_ref, a_ref, p_ref):
        p_ref[...] = (g_ref[...] + a_ref[...]).astype(p_ref.dtype)

    return pl.pallas_call(
        body,
        grid_spec=pltpu.PrefetchScalarGridSpec(
            num_scalar_prefetch=1, grid=(4, r // tr),
            in_specs=[pl.BlockSpec((None, None, tr, c), lambda q, i, pr: (q, pr[1], i, 0)),
                      pl.BlockSpec((None, tr, c), lambda q, i, pr: (q, i, 0))],
            out_specs=pl.BlockSpec((None, tr, c), lambda q, i, pr: (q, i, 0))),
        out_shape=jax.ShapeDtypeStruct((4, r, c), wire_dtype), name=name, compiler_params=_params(),
    )(place, grad.reshape(4, 2, r, c), recv)


def _adamw_sharded(grad, recv, others, place, w, m, v, name, layer=None, fill=None):
    r, c = w.shape[-2:]
    tr = _row_tile(r)

    def body(pl_ref, g_ref, a_ref, oth_ref, w_ref, m_ref, v_ref, *rest):
        g_out, d_out, nm_out, nv_out = rest[-4:]
        g = g_ref[...] + a_ref[...]
        for k in range(3):
            g = g + oth_ref[k].astype(F32)
        delta, nm, nv = _adamw_math(g, w_ref[...], m_ref[...], v_ref[...])
        g_out[...] = g
        d_out[...] = delta
        nm_out[...] = nm
        nv_out[...] = nv

    if layer is None:
        row = pl.BlockSpec((tr, c), lambda i, pr: (i, 0))
    else:
        row = pl.BlockSpec((None, tr, c), lambda i, pr: (layer, i, 0))
    n_fill = 0 if fill is None else 4
    in_specs = [pl.BlockSpec((None, None, tr, c), lambda i, pr: (pr[0], pr[1], i, 0)),
                pl.BlockSpec((None, tr, c), lambda i, pr: (pr[0], i, 0)),
                pl.BlockSpec((3, tr, c), lambda i, pr: (0, i, 0)), row, row, row]
    in_specs += [pl.BlockSpec(memory_space=pl.ANY)] * n_fill
    return pl.pallas_call(
        body,
        grid_spec=pltpu.PrefetchScalarGridSpec(
            num_scalar_prefetch=1, grid=(r // tr,), in_specs=in_specs, out_specs=[row] * 4),
        out_shape=[jax.ShapeDtypeStruct(w.shape, F32)] * 4, name=name, compiler_params=_params(),
        input_output_aliases={7 + j: j for j in range(n_fill)},
    )(place, grad.reshape(4, 2, r, c), recv, others, w, m, v, *([] if fill is None else fill))


def _adamw_summed(parts, ws, ms, vs, name):
    n = len(parts)

    def body(*refs):
        p_refs, w_refs, m_refs, v_refs = refs[:n], refs[n:2 * n], refs[2 * n:3 * n], refs[3 * n:4 * n]
        o_refs = refs[4 * n:]
        for i in range(n):
            g = p_refs[i][0]
            for k in range(1, N_SHARDS):
                g = g + p_refs[i][k]
            delta, nm, nv = _adamw_math(g, w_refs[i][...], m_refs[i][...], v_refs[i][...])
            o_refs[4 * i][...] = g
            o_refs[4 * i + 1][...] = delta
            o_refs[4 * i + 2][...] = nm
            o_refs[4 * i + 3][...] = nv

    shapes = [jax.ShapeDtypeStruct(w.shape, F32) for w in ws for _ in range(4)]
    outs = pl.pallas_call(body, out_shape=shapes, name=name, compiler_params=_params())(*parts, *ws, *ms, *vs)
    return [outs[4 * i:4 * i + 4] for i in range(n)]


def _dup_heads(w):
    lead = w.shape[:-1]
    w4 = w.reshape(lead + (N_KV_HEADS, 1, HEAD_DIM))
    return jnp.broadcast_to(w4, lead + (N_KV_HEADS, 2, HEAD_DIM)).reshape(lead + (N_KV_HEADS * LANES,))


def _fold_heads(g):
    lead = g.shape[:-1]
    return g.reshape(lead + (N_KV_HEADS, 2, HEAD_DIM)).sum(axis=-2).reshape(lead + (N_KV_HEADS * HEAD_DIM,))


def kernel(x, a_norm, a_w_in, a_v_norm, a_w_s, a_b_s, a_w_out, f_norm, f_w_in, f_conv_w, f_conv_b, f_w_out, kv_norm, w_kv, k_norm, b_norm, b_w_q, b_q_norm, b_sinks, b_w_o, loss_target, m_a_norm, m_a_w_in, m_a_v_norm, m_a_w_s, m_a_b_s, m_a_w_out, m_f_norm, m_f_w_in, m_f_conv_w, m_f_conv_b, m_f_w_out, m_kv_norm, m_w_kv, m_k_norm, m_b_norm, m_b_w_q, m_b_q_norm, m_b_sinks, m_b_w_o, v_a_norm, v_a_w_in, v_a_v_norm, v_a_w_s, v_a_b_s, v_a_w_out, v_f_norm, v_f_w_in, v_f_conv_w, v_f_conv_b, v_f_w_out, v_kv_norm, v_w_kv, v_k_norm, v_b_norm, v_b_w_q, v_b_q_norm, v_b_sinks, v_b_w_o):
    d = D_MODEL
    xi, yi, ci = _coords()
    place = jnp.stack([2 * xi + yi, ci]).astype(jnp.int32)
    bf = lambda a: a.astype(BF16)
    row = lambda v_: v_.reshape(1, -1)
    x0, target = x[0], loss_target[0]
    t = x0.shape[0]
    res = {}

    red = {}

    def to_sibling(grads, wire=BF16):
        for k, g in grads.items():
            red[k] = dict(grad=g, wire=wire)
        ex = _ToSibling(list(grads.values()))
        ex.names = list(grads)
        return ex

    def to_chips(ex):
        for k, a in zip(ex.names, ex.results):
            red[k]["recv"] = a
            red[k]["psum"] = _chip_sum(red[k]["grad"], a, place, f"chip_sum_{k}", red[k]["wire"])
        nxt = _ToChips([red[k]["psum"] for k in ex.names])
        nxt.names = ex.names
        return nxt

    def landed(ex):
        for k, b in zip(ex.names, ex.results):
            red[k]["others"] = b

    def halves(ex):
        parts = []
        for h in range(2):
            nr = ex.srcs[0].shape[1] // 2
            part = _ToChips(ex.srcs, rows=(h * nr, nr))
            part.names = ex.names
            parts.append(part)
        return parts

    def landed_halves(parts):
        for j, k in enumerate(parts[0].names):
            red[k]["others"] = jnp.concatenate([p.results[j] for p in parts], axis=1)

    def update(k, w, m, v, layer=None, fill=None):
        r = red[k]
        return _adamw_sharded(r["grad"], r["recv"], r["others"], place, w, m, v,
                              f"adamw_{k}", layer=layer, fill=fill)

    g_a_in, g_a_out, g_a_norm, g_a_v_norm, g_conv = _exchange_alone(
        _Gather([bf(a_w_in[0]), bf(a_w_out[0]), a_norm, a_v_norm, f_conv_w.reshape(6, FF_SHARD)]), "gather_first")
    a_norm_full, a_v_norm_full = g_a_norm.reshape(1, d), g_a_v_norm.reshape(1, d)
    conv_w = lax.reduce_precision(g_conv.reshape(N_SHARDS, 2, 3, FF_SHARD), 8, 7)
    cw = jnp.pad(jnp.transpose(conv_w, (1, 0, 2, 3)), ((0, 0), (0, 0), (0, 5), (0, 0)))
    w_a_in_flat = jnp.transpose(g_a_in, (1, 0, 2)).reshape(d, 2 * d)
    cb = f_conv_b.reshape(2, N_SHARDS, 1, FF_SHARD)
    tri = jnp.tril(jnp.ones((CHUNK, CHUNK), dtype=bool))
    w_causal = jnp.where(tri[None], a_w_s[0], 0.0).astype(BF16)
    w_causal_t = jnp.transpose(w_causal, (0, 2, 1))
    b_sb = jnp.broadcast_to(a_b_s[0][:, :, None], (N_GROUPS, CHUNK, CHUNK))
    w_a_out = g_a_out.reshape(d, d)
    gq = jnp.tile(b_q_norm.reshape(1, HEAD_DIM), (1, 2))
    gk = jnp.tile(k_norm.reshape(1, HEAD_DIM), (1, 2))
    sinks = b_sinks.reshape(N_Q_HEADS)

    (h1,) = _rms_fwd(x0, [a_norm_full], "a_norm_fwd")
    ex = _Gather([bf(f_w_in[0]), bf(f_w_out[0])])
    zpre, x1 = _sgu_fwd(x0, h1, g_a_in, a_v_norm_full, w_causal, b_sb, w_a_out, carry=ex)
    w_in0, w_out0 = ex.results[0], ex.results[1].reshape(D_FF, d)
    ex = _Gather([bf(w_kv), bf(b_w_q[0]), bf(b_w_o[0]), bf(f_w_out[1])])
    x2, hf0, a0, pre0 = _ffn_fwd(x1, f_norm[0:1], w_in0, cw[0], cb[0], w_out0, 0, carry=ex)
    kv_full = ex.results[0].reshape(d, 2 * N_KV_HEADS * HEAD_DIM)
    w_q, w_o = ex.results[1].reshape(d, d), ex.results[2].reshape(d, d)
    w_out1 = ex.results[3].reshape(D_FF, d)
    half = N_KV_HEADS * HEAD_DIM
    w_kv_dup = jnp.concatenate([_dup_heads(kv_full[:, :half]), _dup_heads(kv_full[:, half:])], axis=1)
    hk, hq = _rms_fwd(x2, [row(kv_norm), b_norm], "kvq_norm_fwd")
    kvd = _mm_rows(hk, w_kv_dup, F32, "kv_proj")
    qraw = _mm_rows(hq, w_q, F32, "q_proj")
    ex = _Gather([bf(f_w_in[1])])
    o = _attn_fwd(qraw, kvd, gq, gk, sinks, carry=ex)
    w_in1 = ex.results[0]
    x3 = _mm_rows(o, w_o, F32, "o_proj", res=x2)
    x4, hf1, a1, pre1 = _ffn_fwd(x3, f_norm[1:2], w_in1, cw[1], cb[1], w_out1, 1)
    dy, loss_lanes = _loss_head(x4, target)
    loss = lax.psum(loss_lanes[0, 0], ("x", "y", "c"))

    dhu1, dw_out1, dcb1 = _ffn_bwd_act(pre1, w_out1, dy, 1)
    ex = to_sibling({"f_w_out1": dw_out1.reshape(N_SHARDS, D_FF // N_SHARDS, d)})
    da1, dhf1, dcw1 = _ffn_bwd_in(dhu1, a1, cw[1], w_in1, 1, carry=ex)
    ex = to_chips(ex)
    dw_in1 = _ffn_wgrad_in(hf1, da1, 1, carry=ex)
    landed(ex)
    ex = to_sibling({"f_w_in1": dw_in1})
    dx3, dgf1 = _rms_bwd(x3, [f_norm[1:2]], [dhf1], dy, "f1_norm_bwd", carry=ex)
    ex = to_chips(ex)
    d_o = _mm_rows(dx3, w_o, BF16, "o_proj_bwd", trans_w=True)
    dw_o = _mm_wgrad(o, dx3, "o_wgrad").reshape(N_SHARDS, d // N_SHARDS, d)
    dq, dkv, dsink, dgq, dgk = _attn_bwd(qraw, kvd, d_o, gq, gk, sinks, carry=ex)
    landed(ex)
    dw_q = _mm_wgrad(hq, dq, "q_wgrad").reshape(N_SHARDS, d // N_SHARDS, d)
    dw_kv_dup = _mm_wgrad(hk, dkv, "kv_wgrad")
    dw_kv = jnp.concatenate(
        [_fold_heads(dw_kv_dup[:, :4 * LANES]), _fold_heads(dw_kv_dup[:, 4 * LANES:])], axis=1
    ).reshape(N_SHARDS, d // N_SHARDS, 2 * N_KV_HEADS * HEAD_DIM)
    ex = to_sibling({"b_w_o": dw_o, "b_w_q": dw_q, "w_kv": dw_kv})
    dhq = _mm_rows(dq, w_q, F32, "q_proj_bwd", trans_w=True, carry=ex)
    dhk = _mm_rows(dkv, w_kv_dup, F32, "kv_proj_bwd", trans_w=True)
    ex = to_chips(ex)
    dx2, dg2 = _rms_bwd(x2, [row(kv_norm), b_norm], [dhk, dhq], dx3, "kvq_norm_bwd")
    dhu0, dw_out0, dcb0 = _ffn_bwd_act(pre0, w_out0, dx2, 0, carry=ex)
    landed(ex)
    ex = to_sibling({"f_w_out0": dw_out0.reshape(N_SHARDS, D_FF // N_SHARDS, d)})
    da0, dhf0, dcw0 = _ffn_bwd_in(dhu0, a0, cw[0], w_in0, 0, carry=ex)
    ex = to_chips(ex)
    dw_in0 = _ffn_wgrad_in(hf0, da0, 0, carry=ex)
    landed(ex)
    ex = to_sibling({"f_w_in0": dw_in0})
    dx1, dgf0 = _rms_bwd(x1, [f_norm[0:1]], [dhf0], dx2, "f0_norm_bwd", carry=ex)
    ex_lo, ex_hi = halves(to_chips(ex))
    dz, y, dwc, dbs, dgv = _sgu_bwd(dx1, zpre, w_a_out, a_v_norm_full, w_causal, w_causal_t, b_sb, carry=ex_lo)
    dw_a_out = _mm_wgrad(y, dx1, "a_out_wgrad").reshape(N_SHARDS, d // N_SHARDS, d)
    nsub = g_a_in.shape[2]
    dw_a_in = _mm(
        h1, dz, pl.BlockSpec((t, d), lambda s, j, kk: (0, 0)), pl.BlockSpec((t, nsub), lambda s, j, kk: (0, s)),
        pl.BlockSpec((None, d, nsub), lambda s, j, kk: (s, 0, 0)), jax.ShapeDtypeStruct((N_SHARDS, d, nsub), F32),
        (N_SHARDS, 1, 1), TN, "a_in_wgrad", carry=ex_hi)
    landed_halves([ex_lo, ex_hi])

    def bias_grad(dcb):
        return jnp.transpose(dcb[:, :, 0, :], (1, 0, 2)).reshape(-1)

    g_conv_w = jnp.concatenate([dcw0[:, 0:3, :], dcw1[:, 0:3, :]], axis=1)
    g_a_v_norm = dgv[0].reshape(N_SHARDS, 1, LANES)
    rep = ["a_w_s", "a_b_s", "f_norm", "f_conv_b", "kv_norm", "k_norm", "b_norm", "b_q_norm", "b_sinks"]
    rep_g = dict(
        a_w_s=dwc.reshape(N_GROUPS * CHUNK, CHUNK), a_b_s=dbs[:, :, 0], f_norm=jnp.stack([dgf0[0], dgf1[0]]),
        f_conv_b=jnp.stack([bias_grad(dcb0), bias_grad(dcb1)]), kv_norm=dg2[0:1],
        k_norm=(dgk[0, :HEAD_DIM] + dgk[0, HEAD_DIM:])[None], b_norm=dg2[1:2],
        b_q_norm=(dgq[0, :HEAD_DIM] + dgq[0, HEAD_DIM:])[None], b_sinks=dsink[:, 0][None])
    ex_big = to_sibling({"a_w_out": dw_a_out, "a_w_in": dw_a_in})
    ex_small = to_sibling({"a_v_norm": g_a_v_norm, "f_conv_w": g_conv_w}, wire=F32)
    ex_rep = _Gather([rep_g[k] for k in rep])
    together = _Together([ex_big, ex_small, ex_rep])
    dh1 = _mm_rows(dz, w_a_in_flat, F32, "a_in_bwd", trans_w=True, carry=together)
    together.spread()
    ex_big, ex_small = to_chips(ex_big), to_chips(ex_small)
    together = _Together([ex_big, ex_small])
    grad_x, dg0 = _rms_bwd(x0, [a_norm_full], [dh1], dx1, "a_norm_bwd", carry=together)
    together.spread()
    landed(ex_big)
    landed(ex_small)
    (a_norm_parts,) = _exchange_alone(_ToOwners([dg0[0].reshape(N_SHARDS, 1, LANES)]), "a_norm_to_owners")

    res["f_w_out"] = update("f_w_out1", f_w_out, m_f_w_out, v_f_w_out, layer=1)
    res["f_w_in"] = update("f_w_in1", f_w_in, m_f_w_in, v_f_w_in, layer=1)
    res["b_w_o"] = update("b_w_o", b_w_o, m_b_w_o, v_b_w_o, layer=0)
    res["b_w_q"] = update("b_w_q", b_w_q, m_b_w_q, v_b_w_q, layer=0)
    res["w_kv"] = update("w_kv", w_kv, m_w_kv, v_w_kv)
    res["f_w_out"] = update("f_w_out0", f_w_out, m_f_w_out, v_f_w_out, layer=0, fill=res["f_w_out"])
    res["f_w_in"] = update("f_w_in0", f_w_in, m_f_w_in, v_f_w_in, layer=0, fill=res["f_w_in"])
    res["a_w_out"] = update("a_w_out", a_w_out, m_a_w_out, v_a_w_out, layer=0)
    res["a_w_in"] = update("a_w_in", a_w_in, m_a_w_in, v_a_w_in, layer=0)
    res["a_v_norm"] = update("a_v_norm", a_v_norm, m_a_v_norm, v_a_v_norm)
    res["f_conv_w"] = [o_.reshape(f_conv_w.shape) for o_ in update(
        "f_conv_w", f_conv_w.reshape(6, FF_SHARD), m_f_conv_w.reshape(6, FF_SHARD), v_f_conv_w.reshape(6, FF_SHARD))]

    rep_w = dict(a_w_s=a_w_s, a_b_s=a_b_s, f_norm=f_norm, f_conv_b=f_conv_b, kv_norm=kv_norm, k_norm=k_norm,
                 b_norm=b_norm, b_q_norm=b_q_norm, b_sinks=b_sinks, a_norm=a_norm)
    rep_m = dict(a_w_s=m_a_w_s, a_b_s=m_a_b_s, f_norm=m_f_norm, f_conv_b=m_f_conv_b, kv_norm=m_kv_norm,
                 k_norm=m_k_norm, b_norm=m_b_norm, b_q_norm=m_b_q_norm, b_sinks=m_b_sinks, a_norm=m_a_norm)
    rep_v = dict(a_w_s=v_a_w_s, a_b_s=v_a_b_s, f_norm=v_f_norm, f_conv_b=v_f_conv_b, kv_norm=v_kv_norm,
                 k_norm=v_k_norm, b_norm=v_b_norm, b_q_norm=v_b_q_norm, b_sinks=v_b_sinks, a_norm=v_a_norm)
    keys = rep + ["a_norm"]
    parts = ex_rep.results + [a_norm_parts]
    as2d = lambda a, p: a.reshape(p.shape[1:])
    rep_outs = _adamw_summed(parts, [as2d(rep_w[k], p) for k, p in zip(keys, parts)],
                             [as2d(rep_m[k], p) for k, p in zip(keys, parts)],
                             [as2d(rep_v[k], p) for k, p in zip(keys, parts)], "adamw_replicated")
    for j, key in enumerate(keys):
        res[key] = [o_.reshape(rep_w[key].shape) for o_ in rep_outs[j]]

    order = ["a_norm", "a_w_in", "a_v_norm", "a_w_s", "a_b_s", "a_w_out", "f_norm", "f_w_in", "f_conv_w", "f_conv_b",
             "f_w_out", "kv_norm", "w_kv", "k_norm", "b_norm", "b_w_q", "b_q_norm", "b_sinks", "b_w_o"]
    outs = [loss, grad_x[None]]
    for j in range(4):
        outs += [res[k][j] for k in order]
    return tuple(outs)
```

```python
import jax
import jax.numpy as jnp
from jax import lax
from jax.experimental import pallas as pl
from jax.experimental.pallas import tpu as pltpu

F32 = jnp.float32
BF16 = jnp.bfloat16
EPS = 1e-6
D_MODEL = 1024
CHUNK = 128
N_GROUPS = 8
N_SHARDS = 8
HEAD_DIM = 64
N_Q_HEADS = 16
N_KV_HEADS = 4
D_FF = 2816
FF_SHARD = 2 * D_FF // N_SHARDS
LANES = 128
NEG_BIG = -1e30
ADAM_LR = 0.001
ADAM_B1 = 0.9
ADAM_B2 = 0.999
ADAM_EPS = 1e-08
ADAM_WD = 0.01
ADAM_STEP = 10
VMEM_LIMIT_BYTES = 56 * 1024 * 1024
MESH = pl.DeviceIdType.MESH

NN = (((1,), (0,)), ((), ()))
NT = (((1,), (1,)), ((), ()))
TN = (((0,), (0,)), ((), ()))
SLOPES = tuple(2.0 ** (-8.0 * (h + 1) / N_Q_HEADS) for h in range(N_Q_HEADS))


def _params(sem=None):
    return pltpu.CompilerParams(dimension_semantics=sem, vmem_limit_bytes=VMEM_LIMIT_BYTES)


def _dot(a, b, dims=NN):
    return lax.dot_general(a, b, dims, preferred_element_type=F32)


def _sigmoid(x):
    return 1.0 / (1.0 + jnp.exp(-x))


def _gelu_parts(z):
    cdf = 0.5 * (1.0 + lax.erf(z * (2.0 ** -0.5)))
    pdf = jnp.exp(-0.5 * z * z) * 0.3989422804014327
    return cdf, pdf


def _coords():
    return lax.axis_index("x"), lax.axis_index("y"), lax.axis_index("c")


class _Gather:
    def __init__(self, srcs):
        self.srcs = list(srcs)
        n = len(self.srcs)
        self.out_shapes = [jax.ShapeDtypeStruct((N_SHARDS,) + s.shape, s.dtype) for s in self.srcs]
        self.sems = [pltpu.SemaphoreType.DMA((n, 7)), pltpu.SemaphoreType.DMA((n, 7)), pltpu.SemaphoreType.DMA((n,))]

    def _plan(self, src, dst, sems):
        send_sems, recv_sems, local_sems = sems
        x, y, c = _coords()
        me, sibling = (x, y, c), (x, y, 1 - c)
        chips = [(1 - x, y), (x, 1 - y), (1 - x, 1 - y)]
        n = len(src)

        def rows(e, dev):
            return dst[e].at[4 * dev[0] + 2 * dev[1] + dev[2]]

        def copy(e, slot, block, to, from_own=False):
            return pltpu.make_async_remote_copy(
                src_ref=src[e] if from_own else rows(e, block), dst_ref=rows(e, block),
                send_sem=send_sems.at[e, slot], recv_sem=recv_sems.at[e, slot], device_id=to, device_id_type=MESH)

        mine = [pltpu.make_async_copy(src[e], rows(e, me), local_sems.at[e]) for e in range(n)]
        first = []
        for e in range(n):
            first.append(copy(e, 0, me, sibling, from_own=True))
            first += [copy(e, 1 + j, me, (*chip, c), from_own=True) for j, chip in enumerate(chips)]
        return n, me, sibling, chips, c, copy, mine, first

    def start(self, src, dst, sems):
        _, _, _, _, _, _, mine, first = self._plan(src, dst, sems)
        for cp in mine + first:
            cp.start()

    def finish(self, src, dst, sems):
        n, me, sibling, chips, c, copy, mine, first = self._plan(src, dst, sems)
        passed = []
        for j, chip in enumerate(chips):
            for e in range(n):
                copy(e, 1 + j, (*chip, c), me).wait_recv()
                cp = copy(e, 4 + j, (*chip, c), sibling)
                cp.start()
                passed.append(cp)
        for e in range(n):
            copy(e, 0, sibling, me).wait_recv()
            for j, chip in enumerate(chips):
                copy(e, 4 + j, (*chip, 1 - c), me).wait_recv()
        for cp in first + passed:
            cp.wait_send()
        for cp in mine:
            cp.wait()


class _ToSibling:
    def __init__(self, grads):
        self.srcs = list(grads)
        n = len(self.srcs)
        self.out_shapes = [jax.ShapeDtypeStruct((4,) + g.shape[1:], g.dtype) for g in self.srcs]
        self.sems = [pltpu.SemaphoreType.DMA((n, 4)), pltpu.SemaphoreType.DMA((n, 4))]

    def _copies(self, src, dst, sems):
        send_sems, recv_sems = sems
        x, y, c = _coords()
        return [
            pltpu.make_async_remote_copy(
                src_ref=src[i].at[2 * q + (1 - c)], dst_ref=dst[i].at[q], send_sem=send_sems.at[i, q],
                recv_sem=recv_sems.at[i, q], device_id=(x, y, 1 - c), device_id_type=MESH)
            for i in range(len(src)) for q in range(4)]

    def start(self, src, dst, sems):
        for cp in self._copies(src, dst, sems):
            cp.start()

    def finish(self, src, dst, sems):
        for cp in self._copies(src, dst, sems):
            cp.wait()


class _ToChips:
    def __init__(self, psums, rows=None):
        self.srcs = list(psums)
        n = len(self.srcs)
        self.rows = rows
        self.out_shapes = [
            jax.ShapeDtypeStruct((3, p.shape[1] if rows is None else rows[1]) + p.shape[2:], p.dtype)
            for p in self.srcs]
        self.sems = [pltpu.SemaphoreType.DMA((n, 3)), pltpu.SemaphoreType.DMA((n, 3))]

    def _copies(self, src, dst, sems):
        send_sems, recv_sems = sems
        x, y, c = _coords()
        peers = [(x, 1 - y), (1 - x, y), (1 - x, 1 - y)]

        def part(i, q):
            if self.rows is None:
                return src[i].at[q]
            return src[i].at[q, pl.ds(self.rows[0], self.rows[1])]

        return [
            pltpu.make_async_remote_copy(
                src_ref=part(i, 2 * px + py), dst_ref=dst[i].at[r], send_sem=send_sems.at[i, r],
                recv_sem=recv_sems.at[i, r], device_id=(px, py, c), device_id_type=MESH)
            for i in range(len(src)) for r, (px, py) in enumerate(peers)]

    def start(self, src, dst, sems):
        for cp in self._copies(src, dst, sems):
            cp.start()

    def finish(self, src, dst, sems):
        for cp in self._copies(src, dst, sems):
            cp.wait()


class _ToOwners:
    def __init__(self, grads):
        self.srcs = list(grads)
        n = len(self.srcs)
        self.out_shapes = [jax.ShapeDtypeStruct(g.shape, g.dtype) for g in self.srcs]
        self.sems = [pltpu.SemaphoreType.DMA((n, 7)), pltpu.SemaphoreType.DMA((n, 7)), pltpu.SemaphoreType.DMA((n,))]

    def _copies(self, src, dst, sems):
        send_sems, recv_sems, local_sems = sems
        x, y, c = _coords()
        me = 4 * x + 2 * y + c
        copies = [pltpu.make_async_copy(src[i].at[me], dst[i].at[me], local_sems.at[i]) for i in range(len(src))]
        for i in range(len(src)):
            for rel in range(1, N_SHARDS):
                px = x ^ (rel >> 2) if rel >> 2 else x
                py = y ^ ((rel >> 1) & 1) if (rel >> 1) & 1 else y
                pc = c ^ (rel & 1) if rel & 1 else c
                copies.append(pltpu.make_async_remote_copy(
                    src_ref=src[i].at[4 * px + 2 * py + pc], dst_ref=dst[i].at[me], send_sem=send_sems.at[i, rel - 1],
                    recv_sem=recv_sems.at[i, rel - 1], device_id=(px, py, pc), device_id_type=MESH))
        return copies

    def start(self, src, dst, sems):
        for cp in self._copies(src, dst, sems):
            cp.start()

    def finish(self, src, dst, sems):
        for cp in self._copies(src, dst, sems):
            cp.wait()


class _Together:
    def __init__(self, parts):
        self.parts = list(parts)
        self.srcs = [s for p in self.parts for s in p.srcs]
        self.out_shapes = [s for p in self.parts for s in p.out_shapes]
        self.sems = [s for p in self.parts for s in p.sems]

    def _split(self, src, dst, sems):
        a = b = c = 0
        for p in self.parts:
            na, nc = len(p.srcs), len(p.sems)
            yield p, src[a:a + na], dst[b:b + na], sems[c:c + nc]
            a, b, c = a + na, b + na, c + nc

    def start(self, src, dst, sems):
        for p, s, d, m in self._split(src, dst, sems):
            p.start(s, d, m)

    def finish(self, src, dst, sems):
        for p, s, d, m in self._split(src, dst, sems):
            p.finish(s, d, m)

    def spread(self):
        b = 0
        for p in self.parts:
            p.results = self.results[b:b + len(p.srcs)]
            b += len(p.srcs)


def _call(body, args, *, grid, in_specs, out_specs, out_shape, name, scratch=(), sem=None, carry=None):
    out_shape, out_specs = list(out_shape), list(out_specs)
    if carry is None:
        return pl.pallas_call(
            body, grid=grid, in_specs=list(in_specs), out_specs=out_specs, out_shape=out_shape,
            scratch_shapes=list(scratch), name=name, compiler_params=_params(sem))(*args)
    n_in, n_out, n_scr, n_c = len(args), len(out_shape), len(scratch), len(carry.srcs)
    steps = tuple(grid)

    def carried(*refs):
        ins, rest = refs[:n_in], refs[n_in:]
        c_src, rest = rest[:n_c], rest[n_c:]
        outs, rest = rest[:n_out], rest[n_out:]
        c_dst, rest = rest[:n_c], rest[n_c:]
        scr, sems = rest[:n_scr], rest[n_scr:]
        first = pl.program_id(0) == 0
        last = pl.program_id(0) == steps[0] - 1
        for ax in range(1, len(steps)):
            first = first & (pl.program_id(ax) == 0)
            last = last & (pl.program_id(ax) == steps[ax] - 1)

        @pl.when(first)
        def _():
            carry.start(c_src, c_dst, sems)

        body(*ins, *outs, *scr)

        @pl.when(last)
        def _():
            carry.finish(c_src, c_dst, sems)

    hbm = pl.BlockSpec(memory_space=pl.ANY)
    res = pl.pallas_call(
        carried, grid=grid, in_specs=list(in_specs) + [hbm] * n_c, out_specs=out_specs + [hbm] * n_c,
        out_shape=out_shape + carry.out_shapes, scratch_shapes=list(scratch) + carry.sems, name=name,
        compiler_params=_params(("arbitrary",) * len(steps)))(*args, *carry.srcs)
    carry.results = list(res[n_out:])
    return list(res[:n_out])


def _exchange_alone(ex, name):
    n = len(ex.srcs)

    def body(*refs):
        src, dst, sems = refs[:n], refs[n:2 * n], refs[2 * n:]
        ex.start(src, dst, sems)
        ex.finish(src, dst, sems)

    hbm = pl.BlockSpec(memory_space=pl.ANY)
    res = pl.pallas_call(body, in_specs=[hbm] * n, out_specs=[hbm] * n, out_shape=ex.out_shapes,
                         scratch_shapes=ex.sems, name=name)(*ex.srcs)
    ex.results = list(res)
    return ex.results


def _rms_fwd(x, gains, name, tm=512, carry=None):
    t, d = x.shape
    n = len(gains)

    def body(*refs):
        x_ref, g_refs, h_refs = refs[0], refs[1:1 + n], refs[1 + n:]
        xf = x_ref[...]
        xhat = xf * lax.rsqrt(jnp.mean(xf * xf, axis=-1, keepdims=True) + EPS)
        for g_ref, h_ref in zip(g_refs, h_refs):
            h_ref[...] = (xhat * g_ref[...]).astype(BF16)

    row = pl.BlockSpec((tm, d), lambda i: (i, 0))
    vec = pl.BlockSpec((1, d), lambda i: (0, 0))
    return _call(body, [x, *gains], grid=(t // tm,), in_specs=[row] + [vec] * n, out_specs=[row] * n,
                 out_shape=[jax.ShapeDtypeStruct((t, d), BF16)] * n, name=name, carry=carry)


def _rms_bwd(x, gains, dhs, dres, name, tm=256, carry=None):
    t, d = x.shape
    n = len(gains)

    def body(*refs):
        x_ref, dres_ref = refs[0], refs[1]
        g_refs, dh_refs = refs[2:2 + n], refs[2 + n:2 + 2 * n]
        dx_ref, dg_ref = refs[2 + 2 * n], refs[3 + 2 * n]
        i = pl.program_id(0)

        @pl.when(i == 0)
        def _():
            dg_ref[...] = jnp.zeros_like(dg_ref)

        xf = x_ref[...]
        r = lax.rsqrt(jnp.mean(xf * xf, axis=-1, keepdims=True) + EPS)
        xhat = xf * r
        dx = dres_ref[...]
        for j in range(n):
            dh = dh_refs[j][...]
            dg_ref[j:j + 1, :] += jnp.sum(dh * xhat, axis=0, keepdims=True)
            gy = dh * g_refs[j][...]
            dx = dx + r * (gy - xhat * jnp.mean(gy * xhat, axis=-1, keepdims=True))
        dx_ref[...] = dx

    row = pl.BlockSpec((tm, d), lambda i: (i, 0))
    vec = pl.BlockSpec((1, d), lambda i: (0, 0))
    return _call(body, [x, dres, *gains, *dhs], grid=(t // tm,), in_specs=[row, row] + [vec] * n + [row] * n,
                 out_specs=[row, pl.BlockSpec((8, d), lambda i: (0, 0))],
                 out_shape=[jax.ShapeDtypeStruct((t, d), F32), jax.ShapeDtypeStruct((8, d), F32)],
                 name=name, sem=("arbitrary",), carry=carry)


def _mm(a, b, a_spec, b_spec, o_spec, out_shape, grid, dims, name, res=None, res_spec=None, carry=None):
    nk = grid[2]
    acc_shape = tuple(s for s in o_spec.block_shape if s is not None)

    def body(*refs):
        a_ref, b_ref = refs[0], refs[1]
        r_ref = refs[2] if res is not None else None
        o_ref = refs[3] if res is not None else refs[2]
        p = _dot(a_ref[...].astype(BF16), b_ref[...].astype(BF16), dims)
        if nk == 1:
            if res is not None:
                p = p + r_ref[...]
            o_ref[...] = p.astype(o_ref.dtype)
            return
        acc_ref = refs[-1]
        k = pl.program_id(2)

        @pl.when(k == 0)
        def _():
            acc_ref[...] = p

        @pl.when(k > 0)
        def _():
            acc_ref[...] += p

        @pl.when(k == nk - 1)
        def _():
            out = acc_ref[...]
            if res is not None:
                out = out + r_ref[...]
            o_ref[...] = out.astype(o_ref.dtype)

    ins = [a, b] + ([res] if res is not None else [])
    specs = [a_spec, b_spec] + ([res_spec] if res is not None else [])
    return _call(body, ins, grid=grid, in_specs=specs, out_specs=[o_spec], out_shape=[out_shape],
                 scratch=[pltpu.VMEM(acc_shape, F32)] if nk > 1 else [], name=name,
                 sem=("parallel", "parallel", "arbitrary"), carry=carry)[0]


def _mm_rows(a, w, out_dtype, name, trans_w=False, res=None, tm=512, carry=None):
    t, k = a.shape
    n = w.shape[0] if trans_w else w.shape[1]
    return _mm(
        a, w, pl.BlockSpec((tm, k), lambda i, j, kk: (i, 0)), pl.BlockSpec(w.shape, lambda i, j, kk: (0, 0)),
        pl.BlockSpec((tm, n), lambda i, j, kk: (i, 0)), jax.ShapeDtypeStruct((t, n), out_dtype), (t // tm, 1, 1),
        NT if trans_w else NN, name, res=res,
        res_spec=None if res is None else pl.BlockSpec((tm, n), lambda i, j, kk: (i, 0)), carry=carry)


def _mm_wgrad(a, b, name, carry=None):
    t, m = a.shape
    n = b.shape[1]
    tn = n // (4 if b.dtype == F32 else 2)
    return _mm(
        a, b, pl.BlockSpec((t, m), lambda i, j, kk: (0, 0)), pl.BlockSpec((t, tn), lambda i, j, kk: (0, j)),
        pl.BlockSpec((m, tn), lambda i, j, kk: (0, j)), jax.ShapeDtypeStruct((m, n), F32), (1, n // tn, 1), TN, name,
        carry=carry)


def _sgu_fwd(x0, h1, w_in, g_v, w_c, b_sb, w_out, tm=256, carry=None):
    t, d = x0.shape
    nsub = w_in.shape[2]

    def body(x_ref, h_ref, win_ref, gv_ref, wc_ref, bsb_ref, wout_ref, zpre_ref, x1_ref, u_s, v_s, vn_s, y_s):
        h = h_ref[...]
        for k in range(N_SHARDS):
            zk = _dot(h, win_ref[k])
            zpre_ref[:, k * nsub:(k + 1) * nsub] = zk
            cdf, _ = _gelu_parts(zk)
            if k < N_SHARDS // 2:
                u_s[:, k * nsub:(k + 1) * nsub] = zk * cdf
            else:
                v_s[:, (k - 4) * nsub:(k - 3) * nsub] = zk * cdf
        v = v_s[...]
        rv = lax.rsqrt(jnp.mean(v * v, axis=-1, keepdims=True) + EPS)
        vn_s[...] = (v * rv * gv_ref[...]).astype(BF16)
        for ci in range(tm // CHUNK):
            rows = slice(ci * CHUNK, (ci + 1) * CHUNK)
            for g in range(N_GROUPS):
                cols = slice(g * LANES, (g + 1) * LANES)
                sv = _dot(wc_ref[g], vn_s[rows, cols]) + bsb_ref[g]
                y_s[rows, cols] = (u_s[rows, cols] * sv).astype(BF16)
        x1_ref[...] = x_ref[...] + _dot(y_s[...], wout_ref[...])

    row = pl.BlockSpec((tm, d), lambda i: (i, 0))
    full = lambda a: pl.BlockSpec(a.shape, lambda i: (0,) * a.ndim)
    return _call(
        body, [x0, h1, w_in, g_v, w_c, b_sb, w_out], grid=(t // tm,),
        in_specs=[row, row, full(w_in), full(g_v), full(w_c), full(b_sb), full(w_out)],
        out_specs=[pl.BlockSpec((tm, 2 * d), lambda i: (i, 0)), row],
        out_shape=[jax.ShapeDtypeStruct((t, 2 * d), F32), jax.ShapeDtypeStruct((t, d), F32)],
        scratch=[pltpu.VMEM((tm, d), F32), pltpu.VMEM((tm, d), F32), pltpu.VMEM((tm, d), BF16),
                 pltpu.VMEM((tm, d), BF16)],
        name="sgu_fwd", carry=carry)


def _sgu_bwd(dx1, zpre, w_out, g_v, w_c, w_ct, b_sb, tm=256, carry=None):
    t, d = dx1.shape

    def body(dx_ref, zpre_ref, wout_ref, gv_ref, wc_ref, wct_ref, bsb_ref,
             dz_ref, y_ref, dwc_ref, dbs_ref, dgv_ref, u_s, vn_s, dy_s, du_s, dvn_s):
        i = pl.program_id(0)

        @pl.when(i == 0)
        def _():
            dwc_ref[...] = jnp.zeros_like(dwc_ref)
            dbs_ref[...] = jnp.zeros_like(dbs_ref)
            dgv_ref[...] = jnp.zeros_like(dgv_ref)

        dy_s[...] = _dot(dx_ref[...].astype(BF16), wout_ref[...], NT)
        zu = zpre_ref[:, :d]
        zv = zpre_ref[:, d:]
        cdf_u, pdf_u = _gelu_parts(zu)
        cdf_v, pdf_v = _gelu_parts(zv)
        u_s[...] = zu * cdf_u
        v = zv * cdf_v
        rv = lax.rsqrt(jnp.mean(v * v, axis=-1, keepdims=True) + EPS)
        vhat = v * rv
        gv = gv_ref[...]
        vn_s[...] = (vhat * gv).astype(BF16)
        for ci in range(tm // CHUNK):
            rows = slice(ci * CHUNK, (ci + 1) * CHUNK)
            for g in range(N_GROUPS):
                cols = slice(g * LANES, (g + 1) * LANES)
                vnb = vn_s[rows, cols]
                sv = _dot(wc_ref[g], vnb) + bsb_ref[g]
                dyb = dy_s[rows, cols]
                ub = u_s[rows, cols]
                dsv = dyb * ub
                du_s[rows, cols] = dyb * sv
                y_ref[rows, cols] = (ub * sv).astype(BF16)
                dsvb = dsv.astype(BF16)
                dbs_ref[g] += dsv
                dwc_ref[g] += _dot(dsvb, vnb, NT)
                dvn_s[rows, cols] = _dot(wct_ref[g], dsvb)
        dvn = dvn_s[...]
        dgv_ref[0:1, :] += jnp.sum(dvn * vhat, axis=0, keepdims=True)
        gy = dvn * gv
        dv = rv * (gy - vhat * jnp.mean(gy * vhat, axis=-1, keepdims=True))
        dz_ref[:, :d] = (du_s[...] * (cdf_u + zu * pdf_u)).astype(BF16)
        dz_ref[:, d:] = (dv * (cdf_v + zv * pdf_v)).astype(BF16)

        @pl.when(i == t // tm - 1)
        def _():
            tri = (lax.broadcasted_iota(jnp.int32, (CHUNK, CHUNK), 0)
                   >= lax.broadcasted_iota(jnp.int32, (CHUNK, CHUNK), 1))
            for g in range(N_GROUPS):
                dwc_ref[g] = jnp.where(tri, dwc_ref[g], 0.0)
                dbs_ref[g] = jnp.broadcast_to(jnp.sum(dbs_ref[g], axis=1, keepdims=True), (CHUNK, CHUNK))

    row = pl.BlockSpec((tm, d), lambda i: (i, 0))
    row2 = pl.BlockSpec((tm, 2 * d), lambda i: (i, 0))
    full = lambda a: pl.BlockSpec(a.shape, lambda i: (0,) * a.ndim)
    grp = pl.BlockSpec((N_GROUPS, CHUNK, CHUNK), lambda i: (0, 0, 0))
    return _call(
        body, [dx1, zpre, w_out, g_v, w_c, w_ct, b_sb], grid=(t // tm,),
        in_specs=[row, row2, full(w_out), full(g_v), full(w_c), full(w_ct), full(b_sb)],
        out_specs=[row2, row, grp, grp, pl.BlockSpec((8, d), lambda i: (0, 0))],
        out_shape=[jax.ShapeDtypeStruct((t, 2 * d), BF16), jax.ShapeDtypeStruct((t, d), BF16),
                   jax.ShapeDtypeStruct((N_GROUPS, CHUNK, CHUNK), F32),
                   jax.ShapeDtypeStruct((N_GROUPS, CHUNK, CHUNK), F32), jax.ShapeDtypeStruct((8, d), F32)],
        scratch=[pltpu.VMEM((tm, d), F32), pltpu.VMEM((tm, d), BF16), pltpu.VMEM((tm, d), F32),
                 pltpu.VMEM((tm, d), F32), pltpu.VMEM((tm, d), F32)],
        name="sgu_bwd", sem=("arbitrary",), carry=carry)


ROW_CHUNK = 256
HALO = 16


def _ffn_fwd(x, g, w_in, cw, cb, w_out, layer, tm=512, carry=None):
    t, d = x.shape
    nc = N_SHARDS // 2

    def body(x_ref, xp_ref, g_ref, wg_ref, wu_ref, cwg_ref, cbg_ref, cwu_ref, cbu_ref, wout_ref,
             o_ref, hf_ref, a_ref, pre_ref, hw_s):
        i, c = pl.program_id(0), pl.program_id(1)

        @pl.when(c == 0)
        def _():
            keep = jnp.where(i == 0, 0.0, 1.0)
            xw = jnp.concatenate([xp_ref[...] * keep, x_ref[...]], axis=0)
            xhat = xw * lax.rsqrt(jnp.mean(xw * xw, axis=-1, keepdims=True) + EPS)
            hw_s[...] = (xhat * g_ref[...]).astype(BF16)
            hf_ref[...] = hw_s[HALO:, :]
            o_ref[...] = x_ref[...]

        hw = hw_s[...]
        pre = []
        for j, (w_ref, cw_ref, cb_ref) in enumerate(((wg_ref, cwg_ref, cbg_ref), (wu_ref, cwu_ref, cbu_ref))):
            ab = _dot(hw, w_ref[...]).astype(BF16)
            a_ref[j] = ab[HALO:]
            win = ab.astype(F32)
            cw_v = cw_ref[...]
            pre.append(cw_v[2:3, :] * win[HALO:] + cw_v[1:2, :] * pltpu.roll(win, 1, 0)[HALO:]
                       + cw_v[0:1, :] * pltpu.roll(win, 2, 0)[HALO:] + cb_ref[...])
            pre_ref[j] = pre[j]
        act = (pre[0] * _sigmoid(pre[0]) * pre[1]).astype(BF16)
        o_ref[...] += _dot(act, wout_ref[...])

    row = pl.BlockSpec((tm, d), lambda i, c: (i, 0))
    shard = lambda rows, up: pl.BlockSpec((None, rows, FF_SHARD), lambda i, c: (c + up * nc, 0, 0))
    pair = pl.BlockSpec((2, None, tm, FF_SHARD), lambda i, c: (0, c, i, 0))
    outs = _call(
        body, [x, x, g, w_in, w_in, cw, cb, cw, cb, w_out], grid=(t // tm, nc),
        in_specs=[row, pl.BlockSpec((HALO, d), lambda i, c: (jnp.maximum(i * (tm // HALO) - 1, 0), 0)),
                  pl.BlockSpec((1, d), lambda i, c: (0, 0)), shard(d, 0), shard(d, 1),
                  shard(8, 0), shard(1, 0), shard(8, 1), shard(1, 1), pl.BlockSpec((FF_SHARD, d), lambda i, c: (c, 0))],
        out_specs=[row, row, pair, pair],
        out_shape=[jax.ShapeDtypeStruct((t, d), F32), jax.ShapeDtypeStruct((t, d), BF16),
                   jax.ShapeDtypeStruct((2, nc, t, FF_SHARD), BF16), jax.ShapeDtypeStruct((2, nc, t, FF_SHARD), F32)],
        scratch=[pltpu.VMEM((tm + HALO, d), BF16)], name=f"ffn{layer}_fwd", sem=("parallel", "arbitrary"), carry=carry)
    return outs[0], outs[1], outs[2].reshape(N_SHARDS, t, FF_SHARD), outs[3]


def _ffn_bwd_act(pre, w_out, dxn, layer, tm=512, carry=None):
    t, d = dxn.shape
    nc = N_SHARDS // 2

    def body(pre_ref, wout_ref, dx_ref, dhu_ref, dw_ref, dcb_ref):
        i = pl.program_id(1)

        @pl.when(i == 0)
        def _():
            dw_ref[...] = jnp.zeros_like(dw_ref)
            dcb_ref[...] = jnp.zeros_like(dcb_ref)

        hg, hu = pre_ref[0], pre_ref[1]
        sg = _sigmoid(hg)
        sl = hg * sg
        dxb = dx_ref[...].astype(BF16)
        dact = _dot(dxb, wout_ref[...], NT)
        dw_ref[...] += _dot((sl * hu).astype(BF16), dxb, TN)
        d_up = dact * sl
        d_gate = dact * hu * (sg * (1.0 + hg * (1.0 - sg)))
        for j, dv in enumerate((d_gate, d_up)):
            dhu_ref[j] = dv.astype(BF16)
            dcb_ref[j, 0:1, :] += jnp.sum(dv, axis=0, keepdims=True)

    return _call(
        body, [pre, w_out, dxn], grid=(nc, t // tm),
        in_specs=[pl.BlockSpec((2, None, tm, FF_SHARD), lambda c, i: (0, c, i, 0)),
                  pl.BlockSpec((FF_SHARD, d), lambda c, i: (c, 0)), pl.BlockSpec((tm, d), lambda c, i: (i, 0))],
        out_specs=[pl.BlockSpec((None, 2, tm, FF_SHARD), lambda c, i: (c, 0, i, 0)),
                   pl.BlockSpec((FF_SHARD, d), lambda c, i: (c, 0)),
                   pl.BlockSpec((None, 2, 8, FF_SHARD), lambda c, i: (c, 0, 0, 0))],
        out_shape=[jax.ShapeDtypeStruct((nc, 2, t, FF_SHARD), BF16), jax.ShapeDtypeStruct((D_FF, d), F32),
                   jax.ShapeDtypeStruct((nc, 2, 8, FF_SHARD), F32)],
        name=f"ffn{layer}_bwd_act", sem=("parallel", "arbitrary"), carry=carry)


def _ffn_bwd_in(dhu, a, cw, w_in, layer, tm=1024, carry=None):
    nc, _, t, _ = dhu.shape
    d = D_MODEL
    tm = min(tm, t)
    last_blk = t // 16 - 1

    def body(dh_ref, nx_ref, a_ref, cw_ref, win_ref, da_ref, o_ref, dcw_ref):
        i, s = pl.program_id(0), pl.program_id(1)

        @pl.when(s == 0)
        def _():
            o_ref[...] = jnp.zeros_like(o_ref)

        @pl.when((s == 0) & (i == 0))
        def _():
            dcw_ref[...] = jnp.zeros_like(dcw_ref)

        keep = jnp.where(i == t // tm - 1, 0.0, 1.0)
        cw = cw_ref[...]
        sums = [None] * 3
        for r0 in range(0, tm, ROW_CHUNK):
            rows = slice(r0, r0 + ROW_CHUNK)
            if r0 + ROW_CHUNK == tm:
                win = jnp.concatenate([dh_ref[rows, :].astype(F32), nx_ref[...].astype(F32) * keep], axis=0)
            else:
                win = dh_ref[r0:r0 + ROW_CHUNK + HALO, :].astype(F32)
            n = ROW_CHUNK + HALO
            taps = (pltpu.roll(win, n - 2, 0)[:ROW_CHUNK],
                    pltpu.roll(win, n - 1, 0)[:ROW_CHUNK],
                    win[:ROW_CHUNK])
            da = (cw[0:1, :] * taps[0] + cw[1:2, :] * taps[1] + cw[2:3, :] * taps[2]).astype(BF16)
            da_ref[rows, :] = da
            o_ref[rows, :] += _dot(da, win_ref[...], NT)
            af = a_ref[rows, :].astype(F32)
            parts = [jnp.sum(taps[k] * af, axis=0, keepdims=True) for k in range(3)]
            sums = [p if q is None else q + p for q, p in zip(sums, parts)]
        for k in range(3):
            dcw_ref[pl.ds(s, 1), k:k + 1, :] += sums[k][None]

    return _call(
        body, [dhu, dhu, a, cw, w_in], grid=(t // tm, N_SHARDS),
        in_specs=[pl.BlockSpec((None, None, tm, FF_SHARD), lambda i, s: (s % nc, s // nc, i, 0)),
                  pl.BlockSpec((None, None, 16, FF_SHARD),
                               lambda i, s: (s % nc, s // nc, jnp.minimum((i + 1) * (tm // 16), last_blk), 0)),
                  pl.BlockSpec((None, tm, FF_SHARD), lambda i, s: (s, i, 0)),
                  pl.BlockSpec((None, 8, FF_SHARD), lambda i, s: (s, 0, 0)),
                  pl.BlockSpec((None, d, FF_SHARD), lambda i, s: (s, 0, 0))],
        out_specs=[pl.BlockSpec((None, tm, FF_SHARD), lambda i, s: (s, i, 0)),
                   pl.BlockSpec((tm, d), lambda i, s: (i, 0)),
                   pl.BlockSpec((N_SHARDS, 8, FF_SHARD), lambda i, s: (0, 0, 0))],
        out_shape=[jax.ShapeDtypeStruct((N_SHARDS, t, FF_SHARD), BF16), jax.ShapeDtypeStruct((t, d), F32),
                   jax.ShapeDtypeStruct((N_SHARDS, 8, FF_SHARD), F32)],
        name=f"ffn{layer}_bwd_in", sem=("arbitrary", "arbitrary"), carry=carry)


def _ffn_wgrad_in(hf, da, layer, carry=None):
    t, d = hf.shape
    return _mm(
        da, hf, pl.BlockSpec((None, t, FF_SHARD), lambda s, j, kk: (s, 0, 0)),
        pl.BlockSpec((t, d), lambda s, j, kk: (0, 0)),
        pl.BlockSpec((None, FF_SHARD, d), lambda s, j, kk: (s, 0, 0)),
        jax.ShapeDtypeStruct((N_SHARDS, FF_SHARD, d), F32), (N_SHARDS, 1, 1), TN, f"ffn{layer}_wgrad_in",
        carry=carry)


Q_PER_KV = N_Q_HEADS // N_KV_HEADS
GROUP_ROWS = Q_PER_KV * CHUNK


def _attn_masks(n):
    lane = lax.broadcasted_iota(jnp.int32, (CHUNK, LANES), 1)
    lo = lane < HEAD_DIM
    tq = lax.broadcasted_iota(jnp.int32, (GROUP_ROWS, 2 * CHUNK), 0) & (CHUNK - 1)
    jk = lax.broadcasted_iota(jnp.int32, (GROUP_ROWS, 2 * CHUNK), 1)
    dist = tq + CHUNK - jk
    mask = (dist >= 0) & (dist < CHUNK) & (jk >= jnp.where(n == 0, CHUNK, 0))
    return lo, mask, dist.astype(F32)


def _per_head_column(values):
    r = lax.broadcasted_iota(jnp.int32, (GROUP_ROWS, 1), 0)
    col = jnp.full((GROUP_ROWS, 1), values[Q_PER_KV - 1], F32)
    for j in range(Q_PER_KV - 2, -1, -1):
        col = jnp.where(r < (j + 1) * CHUNK, values[j], col)
    return col


def _half_sum(x, lo):
    s_lo = jnp.sum(jnp.where(lo, x, 0.0), axis=-1, keepdims=True)
    s_hi = jnp.sum(jnp.where(lo, 0.0, x), axis=-1, keepdims=True)
    return jnp.where(lo, s_lo, s_hi)


def _stack_heads(pairs, lo):
    zero = jnp.zeros_like(pairs[0])
    return jnp.concatenate([jnp.where(lo, pairs[0], zero), jnp.where(lo, zero, pairs[0]),
                            jnp.where(lo, pairs[1], zero), jnp.where(lo, zero, pairs[1])], axis=0)


def _unstack_heads(stacked, lo):
    return (jnp.where(lo, stacked[0:CHUNK], stacked[CHUNK:2 * CHUNK]),
            jnp.where(lo, stacked[2 * CHUNK:3 * CHUNK], stacked[3 * CHUNK:]))


def _attn_probs(qs, kn, mask, distf, slope_col, sink_col):
    s = _dot(qs, kn, NT) * (HEAD_DIM ** -0.5)
    s = jnp.where(mask, s - slope_col * distf, NEG_BIG)
    m = jnp.maximum(jnp.max(s, axis=-1, keepdims=True), sink_col)
    e = jnp.exp(s - m)
    den = jnp.sum(e, axis=-1, keepdims=True) + jnp.exp(sink_col - m)
    return e * (1.0 / den), m, den


def _attn_fwd(qraw, kvd, gq, gk, sinks, carry=None):
    t, d = qraw.shape
    nb = t // CHUNK

    def body(sink_ref, q_ref, cur_ref, prev_ref, gq_ref, gk_ref, o_ref):
        n = pl.program_id(0)
        lo, mask, distf = _attn_masks(n)
        gq_v, gk_v = gq_ref[...], gk_ref[...]
        for kvh in range(N_KV_HEADS):
            ks = slice(kvh * LANES, (kvh + 1) * LANES)
            vs = slice(4 * LANES + kvh * LANES, 4 * LANES + (kvh + 1) * LANES)
            kraw = jnp.concatenate([prev_ref[:, ks], cur_ref[:, ks]], axis=0)
            rk = lax.rsqrt(jnp.mean(kraw * kraw, axis=-1, keepdims=True) + EPS)
            kn = (kraw * rk * gk_v).astype(BF16)
            vv = jnp.concatenate([prev_ref[:, vs], cur_ref[:, vs]], axis=0).astype(BF16)
            qn = []
            for p in range(2):
                qp = q_ref[:, (2 * kvh + p) * LANES:(2 * kvh + p + 1) * LANES]
                r = lax.rsqrt(_half_sum(qp * qp, lo) * (1.0 / HEAD_DIM) + EPS)
                qn.append(qp * r * gq_v)
            heads = range(Q_PER_KV * kvh, Q_PER_KV * (kvh + 1))
            pf, _, _ = _attn_probs(_stack_heads(qn, lo).astype(BF16), kn, mask, distf,
                                   _per_head_column([SLOPES[h] for h in heads]),
                                   _per_head_column([sink_ref[h] for h in heads]))
            for p, o_pair in enumerate(_unstack_heads(_dot(pf.astype(BF16), vv), lo)):
                o_ref[:, (2 * kvh + p) * LANES:(2 * kvh + p + 1) * LANES] = o_pair.astype(BF16)

    blk = lambda f: pl.BlockSpec((CHUNK, d), f)
    vec = pl.BlockSpec((1, LANES), lambda n: (0, 0))
    return _call(
        body, [sinks, qraw, kvd, kvd, gq, gk], grid=(nb,),
        in_specs=[pl.BlockSpec(memory_space=pltpu.SMEM), blk(lambda n: (n, 0)), blk(lambda n: (n, 0)),
                  blk(lambda n: (jnp.maximum(n - 1, 0), 0)), vec, vec],
        out_specs=[blk(lambda n: (n, 0))], out_shape=[jax.ShapeDtypeStruct((t, d), BF16)],
        name="attn_fwd", carry=carry)[0]


def _attn_bwd(qraw, kvd, d_o, gq, gk, sinks, carry=None):
    t, d = qraw.shape
    nb = t // CHUNK

    def body(sink_ref, q_ref, cur_ref, prev_ref, do_ref, gq_ref, gk_ref,
             dq_ref, dkv_ref, dsink_ref, dgq_ref, dgk_ref, carry_s, pp_s, cp_s):
        n = pl.program_id(0)

        @pl.when(n == 0)
        def _():
            carry_s[...] = jnp.zeros_like(carry_s)
            dsink_ref[...] = jnp.zeros_like(dsink_ref)
            dgq_ref[...] = jnp.zeros_like(dgq_ref)
            dgk_ref[...] = jnp.zeros_like(dgk_ref)

        @pl.when(n < nb)
        def _():
            lo, mask, distf = _attn_masks(n)
            gq_v, gk_v = gq_ref[...], gk_ref[...]
            for kvh in range(N_KV_HEADS):
                ks = slice(kvh * LANES, (kvh + 1) * LANES)
                vs = slice(4 * LANES + kvh * LANES, 4 * LANES + (kvh + 1) * LANES)
                kraw = jnp.concatenate([prev_ref[:, ks], cur_ref[:, ks]], axis=0)
                rk = lax.rsqrt(jnp.mean(kraw * kraw, axis=-1, keepdims=True) + EPS)
                khat = kraw * rk
                kn = (khat * gk_v).astype(BF16)
                vv = jnp.concatenate([prev_ref[:, vs], cur_ref[:, vs]], axis=0).astype(BF16)
                cols = [slice((2 * kvh + p) * LANES, (2 * kvh + p + 1) * LANES) for p in range(2)]
                rq, qhat = [], []
                for p in range(2):
                    qp = q_ref[:, cols[p]]
                    rq.append(lax.rsqrt(_half_sum(qp * qp, lo) * (1.0 / HEAD_DIM) + EPS))
                    qhat.append(qp * rq[p])
                heads = range(Q_PER_KV * kvh, Q_PER_KV * (kvh + 1))
                qs = _stack_heads([qhat[p] * gq_v for p in range(2)], lo).astype(BF16)
                dos = _stack_heads([do_ref[:, cols[p]] for p in range(2)], lo)
                sink_col = _per_head_column([sink_ref[h] for h in heads])
                pf, m, den = _attn_probs(qs, kn, mask, distf, _per_head_column([SLOPES[h] for h in heads]), sink_col)
                dp = _dot(dos, vv, NT)
                delta = jnp.sum(pf * dp, axis=-1, keepdims=True)
                sink_delta = jnp.exp(sink_col - m) / den * delta
                for j, h in enumerate(heads):
                    dsink_ref[h:h + 1, :] -= jnp.broadcast_to(
                        jnp.sum(sink_delta[j * CHUNK:(j + 1) * CHUNK], axis=0, keepdims=True), (1, LANES))
                ds = (pf * (dp - delta) * (HEAD_DIM ** -0.5)).astype(BF16)
                dkn = _dot(ds, qs, TN)
                dvb = _dot(pf.astype(BF16), dos, TN)
                for p, dqn in enumerate(_unstack_heads(_dot(ds, kn), lo)):
                    dgq_ref[0:1, :] += jnp.sum(dqn * qhat[p], axis=0, keepdims=True)
                    gy = dqn * gq_v
                    mq = _half_sum(gy * qhat[p], lo) * (1.0 / HEAD_DIM)
                    dq_ref[:, cols[p]] = (rq[p] * (gy - qhat[p] * mq)).astype(BF16)
                dgk_ref[0:1, :] += jnp.sum(dkn * khat, axis=0, keepdims=True)
                gyk = dkn * gk_v
                dkraw = rk * (gyk - khat * jnp.mean(gyk * khat, axis=-1, keepdims=True))
                pp_s[:, ks] = dkraw[:CHUNK]
                cp_s[:, ks] = dkraw[CHUNK:]
                pp_s[:, vs] = dvb[:CHUNK]
                cp_s[:, vs] = dvb[CHUNK:]
            dkv_ref[...] = (carry_s[...] + pp_s[...]).astype(BF16)
            carry_s[...] = cp_s[...]

        @pl.when(n == nb)
        def _():
            dkv_ref[...] = carry_s[...].astype(BF16)

    blk = lambda f: pl.BlockSpec((CHUNK, d), f)
    vec = pl.BlockSpec((1, LANES), lambda n: (0, 0))
    cur = lambda n: (jnp.minimum(n, nb - 1), 0)
    prev = lambda n: (jnp.maximum(jnp.minimum(n, nb - 1) - 1, 0), 0)
    small = lambda r: pl.BlockSpec((r, LANES), lambda n: (0, 0))
    return _call(
        body, [sinks, qraw, kvd, kvd, d_o, gq, gk], grid=(nb + 1,),
        in_specs=[pl.BlockSpec(memory_space=pltpu.SMEM), blk(cur), blk(cur), blk(prev), blk(cur), vec, vec],
        out_specs=[blk(cur), blk(lambda n: (jnp.maximum(n - 1, 0), 0)), small(N_Q_HEADS), small(8), small(8)],
        out_shape=[jax.ShapeDtypeStruct((t, d), BF16), jax.ShapeDtypeStruct((t, d), BF16),
                   jax.ShapeDtypeStruct((N_Q_HEADS, LANES), F32), jax.ShapeDtypeStruct((8, LANES), F32),
                   jax.ShapeDtypeStruct((8, LANES), F32)],
        scratch=[pltpu.VMEM((CHUNK, d), F32)] * 3, name="attn_bwd", sem=("arbitrary",), carry=carry)


def _loss_head(y, target, tm=512):
    t, d = y.shape

    def body(y_ref, t_ref, dy_ref, loss_ref, acc_ref):
        i = pl.program_id(0)

        @pl.when(i == 0)
        def _():
            acc_ref[...] = jnp.zeros_like(acc_ref)

        err = y_ref[...] - t_ref[...]
        dy_ref[...] = err * (1.0 / d)
        acc_ref[...] += jnp.sum(err * err, axis=0, keepdims=True)

        @pl.when(i == t // tm - 1)
        def _():
            loss_ref[...] = jnp.broadcast_to(0.5 / d * jnp.sum(acc_ref[...], axis=1, keepdims=True), loss_ref.shape)

    row = pl.BlockSpec((tm, d), lambda i: (i, 0))
    return _call(
        body, [y, target], grid=(t // tm,), in_specs=[row, row],
        out_specs=[row, pl.BlockSpec((8, LANES), lambda i: (0, 0))],
        out_shape=[jax.ShapeDtypeStruct((t, d), F32), jax.ShapeDtypeStruct((8, LANES), F32)],
        scratch=[pltpu.VMEM((1, d), F32)], name="loss_head", sem=("arbitrary",))


def _adamw_math(g, w, m, v):
    m = ADAM_B1 * m + (1.0 - ADAM_B1) * g
    v = ADAM_B2 * v + (1.0 - ADAM_B2) * (g * g)
    m_hat = m / (1.0 - ADAM_B1 ** ADAM_STEP)
    v_hat = v / (1.0 - ADAM_B2 ** ADAM_STEP)
    delta = -ADAM_LR * (m_hat / (jnp.sqrt(v_hat) + ADAM_EPS) + ADAM_WD * w)
    return delta, m, v


def _row_tile(r, cap=128):
    for tr in range(min(r, cap), 0, -1):
        if r % tr == 0 and (tr % 8 == 0 or tr == r):
            return tr
    return r


def _chip_sum(grad, recv, place, name, wire_dtype):
    _, r, c = grad.shape
    tr = _row_tile(r, 256)

    def body(pl_ref, g_ref, a_ref, p_ref):
        p_ref[...] = (g_ref[...] + a_ref[...]).astype(p_ref.dtype)

    return pl.pallas_call(
        body,
        grid_spec=pltpu.PrefetchScalarGridSpec(
            num_scalar_prefetch=1, grid=(4, r // tr),
            in_specs=[pl.BlockSpec((None, None, tr, c), lambda q, i, pr: (q, pr[1], i, 0)),
                      pl.BlockSpec((None, tr, c), lambda q, i, pr: (q, i, 0))],
            out_specs=pl.BlockSpec((None, tr, c), lambda q, i, pr: (q, i, 0))),
        out_shape=jax.ShapeDtypeStruct((4, r, c), wire_dtype), name=name, compiler_params=_params(),
    )(place, grad.reshape(4, 2, r, c), recv)


def _adamw_sharded(grad, recv, others, place, w, m, v, name, layer=None, fill=None):
    r, c = w.shape[-2:]
    tr = _row_tile(r)

    def body(pl_ref, g_ref, a_ref, oth_ref, w_ref, m_ref, v_ref, *rest):
        g_out, d_out, nm_out, nv_out = rest[-4:]
        g = g_ref[...] + a_ref[...]
        for k in range(3):
            g = g + oth_ref[k].astype(F32)
        delta, nm, nv = _adamw_math(g, w_ref[...], m_ref[...], v_ref[...])
        g_out[...] = g
        d_out[...] = delta
        nm_out[...] = nm
        nv_out[...] = nv

    if layer is None:
        row = pl.BlockSpec((tr, c), lambda i, pr: (i, 0))
    else:
        row = pl.BlockSpec((None, tr, c), lambda i, pr: (layer, i, 0))
    n_fill = 0 if fill is None else 4
    in_specs = [pl.BlockSpec((None, None, tr, c), lambda i, pr: (pr[0], pr[1], i, 0)),
                pl.BlockSpec((None, tr, c), lambda i, pr: (pr[0], i, 0)),
                pl.BlockSpec((3, tr, c), lambda i, pr: (0, i, 0)), row, row, row]
    in_specs += [pl.BlockSpec(memory_space=pl.ANY)] * n_fill
    return pl.pallas_call(
        body,
        grid_spec=pltpu.PrefetchScalarGridSpec(
            num_scalar_prefetch=1, grid=(r // tr,), in_specs=in_specs, out_specs=[row] * 4),
        out_shape=[jax.ShapeDtypeStruct(w.shape, F32)] * 4, name=name, compiler_params=_params(),
        input_output_aliases={7 + j: j for j in range(n_fill)},
    )(place, grad.reshape(4, 2, r, c), recv, others, w, m, v, *([] if fill is None else fill))


def _adamw_summed(parts, ws, ms, vs, name):
    n = len(parts)

    def body(*refs):
        p_refs, w_refs, m_refs, v_refs = refs[:n], refs[n:2 * n], refs[2 * n:3 * n], refs[3 * n:4 * n]
        o_refs = refs[4 * n:]
        for i in range(n):
            g = p_refs[i][0]
            for k in range(1, N_SHARDS):
                g = g + p_refs[i][k]
            delta, nm, nv = _adamw_math(g, w_refs[i][...], m_refs[i][...], v_refs[i][...])
            o_refs[4 * i][...] = g
            o_refs[4 * i + 1][...] = delta
            o_refs[4 * i + 2][...] = nm
            o_refs[4 * i + 3][...] = nv

    shapes = [jax.ShapeDtypeStruct(w.shape, F32) for w in ws for _ in range(4)]
    outs = pl.pallas_call(body, out_shape=shapes, name=name, compiler_params=_params())(*parts, *ws, *ms, *vs)
    return [outs[4 * i:4 * i + 4] for i in range(n)]


def _dup_heads(w):
    lead = w.shape[:-1]
    w4 = w.reshape(lead + (N_KV_HEADS, 1, HEAD_DIM))
    return jnp.broadcast_to(w4, lead + (N_KV_HEADS, 2, HEAD_DIM)).reshape(lead + (N_KV_HEADS * LANES,))


def _fold_heads(g):
    lead = g.shape[:-1]
    return g.reshape(lead + (N_KV_HEADS, 2, HEAD_DIM)).sum(axis=-2).reshape(lead + (N_KV_HEADS * HEAD_DIM,))


def kernel(x, a_norm, a_w_in, a_v_norm, a_w_s, a_b_s, a_w_out, f_norm, f_w_in, f_conv_w, f_conv_b, f_w_out, kv_norm, w_kv, k_norm, b_norm, b_w_q, b_q_norm, b_sinks, b_w_o, loss_target, m_a_norm, m_a_w_in, m_a_v_norm, m_a_w_s, m_a_b_s, m_a_w_out, m_f_norm, m_f_w_in, m_f_conv_w, m_f_conv_b, m_f_w_out, m_kv_norm, m_w_kv, m_k_norm, m_b_norm, m_b_w_q, m_b_q_norm, m_b_sinks, m_b_w_o, v_a_norm, v_a_w_in, v_a_v_norm, v_a_w_s, v_a_b_s, v_a_w_out, v_f_norm, v_f_w_in, v_f_conv_w, v_f_conv_b, v_f_w_out, v_kv_norm, v_w_kv, v_k_norm, v_b_norm, v_b_w_q, v_b_q_norm, v_b_sinks, v_b_w_o):
    d = D_MODEL
    xi, yi, ci = _coords()
    place = jnp.stack([2 * xi + yi, ci]).astype(jnp.int32)
    bf = lambda a: a.astype(BF16)
    row = lambda v_: v_.reshape(1, -1)
    x0, target = x[0], loss_target[0]
    t = x0.shape[0]
    res = {}

    red = {}

    def to_sibling(grads, wire=BF16):
        for k, g in grads.items():
            red[k] = dict(grad=g, wire=wire)
        ex = _ToSibling(list(grads.values()))
        ex.names = list(grads)
        return ex

    def to_chips(ex):
        for k, a in zip(ex.names, ex.results):
            red[k]["recv"] = a
            red[k]["psum"] = _chip_sum(red[k]["grad"], a, place, f"chip_sum_{k}", red[k]["wire"])
        nxt = _ToChips([red[k]["psum"] for k in ex.names])
        nxt.names = ex.names
        return nxt

    def landed(ex):
        for k, b in zip(ex.names, ex.results):
            red[k]["others"] = b

    def halves(ex):
        parts = []
        for h in range(2):
            nr = ex.srcs[0].shape[1] // 2
            part = _ToChips(ex.srcs, rows=(h * nr, nr))
            part.names = ex.names
            parts.append(part)
        return parts

    def landed_halves(parts):
        for j, k in enumerate(parts[0].names):
            red[k]["others"] = jnp.concatenate([p.results[j] for p in parts], axis=1)

    def update(k, w, m, v, layer=None, fill=None):
        r = red[k]
        return _adamw_sharded(r["grad"], r["recv"], r["others"], place, w, m, v,
                              f"adamw_{k}", layer=layer, fill=fill)

    g_a_in, g_a_out, g_a_norm, g_a_v_norm, g_conv = _exchange_alone(
        _Gather([bf(a_w_in[0]), bf(a_w_out[0]), a_norm, a_v_norm, f_conv_w.reshape(6, FF_SHARD)]), "gather_first")
    a_norm_full, a_v_norm_full = g_a_norm.reshape(1, d), g_a_v_norm.reshape(1, d)
    conv_w = lax.reduce_precision(g_conv.reshape(N_SHARDS, 2, 3, FF_SHARD), 8, 7)
    cw = jnp.pad(jnp.transpose(conv_w, (1, 0, 2, 3)), ((0, 0), (0, 0), (0, 5), (0, 0)))
    w_a_in_flat = jnp.transpose(g_a_in, (1, 0, 2)).reshape(d, 2 * d)
    cb = f_conv_b.reshape(2, N_SHARDS, 1, FF_SHARD)
    tri = jnp.tril(jnp.ones((CHUNK, CHUNK), dtype=bool))
    w_causal = jnp.where(tri[None], a_w_s[0], 0.0).astype(BF16)
    w_causal_t = jnp.transpose(w_causal, (0, 2, 1))
    b_sb = jnp.broadcast_to(a_b_s[0][:, :, None], (N_GROUPS, CHUNK, CHUNK))
    w_a_out = g_a_out.reshape(d, d)
    gq = jnp.tile(b_q_norm.reshape(1, HEAD_DIM), (1, 2))
    gk = jnp.tile(k_norm.reshape(1, HEAD_DIM), (1, 2))
    sinks = b_sinks.reshape(N_Q_HEADS)

    (h1,) = _rms_fwd(x0, [a_norm_full], "a_norm_fwd")
    ex = _Gather([bf(f_w_in[0]), bf(f_w_out[0])])
    zpre, x1 = _sgu_fwd(x0, h1, g_a_in, a_v_norm_full, w_causal, b_sb, w_a_out, carry=ex)
    w_in0, w_out0 = ex.results[0], ex.results[1].reshape(D_FF, d)
    ex = _Gather([bf(w_kv), bf(b_w_q[0]), bf(b_w_o[0]), bf(f_w_out[1])])
    x2, hf0, a0, pre0 = _ffn_fwd(x1, f_norm[0:1], w_in0, cw[0], cb[0], w_out0, 0, carry=ex)
    kv_full = ex.results[0].reshape(d, 2 * N_KV_HEADS * HEAD_DIM)
    w_q, w_o = ex.results[1].reshape(d, d), ex.results[2].reshape(d, d)
    w_out1 = ex.results[3].reshape(D_FF, d)
    half = N_KV_HEADS * HEAD_DIM
    w_kv_dup = jnp.concatenate([_dup_heads(kv_full[:, :half]), _dup_heads(kv_full[:, half:])], axis=1)
    hk, hq = _rms_fwd(x2, [row(kv_norm), b_norm], "kvq_norm_fwd")
    kvd = _mm_rows(hk, w_kv_dup, F32, "kv_proj")
    qraw = _mm_rows(hq, w_q, F32, "q_proj")
    ex = _Gather([bf(f_w_in[1])])
    o = _attn_fwd(qraw, kvd, gq, gk, sinks, carry=ex)
    w_in1 = ex.results[0]
    x3 = _mm_rows(o, w_o, F32, "o_proj", res=x2)
    x4, hf1, a1, pre1 = _ffn_fwd(x3, f_norm[1:2], w_in1, cw[1], cb[1], w_out1, 1)
    dy, loss_lanes = _loss_head(x4, target)
    loss = lax.psum(loss_lanes[0, 0], ("x", "y", "c"))

    dhu1, dw_out1, dcb1 = _ffn_bwd_act(pre1, w_out1, dy, 1)
    ex = to_sibling({"f_w_out1": dw_out1.reshape(N_SHARDS, D_FF // N_SHARDS, d)})
    da1, dhf1, dcw1 = _ffn_bwd_in(dhu1, a1, cw[1], w_in1, 1, carry=ex)
    ex = to_chips(ex)
    dw_in1 = _ffn_wgrad_in(hf1, da1, 1, carry=ex)
    landed(ex)
    ex = to_sibling({"f_w_in1": dw_in1})
    dx3, dgf1 = _rms_bwd(x3, [f_norm[1:2]], [dhf1], dy, "f1_norm_bwd", carry=ex)
    ex = to_chips(ex)
    d_o = _mm_rows(dx3, w_o, BF16, "o_proj_bwd", trans_w=True)
    dw_o = _mm_wgrad(o, dx3, "o_wgrad").reshape(N_SHARDS, d // N_SHARDS, d)
    dq, dkv, dsink, dgq, dgk = _attn_bwd(qraw, kvd, d_o, gq, gk, sinks, carry=ex)
    landed(ex)
    dw_q = _mm_wgrad(hq, dq, "q_wgrad").reshape(N_SHARDS, d // N_SHARDS, d)
    dw_kv_dup = _mm_wgrad(hk, dkv, "kv_wgrad")
    dw_kv = jnp.concatenate(
        [_fold_heads(dw_kv_dup[:, :4 * LANES]), _fold_heads(dw_kv_dup[:, 4 * LANES:])], axis=1
    ).reshape(N_SHARDS, d // N_SHARDS, 2 * N_KV_HEADS * HEAD_DIM)
    ex = to_sibling({"b_w_o": dw_o, "b_w_q": dw_q, "w_kv": dw_kv})
    dhq = _mm_rows(dq, w_q, F32, "q_proj_bwd", trans_w=True, carry=ex)
    dhk = _mm_rows(dkv, w_kv_dup, F32, "kv_proj_bwd", trans_w=True)
    ex = to_chips(ex)
    dx2, dg2 = _rms_bwd(x2, [row(kv_norm), b_norm], [dhk, dhq], dx3, "kvq_norm_bwd")
    dhu0, dw_out0, dcb0 = _ffn_bwd_act(pre0, w_out0, dx2, 0, carry=ex)
    landed(ex)
    ex = to_sibling({"f_w_out0": dw_out0.reshape(N_SHARDS, D_FF // N_SHARDS, d)})
    da0, dhf0, dcw0 = _ffn_bwd_in(dhu0, a0, cw[0], w_in0, 0, carry=ex)
    ex = to_chips(ex)
    dw_in0 = _ffn_wgrad_in(hf0, da0, 0, carry=ex)
    landed(ex)
    ex = to_sibling({"f_w_in0": dw_in0})
    dx1, dgf0 = _rms_bwd(x1, [f_norm[0:1]], [dhf0], dx2, "f0_norm_bwd", carry=ex)
    ex_lo, ex_hi = halves(to_chips(ex))
    dz, y, dwc, dbs, dgv = _sgu_bwd(dx1, zpre, w_a_out, a_v_norm_full, w_causal, w_causal_t, b_sb, carry=ex_lo)
    dw_a_out = _mm_wgrad(y, dx1, "a_out_wgrad").reshape(N_SHARDS, d // N_SHARDS, d)
    nsub = g_a_in.shape[2]
    dw_a_in = _mm(
        h1, dz, pl.BlockSpec((t, d), lambda s, j, kk: (0, 0)), pl.BlockSpec((t, nsub), lambda s, j, kk: (0, s)),
        pl.BlockSpec((None, d, nsub), lambda s, j, kk: (s, 0, 0)), jax.ShapeDtypeStruct((N_SHARDS, d, nsub), F32),
        (N_SHARDS, 1, 1), TN, "a_in_wgrad", carry=ex_hi)
    landed_halves([ex_lo, ex_hi])

    def bias_grad(dcb):
        return jnp.transpose(dcb[:, :, 0, :], (1, 0, 2)).reshape(-1)

    g_conv_w = jnp.concatenate([dcw0[:, 0:3, :], dcw1[:, 0:3, :]], axis=1)
    g_a_v_norm = dgv[0].reshape(N_SHARDS, 1, LANES)
    rep = ["a_w_s", "a_b_s", "f_norm", "f_conv_b", "kv_norm", "k_norm", "b_norm", "b_q_norm", "b_sinks"]
    rep_g = dict(
        a_w_s=dwc.reshape(N_GROUPS * CHUNK, CHUNK), a_b_s=dbs[:, :, 0], f_norm=jnp.stack([dgf0[0], dgf1[0]]),
        f_conv_b=jnp.stack([bias_grad(dcb0), bias_grad(dcb1)]), kv_norm=dg2[0:1],
        k_norm=(dgk[0, :HEAD_DIM] + dgk[0, HEAD_DIM:])[None], b_norm=dg2[1:2],
        b_q_norm=(dgq[0, :HEAD_DIM] + dgq[0, HEAD_DIM:])[None], b_sinks=dsink[:, 0][None])
    ex_big = to_sibling({"a_w_out": dw_a_out, "a_w_in": dw_a_in})
    ex_small = to_sibling({"a_v_norm": g_a_v_norm, "f_conv_w": g_conv_w}, wire=F32)
    ex_rep = _Gather([rep_g[k] for k in rep])
    together = _Together([ex_big, ex_small, ex_rep])
    dh1 = _mm_rows(dz, w_a_in_flat, F32, "a_in_bwd", trans_w=True, carry=together)
    together.spread()
    ex_big, ex_small = to_chips(ex_big), to_chips(ex_small)
    together = _Together([ex_big, ex_small])
    grad_x, dg0 = _rms_bwd(x0, [a_norm_full], [dh1], dx1, "a_norm_bwd", carry=together)
    together.spread()
    landed(ex_big)
    landed(ex_small)
    (a_norm_parts,) = _exchange_alone(_ToOwners([dg0[0].reshape(N_SHARDS, 1, LANES)]), "a_norm_to_owners")

    res["f_w_out"] = update("f_w_out1", f_w_out, m_f_w_out, v_f_w_out, layer=1)
    w_in_t = [jnp.swapaxes(a_, 1, 2) for a_ in (f_w_in, m_f_w_in, v_f_w_in)]
    res["f_w_in"] = update("f_w_in1", *w_in_t, layer=1)
    res["b_w_o"] = update("b_w_o", b_w_o, m_b_w_o, v_b_w_o, layer=0)
    res["b_w_q"] = update("b_w_q", b_w_q, m_b_w_q, v_b_w_q, layer=0)
    res["w_kv"] = update("w_kv", w_kv, m_w_kv, v_w_kv)
    res["f_w_out"] = update("f_w_out0", f_w_out, m_f_w_out, v_f_w_out, layer=0, fill=res["f_w_out"])
    res["f_w_in"] = [jnp.swapaxes(o_, 1, 2) for o_ in update("f_w_in0", *w_in_t, layer=0, fill=res["f_w_in"])]
    res["a_w_out"] = update("a_w_out", a_w_out, m_a_w_out, v_a_w_out, layer=0)
    res["a_w_in"] = update("a_w_in", a_w_in, m_a_w_in, v_a_w_in, layer=0)
    res["a_v_norm"] = update("a_v_norm", a_v_norm, m_a_v_norm, v_a_v_norm)
    res["f_conv_w"] = [o_.reshape(f_conv_w.shape) for o_ in update(
        "f_conv_w", f_conv_w.reshape(6, FF_SHARD), m_f_conv_w.reshape(6, FF_SHARD), v_f_conv_w.reshape(6, FF_SHARD))]

    rep_w = dict(a_w_s=a_w_s, a_b_s=a_b_s, f_norm=f_norm, f_conv_b=f_conv_b, kv_norm=kv_norm, k_norm=k_norm,
                 b_norm=b_norm, b_q_norm=b_q_norm, b_sinks=b_sinks, a_norm=a_norm)
    rep_m = dict(a_w_s=m_a_w_s, a_b_s=m_a_b_s, f_norm=m_f_norm, f_conv_b=m_f_conv_b, kv_norm=m_kv_norm,
                 k_norm=m_k_norm, b_norm=m_b_norm, b_q_norm=m_b_q_norm, b_sinks=m_b_sinks, a_norm=m_a_norm)
    rep_v = dict(a_w_s=v_a_w_s, a_b_s=v_a_b_s, f_norm=v_f_norm, f_conv_b=v_f_conv_b, kv_norm=v_kv_norm,
                 k_norm=v_k_norm, b_norm=v_b_norm, b_q_norm=v_b_q_norm, b_sinks=v_b_sinks, a_norm=v_a_norm)
    keys = rep + ["a_norm"]
    parts = ex_rep.results + [a_norm_parts]
    as2d = lambda a, p: a.reshape(p.shape[1:])
    rep_outs = _adamw_summed(parts, [as2d(rep_w[k], p) for k, p in zip(keys, parts)],
                             [as2d(rep_m[k], p) for k, p in zip(keys, parts)],
                             [as2d(rep_v[k], p) for k, p in zip(keys, parts)], "adamw_replicated")
    for j, key in enumerate(keys):
        res[key] = [o_.reshape(rep_w[key].shape) for o_ in rep_outs[j]]

    order = ["a_norm", "a_w_in", "a_v_norm", "a_w_s", "a_b_s", "a_w_out", "f_norm", "f_w_in", "f_conv_w", "f_conv_b",
             "f_w_out", "kv_norm", "w_kv", "k_norm", "b_norm", "b_w_q", "b_q_norm", "b_sinks", "b_w_o"]
    outs = [loss, grad_x[None]]
    for j in range(4):
        outs += [res[k][j] for k in order]
    return tuple(outs)
```

```python
import jax
import jax.numpy as jnp
from jax import lax
from jax.experimental import pallas as pl
from jax.experimental.pallas import tpu as pltpu

F32 = jnp.float32
BF16 = jnp.bfloat16
EPS = 1e-6
D_MODEL = 1024
CHUNK = 128
N_GROUPS = 8
N_SHARDS = 8
HEAD_DIM = 64
N_Q_HEADS = 16
N_KV_HEADS = 4
D_FF = 2816
FF_SHARD = 2 * D_FF // N_SHARDS
LANES = 128
NEG_BIG = -1e30
ADAM_LR = 0.001
ADAM_B1 = 0.9
ADAM_B2 = 0.999
ADAM_EPS = 1e-08
ADAM_WD = 0.01
ADAM_STEP = 10
VMEM_LIMIT_BYTES = 56 * 1024 * 1024
MESH = pl.DeviceIdType.MESH

NN = (((1,), (0,)), ((), ()))
NT = (((1,), (1,)), ((), ()))
TN = (((0,), (0,)), ((), ()))
SLOPES = tuple(2.0 ** (-8.0 * (h + 1) / N_Q_HEADS) for h in range(N_Q_HEADS))


def _params(sem=None):
    return pltpu.CompilerParams(dimension_semantics=sem, vmem_limit_bytes=VMEM_LIMIT_BYTES)


def _dot(a, b, dims=NN):
    return lax.dot_general(a, b, dims, preferred_element_type=F32)


def _sigmoid(x):
    return 1.0 / (1.0 + jnp.exp(-x))


def _gelu_parts(z):
    cdf = 0.5 * (1.0 + lax.erf(z * (2.0 ** -0.5)))
    pdf = jnp.exp(-0.5 * z * z) * 0.3989422804014327
    return cdf, pdf


def _coords():
    return lax.axis_index("x"), lax.axis_index("y"), lax.axis_index("c")


class _Gather:
    def __init__(self, srcs):
        self.srcs = list(srcs)
        n = len(self.srcs)
        self.out_shapes = [jax.ShapeDtypeStruct((N_SHARDS,) + s.shape, s.dtype) for s in self.srcs]
        self.sems = [pltpu.SemaphoreType.DMA((n, 7)), pltpu.SemaphoreType.DMA((n, 7)), pltpu.SemaphoreType.DMA((n,))]

    def _plan(self, src, dst, sems):
        send_sems, recv_sems, local_sems = sems
        x, y, c = _coords()
        me, sibling = (x, y, c), (x, y, 1 - c)
        chips = [(1 - x, y), (x, 1 - y), (1 - x, 1 - y)]
        n = len(src)

        def rows(e, dev):
            return dst[e].at[4 * dev[0] + 2 * dev[1] + dev[2]]

        def copy(e, slot, block, to, from_own=False):
            return pltpu.make_async_remote_copy(
                src_ref=src[e] if from_own else rows(e, block), dst_ref=rows(e, block),
                send_sem=send_sems.at[e, slot], recv_sem=recv_sems.at[e, slot], device_id=to, device_id_type=MESH)

        mine = [pltpu.make_async_copy(src[e], rows(e, me), local_sems.at[e]) for e in range(n)]
        first = []
        for e in range(n):
            first.append(copy(e, 0, me, sibling, from_own=True))
            first += [copy(e, 1 + j, me, (*chip, c), from_own=True) for j, chip in enumerate(chips)]
        return n, me, sibling, chips, c, copy, mine, first

    def start(self, src, dst, sems):
        _, _, _, _, _, _, mine, first = self._plan(src, dst, sems)
        for cp in mine + first:
            cp.start()

    def finish(self, src, dst, sems):
        n, me, sibling, chips, c, copy, mine, first = self._plan(src, dst, sems)
        passed = []
        for j, chip in enumerate(chips):
            for e in range(n):
                copy(e, 1 + j, (*chip, c), me).wait_recv()
                cp = copy(e, 4 + j, (*chip, c), sibling)
                cp.start()
                passed.append(cp)
        for e in range(n):
            copy(e, 0, sibling, me).wait_recv()
            for j, chip in enumerate(chips):
                copy(e, 4 + j, (*chip, 1 - c), me).wait_recv()
        for cp in first + passed:
            cp.wait_send()
        for cp in mine:
            cp.wait()


class _ToSibling:
    def __init__(self, grads):
        self.srcs = list(grads)
        n = len(self.srcs)
        self.out_shapes = [jax.ShapeDtypeStruct((4,) + g.shape[1:], g.dtype) for g in self.srcs]
        self.sems = [pltpu.SemaphoreType.DMA((n, 4)), pltpu.SemaphoreType.DMA((n, 4))]

    def _copies(self, src, dst, sems):
        send_sems, recv_sems = sems
        x, y, c = _coords()
        return [
            pltpu.make_async_remote_copy(
                src_ref=src[i].at[2 * q + (1 - c)], dst_ref=dst[i].at[q], send_sem=send_sems.at[i, q],
                recv_sem=recv_sems.at[i, q], device_id=(x, y, 1 - c), device_id_type=MESH)
            for i in range(len(src)) for q in range(4)]

    def start(self, src, dst, sems):
        for cp in self._copies(src, dst, sems):
            cp.start()

    def finish(self, src, dst, sems):
        for cp in self._copies(src, dst, sems):
            cp.wait()


class _ToChips:
    def __init__(self, psums, rows=None):
        self.srcs = list(psums)
        n = len(self.srcs)
        self.rows = rows
        self.out_shapes = [
            jax.ShapeDtypeStruct((3, p.shape[1] if rows is None else rows[1]) + p.shape[2:], p.dtype)
            for p in self.srcs]
        self.sems = [pltpu.SemaphoreType.DMA((n, 3)), pltpu.SemaphoreType.DMA((n, 3))]

    def _copies(self, src, dst, sems):
        send_sems, recv_sems = sems
        x, y, c = _coords()
        peers = [(x, 1 - y), (1 - x, y), (1 - x, 1 - y)]

        def part(i, q):
            if self.rows is None:
                return src[i].at[q]
            return src[i].at[q, pl.ds(self.rows[0], self.rows[1])]

        return [
            pltpu.make_async_remote_copy(
                src_ref=part(i, 2 * px + py), dst_ref=dst[i].at[r], send_sem=send_sems.at[i, r],
                recv_sem=recv_sems.at[i, r], device_id=(px, py, c), device_id_type=MESH)
            for i in range(len(src)) for r, (px, py) in enumerate(peers)]

    def start(self, src, dst, sems):
        for cp in self._copies(src, dst, sems):
            cp.start()

    def finish(self, src, dst, sems):
        for cp in self._copies(src, dst, sems):
            cp.wait()


class _ToOwners:
    def __init__(self, grads):
        self.srcs = list(grads)
        n = len(self.srcs)
        self.out_shapes = [jax.ShapeDtypeStruct(g.shape, g.dtype) for g in self.srcs]
        self.sems = [pltpu.SemaphoreType.DMA((n, 7)), pltpu.SemaphoreType.DMA((n, 7)), pltpu.SemaphoreType.DMA((n,))]

    def _copies(self, src, dst, sems):
        send_sems, recv_sems, local_sems = sems
        x, y, c = _coords()
        me = 4 * x + 2 * y + c
        copies = [pltpu.make_async_copy(src[i].at[me], dst[i].at[me], local_sems.at[i]) for i in range(len(src))]
        for i in range(len(src)):
            for rel in range(1, N_SHARDS):
                px = x ^ (rel >> 2) if rel >> 2 else x
                py = y ^ ((rel >> 1) & 1) if (rel >> 1) & 1 else y
                pc = c ^ (rel & 1) if rel & 1 else c
                copies.append(pltpu.make_async_remote_copy(
                    src_ref=src[i].at[4 * px + 2 * py + pc], dst_ref=dst[i].at[me], send_sem=send_sems.at[i, rel - 1],
                    recv_sem=recv_sems.at[i, rel - 1], device_id=(px, py, pc), device_id_type=MESH))
        return copies

    def start(self, src, dst, sems):
        for cp in self._copies(src, dst, sems):
            cp.start()

    def finish(self, src, dst, sems):
        for cp in self._copies(src, dst, sems):
            cp.wait()


class _Together:
    def __init__(self, parts):
        self.parts = list(parts)
        self.srcs = [s for p in self.parts for s in p.srcs]
        self.out_shapes = [s for p in self.parts for s in p.out_shapes]
        self.sems = [s for p in self.parts for s in p.sems]

    def _split(self, src, dst, sems):
        a = b = c = 0
        for p in self.parts:
            na, nc = len(p.srcs), len(p.sems)
            yield p, src[a:a + na], dst[b:b + na], sems[c:c + nc]
            a, b, c = a + na, b + na, c + nc

    def start(self, src, dst, sems):
        for p, s, d, m in self._split(src, dst, sems):
            p.start(s, d, m)

    def finish(self, src, dst, sems):
        for p, s, d, m in self._split(src, dst, sems):
            p.finish(s, d, m)

    def spread(self):
        b = 0
        for p in self.parts:
            p.results = self.results[b:b + len(p.srcs)]
            b += len(p.srcs)


def _call(body, args, *, grid, in_specs, out_specs, out_shape, name, scratch=(), sem=None, carry=None):
    out_shape, out_specs = list(out_shape), list(out_specs)
    if carry is None:
        return pl.pallas_call(
            body, grid=grid, in_specs=list(in_specs), out_specs=out_specs, out_shape=out_shape,
            scratch_shapes=list(scratch), name=name, compiler_params=_params(sem))(*args)
    n_in, n_out, n_scr, n_c = len(args), len(out_shape), len(scratch), len(carry.srcs)
    steps = tuple(grid)

    def carried(*refs):
        ins, rest = refs[:n_in], refs[n_in:]
        c_src, rest = rest[:n_c], rest[n_c:]
        outs, rest = rest[:n_out], rest[n_out:]
        c_dst, rest = rest[:n_c], rest[n_c:]
        scr, sems = rest[:n_scr], rest[n_scr:]
        first = pl.program_id(0) == 0
        last = pl.program_id(0) == steps[0] - 1
        for ax in range(1, len(steps)):
            first = first & (pl.program_id(ax) == 0)
            last = last & (pl.program_id(ax) == steps[ax] - 1)

        @pl.when(first)
        def _():
            carry.start(c_src, c_dst, sems)

        body(*ins, *outs, *scr)

        @pl.when(last)
        def _():
            carry.finish(c_src, c_dst, sems)

    hbm = pl.BlockSpec(memory_space=pl.ANY)
    res = pl.pallas_call(
        carried, grid=grid, in_specs=list(in_specs) + [hbm] * n_c, out_specs=out_specs + [hbm] * n_c,
        out_shape=out_shape + carry.out_shapes, scratch_shapes=list(scratch) + carry.sems, name=name,
        compiler_params=_params(("arbitrary",) * len(steps)))(*args, *carry.srcs)
    carry.results = list(res[n_out:])
    return list(res[:n_out])


def _exchange_alone(ex, name):
    n = len(ex.srcs)

    def body(*refs):
        src, dst, sems = refs[:n], refs[n:2 * n], refs[2 * n:]
        ex.start(src, dst, sems)
        ex.finish(src, dst, sems)

    hbm = pl.BlockSpec(memory_space=pl.ANY)
    res = pl.pallas_call(body, in_specs=[hbm] * n, out_specs=[hbm] * n, out_shape=ex.out_shapes,
                         scratch_shapes=ex.sems, name=name)(*ex.srcs)
    ex.results = list(res)
    return ex.results


def _rms_fwd(x, gains, name, tm=512, carry=None):
    t, d = x.shape
    n = len(gains)

    def body(*refs):
        x_ref, g_refs, h_refs = refs[0], refs[1:1 + n], refs[1 + n:]
        xf = x_ref[...]
        xhat = xf * lax.rsqrt(jnp.mean(xf * xf, axis=-1, keepdims=True) + EPS)
        for g_ref, h_ref in zip(g_refs, h_refs):
            h_ref[...] = (xhat * g_ref[...]).astype(BF16)

    row = pl.BlockSpec((tm, d), lambda i: (i, 0))
    vec = pl.BlockSpec((1, d), lambda i: (0, 0))
    return _call(body, [x, *gains], grid=(t // tm,), in_specs=[row] + [vec] * n, out_specs=[row] * n,
                 out_shape=[jax.ShapeDtypeStruct((t, d), BF16)] * n, name=name, carry=carry)


def _rms_bwd(x, gains, dhs, dres, name, tm=256, carry=None):
    t, d = x.shape
    n = len(gains)

    def body(*refs):
        x_ref, dres_ref = refs[0], refs[1]
        g_refs, dh_refs = refs[2:2 + n], refs[2 + n:2 + 2 * n]
        dx_ref, dg_ref = refs[2 + 2 * n], refs[3 + 2 * n]
        i = pl.program_id(0)

        @pl.when(i == 0)
        def _():
            dg_ref[...] = jnp.zeros_like(dg_ref)

        xf = x_ref[...]
        r = lax.rsqrt(jnp.mean(xf * xf, axis=-1, keepdims=True) + EPS)
        xhat = xf * r
        dx = dres_ref[...]
        for j in range(n):
            dh = dh_refs[j][...]
            dg_ref[j:j + 1, :] += jnp.sum(dh * xhat, axis=0, keepdims=True)
            gy = dh * g_refs[j][...]
            dx = dx + r * (gy - xhat * jnp.mean(gy * xhat, axis=-1, keepdims=True))
        dx_ref[...] = dx

    row = pl.BlockSpec((tm, d), lambda i: (i, 0))
    vec = pl.BlockSpec((1, d), lambda i: (0, 0))
    return _call(body, [x, dres, *gains, *dhs], grid=(t // tm,), in_specs=[row, row] + [vec] * n + [row] * n,
                 out_specs=[row, pl.BlockSpec((8, d), lambda i: (0, 0))],
                 out_shape=[jax.ShapeDtypeStruct((t, d), F32), jax.ShapeDtypeStruct((8, d), F32)],
                 name=name, sem=("arbitrary",), carry=carry)


def _mm(a, b, a_spec, b_spec, o_spec, out_shape, grid, dims, name, res=None, res_spec=None, carry=None):
    nk = grid[2]
    acc_shape = tuple(s for s in o_spec.block_shape if s is not None)

    def body(*refs):
        a_ref, b_ref = refs[0], refs[1]
        r_ref = refs[2] if res is not None else None
        o_ref = refs[3] if res is not None else refs[2]
        p = _dot(a_ref[...].astype(BF16), b_ref[...].astype(BF16), dims)
        if nk == 1:
            if res is not None:
                p = p + r_ref[...]
            o_ref[...] = p.astype(o_ref.dtype)
            return
        acc_ref = refs[-1]
        k = pl.program_id(2)

        @pl.when(k == 0)
        def _():
            acc_ref[...] = p

        @pl.when(k > 0)
        def _():
            acc_ref[...] += p

        @pl.when(k == nk - 1)
        def _():
            out = acc_ref[...]
            if res is not None:
                out = out + r_ref[...]
            o_ref[...] = out.astype(o_ref.dtype)

    ins = [a, b] + ([res] if res is not None else [])
    specs = [a_spec, b_spec] + ([res_spec] if res is not None else [])
    return _call(body, ins, grid=grid, in_specs=specs, out_specs=[o_spec], out_shape=[out_shape],
                 scratch=[pltpu.VMEM(acc_shape, F32)] if nk > 1 else [], name=name,
                 sem=("parallel", "parallel", "arbitrary"), carry=carry)[0]


def _mm_rows(a, w, out_dtype, name, trans_w=False, res=None, tm=512, carry=None):
    t, k = a.shape
    n = w.shape[0] if trans_w else w.shape[1]
    return _mm(
        a, w, pl.BlockSpec((tm, k), lambda i, j, kk: (i, 0)), pl.BlockSpec(w.shape, lambda i, j, kk: (0, 0)),
        pl.BlockSpec((tm, n), lambda i, j, kk: (i, 0)), jax.ShapeDtypeStruct((t, n), out_dtype), (t // tm, 1, 1),
        NT if trans_w else NN, name, res=res,
        res_spec=None if res is None else pl.BlockSpec((tm, n), lambda i, j, kk: (i, 0)), carry=carry)


def _mm_wgrad(a, b, name, carry=None):
    t, m = a.shape
    n = b.shape[1]
    tn = n // (4 if b.dtype == F32 else 2)
    return _mm(
        a, b, pl.BlockSpec((t, m), lambda i, j, kk: (0, 0)), pl.BlockSpec((t, tn), lambda i, j, kk: (0, j)),
        pl.BlockSpec((m, tn), lambda i, j, kk: (0, j)), jax.ShapeDtypeStruct((m, n), F32), (1, n // tn, 1), TN, name,
        carry=carry)


def _sgu_fwd(x0, h1, w_in, g_v, w_c, b_sb, w_out, tm=256, carry=None):
    t, d = x0.shape
    nsub = w_in.shape[2]

    def body(x_ref, h_ref, win_ref, gv_ref, wc_ref, bsb_ref, wout_ref, zpre_ref, x1_ref, u_s, v_s, vn_s, y_s):
        h = h_ref[...]
        for k in range(N_SHARDS):
            zk = _dot(h, win_ref[k])
            zpre_ref[:, k * nsub:(k + 1) * nsub] = zk
            cdf, _ = _gelu_parts(zk)
            if k < N_SHARDS // 2:
                u_s[:, k * nsub:(k + 1) * nsub] = zk * cdf
            else:
                v_s[:, (k - 4) * nsub:(k - 3) * nsub] = zk * cdf
        v = v_s[...]
        rv = lax.rsqrt(jnp.mean(v * v, axis=-1, keepdims=True) + EPS)
        vn_s[...] = (v * rv * gv_ref[...]).astype(BF16)
        for ci in range(tm // CHUNK):
            rows = slice(ci * CHUNK, (ci + 1) * CHUNK)
            for g in range(N_GROUPS):
                cols = slice(g * LANES, (g + 1) * LANES)
                sv = _dot(wc_ref[g], vn_s[rows, cols]) + bsb_ref[g]
                y_s[rows, cols] = (u_s[rows, cols] * sv).astype(BF16)
        x1_ref[...] = x_ref[...] + _dot(y_s[...], wout_ref[...])

    row = pl.BlockSpec((tm, d), lambda i: (i, 0))
    full = lambda a: pl.BlockSpec(a.shape, lambda i: (0,) * a.ndim)
    return _call(
        body, [x0, h1, w_in, g_v, w_c, b_sb, w_out], grid=(t // tm,),
        in_specs=[row, row, full(w_in), full(g_v), full(w_c), full(b_sb), full(w_out)],
        out_specs=[pl.BlockSpec((tm, 2 * d), lambda i: (i, 0)), row],
        out_shape=[jax.ShapeDtypeStruct((t, 2 * d), F32), jax.ShapeDtypeStruct((t, d), F32)],
        scratch=[pltpu.VMEM((tm, d), F32), pltpu.VMEM((tm, d), F32), pltpu.VMEM((tm, d), BF16),
                 pltpu.VMEM((tm, d), BF16)],
        name="sgu_fwd", carry=carry)


def _sgu_bwd(dx1, zpre, w_out, g_v, w_c, w_ct, b_sb, tm=256, carry=None):
    t, d = dx1.shape

    def body(dx_ref, zpre_ref, wout_ref, gv_ref, wc_ref, wct_ref, bsb_ref,
             dz_ref, y_ref, dwc_ref, dbs_ref, dgv_ref, u_s, vn_s, dy_s, du_s, dvn_s):
        i = pl.program_id(0)

        @pl.when(i == 0)
        def _():
            dwc_ref[...] = jnp.zeros_like(dwc_ref)
            dbs_ref[...] = jnp.zeros_like(dbs_ref)
            dgv_ref[...] = jnp.zeros_like(dgv_ref)

        dy_s[...] = _dot(dx_ref[...].astype(BF16), wout_ref[...], NT)
        zu = zpre_ref[:, :d]
        zv = zpre_ref[:, d:]
        cdf_u, pdf_u = _gelu_parts(zu)
        cdf_v, pdf_v = _gelu_parts(zv)
        u_s[...] = zu * cdf_u
        v = zv * cdf_v
        rv = lax.rsqrt(jnp.mean(v * v, axis=-1, keepdims=True) + EPS)
        vhat = v * rv
        gv = gv_ref[...]
        vn_s[...] = (vhat * gv).astype(BF16)
        for ci in range(tm // CHUNK):
            rows = slice(ci * CHUNK, (ci + 1) * CHUNK)
            for g in range(N_GROUPS):
                cols = slice(g * LANES, (g + 1) * LANES)
                vnb = vn_s[rows, cols]
                sv = _dot(wc_ref[g], vnb) + bsb_ref[g]
                dyb = dy_s[rows, cols]
                ub = u_s[rows, cols]
                dsv = dyb * ub
                du_s[rows, cols] = dyb * sv
                y_ref[rows, cols] = (ub * sv).astype(BF16)
                dsvb = dsv.astype(BF16)
                dbs_ref[g] += dsv
                dwc_ref[g] += _dot(dsvb, vnb, NT)
                dvn_s[rows, cols] = _dot(wct_ref[g], dsvb)
        dvn = dvn_s[...]
        dgv_ref[0:1, :] += jnp.sum(dvn * vhat, axis=0, keepdims=True)
        gy = dvn * gv
        dv = rv * (gy - vhat * jnp.mean(gy * vhat, axis=-1, keepdims=True))
        dz_ref[:, :d] = (du_s[...] * (cdf_u + zu * pdf_u)).astype(BF16)
        dz_ref[:, d:] = (dv * (cdf_v + zv * pdf_v)).astype(BF16)

        @pl.when(i == t // tm - 1)
        def _():
            tri = (lax.broadcasted_iota(jnp.int32, (CHUNK, CHUNK), 0)
                   >= lax.broadcasted_iota(jnp.int32, (CHUNK, CHUNK), 1))
            for g in range(N_GROUPS):
                dwc_ref[g] = jnp.where(tri, dwc_ref[g], 0.0)
                dbs_ref[g] = jnp.broadcast_to(jnp.sum(dbs_ref[g], axis=1, keepdims=True), (CHUNK, CHUNK))

    row = pl.BlockSpec((tm, d), lambda i: (i, 0))
    row2 = pl.BlockSpec((tm, 2 * d), lambda i: (i, 0))
    full = lambda a: pl.BlockSpec(a.shape, lambda i: (0,) * a.ndim)
    grp = pl.BlockSpec((N_GROUPS, CHUNK, CHUNK), lambda i: (0, 0, 0))
    return _call(
        body, [dx1, zpre, w_out, g_v, w_c, w_ct, b_sb], grid=(t // tm,),
        in_specs=[row, row2, full(w_out), full(g_v), full(w_c), full(w_ct), full(b_sb)],
        out_specs=[row2, row, grp, grp, pl.BlockSpec((8, d), lambda i: (0, 0))],
        out_shape=[jax.ShapeDtypeStruct((t, 2 * d), BF16), jax.ShapeDtypeStruct((t, d), BF16),
                   jax.ShapeDtypeStruct((N_GROUPS, CHUNK, CHUNK), F32),
                   jax.ShapeDtypeStruct((N_GROUPS, CHUNK, CHUNK), F32), jax.ShapeDtypeStruct((8, d), F32)],
        scratch=[pltpu.VMEM((tm, d), F32), pltpu.VMEM((tm, d), BF16), pltpu.VMEM((tm, d), F32),
                 pltpu.VMEM((tm, d), F32), pltpu.VMEM((tm, d), F32)],
        name="sgu_bwd", sem=("arbitrary",), carry=carry)


ROW_CHUNK = 256
HALO = 16


def _ffn_fwd(x, g, w_in, cw, cb, w_out, layer, tm=512, carry=None, next_gains=(), loss_target=None):
    t, d = x.shape
    nc = N_SHARDS // 2
    n_gains = len(next_gains)
    with_loss = loss_target is not None

    def body(x_ref, xp_ref, g_ref, wg_ref, wu_ref, cwg_ref, cbg_ref, cwu_ref, cbu_ref, wout_ref, *rest):
        extra_in, rest = rest[:n_gains + with_loss], rest[n_gains + with_loss:]
        o_ref, hf_ref, a_ref, pre_ref = rest[:4]
        extra_out, hw_s = rest[4:-1], rest[-1]
        i, c = pl.program_id(0), pl.program_id(1)

        @pl.when(c == 0)
        def _():
            keep = jnp.where(i == 0, 0.0, 1.0)
            xw = jnp.concatenate([xp_ref[...] * keep, x_ref[...]], axis=0)
            xhat = xw * lax.rsqrt(jnp.mean(xw * xw, axis=-1, keepdims=True) + EPS)
            hw_s[...] = (xhat * g_ref[...]).astype(BF16)
            hf_ref[...] = hw_s[HALO:, :]
            o_ref[...] = x_ref[...]

        hw = hw_s[...]
        pre = []
        for j, (w_ref, cw_ref, cb_ref) in enumerate(((wg_ref, cwg_ref, cbg_ref), (wu_ref, cwu_ref, cbu_ref))):
            ab = _dot(hw, w_ref[...]).astype(BF16)
            a_ref[j] = ab[HALO:]
            win = ab.astype(F32)
            cw_v = cw_ref[...]
            pre.append(cw_v[2:3, :] * win[HALO:] + cw_v[1:2, :] * pltpu.roll(win, 1, 0)[HALO:]
                       + cw_v[0:1, :] * pltpu.roll(win, 2, 0)[HALO:] + cb_ref[...])
            pre_ref[j] = pre[j]
        act = (pre[0] * _sigmoid(pre[0]) * pre[1]).astype(BF16)
        o_ref[...] += _dot(act, wout_ref[...])

        if with_loss:
            @pl.when((i == 0) & (c == 0))
            def _():
                extra_out[-1][...] = jnp.zeros_like(extra_out[-1])

        @pl.when(c == nc - 1)
        def _():
            xn = o_ref[...]
            if n_gains:
                xhat = xn * lax.rsqrt(jnp.mean(xn * xn, axis=-1, keepdims=True) + EPS)
                for k in range(n_gains):
                    extra_out[k][...] = (xhat * extra_in[k][...]).astype(BF16)
            if with_loss:
                err = xn - extra_in[-1][...]
                extra_out[-2][...] = err * (1.0 / d)
                part = jnp.sum(jnp.sum(err * err, axis=0, keepdims=True), axis=1, keepdims=True)
                extra_out[-1][...] += jnp.broadcast_to(0.5 / d * part, extra_out[-1].shape)

    row = pl.BlockSpec((tm, d), lambda i, c: (i, 0))
    vec = pl.BlockSpec((1, d), lambda i, c: (0, 0))
    shard = lambda rows, up: pl.BlockSpec((None, rows, FF_SHARD), lambda i, c: (c + up * nc, 0, 0))
    pair = pl.BlockSpec((2, None, tm, FF_SHARD), lambda i, c: (0, c, i, 0))
    lanes = pl.BlockSpec((8, LANES), lambda i, c: (0, 0))
    outs = _call(
        body, [x, x, g, w_in, w_in, cw, cb, cw, cb, w_out, *next_gains] + ([loss_target] if with_loss else []),
        grid=(t // tm, nc),
        in_specs=[row, pl.BlockSpec((HALO, d), lambda i, c: (jnp.maximum(i * (tm // HALO) - 1, 0), 0)),
                  vec, shard(d, 0), shard(d, 1), shard(8, 0), shard(1, 0), shard(8, 1), shard(1, 1),
                  pl.BlockSpec((FF_SHARD, d), lambda i, c: (c, 0))] + [vec] * n_gains + [row] * with_loss,
        out_specs=[row, row, pair, pair] + [row] * n_gains + [row, lanes] * with_loss,
        out_shape=[jax.ShapeDtypeStruct((t, d), F32), jax.ShapeDtypeStruct((t, d), BF16),
                   jax.ShapeDtypeStruct((2, nc, t, FF_SHARD), BF16), jax.ShapeDtypeStruct((2, nc, t, FF_SHARD), F32)]
        + [jax.ShapeDtypeStruct((t, d), BF16)] * n_gains
        + [jax.ShapeDtypeStruct((t, d), F32), jax.ShapeDtypeStruct((8, LANES), F32)] * with_loss,
        scratch=[pltpu.VMEM((tm + HALO, d), BF16)], name=f"ffn{layer}_fwd", sem=("arbitrary", "arbitrary"), carry=carry)
    return (outs[0], outs[1], outs[2].reshape(N_SHARDS, t, FF_SHARD), outs[3]) + tuple(outs[4:])


def _ffn_bwd_act(pre, w_out, dxn, layer, tm=512, carry=None):
    t, d = dxn.shape
    nc = N_SHARDS // 2

    def body(pre_ref, wout_ref, dx_ref, dhu_ref, dw_ref, dcb_ref):
        i = pl.program_id(1)

        @pl.when(i == 0)
        def _():
            dw_ref[...] = jnp.zeros_like(dw_ref)
            dcb_ref[...] = jnp.zeros_like(dcb_ref)

        hg, hu = pre_ref[0], pre_ref[1]
        sg = _sigmoid(hg)
        sl = hg * sg
        dxb = dx_ref[...].astype(BF16)
        dact = _dot(dxb, wout_ref[...], NT)
        dw_ref[...] += _dot((sl * hu).astype(BF16), dxb, TN)
        d_up = dact * sl
        d_gate = dact * hu * (sg * (1.0 + hg * (1.0 - sg)))
        for j, dv in enumerate((d_gate, d_up)):
            dhu_ref[j] = dv.astype(BF16)
            dcb_ref[j, 0:1, :] += jnp.sum(dv, axis=0, keepdims=True)

    return _call(
        body, [pre, w_out, dxn], grid=(nc, t // tm),
        in_specs=[pl.BlockSpec((2, None, tm, FF_SHARD), lambda c, i: (0, c, i, 0)),
                  pl.BlockSpec((FF_SHARD, d), lambda c, i: (c, 0)), pl.BlockSpec((tm, d), lambda c, i: (i, 0))],
        out_specs=[pl.BlockSpec((None, 2, tm, FF_SHARD), lambda c, i: (c, 0, i, 0)),
                   pl.BlockSpec((FF_SHARD, d), lambda c, i: (c, 0)),
                   pl.BlockSpec((None, 2, 8, FF_SHARD), lambda c, i: (c, 0, 0, 0))],
        out_shape=[jax.ShapeDtypeStruct((nc, 2, t, FF_SHARD), BF16), jax.ShapeDtypeStruct((D_FF, d), F32),
                   jax.ShapeDtypeStruct((nc, 2, 8, FF_SHARD), F32)],
        name=f"ffn{layer}_bwd_act", sem=("parallel", "arbitrary"), carry=carry)


def _ffn_bwd_in(dhu, a, cw, w_in, layer, tm=1024, carry=None):
    nc, _, t, _ = dhu.shape
    d = D_MODEL
    tm = min(tm, t)
    last_blk = t // 16 - 1

    def body(dh_ref, nx_ref, a_ref, cw_ref, win_ref, da_ref, o_ref, dcw_ref):
        i, s = pl.program_id(0), pl.program_id(1)

        @pl.when(s == 0)
        def _():
            o_ref[...] = jnp.zeros_like(o_ref)

        @pl.when((s == 0) & (i == 0))
        def _():
            dcw_ref[...] = jnp.zeros_like(dcw_ref)

        keep = jnp.where(i == t // tm - 1, 0.0, 1.0)
        cw = cw_ref[...]
        sums = [None] * 3
        for r0 in range(0, tm, ROW_CHUNK):
            rows = slice(r0, r0 + ROW_CHUNK)
            if r0 + ROW_CHUNK == tm:
                win = jnp.concatenate([dh_ref[rows, :].astype(F32), nx_ref[...].astype(F32) * keep], axis=0)
            else:
                win = dh_ref[r0:r0 + ROW_CHUNK + HALO, :].astype(F32)
            n = ROW_CHUNK + HALO
            taps = (pltpu.roll(win, n - 2, 0)[:ROW_CHUNK],
                    pltpu.roll(win, n - 1, 0)[:ROW_CHUNK],
                    win[:ROW_CHUNK])
            da = (cw[0:1, :] * taps[0] + cw[1:2, :] * taps[1] + cw[2:3, :] * taps[2]).astype(BF16)
            da_ref[rows, :] = da
            o_ref[rows, :] += _dot(da, win_ref[...], NT)
            af = a_ref[rows, :].astype(F32)
            parts = [jnp.sum(taps[k] * af, axis=0, keepdims=True) for k in range(3)]
            sums = [p if q is None else q + p for q, p in zip(sums, parts)]
        for k in range(3):
            dcw_ref[pl.ds(s, 1), k:k + 1, :] += sums[k][None]

    return _call(
        body, [dhu, dhu, a, cw, w_in], grid=(t // tm, N_SHARDS),
        in_specs=[pl.BlockSpec((None, None, tm, FF_SHARD), lambda i, s: (s % nc, s // nc, i, 0)),
                  pl.BlockSpec((None, None, 16, FF_SHARD),
                               lambda i, s: (s % nc, s // nc, jnp.minimum((i + 1) * (tm // 16), last_blk), 0)),
                  pl.BlockSpec((None, tm, FF_SHARD), lambda i, s: (s, i, 0)),
                  pl.BlockSpec((None, 8, FF_SHARD), lambda i, s: (s, 0, 0)),
                  pl.BlockSpec((None, d, FF_SHARD), lambda i, s: (s, 0, 0))],
        out_specs=[pl.BlockSpec((None, tm, FF_SHARD), lambda i, s: (s, i, 0)),
                   pl.BlockSpec((tm, d), lambda i, s: (i, 0)),
                   pl.BlockSpec((N_SHARDS, 8, FF_SHARD), lambda i, s: (0, 0, 0))],
        out_shape=[jax.ShapeDtypeStruct((N_SHARDS, t, FF_SHARD), BF16), jax.ShapeDtypeStruct((t, d), F32),
                   jax.ShapeDtypeStruct((N_SHARDS, 8, FF_SHARD), F32)],
        name=f"ffn{layer}_bwd_in", sem=("arbitrary", "arbitrary"), carry=carry)


def _ffn_wgrad_in(hf, da, layer, carry=None):
    t, d = hf.shape
    return _mm(
        da, hf, pl.BlockSpec((None, t, FF_SHARD), lambda s, j, kk: (s, 0, 0)),
        pl.BlockSpec((t, d), lambda s, j, kk: (0, 0)),
        pl.BlockSpec((None, FF_SHARD, d), lambda s, j, kk: (s, 0, 0)),
        jax.ShapeDtypeStruct((N_SHARDS, FF_SHARD, d), F32), (N_SHARDS, 1, 1), TN, f"ffn{layer}_wgrad_in",
        carry=carry)


Q_PER_KV = N_Q_HEADS // N_KV_HEADS
GROUP_ROWS = Q_PER_KV * CHUNK


def _attn_masks(n):
    lane = lax.broadcasted_iota(jnp.int32, (CHUNK, LANES), 1)
    lo = lane < HEAD_DIM
    tq = lax.broadcasted_iota(jnp.int32, (GROUP_ROWS, 2 * CHUNK), 0) & (CHUNK - 1)
    jk = lax.broadcasted_iota(jnp.int32, (GROUP_ROWS, 2 * CHUNK), 1)
    dist = tq + CHUNK - jk
    mask = (dist >= 0) & (dist < CHUNK) & (jk >= jnp.where(n == 0, CHUNK, 0))
    return lo, mask, dist.astype(F32)


def _per_head_column(values):
    r = lax.broadcasted_iota(jnp.int32, (GROUP_ROWS, 1), 0)
    col = jnp.full((GROUP_ROWS, 1), values[Q_PER_KV - 1], F32)
    for j in range(Q_PER_KV - 2, -1, -1):
        col = jnp.where(r < (j + 1) * CHUNK, values[j], col)
    return col


def _half_sum(x, lo):
    s_lo = jnp.sum(jnp.where(lo, x, 0.0), axis=-1, keepdims=True)
    s_hi = jnp.sum(jnp.where(lo, 0.0, x), axis=-1, keepdims=True)
    return jnp.where(lo, s_lo, s_hi)


def _stack_heads(pairs, lo):
    zero = jnp.zeros_like(pairs[0])
    return jnp.concatenate([jnp.where(lo, pairs[0], zero), jnp.where(lo, zero, pairs[0]),
                            jnp.where(lo, pairs[1], zero), jnp.where(lo, zero, pairs[1])], axis=0)


def _unstack_heads(stacked, lo):
    return (jnp.where(lo, stacked[0:CHUNK], stacked[CHUNK:2 * CHUNK]),
            jnp.where(lo, stacked[2 * CHUNK:3 * CHUNK], stacked[3 * CHUNK:]))


def _attn_probs(qs, kn, mask, distf, slope_col, sink_col):
    s = _dot(qs, kn, NT) * (HEAD_DIM ** -0.5)
    s = jnp.where(mask, s - slope_col * distf, NEG_BIG)
    m = jnp.maximum(jnp.max(s, axis=-1, keepdims=True), sink_col)
    e = jnp.exp(s - m)
    den = jnp.sum(e, axis=-1, keepdims=True) + jnp.exp(sink_col - m)
    return e * (1.0 / den), m, den


def _attn_fwd(qraw, kvd, gq, gk, sinks, carry=None):
    t, d = qraw.shape
    nb = t // CHUNK

    def body(sink_ref, q_ref, cur_ref, prev_ref, gq_ref, gk_ref, o_ref):
        n = pl.program_id(0)
        lo, mask, distf = _attn_masks(n)
        gq_v, gk_v = gq_ref[...], gk_ref[...]
        for kvh in range(N_KV_HEADS):
            ks = slice(kvh * LANES, (kvh + 1) * LANES)
            vs = slice(4 * LANES + kvh * LANES, 4 * LANES + (kvh + 1) * LANES)
            kraw = jnp.concatenate([prev_ref[:, ks], cur_ref[:, ks]], axis=0)
            rk = lax.rsqrt(jnp.mean(kraw * kraw, axis=-1, keepdims=True) + EPS)
            kn = (kraw * rk * gk_v).astype(BF16)
            vv = jnp.concatenate([prev_ref[:, vs], cur_ref[:, vs]], axis=0).astype(BF16)
            qn = []
            for p in range(2):
                qp = q_ref[:, (2 * kvh + p) * LANES:(2 * kvh + p + 1) * LANES]
                r = lax.rsqrt(_half_sum(qp * qp, lo) * (1.0 / HEAD_DIM) + EPS)
                qn.append(qp * r * gq_v)
            heads = range(Q_PER_KV * kvh, Q_PER_KV * (kvh + 1))
            pf, _, _ = _attn_probs(_stack_heads(qn, lo).astype(BF16), kn, mask, distf,
                                   _per_head_column([SLOPES[h] for h in heads]),
                                   _per_head_column([sink_ref[h] for h in heads]))
            for p, o_pair in enumerate(_unstack_heads(_dot(pf.astype(BF16), vv), lo)):
                o_ref[:, (2 * kvh + p) * LANES:(2 * kvh + p + 1) * LANES] = o_pair.astype(BF16)

    blk = lambda f: pl.BlockSpec((CHUNK, d), f)
    vec = pl.BlockSpec((1, LANES), lambda n: (0, 0))
    return _call(
        body, [sinks, qraw, kvd, kvd, gq, gk], grid=(nb,),
        in_specs=[pl.BlockSpec(memory_space=pltpu.SMEM), blk(lambda n: (n, 0)), blk(lambda n: (n, 0)),
                  blk(lambda n: (jnp.maximum(n - 1, 0), 0)), vec, vec],
        out_specs=[blk(lambda n: (n, 0))], out_shape=[jax.ShapeDtypeStruct((t, d), BF16)],
        name="attn_fwd", carry=carry)[0]


def _attn_bwd(qraw, kvd, d_o, gq, gk, sinks, carry=None):
    t, d = qraw.shape
    nb = t // CHUNK

    def body(sink_ref, q_ref, cur_ref, prev_ref, do_ref, gq_ref, gk_ref,
             dq_ref, dkv_ref, dsink_ref, dgq_ref, dgk_ref, carry_s, pp_s, cp_s):
        n = pl.program_id(0)

        @pl.when(n == 0)
        def _():
            carry_s[...] = jnp.zeros_like(carry_s)
            dsink_ref[...] = jnp.zeros_like(dsink_ref)
            dgq_ref[...] = jnp.zeros_like(dgq_ref)
            dgk_ref[...] = jnp.zeros_like(dgk_ref)

        @pl.when(n < nb)
        def _():
            lo, mask, distf = _attn_masks(n)
            gq_v, gk_v = gq_ref[...], gk_ref[...]
            for kvh in range(N_KV_HEADS):
                ks = slice(kvh * LANES, (kvh + 1) * LANES)
                vs = slice(4 * LANES + kvh * LANES, 4 * LANES + (kvh + 1) * LANES)
                kraw = jnp.concatenate([prev_ref[:, ks], cur_ref[:, ks]], axis=0)
                rk = lax.rsqrt(jnp.mean(kraw * kraw, axis=-1, keepdims=True) + EPS)
                khat = kraw * rk
                kn = (khat * gk_v).astype(BF16)
                vv = jnp.concatenate([prev_ref[:, vs], cur_ref[:, vs]], axis=0).astype(BF16)
                cols = [slice((2 * kvh + p) * LANES, (2 * kvh + p + 1) * LANES) for p in range(2)]
                rq, qhat = [], []
                for p in range(2):
                    qp = q_ref[:, cols[p]]
                    rq.append(lax.rsqrt(_half_sum(qp * qp, lo) * (1.0 / HEAD_DIM) + EPS))
                    qhat.append(qp * rq[p])
                heads = range(Q_PER_KV * kvh, Q_PER_KV * (kvh + 1))
                qs = _stack_heads([qhat[p] * gq_v for p in range(2)], lo).astype(BF16)
                dos = _stack_heads([do_ref[:, cols[p]] for p in range(2)], lo)
                sink_col = _per_head_column([sink_ref[h] for h in heads])
                pf, m, den = _attn_probs(qs, kn, mask, distf, _per_head_column([SLOPES[h] for h in heads]), sink_col)
                dp = _dot(dos, vv, NT)
                delta = jnp.sum(pf * dp, axis=-1, keepdims=True)
                sink_delta = jnp.exp(sink_col - m) / den * delta
                for j, h in enumerate(heads):
                    dsink_ref[h:h + 1, :] -= jnp.broadcast_to(
                        jnp.sum(sink_delta[j * CHUNK:(j + 1) * CHUNK], axis=0, keepdims=True), (1, LANES))
                ds = (pf * (dp - delta) * (HEAD_DIM ** -0.5)).astype(BF16)
                dkn = _dot(ds, qs, TN)
                dvb = _dot(pf.astype(BF16), dos, TN)
                for p, dqn in enumerate(_unstack_heads(_dot(ds, kn), lo)):
                    dgq_ref[0:1, :] += jnp.sum(dqn * qhat[p], axis=0, keepdims=True)
                    gy = dqn * gq_v
                    mq = _half_sum(gy * qhat[p], lo) * (1.0 / HEAD_DIM)
                    dq_ref[:, cols[p]] = (rq[p] * (gy - qhat[p] * mq)).astype(BF16)
                dgk_ref[0:1, :] += jnp.sum(dkn * khat, axis=0, keepdims=True)
                gyk = dkn * gk_v
                dkraw = rk * (gyk - khat * jnp.mean(gyk * khat, axis=-1, keepdims=True))
                pp_s[:, ks] = dkraw[:CHUNK]
                cp_s[:, ks] = dkraw[CHUNK:]
                pp_s[:, vs] = dvb[:CHUNK]
                cp_s[:, vs] = dvb[CHUNK:]
            dkv_ref[...] = (carry_s[...] + pp_s[...]).astype(BF16)
            carry_s[...] = cp_s[...]

        @pl.when(n == nb)
        def _():
            dkv_ref[...] = carry_s[...].astype(BF16)

    blk = lambda f: pl.BlockSpec((CHUNK, d), f)
    vec = pl.BlockSpec((1, LANES), lambda n: (0, 0))
    cur = lambda n: (jnp.minimum(n, nb - 1), 0)
    prev = lambda n: (jnp.maximum(jnp.minimum(n, nb - 1) - 1, 0), 0)
    small = lambda r: pl.BlockSpec((r, LANES), lambda n: (0, 0))
    return _call(
        body, [sinks, qraw, kvd, kvd, d_o, gq, gk], grid=(nb + 1,),
        in_specs=[pl.BlockSpec(memory_space=pltpu.SMEM), blk(cur), blk(cur), blk(prev), blk(cur), vec, vec],
        out_specs=[blk(cur), blk(lambda n: (jnp.maximum(n - 1, 0), 0)), small(N_Q_HEADS), small(8), small(8)],
        out_shape=[jax.ShapeDtypeStruct((t, d), BF16), jax.ShapeDtypeStruct((t, d), BF16),
                   jax.ShapeDtypeStruct((N_Q_HEADS, LANES), F32), jax.ShapeDtypeStruct((8, LANES), F32),
                   jax.ShapeDtypeStruct((8, LANES), F32)],
        scratch=[pltpu.VMEM((CHUNK, d), F32)] * 3, name="attn_bwd", sem=("arbitrary",), carry=carry)


def _adamw_math(g, w, m, v):
    m = ADAM_B1 * m + (1.0 - ADAM_B1) * g
    v = ADAM_B2 * v + (1.0 - ADAM_B2) * (g * g)
    m_hat = m / (1.0 - ADAM_B1 ** ADAM_STEP)
    v_hat = v / (1.0 - ADAM_B2 ** ADAM_STEP)
    delta = -ADAM_LR * (m_hat / (jnp.sqrt(v_hat) + ADAM_EPS) + ADAM_WD * w)
    return delta, m, v


def _row_tile(r, cap=128):
    for tr in range(min(r, cap), 0, -1):
        if r % tr == 0 and (tr % 8 == 0 or tr == r):
            return tr
    return r


def _chip_sum(grad, recv, place, name, wire_dtype):
    _, r, c = grad.shape
    tr = _row_tile(r, 256)

    def body(pl_ref, g_ref, a_ref, p_ref):
        p_ref[...] = (g_ref[...] + a_ref[...]).astype(p_ref.dtype)

    return pl.pallas_call(
        body,
        grid_spec=pltpu.PrefetchScalarGridSpec(
            num_scalar_prefetch=1, grid=(4, r // tr),
            in_specs=[pl.BlockSpec((None, None, tr, c), lambda q, i, pr: (q, pr[1], i, 0)),
                      pl.BlockSpec((None, tr, c), lambda q, i, pr: (q, i, 0))],
            out_specs=pl.BlockSpec((None, tr, c), lambda q, i, pr: (q, i, 0))),
        out_shape=jax.ShapeDtypeStruct((4, r, c), wire_dtype), name=name, compiler_params=_params(),
    )(place, grad.reshape(4, 2, r, c), recv)


def _adamw_sharded(grad, recv, others, place, w, m, v, name, layer=None, fill=None):
    r, c = w.shape[-2:]
    tr = _row_tile(r)

    def body(pl_ref, g_ref, a_ref, oth_ref, w_ref, m_ref, v_ref, *rest):
        g_out, d_out, nm_out, nv_out = rest[-4:]
        g = g_ref[...] + a_ref[...]
        for k in range(3):
            g = g + oth_ref[k].astype(F32)
        delta, nm, nv = _adamw_math(g, w_ref[...], m_ref[...], v_ref[...])
        g_out[...] = g
        d_out[...] = delta
        nm_out[...] = nm
        nv_out[...] = nv

    if layer is None:
        row = pl.BlockSpec((tr, c), lambda i, pr: (i, 0))
    else:
        row = pl.BlockSpec((None, tr, c), lambda i, pr: (layer, i, 0))
    n_fill = 0 if fill is None else 4
    in_specs = [pl.BlockSpec((None, None, tr, c), lambda i, pr: (pr[0], pr[1], i, 0)),
                pl.BlockSpec((None, tr, c), lambda i, pr: (pr[0], i, 0)),
                pl.BlockSpec((3, tr, c), lambda i, pr: (0, i, 0)), row, row, row]
    in_specs += [pl.BlockSpec(memory_space=pl.ANY)] * n_fill
    return pl.pallas_call(
        body,
        grid_spec=pltpu.PrefetchScalarGridSpec(
            num_scalar_prefetch=1, grid=(r // tr,), in_specs=in_specs, out_specs=[row] * 4),
        out_shape=[jax.ShapeDtypeStruct(w.shape, F32)] * 4, name=name, compiler_params=_params(),
        input_output_aliases={7 + j: j for j in range(n_fill)},
    )(place, grad.reshape(4, 2, r, c), recv, others, w, m, v, *([] if fill is None else fill))


def _adamw_summed(parts, ws, ms, vs, name):
    n = len(parts)

    def body(*refs):
        p_refs, w_refs, m_refs, v_refs = refs[:n], refs[n:2 * n], refs[2 * n:3 * n], refs[3 * n:4 * n]
        o_refs = refs[4 * n:]
        for i in range(n):
            g = p_refs[i][0]
            for k in range(1, N_SHARDS):
                g = g + p_refs[i][k]
            delta, nm, nv = _adamw_math(g, w_refs[i][...], m_refs[i][...], v_refs[i][...])
            o_refs[4 * i][...] = g
            o_refs[4 * i + 1][...] = delta
            o_refs[4 * i + 2][...] = nm
            o_refs[4 * i + 3][...] = nv

    shapes = [jax.ShapeDtypeStruct(w.shape, F32) for w in ws for _ in range(4)]
    outs = pl.pallas_call(body, out_shape=shapes, name=name, compiler_params=_params())(*parts, *ws, *ms, *vs)
    return [outs[4 * i:4 * i + 4] for i in range(n)]


def _dup_heads(w):
    lead = w.shape[:-1]
    w4 = w.reshape(lead + (N_KV_HEADS, 1, HEAD_DIM))
    return jnp.broadcast_to(w4, lead + (N_KV_HEADS, 2, HEAD_DIM)).reshape(lead + (N_KV_HEADS * LANES,))


def _fold_heads(g):
    lead = g.shape[:-1]
    return g.reshape(lead + (N_KV_HEADS, 2, HEAD_DIM)).sum(axis=-2).reshape(lead + (N_KV_HEADS * HEAD_DIM,))


def kernel(x, a_norm, a_w_in, a_v_norm, a_w_s, a_b_s, a_w_out, f_norm, f_w_in, f_conv_w, f_conv_b, f_w_out, kv_norm, w_kv, k_norm, b_norm, b_w_q, b_q_norm, b_sinks, b_w_o, loss_target, m_a_norm, m_a_w_in, m_a_v_norm, m_a_w_s, m_a_b_s, m_a_w_out, m_f_norm, m_f_w_in, m_f_conv_w, m_f_conv_b, m_f_w_out, m_kv_norm, m_w_kv, m_k_norm, m_b_norm, m_b_w_q, m_b_q_norm, m_b_sinks, m_b_w_o, v_a_norm, v_a_w_in, v_a_v_norm, v_a_w_s, v_a_b_s, v_a_w_out, v_f_norm, v_f_w_in, v_f_conv_w, v_f_conv_b, v_f_w_out, v_kv_norm, v_w_kv, v_k_norm, v_b_norm, v_b_w_q, v_b_q_norm, v_b_sinks, v_b_w_o):
    d = D_MODEL
    xi, yi, ci = _coords()
    place = jnp.stack([2 * xi + yi, ci]).astype(jnp.int32)
    bf = lambda a: a.astype(BF16)
    row = lambda v_: v_.reshape(1, -1)
    x0, target = x[0], loss_target[0]
    t = x0.shape[0]
    res = {}

    red = {}

    def to_sibling(grads, wire=BF16):
        for k, g in grads.items():
            red[k] = dict(grad=g, wire=wire)
        ex = _ToSibling(list(grads.values()))
        ex.names = list(grads)
        return ex

    def to_chips(ex):
        for k, a in zip(ex.names, ex.results):
            red[k]["recv"] = a
            red[k]["psum"] = _chip_sum(red[k]["grad"], a, place, f"chip_sum_{k}", red[k]["wire"])
        nxt = _ToChips([red[k]["psum"] for k in ex.names])
        nxt.names = ex.names
        return nxt

    def landed(ex):
        for k, b in zip(ex.names, ex.results):
            red[k]["others"] = b

    def halves(ex, first_rows):
        parts = []
        for r0, nr in ((0, first_rows), (first_rows, ex.srcs[0].shape[1] - first_rows)):
            part = _ToChips(ex.srcs, rows=(r0, nr))
            part.names = ex.names
            parts.append(part)
        return parts

    def landed_halves(parts):
        for j, k in enumerate(parts[0].names):
            red[k]["others"] = jnp.concatenate([p.results[j] for p in parts], axis=1)

    def update(k, w, m, v, layer=None, fill=None):
        r = red[k]
        return _adamw_sharded(r["grad"], r["recv"], r["others"], place, w, m, v,
                              f"adamw_{k}", layer=layer, fill=fill)

    g_a_in, g_a_out, g_a_norm, g_a_v_norm, g_conv = _exchange_alone(
        _Gather([bf(a_w_in[0]), bf(a_w_out[0]), a_norm, a_v_norm, f_conv_w.reshape(6, FF_SHARD)]), "gather_first")
    a_norm_full, a_v_norm_full = g_a_norm.reshape(1, d), g_a_v_norm.reshape(1, d)
    conv_w = lax.reduce_precision(g_conv.reshape(N_SHARDS, 2, 3, FF_SHARD), 8, 7)
    cw = jnp.pad(jnp.transpose(conv_w, (1, 0, 2, 3)), ((0, 0), (0, 0), (0, 5), (0, 0)))
    w_a_in_flat = jnp.transpose(g_a_in, (1, 0, 2)).reshape(d, 2 * d)
    cb = f_conv_b.reshape(2, N_SHARDS, 1, FF_SHARD)
    tri = jnp.tril(jnp.ones((CHUNK, CHUNK), dtype=bool))
    w_causal = jnp.where(tri[None], a_w_s[0], 0.0).astype(BF16)
    w_causal_t = jnp.transpose(w_causal, (0, 2, 1))
    b_sb = jnp.broadcast_to(a_b_s[0][:, :, None], (N_GROUPS, CHUNK, CHUNK))
    w_a_out = g_a_out.reshape(d, d)
    gq = jnp.tile(b_q_norm.reshape(1, HEAD_DIM), (1, 2))
    gk = jnp.tile(k_norm.reshape(1, HEAD_DIM), (1, 2))
    sinks = b_sinks.reshape(N_Q_HEADS)

    (h1,) = _rms_fwd(x0, [a_norm_full], "a_norm_fwd")
    ex = _Gather([bf(f_w_in[0]), bf(f_w_out[0])])
    zpre, x1 = _sgu_fwd(x0, h1, g_a_in, a_v_norm_full, w_causal, b_sb, w_a_out, carry=ex)
    w_in0, w_out0 = ex.results[0], ex.results[1].reshape(D_FF, d)
    ex = _Gather([bf(w_kv), bf(b_w_q[0]), bf(b_w_o[0]), bf(f_w_in[1])])
    x2, hf0, a0, pre0, hk, hq = _ffn_fwd(x1, f_norm[0:1], w_in0, cw[0], cb[0], w_out0, 0, carry=ex,
                                         next_gains=[row(kv_norm), b_norm])
    kv_full = ex.results[0].reshape(d, 2 * N_KV_HEADS * HEAD_DIM)
    w_q, w_o = ex.results[1].reshape(d, d), ex.results[2].reshape(d, d)
    w_in1 = ex.results[3]
    half = N_KV_HEADS * HEAD_DIM
    w_kv_dup = jnp.concatenate([_dup_heads(kv_full[:, :half]), _dup_heads(kv_full[:, half:])], axis=1)
    kvd = _mm_rows(hk, w_kv_dup, F32, "kv_proj")
    qraw = _mm_rows(hq, w_q, F32, "q_proj")
    ex = _Gather([bf(f_w_out[1])])
    o = _attn_fwd(qraw, kvd, gq, gk, sinks, carry=ex)
    w_out1 = ex.results[0].reshape(D_FF, d)
    x3 = _mm_rows(o, w_o, F32, "o_proj", res=x2)
    _, hf1, a1, pre1, dy, loss_lanes = _ffn_fwd(x3, f_norm[1:2], w_in1, cw[1], cb[1], w_out1, 1, loss_target=target)
    loss = lax.psum(loss_lanes[0, 0], ("x", "y", "c"))

    dhu1, dw_out1, dcb1 = _ffn_bwd_act(pre1, w_out1, dy, 1)
    ex = to_sibling({"f_w_out1": dw_out1.reshape(N_SHARDS, D_FF // N_SHARDS, d)})
    da1, dhf1, dcw1 = _ffn_bwd_in(dhu1, a1, cw[1], w_in1, 1, carry=ex)
    ex = to_chips(ex)
    dw_in1 = _ffn_wgrad_in(hf1, da1, 1, carry=ex)
    landed(ex)
    ex = to_sibling({"f_w_in1": dw_in1})
    dx3, dgf1 = _rms_bwd(x3, [f_norm[1:2]], [dhf1], dy, "f1_norm_bwd", carry=ex)
    ex = to_chips(ex)
    d_o = _mm_rows(dx3, w_o, BF16, "o_proj_bwd", trans_w=True)
    dw_o = _mm_wgrad(o, dx3, "o_wgrad").reshape(N_SHARDS, d // N_SHARDS, d)
    dq, dkv, dsink, dgq, dgk = _attn_bwd(qraw, kvd, d_o, gq, gk, sinks, carry=ex)
    landed(ex)
    dw_q = _mm_wgrad(hq, dq, "q_wgrad").reshape(N_SHARDS, d // N_SHARDS, d)
    dw_kv_dup = _mm_wgrad(hk, dkv, "kv_wgrad")
    dw_kv = jnp.concatenate(
        [_fold_heads(dw_kv_dup[:, :4 * LANES]), _fold_heads(dw_kv_dup[:, 4 * LANES:])], axis=1
    ).reshape(N_SHARDS, d // N_SHARDS, 2 * N_KV_HEADS * HEAD_DIM)
    ex = to_sibling({"b_w_o": dw_o, "b_w_q": dw_q, "w_kv": dw_kv})
    dhq = _mm_rows(dq, w_q, F32, "q_proj_bwd", trans_w=True, carry=ex)
    dhk = _mm_rows(dkv, w_kv_dup, F32, "kv_proj_bwd", trans_w=True)
    ex = to_chips(ex)
    dx2, dg2 = _rms_bwd(x2, [row(kv_norm), b_norm], [dhk, dhq], dx3, "kvq_norm_bwd")
    dhu0, dw_out0, dcb0 = _ffn_bwd_act(pre0, w_out0, dx2, 0, carry=ex)
    landed(ex)
    ex = to_sibling({"f_w_out0": dw_out0.reshape(N_SHARDS, D_FF // N_SHARDS, d)})
    da0, dhf0, dcw0 = _ffn_bwd_in(dhu0, a0, cw[0], w_in0, 0, carry=ex)
    ex = to_chips(ex)
    dw_in0 = _ffn_wgrad_in(hf0, da0, 0, carry=ex)
    landed(ex)
    ex = to_sibling({"f_w_in0": dw_in0})
    dx1, dgf0 = _rms_bwd(x1, [f_norm[0:1]], [dhf0], dx2, "f0_norm_bwd", carry=ex)
    ex_lo, ex_hi = halves(to_chips(ex), 448)
    dz, y, dwc, dbs, dgv = _sgu_bwd(dx1, zpre, w_a_out, a_v_norm_full, w_causal, w_causal_t, b_sb, carry=ex_lo)
    dw_a_out = _mm_wgrad(y, dx1, "a_out_wgrad").reshape(N_SHARDS, d // N_SHARDS, d)
    nsub = g_a_in.shape[2]
    dw_a_in = _mm(
        h1, dz, pl.BlockSpec((t, d), lambda s, j, kk: (0, 0)), pl.BlockSpec((t, nsub), lambda s, j, kk: (0, s)),
        pl.BlockSpec((None, d, nsub), lambda s, j, kk: (s, 0, 0)), jax.ShapeDtypeStruct((N_SHARDS, d, nsub), F32),
        (N_SHARDS, 1, 1), TN, "a_in_wgrad", carry=ex_hi)
    landed_halves([ex_lo, ex_hi])

    def bias_grad(dcb):
        return jnp.transpose(dcb[:, :, 0, :], (1, 0, 2)).reshape(-1)

    g_conv_w = jnp.concatenate([dcw0[:, 0:3, :], dcw1[:, 0:3, :]], axis=1)
    g_a_v_norm = dgv[0].reshape(N_SHARDS, 1, LANES)
    rep = ["a_w_s", "a_b_s", "f_norm", "f_conv_b", "kv_norm", "k_norm", "b_norm", "b_q_norm", "b_sinks"]
    rep_g = dict(
        a_w_s=dwc.reshape(N_GROUPS * CHUNK, CHUNK), a_b_s=dbs[:, :, 0], f_norm=jnp.stack([dgf0[0], dgf1[0]]),
        f_conv_b=jnp.stack([bias_grad(dcb0), bias_grad(dcb1)]), kv_norm=dg2[0:1],
        k_norm=(dgk[0, :HEAD_DIM] + dgk[0, HEAD_DIM:])[None], b_norm=dg2[1:2],
        b_q_norm=(dgq[0, :HEAD_DIM] + dgq[0, HEAD_DIM:])[None], b_sinks=dsink[:, 0][None])
    ex_big = to_sibling({"a_w_out": dw_a_out, "a_w_in": dw_a_in})
    ex_small = to_sibling({"a_v_norm": g_a_v_norm, "f_conv_w": g_conv_w}, wire=F32)
    ex_rep = _Gather([rep_g[k] for k in rep])
    together = _Together([ex_big, ex_small, ex_rep])
    dh1 = _mm_rows(dz, w_a_in_flat, F32, "a_in_bwd", trans_w=True, carry=together)
    together.spread()
    ex_big, ex_small = to_chips(ex_big), to_chips(ex_small)
    together = _Together([ex_big, ex_small])
    grad_x, dg0 = _rms_bwd(x0, [a_norm_full], [dh1], dx1, "a_norm_bwd", carry=together)
    together.spread()
    landed(ex_big)
    landed(ex_small)
    (a_norm_parts,) = _exchange_alone(_ToOwners([dg0[0].reshape(N_SHARDS, 1, LANES)]), "a_norm_to_owners")

    res["f_w_out"] = update("f_w_out1", f_w_out, m_f_w_out, v_f_w_out, layer=1)
    w_in_t = [jnp.swapaxes(a_, 1, 2) for a_ in (f_w_in, m_f_w_in, v_f_w_in)]
    res["f_w_in"] = update("f_w_in1", *w_in_t, layer=1)
    res["b_w_o"] = update("b_w_o", b_w_o, m_b_w_o, v_b_w_o, layer=0)
    res["b_w_q"] = update("b_w_q", b_w_q, m_b_w_q, v_b_w_q, layer=0)
    res["w_kv"] = update("w_kv", w_kv, m_w_kv, v_w_kv)
    res["f_w_out"] = update("f_w_out0", f_w_out, m_f_w_out, v_f_w_out, layer=0, fill=res["f_w_out"])
    res["f_w_in"] = [jnp.swapaxes(o_, 1, 2) for o_ in update("f_w_in0", *w_in_t, layer=0, fill=res["f_w_in"])]
    res["a_w_out"] = update("a_w_out", a_w_out, m_a_w_out, v_a_w_out, layer=0)
    res["a_w_in"] = update("a_w_in", a_w_in, m_a_w_in, v_a_w_in, layer=0)
    res["a_v_norm"] = update("a_v_norm", a_v_norm, m_a_v_norm, v_a_v_norm)
    res["f_conv_w"] = [o_.reshape(f_conv_w.shape) for o_ in update(
        "f_conv_w", f_conv_w.reshape(6, FF_SHARD), m_f_conv_w.reshape(6, FF_SHARD), v_f_conv_w.reshape(6, FF_SHARD))]

    rep_w = dict(a_w_s=a_w_s, a_b_s=a_b_s, f_norm=f_norm, f_conv_b=f_conv_b, kv_norm=kv_norm, k_norm=k_norm,
                 b_norm=b_norm, b_q_norm=b_q_norm, b_sinks=b_sinks, a_norm=a_norm)
    rep_m = dict(a_w_s=m_a_w_s, a_b_s=m_a_b_s, f_norm=m_f_norm, f_conv_b=m_f_conv_b, kv_norm=m_kv_norm,
                 k_norm=m_k_norm, b_norm=m_b_norm, b_q_norm=m_b_q_norm, b_sinks=m_b_sinks, a_norm=m_a_norm)
    rep_v = dict(a_w_s=v_a_w_s, a_b_s=v_a_b_s, f_norm=v_f_norm, f_conv_b=v_f_conv_b, kv_norm=v_kv_norm,
                 k_norm=v_k_norm, b_norm=v_b_norm, b_q_norm=v_b_q_norm, b_sinks=v_b_sinks, a_norm=v_a_norm)
    keys = rep + ["a_norm"]
    parts = ex_rep.results + [a_norm_parts]
    as2d = lambda a, p: a.reshape(p.shape[1:])
    rep_outs = _adamw_summed(parts, [as2d(rep_w[k], p) for k, p in zip(keys, parts)],
                             [as2d(rep_m[k], p) for k, p in zip(keys, parts)],
                             [as2d(rep_v[k], p) for k, p in zip(keys, parts)], "adamw_replicated")
    for j, key in enumerate(keys):
        res[key] = [o_.reshape(rep_w[key].shape) for o_ in rep_outs[j]]

    order = ["a_norm", "a_w_in", "a_v_norm", "a_w_s", "a_b_s", "a_w_out", "f_norm", "f_w_in", "f_conv_w", "f_conv_b",
             "f_w_out", "kv_norm", "w_kv", "k_norm", "b_norm", "b_w_q", "b_q_norm", "b_sinks", "b_w_o"]
    outs = [loss, grad_x[None]]
    for j in range(4):
        outs += [res[k][j] for k in order]
    return tuple(outs)
```

```python
import jax
import jax.numpy as jnp
from jax import lax
from jax.experimental import pallas as pl
from jax.experimental.pallas import tpu as pltpu

F32 = jnp.float32
BF16 = jnp.bfloat16
EPS = 1e-6
D_MODEL = 1024
CHUNK = 128
N_GROUPS = 8
N_SHARDS = 8
HEAD_DIM = 64
N_Q_HEADS = 16
N_KV_HEADS = 4
D_FF = 2816
FF_SHARD = 2 * D_FF // N_SHARDS
LANES = 128
NEG_BIG = -1e30
ADAM_LR = 0.001
ADAM_B1 = 0.9
ADAM_B2 = 0.999
ADAM_EPS = 1e-08
ADAM_WD = 0.01
ADAM_STEP = 10
VMEM_LIMIT_BYTES = 56 * 1024 * 1024
MESH = pl.DeviceIdType.MESH

NN = (((1,), (0,)), ((), ()))
NT = (((1,), (1,)), ((), ()))
TN = (((0,), (0,)), ((), ()))
SLOPES = tuple(2.0 ** (-8.0 * (h + 1) / N_Q_HEADS) for h in range(N_Q_HEADS))


def _params(sem=None):
    return pltpu.CompilerParams(dimension_semantics=sem, vmem_limit_bytes=VMEM_LIMIT_BYTES)


def _dot(a, b, dims=NN):
    return lax.dot_general(a, b, dims, preferred_element_type=F32)


def _sigmoid(x):
    return 1.0 / (1.0 + jnp.exp(-x))


def _gelu_parts(z):
    cdf = 0.5 * (1.0 + lax.erf(z * (2.0 ** -0.5)))
    pdf = jnp.exp(-0.5 * z * z) * 0.3989422804014327
    return cdf, pdf


def _coords():
    return lax.axis_index("x"), lax.axis_index("y"), lax.axis_index("c")


class _Gather:
    def __init__(self, srcs):
        self.srcs = list(srcs)
        n = len(self.srcs)
        self.out_shapes = [jax.ShapeDtypeStruct((N_SHARDS,) + s.shape, s.dtype) for s in self.srcs]
        self.sems = [pltpu.SemaphoreType.DMA((n, 7)), pltpu.SemaphoreType.DMA((n, 7)), pltpu.SemaphoreType.DMA((n,))]

    def _plan(self, src, dst, sems):
        send_sems, recv_sems, local_sems = sems
        x, y, c = _coords()
        me, sibling = (x, y, c), (x, y, 1 - c)
        chips = [(1 - x, y), (x, 1 - y), (1 - x, 1 - y)]
        n = len(src)

        def rows(e, dev):
            return dst[e].at[4 * dev[0] + 2 * dev[1] + dev[2]]

        def copy(e, slot, block, to, from_own=False):
            return pltpu.make_async_remote_copy(
                src_ref=src[e] if from_own else rows(e, block), dst_ref=rows(e, block),
                send_sem=send_sems.at[e, slot], recv_sem=recv_sems.at[e, slot], device_id=to, device_id_type=MESH)

        mine = [pltpu.make_async_copy(src[e], rows(e, me), local_sems.at[e]) for e in range(n)]
        first = []
        for e in range(n):
            first.append(copy(e, 0, me, sibling, from_own=True))
            first += [copy(e, 1 + j, me, (*chip, c), from_own=True) for j, chip in enumerate(chips)]
        return n, me, sibling, chips, c, copy, mine, first

    def start(self, src, dst, sems):
        _, _, _, _, _, _, mine, first = self._plan(src, dst, sems)
        for cp in mine + first:
            cp.start()

    def finish(self, src, dst, sems):
        n, me, sibling, chips, c, copy, mine, first = self._plan(src, dst, sems)
        passed = []
        for j, chip in enumerate(chips):
            for e in range(n):
                copy(e, 1 + j, (*chip, c), me).wait_recv()
                cp = copy(e, 4 + j, (*chip, c), sibling)
                cp.start()
                passed.append(cp)
        for e in range(n):
            copy(e, 0, sibling, me).wait_recv()
            for j, chip in enumerate(chips):
                copy(e, 4 + j, (*chip, 1 - c), me).wait_recv()
        for cp in first + passed:
            cp.wait_send()
        for cp in mine:
            cp.wait()


class _ToSibling:
    def __init__(self, grads):
        self.srcs = list(grads)
        n = len(self.srcs)
        self.out_shapes = [jax.ShapeDtypeStruct((4,) + g.shape[1:], g.dtype) for g in self.srcs]
        self.sems = [pltpu.SemaphoreType.DMA((n, 4)), pltpu.SemaphoreType.DMA((n, 4))]

    def _copies(self, src, dst, sems):
        send_sems, recv_sems = sems
        x, y, c = _coords()
        return [
            pltpu.make_async_remote_copy(
                src_ref=src[i].at[2 * q + (1 - c)], dst_ref=dst[i].at[q], send_sem=send_sems.at[i, q],
                recv_sem=recv_sems.at[i, q], device_id=(x, y, 1 - c), device_id_type=MESH)
            for i in range(len(src)) for q in range(4)]

    def start(self, src, dst, sems):
        for cp in self._copies(src, dst, sems):
            cp.start()

    def finish(self, src, dst, sems):
        for cp in self._copies(src, dst, sems):
            cp.wait()


class _ToChips:
    def __init__(self, psums, rows=None):
        self.srcs = list(psums)
        n = len(self.srcs)
        self.rows = rows
        self.out_shapes = [
            jax.ShapeDtypeStruct((3, p.shape[1] if rows is None else rows[1]) + p.shape[2:], p.dtype)
            for p in self.srcs]
        self.sems = [pltpu.SemaphoreType.DMA((n, 3)), pltpu.SemaphoreType.DMA((n, 3))]

    def _copies(self, src, dst, sems):
        send_sems, recv_sems = sems
        x, y, c = _coords()
        peers = [(x, 1 - y), (1 - x, y), (1 - x, 1 - y)]

        def part(i, q):
            if self.rows is None:
                return src[i].at[q]
            return src[i].at[q, pl.ds(self.rows[0], self.rows[1])]

        return [
            pltpu.make_async_remote_copy(
                src_ref=part(i, 2 * px + py), dst_ref=dst[i].at[r], send_sem=send_sems.at[i, r],
                recv_sem=recv_sems.at[i, r], device_id=(px, py, c), device_id_type=MESH)
            for i in range(len(src)) for r, (px, py) in enumerate(peers)]

    def start(self, src, dst, sems):
        for cp in self._copies(src, dst, sems):
            cp.start()

    def finish(self, src, dst, sems):
        for cp in self._copies(src, dst, sems):
            cp.wait()


class _ToOwners:
    def __init__(self, grads):
        self.srcs = list(grads)
        n = len(self.srcs)
        self.out_shapes = [jax.ShapeDtypeStruct(g.shape, g.dtype) for g in self.srcs]
        self.sems = [pltpu.SemaphoreType.DMA((n, 7)), pltpu.SemaphoreType.DMA((n, 7)), pltpu.SemaphoreType.DMA((n,))]

    def _copies(self, src, dst, sems):
        send_sems, recv_sems, local_sems = sems
        x, y, c = _coords()
        me = 4 * x + 2 * y + c
        copies = [pltpu.make_async_copy(src[i].at[me], dst[i].at[me], local_sems.at[i]) for i in range(len(src))]
        for i in range(len(src)):
            for rel in range(1, N_SHARDS):
                px = x ^ (rel >> 2) if rel >> 2 else x
                py = y ^ ((rel >> 1) & 1) if (rel >> 1) & 1 else y
                pc = c ^ (rel & 1) if rel & 1 else c
                copies.append(pltpu.make_async_remote_copy(
                    src_ref=src[i].at[4 * px + 2 * py + pc], dst_ref=dst[i].at[me], send_sem=send_sems.at[i, rel - 1],
                    recv_sem=recv_sems.at[i, rel - 1], device_id=(px, py, pc), device_id_type=MESH))
        return copies

    def start(self, src, dst, sems):
        for cp in self._copies(src, dst, sems):
            cp.start()

    def finish(self, src, dst, sems):
        for cp in self._copies(src, dst, sems):
            cp.wait()


class _Together:
    def __init__(self, parts):
        self.parts = list(parts)
        self.srcs = [s for p in self.parts for s in p.srcs]
        self.out_shapes = [s for p in self.parts for s in p.out_shapes]
        self.sems = [s for p in self.parts for s in p.sems]

    def _split(self, src, dst, sems):
        a = b = c = 0
        for p in self.parts:
            na, nc = len(p.srcs), len(p.sems)
            yield p, src[a:a + na], dst[b:b + na], sems[c:c + nc]
            a, b, c = a + na, b + na, c + nc

    def start(self, src, dst, sems):
        for p, s, d, m in self._split(src, dst, sems):
            p.start(s, d, m)

    def finish(self, src, dst, sems):
        for p, s, d, m in self._split(src, dst, sems):
            p.finish(s, d, m)

    def spread(self):
        b = 0
        for p in self.parts:
            p.results = self.results[b:b + len(p.srcs)]
            b += len(p.srcs)


def _call(body, args, *, grid, in_specs, out_specs, out_shape, name, scratch=(), sem=None, carry=None):
    out_shape, out_specs = list(out_shape), list(out_specs)
    if carry is None:
        return pl.pallas_call(
            body, grid=grid, in_specs=list(in_specs), out_specs=out_specs, out_shape=out_shape,
            scratch_shapes=list(scratch), name=name, compiler_params=_params(sem))(*args)
    n_in, n_out, n_scr, n_c = len(args), len(out_shape), len(scratch), len(carry.srcs)
    steps = tuple(grid)

    def carried(*refs):
        ins, rest = refs[:n_in], refs[n_in:]
        c_src, rest = rest[:n_c], rest[n_c:]
        outs, rest = rest[:n_out], rest[n_out:]
        c_dst, rest = rest[:n_c], rest[n_c:]
        scr, sems = rest[:n_scr], rest[n_scr:]
        first = pl.program_id(0) == 0
        last = pl.program_id(0) == steps[0] - 1
        for ax in range(1, len(steps)):
            first = first & (pl.program_id(ax) == 0)
            last = last & (pl.program_id(ax) == steps[ax] - 1)

        @pl.when(first)
        def _():
            carry.start(c_src, c_dst, sems)

        body(*ins, *outs, *scr)

        @pl.when(last)
        def _():
            carry.finish(c_src, c_dst, sems)

    hbm = pl.BlockSpec(memory_space=pl.ANY)
    res = pl.pallas_call(
        carried, grid=grid, in_specs=list(in_specs) + [hbm] * n_c, out_specs=out_specs + [hbm] * n_c,
        out_shape=out_shape + carry.out_shapes, scratch_shapes=list(scratch) + carry.sems, name=name,
        compiler_params=_params(("arbitrary",) * len(steps)))(*args, *carry.srcs)
    carry.results = list(res[n_out:])
    return list(res[:n_out])


def _exchange_alone(ex, name):
    n = len(ex.srcs)

    def body(*refs):
        src, dst, sems = refs[:n], refs[n:2 * n], refs[2 * n:]
        ex.start(src, dst, sems)
        ex.finish(src, dst, sems)

    hbm = pl.BlockSpec(memory_space=pl.ANY)
    res = pl.pallas_call(body, in_specs=[hbm] * n, out_specs=[hbm] * n, out_shape=ex.out_shapes,
                         scratch_shapes=ex.sems, name=name)(*ex.srcs)
    ex.results = list(res)
    return ex.results


def _rms_fwd(x, gains, name, tm=512, carry=None):
    t, d = x.shape
    n = len(gains)

    def body(*refs):
        x_ref, g_refs, h_refs = refs[0], refs[1:1 + n], refs[1 + n:]
        xf = x_ref[...]
        xhat = xf * lax.rsqrt(jnp.mean(xf * xf, axis=-1, keepdims=True) + EPS)
        for g_ref, h_ref in zip(g_refs, h_refs):
            h_ref[...] = (xhat * g_ref[...]).astype(BF16)

    row = pl.BlockSpec((tm, d), lambda i: (i, 0))
    vec = pl.BlockSpec((1, d), lambda i: (0, 0))
    return _call(body, [x, *gains], grid=(t // tm,), in_specs=[row] + [vec] * n, out_specs=[row] * n,
                 out_shape=[jax.ShapeDtypeStruct((t, d), BF16)] * n, name=name, carry=carry)


def _rms_bwd(x, gains, dhs, dres, name, tm=256, carry=None, through=None):
    t, d = x.shape
    n = len(gains)
    n_w = 0 if through is None else n

    def body(*refs):
        x_ref, dres_ref = refs[0], refs[1]
        g_refs, dh_refs, w_refs = refs[2:2 + n], refs[2 + n:2 + 2 * n], refs[2 + 2 * n:2 + 2 * n + n_w]
        dx_ref, dg_ref = refs[2 + 2 * n + n_w], refs[3 + 2 * n + n_w]
        i = pl.program_id(0)

        @pl.when(i == 0)
        def _():
            dg_ref[...] = jnp.zeros_like(dg_ref)

        xf = x_ref[...]
        r = lax.rsqrt(jnp.mean(xf * xf, axis=-1, keepdims=True) + EPS)
        xhat = xf * r
        dx = dres_ref[...]
        for j in range(n):
            dh = dh_refs[j][...]
            if n_w:
                dh = _dot(dh.astype(BF16), w_refs[j][...], NT)
            dg_ref[j:j + 1, :] += jnp.sum(dh * xhat, axis=0, keepdims=True)
            gy = dh * g_refs[j][...]
            dx = dx + r * (gy - xhat * jnp.mean(gy * xhat, axis=-1, keepdims=True))
        dx_ref[...] = dx

    row = pl.BlockSpec((tm, d), lambda i: (i, 0))
    vec = pl.BlockSpec((1, d), lambda i: (0, 0))
    dh_rows = [pl.BlockSpec((tm, dh.shape[1]), lambda i: (i, 0)) for dh in dhs]
    w_full = [] if through is None else [pl.BlockSpec(w.shape, lambda i: (0, 0)) for w in through]
    return _call(body, [x, dres, *gains, *dhs, *(through or [])], grid=(t // tm,),
                 in_specs=[row, row] + [vec] * n + dh_rows + w_full,
                 out_specs=[row, pl.BlockSpec((8, d), lambda i: (0, 0))],
                 out_shape=[jax.ShapeDtypeStruct((t, d), F32), jax.ShapeDtypeStruct((8, d), F32)],
                 name=name, sem=("arbitrary",), carry=carry)


def _mm(a, b, a_spec, b_spec, o_spec, out_shape, grid, dims, name, res=None, res_spec=None, carry=None):
    nk = grid[2]
    acc_shape = tuple(s for s in o_spec.block_shape if s is not None)

    def body(*refs):
        a_ref, b_ref = refs[0], refs[1]
        r_ref = refs[2] if res is not None else None
        o_ref = refs[3] if res is not None else refs[2]
        p = _dot(a_ref[...].astype(BF16), b_ref[...].astype(BF16), dims)
        if nk == 1:
            if res is not None:
                p = p + r_ref[...]
            o_ref[...] = p.astype(o_ref.dtype)
            return
        acc_ref = refs[-1]
        k = pl.program_id(2)

        @pl.when(k == 0)
        def _():
            acc_ref[...] = p

        @pl.when(k > 0)
        def _():
            acc_ref[...] += p

        @pl.when(k == nk - 1)
        def _():
            out = acc_ref[...]
            if res is not None:
                out = out + r_ref[...]
            o_ref[...] = out.astype(o_ref.dtype)

    ins = [a, b] + ([res] if res is not None else [])
    specs = [a_spec, b_spec] + ([res_spec] if res is not None else [])
    return _call(body, ins, grid=grid, in_specs=specs, out_specs=[o_spec], out_shape=[out_shape],
                 scratch=[pltpu.VMEM(acc_shape, F32)] if nk > 1 else [], name=name,
                 sem=("parallel", "parallel", "arbitrary"), carry=carry)[0]


def _mm_rows(a, w, out_dtype, name, trans_w=False, res=None, tm=512, carry=None):
    t, k = a.shape
    n = w.shape[0] if trans_w else w.shape[1]
    return _mm(
        a, w, pl.BlockSpec((tm, k), lambda i, j, kk: (i, 0)), pl.BlockSpec(w.shape, lambda i, j, kk: (0, 0)),
        pl.BlockSpec((tm, n), lambda i, j, kk: (i, 0)), jax.ShapeDtypeStruct((t, n), out_dtype), (t // tm, 1, 1),
        NT if trans_w else NN, name, res=res,
        res_spec=None if res is None else pl.BlockSpec((tm, n), lambda i, j, kk: (i, 0)), carry=carry)


def _mm_wgrad(a, b, name, carry=None):
    t, m = a.shape
    n = b.shape[1]
    tn = n // (4 if b.dtype == F32 else 2)
    return _mm(
        a, b, pl.BlockSpec((t, m), lambda i, j, kk: (0, 0)), pl.BlockSpec((t, tn), lambda i, j, kk: (0, j)),
        pl.BlockSpec((m, tn), lambda i, j, kk: (0, j)), jax.ShapeDtypeStruct((m, n), F32), (1, n // tn, 1), TN, name,
        carry=carry)


def _sgu_fwd(x0, h1, w_in, g_v, w_c, b_sb, w_out, tm=256, carry=None):
    t, d = x0.shape
    nsub = w_in.shape[2]

    def body(x_ref, h_ref, win_ref, gv_ref, wc_ref, bsb_ref, wout_ref, zpre_ref, x1_ref, u_s, v_s, vn_s, y_s):
        h = h_ref[...]
        for k in range(N_SHARDS):
            zk = _dot(h, win_ref[k])
            zpre_ref[:, k * nsub:(k + 1) * nsub] = zk
            cdf, _ = _gelu_parts(zk)
            if k < N_SHARDS // 2:
                u_s[:, k * nsub:(k + 1) * nsub] = zk * cdf
            else:
                v_s[:, (k - 4) * nsub:(k - 3) * nsub] = zk * cdf
        v = v_s[...]
        rv = lax.rsqrt(jnp.mean(v * v, axis=-1, keepdims=True) + EPS)
        vn_s[...] = (v * rv * gv_ref[...]).astype(BF16)
        for ci in range(tm // CHUNK):
            rows = slice(ci * CHUNK, (ci + 1) * CHUNK)
            for g in range(N_GROUPS):
                cols = slice(g * LANES, (g + 1) * LANES)
                sv = _dot(wc_ref[g], vn_s[rows, cols]) + bsb_ref[g]
                y_s[rows, cols] = (u_s[rows, cols] * sv).astype(BF16)
        x1_ref[...] = x_ref[...] + _dot(y_s[...], wout_ref[...])

    row = pl.BlockSpec((tm, d), lambda i: (i, 0))
    full = lambda a: pl.BlockSpec(a.shape, lambda i: (0,) * a.ndim)
    return _call(
        body, [x0, h1, w_in, g_v, w_c, b_sb, w_out], grid=(t // tm,),
        in_specs=[row, row, full(w_in), full(g_v), full(w_c), full(b_sb), full(w_out)],
        out_specs=[pl.BlockSpec((tm, 2 * d), lambda i: (i, 0)), row],
        out_shape=[jax.ShapeDtypeStruct((t, 2 * d), F32), jax.ShapeDtypeStruct((t, d), F32)],
        scratch=[pltpu.VMEM((tm, d), F32), pltpu.VMEM((tm, d), F32), pltpu.VMEM((tm, d), BF16),
                 pltpu.VMEM((tm, d), BF16)],
        name="sgu_fwd", carry=carry)


def _sgu_bwd(dx1, zpre, w_out, g_v, w_c, w_ct, b_sb, tm=256, carry=None):
    t, d = dx1.shape

    def body(dx_ref, zpre_ref, wout_ref, gv_ref, wc_ref, wct_ref, bsb_ref,
             dz_ref, y_ref, dwc_ref, dbs_ref, dgv_ref, u_s, vn_s, dy_s, du_s, dvn_s):
        i = pl.program_id(0)

        @pl.when(i == 0)
        def _():
            dwc_ref[...] = jnp.zeros_like(dwc_ref)
            dbs_ref[...] = jnp.zeros_like(dbs_ref)
            dgv_ref[...] = jnp.zeros_like(dgv_ref)

        dy_s[...] = _dot(dx_ref[...].astype(BF16), wout_ref[...], NT)
        zu = zpre_ref[:, :d]
        zv = zpre_ref[:, d:]
        cdf_u, pdf_u = _gelu_parts(zu)
        cdf_v, pdf_v = _gelu_parts(zv)
        u_s[...] = zu * cdf_u
        v = zv * cdf_v
        rv = lax.rsqrt(jnp.mean(v * v, axis=-1, keepdims=True) + EPS)
        vhat = v * rv
        gv = gv_ref[...]
        vn_s[...] = (vhat * gv).astype(BF16)
        for ci in range(tm // CHUNK):
            rows = slice(ci * CHUNK, (ci + 1) * CHUNK)
            for g in range(N_GROUPS):
                cols = slice(g * LANES, (g + 1) * LANES)
                vnb = vn_s[rows, cols]
                sv = _dot(wc_ref[g], vnb) + bsb_ref[g]
                dyb = dy_s[rows, cols]
                ub = u_s[rows, cols]
                dsv = dyb * ub
                du_s[rows, cols] = dyb * sv
                y_ref[rows, cols] = (ub * sv).astype(BF16)
                dsvb = dsv.astype(BF16)
                dbs_ref[g] += dsv
                dwc_ref[g] += _dot(dsvb, vnb, NT)
                dvn_s[rows, cols] = _dot(wct_ref[g], dsvb)
        dvn = dvn_s[...]
        dgv_ref[0:1, :] += jnp.sum(dvn * vhat, axis=0, keepdims=True)
        gy = dvn * gv
        dv = rv * (gy - vhat * jnp.mean(gy * vhat, axis=-1, keepdims=True))
        dz_ref[:, :d] = (du_s[...] * (cdf_u + zu * pdf_u)).astype(BF16)
        dz_ref[:, d:] = (dv * (cdf_v + zv * pdf_v)).astype(BF16)

        @pl.when(i == t // tm - 1)
        def _():
            tri = (lax.broadcasted_iota(jnp.int32, (CHUNK, CHUNK), 0)
                   >= lax.broadcasted_iota(jnp.int32, (CHUNK, CHUNK), 1))
            for g in range(N_GROUPS):
                dwc_ref[g] = jnp.where(tri, dwc_ref[g], 0.0)
                dbs_ref[g] = jnp.broadcast_to(jnp.sum(dbs_ref[g], axis=1, keepdims=True), (CHUNK, CHUNK))

    row = pl.BlockSpec((tm, d), lambda i: (i, 0))
    row2 = pl.BlockSpec((tm, 2 * d), lambda i: (i, 0))
    full = lambda a: pl.BlockSpec(a.shape, lambda i: (0,) * a.ndim)
    grp = pl.BlockSpec((N_GROUPS, CHUNK, CHUNK), lambda i: (0, 0, 0))
    return _call(
        body, [dx1, zpre, w_out, g_v, w_c, w_ct, b_sb], grid=(t // tm,),
        in_specs=[row, row2, full(w_out), full(g_v), full(w_c), full(w_ct), full(b_sb)],
        out_specs=[row2, row, grp, grp, pl.BlockSpec((8, d), lambda i: (0, 0))],
        out_shape=[jax.ShapeDtypeStruct((t, 2 * d), BF16), jax.ShapeDtypeStruct((t, d), BF16),
                   jax.ShapeDtypeStruct((N_GROUPS, CHUNK, CHUNK), F32),
                   jax.ShapeDtypeStruct((N_GROUPS, CHUNK, CHUNK), F32), jax.ShapeDtypeStruct((8, d), F32)],
        scratch=[pltpu.VMEM((tm, d), F32), pltpu.VMEM((tm, d), BF16), pltpu.VMEM((tm, d), F32),
                 pltpu.VMEM((tm, d), F32), pltpu.VMEM((tm, d), F32)],
        name="sgu_bwd", sem=("arbitrary",), carry=carry)


ROW_CHUNK = 256
HALO = 16


def _ffn_fwd(x, g, w_in, cw, cb, w_out, layer, tm=512, carry=None, next_gains=(), loss_target=None):
    t, d = x.shape
    nc = N_SHARDS // 2
    n_gains = len(next_gains)
    with_loss = loss_target is not None

    def body(x_ref, xp_ref, g_ref, wg_ref, wu_ref, cwg_ref, cbg_ref, cwu_ref, cbu_ref, wout_ref, *rest):
        extra_in, rest = rest[:n_gains + with_loss], rest[n_gains + with_loss:]
        o_ref, hf_ref, a_ref, pre_ref = rest[:4]
        extra_out, hw_s = rest[4:-1], rest[-1]
        i, c = pl.program_id(0), pl.program_id(1)

        @pl.when(c == 0)
        def _():
            keep = jnp.where(i == 0, 0.0, 1.0)
            xw = jnp.concatenate([xp_ref[...] * keep, x_ref[...]], axis=0)
            xhat = xw * lax.rsqrt(jnp.mean(xw * xw, axis=-1, keepdims=True) + EPS)
            hw_s[...] = (xhat * g_ref[...]).astype(BF16)
            hf_ref[...] = hw_s[HALO:, :]
            o_ref[...] = x_ref[...]

        hw = hw_s[...]
        pre = []
        for j, (w_ref, cw_ref, cb_ref) in enumerate(((wg_ref, cwg_ref, cbg_ref), (wu_ref, cwu_ref, cbu_ref))):
            ab = _dot(hw, w_ref[...]).astype(BF16)
            a_ref[j] = ab[HALO:]
            win = ab.astype(F32)
            cw_v = cw_ref[...]
            pre.append(cw_v[2:3, :] * win[HALO:] + cw_v[1:2, :] * pltpu.roll(win, 1, 0)[HALO:]
                       + cw_v[0:1, :] * pltpu.roll(win, 2, 0)[HALO:] + cb_ref[...])
            pre_ref[j] = pre[j]
        act = (pre[0] * _sigmoid(pre[0]) * pre[1]).astype(BF16)
        o_ref[...] += _dot(act, wout_ref[...])

        if with_loss:
            @pl.when((i == 0) & (c == 0))
            def _():
                extra_out[-1][...] = jnp.zeros_like(extra_out[-1])

        @pl.when(c == nc - 1)
        def _():
            xn = o_ref[...]
            if n_gains:
                xhat = xn * lax.rsqrt(jnp.mean(xn * xn, axis=-1, keepdims=True) + EPS)
                for k in range(n_gains):
                    extra_out[k][...] = (xhat * extra_in[k][...]).astype(BF16)
            if with_loss:
                err = xn - extra_in[-1][...]
                extra_out[-2][...] = err * (1.0 / d)
                part = jnp.sum(jnp.sum(err * err, axis=0, keepdims=True), axis=1, keepdims=True)
                extra_out[-1][...] += jnp.broadcast_to(0.5 / d * part, extra_out[-1].shape)

    row = pl.BlockSpec((tm, d), lambda i, c: (i, 0))
    vec = pl.BlockSpec((1, d), lambda i, c: (0, 0))
    shard = lambda rows, up: pl.BlockSpec((None, rows, FF_SHARD), lambda i, c: (c + up * nc, 0, 0))
    pair = pl.BlockSpec((2, None, tm, FF_SHARD), lambda i, c: (0, c, i, 0))
    lanes = pl.BlockSpec((8, LANES), lambda i, c: (0, 0))
    outs = _call(
        body, [x, x, g, w_in, w_in, cw, cb, cw, cb, w_out, *next_gains] + ([loss_target] if with_loss else []),
        grid=(t // tm, nc),
        in_specs=[row, pl.BlockSpec((HALO, d), lambda i, c: (jnp.maximum(i * (tm // HALO) - 1, 0), 0)),
                  vec, shard(d, 0), shard(d, 1), shard(8, 0), shard(1, 0), shard(8, 1), shard(1, 1),
                  pl.BlockSpec((FF_SHARD, d), lambda i, c: (c, 0))] + [vec] * n_gains + [row] * with_loss,
        out_specs=[row, row, pair, pair] + [row] * n_gains + [row, lanes] * with_loss,
        out_shape=[jax.ShapeDtypeStruct((t, d), F32), jax.ShapeDtypeStruct((t, d), BF16),
                   jax.ShapeDtypeStruct((2, nc, t, FF_SHARD), BF16), jax.ShapeDtypeStruct((2, nc, t, FF_SHARD), F32)]
        + [jax.ShapeDtypeStruct((t, d), BF16)] * n_gains
        + [jax.ShapeDtypeStruct((t, d), F32), jax.ShapeDtypeStruct((8, LANES), F32)] * with_loss,
        scratch=[pltpu.VMEM((tm + HALO, d), BF16)], name=f"ffn{layer}_fwd", sem=("arbitrary", "arbitrary"), carry=carry)
    return (outs[0], outs[1], outs[2].reshape(N_SHARDS, t, FF_SHARD), outs[3]) + tuple(outs[4:])


def _ffn_bwd_act(pre, w_out, dxn, layer, tm=512, carry=None):
    t, d = dxn.shape
    nc = N_SHARDS // 2

    def body(pre_ref, wout_ref, dx_ref, dhu_ref, dw_ref, dcb_ref):
        i = pl.program_id(1)

        @pl.when(i == 0)
        def _():
            dw_ref[...] = jnp.zeros_like(dw_ref)
            dcb_ref[...] = jnp.zeros_like(dcb_ref)

        hg, hu = pre_ref[0], pre_ref[1]
        sg = _sigmoid(hg)
        sl = hg * sg
        dxb = dx_ref[...].astype(BF16)
        dact = _dot(dxb, wout_ref[...], NT)
        dw_ref[...] += _dot((sl * hu).astype(BF16), dxb, TN)
        d_up = dact * sl
        d_gate = dact * hu * (sg * (1.0 + hg * (1.0 - sg)))
        for j, dv in enumerate((d_gate, d_up)):
            dhu_ref[j] = dv.astype(BF16)
            dcb_ref[j, 0:1, :] += jnp.sum(dv, axis=0, keepdims=True)

    return _call(
        body, [pre, w_out, dxn], grid=(nc, t // tm),
        in_specs=[pl.BlockSpec((2, None, tm, FF_SHARD), lambda c, i: (0, c, i, 0)),
                  pl.BlockSpec((FF_SHARD, d), lambda c, i: (c, 0)), pl.BlockSpec((tm, d), lambda c, i: (i, 0))],
        out_specs=[pl.BlockSpec((None, 2, tm, FF_SHARD), lambda c, i: (c, 0, i, 0)),
                   pl.BlockSpec((FF_SHARD, d), lambda c, i: (c, 0)),
                   pl.BlockSpec((None, 2, 8, FF_SHARD), lambda c, i: (c, 0, 0, 0))],
        out_shape=[jax.ShapeDtypeStruct((nc, 2, t, FF_SHARD), BF16), jax.ShapeDtypeStruct((D_FF, d), F32),
                   jax.ShapeDtypeStruct((nc, 2, 8, FF_SHARD), F32)],
        name=f"ffn{layer}_bwd_act", sem=("parallel", "arbitrary"), carry=carry)


def _ffn_bwd_in(dhu, a, cw, w_in, layer, tm=1024, carry=None):
    nc, _, t, _ = dhu.shape
    d = D_MODEL
    tm = min(tm, t)
    last_blk = t // 16 - 1

    def body(dh_ref, nx_ref, a_ref, cw_ref, win_ref, da_ref, o_ref, dcw_ref):
        i, s = pl.program_id(0), pl.program_id(1)

        @pl.when(s == 0)
        def _():
            o_ref[...] = jnp.zeros_like(o_ref)

        @pl.when((s == 0) & (i == 0))
        def _():
            dcw_ref[...] = jnp.zeros_like(dcw_ref)

        keep = jnp.where(i == t // tm - 1, 0.0, 1.0)
        cw = cw_ref[...]
        sums = [None] * 3
        for r0 in range(0, tm, ROW_CHUNK):
            rows = slice(r0, r0 + ROW_CHUNK)
            if r0 + ROW_CHUNK == tm:
                win = jnp.concatenate([dh_ref[rows, :].astype(F32), nx_ref[...].astype(F32) * keep], axis=0)
            else:
                win = dh_ref[r0:r0 + ROW_CHUNK + HALO, :].astype(F32)
            n = ROW_CHUNK + HALO
            taps = (pltpu.roll(win, n - 2, 0)[:ROW_CHUNK],
                    pltpu.roll(win, n - 1, 0)[:ROW_CHUNK],
                    win[:ROW_CHUNK])
            da = (cw[0:1, :] * taps[0] + cw[1:2, :] * taps[1] + cw[2:3, :] * taps[2]).astype(BF16)
            da_ref[rows, :] = da
            o_ref[rows, :] += _dot(da, win_ref[...], NT)
            af = a_ref[rows, :].astype(F32)
            parts = [jnp.sum(taps[k] * af, axis=0, keepdims=True) for k in range(3)]
            sums = [p if q is None else q + p for q, p in zip(sums, parts)]
        for k in range(3):
            dcw_ref[pl.ds(s, 1), k:k + 1, :] += sums[k][None]

    return _call(
        body, [dhu, dhu, a, cw, w_in], grid=(t // tm, N_SHARDS),
        in_specs=[pl.BlockSpec((None, None, tm, FF_SHARD), lambda i, s: (s % nc, s // nc, i, 0)),
                  pl.BlockSpec((None, None, 16, FF_SHARD),
                               lambda i, s: (s % nc, s // nc, jnp.minimum((i + 1) * (tm // 16), last_blk), 0)),
                  pl.BlockSpec((None, tm, FF_SHARD), lambda i, s: (s, i, 0)),
                  pl.BlockSpec((None, 8, FF_SHARD), lambda i, s: (s, 0, 0)),
                  pl.BlockSpec((None, d, FF_SHARD), lambda i, s: (s, 0, 0))],
        out_specs=[pl.BlockSpec((None, tm, FF_SHARD), lambda i, s: (s, i, 0)),
                   pl.BlockSpec((tm, d), lambda i, s: (i, 0)),
                   pl.BlockSpec((N_SHARDS, 8, FF_SHARD), lambda i, s: (0, 0, 0))],
        out_shape=[jax.ShapeDtypeStruct((N_SHARDS, t, FF_SHARD), BF16), jax.ShapeDtypeStruct((t, d), F32),
                   jax.ShapeDtypeStruct((N_SHARDS, 8, FF_SHARD), F32)],
        name=f"ffn{layer}_bwd_in", sem=("arbitrary", "arbitrary"), carry=carry)


def _ffn_wgrad_in(hf, da, layer, carry=None):
    t, d = hf.shape
    return _mm(
        da, hf, pl.BlockSpec((None, t, FF_SHARD), lambda s, j, kk: (s, 0, 0)),
        pl.BlockSpec((t, d), lambda s, j, kk: (0, 0)),
        pl.BlockSpec((None, FF_SHARD, d), lambda s, j, kk: (s, 0, 0)),
        jax.ShapeDtypeStruct((N_SHARDS, FF_SHARD, d), F32), (N_SHARDS, 1, 1), TN, f"ffn{layer}_wgrad_in",
        carry=carry)


Q_PER_KV = N_Q_HEADS // N_KV_HEADS
GROUP_ROWS = Q_PER_KV * CHUNK


def _attn_masks(n):
    lane = lax.broadcasted_iota(jnp.int32, (CHUNK, LANES), 1)
    lo = lane < HEAD_DIM
    tq = lax.broadcasted_iota(jnp.int32, (GROUP_ROWS, 2 * CHUNK), 0) & (CHUNK - 1)
    jk = lax.broadcasted_iota(jnp.int32, (GROUP_ROWS, 2 * CHUNK), 1)
    dist = tq + CHUNK - jk
    mask = (dist >= 0) & (dist < CHUNK) & (jk >= jnp.where(n == 0, CHUNK, 0))
    return lo, mask, dist.astype(F32)


def _per_head_column(values):
    r = lax.broadcasted_iota(jnp.int32, (GROUP_ROWS, 1), 0)
    col = jnp.full((GROUP_ROWS, 1), values[Q_PER_KV - 1], F32)
    for j in range(Q_PER_KV - 2, -1, -1):
        col = jnp.where(r < (j + 1) * CHUNK, values[j], col)
    return col


def _half_sum(x, lo):
    s_lo = jnp.sum(jnp.where(lo, x, 0.0), axis=-1, keepdims=True)
    s_hi = jnp.sum(jnp.where(lo, 0.0, x), axis=-1, keepdims=True)
    return jnp.where(lo, s_lo, s_hi)


def _stack_heads(pairs, lo):
    zero = jnp.zeros_like(pairs[0])
    return jnp.concatenate([jnp.where(lo, pairs[0], zero), jnp.where(lo, zero, pairs[0]),
                            jnp.where(lo, pairs[1], zero), jnp.where(lo, zero, pairs[1])], axis=0)


def _unstack_heads(stacked, lo):
    return (jnp.where(lo, stacked[0:CHUNK], stacked[CHUNK:2 * CHUNK]),
            jnp.where(lo, stacked[2 * CHUNK:3 * CHUNK], stacked[3 * CHUNK:]))


def _attn_probs(qs, kn, mask, distf, slope_col, sink_col):
    s = _dot(qs, kn, NT) * (HEAD_DIM ** -0.5)
    s = jnp.where(mask, s - slope_col * distf, NEG_BIG)
    m = jnp.maximum(jnp.max(s, axis=-1, keepdims=True), sink_col)
    e = jnp.exp(s - m)
    den = jnp.sum(e, axis=-1, keepdims=True) + jnp.exp(sink_col - m)
    return e * (1.0 / den), m, den


def _attn_fwd(qraw, kvd, gq, gk, sinks, carry=None):
    t, d = qraw.shape
    nb = t // CHUNK

    def body(sink_ref, q_ref, cur_ref, prev_ref, gq_ref, gk_ref, o_ref):
        n = pl.program_id(0)
        lo, mask, distf = _attn_masks(n)
        gq_v, gk_v = gq_ref[...], gk_ref[...]
        for kvh in range(N_KV_HEADS):
            ks = slice(kvh * LANES, (kvh + 1) * LANES)
            vs = slice(4 * LANES + kvh * LANES, 4 * LANES + (kvh + 1) * LANES)
            kraw = jnp.concatenate([prev_ref[:, ks], cur_ref[:, ks]], axis=0)
            rk = lax.rsqrt(jnp.mean(kraw * kraw, axis=-1, keepdims=True) + EPS)
            kn = (kraw * rk * gk_v).astype(BF16)
            vv = jnp.concatenate([prev_ref[:, vs], cur_ref[:, vs]], axis=0).astype(BF16)
            qn = []
            for p in range(2):
                qp = q_ref[:, (2 * kvh + p) * LANES:(2 * kvh + p + 1) * LANES]
                r = lax.rsqrt(_half_sum(qp * qp, lo) * (1.0 / HEAD_DIM) + EPS)
                qn.append(qp * r * gq_v)
            heads = range(Q_PER_KV * kvh, Q_PER_KV * (kvh + 1))
            pf, _, _ = _attn_probs(_stack_heads(qn, lo).astype(BF16), kn, mask, distf,
                                   _per_head_column([SLOPES[h] for h in heads]),
                                   _per_head_column([sink_ref[h] for h in heads]))
            for p, o_pair in enumerate(_unstack_heads(_dot(pf.astype(BF16), vv), lo)):
                o_ref[:, (2 * kvh + p) * LANES:(2 * kvh + p + 1) * LANES] = o_pair.astype(BF16)

    blk = lambda f: pl.BlockSpec((CHUNK, d), f)
    vec = pl.BlockSpec((1, LANES), lambda n: (0, 0))
    return _call(
        body, [sinks, qraw, kvd, kvd, gq, gk], grid=(nb,),
        in_specs=[pl.BlockSpec(memory_space=pltpu.SMEM), blk(lambda n: (n, 0)), blk(lambda n: (n, 0)),
                  blk(lambda n: (jnp.maximum(n - 1, 0), 0)), vec, vec],
        out_specs=[blk(lambda n: (n, 0))], out_shape=[jax.ShapeDtypeStruct((t, d), BF16)],
        name="attn_fwd", carry=carry)[0]


def _attn_bwd(qraw, kvd, d_o, gq, gk, sinks, carry=None):
    t, d = qraw.shape
    nb = t // CHUNK

    def body(sink_ref, q_ref, cur_ref, prev_ref, do_ref, gq_ref, gk_ref,
             dq_ref, dkv_ref, dsink_ref, dgq_ref, dgk_ref, carry_s, pp_s, cp_s):
        n = pl.program_id(0)

        @pl.when(n == 0)
        def _():
            carry_s[...] = jnp.zeros_like(carry_s)
            dsink_ref[...] = jnp.zeros_like(dsink_ref)
            dgq_ref[...] = jnp.zeros_like(dgq_ref)
            dgk_ref[...] = jnp.zeros_like(dgk_ref)

        @pl.when(n < nb)
        def _():
            lo, mask, distf = _attn_masks(n)
            gq_v, gk_v = gq_ref[...], gk_ref[...]
            for kvh in range(N_KV_HEADS):
                ks = slice(kvh * LANES, (kvh + 1) * LANES)
                vs = slice(4 * LANES + kvh * LANES, 4 * LANES + (kvh + 1) * LANES)
                kraw = jnp.concatenate([prev_ref[:, ks], cur_ref[:, ks]], axis=0)
                rk = lax.rsqrt(jnp.mean(kraw * kraw, axis=-1, keepdims=True) + EPS)
                khat = kraw * rk
                kn = (khat * gk_v).astype(BF16)
                vv = jnp.concatenate([prev_ref[:, vs], cur_ref[:, vs]], axis=0).astype(BF16)
                cols = [slice((2 * kvh + p) * LANES, (2 * kvh + p + 1) * LANES) for p in range(2)]
                rq, qhat = [], []
                for p in range(2):
                    qp = q_ref[:, cols[p]]
                    rq.append(lax.rsqrt(_half_sum(qp * qp, lo) * (1.0 / HEAD_DIM) + EPS))
                    qhat.append(qp * rq[p])
                heads = range(Q_PER_KV * kvh, Q_PER_KV * (kvh + 1))
                qs = _stack_heads([qhat[p] * gq_v for p in range(2)], lo).astype(BF16)
                dos = _stack_heads([do_ref[:, cols[p]] for p in range(2)], lo)
                sink_col = _per_head_column([sink_ref[h] for h in heads])
                pf, m, den = _attn_probs(qs, kn, mask, distf, _per_head_column([SLOPES[h] for h in heads]), sink_col)
                dp = _dot(dos, vv, NT)
                delta = jnp.sum(pf * dp, axis=-1, keepdims=True)
                sink_delta = jnp.exp(sink_col - m) / den * delta
                for j, h in enumerate(heads):
                    dsink_ref[h:h + 1, :] -= jnp.broadcast_to(
                        jnp.sum(sink_delta[j * CHUNK:(j + 1) * CHUNK], axis=0, keepdims=True), (1, LANES))
                ds = (pf * (dp - delta) * (HEAD_DIM ** -0.5)).astype(BF16)
                dkn = _dot(ds, qs, TN)
                dvb = _dot(pf.astype(BF16), dos, TN)
                for p, dqn in enumerate(_unstack_heads(_dot(ds, kn), lo)):
                    dgq_ref[0:1, :] += jnp.sum(dqn * qhat[p], axis=0, keepdims=True)
                    gy = dqn * gq_v
                    mq = _half_sum(gy * qhat[p], lo) * (1.0 / HEAD_DIM)
                    dq_ref[:, cols[p]] = (rq[p] * (gy - qhat[p] * mq)).astype(BF16)
                dgk_ref[0:1, :] += jnp.sum(dkn * khat, axis=0, keepdims=True)
                gyk = dkn * gk_v
                dkraw = rk * (gyk - khat * jnp.mean(gyk * khat, axis=-1, keepdims=True))
                pp_s[:, ks] = dkraw[:CHUNK]
                cp_s[:, ks] = dkraw[CHUNK:]
                pp_s[:, vs] = dvb[:CHUNK]
                cp_s[:, vs] = dvb[CHUNK:]
            dkv_ref[...] = (carry_s[...] + pp_s[...]).astype(BF16)
            carry_s[...] = cp_s[...]

        @pl.when(n == nb)
        def _():
            dkv_ref[...] = carry_s[...].astype(BF16)

    blk = lambda f: pl.BlockSpec((CHUNK, d), f)
    vec = pl.BlockSpec((1, LANES), lambda n: (0, 0))
    cur = lambda n: (jnp.minimum(n, nb - 1), 0)
    prev = lambda n: (jnp.maximum(jnp.minimum(n, nb - 1) - 1, 0), 0)
    small = lambda r: pl.BlockSpec((r, LANES), lambda n: (0, 0))
    return _call(
        body, [sinks, qraw, kvd, kvd, d_o, gq, gk], grid=(nb + 1,),
        in_specs=[pl.BlockSpec(memory_space=pltpu.SMEM), blk(cur), blk(cur), blk(prev), blk(cur), vec, vec],
        out_specs=[blk(cur), blk(lambda n: (jnp.maximum(n - 1, 0), 0)), small(N_Q_HEADS), small(8), small(8)],
        out_shape=[jax.ShapeDtypeStruct((t, d), BF16), jax.ShapeDtypeStruct((t, d), BF16),
                   jax.ShapeDtypeStruct((N_Q_HEADS, LANES), F32), jax.ShapeDtypeStruct((8, LANES), F32),
                   jax.ShapeDtypeStruct((8, LANES), F32)],
        scratch=[pltpu.VMEM((CHUNK, d), F32)] * 3, name="attn_bwd", sem=("arbitrary",), carry=carry)


def _adamw_math(g, w, m, v):
    m = ADAM_B1 * m + (1.0 - ADAM_B1) * g
    v = ADAM_B2 * v + (1.0 - ADAM_B2) * (g * g)
    m_hat = m / (1.0 - ADAM_B1 ** ADAM_STEP)
    v_hat = v / (1.0 - ADAM_B2 ** ADAM_STEP)
    delta = -ADAM_LR * (m_hat / (jnp.sqrt(v_hat) + ADAM_EPS) + ADAM_WD * w)
    return delta, m, v


def _row_tile(r, cap=128):
    for tr in range(min(r, cap), 0, -1):
        if r % tr == 0 and (tr % 8 == 0 or tr == r):
            return tr
    return r


def _chip_sum(grad, recv, place, name, wire_dtype):
    _, r, c = grad.shape
    tr = _row_tile(r, 256)

    def body(pl_ref, g_ref, a_ref, p_ref):
        p_ref[...] = (g_ref[...] + a_ref[...]).astype(p_ref.dtype)

    return pl.pallas_call(
        body,
        grid_spec=pltpu.PrefetchScalarGridSpec(
            num_scalar_prefetch=1, grid=(4, r // tr),
            in_specs=[pl.BlockSpec((None, None, tr, c), lambda q, i, pr: (q, pr[1], i, 0)),
                      pl.BlockSpec((None, tr, c), lambda q, i, pr: (q, i, 0))],
            out_specs=pl.BlockSpec((None, tr, c), lambda q, i, pr: (q, i, 0))),
        out_shape=jax.ShapeDtypeStruct((4, r, c), wire_dtype), name=name, compiler_params=_params(),
    )(place, grad.reshape(4, 2, r, c), recv)


def _adamw_sharded(grad, recv, others, place, w, m, v, name, layer=None, fill=None):
    r, c = w.shape[-2:]
    tr = _row_tile(r)

    def body(pl_ref, g_ref, a_ref, oth_ref, w_ref, m_ref, v_ref, *rest):
        g_out, d_out, nm_out, nv_out = rest[-4:]
        g = g_ref[...] + a_ref[...]
        for k in range(3):
            g = g + oth_ref[k].astype(F32)
        delta, nm, nv = _adamw_math(g, w_ref[...], m_ref[...], v_ref[...])
        g_out[...] = g
        d_out[...] = delta
        nm_out[...] = nm
        nv_out[...] = nv

    if layer is None:
        row = pl.BlockSpec((tr, c), lambda i, pr: (i, 0))
    else:
        row = pl.BlockSpec((None, tr, c), lambda i, pr: (layer, i, 0))
    n_fill = 0 if fill is None else 4
    in_specs = [pl.BlockSpec((None, None, tr, c), lambda i, pr: (pr[0], pr[1], i, 0)),
                pl.BlockSpec((None, tr, c), lambda i, pr: (pr[0], i, 0)),
                pl.BlockSpec((3, tr, c), lambda i, pr: (0, i, 0)), row, row, row]
    in_specs += [pl.BlockSpec(memory_space=pl.ANY)] * n_fill
    return pl.pallas_call(
        body,
        grid_spec=pltpu.PrefetchScalarGridSpec(
            num_scalar_prefetch=1, grid=(r // tr,), in_specs=in_specs, out_specs=[row] * 4),
        out_shape=[jax.ShapeDtypeStruct(w.shape, F32)] * 4, name=name, compiler_params=_params(),
        input_output_aliases={7 + j: j for j in range(n_fill)},
    )(place, grad.reshape(4, 2, r, c), recv, others, w, m, v, *([] if fill is None else fill))


def _adamw_summed(parts, ws, ms, vs, name):
    n = len(parts)

    def body(*refs):
        p_refs, w_refs, m_refs, v_refs = refs[:n], refs[n:2 * n], refs[2 * n:3 * n], refs[3 * n:4 * n]
        o_refs = refs[4 * n:]
        for i in range(n):
            g = p_refs[i][0]
            for k in range(1, N_SHARDS):
                g = g + p_refs[i][k]
            delta, nm, nv = _adamw_math(g, w_refs[i][...], m_refs[i][...], v_refs[i][...])
            o_refs[4 * i][...] = g
            o_refs[4 * i + 1][...] = delta
            o_refs[4 * i + 2][...] = nm
            o_refs[4 * i + 3][...] = nv

    shapes = [jax.ShapeDtypeStruct(w.shape, F32) for w in ws for _ in range(4)]
    outs = pl.pallas_call(body, out_shape=shapes, name=name, compiler_params=_params())(*parts, *ws, *ms, *vs)
    return [outs[4 * i:4 * i + 4] for i in range(n)]


def _dup_heads(w):
    lead = w.shape[:-1]
    w4 = w.reshape(lead + (N_KV_HEADS, 1, HEAD_DIM))
    return jnp.broadcast_to(w4, lead + (N_KV_HEADS, 2, HEAD_DIM)).reshape(lead + (N_KV_HEADS * LANES,))


def _fold_heads(g):
    lead = g.shape[:-1]
    return g.reshape(lead + (N_KV_HEADS, 2, HEAD_DIM)).sum(axis=-2).reshape(lead + (N_KV_HEADS * HEAD_DIM,))


def kernel(x, a_norm, a_w_in, a_v_norm, a_w_s, a_b_s, a_w_out, f_norm, f_w_in, f_conv_w, f_conv_b, f_w_out, kv_norm, w_kv, k_norm, b_norm, b_w_q, b_q_norm, b_sinks, b_w_o, loss_target, m_a_norm, m_a_w_in, m_a_v_norm, m_a_w_s, m_a_b_s, m_a_w_out, m_f_norm, m_f_w_in, m_f_conv_w, m_f_conv_b, m_f_w_out, m_kv_norm, m_w_kv, m_k_norm, m_b_norm, m_b_w_q, m_b_q_norm, m_b_sinks, m_b_w_o, v_a_norm, v_a_w_in, v_a_v_norm, v_a_w_s, v_a_b_s, v_a_w_out, v_f_norm, v_f_w_in, v_f_conv_w, v_f_conv_b, v_f_w_out, v_kv_norm, v_w_kv, v_k_norm, v_b_norm, v_b_w_q, v_b_q_norm, v_b_sinks, v_b_w_o):
    d = D_MODEL
    xi, yi, ci = _coords()
    place = jnp.stack([2 * xi + yi, ci]).astype(jnp.int32)
    bf = lambda a: a.astype(BF16)
    row = lambda v_: v_.reshape(1, -1)
    x0, target = x[0], loss_target[0]
    t = x0.shape[0]
    res = {}

    red = {}

    def to_sibling(grads, wire=BF16):
        for k, g in grads.items():
            red[k] = dict(grad=g, wire=wire)
        ex = _ToSibling(list(grads.values()))
        ex.names = list(grads)
        return ex

    def to_chips(ex):
        for k, a in zip(ex.names, ex.results):
            red[k]["recv"] = a
            red[k]["psum"] = _chip_sum(red[k]["grad"], a, place, f"chip_sum_{k}", red[k]["wire"])
        nxt = _ToChips([red[k]["psum"] for k in ex.names])
        nxt.names = ex.names
        return nxt

    def landed(ex):
        for k, b in zip(ex.names, ex.results):
            red[k]["others"] = b

    def halves(ex, first_rows):
        parts = []
        for r0, nr in ((0, first_rows), (first_rows, ex.srcs[0].shape[1] - first_rows)):
            part = _ToChips(ex.srcs, rows=(r0, nr))
            part.names = ex.names
            parts.append(part)
        return parts

    def landed_halves(parts):
        for j, k in enumerate(parts[0].names):
            red[k]["others"] = jnp.concatenate([p.results[j] for p in parts], axis=1)

    def update(k, w, m, v, layer=None, fill=None):
        r = red[k]
        return _adamw_sharded(r["grad"], r["recv"], r["others"], place, w, m, v,
                              f"adamw_{k}", layer=layer, fill=fill)

    g_a_in, g_a_out, g_a_norm, g_a_v_norm, g_conv = _exchange_alone(
        _Gather([bf(a_w_in[0]), bf(a_w_out[0]), a_norm, a_v_norm, f_conv_w.reshape(6, FF_SHARD)]), "gather_first")
    a_norm_full, a_v_norm_full = g_a_norm.reshape(1, d), g_a_v_norm.reshape(1, d)
    conv_w = lax.reduce_precision(g_conv.reshape(N_SHARDS, 2, 3, FF_SHARD), 8, 7)
    cw = jnp.pad(jnp.transpose(conv_w, (1, 0, 2, 3)), ((0, 0), (0, 0), (0, 5), (0, 0)))
    w_a_in_flat = jnp.transpose(g_a_in, (1, 0, 2)).reshape(d, 2 * d)
    cb = f_conv_b.reshape(2, N_SHARDS, 1, FF_SHARD)
    tri = jnp.tril(jnp.ones((CHUNK, CHUNK), dtype=bool))
    w_causal = jnp.where(tri[None], a_w_s[0], 0.0).astype(BF16)
    w_causal_t = jnp.transpose(w_causal, (0, 2, 1))
    b_sb = jnp.broadcast_to(a_b_s[0][:, :, None], (N_GROUPS, CHUNK, CHUNK))
    w_a_out = g_a_out.reshape(d, d)
    gq = jnp.tile(b_q_norm.reshape(1, HEAD_DIM), (1, 2))
    gk = jnp.tile(k_norm.reshape(1, HEAD_DIM), (1, 2))
    sinks = b_sinks.reshape(N_Q_HEADS)

    (h1,) = _rms_fwd(x0, [a_norm_full], "a_norm_fwd")
    ex = _Gather([bf(f_w_in[0]), bf(f_w_out[0])])
    zpre, x1 = _sgu_fwd(x0, h1, g_a_in, a_v_norm_full, w_causal, b_sb, w_a_out, carry=ex)
    w_in0, w_out0 = ex.results[0], ex.results[1].reshape(D_FF, d)
    ex = _Gather([bf(w_kv), bf(b_w_q[0]), bf(b_w_o[0]), bf(f_w_in[1])])
    x2, hf0, a0, pre0, hk, hq = _ffn_fwd(x1, f_norm[0:1], w_in0, cw[0], cb[0], w_out0, 0, carry=ex,
                                         next_gains=[row(kv_norm), b_norm])
    kv_full = ex.results[0].reshape(d, 2 * N_KV_HEADS * HEAD_DIM)
    w_q, w_o = ex.results[1].reshape(d, d), ex.results[2].reshape(d, d)
    w_in1 = ex.results[3]
    half = N_KV_HEADS * HEAD_DIM
    w_kv_dup = jnp.concatenate([_dup_heads(kv_full[:, :half]), _dup_heads(kv_full[:, half:])], axis=1)
    kvd = _mm_rows(hk, w_kv_dup, F32, "kv_proj")
    qraw = _mm_rows(hq, w_q, F32, "q_proj")
    ex = _Gather([bf(f_w_out[1])])
    o = _attn_fwd(qraw, kvd, gq, gk, sinks, carry=ex)
    w_out1 = ex.results[0].reshape(D_FF, d)
    x3 = _mm_rows(o, w_o, F32, "o_proj", res=x2)
    _, hf1, a1, pre1, dy, loss_lanes = _ffn_fwd(x3, f_norm[1:2], w_in1, cw[1], cb[1], w_out1, 1, loss_target=target)
    loss = lax.psum(loss_lanes[0, 0], ("x", "y", "c"))

    dhu1, dw_out1, dcb1 = _ffn_bwd_act(pre1, w_out1, dy, 1)
    ex = to_sibling({"f_w_out1": dw_out1.reshape(N_SHARDS, D_FF // N_SHARDS, d)})
    da1, dhf1, dcw1 = _ffn_bwd_in(dhu1, a1, cw[1], w_in1, 1, carry=ex)
    ex = to_chips(ex)
    dw_in1 = _ffn_wgrad_in(hf1, da1, 1, carry=ex)
    landed(ex)
    ex = to_sibling({"f_w_in1": dw_in1})
    dx3, dgf1 = _rms_bwd(x3, [f_norm[1:2]], [dhf1], dy, "f1_norm_bwd", carry=ex)
    ex = to_chips(ex)
    d_o = _mm_rows(dx3, w_o, BF16, "o_proj_bwd", trans_w=True)
    dw_o = _mm_wgrad(o, dx3, "o_wgrad").reshape(N_SHARDS, d // N_SHARDS, d)
    dq, dkv, dsink, dgq, dgk = _attn_bwd(qraw, kvd, d_o, gq, gk, sinks, carry=ex)
    landed(ex)
    dw_q = _mm_wgrad(hq, dq, "q_wgrad").reshape(N_SHARDS, d // N_SHARDS, d)
    dw_kv_dup = _mm_wgrad(hk, dkv, "kv_wgrad")
    dw_kv = jnp.concatenate(
        [_fold_heads(dw_kv_dup[:, :4 * LANES]), _fold_heads(dw_kv_dup[:, 4 * LANES:])], axis=1
    ).reshape(N_SHARDS, d // N_SHARDS, 2 * N_KV_HEADS * HEAD_DIM)
    ex = to_sibling({"b_w_o": dw_o, "b_w_q": dw_q, "w_kv": dw_kv})
    dx2, dg2 = _rms_bwd(x2, [row(kv_norm), b_norm], [dkv, dq], dx3, "kvq_norm_bwd", tm=512, carry=ex,
                        through=[w_kv_dup, w_q])
    ex = to_chips(ex)
    dhu0, dw_out0, dcb0 = _ffn_bwd_act(pre0, w_out0, dx2, 0, carry=ex)
    landed(ex)
    ex = to_sibling({"f_w_out0": dw_out0.reshape(N_SHARDS, D_FF // N_SHARDS, d)})
    da0, dhf0, dcw0 = _ffn_bwd_in(dhu0, a0, cw[0], w_in0, 0, carry=ex)
    ex = to_chips(ex)
    dw_in0 = _ffn_wgrad_in(hf0, da0, 0, carry=ex)
    landed(ex)
    ex = to_sibling({"f_w_in0": dw_in0})
    dx1, dgf0 = _rms_bwd(x1, [f_norm[0:1]], [dhf0], dx2, "f0_norm_bwd", carry=ex)
    ex_lo, ex_hi = halves(to_chips(ex), 448)
    dz, y, dwc, dbs, dgv = _sgu_bwd(dx1, zpre, w_a_out, a_v_norm_full, w_causal, w_causal_t, b_sb, carry=ex_lo)
    dw_a_out = _mm_wgrad(y, dx1, "a_out_wgrad").reshape(N_SHARDS, d // N_SHARDS, d)
    nsub = g_a_in.shape[2]
    dw_a_in = _mm(
        h1, dz, pl.BlockSpec((t, d), lambda s, j, kk: (0, 0)), pl.BlockSpec((t, nsub), lambda s, j, kk: (0, s)),
        pl.BlockSpec((None, d, nsub), lambda s, j, kk: (s, 0, 0)), jax.ShapeDtypeStruct((N_SHARDS, d, nsub), F32),
        (N_SHARDS, 1, 1), TN, "a_in_wgrad", carry=ex_hi)
    landed_halves([ex_lo, ex_hi])

    def bias_grad(dcb):
        return jnp.transpose(dcb[:, :, 0, :], (1, 0, 2)).reshape(-1)

    g_conv_w = jnp.concatenate([dcw0[:, 0:3, :], dcw1[:, 0:3, :]], axis=1)
    g_a_v_norm = dgv[0].reshape(N_SHARDS, 1, LANES)
    rep = ["a_w_s", "a_b_s", "f_norm", "f_conv_b", "kv_norm", "k_norm", "b_norm", "b_q_norm", "b_sinks"]
    rep_g = dict(
        a_w_s=dwc.reshape(N_GROUPS * CHUNK, CHUNK), a_b_s=dbs[:, :, 0], f_norm=jnp.stack([dgf0[0], dgf1[0]]),
        f_conv_b=jnp.stack([bias_grad(dcb0), bias_grad(dcb1)]), kv_norm=dg2[0:1],
        k_norm=(dgk[0, :HEAD_DIM] + dgk[0, HEAD_DIM:])[None], b_norm=dg2[1:2],
        b_q_norm=(dgq[0, :HEAD_DIM] + dgq[0, HEAD_DIM:])[None], b_sinks=dsink[:, 0][None])
    ex_big = to_sibling({"a_w_out": dw_a_out, "a_w_in": dw_a_in})
    ex_small = to_sibling({"a_v_norm": g_a_v_norm, "f_conv_w": g_conv_w}, wire=F32)
    ex_rep = _Gather([rep_g[k] for k in rep])
    together = _Together([ex_big, ex_small, ex_rep])
    dh1 = _mm_rows(dz, w_a_in_flat, F32, "a_in_bwd", trans_w=True, carry=together)
    together.spread()
    ex_big, ex_small = to_chips(ex_big), to_chips(ex_small)
    together = _Together([ex_big, ex_small])
    grad_x, dg0 = _rms_bwd(x0, [a_norm_full], [dh1], dx1, "a_norm_bwd", carry=together)
    together.spread()
    landed(ex_big)
    landed(ex_small)
    (a_norm_parts,) = _exchange_alone(_ToOwners([dg0[0].reshape(N_SHARDS, 1, LANES)]), "a_norm_to_owners")

    res["f_w_out"] = update("f_w_out1", f_w_out, m_f_w_out, v_f_w_out, layer=1)
    w_in_t = [jnp.swapaxes(a_, 1, 2) for a_ in (f_w_in, m_f_w_in, v_f_w_in)]
    res["f_w_in"] = update("f_w_in1", *w_in_t, layer=1)
    res["b_w_o"] = update("b_w_o", b_w_o, m_b_w_o, v_b_w_o, layer=0)
    res["b_w_q"] = update("b_w_q", b_w_q, m_b_w_q, v_b_w_q, layer=0)
    res["w_kv"] = update("w_kv", w_kv, m_w_kv, v_w_kv)
    res["f_w_out"] = update("f_w_out0", f_w_out, m_f_w_out, v_f_w_out, layer=0, fill=res["f_w_out"])
    res["f_w_in"] = [jnp.swapaxes(o_, 1, 2) for o_ in update("f_w_in0", *w_in_t, layer=0, fill=res["f_w_in"])]
    res["a_w_out"] = update("a_w_out", a_w_out, m_a_w_out, v_a_w_out, layer=0)
    res["a_w_in"] = update("a_w_in", a_w_in, m_a_w_in, v_a_w_in, layer=0)
    res["a_v_norm"] = update("a_v_norm", a_v_norm, m_a_v_norm, v_a_v_norm)
    res["f_conv_w"] = [o_.reshape(f_conv_w.shape) for o_ in update(
        "f_conv_w", f_conv_w.reshape(6, FF_SHARD), m_f_conv_w.reshape(6, FF_SHARD), v_f_conv_w.reshape(6, FF_SHARD))]

    rep_w = dict(a_w_s=a_w_s, a_b_s=a_b_s, f_norm=f_norm, f_conv_b=f_conv_b, kv_norm=kv_norm, k_norm=k_norm,
                 b_norm=b_norm, b_q_norm=b_q_norm, b_sinks=b_sinks, a_norm=a_norm)
    rep_m = dict(a_w_s=m_a_w_s, a_b_s=m_a_b_s, f_norm=m_f_norm, f_conv_b=m_f_conv_b, kv_norm=m_kv_norm,
                 k_norm=m_k_norm, b_norm=m_b_norm, b_q_norm=m_b_q_norm, b_sinks=m_b_sinks, a_norm=m_a_norm)
    rep_v = dict(a_w_s=v_a_w_s, a_b_s=v_a_b_s, f_norm=v_f_norm, f_conv_b=v_f_conv_b, kv_norm=v_kv_norm,
                 k_norm=v_k_norm, b_norm=v_b_norm, b_q_norm=v_b_q_norm, b_sinks=v_b_sinks, a_norm=v_a_norm)
    keys = rep + ["a_norm"]
    parts = ex_rep.results + [a_norm_parts]
    as2d = lambda a, p: a.reshape(p.shape[1:])
    rep_outs = _adamw_summed(parts, [as2d(rep_w[k], p) for k, p in zip(keys, parts)],
                             [as2d(rep_m[k], p) for k, p in zip(keys, parts)],
                             [as2d(rep_v[k], p) for k, p in zip(keys, parts)], "adamw_replicated")
    for j, key in enumerate(keys):
        res[key] = [o_.reshape(rep_w[key].shape) for o_ in rep_outs[j]]

    order = ["a_norm", "a_w_in", "a_v_norm", "a_w_s", "a_b_s", "a_w_out", "f_norm", "f_w_in", "f_conv_w", "f_conv_b",
             "f_w_out", "kv_norm", "w_kv", "k_norm", "b_norm", "b_w_q", "b_q_norm", "b_sinks", "b_w_o"]
    outs = [loss, grad_x[None]]
    for j in range(4):
        outs += [res[k][j] for k in order]
    return tuple(outs)
```

```python
import jax
import jax.numpy as jnp
from jax import lax
from jax.experimental import pallas as pl
from jax.experimental.pallas import tpu as pltpu

F32 = jnp.float32
BF16 = jnp.bfloat16
EPS = 1e-6
D_MODEL = 1024
CHUNK = 128
N_GROUPS = 8
N_SHARDS = 8
HEAD_DIM = 64
N_Q_HEADS = 16
N_KV_HEADS = 4
D_FF = 2816
FF_SHARD = 2 * D_FF // N_SHARDS
LANES = 128
NEG_BIG = -1e30
ADAM_LR = 0.001
ADAM_B1 = 0.9
ADAM_B2 = 0.999
ADAM_EPS = 1e-08
ADAM_WD = 0.01
ADAM_STEP = 10
VMEM_LIMIT_BYTES = 56 * 1024 * 1024
MESH = pl.DeviceIdType.MESH

NN = (((1,), (0,)), ((), ()))
NT = (((1,), (1,)), ((), ()))
TN = (((0,), (0,)), ((), ()))
SLOPES = tuple(2.0 ** (-8.0 * (h + 1) / N_Q_HEADS) for h in range(N_Q_HEADS))


def _params(sem=None):
    return pltpu.CompilerParams(dimension_semantics=sem, vmem_limit_bytes=VMEM_LIMIT_BYTES)


def _dot(a, b, dims=NN):
    return lax.dot_general(a, b, dims, preferred_element_type=F32)


def _sigmoid(x):
    return 1.0 / (1.0 + jnp.exp(-x))


def _gelu_parts(z):
    cdf = 0.5 * (1.0 + lax.erf(z * (2.0 ** -0.5)))
    pdf = jnp.exp(-0.5 * z * z) * 0.3989422804014327
    return cdf, pdf


def _coords():
    return lax.axis_index("x"), lax.axis_index("y"), lax.axis_index("c")


class _Gather:
    def __init__(self, srcs):
        self.srcs = list(srcs)
        n = len(self.srcs)
        self.relayed = [s.shape[0] % 32 == 0 for s in self.srcs]
        self.out_shapes = [jax.ShapeDtypeStruct((N_SHARDS,) + s.shape, s.dtype) for s in self.srcs]
        self.sems = [pltpu.SemaphoreType.DMA((n, 9)), pltpu.SemaphoreType.DMA((n, 9)), pltpu.SemaphoreType.DMA((n,))]

    def _plan(self, src, dst, sems):
        send_sems, recv_sems, local_sems = sems
        x, y, c = _coords()
        n = len(src)

        def rows(e, dev, half=None):
            block = dst[e].at[4 * dev[0] + 2 * dev[1] + dev[2]]
            if half is None:
                return block
            nr = self.srcs[e].shape[0] // 2
            return block.at[pl.ds(half * nr, nr)]

        def copy(e, slot, block, to, half=None, from_own=False):
            return pltpu.make_async_remote_copy(
                src_ref=src[e] if from_own else rows(e, block, half), dst_ref=rows(e, block, half),
                send_sem=send_sems.at[e, slot], recv_sem=recv_sems.at[e, slot], device_id=to, device_id_type=MESH)

        return n, x, y, c, rows, copy, local_sems

    def start(self, src, dst, sems):
        n, x, y, c, rows, copy, local_sems = self._plan(src, dst, sems)
        me = (x, y, c)
        for e in range(n):
            pltpu.make_async_copy(src[e], rows(e, me), local_sems.at[e]).start()
            copy(e, 0, me, (x, y, 1 - c), from_own=True).start()
            copy(e, 1, me, (1 - x, y, c), from_own=True).start()
            copy(e, 2, me, (x, 1 - y, c), from_own=True).start()
            if not self.relayed[e]:
                copy(e, 3, me, (1 - x, 1 - y, c), from_own=True).start()

    def finish(self, src, dst, sems):
        n, x, y, c, rows, copy, local_sems = self._plan(src, dst, sems)
        me, sibling = (x, y, c), (x, y, 1 - c)
        over_x, over_y, diagonal = (1 - x, y, c), (x, 1 - y, c), (1 - x, 1 - y, c)
        sent = []

        def send(cp):
            cp.start()
            sent.append(cp)

        for slot, owner, onward, half in ((1, over_x, over_y, 0), (2, over_y, over_x, 1)):
            for e in range(n):
                copy(e, slot, owner, me).wait_recv()
                if self.relayed[e]:
                    send(copy(e, 3 + half, owner, onward, half=half))
                send(copy(e, 4 + slot, owner, sibling))
        for e in range(n):
            if self.relayed[e]:
                for half in (0, 1):
                    copy(e, 3 + half, diagonal, me, half=half).wait_recv()
                    send(copy(e, 7 + half, diagonal, sibling, half=half))
            else:
                copy(e, 3, diagonal, me).wait_recv()
                send(copy(e, 7, diagonal, sibling))
        for e in range(n):
            copy(e, 0, sibling, me).wait_recv()
            copy(e, 5, (1 - x, y, 1 - c), me).wait_recv()
            copy(e, 6, (x, 1 - y, 1 - c), me).wait_recv()
            if self.relayed[e]:
                for half in (0, 1):
                    copy(e, 7 + half, (1 - x, 1 - y, 1 - c), me, half=half).wait_recv()
            else:
                copy(e, 7, (1 - x, 1 - y, 1 - c), me).wait_recv()
        for e in range(n):
            copy(e, 0, me, sibling, from_own=True).wait_send()
            copy(e, 1, me, over_x, from_own=True).wait_send()
            copy(e, 2, me, over_y, from_own=True).wait_send()
            if not self.relayed[e]:
                copy(e, 3, me, diagonal, from_own=True).wait_send()
            pltpu.make_async_copy(src[e], rows(e, me), local_sems.at[e]).wait()
        for cp in sent:
            cp.wait_send()


class _ToSibling:
    def __init__(self, grads):
        self.srcs = list(grads)
        n = len(self.srcs)
        self.out_shapes = [jax.ShapeDtypeStruct((4,) + g.shape[1:], g.dtype) for g in self.srcs]
        self.sems = [pltpu.SemaphoreType.DMA((n, 4)), pltpu.SemaphoreType.DMA((n, 4))]

    def _copies(self, src, dst, sems):
        send_sems, recv_sems = sems
        x, y, c = _coords()
        return [
            pltpu.make_async_remote_copy(
                src_ref=src[i].at[2 * q + (1 - c)], dst_ref=dst[i].at[q], send_sem=send_sems.at[i, q],
                recv_sem=recv_sems.at[i, q], device_id=(x, y, 1 - c), device_id_type=MESH)
            for i in range(len(src)) for q in range(4)]

    def start(self, src, dst, sems):
        for cp in self._copies(src, dst, sems):
            cp.start()

    def finish(self, src, dst, sems):
        for cp in self._copies(src, dst, sems):
            cp.wait()


class _ToChips:
    def __init__(self, psums, rows=None):
        self.srcs = list(psums)
        n = len(self.srcs)
        self.rows = rows
        self.out_shapes = [
            jax.ShapeDtypeStruct((3, p.shape[1] if rows is None else rows[1]) + p.shape[2:], p.dtype)
            for p in self.srcs]
        self.sems = [pltpu.SemaphoreType.DMA((n, 3)), pltpu.SemaphoreType.DMA((n, 3))]

    def _copies(self, src, dst, sems):
        send_sems, recv_sems = sems
        x, y, c = _coords()
        peers = [(x, 1 - y), (1 - x, y), (1 - x, 1 - y)]

        def part(i, q):
            if self.rows is None:
                return src[i].at[q]
            return src[i].at[q, pl.ds(self.rows[0], self.rows[1])]

        return [
            pltpu.make_async_remote_copy(
                src_ref=part(i, 2 * px + py), dst_ref=dst[i].at[r], send_sem=send_sems.at[i, r],
                recv_sem=recv_sems.at[i, r], device_id=(px, py, c), device_id_type=MESH)
            for i in range(len(src)) for r, (px, py) in enumerate(peers)]

    def start(self, src, dst, sems):
        for cp in self._copies(src, dst, sems):
            cp.start()

    def finish(self, src, dst, sems):
        for cp in self._copies(src, dst, sems):
            cp.wait()


class _ToOwners:
    def __init__(self, grads):
        self.srcs = list(grads)
        n = len(self.srcs)
        self.out_shapes = [jax.ShapeDtypeStruct(g.shape, g.dtype) for g in self.srcs]
        self.sems = [pltpu.SemaphoreType.DMA((n, 7)), pltpu.SemaphoreType.DMA((n, 7)), pltpu.SemaphoreType.DMA((n,))]

    def _copies(self, src, dst, sems):
        send_sems, recv_sems, local_sems = sems
        x, y, c = _coords()
        me = 4 * x + 2 * y + c
        copies = [pltpu.make_async_copy(src[i].at[me], dst[i].at[me], local_sems.at[i]) for i in range(len(src))]
        for i in range(len(src)):
            for rel in range(1, N_SHARDS):
                px = x ^ (rel >> 2) if rel >> 2 else x
                py = y ^ ((rel >> 1) & 1) if (rel >> 1) & 1 else y
                pc = c ^ (rel & 1) if rel & 1 else c
                copies.append(pltpu.make_async_remote_copy(
                    src_ref=src[i].at[4 * px + 2 * py + pc], dst_ref=dst[i].at[me], send_sem=send_sems.at[i, rel - 1],
                    recv_sem=recv_sems.at[i, rel - 1], device_id=(px, py, pc), device_id_type=MESH))
        return copies

    def start(self, src, dst, sems):
        for cp in self._copies(src, dst, sems):
            cp.start()

    def finish(self, src, dst, sems):
        for cp in self._copies(src, dst, sems):
            cp.wait()


class _Together:
    def __init__(self, parts):
        self.parts = list(parts)
        self.srcs = [s for p in self.parts for s in p.srcs]
        self.out_shapes = [s for p in self.parts for s in p.out_shapes]
        self.sems = [s for p in self.parts for s in p.sems]

    def _split(self, src, dst, sems):
        a = b = c = 0
        for p in self.parts:
            na, nc = len(p.srcs), len(p.sems)
            yield p, src[a:a + na], dst[b:b + na], sems[c:c + nc]
            a, b, c = a + na, b + na, c + nc

    def start(self, src, dst, sems):
        for p, s, d, m in self._split(src, dst, sems):
            p.start(s, d, m)

    def finish(self, src, dst, sems):
        for p, s, d, m in self._split(src, dst, sems):
            p.finish(s, d, m)

    def spread(self):
        b = 0
        for p in self.parts:
            p.results = self.results[b:b + len(p.srcs)]
            b += len(p.srcs)


def _call(body, args, *, grid, in_specs, out_specs, out_shape, name, scratch=(), sem=None, carry=None):
    out_shape, out_specs = list(out_shape), list(out_specs)
    if carry is None:
        return pl.pallas_call(
            body, grid=grid, in_specs=list(in_specs), out_specs=out_specs, out_shape=out_shape,
            scratch_shapes=list(scratch), name=name, compiler_params=_params(sem))(*args)
    n_in, n_out, n_scr, n_c = len(args), len(out_shape), len(scratch), len(carry.srcs)
    steps = tuple(grid)

    def carried(*refs):
        ins, rest = refs[:n_in], refs[n_in:]
        c_src, rest = rest[:n_c], rest[n_c:]
        outs, rest = rest[:n_out], rest[n_out:]
        c_dst, rest = rest[:n_c], rest[n_c:]
        scr, sems = rest[:n_scr], rest[n_scr:]
        first = pl.program_id(0) == 0
        last = pl.program_id(0) == steps[0] - 1
        for ax in range(1, len(steps)):
            first = first & (pl.program_id(ax) == 0)
            last = last & (pl.program_id(ax) == steps[ax] - 1)

        @pl.when(first)
        def _():
            carry.start(c_src, c_dst, sems)

        body(*ins, *outs, *scr)

        @pl.when(last)
        def _():
            carry.finish(c_src, c_dst, sems)

    hbm = pl.BlockSpec(memory_space=pl.ANY)
    res = pl.pallas_call(
        carried, grid=grid, in_specs=list(in_specs) + [hbm] * n_c, out_specs=out_specs + [hbm] * n_c,
        out_shape=out_shape + carry.out_shapes, scratch_shapes=list(scratch) + carry.sems, name=name,
        compiler_params=_params(("arbitrary",) * len(steps)))(*args, *carry.srcs)
    carry.results = list(res[n_out:])
    return list(res[:n_out])


def _exchange_alone(ex, name):
    n = len(ex.srcs)

    def body(*refs):
        src, dst, sems = refs[:n], refs[n:2 * n], refs[2 * n:]
        ex.start(src, dst, sems)
        ex.finish(src, dst, sems)

    hbm = pl.BlockSpec(memory_space=pl.ANY)
    res = pl.pallas_call(body, in_specs=[hbm] * n, out_specs=[hbm] * n, out_shape=ex.out_shapes,
                         scratch_shapes=ex.sems, name=name)(*ex.srcs)
    ex.results = list(res)
    return ex.results


def _rms_fwd(x, gains, name, tm=512, carry=None):
    t, d = x.shape
    n = len(gains)

    def body(*refs):
        x_ref, g_refs, h_refs = refs[0], refs[1:1 + n], refs[1 + n:]
        xf = x_ref[...]
        xhat = xf * lax.rsqrt(jnp.mean(xf * xf, axis=-1, keepdims=True) + EPS)
        for g_ref, h_ref in zip(g_refs, h_refs):
            h_ref[...] = (xhat * g_ref[...]).astype(BF16)

    row = pl.BlockSpec((tm, d), lambda i: (i, 0))
    vec = pl.BlockSpec((1, d), lambda i: (0, 0))
    return _call(body, [x, *gains], grid=(t // tm,), in_specs=[row] + [vec] * n, out_specs=[row] * n,
                 out_shape=[jax.ShapeDtypeStruct((t, d), BF16)] * n, name=name, carry=carry)


def _rms_bwd(x, gains, dhs, dres, name, tm=256, carry=None, through=None):
    t, d = x.shape
    n = len(gains)
    n_w = 0 if through is None else n

    def body(*refs):
        x_ref, dres_ref = refs[0], refs[1]
        g_refs, dh_refs, w_refs = refs[2:2 + n], refs[2 + n:2 + 2 * n], refs[2 + 2 * n:2 + 2 * n + n_w]
        dx_ref, dg_ref = refs[2 + 2 * n + n_w], refs[3 + 2 * n + n_w]
        i = pl.program_id(0)

        @pl.when(i == 0)
        def _():
            dg_ref[...] = jnp.zeros_like(dg_ref)

        xf = x_ref[...]
        r = lax.rsqrt(jnp.mean(xf * xf, axis=-1, keepdims=True) + EPS)
        xhat = xf * r
        dx = dres_ref[...]
        for j in range(n):
            dh = dh_refs[j][...]
            if n_w:
                dh = _dot(dh.astype(BF16), w_refs[j][...], NT)
            dg_ref[j:j + 1, :] += jnp.sum(dh * xhat, axis=0, keepdims=True)
            gy = dh * g_refs[j][...]
            dx = dx + r * (gy - xhat * jnp.mean(gy * xhat, axis=-1, keepdims=True))
        dx_ref[...] = dx

    row = pl.BlockSpec((tm, d), lambda i: (i, 0))
    vec = pl.BlockSpec((1, d), lambda i: (0, 0))
    dh_rows = [pl.BlockSpec((tm, dh.shape[1]), lambda i: (i, 0)) for dh in dhs]
    w_full = [] if through is None else [pl.BlockSpec(w.shape, lambda i: (0, 0)) for w in through]
    return _call(body, [x, dres, *gains, *dhs, *(through or [])], grid=(t // tm,),
                 in_specs=[row, row] + [vec] * n + dh_rows + w_full,
                 out_specs=[row, pl.BlockSpec((8, d), lambda i: (0, 0))],
                 out_shape=[jax.ShapeDtypeStruct((t, d), F32), jax.ShapeDtypeStruct((8, d), F32)],
                 name=name, sem=("arbitrary",), carry=carry)


def _mm(a, b, a_spec, b_spec, o_spec, out_shape, grid, dims, name, res=None, res_spec=None, carry=None):
    nk = grid[2]
    acc_shape = tuple(s for s in o_spec.block_shape if s is not None)

    def body(*refs):
        a_ref, b_ref = refs[0], refs[1]
        r_ref = refs[2] if res is not None else None
        o_ref = refs[3] if res is not None else refs[2]
        p = _dot(a_ref[...].astype(BF16), b_ref[...].astype(BF16), dims)
        if nk == 1:
            if res is not None:
                p = p + r_ref[...]
            o_ref[...] = p.astype(o_ref.dtype)
            return
        acc_ref = refs[-1]
        k = pl.program_id(2)

        @pl.when(k == 0)
        def _():
            acc_ref[...] = p

        @pl.when(k > 0)
        def _():
            acc_ref[...] += p

        @pl.when(k == nk - 1)
        def _():
            out = acc_ref[...]
            if res is not None:
                out = out + r_ref[...]
            o_ref[...] = out.astype(o_ref.dtype)

    ins = [a, b] + ([res] if res is not None else [])
    specs = [a_spec, b_spec] + ([res_spec] if res is not None else [])
    return _call(body, ins, grid=grid, in_specs=specs, out_specs=[o_spec], out_shape=[out_shape],
                 scratch=[pltpu.VMEM(acc_shape, F32)] if nk > 1 else [], name=name,
                 sem=("parallel", "parallel", "arbitrary"), carry=carry)[0]


def _mm_rows(a, w, out_dtype, name, trans_w=False, res=None, tm=512, carry=None):
    t, k = a.shape
    n = w.shape[0] if trans_w else w.shape[1]
    return _mm(
        a, w, pl.BlockSpec((tm, k), lambda i, j, kk: (i, 0)), pl.BlockSpec(w.shape, lambda i, j, kk: (0, 0)),
        pl.BlockSpec((tm, n), lambda i, j, kk: (i, 0)), jax.ShapeDtypeStruct((t, n), out_dtype), (t // tm, 1, 1),
        NT if trans_w else NN, name, res=res,
        res_spec=None if res is None else pl.BlockSpec((tm, n), lambda i, j, kk: (i, 0)), carry=carry)


def _mm_wgrad(a, b, name, carry=None):
    t, m = a.shape
    n = b.shape[1]
    tn = n // (4 if b.dtype == F32 else 2)
    return _mm(
        a, b, pl.BlockSpec((t, m), lambda i, j, kk: (0, 0)), pl.BlockSpec((t, tn), lambda i, j, kk: (0, j)),
        pl.BlockSpec((m, tn), lambda i, j, kk: (0, j)), jax.ShapeDtypeStruct((m, n), F32), (1, n // tn, 1), TN, name,
        carry=carry)


def _sgu_fwd(x0, h1, w_in, g_v, w_c, b_sb, w_out, tm=256, carry=None):
    t, d = x0.shape
    nsub = w_in.shape[2]

    def body(x_ref, h_ref, win_ref, gv_ref, wc_ref, bsb_ref, wout_ref, zpre_ref, x1_ref, u_s, v_s, vn_s, y_s):
        h = h_ref[...]
        for k in range(N_SHARDS):
            zk = _dot(h, win_ref[k])
            zpre_ref[:, k * nsub:(k + 1) * nsub] = zk
            cdf, _ = _gelu_parts(zk)
            if k < N_SHARDS // 2:
                u_s[:, k * nsub:(k + 1) * nsub] = zk * cdf
            else:
                v_s[:, (k - 4) * nsub:(k - 3) * nsub] = zk * cdf
        v = v_s[...]
        rv = lax.rsqrt(jnp.mean(v * v, axis=-1, keepdims=True) + EPS)
        vn_s[...] = (v * rv * gv_ref[...]).astype(BF16)
        for ci in range(tm // CHUNK):
            rows = slice(ci * CHUNK, (ci + 1) * CHUNK)
            for g in range(N_GROUPS):
                cols = slice(g * LANES, (g + 1) * LANES)
                sv = _dot(wc_ref[g], vn_s[rows, cols]) + bsb_ref[g]
                y_s[rows, cols] = (u_s[rows, cols] * sv).astype(BF16)
        x1_ref[...] = x_ref[...] + _dot(y_s[...], wout_ref[...])

    row = pl.BlockSpec((tm, d), lambda i: (i, 0))
    full = lambda a: pl.BlockSpec(a.shape, lambda i: (0,) * a.ndim)
    return _call(
        body, [x0, h1, w_in, g_v, w_c, b_sb, w_out], grid=(t // tm,),
        in_specs=[row, row, full(w_in), full(g_v), full(w_c), full(b_sb), full(w_out)],
        out_specs=[pl.BlockSpec((tm, 2 * d), lambda i: (i, 0)), row],
        out_shape=[jax.ShapeDtypeStruct((t, 2 * d), F32), jax.ShapeDtypeStruct((t, d), F32)],
        scratch=[pltpu.VMEM((tm, d), F32), pltpu.VMEM((tm, d), F32), pltpu.VMEM((tm, d), BF16),
                 pltpu.VMEM((tm, d), BF16)],
        name="sgu_fwd", carry=carry)


def _sgu_bwd(dx1, zpre, w_out, g_v, w_c, w_ct, b_sb, tm=256, carry=None):
    t, d = dx1.shape

    def body(dx_ref, zpre_ref, wout_ref, gv_ref, wc_ref, wct_ref, bsb_ref,
             dz_ref, y_ref, dwc_ref, dbs_ref, dgv_ref, u_s, vn_s, dy_s, du_s, dvn_s):
        i = pl.program_id(0)

        @pl.when(i == 0)
        def _():
            dwc_ref[...] = jnp.zeros_like(dwc_ref)
            dbs_ref[...] = jnp.zeros_like(dbs_ref)
            dgv_ref[...] = jnp.zeros_like(dgv_ref)

        dy_s[...] = _dot(dx_ref[...].astype(BF16), wout_ref[...], NT)
        zu = zpre_ref[:, :d]
        zv = zpre_ref[:, d:]
        cdf_u, pdf_u = _gelu_parts(zu)
        cdf_v, pdf_v = _gelu_parts(zv)
        u_s[...] = zu * cdf_u
        v = zv * cdf_v
        rv = lax.rsqrt(jnp.mean(v * v, axis=-1, keepdims=True) + EPS)
        vhat = v * rv
        gv = gv_ref[...]
        vn_s[...] = (vhat * gv).astype(BF16)
        for ci in range(tm // CHUNK):
            rows = slice(ci * CHUNK, (ci + 1) * CHUNK)
            for g in range(N_GROUPS):
                cols = slice(g * LANES, (g + 1) * LANES)
                vnb = vn_s[rows, cols]
                sv = _dot(wc_ref[g], vnb) + bsb_ref[g]
                dyb = dy_s[rows, cols]
                ub = u_s[rows, cols]
                dsv = dyb * ub
                du_s[rows, cols] = dyb * sv
                y_ref[rows, cols] = (ub * sv).astype(BF16)
                dsvb = dsv.astype(BF16)
                dbs_ref[g] += dsv
                dwc_ref[g] += _dot(dsvb, vnb, NT)
                dvn_s[rows, cols] = _dot(wct_ref[g], dsvb)
        dvn = dvn_s[...]
        dgv_ref[0:1, :] += jnp.sum(dvn * vhat, axis=0, keepdims=True)
        gy = dvn * gv
        dv = rv * (gy - vhat * jnp.mean(gy * vhat, axis=-1, keepdims=True))
        dz_ref[:, :d] = (du_s[...] * (cdf_u + zu * pdf_u)).astype(BF16)
        dz_ref[:, d:] = (dv * (cdf_v + zv * pdf_v)).astype(BF16)

        @pl.when(i == t // tm - 1)
        def _():
            tri = (lax.broadcasted_iota(jnp.int32, (CHUNK, CHUNK), 0)
                   >= lax.broadcasted_iota(jnp.int32, (CHUNK, CHUNK), 1))
            for g in range(N_GROUPS):
                dwc_ref[g] = jnp.where(tri, dwc_ref[g], 0.0)
                dbs_ref[g] = jnp.broadcast_to(jnp.sum(dbs_ref[g], axis=1, keepdims=True), (CHUNK, CHUNK))

    row = pl.BlockSpec((tm, d), lambda i: (i, 0))
    row2 = pl.BlockSpec((tm, 2 * d), lambda i: (i, 0))
    full = lambda a: pl.BlockSpec(a.shape, lambda i: (0,) * a.ndim)
    grp = pl.BlockSpec((N_GROUPS, CHUNK, CHUNK), lambda i: (0, 0, 0))
    return _call(
        body, [dx1, zpre, w_out, g_v, w_c, w_ct, b_sb], grid=(t // tm,),
        in_specs=[row, row2, full(w_out), full(g_v), full(w_c), full(w_ct), full(b_sb)],
        out_specs=[row2, row, grp, grp, pl.BlockSpec((8, d), lambda i: (0, 0))],
        out_shape=[jax.ShapeDtypeStruct((t, 2 * d), BF16), jax.ShapeDtypeStruct((t, d), BF16),
                   jax.ShapeDtypeStruct((N_GROUPS, CHUNK, CHUNK), F32),
                   jax.ShapeDtypeStruct((N_GROUPS, CHUNK, CHUNK), F32), jax.ShapeDtypeStruct((8, d), F32)],
        scratch=[pltpu.VMEM((tm, d), F32), pltpu.VMEM((tm, d), BF16), pltpu.VMEM((tm, d), F32),
                 pltpu.VMEM((tm, d), F32), pltpu.VMEM((tm, d), F32)],
        name="sgu_bwd", sem=("arbitrary",), carry=carry)


ROW_CHUNK = 256
HALO = 16


def _ffn_fwd(x, g, w_in, cw, cb, w_out, layer, tm=512, carry=None, next_gains=(), loss_target=None):
    t, d = x.shape
    nc = N_SHARDS // 2
    n_gains = len(next_gains)
    with_loss = loss_target is not None

    def body(x_ref, xp_ref, g_ref, wg_ref, wu_ref, cwg_ref, cbg_ref, cwu_ref, cbu_ref, wout_ref, *rest):
        extra_in, rest = rest[:n_gains + with_loss], rest[n_gains + with_loss:]
        o_ref, hf_ref, a_ref, pre_ref = rest[:4]
        extra_out, hw_s = rest[4:-1], rest[-1]
        i, c = pl.program_id(0), pl.program_id(1)

        @pl.when(c == 0)
        def _():
            keep = jnp.where(i == 0, 0.0, 1.0)
            xw = jnp.concatenate([xp_ref[...] * keep, x_ref[...]], axis=0)
            xhat = xw * lax.rsqrt(jnp.mean(xw * xw, axis=-1, keepdims=True) + EPS)
            hw_s[...] = (xhat * g_ref[...]).astype(BF16)
            hf_ref[...] = hw_s[HALO:, :]
            o_ref[...] = x_ref[...]

        hw = hw_s[...]
        pre = []
        for j, (w_ref, cw_ref, cb_ref) in enumerate(((wg_ref, cwg_ref, cbg_ref), (wu_ref, cwu_ref, cbu_ref))):
            ab = _dot(hw, w_ref[...]).astype(BF16)
            a_ref[j] = ab[HALO:]
            win = ab.astype(F32)
            cw_v = cw_ref[...]
            pre.append(cw_v[2:3, :] * win[HALO:] + cw_v[1:2, :] * pltpu.roll(win, 1, 0)[HALO:]
                       + cw_v[0:1, :] * pltpu.roll(win, 2, 0)[HALO:] + cb_ref[...])
            pre_ref[j] = pre[j]
        act = (pre[0] * _sigmoid(pre[0]) * pre[1]).astype(BF16)
        o_ref[...] += _dot(act, wout_ref[...])

        if with_loss:
            @pl.when((i == 0) & (c == 0))
            def _():
                extra_out[-1][...] = jnp.zeros_like(extra_out[-1])

        @pl.when(c == nc - 1)
        def _():
            xn = o_ref[...]
            if n_gains:
                xhat = xn * lax.rsqrt(jnp.mean(xn * xn, axis=-1, keepdims=True) + EPS)
                for k in range(n_gains):
                    extra_out[k][...] = (xhat * extra_in[k][...]).astype(BF16)
            if with_loss:
                err = xn - extra_in[-1][...]
                extra_out[-2][...] = err * (1.0 / d)
                part = jnp.sum(jnp.sum(err * err, axis=0, keepdims=True), axis=1, keepdims=True)
                extra_out[-1][...] += jnp.broadcast_to(0.5 / d * part, extra_out[-1].shape)

    row = pl.BlockSpec((tm, d), lambda i, c: (i, 0))
    vec = pl.BlockSpec((1, d), lambda i, c: (0, 0))
    shard = lambda rows, up: pl.BlockSpec((None, rows, FF_SHARD), lambda i, c: (c + up * nc, 0, 0))
    pair = pl.BlockSpec((2, None, tm, FF_SHARD), lambda i, c: (0, c, i, 0))
    lanes = pl.BlockSpec((8, LANES), lambda i, c: (0, 0))
    outs = _call(
        body, [x, x, g, w_in, w_in, cw, cb, cw, cb, w_out, *next_gains] + ([loss_target] if with_loss else []),
        grid=(t // tm, nc),
        in_specs=[row, pl.BlockSpec((HALO, d), lambda i, c: (jnp.maximum(i * (tm // HALO) - 1, 0), 0)),
                  vec, shard(d, 0), shard(d, 1), shard(8, 0), shard(1, 0), shard(8, 1), shard(1, 1),
                  pl.BlockSpec((FF_SHARD, d), lambda i, c: (c, 0))] + [vec] * n_gains + [row] * with_loss,
        out_specs=[row, row, pair, pair] + [row] * n_gains + [row, lanes] * with_loss,
        out_shape=[jax.ShapeDtypeStruct((t, d), F32), jax.ShapeDtypeStruct((t, d), BF16),
                   jax.ShapeDtypeStruct((2, nc, t, FF_SHARD), BF16), jax.ShapeDtypeStruct((2, nc, t, FF_SHARD), F32)]
        + [jax.ShapeDtypeStruct((t, d), BF16)] * n_gains
        + [jax.ShapeDtypeStruct((t, d), F32), jax.ShapeDtypeStruct((8, LANES), F32)] * with_loss,
        scratch=[pltpu.VMEM((tm + HALO, d), BF16)], name=f"ffn{layer}_fwd", sem=("arbitrary", "arbitrary"), carry=carry)
    return (outs[0], outs[1], outs[2].reshape(N_SHARDS, t, FF_SHARD), outs[3]) + tuple(outs[4:])


def _ffn_bwd_act(pre, w_out, dxn, layer, tm=512, carry=None):
    t, d = dxn.shape
    nc = N_SHARDS // 2

    def body(pre_ref, wout_ref, dx_ref, dhu_ref, dw_ref, dcb_ref):
        i = pl.program_id(1)

        @pl.when(i == 0)
        def _():
            dw_ref[...] = jnp.zeros_like(dw_ref)
            dcb_ref[...] = jnp.zeros_like(dcb_ref)

        hg, hu = pre_ref[0], pre_ref[1]
        sg = _sigmoid(hg)
        sl = hg * sg
        dxb = dx_ref[...].astype(BF16)
        dact = _dot(dxb, wout_ref[...], NT)
        dw_ref[...] += _dot((sl * hu).astype(BF16), dxb, TN)
        d_up = dact * sl
        d_gate = dact * hu * (sg * (1.0 + hg * (1.0 - sg)))
        for j, dv in enumerate((d_gate, d_up)):
            dhu_ref[j] = dv.astype(BF16)
            dcb_ref[j, 0:1, :] += jnp.sum(dv, axis=0, keepdims=True)

    return _call(
        body, [pre, w_out, dxn], grid=(nc, t // tm),
        in_specs=[pl.BlockSpec((2, None, tm, FF_SHARD), lambda c, i: (0, c, i, 0)),
                  pl.BlockSpec((FF_SHARD, d), lambda c, i: (c, 0)), pl.BlockSpec((tm, d), lambda c, i: (i, 0))],
        out_specs=[pl.BlockSpec((None, 2, tm, FF_SHARD), lambda c, i: (c, 0, i, 0)),
                   pl.BlockSpec((FF_SHARD, d), lambda c, i: (c, 0)),
                   pl.BlockSpec((None, 2, 8, FF_SHARD), lambda c, i: (c, 0, 0, 0))],
        out_shape=[jax.ShapeDtypeStruct((nc, 2, t, FF_SHARD), BF16), jax.ShapeDtypeStruct((D_FF, d), F32),
                   jax.ShapeDtypeStruct((nc, 2, 8, FF_SHARD), F32)],
        name=f"ffn{layer}_bwd_act", sem=("parallel", "arbitrary"), carry=carry)


def _ffn_bwd_in(dhu, a, cw, w_in, layer, tm=1024, carry=None):
    nc, _, t, _ = dhu.shape
    d = D_MODEL
    tm = min(tm, t)
    last_blk = t // 16 - 1

    def body(dh_ref, nx_ref, a_ref, cw_ref, win_ref, da_ref, o_ref, dcw_ref):
        i, s = pl.program_id(0), pl.program_id(1)

        @pl.when(s == 0)
        def _():
            o_ref[...] = jnp.zeros_like(o_ref)

        @pl.when((s == 0) & (i == 0))
        def _():
            dcw_ref[...] = jnp.zeros_like(dcw_ref)

        keep = jnp.where(i == t // tm - 1, 0.0, 1.0)
        cw = cw_ref[...]
        sums = [None] * 3
        for r0 in range(0, tm, ROW_CHUNK):
            rows = slice(r0, r0 + ROW_CHUNK)
            if r0 + ROW_CHUNK == tm:
                win = jnp.concatenate([dh_ref[rows, :].astype(F32), nx_ref[...].astype(F32) * keep], axis=0)
            else:
                win = dh_ref[r0:r0 + ROW_CHUNK + HALO, :].astype(F32)
            n = ROW_CHUNK + HALO
            taps = (pltpu.roll(win, n - 2, 0)[:ROW_CHUNK],
                    pltpu.roll(win, n - 1, 0)[:ROW_CHUNK],
                    win[:ROW_CHUNK])
            da = (cw[0:1, :] * taps[0] + cw[1:2, :] * taps[1] + cw[2:3, :] * taps[2]).astype(BF16)
            da_ref[rows, :] = da
            o_ref[rows, :] += _dot(da, win_ref[...], NT)
            af = a_ref[rows, :].astype(F32)
            parts = [jnp.sum(taps[k] * af, axis=0, keepdims=True) for k in range(3)]
            sums = [p if q is None else q + p for q, p in zip(sums, parts)]
        for k in range(3):
            dcw_ref[pl.ds(s, 1), k:k + 1, :] += sums[k][None]

    return _call(
        body, [dhu, dhu, a, cw, w_in], grid=(t // tm, N_SHARDS),
        in_specs=[pl.BlockSpec((None, None, tm, FF_SHARD), lambda i, s: (s % nc, s // nc, i, 0)),
                  pl.BlockSpec((None, None, 16, FF_SHARD),
                               lambda i, s: (s % nc, s // nc, jnp.minimum((i + 1) * (tm // 16), last_blk), 0)),
                  pl.BlockSpec((None, tm, FF_SHARD), lambda i, s: (s, i, 0)),
                  pl.BlockSpec((None, 8, FF_SHARD), lambda i, s: (s, 0, 0)),
                  pl.BlockSpec((None, d, FF_SHARD), lambda i, s: (s, 0, 0))],
        out_specs=[pl.BlockSpec((None, tm, FF_SHARD), lambda i, s: (s, i, 0)),
                   pl.BlockSpec((tm, d), lambda i, s: (i, 0)),
                   pl.BlockSpec((N_SHARDS, 8, FF_SHARD), lambda i, s: (0, 0, 0))],
        out_shape=[jax.ShapeDtypeStruct((N_SHARDS, t, FF_SHARD), BF16), jax.ShapeDtypeStruct((t, d), F32),
                   jax.ShapeDtypeStruct((N_SHARDS, 8, FF_SHARD), F32)],
        name=f"ffn{layer}_bwd_in", sem=("arbitrary", "arbitrary"), carry=carry)


def _ffn_wgrad_in(hf, da, layer, carry=None):
    t, d = hf.shape
    return _mm(
        da, hf, pl.BlockSpec((None, t, FF_SHARD), lambda s, j, kk: (s, 0, 0)),
        pl.BlockSpec((t, d), lambda s, j, kk: (0, 0)),
        pl.BlockSpec((None, FF_SHARD, d), lambda s, j, kk: (s, 0, 0)),
        jax.ShapeDtypeStruct((N_SHARDS, FF_SHARD, d), F32), (N_SHARDS, 1, 1), TN, f"ffn{layer}_wgrad_in",
        carry=carry)


Q_PER_KV = N_Q_HEADS // N_KV_HEADS
GROUP_ROWS = Q_PER_KV * CHUNK


def _attn_masks(n):
    lane = lax.broadcasted_iota(jnp.int32, (CHUNK, LANES), 1)
    lo = lane < HEAD_DIM
    tq = lax.broadcasted_iota(jnp.int32, (GROUP_ROWS, 2 * CHUNK), 0) & (CHUNK - 1)
    jk = lax.broadcasted_iota(jnp.int32, (GROUP_ROWS, 2 * CHUNK), 1)
    dist = tq + CHUNK - jk
    mask = (dist >= 0) & (dist < CHUNK) & (jk >= jnp.where(n == 0, CHUNK, 0))
    return lo, mask, dist.astype(F32)


def _per_head_column(values):
    r = lax.broadcasted_iota(jnp.int32, (GROUP_ROWS, 1), 0)
    col = jnp.full((GROUP_ROWS, 1), values[Q_PER_KV - 1], F32)
    for j in range(Q_PER_KV - 2, -1, -1):
        col = jnp.where(r < (j + 1) * CHUNK, values[j], col)
    return col


def _half_sum(x, lo):
    s_lo = jnp.sum(jnp.where(lo, x, 0.0), axis=-1, keepdims=True)
    s_hi = jnp.sum(jnp.where(lo, 0.0, x), axis=-1, keepdims=True)
    return jnp.where(lo, s_lo, s_hi)


def _stack_heads(pairs, lo):
    zero = jnp.zeros_like(pairs[0])
    return jnp.concatenate([jnp.where(lo, pairs[0], zero), jnp.where(lo, zero, pairs[0]),
                            jnp.where(lo, pairs[1], zero), jnp.where(lo, zero, pairs[1])], axis=0)


def _unstack_heads(stacked, lo):
    return (jnp.where(lo, stacked[0:CHUNK], stacked[CHUNK:2 * CHUNK]),
            jnp.where(lo, stacked[2 * CHUNK:3 * CHUNK], stacked[3 * CHUNK:]))


def _attn_probs(qs, kn, mask, distf, slope_col, sink_col):
    s = _dot(qs, kn, NT) * (HEAD_DIM ** -0.5)
    s = jnp.where(mask, s - slope_col * distf, NEG_BIG)
    m = jnp.maximum(jnp.max(s, axis=-1, keepdims=True), sink_col)
    e = jnp.exp(s - m)
    den = jnp.sum(e, axis=-1, keepdims=True) + jnp.exp(sink_col - m)
    return e * (1.0 / den), m, den


def _attn_fwd(qraw, kvd, gq, gk, sinks, carry=None):
    t, d = qraw.shape
    nb = t // CHUNK

    def body(sink_ref, q_ref, cur_ref, prev_ref, gq_ref, gk_ref, o_ref):
        n = pl.program_id(0)
        lo, mask, distf = _attn_masks(n)
        gq_v, gk_v = gq_ref[...], gk_ref[...]
        for kvh in range(N_KV_HEADS):
            ks = slice(kvh * LANES, (kvh + 1) * LANES)
            vs = slice(4 * LANES + kvh * LANES, 4 * LANES + (kvh + 1) * LANES)
            kraw = jnp.concatenate([prev_ref[:, ks], cur_ref[:, ks]], axis=0)
            rk = lax.rsqrt(jnp.mean(kraw * kraw, axis=-1, keepdims=True) + EPS)
            kn = (kraw * rk * gk_v).astype(BF16)
            vv = jnp.concatenate([prev_ref[:, vs], cur_ref[:, vs]], axis=0).astype(BF16)
            qn = []
            for p in range(2):
                qp = q_ref[:, (2 * kvh + p) * LANES:(2 * kvh + p + 1) * LANES]
                r = lax.rsqrt(_half_sum(qp * qp, lo) * (1.0 / HEAD_DIM) + EPS)
                qn.append(qp * r * gq_v)
            heads = range(Q_PER_KV * kvh, Q_PER_KV * (kvh + 1))
            pf, _, _ = _attn_probs(_stack_heads(qn, lo).astype(BF16), kn, mask, distf,
                                   _per_head_column([SLOPES[h] for h in heads]),
                                   _per_head_column([sink_ref[h] for h in heads]))
            for p, o_pair in enumerate(_unstack_heads(_dot(pf.astype(BF16), vv), lo)):
                o_ref[:, (2 * kvh + p) * LANES:(2 * kvh + p + 1) * LANES] = o_pair.astype(BF16)

    blk = lambda f: pl.BlockSpec((CHUNK, d), f)
    vec = pl.BlockSpec((1, LANES), lambda n: (0, 0))
    return _call(
        body, [sinks, qraw, kvd, kvd, gq, gk], grid=(nb,),
        in_specs=[pl.BlockSpec(memory_space=pltpu.SMEM), blk(lambda n: (n, 0)), blk(lambda n: (n, 0)),
                  blk(lambda n: (jnp.maximum(n - 1, 0), 0)), vec, vec],
        out_specs=[blk(lambda n: (n, 0))], out_shape=[jax.ShapeDtypeStruct((t, d), BF16)],
        name="attn_fwd", carry=carry)[0]


def _attn_bwd(qraw, kvd, d_o, gq, gk, sinks, carry=None):
    t, d = qraw.shape
    nb = t // CHUNK

    def body(sink_ref, q_ref, cur_ref, prev_ref, do_ref, gq_ref, gk_ref,
             dq_ref, dkv_ref, dsink_ref, dgq_ref, dgk_ref, carry_s, pp_s, cp_s):
        n = pl.program_id(0)

        @pl.when(n == 0)
        def _():
            carry_s[...] = jnp.zeros_like(carry_s)
            dsink_ref[...] = jnp.zeros_like(dsink_ref)
            dgq_ref[...] = jnp.zeros_like(dgq_ref)
            dgk_ref[...] = jnp.zeros_like(dgk_ref)

        @pl.when(n < nb)
        def _():
            lo, mask, distf = _attn_masks(n)
            gq_v, gk_v = gq_ref[...], gk_ref[...]
            for kvh in range(N_KV_HEADS):
                ks = slice(kvh * LANES, (kvh + 1) * LANES)
                vs = slice(4 * LANES + kvh * LANES, 4 * LANES + (kvh + 1) * LANES)
                kraw = jnp.concatenate([prev_ref[:, ks], cur_ref[:, ks]], axis=0)
                rk = lax.rsqrt(jnp.mean(kraw * kraw, axis=-1, keepdims=True) + EPS)
                khat = kraw * rk
                kn = (khat * gk_v).astype(BF16)
                vv = jnp.concatenate([prev_ref[:, vs], cur_ref[:, vs]], axis=0).astype(BF16)
                cols = [slice((2 * kvh + p) * LANES, (2 * kvh + p + 1) * LANES) for p in range(2)]
                rq, qhat = [], []
                for p in range(2):
                    qp = q_ref[:, cols[p]]
                    rq.append(lax.rsqrt(_half_sum(qp * qp, lo) * (1.0 / HEAD_DIM) + EPS))
                    qhat.append(qp * rq[p])
                heads = range(Q_PER_KV * kvh, Q_PER_KV * (kvh + 1))
                qs = _stack_heads([qhat[p] * gq_v for p in range(2)], lo).astype(BF16)
                dos = _stack_heads([do_ref[:, cols[p]] for p in range(2)], lo)
                sink_col = _per_head_column([sink_ref[h] for h in heads])
                pf, m, den = _attn_probs(qs, kn, mask, distf, _per_head_column([SLOPES[h] for h in heads]), sink_col)
                dp = _dot(dos, vv, NT)
                delta = jnp.sum(pf * dp, axis=-1, keepdims=True)
                sink_delta = jnp.exp(sink_col - m) / den * delta
                for j, h in enumerate(heads):
                    dsink_ref[h:h + 1, :] -= jnp.broadcast_to(
                        jnp.sum(sink_delta[j * CHUNK:(j + 1) * CHUNK], axis=0, keepdims=True), (1, LANES))
                ds = (pf * (dp - delta) * (HEAD_DIM ** -0.5)).astype(BF16)
                dkn = _dot(ds, qs, TN)
                dvb = _dot(pf.astype(BF16), dos, TN)
                for p, dqn in enumerate(_unstack_heads(_dot(ds, kn), lo)):
                    dgq_ref[0:1, :] += jnp.sum(dqn * qhat[p], axis=0, keepdims=True)
                    gy = dqn * gq_v
                    mq = _half_sum(gy * qhat[p], lo) * (1.0 / HEAD_DIM)
                    dq_ref[:, cols[p]] = (rq[p] * (gy - qhat[p] * mq)).astype(BF16)
                dgk_ref[0:1, :] += jnp.sum(dkn * khat, axis=0, keepdims=True)
                gyk = dkn * gk_v
                dkraw = rk * (gyk - khat * jnp.mean(gyk * khat, axis=-1, keepdims=True))
                pp_s[:, ks] = dkraw[:CHUNK]
                cp_s[:, ks] = dkraw[CHUNK:]
                pp_s[:, vs] = dvb[:CHUNK]
                cp_s[:, vs] = dvb[CHUNK:]
            dkv_ref[...] = (carry_s[...] + pp_s[...]).astype(BF16)
            carry_s[...] = cp_s[...]

        @pl.when(n == nb)
        def _():
            dkv_ref[...] = carry_s[...].astype(BF16)

    blk = lambda f: pl.BlockSpec((CHUNK, d), f)
    vec = pl.BlockSpec((1, LANES), lambda n: (0, 0))
    cur = lambda n: (jnp.minimum(n, nb - 1), 0)
    prev = lambda n: (jnp.maximum(jnp.minimum(n, nb - 1) - 1, 0), 0)
    small = lambda r: pl.BlockSpec((r, LANES), lambda n: (0, 0))
    return _call(
        body, [sinks, qraw, kvd, kvd, d_o, gq, gk], grid=(nb + 1,),
        in_specs=[pl.BlockSpec(memory_space=pltpu.SMEM), blk(cur), blk(cur), blk(prev), blk(cur), vec, vec],
        out_specs=[blk(cur), blk(lambda n: (jnp.maximum(n - 1, 0), 0)), small(N_Q_HEADS), small(8), small(8)],
        out_shape=[jax.ShapeDtypeStruct((t, d), BF16), jax.ShapeDtypeStruct((t, d), BF16),
                   jax.ShapeDtypeStruct((N_Q_HEADS, LANES), F32), jax.ShapeDtypeStruct((8, LANES), F32),
                   jax.ShapeDtypeStruct((8, LANES), F32)],
        scratch=[pltpu.VMEM((CHUNK, d), F32)] * 3, name="attn_bwd", sem=("arbitrary",), carry=carry)


def _adamw_math(g, w, m, v):
    m = ADAM_B1 * m + (1.0 - ADAM_B1) * g
    v = ADAM_B2 * v + (1.0 - ADAM_B2) * (g * g)
    m_hat = m / (1.0 - ADAM_B1 ** ADAM_STEP)
    v_hat = v / (1.0 - ADAM_B2 ** ADAM_STEP)
    delta = -ADAM_LR * (m_hat / (jnp.sqrt(v_hat) + ADAM_EPS) + ADAM_WD * w)
    return delta, m, v


def _row_tile(r, cap=128):
    for tr in range(min(r, cap), 0, -1):
        if r % tr == 0 and (tr % 8 == 0 or tr == r):
            return tr
    return r


def _chip_sum(grad, recv, place, name, wire_dtype):
    _, r, c = grad.shape
    tr = _row_tile(r, 256)

    def body(pl_ref, g_ref, a_ref, p_ref):
        p_ref[...] = (g_ref[...] + a_ref[...]).astype(p_ref.dtype)

    return pl.pallas_call(
        body,
        grid_spec=pltpu.PrefetchScalarGridSpec(
            num_scalar_prefetch=1, grid=(4, r // tr),
            in_specs=[pl.BlockSpec((None, None, tr, c), lambda q, i, pr: (q, pr[1], i, 0)),
                      pl.BlockSpec((None, tr, c), lambda q, i, pr: (q, i, 0))],
            out_specs=pl.BlockSpec((None, tr, c), lambda q, i, pr: (q, i, 0))),
        out_shape=jax.ShapeDtypeStruct((4, r, c), wire_dtype), name=name, compiler_params=_params(),
    )(place, grad.reshape(4, 2, r, c), recv)


def _adamw_sharded(grad, recv, others, place, w, m, v, name, layer=None, fill=None):
    r, c = w.shape[-2:]
    tr = _row_tile(r)

    def body(pl_ref, g_ref, a_ref, oth_ref, w_ref, m_ref, v_ref, *rest):
        g_out, d_out, nm_out, nv_out = rest[-4:]
        g = g_ref[...] + a_ref[...]
        for k in range(3):
            g = g + oth_ref[k].astype(F32)
        delta, nm, nv = _adamw_math(g, w_ref[...], m_ref[...], v_ref[...])
        g_out[...] = g
        d_out[...] = delta
        nm_out[...] = nm
        nv_out[...] = nv

    if layer is None:
        row = pl.BlockSpec((tr, c), lambda i, pr: (i, 0))
    else:
        row = pl.BlockSpec((None, tr, c), lambda i, pr: (layer, i, 0))
    n_fill = 0 if fill is None else 4
    in_specs = [pl.BlockSpec((None, None, tr, c), lambda i, pr: (pr[0], pr[1], i, 0)),
                pl.BlockSpec((None, tr, c), lambda i, pr: (pr[0], i, 0)),
                pl.BlockSpec((3, tr, c), lambda i, pr: (0, i, 0)), row, row, row]
    in_specs += [pl.BlockSpec(memory_space=pl.ANY)] * n_fill
    return pl.pallas_call(
        body,
        grid_spec=pltpu.PrefetchScalarGridSpec(
            num_scalar_prefetch=1, grid=(r // tr,), in_specs=in_specs, out_specs=[row] * 4),
        out_shape=[jax.ShapeDtypeStruct(w.shape, F32)] * 4, name=name, compiler_params=_params(),
        input_output_aliases={7 + j: j for j in range(n_fill)},
    )(place, grad.reshape(4, 2, r, c), recv, others, w, m, v, *([] if fill is None else fill))


def _adamw_summed(parts, ws, ms, vs, name):
    n = len(parts)

    def body(*refs):
        p_refs, w_refs, m_refs, v_refs = refs[:n], refs[n:2 * n], refs[2 * n:3 * n], refs[3 * n:4 * n]
        o_refs = refs[4 * n:]
        for i in range(n):
            g = p_refs[i][0]
            for k in range(1, N_SHARDS):
                g = g + p_refs[i][k]
            delta, nm, nv = _adamw_math(g, w_refs[i][...], m_refs[i][...], v_refs[i][...])
            o_refs[4 * i][...] = g
            o_refs[4 * i + 1][...] = delta
            o_refs[4 * i + 2][...] = nm
            o_refs[4 * i + 3][...] = nv

    shapes = [jax.ShapeDtypeStruct(w.shape, F32) for w in ws for _ in range(4)]
    outs = pl.pallas_call(body, out_shape=shapes, name=name, compiler_params=_params())(*parts, *ws, *ms, *vs)
    return [outs[4 * i:4 * i + 4] for i in range(n)]


def _dup_heads(w):
    lead = w.shape[:-1]
    w4 = w.reshape(lead + (N_KV_HEADS, 1, HEAD_DIM))
    return jnp.broadcast_to(w4, lead + (N_KV_HEADS, 2, HEAD_DIM)).reshape(lead + (N_KV_HEADS * LANES,))


def _fold_heads(g):
    lead = g.shape[:-1]
    return g.reshape(lead + (N_KV_HEADS, 2, HEAD_DIM)).sum(axis=-2).reshape(lead + (N_KV_HEADS * HEAD_DIM,))


def kernel(x, a_norm, a_w_in, a_v_norm, a_w_s, a_b_s, a_w_out, f_norm, f_w_in, f_conv_w, f_conv_b, f_w_out, kv_norm, w_kv, k_norm, b_norm, b_w_q, b_q_norm, b_sinks, b_w_o, loss_target, m_a_norm, m_a_w_in, m_a_v_norm, m_a_w_s, m_a_b_s, m_a_w_out, m_f_norm, m_f_w_in, m_f_conv_w, m_f_conv_b, m_f_w_out, m_kv_norm, m_w_kv, m_k_norm, m_b_norm, m_b_w_q, m_b_q_norm, m_b_sinks, m_b_w_o, v_a_norm, v_a_w_in, v_a_v_norm, v_a_w_s, v_a_b_s, v_a_w_out, v_f_norm, v_f_w_in, v_f_conv_w, v_f_conv_b, v_f_w_out, v_kv_norm, v_w_kv, v_k_norm, v_b_norm, v_b_w_q, v_b_q_norm, v_b_sinks, v_b_w_o):
    d = D_MODEL
    xi, yi, ci = _coords()
    place = jnp.stack([2 * xi + yi, ci]).astype(jnp.int32)
    bf = lambda a: a.astype(BF16)
    row = lambda v_: v_.reshape(1, -1)
    x0, target = x[0], loss_target[0]
    t = x0.shape[0]
    res = {}

    red = {}

    def to_sibling(grads, wire=BF16):
        for k, g in grads.items():
            red[k] = dict(grad=g, wire=wire)
        ex = _ToSibling(list(grads.values()))
        ex.names = list(grads)
        return ex

    def to_chips(ex):
        for k, a in zip(ex.names, ex.results):
            red[k]["recv"] = a
            red[k]["psum"] = _chip_sum(red[k]["grad"], a, place, f"chip_sum_{k}", red[k]["wire"])
        nxt = _ToChips([red[k]["psum"] for k in ex.names])
        nxt.names = ex.names
        return nxt

    def landed(ex):
        for k, b in zip(ex.names, ex.results):
            red[k]["others"] = b

    def halves(ex, first_rows):
        parts = []
        for r0, nr in ((0, first_rows), (first_rows, ex.srcs[0].shape[1] - first_rows)):
            part = _ToChips(ex.srcs, rows=(r0, nr))
            part.names = ex.names
            parts.append(part)
        return parts

    def landed_halves(parts):
        for j, k in enumerate(parts[0].names):
            red[k]["others"] = jnp.concatenate([p.results[j] for p in parts], axis=1)

    def update(k, w, m, v, layer=None, fill=None):
        r = red[k]
        return _adamw_sharded(r["grad"], r["recv"], r["others"], place, w, m, v,
                              f"adamw_{k}", layer=layer, fill=fill)

    g_a_in, g_a_out, g_a_norm, g_a_v_norm, g_conv = _exchange_alone(
        _Gather([bf(a_w_in[0]), bf(a_w_out[0]), a_norm, a_v_norm, f_conv_w.reshape(6, FF_SHARD)]), "gather_first")
    a_norm_full, a_v_norm_full = g_a_norm.reshape(1, d), g_a_v_norm.reshape(1, d)
    conv_w = lax.reduce_precision(g_conv.reshape(N_SHARDS, 2, 3, FF_SHARD), 8, 7)
    cw = jnp.pad(jnp.transpose(conv_w, (1, 0, 2, 3)), ((0, 0), (0, 0), (0, 5), (0, 0)))
    w_a_in_flat = jnp.transpose(g_a_in, (1, 0, 2)).reshape(d, 2 * d)
    cb = f_conv_b.reshape(2, N_SHARDS, 1, FF_SHARD)
    tri = jnp.tril(jnp.ones((CHUNK, CHUNK), dtype=bool))
    w_causal = jnp.where(tri[None], a_w_s[0], 0.0).astype(BF16)
    w_causal_t = jnp.transpose(w_causal, (0, 2, 1))
    b_sb = jnp.broadcast_to(a_b_s[0][:, :, None], (N_GROUPS, CHUNK, CHUNK))
    w_a_out = g_a_out.reshape(d, d)
    gq = jnp.tile(b_q_norm.reshape(1, HEAD_DIM), (1, 2))
    gk = jnp.tile(k_norm.reshape(1, HEAD_DIM), (1, 2))
    sinks = b_sinks.reshape(N_Q_HEADS)

    (h1,) = _rms_fwd(x0, [a_norm_full], "a_norm_fwd")
    ex = _Gather([bf(f_w_in[0]), bf(f_w_out[0])])
    zpre, x1 = _sgu_fwd(x0, h1, g_a_in, a_v_norm_full, w_causal, b_sb, w_a_out, carry=ex)
    w_in0, w_out0 = ex.results[0], ex.results[1].reshape(D_FF, d)
    ex = _Gather([bf(w_kv), bf(b_w_q[0]), bf(b_w_o[0]), bf(f_w_in[1])])
    x2, hf0, a0, pre0, hk, hq = _ffn_fwd(x1, f_norm[0:1], w_in0, cw[0], cb[0], w_out0, 0, carry=ex,
                                         next_gains=[row(kv_norm), b_norm])
    kv_full = ex.results[0].reshape(d, 2 * N_KV_HEADS * HEAD_DIM)
    w_q, w_o = ex.results[1].reshape(d, d), ex.results[2].reshape(d, d)
    w_in1 = ex.results[3]
    half = N_KV_HEADS * HEAD_DIM
    w_kv_dup = jnp.concatenate([_dup_heads(kv_full[:, :half]), _dup_heads(kv_full[:, half:])], axis=1)
    kvd = _mm_rows(hk, w_kv_dup, F32, "kv_proj")
    qraw = _mm_rows(hq, w_q, F32, "q_proj")
    ex = _Gather([bf(f_w_out[1])])
    o = _attn_fwd(qraw, kvd, gq, gk, sinks, carry=ex)
    w_out1 = ex.results[0].reshape(D_FF, d)
    x3 = _mm_rows(o, w_o, F32, "o_proj", res=x2)
    _, hf1, a1, pre1, dy, loss_lanes = _ffn_fwd(x3, f_norm[1:2], w_in1, cw[1], cb[1], w_out1, 1, loss_target=target)
    loss = lax.psum(loss_lanes[0, 0], ("x", "y", "c"))

    dhu1, dw_out1, dcb1 = _ffn_bwd_act(pre1, w_out1, dy, 1)
    ex = to_sibling({"f_w_out1": dw_out1.reshape(N_SHARDS, D_FF // N_SHARDS, d)})
    da1, dhf1, dcw1 = _ffn_bwd_in(dhu1, a1, cw[1], w_in1, 1, carry=ex)
    ex = to_chips(ex)
    dw_in1 = _ffn_wgrad_in(hf1, da1, 1, carry=ex)
    landed(ex)
    ex = to_sibling({"f_w_in1": dw_in1})
    dx3, dgf1 = _rms_bwd(x3, [f_norm[1:2]], [dhf1], dy, "f1_norm_bwd", carry=ex)
    ex = to_chips(ex)
    d_o = _mm_rows(dx3, w_o, BF16, "o_proj_bwd", trans_w=True)
    dw_o = _mm_wgrad(o, dx3, "o_wgrad").reshape(N_SHARDS, d // N_SHARDS, d)
    dq, dkv, dsink, dgq, dgk = _attn_bwd(qraw, kvd, d_o, gq, gk, sinks, carry=ex)
    landed(ex)
    dw_q = _mm_wgrad(hq, dq, "q_wgrad").reshape(N_SHARDS, d // N_SHARDS, d)
    dw_kv_dup = _mm_wgrad(hk, dkv, "kv_wgrad")
    dw_kv = jnp.concatenate(
        [_fold_heads(dw_kv_dup[:, :4 * LANES]), _fold_heads(dw_kv_dup[:, 4 * LANES:])], axis=1
    ).reshape(N_SHARDS, d // N_SHARDS, 2 * N_KV_HEADS * HEAD_DIM)
    ex = to_sibling({"b_w_o": dw_o, "b_w_q": dw_q, "w_kv": dw_kv})
    dx2, dg2 = _rms_bwd(x2, [row(kv_norm), b_norm], [dkv, dq], dx3, "kvq_norm_bwd", tm=512, carry=ex,
                        through=[w_kv_dup, w_q])
    ex = to_chips(ex)
    dhu0, dw_out0, dcb0 = _ffn_bwd_act(pre0, w_out0, dx2, 0, carry=ex)
    landed(ex)
    ex = to_sibling({"f_w_out0": dw_out0.reshape(N_SHARDS, D_FF // N_SHARDS, d)})
    da0, dhf0, dcw0 = _ffn_bwd_in(dhu0, a0, cw[0], w_in0, 0, carry=ex)
    ex = to_chips(ex)
    dw_in0 = _ffn_wgrad_in(hf0, da0, 0, carry=ex)
    landed(ex)
    ex = to_sibling({"f_w_in0": dw_in0})
    dx1, dgf0 = _rms_bwd(x1, [f_norm[0:1]], [dhf0], dx2, "f0_norm_bwd", carry=ex)
    ex_lo, ex_hi = halves(to_chips(ex), 448)
    dz, y, dwc, dbs, dgv = _sgu_bwd(dx1, zpre, w_a_out, a_v_norm_full, w_causal, w_causal_t, b_sb, carry=ex_lo)
    dw_a_out = _mm_wgrad(y, dx1, "a_out_wgrad").reshape(N_SHARDS, d // N_SHARDS, d)
    nsub = g_a_in.shape[2]
    dw_a_in = _mm(
        h1, dz, pl.BlockSpec((t, d), lambda s, j, kk: (0, 0)), pl.BlockSpec((t, nsub), lambda s, j, kk: (0, s)),
        pl.BlockSpec((None, d, nsub), lambda s, j, kk: (s, 0, 0)), jax.ShapeDtypeStruct((N_SHARDS, d, nsub), F32),
        (N_SHARDS, 1, 1), TN, "a_in_wgrad", carry=ex_hi)
    landed_halves([ex_lo, ex_hi])

    def bias_grad(dcb):
        return jnp.transpose(dcb[:, :, 0, :], (1, 0, 2)).reshape(-1)

    g_conv_w = jnp.concatenate([dcw0[:, 0:3, :], dcw1[:, 0:3, :]], axis=1)
    g_a_v_norm = dgv[0].reshape(N_SHARDS, 1, LANES)
    rep = ["a_w_s", "a_b_s", "f_norm", "f_conv_b", "kv_norm", "k_norm", "b_norm", "b_q_norm", "b_sinks"]
    rep_g = dict(
        a_w_s=dwc.reshape(N_GROUPS * CHUNK, CHUNK), a_b_s=dbs[:, :, 0], f_norm=jnp.stack([dgf0[0], dgf1[0]]),
        f_conv_b=jnp.stack([bias_grad(dcb0), bias_grad(dcb1)]), kv_norm=dg2[0:1],
        k_norm=(dgk[0, :HEAD_DIM] + dgk[0, HEAD_DIM:])[None], b_norm=dg2[1:2],
        b_q_norm=(dgq[0, :HEAD_DIM] + dgq[0, HEAD_DIM:])[None], b_sinks=dsink[:, 0][None])
    ex_big = to_sibling({"a_w_out": dw_a_out, "a_w_in": dw_a_in})
    ex_small = to_sibling({"a_v_norm": g_a_v_norm, "f_conv_w": g_conv_w}, wire=F32)
    ex_rep = _Gather([rep_g[k] for k in rep])
    together = _Together([ex_big, ex_small, ex_rep])
    dh1 = _mm_rows(dz, w_a_in_flat, F32, "a_in_bwd", trans_w=True, carry=together)
    together.spread()
    ex_big, ex_small = to_chips(ex_big), to_chips(ex_small)
    together = _Together([ex_big, ex_small])
    grad_x, dg0 = _rms_bwd(x0, [a_norm_full], [dh1], dx1, "a_norm_bwd", carry=together)
    together.spread()
    landed(ex_big)
    landed(ex_small)
    (a_norm_parts,) = _exchange_alone(_ToOwners([dg0[0].reshape(N_SHARDS, 1, LANES)]), "a_norm_to_owners")

    res["f_w_out"] = update("f_w_out1", f_w_out, m_f_w_out, v_f_w_out, layer=1)
    w_in_t = [jnp.swapaxes(a_, 1, 2) for a_ in (f_w_in, m_f_w_in, v_f_w_in)]
    res["f_w_in"] = update("f_w_in1", *w_in_t, layer=1)
    res["b_w_o"] = update("b_w_o", b_w_o, m_b_w_o, v_b_w_o, layer=0)
    res["b_w_q"] = update("b_w_q", b_w_q, m_b_w_q, v_b_w_q, layer=0)
    res["w_kv"] = update("w_kv", w_kv, m_w_kv, v_w_kv)
    res["f_w_out"] = update("f_w_out0", f_w_out, m_f_w_out, v_f_w_out, layer=0, fill=res["f_w_out"])
    res["f_w_in"] = [jnp.swapaxes(o_, 1, 2) for o_ in update("f_w_in0", *w_in_t, layer=0, fill=res["f_w_in"])]
    res["a_w_out"] = update("a_w_out", a_w_out, m_a_w_out, v_a_w_out, layer=0)
    res["a_w_in"] = update("a_w_in", a_w_in, m_a_w_in, v_a_w_in, layer=0)
    res["a_v_norm"] = update("a_v_norm", a_v_norm, m_a_v_norm, v_a_v_norm)
    res["f_conv_w"] = [o_.reshape(f_conv_w.shape) for o_ in update(
        "f_conv_w", f_conv_w.reshape(6, FF_SHARD), m_f_conv_w.reshape(6, FF_SHARD), v_f_conv_w.reshape(6, FF_SHARD))]

    rep_w = dict(a_w_s=a_w_s, a_b_s=a_b_s, f_norm=f_norm, f_conv_b=f_conv_b, kv_norm=kv_norm, k_norm=k_norm,
                 b_norm=b_norm, b_q_norm=b_q_norm, b_sinks=b_sinks, a_norm=a_norm)
    rep_m = dict(a_w_s=m_a_w_s, a_b_s=m_a_b_s, f_norm=m_f_norm, f_conv_b=m_f_conv_b, kv_norm=m_kv_norm,
                 k_norm=m_k_norm, b_norm=m_b_norm, b_q_norm=m_b_q_norm, b_sinks=m_b_sinks, a_norm=m_a_norm)
    rep_v = dict(a_w_s=v_a_w_s, a_b_s=v_a_b_s, f_norm=v_f_norm, f_conv_b=v_f_conv_b, kv_norm=v_kv_norm,
                 k_norm=v_k_norm, b_norm=v_b_norm, b_q_norm=v_b_q_norm, b_sinks=v_b_sinks, a_norm=v_a_norm)
    keys = rep + ["a_norm"]
    parts = ex_rep.results + [a_norm_parts]
    as2d = lambda a, p: a.reshape(p.shape[1:])
    rep_outs = _adamw_summed(parts, [as2d(rep_w[k], p) for k, p in zip(keys, parts)],
                             [as2d(rep_m[k], p) for k, p in zip(keys, parts)],
                             [as2d(rep_v[k], p) for k, p in zip(keys, parts)], "adamw_replicated")
    for j, key in enumerate(keys):
        res[key] = [o_.reshape(rep_w[key].shape) for o_ in rep_outs[j]]

    order = ["a_norm", "a_w_in", "a_v_norm", "a_w_s", "a_b_s", "a_w_out", "f_norm", "f_w_in", "f_conv_w", "f_conv_b",
             "f_w_out", "kv_norm", "w_kv", "k_norm", "b_norm", "b_w_q", "b_q_norm", "b_sinks", "b_w_o"]
    outs = [loss, grad_x[None]]
    for j in range(4):
        outs += [res[k][j] for k in order]
    return tuple(outs)
```

```python
import jax
import jax.numpy as jnp
from jax import lax
from jax.experimental import pallas as pl
from jax.experimental.pallas import tpu as pltpu

F32 = jnp.float32
BF16 = jnp.bfloat16
EPS = 1e-6
D_MODEL = 1024
CHUNK = 128
N_GROUPS = 8
N_SHARDS = 8
HEAD_DIM = 64
N_Q_HEADS = 16
N_KV_HEADS = 4
D_FF = 2816
FF_SHARD = 2 * D_FF // N_SHARDS
LANES = 128
NEG_BIG = -1e30
ADAM_LR = 0.001
ADAM_B1 = 0.9
ADAM_B2 = 0.999
ADAM_EPS = 1e-08
ADAM_WD = 0.01
ADAM_STEP = 10
VMEM_LIMIT_BYTES = 56 * 1024 * 1024
MESH = pl.DeviceIdType.MESH

NN = (((1,), (0,)), ((), ()))
NT = (((1,), (1,)), ((), ()))
TN = (((0,), (0,)), ((), ()))
SLOPES = tuple(2.0 ** (-8.0 * (h + 1) / N_Q_HEADS) for h in range(N_Q_HEADS))


def _params(sem=None):
    return pltpu.CompilerParams(dimension_semantics=sem, vmem_limit_bytes=VMEM_LIMIT_BYTES)


def _dot(a, b, dims=NN):
    return lax.dot_general(a, b, dims, preferred_element_type=F32)


def _sigmoid(x):
    return 1.0 / (1.0 + jnp.exp(-x))


def _gelu_parts(z):
    cdf = 0.5 * (1.0 + lax.erf(z * (2.0 ** -0.5)))
    pdf = jnp.exp(-0.5 * z * z) * 0.3989422804014327
    return cdf, pdf


def _coords():
    return lax.axis_index("x"), lax.axis_index("y"), lax.axis_index("c")


class _Gather:
    def __init__(self, srcs, relay=True, early=False):
        self.srcs = list(srcs)
        self.early = early
        n = len(self.srcs)
        self.relayed = [relay and s.shape[0] % 32 == 0 for s in self.srcs]
        self.out_shapes = [jax.ShapeDtypeStruct((N_SHARDS,) + s.shape, s.dtype) for s in self.srcs]
        self.sems = [pltpu.SemaphoreType.DMA((n, 9)), pltpu.SemaphoreType.DMA((n, 9)), pltpu.SemaphoreType.DMA((n,))]

    def _plan(self, src, dst, sems):
        send_sems, recv_sems, local_sems = sems
        x, y, c = _coords()
        n = len(src)

        def rows(e, dev, half=None):
            block = dst[e].at[4 * dev[0] + 2 * dev[1] + dev[2]]
            if half is None:
                return block
            nr = self.srcs[e].shape[0] // 2
            return block.at[pl.ds(half * nr, nr)]

        def copy(e, slot, block, to, half=None, from_own=False):
            return pltpu.make_async_remote_copy(
                src_ref=src[e] if from_own else rows(e, block, half), dst_ref=rows(e, block, half),
                send_sem=send_sems.at[e, slot], recv_sem=recv_sems.at[e, slot], device_id=to, device_id_type=MESH)

        return n, x, y, c, rows, copy, local_sems

    def start(self, src, dst, sems):
        n, x, y, c, rows, copy, local_sems = self._plan(src, dst, sems)
        me = (x, y, c)
        for e in range(n):
            pltpu.make_async_copy(src[e], rows(e, me), local_sems.at[e]).start()
            copy(e, 0, me, (x, y, 1 - c), from_own=True).start()
            copy(e, 1, me, (1 - x, y, c), from_own=True).start()
            copy(e, 2, me, (x, 1 - y, c), from_own=True).start()
            if not self.relayed[e]:
                copy(e, 3, me, (1 - x, 1 - y, c), from_own=True).start()

    def pass_on(self, src, dst, sems, wait=True):
        n, x, y, c, rows, copy, local_sems = self._plan(src, dst, sems)
        me, sibling = (x, y, c), (x, y, 1 - c)
        over_x, over_y, diagonal = (1 - x, y, c), (x, 1 - y, c), (1 - x, 1 - y, c)
        sent = []

        def arrived(cp):
            if wait:
                cp.wait_recv()

        def send(cp):
            if wait:
                cp.start()
            sent.append(cp)

        for slot, owner, onward, half in ((1, over_x, over_y, 0), (2, over_y, over_x, 1)):
            for e in range(n):
                arrived(copy(e, slot, owner, me))
                if self.relayed[e]:
                    send(copy(e, 3 + half, owner, onward, half=half))
                send(copy(e, 4 + slot, owner, sibling))
        for e in range(n):
            if self.relayed[e]:
                for half in (0, 1):
                    arrived(copy(e, 3 + half, diagonal, me, half=half))
                    send(copy(e, 7 + half, diagonal, sibling, half=half))
            else:
                arrived(copy(e, 3, diagonal, me))
                send(copy(e, 7, diagonal, sibling))
        return sent

    def finish(self, src, dst, sems, passed_on=False):
        n, x, y, c, rows, copy, local_sems = self._plan(src, dst, sems)
        me, sibling = (x, y, c), (x, y, 1 - c)
        over_x, over_y, diagonal = (1 - x, y, c), (x, 1 - y, c), (1 - x, 1 - y, c)
        sent = self.pass_on(src, dst, sems, wait=not passed_on)
        for e in range(n):
            copy(e, 0, sibling, me).wait_recv()
            copy(e, 5, (1 - x, y, 1 - c), me).wait_recv()
            copy(e, 6, (x, 1 - y, 1 - c), me).wait_recv()
            if self.relayed[e]:
                for half in (0, 1):
                    copy(e, 7 + half, (1 - x, 1 - y, 1 - c), me, half=half).wait_recv()
            else:
                copy(e, 7, (1 - x, 1 - y, 1 - c), me).wait_recv()
        for e in range(n):
            copy(e, 0, me, sibling, from_own=True).wait_send()
            copy(e, 1, me, over_x, from_own=True).wait_send()
            copy(e, 2, me, over_y, from_own=True).wait_send()
            if not self.relayed[e]:
                copy(e, 3, me, diagonal, from_own=True).wait_send()
            pltpu.make_async_copy(src[e], rows(e, me), local_sems.at[e]).wait()
        for cp in sent:
            cp.wait_send()


class _ToSibling:
    def __init__(self, grads):
        self.srcs = list(grads)
        n = len(self.srcs)
        self.out_shapes = [jax.ShapeDtypeStruct((4,) + g.shape[1:], g.dtype) for g in self.srcs]
        self.sems = [pltpu.SemaphoreType.DMA((n, 4)), pltpu.SemaphoreType.DMA((n, 4))]

    def _copies(self, src, dst, sems):
        send_sems, recv_sems = sems
        x, y, c = _coords()
        return [
            pltpu.make_async_remote_copy(
                src_ref=src[i].at[2 * q + (1 - c)], dst_ref=dst[i].at[q], send_sem=send_sems.at[i, q],
                recv_sem=recv_sems.at[i, q], device_id=(x, y, 1 - c), device_id_type=MESH)
            for i in range(len(src)) for q in range(4)]

    def start(self, src, dst, sems):
        for cp in self._copies(src, dst, sems):
            cp.start()

    def finish(self, src, dst, sems):
        for cp in self._copies(src, dst, sems):
            cp.wait()


class _ToChips:
    def __init__(self, psums, rows=None):
        self.srcs = list(psums)
        n = len(self.srcs)
        self.rows = rows
        self.out_shapes = [
            jax.ShapeDtypeStruct((3, p.shape[1] if rows is None else rows[1]) + p.shape[2:], p.dtype)
            for p in self.srcs]
        self.sems = [pltpu.SemaphoreType.DMA((n, 3)), pltpu.SemaphoreType.DMA((n, 3))]

    def _copies(self, src, dst, sems):
        send_sems, recv_sems = sems
        x, y, c = _coords()
        peers = [(x, 1 - y), (1 - x, y), (1 - x, 1 - y)]

        def part(i, q):
            if self.rows is None:
                return src[i].at[q]
            return src[i].at[q, pl.ds(self.rows[0], self.rows[1])]

        return [
            pltpu.make_async_remote_copy(
                src_ref=part(i, 2 * px + py), dst_ref=dst[i].at[r], send_sem=send_sems.at[i, r],
                recv_sem=recv_sems.at[i, r], device_id=(px, py, c), device_id_type=MESH)
            for i in range(len(src)) for r, (px, py) in enumerate(peers)]

    def start(self, src, dst, sems):
        for cp in self._copies(src, dst, sems):
            cp.start()

    def finish(self, src, dst, sems):
        for cp in self._copies(src, dst, sems):
            cp.wait()


class _ToOwners:
    def __init__(self, grads):
        self.srcs = list(grads)
        n = len(self.srcs)
        self.out_shapes = [jax.ShapeDtypeStruct(g.shape, g.dtype) for g in self.srcs]
        self.sems = [pltpu.SemaphoreType.DMA((n, 7)), pltpu.SemaphoreType.DMA((n, 7)), pltpu.SemaphoreType.DMA((n,))]

    def _copies(self, src, dst, sems):
        send_sems, recv_sems, local_sems = sems
        x, y, c = _coords()
        me = 4 * x + 2 * y + c
        copies = [pltpu.make_async_copy(src[i].at[me], dst[i].at[me], local_sems.at[i]) for i in range(len(src))]
        for i in range(len(src)):
            for rel in range(1, N_SHARDS):
                px = x ^ (rel >> 2) if rel >> 2 else x
                py = y ^ ((rel >> 1) & 1) if (rel >> 1) & 1 else y
                pc = c ^ (rel & 1) if rel & 1 else c
                copies.append(pltpu.make_async_remote_copy(
                    src_ref=src[i].at[4 * px + 2 * py + pc], dst_ref=dst[i].at[me], send_sem=send_sems.at[i, rel - 1],
                    recv_sem=recv_sems.at[i, rel - 1], device_id=(px, py, pc), device_id_type=MESH))
        return copies

    def start(self, src, dst, sems):
        for cp in self._copies(src, dst, sems):
            cp.start()

    def finish(self, src, dst, sems):
        for cp in self._copies(src, dst, sems):
            cp.wait()


class _Together:
    def __init__(self, parts):
        self.parts = list(parts)
        self.srcs = [s for p in self.parts for s in p.srcs]
        self.out_shapes = [s for p in self.parts for s in p.out_shapes]
        self.sems = [s for p in self.parts for s in p.sems]

    def _split(self, src, dst, sems):
        a = b = c = 0
        for p in self.parts:
            na, nc = len(p.srcs), len(p.sems)
            yield p, src[a:a + na], dst[b:b + na], sems[c:c + nc]
            a, b, c = a + na, b + na, c + nc

    def start(self, src, dst, sems):
        for p, s, d, m in self._split(src, dst, sems):
            p.start(s, d, m)

    def finish(self, src, dst, sems):
        for p, s, d, m in self._split(src, dst, sems):
            p.finish(s, d, m)

    def spread(self):
        b = 0
        for p in self.parts:
            p.results = self.results[b:b + len(p.srcs)]
            b += len(p.srcs)


def _call(body, args, *, grid, in_specs, out_specs, out_shape, name, scratch=(), sem=None, carry=None):
    out_shape, out_specs = list(out_shape), list(out_specs)
    if carry is None:
        return pl.pallas_call(
            body, grid=grid, in_specs=list(in_specs), out_specs=out_specs, out_shape=out_shape,
            scratch_shapes=list(scratch), name=name, compiler_params=_params(sem))(*args)
    n_in, n_out, n_scr, n_c = len(args), len(out_shape), len(scratch), len(carry.srcs)
    steps = tuple(grid)
    total = 1
    for n_ax in steps:
        total *= n_ax
    early = getattr(carry, "early", False) and total >= 8
    early_step = total - max(2, total // 8)

    def carried(*refs):
        ins, rest = refs[:n_in], refs[n_in:]
        c_src, rest = rest[:n_c], rest[n_c:]
        outs, rest = rest[:n_out], rest[n_out:]
        c_dst, rest = rest[:n_c], rest[n_c:]
        scr, sems = rest[:n_scr], rest[n_scr:]
        step = pl.program_id(0)
        for ax in range(1, len(steps)):
            step = step * steps[ax] + pl.program_id(ax)

        @pl.when(step == 0)
        def _():
            carry.start(c_src, c_dst, sems)

        body(*ins, *outs, *scr)

        if early:
            @pl.when(step == early_step)
            def _():
                carry.pass_on(c_src, c_dst, sems)

        @pl.when(step == total - 1)
        def _():
            if early:
                carry.finish(c_src, c_dst, sems, passed_on=True)
            else:
                carry.finish(c_src, c_dst, sems)

    hbm = pl.BlockSpec(memory_space=pl.ANY)
    res = pl.pallas_call(
        carried, grid=grid, in_specs=list(in_specs) + [hbm] * n_c, out_specs=out_specs + [hbm] * n_c,
        out_shape=out_shape + carry.out_shapes, scratch_shapes=list(scratch) + carry.sems, name=name,
        compiler_params=_params(("arbitrary",) * len(steps)))(*args, *carry.srcs)
    carry.results = list(res[n_out:])
    return list(res[:n_out])


def _exchange_alone(ex, name):
    n = len(ex.srcs)

    def body(*refs):
        src, dst, sems = refs[:n], refs[n:2 * n], refs[2 * n:]
        ex.start(src, dst, sems)
        ex.finish(src, dst, sems)

    hbm = pl.BlockSpec(memory_space=pl.ANY)
    res = pl.pallas_call(body, in_specs=[hbm] * n, out_specs=[hbm] * n, out_shape=ex.out_shapes,
                         scratch_shapes=ex.sems, name=name)(*ex.srcs)
    ex.results = list(res)
    return ex.results


def _rms_fwd(x, gains, name, tm=512, carry=None):
    t, d = x.shape
    n = len(gains)

    def body(*refs):
        x_ref, g_refs, h_refs = refs[0], refs[1:1 + n], refs[1 + n:]
        xf = x_ref[...]
        xhat = xf * lax.rsqrt(jnp.mean(xf * xf, axis=-1, keepdims=True) + EPS)
        for g_ref, h_ref in zip(g_refs, h_refs):
            h_ref[...] = (xhat * g_ref[...]).astype(BF16)

    row = pl.BlockSpec((tm, d), lambda i: (i, 0))
    vec = pl.BlockSpec((1, d), lambda i: (0, 0))
    return _call(body, [x, *gains], grid=(t // tm,), in_specs=[row] + [vec] * n, out_specs=[row] * n,
                 out_shape=[jax.ShapeDtypeStruct((t, d), BF16)] * n, name=name, carry=carry)


def _rms_bwd(x, gains, dhs, dres, name, tm=256, carry=None, through=None):
    t, d = x.shape
    n = len(gains)
    n_w = 0 if through is None else n

    def body(*refs):
        x_ref, dres_ref = refs[0], refs[1]
        g_refs, dh_refs, w_refs = refs[2:2 + n], refs[2 + n:2 + 2 * n], refs[2 + 2 * n:2 + 2 * n + n_w]
        dx_ref, dg_ref = refs[2 + 2 * n + n_w], refs[3 + 2 * n + n_w]
        i = pl.program_id(0)

        @pl.when(i == 0)
        def _():
            dg_ref[...] = jnp.zeros_like(dg_ref)

        xf = x_ref[...]
        r = lax.rsqrt(jnp.mean(xf * xf, axis=-1, keepdims=True) + EPS)
        xhat = xf * r
        dx = dres_ref[...]
        for j in range(n):
            dh = dh_refs[j][...]
            if n_w:
                dh = _dot(dh.astype(BF16), w_refs[j][...], NT)
            dg_ref[j:j + 1, :] += jnp.sum(dh * xhat, axis=0, keepdims=True)
            gy = dh * g_refs[j][...]
            dx = dx + r * (gy - xhat * jnp.mean(gy * xhat, axis=-1, keepdims=True))
        dx_ref[...] = dx

    row = pl.BlockSpec((tm, d), lambda i: (i, 0))
    vec = pl.BlockSpec((1, d), lambda i: (0, 0))
    dh_rows = [pl.BlockSpec((tm, dh.shape[1]), lambda i: (i, 0)) for dh in dhs]
    w_full = [] if through is None else [pl.BlockSpec(w.shape, lambda i: (0, 0)) for w in through]
    return _call(body, [x, dres, *gains, *dhs, *(through or [])], grid=(t // tm,),
                 in_specs=[row, row] + [vec] * n + dh_rows + w_full,
                 out_specs=[row, pl.BlockSpec((8, d), lambda i: (0, 0))],
                 out_shape=[jax.ShapeDtypeStruct((t, d), F32), jax.ShapeDtypeStruct((8, d), F32)],
                 name=name, sem=("arbitrary",), carry=carry)


def _mm(a, b, a_spec, b_spec, o_spec, out_shape, grid, dims, name, res=None, res_spec=None, carry=None):
    nk = grid[2]
    acc_shape = tuple(s for s in o_spec.block_shape if s is not None)

    def body(*refs):
        a_ref, b_ref = refs[0], refs[1]
        r_ref = refs[2] if res is not None else None
        o_ref = refs[3] if res is not None else refs[2]
        p = _dot(a_ref[...].astype(BF16), b_ref[...].astype(BF16), dims)
        if nk == 1:
            if res is not None:
                p = p + r_ref[...]
            o_ref[...] = p.astype(o_ref.dtype)
            return
        acc_ref = refs[-1]
        k = pl.program_id(2)

        @pl.when(k == 0)
        def _():
            acc_ref[...] = p

        @pl.when(k > 0)
        def _():
            acc_ref[...] += p

        @pl.when(k == nk - 1)
        def _():
            out = acc_ref[...]
            if res is not None:
                out = out + r_ref[...]
            o_ref[...] = out.astype(o_ref.dtype)

    ins = [a, b] + ([res] if res is not None else [])
    specs = [a_spec, b_spec] + ([res_spec] if res is not None else [])
    return _call(body, ins, grid=grid, in_specs=specs, out_specs=[o_spec], out_shape=[out_shape],
                 scratch=[pltpu.VMEM(acc_shape, F32)] if nk > 1 else [], name=name,
                 sem=("parallel", "parallel", "arbitrary"), carry=carry)[0]


def _mm_rows(a, w, out_dtype, name, trans_w=False, res=None, tm=512, carry=None):
    t, k = a.shape
    n = w.shape[0] if trans_w else w.shape[1]
    return _mm(
        a, w, pl.BlockSpec((tm, k), lambda i, j, kk: (i, 0)), pl.BlockSpec(w.shape, lambda i, j, kk: (0, 0)),
        pl.BlockSpec((tm, n), lambda i, j, kk: (i, 0)), jax.ShapeDtypeStruct((t, n), out_dtype), (t // tm, 1, 1),
        NT if trans_w else NN, name, res=res,
        res_spec=None if res is None else pl.BlockSpec((tm, n), lambda i, j, kk: (i, 0)), carry=carry)


def _mm_wgrad(a, b, name, carry=None):
    t, m = a.shape
    n = b.shape[1]
    tn = n // (4 if b.dtype == F32 else 2)
    return _mm(
        a, b, pl.BlockSpec((t, m), lambda i, j, kk: (0, 0)), pl.BlockSpec((t, tn), lambda i, j, kk: (0, j)),
        pl.BlockSpec((m, tn), lambda i, j, kk: (0, j)), jax.ShapeDtypeStruct((m, n), F32), (1, n // tn, 1), TN, name,
        carry=carry)


def _sgu_fwd(x0, h1, w_in, g_v, w_c, b_sb, w_out, tm=256, carry=None):
    t, d = x0.shape
    nsub = w_in.shape[2]

    def body(x_ref, h_ref, win_ref, gv_ref, wc_ref, bsb_ref, wout_ref, zpre_ref, x1_ref, u_s, v_s, vn_s, y_s):
        h = h_ref[...]
        for k in range(N_SHARDS):
            zk = _dot(h, win_ref[k])
            zpre_ref[:, k * nsub:(k + 1) * nsub] = zk
            cdf, _ = _gelu_parts(zk)
            if k < N_SHARDS // 2:
                u_s[:, k * nsub:(k + 1) * nsub] = zk * cdf
            else:
                v_s[:, (k - 4) * nsub:(k - 3) * nsub] = zk * cdf
        v = v_s[...]
        rv = lax.rsqrt(jnp.mean(v * v, axis=-1, keepdims=True) + EPS)
        vn_s[...] = (v * rv * gv_ref[...]).astype(BF16)
        for ci in range(tm // CHUNK):
            rows = slice(ci * CHUNK, (ci + 1) * CHUNK)
            for g in range(N_GROUPS):
                cols = slice(g * LANES, (g + 1) * LANES)
                sv = _dot(wc_ref[g], vn_s[rows, cols]) + bsb_ref[g]
                y_s[rows, cols] = (u_s[rows, cols] * sv).astype(BF16)
        x1_ref[...] = x_ref[...] + _dot(y_s[...], wout_ref[...])

    row = pl.BlockSpec((tm, d), lambda i: (i, 0))
    full = lambda a: pl.BlockSpec(a.shape, lambda i: (0,) * a.ndim)
    return _call(
        body, [x0, h1, w_in, g_v, w_c, b_sb, w_out], grid=(t // tm,),
        in_specs=[row, row, full(w_in), full(g_v), full(w_c), full(b_sb), full(w_out)],
        out_specs=[pl.BlockSpec((tm, 2 * d), lambda i: (i, 0)), row],
        out_shape=[jax.ShapeDtypeStruct((t, 2 * d), F32), jax.ShapeDtypeStruct((t, d), F32)],
        scratch=[pltpu.VMEM((tm, d), F32), pltpu.VMEM((tm, d), F32), pltpu.VMEM((tm, d), BF16),
                 pltpu.VMEM((tm, d), BF16)],
        name="sgu_fwd", carry=carry)


def _sgu_bwd(dx1, zpre, w_out, g_v, w_c, w_ct, b_sb, tm=256, carry=None):
    t, d = dx1.shape

    def body(dx_ref, zpre_ref, wout_ref, gv_ref, wc_ref, wct_ref, bsb_ref,
             dz_ref, y_ref, dwc_ref, dbs_ref, dgv_ref, u_s, vn_s, dy_s, du_s, dvn_s):
        i = pl.program_id(0)

        @pl.when(i == 0)
        def _():
            dwc_ref[...] = jnp.zeros_like(dwc_ref)
            dbs_ref[...] = jnp.zeros_like(dbs_ref)
            dgv_ref[...] = jnp.zeros_like(dgv_ref)

        dy_s[...] = _dot(dx_ref[...].astype(BF16), wout_ref[...], NT)
        zu = zpre_ref[:, :d]
        zv = zpre_ref[:, d:]
        cdf_u, pdf_u = _gelu_parts(zu)
        cdf_v, pdf_v = _gelu_parts(zv)
        u_s[...] = zu * cdf_u
        v = zv * cdf_v
        rv = lax.rsqrt(jnp.mean(v * v, axis=-1, keepdims=True) + EPS)
        vhat = v * rv
        gv = gv_ref[...]
        vn_s[...] = (vhat * gv).astype(BF16)
        for ci in range(tm // CHUNK):
            rows = slice(ci * CHUNK, (ci + 1) * CHUNK)
            for g in range(N_GROUPS):
                cols = slice(g * LANES, (g + 1) * LANES)
                vnb = vn_s[rows, cols]
                sv = _dot(wc_ref[g], vnb) + bsb_ref[g]
                dyb = dy_s[rows, cols]
                ub = u_s[rows, cols]
                dsv = dyb * ub
                du_s[rows, cols] = dyb * sv
                y_ref[rows, cols] = (ub * sv).astype(BF16)
                dsvb = dsv.astype(BF16)
                dbs_ref[g] += dsv
                dwc_ref[g] += _dot(dsvb, vnb, NT)
                dvn_s[rows, cols] = _dot(wct_ref[g], dsvb)
        dvn = dvn_s[...]
        dgv_ref[0:1, :] += jnp.sum(dvn * vhat, axis=0, keepdims=True)
        gy = dvn * gv
        dv = rv * (gy - vhat * jnp.mean(gy * vhat, axis=-1, keepdims=True))
        dz_ref[:, :d] = (du_s[...] * (cdf_u + zu * pdf_u)).astype(BF16)
        dz_ref[:, d:] = (dv * (cdf_v + zv * pdf_v)).astype(BF16)

        @pl.when(i == t // tm - 1)
        def _():
            tri = (lax.broadcasted_iota(jnp.int32, (CHUNK, CHUNK), 0)
                   >= lax.broadcasted_iota(jnp.int32, (CHUNK, CHUNK), 1))
            for g in range(N_GROUPS):
                dwc_ref[g] = jnp.where(tri, dwc_ref[g], 0.0)
                dbs_ref[g] = jnp.broadcast_to(jnp.sum(dbs_ref[g], axis=1, keepdims=True), (CHUNK, CHUNK))

    row = pl.BlockSpec((tm, d), lambda i: (i, 0))
    row2 = pl.BlockSpec((tm, 2 * d), lambda i: (i, 0))
    full = lambda a: pl.BlockSpec(a.shape, lambda i: (0,) * a.ndim)
    grp = pl.BlockSpec((N_GROUPS, CHUNK, CHUNK), lambda i: (0, 0, 0))
    return _call(
        body, [dx1, zpre, w_out, g_v, w_c, w_ct, b_sb], grid=(t // tm,),
        in_specs=[row, row2, full(w_out), full(g_v), full(w_c), full(w_ct), full(b_sb)],
        out_specs=[row2, row, grp, grp, pl.BlockSpec((8, d), lambda i: (0, 0))],
        out_shape=[jax.ShapeDtypeStruct((t, 2 * d), BF16), jax.ShapeDtypeStruct((t, d), BF16),
                   jax.ShapeDtypeStruct((N_GROUPS, CHUNK, CHUNK), F32),
                   jax.ShapeDtypeStruct((N_GROUPS, CHUNK, CHUNK), F32), jax.ShapeDtypeStruct((8, d), F32)],
        scratch=[pltpu.VMEM((tm, d), F32), pltpu.VMEM((tm, d), BF16), pltpu.VMEM((tm, d), F32),
                 pltpu.VMEM((tm, d), F32), pltpu.VMEM((tm, d), F32)],
        name="sgu_bwd", sem=("arbitrary",), carry=carry)


ROW_CHUNK = 256
HALO = 16


def _ffn_fwd(x, g, w_in, cw, cb, w_out, layer, tm=512, carry=None, next_gains=(), loss_target=None):
    t, d = x.shape
    nc = N_SHARDS // 2
    n_gains = len(next_gains)
    with_loss = loss_target is not None

    def body(x_ref, xp_ref, g_ref, wg_ref, wu_ref, cwg_ref, cbg_ref, cwu_ref, cbu_ref, wout_ref, *rest):
        extra_in, rest = rest[:n_gains + with_loss], rest[n_gains + with_loss:]
        o_ref, hf_ref, a_ref, pre_ref = rest[:4]
        extra_out, hw_s = rest[4:-1], rest[-1]
        i, c = pl.program_id(0), pl.program_id(1)

        @pl.when(c == 0)
        def _():
            keep = jnp.where(i == 0, 0.0, 1.0)
            xw = jnp.concatenate([xp_ref[...] * keep, x_ref[...]], axis=0)
            xhat = xw * lax.rsqrt(jnp.mean(xw * xw, axis=-1, keepdims=True) + EPS)
            hw_s[...] = (xhat * g_ref[...]).astype(BF16)
            hf_ref[...] = hw_s[HALO:, :]
            o_ref[...] = x_ref[...]

        hw = hw_s[...]
        pre = []
        for j, (w_ref, cw_ref, cb_ref) in enumerate(((wg_ref, cwg_ref, cbg_ref), (wu_ref, cwu_ref, cbu_ref))):
            ab = _dot(hw, w_ref[...]).astype(BF16)
            a_ref[j] = ab[HALO:]
            win = ab.astype(F32)
            cw_v = cw_ref[...]
            pre.append(cw_v[2:3, :] * win[HALO:] + cw_v[1:2, :] * pltpu.roll(win, 1, 0)[HALO:]
                       + cw_v[0:1, :] * pltpu.roll(win, 2, 0)[HALO:] + cb_ref[...])
            pre_ref[j] = pre[j]
        act = (pre[0] * _sigmoid(pre[0]) * pre[1]).astype(BF16)
        o_ref[...] += _dot(act, wout_ref[...])

        if with_loss:
            @pl.when((i == 0) & (c == 0))
            def _():
                extra_out[-1][...] = jnp.zeros_like(extra_out[-1])

        @pl.when(c == nc - 1)
        def _():
            xn = o_ref[...]
            if n_gains:
                xhat = xn * lax.rsqrt(jnp.mean(xn * xn, axis=-1, keepdims=True) + EPS)
                for k in range(n_gains):
                    extra_out[k][...] = (xhat * extra_in[k][...]).astype(BF16)
            if with_loss:
                err = xn - extra_in[-1][...]
                extra_out[-2][...] = err * (1.0 / d)
                part = jnp.sum(jnp.sum(err * err, axis=0, keepdims=True), axis=1, keepdims=True)
                extra_out[-1][...] += jnp.broadcast_to(0.5 / d * part, extra_out[-1].shape)

    row = pl.BlockSpec((tm, d), lambda i, c: (i, 0))
    vec = pl.BlockSpec((1, d), lambda i, c: (0, 0))
    shard = lambda rows, up: pl.BlockSpec((None, rows, FF_SHARD), lambda i, c: (c + up * nc, 0, 0))
    pair = pl.BlockSpec((2, None, tm, FF_SHARD), lambda i, c: (0, c, i, 0))
    lanes = pl.BlockSpec((8, LANES), lambda i, c: (0, 0))
    outs = _call(
        body, [x, x, g, w_in, w_in, cw, cb, cw, cb, w_out, *next_gains] + ([loss_target] if with_loss else []),
        grid=(t // tm, nc),
        in_specs=[row, pl.BlockSpec((HALO, d), lambda i, c: (jnp.maximum(i * (tm // HALO) - 1, 0), 0)),
                  vec, shard(d, 0), shard(d, 1), shard(8, 0), shard(1, 0), shard(8, 1), shard(1, 1),
                  pl.BlockSpec((FF_SHARD, d), lambda i, c: (c, 0))] + [vec] * n_gains + [row] * with_loss,
        out_specs=[row, row, pair, pair] + [row] * n_gains + [row, lanes] * with_loss,
        out_shape=[jax.ShapeDtypeStruct((t, d), F32), jax.ShapeDtypeStruct((t, d), BF16),
                   jax.ShapeDtypeStruct((2, nc, t, FF_SHARD), BF16), jax.ShapeDtypeStruct((2, nc, t, FF_SHARD), F32)]
        + [jax.ShapeDtypeStruct((t, d), BF16)] * n_gains
        + [jax.ShapeDtypeStruct((t, d), F32), jax.ShapeDtypeStruct((8, LANES), F32)] * with_loss,
        scratch=[pltpu.VMEM((tm + HALO, d), BF16)], name=f"ffn{layer}_fwd", sem=("arbitrary", "arbitrary"), carry=carry)
    return (outs[0], outs[1], outs[2].reshape(N_SHARDS, t, FF_SHARD), outs[3]) + tuple(outs[4:])


def _ffn_bwd_act(pre, w_out, dxn, layer, tm=512, carry=None):
    t, d = dxn.shape
    nc = N_SHARDS // 2

    def body(pre_ref, wout_ref, dx_ref, dhu_ref, dw_ref, dcb_ref):
        i = pl.program_id(1)

        @pl.when(i == 0)
        def _():
            dw_ref[...] = jnp.zeros_like(dw_ref)
            dcb_ref[...] = jnp.zeros_like(dcb_ref)

        hg, hu = pre_ref[0], pre_ref[1]
        sg = _sigmoid(hg)
        sl = hg * sg
        dxb = dx_ref[...].astype(BF16)
        dact = _dot(dxb, wout_ref[...], NT)
        dw_ref[...] += _dot((sl * hu).astype(BF16), dxb, TN)
        d_up = dact * sl
        d_gate = dact * hu * (sg * (1.0 + hg * (1.0 - sg)))
        for j, dv in enumerate((d_gate, d_up)):
            dhu_ref[j] = dv.astype(BF16)
            dcb_ref[j, 0:1, :] += jnp.sum(dv, axis=0, keepdims=True)

    return _call(
        body, [pre, w_out, dxn], grid=(nc, t // tm),
        in_specs=[pl.BlockSpec((2, None, tm, FF_SHARD), lambda c, i: (0, c, i, 0)),
                  pl.BlockSpec((FF_SHARD, d), lambda c, i: (c, 0)), pl.BlockSpec((tm, d), lambda c, i: (i, 0))],
        out_specs=[pl.BlockSpec((None, 2, tm, FF_SHARD), lambda c, i: (c, 0, i, 0)),
                   pl.BlockSpec((FF_SHARD, d), lambda c, i: (c, 0)),
                   pl.BlockSpec((None, 2, 8, FF_SHARD), lambda c, i: (c, 0, 0, 0))],
        out_shape=[jax.ShapeDtypeStruct((nc, 2, t, FF_SHARD), BF16), jax.ShapeDtypeStruct((D_FF, d), F32),
                   jax.ShapeDtypeStruct((nc, 2, 8, FF_SHARD), F32)],
        name=f"ffn{layer}_bwd_act", sem=("parallel", "arbitrary"), carry=carry)


def _ffn_bwd_in(dhu, a, cw, w_in, layer, tm=1024, carry=None):
    nc, _, t, _ = dhu.shape
    d = D_MODEL
    tm = min(tm, t)
    last_blk = t // 16 - 1

    def body(dh_ref, nx_ref, a_ref, cw_ref, win_ref, da_ref, o_ref, dcw_ref):
        i, s = pl.program_id(0), pl.program_id(1)

        @pl.when(s == 0)
        def _():
            o_ref[...] = jnp.zeros_like(o_ref)

        @pl.when((s == 0) & (i == 0))
        def _():
            dcw_ref[...] = jnp.zeros_like(dcw_ref)

        keep = jnp.where(i == t // tm - 1, 0.0, 1.0)
        cw = cw_ref[...]
        sums = [None] * 3
        for r0 in range(0, tm, ROW_CHUNK):
            rows = slice(r0, r0 + ROW_CHUNK)
            if r0 + ROW_CHUNK == tm:
                win = jnp.concatenate([dh_ref[rows, :].astype(F32), nx_ref[...].astype(F32) * keep], axis=0)
            else:
                win = dh_ref[r0:r0 + ROW_CHUNK + HALO, :].astype(F32)
            n = ROW_CHUNK + HALO
            taps = (pltpu.roll(win, n - 2, 0)[:ROW_CHUNK],
                    pltpu.roll(win, n - 1, 0)[:ROW_CHUNK],
                    win[:ROW_CHUNK])
            da = (cw[0:1, :] * taps[0] + cw[1:2, :] * taps[1] + cw[2:3, :] * taps[2]).astype(BF16)
            da_ref[rows, :] = da
            o_ref[rows, :] += _dot(da, win_ref[...], NT)
            af = a_ref[rows, :].astype(F32)
            parts = [jnp.sum(taps[k] * af, axis=0, keepdims=True) for k in range(3)]
            sums = [p if q is None else q + p for q, p in zip(sums, parts)]
        for k in range(3):
            dcw_ref[pl.ds(s, 1), k:k + 1, :] += sums[k][None]

    return _call(
        body, [dhu, dhu, a, cw, w_in], grid=(t // tm, N_SHARDS),
        in_specs=[pl.BlockSpec((None, None, tm, FF_SHARD), lambda i, s: (s % nc, s // nc, i, 0)),
                  pl.BlockSpec((None, None, 16, FF_SHARD),
                               lambda i, s: (s % nc, s // nc, jnp.minimum((i + 1) * (tm // 16), last_blk), 0)),
                  pl.BlockSpec((None, tm, FF_SHARD), lambda i, s: (s, i, 0)),
                  pl.BlockSpec((None, 8, FF_SHARD), lambda i, s: (s, 0, 0)),
                  pl.BlockSpec((None, d, FF_SHARD), lambda i, s: (s, 0, 0))],
        out_specs=[pl.BlockSpec((None, tm, FF_SHARD), lambda i, s: (s, i, 0)),
                   pl.BlockSpec((tm, d), lambda i, s: (i, 0)),
                   pl.BlockSpec((N_SHARDS, 8, FF_SHARD), lambda i, s: (0, 0, 0))],
        out_shape=[jax.ShapeDtypeStruct((N_SHARDS, t, FF_SHARD), BF16), jax.ShapeDtypeStruct((t, d), F32),
                   jax.ShapeDtypeStruct((N_SHARDS, 8, FF_SHARD), F32)],
        name=f"ffn{layer}_bwd_in", sem=("arbitrary", "arbitrary"), carry=carry)


def _ffn_wgrad_in(hf, da, layer, carry=None):
    t, d = hf.shape
    return _mm(
        da, hf, pl.BlockSpec((None, t, FF_SHARD), lambda s, j, kk: (s, 0, 0)),
        pl.BlockSpec((t, d), lambda s, j, kk: (0, 0)),
        pl.BlockSpec((None, FF_SHARD, d), lambda s, j, kk: (s, 0, 0)),
        jax.ShapeDtypeStruct((N_SHARDS, FF_SHARD, d), F32), (N_SHARDS, 1, 1), TN, f"ffn{layer}_wgrad_in",
        carry=carry)


Q_PER_KV = N_Q_HEADS // N_KV_HEADS
GROUP_ROWS = Q_PER_KV * CHUNK


def _attn_masks(n):
    lane = lax.broadcasted_iota(jnp.int32, (CHUNK, LANES), 1)
    lo = lane < HEAD_DIM
    tq = lax.broadcasted_iota(jnp.int32, (GROUP_ROWS, 2 * CHUNK), 0) & (CHUNK - 1)
    jk = lax.broadcasted_iota(jnp.int32, (GROUP_ROWS, 2 * CHUNK), 1)
    dist = tq + CHUNK - jk
    mask = (dist >= 0) & (dist < CHUNK) & (jk >= jnp.where(n == 0, CHUNK, 0))
    return lo, mask, dist.astype(F32)


def _per_head_column(values):
    r = lax.broadcasted_iota(jnp.int32, (GROUP_ROWS, 1), 0)
    col = jnp.full((GROUP_ROWS, 1), values[Q_PER_KV - 1], F32)
    for j in range(Q_PER_KV - 2, -1, -1):
        col = jnp.where(r < (j + 1) * CHUNK, values[j], col)
    return col


def _half_sum(x, lo):
    s_lo = jnp.sum(jnp.where(lo, x, 0.0), axis=-1, keepdims=True)
    s_hi = jnp.sum(jnp.where(lo, 0.0, x), axis=-1, keepdims=True)
    return jnp.where(lo, s_lo, s_hi)


def _stack_heads(pairs, lo):
    zero = jnp.zeros_like(pairs[0])
    return jnp.concatenate([jnp.where(lo, pairs[0], zero), jnp.where(lo, zero, pairs[0]),
                            jnp.where(lo, pairs[1], zero), jnp.where(lo, zero, pairs[1])], axis=0)


def _unstack_heads(stacked, lo):
    return (jnp.where(lo, stacked[0:CHUNK], stacked[CHUNK:2 * CHUNK]),
            jnp.where(lo, stacked[2 * CHUNK:3 * CHUNK], stacked[3 * CHUNK:]))


def _attn_probs(qs, kn, mask, distf, slope_col, sink_col):
    s = _dot(qs, kn, NT) * (HEAD_DIM ** -0.5)
    s = jnp.where(mask, s - slope_col * distf, NEG_BIG)
    m = jnp.maximum(jnp.max(s, axis=-1, keepdims=True), sink_col)
    e = jnp.exp(s - m)
    den = jnp.sum(e, axis=-1, keepdims=True) + jnp.exp(sink_col - m)
    return e * (1.0 / den), m, den


def _attn_fwd(qraw, kvd, gq, gk, sinks, carry=None):
    t, d = qraw.shape
    nb = t // CHUNK

    def body(sink_ref, q_ref, cur_ref, prev_ref, gq_ref, gk_ref, o_ref):
        n = pl.program_id(0)
        lo, mask, distf = _attn_masks(n)
        gq_v, gk_v = gq_ref[...], gk_ref[...]
        for kvh in range(N_KV_HEADS):
            ks = slice(kvh * LANES, (kvh + 1) * LANES)
            vs = slice(4 * LANES + kvh * LANES, 4 * LANES + (kvh + 1) * LANES)
            kraw = jnp.concatenate([prev_ref[:, ks], cur_ref[:, ks]], axis=0)
            rk = lax.rsqrt(jnp.mean(kraw * kraw, axis=-1, keepdims=True) + EPS)
            kn = (kraw * rk * gk_v).astype(BF16)
            vv = jnp.concatenate([prev_ref[:, vs], cur_ref[:, vs]], axis=0).astype(BF16)
            qn = []
            for p in range(2):
                qp = q_ref[:, (2 * kvh + p) * LANES:(2 * kvh + p + 1) * LANES]
                r = lax.rsqrt(_half_sum(qp * qp, lo) * (1.0 / HEAD_DIM) + EPS)
                qn.append(qp * r * gq_v)
            heads = range(Q_PER_KV * kvh, Q_PER_KV * (kvh + 1))
            pf, _, _ = _attn_probs(_stack_heads(qn, lo).astype(BF16), kn, mask, distf,
                                   _per_head_column([SLOPES[h] for h in heads]),
                                   _per_head_column([sink_ref[h] for h in heads]))
            for p, o_pair in enumerate(_unstack_heads(_dot(pf.astype(BF16), vv), lo)):
                o_ref[:, (2 * kvh + p) * LANES:(2 * kvh + p + 1) * LANES] = o_pair.astype(BF16)

    blk = lambda f: pl.BlockSpec((CHUNK, d), f)
    vec = pl.BlockSpec((1, LANES), lambda n: (0, 0))
    return _call(
        body, [sinks, qraw, kvd, kvd, gq, gk], grid=(nb,),
        in_specs=[pl.BlockSpec(memory_space=pltpu.SMEM), blk(lambda n: (n, 0)), blk(lambda n: (n, 0)),
                  blk(lambda n: (jnp.maximum(n - 1, 0), 0)), vec, vec],
        out_specs=[blk(lambda n: (n, 0))], out_shape=[jax.ShapeDtypeStruct((t, d), BF16)],
        name="attn_fwd", carry=carry)[0]


def _attn_bwd(qraw, kvd, d_o, gq, gk, sinks, carry=None):
    t, d = qraw.shape
    nb = t // CHUNK

    def body(sink_ref, q_ref, cur_ref, prev_ref, do_ref, gq_ref, gk_ref,
             dq_ref, dkv_ref, dsink_ref, dgq_ref, dgk_ref, carry_s, pp_s, cp_s):
        n = pl.program_id(0)

        @pl.when(n == 0)
        def _():
            carry_s[...] = jnp.zeros_like(carry_s)
            dsink_ref[...] = jnp.zeros_like(dsink_ref)
            dgq_ref[...] = jnp.zeros_like(dgq_ref)
            dgk_ref[...] = jnp.zeros_like(dgk_ref)

        @pl.when(n < nb)
        def _():
            lo, mask, distf = _attn_masks(n)
            gq_v, gk_v = gq_ref[...], gk_ref[...]
            for kvh in range(N_KV_HEADS):
                ks = slice(kvh * LANES, (kvh + 1) * LANES)
                vs = slice(4 * LANES + kvh * LANES, 4 * LANES + (kvh + 1) * LANES)
                kraw = jnp.concatenate([prev_ref[:, ks], cur_ref[:, ks]], axis=0)
                rk = lax.rsqrt(jnp.mean(kraw * kraw, axis=-1, keepdims=True) + EPS)
                khat = kraw * rk
                kn = (khat * gk_v).astype(BF16)
                vv = jnp.concatenate([prev_ref[:, vs], cur_ref[:, vs]], axis=0).astype(BF16)
                cols = [slice((2 * kvh + p) * LANES, (2 * kvh + p + 1) * LANES) for p in range(2)]
                rq, qhat = [], []
                for p in range(2):
                    qp = q_ref[:, cols[p]]
                    rq.append(lax.rsqrt(_half_sum(qp * qp, lo) * (1.0 / HEAD_DIM) + EPS))
                    qhat.append(qp * rq[p])
                heads = range(Q_PER_KV * kvh, Q_PER_KV * (kvh + 1))
                qs = _stack_heads([qhat[p] * gq_v for p in range(2)], lo).astype(BF16)
                dos = _stack_heads([do_ref[:, cols[p]] for p in range(2)], lo)
                sink_col = _per_head_column([sink_ref[h] for h in heads])
                pf, m, den = _attn_probs(qs, kn, mask, distf, _per_head_column([SLOPES[h] for h in heads]), sink_col)
                dp = _dot(dos, vv, NT)
                delta = jnp.sum(pf * dp, axis=-1, keepdims=True)
                sink_delta = jnp.exp(sink_col - m) / den * delta
                for j, h in enumerate(heads):
                    dsink_ref[h:h + 1, :] -= jnp.broadcast_to(
                        jnp.sum(sink_delta[j * CHUNK:(j + 1) * CHUNK], axis=0, keepdims=True), (1, LANES))
                ds = (pf * (dp - delta) * (HEAD_DIM ** -0.5)).astype(BF16)
                dkn = _dot(ds, qs, TN)
                dvb = _dot(pf.astype(BF16), dos, TN)
                for p, dqn in enumerate(_unstack_heads(_dot(ds, kn), lo)):
                    dgq_ref[0:1, :] += jnp.sum(dqn * qhat[p], axis=0, keepdims=True)
                    gy = dqn * gq_v
                    mq = _half_sum(gy * qhat[p], lo) * (1.0 / HEAD_DIM)
                    dq_ref[:, cols[p]] = (rq[p] * (gy - qhat[p] * mq)).astype(BF16)
                dgk_ref[0:1, :] += jnp.sum(dkn * khat, axis=0, keepdims=True)
                gyk = dkn * gk_v
                dkraw = rk * (gyk - khat * jnp.mean(gyk * khat, axis=-1, keepdims=True))
                pp_s[:, ks] = dkraw[:CHUNK]
                cp_s[:, ks] = dkraw[CHUNK:]
                pp_s[:, vs] = dvb[:CHUNK]
                cp_s[:, vs] = dvb[CHUNK:]
            dkv_ref[...] = (carry_s[...] + pp_s[...]).astype(BF16)
            carry_s[...] = cp_s[...]

        @pl.when(n == nb)
        def _():
            dkv_ref[...] = carry_s[...].astype(BF16)

    blk = lambda f: pl.BlockSpec((CHUNK, d), f)
    vec = pl.BlockSpec((1, LANES), lambda n: (0, 0))
    cur = lambda n: (jnp.minimum(n, nb - 1), 0)
    prev = lambda n: (jnp.maximum(jnp.minimum(n, nb - 1) - 1, 0), 0)
    small = lambda r: pl.BlockSpec((r, LANES), lambda n: (0, 0))
    return _call(
        body, [sinks, qraw, kvd, kvd, d_o, gq, gk], grid=(nb + 1,),
        in_specs=[pl.BlockSpec(memory_space=pltpu.SMEM), blk(cur), blk(cur), blk(prev), blk(cur), vec, vec],
        out_specs=[blk(cur), blk(lambda n: (jnp.maximum(n - 1, 0), 0)), small(N_Q_HEADS), small(8), small(8)],
        out_shape=[jax.ShapeDtypeStruct((t, d), BF16), jax.ShapeDtypeStruct((t, d), BF16),
                   jax.ShapeDtypeStruct((N_Q_HEADS, LANES), F32), jax.ShapeDtypeStruct((8, LANES), F32),
                   jax.ShapeDtypeStruct((8, LANES), F32)],
        scratch=[pltpu.VMEM((CHUNK, d), F32)] * 3, name="attn_bwd", sem=("arbitrary",), carry=carry)


def _adamw_math(g, w, m, v):
    m = ADAM_B1 * m + (1.0 - ADAM_B1) * g
    v = ADAM_B2 * v + (1.0 - ADAM_B2) * (g * g)
    m_hat = m / (1.0 - ADAM_B1 ** ADAM_STEP)
    v_hat = v / (1.0 - ADAM_B2 ** ADAM_STEP)
    delta = -ADAM_LR * (m_hat / (jnp.sqrt(v_hat) + ADAM_EPS) + ADAM_WD * w)
    return delta, m, v


def _row_tile(r, cap=128):
    for tr in range(min(r, cap), 0, -1):
        if r % tr == 0 and (tr % 8 == 0 or tr == r):
            return tr
    return r


def _chip_sum(grad, recv, place, name, wire_dtype):
    _, r, c = grad.shape
    tr = _row_tile(r, 256)

    def body(pl_ref, g_ref, a_ref, p_ref):
        p_ref[...] = (g_ref[...] + a_ref[...]).astype(p_ref.dtype)

    return pl.pallas_call(
        body,
        grid_spec=pltpu.PrefetchScalarGridSpec(
            num_scalar_prefetch=1, grid=(4, r // tr),
            in_specs=[pl.BlockSpec((None, None, tr, c), lambda q, i, pr: (q, pr[1], i, 0)),
                      pl.BlockSpec((None, tr, c), lambda q, i, pr: (q, i, 0))],
            out_specs=pl.BlockSpec((None, tr, c), lambda q, i, pr: (q, i, 0))),
        out_shape=jax.ShapeDtypeStruct((4, r, c), wire_dtype), name=name, compiler_params=_params(),
    )(place, grad.reshape(4, 2, r, c), recv)


def _adamw_sharded(grad, recv, others, place, w, m, v, name, layer=None, fill=None):
    r, c = w.shape[-2:]
    tr = _row_tile(r)

    def body(pl_ref, g_ref, a_ref, oth_ref, w_ref, m_ref, v_ref, *rest):
        g_out, d_out, nm_out, nv_out = rest[-4:]
        g = g_ref[...] + a_ref[...]
        for k in range(3):
            g = g + oth_ref[k].astype(F32)
        delta, nm, nv = _adamw_math(g, w_ref[...], m_ref[...], v_ref[...])
        g_out[...] = g
        d_out[...] = delta
        nm_out[...] = nm
        nv_out[...] = nv

    if layer is None:
        row = pl.BlockSpec((tr, c), lambda i, pr: (i, 0))
    else:
        row = pl.BlockSpec((None, tr, c), lambda i, pr: (layer, i, 0))
    n_fill = 0 if fill is None else 4
    in_specs = [pl.BlockSpec((None, None, tr, c), lambda i, pr: (pr[0], pr[1], i, 0)),
                pl.BlockSpec((None, tr, c), lambda i, pr: (pr[0], i, 0)),
                pl.BlockSpec((3, tr, c), lambda i, pr: (0, i, 0)), row, row, row]
    in_specs += [pl.BlockSpec(memory_space=pl.ANY)] * n_fill
    return pl.pallas_call(
        body,
        grid_spec=pltpu.PrefetchScalarGridSpec(
            num_scalar_prefetch=1, grid=(r // tr,), in_specs=in_specs, out_specs=[row] * 4),
        out_shape=[jax.ShapeDtypeStruct(w.shape, F32)] * 4, name=name, compiler_params=_params(),
        input_output_aliases={7 + j: j for j in range(n_fill)},
    )(place, grad.reshape(4, 2, r, c), recv, others, w, m, v, *([] if fill is None else fill))


def _adamw_summed(parts, ws, ms, vs, name):
    n = len(parts)

    def body(*refs):
        p_refs, w_refs, m_refs, v_refs = refs[:n], refs[n:2 * n], refs[2 * n:3 * n], refs[3 * n:4 * n]
        o_refs = refs[4 * n:]
        for i in range(n):
            g = p_refs[i][0]
            for k in range(1, N_SHARDS):
                g = g + p_refs[i][k]
            delta, nm, nv = _adamw_math(g, w_refs[i][...], m_refs[i][...], v_refs[i][...])
            o_refs[4 * i][...] = g
            o_refs[4 * i + 1][...] = delta
            o_refs[4 * i + 2][...] = nm
            o_refs[4 * i + 3][...] = nv

    shapes = [jax.ShapeDtypeStruct(w.shape, F32) for w in ws for _ in range(4)]
    outs = pl.pallas_call(body, out_shape=shapes, name=name, compiler_params=_params())(*parts, *ws, *ms, *vs)
    return [outs[4 * i:4 * i + 4] for i in range(n)]


def _dup_heads(w):
    lead = w.shape[:-1]
    w4 = w.reshape(lead + (N_KV_HEADS, 1, HEAD_DIM))
    return jnp.broadcast_to(w4, lead + (N_KV_HEADS, 2, HEAD_DIM)).reshape(lead + (N_KV_HEADS * LANES,))


def _fold_heads(g):
    lead = g.shape[:-1]
    return g.reshape(lead + (N_KV_HEADS, 2, HEAD_DIM)).sum(axis=-2).reshape(lead + (N_KV_HEADS * HEAD_DIM,))


def kernel(x, a_norm, a_w_in, a_v_norm, a_w_s, a_b_s, a_w_out, f_norm, f_w_in, f_conv_w, f_conv_b, f_w_out, kv_norm, w_kv, k_norm, b_norm, b_w_q, b_q_norm, b_sinks, b_w_o, loss_target, m_a_norm, m_a_w_in, m_a_v_norm, m_a_w_s, m_a_b_s, m_a_w_out, m_f_norm, m_f_w_in, m_f_conv_w, m_f_conv_b, m_f_w_out, m_kv_norm, m_w_kv, m_k_norm, m_b_norm, m_b_w_q, m_b_q_norm, m_b_sinks, m_b_w_o, v_a_norm, v_a_w_in, v_a_v_norm, v_a_w_s, v_a_b_s, v_a_w_out, v_f_norm, v_f_w_in, v_f_conv_w, v_f_conv_b, v_f_w_out, v_kv_norm, v_w_kv, v_k_norm, v_b_norm, v_b_w_q, v_b_q_norm, v_b_sinks, v_b_w_o):
    d = D_MODEL
    xi, yi, ci = _coords()
    place = jnp.stack([2 * xi + yi, ci]).astype(jnp.int32)
    bf = lambda a: a.astype(BF16)
    row = lambda v_: v_.reshape(1, -1)
    x0, target = x[0], loss_target[0]
    t = x0.shape[0]
    res = {}

    red = {}

    def to_sibling(grads, wire=BF16):
        for k, g in grads.items():
            red[k] = dict(grad=g, wire=wire)
        ex = _ToSibling(list(grads.values()))
        ex.names = list(grads)
        return ex

    def to_chips(ex):
        for k, a in zip(ex.names, ex.results):
            red[k]["recv"] = a
            red[k]["psum"] = _chip_sum(red[k]["grad"], a, place, f"chip_sum_{k}", red[k]["wire"])
        nxt = _ToChips([red[k]["psum"] for k in ex.names])
        nxt.names = ex.names
        return nxt

    def landed(ex):
        for k, b in zip(ex.names, ex.results):
            red[k]["others"] = b

    def halves(ex, first_rows):
        parts = []
        for r0, nr in ((0, first_rows), (first_rows, ex.srcs[0].shape[1] - first_rows)):
            part = _ToChips(ex.srcs, rows=(r0, nr))
            part.names = ex.names
            parts.append(part)
        return parts

    def landed_halves(parts):
        for j, k in enumerate(parts[0].names):
            red[k]["others"] = jnp.concatenate([p.results[j] for p in parts], axis=1)

    def update(k, w, m, v, layer=None, fill=None):
        r = red[k]
        return _adamw_sharded(r["grad"], r["recv"], r["others"], place, w, m, v,
                              f"adamw_{k}", layer=layer, fill=fill)

    g_a_in, g_a_out, g_a_norm, g_a_v_norm, g_conv = _exchange_alone(
        _Gather([bf(a_w_in[0]), bf(a_w_out[0]), a_norm, a_v_norm, f_conv_w.reshape(6, FF_SHARD)]), "gather_first")
    a_norm_full, a_v_norm_full = g_a_norm.reshape(1, d), g_a_v_norm.reshape(1, d)
    conv_w = lax.reduce_precision(g_conv.reshape(N_SHARDS, 2, 3, FF_SHARD), 8, 7)
    cw = jnp.pad(jnp.transpose(conv_w, (1, 0, 2, 3)), ((0, 0), (0, 0), (0, 5), (0, 0)))
    w_a_in_flat = jnp.transpose(g_a_in, (1, 0, 2)).reshape(d, 2 * d)
    cb = f_conv_b.reshape(2, N_SHARDS, 1, FF_SHARD)
    tri = jnp.tril(jnp.ones((CHUNK, CHUNK), dtype=bool))
    w_causal = jnp.where(tri[None], a_w_s[0], 0.0).astype(BF16)
    w_causal_t = jnp.transpose(w_causal, (0, 2, 1))
    b_sb = jnp.broadcast_to(a_b_s[0][:, :, None], (N_GROUPS, CHUNK, CHUNK))
    w_a_out = g_a_out.reshape(d, d)
    gq = jnp.tile(b_q_norm.reshape(1, HEAD_DIM), (1, 2))
    gk = jnp.tile(k_norm.reshape(1, HEAD_DIM), (1, 2))
    sinks = b_sinks.reshape(N_Q_HEADS)

    (h1,) = _rms_fwd(x0, [a_norm_full], "a_norm_fwd")
    ex = _Gather([bf(f_w_in[0]), bf(f_w_out[0])])
    zpre, x1 = _sgu_fwd(x0, h1, g_a_in, a_v_norm_full, w_causal, b_sb, w_a_out, carry=ex)
    w_in0, w_out0 = ex.results[0], ex.results[1].reshape(D_FF, d)
    ex = _Gather([bf(w_kv), bf(b_w_q[0]), bf(b_w_o[0]), bf(f_w_in[1])], relay=False, early=True)
    x2, hf0, a0, pre0, hk, hq = _ffn_fwd(x1, f_norm[0:1], w_in0, cw[0], cb[0], w_out0, 0, carry=ex,
                                         next_gains=[row(kv_norm), b_norm])
    kv_full = ex.results[0].reshape(d, 2 * N_KV_HEADS * HEAD_DIM)
    w_q, w_o = ex.results[1].reshape(d, d), ex.results[2].reshape(d, d)
    w_in1 = ex.results[3]
    half = N_KV_HEADS * HEAD_DIM
    w_kv_dup = jnp.concatenate([_dup_heads(kv_full[:, :half]), _dup_heads(kv_full[:, half:])], axis=1)
    kvd = _mm_rows(hk, w_kv_dup, F32, "kv_proj")
    qraw = _mm_rows(hq, w_q, F32, "q_proj")
    ex = _Gather([bf(f_w_out[1])], relay=False, early=True)
    o = _attn_fwd(qraw, kvd, gq, gk, sinks, carry=ex)
    w_out1 = ex.results[0].reshape(D_FF, d)
    x3 = _mm_rows(o, w_o, F32, "o_proj", res=x2)
    _, hf1, a1, pre1, dy, loss_lanes = _ffn_fwd(x3, f_norm[1:2], w_in1, cw[1], cb[1], w_out1, 1, loss_target=target)
    loss = lax.psum(loss_lanes[0, 0], ("x", "y", "c"))

    dhu1, dw_out1, dcb1 = _ffn_bwd_act(pre1, w_out1, dy, 1)
    ex = to_sibling({"f_w_out1": dw_out1.reshape(N_SHARDS, D_FF // N_SHARDS, d)})
    da1, dhf1, dcw1 = _ffn_bwd_in(dhu1, a1, cw[1], w_in1, 1, carry=ex)
    ex = to_chips(ex)
    dw_in1 = _ffn_wgrad_in(hf1, da1, 1, carry=ex)
    landed(ex)
    ex = to_sibling({"f_w_in1": dw_in1})
    dx3, dgf1 = _rms_bwd(x3, [f_norm[1:2]], [dhf1], dy, "f1_norm_bwd", carry=ex)
    ex = to_chips(ex)
    d_o = _mm_rows(dx3, w_o, BF16, "o_proj_bwd", trans_w=True)
    dw_o = _mm_wgrad(o, dx3, "o_wgrad").reshape(N_SHARDS, d // N_SHARDS, d)
    dq, dkv, dsink, dgq, dgk = _attn_bwd(qraw, kvd, d_o, gq, gk, sinks, carry=ex)
    landed(ex)
    dw_q = _mm_wgrad(hq, dq, "q_wgrad").reshape(N_SHARDS, d // N_SHARDS, d)
    dw_kv_dup = _mm_wgrad(hk, dkv, "kv_wgrad")
    dw_kv = jnp.concatenate(
        [_fold_heads(dw_kv_dup[:, :4 * LANES]), _fold_heads(dw_kv_dup[:, 4 * LANES:])], axis=1
    ).reshape(N_SHARDS, d // N_SHARDS, 2 * N_KV_HEADS * HEAD_DIM)
    ex = to_sibling({"b_w_o": dw_o, "b_w_q": dw_q, "w_kv": dw_kv})
    dx2, dg2 = _rms_bwd(x2, [row(kv_norm), b_norm], [dkv, dq], dx3, "kvq_norm_bwd", tm=512, carry=ex,
                        through=[w_kv_dup, w_q])
    ex = to_chips(ex)
    dhu0, dw_out0, dcb0 = _ffn_bwd_act(pre0, w_out0, dx2, 0, carry=ex)
    landed(ex)
    ex = to_sibling({"f_w_out0": dw_out0.reshape(N_SHARDS, D_FF // N_SHARDS, d)})
    da0, dhf0, dcw0 = _ffn_bwd_in(dhu0, a0, cw[0], w_in0, 0, carry=ex)
    ex = to_chips(ex)
    dw_in0 = _ffn_wgrad_in(hf0, da0, 0, carry=ex)
    landed(ex)
    ex = to_sibling({"f_w_in0": dw_in0})
    dx1, dgf0 = _rms_bwd(x1, [f_norm[0:1]], [dhf0], dx2, "f0_norm_bwd", carry=ex)
    ex_lo, ex_hi = halves(to_chips(ex), 448)
    dz, y, dwc, dbs, dgv = _sgu_bwd(dx1, zpre, w_a_out, a_v_norm_full, w_causal, w_causal_t, b_sb, carry=ex_lo)
    dw_a_out = _mm_wgrad(y, dx1, "a_out_wgrad").reshape(N_SHARDS, d // N_SHARDS, d)
    nsub = g_a_in.shape[2]
    dw_a_in = _mm(
        h1, dz, pl.BlockSpec((t, d), lambda s, j, kk: (0, 0)), pl.BlockSpec((t, nsub), lambda s, j, kk: (0, s)),
        pl.BlockSpec((None, d, nsub), lambda s, j, kk: (s, 0, 0)), jax.ShapeDtypeStruct((N_SHARDS, d, nsub), F32),
        (N_SHARDS, 1, 1), TN, "a_in_wgrad", carry=ex_hi)
    landed_halves([ex_lo, ex_hi])

    def bias_grad(dcb):
        return jnp.transpose(dcb[:, :, 0, :], (1, 0, 2)).reshape(-1)

    g_conv_w = jnp.concatenate([dcw0[:, 0:3, :], dcw1[:, 0:3, :]], axis=1)
    g_a_v_norm = dgv[0].reshape(N_SHARDS, 1, LANES)
    rep = ["a_w_s", "a_b_s", "f_norm", "f_conv_b", "kv_norm", "k_norm", "b_norm", "b_q_norm", "b_sinks"]
    rep_g = dict(
        a_w_s=dwc.reshape(N_GROUPS * CHUNK, CHUNK), a_b_s=dbs[:, :, 0], f_norm=jnp.stack([dgf0[0], dgf1[0]]),
        f_conv_b=jnp.stack([bias_grad(dcb0), bias_grad(dcb1)]), kv_norm=dg2[0:1],
        k_norm=(dgk[0, :HEAD_DIM] + dgk[0, HEAD_DIM:])[None], b_norm=dg2[1:2],
        b_q_norm=(dgq[0, :HEAD_DIM] + dgq[0, HEAD_DIM:])[None], b_sinks=dsink[:, 0][None])
    ex_big = to_sibling({"a_w_out": dw_a_out, "a_w_in": dw_a_in})
    ex_small = to_sibling({"a_v_norm": g_a_v_norm, "f_conv_w": g_conv_w}, wire=F32)
    ex_rep = _Gather([rep_g[k] for k in rep], relay=False)
    together = _Together([ex_big, ex_small, ex_rep])
    dh1 = _mm_rows(dz, w_a_in_flat, F32, "a_in_bwd", trans_w=True, carry=together)
    together.spread()
    ex_big, ex_small = to_chips(ex_big), to_chips(ex_small)
    together = _Together([ex_big, ex_small])
    grad_x, dg0 = _rms_bwd(x0, [a_norm_full], [dh1], dx1, "a_norm_bwd", carry=together)
    together.spread()
    landed(ex_big)
    landed(ex_small)
    (a_norm_parts,) = _exchange_alone(_ToOwners([dg0[0].reshape(N_SHARDS, 1, LANES)]), "a_norm_to_owners")

    res["f_w_out"] = update("f_w_out1", f_w_out, m_f_w_out, v_f_w_out, layer=1)
    w_in_t = [jnp.swapaxes(a_, 1, 2) for a_ in (f_w_in, m_f_w_in, v_f_w_in)]
    res["f_w_in"] = update("f_w_in1", *w_in_t, layer=1)
    res["b_w_o"] = update("b_w_o", b_w_o, m_b_w_o, v_b_w_o, layer=0)
    res["b_w_q"] = update("b_w_q", b_w_q, m_b_w_q, v_b_w_q, layer=0)
    res["w_kv"] = update("w_kv", w_kv, m_w_kv, v_w_kv)
    res["f_w_out"] = update("f_w_out0", f_w_out, m_f_w_out, v_f_w_out, layer=0, fill=res["f_w_out"])
    res["f_w_in"] = [jnp.swapaxes(o_, 1, 2) for o_ in update("f_w_in0", *w_in_t, layer=0, fill=res["f_w_in"])]
    res["a_w_out"] = update("a_w_out", a_w_out, m_a_w_out, v_a_w_out, layer=0)
    res["a_w_in"] = update("a_w_in", a_w_in, m_a_w_in, v_a_w_in, layer=0)
    res["a_v_norm"] = update("a_v_norm", a_v_norm, m_a_v_norm, v_a_v_norm)
    res["f_conv_w"] = [o_.reshape(f_conv_w.shape) for o_ in update(
        "f_conv_w", f_conv_w.reshape(6, FF_SHARD), m_f_conv_w.reshape(6, FF_SHARD), v_f_conv_w.reshape(6, FF_SHARD))]

    rep_w = dict(a_w_s=a_w_s, a_b_s=a_b_s, f_norm=f_norm, f_conv_b=f_conv_b, kv_norm=kv_norm, k_norm=k_norm,
                 b_norm=b_norm, b_q_norm=b_q_norm, b_sinks=b_sinks, a_norm=a_norm)
    rep_m = dict(a_w_s=m_a_w_s, a_b_s=m_a_b_s, f_norm=m_f_norm, f_conv_b=m_f_conv_b, kv_norm=m_kv_norm,
                 k_norm=m_k_norm, b_norm=m_b_norm, b_q_norm=m_b_q_norm, b_sinks=m_b_sinks, a_norm=m_a_norm)
    rep_v = dict(a_w_s=v_a_w_s, a_b_s=v_a_b_s, f_norm=v_f_norm, f_conv_b=v_f_conv_b, kv_norm=v_kv_norm,
                 k_norm=v_k_norm, b_norm=v_b_norm, b_q_norm=v_b_q_norm, b_sinks=v_b_sinks, a_norm=v_a_norm)
    keys = rep + ["a_norm"]
    parts = ex_rep.results + [a_norm_parts]
    as2d = lambda a, p: a.reshape(p.shape[1:])
    rep_outs = _adamw_summed(parts, [as2d(rep_w[k], p) for k, p in zip(keys, parts)],
                             [as2d(rep_m[k], p) for k, p in zip(keys, parts)],
                             [as2d(rep_v[k], p) for k, p in zip(keys, parts)], "adamw_replicated")
    for j, key in enumerate(keys):
        res[key] = [o_.reshape(rep_w[key].shape) for o_ in rep_outs[j]]

    order = ["a_norm", "a_w_in", "a_v_norm", "a_w_s", "a_b_s", "a_w_out", "f_norm", "f_w_in", "f_conv_w", "f_conv_b",
             "f_w_out", "kv_norm", "w_kv", "k_norm", "b_norm", "b_w_q", "b_q_norm", "b_sinks", "b_w_o"]
    outs = [loss, grad_x[None]]
    for j in range(4):
        outs += [res[k][j] for k in order]
    return tuple(outs)
```

```python
import jax
import jax.numpy as jnp
from jax import lax
from jax.experimental import pallas as pl
from jax.experimental.pallas import tpu as pltpu

F32 = jnp.float32
BF16 = jnp.bfloat16
EPS = 1e-6
D_MODEL = 1024
CHUNK = 128
N_GROUPS = 8
N_SHARDS = 8
HEAD_DIM = 64
N_Q_HEADS = 16
N_KV_HEADS = 4
D_FF = 2816
FF_SHARD = 2 * D_FF // N_SHARDS
LANES = 128
NEG_BIG = -1e30
ADAM_LR = 0.001
ADAM_B1 = 0.9
ADAM_B2 = 0.999
ADAM_EPS = 1e-08
ADAM_WD = 0.01
ADAM_STEP = 10
VMEM_LIMIT_BYTES = 56 * 1024 * 1024
MESH = pl.DeviceIdType.MESH

NN = (((1,), (0,)), ((), ()))
NT = (((1,), (1,)), ((), ()))
TN = (((0,), (0,)), ((), ()))
SLOPES = tuple(2.0 ** (-8.0 * (h + 1) / N_Q_HEADS) for h in range(N_Q_HEADS))


def _params(sem=None):
    return pltpu.CompilerParams(dimension_semantics=sem, vmem_limit_bytes=VMEM_LIMIT_BYTES)


def _dot(a, b, dims=NN):
    return lax.dot_general(a, b, dims, preferred_element_type=F32)


def _sigmoid(x):
    return 1.0 / (1.0 + jnp.exp(-x))


def _gelu_parts(z):
    cdf = 0.5 * (1.0 + lax.erf(z * (2.0 ** -0.5)))
    pdf = jnp.exp(-0.5 * z * z) * 0.3989422804014327
    return cdf, pdf


def _coords():
    return lax.axis_index("x"), lax.axis_index("y"), lax.axis_index("c")


class _Gather:
    def __init__(self, srcs, relay=True, early=False):
        self.srcs = list(srcs)
        self.early = early
        n = len(self.srcs)
        self.relayed = [relay and s.shape[0] % 32 == 0 for s in self.srcs]
        self.out_shapes = [jax.ShapeDtypeStruct((N_SHARDS,) + s.shape, s.dtype) for s in self.srcs]
        self.sems = [pltpu.SemaphoreType.DMA((n, 9)), pltpu.SemaphoreType.DMA((n, 9)), pltpu.SemaphoreType.DMA((n,))]

    def _plan(self, src, dst, sems):
        send_sems, recv_sems, local_sems = sems
        x, y, c = _coords()
        n = len(src)

        def rows(e, dev, half=None):
            block = dst[e].at[4 * dev[0] + 2 * dev[1] + dev[2]]
            if half is None:
                return block
            nr = self.srcs[e].shape[0] // 2
            return block.at[pl.ds(half * nr, nr)]

        def copy(e, slot, block, to, half=None, from_own=False):
            return pltpu.make_async_remote_copy(
                src_ref=src[e] if from_own else rows(e, block, half), dst_ref=rows(e, block, half),
                send_sem=send_sems.at[e, slot], recv_sem=recv_sems.at[e, slot], device_id=to, device_id_type=MESH)

        return n, x, y, c, rows, copy, local_sems

    def start(self, src, dst, sems):
        n, x, y, c, rows, copy, local_sems = self._plan(src, dst, sems)
        me = (x, y, c)
        for e in range(n):
            pltpu.make_async_copy(src[e], rows(e, me), local_sems.at[e]).start()
            copy(e, 0, me, (x, y, 1 - c), from_own=True).start()
            copy(e, 1, me, (1 - x, y, c), from_own=True).start()
            copy(e, 2, me, (x, 1 - y, c), from_own=True).start()
            if not self.relayed[e]:
                copy(e, 3, me, (1 - x, 1 - y, c), from_own=True).start()

    def pass_on(self, src, dst, sems, wait=True):
        n, x, y, c, rows, copy, local_sems = self._plan(src, dst, sems)
        me, sibling = (x, y, c), (x, y, 1 - c)
        over_x, over_y, diagonal = (1 - x, y, c), (x, 1 - y, c), (1 - x, 1 - y, c)
        sent = []

        def arrived(cp):
            if wait:
                cp.wait_recv()

        def send(cp):
            if wait:
                cp.start()
            sent.append(cp)

        for slot, owner, onward, half in ((1, over_x, over_y, 0), (2, over_y, over_x, 1)):
            for e in range(n):
                arrived(copy(e, slot, owner, me))
                if self.relayed[e]:
                    send(copy(e, 3 + half, owner, onward, half=half))
                send(copy(e, 4 + slot, owner, sibling))
        for e in range(n):
            if self.relayed[e]:
                for half in (0, 1):
                    arrived(copy(e, 3 + half, diagonal, me, half=half))
                    send(copy(e, 7 + half, diagonal, sibling, half=half))
            else:
                arrived(copy(e, 3, diagonal, me))
                send(copy(e, 7, diagonal, sibling))
        return sent

    def finish(self, src, dst, sems, passed_on=False):
        n, x, y, c, rows, copy, local_sems = self._plan(src, dst, sems)
        me, sibling = (x, y, c), (x, y, 1 - c)
        over_x, over_y, diagonal = (1 - x, y, c), (x, 1 - y, c), (1 - x, 1 - y, c)
        sent = self.pass_on(src, dst, sems, wait=not passed_on)
        for e in range(n):
            copy(e, 0, sibling, me).wait_recv()
            copy(e, 5, (1 - x, y, 1 - c), me).wait_recv()
            copy(e, 6, (x, 1 - y, 1 - c), me).wait_recv()
            if self.relayed[e]:
                for half in (0, 1):
                    copy(e, 7 + half, (1 - x, 1 - y, 1 - c), me, half=half).wait_recv()
            else:
                copy(e, 7, (1 - x, 1 - y, 1 - c), me).wait_recv()
        for e in range(n):
            copy(e, 0, me, sibling, from_own=True).wait_send()
            copy(e, 1, me, over_x, from_own=True).wait_send()
            copy(e, 2, me, over_y, from_own=True).wait_send()
            if not self.relayed[e]:
                copy(e, 3, me, diagonal, from_own=True).wait_send()
            pltpu.make_async_copy(src[e], rows(e, me), local_sems.at[e]).wait()
        for cp in sent:
            cp.wait_send()


class _ToSibling:
    def __init__(self, grads):
        self.srcs = list(grads)
        n = len(self.srcs)
        self.out_shapes = [jax.ShapeDtypeStruct((4,) + g.shape[1:], g.dtype) for g in self.srcs]
        self.sems = [pltpu.SemaphoreType.DMA((n, 4)), pltpu.SemaphoreType.DMA((n, 4))]

    def _copies(self, src, dst, sems):
        send_sems, recv_sems = sems
        x, y, c = _coords()
        return [
            pltpu.make_async_remote_copy(
                src_ref=src[i].at[2 * q + (1 - c)], dst_ref=dst[i].at[q], send_sem=send_sems.at[i, q],
                recv_sem=recv_sems.at[i, q], device_id=(x, y, 1 - c), device_id_type=MESH)
            for i in range(len(src)) for q in range(4)]

    def start(self, src, dst, sems):
        for cp in self._copies(src, dst, sems):
            cp.start()

    def finish(self, src, dst, sems):
        for cp in self._copies(src, dst, sems):
            cp.wait()


class _ToChips:
    def __init__(self, psums, rows=None):
        self.srcs = list(psums)
        n = len(self.srcs)
        self.rows = rows
        self.out_shapes = [
            jax.ShapeDtypeStruct((3, p.shape[1] if rows is None else rows[1]) + p.shape[2:], p.dtype)
            for p in self.srcs]
        self.sems = [pltpu.SemaphoreType.DMA((n, 3)), pltpu.SemaphoreType.DMA((n, 3))]

    def _copies(self, src, dst, sems):
        send_sems, recv_sems = sems
        x, y, c = _coords()
        peers = [(x, 1 - y), (1 - x, y), (1 - x, 1 - y)]

        def part(i, q):
            if self.rows is None:
                return src[i].at[q]
            return src[i].at[q, pl.ds(self.rows[0], self.rows[1])]

        return [
            pltpu.make_async_remote_copy(
                src_ref=part(i, 2 * px + py), dst_ref=dst[i].at[r], send_sem=send_sems.at[i, r],
                recv_sem=recv_sems.at[i, r], device_id=(px, py, c), device_id_type=MESH)
            for i in range(len(src)) for r, (px, py) in enumerate(peers)]

    def start(self, src, dst, sems):
        for cp in self._copies(src, dst, sems):
            cp.start()

    def finish(self, src, dst, sems):
        for cp in self._copies(src, dst, sems):
            cp.wait()


class _ToOwners:
    def __init__(self, grads):
        self.srcs = list(grads)
        n = len(self.srcs)
        self.out_shapes = [jax.ShapeDtypeStruct(g.shape, g.dtype) for g in self.srcs]
        self.sems = [pltpu.SemaphoreType.DMA((n, 7)), pltpu.SemaphoreType.DMA((n, 7)), pltpu.SemaphoreType.DMA((n,))]

    def _copies(self, src, dst, sems):
        send_sems, recv_sems, local_sems = sems
        x, y, c = _coords()
        me = 4 * x + 2 * y + c
        copies = [pltpu.make_async_copy(src[i].at[me], dst[i].at[me], local_sems.at[i]) for i in range(len(src))]
        for i in range(len(src)):
            for rel in range(1, N_SHARDS):
                px = x ^ (rel >> 2) if rel >> 2 else x
                py = y ^ ((rel >> 1) & 1) if (rel >> 1) & 1 else y
                pc = c ^ (rel & 1) if rel & 1 else c
                copies.append(pltpu.make_async_remote_copy(
                    src_ref=src[i].at[4 * px + 2 * py + pc], dst_ref=dst[i].at[me], send_sem=send_sems.at[i, rel - 1],
                    recv_sem=recv_sems.at[i, rel - 1], device_id=(px, py, pc), device_id_type=MESH))
        return copies

    def start(self, src, dst, sems):
        for cp in self._copies(src, dst, sems):
            cp.start()

    def finish(self, src, dst, sems):
        for cp in self._copies(src, dst, sems):
            cp.wait()


class _Together:
    def __init__(self, parts):
        self.parts = list(parts)
        self.srcs = [s for p in self.parts for s in p.srcs]
        self.out_shapes = [s for p in self.parts for s in p.out_shapes]
        self.sems = [s for p in self.parts for s in p.sems]

    def _split(self, src, dst, sems):
        a = b = c = 0
        for p in self.parts:
            na, nc = len(p.srcs), len(p.sems)
            yield p, src[a:a + na], dst[b:b + na], sems[c:c + nc]
            a, b, c = a + na, b + na, c + nc

    def start(self, src, dst, sems):
        for p, s, d, m in self._split(src, dst, sems):
            p.start(s, d, m)

    def finish(self, src, dst, sems):
        for p, s, d, m in self._split(src, dst, sems):
            p.finish(s, d, m)

    def spread(self):
        b = 0
        for p in self.parts:
            p.results = self.results[b:b + len(p.srcs)]
            b += len(p.srcs)


def _call(body, args, *, grid, in_specs, out_specs, out_shape, name, scratch=(), sem=None, carry=None):
    out_shape, out_specs = list(out_shape), list(out_specs)
    if carry is None:
        return pl.pallas_call(
            body, grid=grid, in_specs=list(in_specs), out_specs=out_specs, out_shape=out_shape,
            scratch_shapes=list(scratch), name=name, compiler_params=_params(sem))(*args)
    n_in, n_out, n_scr, n_c = len(args), len(out_shape), len(scratch), len(carry.srcs)
    steps = tuple(grid)
    total = 1
    for n_ax in steps:
        total *= n_ax
    early = getattr(carry, "early", False) and total >= 8
    early_step = total - max(2, total // 8)

    def carried(*refs):
        ins, rest = refs[:n_in], refs[n_in:]
        c_src, rest = rest[:n_c], rest[n_c:]
        outs, rest = rest[:n_out], rest[n_out:]
        c_dst, rest = rest[:n_c], rest[n_c:]
        scr, sems = rest[:n_scr], rest[n_scr:]
        step = pl.program_id(0)
        for ax in range(1, len(steps)):
            step = step * steps[ax] + pl.program_id(ax)

        @pl.when(step == 0)
        def _():
            carry.start(c_src, c_dst, sems)

        body(*ins, *outs, *scr)

        if early:
            @pl.when(step == early_step)
            def _():
                carry.pass_on(c_src, c_dst, sems)

        @pl.when(step == total - 1)
        def _():
            if early:
                carry.finish(c_src, c_dst, sems, passed_on=True)
            else:
                carry.finish(c_src, c_dst, sems)

    hbm = pl.BlockSpec(memory_space=pl.ANY)
    res = pl.pallas_call(
        carried, grid=grid, in_specs=list(in_specs) + [hbm] * n_c, out_specs=out_specs + [hbm] * n_c,
        out_shape=out_shape + carry.out_shapes, scratch_shapes=list(scratch) + carry.sems, name=name,
        compiler_params=_params(("arbitrary",) * len(steps)))(*args, *carry.srcs)
    carry.results = list(res[n_out:])
    return list(res[:n_out])


def _exchange_alone(ex, name):
    n = len(ex.srcs)

    def body(*refs):
        src, dst, sems = refs[:n], refs[n:2 * n], refs[2 * n:]
        ex.start(src, dst, sems)
        ex.finish(src, dst, sems)

    hbm = pl.BlockSpec(memory_space=pl.ANY)
    res = pl.pallas_call(body, in_specs=[hbm] * n, out_specs=[hbm] * n, out_shape=ex.out_shapes,
                         scratch_shapes=ex.sems, name=name)(*ex.srcs)
    ex.results = list(res)
    return ex.results


def _rms_bwd(x, gains, dhs, dres, name, tm=256, carry=None, through=None):
    t, d = x.shape
    n = len(gains)
    n_w = 0 if through is None else n

    def body(*refs):
        x_ref, dres_ref = refs[0], refs[1]
        g_refs, dh_refs, w_refs = refs[2:2 + n], refs[2 + n:2 + 2 * n], refs[2 + 2 * n:2 + 2 * n + n_w]
        dx_ref, dg_ref = refs[2 + 2 * n + n_w], refs[3 + 2 * n + n_w]
        i = pl.program_id(0)

        @pl.when(i == 0)
        def _():
            dg_ref[...] = jnp.zeros_like(dg_ref)

        xf = x_ref[...]
        r = lax.rsqrt(jnp.mean(xf * xf, axis=-1, keepdims=True) + EPS)
        xhat = xf * r
        dx = dres_ref[...]
        for j in range(n):
            dh = dh_refs[j][...]
            if n_w:
                dh = _dot(dh.astype(BF16), w_refs[j][...], NT)
            dg_ref[j:j + 1, :] += jnp.sum(dh * xhat, axis=0, keepdims=True)
            gy = dh * g_refs[j][...]
            dx = dx + r * (gy - xhat * jnp.mean(gy * xhat, axis=-1, keepdims=True))
        dx_ref[...] = dx

    row = pl.BlockSpec((tm, d), lambda i: (i, 0))
    vec = pl.BlockSpec((1, d), lambda i: (0, 0))
    dh_rows = [pl.BlockSpec((tm, dh.shape[1]), lambda i: (i, 0)) for dh in dhs]
    w_full = [] if through is None else [pl.BlockSpec(w.shape, lambda i: (0, 0)) for w in through]
    return _call(body, [x, dres, *gains, *dhs, *(through or [])], grid=(t // tm,),
                 in_specs=[row, row] + [vec] * n + dh_rows + w_full,
                 out_specs=[row, pl.BlockSpec((8, d), lambda i: (0, 0))],
                 out_shape=[jax.ShapeDtypeStruct((t, d), F32), jax.ShapeDtypeStruct((8, d), F32)],
                 name=name, sem=("arbitrary",), carry=carry)


def _mm(a, b, a_spec, b_spec, o_spec, out_shape, grid, dims, name, res=None, res_spec=None, carry=None):
    nk = grid[2]
    acc_shape = tuple(s for s in o_spec.block_shape if s is not None)

    def body(*refs):
        a_ref, b_ref = refs[0], refs[1]
        r_ref = refs[2] if res is not None else None
        o_ref = refs[3] if res is not None else refs[2]
        p = _dot(a_ref[...].astype(BF16), b_ref[...].astype(BF16), dims)
        if nk == 1:
            if res is not None:
                p = p + r_ref[...]
            o_ref[...] = p.astype(o_ref.dtype)
            return
        acc_ref = refs[-1]
        k = pl.program_id(2)

        @pl.when(k == 0)
        def _():
            acc_ref[...] = p

        @pl.when(k > 0)
        def _():
            acc_ref[...] += p

        @pl.when(k == nk - 1)
        def _():
            out = acc_ref[...]
            if res is not None:
                out = out + r_ref[...]
            o_ref[...] = out.astype(o_ref.dtype)

    ins = [a, b] + ([res] if res is not None else [])
    specs = [a_spec, b_spec] + ([res_spec] if res is not None else [])
    return _call(body, ins, grid=grid, in_specs=specs, out_specs=[o_spec], out_shape=[out_shape],
                 scratch=[pltpu.VMEM(acc_shape, F32)] if nk > 1 else [], name=name,
                 sem=("parallel", "parallel", "arbitrary"), carry=carry)[0]


def _mm_rows(a, w, out_dtype, name, trans_w=False, res=None, tm=1024, carry=None):
    t, k = a.shape
    tm = min(tm, t)
    n = w.shape[0] if trans_w else w.shape[1]
    return _mm(
        a, w, pl.BlockSpec((tm, k), lambda i, j, kk: (i, 0)), pl.BlockSpec(w.shape, lambda i, j, kk: (0, 0)),
        pl.BlockSpec((tm, n), lambda i, j, kk: (i, 0)), jax.ShapeDtypeStruct((t, n), out_dtype), (t // tm, 1, 1),
        NT if trans_w else NN, name, res=res,
        res_spec=None if res is None else pl.BlockSpec((tm, n), lambda i, j, kk: (i, 0)), carry=carry)


def _mm_wgrad(a, b, name, carry=None):
    t, m = a.shape
    n = b.shape[1]
    tn = n // (4 if b.dtype == F32 else 2)
    return _mm(
        a, b, pl.BlockSpec((t, m), lambda i, j, kk: (0, 0)), pl.BlockSpec((t, tn), lambda i, j, kk: (0, j)),
        pl.BlockSpec((m, tn), lambda i, j, kk: (0, j)), jax.ShapeDtypeStruct((m, n), F32), (1, n // tn, 1), TN, name,
        carry=carry)


def _sgu_fwd(x0, g, w_in, g_v, w_c, b_sb, w_out, tm=256, carry=None):
    t, d = x0.shape
    nsub = w_in.shape[2]

    def body(x_ref, g_ref, win_ref, gv_ref, wc_ref, bsb_ref, wout_ref, zpre_ref, x1_ref, h_ref, u_s, v_s, vn_s, y_s):
        xf = x_ref[...]
        h = (xf * lax.rsqrt(jnp.mean(xf * xf, axis=-1, keepdims=True) + EPS) * g_ref[...]).astype(BF16)
        h_ref[...] = h
        for k in range(N_SHARDS):
            zk = _dot(h, win_ref[k])
            zpre_ref[:, k * nsub:(k + 1) * nsub] = zk
            cdf, _ = _gelu_parts(zk)
            if k < N_SHARDS // 2:
                u_s[:, k * nsub:(k + 1) * nsub] = zk * cdf
            else:
                v_s[:, (k - 4) * nsub:(k - 3) * nsub] = zk * cdf
        v = v_s[...]
        rv = lax.rsqrt(jnp.mean(v * v, axis=-1, keepdims=True) + EPS)
        vn_s[...] = (v * rv * gv_ref[...]).astype(BF16)
        for ci in range(tm // CHUNK):
            rows = slice(ci * CHUNK, (ci + 1) * CHUNK)
            for g in range(N_GROUPS):
                cols = slice(g * LANES, (g + 1) * LANES)
                sv = _dot(wc_ref[g], vn_s[rows, cols]) + bsb_ref[g]
                y_s[rows, cols] = (u_s[rows, cols] * sv).astype(BF16)
        x1_ref[...] = x_ref[...] + _dot(y_s[...], wout_ref[...])

    row = pl.BlockSpec((tm, d), lambda i: (i, 0))
    full = lambda a: pl.BlockSpec(a.shape, lambda i: (0,) * a.ndim)
    return _call(
        body, [x0, g, w_in, g_v, w_c, b_sb, w_out], grid=(t // tm,),
        in_specs=[row, full(g), full(w_in), full(g_v), full(w_c), full(b_sb), full(w_out)],
        out_specs=[pl.BlockSpec((tm, 2 * d), lambda i: (i, 0)), row, row],
        out_shape=[jax.ShapeDtypeStruct((t, 2 * d), F32), jax.ShapeDtypeStruct((t, d), F32),
                   jax.ShapeDtypeStruct((t, d), BF16)],
        scratch=[pltpu.VMEM((tm, d), F32), pltpu.VMEM((tm, d), F32), pltpu.VMEM((tm, d), BF16),
                 pltpu.VMEM((tm, d), BF16)],
        name="sgu_fwd", carry=carry)


def _sgu_bwd(dx1, zpre, w_out, g_v, w_c, w_ct, b_sb, tm=256, carry=None):
    t, d = dx1.shape

    def body(dx_ref, zpre_ref, wout_ref, gv_ref, wc_ref, wct_ref, bsb_ref,
             dz_ref, y_ref, dwc_ref, dbs_ref, dgv_ref, u_s, vn_s, dy_s, du_s, dvn_s):
        i = pl.program_id(0)

        @pl.when(i == 0)
        def _():
            dwc_ref[...] = jnp.zeros_like(dwc_ref)
            dbs_ref[...] = jnp.zeros_like(dbs_ref)
            dgv_ref[...] = jnp.zeros_like(dgv_ref)

        dy_s[...] = _dot(dx_ref[...].astype(BF16), wout_ref[...], NT)
        zu = zpre_ref[:, :d]
        zv = zpre_ref[:, d:]
        cdf_u, pdf_u = _gelu_parts(zu)
        cdf_v, pdf_v = _gelu_parts(zv)
        u_s[...] = zu * cdf_u
        v = zv * cdf_v
        rv = lax.rsqrt(jnp.mean(v * v, axis=-1, keepdims=True) + EPS)
        vhat = v * rv
        gv = gv_ref[...]
        vn_s[...] = (vhat * gv).astype(BF16)
        for ci in range(tm // CHUNK):
            rows = slice(ci * CHUNK, (ci + 1) * CHUNK)
            for g in range(N_GROUPS):
                cols = slice(g * LANES, (g + 1) * LANES)
                vnb = vn_s[rows, cols]
                sv = _dot(wc_ref[g], vnb) + bsb_ref[g]
                dyb = dy_s[rows, cols]
                ub = u_s[rows, cols]
                dsv = dyb * ub
                du_s[rows, cols] = dyb * sv
                y_ref[rows, cols] = (ub * sv).astype(BF16)
                dsvb = dsv.astype(BF16)
                dbs_ref[g] += dsv
                dwc_ref[g] += _dot(dsvb, vnb, NT)
                dvn_s[rows, cols] = _dot(wct_ref[g], dsvb)
        dvn = dvn_s[...]
        dgv_ref[0:1, :] += jnp.sum(dvn * vhat, axis=0, keepdims=True)
        gy = dvn * gv
        dv = rv * (gy - vhat * jnp.mean(gy * vhat, axis=-1, keepdims=True))
        dz_ref[:, :d] = (du_s[...] * (cdf_u + zu * pdf_u)).astype(BF16)
        dz_ref[:, d:] = (dv * (cdf_v + zv * pdf_v)).astype(BF16)

        @pl.when(i == t // tm - 1)
        def _():
            tri = (lax.broadcasted_iota(jnp.int32, (CHUNK, CHUNK), 0)
                   >= lax.broadcasted_iota(jnp.int32, (CHUNK, CHUNK), 1))
            for g in range(N_GROUPS):
                dwc_ref[g] = jnp.where(tri, dwc_ref[g], 0.0)
                dbs_ref[g] = jnp.broadcast_to(jnp.sum(dbs_ref[g], axis=1, keepdims=True), (CHUNK, CHUNK))

    row = pl.BlockSpec((tm, d), lambda i: (i, 0))
    row2 = pl.BlockSpec((tm, 2 * d), lambda i: (i, 0))
    full = lambda a: pl.BlockSpec(a.shape, lambda i: (0,) * a.ndim)
    grp = pl.BlockSpec((N_GROUPS, CHUNK, CHUNK), lambda i: (0, 0, 0))
    return _call(
        body, [dx1, zpre, w_out, g_v, w_c, w_ct, b_sb], grid=(t // tm,),
        in_specs=[row, row2, full(w_out), full(g_v), full(w_c), full(w_ct), full(b_sb)],
        out_specs=[row2, row, grp, grp, pl.BlockSpec((8, d), lambda i: (0, 0))],
        out_shape=[jax.ShapeDtypeStruct((t, 2 * d), BF16), jax.ShapeDtypeStruct((t, d), BF16),
                   jax.ShapeDtypeStruct((N_GROUPS, CHUNK, CHUNK), F32),
                   jax.ShapeDtypeStruct((N_GROUPS, CHUNK, CHUNK), F32), jax.ShapeDtypeStruct((8, d), F32)],
        scratch=[pltpu.VMEM((tm, d), F32), pltpu.VMEM((tm, d), BF16), pltpu.VMEM((tm, d), F32),
                 pltpu.VMEM((tm, d), F32), pltpu.VMEM((tm, d), F32)],
        name="sgu_bwd", sem=("arbitrary",), carry=carry)


ROW_CHUNK = 256
HALO = 16


def _ffn_fwd(x, g, w_in, cw, cb, w_out, layer, tm=512, carry=None, next_gains=(), loss_target=None):
    t, d = x.shape
    nc = N_SHARDS // 2
    n_gains = len(next_gains)
    with_loss = loss_target is not None

    def body(x_ref, xp_ref, g_ref, wg_ref, wu_ref, cwg_ref, cbg_ref, cwu_ref, cbu_ref, wout_ref, *rest):
        extra_in, rest = rest[:n_gains + with_loss], rest[n_gains + with_loss:]
        o_ref, hf_ref, a_ref, pre_ref = rest[:4]
        extra_out, hw_s = rest[4:-1], rest[-1]
        i, c = pl.program_id(0), pl.program_id(1)

        @pl.when(c == 0)
        def _():
            keep = jnp.where(i == 0, 0.0, 1.0)
            xw = jnp.concatenate([xp_ref[...] * keep, x_ref[...]], axis=0)
            xhat = xw * lax.rsqrt(jnp.mean(xw * xw, axis=-1, keepdims=True) + EPS)
            hw_s[...] = (xhat * g_ref[...]).astype(BF16)
            hf_ref[...] = hw_s[HALO:, :]
            o_ref[...] = x_ref[...]

        hw = hw_s[...]
        pre = []
        for j, (w_ref, cw_ref, cb_ref) in enumerate(((wg_ref, cwg_ref, cbg_ref), (wu_ref, cwu_ref, cbu_ref))):
            ab = _dot(hw, w_ref[...]).astype(BF16)
            a_ref[j] = ab[HALO:]
            win = ab.astype(F32)
            cw_v = cw_ref[...]
            pre.append(cw_v[2:3, :] * win[HALO:] + cw_v[1:2, :] * pltpu.roll(win, 1, 0)[HALO:]
                       + cw_v[0:1, :] * pltpu.roll(win, 2, 0)[HALO:] + cb_ref[...])
            pre_ref[j] = pre[j]
        act = (pre[0] * _sigmoid(pre[0]) * pre[1]).astype(BF16)
        o_ref[...] += _dot(act, wout_ref[...])

        if with_loss:
            @pl.when((i == 0) & (c == 0))
            def _():
                extra_out[-1][...] = jnp.zeros_like(extra_out[-1])

        @pl.when(c == nc - 1)
        def _():
            xn = o_ref[...]
            if n_gains:
                xhat = xn * lax.rsqrt(jnp.mean(xn * xn, axis=-1, keepdims=True) + EPS)
                for k in range(n_gains):
                    extra_out[k][...] = (xhat * extra_in[k][...]).astype(BF16)
            if with_loss:
                err = xn - extra_in[-1][...]
                extra_out[-2][...] = err * (1.0 / d)
                part = jnp.sum(jnp.sum(err * err, axis=0, keepdims=True), axis=1, keepdims=True)
                extra_out[-1][...] += jnp.broadcast_to(0.5 / d * part, extra_out[-1].shape)

    row = pl.BlockSpec((tm, d), lambda i, c: (i, 0))
    vec = pl.BlockSpec((1, d), lambda i, c: (0, 0))
    shard = lambda rows, up: pl.BlockSpec((None, rows, FF_SHARD), lambda i, c: (c + up * nc, 0, 0))
    pair = pl.BlockSpec((2, None, tm, FF_SHARD), lambda i, c: (0, c, i, 0))
    lanes = pl.BlockSpec((8, LANES), lambda i, c: (0, 0))
    outs = _call(
        body, [x, x, g, w_in, w_in, cw, cb, cw, cb, w_out, *next_gains] + ([loss_target] if with_loss else []),
        grid=(t // tm, nc),
        in_specs=[row, pl.BlockSpec((HALO, d), lambda i, c: (jnp.maximum(i * (tm // HALO) - 1, 0), 0)),
                  vec, shard(d, 0), shard(d, 1), shard(8, 0), shard(1, 0), shard(8, 1), shard(1, 1),
                  pl.BlockSpec((FF_SHARD, d), lambda i, c: (c, 0))] + [vec] * n_gains + [row] * with_loss,
        out_specs=[row, row, pair, pair] + [row] * n_gains + [row, lanes] * with_loss,
        out_shape=[jax.ShapeDtypeStruct((t, d), F32), jax.ShapeDtypeStruct((t, d), BF16),
                   jax.ShapeDtypeStruct((2, nc, t, FF_SHARD), BF16), jax.ShapeDtypeStruct((2, nc, t, FF_SHARD), F32)]
        + [jax.ShapeDtypeStruct((t, d), BF16)] * n_gains
        + [jax.ShapeDtypeStruct((t, d), F32), jax.ShapeDtypeStruct((8, LANES), F32)] * with_loss,
        scratch=[pltpu.VMEM((tm + HALO, d), BF16)], name=f"ffn{layer}_fwd", sem=("arbitrary", "arbitrary"), carry=carry)
    return (outs[0], outs[1], outs[2].reshape(N_SHARDS, t, FF_SHARD), outs[3]) + tuple(outs[4:])


def _ffn_bwd_act(pre, w_out, dxn, layer, tm=512, carry=None):
    t, d = dxn.shape
    nc = N_SHARDS // 2

    def body(pre_ref, wout_ref, dx_ref, dhu_ref, dw_ref, dcb_ref):
        i = pl.program_id(1)

        @pl.when(i == 0)
        def _():
            dw_ref[...] = jnp.zeros_like(dw_ref)
            dcb_ref[...] = jnp.zeros_like(dcb_ref)

        hg, hu = pre_ref[0], pre_ref[1]
        sg = _sigmoid(hg)
        sl = hg * sg
        dxb = dx_ref[...].astype(BF16)
        dact = _dot(dxb, wout_ref[...], NT)
        dw_ref[...] += _dot((sl * hu).astype(BF16), dxb, TN)
        d_up = dact * sl
        d_gate = dact * hu * (sg * (1.0 + hg * (1.0 - sg)))
        for j, dv in enumerate((d_gate, d_up)):
            dhu_ref[j] = dv.astype(BF16)
            dcb_ref[j, 0:1, :] += jnp.sum(dv, axis=0, keepdims=True)

    return _call(
        body, [pre, w_out, dxn], grid=(nc, t // tm),
        in_specs=[pl.BlockSpec((2, None, tm, FF_SHARD), lambda c, i: (0, c, i, 0)),
                  pl.BlockSpec((FF_SHARD, d), lambda c, i: (c, 0)), pl.BlockSpec((tm, d), lambda c, i: (i, 0))],
        out_specs=[pl.BlockSpec((None, 2, tm, FF_SHARD), lambda c, i: (c, 0, i, 0)),
                   pl.BlockSpec((FF_SHARD, d), lambda c, i: (c, 0)),
                   pl.BlockSpec((None, 2, 8, FF_SHARD), lambda c, i: (c, 0, 0, 0))],
        out_shape=[jax.ShapeDtypeStruct((nc, 2, t, FF_SHARD), BF16), jax.ShapeDtypeStruct((D_FF, d), F32),
                   jax.ShapeDtypeStruct((nc, 2, 8, FF_SHARD), F32)],
        name=f"ffn{layer}_bwd_act", sem=("parallel", "arbitrary"), carry=carry)


def _ffn_bwd_in(dhu, a, cw, w_in, layer, tm=1024, carry=None):
    nc, _, t, _ = dhu.shape
    d = D_MODEL
    tm = min(tm, t)
    last_blk = t // 16 - 1

    def body(dh_ref, nx_ref, a_ref, cw_ref, win_ref, da_ref, o_ref, dcw_ref):
        i, s = pl.program_id(0), pl.program_id(1)

        @pl.when(s == 0)
        def _():
            o_ref[...] = jnp.zeros_like(o_ref)

        @pl.when((s == 0) & (i == 0))
        def _():
            dcw_ref[...] = jnp.zeros_like(dcw_ref)

        keep = jnp.where(i == t // tm - 1, 0.0, 1.0)
        cw = cw_ref[...]
        sums = [None] * 3
        for r0 in range(0, tm, ROW_CHUNK):
            rows = slice(r0, r0 + ROW_CHUNK)
            if r0 + ROW_CHUNK == tm:
                win = jnp.concatenate([dh_ref[rows, :].astype(F32), nx_ref[...].astype(F32) * keep], axis=0)
            else:
                win = dh_ref[r0:r0 + ROW_CHUNK + HALO, :].astype(F32)
            n = ROW_CHUNK + HALO
            taps = (pltpu.roll(win, n - 2, 0)[:ROW_CHUNK],
                    pltpu.roll(win, n - 1, 0)[:ROW_CHUNK],
                    win[:ROW_CHUNK])
            da = (cw[0:1, :] * taps[0] + cw[1:2, :] * taps[1] + cw[2:3, :] * taps[2]).astype(BF16)
            da_ref[rows, :] = da
            o_ref[rows, :] += _dot(da, win_ref[...], NT)
            af = a_ref[rows, :].astype(F32)
            parts = [jnp.sum(taps[k] * af, axis=0, keepdims=True) for k in range(3)]
            sums = [p if q is None else q + p for q, p in zip(sums, parts)]
        for k in range(3):
            dcw_ref[pl.ds(s, 1), k:k + 1, :] += sums[k][None]

    return _call(
        body, [dhu, dhu, a, cw, w_in], grid=(t // tm, N_SHARDS),
        in_specs=[pl.BlockSpec((None, None, tm, FF_SHARD), lambda i, s: (s % nc, s // nc, i, 0)),
                  pl.BlockSpec((None, None, 16, FF_SHARD),
                               lambda i, s: (s % nc, s // nc, jnp.minimum((i + 1) * (tm // 16), last_blk), 0)),
                  pl.BlockSpec((None, tm, FF_SHARD), lambda i, s: (s, i, 0)),
                  pl.BlockSpec((None, 8, FF_SHARD), lambda i, s: (s, 0, 0)),
                  pl.BlockSpec((None, d, FF_SHARD), lambda i, s: (s, 0, 0))],
        out_specs=[pl.BlockSpec((None, tm, FF_SHARD), lambda i, s: (s, i, 0)),
                   pl.BlockSpec((tm, d), lambda i, s: (i, 0)),
                   pl.BlockSpec((N_SHARDS, 8, FF_SHARD), lambda i, s: (0, 0, 0))],
        out_shape=[jax.ShapeDtypeStruct((N_SHARDS, t, FF_SHARD), BF16), jax.ShapeDtypeStruct((t, d), F32),
                   jax.ShapeDtypeStruct((N_SHARDS, 8, FF_SHARD), F32)],
        name=f"ffn{layer}_bwd_in", sem=("arbitrary", "arbitrary"), carry=carry)


def _ffn_wgrad_in(hf, da, layer, carry=None):
    t, d = hf.shape
    return _mm(
        da, hf, pl.BlockSpec((None, t, FF_SHARD), lambda s, j, kk: (s, 0, 0)),
        pl.BlockSpec((t, d), lambda s, j, kk: (0, 0)),
        pl.BlockSpec((None, FF_SHARD, d), lambda s, j, kk: (s, 0, 0)),
        jax.ShapeDtypeStruct((N_SHARDS, FF_SHARD, d), F32), (N_SHARDS, 1, 1), TN, f"ffn{layer}_wgrad_in",
        carry=carry)


Q_PER_KV = N_Q_HEADS // N_KV_HEADS
GROUP_ROWS = Q_PER_KV * CHUNK


def _attn_masks(n):
    lane = lax.broadcasted_iota(jnp.int32, (CHUNK, LANES), 1)
    lo = lane < HEAD_DIM
    tq = lax.broadcasted_iota(jnp.int32, (GROUP_ROWS, 2 * CHUNK), 0) & (CHUNK - 1)
    jk = lax.broadcasted_iota(jnp.int32, (GROUP_ROWS, 2 * CHUNK), 1)
    dist = tq + CHUNK - jk
    mask = (dist >= 0) & (dist < CHUNK) & (jk >= jnp.where(n == 0, CHUNK, 0))
    return lo, mask, dist.astype(F32)


def _per_head_column(values):
    r = lax.broadcasted_iota(jnp.int32, (GROUP_ROWS, 1), 0)
    col = jnp.full((GROUP_ROWS, 1), values[Q_PER_KV - 1], F32)
    for j in range(Q_PER_KV - 2, -1, -1):
        col = jnp.where(r < (j + 1) * CHUNK, values[j], col)
    return col


def _half_sum(x, lo):
    s_lo = jnp.sum(jnp.where(lo, x, 0.0), axis=-1, keepdims=True)
    s_hi = jnp.sum(jnp.where(lo, 0.0, x), axis=-1, keepdims=True)
    return jnp.where(lo, s_lo, s_hi)


def _stack_heads(pairs, lo):
    zero = jnp.zeros_like(pairs[0])
    return jnp.concatenate([jnp.where(lo, pairs[0], zero), jnp.where(lo, zero, pairs[0]),
                            jnp.where(lo, pairs[1], zero), jnp.where(lo, zero, pairs[1])], axis=0)


def _unstack_heads(stacked, lo):
    return (jnp.where(lo, stacked[0:CHUNK], stacked[CHUNK:2 * CHUNK]),
            jnp.where(lo, stacked[2 * CHUNK:3 * CHUNK], stacked[3 * CHUNK:]))


def _attn_probs(qs, kn, mask, distf, slope_col, sink_col):
    s = _dot(qs, kn, NT) * (HEAD_DIM ** -0.5)
    s = jnp.where(mask, s - slope_col * distf, NEG_BIG)
    m = jnp.maximum(jnp.max(s, axis=-1, keepdims=True), sink_col)
    e = jnp.exp(s - m)
    den = jnp.sum(e, axis=-1, keepdims=True) + jnp.exp(sink_col - m)
    return e * (1.0 / den), m, den


def _attn_fwd(qraw, kvd, gq, gk, sinks, carry=None):
    t, d = qraw.shape
    nb = t // CHUNK

    def body(sink_ref, q_ref, cur_ref, prev_ref, gq_ref, gk_ref, o_ref):
        n = pl.program_id(0)
        lo, mask, distf = _attn_masks(n)
        gq_v, gk_v = gq_ref[...], gk_ref[...]
        for kvh in range(N_KV_HEADS):
            ks = slice(kvh * LANES, (kvh + 1) * LANES)
            vs = slice(4 * LANES + kvh * LANES, 4 * LANES + (kvh + 1) * LANES)
            kraw = jnp.concatenate([prev_ref[:, ks], cur_ref[:, ks]], axis=0)
            rk = lax.rsqrt(jnp.mean(kraw * kraw, axis=-1, keepdims=True) + EPS)
            kn = (kraw * rk * gk_v).astype(BF16)
            vv = jnp.concatenate([prev_ref[:, vs], cur_ref[:, vs]], axis=0).astype(BF16)
            qn = []
            for p in range(2):
                qp = q_ref[:, (2 * kvh + p) * LANES:(2 * kvh + p + 1) * LANES]
                r = lax.rsqrt(_half_sum(qp * qp, lo) * (1.0 / HEAD_DIM) + EPS)
                qn.append(qp * r * gq_v)
            heads = range(Q_PER_KV * kvh, Q_PER_KV * (kvh + 1))
            pf, _, _ = _attn_probs(_stack_heads(qn, lo).astype(BF16), kn, mask, distf,
                                   _per_head_column([SLOPES[h] for h in heads]),
                                   _per_head_column([sink_ref[h] for h in heads]))
            for p, o_pair in enumerate(_unstack_heads(_dot(pf.astype(BF16), vv), lo)):
                o_ref[:, (2 * kvh + p) * LANES:(2 * kvh + p + 1) * LANES] = o_pair.astype(BF16)

    blk = lambda f: pl.BlockSpec((CHUNK, d), f)
    vec = pl.BlockSpec((1, LANES), lambda n: (0, 0))
    return _call(
        body, [sinks, qraw, kvd, kvd, gq, gk], grid=(nb,),
        in_specs=[pl.BlockSpec(memory_space=pltpu.SMEM), blk(lambda n: (n, 0)), blk(lambda n: (n, 0)),
                  blk(lambda n: (jnp.maximum(n - 1, 0), 0)), vec, vec],
        out_specs=[blk(lambda n: (n, 0))], out_shape=[jax.ShapeDtypeStruct((t, d), BF16)],
        name="attn_fwd", carry=carry)[0]


def _attn_bwd(qraw, kvd, d_o, gq, gk, sinks, carry=None):
    t, d = qraw.shape
    nb = t // CHUNK

    def body(sink_ref, q_ref, cur_ref, prev_ref, do_ref, gq_ref, gk_ref,
             dq_ref, dkv_ref, dsink_ref, dgq_ref, dgk_ref, carry_s, pp_s, cp_s):
        n = pl.program_id(0)

        @pl.when(n == 0)
        def _():
            carry_s[...] = jnp.zeros_like(carry_s)
            dsink_ref[...] = jnp.zeros_like(dsink_ref)
            dgq_ref[...] = jnp.zeros_like(dgq_ref)
            dgk_ref[...] = jnp.zeros_like(dgk_ref)

        @pl.when(n < nb)
        def _():
            lo, mask, distf = _attn_masks(n)
            gq_v, gk_v = gq_ref[...], gk_ref[...]
            for kvh in range(N_KV_HEADS):
                ks = slice(kvh * LANES, (kvh + 1) * LANES)
                vs = slice(4 * LANES + kvh * LANES, 4 * LANES + (kvh + 1) * LANES)
                kraw = jnp.concatenate([prev_ref[:, ks], cur_ref[:, ks]], axis=0)
                rk = lax.rsqrt(jnp.mean(kraw * kraw, axis=-1, keepdims=True) + EPS)
                khat = kraw * rk
                kn = (khat * gk_v).astype(BF16)
                vv = jnp.concatenate([prev_ref[:, vs], cur_ref[:, vs]], axis=0).astype(BF16)
                cols = [slice((2 * kvh + p) * LANES, (2 * kvh + p + 1) * LANES) for p in range(2)]
                rq, qhat = [], []
                for p in range(2):
                    qp = q_ref[:, cols[p]]
                    rq.append(lax.rsqrt(_half_sum(qp * qp, lo) * (1.0 / HEAD_DIM) + EPS))
                    qhat.append(qp * rq[p])
                heads = range(Q_PER_KV * kvh, Q_PER_KV * (kvh + 1))
                qs = _stack_heads([qhat[p] * gq_v for p in range(2)], lo).astype(BF16)
                dos = _stack_heads([do_ref[:, cols[p]] for p in range(2)], lo)
                sink_col = _per_head_column([sink_ref[h] for h in heads])
                pf, m, den = _attn_probs(qs, kn, mask, distf, _per_head_column([SLOPES[h] for h in heads]), sink_col)
                dp = _dot(dos, vv, NT)
                delta = jnp.sum(pf * dp, axis=-1, keepdims=True)
                sink_delta = jnp.exp(sink_col - m) / den * delta
                for j, h in enumerate(heads):
                    dsink_ref[h:h + 1, :] -= jnp.broadcast_to(
                        jnp.sum(sink_delta[j * CHUNK:(j + 1) * CHUNK], axis=0, keepdims=True), (1, LANES))
                ds = (pf * (dp - delta) * (HEAD_DIM ** -0.5)).astype(BF16)
                dkn = _dot(ds, qs, TN)
                dvb = _dot(pf.astype(BF16), dos, TN)
                for p, dqn in enumerate(_unstack_heads(_dot(ds, kn), lo)):
                    dgq_ref[0:1, :] += jnp.sum(dqn * qhat[p], axis=0, keepdims=True)
                    gy = dqn * gq_v
                    mq = _half_sum(gy * qhat[p], lo) * (1.0 / HEAD_DIM)
                    dq_ref[:, cols[p]] = (rq[p] * (gy - qhat[p] * mq)).astype(BF16)
                dgk_ref[0:1, :] += jnp.sum(dkn * khat, axis=0, keepdims=True)
                gyk = dkn * gk_v
                dkraw = rk * (gyk - khat * jnp.mean(gyk * khat, axis=-1, keepdims=True))
                pp_s[:, ks] = dkraw[:CHUNK]
                cp_s[:, ks] = dkraw[CHUNK:]
                pp_s[:, vs] = dvb[:CHUNK]
                cp_s[:, vs] = dvb[CHUNK:]
            dkv_ref[...] = (carry_s[...] + pp_s[...]).astype(BF16)
            carry_s[...] = cp_s[...]

        @pl.when(n == nb)
        def _():
            dkv_ref[...] = carry_s[...].astype(BF16)

    blk = lambda f: pl.BlockSpec((CHUNK, d), f)
    vec = pl.BlockSpec((1, LANES), lambda n: (0, 0))
    cur = lambda n: (jnp.minimum(n, nb - 1), 0)
    prev = lambda n: (jnp.maximum(jnp.minimum(n, nb - 1) - 1, 0), 0)
    small = lambda r: pl.BlockSpec((r, LANES), lambda n: (0, 0))
    return _call(
        body, [sinks, qraw, kvd, kvd, d_o, gq, gk], grid=(nb + 1,),
        in_specs=[pl.BlockSpec(memory_space=pltpu.SMEM), blk(cur), blk(cur), blk(prev), blk(cur), vec, vec],
        out_specs=[blk(cur), blk(lambda n: (jnp.maximum(n - 1, 0), 0)), small(N_Q_HEADS), small(8), small(8)],
        out_shape=[jax.ShapeDtypeStruct((t, d), BF16), jax.ShapeDtypeStruct((t, d), BF16),
                   jax.ShapeDtypeStruct((N_Q_HEADS, LANES), F32), jax.ShapeDtypeStruct((8, LANES), F32),
                   jax.ShapeDtypeStruct((8, LANES), F32)],
        scratch=[pltpu.VMEM((CHUNK, d), F32)] * 3, name="attn_bwd", sem=("arbitrary",), carry=carry)


def _adamw_math(g, w, m, v):
    m = ADAM_B1 * m + (1.0 - ADAM_B1) * g
    v = ADAM_B2 * v + (1.0 - ADAM_B2) * (g * g)
    m_hat = m / (1.0 - ADAM_B1 ** ADAM_STEP)
    v_hat = v / (1.0 - ADAM_B2 ** ADAM_STEP)
    delta = -ADAM_LR * (m_hat / (jnp.sqrt(v_hat) + ADAM_EPS) + ADAM_WD * w)
    return delta, m, v


def _row_tile(r, cap=128):
    for tr in range(min(r, cap), 0, -1):
        if r % tr == 0 and (tr % 8 == 0 or tr == r):
            return tr
    return r


def _chip_sum(grad, recv, place, name, wire_dtype):
    _, r, c = grad.shape
    tr = _row_tile(r, 256)

    def body(pl_ref, g_ref, a_ref, p_ref):
        p_ref[...] = (g_ref[...] + a_ref[...]).astype(p_ref.dtype)

    return pl.pallas_call(
        body,
        grid_spec=pltpu.PrefetchScalarGridSpec(
            num_scalar_prefetch=1, grid=(4, r // tr),
            in_specs=[pl.BlockSpec((None, None, tr, c), lambda q, i, pr: (q, pr[1], i, 0)),
                      pl.BlockSpec((None, tr, c), lambda q, i, pr: (q, i, 0))],
            out_specs=pl.BlockSpec((None, tr, c), lambda q, i, pr: (q, i, 0))),
        out_shape=jax.ShapeDtypeStruct((4, r, c), wire_dtype), name=name, compiler_params=_params(),
    )(place, grad.reshape(4, 2, r, c), recv)


def _adamw_sharded(grad, recv, others, place, w, m, v, name, layer=None, fill=None):
    r, c = w.shape[-2:]
    tr = _row_tile(r)

    def body(pl_ref, g_ref, a_ref, oth_ref, w_ref, m_ref, v_ref, *rest):
        g_out, d_out, nm_out, nv_out = rest[-4:]
        g = g_ref[...] + a_ref[...]
        for k in range(3):
            g = g + oth_ref[k].astype(F32)
        delta, nm, nv = _adamw_math(g, w_ref[...], m_ref[...], v_ref[...])
        g_out[...] = g
        d_out[...] = delta
        nm_out[...] = nm
        nv_out[...] = nv

    if layer is None:
        row = pl.BlockSpec((tr, c), lambda i, pr: (i, 0))
    else:
        row = pl.BlockSpec((None, tr, c), lambda i, pr: (layer, i, 0))
    n_fill = 0 if fill is None else 4
    in_specs = [pl.BlockSpec((None, None, tr, c), lambda i, pr: (pr[0], pr[1], i, 0)),
                pl.BlockSpec((None, tr, c), lambda i, pr: (pr[0], i, 0)),
                pl.BlockSpec((3, tr, c), lambda i, pr: (0, i, 0)), row, row, row]
    in_specs += [pl.BlockSpec(memory_space=pl.ANY)] * n_fill
    return pl.pallas_call(
        body,
        grid_spec=pltpu.PrefetchScalarGridSpec(
            num_scalar_prefetch=1, grid=(r // tr,), in_specs=in_specs, out_specs=[row] * 4),
        out_shape=[jax.ShapeDtypeStruct(w.shape, F32)] * 4, name=name, compiler_params=_params(),
        input_output_aliases={7 + j: j for j in range(n_fill)},
    )(place, grad.reshape(4, 2, r, c), recv, others, w, m, v, *([] if fill is None else fill))


def _sum_devices(parts, name):
    def body(p_ref, o_ref):
        total = p_ref[0]
        for k in range(1, N_SHARDS):
            total = total + p_ref[k]
        o_ref[...] = total

    return pl.pallas_call(body, out_shape=jax.ShapeDtypeStruct(parts.shape[1:], F32), name=name)(parts)


def _adamw_summed(parts, ws, ms, vs, name):
    n = len(parts)

    def body(*refs):
        p_refs, w_refs, m_refs, v_refs = refs[:n], refs[n:2 * n], refs[2 * n:3 * n], refs[3 * n:4 * n]
        o_refs = refs[4 * n:]
        for i in range(n):
            g = p_refs[i][0]
            for k in range(1, N_SHARDS):
                g = g + p_refs[i][k]
            delta, nm, nv = _adamw_math(g, w_refs[i][...], m_refs[i][...], v_refs[i][...])
            o_refs[4 * i][...] = g
            o_refs[4 * i + 1][...] = delta
            o_refs[4 * i + 2][...] = nm
            o_refs[4 * i + 3][...] = nv

    shapes = [jax.ShapeDtypeStruct(w.shape, F32) for w in ws for _ in range(4)]
    outs = pl.pallas_call(body, out_shape=shapes, name=name, compiler_params=_params())(*parts, *ws, *ms, *vs)
    return [outs[4 * i:4 * i + 4] for i in range(n)]


def _dup_heads(w):
    lead = w.shape[:-1]
    w4 = w.reshape(lead + (N_KV_HEADS, 1, HEAD_DIM))
    return jnp.broadcast_to(w4, lead + (N_KV_HEADS, 2, HEAD_DIM)).reshape(lead + (N_KV_HEADS * LANES,))


def _fold_heads(g):
    lead = g.shape[:-1]
    return g.reshape(lead + (N_KV_HEADS, 2, HEAD_DIM)).sum(axis=-2).reshape(lead + (N_KV_HEADS * HEAD_DIM,))


def kernel(x, a_norm, a_w_in, a_v_norm, a_w_s, a_b_s, a_w_out, f_norm, f_w_in, f_conv_w, f_conv_b, f_w_out, kv_norm, w_kv, k_norm, b_norm, b_w_q, b_q_norm, b_sinks, b_w_o, loss_target, m_a_norm, m_a_w_in, m_a_v_norm, m_a_w_s, m_a_b_s, m_a_w_out, m_f_norm, m_f_w_in, m_f_conv_w, m_f_conv_b, m_f_w_out, m_kv_norm, m_w_kv, m_k_norm, m_b_norm, m_b_w_q, m_b_q_norm, m_b_sinks, m_b_w_o, v_a_norm, v_a_w_in, v_a_v_norm, v_a_w_s, v_a_b_s, v_a_w_out, v_f_norm, v_f_w_in, v_f_conv_w, v_f_conv_b, v_f_w_out, v_kv_norm, v_w_kv, v_k_norm, v_b_norm, v_b_w_q, v_b_q_norm, v_b_sinks, v_b_w_o):
    d = D_MODEL
    xi, yi, ci = _coords()
    place = jnp.stack([2 * xi + yi, ci]).astype(jnp.int32)
    bf = lambda a: a.astype(BF16)
    row = lambda v_: v_.reshape(1, -1)
    x0, target = x[0], loss_target[0]
    t = x0.shape[0]
    res = {}

    red = {}

    def to_sibling(grads, wire=BF16):
        for k, g in grads.items():
            red[k] = dict(grad=g, wire=wire)
        ex = _ToSibling(list(grads.values()))
        ex.names = list(grads)
        return ex

    def to_chips(ex):
        for k, a in zip(ex.names, ex.results):
            red[k]["recv"] = a
            red[k]["psum"] = _chip_sum(red[k]["grad"], a, place, f"chip_sum_{k}", red[k]["wire"])
        nxt = _ToChips([red[k]["psum"] for k in ex.names])
        nxt.names = ex.names
        return nxt

    def landed(ex):
        for k, b in zip(ex.names, ex.results):
            red[k]["others"] = b

    def halves(ex, first_rows):
        parts = []
        for r0, nr in ((0, first_rows), (first_rows, ex.srcs[0].shape[1] - first_rows)):
            part = _ToChips(ex.srcs, rows=(r0, nr))
            part.names = ex.names
            parts.append(part)
        return parts

    def landed_halves(parts):
        for j, k in enumerate(parts[0].names):
            red[k]["others"] = jnp.concatenate([p.results[j] for p in parts], axis=1)

    def update(k, w, m, v, layer=None, fill=None):
        r = red[k]
        return _adamw_sharded(r["grad"], r["recv"], r["others"], place, w, m, v,
                              f"adamw_{k}", layer=layer, fill=fill)

    g_a_in, g_a_out, g_a_norm, g_a_v_norm, g_conv = _exchange_alone(
        _Gather([bf(a_w_in[0]), bf(a_w_out[0]), a_norm, a_v_norm, f_conv_w.reshape(6, FF_SHARD)]), "gather_first")
    a_norm_full, a_v_norm_full = g_a_norm.reshape(1, d), g_a_v_norm.reshape(1, d)
    conv_w = lax.reduce_precision(g_conv.reshape(N_SHARDS, 2, 3, FF_SHARD), 8, 7)
    cw = jnp.pad(jnp.transpose(conv_w, (1, 0, 2, 3)), ((0, 0), (0, 0), (0, 5), (0, 0)))
    w_a_in_flat = jnp.transpose(g_a_in, (1, 0, 2)).reshape(d, 2 * d)
    cb = f_conv_b.reshape(2, N_SHARDS, 1, FF_SHARD)
    tri = jnp.tril(jnp.ones((CHUNK, CHUNK), dtype=bool))
    w_causal = jnp.where(tri[None], a_w_s[0], 0.0).astype(BF16)
    w_causal_t = jnp.transpose(w_causal, (0, 2, 1))
    b_sb = jnp.broadcast_to(a_b_s[0][:, :, None], (N_GROUPS, CHUNK, CHUNK))
    w_a_out = g_a_out.reshape(d, d)
    gq = jnp.tile(b_q_norm.reshape(1, HEAD_DIM), (1, 2))
    gk = jnp.tile(k_norm.reshape(1, HEAD_DIM), (1, 2))
    sinks = b_sinks.reshape(N_Q_HEADS)

    ex = _Gather([bf(f_w_in[0]), bf(f_w_out[0])])
    zpre, x1, h1 = _sgu_fwd(x0, a_norm_full, g_a_in, a_v_norm_full, w_causal, b_sb, w_a_out, carry=ex)
    w_in0, w_out0 = ex.results[0], ex.results[1].reshape(D_FF, d)
    ex = _Gather([bf(w_kv), bf(b_w_q[0]), bf(b_w_o[0]), bf(f_w_in[1])], relay=False, early=True)
    x2, hf0, a0, pre0, hk, hq = _ffn_fwd(x1, f_norm[0:1], w_in0, cw[0], cb[0], w_out0, 0, carry=ex,
                                         next_gains=[row(kv_norm), b_norm])
    kv_full = ex.results[0].reshape(d, 2 * N_KV_HEADS * HEAD_DIM)
    w_q, w_o = ex.results[1].reshape(d, d), ex.results[2].reshape(d, d)
    w_in1 = ex.results[3]
    half = N_KV_HEADS * HEAD_DIM
    w_kv_dup = jnp.concatenate([_dup_heads(kv_full[:, :half]), _dup_heads(kv_full[:, half:])], axis=1)
    kvd = _mm_rows(hk, w_kv_dup, F32, "kv_proj")
    qraw = _mm_rows(hq, w_q, F32, "q_proj")
    ex = _Gather([bf(f_w_out[1])], relay=False, early=True)
    o = _attn_fwd(qraw, kvd, gq, gk, sinks, carry=ex)
    w_out1 = ex.results[0].reshape(D_FF, d)
    x3 = _mm_rows(o, w_o, F32, "o_proj", res=x2)
    _, hf1, a1, pre1, dy, loss_lanes = _ffn_fwd(x3, f_norm[1:2], w_in1, cw[1], cb[1], w_out1, 1, loss_target=target)

    dhu1, dw_out1, dcb1 = _ffn_bwd_act(pre1, w_out1, dy, 1)
    ex = to_sibling({"f_w_out1": dw_out1.reshape(N_SHARDS, D_FF // N_SHARDS, d)})
    da1, dhf1, dcw1 = _ffn_bwd_in(dhu1, a1, cw[1], w_in1, 1, carry=ex)
    ex = to_chips(ex)
    dw_in1 = _ffn_wgrad_in(hf1, da1, 1, carry=ex)
    landed(ex)
    ex = to_sibling({"f_w_in1": dw_in1})
    dx3, dgf1 = _rms_bwd(x3, [f_norm[1:2]], [dhf1], dy, "f1_norm_bwd", carry=ex)
    ex = to_chips(ex)
    d_o = _mm_rows(dx3, w_o, BF16, "o_proj_bwd", trans_w=True)
    dw_o = _mm_wgrad(o, dx3, "o_wgrad").reshape(N_SHARDS, d // N_SHARDS, d)
    dq, dkv, dsink, dgq, dgk = _attn_bwd(qraw, kvd, d_o, gq, gk, sinks, carry=ex)
    landed(ex)
    dw_q = _mm_wgrad(hq, dq, "q_wgrad").reshape(N_SHARDS, d // N_SHARDS, d)
    dw_kv_dup = _mm_wgrad(hk, dkv, "kv_wgrad")
    dw_kv = jnp.concatenate(
        [_fold_heads(dw_kv_dup[:, :4 * LANES]), _fold_heads(dw_kv_dup[:, 4 * LANES:])], axis=1
    ).reshape(N_SHARDS, d // N_SHARDS, 2 * N_KV_HEADS * HEAD_DIM)
    ex = to_sibling({"b_w_o": dw_o, "b_w_q": dw_q, "w_kv": dw_kv})
    dx2, dg2 = _rms_bwd(x2, [row(kv_norm), b_norm], [dkv, dq], dx3, "kvq_norm_bwd", tm=512, carry=ex,
                        through=[w_kv_dup, w_q])
    ex = to_chips(ex)
    dhu0, dw_out0, dcb0 = _ffn_bwd_act(pre0, w_out0, dx2, 0, carry=ex)
    landed(ex)
    ex = to_sibling({"f_w_out0": dw_out0.reshape(N_SHARDS, D_FF // N_SHARDS, d)})
    da0, dhf0, dcw0 = _ffn_bwd_in(dhu0, a0, cw[0], w_in0, 0, carry=ex)
    ex = to_chips(ex)
    dw_in0 = _ffn_wgrad_in(hf0, da0, 0, carry=ex)
    landed(ex)
    ex = to_sibling({"f_w_in0": dw_in0})
    dx1, dgf0 = _rms_bwd(x1, [f_norm[0:1]], [dhf0], dx2, "f0_norm_bwd", carry=ex)
    ex_lo, ex_hi = halves(to_chips(ex), 448)
    dz, y, dwc, dbs, dgv = _sgu_bwd(dx1, zpre, w_a_out, a_v_norm_full, w_causal, w_causal_t, b_sb, carry=ex_lo)
    dw_a_out = _mm_wgrad(y, dx1, "a_out_wgrad").reshape(N_SHARDS, d // N_SHARDS, d)
    nsub = g_a_in.shape[2]
    dw_a_in = _mm(
        h1, dz, pl.BlockSpec((t, d), lambda s, j, kk: (0, 0)), pl.BlockSpec((t, nsub), lambda s, j, kk: (0, s)),
        pl.BlockSpec((None, d, nsub), lambda s, j, kk: (s, 0, 0)), jax.ShapeDtypeStruct((N_SHARDS, d, nsub), F32),
        (N_SHARDS, 1, 1), TN, "a_in_wgrad", carry=ex_hi)
    landed_halves([ex_lo, ex_hi])

    def bias_grad(dcb):
        return jnp.transpose(dcb[:, :, 0, :], (1, 0, 2)).reshape(-1)

    g_conv_w = jnp.concatenate([dcw0[:, 0:3, :], dcw1[:, 0:3, :]], axis=1)
    g_a_v_norm = dgv[0].reshape(N_SHARDS, 1, LANES)
    rep = ["a_w_s", "a_b_s", "f_norm", "f_conv_b", "kv_norm", "k_norm", "b_norm", "b_q_norm", "b_sinks"]
    rep_g = dict(
        a_w_s=dwc.reshape(N_GROUPS * CHUNK, CHUNK), a_b_s=dbs[:, :, 0], f_norm=jnp.stack([dgf0[0], dgf1[0]]),
        f_conv_b=jnp.stack([bias_grad(dcb0), bias_grad(dcb1)]), kv_norm=dg2[0:1],
        k_norm=(dgk[0, :HEAD_DIM] + dgk[0, HEAD_DIM:])[None], b_norm=dg2[1:2],
        b_q_norm=(dgq[0, :HEAD_DIM] + dgq[0, HEAD_DIM:])[None], b_sinks=dsink[:, 0][None])
    ex_big = to_sibling({"a_w_out": dw_a_out, "a_w_in": dw_a_in})
    ex_small = to_sibling({"a_v_norm": g_a_v_norm, "f_conv_w": g_conv_w}, wire=F32)
    ex_rep = _Gather([rep_g[k] for k in rep] + [loss_lanes], relay=False)
    together = _Together([ex_big, ex_small, ex_rep])
    dh1 = _mm_rows(dz, w_a_in_flat, F32, "a_in_bwd", trans_w=True, carry=together)
    together.spread()
    ex_big, ex_small = to_chips(ex_big), to_chips(ex_small)
    together = _Together([ex_big, ex_small])
    grad_x, dg0 = _rms_bwd(x0, [a_norm_full], [dh1], dx1, "a_norm_bwd", carry=together)
    together.spread()
    landed(ex_big)
    landed(ex_small)
    (a_norm_parts,) = _exchange_alone(_ToOwners([dg0[0].reshape(N_SHARDS, 1, LANES)]), "a_norm_to_owners")

    res["f_w_out"] = update("f_w_out1", f_w_out, m_f_w_out, v_f_w_out, layer=1)
    w_in_t = [jnp.swapaxes(a_, 1, 2) for a_ in (f_w_in, m_f_w_in, v_f_w_in)]
    res["f_w_in"] = update("f_w_in1", *w_in_t, layer=1)
    res["b_w_o"] = update("b_w_o", b_w_o, m_b_w_o, v_b_w_o, layer=0)
    res["b_w_q"] = update("b_w_q", b_w_q, m_b_w_q, v_b_w_q, layer=0)
    res["w_kv"] = update("w_kv", w_kv, m_w_kv, v_w_kv)
    res["f_w_out"] = update("f_w_out0", f_w_out, m_f_w_out, v_f_w_out, layer=0, fill=res["f_w_out"])
    res["f_w_in"] = [jnp.swapaxes(o_, 1, 2) for o_ in update("f_w_in0", *w_in_t, layer=0, fill=res["f_w_in"])]
    res["a_w_out"] = update("a_w_out", a_w_out, m_a_w_out, v_a_w_out, layer=0)
    res["a_w_in"] = update("a_w_in", a_w_in, m_a_w_in, v_a_w_in, layer=0)
    res["a_v_norm"] = update("a_v_norm", a_v_norm, m_a_v_norm, v_a_v_norm)
    res["f_conv_w"] = [o_.reshape(f_conv_w.shape) for o_ in update(
        "f_conv_w", f_conv_w.reshape(6, FF_SHARD), m_f_conv_w.reshape(6, FF_SHARD), v_f_conv_w.reshape(6, FF_SHARD))]

    rep_w = dict(a_w_s=a_w_s, a_b_s=a_b_s, f_norm=f_norm, f_conv_b=f_conv_b, kv_norm=kv_norm, k_norm=k_norm,
                 b_norm=b_norm, b_q_norm=b_q_norm, b_sinks=b_sinks, a_norm=a_norm)
    rep_m = dict(a_w_s=m_a_w_s, a_b_s=m_a_b_s, f_norm=m_f_norm, f_conv_b=m_f_conv_b, kv_norm=m_kv_norm,
                 k_norm=m_k_norm, b_norm=m_b_norm, b_q_norm=m_b_q_norm, b_sinks=m_b_sinks, a_norm=m_a_norm)
    rep_v = dict(a_w_s=v_a_w_s, a_b_s=v_a_b_s, f_norm=v_f_norm, f_conv_b=v_f_conv_b, kv_norm=v_kv_norm,
                 k_norm=v_k_norm, b_norm=v_b_norm, b_q_norm=v_b_q_norm, b_sinks=v_b_sinks, a_norm=v_a_norm)
    keys = rep + ["a_norm"]
    loss = _sum_devices(ex_rep.results[-1], "loss_sum")[0, 0]
    parts = ex_rep.results[:-1] + [a_norm_parts]
    as2d = lambda a, p: a.reshape(p.shape[1:])
    rep_outs = _adamw_summed(parts, [as2d(rep_w[k], p) for k, p in zip(keys, parts)],
                             [as2d(rep_m[k], p) for k, p in zip(keys, parts)],
                             [as2d(rep_v[k], p) for k, p in zip(keys, parts)], "adamw_replicated")
    for j, key in enumerate(keys):
        res[key] = [o_.reshape(rep_w[key].shape) for o_ in rep_outs[j]]

    order = ["a_norm", "a_w_in", "a_v_norm", "a_w_s", "a_b_s", "a_w_out", "f_norm", "f_w_in", "f_conv_w", "f_conv_b",
             "f_w_out", "kv_norm", "w_kv", "k_norm", "b_norm", "b_w_q", "b_q_norm", "b_sinks", "b_w_o"]
    outs = [loss, grad_x[None]]
    for j in range(4):
        outs += [res[k][j] for k in order]
    return tuple(outs)
```

```python
import jax
import jax.numpy as jnp
from jax import lax
from jax.experimental import pallas as pl
from jax.experimental.pallas import tpu as pltpu

F32 = jnp.float32
BF16 = jnp.bfloat16
EPS = 1e-6
D_MODEL = 1024
CHUNK = 128
N_GROUPS = 8
N_SHARDS = 8
HEAD_DIM = 64
N_Q_HEADS = 16
N_KV_HEADS = 4
D_FF = 2816
FF_SHARD = 2 * D_FF // N_SHARDS
LANES = 128
NEG_BIG = -1e30
ADAM_LR = 0.001
ADAM_B1 = 0.9
ADAM_B2 = 0.999
ADAM_EPS = 1e-08
ADAM_WD = 0.01
ADAM_STEP = 10
VMEM_LIMIT_BYTES = 56 * 1024 * 1024
MESH = pl.DeviceIdType.MESH

NN = (((1,), (0,)), ((), ()))
NT = (((1,), (1,)), ((), ()))
TN = (((0,), (0,)), ((), ()))
SLOPES = tuple(2.0 ** (-8.0 * (h + 1) / N_Q_HEADS) for h in range(N_Q_HEADS))


def _params(sem=None):
    return pltpu.CompilerParams(dimension_semantics=sem, vmem_limit_bytes=VMEM_LIMIT_BYTES)


def _dot(a, b, dims=NN):
    return lax.dot_general(a, b, dims, preferred_element_type=F32)


def _sigmoid(x):
    return 1.0 / (1.0 + jnp.exp(-x))


def _gelu_parts(z):
    cdf = 0.5 * (1.0 + lax.erf(z * (2.0 ** -0.5)))
    pdf = jnp.exp(-0.5 * z * z) * 0.3989422804014327
    return cdf, pdf


def _coords():
    return lax.axis_index("x"), lax.axis_index("y"), lax.axis_index("c")


class _Gather:
    def __init__(self, srcs, relay=True, early=False):
        self.srcs = list(srcs)
        self.early = early
        n = len(self.srcs)
        self.relayed = [relay and s.shape[0] % 32 == 0 for s in self.srcs]
        self.out_shapes = [jax.ShapeDtypeStruct((N_SHARDS,) + s.shape, s.dtype) for s in self.srcs]
        self.sems = [pltpu.SemaphoreType.DMA((n, 9)), pltpu.SemaphoreType.DMA((n, 9)), pltpu.SemaphoreType.DMA((n,))]

    def _plan(self, src, dst, sems):
        send_sems, recv_sems, local_sems = sems
        x, y, c = _coords()
        n = len(src)

        def rows(e, dev, half=None):
            block = dst[e].at[4 * dev[0] + 2 * dev[1] + dev[2]]
            if half is None:
                return block
            nr = self.srcs[e].shape[0] // 2
            return block.at[pl.ds(half * nr, nr)]

        def copy(e, slot, block, to, half=None, from_own=False):
            return pltpu.make_async_remote_copy(
                src_ref=src[e] if from_own else rows(e, block, half), dst_ref=rows(e, block, half),
                send_sem=send_sems.at[e, slot], recv_sem=recv_sems.at[e, slot], device_id=to, device_id_type=MESH)

        return n, x, y, c, rows, copy, local_sems

    def start(self, src, dst, sems):
        n, x, y, c, rows, copy, local_sems = self._plan(src, dst, sems)
        me = (x, y, c)
        for e in range(n):
            pltpu.make_async_copy(src[e], rows(e, me), local_sems.at[e]).start()
            copy(e, 0, me, (x, y, 1 - c), from_own=True).start()
            copy(e, 1, me, (1 - x, y, c), from_own=True).start()
            copy(e, 2, me, (x, 1 - y, c), from_own=True).start()
            if not self.relayed[e]:
                copy(e, 3, me, (1 - x, 1 - y, c), from_own=True).start()

    def pass_on(self, src, dst, sems, wait=True):
        n, x, y, c, rows, copy, local_sems = self._plan(src, dst, sems)
        me, sibling = (x, y, c), (x, y, 1 - c)
        over_x, over_y, diagonal = (1 - x, y, c), (x, 1 - y, c), (1 - x, 1 - y, c)
        sent = []

        def arrived(cp):
            if wait:
                cp.wait_recv()

        def send(cp):
            if wait:
                cp.start()
            sent.append(cp)

        for slot, owner, onward, half in ((1, over_x, over_y, 0), (2, over_y, over_x, 1)):
            for e in range(n):
                arrived(copy(e, slot, owner, me))
                if self.relayed[e]:
                    send(copy(e, 3 + half, owner, onward, half=half))
                send(copy(e, 4 + slot, owner, sibling))
        for e in range(n):
            if self.relayed[e]:
                for half in (0, 1):
                    arrived(copy(e, 3 + half, diagonal, me, half=half))
                    send(copy(e, 7 + half, diagonal, sibling, half=half))
            else:
                arrived(copy(e, 3, diagonal, me))
                send(copy(e, 7, diagonal, sibling))
        return sent

    def finish(self, src, dst, sems, passed_on=False):
        n, x, y, c, rows, copy, local_sems = self._plan(src, dst, sems)
        me, sibling = (x, y, c), (x, y, 1 - c)
        over_x, over_y, diagonal = (1 - x, y, c), (x, 1 - y, c), (1 - x, 1 - y, c)
        sent = self.pass_on(src, dst, sems, wait=not passed_on)
        for e in range(n):
            copy(e, 0, sibling, me).wait_recv()
            copy(e, 5, (1 - x, y, 1 - c), me).wait_recv()
            copy(e, 6, (x, 1 - y, 1 - c), me).wait_recv()
            if self.relayed[e]:
                for half in (0, 1):
                    copy(e, 7 + half, (1 - x, 1 - y, 1 - c), me, half=half).wait_recv()
            else:
                copy(e, 7, (1 - x, 1 - y, 1 - c), me).wait_recv()
        for e in range(n):
            copy(e, 0, me, sibling, from_own=True).wait_send()
            copy(e, 1, me, over_x, from_own=True).wait_send()
            copy(e, 2, me, over_y, from_own=True).wait_send()
            if not self.relayed[e]:
                copy(e, 3, me, diagonal, from_own=True).wait_send()
            pltpu.make_async_copy(src[e], rows(e, me), local_sems.at[e]).wait()
        for cp in sent:
            cp.wait_send()


class _ToSibling:
    def __init__(self, grads):
        self.srcs = list(grads)
        n = len(self.srcs)
        self.out_shapes = [jax.ShapeDtypeStruct((4,) + g.shape[1:], g.dtype) for g in self.srcs]
        self.sems = [pltpu.SemaphoreType.DMA((n, 4)), pltpu.SemaphoreType.DMA((n, 4))]

    def _copies(self, src, dst, sems):
        send_sems, recv_sems = sems
        x, y, c = _coords()
        return [
            pltpu.make_async_remote_copy(
                src_ref=src[i].at[2 * q + (1 - c)], dst_ref=dst[i].at[q], send_sem=send_sems.at[i, q],
                recv_sem=recv_sems.at[i, q], device_id=(x, y, 1 - c), device_id_type=MESH)
            for i in range(len(src)) for q in range(4)]

    def start(self, src, dst, sems):
        for cp in self._copies(src, dst, sems):
            cp.start()

    def finish(self, src, dst, sems):
        for cp in self._copies(src, dst, sems):
            cp.wait()


class _ToChips:
    def __init__(self, psums, rows=None):
        self.srcs = list(psums)
        n = len(self.srcs)
        self.rows = rows
        self.out_shapes = [
            jax.ShapeDtypeStruct((3, p.shape[1] if rows is None else rows[1]) + p.shape[2:], p.dtype)
            for p in self.srcs]
        self.sems = [pltpu.SemaphoreType.DMA((n, 3)), pltpu.SemaphoreType.DMA((n, 3))]

    def _copies(self, src, dst, sems):
        send_sems, recv_sems = sems
        x, y, c = _coords()
        peers = [(x, 1 - y), (1 - x, y), (1 - x, 1 - y)]

        def part(i, q):
            if self.rows is None:
                return src[i].at[q]
            return src[i].at[q, pl.ds(self.rows[0], self.rows[1])]

        return [
            pltpu.make_async_remote_copy(
                src_ref=part(i, 2 * px + py), dst_ref=dst[i].at[r], send_sem=send_sems.at[i, r],
                recv_sem=recv_sems.at[i, r], device_id=(px, py, c), device_id_type=MESH)
            for i in range(len(src)) for r, (px, py) in enumerate(peers)]

    def start(self, src, dst, sems):
        for cp in self._copies(src, dst, sems):
            cp.start()

    def finish(self, src, dst, sems):
        for cp in self._copies(src, dst, sems):
            cp.wait()


class _ToOwners:
    def __init__(self, grads):
        self.srcs = list(grads)
        n = len(self.srcs)
        self.out_shapes = [jax.ShapeDtypeStruct(g.shape, g.dtype) for g in self.srcs]
        self.sems = [pltpu.SemaphoreType.DMA((n, 7)), pltpu.SemaphoreType.DMA((n, 7)), pltpu.SemaphoreType.DMA((n,))]

    def _copies(self, src, dst, sems):
        send_sems, recv_sems, local_sems = sems
        x, y, c = _coords()
        me = 4 * x + 2 * y + c
        copies = [pltpu.make_async_copy(src[i].at[me], dst[i].at[me], local_sems.at[i]) for i in range(len(src))]
        for i in range(len(src)):
            for rel in range(1, N_SHARDS):
                px = x ^ (rel >> 2) if rel >> 2 else x
                py = y ^ ((rel >> 1) & 1) if (rel >> 1) & 1 else y
                pc = c ^ (rel & 1) if rel & 1 else c
                copies.append(pltpu.make_async_remote_copy(
                    src_ref=src[i].at[4 * px + 2 * py + pc], dst_ref=dst[i].at[me], send_sem=send_sems.at[i, rel - 1],
                    recv_sem=recv_sems.at[i, rel - 1], device_id=(px, py, pc), device_id_type=MESH))
        return copies

    def start(self, src, dst, sems):
        for cp in self._copies(src, dst, sems):
            cp.start()

    def finish(self, src, dst, sems):
        for cp in self._copies(src, dst, sems):
            cp.wait()


class _Together:
    def __init__(self, parts):
        self.parts = list(parts)
        self.srcs = [s for p in self.parts for s in p.srcs]
        self.out_shapes = [s for p in self.parts for s in p.out_shapes]
        self.sems = [s for p in self.parts for s in p.sems]

    def _split(self, src, dst, sems):
        a = b = c = 0
        for p in self.parts:
            na, nc = len(p.srcs), len(p.sems)
            yield p, src[a:a + na], dst[b:b + na], sems[c:c + nc]
            a, b, c = a + na, b + na, c + nc

    def start(self, src, dst, sems):
        for p, s, d, m in self._split(src, dst, sems):
            p.start(s, d, m)

    def finish(self, src, dst, sems):
        for p, s, d, m in self._split(src, dst, sems):
            p.finish(s, d, m)

    def spread(self):
        b = 0
        for p in self.parts:
            p.results = self.results[b:b + len(p.srcs)]
            b += len(p.srcs)


def _call(body, args, *, grid, in_specs, out_specs, out_shape, name, scratch=(), sem=None, carry=None):
    out_shape, out_specs = list(out_shape), list(out_specs)
    if carry is None:
        return pl.pallas_call(
            body, grid=grid, in_specs=list(in_specs), out_specs=out_specs, out_shape=out_shape,
            scratch_shapes=list(scratch), name=name, compiler_params=_params(sem))(*args)
    n_in, n_out, n_scr, n_c = len(args), len(out_shape), len(scratch), len(carry.srcs)
    steps = tuple(grid)
    total = 1
    for n_ax in steps:
        total *= n_ax
    early = getattr(carry, "early", False) and total >= 8
    early_step = total - max(2, total // 8)

    def carried(*refs):
        ins, rest = refs[:n_in], refs[n_in:]
        c_src, rest = rest[:n_c], rest[n_c:]
        outs, rest = rest[:n_out], rest[n_out:]
        c_dst, rest = rest[:n_c], rest[n_c:]
        scr, sems = rest[:n_scr], rest[n_scr:]
        step = pl.program_id(0)
        for ax in range(1, len(steps)):
            step = step * steps[ax] + pl.program_id(ax)

        @pl.when(step == 0)
        def _():
            carry.start(c_src, c_dst, sems)

        body(*ins, *outs, *scr)

        if early:
            @pl.when(step == early_step)
            def _():
                carry.pass_on(c_src, c_dst, sems)

        @pl.when(step == total - 1)
        def _():
            if early:
                carry.finish(c_src, c_dst, sems, passed_on=True)
            else:
                carry.finish(c_src, c_dst, sems)

    hbm = pl.BlockSpec(memory_space=pl.ANY)
    res = pl.pallas_call(
        carried, grid=grid, in_specs=list(in_specs) + [hbm] * n_c, out_specs=out_specs + [hbm] * n_c,
        out_shape=out_shape + carry.out_shapes, scratch_shapes=list(scratch) + carry.sems, name=name,
        compiler_params=_params(("arbitrary",) * len(steps)))(*args, *carry.srcs)
    carry.results = list(res[n_out:])
    return list(res[:n_out])


def _exchange_alone(ex, name):
    n = len(ex.srcs)

    def body(*refs):
        src, dst, sems = refs[:n], refs[n:2 * n], refs[2 * n:]
        ex.start(src, dst, sems)
        ex.finish(src, dst, sems)

    hbm = pl.BlockSpec(memory_space=pl.ANY)
    res = pl.pallas_call(body, in_specs=[hbm] * n, out_specs=[hbm] * n, out_shape=ex.out_shapes,
                         scratch_shapes=ex.sems, name=name)(*ex.srcs)
    ex.results = list(res)
    return ex.results


def _rms_bwd(x, gains, dhs, dres, name, tm=256, carry=None, through=None):
    t, d = x.shape
    n = len(gains)
    n_w = 0 if through is None else n

    def body(*refs):
        x_ref, dres_ref = refs[0], refs[1]
        g_refs, dh_refs, w_refs = refs[2:2 + n], refs[2 + n:2 + 2 * n], refs[2 + 2 * n:2 + 2 * n + n_w]
        dx_ref, dg_ref = refs[2 + 2 * n + n_w], refs[3 + 2 * n + n_w]
        i = pl.program_id(0)

        @pl.when(i == 0)
        def _():
            dg_ref[...] = jnp.zeros_like(dg_ref)

        xf = x_ref[...]
        r = lax.rsqrt(jnp.mean(xf * xf, axis=-1, keepdims=True) + EPS)
        xhat = xf * r
        dx = dres_ref[...]
        for j in range(n):
            dh = dh_refs[j][...]
            if n_w:
                dh = _dot(dh.astype(BF16), w_refs[j][...], NT)
            dg_ref[j:j + 1, :] += jnp.sum(dh * xhat, axis=0, keepdims=True)
            gy = dh * g_refs[j][...]
            dx = dx + r * (gy - xhat * jnp.mean(gy * xhat, axis=-1, keepdims=True))
        dx_ref[...] = dx

    row = pl.BlockSpec((tm, d), lambda i: (i, 0))
    vec = pl.BlockSpec((1, d), lambda i: (0, 0))
    dh_rows = [pl.BlockSpec((tm, dh.shape[1]), lambda i: (i, 0)) for dh in dhs]
    w_full = [] if through is None else [pl.BlockSpec(w.shape, lambda i: (0, 0)) for w in through]
    return _call(body, [x, dres, *gains, *dhs, *(through or [])], grid=(t // tm,),
                 in_specs=[row, row] + [vec] * n + dh_rows + w_full,
                 out_specs=[row, pl.BlockSpec((8, d), lambda i: (0, 0))],
                 out_shape=[jax.ShapeDtypeStruct((t, d), F32), jax.ShapeDtypeStruct((8, d), F32)],
                 name=name, sem=("arbitrary",), carry=carry)


def _mm(a, b, a_spec, b_spec, o_spec, out_shape, grid, dims, name, res=None, res_spec=None, carry=None):
    nk = grid[2]
    acc_shape = tuple(s for s in o_spec.block_shape if s is not None)

    def body(*refs):
        a_ref, b_ref = refs[0], refs[1]
        r_ref = refs[2] if res is not None else None
        o_ref = refs[3] if res is not None else refs[2]
        p = _dot(a_ref[...].astype(BF16), b_ref[...].astype(BF16), dims)
        if nk == 1:
            if res is not None:
                p = p + r_ref[...]
            o_ref[...] = p.astype(o_ref.dtype)
            return
        acc_ref = refs[-1]
        k = pl.program_id(2)

        @pl.when(k == 0)
        def _():
            acc_ref[...] = p

        @pl.when(k > 0)
        def _():
            acc_ref[...] += p

        @pl.when(k == nk - 1)
        def _():
            out = acc_ref[...]
            if res is not None:
                out = out + r_ref[...]
            o_ref[...] = out.astype(o_ref.dtype)

    ins = [a, b] + ([res] if res is not None else [])
    specs = [a_spec, b_spec] + ([res_spec] if res is not None else [])
    return _call(body, ins, grid=grid, in_specs=specs, out_specs=[o_spec], out_shape=[out_shape],
                 scratch=[pltpu.VMEM(acc_shape, F32)] if nk > 1 else [], name=name,
                 sem=("parallel", "parallel", "arbitrary"), carry=carry)[0]


def _mm_rows(a, w, out_dtype, name, trans_w=False, res=None, tm=1024, carry=None):
    t, k = a.shape
    tm = min(tm, t)
    n = w.shape[0] if trans_w else w.shape[1]
    return _mm(
        a, w, pl.BlockSpec((tm, k), lambda i, j, kk: (i, 0)), pl.BlockSpec(w.shape, lambda i, j, kk: (0, 0)),
        pl.BlockSpec((tm, n), lambda i, j, kk: (i, 0)), jax.ShapeDtypeStruct((t, n), out_dtype), (t // tm, 1, 1),
        NT if trans_w else NN, name, res=res,
        res_spec=None if res is None else pl.BlockSpec((tm, n), lambda i, j, kk: (i, 0)), carry=carry)


def _mm_wgrad(a, b, name, carry=None):
    t, m = a.shape
    n = b.shape[1]
    tn = n // (4 if b.dtype == F32 else 2)
    return _mm(
        a, b, pl.BlockSpec((t, m), lambda i, j, kk: (0, 0)), pl.BlockSpec((t, tn), lambda i, j, kk: (0, j)),
        pl.BlockSpec((m, tn), lambda i, j, kk: (0, j)), jax.ShapeDtypeStruct((m, n), F32), (1, n // tn, 1), TN, name,
        carry=carry)


def _sgu_fwd(x0, g, w_in, g_v, w_c, b_sb, w_out, tm=256, carry=None):
    t, d = x0.shape
    nsub = w_in.shape[2]

    def body(x_ref, g_ref, win_ref, gv_ref, wc_ref, bsb_ref, wout_ref, zpre_ref, x1_ref, h_ref, u_s, v_s, vn_s, y_s):
        xf = x_ref[...]
        h = (xf * lax.rsqrt(jnp.mean(xf * xf, axis=-1, keepdims=True) + EPS) * g_ref[...]).astype(BF16)
        h_ref[...] = h
        for k in range(N_SHARDS):
            zk = _dot(h, win_ref[k])
            zpre_ref[:, k * nsub:(k + 1) * nsub] = zk
            cdf, _ = _gelu_parts(zk)
            if k < N_SHARDS // 2:
                u_s[:, k * nsub:(k + 1) * nsub] = zk * cdf
            else:
                v_s[:, (k - 4) * nsub:(k - 3) * nsub] = zk * cdf
        v = v_s[...]
        rv = lax.rsqrt(jnp.mean(v * v, axis=-1, keepdims=True) + EPS)
        vn_s[...] = (v * rv * gv_ref[...]).astype(BF16)
        for ci in range(tm // CHUNK):
            rows = slice(ci * CHUNK, (ci + 1) * CHUNK)
            for g in range(N_GROUPS):
                cols = slice(g * LANES, (g + 1) * LANES)
                sv = _dot(wc_ref[g], vn_s[rows, cols]) + bsb_ref[g]
                y_s[rows, cols] = (u_s[rows, cols] * sv).astype(BF16)
        x1_ref[...] = x_ref[...] + _dot(y_s[...], wout_ref[...])

    row = pl.BlockSpec((tm, d), lambda i: (i, 0))
    full = lambda a: pl.BlockSpec(a.shape, lambda i: (0,) * a.ndim)
    return _call(
        body, [x0, g, w_in, g_v, w_c, b_sb, w_out], grid=(t // tm,),
        in_specs=[row, full(g), full(w_in), full(g_v), full(w_c), full(b_sb), full(w_out)],
        out_specs=[pl.BlockSpec((tm, 2 * d), lambda i: (i, 0)), row, row],
        out_shape=[jax.ShapeDtypeStruct((t, 2 * d), F32), jax.ShapeDtypeStruct((t, d), F32),
                   jax.ShapeDtypeStruct((t, d), BF16)],
        scratch=[pltpu.VMEM((tm, d), F32), pltpu.VMEM((tm, d), F32), pltpu.VMEM((tm, d), BF16),
                 pltpu.VMEM((tm, d), BF16)],
        name="sgu_fwd", carry=carry)


def _sgu_bwd(dx1, zpre, w_out, g_v, w_c, w_ct, b_sb, tm=256, carry=None):
    t, d = dx1.shape

    def body(dx_ref, zpre_ref, wout_ref, gv_ref, wc_ref, wct_ref, bsb_ref,
             dz_ref, y_ref, dwc_ref, dbs_ref, dgv_ref, u_s, vn_s, dy_s, du_s, dvn_s):
        i = pl.program_id(0)

        @pl.when(i == 0)
        def _():
            dwc_ref[...] = jnp.zeros_like(dwc_ref)
            dbs_ref[...] = jnp.zeros_like(dbs_ref)
            dgv_ref[...] = jnp.zeros_like(dgv_ref)

        dy_s[...] = _dot(dx_ref[...].astype(BF16), wout_ref[...], NT)
        zu = zpre_ref[:, :d]
        zv = zpre_ref[:, d:]
        cdf_u, pdf_u = _gelu_parts(zu)
        cdf_v, pdf_v = _gelu_parts(zv)
        u_s[...] = zu * cdf_u
        v = zv * cdf_v
        rv = lax.rsqrt(jnp.mean(v * v, axis=-1, keepdims=True) + EPS)
        vhat = v * rv
        gv = gv_ref[...]
        vn_s[...] = (vhat * gv).astype(BF16)
        for ci in range(tm // CHUNK):
            rows = slice(ci * CHUNK, (ci + 1) * CHUNK)
            for g in range(N_GROUPS):
                cols = slice(g * LANES, (g + 1) * LANES)
                vnb = vn_s[rows, cols]
                sv = _dot(wc_ref[g], vnb) + bsb_ref[g]
                dyb = dy_s[rows, cols]
                ub = u_s[rows, cols]
                dsv = dyb * ub
                du_s[rows, cols] = dyb * sv
                y_ref[rows, cols] = (ub * sv).astype(BF16)
                dsvb = dsv.astype(BF16)
                dbs_ref[g] += dsv
                dwc_ref[g] += _dot(dsvb, vnb, NT)
                dvn_s[rows, cols] = _dot(wct_ref[g], dsvb)
        dvn = dvn_s[...]
        dgv_ref[0:1, :] += jnp.sum(dvn * vhat, axis=0, keepdims=True)
        gy = dvn * gv
        dv = rv * (gy - vhat * jnp.mean(gy * vhat, axis=-1, keepdims=True))
        dz_ref[:, :d] = (du_s[...] * (cdf_u + zu * pdf_u)).astype(BF16)
        dz_ref[:, d:] = (dv * (cdf_v + zv * pdf_v)).astype(BF16)

        @pl.when(i == t // tm - 1)
        def _():
            tri = (lax.broadcasted_iota(jnp.int32, (CHUNK, CHUNK), 0)
                   >= lax.broadcasted_iota(jnp.int32, (CHUNK, CHUNK), 1))
            for g in range(N_GROUPS):
                dwc_ref[g] = jnp.where(tri, dwc_ref[g], 0.0)
                dbs_ref[g] = jnp.broadcast_to(jnp.sum(dbs_ref[g], axis=1, keepdims=True), (CHUNK, CHUNK))

    row = pl.BlockSpec((tm, d), lambda i: (i, 0))
    row2 = pl.BlockSpec((tm, 2 * d), lambda i: (i, 0))
    full = lambda a: pl.BlockSpec(a.shape, lambda i: (0,) * a.ndim)
    grp = pl.BlockSpec((N_GROUPS, CHUNK, CHUNK), lambda i: (0, 0, 0))
    return _call(
        body, [dx1, zpre, w_out, g_v, w_c, w_ct, b_sb], grid=(t // tm,),
        in_specs=[row, row2, full(w_out), full(g_v), full(w_c), full(w_ct), full(b_sb)],
        out_specs=[row2, row, grp, grp, pl.BlockSpec((8, d), lambda i: (0, 0))],
        out_shape=[jax.ShapeDtypeStruct((t, 2 * d), BF16), jax.ShapeDtypeStruct((t, d), BF16),
                   jax.ShapeDtypeStruct((N_GROUPS, CHUNK, CHUNK), F32),
                   jax.ShapeDtypeStruct((N_GROUPS, CHUNK, CHUNK), F32), jax.ShapeDtypeStruct((8, d), F32)],
        scratch=[pltpu.VMEM((tm, d), F32), pltpu.VMEM((tm, d), BF16), pltpu.VMEM((tm, d), F32),
                 pltpu.VMEM((tm, d), F32), pltpu.VMEM((tm, d), F32)],
        name="sgu_bwd", sem=("arbitrary",), carry=carry)


ROW_CHUNK = 256
HALO = 16


def _ffn_fwd(x, g, w_in, cw, cb, w_out, layer, tm=512, carry=None, next_gains=(), loss_target=None):
    t, d = x.shape
    nc = N_SHARDS // 2
    n_gains = len(next_gains)
    with_loss = loss_target is not None

    def body(x_ref, xp_ref, g_ref, wg_ref, wu_ref, cwg_ref, cbg_ref, cwu_ref, cbu_ref, wout_ref, *rest):
        extra_in, rest = rest[:n_gains + with_loss], rest[n_gains + with_loss:]
        o_ref, hf_ref, a_ref, pre_ref = rest[:4]
        extra_out, hw_s = rest[4:-1], rest[-1]
        i, c = pl.program_id(0), pl.program_id(1)

        @pl.when(c == 0)
        def _():
            keep = jnp.where(i == 0, 0.0, 1.0)
            xw = jnp.concatenate([xp_ref[...] * keep, x_ref[...]], axis=0)
            xhat = xw * lax.rsqrt(jnp.mean(xw * xw, axis=-1, keepdims=True) + EPS)
            hw_s[...] = (xhat * g_ref[...]).astype(BF16)
            hf_ref[...] = hw_s[HALO:, :]
            o_ref[...] = x_ref[...]

        hw = hw_s[...]
        pre = []
        for j, (w_ref, cw_ref, cb_ref) in enumerate(((wg_ref, cwg_ref, cbg_ref), (wu_ref, cwu_ref, cbu_ref))):
            ab = _dot(hw, w_ref[...]).astype(BF16)
            a_ref[j] = ab[HALO:]
            win = ab.astype(F32)
            cw_v = cw_ref[...]
            pre.append(cw_v[2:3, :] * win[HALO:] + cw_v[1:2, :] * pltpu.roll(win, 1, 0)[HALO:]
                       + cw_v[0:1, :] * pltpu.roll(win, 2, 0)[HALO:] + cb_ref[...])
            pre_ref[j] = pre[j]
        act = (pre[0] * _sigmoid(pre[0]) * pre[1]).astype(BF16)
        o_ref[...] += _dot(act, wout_ref[...])

        if with_loss:
            @pl.when((i == 0) & (c == 0))
            def _():
                extra_out[-1][...] = jnp.zeros_like(extra_out[-1])

        @pl.when(c == nc - 1)
        def _():
            xn = o_ref[...]
            if n_gains:
                xhat = xn * lax.rsqrt(jnp.mean(xn * xn, axis=-1, keepdims=True) + EPS)
                for k in range(n_gains):
                    extra_out[k][...] = (xhat * extra_in[k][...]).astype(BF16)
            if with_loss:
                err = xn - extra_in[-1][...]
                extra_out[-2][...] = err * (1.0 / d)
                part = jnp.sum(jnp.sum(err * err, axis=0, keepdims=True), axis=1, keepdims=True)
                extra_out[-1][...] += jnp.broadcast_to(0.5 / d * part, extra_out[-1].shape)

    row = pl.BlockSpec((tm, d), lambda i, c: (i, 0))
    vec = pl.BlockSpec((1, d), lambda i, c: (0, 0))
    shard = lambda rows, up: pl.BlockSpec((None, rows, FF_SHARD), lambda i, c: (c + up * nc, 0, 0))
    pair = pl.BlockSpec((2, None, tm, FF_SHARD), lambda i, c: (0, c, i, 0))
    lanes = pl.BlockSpec((8, LANES), lambda i, c: (0, 0))
    outs = _call(
        body, [x, x, g, w_in, w_in, cw, cb, cw, cb, w_out, *next_gains] + ([loss_target] if with_loss else []),
        grid=(t // tm, nc),
        in_specs=[row, pl.BlockSpec((HALO, d), lambda i, c: (jnp.maximum(i * (tm // HALO) - 1, 0), 0)),
                  vec, shard(d, 0), shard(d, 1), shard(8, 0), shard(1, 0), shard(8, 1), shard(1, 1),
                  pl.BlockSpec((FF_SHARD, d), lambda i, c: (c, 0))] + [vec] * n_gains + [row] * with_loss,
        out_specs=[row, row, pair, pair] + [row] * n_gains + [row, lanes] * with_loss,
        out_shape=[jax.ShapeDtypeStruct((t, d), F32), jax.ShapeDtypeStruct((t, d), BF16),
                   jax.ShapeDtypeStruct((2, nc, t, FF_SHARD), BF16), jax.ShapeDtypeStruct((2, nc, t, FF_SHARD), F32)]
        + [jax.ShapeDtypeStruct((t, d), BF16)] * n_gains
        + [jax.ShapeDtypeStruct((t, d), F32), jax.ShapeDtypeStruct((8, LANES), F32)] * with_loss,
        scratch=[pltpu.VMEM((tm + HALO, d), BF16)], name=f"ffn{layer}_fwd", sem=("arbitrary", "arbitrary"), carry=carry)
    return (outs[0], outs[1], outs[2].reshape(N_SHARDS, t, FF_SHARD), outs[3]) + tuple(outs[4:])


def _ffn_bwd_act(pre, w_out, dxn, layer, tm=512, carry=None):
    t, d = dxn.shape
    nc = N_SHARDS // 2

    def body(pre_ref, wout_ref, dx_ref, dhu_ref, dw_ref, dcb_ref):
        i = pl.program_id(1)

        @pl.when(i == 0)
        def _():
            dw_ref[...] = jnp.zeros_like(dw_ref)
            dcb_ref[...] = jnp.zeros_like(dcb_ref)

        hg, hu = pre_ref[0], pre_ref[1]
        sg = _sigmoid(hg)
        sl = hg * sg
        dxb = dx_ref[...].astype(BF16)
        dact = _dot(dxb, wout_ref[...], NT)
        dw_ref[...] += _dot((sl * hu).astype(BF16), dxb, TN)
        d_up = dact * sl
        d_gate = dact * hu * (sg * (1.0 + hg * (1.0 - sg)))
        for j, dv in enumerate((d_gate, d_up)):
            dhu_ref[j] = dv.astype(BF16)
            dcb_ref[j, 0:1, :] += jnp.sum(dv, axis=0, keepdims=True)

    return _call(
        body, [pre, w_out, dxn], grid=(nc, t // tm),
        in_specs=[pl.BlockSpec((2, None, tm, FF_SHARD), lambda c, i: (0, c, i, 0)),
                  pl.BlockSpec((FF_SHARD, d), lambda c, i: (c, 0)), pl.BlockSpec((tm, d), lambda c, i: (i, 0))],
        out_specs=[pl.BlockSpec((None, 2, tm, FF_SHARD), lambda c, i: (c, 0, i, 0)),
                   pl.BlockSpec((FF_SHARD, d), lambda c, i: (c, 0)),
                   pl.BlockSpec((None, 2, 8, FF_SHARD), lambda c, i: (c, 0, 0, 0))],
        out_shape=[jax.ShapeDtypeStruct((nc, 2, t, FF_SHARD), BF16), jax.ShapeDtypeStruct((D_FF, d), F32),
                   jax.ShapeDtypeStruct((nc, 2, 8, FF_SHARD), F32)],
        name=f"ffn{layer}_bwd_act", sem=("parallel", "arbitrary"), carry=carry)


def _ffn_bwd_in(dhu, a, cw, w_in, layer, tm=1024, carry=None, norm=None):
    nc, _, t, _ = dhu.shape
    d = D_MODEL
    tm = min(tm, t)
    last_blk = t // 16 - 1
    n_norm = 0 if norm is None else 3

    def body(dh_ref, nx_ref, a_ref, cw_ref, win_ref, *rest):
        norm_refs, (da_ref, o_ref, dcw_ref), dg_refs = rest[:n_norm], rest[n_norm:n_norm + 3], rest[n_norm + 3:]
        i, s = pl.program_id(0), pl.program_id(1)

        @pl.when(s == 0)
        def _():
            o_ref[...] = jnp.zeros_like(o_ref)

        @pl.when((s == 0) & (i == 0))
        def _():
            dcw_ref[...] = jnp.zeros_like(dcw_ref)

        keep = jnp.where(i == t // tm - 1, 0.0, 1.0)
        cw = cw_ref[...]
        sums = [None] * 3
        for r0 in range(0, tm, ROW_CHUNK):
            rows = slice(r0, r0 + ROW_CHUNK)
            if r0 + ROW_CHUNK == tm:
                win = jnp.concatenate([dh_ref[rows, :].astype(F32), nx_ref[...].astype(F32) * keep], axis=0)
            else:
                win = dh_ref[r0:r0 + ROW_CHUNK + HALO, :].astype(F32)
            n = ROW_CHUNK + HALO
            taps = (pltpu.roll(win, n - 2, 0)[:ROW_CHUNK],
                    pltpu.roll(win, n - 1, 0)[:ROW_CHUNK],
                    win[:ROW_CHUNK])
            da = (cw[0:1, :] * taps[0] + cw[1:2, :] * taps[1] + cw[2:3, :] * taps[2]).astype(BF16)
            da_ref[rows, :] = da
            o_ref[rows, :] += _dot(da, win_ref[...], NT)
            af = a_ref[rows, :].astype(F32)
            parts = [jnp.sum(taps[k] * af, axis=0, keepdims=True) for k in range(3)]
            sums = [p if q is None else q + p for q, p in zip(sums, parts)]
        for k in range(3):
            dcw_ref[pl.ds(s, 1), k:k + 1, :] += sums[k][None]

        if norm is not None:
            x_ref, g_ref, dres_ref = norm_refs
            dg_ref = dg_refs[0]

            @pl.when((s == 0) & (i == 0))
            def _():
                dg_ref[...] = jnp.zeros_like(dg_ref)

            @pl.when(s == N_SHARDS - 1)
            def _():
                xf = x_ref[...]
                r = lax.rsqrt(jnp.mean(xf * xf, axis=-1, keepdims=True) + EPS)
                xhat = xf * r
                dh = o_ref[...]
                dg_ref[0:1, :] += jnp.sum(dh * xhat, axis=0, keepdims=True)
                gy = dh * g_ref[...]
                o_ref[...] = dres_ref[...] + r * (gy - xhat * jnp.mean(gy * xhat, axis=-1, keepdims=True))

    row = pl.BlockSpec((tm, d), lambda i, s: (i, 0))
    norm_args = [] if norm is None else list(norm)
    norm_specs = [] if norm is None else [row, pl.BlockSpec((1, d), lambda i, s: (0, 0)), row]
    return _call(
        body, [dhu, dhu, a, cw, w_in] + norm_args, grid=(t // tm, N_SHARDS),
        in_specs=[pl.BlockSpec((None, None, tm, FF_SHARD), lambda i, s: (s % nc, s // nc, i, 0)),
                  pl.BlockSpec((None, None, 16, FF_SHARD),
                               lambda i, s: (s % nc, s // nc, jnp.minimum((i + 1) * (tm // 16), last_blk), 0)),
                  pl.BlockSpec((None, tm, FF_SHARD), lambda i, s: (s, i, 0)),
                  pl.BlockSpec((None, 8, FF_SHARD), lambda i, s: (s, 0, 0)),
                  pl.BlockSpec((None, d, FF_SHARD), lambda i, s: (s, 0, 0))] + norm_specs,
        out_specs=[pl.BlockSpec((None, tm, FF_SHARD), lambda i, s: (s, i, 0)), row,
                   pl.BlockSpec((N_SHARDS, 8, FF_SHARD), lambda i, s: (0, 0, 0))]
        + ([] if norm is None else [pl.BlockSpec((8, d), lambda i, s: (0, 0))]),
        out_shape=[jax.ShapeDtypeStruct((N_SHARDS, t, FF_SHARD), BF16), jax.ShapeDtypeStruct((t, d), F32),
                   jax.ShapeDtypeStruct((N_SHARDS, 8, FF_SHARD), F32)]
        + ([] if norm is None else [jax.ShapeDtypeStruct((8, d), F32)]),
        name=f"ffn{layer}_bwd_in", sem=("arbitrary", "arbitrary"), carry=carry)


def _ffn_wgrad_in(hf, da, layer, carry=None):
    t, d = hf.shape
    return _mm(
        da, hf, pl.BlockSpec((None, t, FF_SHARD), lambda s, j, kk: (s, 0, 0)),
        pl.BlockSpec((t, d), lambda s, j, kk: (0, 0)),
        pl.BlockSpec((None, FF_SHARD, d), lambda s, j, kk: (s, 0, 0)),
        jax.ShapeDtypeStruct((N_SHARDS, FF_SHARD, d), F32), (N_SHARDS, 1, 1), TN, f"ffn{layer}_wgrad_in",
        carry=carry)


Q_PER_KV = N_Q_HEADS // N_KV_HEADS
GROUP_ROWS = Q_PER_KV * CHUNK


def _lane_half():
    return lax.broadcasted_iota(jnp.int32, (CHUNK, LANES), 1) < HEAD_DIM


def _fill_attn_bias(bias_s):
    tq = lax.broadcasted_iota(jnp.int32, (GROUP_ROWS, 2 * CHUNK), 0) & (CHUNK - 1)
    jk = lax.broadcasted_iota(jnp.int32, (GROUP_ROWS, 2 * CHUNK), 1)
    dist = tq + CHUNK - jk
    window = (dist >= 0) & (dist < CHUNK)
    distf = dist.astype(F32)
    for kvh in range(N_KV_HEADS):
        alibi = _per_head_column([-SLOPES[h] for h in range(Q_PER_KV * kvh, Q_PER_KV * (kvh + 1))]) * distf
        bias_s[0, kvh] = jnp.where(window & (jk >= CHUNK), alibi, NEG_BIG)
        bias_s[1, kvh] = jnp.where(window, alibi, NEG_BIG)


def _per_head_column(values):
    r = lax.broadcasted_iota(jnp.int32, (GROUP_ROWS, 1), 0)
    col = jnp.full((GROUP_ROWS, 1), values[Q_PER_KV - 1], F32)
    for j in range(Q_PER_KV - 2, -1, -1):
        col = jnp.where(r < (j + 1) * CHUNK, values[j], col)
    return col


def _half_sum(x, lo):
    s_lo = jnp.sum(jnp.where(lo, x, 0.0), axis=-1, keepdims=True)
    s_hi = jnp.sum(jnp.where(lo, 0.0, x), axis=-1, keepdims=True)
    return jnp.where(lo, s_lo, s_hi)


def _stack_heads(pairs, lo):
    zero = jnp.zeros_like(pairs[0])
    return jnp.concatenate([jnp.where(lo, pairs[0], zero), jnp.where(lo, zero, pairs[0]),
                            jnp.where(lo, pairs[1], zero), jnp.where(lo, zero, pairs[1])], axis=0)


def _unstack_heads(stacked, lo):
    return (jnp.where(lo, stacked[0:CHUNK], stacked[CHUNK:2 * CHUNK]),
            jnp.where(lo, stacked[2 * CHUNK:3 * CHUNK], stacked[3 * CHUNK:]))


def _attn_probs(qs, kn, bias, sink_col):
    s = _dot(qs, kn, NT) * (HEAD_DIM ** -0.5) + bias
    m = jnp.maximum(jnp.max(s, axis=-1, keepdims=True), sink_col)
    e = jnp.exp(s - m)
    den = jnp.sum(e, axis=-1, keepdims=True) + jnp.exp(sink_col - m)
    return e * (1.0 / den), m, den


def _attn_fwd(qraw, kvd, gq, gk, sinks, carry=None):
    t, d = qraw.shape
    nb = t // CHUNK

    def body(sink_ref, q_ref, cur_ref, prev_ref, gq_ref, gk_ref, o_ref, bias_s):
        n = pl.program_id(0)

        @pl.when(n == 0)
        def _():
            _fill_attn_bias(bias_s)

        lo = _lane_half()
        which = jnp.where(n == 0, 0, 1)
        gq_v, gk_v = gq_ref[...], gk_ref[...]
        for kvh in range(N_KV_HEADS):
            ks = slice(kvh * LANES, (kvh + 1) * LANES)
            vs = slice(4 * LANES + kvh * LANES, 4 * LANES + (kvh + 1) * LANES)
            kraw = jnp.concatenate([prev_ref[:, ks], cur_ref[:, ks]], axis=0)
            rk = lax.rsqrt(jnp.mean(kraw * kraw, axis=-1, keepdims=True) + EPS)
            kn = (kraw * rk * gk_v).astype(BF16)
            vv = jnp.concatenate([prev_ref[:, vs], cur_ref[:, vs]], axis=0).astype(BF16)
            qn = []
            for p in range(2):
                qp = q_ref[:, (2 * kvh + p) * LANES:(2 * kvh + p + 1) * LANES]
                r = lax.rsqrt(_half_sum(qp * qp, lo) * (1.0 / HEAD_DIM) + EPS)
                qn.append(qp * r * gq_v)
            heads = range(Q_PER_KV * kvh, Q_PER_KV * (kvh + 1))
            pf, _, _ = _attn_probs(_stack_heads(qn, lo).astype(BF16), kn, bias_s[which, kvh],
                                   _per_head_column([sink_ref[h] for h in heads]))
            for p, o_pair in enumerate(_unstack_heads(_dot(pf.astype(BF16), vv), lo)):
                o_ref[:, (2 * kvh + p) * LANES:(2 * kvh + p + 1) * LANES] = o_pair.astype(BF16)

    blk = lambda f: pl.BlockSpec((CHUNK, d), f)
    vec = pl.BlockSpec((1, LANES), lambda n: (0, 0))
    return _call(
        body, [sinks, qraw, kvd, kvd, gq, gk], grid=(nb,),
        in_specs=[pl.BlockSpec(memory_space=pltpu.SMEM), blk(lambda n: (n, 0)), blk(lambda n: (n, 0)),
                  blk(lambda n: (jnp.maximum(n - 1, 0), 0)), vec, vec],
        out_specs=[blk(lambda n: (n, 0))], out_shape=[jax.ShapeDtypeStruct((t, d), BF16)],
        scratch=[pltpu.VMEM((2, N_KV_HEADS, GROUP_ROWS, 2 * CHUNK), F32)], name="attn_fwd", sem=("arbitrary",),
        carry=carry)[0]


def _attn_bwd(qraw, kvd, d_o, gq, gk, sinks, carry=None):
    t, d = qraw.shape
    nb = t // CHUNK

    def body(sink_ref, q_ref, cur_ref, prev_ref, do_ref, gq_ref, gk_ref,
             dq_ref, dkv_ref, dsink_ref, dgq_ref, dgk_ref, carry_s, pp_s, cp_s, bias_s):
        n = pl.program_id(0)

        @pl.when(n == 0)
        def _():
            carry_s[...] = jnp.zeros_like(carry_s)
            dsink_ref[...] = jnp.zeros_like(dsink_ref)
            dgq_ref[...] = jnp.zeros_like(dgq_ref)
            dgk_ref[...] = jnp.zeros_like(dgk_ref)
            _fill_attn_bias(bias_s)

        @pl.when(n < nb)
        def _():
            lo = _lane_half()
            which = jnp.where(n == 0, 0, 1)
            gq_v, gk_v = gq_ref[...], gk_ref[...]
            for kvh in range(N_KV_HEADS):
                ks = slice(kvh * LANES, (kvh + 1) * LANES)
                vs = slice(4 * LANES + kvh * LANES, 4 * LANES + (kvh + 1) * LANES)
                kraw = jnp.concatenate([prev_ref[:, ks], cur_ref[:, ks]], axis=0)
                rk = lax.rsqrt(jnp.mean(kraw * kraw, axis=-1, keepdims=True) + EPS)
                khat = kraw * rk
                kn = (khat * gk_v).astype(BF16)
                vv = jnp.concatenate([prev_ref[:, vs], cur_ref[:, vs]], axis=0).astype(BF16)
                cols = [slice((2 * kvh + p) * LANES, (2 * kvh + p + 1) * LANES) for p in range(2)]
                rq, qhat = [], []
                for p in range(2):
                    qp = q_ref[:, cols[p]]
                    rq.append(lax.rsqrt(_half_sum(qp * qp, lo) * (1.0 / HEAD_DIM) + EPS))
                    qhat.append(qp * rq[p])
                heads = range(Q_PER_KV * kvh, Q_PER_KV * (kvh + 1))
                qs = _stack_heads([qhat[p] * gq_v for p in range(2)], lo).astype(BF16)
                dos = _stack_heads([do_ref[:, cols[p]] for p in range(2)], lo)
                sink_col = _per_head_column([sink_ref[h] for h in heads])
                pf, m, den = _attn_probs(qs, kn, bias_s[which, kvh], sink_col)
                dp = _dot(dos, vv, NT)
                delta = jnp.sum(pf * dp, axis=-1, keepdims=True)
                sink_delta = jnp.exp(sink_col - m) / den * delta
                for j, h in enumerate(heads):
                    dsink_ref[h:h + 1, :] -= jnp.broadcast_to(
                        jnp.sum(sink_delta[j * CHUNK:(j + 1) * CHUNK], axis=0, keepdims=True), (1, LANES))
                ds = (pf * (dp - delta) * (HEAD_DIM ** -0.5)).astype(BF16)
                dkn = _dot(ds, qs, TN)
                dvb = _dot(pf.astype(BF16), dos, TN)
                for p, dqn in enumerate(_unstack_heads(_dot(ds, kn), lo)):
                    dgq_ref[0:1, :] += jnp.sum(dqn * qhat[p], axis=0, keepdims=True)
                    gy = dqn * gq_v
                    mq = _half_sum(gy * qhat[p], lo) * (1.0 / HEAD_DIM)
                    dq_ref[:, cols[p]] = (rq[p] * (gy - qhat[p] * mq)).astype(BF16)
                dgk_ref[0:1, :] += jnp.sum(dkn * khat, axis=0, keepdims=True)
                gyk = dkn * gk_v
                dkraw = rk * (gyk - khat * jnp.mean(gyk * khat, axis=-1, keepdims=True))
                pp_s[:, ks] = dkraw[:CHUNK]
                cp_s[:, ks] = dkraw[CHUNK:]
                pp_s[:, vs] = dvb[:CHUNK]
                cp_s[:, vs] = dvb[CHUNK:]
            dkv_ref[...] = (carry_s[...] + pp_s[...]).astype(BF16)
            carry_s[...] = cp_s[...]

        @pl.when(n == nb)
        def _():
            dkv_ref[...] = carry_s[...].astype(BF16)

    blk = lambda f: pl.BlockSpec((CHUNK, d), f)
    vec = pl.BlockSpec((1, LANES), lambda n: (0, 0))
    cur = lambda n: (jnp.minimum(n, nb - 1), 0)
    prev = lambda n: (jnp.maximum(jnp.minimum(n, nb - 1) - 1, 0), 0)
    small = lambda r: pl.BlockSpec((r, LANES), lambda n: (0, 0))
    return _call(
        body, [sinks, qraw, kvd, kvd, d_o, gq, gk], grid=(nb + 1,),
        in_specs=[pl.BlockSpec(memory_space=pltpu.SMEM), blk(cur), blk(cur), blk(prev), blk(cur), vec, vec],
        out_specs=[blk(cur), blk(lambda n: (jnp.maximum(n - 1, 0), 0)), small(N_Q_HEADS), small(8), small(8)],
        out_shape=[jax.ShapeDtypeStruct((t, d), BF16), jax.ShapeDtypeStruct((t, d), BF16),
                   jax.ShapeDtypeStruct((N_Q_HEADS, LANES), F32), jax.ShapeDtypeStruct((8, LANES), F32),
                   jax.ShapeDtypeStruct((8, LANES), F32)],
        scratch=[pltpu.VMEM((CHUNK, d), F32)] * 3 + [pltpu.VMEM((2, N_KV_HEADS, GROUP_ROWS, 2 * CHUNK), F32)],
        name="attn_bwd", sem=("arbitrary",), carry=carry)


def _adamw_math(g, w, m, v):
    m = ADAM_B1 * m + (1.0 - ADAM_B1) * g
    v = ADAM_B2 * v + (1.0 - ADAM_B2) * (g * g)
    m_hat = m / (1.0 - ADAM_B1 ** ADAM_STEP)
    v_hat = v / (1.0 - ADAM_B2 ** ADAM_STEP)
    delta = -ADAM_LR * (m_hat / (jnp.sqrt(v_hat) + ADAM_EPS) + ADAM_WD * w)
    return delta, m, v


def _row_tile(r, cap=128):
    for tr in range(min(r, cap), 0, -1):
        if r % tr == 0 and (tr % 8 == 0 or tr == r):
            return tr
    return r


def _chip_sum(grad, recv, place, name, wire_dtype):
    _, r, c = grad.shape
    tr = _row_tile(r, 256)

    def body(pl_ref, g_ref, a_ref, p_ref):
        p_ref[...] = (g_ref[...] + a_ref[...]).astype(p_ref.dtype)

    return pl.pallas_call(
        body,
        grid_spec=pltpu.PrefetchScalarGridSpec(
            num_scalar_prefetch=1, grid=(4, r // tr),
            in_specs=[pl.BlockSpec((None, None, tr, c), lambda q, i, pr: (q, pr[1], i, 0)),
                      pl.BlockSpec((None, tr, c), lambda q, i, pr: (q, i, 0))],
            out_specs=pl.BlockSpec((None, tr, c), lambda q, i, pr: (q, i, 0))),
        out_shape=jax.ShapeDtypeStruct((4, r, c), wire_dtype), name=name, compiler_params=_params(),
    )(place, grad.reshape(4, 2, r, c), recv)


def _adamw_sharded(grad, recv, others, place, w, m, v, name, layer=None, fill=None):
    r, c = w.shape[-2:]
    tr = _row_tile(r)

    def body(pl_ref, g_ref, a_ref, oth_ref, w_ref, m_ref, v_ref, *rest):
        g_out, d_out, nm_out, nv_out = rest[-4:]
        g = g_ref[...] + a_ref[...]
        for k in range(3):
            g = g + oth_ref[k].astype(F32)
        delta, nm, nv = _adamw_math(g, w_ref[...], m_ref[...], v_ref[...])
        g_out[...] = g
        d_out[...] = delta
        nm_out[...] = nm
        nv_out[...] = nv

    if layer is None:
        row = pl.BlockSpec((tr, c), lambda i, pr: (i, 0))
    else:
        row = pl.BlockSpec((None, tr, c), lambda i, pr: (layer, i, 0))
    n_fill = 0 if fill is None else 4
    in_specs = [pl.BlockSpec((None, None, tr, c), lambda i, pr: (pr[0], pr[1], i, 0)),
                pl.BlockSpec((None, tr, c), lambda i, pr: (pr[0], i, 0)),
                pl.BlockSpec((3, tr, c), lambda i, pr: (0, i, 0)), row, row, row]
    in_specs += [pl.BlockSpec(memory_space=pl.ANY)] * n_fill
    return pl.pallas_call(
        body,
        grid_spec=pltpu.PrefetchScalarGridSpec(
            num_scalar_prefetch=1, grid=(r // tr,), in_specs=in_specs, out_specs=[row] * 4),
        out_shape=[jax.ShapeDtypeStruct(w.shape, F32)] * 4, name=name, compiler_params=_params(),
        input_output_aliases={7 + j: j for j in range(n_fill)},
    )(place, grad.reshape(4, 2, r, c), recv, others, w, m, v, *([] if fill is None else fill))


def _sum_devices(parts, name):
    def body(p_ref, o_ref):
        total = p_ref[0]
        for k in range(1, N_SHARDS):
            total = total + p_ref[k]
        o_ref[...] = total

    return pl.pallas_call(body, out_shape=jax.ShapeDtypeStruct(parts.shape[1:], F32), name=name)(parts)


def _adamw_summed(parts, ws, ms, vs, name):
    n = len(parts)

    def body(*refs):
        p_refs, w_refs, m_refs, v_refs = refs[:n], refs[n:2 * n], refs[2 * n:3 * n], refs[3 * n:4 * n]
        o_refs = refs[4 * n:]
        for i in range(n):
            g = p_refs[i][0]
            for k in range(1, N_SHARDS):
                g = g + p_refs[i][k]
            delta, nm, nv = _adamw_math(g, w_refs[i][...], m_refs[i][...], v_refs[i][...])
            o_refs[4 * i][...] = g
            o_refs[4 * i + 1][...] = delta
            o_refs[4 * i + 2][...] = nm
            o_refs[4 * i + 3][...] = nv

    shapes = [jax.ShapeDtypeStruct(w.shape, F32) for w in ws for _ in range(4)]
    outs = pl.pallas_call(body, out_shape=shapes, name=name, compiler_params=_params())(*parts, *ws, *ms, *vs)
    return [outs[4 * i:4 * i + 4] for i in range(n)]


def _dup_heads(w):
    lead = w.shape[:-1]
    w4 = w.reshape(lead + (N_KV_HEADS, 1, HEAD_DIM))
    return jnp.broadcast_to(w4, lead + (N_KV_HEADS, 2, HEAD_DIM)).reshape(lead + (N_KV_HEADS * LANES,))


def _fold_heads(g):
    lead = g.shape[:-1]
    return g.reshape(lead + (N_KV_HEADS, 2, HEAD_DIM)).sum(axis=-2).reshape(lead + (N_KV_HEADS * HEAD_DIM,))


def kernel(x, a_norm, a_w_in, a_v_norm, a_w_s, a_b_s, a_w_out, f_norm, f_w_in, f_conv_w, f_conv_b, f_w_out, kv_norm, w_kv, k_norm, b_norm, b_w_q, b_q_norm, b_sinks, b_w_o, loss_target, m_a_norm, m_a_w_in, m_a_v_norm, m_a_w_s, m_a_b_s, m_a_w_out, m_f_norm, m_f_w_in, m_f_conv_w, m_f_conv_b, m_f_w_out, m_kv_norm, m_w_kv, m_k_norm, m_b_norm, m_b_w_q, m_b_q_norm, m_b_sinks, m_b_w_o, v_a_norm, v_a_w_in, v_a_v_norm, v_a_w_s, v_a_b_s, v_a_w_out, v_f_norm, v_f_w_in, v_f_conv_w, v_f_conv_b, v_f_w_out, v_kv_norm, v_w_kv, v_k_norm, v_b_norm, v_b_w_q, v_b_q_norm, v_b_sinks, v_b_w_o):
    d = D_MODEL
    xi, yi, ci = _coords()
    place = jnp.stack([2 * xi + yi, ci]).astype(jnp.int32)
    bf = lambda a: a.astype(BF16)
    row = lambda v_: v_.reshape(1, -1)
    x0, target = x[0], loss_target[0]
    t = x0.shape[0]
    res = {}

    red = {}

    def to_sibling(grads, wire=BF16):
        for k, g in grads.items():
            red[k] = dict(grad=g, wire=wire)
        ex = _ToSibling(list(grads.values()))
        ex.names = list(grads)
        return ex

    def to_chips(ex):
        for k, a in zip(ex.names, ex.results):
            red[k]["recv"] = a
            red[k]["psum"] = _chip_sum(red[k]["grad"], a, place, f"chip_sum_{k}", red[k]["wire"])
        nxt = _ToChips([red[k]["psum"] for k in ex.names])
        nxt.names = ex.names
        return nxt

    def landed(ex):
        for k, b in zip(ex.names, ex.results):
            red[k]["others"] = b

    def halves(ex, first_rows):
        parts = []
        for r0, nr in ((0, first_rows), (first_rows, ex.srcs[0].shape[1] - first_rows)):
            part = _ToChips(ex.srcs, rows=(r0, nr))
            part.names = ex.names
            parts.append(part)
        return parts

    def landed_halves(parts):
        for j, k in enumerate(parts[0].names):
            red[k]["others"] = jnp.concatenate([p.results[j] for p in parts], axis=1)

    def update(k, w, m, v, layer=None, fill=None):
        r = red[k]
        return _adamw_sharded(r["grad"], r["recv"], r["others"], place, w, m, v,
                              f"adamw_{k}", layer=layer, fill=fill)

    g_a_in, g_a_out, g_a_norm, g_a_v_norm, g_conv = _exchange_alone(
        _Gather([bf(a_w_in[0]), bf(a_w_out[0]), a_norm, a_v_norm, f_conv_w.reshape(6, FF_SHARD)]), "gather_first")
    a_norm_full, a_v_norm_full = g_a_norm.reshape(1, d), g_a_v_norm.reshape(1, d)
    conv_w = lax.reduce_precision(g_conv.reshape(N_SHARDS, 2, 3, FF_SHARD), 8, 7)
    cw = jnp.pad(jnp.transpose(conv_w, (1, 0, 2, 3)), ((0, 0), (0, 0), (0, 5), (0, 0)))
    w_a_in_flat = jnp.transpose(g_a_in, (1, 0, 2)).reshape(d, 2 * d)
    cb = f_conv_b.reshape(2, N_SHARDS, 1, FF_SHARD)
    tri = jnp.tril(jnp.ones((CHUNK, CHUNK), dtype=bool))
    w_causal = jnp.where(tri[None], a_w_s[0], 0.0).astype(BF16)
    w_causal_t = jnp.transpose(w_causal, (0, 2, 1))
    b_sb = jnp.broadcast_to(a_b_s[0][:, :, None], (N_GROUPS, CHUNK, CHUNK))
    w_a_out = g_a_out.reshape(d, d)
    gq = jnp.tile(b_q_norm.reshape(1, HEAD_DIM), (1, 2))
    gk = jnp.tile(k_norm.reshape(1, HEAD_DIM), (1, 2))
    sinks = b_sinks.reshape(N_Q_HEADS)

    ex = _Gather([bf(f_w_in[0]), bf(f_w_out[0])])
    zpre, x1, h1 = _sgu_fwd(x0, a_norm_full, g_a_in, a_v_norm_full, w_causal, b_sb, w_a_out, carry=ex)
    w_in0, w_out0 = ex.results[0], ex.results[1].reshape(D_FF, d)
    ex = _Gather([bf(w_kv), bf(b_w_q[0]), bf(b_w_o[0]), bf(f_w_in[1])], relay=False, early=True)
    x2, hf0, a0, pre0, hk, hq = _ffn_fwd(x1, f_norm[0:1], w_in0, cw[0], cb[0], w_out0, 0, carry=ex,
                                         next_gains=[row(kv_norm), b_norm])
    kv_full = ex.results[0].reshape(d, 2 * N_KV_HEADS * HEAD_DIM)
    w_q, w_o = ex.results[1].reshape(d, d), ex.results[2].reshape(d, d)
    w_in1 = ex.results[3]
    half = N_KV_HEADS * HEAD_DIM
    w_kv_dup = jnp.concatenate([_dup_heads(kv_full[:, :half]), _dup_heads(kv_full[:, half:])], axis=1)
    kvd = _mm_rows(hk, w_kv_dup, F32, "kv_proj")
    qraw = _mm_rows(hq, w_q, F32, "q_proj")
    ex = _Gather([bf(f_w_out[1])], relay=False, early=True)
    o = _attn_fwd(qraw, kvd, gq, gk, sinks, carry=ex)
    w_out1 = ex.results[0].reshape(D_FF, d)
    x3 = _mm_rows(o, w_o, F32, "o_proj", res=x2)
    _, hf1, a1, pre1, dy, loss_lanes = _ffn_fwd(x3, f_norm[1:2], w_in1, cw[1], cb[1], w_out1, 1, loss_target=target)

    dhu1, dw_out1, dcb1 = _ffn_bwd_act(pre1, w_out1, dy, 1)
    ex = to_sibling({"f_w_out1": dw_out1.reshape(N_SHARDS, D_FF // N_SHARDS, d)})
    da1, dx3, dcw1, dgf1 = _ffn_bwd_in(dhu1, a1, cw[1], w_in1, 1, carry=ex, norm=(x3, f_norm[1:2], dy))
    ex = to_chips(ex)
    dw_in1 = _ffn_wgrad_in(hf1, da1, 1, carry=ex)
    landed(ex)
    ex = to_sibling({"f_w_in1": dw_in1})
    d_o = _mm_rows(dx3, w_o, BF16, "o_proj_bwd", trans_w=True, carry=ex)
    ex = to_chips(ex)
    dw_o = _mm_wgrad(o, dx3, "o_wgrad").reshape(N_SHARDS, d // N_SHARDS, d)
    dq, dkv, dsink, dgq, dgk = _attn_bwd(qraw, kvd, d_o, gq, gk, sinks, carry=ex)
    landed(ex)
    dw_q = _mm_wgrad(hq, dq, "q_wgrad").reshape(N_SHARDS, d // N_SHARDS, d)
    dw_kv_dup = _mm_wgrad(hk, dkv, "kv_wgrad")
    dw_kv = jnp.concatenate(
        [_fold_heads(dw_kv_dup[:, :4 * LANES]), _fold_heads(dw_kv_dup[:, 4 * LANES:])], axis=1
    ).reshape(N_SHARDS, d // N_SHARDS, 2 * N_KV_HEADS * HEAD_DIM)
    ex = to_sibling({"b_w_o": dw_o, "b_w_q": dw_q, "w_kv": dw_kv})
    dx2, dg2 = _rms_bwd(x2, [row(kv_norm), b_norm], [dkv, dq], dx3, "kvq_norm_bwd", tm=512, carry=ex,
                        through=[w_kv_dup, w_q])
    ex = to_chips(ex)
    dhu0, dw_out0, dcb0 = _ffn_bwd_act(pre0, w_out0, dx2, 0, carry=ex)
    landed(ex)
    ex = to_sibling({"f_w_out0": dw_out0.reshape(N_SHARDS, D_FF // N_SHARDS, d)})
    da0, dhf0, dcw0 = _ffn_bwd_in(dhu0, a0, cw[0], w_in0, 0, carry=ex)
    ex = to_chips(ex)
    dw_in0 = _ffn_wgrad_in(hf0, da0, 0, carry=ex)
    landed(ex)
    ex = to_sibling({"f_w_in0": dw_in0})
    dx1, dgf0 = _rms_bwd(x1, [f_norm[0:1]], [dhf0], dx2, "f0_norm_bwd", carry=ex)
    ex_lo, ex_hi = halves(to_chips(ex), 448)
    dz, y, dwc, dbs, dgv = _sgu_bwd(dx1, zpre, w_a_out, a_v_norm_full, w_causal, w_causal_t, b_sb, carry=ex_lo)
    dw_a_out = _mm_wgrad(y, dx1, "a_out_wgrad").reshape(N_SHARDS, d // N_SHARDS, d)
    nsub = g_a_in.shape[2]
    dw_a_in = _mm(
        h1, dz, pl.BlockSpec((t, d), lambda s, j, kk: (0, 0)), pl.BlockSpec((t, nsub), lambda s, j, kk: (0, s)),
        pl.BlockSpec((None, d, nsub), lambda s, j, kk: (s, 0, 0)), jax.ShapeDtypeStruct((N_SHARDS, d, nsub), F32),
        (N_SHARDS, 1, 1), TN, "a_in_wgrad", carry=ex_hi)
    landed_halves([ex_lo, ex_hi])

    def bias_grad(dcb):
        return jnp.transpose(dcb[:, :, 0, :], (1, 0, 2)).reshape(-1)

    g_conv_w = jnp.concatenate([dcw0[:, 0:3, :], dcw1[:, 0:3, :]], axis=1)
    g_a_v_norm = dgv[0].reshape(N_SHARDS, 1, LANES)
    rep = ["a_w_s", "a_b_s", "f_norm", "f_conv_b", "kv_norm", "k_norm", "b_norm", "b_q_norm", "b_sinks"]
    rep_g = dict(
        a_w_s=dwc.reshape(N_GROUPS * CHUNK, CHUNK), a_b_s=dbs[:, :, 0], f_norm=jnp.stack([dgf0[0], dgf1[0]]),
        f_conv_b=jnp.stack([bias_grad(dcb0), bias_grad(dcb1)]), kv_norm=dg2[0:1],
        k_norm=(dgk[0, :HEAD_DIM] + dgk[0, HEAD_DIM:])[None], b_norm=dg2[1:2],
        b_q_norm=(dgq[0, :HEAD_DIM] + dgq[0, HEAD_DIM:])[None], b_sinks=dsink[:, 0][None])
    ex_big = to_sibling({"a_w_out": dw_a_out, "a_w_in": dw_a_in})
    ex_small = to_sibling({"a_v_norm": g_a_v_norm, "f_conv_w": g_conv_w}, wire=F32)
    ex_rep = _Gather([rep_g[k] for k in rep] + [loss_lanes], relay=False)
    together = _Together([ex_big, ex_small, ex_rep])
    dh1 = _mm_rows(dz, w_a_in_flat, F32, "a_in_bwd", trans_w=True, carry=together)
    together.spread()
    ex_big, ex_small = to_chips(ex_big), to_chips(ex_small)
    together = _Together([ex_big, ex_small])
    grad_x, dg0 = _rms_bwd(x0, [a_norm_full], [dh1], dx1, "a_norm_bwd", carry=together)
    together.spread()
    landed(ex_big)
    landed(ex_small)
    (a_norm_parts,) = _exchange_alone(_ToOwners([dg0[0].reshape(N_SHARDS, 1, LANES)]), "a_norm_to_owners")

    res["f_w_out"] = update("f_w_out1", f_w_out, m_f_w_out, v_f_w_out, layer=1)
    w_in_t = [jnp.swapaxes(a_, 1, 2) for a_ in (f_w_in, m_f_w_in, v_f_w_in)]
    res["f_w_in"] = update("f_w_in1", *w_in_t, layer=1)
    res["b_w_o"] = update("b_w_o", b_w_o, m_b_w_o, v_b_w_o, layer=0)
    res["b_w_q"] = update("b_w_q", b_w_q, m_b_w_q, v_b_w_q, layer=0)
    res["w_kv"] = update("w_kv", w_kv, m_w_kv, v_w_kv)
    res["f_w_out"] = update("f_w_out0", f_w_out, m_f_w_out, v_f_w_out, layer=0, fill=res["f_w_out"])
    res["f_w_in"] = [jnp.swapaxes(o_, 1, 2) for o_ in update("f_w_in0", *w_in_t, layer=0, fill=res["f_w_in"])]
    res["a_w_out"] = update("a_w_out", a_w_out, m_a_w_out, v_a_w_out, layer=0)
    res["a_w_in"] = update("a_w_in", a_w_in, m_a_w_in, v_a_w_in, layer=0)
    res["a_v_norm"] = update("a_v_norm", a_v_norm, m_a_v_norm, v_a_v_norm)
    res["f_conv_w"] = [o_.reshape(f_conv_w.shape) for o_ in update(
        "f_conv_w", f_conv_w.reshape(6, FF_SHARD), m_f_conv_w.reshape(6, FF_SHARD), v_f_conv_w.reshape(6, FF_SHARD))]

    rep_w = dict(a_w_s=a_w_s, a_b_s=a_b_s, f_norm=f_norm, f_conv_b=f_conv_b, kv_norm=kv_norm, k_norm=k_norm,
                 b_norm=b_norm, b_q_norm=b_q_norm, b_sinks=b_sinks, a_norm=a_norm)
    rep_m = dict(a_w_s=m_a_w_s, a_b_s=m_a_b_s, f_norm=m_f_norm, f_conv_b=m_f_conv_b, kv_norm=m_kv_norm,
                 k_norm=m_k_norm, b_norm=m_b_norm, b_q_norm=m_b_q_norm, b_sinks=m_b_sinks, a_norm=m_a_norm)
    rep_v = dict(a_w_s=v_a_w_s, a_b_s=v_a_b_s, f_norm=v_f_norm, f_conv_b=v_f_conv_b, kv_norm=v_kv_norm,
                 k_norm=v_k_norm, b_norm=v_b_norm, b_q_norm=v_b_q_norm, b_sinks=v_b_sinks, a_norm=v_a_norm)
    keys = rep + ["a_norm"]
    loss = _sum_devices(ex_rep.results[-1], "loss_sum")[0, 0]
    parts = ex_rep.results[:-1] + [a_norm_parts]
    as2d = lambda a, p: a.reshape(p.shape[1:])
    rep_outs = _adamw_summed(parts, [as2d(rep_w[k], p) for k, p in zip(keys, parts)],
                             [as2d(rep_m[k], p) for k, p in zip(keys, parts)],
                             [as2d(rep_v[k], p) for k, p in zip(keys, parts)], "adamw_replicated")
    for j, key in enumerate(keys):
        res[key] = [o_.reshape(rep_w[key].shape) for o_ in rep_outs[j]]

    order = ["a_norm", "a_w_in", "a_v_norm", "a_w_s", "a_b_s", "a_w_out", "f_norm", "f_w_in", "f_conv_w", "f_conv_b",
             "f_w_out", "kv_norm", "w_kv", "k_norm", "b_norm", "b_w_q", "b_q_norm", "b_sinks", "b_w_o"]
    outs = [loss, grad_x[None]]
    for j in range(4):
        outs += [res[k][j] for k in order]
    return tuple(outs)
```

```python
import jax
import jax.numpy as jnp
from jax import lax
from jax.experimental import pallas as pl
from jax.experimental.pallas import tpu as pltpu

F32 = jnp.float32
BF16 = jnp.bfloat16
EPS = 1e-6
D_MODEL = 1024
CHUNK = 128
N_GROUPS = 8
N_SHARDS = 8
HEAD_DIM = 64
N_Q_HEADS = 16
N_KV_HEADS = 4
D_FF = 2816
FF_SHARD = 2 * D_FF // N_SHARDS
LANES = 128
NEG_BIG = -1e30
ADAM_LR = 0.001
ADAM_B1 = 0.9
ADAM_B2 = 0.999
ADAM_EPS = 1e-08
ADAM_WD = 0.01
ADAM_STEP = 10
VMEM_LIMIT_BYTES = 56 * 1024 * 1024
MESH = pl.DeviceIdType.MESH

NN = (((1,), (0,)), ((), ()))
NT = (((1,), (1,)), ((), ()))
TN = (((0,), (0,)), ((), ()))
SLOPES = tuple(2.0 ** (-8.0 * (h + 1) / N_Q_HEADS) for h in range(N_Q_HEADS))


def _params(sem=None):
    return pltpu.CompilerParams(dimension_semantics=sem, vmem_limit_bytes=VMEM_LIMIT_BYTES)


def _dot(a, b, dims=NN):
    return lax.dot_general(a, b, dims, preferred_element_type=F32)


def _sigmoid(x):
    return 1.0 / (1.0 + jnp.exp(-x))


def _gelu_parts(z):
    cdf = 0.5 * (1.0 + lax.erf(z * (2.0 ** -0.5)))
    pdf = jnp.exp(-0.5 * z * z) * 0.3989422804014327
    return cdf, pdf


def _coords():
    return lax.axis_index("x"), lax.axis_index("y"), lax.axis_index("c")


class _Gather:
    def __init__(self, srcs, relay=True, early=False):
        self.srcs = list(srcs)
        self.early = early
        n = len(self.srcs)
        self.relayed = [relay and s.shape[0] % 32 == 0 for s in self.srcs]
        self.out_shapes = [jax.ShapeDtypeStruct((N_SHARDS,) + s.shape, s.dtype) for s in self.srcs]
        self.sems = [pltpu.SemaphoreType.DMA((n, 9)), pltpu.SemaphoreType.DMA((n, 9)), pltpu.SemaphoreType.DMA((n,))]

    def _plan(self, src, dst, sems):
        send_sems, recv_sems, local_sems = sems
        x, y, c = _coords()
        n = len(src)

        def rows(e, dev, half=None):
            block = dst[e].at[4 * dev[0] + 2 * dev[1] + dev[2]]
            if half is None:
                return block
            nr = self.srcs[e].shape[0] // 2
            return block.at[pl.ds(half * nr, nr)]

        def copy(e, slot, block, to, half=None, from_own=False):
            return pltpu.make_async_remote_copy(
                src_ref=src[e] if from_own else rows(e, block, half), dst_ref=rows(e, block, half),
                send_sem=send_sems.at[e, slot], recv_sem=recv_sems.at[e, slot], device_id=to, device_id_type=MESH)

        return n, x, y, c, rows, copy, local_sems

    def start(self, src, dst, sems):
        n, x, y, c, rows, copy, local_sems = self._plan(src, dst, sems)
        me = (x, y, c)
        for e in range(n):
            pltpu.make_async_copy(src[e], rows(e, me), local_sems.at[e]).start()
            copy(e, 0, me, (x, y, 1 - c), from_own=True).start()
            copy(e, 1, me, (1 - x, y, c), from_own=True).start()
            copy(e, 2, me, (x, 1 - y, c), from_own=True).start()
            if not self.relayed[e]:
                copy(e, 3, me, (1 - x, 1 - y, c), from_own=True).start()

    def pass_on(self, src, dst, sems, wait=True):
        n, x, y, c, rows, copy, local_sems = self._plan(src, dst, sems)
        me, sibling = (x, y, c), (x, y, 1 - c)
        over_x, over_y, diagonal = (1 - x, y, c), (x, 1 - y, c), (1 - x, 1 - y, c)
        sent = []

        def arrived(cp):
            if wait:
                cp.wait_recv()

        def send(cp):
            if wait:
                cp.start()
            sent.append(cp)

        for slot, owner, onward, half in ((1, over_x, over_y, 0), (2, over_y, over_x, 1)):
            for e in range(n):
                arrived(copy(e, slot, owner, me))
                if self.relayed[e]:
                    send(copy(e, 3 + half, owner, onward, half=half))
                send(copy(e, 4 + slot, owner, sibling))
        for e in range(n):
            if self.relayed[e]:
                for half in (0, 1):
                    arrived(copy(e, 3 + half, diagonal, me, half=half))
                    send(copy(e, 7 + half, diagonal, sibling, half=half))
            else:
                arrived(copy(e, 3, diagonal, me))
                send(copy(e, 7, diagonal, sibling))
        return sent

    def finish(self, src, dst, sems, passed_on=False):
        n, x, y, c, rows, copy, local_sems = self._plan(src, dst, sems)
        me, sibling = (x, y, c), (x, y, 1 - c)
        over_x, over_y, diagonal = (1 - x, y, c), (x, 1 - y, c), (1 - x, 1 - y, c)
        sent = self.pass_on(src, dst, sems, wait=not passed_on)
        for e in range(n):
            copy(e, 0, sibling, me).wait_recv()
            copy(e, 5, (1 - x, y, 1 - c), me).wait_recv()
            copy(e, 6, (x, 1 - y, 1 - c), me).wait_recv()
            if self.relayed[e]:
                for half in (0, 1):
                    copy(e, 7 + half, (1 - x, 1 - y, 1 - c), me, half=half).wait_recv()
            else:
                copy(e, 7, (1 - x, 1 - y, 1 - c), me).wait_recv()
        for e in range(n):
            copy(e, 0, me, sibling, from_own=True).wait_send()
            copy(e, 1, me, over_x, from_own=True).wait_send()
            copy(e, 2, me, over_y, from_own=True).wait_send()
            if not self.relayed[e]:
                copy(e, 3, me, diagonal, from_own=True).wait_send()
            pltpu.make_async_copy(src[e], rows(e, me), local_sems.at[e]).wait()
        for cp in sent:
            cp.wait_send()


class _ToSibling:
    def __init__(self, grads):
        self.srcs = list(grads)
        n = len(self.srcs)
        self.out_shapes = [jax.ShapeDtypeStruct((4,) + g.shape[1:], g.dtype) for g in self.srcs]
        self.sems = [pltpu.SemaphoreType.DMA((n, 4)), pltpu.SemaphoreType.DMA((n, 4))]

    def _copies(self, src, dst, sems):
        send_sems, recv_sems = sems
        x, y, c = _coords()
        return [
            pltpu.make_async_remote_copy(
                src_ref=src[i].at[2 * q + (1 - c)], dst_ref=dst[i].at[q], send_sem=send_sems.at[i, q],
                recv_sem=recv_sems.at[i, q], device_id=(x, y, 1 - c), device_id_type=MESH)
            for i in range(len(src)) for q in range(4)]

    def start(self, src, dst, sems):
        for cp in self._copies(src, dst, sems):
            cp.start()

    def finish(self, src, dst, sems):
        for cp in self._copies(src, dst, sems):
            cp.wait()


class _ToChips:
    def __init__(self, psums, rows=None):
        self.srcs = list(psums)
        n = len(self.srcs)
        self.rows = rows
        self.out_shapes = [
            jax.ShapeDtypeStruct((3, p.shape[1] if rows is None else rows[1]) + p.shape[2:], p.dtype)
            for p in self.srcs]
        self.sems = [pltpu.SemaphoreType.DMA((n, 3)), pltpu.SemaphoreType.DMA((n, 3))]

    def _copies(self, src, dst, sems):
        send_sems, recv_sems = sems
        x, y, c = _coords()
        peers = [(x, 1 - y), (1 - x, y), (1 - x, 1 - y)]

        def part(i, q):
            if self.rows is None:
                return src[i].at[q]
            return src[i].at[q, pl.ds(self.rows[0], self.rows[1])]

        return [
            pltpu.make_async_remote_copy(
                src_ref=part(i, 2 * px + py), dst_ref=dst[i].at[r], send_sem=send_sems.at[i, r],
                recv_sem=recv_sems.at[i, r], device_id=(px, py, c), device_id_type=MESH)
            for i in range(len(src)) for r, (px, py) in enumerate(peers)]

    def start(self, src, dst, sems):
        for cp in self._copies(src, dst, sems):
            cp.start()

    def finish(self, src, dst, sems):
        for cp in self._copies(src, dst, sems):
            cp.wait()


class _ToOwners:
    def __init__(self, grads):
        self.srcs = list(grads)
        n = len(self.srcs)
        self.out_shapes = [jax.ShapeDtypeStruct(g.shape, g.dtype) for g in self.srcs]
        self.sems = [pltpu.SemaphoreType.DMA((n, 7)), pltpu.SemaphoreType.DMA((n, 7)), pltpu.SemaphoreType.DMA((n,))]

    def _copies(self, src, dst, sems):
        send_sems, recv_sems, local_sems = sems
        x, y, c = _coords()
        me = 4 * x + 2 * y + c
        copies = [pltpu.make_async_copy(src[i].at[me], dst[i].at[me], local_sems.at[i]) for i in range(len(src))]
        for i in range(len(src)):
            for rel in range(1, N_SHARDS):
                px = x ^ (rel >> 2) if rel >> 2 else x
                py = y ^ ((rel >> 1) & 1) if (rel >> 1) & 1 else y
                pc = c ^ (rel & 1) if rel & 1 else c
                copies.append(pltpu.make_async_remote_copy(
                    src_ref=src[i].at[4 * px + 2 * py + pc], dst_ref=dst[i].at[me], send_sem=send_sems.at[i, rel - 1],
                    recv_sem=recv_sems.at[i, rel - 1], device_id=(px, py, pc), device_id_type=MESH))
        return copies

    def start(self, src, dst, sems):
        for cp in self._copies(src, dst, sems):
            cp.start()

    def finish(self, src, dst, sems):
        for cp in self._copies(src, dst, sems):
            cp.wait()


class _Together:
    def __init__(self, parts):
        self.parts = list(parts)
        self.srcs = [s for p in self.parts for s in p.srcs]
        self.out_shapes = [s for p in self.parts for s in p.out_shapes]
        self.sems = [s for p in self.parts for s in p.sems]

    def _split(self, src, dst, sems):
        a = b = c = 0
        for p in self.parts:
            na, nc = len(p.srcs), len(p.sems)
            yield p, src[a:a + na], dst[b:b + na], sems[c:c + nc]
            a, b, c = a + na, b + na, c + nc

    def start(self, src, dst, sems):
        for p, s, d, m in self._split(src, dst, sems):
            p.start(s, d, m)

    def finish(self, src, dst, sems):
        for p, s, d, m in self._split(src, dst, sems):
            p.finish(s, d, m)

    def spread(self):
        b = 0
        for p in self.parts:
            p.results = self.results[b:b + len(p.srcs)]
            b += len(p.srcs)


def _call(body, args, *, grid, in_specs, out_specs, out_shape, name, scratch=(), sem=None, carry=None):
    out_shape, out_specs = list(out_shape), list(out_specs)
    if carry is None:
        return pl.pallas_call(
            body, grid=grid, in_specs=list(in_specs), out_specs=out_specs, out_shape=out_shape,
            scratch_shapes=list(scratch), name=name, compiler_params=_params(sem))(*args)
    n_in, n_out, n_scr, n_c = len(args), len(out_shape), len(scratch), len(carry.srcs)
    steps = tuple(grid)
    total = 1
    for n_ax in steps:
        total *= n_ax
    early = getattr(carry, "early", False) and total >= 8
    early_step = total - max(2, total // 8)

    def carried(*refs):
        ins, rest = refs[:n_in], refs[n_in:]
        c_src, rest = rest[:n_c], rest[n_c:]
        outs, rest = rest[:n_out], rest[n_out:]
        c_dst, rest = rest[:n_c], rest[n_c:]
        scr, sems = rest[:n_scr], rest[n_scr:]
        step = pl.program_id(0)
        for ax in range(1, len(steps)):
            step = step * steps[ax] + pl.program_id(ax)

        @pl.when(step == 0)
        def _():
            carry.start(c_src, c_dst, sems)

        body(*ins, *outs, *scr)

        if early:
            @pl.when(step == early_step)
            def _():
                carry.pass_on(c_src, c_dst, sems)

        @pl.when(step == total - 1)
        def _():
            if early:
                carry.finish(c_src, c_dst, sems, passed_on=True)
            else:
                carry.finish(c_src, c_dst, sems)

    hbm = pl.BlockSpec(memory_space=pl.ANY)
    res = pl.pallas_call(
        carried, grid=grid, in_specs=list(in_specs) + [hbm] * n_c, out_specs=out_specs + [hbm] * n_c,
        out_shape=out_shape + carry.out_shapes, scratch_shapes=list(scratch) + carry.sems, name=name,
        compiler_params=_params(("arbitrary",) * len(steps)))(*args, *carry.srcs)
    carry.results = list(res[n_out:])
    return list(res[:n_out])


def _exchange_alone(ex, name):
    n = len(ex.srcs)

    def body(*refs):
        src, dst, sems = refs[:n], refs[n:2 * n], refs[2 * n:]
        ex.start(src, dst, sems)
        ex.finish(src, dst, sems)

    hbm = pl.BlockSpec(memory_space=pl.ANY)
    res = pl.pallas_call(body, in_specs=[hbm] * n, out_specs=[hbm] * n, out_shape=ex.out_shapes,
                         scratch_shapes=ex.sems, name=name)(*ex.srcs)
    ex.results = list(res)
    return ex.results


def _rms_bwd(x, gains, dhs, dres, name, tm=256, carry=None, through=None):
    t, d = x.shape
    n = len(gains)
    n_w = 0 if through is None else n

    def body(*refs):
        x_ref, dres_ref = refs[0], refs[1]
        g_refs, dh_refs, w_refs = refs[2:2 + n], refs[2 + n:2 + 2 * n], refs[2 + 2 * n:2 + 2 * n + n_w]
        dx_ref, dg_ref = refs[2 + 2 * n + n_w], refs[3 + 2 * n + n_w]
        i = pl.program_id(0)

        @pl.when(i == 0)
        def _():
            dg_ref[...] = jnp.zeros_like(dg_ref)

        xf = x_ref[...]
        r = lax.rsqrt(jnp.mean(xf * xf, axis=-1, keepdims=True) + EPS)
        xhat = xf * r
        dx = dres_ref[...]
        for j in range(n):
            dh = dh_refs[j][...]
            if n_w:
                dh = _dot(dh.astype(BF16), w_refs[j][...], NT)
            dg_ref[j:j + 1, :] += jnp.sum(dh * xhat, axis=0, keepdims=True)
            gy = dh * g_refs[j][...]
            dx = dx + r * (gy - xhat * jnp.mean(gy * xhat, axis=-1, keepdims=True))
        dx_ref[...] = dx

    row = pl.BlockSpec((tm, d), lambda i: (i, 0))
    vec = pl.BlockSpec((1, d), lambda i: (0, 0))
    dh_rows = [pl.BlockSpec((tm, dh.shape[1]), lambda i: (i, 0)) for dh in dhs]
    w_full = [] if through is None else [pl.BlockSpec(w.shape, lambda i: (0, 0)) for w in through]
    return _call(body, [x, dres, *gains, *dhs, *(through or [])], grid=(t // tm,),
                 in_specs=[row, row] + [vec] * n + dh_rows + w_full,
                 out_specs=[row, pl.BlockSpec((8, d), lambda i: (0, 0))],
                 out_shape=[jax.ShapeDtypeStruct((t, d), F32), jax.ShapeDtypeStruct((8, d), F32)],
                 name=name, sem=("arbitrary",), carry=carry)


def _mm(a, b, a_spec, b_spec, o_spec, out_shape, grid, dims, name, res=None, res_spec=None, carry=None):
    nk = grid[2]
    acc_shape = tuple(s for s in o_spec.block_shape if s is not None)

    def body(*refs):
        a_ref, b_ref = refs[0], refs[1]
        r_ref = refs[2] if res is not None else None
        o_ref = refs[3] if res is not None else refs[2]
        p = _dot(a_ref[...].astype(BF16), b_ref[...].astype(BF16), dims)
        if nk == 1:
            if res is not None:
                p = p + r_ref[...]
            o_ref[...] = p.astype(o_ref.dtype)
            return
        acc_ref = refs[-1]
        k = pl.program_id(2)

        @pl.when(k == 0)
        def _():
            acc_ref[...] = p

        @pl.when(k > 0)
        def _():
            acc_ref[...] += p

        @pl.when(k == nk - 1)
        def _():
            out = acc_ref[...]
            if res is not None:
                out = out + r_ref[...]
            o_ref[...] = out.astype(o_ref.dtype)

    ins = [a, b] + ([res] if res is not None else [])
    specs = [a_spec, b_spec] + ([res_spec] if res is not None else [])
    return _call(body, ins, grid=grid, in_specs=specs, out_specs=[o_spec], out_shape=[out_shape],
                 scratch=[pltpu.VMEM(acc_shape, F32)] if nk > 1 else [], name=name,
                 sem=("parallel", "parallel", "arbitrary"), carry=carry)[0]


def _mm_rows(a, w, out_dtype, name, trans_w=False, res=None, tm=1024, carry=None):
    t, k = a.shape
    tm = min(tm, t)
    n = w.shape[0] if trans_w else w.shape[1]
    return _mm(
        a, w, pl.BlockSpec((tm, k), lambda i, j, kk: (i, 0)), pl.BlockSpec(w.shape, lambda i, j, kk: (0, 0)),
        pl.BlockSpec((tm, n), lambda i, j, kk: (i, 0)), jax.ShapeDtypeStruct((t, n), out_dtype), (t // tm, 1, 1),
        NT if trans_w else NN, name, res=res,
        res_spec=None if res is None else pl.BlockSpec((tm, n), lambda i, j, kk: (i, 0)), carry=carry)


def _mm_wgrad(a, b, name, carry=None):
    t, m = a.shape
    n = b.shape[1]
    tn = n // (4 if b.dtype == F32 else 2)
    return _mm(
        a, b, pl.BlockSpec((t, m), lambda i, j, kk: (0, 0)), pl.BlockSpec((t, tn), lambda i, j, kk: (0, j)),
        pl.BlockSpec((m, tn), lambda i, j, kk: (0, j)), jax.ShapeDtypeStruct((m, n), F32), (1, n // tn, 1), TN, name,
        carry=carry)


def _sgu_fwd(x0, g, w_in, g_v, w_c, b_sb, w_out, tm=256, carry=None):
    t, d = x0.shape
    nsub = w_in.shape[2]

    def body(x_ref, g_ref, win_ref, gv_ref, wc_ref, bsb_ref, wout_ref, zpre_ref, x1_ref, h_ref, u_s, v_s, vn_s, y_s):
        xf = x_ref[...]
        h = (xf * lax.rsqrt(jnp.mean(xf * xf, axis=-1, keepdims=True) + EPS) * g_ref[...]).astype(BF16)
        h_ref[...] = h
        for k in range(N_SHARDS):
            zk = _dot(h, win_ref[k])
            zpre_ref[:, k * nsub:(k + 1) * nsub] = zk
            cdf, _ = _gelu_parts(zk)
            if k < N_SHARDS // 2:
                u_s[:, k * nsub:(k + 1) * nsub] = zk * cdf
            else:
                v_s[:, (k - 4) * nsub:(k - 3) * nsub] = zk * cdf
        v = v_s[...]
        rv = lax.rsqrt(jnp.mean(v * v, axis=-1, keepdims=True) + EPS)
        vn_s[...] = (v * rv * gv_ref[...]).astype(BF16)
        for ci in range(tm // CHUNK):
            rows = slice(ci * CHUNK, (ci + 1) * CHUNK)
            for g in range(N_GROUPS):
                cols = slice(g * LANES, (g + 1) * LANES)
                sv = _dot(wc_ref[g], vn_s[rows, cols]) + bsb_ref[g]
                y_s[rows, cols] = (u_s[rows, cols] * sv).astype(BF16)
        x1_ref[...] = x_ref[...] + _dot(y_s[...], wout_ref[...])

    row = pl.BlockSpec((tm, d), lambda i: (i, 0))
    full = lambda a: pl.BlockSpec(a.shape, lambda i: (0,) * a.ndim)
    return _call(
        body, [x0, g, w_in, g_v, w_c, b_sb, w_out], grid=(t // tm,),
        in_specs=[row, full(g), full(w_in), full(g_v), full(w_c), full(b_sb), full(w_out)],
        out_specs=[pl.BlockSpec((tm, 2 * d), lambda i: (i, 0)), row, row],
        out_shape=[jax.ShapeDtypeStruct((t, 2 * d), F32), jax.ShapeDtypeStruct((t, d), F32),
                   jax.ShapeDtypeStruct((t, d), BF16)],
        scratch=[pltpu.VMEM((tm, d), F32), pltpu.VMEM((tm, d), F32), pltpu.VMEM((tm, d), BF16),
                 pltpu.VMEM((tm, d), BF16)],
        name="sgu_fwd", carry=carry)


def _sgu_bwd(dx1, zpre, w_out, g_v, w_c, w_ct, b_sb, tm=256, carry=None):
    t, d = dx1.shape

    def body(dx_ref, zpre_ref, wout_ref, gv_ref, wc_ref, wct_ref, bsb_ref,
             dz_ref, y_ref, dwc_ref, dbs_ref, dgv_ref, u_s, vn_s, dy_s, du_s, dvn_s):
        i = pl.program_id(0)

        @pl.when(i == 0)
        def _():
            dwc_ref[...] = jnp.zeros_like(dwc_ref)
            dbs_ref[...] = jnp.zeros_like(dbs_ref)
            dgv_ref[...] = jnp.zeros_like(dgv_ref)

        dy_s[...] = _dot(dx_ref[...].astype(BF16), wout_ref[...], NT)
        zu = zpre_ref[:, :d]
        zv = zpre_ref[:, d:]
        cdf_u, pdf_u = _gelu_parts(zu)
        cdf_v, pdf_v = _gelu_parts(zv)
        u_s[...] = zu * cdf_u
        v = zv * cdf_v
        rv = lax.rsqrt(jnp.mean(v * v, axis=-1, keepdims=True) + EPS)
        vhat = v * rv
        gv = gv_ref[...]
        vn_s[...] = (vhat * gv).astype(BF16)
        for ci in range(tm // CHUNK):
            rows = slice(ci * CHUNK, (ci + 1) * CHUNK)
            for g in range(N_GROUPS):
                cols = slice(g * LANES, (g + 1) * LANES)
                vnb = vn_s[rows, cols]
                sv = _dot(wc_ref[g], vnb) + bsb_ref[g]
                dyb = dy_s[rows, cols]
                ub = u_s[rows, cols]
                dsv = dyb * ub
                du_s[rows, cols] = dyb * sv
                y_ref[rows, cols] = (ub * sv).astype(BF16)
                dsvb = dsv.astype(BF16)
                dbs_ref[g] += dsv
                dwc_ref[g] += _dot(dsvb, vnb, NT)
                dvn_s[rows, cols] = _dot(wct_ref[g], dsvb)
        dvn = dvn_s[...]
        dgv_ref[0:1, :] += jnp.sum(dvn * vhat, axis=0, keepdims=True)
        gy = dvn * gv
        dv = rv * (gy - vhat * jnp.mean(gy * vhat, axis=-1, keepdims=True))
        dz_ref[:, :d] = (du_s[...] * (cdf_u + zu * pdf_u)).astype(BF16)
        dz_ref[:, d:] = (dv * (cdf_v + zv * pdf_v)).astype(BF16)

        @pl.when(i == t // tm - 1)
        def _():
            tri = (lax.broadcasted_iota(jnp.int32, (CHUNK, CHUNK), 0)
                   >= lax.broadcasted_iota(jnp.int32, (CHUNK, CHUNK), 1))
            for g in range(N_GROUPS):
                dwc_ref[g] = jnp.where(tri, dwc_ref[g], 0.0)
                dbs_ref[g] = jnp.broadcast_to(jnp.sum(dbs_ref[g], axis=1, keepdims=True), (CHUNK, CHUNK))

    row = pl.BlockSpec((tm, d), lambda i: (i, 0))
    row2 = pl.BlockSpec((tm, 2 * d), lambda i: (i, 0))
    full = lambda a: pl.BlockSpec(a.shape, lambda i: (0,) * a.ndim)
    grp = pl.BlockSpec((N_GROUPS, CHUNK, CHUNK), lambda i: (0, 0, 0))
    return _call(
        body, [dx1, zpre, w_out, g_v, w_c, w_ct, b_sb], grid=(t // tm,),
        in_specs=[row, row2, full(w_out), full(g_v), full(w_c), full(w_ct), full(b_sb)],
        out_specs=[row2, row, grp, grp, pl.BlockSpec((8, d), lambda i: (0, 0))],
        out_shape=[jax.ShapeDtypeStruct((t, 2 * d), BF16), jax.ShapeDtypeStruct((t, d), BF16),
                   jax.ShapeDtypeStruct((N_GROUPS, CHUNK, CHUNK), F32),
                   jax.ShapeDtypeStruct((N_GROUPS, CHUNK, CHUNK), F32), jax.ShapeDtypeStruct((8, d), F32)],
        scratch=[pltpu.VMEM((tm, d), F32), pltpu.VMEM((tm, d), BF16), pltpu.VMEM((tm, d), F32),
                 pltpu.VMEM((tm, d), F32), pltpu.VMEM((tm, d), F32)],
        name="sgu_bwd", sem=("arbitrary",), carry=carry)


ROW_CHUNK = 256
HALO = 16


def _ffn_fwd(x, g, w_in, cw, cb, w_out, layer, tm=512, carry=None, next_gains=(), loss_target=None):
    t, d = x.shape
    nc = N_SHARDS // 2
    n_gains = len(next_gains)
    with_loss = loss_target is not None

    def body(x_ref, xp_ref, g_ref, wg_ref, wu_ref, cwg_ref, cbg_ref, cwu_ref, cbu_ref, wout_ref, *rest):
        extra_in, rest = rest[:n_gains + with_loss], rest[n_gains + with_loss:]
        o_ref, hf_ref, a_ref, pre_ref = rest[:4]
        extra_out, hw_s = rest[4:-1], rest[-1]
        i, c = pl.program_id(0), pl.program_id(1)

        @pl.when(c == 0)
        def _():
            keep = jnp.where(i == 0, 0.0, 1.0)
            xw = jnp.concatenate([xp_ref[...] * keep, x_ref[...]], axis=0)
            xhat = xw * lax.rsqrt(jnp.mean(xw * xw, axis=-1, keepdims=True) + EPS)
            hw_s[...] = (xhat * g_ref[...]).astype(BF16)
            hf_ref[...] = hw_s[HALO:, :]
            o_ref[...] = x_ref[...]

        hw = hw_s[...]
        pre = []
        for j, (w_ref, cw_ref, cb_ref) in enumerate(((wg_ref, cwg_ref, cbg_ref), (wu_ref, cwu_ref, cbu_ref))):
            ab = _dot(hw, w_ref[...]).astype(BF16)
            a_ref[j] = ab[HALO:]
            win = ab.astype(F32)
            cw_v = cw_ref[...]
            pre.append(cw_v[2:3, :] * win[HALO:] + cw_v[1:2, :] * pltpu.roll(win, 1, 0)[HALO:]
                       + cw_v[0:1, :] * pltpu.roll(win, 2, 0)[HALO:] + cb_ref[...])
            pre_ref[j] = pre[j]
        act = (pre[0] * _sigmoid(pre[0]) * pre[1]).astype(BF16)
        o_ref[...] += _dot(act, wout_ref[...])

        if with_loss:
            @pl.when((i == 0) & (c == 0))
            def _():
                extra_out[-1][...] = jnp.zeros_like(extra_out[-1])

        @pl.when(c == nc - 1)
        def _():
            xn = o_ref[...]
            if n_gains:
                xhat = xn * lax.rsqrt(jnp.mean(xn * xn, axis=-1, keepdims=True) + EPS)
                for k in range(n_gains):
                    extra_out[k][...] = (xhat * extra_in[k][...]).astype(BF16)
            if with_loss:
                err = xn - extra_in[-1][...]
                extra_out[-2][...] = err * (1.0 / d)
                part = jnp.sum(jnp.sum(err * err, axis=0, keepdims=True), axis=1, keepdims=True)
                extra_out[-1][...] += jnp.broadcast_to(0.5 / d * part, extra_out[-1].shape)

    row = pl.BlockSpec((tm, d), lambda i, c: (i, 0))
    vec = pl.BlockSpec((1, d), lambda i, c: (0, 0))
    shard = lambda rows, up: pl.BlockSpec((None, rows, FF_SHARD), lambda i, c: (c + up * nc, 0, 0))
    pair = pl.BlockSpec((2, None, tm, FF_SHARD), lambda i, c: (0, c, i, 0))
    lanes = pl.BlockSpec((8, LANES), lambda i, c: (0, 0))
    outs = _call(
        body, [x, x, g, w_in, w_in, cw, cb, cw, cb, w_out, *next_gains] + ([loss_target] if with_loss else []),
        grid=(t // tm, nc),
        in_specs=[row, pl.BlockSpec((HALO, d), lambda i, c: (jnp.maximum(i * (tm // HALO) - 1, 0), 0)),
                  vec, shard(d, 0), shard(d, 1), shard(8, 0), shard(1, 0), shard(8, 1), shard(1, 1),
                  pl.BlockSpec((FF_SHARD, d), lambda i, c: (c, 0))] + [vec] * n_gains + [row] * with_loss,
        out_specs=[row, row, pair, pair] + [row] * n_gains + [row, lanes] * with_loss,
        out_shape=[jax.ShapeDtypeStruct((t, d), F32), jax.ShapeDtypeStruct((t, d), BF16),
                   jax.ShapeDtypeStruct((2, nc, t, FF_SHARD), BF16), jax.ShapeDtypeStruct((2, nc, t, FF_SHARD), F32)]
        + [jax.ShapeDtypeStruct((t, d), BF16)] * n_gains
        + [jax.ShapeDtypeStruct((t, d), F32), jax.ShapeDtypeStruct((8, LANES), F32)] * with_loss,
        scratch=[pltpu.VMEM((tm + HALO, d), BF16)], name=f"ffn{layer}_fwd", sem=("arbitrary", "arbitrary"), carry=carry)
    return (outs[0], outs[1], outs[2].reshape(N_SHARDS, t, FF_SHARD), outs[3]) + tuple(outs[4:])


def _ffn_bwd_act(pre, w_out, dxn, layer, tm=512, carry=None):
    t, d = dxn.shape
    nc = N_SHARDS // 2

    def body(pre_ref, wout_ref, dx_ref, dhu_ref, dw_ref, dcb_ref):
        i = pl.program_id(1)

        @pl.when(i == 0)
        def _():
            dw_ref[...] = jnp.zeros_like(dw_ref)
            dcb_ref[...] = jnp.zeros_like(dcb_ref)

        hg, hu = pre_ref[0], pre_ref[1]
        sg = _sigmoid(hg)
        sl = hg * sg
        dxb = dx_ref[...].astype(BF16)
        dact = _dot(dxb, wout_ref[...], NT)
        dw_ref[...] += _dot((sl * hu).astype(BF16), dxb, TN)
        d_up = dact * sl
        d_gate = dact * hu * (sg * (1.0 + hg * (1.0 - sg)))
        for j, dv in enumerate((d_gate, d_up)):
            dhu_ref[j] = dv.astype(BF16)
            dcb_ref[j, 0:1, :] += jnp.sum(dv, axis=0, keepdims=True)

    return _call(
        body, [pre, w_out, dxn], grid=(nc, t // tm),
        in_specs=[pl.BlockSpec((2, None, tm, FF_SHARD), lambda c, i: (0, c, i, 0)),
                  pl.BlockSpec((FF_SHARD, d), lambda c, i: (c, 0)), pl.BlockSpec((tm, d), lambda c, i: (i, 0))],
        out_specs=[pl.BlockSpec((None, 2, tm, FF_SHARD), lambda c, i: (c, 0, i, 0)),
                   pl.BlockSpec((FF_SHARD, d), lambda c, i: (c, 0)),
                   pl.BlockSpec((None, 2, 8, FF_SHARD), lambda c, i: (c, 0, 0, 0))],
        out_shape=[jax.ShapeDtypeStruct((nc, 2, t, FF_SHARD), BF16), jax.ShapeDtypeStruct((D_FF, d), F32),
                   jax.ShapeDtypeStruct((nc, 2, 8, FF_SHARD), F32)],
        name=f"ffn{layer}_bwd_act", sem=("parallel", "arbitrary"), carry=carry)


def _ffn_bwd_in(dhu, a, cw, w_in, layer, tm=1024, carry=None, norm=None):
    nc, _, t, _ = dhu.shape
    d = D_MODEL
    tm = min(tm, t)
    last_blk = t // 16 - 1
    n_norm = 0 if norm is None else 3

    def body(dh_ref, nx_ref, a_ref, cw_ref, win_ref, *rest):
        norm_refs, (da_ref, o_ref, dcw_ref), dg_refs = rest[:n_norm], rest[n_norm:n_norm + 3], rest[n_norm + 3:]
        i, s = pl.program_id(0), pl.program_id(1)

        @pl.when(s == 0)
        def _():
            o_ref[...] = jnp.zeros_like(o_ref)

        @pl.when((s == 0) & (i == 0))
        def _():
            dcw_ref[...] = jnp.zeros_like(dcw_ref)

        keep = jnp.where(i == t // tm - 1, 0.0, 1.0)
        cw = cw_ref[...]
        sums = [None] * 3
        for r0 in range(0, tm, ROW_CHUNK):
            rows = slice(r0, r0 + ROW_CHUNK)
            if r0 + ROW_CHUNK == tm:
                win = jnp.concatenate([dh_ref[rows, :].astype(F32), nx_ref[...].astype(F32) * keep], axis=0)
            else:
                win = dh_ref[r0:r0 + ROW_CHUNK + HALO, :].astype(F32)
            n = ROW_CHUNK + HALO
            taps = (pltpu.roll(win, n - 2, 0)[:ROW_CHUNK],
                    pltpu.roll(win, n - 1, 0)[:ROW_CHUNK],
                    win[:ROW_CHUNK])
            da = (cw[0:1, :] * taps[0] + cw[1:2, :] * taps[1] + cw[2:3, :] * taps[2]).astype(BF16)
            da_ref[rows, :] = da
            o_ref[rows, :] += _dot(da, win_ref[...], NT)
            af = a_ref[rows, :].astype(F32)
            parts = [jnp.sum(taps[k] * af, axis=0, keepdims=True) for k in range(3)]
            sums = [p if q is None else q + p for q, p in zip(sums, parts)]
        for k in range(3):
            dcw_ref[pl.ds(s, 1), k:k + 1, :] += sums[k][None]

        if norm is not None:
            x_ref, g_ref, dres_ref = norm_refs
            dg_ref = dg_refs[0]

            @pl.when((s == 0) & (i == 0))
            def _():
                dg_ref[...] = jnp.zeros_like(dg_ref)

            @pl.when(s == N_SHARDS - 1)
            def _():
                xf = x_ref[...]
                r = lax.rsqrt(jnp.mean(xf * xf, axis=-1, keepdims=True) + EPS)
                xhat = xf * r
                dh = o_ref[...]
                dg_ref[0:1, :] += jnp.sum(dh * xhat, axis=0, keepdims=True)
                gy = dh * g_ref[...]
                o_ref[...] = dres_ref[...] + r * (gy - xhat * jnp.mean(gy * xhat, axis=-1, keepdims=True))

    row = pl.BlockSpec((tm, d), lambda i, s: (i, 0))
    norm_args = [] if norm is None else list(norm)
    norm_specs = [] if norm is None else [row, pl.BlockSpec((1, d), lambda i, s: (0, 0)), row]
    return _call(
        body, [dhu, dhu, a, cw, w_in] + norm_args, grid=(t // tm, N_SHARDS),
        in_specs=[pl.BlockSpec((None, None, tm, FF_SHARD), lambda i, s: (s % nc, s // nc, i, 0)),
                  pl.BlockSpec((None, None, 16, FF_SHARD),
                               lambda i, s: (s % nc, s // nc, jnp.minimum((i + 1) * (tm // 16), last_blk), 0)),
                  pl.BlockSpec((None, tm, FF_SHARD), lambda i, s: (s, i, 0)),
                  pl.BlockSpec((None, 8, FF_SHARD), lambda i, s: (s, 0, 0)),
                  pl.BlockSpec((None, d, FF_SHARD), lambda i, s: (s, 0, 0))] + norm_specs,
        out_specs=[pl.BlockSpec((None, tm, FF_SHARD), lambda i, s: (s, i, 0)), row,
                   pl.BlockSpec((N_SHARDS, 8, FF_SHARD), lambda i, s: (0, 0, 0))]
        + ([] if norm is None else [pl.BlockSpec((8, d), lambda i, s: (0, 0))]),
        out_shape=[jax.ShapeDtypeStruct((N_SHARDS, t, FF_SHARD), BF16), jax.ShapeDtypeStruct((t, d), F32),
                   jax.ShapeDtypeStruct((N_SHARDS, 8, FF_SHARD), F32)]
        + ([] if norm is None else [jax.ShapeDtypeStruct((8, d), F32)]),
        name=f"ffn{layer}_bwd_in", sem=("arbitrary", "arbitrary"), carry=carry)


def _ffn_wgrad_in(hf, da, layer, carry=None):
    t, d = hf.shape
    return _mm(
        da, hf, pl.BlockSpec((None, t, FF_SHARD), lambda s, j, kk: (s, 0, 0)),
        pl.BlockSpec((t, d), lambda s, j, kk: (0, 0)),
        pl.BlockSpec((None, FF_SHARD, d), lambda s, j, kk: (s, 0, 0)),
        jax.ShapeDtypeStruct((N_SHARDS, FF_SHARD, d), F32), (N_SHARDS, 1, 1), TN, f"ffn{layer}_wgrad_in",
        carry=carry)


Q_PER_KV = N_Q_HEADS // N_KV_HEADS
GROUP_ROWS = Q_PER_KV * CHUNK


def _lane_half():
    return lax.broadcasted_iota(jnp.int32, (CHUNK, LANES), 1) < HEAD_DIM


def _fill_attn_bias(bias_s):
    tq = lax.broadcasted_iota(jnp.int32, (GROUP_ROWS, 2 * CHUNK), 0) & (CHUNK - 1)
    jk = lax.broadcasted_iota(jnp.int32, (GROUP_ROWS, 2 * CHUNK), 1)
    dist = tq + CHUNK - jk
    window = (dist >= 0) & (dist < CHUNK)
    distf = dist.astype(F32)
    for kvh in range(N_KV_HEADS):
        alibi = _per_head_column([-SLOPES[h] for h in range(Q_PER_KV * kvh, Q_PER_KV * (kvh + 1))]) * distf
        bias_s[0, kvh] = jnp.where(window & (jk >= CHUNK), alibi, NEG_BIG)
        bias_s[1, kvh] = jnp.where(window, alibi, NEG_BIG)


def _per_head_column(values):
    r = lax.broadcasted_iota(jnp.int32, (GROUP_ROWS, 1), 0)
    col = jnp.full((GROUP_ROWS, 1), values[Q_PER_KV - 1], F32)
    for j in range(Q_PER_KV - 2, -1, -1):
        col = jnp.where(r < (j + 1) * CHUNK, values[j], col)
    return col


def _half_sum(x, lo):
    s_lo = jnp.sum(jnp.where(lo, x, 0.0), axis=-1, keepdims=True)
    s_hi = jnp.sum(jnp.where(lo, 0.0, x), axis=-1, keepdims=True)
    return jnp.where(lo, s_lo, s_hi)


def _stack_heads(pairs, lo):
    zero = jnp.zeros_like(pairs[0])
    return jnp.concatenate([jnp.where(lo, pairs[0], zero), jnp.where(lo, zero, pairs[0]),
                            jnp.where(lo, pairs[1], zero), jnp.where(lo, zero, pairs[1])], axis=0)


def _unstack_heads(stacked, lo):
    return (jnp.where(lo, stacked[0:CHUNK], stacked[CHUNK:2 * CHUNK]),
            jnp.where(lo, stacked[2 * CHUNK:3 * CHUNK], stacked[3 * CHUNK:]))


def _attn_probs(qs, kn, bias, sink_col):
    s = _dot(qs, kn, NT) * (HEAD_DIM ** -0.5) + bias
    m = jnp.maximum(jnp.max(s, axis=-1, keepdims=True), sink_col)
    e = jnp.exp(s - m)
    den = jnp.sum(e, axis=-1, keepdims=True) + jnp.exp(sink_col - m)
    return e * (1.0 / den), m, den


def _attn_fwd(qraw, kvd, gq, gk, sinks, carry=None):
    t, d = qraw.shape
    nb = t // CHUNK

    def body(sink_ref, q_ref, cur_ref, prev_ref, gq_ref, gk_ref, o_ref, bias_s):
        n = pl.program_id(0)

        @pl.when(n == 0)
        def _():
            _fill_attn_bias(bias_s)

        lo = _lane_half()
        which = jnp.where(n == 0, 0, 1)
        gq_v, gk_v = gq_ref[...], gk_ref[...]
        for kvh in range(N_KV_HEADS):
            ks = slice(kvh * LANES, (kvh + 1) * LANES)
            vs = slice(4 * LANES + kvh * LANES, 4 * LANES + (kvh + 1) * LANES)
            kraw = jnp.concatenate([prev_ref[:, ks], cur_ref[:, ks]], axis=0)
            rk = lax.rsqrt(jnp.mean(kraw * kraw, axis=-1, keepdims=True) + EPS)
            kn = (kraw * rk * gk_v).astype(BF16)
            vv = jnp.concatenate([prev_ref[:, vs], cur_ref[:, vs]], axis=0).astype(BF16)
            qn = []
            for p in range(2):
                qp = q_ref[:, (2 * kvh + p) * LANES:(2 * kvh + p + 1) * LANES]
                r = lax.rsqrt(_half_sum(qp * qp, lo) * (1.0 / HEAD_DIM) + EPS)
                qn.append(qp * r * gq_v)
            heads = range(Q_PER_KV * kvh, Q_PER_KV * (kvh + 1))
            pf, _, _ = _attn_probs(_stack_heads(qn, lo).astype(BF16), kn, bias_s[which, kvh],
                                   _per_head_column([sink_ref[h] for h in heads]))
            for p, o_pair in enumerate(_unstack_heads(_dot(pf.astype(BF16), vv), lo)):
                o_ref[:, (2 * kvh + p) * LANES:(2 * kvh + p + 1) * LANES] = o_pair.astype(BF16)

    blk = lambda f: pl.BlockSpec((CHUNK, d), f)
    vec = pl.BlockSpec((1, LANES), lambda n: (0, 0))
    return _call(
        body, [sinks, qraw, kvd, kvd, gq, gk], grid=(nb,),
        in_specs=[pl.BlockSpec(memory_space=pltpu.SMEM), blk(lambda n: (n, 0)), blk(lambda n: (n, 0)),
                  blk(lambda n: (jnp.maximum(n - 1, 0), 0)), vec, vec],
        out_specs=[blk(lambda n: (n, 0))], out_shape=[jax.ShapeDtypeStruct((t, d), BF16)],
        scratch=[pltpu.VMEM((2, N_KV_HEADS, GROUP_ROWS, 2 * CHUNK), F32)], name="attn_fwd", sem=("arbitrary",),
        carry=carry)[0]


def _attn_bwd(qraw, kvd, d_o, gq, gk, sinks, carry=None):
    t, d = qraw.shape
    nb = t // CHUNK

    def body(sink_ref, q_ref, cur_ref, prev_ref, do_ref, gq_ref, gk_ref,
             dq_ref, dkv_ref, dsink_ref, dgq_ref, dgk_ref, carry_s, pp_s, cp_s, bias_s):
        n = pl.program_id(0)

        @pl.when(n == 0)
        def _():
            carry_s[...] = jnp.zeros_like(carry_s)
            dsink_ref[...] = jnp.zeros_like(dsink_ref)
            dgq_ref[...] = jnp.zeros_like(dgq_ref)
            dgk_ref[...] = jnp.zeros_like(dgk_ref)
            _fill_attn_bias(bias_s)

        @pl.when(n < nb)
        def _():
            lo = _lane_half()
            which = jnp.where(n == 0, 0, 1)
            gq_v, gk_v = gq_ref[...], gk_ref[...]
            for kvh in range(N_KV_HEADS):
                ks = slice(kvh * LANES, (kvh + 1) * LANES)
                vs = slice(4 * LANES + kvh * LANES, 4 * LANES + (kvh + 1) * LANES)
                kraw = jnp.concatenate([prev_ref[:, ks], cur_ref[:, ks]], axis=0)
                rk = lax.rsqrt(jnp.mean(kraw * kraw, axis=-1, keepdims=True) + EPS)
                khat = kraw * rk
                kn = (khat * gk_v).astype(BF16)
                vv = jnp.concatenate([prev_ref[:, vs], cur_ref[:, vs]], axis=0).astype(BF16)
                cols = [slice((2 * kvh + p) * LANES, (2 * kvh + p + 1) * LANES) for p in range(2)]
                rq, qhat = [], []
                for p in range(2):
                    qp = q_ref[:, cols[p]]
                    rq.append(lax.rsqrt(_half_sum(qp * qp, lo) * (1.0 / HEAD_DIM) + EPS))
                    qhat.append(qp * rq[p])
                heads = range(Q_PER_KV * kvh, Q_PER_KV * (kvh + 1))
                qs = _stack_heads([qhat[p] * gq_v for p in range(2)], lo).astype(BF16)
                dos = _stack_heads([do_ref[:, cols[p]] for p in range(2)], lo)
                sink_col = _per_head_column([sink_ref[h] for h in heads])
                pf, m, den = _attn_probs(qs, kn, bias_s[which, kvh], sink_col)
                dp = _dot(dos, vv, NT)
                delta = jnp.sum(pf * dp, axis=-1, keepdims=True)
                sink_delta = jnp.exp(sink_col - m) / den * delta
                for j, h in enumerate(heads):
                    dsink_ref[h:h + 1, :] -= jnp.broadcast_to(
                        jnp.sum(sink_delta[j * CHUNK:(j + 1) * CHUNK], axis=0, keepdims=True), (1, LANES))
                ds = (pf * (dp - delta) * (HEAD_DIM ** -0.5)).astype(BF16)
                dkn = _dot(ds, qs, TN)
                dvb = _dot(pf.astype(BF16), dos, TN)
                for p, dqn in enumerate(_unstack_heads(_dot(ds, kn), lo)):
                    dgq_ref[0:1, :] += jnp.sum(dqn * qhat[p], axis=0, keepdims=True)
                    gy = dqn * gq_v
                    mq = _half_sum(gy * qhat[p], lo) * (1.0 / HEAD_DIM)
                    dq_ref[:, cols[p]] = (rq[p] * (gy - qhat[p] * mq)).astype(BF16)
                dgk_ref[0:1, :] += jnp.sum(dkn * khat, axis=0, keepdims=True)
                gyk = dkn * gk_v
                dkraw = rk * (gyk - khat * jnp.mean(gyk * khat, axis=-1, keepdims=True))
                pp_s[:, ks] = dkraw[:CHUNK]
                cp_s[:, ks] = dkraw[CHUNK:]
                pp_s[:, vs] = dvb[:CHUNK]
                cp_s[:, vs] = dvb[CHUNK:]
            dkv_ref[...] = (carry_s[...] + pp_s[...]).astype(BF16)
            carry_s[...] = cp_s[...]

        @pl.when(n == nb)
        def _():
            dkv_ref[...] = carry_s[...].astype(BF16)

    blk = lambda f: pl.BlockSpec((CHUNK, d), f)
    vec = pl.BlockSpec((1, LANES), lambda n: (0, 0))
    cur = lambda n: (jnp.minimum(n, nb - 1), 0)
    prev = lambda n: (jnp.maximum(jnp.minimum(n, nb - 1) - 1, 0), 0)
    small = lambda r: pl.BlockSpec((r, LANES), lambda n: (0, 0))
    return _call(
        body, [sinks, qraw, kvd, kvd, d_o, gq, gk], grid=(nb + 1,),
        in_specs=[pl.BlockSpec(memory_space=pltpu.SMEM), blk(cur), blk(cur), blk(prev), blk(cur), vec, vec],
        out_specs=[blk(cur), blk(lambda n: (jnp.maximum(n - 1, 0), 0)), small(N_Q_HEADS), small(8), small(8)],
        out_shape=[jax.ShapeDtypeStruct((t, d), BF16), jax.ShapeDtypeStruct((t, d), BF16),
                   jax.ShapeDtypeStruct((N_Q_HEADS, LANES), F32), jax.ShapeDtypeStruct((8, LANES), F32),
                   jax.ShapeDtypeStruct((8, LANES), F32)],
        scratch=[pltpu.VMEM((CHUNK, d), F32)] * 3 + [pltpu.VMEM((2, N_KV_HEADS, GROUP_ROWS, 2 * CHUNK), F32)],
        name="attn_bwd", sem=("arbitrary",), carry=carry)


def _adamw_math(g, w, m, v):
    m = ADAM_B1 * m + (1.0 - ADAM_B1) * g
    v = ADAM_B2 * v + (1.0 - ADAM_B2) * (g * g)
    m_hat = m / (1.0 - ADAM_B1 ** ADAM_STEP)
    v_hat = v / (1.0 - ADAM_B2 ** ADAM_STEP)
    delta = -ADAM_LR * (m_hat / (jnp.sqrt(v_hat) + ADAM_EPS) + ADAM_WD * w)
    return delta, m, v


def _row_tile(r, cap=128):
    for tr in range(min(r, cap), 0, -1):
        if r % tr == 0 and (tr % 8 == 0 or tr == r):
            return tr
    return r


def _chip_sum(grad, recv, place, name, wire_dtype):
    _, r, c = grad.shape
    tr = _row_tile(r, 256)

    def body(pl_ref, g_ref, a_ref, p_ref):
        p_ref[...] = (g_ref[...] + a_ref[...]).astype(p_ref.dtype)

    return pl.pallas_call(
        body,
        grid_spec=pltpu.PrefetchScalarGridSpec(
            num_scalar_prefetch=1, grid=(4, r // tr),
            in_specs=[pl.BlockSpec((None, None, tr, c), lambda q, i, pr: (q, pr[1], i, 0)),
                      pl.BlockSpec((None, tr, c), lambda q, i, pr: (q, i, 0))],
            out_specs=pl.BlockSpec((None, tr, c), lambda q, i, pr: (q, i, 0))),
        out_shape=jax.ShapeDtypeStruct((4, r, c), wire_dtype), name=name, compiler_params=_params(),
    )(place, grad.reshape(4, 2, r, c), recv)


def _adamw_sharded(grad, recv, others, place, w, m, v, name, layer=None, fill=None):
    r, c = w.shape[-2:]
    tr = _row_tile(r)

    def body(pl_ref, g_ref, a_ref, oth_ref, w_ref, m_ref, v_ref, *rest):
        g_out, d_out, nm_out, nv_out = rest[-4:]
        g = g_ref[...] + a_ref[...]
        for k in range(3):
            g = g + oth_ref[k].astype(F32)
        delta, nm, nv = _adamw_math(g, w_ref[...], m_ref[...], v_ref[...])
        g_out[...] = g
        d_out[...] = delta
        nm_out[...] = nm
        nv_out[...] = nv

    if layer is None:
        row = pl.BlockSpec((tr, c), lambda i, pr: (i, 0))
    else:
        row = pl.BlockSpec((None, tr, c), lambda i, pr: (layer, i, 0))
    n_fill = 0 if fill is None else 4
    in_specs = [pl.BlockSpec((None, None, tr, c), lambda i, pr: (pr[0], pr[1], i, 0)),
                pl.BlockSpec((None, tr, c), lambda i, pr: (pr[0], i, 0)),
                pl.BlockSpec((3, tr, c), lambda i, pr: (0, i, 0)), row, row, row]
    in_specs += [pl.BlockSpec(memory_space=pl.ANY)] * n_fill
    return pl.pallas_call(
        body,
        grid_spec=pltpu.PrefetchScalarGridSpec(
            num_scalar_prefetch=1, grid=(r // tr,), in_specs=in_specs, out_specs=[row] * 4),
        out_shape=[jax.ShapeDtypeStruct(w.shape, F32)] * 4, name=name, compiler_params=_params(),
        input_output_aliases={7 + j: j for j in range(n_fill)},
    )(place, grad.reshape(4, 2, r, c), recv, others, w, m, v, *([] if fill is None else fill))


def _sum_devices(parts, name):
    def body(p_ref, o_ref):
        total = p_ref[0]
        for k in range(1, N_SHARDS):
            total = total + p_ref[k]
        o_ref[...] = total

    return pl.pallas_call(body, out_shape=jax.ShapeDtypeStruct(parts.shape[1:], F32), name=name)(parts)


def _adamw_summed(parts, ws, ms, vs, name):
    n = len(parts)

    def body(*refs):
        p_refs, w_refs, m_refs, v_refs = refs[:n], refs[n:2 * n], refs[2 * n:3 * n], refs[3 * n:4 * n]
        o_refs = refs[4 * n:]
        for i in range(n):
            g = p_refs[i][0]
            for k in range(1, N_SHARDS):
                g = g + p_refs[i][k]
            delta, nm, nv = _adamw_math(g, w_refs[i][...], m_refs[i][...], v_refs[i][...])
            o_refs[4 * i][...] = g
            o_refs[4 * i + 1][...] = delta
            o_refs[4 * i + 2][...] = nm
            o_refs[4 * i + 3][...] = nv

    shapes = [jax.ShapeDtypeStruct(w.shape, F32) for w in ws for _ in range(4)]
    outs = pl.pallas_call(body, out_shape=shapes, name=name, compiler_params=_params())(*parts, *ws, *ms, *vs)
    return [outs[4 * i:4 * i + 4] for i in range(n)]


def _dup_heads(w):
    lead = w.shape[:-1]
    w4 = w.reshape(lead + (N_KV_HEADS, 1, HEAD_DIM))
    return jnp.broadcast_to(w4, lead + (N_KV_HEADS, 2, HEAD_DIM)).reshape(lead + (N_KV_HEADS * LANES,))


def _fold_heads(g):
    lead = g.shape[:-1]
    return g.reshape(lead + (N_KV_HEADS, 2, HEAD_DIM)).sum(axis=-2).reshape(lead + (N_KV_HEADS * HEAD_DIM,))


def kernel(x, a_norm, a_w_in, a_v_norm, a_w_s, a_b_s, a_w_out, f_norm, f_w_in, f_conv_w, f_conv_b, f_w_out, kv_norm, w_kv, k_norm, b_norm, b_w_q, b_q_norm, b_sinks, b_w_o, loss_target, m_a_norm, m_a_w_in, m_a_v_norm, m_a_w_s, m_a_b_s, m_a_w_out, m_f_norm, m_f_w_in, m_f_conv_w, m_f_conv_b, m_f_w_out, m_kv_norm, m_w_kv, m_k_norm, m_b_norm, m_b_w_q, m_b_q_norm, m_b_sinks, m_b_w_o, v_a_norm, v_a_w_in, v_a_v_norm, v_a_w_s, v_a_b_s, v_a_w_out, v_f_norm, v_f_w_in, v_f_conv_w, v_f_conv_b, v_f_w_out, v_kv_norm, v_w_kv, v_k_norm, v_b_norm, v_b_w_q, v_b_q_norm, v_b_sinks, v_b_w_o):
    d = D_MODEL
    xi, yi, ci = _coords()
    place = jnp.stack([2 * xi + yi, ci]).astype(jnp.int32)
    bf = lambda a: a.astype(BF16)
    row = lambda v_: v_.reshape(1, -1)
    x0, target = x[0], loss_target[0]
    t = x0.shape[0]
    res = {}

    red = {}

    def to_sibling(grads, wire=BF16):
        for k, g in grads.items():
            red[k] = dict(grad=g, wire=wire)
        ex = _ToSibling(list(grads.values()))
        ex.names = list(grads)
        return ex

    def to_chips(ex):
        for k, a in zip(ex.names, ex.results):
            red[k]["recv"] = a
            red[k]["psum"] = _chip_sum(red[k]["grad"], a, place, f"chip_sum_{k}", red[k]["wire"])
        nxt = _ToChips([red[k]["psum"] for k in ex.names])
        nxt.names = ex.names
        return nxt

    def landed(ex):
        for k, b in zip(ex.names, ex.results):
            red[k]["others"] = b

    def halves(ex, first_rows):
        parts = []
        for r0, nr in ((0, first_rows), (first_rows, ex.srcs[0].shape[1] - first_rows)):
            part = _ToChips(ex.srcs, rows=(r0, nr))
            part.names = ex.names
            parts.append(part)
        return parts

    def landed_halves(parts):
        for j, k in enumerate(parts[0].names):
            red[k]["others"] = jnp.concatenate([p.results[j] for p in parts], axis=1)

    def update(k, w, m, v, layer=None, fill=None):
        r = red[k]
        return _adamw_sharded(r["grad"], r["recv"], r["others"], place, w, m, v,
                              f"adamw_{k}", layer=layer, fill=fill)

    g_a_in, g_a_out, g_a_norm, g_a_v_norm, g_conv = _exchange_alone(
        _Gather([bf(a_w_in[0]), bf(a_w_out[0]), a_norm, a_v_norm, f_conv_w.reshape(6, FF_SHARD)]), "gather_first")
    a_norm_full, a_v_norm_full = g_a_norm.reshape(1, d), g_a_v_norm.reshape(1, d)
    conv_w = lax.reduce_precision(g_conv.reshape(N_SHARDS, 2, 3, FF_SHARD), 8, 7)
    cw = jnp.pad(jnp.transpose(conv_w, (1, 0, 2, 3)), ((0, 0), (0, 0), (0, 5), (0, 0)))
    w_a_in_flat = jnp.transpose(g_a_in, (1, 0, 2)).reshape(d, 2 * d)
    cb = f_conv_b.reshape(2, N_SHARDS, 1, FF_SHARD)
    tri = jnp.tril(jnp.ones((CHUNK, CHUNK), dtype=bool))
    w_causal = jnp.where(tri[None], a_w_s[0], 0.0).astype(BF16)
    w_causal_t = jnp.transpose(w_causal, (0, 2, 1))
    b_sb = jnp.broadcast_to(a_b_s[0][:, :, None], (N_GROUPS, CHUNK, CHUNK))
    w_a_out = g_a_out.reshape(d, d)
    gq = jnp.tile(b_q_norm.reshape(1, HEAD_DIM), (1, 2))
    gk = jnp.tile(k_norm.reshape(1, HEAD_DIM), (1, 2))
    sinks = b_sinks.reshape(N_Q_HEADS)

    ex = _Gather([bf(f_w_in[0]), bf(f_w_out[0])])
    zpre, x1, h1 = _sgu_fwd(x0, a_norm_full, g_a_in, a_v_norm_full, w_causal, b_sb, w_a_out, carry=ex)
    w_in0, w_out0 = ex.results[0], ex.results[1].reshape(D_FF, d)
    ex = _Gather([bf(w_kv), bf(b_w_q[0]), bf(b_w_o[0]), bf(f_w_in[1])], relay=False, early=True)
    x2, hf0, a0, pre0, hk, hq = _ffn_fwd(x1, f_norm[0:1], w_in0, cw[0], cb[0], w_out0, 0, carry=ex,
                                         next_gains=[row(kv_norm), b_norm])
    kv_full = ex.results[0].reshape(d, 2 * N_KV_HEADS * HEAD_DIM)
    w_q, w_o = ex.results[1].reshape(d, d), ex.results[2].reshape(d, d)
    w_in1 = ex.results[3]
    half = N_KV_HEADS * HEAD_DIM
    w_kv_dup = jnp.concatenate([_dup_heads(kv_full[:, :half]), _dup_heads(kv_full[:, half:])], axis=1)
    kvd = _mm_rows(hk, w_kv_dup, F32, "kv_proj")
    qraw = _mm_rows(hq, w_q, F32, "q_proj")
    ex = _Gather([bf(f_w_out[1])], relay=False, early=True)
    o = _attn_fwd(qraw, kvd, gq, gk, sinks, carry=ex)
    w_out1 = ex.results[0].reshape(D_FF, d)
    x3 = _mm_rows(o, w_o, F32, "o_proj", res=x2)
    _, hf1, a1, pre1, dy, loss_lanes = _ffn_fwd(x3, f_norm[1:2], w_in1, cw[1], cb[1], w_out1, 1, loss_target=target)

    dhu1, dw_out1, dcb1 = _ffn_bwd_act(pre1, w_out1, dy, 1)
    ex = to_sibling({"f_w_out1": dw_out1.reshape(N_SHARDS, D_FF // N_SHARDS, d)})
    da1, dx3, dcw1, dgf1 = _ffn_bwd_in(dhu1, a1, cw[1], w_in1, 1, carry=ex, norm=(x3, f_norm[1:2], dy))
    ex = to_chips(ex)
    dw_in1 = _ffn_wgrad_in(hf1, da1, 1, carry=ex)
    landed(ex)
    ex = to_sibling({"f_w_in1": dw_in1})
    d_o = _mm_rows(dx3, w_o, BF16, "o_proj_bwd", trans_w=True, carry=ex)
    ex = to_chips(ex)
    dw_o = _mm_wgrad(o, dx3, "o_wgrad").reshape(N_SHARDS, d // N_SHARDS, d)
    dq, dkv, dsink, dgq, dgk = _attn_bwd(qraw, kvd, d_o, gq, gk, sinks, carry=ex)
    landed(ex)
    dw_q = _mm_wgrad(hq, dq, "q_wgrad").reshape(N_SHARDS, d // N_SHARDS, d)
    dw_kv_dup = _mm_wgrad(hk, dkv, "kv_wgrad")
    dw_kv = jnp.concatenate(
        [_fold_heads(dw_kv_dup[:, :4 * LANES]), _fold_heads(dw_kv_dup[:, 4 * LANES:])], axis=1
    ).reshape(N_SHARDS, d // N_SHARDS, 2 * N_KV_HEADS * HEAD_DIM)
    ex = to_sibling({"b_w_o": dw_o, "b_w_q": dw_q, "w_kv": dw_kv})
    dx2, dg2 = _rms_bwd(x2, [row(kv_norm), b_norm], [dkv, dq], dx3, "kvq_norm_bwd", tm=512, carry=ex,
                        through=[w_kv_dup, w_q])
    ex = to_chips(ex)
    dhu0, dw_out0, dcb0 = _ffn_bwd_act(pre0, w_out0, dx2, 0, carry=ex)
    landed(ex)
    ex = to_sibling({"f_w_out0": dw_out0.reshape(N_SHARDS, D_FF // N_SHARDS, d)})
    da0, dhf0, dcw0 = _ffn_bwd_in(dhu0, a0, cw[0], w_in0, 0, carry=ex)
    ex = to_chips(ex)
    dw_in0 = _ffn_wgrad_in(hf0, da0, 0, carry=ex)
    landed(ex)
    ex = to_sibling({"f_w_in0": dw_in0})
    dx1, dgf0 = _rms_bwd(x1, [f_norm[0:1]], [dhf0], dx2, "f0_norm_bwd", carry=ex)
    ex_lo, ex_hi = halves(to_chips(ex), 448)
    dz, y, dwc, dbs, dgv = _sgu_bwd(dx1, zpre, w_a_out, a_v_norm_full, w_causal, w_causal_t, b_sb, carry=ex_lo)
    nsub = g_a_in.shape[2]
    dw_a_in = _mm(
        h1, dz, pl.BlockSpec((t, d), lambda s, j, kk: (0, 0)), pl.BlockSpec((t, nsub), lambda s, j, kk: (0, s)),
        pl.BlockSpec((None, d, nsub), lambda s, j, kk: (s, 0, 0)), jax.ShapeDtypeStruct((N_SHARDS, d, nsub), F32),
        (N_SHARDS, 1, 1), TN, "a_in_wgrad", carry=ex_hi)
    landed_halves([ex_lo, ex_hi])
    ex_a_in = to_sibling({"a_w_in": dw_a_in})
    dw_a_out = _mm_wgrad(y, dx1, "a_out_wgrad", carry=ex_a_in).reshape(N_SHARDS, d // N_SHARDS, d)
    ex_a_in = to_chips(ex_a_in)

    def bias_grad(dcb):
        return jnp.transpose(dcb[:, :, 0, :], (1, 0, 2)).reshape(-1)

    g_conv_w = jnp.concatenate([dcw0[:, 0:3, :], dcw1[:, 0:3, :]], axis=1)
    g_a_v_norm = dgv[0].reshape(N_SHARDS, 1, LANES)
    rep = ["a_w_s", "a_b_s", "f_norm", "f_conv_b", "kv_norm", "k_norm", "b_norm", "b_q_norm", "b_sinks"]
    rep_g = dict(
        a_w_s=dwc.reshape(N_GROUPS * CHUNK, CHUNK), a_b_s=dbs[:, :, 0], f_norm=jnp.stack([dgf0[0], dgf1[0]]),
        f_conv_b=jnp.stack([bias_grad(dcb0), bias_grad(dcb1)]), kv_norm=dg2[0:1],
        k_norm=(dgk[0, :HEAD_DIM] + dgk[0, HEAD_DIM:])[None], b_norm=dg2[1:2],
        b_q_norm=(dgq[0, :HEAD_DIM] + dgq[0, HEAD_DIM:])[None], b_sinks=dsink[:, 0][None])
    ex_big = to_sibling({"a_w_out": dw_a_out})
    ex_small = to_sibling({"a_v_norm": g_a_v_norm, "f_conv_w": g_conv_w}, wire=F32)
    ex_rep = _Gather([rep_g[k] for k in rep] + [loss_lanes], relay=False)
    together = _Together([ex_a_in, ex_big, ex_small, ex_rep])
    dh1 = _mm_rows(dz, w_a_in_flat, F32, "a_in_bwd", trans_w=True, carry=together)
    together.spread()
    landed(ex_a_in)
    ex_big, ex_small = to_chips(ex_big), to_chips(ex_small)
    together = _Together([ex_big, ex_small])
    grad_x, dg0 = _rms_bwd(x0, [a_norm_full], [dh1], dx1, "a_norm_bwd", carry=together)
    together.spread()
    landed(ex_big)
    landed(ex_small)
    (a_norm_parts,) = _exchange_alone(_ToOwners([dg0[0].reshape(N_SHARDS, 1, LANES)]), "a_norm_to_owners")

    res["f_w_out"] = update("f_w_out1", f_w_out, m_f_w_out, v_f_w_out, layer=1)
    w_in_t = [jnp.swapaxes(a_, 1, 2) for a_ in (f_w_in, m_f_w_in, v_f_w_in)]
    res["f_w_in"] = update("f_w_in1", *w_in_t, layer=1)
    res["b_w_o"] = update("b_w_o", b_w_o, m_b_w_o, v_b_w_o, layer=0)
    res["b_w_q"] = update("b_w_q", b_w_q, m_b_w_q, v_b_w_q, layer=0)
    res["w_kv"] = update("w_kv", w_kv, m_w_kv, v_w_kv)
    res["f_w_out"] = update("f_w_out0", f_w_out, m_f_w_out, v_f_w_out, layer=0, fill=res["f_w_out"])
    res["f_w_in"] = [jnp.swapaxes(o_, 1, 2) for o_ in update("f_w_in0", *w_in_t, layer=0, fill=res["f_w_in"])]
    res["a_w_out"] = update("a_w_out", a_w_out, m_a_w_out, v_a_w_out, layer=0)
    res["a_w_in"] = update("a_w_in", a_w_in, m_a_w_in, v_a_w_in, layer=0)
    res["a_v_norm"] = update("a_v_norm", a_v_norm, m_a_v_norm, v_a_v_norm)
    res["f_conv_w"] = [o_.reshape(f_conv_w.shape) for o_ in update(
        "f_conv_w", f_conv_w.reshape(6, FF_SHARD), m_f_conv_w.reshape(6, FF_SHARD), v_f_conv_w.reshape(6, FF_SHARD))]

    rep_w = dict(a_w_s=a_w_s, a_b_s=a_b_s, f_norm=f_norm, f_conv_b=f_conv_b, kv_norm=kv_norm, k_norm=k_norm,
                 b_norm=b_norm, b_q_norm=b_q_norm, b_sinks=b_sinks, a_norm=a_norm)
    rep_m = dict(a_w_s=m_a_w_s, a_b_s=m_a_b_s, f_norm=m_f_norm, f_conv_b=m_f_conv_b, kv_norm=m_kv_norm,
                 k_norm=m_k_norm, b_norm=m_b_norm, b_q_norm=m_b_q_norm, b_sinks=m_b_sinks, a_norm=m_a_norm)
    rep_v = dict(a_w_s=v_a_w_s, a_b_s=v_a_b_s, f_norm=v_f_norm, f_conv_b=v_f_conv_b, kv_norm=v_kv_norm,
                 k_norm=v_k_norm, b_norm=v_b_norm, b_q_norm=v_b_q_norm, b_sinks=v_b_sinks, a_norm=v_a_norm)
    keys = rep + ["a_norm"]
    loss = _sum_devices(ex_rep.results[-1], "loss_sum")[0, 0]
    parts = ex_rep.results[:-1] + [a_norm_parts]
    as2d = lambda a, p: a.reshape(p.shape[1:])
    rep_outs = _adamw_summed(parts, [as2d(rep_w[k], p) for k, p in zip(keys, parts)],
                             [as2d(rep_m[k], p) for k, p in zip(keys, parts)],
                             [as2d(rep_v[k], p) for k, p in zip(keys, parts)], "adamw_replicated")
    for j, key in enumerate(keys):
        res[key] = [o_.reshape(rep_w[key].shape) for o_ in rep_outs[j]]

    order = ["a_norm", "a_w_in", "a_v_norm", "a_w_s", "a_b_s", "a_w_out", "f_norm", "f_w_in", "f_conv_w", "f_conv_b",
             "f_w_out", "kv_norm", "w_kv", "k_norm", "b_norm", "b_w_q", "b_q_norm", "b_sinks", "b_w_o"]
    outs = [loss, grad_x[None]]
    for j in range(4):
        outs += [res[k][j] for k in order]
    return tuple(outs)
```

```python
import jax
import jax.numpy as jnp
from jax import lax
from jax.experimental import pallas as pl
from jax.experimental.pallas import tpu as pltpu

F32 = jnp.float32
BF16 = jnp.bfloat16
EPS = 1e-6
D_MODEL = 1024
CHUNK = 128
N_GROUPS = 8
N_SHARDS = 8
HEAD_DIM = 64
N_Q_HEADS = 16
N_KV_HEADS = 4
D_FF = 2816
FF_SHARD = 2 * D_FF // N_SHARDS
LANES = 128
NEG_BIG = -1e30
ADAM_LR = 0.001
ADAM_B1 = 0.9
ADAM_B2 = 0.999
ADAM_EPS = 1e-08
ADAM_WD = 0.01
ADAM_STEP = 10
VMEM_LIMIT_BYTES = 56 * 1024 * 1024
MESH = pl.DeviceIdType.MESH

NN = (((1,), (0,)), ((), ()))
NT = (((1,), (1,)), ((), ()))
TN = (((0,), (0,)), ((), ()))
SLOPES = tuple(2.0 ** (-8.0 * (h + 1) / N_Q_HEADS) for h in range(N_Q_HEADS))


def _params(sem=None):
    return pltpu.CompilerParams(dimension_semantics=sem, vmem_limit_bytes=VMEM_LIMIT_BYTES)


def _dot(a, b, dims=NN):
    return lax.dot_general(a, b, dims, preferred_element_type=F32)


def _sigmoid(x):
    return 1.0 / (1.0 + jnp.exp(-x))


def _gelu_parts(z):
    cdf = 0.5 * (1.0 + lax.erf(z * (2.0 ** -0.5)))
    pdf = jnp.exp(-0.5 * z * z) * 0.3989422804014327
    return cdf, pdf


def _coords():
    return lax.axis_index("x"), lax.axis_index("y"), lax.axis_index("c")


class _Gather:
    def __init__(self, srcs, relay=True, early=False):
        self.srcs = list(srcs)
        self.early = early
        n = len(self.srcs)
        self.relayed = [relay and s.shape[0] % 32 == 0 for s in self.srcs]
        self.out_shapes = [jax.ShapeDtypeStruct((N_SHARDS,) + s.shape, s.dtype) for s in self.srcs]
        self.sems = [pltpu.SemaphoreType.DMA((n, 9)), pltpu.SemaphoreType.DMA((n, 9)), pltpu.SemaphoreType.DMA((n,))]

    def _plan(self, src, dst, sems):
        send_sems, recv_sems, local_sems = sems
        x, y, c = _coords()
        n = len(src)

        def rows(e, dev, half=None):
            block = dst[e].at[4 * dev[0] + 2 * dev[1] + dev[2]]
            if half is None:
                return block
            nr = self.srcs[e].shape[0] // 2
            return block.at[pl.ds(half * nr, nr)]

        def copy(e, slot, block, to, half=None, from_own=False):
            return pltpu.make_async_remote_copy(
                src_ref=src[e] if from_own else rows(e, block, half), dst_ref=rows(e, block, half),
                send_sem=send_sems.at[e, slot], recv_sem=recv_sems.at[e, slot], device_id=to, device_id_type=MESH)

        return n, x, y, c, rows, copy, local_sems

    def start(self, src, dst, sems):
        n, x, y, c, rows, copy, local_sems = self._plan(src, dst, sems)
        me = (x, y, c)
        for e in range(n):
            pltpu.make_async_copy(src[e], rows(e, me), local_sems.at[e]).start()
            copy(e, 0, me, (x, y, 1 - c), from_own=True).start()
            copy(e, 1, me, (1 - x, y, c), from_own=True).start()
            copy(e, 2, me, (x, 1 - y, c), from_own=True).start()
            if not self.relayed[e]:
                copy(e, 3, me, (1 - x, 1 - y, c), from_own=True).start()

    def pass_on(self, src, dst, sems, wait=True):
        n, x, y, c, rows, copy, local_sems = self._plan(src, dst, sems)
        me, sibling = (x, y, c), (x, y, 1 - c)
        over_x, over_y, diagonal = (1 - x, y, c), (x, 1 - y, c), (1 - x, 1 - y, c)
        sent = []

        def arrived(cp):
            if wait:
                cp.wait_recv()

        def send(cp):
            if wait:
                cp.start()
            sent.append(cp)

        for slot, owner, onward, half in ((1, over_x, over_y, 0), (2, over_y, over_x, 1)):
            for e in range(n):
                arrived(copy(e, slot, owner, me))
                if self.relayed[e]:
                    send(copy(e, 3 + half, owner, onward, half=half))
                send(copy(e, 4 + slot, owner, sibling))
        for e in range(n):
            if self.relayed[e]:
                for half in (0, 1):
                    arrived(copy(e, 3 + half, diagonal, me, half=half))
                    send(copy(e, 7 + half, diagonal, sibling, half=half))
            else:
                arrived(copy(e, 3, diagonal, me))
                send(copy(e, 7, diagonal, sibling))
        return sent

    def finish(self, src, dst, sems, passed_on=False):
        n, x, y, c, rows, copy, local_sems = self._plan(src, dst, sems)
        me, sibling = (x, y, c), (x, y, 1 - c)
        over_x, over_y, diagonal = (1 - x, y, c), (x, 1 - y, c), (1 - x, 1 - y, c)
        sent = self.pass_on(src, dst, sems, wait=not passed_on)
        for e in range(n):
            copy(e, 0, sibling, me).wait_recv()
            copy(e, 5, (1 - x, y, 1 - c), me).wait_recv()
            copy(e, 6, (x, 1 - y, 1 - c), me).wait_recv()
            if self.relayed[e]:
                for half in (0, 1):
                    copy(e, 7 + half, (1 - x, 1 - y, 1 - c), me, half=half).wait_recv()
            else:
                copy(e, 7, (1 - x, 1 - y, 1 - c), me).wait_recv()
        for e in range(n):
            copy(e, 0, me, sibling, from_own=True).wait_send()
            copy(e, 1, me, over_x, from_own=True).wait_send()
            copy(e, 2, me, over_y, from_own=True).wait_send()
            if not self.relayed[e]:
                copy(e, 3, me, diagonal, from_own=True).wait_send()
            pltpu.make_async_copy(src[e], rows(e, me), local_sems.at[e]).wait()
        for cp in sent:
            cp.wait_send()


class _ToSibling:
    def __init__(self, grads):
        self.srcs = list(grads)
        n = len(self.srcs)
        self.out_shapes = [jax.ShapeDtypeStruct((4,) + g.shape[1:], g.dtype) for g in self.srcs]
        self.sems = [pltpu.SemaphoreType.DMA((n, 4)), pltpu.SemaphoreType.DMA((n, 4))]

    def _copies(self, src, dst, sems):
        send_sems, recv_sems = sems
        x, y, c = _coords()
        return [
            pltpu.make_async_remote_copy(
                src_ref=src[i].at[2 * q + (1 - c)], dst_ref=dst[i].at[q], send_sem=send_sems.at[i, q],
                recv_sem=recv_sems.at[i, q], device_id=(x, y, 1 - c), device_id_type=MESH)
            for i in range(len(src)) for q in range(4)]

    def start(self, src, dst, sems):
        for cp in self._copies(src, dst, sems):
            cp.start()

    def finish(self, src, dst, sems):
        for cp in self._copies(src, dst, sems):
            cp.wait()


class _ToChips:
    def __init__(self, psums, rows=None):
        self.srcs = list(psums)
        n = len(self.srcs)
        self.rows = rows
        self.out_shapes = [
            jax.ShapeDtypeStruct((3, p.shape[1] if rows is None else rows[1]) + p.shape[2:], p.dtype)
            for p in self.srcs]
        self.sems = [pltpu.SemaphoreType.DMA((n, 3)), pltpu.SemaphoreType.DMA((n, 3))]

    def _copies(self, src, dst, sems):
        send_sems, recv_sems = sems
        x, y, c = _coords()
        peers = [(x, 1 - y), (1 - x, y), (1 - x, 1 - y)]

        def part(i, q):
            if self.rows is None:
                return src[i].at[q]
            return src[i].at[q, pl.ds(self.rows[0], self.rows[1])]

        return [
            pltpu.make_async_remote_copy(
                src_ref=part(i, 2 * px + py), dst_ref=dst[i].at[r], send_sem=send_sems.at[i, r],
                recv_sem=recv_sems.at[i, r], device_id=(px, py, c), device_id_type=MESH)
            for i in range(len(src)) for r, (px, py) in enumerate(peers)]

    def start(self, src, dst, sems):
        for cp in self._copies(src, dst, sems):
            cp.start()

    def finish(self, src, dst, sems):
        for cp in self._copies(src, dst, sems):
            cp.wait()


class _ToOwners:
    def __init__(self, grads):
        self.srcs = list(grads)
        n = len(self.srcs)
        self.out_shapes = [jax.ShapeDtypeStruct(g.shape, g.dtype) for g in self.srcs]
        self.sems = [pltpu.SemaphoreType.DMA((n, 7)), pltpu.SemaphoreType.DMA((n, 7)), pltpu.SemaphoreType.DMA((n,))]

    def _copies(self, src, dst, sems):
        send_sems, recv_sems, local_sems = sems
        x, y, c = _coords()
        me = 4 * x + 2 * y + c
        copies = [pltpu.make_async_copy(src[i].at[me], dst[i].at[me], local_sems.at[i]) for i in range(len(src))]
        for i in range(len(src)):
            for rel in range(1, N_SHARDS):
                px = x ^ (rel >> 2) if rel >> 2 else x
                py = y ^ ((rel >> 1) & 1) if (rel >> 1) & 1 else y
                pc = c ^ (rel & 1) if rel & 1 else c
                copies.append(pltpu.make_async_remote_copy(
                    src_ref=src[i].at[4 * px + 2 * py + pc], dst_ref=dst[i].at[me], send_sem=send_sems.at[i, rel - 1],
                    recv_sem=recv_sems.at[i, rel - 1], device_id=(px, py, pc), device_id_type=MESH))
        return copies

    def start(self, src, dst, sems):
        for cp in self._copies(src, dst, sems):
            cp.start()

    def finish(self, src, dst, sems):
        for cp in self._copies(src, dst, sems):
            cp.wait()


class _Together:
    def __init__(self, parts):
        self.parts = list(parts)
        self.srcs = [s for p in self.parts for s in p.srcs]
        self.out_shapes = [s for p in self.parts for s in p.out_shapes]
        self.sems = [s for p in self.parts for s in p.sems]

    def _split(self, src, dst, sems):
        a = b = c = 0
        for p in self.parts:
            na, nc = len(p.srcs), len(p.sems)
            yield p, src[a:a + na], dst[b:b + na], sems[c:c + nc]
            a, b, c = a + na, b + na, c + nc

    def start(self, src, dst, sems):
        for p, s, d, m in self._split(src, dst, sems):
            p.start(s, d, m)

    def finish(self, src, dst, sems):
        for p, s, d, m in self._split(src, dst, sems):
            p.finish(s, d, m)

    def spread(self):
        b = 0
        for p in self.parts:
            p.results = self.results[b:b + len(p.srcs)]
            b += len(p.srcs)


def _call(body, args, *, grid, in_specs, out_specs, out_shape, name, scratch=(), sem=None, carry=None):
    out_shape, out_specs = list(out_shape), list(out_specs)
    if carry is None:
        return pl.pallas_call(
            body, grid=grid, in_specs=list(in_specs), out_specs=out_specs, out_shape=out_shape,
            scratch_shapes=list(scratch), name=name, compiler_params=_params(sem))(*args)
    n_in, n_out, n_scr, n_c = len(args), len(out_shape), len(scratch), len(carry.srcs)
    steps = tuple(grid)
    total = 1
    for n_ax in steps:
        total *= n_ax
    early = getattr(carry, "early", False) and total >= 8
    early_step = total - max(2, total // 8)

    def carried(*refs):
        ins, rest = refs[:n_in], refs[n_in:]
        c_src, rest = rest[:n_c], rest[n_c:]
        outs, rest = rest[:n_out], rest[n_out:]
        c_dst, rest = rest[:n_c], rest[n_c:]
        scr, sems = rest[:n_scr], rest[n_scr:]
        step = pl.program_id(0)
        for ax in range(1, len(steps)):
            step = step * steps[ax] + pl.program_id(ax)

        @pl.when(step == 0)
        def _():
            carry.start(c_src, c_dst, sems)

        body(*ins, *outs, *scr)

        if early:
            @pl.when(step == early_step)
            def _():
                carry.pass_on(c_src, c_dst, sems)

        @pl.when(step == total - 1)
        def _():
            if early:
                carry.finish(c_src, c_dst, sems, passed_on=True)
            else:
                carry.finish(c_src, c_dst, sems)

    hbm = pl.BlockSpec(memory_space=pl.ANY)
    res = pl.pallas_call(
        carried, grid=grid, in_specs=list(in_specs) + [hbm] * n_c, out_specs=out_specs + [hbm] * n_c,
        out_shape=out_shape + carry.out_shapes, scratch_shapes=list(scratch) + carry.sems, name=name,
        compiler_params=_params(("arbitrary",) * len(steps)))(*args, *carry.srcs)
    carry.results = list(res[n_out:])
    return list(res[:n_out])


def _exchange_alone(ex, name):
    n = len(ex.srcs)

    def body(*refs):
        src, dst, sems = refs[:n], refs[n:2 * n], refs[2 * n:]
        ex.start(src, dst, sems)
        ex.finish(src, dst, sems)

    hbm = pl.BlockSpec(memory_space=pl.ANY)
    res = pl.pallas_call(body, in_specs=[hbm] * n, out_specs=[hbm] * n, out_shape=ex.out_shapes,
                         scratch_shapes=ex.sems, name=name)(*ex.srcs)
    ex.results = list(res)
    return ex.results


def _rms_bwd(x, gains, dhs, dres, name, tm=256, carry=None, through=None):
    t, d = x.shape
    n = len(gains)
    n_w = 0 if through is None else n

    def body(*refs):
        x_ref, dres_ref = refs[0], refs[1]
        g_refs, dh_refs, w_refs = refs[2:2 + n], refs[2 + n:2 + 2 * n], refs[2 + 2 * n:2 + 2 * n + n_w]
        dx_ref, dg_ref = refs[2 + 2 * n + n_w], refs[3 + 2 * n + n_w]
        i = pl.program_id(0)

        @pl.when(i == 0)
        def _():
            dg_ref[...] = jnp.zeros_like(dg_ref)

        xf = x_ref[...]
        r = lax.rsqrt(jnp.mean(xf * xf, axis=-1, keepdims=True) + EPS)
        xhat = xf * r
        dx = dres_ref[...]
        for j in range(n):
            dh = dh_refs[j][...]
            if n_w:
                dh = _dot(dh.astype(BF16), w_refs[j][...], NT)
            dg_ref[j:j + 1, :] += jnp.sum(dh * xhat, axis=0, keepdims=True)
            gy = dh * g_refs[j][...]
            dx = dx + r * (gy - xhat * jnp.mean(gy * xhat, axis=-1, keepdims=True))
        dx_ref[...] = dx

    row = pl.BlockSpec((tm, d), lambda i: (i, 0))
    vec = pl.BlockSpec((1, d), lambda i: (0, 0))
    dh_rows = [pl.BlockSpec((tm, dh.shape[1]), lambda i: (i, 0)) for dh in dhs]
    w_full = [] if through is None else [pl.BlockSpec(w.shape, lambda i: (0, 0)) for w in through]
    return _call(body, [x, dres, *gains, *dhs, *(through or [])], grid=(t // tm,),
                 in_specs=[row, row] + [vec] * n + dh_rows + w_full,
                 out_specs=[row, pl.BlockSpec((8, d), lambda i: (0, 0))],
                 out_shape=[jax.ShapeDtypeStruct((t, d), F32), jax.ShapeDtypeStruct((8, d), F32)],
                 name=name, sem=("arbitrary",), carry=carry)


def _mm(a, b, a_spec, b_spec, o_spec, out_shape, grid, dims, name, res=None, res_spec=None, carry=None):
    nk = grid[2]
    acc_shape = tuple(s for s in o_spec.block_shape if s is not None)

    def body(*refs):
        a_ref, b_ref = refs[0], refs[1]
        r_ref = refs[2] if res is not None else None
        o_ref = refs[3] if res is not None else refs[2]
        p = _dot(a_ref[...].astype(BF16), b_ref[...].astype(BF16), dims)
        if nk == 1:
            if res is not None:
                p = p + r_ref[...]
            o_ref[...] = p.astype(o_ref.dtype)
            return
        acc_ref = refs[-1]
        k = pl.program_id(2)

        @pl.when(k == 0)
        def _():
            acc_ref[...] = p

        @pl.when(k > 0)
        def _():
            acc_ref[...] += p

        @pl.when(k == nk - 1)
        def _():
            out = acc_ref[...]
            if res is not None:
                out = out + r_ref[...]
            o_ref[...] = out.astype(o_ref.dtype)

    ins = [a, b] + ([res] if res is not None else [])
    specs = [a_spec, b_spec] + ([res_spec] if res is not None else [])
    return _call(body, ins, grid=grid, in_specs=specs, out_specs=[o_spec], out_shape=[out_shape],
                 scratch=[pltpu.VMEM(acc_shape, F32)] if nk > 1 else [], name=name,
                 sem=("parallel", "parallel", "arbitrary"), carry=carry)[0]


def _mm_rows(a, w, out_dtype, name, trans_w=False, res=None, tm=1024, carry=None):
    t, k = a.shape
    tm = min(tm, t)
    n = w.shape[0] if trans_w else w.shape[1]
    return _mm(
        a, w, pl.BlockSpec((tm, k), lambda i, j, kk: (i, 0)), pl.BlockSpec(w.shape, lambda i, j, kk: (0, 0)),
        pl.BlockSpec((tm, n), lambda i, j, kk: (i, 0)), jax.ShapeDtypeStruct((t, n), out_dtype), (t // tm, 1, 1),
        NT if trans_w else NN, name, res=res,
        res_spec=None if res is None else pl.BlockSpec((tm, n), lambda i, j, kk: (i, 0)), carry=carry)


def _mm_wgrad(a, b, name, carry=None):
    t, m = a.shape
    n = b.shape[1]
    tn = n // (4 if b.dtype == F32 else 2)
    return _mm(
        a, b, pl.BlockSpec((t, m), lambda i, j, kk: (0, 0)), pl.BlockSpec((t, tn), lambda i, j, kk: (0, j)),
        pl.BlockSpec((m, tn), lambda i, j, kk: (0, j)), jax.ShapeDtypeStruct((m, n), F32), (1, n // tn, 1), TN, name,
        carry=carry)


def _sgu_fwd(x0, g, w_in, g_v, w_c, b_sb, w_out, tm=256, carry=None):
    t, d = x0.shape
    nsub = w_in.shape[2]

    def body(x_ref, g_ref, win_ref, gv_ref, wc_ref, bsb_ref, wout_ref, zpre_ref, x1_ref, h_ref, u_s, v_s, vn_s, y_s):
        xf = x_ref[...]
        h = (xf * lax.rsqrt(jnp.mean(xf * xf, axis=-1, keepdims=True) + EPS) * g_ref[...]).astype(BF16)
        h_ref[...] = h
        for k in range(N_SHARDS):
            zk = _dot(h, win_ref[k])
            zpre_ref[:, k * nsub:(k + 1) * nsub] = zk
            cdf, _ = _gelu_parts(zk)
            if k < N_SHARDS // 2:
                u_s[:, k * nsub:(k + 1) * nsub] = zk * cdf
            else:
                v_s[:, (k - 4) * nsub:(k - 3) * nsub] = zk * cdf
        v = v_s[...]
        rv = lax.rsqrt(jnp.mean(v * v, axis=-1, keepdims=True) + EPS)
        vn_s[...] = (v * rv * gv_ref[...]).astype(BF16)
        for ci in range(tm // CHUNK):
            rows = slice(ci * CHUNK, (ci + 1) * CHUNK)
            for g in range(N_GROUPS):
                cols = slice(g * LANES, (g + 1) * LANES)
                sv = _dot(wc_ref[g], vn_s[rows, cols]) + bsb_ref[g]
                y_s[rows, cols] = (u_s[rows, cols] * sv).astype(BF16)
        x1_ref[...] = x_ref[...] + _dot(y_s[...], wout_ref[...])

    row = pl.BlockSpec((tm, d), lambda i: (i, 0))
    full = lambda a: pl.BlockSpec(a.shape, lambda i: (0,) * a.ndim)
    return _call(
        body, [x0, g, w_in, g_v, w_c, b_sb, w_out], grid=(t // tm,),
        in_specs=[row, full(g), full(w_in), full(g_v), full(w_c), full(b_sb), full(w_out)],
        out_specs=[pl.BlockSpec((tm, 2 * d), lambda i: (i, 0)), row, row],
        out_shape=[jax.ShapeDtypeStruct((t, 2 * d), F32), jax.ShapeDtypeStruct((t, d), F32),
                   jax.ShapeDtypeStruct((t, d), BF16)],
        scratch=[pltpu.VMEM((tm, d), F32), pltpu.VMEM((tm, d), F32), pltpu.VMEM((tm, d), BF16),
                 pltpu.VMEM((tm, d), BF16)],
        name="sgu_fwd", carry=carry)


def _sgu_bwd(dx1, zpre, w_out, g_v, w_c, w_ct, b_sb, tm=256, carry=None):
    t, d = dx1.shape

    def body(dx_ref, zpre_ref, wout_ref, gv_ref, wc_ref, wct_ref, bsb_ref,
             dz_ref, y_ref, dwc_ref, dbs_ref, dgv_ref, u_s, vn_s, dy_s, du_s, dvn_s):
        i = pl.program_id(0)

        @pl.when(i == 0)
        def _():
            dwc_ref[...] = jnp.zeros_like(dwc_ref)
            dbs_ref[...] = jnp.zeros_like(dbs_ref)
            dgv_ref[...] = jnp.zeros_like(dgv_ref)

        dy_s[...] = _dot(dx_ref[...].astype(BF16), wout_ref[...], NT)
        zu = zpre_ref[:, :d]
        zv = zpre_ref[:, d:]
        cdf_u, pdf_u = _gelu_parts(zu)
        cdf_v, pdf_v = _gelu_parts(zv)
        u_s[...] = zu * cdf_u
        v = zv * cdf_v
        rv = lax.rsqrt(jnp.mean(v * v, axis=-1, keepdims=True) + EPS)
        vhat = v * rv
        gv = gv_ref[...]
        vn_s[...] = (vhat * gv).astype(BF16)
        for ci in range(tm // CHUNK):
            rows = slice(ci * CHUNK, (ci + 1) * CHUNK)
            for g in range(N_GROUPS):
                cols = slice(g * LANES, (g + 1) * LANES)
                vnb = vn_s[rows, cols]
                sv = _dot(wc_ref[g], vnb) + bsb_ref[g]
                dyb = dy_s[rows, cols]
                ub = u_s[rows, cols]
                dsv = dyb * ub
                du_s[rows, cols] = dyb * sv
                y_ref[rows, cols] = (ub * sv).astype(BF16)
                dsvb = dsv.astype(BF16)
                dbs_ref[g] += dsv
                dwc_ref[g] += _dot(dsvb, vnb, NT)
                dvn_s[rows, cols] = _dot(wct_ref[g], dsvb)
        dvn = dvn_s[...]
        dgv_ref[0:1, :] += jnp.sum(dvn * vhat, axis=0, keepdims=True)
        gy = dvn * gv
        dv = rv * (gy - vhat * jnp.mean(gy * vhat, axis=-1, keepdims=True))
        dz_ref[:, :d] = (du_s[...] * (cdf_u + zu * pdf_u)).astype(BF16)
        dz_ref[:, d:] = (dv * (cdf_v + zv * pdf_v)).astype(BF16)

        @pl.when(i == t // tm - 1)
        def _():
            tri = (lax.broadcasted_iota(jnp.int32, (CHUNK, CHUNK), 0)
                   >= lax.broadcasted_iota(jnp.int32, (CHUNK, CHUNK), 1))
            for g in range(N_GROUPS):
                dwc_ref[g] = jnp.where(tri, dwc_ref[g], 0.0)
                dbs_ref[g] = jnp.broadcast_to(jnp.sum(dbs_ref[g], axis=1, keepdims=True), (CHUNK, CHUNK))

    row = pl.BlockSpec((tm, d), lambda i: (i, 0))
    row2 = pl.BlockSpec((tm, 2 * d), lambda i: (i, 0))
    full = lambda a: pl.BlockSpec(a.shape, lambda i: (0,) * a.ndim)
    grp = pl.BlockSpec((N_GROUPS, CHUNK, CHUNK), lambda i: (0, 0, 0))
    return _call(
        body, [dx1, zpre, w_out, g_v, w_c, w_ct, b_sb], grid=(t // tm,),
        in_specs=[row, row2, full(w_out), full(g_v), full(w_c), full(w_ct), full(b_sb)],
        out_specs=[row2, row, grp, grp, pl.BlockSpec((8, d), lambda i: (0, 0))],
        out_shape=[jax.ShapeDtypeStruct((t, 2 * d), BF16), jax.ShapeDtypeStruct((t, d), BF16),
                   jax.ShapeDtypeStruct((N_GROUPS, CHUNK, CHUNK), F32),
                   jax.ShapeDtypeStruct((N_GROUPS, CHUNK, CHUNK), F32), jax.ShapeDtypeStruct((8, d), F32)],
        scratch=[pltpu.VMEM((tm, d), F32), pltpu.VMEM((tm, d), BF16), pltpu.VMEM((tm, d), F32),
                 pltpu.VMEM((tm, d), F32), pltpu.VMEM((tm, d), F32)],
        name="sgu_bwd", sem=("arbitrary",), carry=carry)


ROW_CHUNK = 256
HALO = 16


def _ffn_fwd(x, g, w_in, cw, cb, w_out, layer, tm=512, carry=None, next_gains=(), loss_target=None):
    t, d = x.shape
    nc = N_SHARDS // 2
    n_gains = len(next_gains)
    with_loss = loss_target is not None

    def body(x_ref, xp_ref, g_ref, wg_ref, wu_ref, cwg_ref, cbg_ref, cwu_ref, cbu_ref, wout_ref, *rest):
        extra_in, rest = rest[:n_gains + with_loss], rest[n_gains + with_loss:]
        o_ref, hf_ref, a_ref, pre_ref = rest[:4]
        extra_out, hw_s = rest[4:-1], rest[-1]
        i, c = pl.program_id(0), pl.program_id(1)

        @pl.when(c == 0)
        def _():
            keep = jnp.where(i == 0, 0.0, 1.0)
            xw = jnp.concatenate([xp_ref[...] * keep, x_ref[...]], axis=0)
            xhat = xw * lax.rsqrt(jnp.mean(xw * xw, axis=-1, keepdims=True) + EPS)
            hw_s[...] = (xhat * g_ref[...]).astype(BF16)
            hf_ref[...] = hw_s[HALO:, :]
            o_ref[...] = x_ref[...]

        hw = hw_s[...]
        pre = []
        for j, (w_ref, cw_ref, cb_ref) in enumerate(((wg_ref, cwg_ref, cbg_ref), (wu_ref, cwu_ref, cbu_ref))):
            ab = _dot(hw, w_ref[...]).astype(BF16)
            a_ref[j] = ab[HALO:]
            win = ab.astype(F32)
            cw_v = cw_ref[...]
            pre.append(cw_v[2:3, :] * win[HALO:] + cw_v[1:2, :] * pltpu.roll(win, 1, 0)[HALO:]
                       + cw_v[0:1, :] * pltpu.roll(win, 2, 0)[HALO:] + cb_ref[...])
            pre_ref[j] = pre[j]
        act = (pre[0] * _sigmoid(pre[0]) * pre[1]).astype(BF16)
        o_ref[...] += _dot(act, wout_ref[...])

        if with_loss:
            @pl.when((i == 0) & (c == 0))
            def _():
                extra_out[-1][...] = jnp.zeros_like(extra_out[-1])

        @pl.when(c == nc - 1)
        def _():
            xn = o_ref[...]
            if n_gains:
                xhat = xn * lax.rsqrt(jnp.mean(xn * xn, axis=-1, keepdims=True) + EPS)
                for k in range(n_gains):
                    extra_out[k][...] = (xhat * extra_in[k][...]).astype(BF16)
            if with_loss:
                err = xn - extra_in[-1][...]
                extra_out[-2][...] = err * (1.0 / d)
                part = jnp.sum(jnp.sum(err * err, axis=0, keepdims=True), axis=1, keepdims=True)
                extra_out[-1][...] += jnp.broadcast_to(0.5 / d * part, extra_out[-1].shape)

    row = pl.BlockSpec((tm, d), lambda i, c: (i, 0))
    vec = pl.BlockSpec((1, d), lambda i, c: (0, 0))
    shard = lambda rows, up: pl.BlockSpec((None, rows, FF_SHARD), lambda i, c: (c + up * nc, 0, 0))
    pair = pl.BlockSpec((2, None, tm, FF_SHARD), lambda i, c: (0, c, i, 0))
    lanes = pl.BlockSpec((8, LANES), lambda i, c: (0, 0))
    outs = _call(
        body, [x, x, g, w_in, w_in, cw, cb, cw, cb, w_out, *next_gains] + ([loss_target] if with_loss else []),
        grid=(t // tm, nc),
        in_specs=[row, pl.BlockSpec((HALO, d), lambda i, c: (jnp.maximum(i * (tm // HALO) - 1, 0), 0)),
                  vec, shard(d, 0), shard(d, 1), shard(8, 0), shard(1, 0), shard(8, 1), shard(1, 1),
                  pl.BlockSpec((FF_SHARD, d), lambda i, c: (c, 0))] + [vec] * n_gains + [row] * with_loss,
        out_specs=[row, row, pair, pair] + [row] * n_gains + [row, lanes] * with_loss,
        out_shape=[jax.ShapeDtypeStruct((t, d), F32), jax.ShapeDtypeStruct((t, d), BF16),
                   jax.ShapeDtypeStruct((2, nc, t, FF_SHARD), BF16), jax.ShapeDtypeStruct((2, nc, t, FF_SHARD), F32)]
        + [jax.ShapeDtypeStruct((t, d), BF16)] * n_gains
        + [jax.ShapeDtypeStruct((t, d), F32), jax.ShapeDtypeStruct((8, LANES), F32)] * with_loss,
        scratch=[pltpu.VMEM((tm + HALO, d), BF16)], name=f"ffn{layer}_fwd", sem=("arbitrary", "arbitrary"), carry=carry)
    return (outs[0], outs[1], outs[2].reshape(N_SHARDS, t, FF_SHARD), outs[3]) + tuple(outs[4:])


def _ffn_bwd_act(pre, w_out, dxn, layer, tm=512, carry=None):
    t, d = dxn.shape
    nc = N_SHARDS // 2

    def body(pre_ref, wout_ref, dx_ref, dhu_ref, dw_ref, dcb_ref):
        i = pl.program_id(1)

        @pl.when(i == 0)
        def _():
            dw_ref[...] = jnp.zeros_like(dw_ref)
            dcb_ref[...] = jnp.zeros_like(dcb_ref)

        hg, hu = pre_ref[0], pre_ref[1]
        sg = _sigmoid(hg)
        sl = hg * sg
        dxb = dx_ref[...].astype(BF16)
        dact = _dot(dxb, wout_ref[...], NT)
        dw_ref[...] += _dot((sl * hu).astype(BF16), dxb, TN)
        d_up = dact * sl
        d_gate = dact * hu * (sg * (1.0 + hg * (1.0 - sg)))
        for j, dv in enumerate((d_gate, d_up)):
            dhu_ref[j] = dv.astype(BF16)
            dcb_ref[j, 0:1, :] += jnp.sum(dv, axis=0, keepdims=True)

    return _call(
        body, [pre, w_out, dxn], grid=(nc, t // tm),
        in_specs=[pl.BlockSpec((2, None, tm, FF_SHARD), lambda c, i: (0, c, i, 0)),
                  pl.BlockSpec((FF_SHARD, d), lambda c, i: (c, 0)), pl.BlockSpec((tm, d), lambda c, i: (i, 0))],
        out_specs=[pl.BlockSpec((None, 2, tm, FF_SHARD), lambda c, i: (c, 0, i, 0)),
                   pl.BlockSpec((FF_SHARD, d), lambda c, i: (c, 0)),
                   pl.BlockSpec((None, 2, 8, FF_SHARD), lambda c, i: (c, 0, 0, 0))],
        out_shape=[jax.ShapeDtypeStruct((nc, 2, t, FF_SHARD), BF16), jax.ShapeDtypeStruct((D_FF, d), F32),
                   jax.ShapeDtypeStruct((nc, 2, 8, FF_SHARD), F32)],
        name=f"ffn{layer}_bwd_act", sem=("parallel", "arbitrary"), carry=carry)


def _ffn_bwd_in(dhu, a, cw, w_in, layer, tm=1024, carry=None, norm=None):
    nc, _, t, _ = dhu.shape
    d = D_MODEL
    tm = min(tm, t)
    last_blk = t // 16 - 1
    n_norm = 0 if norm is None else 3

    def body(dh_ref, nx_ref, a_ref, cw_ref, win_ref, *rest):
        norm_refs, (da_ref, o_ref, dcw_ref), dg_refs = rest[:n_norm], rest[n_norm:n_norm + 3], rest[n_norm + 3:]
        i, s = pl.program_id(0), pl.program_id(1)

        @pl.when(s == 0)
        def _():
            o_ref[...] = jnp.zeros_like(o_ref)

        @pl.when((s == 0) & (i == 0))
        def _():
            dcw_ref[...] = jnp.zeros_like(dcw_ref)

        keep = jnp.where(i == t // tm - 1, 0.0, 1.0)
        cw = cw_ref[...]
        sums = [None] * 3
        for r0 in range(0, tm, ROW_CHUNK):
            rows = slice(r0, r0 + ROW_CHUNK)
            if r0 + ROW_CHUNK == tm:
                win = jnp.concatenate([dh_ref[rows, :].astype(F32), nx_ref[...].astype(F32) * keep], axis=0)
            else:
                win = dh_ref[r0:r0 + ROW_CHUNK + HALO, :].astype(F32)
            n = ROW_CHUNK + HALO
            taps = (pltpu.roll(win, n - 2, 0)[:ROW_CHUNK],
                    pltpu.roll(win, n - 1, 0)[:ROW_CHUNK],
                    win[:ROW_CHUNK])
            da = (cw[0:1, :] * taps[0] + cw[1:2, :] * taps[1] + cw[2:3, :] * taps[2]).astype(BF16)
            da_ref[rows, :] = da
            o_ref[rows, :] += _dot(da, win_ref[...], NT)
            af = a_ref[rows, :].astype(F32)
            parts = [jnp.sum(taps[k] * af, axis=0, keepdims=True) for k in range(3)]
            sums = [p if q is None else q + p for q, p in zip(sums, parts)]
        for k in range(3):
            dcw_ref[pl.ds(s, 1), k:k + 1, :] += sums[k][None]

        if norm is not None:
            x_ref, g_ref, dres_ref = norm_refs
            dg_ref = dg_refs[0]

            @pl.when((s == 0) & (i == 0))
            def _():
                dg_ref[...] = jnp.zeros_like(dg_ref)

            @pl.when(s == N_SHARDS - 1)
            def _():
                xf = x_ref[...]
                r = lax.rsqrt(jnp.mean(xf * xf, axis=-1, keepdims=True) + EPS)
                xhat = xf * r
                dh = o_ref[...]
                dg_ref[0:1, :] += jnp.sum(dh * xhat, axis=0, keepdims=True)
                gy = dh * g_ref[...]
                o_ref[...] = dres_ref[...] + r * (gy - xhat * jnp.mean(gy * xhat, axis=-1, keepdims=True))

    row = pl.BlockSpec((tm, d), lambda i, s: (i, 0))
    norm_args = [] if norm is None else list(norm)
    norm_specs = [] if norm is None else [row, pl.BlockSpec((1, d), lambda i, s: (0, 0)), row]
    return _call(
        body, [dhu, dhu, a, cw, w_in] + norm_args, grid=(t // tm, N_SHARDS),
        in_specs=[pl.BlockSpec((None, None, tm, FF_SHARD), lambda i, s: (s % nc, s // nc, i, 0)),
                  pl.BlockSpec((None, None, 16, FF_SHARD),
                               lambda i, s: (s % nc, s // nc, jnp.minimum((i + 1) * (tm // 16), last_blk), 0)),
                  pl.BlockSpec((None, tm, FF_SHARD), lambda i, s: (s, i, 0)),
                  pl.BlockSpec((None, 8, FF_SHARD), lambda i, s: (s, 0, 0)),
                  pl.BlockSpec((None, d, FF_SHARD), lambda i, s: (s, 0, 0))] + norm_specs,
        out_specs=[pl.BlockSpec((None, tm, FF_SHARD), lambda i, s: (s, i, 0)), row,
                   pl.BlockSpec((N_SHARDS, 8, FF_SHARD), lambda i, s: (0, 0, 0))]
        + ([] if norm is None else [pl.BlockSpec((8, d), lambda i, s: (0, 0))]),
        out_shape=[jax.ShapeDtypeStruct((N_SHARDS, t, FF_SHARD), BF16), jax.ShapeDtypeStruct((t, d), F32),
                   jax.ShapeDtypeStruct((N_SHARDS, 8, FF_SHARD), F32)]
        + ([] if norm is None else [jax.ShapeDtypeStruct((8, d), F32)]),
        name=f"ffn{layer}_bwd_in", sem=("arbitrary", "arbitrary"), carry=carry)


def _ffn_wgrad_in(hf, da, layer, carry=None):
    t, d = hf.shape
    return _mm(
        da, hf, pl.BlockSpec((None, t, FF_SHARD), lambda s, j, kk: (s, 0, 0)),
        pl.BlockSpec((t, d), lambda s, j, kk: (0, 0)),
        pl.BlockSpec((None, FF_SHARD, d), lambda s, j, kk: (s, 0, 0)),
        jax.ShapeDtypeStruct((N_SHARDS, FF_SHARD, d), F32), (N_SHARDS, 1, 1), TN, f"ffn{layer}_wgrad_in",
        carry=carry)


Q_PER_KV = N_Q_HEADS // N_KV_HEADS
GROUP_ROWS = Q_PER_KV * CHUNK


def _lane_half():
    return lax.broadcasted_iota(jnp.int32, (CHUNK, LANES), 1) < HEAD_DIM


def _fill_attn_bias(bias_s):
    tq = lax.broadcasted_iota(jnp.int32, (GROUP_ROWS, 2 * CHUNK), 0) & (CHUNK - 1)
    jk = lax.broadcasted_iota(jnp.int32, (GROUP_ROWS, 2 * CHUNK), 1)
    dist = tq + CHUNK - jk
    window = (dist >= 0) & (dist < CHUNK)
    distf = dist.astype(F32)
    for kvh in range(N_KV_HEADS):
        alibi = _per_head_column([-SLOPES[h] for h in range(Q_PER_KV * kvh, Q_PER_KV * (kvh + 1))]) * distf
        bias_s[0, kvh] = jnp.where(window & (jk >= CHUNK), alibi, NEG_BIG)
        bias_s[1, kvh] = jnp.where(window, alibi, NEG_BIG)


def _per_head_column(values):
    r = lax.broadcasted_iota(jnp.int32, (GROUP_ROWS, 1), 0)
    col = jnp.full((GROUP_ROWS, 1), values[Q_PER_KV - 1], F32)
    for j in range(Q_PER_KV - 2, -1, -1):
        col = jnp.where(r < (j + 1) * CHUNK, values[j], col)
    return col


def _half_sum(x, lo):
    s_lo = jnp.sum(jnp.where(lo, x, 0.0), axis=-1, keepdims=True)
    s_hi = jnp.sum(jnp.where(lo, 0.0, x), axis=-1, keepdims=True)
    return jnp.where(lo, s_lo, s_hi)


def _stack_heads(pairs, lo):
    zero = jnp.zeros_like(pairs[0])
    return jnp.concatenate([jnp.where(lo, pairs[0], zero), jnp.where(lo, zero, pairs[0]),
                            jnp.where(lo, pairs[1], zero), jnp.where(lo, zero, pairs[1])], axis=0)


def _unstack_heads(stacked, lo):
    return (jnp.where(lo, stacked[0:CHUNK], stacked[CHUNK:2 * CHUNK]),
            jnp.where(lo, stacked[2 * CHUNK:3 * CHUNK], stacked[3 * CHUNK:]))


def _attn_probs(qs, kn, bias, sink_col):
    s = _dot(qs, kn, NT) * (HEAD_DIM ** -0.5) + bias
    m = jnp.maximum(jnp.max(s, axis=-1, keepdims=True), sink_col)
    e = jnp.exp(s - m)
    den = jnp.sum(e, axis=-1, keepdims=True) + jnp.exp(sink_col - m)
    return e * (1.0 / den), m, den


def _attn_fwd(qraw, kvd, gq, gk, sinks, carry=None):
    t, d = qraw.shape
    nb = t // CHUNK

    def body(sink_ref, q_ref, cur_ref, prev_ref, gq_ref, gk_ref, o_ref, bias_s):
        n = pl.program_id(0)

        @pl.when(n == 0)
        def _():
            _fill_attn_bias(bias_s)

        lo = _lane_half()
        which = jnp.where(n == 0, 0, 1)
        gq_v, gk_v = gq_ref[...], gk_ref[...]
        for kvh in range(N_KV_HEADS):
            ks = slice(kvh * LANES, (kvh + 1) * LANES)
            vs = slice(4 * LANES + kvh * LANES, 4 * LANES + (kvh + 1) * LANES)
            kraw = jnp.concatenate([prev_ref[:, ks], cur_ref[:, ks]], axis=0)
            rk = lax.rsqrt(jnp.mean(kraw * kraw, axis=-1, keepdims=True) + EPS)
            kn = (kraw * rk * gk_v).astype(BF16)
            vv = jnp.concatenate([prev_ref[:, vs], cur_ref[:, vs]], axis=0).astype(BF16)
            qn = []
            for p in range(2):
                qp = q_ref[:, (2 * kvh + p) * LANES:(2 * kvh + p + 1) * LANES]
                r = lax.rsqrt(_half_sum(qp * qp, lo) * (1.0 / HEAD_DIM) + EPS)
                qn.append(qp * r * gq_v)
            heads = range(Q_PER_KV * kvh, Q_PER_KV * (kvh + 1))
            pf, _, _ = _attn_probs(_stack_heads(qn, lo).astype(BF16), kn, bias_s[which, kvh],
                                   _per_head_column([sink_ref[h] for h in heads]))
            for p, o_pair in enumerate(_unstack_heads(_dot(pf.astype(BF16), vv), lo)):
                o_ref[:, (2 * kvh + p) * LANES:(2 * kvh + p + 1) * LANES] = o_pair.astype(BF16)

    blk = lambda f: pl.BlockSpec((CHUNK, d), f)
    vec = pl.BlockSpec((1, LANES), lambda n: (0, 0))
    return _call(
        body, [sinks, qraw, kvd, kvd, gq, gk], grid=(nb,),
        in_specs=[pl.BlockSpec(memory_space=pltpu.SMEM), blk(lambda n: (n, 0)), blk(lambda n: (n, 0)),
                  blk(lambda n: (jnp.maximum(n - 1, 0), 0)), vec, vec],
        out_specs=[blk(lambda n: (n, 0))], out_shape=[jax.ShapeDtypeStruct((t, d), BF16)],
        scratch=[pltpu.VMEM((2, N_KV_HEADS, GROUP_ROWS, 2 * CHUNK), F32)], name="attn_fwd", sem=("arbitrary",),
        carry=carry)[0]


def _attn_bwd(qraw, kvd, d_o, gq, gk, sinks, carry=None):
    t, d = qraw.shape
    nb = t // CHUNK

    def body(sink_ref, q_ref, cur_ref, prev_ref, do_ref, gq_ref, gk_ref,
             dq_ref, dkv_ref, dsink_ref, dgq_ref, dgk_ref, carry_s, pp_s, cp_s, bias_s):
        n = pl.program_id(0)

        @pl.when(n == 0)
        def _():
            carry_s[...] = jnp.zeros_like(carry_s)
            dsink_ref[...] = jnp.zeros_like(dsink_ref)
            dgq_ref[...] = jnp.zeros_like(dgq_ref)
            dgk_ref[...] = jnp.zeros_like(dgk_ref)
            _fill_attn_bias(bias_s)

        @pl.when(n < nb)
        def _():
            lo = _lane_half()
            which = jnp.where(n == 0, 0, 1)
            gq_v, gk_v = gq_ref[...], gk_ref[...]
            for kvh in range(N_KV_HEADS):
                ks = slice(kvh * LANES, (kvh + 1) * LANES)
                vs = slice(4 * LANES + kvh * LANES, 4 * LANES + (kvh + 1) * LANES)
                kraw = jnp.concatenate([prev_ref[:, ks], cur_ref[:, ks]], axis=0)
                rk = lax.rsqrt(jnp.mean(kraw * kraw, axis=-1, keepdims=True) + EPS)
                khat = kraw * rk
                kn = (khat * gk_v).astype(BF16)
                vv = jnp.concatenate([prev_ref[:, vs], cur_ref[:, vs]], axis=0).astype(BF16)
                cols = [slice((2 * kvh + p) * LANES, (2 * kvh + p + 1) * LANES) for p in range(2)]
                rq, qhat = [], []
                for p in range(2):
                    qp = q_ref[:, cols[p]]
                    rq.append(lax.rsqrt(_half_sum(qp * qp, lo) * (1.0 / HEAD_DIM) + EPS))
                    qhat.append(qp * rq[p])
                heads = range(Q_PER_KV * kvh, Q_PER_KV * (kvh + 1))
                qs = _stack_heads([qhat[p] * gq_v for p in range(2)], lo).astype(BF16)
                dos = _stack_heads([do_ref[:, cols[p]] for p in range(2)], lo)
                sink_col = _per_head_column([sink_ref[h] for h in heads])
                pf, m, den = _attn_probs(qs, kn, bias_s[which, kvh], sink_col)
                dp = _dot(dos, vv, NT)
                delta = jnp.sum(pf * dp, axis=-1, keepdims=True)
                sink_delta = jnp.exp(sink_col - m) / den * delta
                for j, h in enumerate(heads):
                    dsink_ref[h:h + 1, :] -= jnp.broadcast_to(
                        jnp.sum(sink_delta[j * CHUNK:(j + 1) * CHUNK], axis=0, keepdims=True), (1, LANES))
                ds = (pf * (dp - delta) * (HEAD_DIM ** -0.5)).astype(BF16)
                dkn = _dot(ds, qs, TN)
                dvb = _dot(pf.astype(BF16), dos, TN)
                for p, dqn in enumerate(_unstack_heads(_dot(ds, kn), lo)):
                    dgq_ref[0:1, :] += jnp.sum(dqn * qhat[p], axis=0, keepdims=True)
                    gy = dqn * gq_v
                    mq = _half_sum(gy * qhat[p], lo) * (1.0 / HEAD_DIM)
                    dq_ref[:, cols[p]] = (rq[p] * (gy - qhat[p] * mq)).astype(BF16)
                dgk_ref[0:1, :] += jnp.sum(dkn * khat, axis=0, keepdims=True)
                gyk = dkn * gk_v
                dkraw = rk * (gyk - khat * jnp.mean(gyk * khat, axis=-1, keepdims=True))
                pp_s[:, ks] = dkraw[:CHUNK]
                cp_s[:, ks] = dkraw[CHUNK:]
                pp_s[:, vs] = dvb[:CHUNK]
                cp_s[:, vs] = dvb[CHUNK:]
            dkv_ref[...] = (carry_s[...] + pp_s[...]).astype(BF16)
            carry_s[...] = cp_s[...]

        @pl.when(n == nb)
        def _():
            dkv_ref[...] = carry_s[...].astype(BF16)

    blk = lambda f: pl.BlockSpec((CHUNK, d), f)
    vec = pl.BlockSpec((1, LANES), lambda n: (0, 0))
    cur = lambda n: (jnp.minimum(n, nb - 1), 0)
    prev = lambda n: (jnp.maximum(jnp.minimum(n, nb - 1) - 1, 0), 0)
    small = lambda r: pl.BlockSpec((r, LANES), lambda n: (0, 0))
    return _call(
        body, [sinks, qraw, kvd, kvd, d_o, gq, gk], grid=(nb + 1,),
        in_specs=[pl.BlockSpec(memory_space=pltpu.SMEM), blk(cur), blk(cur), blk(prev), blk(cur), vec, vec],
        out_specs=[blk(cur), blk(lambda n: (jnp.maximum(n - 1, 0), 0)), small(N_Q_HEADS), small(8), small(8)],
        out_shape=[jax.ShapeDtypeStruct((t, d), BF16), jax.ShapeDtypeStruct((t, d), BF16),
                   jax.ShapeDtypeStruct((N_Q_HEADS, LANES), F32), jax.ShapeDtypeStruct((8, LANES), F32),
                   jax.ShapeDtypeStruct((8, LANES), F32)],
        scratch=[pltpu.VMEM((CHUNK, d), F32)] * 3 + [pltpu.VMEM((2, N_KV_HEADS, GROUP_ROWS, 2 * CHUNK), F32)],
        name="attn_bwd", sem=("arbitrary",), carry=carry)


def _adamw_math(g, w, m, v):
    m = ADAM_B1 * m + (1.0 - ADAM_B1) * g
    v = ADAM_B2 * v + (1.0 - ADAM_B2) * (g * g)
    m_hat = m / (1.0 - ADAM_B1 ** ADAM_STEP)
    v_hat = v / (1.0 - ADAM_B2 ** ADAM_STEP)
    delta = -ADAM_LR * (m_hat / (jnp.sqrt(v_hat) + ADAM_EPS) + ADAM_WD * w)
    return delta, m, v


def _row_tile(r, cap=128):
    for tr in range(min(r, cap), 0, -1):
        if r % tr == 0 and (tr % 8 == 0 or tr == r):
            return tr
    return r


def _chip_sum(grad, recv, place, name, wire_dtype):
    _, r, c = grad.shape
    tr = _row_tile(r, 256)

    def body(pl_ref, g_ref, a_ref, p_ref):
        p_ref[...] = (g_ref[...] + a_ref[...]).astype(p_ref.dtype)

    other = lambda rel, pr: pr[0] ^ (rel + 1)
    return pl.pallas_call(
        body,
        grid_spec=pltpu.PrefetchScalarGridSpec(
            num_scalar_prefetch=1, grid=(3, r // tr),
            in_specs=[pl.BlockSpec((None, None, tr, c), lambda rel, i, pr: (other(rel, pr), pr[1], i, 0)),
                      pl.BlockSpec((None, tr, c), lambda rel, i, pr: (other(rel, pr), i, 0))],
            out_specs=pl.BlockSpec((None, tr, c), lambda rel, i, pr: (other(rel, pr), i, 0))),
        out_shape=jax.ShapeDtypeStruct((4, r, c), wire_dtype), name=name, compiler_params=_params(),
    )(place, grad.reshape(4, 2, r, c), recv)


def _adamw_sharded(grad, recv, others, place, w, m, v, name, layer=None, fill=None):
    r, c = w.shape[-2:]
    tr = _row_tile(r)

    def body(pl_ref, g_ref, a_ref, oth_ref, w_ref, m_ref, v_ref, *rest):
        g_out, d_out, nm_out, nv_out = rest[-4:]
        g = g_ref[...] + a_ref[...]
        for k in range(3):
            g = g + oth_ref[k].astype(F32)
        delta, nm, nv = _adamw_math(g, w_ref[...], m_ref[...], v_ref[...])
        g_out[...] = g
        d_out[...] = delta
        nm_out[...] = nm
        nv_out[...] = nv

    if layer is None:
        row = pl.BlockSpec((tr, c), lambda i, pr: (i, 0))
    else:
        row = pl.BlockSpec((None, tr, c), lambda i, pr: (layer, i, 0))
    n_fill = 0 if fill is None else 4
    in_specs = [pl.BlockSpec((None, None, tr, c), lambda i, pr: (pr[0], pr[1], i, 0)),
                pl.BlockSpec((None, tr, c), lambda i, pr: (pr[0], i, 0)),
                pl.BlockSpec((3, tr, c), lambda i, pr: (0, i, 0)), row, row, row]
    in_specs += [pl.BlockSpec(memory_space=pl.ANY)] * n_fill
    return pl.pallas_call(
        body,
        grid_spec=pltpu.PrefetchScalarGridSpec(
            num_scalar_prefetch=1, grid=(r // tr,), in_specs=in_specs, out_specs=[row] * 4),
        out_shape=[jax.ShapeDtypeStruct(w.shape, F32)] * 4, name=name, compiler_params=_params(),
        input_output_aliases={7 + j: j for j in range(n_fill)},
    )(place, grad.reshape(4, 2, r, c), recv, others, w, m, v, *([] if fill is None else fill))


def _sum_devices(parts, name):
    def body(p_ref, o_ref):
        total = p_ref[0]
        for k in range(1, N_SHARDS):
            total = total + p_ref[k]
        o_ref[...] = total

    return pl.pallas_call(body, out_shape=jax.ShapeDtypeStruct(parts.shape[1:], F32), name=name)(parts)


def _adamw_summed(parts, ws, ms, vs, name):
    n = len(parts)

    def body(*refs):
        p_refs, w_refs, m_refs, v_refs = refs[:n], refs[n:2 * n], refs[2 * n:3 * n], refs[3 * n:4 * n]
        o_refs = refs[4 * n:]
        for i in range(n):
            g = p_refs[i][0]
            for k in range(1, N_SHARDS):
                g = g + p_refs[i][k]
            delta, nm, nv = _adamw_math(g, w_refs[i][...], m_refs[i][...], v_refs[i][...])
            o_refs[4 * i][...] = g
            o_refs[4 * i + 1][...] = delta
            o_refs[4 * i + 2][...] = nm
            o_refs[4 * i + 3][...] = nv

    shapes = [jax.ShapeDtypeStruct(w.shape, F32) for w in ws for _ in range(4)]
    outs = pl.pallas_call(body, out_shape=shapes, name=name, compiler_params=_params())(*parts, *ws, *ms, *vs)
    return [outs[4 * i:4 * i + 4] for i in range(n)]


def _dup_heads(w):
    lead = w.shape[:-1]
    w4 = w.reshape(lead + (N_KV_HEADS, 1, HEAD_DIM))
    return jnp.broadcast_to(w4, lead + (N_KV_HEADS, 2, HEAD_DIM)).reshape(lead + (N_KV_HEADS * LANES,))


def _fold_heads(g):
    lead = g.shape[:-1]
    return g.reshape(lead + (N_KV_HEADS, 2, HEAD_DIM)).sum(axis=-2).reshape(lead + (N_KV_HEADS * HEAD_DIM,))


def kernel(x, a_norm, a_w_in, a_v_norm, a_w_s, a_b_s, a_w_out, f_norm, f_w_in, f_conv_w, f_conv_b, f_w_out, kv_norm, w_kv, k_norm, b_norm, b_w_q, b_q_norm, b_sinks, b_w_o, loss_target, m_a_norm, m_a_w_in, m_a_v_norm, m_a_w_s, m_a_b_s, m_a_w_out, m_f_norm, m_f_w_in, m_f_conv_w, m_f_conv_b, m_f_w_out, m_kv_norm, m_w_kv, m_k_norm, m_b_norm, m_b_w_q, m_b_q_norm, m_b_sinks, m_b_w_o, v_a_norm, v_a_w_in, v_a_v_norm, v_a_w_s, v_a_b_s, v_a_w_out, v_f_norm, v_f_w_in, v_f_conv_w, v_f_conv_b, v_f_w_out, v_kv_norm, v_w_kv, v_k_norm, v_b_norm, v_b_w_q, v_b_q_norm, v_b_sinks, v_b_w_o):
    d = D_MODEL
    xi, yi, ci = _coords()
    place = jnp.stack([2 * xi + yi, ci]).astype(jnp.int32)
    bf = lambda a: a.astype(BF16)
    row = lambda v_: v_.reshape(1, -1)
    x0, target = x[0], loss_target[0]
    t = x0.shape[0]
    res = {}

    red = {}

    def to_sibling(grads, wire=BF16):
        for k, g in grads.items():
            red[k] = dict(grad=g, wire=wire)
        ex = _ToSibling(list(grads.values()))
        ex.names = list(grads)
        return ex

    def to_chips(ex):
        for k, a in zip(ex.names, ex.results):
            red[k]["recv"] = a
            red[k]["psum"] = _chip_sum(red[k]["grad"], a, place, f"chip_sum_{k}", red[k]["wire"])
        nxt = _ToChips([red[k]["psum"] for k in ex.names])
        nxt.names = ex.names
        return nxt

    def landed(ex):
        for k, b in zip(ex.names, ex.results):
            red[k]["others"] = b

    def halves(ex, first_rows):
        parts = []
        for r0, nr in ((0, first_rows), (first_rows, ex.srcs[0].shape[1] - first_rows)):
            part = _ToChips(ex.srcs, rows=(r0, nr))
            part.names = ex.names
            parts.append(part)
        return parts

    def landed_halves(parts):
        for j, k in enumerate(parts[0].names):
            red[k]["others"] = jnp.concatenate([p.results[j] for p in parts], axis=1)

    def update(k, w, m, v, layer=None, fill=None):
        r = red[k]
        return _adamw_sharded(r["grad"], r["recv"], r["others"], place, w, m, v,
                              f"adamw_{k}", layer=layer, fill=fill)

    g_a_in, g_a_out, g_a_norm, g_a_v_norm, g_conv = _exchange_alone(
        _Gather([bf(a_w_in[0]), bf(a_w_out[0]), a_norm, a_v_norm, f_conv_w.reshape(6, FF_SHARD)]), "gather_first")
    a_norm_full, a_v_norm_full = g_a_norm.reshape(1, d), g_a_v_norm.reshape(1, d)
    conv_w = lax.reduce_precision(g_conv.reshape(N_SHARDS, 2, 3, FF_SHARD), 8, 7)
    cw = jnp.pad(jnp.transpose(conv_w, (1, 0, 2, 3)), ((0, 0), (0, 0), (0, 5), (0, 0)))
    w_a_in_flat = jnp.transpose(g_a_in, (1, 0, 2)).reshape(d, 2 * d)
    cb = f_conv_b.reshape(2, N_SHARDS, 1, FF_SHARD)
    tri = jnp.tril(jnp.ones((CHUNK, CHUNK), dtype=bool))
    w_causal = jnp.where(tri[None], a_w_s[0], 0.0).astype(BF16)
    w_causal_t = jnp.transpose(w_causal, (0, 2, 1))
    b_sb = jnp.broadcast_to(a_b_s[0][:, :, None], (N_GROUPS, CHUNK, CHUNK))
    w_a_out = g_a_out.reshape(d, d)
    gq = jnp.tile(b_q_norm.reshape(1, HEAD_DIM), (1, 2))
    gk = jnp.tile(k_norm.reshape(1, HEAD_DIM), (1, 2))
    sinks = b_sinks.reshape(N_Q_HEADS)

    ex = _Gather([bf(f_w_in[0]), bf(f_w_out[0])])
    zpre, x1, h1 = _sgu_fwd(x0, a_norm_full, g_a_in, a_v_norm_full, w_causal, b_sb, w_a_out, carry=ex)
    w_in0, w_out0 = ex.results[0], ex.results[1].reshape(D_FF, d)
    ex = _Gather([bf(w_kv), bf(b_w_q[0]), bf(b_w_o[0]), bf(f_w_in[1])], relay=False, early=True)
    x2, hf0, a0, pre0, hk, hq = _ffn_fwd(x1, f_norm[0:1], w_in0, cw[0], cb[0], w_out0, 0, carry=ex,
                                         next_gains=[row(kv_norm), b_norm])
    kv_full = ex.results[0].reshape(d, 2 * N_KV_HEADS * HEAD_DIM)
    w_q, w_o = ex.results[1].reshape(d, d), ex.results[2].reshape(d, d)
    w_in1 = ex.results[3]
    half = N_KV_HEADS * HEAD_DIM
    w_kv_dup = jnp.concatenate([_dup_heads(kv_full[:, :half]), _dup_heads(kv_full[:, half:])], axis=1)
    kvd = _mm_rows(hk, w_kv_dup, F32, "kv_proj")
    qraw = _mm_rows(hq, w_q, F32, "q_proj")
    ex = _Gather([bf(f_w_out[1])], relay=False, early=True)
    o = _attn_fwd(qraw, kvd, gq, gk, sinks, carry=ex)
    w_out1 = ex.results[0].reshape(D_FF, d)
    x3 = _mm_rows(o, w_o, F32, "o_proj", res=x2)
    _, hf1, a1, pre1, dy, loss_lanes = _ffn_fwd(x3, f_norm[1:2], w_in1, cw[1], cb[1], w_out1, 1, loss_target=target)

    dhu1, dw_out1, dcb1 = _ffn_bwd_act(pre1, w_out1, dy, 1)
    ex = to_sibling({"f_w_out1": dw_out1.reshape(N_SHARDS, D_FF // N_SHARDS, d)})
    da1, dx3, dcw1, dgf1 = _ffn_bwd_in(dhu1, a1, cw[1], w_in1, 1, carry=ex, norm=(x3, f_norm[1:2], dy))
    ex = to_chips(ex)
    dw_in1 = _ffn_wgrad_in(hf1, da1, 1, carry=ex)
    landed(ex)
    ex = to_sibling({"f_w_in1": dw_in1})
    d_o = _mm_rows(dx3, w_o, BF16, "o_proj_bwd", trans_w=True, carry=ex)
    ex = to_chips(ex)
    dw_o = _mm_wgrad(o, dx3, "o_wgrad").reshape(N_SHARDS, d // N_SHARDS, d)
    dq, dkv, dsink, dgq, dgk = _attn_bwd(qraw, kvd, d_o, gq, gk, sinks, carry=ex)
    landed(ex)
    dw_q = _mm_wgrad(hq, dq, "q_wgrad").reshape(N_SHARDS, d // N_SHARDS, d)
    dw_kv_dup = _mm_wgrad(hk, dkv, "kv_wgrad")
    dw_kv = jnp.concatenate(
        [_fold_heads(dw_kv_dup[:, :4 * LANES]), _fold_heads(dw_kv_dup[:, 4 * LANES:])], axis=1
    ).reshape(N_SHARDS, d // N_SHARDS, 2 * N_KV_HEADS * HEAD_DIM)
    ex = to_sibling({"b_w_o": dw_o, "b_w_q": dw_q, "w_kv": dw_kv})
    dx2, dg2 = _rms_bwd(x2, [row(kv_norm), b_norm], [dkv, dq], dx3, "kvq_norm_bwd", tm=512, carry=ex,
                        through=[w_kv_dup, w_q])
    ex = to_chips(ex)
    dhu0, dw_out0, dcb0 = _ffn_bwd_act(pre0, w_out0, dx2, 0, carry=ex)
    landed(ex)
    ex = to_sibling({"f_w_out0": dw_out0.reshape(N_SHARDS, D_FF // N_SHARDS, d)})
    da0, dhf0, dcw0 = _ffn_bwd_in(dhu0, a0, cw[0], w_in0, 0, carry=ex)
    ex = to_chips(ex)
    dw_in0 = _ffn_wgrad_in(hf0, da0, 0, carry=ex)
    landed(ex)
    ex = to_sibling({"f_w_in0": dw_in0})
    dx1, dgf0 = _rms_bwd(x1, [f_norm[0:1]], [dhf0], dx2, "f0_norm_bwd", carry=ex)
    ex_lo, ex_hi = halves(to_chips(ex), 448)
    dz, y, dwc, dbs, dgv = _sgu_bwd(dx1, zpre, w_a_out, a_v_norm_full, w_causal, w_causal_t, b_sb, carry=ex_lo)
    dw_a_out = _mm_wgrad(y, dx1, "a_out_wgrad").reshape(N_SHARDS, d // N_SHARDS, d)
    nsub = g_a_in.shape[2]
    dw_a_in = _mm(
        h1, dz, pl.BlockSpec((t, d), lambda s, j, kk: (0, 0)), pl.BlockSpec((t, nsub), lambda s, j, kk: (0, s)),
        pl.BlockSpec((None, d, nsub), lambda s, j, kk: (s, 0, 0)), jax.ShapeDtypeStruct((N_SHARDS, d, nsub), F32),
        (N_SHARDS, 1, 1), TN, "a_in_wgrad", carry=ex_hi)
    landed_halves([ex_lo, ex_hi])

    def bias_grad(dcb):
        return jnp.transpose(dcb[:, :, 0, :], (1, 0, 2)).reshape(-1)

    g_conv_w = jnp.concatenate([dcw0[:, 0:3, :], dcw1[:, 0:3, :]], axis=1)
    g_a_v_norm = dgv[0].reshape(N_SHARDS, 1, LANES)
    rep = ["a_w_s", "a_b_s", "f_norm", "f_conv_b", "kv_norm", "k_norm", "b_norm", "b_q_norm", "b_sinks"]
    rep_g = dict(
        a_w_s=dwc.reshape(N_GROUPS * CHUNK, CHUNK), a_b_s=dbs[:, :, 0], f_norm=jnp.stack([dgf0[0], dgf1[0]]),
        f_conv_b=jnp.stack([bias_grad(dcb0), bias_grad(dcb1)]), kv_norm=dg2[0:1],
        k_norm=(dgk[0, :HEAD_DIM] + dgk[0, HEAD_DIM:])[None], b_norm=dg2[1:2],
        b_q_norm=(dgq[0, :HEAD_DIM] + dgq[0, HEAD_DIM:])[None], b_sinks=dsink[:, 0][None])
    ex_big = to_sibling({"a_w_out": dw_a_out, "a_w_in": dw_a_in})
    ex_small = to_sibling({"a_v_norm": g_a_v_norm, "f_conv_w": g_conv_w}, wire=F32)
    ex_rep = _Gather([rep_g[k] for k in rep] + [loss_lanes], relay=False)
    together = _Together([ex_big, ex_small, ex_rep])
    dh1 = _mm_rows(dz, w_a_in_flat, F32, "a_in_bwd", trans_w=True, carry=together)
    together.spread()
    ex_big, ex_small = to_chips(ex_big), to_chips(ex_small)
    together = _Together([ex_big, ex_small])
    grad_x, dg0 = _rms_bwd(x0, [a_norm_full], [dh1], dx1, "a_norm_bwd", carry=together)
    together.spread()
    landed(ex_big)
    landed(ex_small)
    (a_norm_parts,) = _exchange_alone(_ToOwners([dg0[0].reshape(N_SHARDS, 1, LANES)]), "a_norm_to_owners")

    res["f_w_out"] = update("f_w_out1", f_w_out, m_f_w_out, v_f_w_out, layer=1)
    w_in_t = [jnp.swapaxes(a_, 1, 2) for a_ in (f_w_in, m_f_w_in, v_f_w_in)]
    res["f_w_in"] = update("f_w_in1", *w_in_t, layer=1)
    res["b_w_o"] = update("b_w_o", b_w_o, m_b_w_o, v_b_w_o, layer=0)
    res["b_w_q"] = update("b_w_q", b_w_q, m_b_w_q, v_b_w_q, layer=0)
    res["w_kv"] = update("w_kv", w_kv, m_w_kv, v_w_kv)
    res["f_w_out"] = update("f_w_out0", f_w_out, m_f_w_out, v_f_w_out, layer=0, fill=res["f_w_out"])
    res["f_w_in"] = [jnp.swapaxes(o_, 1, 2) for o_ in update("f_w_in0", *w_in_t, layer=0, fill=res["f_w_in"])]
    res["a_w_out"] = update("a_w_out", a_w_out, m_a_w_out, v_a_w_out, layer=0)
    res["a_w_in"] = update("a_w_in", a_w_in, m_a_w_in, v_a_w_in, layer=0)
    res["a_v_norm"] = update("a_v_norm", a_v_norm, m_a_v_norm, v_a_v_norm)
    res["f_conv_w"] = [o_.reshape(f_conv_w.shape) for o_ in update(
        "f_conv_w", f_conv_w.reshape(6, FF_SHARD), m_f_conv_w.reshape(6, FF_SHARD), v_f_conv_w.reshape(6, FF_SHARD))]

    rep_w = dict(a_w_s=a_w_s, a_b_s=a_b_s, f_norm=f_norm, f_conv_b=f_conv_b, kv_norm=kv_norm, k_norm=k_norm,
                 b_norm=b_norm, b_q_norm=b_q_norm, b_sinks=b_sinks, a_norm=a_norm)
    rep_m = dict(a_w_s=m_a_w_s, a_b_s=m_a_b_s, f_norm=m_f_norm, f_conv_b=m_f_conv_b, kv_norm=m_kv_norm,
                 k_norm=m_k_norm, b_norm=m_b_norm, b_q_norm=m_b_q_norm, b_sinks=m_b_sinks, a_norm=m_a_norm)
    rep_v = dict(a_w_s=v_a_w_s, a_b_s=v_a_b_s, f_norm=v_f_norm, f_conv_b=v_f_conv_b, kv_norm=v_kv_norm,
                 k_norm=v_k_norm, b_norm=v_b_norm, b_q_norm=v_b_q_norm, b_sinks=v_b_sinks, a_norm=v_a_norm)
    keys = rep + ["a_norm"]
    loss = _sum_devices(ex_rep.results[-1], "loss_sum")[0, 0]
    parts = ex_rep.results[:-1] + [a_norm_parts]
    as2d = lambda a, p: a.reshape(p.shape[1:])
    rep_outs = _adamw_summed(parts, [as2d(rep_w[k], p) for k, p in zip(keys, parts)],
                             [as2d(rep_m[k], p) for k, p in zip(keys, parts)],
                             [as2d(rep_v[k], p) for k, p in zip(keys, parts)], "adamw_replicated")
    for j, key in enumerate(keys):
        res[key] = [o_.reshape(rep_w[key].shape) for o_ in rep_outs[j]]

    order = ["a_norm", "a_w_in", "a_v_norm", "a_w_s", "a_b_s", "a_w_out", "f_norm", "f_w_in", "f_conv_w", "f_conv_b",
             "f_w_out", "kv_norm", "w_kv", "k_norm", "b_norm", "b_w_q", "b_q_norm", "b_sinks", "b_w_o"]
    outs = [loss, grad_x[None]]
    for j in range(4):
        outs += [res[k][j] for k in order]
    return tuple(outs)
```

```python
import jax
import jax.numpy as jnp
from jax import lax
from jax.experimental import pallas as pl
from jax.experimental.pallas import tpu as pltpu

F32 = jnp.float32
BF16 = jnp.bfloat16
EPS = 1e-6
D_MODEL = 1024
CHUNK = 128
N_GROUPS = 8
N_SHARDS = 8
HEAD_DIM = 64
N_Q_HEADS = 16
N_KV_HEADS = 4
D_FF = 2816
FF_SHARD = 2 * D_FF // N_SHARDS
LANES = 128
NEG_BIG = -1e30
ADAM_LR = 0.001
ADAM_B1 = 0.9
ADAM_B2 = 0.999
ADAM_EPS = 1e-08
ADAM_WD = 0.01
ADAM_STEP = 10
VMEM_LIMIT_BYTES = 56 * 1024 * 1024
MESH = pl.DeviceIdType.MESH

NN = (((1,), (0,)), ((), ()))
NT = (((1,), (1,)), ((), ()))
TN = (((0,), (0,)), ((), ()))
SLOPES = tuple(2.0 ** (-8.0 * (h + 1) / N_Q_HEADS) for h in range(N_Q_HEADS))


def _params(sem=None):
    return pltpu.CompilerParams(dimension_semantics=sem, vmem_limit_bytes=VMEM_LIMIT_BYTES)


def _dot(a, b, dims=NN):
    return lax.dot_general(a, b, dims, preferred_element_type=F32)


def _sigmoid(x):
    return 1.0 / (1.0 + jnp.exp(-x))


def _gelu_parts(z):
    cdf = 0.5 * (1.0 + lax.erf(z * (2.0 ** -0.5)))
    pdf = jnp.exp(-0.5 * z * z) * 0.3989422804014327
    return cdf, pdf


def _coords():
    return lax.axis_index("x"), lax.axis_index("y"), lax.axis_index("c")


class _Gather:
    def __init__(self, srcs, relay=True, early=False):
        self.srcs = list(srcs)
        self.early = early
        n = len(self.srcs)
        self.relayed = [relay and s.shape[0] % 32 == 0 for s in self.srcs]
        self.out_shapes = [jax.ShapeDtypeStruct((N_SHARDS,) + s.shape, s.dtype) for s in self.srcs]
        self.sems = [pltpu.SemaphoreType.DMA((n, 9)), pltpu.SemaphoreType.DMA((n, 9)), pltpu.SemaphoreType.DMA((n,))]

    def _plan(self, src, dst, sems):
        send_sems, recv_sems, local_sems = sems
        x, y, c = _coords()
        n = len(src)

        def rows(e, dev, half=None):
            block = dst[e].at[4 * dev[0] + 2 * dev[1] + dev[2]]
            if half is None:
                return block
            nr = self.srcs[e].shape[0] // 2
            return block.at[pl.ds(half * nr, nr)]

        def copy(e, slot, block, to, half=None, from_own=False):
            return pltpu.make_async_remote_copy(
                src_ref=src[e] if from_own else rows(e, block, half), dst_ref=rows(e, block, half),
                send_sem=send_sems.at[e, slot], recv_sem=recv_sems.at[e, slot], device_id=to, device_id_type=MESH)

        return n, x, y, c, rows, copy, local_sems

    def start(self, src, dst, sems):
        n, x, y, c, rows, copy, local_sems = self._plan(src, dst, sems)
        me = (x, y, c)
        for e in range(n):
            pltpu.make_async_copy(src[e], rows(e, me), local_sems.at[e]).start()
            copy(e, 0, me, (x, y, 1 - c), from_own=True).start()
            copy(e, 1, me, (1 - x, y, c), from_own=True).start()
            copy(e, 2, me, (x, 1 - y, c), from_own=True).start()
            if not self.relayed[e]:
                copy(e, 3, me, (1 - x, 1 - y, c), from_own=True).start()

    def pass_on(self, src, dst, sems, wait=True):
        n, x, y, c, rows, copy, local_sems = self._plan(src, dst, sems)
        me, sibling = (x, y, c), (x, y, 1 - c)
        over_x, over_y, diagonal = (1 - x, y, c), (x, 1 - y, c), (1 - x, 1 - y, c)
        sent = []

        def arrived(cp):
            if wait:
                cp.wait_recv()

        def send(cp):
            if wait:
                cp.start()
            sent.append(cp)

        for slot, owner, onward, half in ((1, over_x, over_y, 0), (2, over_y, over_x, 1)):
            for e in range(n):
                arrived(copy(e, slot, owner, me))
                if self.relayed[e]:
                    send(copy(e, 3 + half, owner, onward, half=half))
                send(copy(e, 4 + slot, owner, sibling))
        for e in range(n):
            if self.relayed[e]:
                for half in (0, 1):
                    arrived(copy(e, 3 + half, diagonal, me, half=half))
                    send(copy(e, 7 + half, diagonal, sibling, half=half))
            else:
                arrived(copy(e, 3, diagonal, me))
                send(copy(e, 7, diagonal, sibling))
        return sent

    def finish(self, src, dst, sems, passed_on=False):
        n, x, y, c, rows, copy, local_sems = self._plan(src, dst, sems)
        me, sibling = (x, y, c), (x, y, 1 - c)
        over_x, over_y, diagonal = (1 - x, y, c), (x, 1 - y, c), (1 - x, 1 - y, c)
        sent = self.pass_on(src, dst, sems, wait=not passed_on)
        for e in range(n):
            copy(e, 0, sibling, me).wait_recv()
            copy(e, 5, (1 - x, y, 1 - c), me).wait_recv()
            copy(e, 6, (x, 1 - y, 1 - c), me).wait_recv()
            if self.relayed[e]:
                for half in (0, 1):
                    copy(e, 7 + half, (1 - x, 1 - y, 1 - c), me, half=half).wait_recv()
            else:
                copy(e, 7, (1 - x, 1 - y, 1 - c), me).wait_recv()
        for e in range(n):
            copy(e, 0, me, sibling, from_own=True).wait_send()
            copy(e, 1, me, over_x, from_own=True).wait_send()
            copy(e, 2, me, over_y, from_own=True).wait_send()
            if not self.relayed[e]:
                copy(e, 3, me, diagonal, from_own=True).wait_send()
            pltpu.make_async_copy(src[e], rows(e, me), local_sems.at[e]).wait()
        for cp in sent:
            cp.wait_send()


class _ToSibling:
    def __init__(self, grads):
        self.srcs = list(grads)
        n = len(self.srcs)
        self.out_shapes = [jax.ShapeDtypeStruct((4,) + g.shape[1:], g.dtype) for g in self.srcs]
        self.sems = [pltpu.SemaphoreType.DMA((n, 4)), pltpu.SemaphoreType.DMA((n, 4))]

    def _copies(self, src, dst, sems):
        send_sems, recv_sems = sems
        x, y, c = _coords()
        return [
            pltpu.make_async_remote_copy(
                src_ref=src[i].at[2 * q + (1 - c)], dst_ref=dst[i].at[q], send_sem=send_sems.at[i, q],
                recv_sem=recv_sems.at[i, q], device_id=(x, y, 1 - c), device_id_type=MESH)
            for i in range(len(src)) for q in range(4)]

    def start(self, src, dst, sems):
        for cp in self._copies(src, dst, sems):
            cp.start()

    def finish(self, src, dst, sems):
        for cp in self._copies(src, dst, sems):
            cp.wait()


class _ToChips:
    def __init__(self, psums, rows=None):
        self.srcs = list(psums)
        n = len(self.srcs)
        self.rows = rows
        self.out_shapes = [
            jax.ShapeDtypeStruct((3, p.shape[1] if rows is None else rows[1]) + p.shape[2:], p.dtype)
            for p in self.srcs]
        self.sems = [pltpu.SemaphoreType.DMA((n, 3)), pltpu.SemaphoreType.DMA((n, 3))]

    def _copies(self, src, dst, sems):
        send_sems, recv_sems = sems
        x, y, c = _coords()
        peers = [(x, 1 - y), (1 - x, y), (1 - x, 1 - y)]

        def part(i, q):
            if self.rows is None:
                return src[i].at[q]
            return src[i].at[q, pl.ds(self.rows[0], self.rows[1])]

        return [
            pltpu.make_async_remote_copy(
                src_ref=part(i, 2 * px + py), dst_ref=dst[i].at[r], send_sem=send_sems.at[i, r],
                recv_sem=recv_sems.at[i, r], device_id=(px, py, c), device_id_type=MESH)
            for i in range(len(src)) for r, (px, py) in enumerate(peers)]

    def start(self, src, dst, sems):
        for cp in self._copies(src, dst, sems):
            cp.start()

    def finish(self, src, dst, sems):
        for cp in self._copies(src, dst, sems):
            cp.wait()


class _ToOwners:
    def __init__(self, grads):
        self.srcs = list(grads)
        n = len(self.srcs)
        self.out_shapes = [jax.ShapeDtypeStruct(g.shape, g.dtype) for g in self.srcs]
        self.sems = [pltpu.SemaphoreType.DMA((n, 7)), pltpu.SemaphoreType.DMA((n, 7)), pltpu.SemaphoreType.DMA((n,))]

    def _copies(self, src, dst, sems):
        send_sems, recv_sems, local_sems = sems
        x, y, c = _coords()
        me = 4 * x + 2 * y + c
        copies = [pltpu.make_async_copy(src[i].at[me], dst[i].at[me], local_sems.at[i]) for i in range(len(src))]
        for i in range(len(src)):
            for rel in range(1, N_SHARDS):
                px = x ^ (rel >> 2) if rel >> 2 else x
                py = y ^ ((rel >> 1) & 1) if (rel >> 1) & 1 else y
                pc = c ^ (rel & 1) if rel & 1 else c
                copies.append(pltpu.make_async_remote_copy(
                    src_ref=src[i].at[4 * px + 2 * py + pc], dst_ref=dst[i].at[me], send_sem=send_sems.at[i, rel - 1],
                    recv_sem=recv_sems.at[i, rel - 1], device_id=(px, py, pc), device_id_type=MESH))
        return copies

    def start(self, src, dst, sems):
        for cp in self._copies(src, dst, sems):
            cp.start()

    def finish(self, src, dst, sems):
        for cp in self._copies(src, dst, sems):
            cp.wait()


class _Together:
    def __init__(self, parts):
        self.parts = list(parts)
        self.srcs = [s for p in self.parts for s in p.srcs]
        self.out_shapes = [s for p in self.parts for s in p.out_shapes]
        self.sems = [s for p in self.parts for s in p.sems]

    def _split(self, src, dst, sems):
        a = b = c = 0
        for p in self.parts:
            na, nc = len(p.srcs), len(p.sems)
            yield p, src[a:a + na], dst[b:b + na], sems[c:c + nc]
            a, b, c = a + na, b + na, c + nc

    def start(self, src, dst, sems):
        for p, s, d, m in self._split(src, dst, sems):
            p.start(s, d, m)

    def finish(self, src, dst, sems):
        for p, s, d, m in self._split(src, dst, sems):
            p.finish(s, d, m)

    def spread(self):
        b = 0
        for p in self.parts:
            p.results = self.results[b:b + len(p.srcs)]
            b += len(p.srcs)


def _call(body, args, *, grid, in_specs, out_specs, out_shape, name, scratch=(), sem=None, carry=None):
    out_shape, out_specs = list(out_shape), list(out_specs)
    if carry is None:
        return pl.pallas_call(
            body, grid=grid, in_specs=list(in_specs), out_specs=out_specs, out_shape=out_shape,
            scratch_shapes=list(scratch), name=name, compiler_params=_params(sem))(*args)
    n_in, n_out, n_scr, n_c = len(args), len(out_shape), len(scratch), len(carry.srcs)
    steps = tuple(grid)
    total = 1
    for n_ax in steps:
        total *= n_ax
    early = getattr(carry, "early", False) and total >= 8
    early_step = total - max(2, total // 8)

    def carried(*refs):
        ins, rest = refs[:n_in], refs[n_in:]
        c_src, rest = rest[:n_c], rest[n_c:]
        outs, rest = rest[:n_out], rest[n_out:]
        c_dst, rest = rest[:n_c], rest[n_c:]
        scr, sems = rest[:n_scr], rest[n_scr:]
        step = pl.program_id(0)
        for ax in range(1, len(steps)):
            step = step * steps[ax] + pl.program_id(ax)

        @pl.when(step == 0)
        def _():
            carry.start(c_src, c_dst, sems)

        body(*ins, *outs, *scr)

        if early:
            @pl.when(step == early_step)
            def _():
                carry.pass_on(c_src, c_dst, sems)

        @pl.when(step == total - 1)
        def _():
            if early:
                carry.finish(c_src, c_dst, sems, passed_on=True)
            else:
                carry.finish(c_src, c_dst, sems)

    hbm = pl.BlockSpec(memory_space=pl.ANY)
    res = pl.pallas_call(
        carried, grid=grid, in_specs=list(in_specs) + [hbm] * n_c, out_specs=out_specs + [hbm] * n_c,
        out_shape=out_shape + carry.out_shapes, scratch_shapes=list(scratch) + carry.sems, name=name,
        compiler_params=_params(("arbitrary",) * len(steps)))(*args, *carry.srcs)
    carry.results = list(res[n_out:])
    return list(res[:n_out])


def _exchange_alone(ex, name):
    n = len(ex.srcs)

    def body(*refs):
        src, dst, sems = refs[:n], refs[n:2 * n], refs[2 * n:]
        ex.start(src, dst, sems)
        ex.finish(src, dst, sems)

    hbm = pl.BlockSpec(memory_space=pl.ANY)
    res = pl.pallas_call(body, in_specs=[hbm] * n, out_specs=[hbm] * n, out_shape=ex.out_shapes,
                         scratch_shapes=ex.sems, name=name)(*ex.srcs)
    ex.results = list(res)
    return ex.results


def _rms_bwd(x, gains, dhs, dres, name, tm=256, carry=None, through=None):
    t, d = x.shape
    n = len(gains)
    n_w = 0 if through is None else n

    def body(*refs):
        x_ref, dres_ref = refs[0], refs[1]
        g_refs, dh_refs, w_refs = refs[2:2 + n], refs[2 + n:2 + 2 * n], refs[2 + 2 * n:2 + 2 * n + n_w]
        dx_ref, dg_ref = refs[2 + 2 * n + n_w], refs[3 + 2 * n + n_w]
        i = pl.program_id(0)

        @pl.when(i == 0)
        def _():
            dg_ref[...] = jnp.zeros_like(dg_ref)

        xf = x_ref[...]
        r = lax.rsqrt(jnp.mean(xf * xf, axis=-1, keepdims=True) + EPS)
        xhat = xf * r
        dx = dres_ref[...]
        for j in range(n):
            dh = dh_refs[j][...]
            if n_w:
                dh = _dot(dh.astype(BF16), w_refs[j][...], NT)
            dg_ref[j:j + 1, :] += jnp.sum(dh * xhat, axis=0, keepdims=True)
            gy = dh * g_refs[j][...]
            dx = dx + r * (gy - xhat * jnp.mean(gy * xhat, axis=-1, keepdims=True))
        dx_ref[...] = dx

    row = pl.BlockSpec((tm, d), lambda i: (i, 0))
    vec = pl.BlockSpec((1, d), lambda i: (0, 0))
    dh_rows = [pl.BlockSpec((tm, dh.shape[1]), lambda i: (i, 0)) for dh in dhs]
    w_full = [] if through is None else [pl.BlockSpec(w.shape, lambda i: (0, 0)) for w in through]
    return _call(body, [x, dres, *gains, *dhs, *(through or [])], grid=(t // tm,),
                 in_specs=[row, row] + [vec] * n + dh_rows + w_full,
                 out_specs=[row, pl.BlockSpec((8, d), lambda i: (0, 0))],
                 out_shape=[jax.ShapeDtypeStruct((t, d), F32), jax.ShapeDtypeStruct((8, d), F32)],
                 name=name, sem=("arbitrary",), carry=carry)


def _mm(a, b, a_spec, b_spec, o_spec, out_shape, grid, dims, name, res=None, res_spec=None, carry=None):
    nk = grid[2]
    acc_shape = tuple(s for s in o_spec.block_shape if s is not None)

    def body(*refs):
        a_ref, b_ref = refs[0], refs[1]
        r_ref = refs[2] if res is not None else None
        o_ref = refs[3] if res is not None else refs[2]
        p = _dot(a_ref[...].astype(BF16), b_ref[...].astype(BF16), dims)
        if nk == 1:
            if res is not None:
                p = p + r_ref[...]
            o_ref[...] = p.astype(o_ref.dtype)
            return
        acc_ref = refs[-1]
        k = pl.program_id(2)

        @pl.when(k == 0)
        def _():
            acc_ref[...] = p

        @pl.when(k > 0)
        def _():
            acc_ref[...] += p

        @pl.when(k == nk - 1)
        def _():
            out = acc_ref[...]
            if res is not None:
                out = out + r_ref[...]
            o_ref[...] = out.astype(o_ref.dtype)

    ins = [a, b] + ([res] if res is not None else [])
    specs = [a_spec, b_spec] + ([res_spec] if res is not None else [])
    return _call(body, ins, grid=grid, in_specs=specs, out_specs=[o_spec], out_shape=[out_shape],
                 scratch=[pltpu.VMEM(acc_shape, F32)] if nk > 1 else [], name=name,
                 sem=("parallel", "parallel", "arbitrary"), carry=carry)[0]


def _mm_rows(a, w, out_dtype, name, trans_w=False, res=None, tm=1024, carry=None):
    t, k = a.shape
    tm = min(tm, t)
    n = w.shape[0] if trans_w else w.shape[1]
    return _mm(
        a, w, pl.BlockSpec((tm, k), lambda i, j, kk: (i, 0)), pl.BlockSpec(w.shape, lambda i, j, kk: (0, 0)),
        pl.BlockSpec((tm, n), lambda i, j, kk: (i, 0)), jax.ShapeDtypeStruct((t, n), out_dtype), (t // tm, 1, 1),
        NT if trans_w else NN, name, res=res,
        res_spec=None if res is None else pl.BlockSpec((tm, n), lambda i, j, kk: (i, 0)), carry=carry)


def _mm_wgrad(a, b, name, carry=None):
    t, m = a.shape
    n = b.shape[1]
    tn = n // (4 if b.dtype == F32 else 2)
    return _mm(
        a, b, pl.BlockSpec((t, m), lambda i, j, kk: (0, 0)), pl.BlockSpec((t, tn), lambda i, j, kk: (0, j)),
        pl.BlockSpec((m, tn), lambda i, j, kk: (0, j)), jax.ShapeDtypeStruct((m, n), F32), (1, n // tn, 1), TN, name,
        carry=carry)


def _sgu_fwd(x0, g, w_in, g_v, w_c, b_sb, w_out, tm=256, carry=None):
    t, d = x0.shape
    nsub = w_in.shape[2]

    def body(x_ref, g_ref, win_ref, gv_ref, wc_ref, bsb_ref, wout_ref, zpre_ref, x1_ref, h_ref, u_s, v_s, vn_s, y_s):
        xf = x_ref[...]
        h = (xf * lax.rsqrt(jnp.mean(xf * xf, axis=-1, keepdims=True) + EPS) * g_ref[...]).astype(BF16)
        h_ref[...] = h
        for k in range(N_SHARDS):
            zk = _dot(h, win_ref[k])
            zpre_ref[:, k * nsub:(k + 1) * nsub] = zk
            cdf, _ = _gelu_parts(zk)
            if k < N_SHARDS // 2:
                u_s[:, k * nsub:(k + 1) * nsub] = zk * cdf
            else:
                v_s[:, (k - 4) * nsub:(k - 3) * nsub] = zk * cdf
        v = v_s[...]
        rv = lax.rsqrt(jnp.mean(v * v, axis=-1, keepdims=True) + EPS)
        vn_s[...] = (v * rv * gv_ref[...]).astype(BF16)
        for ci in range(tm // CHUNK):
            rows = slice(ci * CHUNK, (ci + 1) * CHUNK)
            for g in range(N_GROUPS):
                cols = slice(g * LANES, (g + 1) * LANES)
                sv = _dot(wc_ref[g], vn_s[rows, cols]) + bsb_ref[g]
                y_s[rows, cols] = (u_s[rows, cols] * sv).astype(BF16)
        x1_ref[...] = x_ref[...] + _dot(y_s[...], wout_ref[...])

    row = pl.BlockSpec((tm, d), lambda i: (i, 0))
    full = lambda a: pl.BlockSpec(a.shape, lambda i: (0,) * a.ndim)
    return _call(
        body, [x0, g, w_in, g_v, w_c, b_sb, w_out], grid=(t // tm,),
        in_specs=[row, full(g), full(w_in), full(g_v), full(w_c), full(b_sb), full(w_out)],
        out_specs=[pl.BlockSpec((tm, 2 * d), lambda i: (i, 0)), row, row],
        out_shape=[jax.ShapeDtypeStruct((t, 2 * d), F32), jax.ShapeDtypeStruct((t, d), F32),
                   jax.ShapeDtypeStruct((t, d), BF16)],
        scratch=[pltpu.VMEM((tm, d), F32), pltpu.VMEM((tm, d), F32), pltpu.VMEM((tm, d), BF16),
                 pltpu.VMEM((tm, d), BF16)],
        name="sgu_fwd", carry=carry)


def _sgu_bwd(dx1, zpre, w_out, g_v, w_c, w_ct, b_sb, tm=256, carry=None):
    t, d = dx1.shape

    def body(dx_ref, zpre_ref, wout_ref, gv_ref, wc_ref, wct_ref, bsb_ref,
             dz_ref, y_ref, dwc_ref, dbs_ref, dgv_ref, u_s, vn_s, dy_s, du_s, dvn_s):
        i = pl.program_id(0)

        @pl.when(i == 0)
        def _():
            dwc_ref[...] = jnp.zeros_like(dwc_ref)
            dbs_ref[...] = jnp.zeros_like(dbs_ref)
            dgv_ref[...] = jnp.zeros_like(dgv_ref)

        dy_s[...] = _dot(dx_ref[...].astype(BF16), wout_ref[...], NT)
        zu = zpre_ref[:, :d]
        zv = zpre_ref[:, d:]
        cdf_u, pdf_u = _gelu_parts(zu)
        cdf_v, pdf_v = _gelu_parts(zv)
        u_s[...] = zu * cdf_u
        v = zv * cdf_v
        rv = lax.rsqrt(jnp.mean(v * v, axis=-1, keepdims=True) + EPS)
        vhat = v * rv
        gv = gv_ref[...]
        vn_s[...] = (vhat * gv).astype(BF16)
        for ci in range(tm // CHUNK):
            rows = slice(ci * CHUNK, (ci + 1) * CHUNK)
            for g in range(N_GROUPS):
                cols = slice(g * LANES, (g + 1) * LANES)
                vnb = vn_s[rows, cols]
                sv = _dot(wc_ref[g], vnb) + bsb_ref[g]
                dyb = dy_s[rows, cols]
                ub = u_s[rows, cols]
                dsv = dyb * ub
                du_s[rows, cols] = dyb * sv
                y_ref[rows, cols] = (ub * sv).astype(BF16)
                dsvb = dsv.astype(BF16)
                dbs_ref[g] += dsv
                dwc_ref[g] += _dot(dsvb, vnb, NT)
                dvn_s[rows, cols] = _dot(wct_ref[g], dsvb)
        dvn = dvn_s[...]
        dgv_ref[0:1, :] += jnp.sum(dvn * vhat, axis=0, keepdims=True)
        gy = dvn * gv
        dv = rv * (gy - vhat * jnp.mean(gy * vhat, axis=-1, keepdims=True))
        dz_ref[:, :d] = (du_s[...] * (cdf_u + zu * pdf_u)).astype(BF16)
        dz_ref[:, d:] = (dv * (cdf_v + zv * pdf_v)).astype(BF16)

        @pl.when(i == t // tm - 1)
        def _():
            tri = (lax.broadcasted_iota(jnp.int32, (CHUNK, CHUNK), 0)
                   >= lax.broadcasted_iota(jnp.int32, (CHUNK, CHUNK), 1))
            for g in range(N_GROUPS):
                dwc_ref[g] = jnp.where(tri, dwc_ref[g], 0.0)
                dbs_ref[g] = jnp.broadcast_to(jnp.sum(dbs_ref[g], axis=1, keepdims=True), (CHUNK, CHUNK))

    row = pl.BlockSpec((tm, d), lambda i: (i, 0))
    row2 = pl.BlockSpec((tm, 2 * d), lambda i: (i, 0))
    full = lambda a: pl.BlockSpec(a.shape, lambda i: (0,) * a.ndim)
    grp = pl.BlockSpec((N_GROUPS, CHUNK, CHUNK), lambda i: (0, 0, 0))
    return _call(
        body, [dx1, zpre, w_out, g_v, w_c, w_ct, b_sb], grid=(t // tm,),
        in_specs=[row, row2, full(w_out), full(g_v), full(w_c), full(w_ct), full(b_sb)],
        out_specs=[row2, row, grp, grp, pl.BlockSpec((8, d), lambda i: (0, 0))],
        out_shape=[jax.ShapeDtypeStruct((t, 2 * d), BF16), jax.ShapeDtypeStruct((t, d), BF16),
                   jax.ShapeDtypeStruct((N_GROUPS, CHUNK, CHUNK), F32),
                   jax.ShapeDtypeStruct((N_GROUPS, CHUNK, CHUNK), F32), jax.ShapeDtypeStruct((8, d), F32)],
        scratch=[pltpu.VMEM((tm, d), F32), pltpu.VMEM((tm, d), BF16), pltpu.VMEM((tm, d), F32),
                 pltpu.VMEM((tm, d), F32), pltpu.VMEM((tm, d), F32)],
        name="sgu_bwd", sem=("arbitrary",), carry=carry)


ROW_CHUNK = 256
HALO = 16


def _ffn_fwd(x, g, w_in, cw, cb, w_out, layer, tm=512, carry=None, next_gains=(), loss_target=None):
    t, d = x.shape
    nc = N_SHARDS // 2
    n_gains = len(next_gains)
    with_loss = loss_target is not None

    def body(x_ref, xp_ref, g_ref, wg_ref, wu_ref, cwg_ref, cbg_ref, cwu_ref, cbu_ref, wout_ref, *rest):
        extra_in, rest = rest[:n_gains + with_loss], rest[n_gains + with_loss:]
        o_ref, hf_ref, a_ref, pre_ref = rest[:4]
        extra_out, hw_s = rest[4:-1], rest[-1]
        i, c = pl.program_id(0), pl.program_id(1)

        @pl.when(c == 0)
        def _():
            keep = jnp.where(i == 0, 0.0, 1.0)
            xw = jnp.concatenate([xp_ref[...] * keep, x_ref[...]], axis=0)
            xhat = xw * lax.rsqrt(jnp.mean(xw * xw, axis=-1, keepdims=True) + EPS)
            hw_s[...] = (xhat * g_ref[...]).astype(BF16)
            hf_ref[...] = hw_s[HALO:, :]
            o_ref[...] = x_ref[...]

        hw = hw_s[...]
        pre = []
        for j, (w_ref, cw_ref, cb_ref) in enumerate(((wg_ref, cwg_ref, cbg_ref), (wu_ref, cwu_ref, cbu_ref))):
            ab = _dot(hw, w_ref[...]).astype(BF16)
            a_ref[j] = ab[HALO:]
            win = ab.astype(F32)
            cw_v = cw_ref[...]
            pre.append(cw_v[2:3, :] * win[HALO:] + cw_v[1:2, :] * pltpu.roll(win, 1, 0)[HALO:]
                       + cw_v[0:1, :] * pltpu.roll(win, 2, 0)[HALO:] + cb_ref[...])
            pre_ref[j] = pre[j]
        act = (pre[0] * _sigmoid(pre[0]) * pre[1]).astype(BF16)
        o_ref[...] += _dot(act, wout_ref[...])

        if with_loss:
            @pl.when((i == 0) & (c == 0))
            def _():
                extra_out[-1][...] = jnp.zeros_like(extra_out[-1])

        @pl.when(c == nc - 1)
        def _():
            xn = o_ref[...]
            if n_gains:
                xhat = xn * lax.rsqrt(jnp.mean(xn * xn, axis=-1, keepdims=True) + EPS)
                for k in range(n_gains):
                    extra_out[k][...] = (xhat * extra_in[k][...]).astype(BF16)
            if with_loss:
                err = xn - extra_in[-1][...]
                extra_out[-2][...] = err * (1.0 / d)
                part = jnp.sum(jnp.sum(err * err, axis=0, keepdims=True), axis=1, keepdims=True)
                extra_out[-1][...] += jnp.broadcast_to(0.5 / d * part, extra_out[-1].shape)

    row = pl.BlockSpec((tm, d), lambda i, c: (i, 0))
    vec = pl.BlockSpec((1, d), lambda i, c: (0, 0))
    shard = lambda rows, up: pl.BlockSpec((None, rows, FF_SHARD), lambda i, c: (c + up * nc, 0, 0))
    pair = pl.BlockSpec((2, None, tm, FF_SHARD), lambda i, c: (0, c, i, 0))
    lanes = pl.BlockSpec((8, LANES), lambda i, c: (0, 0))
    outs = _call(
        body, [x, x, g, w_in, w_in, cw, cb, cw, cb, w_out, *next_gains] + ([loss_target] if with_loss else []),
        grid=(t // tm, nc),
        in_specs=[row, pl.BlockSpec((HALO, d), lambda i, c: (jnp.maximum(i * (tm // HALO) - 1, 0), 0)),
                  vec, shard(d, 0), shard(d, 1), shard(8, 0), shard(1, 0), shard(8, 1), shard(1, 1),
                  pl.BlockSpec((FF_SHARD, d), lambda i, c: (c, 0))] + [vec] * n_gains + [row] * with_loss,
        out_specs=[row, row, pair, pair] + [row] * n_gains + [row, lanes] * with_loss,
        out_shape=[jax.ShapeDtypeStruct((t, d), F32), jax.ShapeDtypeStruct((t, d), BF16),
                   jax.ShapeDtypeStruct((2, nc, t, FF_SHARD), BF16), jax.ShapeDtypeStruct((2, nc, t, FF_SHARD), F32)]
        + [jax.ShapeDtypeStruct((t, d), BF16)] * n_gains
        + [jax.ShapeDtypeStruct((t, d), F32), jax.ShapeDtypeStruct((8, LANES), F32)] * with_loss,
        scratch=[pltpu.VMEM((tm + HALO, d), BF16)], name=f"ffn{layer}_fwd", sem=("arbitrary", "arbitrary"), carry=carry)
    return (outs[0], outs[1], outs[2].reshape(N_SHARDS, t, FF_SHARD), outs[3]) + tuple(outs[4:])


def _ffn_bwd_act(pre, w_out, dxn, layer, tm=512, carry=None):
    t, d = dxn.shape
    nc = N_SHARDS // 2

    def body(pre_ref, wout_ref, dx_ref, dhu_ref, dw_ref, dcb_ref, dxb_s):
        i, c = pl.program_id(0), pl.program_id(1)

        @pl.when((i == 0) & (c == 0))
        def _():
            dw_ref[...] = jnp.zeros_like(dw_ref)
            dcb_ref[...] = jnp.zeros_like(dcb_ref)

        @pl.when(c == 0)
        def _():
            dxb_s[...] = dx_ref[...].astype(BF16)

        hg, hu = pre_ref[0], pre_ref[1]
        sg = _sigmoid(hg)
        sl = hg * sg
        dxb = dxb_s[...]
        dact = _dot(dxb, wout_ref[...], NT)
        dw_ref[pl.ds(pl.multiple_of(c * FF_SHARD, 8), FF_SHARD), :] += _dot((sl * hu).astype(BF16), dxb, TN)
        d_up = dact * sl
        d_gate = dact * hu * (sg * (1.0 + hg * (1.0 - sg)))
        for j, dv in enumerate((d_gate, d_up)):
            dhu_ref[j] = dv.astype(BF16)
            dcb_ref[pl.ds(c, 1), j, 0:1, :] += jnp.sum(dv, axis=0, keepdims=True)[None]

    return _call(
        body, [pre, w_out, dxn], grid=(t // tm, nc),
        in_specs=[pl.BlockSpec((2, None, tm, FF_SHARD), lambda i, c: (0, c, i, 0)),
                  pl.BlockSpec((FF_SHARD, d), lambda i, c: (c, 0)), pl.BlockSpec((tm, d), lambda i, c: (i, 0))],
        out_specs=[pl.BlockSpec((None, 2, tm, FF_SHARD), lambda i, c: (c, 0, i, 0)),
                   pl.BlockSpec((D_FF, d), lambda i, c: (0, 0)),
                   pl.BlockSpec((nc, 2, 8, FF_SHARD), lambda i, c: (0, 0, 0, 0))],
        out_shape=[jax.ShapeDtypeStruct((nc, 2, t, FF_SHARD), BF16), jax.ShapeDtypeStruct((D_FF, d), F32),
                   jax.ShapeDtypeStruct((nc, 2, 8, FF_SHARD), F32)],
        scratch=[pltpu.VMEM((tm, d), BF16)], name=f"ffn{layer}_bwd_act", sem=("arbitrary", "arbitrary"), carry=carry)


def _ffn_bwd_in(dhu, a, cw, w_in, layer, tm=1024, carry=None, norm=None):
    nc, _, t, _ = dhu.shape
    d = D_MODEL
    tm = min(tm, t)
    last_blk = t // 16 - 1
    n_norm = 0 if norm is None else 3

    def body(dh_ref, nx_ref, a_ref, cw_ref, win_ref, *rest):
        norm_refs, (da_ref, o_ref, dcw_ref), dg_refs = rest[:n_norm], rest[n_norm:n_norm + 3], rest[n_norm + 3:]
        i, s = pl.program_id(0), pl.program_id(1)

        @pl.when(s == 0)
        def _():
            o_ref[...] = jnp.zeros_like(o_ref)

        @pl.when((s == 0) & (i == 0))
        def _():
            dcw_ref[...] = jnp.zeros_like(dcw_ref)

        keep = jnp.where(i == t // tm - 1, 0.0, 1.0)
        cw = cw_ref[...]
        sums = [None] * 3
        for r0 in range(0, tm, ROW_CHUNK):
            rows = slice(r0, r0 + ROW_CHUNK)
            if r0 + ROW_CHUNK == tm:
                win = jnp.concatenate([dh_ref[rows, :].astype(F32), nx_ref[...].astype(F32) * keep], axis=0)
            else:
                win = dh_ref[r0:r0 + ROW_CHUNK + HALO, :].astype(F32)
            n = ROW_CHUNK + HALO
            taps = (pltpu.roll(win, n - 2, 0)[:ROW_CHUNK],
                    pltpu.roll(win, n - 1, 0)[:ROW_CHUNK],
                    win[:ROW_CHUNK])
            da = (cw[0:1, :] * taps[0] + cw[1:2, :] * taps[1] + cw[2:3, :] * taps[2]).astype(BF16)
            da_ref[rows, :] = da
            o_ref[rows, :] += _dot(da, win_ref[...], NT)
            af = a_ref[rows, :].astype(F32)
            parts = [jnp.sum(taps[k] * af, axis=0, keepdims=True) for k in range(3)]
            sums = [p if q is None else q + p for q, p in zip(sums, parts)]
        for k in range(3):
            dcw_ref[pl.ds(s, 1), k:k + 1, :] += sums[k][None]

        if norm is not None:
            x_ref, g_ref, dres_ref = norm_refs
            dg_ref = dg_refs[0]

            @pl.when((s == 0) & (i == 0))
            def _():
                dg_ref[...] = jnp.zeros_like(dg_ref)

            @pl.when(s == N_SHARDS - 1)
            def _():
                xf = x_ref[...]
                r = lax.rsqrt(jnp.mean(xf * xf, axis=-1, keepdims=True) + EPS)
                xhat = xf * r
                dh = o_ref[...]
                dg_ref[0:1, :] += jnp.sum(dh * xhat, axis=0, keepdims=True)
                gy = dh * g_ref[...]
                o_ref[...] = dres_ref[...] + r * (gy - xhat * jnp.mean(gy * xhat, axis=-1, keepdims=True))

    row = pl.BlockSpec((tm, d), lambda i, s: (i, 0))
    norm_args = [] if norm is None else list(norm)
    norm_specs = [] if norm is None else [row, pl.BlockSpec((1, d), lambda i, s: (0, 0)), row]
    return _call(
        body, [dhu, dhu, a, cw, w_in] + norm_args, grid=(t // tm, N_SHARDS),
        in_specs=[pl.BlockSpec((None, None, tm, FF_SHARD), lambda i, s: (s % nc, s // nc, i, 0)),
                  pl.BlockSpec((None, None, 16, FF_SHARD),
                               lambda i, s: (s % nc, s // nc, jnp.minimum((i + 1) * (tm // 16), last_blk), 0)),
                  pl.BlockSpec((None, tm, FF_SHARD), lambda i, s: (s, i, 0)),
                  pl.BlockSpec((None, 8, FF_SHARD), lambda i, s: (s, 0, 0)),
                  pl.BlockSpec((None, d, FF_SHARD), lambda i, s: (s, 0, 0))] + norm_specs,
        out_specs=[pl.BlockSpec((None, tm, FF_SHARD), lambda i, s: (s, i, 0)), row,
                   pl.BlockSpec((N_SHARDS, 8, FF_SHARD), lambda i, s: (0, 0, 0))]
        + ([] if norm is None else [pl.BlockSpec((8, d), lambda i, s: (0, 0))]),
        out_shape=[jax.ShapeDtypeStruct((N_SHARDS, t, FF_SHARD), BF16), jax.ShapeDtypeStruct((t, d), F32),
                   jax.ShapeDtypeStruct((N_SHARDS, 8, FF_SHARD), F32)]
        + ([] if norm is None else [jax.ShapeDtypeStruct((8, d), F32)]),
        name=f"ffn{layer}_bwd_in", sem=("arbitrary", "arbitrary"), carry=carry)


def _ffn_wgrad_in(hf, da, layer, carry=None):
    t, d = hf.shape
    return _mm(
        da, hf, pl.BlockSpec((None, t, FF_SHARD), lambda s, j, kk: (s, 0, 0)),
        pl.BlockSpec((t, d), lambda s, j, kk: (0, 0)),
        pl.BlockSpec((None, FF_SHARD, d), lambda s, j, kk: (s, 0, 0)),
        jax.ShapeDtypeStruct((N_SHARDS, FF_SHARD, d), F32), (N_SHARDS, 1, 1), TN, f"ffn{layer}_wgrad_in",
        carry=carry)


Q_PER_KV = N_Q_HEADS // N_KV_HEADS
GROUP_ROWS = Q_PER_KV * CHUNK


def _lane_half():
    return lax.broadcasted_iota(jnp.int32, (CHUNK, LANES), 1) < HEAD_DIM


def _fill_attn_bias(bias_s):
    tq = lax.broadcasted_iota(jnp.int32, (GROUP_ROWS, 2 * CHUNK), 0) & (CHUNK - 1)
    jk = lax.broadcasted_iota(jnp.int32, (GROUP_ROWS, 2 * CHUNK), 1)
    dist = tq + CHUNK - jk
    window = (dist >= 0) & (dist < CHUNK)
    distf = dist.astype(F32)
    for kvh in range(N_KV_HEADS):
        alibi = _per_head_column([-SLOPES[h] for h in range(Q_PER_KV * kvh, Q_PER_KV * (kvh + 1))]) * distf
        bias_s[0, kvh] = jnp.where(window & (jk >= CHUNK), alibi, NEG_BIG)
        bias_s[1, kvh] = jnp.where(window, alibi, NEG_BIG)


def _per_head_column(values):
    r = lax.broadcasted_iota(jnp.int32, (GROUP_ROWS, 1), 0)
    col = jnp.full((GROUP_ROWS, 1), values[Q_PER_KV - 1], F32)
    for j in range(Q_PER_KV - 2, -1, -1):
        col = jnp.where(r < (j + 1) * CHUNK, values[j], col)
    return col


def _half_sum(x, lo):
    s_lo = jnp.sum(jnp.where(lo, x, 0.0), axis=-1, keepdims=True)
    s_hi = jnp.sum(jnp.where(lo, 0.0, x), axis=-1, keepdims=True)
    return jnp.where(lo, s_lo, s_hi)


def _stack_heads(pairs, lo):
    zero = jnp.zeros_like(pairs[0])
    return jnp.concatenate([jnp.where(lo, pairs[0], zero), jnp.where(lo, zero, pairs[0]),
                            jnp.where(lo, pairs[1], zero), jnp.where(lo, zero, pairs[1])], axis=0)


def _unstack_heads(stacked, lo):
    return (jnp.where(lo, stacked[0:CHUNK], stacked[CHUNK:2 * CHUNK]),
            jnp.where(lo, stacked[2 * CHUNK:3 * CHUNK], stacked[3 * CHUNK:]))


def _attn_probs(qs, kn, bias, sink_col):
    s = _dot(qs, kn, NT) * (HEAD_DIM ** -0.5) + bias
    m = jnp.maximum(jnp.max(s, axis=-1, keepdims=True), sink_col)
    e = jnp.exp(s - m)
    den = jnp.sum(e, axis=-1, keepdims=True) + jnp.exp(sink_col - m)
    return e * (1.0 / den), m, den


def _attn_fwd(qraw, kvd, gq, gk, sinks, carry=None):
    t, d = qraw.shape
    nb = t // CHUNK

    def body(sink_ref, q_ref, cur_ref, prev_ref, gq_ref, gk_ref, o_ref, bias_s):
        n = pl.program_id(0)

        @pl.when(n == 0)
        def _():
            _fill_attn_bias(bias_s)

        lo = _lane_half()
        which = jnp.where(n == 0, 0, 1)
        gq_v, gk_v = gq_ref[...], gk_ref[...]
        for kvh in range(N_KV_HEADS):
            ks = slice(kvh * LANES, (kvh + 1) * LANES)
            vs = slice(4 * LANES + kvh * LANES, 4 * LANES + (kvh + 1) * LANES)
            kraw = jnp.concatenate([prev_ref[:, ks], cur_ref[:, ks]], axis=0)
            rk = lax.rsqrt(jnp.mean(kraw * kraw, axis=-1, keepdims=True) + EPS)
            kn = (kraw * rk * gk_v).astype(BF16)
            vv = jnp.concatenate([prev_ref[:, vs], cur_ref[:, vs]], axis=0).astype(BF16)
            qn = []
            for p in range(2):
                qp = q_ref[:, (2 * kvh + p) * LANES:(2 * kvh + p + 1) * LANES]
                r = lax.rsqrt(_half_sum(qp * qp, lo) * (1.0 / HEAD_DIM) + EPS)
                qn.append(qp * r * gq_v)
            heads = range(Q_PER_KV * kvh, Q_PER_KV * (kvh + 1))
            pf, _, _ = _attn_probs(_stack_heads(qn, lo).astype(BF16), kn, bias_s[which, kvh],
                                   _per_head_column([sink_ref[h] for h in heads]))
            for p, o_pair in enumerate(_unstack_heads(_dot(pf.astype(BF16), vv), lo)):
                o_ref[:, (2 * kvh + p) * LANES:(2 * kvh + p + 1) * LANES] = o_pair.astype(BF16)

    blk = lambda f: pl.BlockSpec((CHUNK, d), f)
    vec = pl.BlockSpec((1, LANES), lambda n: (0, 0))
    return _call(
        body, [sinks, qraw, kvd, kvd, gq, gk], grid=(nb,),
        in_specs=[pl.BlockSpec(memory_space=pltpu.SMEM), blk(lambda n: (n, 0)), blk(lambda n: (n, 0)),
                  blk(lambda n: (jnp.maximum(n - 1, 0), 0)), vec, vec],
        out_specs=[blk(lambda n: (n, 0))], out_shape=[jax.ShapeDtypeStruct((t, d), BF16)],
        scratch=[pltpu.VMEM((2, N_KV_HEADS, GROUP_ROWS, 2 * CHUNK), F32)], name="attn_fwd", sem=("arbitrary",),
        carry=carry)[0]


def _attn_bwd(qraw, kvd, d_o, gq, gk, sinks, carry=None):
    t, d = qraw.shape
    nb = t // CHUNK

    def body(sink_ref, q_ref, cur_ref, prev_ref, do_ref, gq_ref, gk_ref,
             dq_ref, dkv_ref, dsink_ref, dgq_ref, dgk_ref, carry_s, pp_s, cp_s, bias_s):
        n = pl.program_id(0)

        @pl.when(n == 0)
        def _():
            carry_s[...] = jnp.zeros_like(carry_s)
            dsink_ref[...] = jnp.zeros_like(dsink_ref)
            dgq_ref[...] = jnp.zeros_like(dgq_ref)
            dgk_ref[...] = jnp.zeros_like(dgk_ref)
            _fill_attn_bias(bias_s)

        @pl.when(n < nb)
        def _():
            lo = _lane_half()
            which = jnp.where(n == 0, 0, 1)
            gq_v, gk_v = gq_ref[...], gk_ref[...]
            for kvh in range(N_KV_HEADS):
                ks = slice(kvh * LANES, (kvh + 1) * LANES)
                vs = slice(4 * LANES + kvh * LANES, 4 * LANES + (kvh + 1) * LANES)
                kraw = jnp.concatenate([prev_ref[:, ks], cur_ref[:, ks]], axis=0)
                rk = lax.rsqrt(jnp.mean(kraw * kraw, axis=-1, keepdims=True) + EPS)
                khat = kraw * rk
                kn = (khat * gk_v).astype(BF16)
                vv = jnp.concatenate([prev_ref[:, vs], cur_ref[:, vs]], axis=0).astype(BF16)
                cols = [slice((2 * kvh + p) * LANES, (2 * kvh + p + 1) * LANES) for p in range(2)]
                rq, qhat = [], []
                for p in range(2):
                    qp = q_ref[:, cols[p]]
                    rq.append(lax.rsqrt(_half_sum(qp * qp, lo) * (1.0 / HEAD_DIM) + EPS))
                    qhat.append(qp * rq[p])
                heads = range(Q_PER_KV * kvh, Q_PER_KV * (kvh + 1))
                qs = _stack_heads([qhat[p] * gq_v for p in range(2)], lo).astype(BF16)
                dos = _stack_heads([do_ref[:, cols[p]] for p in range(2)], lo)
                sink_col = _per_head_column([sink_ref[h] for h in heads])
                pf, m, den = _attn_probs(qs, kn, bias_s[which, kvh], sink_col)
                dp = _dot(dos, vv, NT)
                delta = jnp.sum(pf * dp, axis=-1, keepdims=True)
                sink_delta = jnp.exp(sink_col - m) / den * delta
                for j, h in enumerate(heads):
                    dsink_ref[h:h + 1, :] -= jnp.broadcast_to(
                        jnp.sum(sink_delta[j * CHUNK:(j + 1) * CHUNK], axis=0, keepdims=True), (1, LANES))
                ds = (pf * (dp - delta) * (HEAD_DIM ** -0.5)).astype(BF16)
                dkn = _dot(ds, qs, TN)
                dvb = _dot(pf.astype(BF16), dos, TN)
                for p, dqn in enumerate(_unstack_heads(_dot(ds, kn), lo)):
                    dgq_ref[0:1, :] += jnp.sum(dqn * qhat[p], axis=0, keepdims=True)
                    gy = dqn * gq_v
                    mq = _half_sum(gy * qhat[p], lo) * (1.0 / HEAD_DIM)
                    dq_ref[:, cols[p]] = (rq[p] * (gy - qhat[p] * mq)).astype(BF16)
                dgk_ref[0:1, :] += jnp.sum(dkn * khat, axis=0, keepdims=True)
                gyk = dkn * gk_v
                dkraw = rk * (gyk - khat * jnp.mean(gyk * khat, axis=-1, keepdims=True))
                pp_s[:, ks] = dkraw[:CHUNK]
                cp_s[:, ks] = dkraw[CHUNK:]
                pp_s[:, vs] = dvb[:CHUNK]
                cp_s[:, vs] = dvb[CHUNK:]
            dkv_ref[...] = (carry_s[...] + pp_s[...]).astype(BF16)
            carry_s[...] = cp_s[...]

        @pl.when(n == nb)
        def _():
            dkv_ref[...] = carry_s[...].astype(BF16)

    blk = lambda f: pl.BlockSpec((CHUNK, d), f)
    vec = pl.BlockSpec((1, LANES), lambda n: (0, 0))
    cur = lambda n: (jnp.minimum(n, nb - 1), 0)
    prev = lambda n: (jnp.maximum(jnp.minimum(n, nb - 1) - 1, 0), 0)
    small = lambda r: pl.BlockSpec((r, LANES), lambda n: (0, 0))
    return _call(
        body, [sinks, qraw, kvd, kvd, d_o, gq, gk], grid=(nb + 1,),
        in_specs=[pl.BlockSpec(memory_space=pltpu.SMEM), blk(cur), blk(cur), blk(prev), blk(cur), vec, vec],
        out_specs=[blk(cur), blk(lambda n: (jnp.maximum(n - 1, 0), 0)), small(N_Q_HEADS), small(8), small(8)],
        out_shape=[jax.ShapeDtypeStruct((t, d), BF16), jax.ShapeDtypeStruct((t, d), BF16),
                   jax.ShapeDtypeStruct((N_Q_HEADS, LANES), F32), jax.ShapeDtypeStruct((8, LANES), F32),
                   jax.ShapeDtypeStruct((8, LANES), F32)],
        scratch=[pltpu.VMEM((CHUNK, d), F32)] * 3 + [pltpu.VMEM((2, N_KV_HEADS, GROUP_ROWS, 2 * CHUNK), F32)],
        name="attn_bwd", sem=("arbitrary",), carry=carry)


def _adamw_math(g, w, m, v):
    m = ADAM_B1 * m + (1.0 - ADAM_B1) * g
    v = ADAM_B2 * v + (1.0 - ADAM_B2) * (g * g)
    m_hat = m / (1.0 - ADAM_B1 ** ADAM_STEP)
    v_hat = v / (1.0 - ADAM_B2 ** ADAM_STEP)
    delta = -ADAM_LR * (m_hat / (jnp.sqrt(v_hat) + ADAM_EPS) + ADAM_WD * w)
    return delta, m, v


def _row_tile(r, cap=128):
    for tr in range(min(r, cap), 0, -1):
        if r % tr == 0 and (tr % 8 == 0 or tr == r):
            return tr
    return r


def _chip_sum(grad, recv, place, name, wire_dtype):
    _, r, c = grad.shape
    tr = _row_tile(r, 256)

    def body(pl_ref, g_ref, a_ref, p_ref):
        p_ref[...] = (g_ref[...] + a_ref[...]).astype(p_ref.dtype)

    other = lambda rel, pr: pr[0] ^ (rel + 1)
    return pl.pallas_call(
        body,
        grid_spec=pltpu.PrefetchScalarGridSpec(
            num_scalar_prefetch=1, grid=(3, r // tr),
            in_specs=[pl.BlockSpec((None, None, tr, c), lambda rel, i, pr: (other(rel, pr), pr[1], i, 0)),
                      pl.BlockSpec((None, tr, c), lambda rel, i, pr: (other(rel, pr), i, 0))],
            out_specs=pl.BlockSpec((None, tr, c), lambda rel, i, pr: (other(rel, pr), i, 0))),
        out_shape=jax.ShapeDtypeStruct((4, r, c), wire_dtype), name=name, compiler_params=_params(),
    )(place, grad.reshape(4, 2, r, c), recv)


def _adamw_sharded(grad, recv, others, place, w, m, v, name, layer=None, fill=None):
    r, c = w.shape[-2:]
    tr = _row_tile(r)

    def body(pl_ref, g_ref, a_ref, oth_ref, w_ref, m_ref, v_ref, *rest):
        g_out, d_out, nm_out, nv_out = rest[-4:]
        g = g_ref[...] + a_ref[...]
        for k in range(3):
            g = g + oth_ref[k].astype(F32)
        delta, nm, nv = _adamw_math(g, w_ref[...], m_ref[...], v_ref[...])
        g_out[...] = g
        d_out[...] = delta
        nm_out[...] = nm
        nv_out[...] = nv

    if layer is None:
        row = pl.BlockSpec((tr, c), lambda i, pr: (i, 0))
    else:
        row = pl.BlockSpec((None, tr, c), lambda i, pr: (layer, i, 0))
    n_fill = 0 if fill is None else 4
    in_specs = [pl.BlockSpec((None, None, tr, c), lambda i, pr: (pr[0], pr[1], i, 0)),
                pl.BlockSpec((None, tr, c), lambda i, pr: (pr[0], i, 0)),
                pl.BlockSpec((3, tr, c), lambda i, pr: (0, i, 0)), row, row, row]
    in_specs += [pl.BlockSpec(memory_space=pl.ANY)] * n_fill
    return pl.pallas_call(
        body,
        grid_spec=pltpu.PrefetchScalarGridSpec(
            num_scalar_prefetch=1, grid=(r // tr,), in_specs=in_specs, out_specs=[row] * 4),
        out_shape=[jax.ShapeDtypeStruct(w.shape, F32)] * 4, name=name, compiler_params=_params(),
        input_output_aliases={7 + j: j for j in range(n_fill)},
    )(place, grad.reshape(4, 2, r, c), recv, others, w, m, v, *([] if fill is None else fill))


def _sum_devices(parts, name):
    def body(p_ref, o_ref):
        total = p_ref[0]
        for k in range(1, N_SHARDS):
            total = total + p_ref[k]
        o_ref[...] = total

    return pl.pallas_call(body, out_shape=jax.ShapeDtypeStruct(parts.shape[1:], F32), name=name)(parts)


def _adamw_summed(parts, ws, ms, vs, name):
    n = len(parts)

    def body(*refs):
        p_refs, w_refs, m_refs, v_refs = refs[:n], refs[n:2 * n], refs[2 * n:3 * n], refs[3 * n:4 * n]
        o_refs = refs[4 * n:]
        for i in range(n):
            g = p_refs[i][0]
            for k in range(1, N_SHARDS):
                g = g + p_refs[i][k]
            delta, nm, nv = _adamw_math(g, w_refs[i][...], m_refs[i][...], v_refs[i][...])
            o_refs[4 * i][...] = g
            o_refs[4 * i + 1][...] = delta
            o_refs[4 * i + 2][...] = nm
            o_refs[4 * i + 3][...] = nv

    shapes = [jax.ShapeDtypeStruct(w.shape, F32) for w in ws for _ in range(4)]
    outs = pl.pallas_call(body, out_shape=shapes, name=name, compiler_params=_params())(*parts, *ws, *ms, *vs)
    return [outs[4 * i:4 * i + 4] for i in range(n)]


def _dup_heads(w):
    lead = w.shape[:-1]
    w4 = w.reshape(lead + (N_KV_HEADS, 1, HEAD_DIM))
    return jnp.broadcast_to(w4, lead + (N_KV_HEADS, 2, HEAD_DIM)).reshape(lead + (N_KV_HEADS * LANES,))


def _fold_heads(g):
    lead = g.shape[:-1]
    return g.reshape(lead + (N_KV_HEADS, 2, HEAD_DIM)).sum(axis=-2).reshape(lead + (N_KV_HEADS * HEAD_DIM,))


def kernel(x, a_norm, a_w_in, a_v_norm, a_w_s, a_b_s, a_w_out, f_norm, f_w_in, f_conv_w, f_conv_b, f_w_out, kv_norm, w_kv, k_norm, b_norm, b_w_q, b_q_norm, b_sinks, b_w_o, loss_target, m_a_norm, m_a_w_in, m_a_v_norm, m_a_w_s, m_a_b_s, m_a_w_out, m_f_norm, m_f_w_in, m_f_conv_w, m_f_conv_b, m_f_w_out, m_kv_norm, m_w_kv, m_k_norm, m_b_norm, m_b_w_q, m_b_q_norm, m_b_sinks, m_b_w_o, v_a_norm, v_a_w_in, v_a_v_norm, v_a_w_s, v_a_b_s, v_a_w_out, v_f_norm, v_f_w_in, v_f_conv_w, v_f_conv_b, v_f_w_out, v_kv_norm, v_w_kv, v_k_norm, v_b_norm, v_b_w_q, v_b_q_norm, v_b_sinks, v_b_w_o):
    d = D_MODEL
    xi, yi, ci = _coords()
    place = jnp.stack([2 * xi + yi, ci]).astype(jnp.int32)
    bf = lambda a: a.astype(BF16)
    row = lambda v_: v_.reshape(1, -1)
    x0, target = x[0], loss_target[0]
    t = x0.shape[0]
    res = {}

    red = {}

    def to_sibling(grads, wire=BF16):
        for k, g in grads.items():
            red[k] = dict(grad=g, wire=wire)
        ex = _ToSibling(list(grads.values()))
        ex.names = list(grads)
        return ex

    def to_chips(ex):
        for k, a in zip(ex.names, ex.results):
            red[k]["recv"] = a
            red[k]["psum"] = _chip_sum(red[k]["grad"], a, place, f"chip_sum_{k}", red[k]["wire"])
        nxt = _ToChips([red[k]["psum"] for k in ex.names])
        nxt.names = ex.names
        return nxt

    def landed(ex):
        for k, b in zip(ex.names, ex.results):
            red[k]["others"] = b

    def halves(ex, first_rows):
        parts = []
        for r0, nr in ((0, first_rows), (first_rows, ex.srcs[0].shape[1] - first_rows)):
            part = _ToChips(ex.srcs, rows=(r0, nr))
            part.names = ex.names
            parts.append(part)
        return parts

    def landed_halves(parts):
        for j, k in enumerate(parts[0].names):
            red[k]["others"] = jnp.concatenate([p.results[j] for p in parts], axis=1)

    def update(k, w, m, v, layer=None, fill=None):
        r = red[k]
        return _adamw_sharded(r["grad"], r["recv"], r["others"], place, w, m, v,
                              f"adamw_{k}", layer=layer, fill=fill)

    g_a_in, g_a_out, g_a_norm, g_a_v_norm, g_conv = _exchange_alone(
        _Gather([bf(a_w_in[0]), bf(a_w_out[0]), a_norm, a_v_norm, f_conv_w.reshape(6, FF_SHARD)]), "gather_first")
    a_norm_full, a_v_norm_full = g_a_norm.reshape(1, d), g_a_v_norm.reshape(1, d)
    conv_w = lax.reduce_precision(g_conv.reshape(N_SHARDS, 2, 3, FF_SHARD), 8, 7)
    cw = jnp.pad(jnp.transpose(conv_w, (1, 0, 2, 3)), ((0, 0), (0, 0), (0, 5), (0, 0)))
    w_a_in_flat = jnp.transpose(g_a_in, (1, 0, 2)).reshape(d, 2 * d)
    cb = f_conv_b.reshape(2, N_SHARDS, 1, FF_SHARD)
    tri = jnp.tril(jnp.ones((CHUNK, CHUNK), dtype=bool))
    w_causal = jnp.where(tri[None], a_w_s[0], 0.0).astype(BF16)
    w_causal_t = jnp.transpose(w_causal, (0, 2, 1))
    b_sb = jnp.broadcast_to(a_b_s[0][:, :, None], (N_GROUPS, CHUNK, CHUNK))
    w_a_out = g_a_out.reshape(d, d)
    gq = jnp.tile(b_q_norm.reshape(1, HEAD_DIM), (1, 2))
    gk = jnp.tile(k_norm.reshape(1, HEAD_DIM), (1, 2))
    sinks = b_sinks.reshape(N_Q_HEADS)

    ex = _Gather([bf(f_w_in[0]), bf(f_w_out[0])])
    zpre, x1, h1 = _sgu_fwd(x0, a_norm_full, g_a_in, a_v_norm_full, w_causal, b_sb, w_a_out, carry=ex)
    w_in0, w_out0 = ex.results[0], ex.results[1].reshape(D_FF, d)
    ex = _Gather([bf(w_kv), bf(b_w_q[0]), bf(b_w_o[0]), bf(f_w_in[1])], relay=False, early=True)
    x2, hf0, a0, pre0, hk, hq = _ffn_fwd(x1, f_norm[0:1], w_in0, cw[0], cb[0], w_out0, 0, carry=ex,
                                         next_gains=[row(kv_norm), b_norm])
    kv_full = ex.results[0].reshape(d, 2 * N_KV_HEADS * HEAD_DIM)
    w_q, w_o = ex.results[1].reshape(d, d), ex.results[2].reshape(d, d)
    w_in1 = ex.results[3]
    half = N_KV_HEADS * HEAD_DIM
    w_kv_dup = jnp.concatenate([_dup_heads(kv_full[:, :half]), _dup_heads(kv_full[:, half:])], axis=1)
    kvd = _mm_rows(hk, w_kv_dup, F32, "kv_proj")
    qraw = _mm_rows(hq, w_q, F32, "q_proj")
    ex = _Gather([bf(f_w_out[1])], relay=False, early=True)
    o = _attn_fwd(qraw, kvd, gq, gk, sinks, carry=ex)
    w_out1 = ex.results[0].reshape(D_FF, d)
    x3 = _mm_rows(o, w_o, F32, "o_proj", res=x2)
    _, hf1, a1, pre1, dy, loss_lanes = _ffn_fwd(x3, f_norm[1:2], w_in1, cw[1], cb[1], w_out1, 1, loss_target=target)

    dhu1, dw_out1, dcb1 = _ffn_bwd_act(pre1, w_out1, dy, 1)
    ex = to_sibling({"f_w_out1": dw_out1.reshape(N_SHARDS, D_FF // N_SHARDS, d)})
    da1, dx3, dcw1, dgf1 = _ffn_bwd_in(dhu1, a1, cw[1], w_in1, 1, carry=ex, norm=(x3, f_norm[1:2], dy))
    ex = to_chips(ex)
    dw_in1 = _ffn_wgrad_in(hf1, da1, 1, carry=ex)
    landed(ex)
    ex = to_sibling({"f_w_in1": dw_in1})
    d_o = _mm_rows(dx3, w_o, BF16, "o_proj_bwd", trans_w=True, carry=ex)
    ex = to_chips(ex)
    dw_o = _mm_wgrad(o, dx3, "o_wgrad").reshape(N_SHARDS, d // N_SHARDS, d)
    dq, dkv, dsink, dgq, dgk = _attn_bwd(qraw, kvd, d_o, gq, gk, sinks, carry=ex)
    landed(ex)
    dw_q = _mm_wgrad(hq, dq, "q_wgrad").reshape(N_SHARDS, d // N_SHARDS, d)
    dw_kv_dup = _mm_wgrad(hk, dkv, "kv_wgrad")
    dw_kv = jnp.concatenate(
        [_fold_heads(dw_kv_dup[:, :4 * LANES]), _fold_heads(dw_kv_dup[:, 4 * LANES:])], axis=1
    ).reshape(N_SHARDS, d // N_SHARDS, 2 * N_KV_HEADS * HEAD_DIM)
    ex = to_sibling({"b_w_o": dw_o, "b_w_q": dw_q, "w_kv": dw_kv})
    dx2, dg2 = _rms_bwd(x2, [row(kv_norm), b_norm], [dkv, dq], dx3, "kvq_norm_bwd", tm=512, carry=ex,
                        through=[w_kv_dup, w_q])
    ex = to_chips(ex)
    dhu0, dw_out0, dcb0 = _ffn_bwd_act(pre0, w_out0, dx2, 0, carry=ex)
    landed(ex)
    ex = to_sibling({"f_w_out0": dw_out0.reshape(N_SHARDS, D_FF // N_SHARDS, d)})
    da0, dhf0, dcw0 = _ffn_bwd_in(dhu0, a0, cw[0], w_in0, 0, carry=ex)
    ex = to_chips(ex)
    dw_in0 = _ffn_wgrad_in(hf0, da0, 0, carry=ex)
    landed(ex)
    ex = to_sibling({"f_w_in0": dw_in0})
    dx1, dgf0 = _rms_bwd(x1, [f_norm[0:1]], [dhf0], dx2, "f0_norm_bwd", carry=ex)
    ex_lo, ex_hi = halves(to_chips(ex), 448)
    dz, y, dwc, dbs, dgv = _sgu_bwd(dx1, zpre, w_a_out, a_v_norm_full, w_causal, w_causal_t, b_sb, carry=ex_lo)
    dw_a_out = _mm_wgrad(y, dx1, "a_out_wgrad").reshape(N_SHARDS, d // N_SHARDS, d)
    nsub = g_a_in.shape[2]
    dw_a_in = _mm(
        h1, dz, pl.BlockSpec((t, d), lambda s, j, kk: (0, 0)), pl.BlockSpec((t, nsub), lambda s, j, kk: (0, s)),
        pl.BlockSpec((None, d, nsub), lambda s, j, kk: (s, 0, 0)), jax.ShapeDtypeStruct((N_SHARDS, d, nsub), F32),
        (N_SHARDS, 1, 1), TN, "a_in_wgrad", carry=ex_hi)
    landed_halves([ex_lo, ex_hi])

    def bias_grad(dcb):
        return jnp.transpose(dcb[:, :, 0, :], (1, 0, 2)).reshape(-1)

    g_conv_w = jnp.concatenate([dcw0[:, 0:3, :], dcw1[:, 0:3, :]], axis=1)
    g_a_v_norm = dgv[0].reshape(N_SHARDS, 1, LANES)
    rep = ["a_w_s", "a_b_s", "f_norm", "f_conv_b", "kv_norm", "k_norm", "b_norm", "b_q_norm", "b_sinks"]
    rep_g = dict(
        a_w_s=dwc.reshape(N_GROUPS * CHUNK, CHUNK), a_b_s=dbs[:, :, 0], f_norm=jnp.stack([dgf0[0], dgf1[0]]),
        f_conv_b=jnp.stack([bias_grad(dcb0), bias_grad(dcb1)]), kv_norm=dg2[0:1],
        k_norm=(dgk[0, :HEAD_DIM] + dgk[0, HEAD_DIM:])[None], b_norm=dg2[1:2],
        b_q_norm=(dgq[0, :HEAD_DIM] + dgq[0, HEAD_DIM:])[None], b_sinks=dsink[:, 0][None])
    ex_big = to_sibling({"a_w_out": dw_a_out, "a_w_in": dw_a_in})
    ex_small = to_sibling({"a_v_norm": g_a_v_norm, "f_conv_w": g_conv_w}, wire=F32)
    ex_rep = _Gather([rep_g[k] for k in rep] + [loss_lanes], relay=False)
    together = _Together([ex_big, ex_small, ex_rep])
    dh1 = _mm_rows(dz, w_a_in_flat, F32, "a_in_bwd", trans_w=True, carry=together)
    together.spread()
    ex_big, ex_small = to_chips(ex_big), to_chips(ex_small)
    together = _Together([ex_big, ex_small])
    grad_x, dg0 = _rms_bwd(x0, [a_norm_full], [dh1], dx1, "a_norm_bwd", carry=together)
    together.spread()
    landed(ex_big)
    landed(ex_small)
    (a_norm_parts,) = _exchange_alone(_ToOwners([dg0[0].reshape(N_SHARDS, 1, LANES)]), "a_norm_to_owners")

    res["f_w_out"] = update("f_w_out1", f_w_out, m_f_w_out, v_f_w_out, layer=1)
    w_in_t = [jnp.swapaxes(a_, 1, 2) for a_ in (f_w_in, m_f_w_in, v_f_w_in)]
    res["f_w_in"] = update("f_w_in1", *w_in_t, layer=1)
    res["b_w_o"] = update("b_w_o", b_w_o, m_b_w_o, v_b_w_o, layer=0)
    res["b_w_q"] = update("b_w_q", b_w_q, m_b_w_q, v_b_w_q, layer=0)
    res["w_kv"] = update("w_kv", w_kv, m_w_kv, v_w_kv)
    res["f_w_out"] = update("f_w_out0", f_w_out, m_f_w_out, v_f_w_out, layer=0, fill=res["f_w_out"])
    res["f_w_in"] = [jnp.swapaxes(o_, 1, 2) for o_ in update("f_w_in0", *w_in_t, layer=0, fill=res["f_w_in"])]
    res["a_w_out"] = update("a_w_out", a_w_out, m_a_w_out, v_a_w_out, layer=0)
    res["a_w_in"] = update("a_w_in", a_w_in, m_a_w_in, v_a_w_in, layer=0)
    res["a_v_norm"] = update("a_v_norm", a_v_norm, m_a_v_norm, v_a_v_norm)
    res["f_conv_w"] = [o_.reshape(f_conv_w.shape) for o_ in update(
        "f_conv_w", f_conv_w.reshape(6, FF_SHARD), m_f_conv_w.reshape(6, FF_SHARD), v_f_conv_w.reshape(6, FF_SHARD))]

    rep_w = dict(a_w_s=a_w_s, a_b_s=a_b_s, f_norm=f_norm, f_conv_b=f_conv_b, kv_norm=kv_norm, k_norm=k_norm,
                 b_norm=b_norm, b_q_norm=b_q_norm, b_sinks=b_sinks, a_norm=a_norm)
    rep_m = dict(a_w_s=m_a_w_s, a_b_s=m_a_b_s, f_norm=m_f_norm, f_conv_b=m_f_conv_b, kv_norm=m_kv_norm,
                 k_norm=m_k_norm, b_norm=m_b_norm, b_q_norm=m_b_q_norm, b_sinks=m_b_sinks, a_norm=m_a_norm)
    rep_v = dict(a_w_s=v_a_w_s, a_b_s=v_a_b_s, f_norm=v_f_norm, f_conv_b=v_f_conv_b, kv_norm=v_kv_norm,
                 k_norm=v_k_norm, b_norm=v_b_norm, b_q_norm=v_b_q_norm, b_sinks=v_b_sinks, a_norm=v_a_norm)
    keys = rep + ["a_norm"]
    loss = _sum_devices(ex_rep.results[-1], "loss_sum")[0, 0]
    parts = ex_rep.results[:-1] + [a_norm_parts]
    as2d = lambda a, p: a.reshape(p.shape[1:])
    rep_outs = _adamw_summed(parts, [as2d(rep_w[k], p) for k, p in zip(keys, parts)],
                             [as2d(rep_m[k], p) for k, p in zip(keys, parts)],
                             [as2d(rep_v[k], p) for k, p in zip(keys, parts)], "adamw_replicated")
    for j, key in enumerate(keys):
        res[key] = [o_.reshape(rep_w[key].shape) for o_ in rep_outs[j]]

    order = ["a_norm", "a_w_in", "a_v_norm", "a_w_s", "a_b_s", "a_w_out", "f_norm", "f_w_in", "f_conv_w", "f_conv_b",
             "f_w_out", "kv_norm", "w_kv", "k_norm", "b_norm", "b_w_q", "b_q_norm", "b_sinks", "b_w_o"]
    outs = [loss, grad_x[None]]
    for j in range(4):
        outs += [res[k][j] for k in order]
    return tuple(outs)
```

```python
import jax
import jax.numpy as jnp
from jax import lax
from jax.experimental import pallas as pl
from jax.experimental.pallas import tpu as pltpu

F32 = jnp.float32
BF16 = jnp.bfloat16
EPS = 1e-6
D_MODEL = 1024
CHUNK = 128
N_GROUPS = 8
N_SHARDS = 8
HEAD_DIM = 64
N_Q_HEADS = 16
N_KV_HEADS = 4
D_FF = 2816
FF_SHARD = 2 * D_FF // N_SHARDS
LANES = 128
NEG_BIG = -1e30
ADAM_LR = 0.001
ADAM_B1 = 0.9
ADAM_B2 = 0.999
ADAM_EPS = 1e-08
ADAM_WD = 0.01
ADAM_STEP = 10
VMEM_LIMIT_BYTES = 56 * 1024 * 1024
MESH = pl.DeviceIdType.MESH

NN = (((1,), (0,)), ((), ()))
NT = (((1,), (1,)), ((), ()))
TN = (((0,), (0,)), ((), ()))
SLOPES = tuple(2.0 ** (-8.0 * (h + 1) / N_Q_HEADS) for h in range(N_Q_HEADS))


def _params(sem=None):
    return pltpu.CompilerParams(dimension_semantics=sem, vmem_limit_bytes=VMEM_LIMIT_BYTES)


def _dot(a, b, dims=NN):
    return lax.dot_general(a, b, dims, preferred_element_type=F32)


def _sigmoid(x):
    return 1.0 / (1.0 + jnp.exp(-x))


def _gelu_parts(z):
    cdf = 0.5 * (1.0 + lax.erf(z * (2.0 ** -0.5)))
    pdf = jnp.exp(-0.5 * z * z) * 0.3989422804014327
    return cdf, pdf


def _coords():
    return lax.axis_index("x"), lax.axis_index("y"), lax.axis_index("c")


class _Gather:
    def __init__(self, srcs, relay=True, early=False):
        self.srcs = list(srcs)
        self.early = early
        n = len(self.srcs)
        self.relayed = [relay and s.shape[0] % 32 == 0 for s in self.srcs]
        self.out_shapes = [jax.ShapeDtypeStruct((N_SHARDS,) + s.shape, s.dtype) for s in self.srcs]
        self.sems = [pltpu.SemaphoreType.DMA((n, 9)), pltpu.SemaphoreType.DMA((n, 9)), pltpu.SemaphoreType.DMA((n,))]

    def _plan(self, src, dst, sems):
        send_sems, recv_sems, local_sems = sems
        x, y, c = _coords()
        n = len(src)

        def rows(e, dev, half=None):
            block = dst[e].at[4 * dev[0] + 2 * dev[1] + dev[2]]
            if half is None:
                return block
            nr = self.srcs[e].shape[0] // 2
            return block.at[pl.ds(half * nr, nr)]

        def copy(e, slot, block, to, half=None, from_own=False):
            return pltpu.make_async_remote_copy(
                src_ref=src[e] if from_own else rows(e, block, half), dst_ref=rows(e, block, half),
                send_sem=send_sems.at[e, slot], recv_sem=recv_sems.at[e, slot], device_id=to, device_id_type=MESH)

        return n, x, y, c, rows, copy, local_sems

    def start(self, src, dst, sems):
        n, x, y, c, rows, copy, local_sems = self._plan(src, dst, sems)
        me = (x, y, c)
        for e in range(n):
            pltpu.make_async_copy(src[e], rows(e, me), local_sems.at[e]).start()
            copy(e, 0, me, (x, y, 1 - c), from_own=True).start()
            copy(e, 1, me, (1 - x, y, c), from_own=True).start()
            copy(e, 2, me, (x, 1 - y, c), from_own=True).start()
            if not self.relayed[e]:
                copy(e, 3, me, (1 - x, 1 - y, c), from_own=True).start()

    def pass_on(self, src, dst, sems, wait=True):
        n, x, y, c, rows, copy, local_sems = self._plan(src, dst, sems)
        me, sibling = (x, y, c), (x, y, 1 - c)
        over_x, over_y, diagonal = (1 - x, y, c), (x, 1 - y, c), (1 - x, 1 - y, c)
        sent = []

        def arrived(cp):
            if wait:
                cp.wait_recv()

        def send(cp):
            if wait:
                cp.start()
            sent.append(cp)

        for slot, owner, onward, half in ((1, over_x, over_y, 0), (2, over_y, over_x, 1)):
            for e in range(n):
                arrived(copy(e, slot, owner, me))
                if self.relayed[e]:
                    send(copy(e, 3 + half, owner, onward, half=half))
                send(copy(e, 4 + slot, owner, sibling))
        for e in range(n):
            if self.relayed[e]:
                for half in (0, 1):
                    arrived(copy(e, 3 + half, diagonal, me, half=half))
                    send(copy(e, 7 + half, diagonal, sibling, half=half))
            else:
                arrived(copy(e, 3, diagonal, me))
                send(copy(e, 7, diagonal, sibling))
        return sent

    def finish(self, src, dst, sems, passed_on=False):
        n, x, y, c, rows, copy, local_sems = self._plan(src, dst, sems)
        me, sibling = (x, y, c), (x, y, 1 - c)
        over_x, over_y, diagonal = (1 - x, y, c), (x, 1 - y, c), (1 - x, 1 - y, c)
        sent = self.pass_on(src, dst, sems, wait=not passed_on)
        for e in range(n):
            copy(e, 0, sibling, me).wait_recv()
            copy(e, 5, (1 - x, y, 1 - c), me).wait_recv()
            copy(e, 6, (x, 1 - y, 1 - c), me).wait_recv()
            if self.relayed[e]:
                for half in (0, 1):
                    copy(e, 7 + half, (1 - x, 1 - y, 1 - c), me, half=half).wait_recv()
            else:
                copy(e, 7, (1 - x, 1 - y, 1 - c), me).wait_recv()
        for e in range(n):
            copy(e, 0, me, sibling, from_own=True).wait_send()
            copy(e, 1, me, over_x, from_own=True).wait_send()
            copy(e, 2, me, over_y, from_own=True).wait_send()
            if not self.relayed[e]:
                copy(e, 3, me, diagonal, from_own=True).wait_send()
            pltpu.make_async_copy(src[e], rows(e, me), local_sems.at[e]).wait()
        for cp in sent:
            cp.wait_send()


class _ToSibling:
    def __init__(self, grads):
        self.srcs = list(grads)
        n = len(self.srcs)
        self.out_shapes = [jax.ShapeDtypeStruct((4,) + g.shape[1:], g.dtype) for g in self.srcs]
        self.sems = [pltpu.SemaphoreType.DMA((n, 4)), pltpu.SemaphoreType.DMA((n, 4))]

    def _copies(self, src, dst, sems):
        send_sems, recv_sems = sems
        x, y, c = _coords()
        return [
            pltpu.make_async_remote_copy(
                src_ref=src[i].at[2 * q + (1 - c)], dst_ref=dst[i].at[q], send_sem=send_sems.at[i, q],
                recv_sem=recv_sems.at[i, q], device_id=(x, y, 1 - c), device_id_type=MESH)
            for i in range(len(src)) for q in range(4)]

    def start(self, src, dst, sems):
        for cp in self._copies(src, dst, sems):
            cp.start()

    def finish(self, src, dst, sems):
        for cp in self._copies(src, dst, sems):
            cp.wait()


class _ToChips:
    def __init__(self, psums, rows=None):
        self.srcs = list(psums)
        n = len(self.srcs)
        self.rows = rows
        self.out_shapes = [
            jax.ShapeDtypeStruct((3, p.shape[1] if rows is None else rows[1]) + p.shape[2:], p.dtype)
            for p in self.srcs]
        self.sems = [pltpu.SemaphoreType.DMA((n, 3)), pltpu.SemaphoreType.DMA((n, 3))]

    def _copies(self, src, dst, sems):
        send_sems, recv_sems = sems
        x, y, c = _coords()
        peers = [(x, 1 - y), (1 - x, y), (1 - x, 1 - y)]

        def part(i, q):
            if self.rows is None:
                return src[i].at[q]
            return src[i].at[q, pl.ds(self.rows[0], self.rows[1])]

        return [
            pltpu.make_async_remote_copy(
                src_ref=part(i, 2 * px + py), dst_ref=dst[i].at[r], send_sem=send_sems.at[i, r],
                recv_sem=recv_sems.at[i, r], device_id=(px, py, c), device_id_type=MESH)
            for i in range(len(src)) for r, (px, py) in enumerate(peers)]

    def start(self, src, dst, sems):
        for cp in self._copies(src, dst, sems):
            cp.start()

    def finish(self, src, dst, sems):
        for cp in self._copies(src, dst, sems):
            cp.wait()


class _ToOwners:
    def __init__(self, grads):
        self.srcs = list(grads)
        n = len(self.srcs)
        self.out_shapes = [jax.ShapeDtypeStruct(g.shape, g.dtype) for g in self.srcs]
        self.sems = [pltpu.SemaphoreType.DMA((n, 7)), pltpu.SemaphoreType.DMA((n, 7)), pltpu.SemaphoreType.DMA((n,))]

    def _copies(self, src, dst, sems):
        send_sems, recv_sems, local_sems = sems
        x, y, c = _coords()
        me = 4 * x + 2 * y + c
        copies = [pltpu.make_async_copy(src[i].at[me], dst[i].at[me], local_sems.at[i]) for i in range(len(src))]
        for i in range(len(src)):
            for rel in range(1, N_SHARDS):
                px = x ^ (rel >> 2) if rel >> 2 else x
                py = y ^ ((rel >> 1) & 1) if (rel >> 1) & 1 else y
                pc = c ^ (rel & 1) if rel & 1 else c
                copies.append(pltpu.make_async_remote_copy(
                    src_ref=src[i].at[4 * px + 2 * py + pc], dst_ref=dst[i].at[me], send_sem=send_sems.at[i, rel - 1],
                    recv_sem=recv_sems.at[i, rel - 1], device_id=(px, py, pc), device_id_type=MESH))
        return copies

    def start(self, src, dst, sems):
        for cp in self._copies(src, dst, sems):
            cp.start()

    def finish(self, src, dst, sems):
        for cp in self._copies(src, dst, sems):
            cp.wait()


class _Together:
    def __init__(self, parts):
        self.parts = list(parts)
        self.srcs = [s for p in self.parts for s in p.srcs]
        self.out_shapes = [s for p in self.parts for s in p.out_shapes]
        self.sems = [s for p in self.parts for s in p.sems]

    def _split(self, src, dst, sems):
        a = b = c = 0
        for p in self.parts:
            na, nc = len(p.srcs), len(p.sems)
            yield p, src[a:a + na], dst[b:b + na], sems[c:c + nc]
            a, b, c = a + na, b + na, c + nc

    def start(self, src, dst, sems):
        for p, s, d, m in self._split(src, dst, sems):
            p.start(s, d, m)

    def finish(self, src, dst, sems):
        for p, s, d, m in self._split(src, dst, sems):
            p.finish(s, d, m)

    def spread(self):
        b = 0
        for p in self.parts:
            p.results = self.results[b:b + len(p.srcs)]
            b += len(p.srcs)


def _call(body, args, *, grid, in_specs, out_specs, out_shape, name, scratch=(), sem=None, carry=None):
    out_shape, out_specs = list(out_shape), list(out_specs)
    if carry is None:
        return pl.pallas_call(
            body, grid=grid, in_specs=list(in_specs), out_specs=out_specs, out_shape=out_shape,
            scratch_shapes=list(scratch), name=name, compiler_params=_params(sem))(*args)
    n_in, n_out, n_scr, n_c = len(args), len(out_shape), len(scratch), len(carry.srcs)
    steps = tuple(grid)
    total = 1
    for n_ax in steps:
        total *= n_ax
    early = getattr(carry, "early", False) and total >= 8
    early_step = total - max(2, total // 8)

    def carried(*refs):
        ins, rest = refs[:n_in], refs[n_in:]
        c_src, rest = rest[:n_c], rest[n_c:]
        outs, rest = rest[:n_out], rest[n_out:]
        c_dst, rest = rest[:n_c], rest[n_c:]
        scr, sems = rest[:n_scr], rest[n_scr:]
        step = pl.program_id(0)
        for ax in range(1, len(steps)):
            step = step * steps[ax] + pl.program_id(ax)

        @pl.when(step == 0)
        def _():
            carry.start(c_src, c_dst, sems)

        body(*ins, *outs, *scr)

        if early:
            @pl.when(step == early_step)
            def _():
                carry.pass_on(c_src, c_dst, sems)

        @pl.when(step == total - 1)
        def _():
            if early:
                carry.finish(c_src, c_dst, sems, passed_on=True)
            else:
                carry.finish(c_src, c_dst, sems)

    hbm = pl.BlockSpec(memory_space=pl.ANY)
    res = pl.pallas_call(
        carried, grid=grid, in_specs=list(in_specs) + [hbm] * n_c, out_specs=out_specs + [hbm] * n_c,
        out_shape=out_shape + carry.out_shapes, scratch_shapes=list(scratch) + carry.sems, name=name,
        compiler_params=_params(("arbitrary",) * len(steps)))(*args, *carry.srcs)
    carry.results = list(res[n_out:])
    return list(res[:n_out])


def _exchange_alone(ex, name):
    n = len(ex.srcs)

    def body(*refs):
        src, dst, sems = refs[:n], refs[n:2 * n], refs[2 * n:]
        ex.start(src, dst, sems)
        ex.finish(src, dst, sems)

    hbm = pl.BlockSpec(memory_space=pl.ANY)
    res = pl.pallas_call(body, in_specs=[hbm] * n, out_specs=[hbm] * n, out_shape=ex.out_shapes,
                         scratch_shapes=ex.sems, name=name)(*ex.srcs)
    ex.results = list(res)
    return ex.results


def _rms_bwd(x, gains, dhs, dres, name, tm=256, carry=None, through=None):
    t, d = x.shape
    n = len(gains)
    n_w = 0 if through is None else n

    def body(*refs):
        x_ref, dres_ref = refs[0], refs[1]
        g_refs, dh_refs, w_refs = refs[2:2 + n], refs[2 + n:2 + 2 * n], refs[2 + 2 * n:2 + 2 * n + n_w]
        dx_ref, dg_ref = refs[2 + 2 * n + n_w], refs[3 + 2 * n + n_w]
        i = pl.program_id(0)

        @pl.when(i == 0)
        def _():
            dg_ref[...] = jnp.zeros_like(dg_ref)

        xf = x_ref[...]
        r = lax.rsqrt(jnp.mean(xf * xf, axis=-1, keepdims=True) + EPS)
        xhat = xf * r
        dx = dres_ref[...]
        for j in range(n):
            dh = dh_refs[j][...]
            if n_w:
                dh = _dot(dh.astype(BF16), w_refs[j][...], NT)
            dg_ref[j:j + 1, :] += jnp.sum(dh * xhat, axis=0, keepdims=True)
            gy = dh * g_refs[j][...]
            dx = dx + r * (gy - xhat * jnp.mean(gy * xhat, axis=-1, keepdims=True))
        dx_ref[...] = dx

    row = pl.BlockSpec((tm, d), lambda i: (i, 0))
    vec = pl.BlockSpec((1, d), lambda i: (0, 0))
    dh_rows = [pl.BlockSpec((tm, dh.shape[1]), lambda i: (i, 0)) for dh in dhs]
    w_full = [] if through is None else [pl.BlockSpec(w.shape, lambda i: (0, 0)) for w in through]
    return _call(body, [x, dres, *gains, *dhs, *(through or [])], grid=(t // tm,),
                 in_specs=[row, row] + [vec] * n + dh_rows + w_full,
                 out_specs=[row, pl.BlockSpec((8, d), lambda i: (0, 0))],
                 out_shape=[jax.ShapeDtypeStruct((t, d), F32), jax.ShapeDtypeStruct((8, d), F32)],
                 name=name, sem=("arbitrary",), carry=carry)


def _mm(a, b, a_spec, b_spec, o_spec, out_shape, grid, dims, name, res=None, res_spec=None, carry=None):
    nk = grid[2]
    acc_shape = tuple(s for s in o_spec.block_shape if s is not None)

    def body(*refs):
        a_ref, b_ref = refs[0], refs[1]
        r_ref = refs[2] if res is not None else None
        o_ref = refs[3] if res is not None else refs[2]
        p = _dot(a_ref[...].astype(BF16), b_ref[...].astype(BF16), dims)
        if nk == 1:
            if res is not None:
                p = p + r_ref[...]
            o_ref[...] = p.astype(o_ref.dtype)
            return
        acc_ref = refs[-1]
        k = pl.program_id(2)

        @pl.when(k == 0)
        def _():
            acc_ref[...] = p

        @pl.when(k > 0)
        def _():
            acc_ref[...] += p

        @pl.when(k == nk - 1)
        def _():
            out = acc_ref[...]
            if res is not None:
                out = out + r_ref[...]
            o_ref[...] = out.astype(o_ref.dtype)

    ins = [a, b] + ([res] if res is not None else [])
    specs = [a_spec, b_spec] + ([res_spec] if res is not None else [])
    return _call(body, ins, grid=grid, in_specs=specs, out_specs=[o_spec], out_shape=[out_shape],
                 scratch=[pltpu.VMEM(acc_shape, F32)] if nk > 1 else [], name=name,
                 sem=("parallel", "parallel", "arbitrary"), carry=carry)[0]


def _mm_rows(a, w, out_dtype, name, trans_w=False, res=None, tm=1024, carry=None):
    t, k = a.shape
    tm = min(tm, t)
    n = w.shape[0] if trans_w else w.shape[1]
    return _mm(
        a, w, pl.BlockSpec((tm, k), lambda i, j, kk: (i, 0)), pl.BlockSpec(w.shape, lambda i, j, kk: (0, 0)),
        pl.BlockSpec((tm, n), lambda i, j, kk: (i, 0)), jax.ShapeDtypeStruct((t, n), out_dtype), (t // tm, 1, 1),
        NT if trans_w else NN, name, res=res,
        res_spec=None if res is None else pl.BlockSpec((tm, n), lambda i, j, kk: (i, 0)), carry=carry)


def _mm_wgrad(a, b, name, carry=None):
    t, m = a.shape
    n = b.shape[1]
    tn = n // (4 if b.dtype == F32 else 2)
    return _mm(
        a, b, pl.BlockSpec((t, m), lambda i, j, kk: (0, 0)), pl.BlockSpec((t, tn), lambda i, j, kk: (0, j)),
        pl.BlockSpec((m, tn), lambda i, j, kk: (0, j)), jax.ShapeDtypeStruct((m, n), F32), (1, n // tn, 1), TN, name,
        carry=carry)


def _sgu_fwd(x0, g, w_in, g_v, w_c, b_sb, w_out, tm=256, carry=None):
    t, d = x0.shape
    nsub = w_in.shape[2]

    def body(x_ref, g_ref, win_ref, gv_ref, wc_ref, bsb_ref, wout_ref, zpre_ref, x1_ref, h_ref, u_s, v_s, vn_s, y_s):
        xf = x_ref[...]
        h = (xf * lax.rsqrt(jnp.mean(xf * xf, axis=-1, keepdims=True) + EPS) * g_ref[...]).astype(BF16)
        h_ref[...] = h
        for k in range(N_SHARDS):
            zk = _dot(h, win_ref[k])
            zpre_ref[:, k * nsub:(k + 1) * nsub] = zk
            cdf, _ = _gelu_parts(zk)
            if k < N_SHARDS // 2:
                u_s[:, k * nsub:(k + 1) * nsub] = zk * cdf
            else:
                v_s[:, (k - 4) * nsub:(k - 3) * nsub] = zk * cdf
        v = v_s[...]
        rv = lax.rsqrt(jnp.mean(v * v, axis=-1, keepdims=True) + EPS)
        vn_s[...] = (v * rv * gv_ref[...]).astype(BF16)
        for ci in range(tm // CHUNK):
            rows = slice(ci * CHUNK, (ci + 1) * CHUNK)
            for g in range(N_GROUPS):
                cols = slice(g * LANES, (g + 1) * LANES)
                sv = _dot(wc_ref[g], vn_s[rows, cols]) + bsb_ref[g]
                y_s[rows, cols] = (u_s[rows, cols] * sv).astype(BF16)
        x1_ref[...] = x_ref[...] + _dot(y_s[...], wout_ref[...])

    row = pl.BlockSpec((tm, d), lambda i: (i, 0))
    full = lambda a: pl.BlockSpec(a.shape, lambda i: (0,) * a.ndim)
    return _call(
        body, [x0, g, w_in, g_v, w_c, b_sb, w_out], grid=(t // tm,),
        in_specs=[row, full(g), full(w_in), full(g_v), full(w_c), full(b_sb), full(w_out)],
        out_specs=[pl.BlockSpec((tm, 2 * d), lambda i: (i, 0)), row, row],
        out_shape=[jax.ShapeDtypeStruct((t, 2 * d), F32), jax.ShapeDtypeStruct((t, d), F32),
                   jax.ShapeDtypeStruct((t, d), BF16)],
        scratch=[pltpu.VMEM((tm, d), F32), pltpu.VMEM((tm, d), F32), pltpu.VMEM((tm, d), BF16),
                 pltpu.VMEM((tm, d), BF16)],
        name="sgu_fwd", carry=carry)


def _sgu_bwd(dx1, zpre, w_out, g_v, w_c, w_ct, b_sb, tm=512, carry=None):
    t, d = dx1.shape

    def body(dx_ref, zpre_ref, wout_ref, gv_ref, wc_ref, wct_ref, bsb_ref,
             dz_ref, y_ref, dwc_ref, dbs_ref, dgv_ref, u_s, vn_s, dy_s, du_s, dvn_s):
        i = pl.program_id(0)

        @pl.when(i == 0)
        def _():
            dwc_ref[...] = jnp.zeros_like(dwc_ref)
            dbs_ref[...] = jnp.zeros_like(dbs_ref)
            dgv_ref[...] = jnp.zeros_like(dgv_ref)

        dy_s[...] = _dot(dx_ref[...].astype(BF16), wout_ref[...], NT)
        zu = zpre_ref[:, :d]
        zv = zpre_ref[:, d:]
        cdf_u, pdf_u = _gelu_parts(zu)
        cdf_v, pdf_v = _gelu_parts(zv)
        u_s[...] = zu * cdf_u
        v = zv * cdf_v
        rv = lax.rsqrt(jnp.mean(v * v, axis=-1, keepdims=True) + EPS)
        vhat = v * rv
        gv = gv_ref[...]
        vn_s[...] = (vhat * gv).astype(BF16)
        for ci in range(tm // CHUNK):
            rows = slice(ci * CHUNK, (ci + 1) * CHUNK)
            for g in range(N_GROUPS):
                cols = slice(g * LANES, (g + 1) * LANES)
                vnb = vn_s[rows, cols]
                sv = _dot(wc_ref[g], vnb) + bsb_ref[g]
                dyb = dy_s[rows, cols]
                ub = u_s[rows, cols]
                dsv = dyb * ub
                du_s[rows, cols] = dyb * sv
                y_ref[rows, cols] = (ub * sv).astype(BF16)
                dsvb = dsv.astype(BF16)
                dbs_ref[g] += dsv
                dwc_ref[g] += _dot(dsvb, vnb, NT)
                dvn_s[rows, cols] = _dot(wct_ref[g], dsvb)
        dvn = dvn_s[...]
        dgv_ref[0:1, :] += jnp.sum(dvn * vhat, axis=0, keepdims=True)
        gy = dvn * gv
        dv = rv * (gy - vhat * jnp.mean(gy * vhat, axis=-1, keepdims=True))
        dz_ref[:, :d] = (du_s[...] * (cdf_u + zu * pdf_u)).astype(BF16)
        dz_ref[:, d:] = (dv * (cdf_v + zv * pdf_v)).astype(BF16)

        @pl.when(i == t // tm - 1)
        def _():
            tri = (lax.broadcasted_iota(jnp.int32, (CHUNK, CHUNK), 0)
                   >= lax.broadcasted_iota(jnp.int32, (CHUNK, CHUNK), 1))
            for g in range(N_GROUPS):
                dwc_ref[g] = jnp.where(tri, dwc_ref[g], 0.0)
                dbs_ref[g] = jnp.broadcast_to(jnp.sum(dbs_ref[g], axis=1, keepdims=True), (CHUNK, CHUNK))

    row = pl.BlockSpec((tm, d), lambda i: (i, 0))
    row2 = pl.BlockSpec((tm, 2 * d), lambda i: (i, 0))
    full = lambda a: pl.BlockSpec(a.shape, lambda i: (0,) * a.ndim)
    grp = pl.BlockSpec((N_GROUPS, CHUNK, CHUNK), lambda i: (0, 0, 0))
    return _call(
        body, [dx1, zpre, w_out, g_v, w_c, w_ct, b_sb], grid=(t // tm,),
        in_specs=[row, row2, full(w_out), full(g_v), full(w_c), full(w_ct), full(b_sb)],
        out_specs=[row2, row, grp, grp, pl.BlockSpec((8, d), lambda i: (0, 0))],
        out_shape=[jax.ShapeDtypeStruct((t, 2 * d), BF16), jax.ShapeDtypeStruct((t, d), BF16),
                   jax.ShapeDtypeStruct((N_GROUPS, CHUNK, CHUNK), F32),
                   jax.ShapeDtypeStruct((N_GROUPS, CHUNK, CHUNK), F32), jax.ShapeDtypeStruct((8, d), F32)],
        scratch=[pltpu.VMEM((tm, d), F32), pltpu.VMEM((tm, d), BF16), pltpu.VMEM((tm, d), F32),
                 pltpu.VMEM((tm, d), F32), pltpu.VMEM((tm, d), F32)],
        name="sgu_bwd", sem=("arbitrary",), carry=carry)


ROW_CHUNK = 256
HALO = 16


def _ffn_fwd(x, g, w_in, cw, cb, w_out, layer, tm=512, carry=None, next_gains=(), loss_target=None):
    t, d = x.shape
    nc = N_SHARDS // 2
    n_gains = len(next_gains)
    with_loss = loss_target is not None

    def body(x_ref, xp_ref, g_ref, wg_ref, wu_ref, cwg_ref, cbg_ref, cwu_ref, cbu_ref, wout_ref, *rest):
        extra_in, rest = rest[:n_gains + with_loss], rest[n_gains + with_loss:]
        o_ref, hf_ref, a_ref, pre_ref = rest[:4]
        extra_out, hw_s = rest[4:-1], rest[-1]
        i, c = pl.program_id(0), pl.program_id(1)

        @pl.when(c == 0)
        def _():
            keep = jnp.where(i == 0, 0.0, 1.0)
            xw = jnp.concatenate([xp_ref[...] * keep, x_ref[...]], axis=0)
            xhat = xw * lax.rsqrt(jnp.mean(xw * xw, axis=-1, keepdims=True) + EPS)
            hw_s[...] = (xhat * g_ref[...]).astype(BF16)
            hf_ref[...] = hw_s[HALO:, :]
            o_ref[...] = x_ref[...]

        hw = hw_s[...]
        pre = []
        for j, (w_ref, cw_ref, cb_ref) in enumerate(((wg_ref, cwg_ref, cbg_ref), (wu_ref, cwu_ref, cbu_ref))):
            ab = _dot(hw, w_ref[...]).astype(BF16)
            a_ref[j] = ab[HALO:]
            win = ab.astype(F32)
            cw_v = cw_ref[...]
            pre.append(cw_v[2:3, :] * win[HALO:] + cw_v[1:2, :] * pltpu.roll(win, 1, 0)[HALO:]
                       + cw_v[0:1, :] * pltpu.roll(win, 2, 0)[HALO:] + cb_ref[...])
            pre_ref[j] = pre[j]
        act = (pre[0] * _sigmoid(pre[0]) * pre[1]).astype(BF16)
        o_ref[...] += _dot(act, wout_ref[...])

        if with_loss:
            @pl.when((i == 0) & (c == 0))
            def _():
                extra_out[-1][...] = jnp.zeros_like(extra_out[-1])

        @pl.when(c == nc - 1)
        def _():
            xn = o_ref[...]
            if n_gains:
                xhat = xn * lax.rsqrt(jnp.mean(xn * xn, axis=-1, keepdims=True) + EPS)
                for k in range(n_gains):
                    extra_out[k][...] = (xhat * extra_in[k][...]).astype(BF16)
            if with_loss:
                err = xn - extra_in[-1][...]
                extra_out[-2][...] = err * (1.0 / d)
                part = jnp.sum(jnp.sum(err * err, axis=0, keepdims=True), axis=1, keepdims=True)
                extra_out[-1][...] += jnp.broadcast_to(0.5 / d * part, extra_out[-1].shape)

    row = pl.BlockSpec((tm, d), lambda i, c: (i, 0))
    vec = pl.BlockSpec((1, d), lambda i, c: (0, 0))
    shard = lambda rows, up: pl.BlockSpec((None, rows, FF_SHARD), lambda i, c: (c + up * nc, 0, 0))
    pair = pl.BlockSpec((2, None, tm, FF_SHARD), lambda i, c: (0, c, i, 0))
    lanes = pl.BlockSpec((8, LANES), lambda i, c: (0, 0))
    outs = _call(
        body, [x, x, g, w_in, w_in, cw, cb, cw, cb, w_out, *next_gains] + ([loss_target] if with_loss else []),
        grid=(t // tm, nc),
        in_specs=[row, pl.BlockSpec((HALO, d), lambda i, c: (jnp.maximum(i * (tm // HALO) - 1, 0), 0)),
                  vec, shard(d, 0), shard(d, 1), shard(8, 0), shard(1, 0), shard(8, 1), shard(1, 1),
                  pl.BlockSpec((FF_SHARD, d), lambda i, c: (c, 0))] + [vec] * n_gains + [row] * with_loss,
        out_specs=[row, row, pair, pair] + [row] * n_gains + [row, lanes] * with_loss,
        out_shape=[jax.ShapeDtypeStruct((t, d), F32), jax.ShapeDtypeStruct((t, d), BF16),
                   jax.ShapeDtypeStruct((2, nc, t, FF_SHARD), BF16), jax.ShapeDtypeStruct((2, nc, t, FF_SHARD), F32)]
        + [jax.ShapeDtypeStruct((t, d), BF16)] * n_gains
        + [jax.ShapeDtypeStruct((t, d), F32), jax.ShapeDtypeStruct((8, LANES), F32)] * with_loss,
        scratch=[pltpu.VMEM((tm + HALO, d), BF16)], name=f"ffn{layer}_fwd", sem=("arbitrary", "arbitrary"), carry=carry)
    return (outs[0], outs[1], outs[2].reshape(N_SHARDS, t, FF_SHARD), outs[3]) + tuple(outs[4:])


def _ffn_bwd_act(pre, w_out, dxn, layer, tm=1024, carry=None):
    t, d = dxn.shape
    tm = min(tm, t)
    nc = N_SHARDS // 2

    def body(pre_ref, wout_ref, dx_ref, dhu_ref, dw_ref, dcb_ref):
        i = pl.program_id(1)

        @pl.when(i == 0)
        def _():
            dw_ref[...] = jnp.zeros_like(dw_ref)
            dcb_ref[...] = jnp.zeros_like(dcb_ref)

        hg, hu = pre_ref[0], pre_ref[1]
        sg = _sigmoid(hg)
        sl = hg * sg
        dxb = dx_ref[...].astype(BF16)
        dact = _dot(dxb, wout_ref[...], NT)
        dw_ref[...] += _dot((sl * hu).astype(BF16), dxb, TN)
        d_up = dact * sl
        d_gate = dact * hu * (sg * (1.0 + hg * (1.0 - sg)))
        for j, dv in enumerate((d_gate, d_up)):
            dhu_ref[j] = dv.astype(BF16)
            dcb_ref[j, 0:1, :] += jnp.sum(dv, axis=0, keepdims=True)

    return _call(
        body, [pre, w_out, dxn], grid=(nc, t // tm),
        in_specs=[pl.BlockSpec((2, None, tm, FF_SHARD), lambda c, i: (0, c, i, 0)),
                  pl.BlockSpec((FF_SHARD, d), lambda c, i: (c, 0)), pl.BlockSpec((tm, d), lambda c, i: (i, 0))],
        out_specs=[pl.BlockSpec((None, 2, tm, FF_SHARD), lambda c, i: (c, 0, i, 0)),
                   pl.BlockSpec((FF_SHARD, d), lambda c, i: (c, 0)),
                   pl.BlockSpec((None, 2, 8, FF_SHARD), lambda c, i: (c, 0, 0, 0))],
        out_shape=[jax.ShapeDtypeStruct((nc, 2, t, FF_SHARD), BF16), jax.ShapeDtypeStruct((D_FF, d), F32),
                   jax.ShapeDtypeStruct((nc, 2, 8, FF_SHARD), F32)],
        name=f"ffn{layer}_bwd_act", sem=("parallel", "arbitrary"), carry=carry)


def _ffn_bwd_in(dhu, a, cw, w_in, layer, tm=1024, carry=None, norm=None):
    nc, _, t, _ = dhu.shape
    d = D_MODEL
    tm = min(tm, t)
    last_blk = t // 16 - 1
    n_norm = 0 if norm is None else 3

    def body(dh_ref, nx_ref, a_ref, cw_ref, win_ref, *rest):
        norm_refs, (da_ref, o_ref, dcw_ref), dg_refs = rest[:n_norm], rest[n_norm:n_norm + 3], rest[n_norm + 3:]
        i, s = pl.program_id(0), pl.program_id(1)

        @pl.when(s == 0)
        def _():
            o_ref[...] = jnp.zeros_like(o_ref)

        @pl.when((s == 0) & (i == 0))
        def _():
            dcw_ref[...] = jnp.zeros_like(dcw_ref)

        keep = jnp.where(i == t // tm - 1, 0.0, 1.0)
        cw = cw_ref[...]
        sums = [None] * 3
        for r0 in range(0, tm, ROW_CHUNK):
            rows = slice(r0, r0 + ROW_CHUNK)
            if r0 + ROW_CHUNK == tm:
                win = jnp.concatenate([dh_ref[rows, :].astype(F32), nx_ref[...].astype(F32) * keep], axis=0)
            else:
                win = dh_ref[r0:r0 + ROW_CHUNK + HALO, :].astype(F32)
            n = ROW_CHUNK + HALO
            taps = (pltpu.roll(win, n - 2, 0)[:ROW_CHUNK],
                    pltpu.roll(win, n - 1, 0)[:ROW_CHUNK],
                    win[:ROW_CHUNK])
            da = (cw[0:1, :] * taps[0] + cw[1:2, :] * taps[1] + cw[2:3, :] * taps[2]).astype(BF16)
            da_ref[rows, :] = da
            o_ref[rows, :] += _dot(da, win_ref[...], NT)
            af = a_ref[rows, :].astype(F32)
            parts = [jnp.sum(taps[k] * af, axis=0, keepdims=True) for k in range(3)]
            sums = [p if q is None else q + p for q, p in zip(sums, parts)]
        for k in range(3):
            dcw_ref[pl.ds(s, 1), k:k + 1, :] += sums[k][None]

        if norm is not None:
            x_ref, g_ref, dres_ref = norm_refs
            dg_ref = dg_refs[0]

            @pl.when((s == 0) & (i == 0))
            def _():
                dg_ref[...] = jnp.zeros_like(dg_ref)

            @pl.when(s == N_SHARDS - 1)
            def _():
                xf = x_ref[...]
                r = lax.rsqrt(jnp.mean(xf * xf, axis=-1, keepdims=True) + EPS)
                xhat = xf * r
                dh = o_ref[...]
                dg_ref[0:1, :] += jnp.sum(dh * xhat, axis=0, keepdims=True)
                gy = dh * g_ref[...]
                o_ref[...] = dres_ref[...] + r * (gy - xhat * jnp.mean(gy * xhat, axis=-1, keepdims=True))

    row = pl.BlockSpec((tm, d), lambda i, s: (i, 0))
    norm_args = [] if norm is None else list(norm)
    norm_specs = [] if norm is None else [row, pl.BlockSpec((1, d), lambda i, s: (0, 0)), row]
    return _call(
        body, [dhu, dhu, a, cw, w_in] + norm_args, grid=(t // tm, N_SHARDS),
        in_specs=[pl.BlockSpec((None, None, tm, FF_SHARD), lambda i, s: (s % nc, s // nc, i, 0)),
                  pl.BlockSpec((None, None, 16, FF_SHARD),
                               lambda i, s: (s % nc, s // nc, jnp.minimum((i + 1) * (tm // 16), last_blk), 0)),
                  pl.BlockSpec((None, tm, FF_SHARD), lambda i, s: (s, i, 0)),
                  pl.BlockSpec((None, 8, FF_SHARD), lambda i, s: (s, 0, 0)),
                  pl.BlockSpec((None, d, FF_SHARD), lambda i, s: (s, 0, 0))] + norm_specs,
        out_specs=[pl.BlockSpec((None, tm, FF_SHARD), lambda i, s: (s, i, 0)), row,
                   pl.BlockSpec((N_SHARDS, 8, FF_SHARD), lambda i, s: (0, 0, 0))]
        + ([] if norm is None else [pl.BlockSpec((8, d), lambda i, s: (0, 0))]),
        out_shape=[jax.ShapeDtypeStruct((N_SHARDS, t, FF_SHARD), BF16), jax.ShapeDtypeStruct((t, d), F32),
                   jax.ShapeDtypeStruct((N_SHARDS, 8, FF_SHARD), F32)]
        + ([] if norm is None else [jax.ShapeDtypeStruct((8, d), F32)]),
        name=f"ffn{layer}_bwd_in", sem=("arbitrary", "arbitrary"), carry=carry)


def _ffn_wgrad_in(hf, da, layer, carry=None):
    t, d = hf.shape
    return _mm(
        da, hf, pl.BlockSpec((None, t, FF_SHARD), lambda s, j, kk: (s, 0, 0)),
        pl.BlockSpec((t, d), lambda s, j, kk: (0, 0)),
        pl.BlockSpec((None, FF_SHARD, d), lambda s, j, kk: (s, 0, 0)),
        jax.ShapeDtypeStruct((N_SHARDS, FF_SHARD, d), F32), (N_SHARDS, 1, 1), TN, f"ffn{layer}_wgrad_in",
        carry=carry)


Q_PER_KV = N_Q_HEADS // N_KV_HEADS
GROUP_ROWS = Q_PER_KV * CHUNK


def _lane_half():
    return lax.broadcasted_iota(jnp.int32, (CHUNK, LANES), 1) < HEAD_DIM


def _fill_attn_bias(bias_s):
    tq = lax.broadcasted_iota(jnp.int32, (GROUP_ROWS, 2 * CHUNK), 0) & (CHUNK - 1)
    jk = lax.broadcasted_iota(jnp.int32, (GROUP_ROWS, 2 * CHUNK), 1)
    dist = tq + CHUNK - jk
    window = (dist >= 0) & (dist < CHUNK)
    distf = dist.astype(F32)
    for kvh in range(N_KV_HEADS):
        alibi = _per_head_column([-SLOPES[h] for h in range(Q_PER_KV * kvh, Q_PER_KV * (kvh + 1))]) * distf
        bias_s[0, kvh] = jnp.where(window & (jk >= CHUNK), alibi, NEG_BIG)
        bias_s[1, kvh] = jnp.where(window, alibi, NEG_BIG)


def _per_head_column(values):
    r = lax.broadcasted_iota(jnp.int32, (GROUP_ROWS, 1), 0)
    col = jnp.full((GROUP_ROWS, 1), values[Q_PER_KV - 1], F32)
    for j in range(Q_PER_KV - 2, -1, -1):
        col = jnp.where(r < (j + 1) * CHUNK, values[j], col)
    return col


def _half_sum(x, lo):
    s_lo = jnp.sum(jnp.where(lo, x, 0.0), axis=-1, keepdims=True)
    s_hi = jnp.sum(jnp.where(lo, 0.0, x), axis=-1, keepdims=True)
    return jnp.where(lo, s_lo, s_hi)


def _stack_heads(pairs, lo):
    zero = jnp.zeros_like(pairs[0])
    return jnp.concatenate([jnp.where(lo, pairs[0], zero), jnp.where(lo, zero, pairs[0]),
                            jnp.where(lo, pairs[1], zero), jnp.where(lo, zero, pairs[1])], axis=0)


def _unstack_heads(stacked, lo):
    return (jnp.where(lo, stacked[0:CHUNK], stacked[CHUNK:2 * CHUNK]),
            jnp.where(lo, stacked[2 * CHUNK:3 * CHUNK], stacked[3 * CHUNK:]))


def _attn_probs(qs, kn, bias, sink_col):
    s = _dot(qs, kn, NT) * (HEAD_DIM ** -0.5) + bias
    m = jnp.maximum(jnp.max(s, axis=-1, keepdims=True), sink_col)
    e = jnp.exp(s - m)
    den = jnp.sum(e, axis=-1, keepdims=True) + jnp.exp(sink_col - m)
    return e * (1.0 / den), m, den


def _attn_fwd(qraw, kvd, gq, gk, sinks, carry=None):
    t, d = qraw.shape
    nb = t // CHUNK

    def body(sink_ref, q_ref, cur_ref, prev_ref, gq_ref, gk_ref, o_ref, bias_s):
        n = pl.program_id(0)

        @pl.when(n == 0)
        def _():
            _fill_attn_bias(bias_s)

        lo = _lane_half()
        which = jnp.where(n == 0, 0, 1)
        gq_v, gk_v = gq_ref[...], gk_ref[...]
        for kvh in range(N_KV_HEADS):
            ks = slice(kvh * LANES, (kvh + 1) * LANES)
            vs = slice(4 * LANES + kvh * LANES, 4 * LANES + (kvh + 1) * LANES)
            kraw = jnp.concatenate([prev_ref[:, ks], cur_ref[:, ks]], axis=0)
            rk = lax.rsqrt(jnp.mean(kraw * kraw, axis=-1, keepdims=True) + EPS)
            kn = (kraw * rk * gk_v).astype(BF16)
            vv = jnp.concatenate([prev_ref[:, vs], cur_ref[:, vs]], axis=0).astype(BF16)
            qn = []
            for p in range(2):
                qp = q_ref[:, (2 * kvh + p) * LANES:(2 * kvh + p + 1) * LANES]
                r = lax.rsqrt(_half_sum(qp * qp, lo) * (1.0 / HEAD_DIM) + EPS)
                qn.append(qp * r * gq_v)
            heads = range(Q_PER_KV * kvh, Q_PER_KV * (kvh + 1))
            pf, _, _ = _attn_probs(_stack_heads(qn, lo).astype(BF16), kn, bias_s[which, kvh],
                                   _per_head_column([sink_ref[h] for h in heads]))
            for p, o_pair in enumerate(_unstack_heads(_dot(pf.astype(BF16), vv), lo)):
                o_ref[:, (2 * kvh + p) * LANES:(2 * kvh + p + 1) * LANES] = o_pair.astype(BF16)

    blk = lambda f: pl.BlockSpec((CHUNK, d), f)
    vec = pl.BlockSpec((1, LANES), lambda n: (0, 0))
    return _call(
        body, [sinks, qraw, kvd, kvd, gq, gk], grid=(nb,),
        in_specs=[pl.BlockSpec(memory_space=pltpu.SMEM), blk(lambda n: (n, 0)), blk(lambda n: (n, 0)),
                  blk(lambda n: (jnp.maximum(n - 1, 0), 0)), vec, vec],
        out_specs=[blk(lambda n: (n, 0))], out_shape=[jax.ShapeDtypeStruct((t, d), BF16)],
        scratch=[pltpu.VMEM((2, N_KV_HEADS, GROUP_ROWS, 2 * CHUNK), F32)], name="attn_fwd", sem=("arbitrary",),
        carry=carry)[0]


def _attn_bwd(qraw, kvd, d_o, gq, gk, sinks, carry=None):
    t, d = qraw.shape
    nb = t // CHUNK

    def body(sink_ref, q_ref, cur_ref, prev_ref, do_ref, gq_ref, gk_ref,
             dq_ref, dkv_ref, dsink_ref, dgq_ref, dgk_ref, carry_s, pp_s, cp_s, bias_s):
        n = pl.program_id(0)

        @pl.when(n == 0)
        def _():
            carry_s[...] = jnp.zeros_like(carry_s)
            dsink_ref[...] = jnp.zeros_like(dsink_ref)
            dgq_ref[...] = jnp.zeros_like(dgq_ref)
            dgk_ref[...] = jnp.zeros_like(dgk_ref)
            _fill_attn_bias(bias_s)

        @pl.when(n < nb)
        def _():
            lo = _lane_half()
            which = jnp.where(n == 0, 0, 1)
            gq_v, gk_v = gq_ref[...], gk_ref[...]
            for kvh in range(N_KV_HEADS):
                ks = slice(kvh * LANES, (kvh + 1) * LANES)
                vs = slice(4 * LANES + kvh * LANES, 4 * LANES + (kvh + 1) * LANES)
                kraw = jnp.concatenate([prev_ref[:, ks], cur_ref[:, ks]], axis=0)
                rk = lax.rsqrt(jnp.mean(kraw * kraw, axis=-1, keepdims=True) + EPS)
                khat = kraw * rk
                kn = (khat * gk_v).astype(BF16)
                vv = jnp.concatenate([prev_ref[:, vs], cur_ref[:, vs]], axis=0).astype(BF16)
                cols = [slice((2 * kvh + p) * LANES, (2 * kvh + p + 1) * LANES) for p in range(2)]
                rq, qhat = [], []
                for p in range(2):
                    qp = q_ref[:, cols[p]]
                    rq.append(lax.rsqrt(_half_sum(qp * qp, lo) * (1.0 / HEAD_DIM) + EPS))
                    qhat.append(qp * rq[p])
                heads = range(Q_PER_KV * kvh, Q_PER_KV * (kvh + 1))
                qs = _stack_heads([qhat[p] * gq_v for p in range(2)], lo).astype(BF16)
                dos = _stack_heads([do_ref[:, cols[p]] for p in range(2)], lo)
                sink_col = _per_head_column([sink_ref[h] for h in heads])
                pf, m, den = _attn_probs(qs, kn, bias_s[which, kvh], sink_col)
                dp = _dot(dos, vv, NT)
                delta = jnp.sum(pf * dp, axis=-1, keepdims=True)
                sink_delta = jnp.exp(sink_col - m) / den * delta
                for j, h in enumerate(heads):
                    dsink_ref[h:h + 1, :] -= jnp.broadcast_to(
                        jnp.sum(sink_delta[j * CHUNK:(j + 1) * CHUNK], axis=0, keepdims=True), (1, LANES))
                ds = (pf * (dp - delta) * (HEAD_DIM ** -0.5)).astype(BF16)
                dkn = _dot(ds, qs, TN)
                dvb = _dot(pf.astype(BF16), dos, TN)
                for p, dqn in enumerate(_unstack_heads(_dot(ds, kn), lo)):
                    dgq_ref[0:1, :] += jnp.sum(dqn * qhat[p], axis=0, keepdims=True)
                    gy = dqn * gq_v
                    mq = _half_sum(gy * qhat[p], lo) * (1.0 / HEAD_DIM)
                    dq_ref[:, cols[p]] = (rq[p] * (gy - qhat[p] * mq)).astype(BF16)
                dgk_ref[0:1, :] += jnp.sum(dkn * khat, axis=0, keepdims=True)
                gyk = dkn * gk_v
                dkraw = rk * (gyk - khat * jnp.mean(gyk * khat, axis=-1, keepdims=True))
                pp_s[:, ks] = dkraw[:CHUNK]
                cp_s[:, ks] = dkraw[CHUNK:]
                pp_s[:, vs] = dvb[:CHUNK]
                cp_s[:, vs] = dvb[CHUNK:]
            dkv_ref[...] = (carry_s[...] + pp_s[...]).astype(BF16)
            carry_s[...] = cp_s[...]

        @pl.when(n == nb)
        def _():
            dkv_ref[...] = carry_s[...].astype(BF16)

    blk = lambda f: pl.BlockSpec((CHUNK, d), f)
    vec = pl.BlockSpec((1, LANES), lambda n: (0, 0))
    cur = lambda n: (jnp.minimum(n, nb - 1), 0)
    prev = lambda n: (jnp.maximum(jnp.minimum(n, nb - 1) - 1, 0), 0)
    small = lambda r: pl.BlockSpec((r, LANES), lambda n: (0, 0))
    return _call(
        body, [sinks, qraw, kvd, kvd, d_o, gq, gk], grid=(nb + 1,),
        in_specs=[pl.BlockSpec(memory_space=pltpu.SMEM), blk(cur), blk(cur), blk(prev), blk(cur), vec, vec],
        out_specs=[blk(cur), blk(lambda n: (jnp.maximum(n - 1, 0), 0)), small(N_Q_HEADS), small(8), small(8)],
        out_shape=[jax.ShapeDtypeStruct((t, d), BF16), jax.ShapeDtypeStruct((t, d), BF16),
                   jax.ShapeDtypeStruct((N_Q_HEADS, LANES), F32), jax.ShapeDtypeStruct((8, LANES), F32),
                   jax.ShapeDtypeStruct((8, LANES), F32)],
        scratch=[pltpu.VMEM((CHUNK, d), F32)] * 3 + [pltpu.VMEM((2, N_KV_HEADS, GROUP_ROWS, 2 * CHUNK), F32)],
        name="attn_bwd", sem=("arbitrary",), carry=carry)


def _adamw_math(g, w, m, v):
    m = ADAM_B1 * m + (1.0 - ADAM_B1) * g
    v = ADAM_B2 * v + (1.0 - ADAM_B2) * (g * g)
    m_hat = m / (1.0 - ADAM_B1 ** ADAM_STEP)
    v_hat = v / (1.0 - ADAM_B2 ** ADAM_STEP)
    delta = -ADAM_LR * (m_hat / (jnp.sqrt(v_hat) + ADAM_EPS) + ADAM_WD * w)
    return delta, m, v


def _row_tile(r, cap=128):
    for tr in range(min(r, cap), 0, -1):
        if r % tr == 0 and (tr % 8 == 0 or tr == r):
            return tr
    return r


def _chip_sum(grad, recv, place, name, wire_dtype):
    _, r, c = grad.shape
    tr = _row_tile(r, 256)

    def body(pl_ref, g_ref, a_ref, p_ref):
        p_ref[...] = (g_ref[...] + a_ref[...]).astype(p_ref.dtype)

    other = lambda rel, pr: pr[0] ^ (rel + 1)
    return pl.pallas_call(
        body,
        grid_spec=pltpu.PrefetchScalarGridSpec(
            num_scalar_prefetch=1, grid=(3, r // tr),
            in_specs=[pl.BlockSpec((None, None, tr, c), lambda rel, i, pr: (other(rel, pr), pr[1], i, 0)),
                      pl.BlockSpec((None, tr, c), lambda rel, i, pr: (other(rel, pr), i, 0))],
            out_specs=pl.BlockSpec((None, tr, c), lambda rel, i, pr: (other(rel, pr), i, 0))),
        out_shape=jax.ShapeDtypeStruct((4, r, c), wire_dtype), name=name, compiler_params=_params(),
    )(place, grad.reshape(4, 2, r, c), recv)


def _adamw_sharded(grad, recv, others, place, w, m, v, name, layer=None, fill=None):
    r, c = w.shape[-2:]
    tr = _row_tile(r)

    def body(pl_ref, g_ref, a_ref, oth_ref, w_ref, m_ref, v_ref, *rest):
        g_out, d_out, nm_out, nv_out = rest[-4:]
        g = g_ref[...] + a_ref[...]
        for k in range(3):
            g = g + oth_ref[k].astype(F32)
        delta, nm, nv = _adamw_math(g, w_ref[...], m_ref[...], v_ref[...])
        g_out[...] = g
        d_out[...] = delta
        nm_out[...] = nm
        nv_out[...] = nv

    if layer is None:
        row = pl.BlockSpec((tr, c), lambda i, pr: (i, 0))
    else:
        row = pl.BlockSpec((None, tr, c), lambda i, pr: (layer, i, 0))
    n_fill = 0 if fill is None else 4
    in_specs = [pl.BlockSpec((None, None, tr, c), lambda i, pr: (pr[0], pr[1], i, 0)),
                pl.BlockSpec((None, tr, c), lambda i, pr: (pr[0], i, 0)),
                pl.BlockSpec((3, tr, c), lambda i, pr: (0, i, 0)), row, row, row]
    in_specs += [pl.BlockSpec(memory_space=pl.ANY)] * n_fill
    return pl.pallas_call(
        body,
        grid_spec=pltpu.PrefetchScalarGridSpec(
            num_scalar_prefetch=1, grid=(r // tr,), in_specs=in_specs, out_specs=[row] * 4),
        out_shape=[jax.ShapeDtypeStruct(w.shape, F32)] * 4, name=name, compiler_params=_params(),
        input_output_aliases={7 + j: j for j in range(n_fill)},
    )(place, grad.reshape(4, 2, r, c), recv, others, w, m, v, *([] if fill is None else fill))


def _sum_devices(parts, name):
    def body(p_ref, o_ref):
        total = p_ref[0]
        for k in range(1, N_SHARDS):
            total = total + p_ref[k]
        o_ref[...] = total

    return pl.pallas_call(body, out_shape=jax.ShapeDtypeStruct(parts.shape[1:], F32), name=name)(parts)


def _adamw_summed(parts, ws, ms, vs, name):
    n = len(parts)

    def body(*refs):
        p_refs, w_refs, m_refs, v_refs = refs[:n], refs[n:2 * n], refs[2 * n:3 * n], refs[3 * n:4 * n]
        o_refs = refs[4 * n:]
        for i in range(n):
            g = p_refs[i][0]
            for k in range(1, N_SHARDS):
                g = g + p_refs[i][k]
            delta, nm, nv = _adamw_math(g, w_refs[i][...], m_refs[i][...], v_refs[i][...])
            o_refs[4 * i][...] = g
            o_refs[4 * i + 1][...] = delta
            o_refs[4 * i + 2][...] = nm
            o_refs[4 * i + 3][...] = nv

    shapes = [jax.ShapeDtypeStruct(w.shape, F32) for w in ws for _ in range(4)]
    outs = pl.pallas_call(body, out_shape=shapes, name=name, compiler_params=_params())(*parts, *ws, *ms, *vs)
    return [outs[4 * i:4 * i + 4] for i in range(n)]


def _dup_heads(w):
    lead = w.shape[:-1]
    w4 = w.reshape(lead + (N_KV_HEADS, 1, HEAD_DIM))
    return jnp.broadcast_to(w4, lead + (N_KV_HEADS, 2, HEAD_DIM)).reshape(lead + (N_KV_HEADS * LANES,))


def _fold_heads(g):
    lead = g.shape[:-1]
    return g.reshape(lead + (N_KV_HEADS, 2, HEAD_DIM)).sum(axis=-2).reshape(lead + (N_KV_HEADS * HEAD_DIM,))


def kernel(x, a_norm, a_w_in, a_v_norm, a_w_s, a_b_s, a_w_out, f_norm, f_w_in, f_conv_w, f_conv_b, f_w_out, kv_norm, w_kv, k_norm, b_norm, b_w_q, b_q_norm, b_sinks, b_w_o, loss_target, m_a_norm, m_a_w_in, m_a_v_norm, m_a_w_s, m_a_b_s, m_a_w_out, m_f_norm, m_f_w_in, m_f_conv_w, m_f_conv_b, m_f_w_out, m_kv_norm, m_w_kv, m_k_norm, m_b_norm, m_b_w_q, m_b_q_norm, m_b_sinks, m_b_w_o, v_a_norm, v_a_w_in, v_a_v_norm, v_a_w_s, v_a_b_s, v_a_w_out, v_f_norm, v_f_w_in, v_f_conv_w, v_f_conv_b, v_f_w_out, v_kv_norm, v_w_kv, v_k_norm, v_b_norm, v_b_w_q, v_b_q_norm, v_b_sinks, v_b_w_o):
    d = D_MODEL
    xi, yi, ci = _coords()
    place = jnp.stack([2 * xi + yi, ci]).astype(jnp.int32)
    bf = lambda a: a.astype(BF16)
    row = lambda v_: v_.reshape(1, -1)
    x0, target = x[0], loss_target[0]
    t = x0.shape[0]
    res = {}

    red = {}

    def to_sibling(grads, wire=BF16):
        for k, g in grads.items():
            red[k] = dict(grad=g, wire=wire)
        ex = _ToSibling(list(grads.values()))
        ex.names = list(grads)
        return ex

    def to_chips(ex):
        for k, a in zip(ex.names, ex.results):
            red[k]["recv"] = a
            red[k]["psum"] = _chip_sum(red[k]["grad"], a, place, f"chip_sum_{k}", red[k]["wire"])
        nxt = _ToChips([red[k]["psum"] for k in ex.names])
        nxt.names = ex.names
        return nxt

    def landed(ex):
        for k, b in zip(ex.names, ex.results):
            red[k]["others"] = b

    def halves(ex, first_rows):
        parts = []
        for r0, nr in ((0, first_rows), (first_rows, ex.srcs[0].shape[1] - first_rows)):
            part = _ToChips(ex.srcs, rows=(r0, nr))
            part.names = ex.names
            parts.append(part)
        return parts

    def landed_halves(parts):
        for j, k in enumerate(parts[0].names):
            red[k]["others"] = jnp.concatenate([p.results[j] for p in parts], axis=1)

    def update(k, w, m, v, layer=None, fill=None):
        r = red[k]
        return _adamw_sharded(r["grad"], r["recv"], r["others"], place, w, m, v,
                              f"adamw_{k}", layer=layer, fill=fill)

    g_a_in, g_a_out, g_a_norm, g_a_v_norm, g_conv = _exchange_alone(
        _Gather([bf(a_w_in[0]), bf(a_w_out[0]), a_norm, a_v_norm, f_conv_w.reshape(6, FF_SHARD)]), "gather_first")
    a_norm_full, a_v_norm_full = g_a_norm.reshape(1, d), g_a_v_norm.reshape(1, d)
    conv_w = lax.reduce_precision(g_conv.reshape(N_SHARDS, 2, 3, FF_SHARD), 8, 7)
    cw = jnp.pad(jnp.transpose(conv_w, (1, 0, 2, 3)), ((0, 0), (0, 0), (0, 5), (0, 0)))
    w_a_in_flat = jnp.transpose(g_a_in, (1, 0, 2)).reshape(d, 2 * d)
    cb = f_conv_b.reshape(2, N_SHARDS, 1, FF_SHARD)
    tri = jnp.tril(jnp.ones((CHUNK, CHUNK), dtype=bool))
    w_causal = jnp.where(tri[None], a_w_s[0], 0.0).astype(BF16)
    w_causal_t = jnp.transpose(w_causal, (0, 2, 1))
    b_sb = jnp.broadcast_to(a_b_s[0][:, :, None], (N_GROUPS, CHUNK, CHUNK))
    w_a_out = g_a_out.reshape(d, d)
    gq = jnp.tile(b_q_norm.reshape(1, HEAD_DIM), (1, 2))
    gk = jnp.tile(k_norm.reshape(1, HEAD_DIM), (1, 2))
    sinks = b_sinks.reshape(N_Q_HEADS)

    ex = _Gather([bf(f_w_in[0]), bf(f_w_out[0])])
    zpre, x1, h1 = _sgu_fwd(x0, a_norm_full, g_a_in, a_v_norm_full, w_causal, b_sb, w_a_out, carry=ex)
    w_in0, w_out0 = ex.results[0], ex.results[1].reshape(D_FF, d)
    ex = _Gather([bf(w_kv), bf(b_w_q[0]), bf(b_w_o[0]), bf(f_w_in[1])], relay=False, early=True)
    x2, hf0, a0, pre0, hk, hq = _ffn_fwd(x1, f_norm[0:1], w_in0, cw[0], cb[0], w_out0, 0, carry=ex,
                                         next_gains=[row(kv_norm), b_norm])
    kv_full = ex.results[0].reshape(d, 2 * N_KV_HEADS * HEAD_DIM)
    w_q, w_o = ex.results[1].reshape(d, d), ex.results[2].reshape(d, d)
    w_in1 = ex.results[3]
    half = N_KV_HEADS * HEAD_DIM
    w_kv_dup = jnp.concatenate([_dup_heads(kv_full[:, :half]), _dup_heads(kv_full[:, half:])], axis=1)
    kvd = _mm_rows(hk, w_kv_dup, F32, "kv_proj")
    qraw = _mm_rows(hq, w_q, F32, "q_proj")
    ex = _Gather([bf(f_w_out[1])], relay=False, early=True)
    o = _attn_fwd(qraw, kvd, gq, gk, sinks, carry=ex)
    w_out1 = ex.results[0].reshape(D_FF, d)
    x3 = _mm_rows(o, w_o, F32, "o_proj", res=x2)
    _, hf1, a1, pre1, dy, loss_lanes = _ffn_fwd(x3, f_norm[1:2], w_in1, cw[1], cb[1], w_out1, 1, loss_target=target)

    dhu1, dw_out1, dcb1 = _ffn_bwd_act(pre1, w_out1, dy, 1)
    ex = to_sibling({"f_w_out1": dw_out1.reshape(N_SHARDS, D_FF // N_SHARDS, d)})
    da1, dx3, dcw1, dgf1 = _ffn_bwd_in(dhu1, a1, cw[1], w_in1, 1, carry=ex, norm=(x3, f_norm[1:2], dy))
    ex = to_chips(ex)
    dw_in1 = _ffn_wgrad_in(hf1, da1, 1, carry=ex)
    landed(ex)
    ex = to_sibling({"f_w_in1": dw_in1})
    d_o = _mm_rows(dx3, w_o, BF16, "o_proj_bwd", trans_w=True, carry=ex)
    ex = to_chips(ex)
    dw_o = _mm_wgrad(o, dx3, "o_wgrad").reshape(N_SHARDS, d // N_SHARDS, d)
    dq, dkv, dsink, dgq, dgk = _attn_bwd(qraw, kvd, d_o, gq, gk, sinks, carry=ex)
    landed(ex)
    dw_q = _mm_wgrad(hq, dq, "q_wgrad").reshape(N_SHARDS, d // N_SHARDS, d)
    dw_kv_dup = _mm_wgrad(hk, dkv, "kv_wgrad")
    dw_kv = jnp.concatenate(
        [_fold_heads(dw_kv_dup[:, :4 * LANES]), _fold_heads(dw_kv_dup[:, 4 * LANES:])], axis=1
    ).reshape(N_SHARDS, d // N_SHARDS, 2 * N_KV_HEADS * HEAD_DIM)
    ex = to_sibling({"b_w_o": dw_o, "b_w_q": dw_q, "w_kv": dw_kv})
    dx2, dg2 = _rms_bwd(x2, [row(kv_norm), b_norm], [dkv, dq], dx3, "kvq_norm_bwd", tm=512, carry=ex,
                        through=[w_kv_dup, w_q])
    ex = to_chips(ex)
    dhu0, dw_out0, dcb0 = _ffn_bwd_act(pre0, w_out0, dx2, 0, carry=ex)
    landed(ex)
    ex = to_sibling({"f_w_out0": dw_out0.reshape(N_SHARDS, D_FF // N_SHARDS, d)})
    da0, dhf0, dcw0 = _ffn_bwd_in(dhu0, a0, cw[0], w_in0, 0, tm=2048, carry=ex)
    ex = to_chips(ex)
    dw_in0 = _ffn_wgrad_in(hf0, da0, 0, carry=ex)
    landed(ex)
    ex = to_sibling({"f_w_in0": dw_in0})
    dx1, dgf0 = _rms_bwd(x1, [f_norm[0:1]], [dhf0], dx2, "f0_norm_bwd", carry=ex)
    ex_lo, ex_hi = halves(to_chips(ex), 448)
    dz, y, dwc, dbs, dgv = _sgu_bwd(dx1, zpre, w_a_out, a_v_norm_full, w_causal, w_causal_t, b_sb, carry=ex_lo)
    dw_a_out = _mm_wgrad(y, dx1, "a_out_wgrad").reshape(N_SHARDS, d // N_SHARDS, d)
    nsub = g_a_in.shape[2]
    dw_a_in = _mm(
        h1, dz, pl.BlockSpec((t, d), lambda s, j, kk: (0, 0)), pl.BlockSpec((t, nsub), lambda s, j, kk: (0, s)),
        pl.BlockSpec((None, d, nsub), lambda s, j, kk: (s, 0, 0)), jax.ShapeDtypeStruct((N_SHARDS, d, nsub), F32),
        (N_SHARDS, 1, 1), TN, "a_in_wgrad", carry=ex_hi)
    landed_halves([ex_lo, ex_hi])

    def bias_grad(dcb):
        return jnp.transpose(dcb[:, :, 0, :], (1, 0, 2)).reshape(-1)

    g_conv_w = jnp.concatenate([dcw0[:, 0:3, :], dcw1[:, 0:3, :]], axis=1)
    g_a_v_norm = dgv[0].reshape(N_SHARDS, 1, LANES)
    rep = ["a_w_s", "a_b_s", "f_norm", "f_conv_b", "kv_norm", "k_norm", "b_norm", "b_q_norm", "b_sinks"]
    rep_g = dict(
        a_w_s=dwc.reshape(N_GROUPS * CHUNK, CHUNK), a_b_s=dbs[:, :, 0], f_norm=jnp.stack([dgf0[0], dgf1[0]]),
        f_conv_b=jnp.stack([bias_grad(dcb0), bias_grad(dcb1)]), kv_norm=dg2[0:1],
        k_norm=(dgk[0, :HEAD_DIM] + dgk[0, HEAD_DIM:])[None], b_norm=dg2[1:2],
        b_q_norm=(dgq[0, :HEAD_DIM] + dgq[0, HEAD_DIM:])[None], b_sinks=dsink[:, 0][None])
    ex_big = to_sibling({"a_w_out": dw_a_out, "a_w_in": dw_a_in})
    ex_small = to_sibling({"a_v_norm": g_a_v_norm, "f_conv_w": g_conv_w}, wire=F32)
    ex_rep = _Gather([rep_g[k] for k in rep] + [loss_lanes], relay=False)
    together = _Together([ex_big, ex_small, ex_rep])
    dh1 = _mm_rows(dz, w_a_in_flat, F32, "a_in_bwd", trans_w=True, carry=together)
    together.spread()
    ex_big, ex_small = to_chips(ex_big), to_chips(ex_small)
    together = _Together([ex_big, ex_small])
    grad_x, dg0 = _rms_bwd(x0, [a_norm_full], [dh1], dx1, "a_norm_bwd", carry=together)
    together.spread()
    landed(ex_big)
    landed(ex_small)
    (a_norm_parts,) = _exchange_alone(_ToOwners([dg0[0].reshape(N_SHARDS, 1, LANES)]), "a_norm_to_owners")

    res["f_w_out"] = update("f_w_out1", f_w_out, m_f_w_out, v_f_w_out, layer=1)
    w_in_t = [jnp.swapaxes(a_, 1, 2) for a_ in (f_w_in, m_f_w_in, v_f_w_in)]
    res["f_w_in"] = update("f_w_in1", *w_in_t, layer=1)
    res["b_w_o"] = update("b_w_o", b_w_o, m_b_w_o, v_b_w_o, layer=0)
    res["b_w_q"] = update("b_w_q", b_w_q, m_b_w_q, v_b_w_q, layer=0)
    res["w_kv"] = update("w_kv", w_kv, m_w_kv, v_w_kv)
    res["f_w_out"] = update("f_w_out0", f_w_out, m_f_w_out, v_f_w_out, layer=0, fill=res["f_w_out"])
    res["f_w_in"] = [jnp.swapaxes(o_, 1, 2) for o_ in update("f_w_in0", *w_in_t, layer=0, fill=res["f_w_in"])]
    res["a_w_out"] = update("a_w_out", a_w_out, m_a_w_out, v_a_w_out, layer=0)
    res["a_w_in"] = update("a_w_in", a_w_in, m_a_w_in, v_a_w_in, layer=0)
    res["a_v_norm"] = update("a_v_norm", a_v_norm, m_a_v_norm, v_a_v_norm)
    res["f_conv_w"] = [o_.reshape(f_conv_w.shape) for o_ in update(
        "f_conv_w", f_conv_w.reshape(6, FF_SHARD), m_f_conv_w.reshape(6, FF_SHARD), v_f_conv_w.reshape(6, FF_SHARD))]

    rep_w = dict(a_w_s=a_w_s, a_b_s=a_b_s, f_norm=f_norm, f_conv_b=f_conv_b, kv_norm=kv_norm, k_norm=k_norm,
                 b_norm=b_norm, b_q_norm=b_q_norm, b_sinks=b_sinks, a_norm=a_norm)
    rep_m = dict(a_w_s=m_a_w_s, a_b_s=m_a_b_s, f_norm=m_f_norm, f_conv_b=m_f_conv_b, kv_norm=m_kv_norm,
                 k_norm=m_k_norm, b_norm=m_b_norm, b_q_norm=m_b_q_norm, b_sinks=m_b_sinks, a_norm=m_a_norm)
    rep_v = dict(a_w_s=v_a_w_s, a_b_s=v_a_b_s, f_norm=v_f_norm, f_conv_b=v_f_conv_b, kv_norm=v_kv_norm,
                 k_norm=v_k_norm, b_norm=v_b_norm, b_q_norm=v_b_q_norm, b_sinks=v_b_sinks, a_norm=v_a_norm)
    keys = rep + ["a_norm"]
    loss = _sum_devices(ex_rep.results[-1], "loss_sum")[0, 0]
    parts = ex_rep.results[:-1] + [a_norm_parts]
    as2d = lambda a, p: a.reshape(p.shape[1:])
    rep_outs = _adamw_summed(parts, [as2d(rep_w[k], p) for k, p in zip(keys, parts)],
                             [as2d(rep_m[k], p) for k, p in zip(keys, parts)],
                             [as2d(rep_v[k], p) for k, p in zip(keys, parts)], "adamw_replicated")
    for j, key in enumerate(keys):
        res[key] = [o_.reshape(rep_w[key].shape) for o_ in rep_outs[j]]

    order = ["a_norm", "a_w_in", "a_v_norm", "a_w_s", "a_b_s", "a_w_out", "f_norm", "f_w_in", "f_conv_w", "f_conv_b",
             "f_w_out", "kv_norm", "w_kv", "k_norm", "b_norm", "b_w_q", "b_q_norm", "b_sinks", "b_w_o"]
    outs = [loss, grad_x[None]]
    for j in range(4):
        outs += [res[k][j] for k in order]
    return tuple(outs)
```

```python
import jax
import jax.numpy as jnp
from jax import lax
from jax.experimental import pallas as pl
from jax.experimental.pallas import tpu as pltpu

F32 = jnp.float32
BF16 = jnp.bfloat16
EPS = 1e-6
D_MODEL = 1024
CHUNK = 128
N_GROUPS = 8
N_SHARDS = 8
HEAD_DIM = 64
N_Q_HEADS = 16
N_KV_HEADS = 4
D_FF = 2816
FF_SHARD = 2 * D_FF // N_SHARDS
LANES = 128
NEG_BIG = -1e30
ADAM_LR = 0.001
ADAM_B1 = 0.9
ADAM_B2 = 0.999
ADAM_EPS = 1e-08
ADAM_WD = 0.01
ADAM_STEP = 10
VMEM_LIMIT_BYTES = 56 * 1024 * 1024
MESH = pl.DeviceIdType.MESH

NN = (((1,), (0,)), ((), ()))
NT = (((1,), (1,)), ((), ()))
TN = (((0,), (0,)), ((), ()))
SLOPES = tuple(2.0 ** (-8.0 * (h + 1) / N_Q_HEADS) for h in range(N_Q_HEADS))


def _params(sem=None):
    return pltpu.CompilerParams(dimension_semantics=sem, vmem_limit_bytes=VMEM_LIMIT_BYTES)


def _dot(a, b, dims=NN):
    return lax.dot_general(a, b, dims, preferred_element_type=F32)


def _sigmoid(x):
    return 1.0 / (1.0 + jnp.exp(-x))


def _gelu_parts(z):
    cdf = 0.5 * (1.0 + lax.erf(z * (2.0 ** -0.5)))
    pdf = jnp.exp(-0.5 * z * z) * 0.3989422804014327
    return cdf, pdf


def _coords():
    return lax.axis_index("x"), lax.axis_index("y"), lax.axis_index("c")


class _Gather:
    def __init__(self, srcs, relay=True, early=False):
        self.srcs = list(srcs)
        self.early = early
        n = len(self.srcs)
        self.relayed = [relay and s.shape[0] % 32 == 0 for s in self.srcs]
        self.out_shapes = [jax.ShapeDtypeStruct((N_SHARDS,) + s.shape, s.dtype) for s in self.srcs]
        self.sems = [pltpu.SemaphoreType.DMA((n, 9)), pltpu.SemaphoreType.DMA((n, 9)), pltpu.SemaphoreType.DMA((n,))]

    def _plan(self, src, dst, sems):
        send_sems, recv_sems, local_sems = sems
        x, y, c = _coords()
        n = len(src)

        def rows(e, dev, half=None):
            block = dst[e].at[4 * dev[0] + 2 * dev[1] + dev[2]]
            if half is None:
                return block
            nr = self.srcs[e].shape[0] // 2
            return block.at[pl.ds(half * nr, nr)]

        def copy(e, slot, block, to, half=None, from_own=False):
            return pltpu.make_async_remote_copy(
                src_ref=src[e] if from_own else rows(e, block, half), dst_ref=rows(e, block, half),
                send_sem=send_sems.at[e, slot], recv_sem=recv_sems.at[e, slot], device_id=to, device_id_type=MESH)

        return n, x, y, c, rows, copy, local_sems

    def start(self, src, dst, sems):
        n, x, y, c, rows, copy, local_sems = self._plan(src, dst, sems)
        me = (x, y, c)
        for e in range(n):
            pltpu.make_async_copy(src[e], rows(e, me), local_sems.at[e]).start()
            copy(e, 0, me, (x, y, 1 - c), from_own=True).start()
            copy(e, 1, me, (1 - x, y, c), from_own=True).start()
            copy(e, 2, me, (x, 1 - y, c), from_own=True).start()
            if not self.relayed[e]:
                copy(e, 3, me, (1 - x, 1 - y, c), from_own=True).start()

    def pass_on(self, src, dst, sems, wait=True):
        n, x, y, c, rows, copy, local_sems = self._plan(src, dst, sems)
        me, sibling = (x, y, c), (x, y, 1 - c)
        over_x, over_y, diagonal = (1 - x, y, c), (x, 1 - y, c), (1 - x, 1 - y, c)
        sent = []

        def arrived(cp):
            if wait:
                cp.wait_recv()

        def send(cp):
            if wait:
                cp.start()
            sent.append(cp)

        for slot, owner, onward, half in ((1, over_x, over_y, 0), (2, over_y, over_x, 1)):
            for e in range(n):
                arrived(copy(e, slot, owner, me))
                if self.relayed[e]:
                    send(copy(e, 3 + half, owner, onward, half=half))
                send(copy(e, 4 + slot, owner, sibling))
        for e in range(n):
            if self.relayed[e]:
                for half in (0, 1):
                    arrived(copy(e, 3 + half, diagonal, me, half=half))
                    send(copy(e, 7 + half, diagonal, sibling, half=half))
            else:
                arrived(copy(e, 3, diagonal, me))
                send(copy(e, 7, diagonal, sibling))
        return sent

    def finish(self, src, dst, sems, passed_on=False):
        n, x, y, c, rows, copy, local_sems = self._plan(src, dst, sems)
        me, sibling = (x, y, c), (x, y, 1 - c)
        over_x, over_y, diagonal = (1 - x, y, c), (x, 1 - y, c), (1 - x, 1 - y, c)
        sent = self.pass_on(src, dst, sems, wait=not passed_on)
        for e in range(n):
            copy(e, 0, sibling, me).wait_recv()
            copy(e, 5, (1 - x, y, 1 - c), me).wait_recv()
            copy(e, 6, (x, 1 - y, 1 - c), me).wait_recv()
            if self.relayed[e]:
                for half in (0, 1):
                    copy(e, 7 + half, (1 - x, 1 - y, 1 - c), me, half=half).wait_recv()
            else:
                copy(e, 7, (1 - x, 1 - y, 1 - c), me).wait_recv()
        for e in range(n):
            copy(e, 0, me, sibling, from_own=True).wait_send()
            copy(e, 1, me, over_x, from_own=True).wait_send()
            copy(e, 2, me, over_y, from_own=True).wait_send()
            if not self.relayed[e]:
                copy(e, 3, me, diagonal, from_own=True).wait_send()
            pltpu.make_async_copy(src[e], rows(e, me), local_sems.at[e]).wait()
        for cp in sent:
            cp.wait_send()


class _ToSibling:
    def __init__(self, grads):
        self.srcs = list(grads)
        n = len(self.srcs)
        self.out_shapes = [jax.ShapeDtypeStruct((4,) + g.shape[1:], g.dtype) for g in self.srcs]
        self.sems = [pltpu.SemaphoreType.DMA((n, 4)), pltpu.SemaphoreType.DMA((n, 4))]

    def _copies(self, src, dst, sems):
        send_sems, recv_sems = sems
        x, y, c = _coords()
        return [
            pltpu.make_async_remote_copy(
                src_ref=src[i].at[2 * q + (1 - c)], dst_ref=dst[i].at[q], send_sem=send_sems.at[i, q],
                recv_sem=recv_sems.at[i, q], device_id=(x, y, 1 - c), device_id_type=MESH)
            for i in range(len(src)) for q in range(4)]

    def start(self, src, dst, sems):
        for cp in self._copies(src, dst, sems):
            cp.start()

    def finish(self, src, dst, sems):
        for cp in self._copies(src, dst, sems):
            cp.wait()


class _ToChips:
    def __init__(self, psums, rows=None):
        self.srcs = list(psums)
        n = len(self.srcs)
        self.rows = rows
        self.out_shapes = [
            jax.ShapeDtypeStruct((3, p.shape[1] if rows is None else rows[1]) + p.shape[2:], p.dtype)
            for p in self.srcs]
        self.sems = [pltpu.SemaphoreType.DMA((n, 3)), pltpu.SemaphoreType.DMA((n, 3))]

    def _copies(self, src, dst, sems):
        send_sems, recv_sems = sems
        x, y, c = _coords()
        peers = [(x, 1 - y), (1 - x, y), (1 - x, 1 - y)]

        def part(i, q):
            if self.rows is None:
                return src[i].at[q]
            return src[i].at[q, pl.ds(self.rows[0], self.rows[1])]

        return [
            pltpu.make_async_remote_copy(
                src_ref=part(i, 2 * px + py), dst_ref=dst[i].at[r], send_sem=send_sems.at[i, r],
                recv_sem=recv_sems.at[i, r], device_id=(px, py, c), device_id_type=MESH)
            for i in range(len(src)) for r, (px, py) in enumerate(peers)]

    def start(self, src, dst, sems):
        for cp in self._copies(src, dst, sems):
            cp.start()

    def finish(self, src, dst, sems):
        for cp in self._copies(src, dst, sems):
            cp.wait()


class _ToOwners:
    def __init__(self, grads):
        self.srcs = list(grads)
        n = len(self.srcs)
        self.out_shapes = [jax.ShapeDtypeStruct(g.shape, g.dtype) for g in self.srcs]
        self.sems = [pltpu.SemaphoreType.DMA((n, 7)), pltpu.SemaphoreType.DMA((n, 7)), pltpu.SemaphoreType.DMA((n,))]

    def _copies(self, src, dst, sems):
        send_sems, recv_sems, local_sems = sems
        x, y, c = _coords()
        me = 4 * x + 2 * y + c
        copies = [pltpu.make_async_copy(src[i].at[me], dst[i].at[me], local_sems.at[i]) for i in range(len(src))]
        for i in range(len(src)):
            for rel in range(1, N_SHARDS):
                px = x ^ (rel >> 2) if rel >> 2 else x
                py = y ^ ((rel >> 1) & 1) if (rel >> 1) & 1 else y
                pc = c ^ (rel & 1) if rel & 1 else c
                copies.append(pltpu.make_async_remote_copy(
                    src_ref=src[i].at[4 * px + 2 * py + pc], dst_ref=dst[i].at[me], send_sem=send_sems.at[i, rel - 1],
                    recv_sem=recv_sems.at[i, rel - 1], device_id=(px, py, pc), device_id_type=MESH))
        return copies

    def start(self, src, dst, sems):
        for cp in self._copies(src, dst, sems):
            cp.start()

    def finish(self, src, dst, sems):
        for cp in self._copies(src, dst, sems):
            cp.wait()


class _Together:
    def __init__(self, parts):
        self.parts = list(parts)
        self.srcs = [s for p in self.parts for s in p.srcs]
        self.out_shapes = [s for p in self.parts for s in p.out_shapes]
        self.sems = [s for p in self.parts for s in p.sems]

    def _split(self, src, dst, sems):
        a = b = c = 0
        for p in self.parts:
            na, nc = len(p.srcs), len(p.sems)
            yield p, src[a:a + na], dst[b:b + na], sems[c:c + nc]
            a, b, c = a + na, b + na, c + nc

    def start(self, src, dst, sems):
        for p, s, d, m in self._split(src, dst, sems):
            p.start(s, d, m)

    def finish(self, src, dst, sems):
        for p, s, d, m in self._split(src, dst, sems):
            p.finish(s, d, m)

    def spread(self):
        b = 0
        for p in self.parts:
            p.results = self.results[b:b + len(p.srcs)]
            b += len(p.srcs)


def _call(body, args, *, grid, in_specs, out_specs, out_shape, name, scratch=(), sem=None, carry=None):
    out_shape, out_specs = list(out_shape), list(out_specs)
    if carry is None:
        return pl.pallas_call(
            body, grid=grid, in_specs=list(in_specs), out_specs=out_specs, out_shape=out_shape,
            scratch_shapes=list(scratch), name=name, compiler_params=_params(sem))(*args)
    n_in, n_out, n_scr, n_c = len(args), len(out_shape), len(scratch), len(carry.srcs)
    steps = tuple(grid)
    total = 1
    for n_ax in steps:
        total *= n_ax
    early = getattr(carry, "early", False) and total >= 8
    early_step = total - max(2, total // 8)

    def carried(*refs):
        ins, rest = refs[:n_in], refs[n_in:]
        c_src, rest = rest[:n_c], rest[n_c:]
        outs, rest = rest[:n_out], rest[n_out:]
        c_dst, rest = rest[:n_c], rest[n_c:]
        scr, sems = rest[:n_scr], rest[n_scr:]
        step = pl.program_id(0)
        for ax in range(1, len(steps)):
            step = step * steps[ax] + pl.program_id(ax)

        @pl.when(step == 0)
        def _():
            carry.start(c_src, c_dst, sems)

        body(*ins, *outs, *scr)

        if early:
            @pl.when(step == early_step)
            def _():
                carry.pass_on(c_src, c_dst, sems)

        @pl.when(step == total - 1)
        def _():
            if early:
                carry.finish(c_src, c_dst, sems, passed_on=True)
            else:
                carry.finish(c_src, c_dst, sems)

    hbm = pl.BlockSpec(memory_space=pl.ANY)
    res = pl.pallas_call(
        carried, grid=grid, in_specs=list(in_specs) + [hbm] * n_c, out_specs=out_specs + [hbm] * n_c,
        out_shape=out_shape + carry.out_shapes, scratch_shapes=list(scratch) + carry.sems, name=name,
        compiler_params=_params(("arbitrary",) * len(steps)))(*args, *carry.srcs)
    carry.results = list(res[n_out:])
    return list(res[:n_out])


def _exchange_alone(ex, name):
    n = len(ex.srcs)

    def body(*refs):
        src, dst, sems = refs[:n], refs[n:2 * n], refs[2 * n:]
        ex.start(src, dst, sems)
        ex.finish(src, dst, sems)

    hbm = pl.BlockSpec(memory_space=pl.ANY)
    res = pl.pallas_call(body, in_specs=[hbm] * n, out_specs=[hbm] * n, out_shape=ex.out_shapes,
                         scratch_shapes=ex.sems, name=name)(*ex.srcs)
    ex.results = list(res)
    return ex.results


def _rms_bwd(x, gains, dhs, dres, name, tm=512, carry=None, through=None):
    t, d = x.shape
    n = len(gains)
    n_w = 0 if through is None else n

    def body(*refs):
        x_ref, dres_ref = refs[0], refs[1]
        g_refs, dh_refs, w_refs = refs[2:2 + n], refs[2 + n:2 + 2 * n], refs[2 + 2 * n:2 + 2 * n + n_w]
        dx_ref, dg_ref = refs[2 + 2 * n + n_w], refs[3 + 2 * n + n_w]
        i = pl.program_id(0)

        @pl.when(i == 0)
        def _():
            dg_ref[...] = jnp.zeros_like(dg_ref)

        xf = x_ref[...]
        r = lax.rsqrt(jnp.mean(xf * xf, axis=-1, keepdims=True) + EPS)
        xhat = xf * r
        dx = dres_ref[...]
        for j in range(n):
            dh = dh_refs[j][...]
            if n_w:
                dh = _dot(dh.astype(BF16), w_refs[j][...], NT)
            dg_ref[j:j + 1, :] += jnp.sum(dh * xhat, axis=0, keepdims=True)
            gy = dh * g_refs[j][...]
            dx = dx + r * (gy - xhat * jnp.mean(gy * xhat, axis=-1, keepdims=True))
        dx_ref[...] = dx

    row = pl.BlockSpec((tm, d), lambda i: (i, 0))
    vec = pl.BlockSpec((1, d), lambda i: (0, 0))
    dh_rows = [pl.BlockSpec((tm, dh.shape[1]), lambda i: (i, 0)) for dh in dhs]
    w_full = [] if through is None else [pl.BlockSpec(w.shape, lambda i: (0, 0)) for w in through]
    return _call(body, [x, dres, *gains, *dhs, *(through or [])], grid=(t // tm,),
                 in_specs=[row, row] + [vec] * n + dh_rows + w_full,
                 out_specs=[row, pl.BlockSpec((8, d), lambda i: (0, 0))],
                 out_shape=[jax.ShapeDtypeStruct((t, d), F32), jax.ShapeDtypeStruct((8, d), F32)],
                 name=name, sem=("arbitrary",), carry=carry)


def _mm(a, b, a_spec, b_spec, o_spec, out_shape, grid, dims, name, res=None, res_spec=None, carry=None):
    nk = grid[2]
    acc_shape = tuple(s for s in o_spec.block_shape if s is not None)

    def body(*refs):
        a_ref, b_ref = refs[0], refs[1]
        r_ref = refs[2] if res is not None else None
        o_ref = refs[3] if res is not None else refs[2]
        p = _dot(a_ref[...].astype(BF16), b_ref[...].astype(BF16), dims)
        if nk == 1:
            if res is not None:
                p = p + r_ref[...]
            o_ref[...] = p.astype(o_ref.dtype)
            return
        acc_ref = refs[-1]
        k = pl.program_id(2)

        @pl.when(k == 0)
        def _():
            acc_ref[...] = p

        @pl.when(k > 0)
        def _():
            acc_ref[...] += p

        @pl.when(k == nk - 1)
        def _():
            out = acc_ref[...]
            if res is not None:
                out = out + r_ref[...]
            o_ref[...] = out.astype(o_ref.dtype)

    ins = [a, b] + ([res] if res is not None else [])
    specs = [a_spec, b_spec] + ([res_spec] if res is not None else [])
    return _call(body, ins, grid=grid, in_specs=specs, out_specs=[o_spec], out_shape=[out_shape],
                 scratch=[pltpu.VMEM(acc_shape, F32)] if nk > 1 else [], name=name,
                 sem=("parallel", "parallel", "arbitrary"), carry=carry)[0]


def _mm_rows(a, w, out_dtype, name, trans_w=False, res=None, tm=1024, carry=None):
    t, k = a.shape
    tm = min(tm, t)
    n = w.shape[0] if trans_w else w.shape[1]
    return _mm(
        a, w, pl.BlockSpec((tm, k), lambda i, j, kk: (i, 0)), pl.BlockSpec(w.shape, lambda i, j, kk: (0, 0)),
        pl.BlockSpec((tm, n), lambda i, j, kk: (i, 0)), jax.ShapeDtypeStruct((t, n), out_dtype), (t // tm, 1, 1),
        NT if trans_w else NN, name, res=res,
        res_spec=None if res is None else pl.BlockSpec((tm, n), lambda i, j, kk: (i, 0)), carry=carry)


def _mm_wgrad(a, b, name, carry=None):
    t, m = a.shape
    n = b.shape[1]
    tn = n // (4 if b.dtype == F32 else 2)
    return _mm(
        a, b, pl.BlockSpec((t, m), lambda i, j, kk: (0, 0)), pl.BlockSpec((t, tn), lambda i, j, kk: (0, j)),
        pl.BlockSpec((m, tn), lambda i, j, kk: (0, j)), jax.ShapeDtypeStruct((m, n), F32), (1, n // tn, 1), TN, name,
        carry=carry)


def _sgu_fwd(x0, g, w_in, g_v, w_c, b_sb, w_out, tm=256, carry=None):
    t, d = x0.shape
    nsub = w_in.shape[2]

    def body(x_ref, g_ref, win_ref, gv_ref, wc_ref, bsb_ref, wout_ref, zpre_ref, x1_ref, h_ref, u_s, v_s, vn_s, y_s):
        xf = x_ref[...]
        h = (xf * lax.rsqrt(jnp.mean(xf * xf, axis=-1, keepdims=True) + EPS) * g_ref[...]).astype(BF16)
        h_ref[...] = h
        for k in range(N_SHARDS):
            zk = _dot(h, win_ref[k])
            zpre_ref[:, k * nsub:(k + 1) * nsub] = zk
            cdf, _ = _gelu_parts(zk)
            if k < N_SHARDS // 2:
                u_s[:, k * nsub:(k + 1) * nsub] = zk * cdf
            else:
                v_s[:, (k - 4) * nsub:(k - 3) * nsub] = zk * cdf
        v = v_s[...]
        rv = lax.rsqrt(jnp.mean(v * v, axis=-1, keepdims=True) + EPS)
        vn_s[...] = (v * rv * gv_ref[...]).astype(BF16)
        for ci in range(tm // CHUNK):
            rows = slice(ci * CHUNK, (ci + 1) * CHUNK)
            for g in range(N_GROUPS):
                cols = slice(g * LANES, (g + 1) * LANES)
                sv = _dot(wc_ref[g], vn_s[rows, cols]) + bsb_ref[g]
                y_s[rows, cols] = (u_s[rows, cols] * sv).astype(BF16)
        x1_ref[...] = x_ref[...] + _dot(y_s[...], wout_ref[...])

    row = pl.BlockSpec((tm, d), lambda i: (i, 0))
    full = lambda a: pl.BlockSpec(a.shape, lambda i: (0,) * a.ndim)
    return _call(
        body, [x0, g, w_in, g_v, w_c, b_sb, w_out], grid=(t // tm,),
        in_specs=[row, full(g), full(w_in), full(g_v), full(w_c), full(b_sb), full(w_out)],
        out_specs=[pl.BlockSpec((tm, 2 * d), lambda i: (i, 0)), row, row],
        out_shape=[jax.ShapeDtypeStruct((t, 2 * d), F32), jax.ShapeDtypeStruct((t, d), F32),
                   jax.ShapeDtypeStruct((t, d), BF16)],
        scratch=[pltpu.VMEM((tm, d), F32), pltpu.VMEM((tm, d), F32), pltpu.VMEM((tm, d), BF16),
                 pltpu.VMEM((tm, d), BF16)],
        name="sgu_fwd", carry=carry)


def _sgu_bwd(dx1, zpre, w_out, g_v, w_c, w_ct, b_sb, tm=512, carry=None):
    t, d = dx1.shape

    def body(dx_ref, zpre_ref, wout_ref, gv_ref, wc_ref, wct_ref, bsb_ref,
             dz_ref, y_ref, dwc_ref, dbs_ref, dgv_ref, u_s, vn_s, dy_s, du_s, dvn_s):
        i = pl.program_id(0)

        @pl.when(i == 0)
        def _():
            dwc_ref[...] = jnp.zeros_like(dwc_ref)
            dbs_ref[...] = jnp.zeros_like(dbs_ref)
            dgv_ref[...] = jnp.zeros_like(dgv_ref)

        dy_s[...] = _dot(dx_ref[...].astype(BF16), wout_ref[...], NT)
        zu = zpre_ref[:, :d]
        zv = zpre_ref[:, d:]
        cdf_u, pdf_u = _gelu_parts(zu)
        cdf_v, pdf_v = _gelu_parts(zv)
        u_s[...] = zu * cdf_u
        v = zv * cdf_v
        rv = lax.rsqrt(jnp.mean(v * v, axis=-1, keepdims=True) + EPS)
        vhat = v * rv
        gv = gv_ref[...]
        vn_s[...] = (vhat * gv).astype(BF16)
        for ci in range(tm // CHUNK):
            rows = slice(ci * CHUNK, (ci + 1) * CHUNK)
            for g in range(N_GROUPS):
                cols = slice(g * LANES, (g + 1) * LANES)
                vnb = vn_s[rows, cols]
                sv = _dot(wc_ref[g], vnb) + bsb_ref[g]
                dyb = dy_s[rows, cols]
                ub = u_s[rows, cols]
                dsv = dyb * ub
                du_s[rows, cols] = dyb * sv
                y_ref[rows, cols] = (ub * sv).astype(BF16)
                dsvb = dsv.astype(BF16)
                dbs_ref[g] += dsv
                dwc_ref[g] += _dot(dsvb, vnb, NT)
                dvn_s[rows, cols] = _dot(wct_ref[g], dsvb)
        dvn = dvn_s[...]
        dgv_ref[0:1, :] += jnp.sum(dvn * vhat, axis=0, keepdims=True)
        gy = dvn * gv
        dv = rv * (gy - vhat * jnp.mean(gy * vhat, axis=-1, keepdims=True))
        dz_ref[:, :d] = (du_s[...] * (cdf_u + zu * pdf_u)).astype(BF16)
        dz_ref[:, d:] = (dv * (cdf_v + zv * pdf_v)).astype(BF16)

        @pl.when(i == t // tm - 1)
        def _():
            tri = (lax.broadcasted_iota(jnp.int32, (CHUNK, CHUNK), 0)
                   >= lax.broadcasted_iota(jnp.int32, (CHUNK, CHUNK), 1))
            for g in range(N_GROUPS):
                dwc_ref[g] = jnp.where(tri, dwc_ref[g], 0.0)
                dbs_ref[g] = jnp.broadcast_to(jnp.sum(dbs_ref[g], axis=1, keepdims=True), (CHUNK, CHUNK))

    row = pl.BlockSpec((tm, d), lambda i: (i, 0))
    row2 = pl.BlockSpec((tm, 2 * d), lambda i: (i, 0))
    full = lambda a: pl.BlockSpec(a.shape, lambda i: (0,) * a.ndim)
    grp = pl.BlockSpec((N_GROUPS, CHUNK, CHUNK), lambda i: (0, 0, 0))
    return _call(
        body, [dx1, zpre, w_out, g_v, w_c, w_ct, b_sb], grid=(t // tm,),
        in_specs=[row, row2, full(w_out), full(g_v), full(w_c), full(w_ct), full(b_sb)],
        out_specs=[row2, row, grp, grp, pl.BlockSpec((8, d), lambda i: (0, 0))],
        out_shape=[jax.ShapeDtypeStruct((t, 2 * d), BF16), jax.ShapeDtypeStruct((t, d), BF16),
                   jax.ShapeDtypeStruct((N_GROUPS, CHUNK, CHUNK), F32),
                   jax.ShapeDtypeStruct((N_GROUPS, CHUNK, CHUNK), F32), jax.ShapeDtypeStruct((8, d), F32)],
        scratch=[pltpu.VMEM((tm, d), F32), pltpu.VMEM((tm, d), BF16), pltpu.VMEM((tm, d), F32),
                 pltpu.VMEM((tm, d), F32), pltpu.VMEM((tm, d), F32)],
        name="sgu_bwd", sem=("arbitrary",), carry=carry)


ROW_CHUNK = 256
HALO = 16


def _ffn_fwd(x, g, w_in, cw, cb, w_out, layer, tm=512, carry=None, next_gains=(), loss_target=None):
    t, d = x.shape
    nc = N_SHARDS // 2
    n_gains = len(next_gains)
    with_loss = loss_target is not None

    def body(x_ref, xp_ref, g_ref, wg_ref, wu_ref, cwg_ref, cbg_ref, cwu_ref, cbu_ref, wout_ref, *rest):
        extra_in, rest = rest[:n_gains + with_loss], rest[n_gains + with_loss:]
        o_ref, hf_ref, a_ref, pre_ref = rest[:4]
        extra_out, hw_s = rest[4:-1], rest[-1]
        i, c = pl.program_id(0), pl.program_id(1)

        @pl.when(c == 0)
        def _():
            keep = jnp.where(i == 0, 0.0, 1.0)
            xw = jnp.concatenate([xp_ref[...] * keep, x_ref[...]], axis=0)
            xhat = xw * lax.rsqrt(jnp.mean(xw * xw, axis=-1, keepdims=True) + EPS)
            hw_s[...] = (xhat * g_ref[...]).astype(BF16)
            hf_ref[...] = hw_s[HALO:, :]
            o_ref[...] = x_ref[...]

        hw = hw_s[...]
        pre = []
        for j, (w_ref, cw_ref, cb_ref) in enumerate(((wg_ref, cwg_ref, cbg_ref), (wu_ref, cwu_ref, cbu_ref))):
            ab = _dot(hw, w_ref[...]).astype(BF16)
            a_ref[j] = ab[HALO:]
            win = ab.astype(F32)
            cw_v = cw_ref[...]
            pre.append(cw_v[2:3, :] * win[HALO:] + cw_v[1:2, :] * pltpu.roll(win, 1, 0)[HALO:]
                       + cw_v[0:1, :] * pltpu.roll(win, 2, 0)[HALO:] + cb_ref[...])
            pre_ref[j] = pre[j]
        act = (pre[0] * _sigmoid(pre[0]) * pre[1]).astype(BF16)
        o_ref[...] += _dot(act, wout_ref[...])

        if with_loss:
            @pl.when((i == 0) & (c == 0))
            def _():
                extra_out[-1][...] = jnp.zeros_like(extra_out[-1])

        @pl.when(c == nc - 1)
        def _():
            xn = o_ref[...]
            if n_gains:
                xhat = xn * lax.rsqrt(jnp.mean(xn * xn, axis=-1, keepdims=True) + EPS)
                for k in range(n_gains):
                    extra_out[k][...] = (xhat * extra_in[k][...]).astype(BF16)
            if with_loss:
                err = xn - extra_in[-1][...]
                extra_out[-2][...] = err * (1.0 / d)
                part = jnp.sum(jnp.sum(err * err, axis=0, keepdims=True), axis=1, keepdims=True)
                extra_out[-1][...] += jnp.broadcast_to(0.5 / d * part, extra_out[-1].shape)

    row = pl.BlockSpec((tm, d), lambda i, c: (i, 0))
    vec = pl.BlockSpec((1, d), lambda i, c: (0, 0))
    shard = lambda rows, up: pl.BlockSpec((None, rows, FF_SHARD), lambda i, c: (c + up * nc, 0, 0))
    pair = pl.BlockSpec((2, None, tm, FF_SHARD), lambda i, c: (0, c, i, 0))
    lanes = pl.BlockSpec((8, LANES), lambda i, c: (0, 0))
    outs = _call(
        body, [x, x, g, w_in, w_in, cw, cb, cw, cb, w_out, *next_gains] + ([loss_target] if with_loss else []),
        grid=(t // tm, nc),
        in_specs=[row, pl.BlockSpec((HALO, d), lambda i, c: (jnp.maximum(i * (tm // HALO) - 1, 0), 0)),
                  vec, shard(d, 0), shard(d, 1), shard(8, 0), shard(1, 0), shard(8, 1), shard(1, 1),
                  pl.BlockSpec((FF_SHARD, d), lambda i, c: (c, 0))] + [vec] * n_gains + [row] * with_loss,
        out_specs=[row, row, pair, pair] + [row] * n_gains + [row, lanes] * with_loss,
        out_shape=[jax.ShapeDtypeStruct((t, d), F32), jax.ShapeDtypeStruct((t, d), BF16),
                   jax.ShapeDtypeStruct((2, nc, t, FF_SHARD), BF16), jax.ShapeDtypeStruct((2, nc, t, FF_SHARD), F32)]
        + [jax.ShapeDtypeStruct((t, d), BF16)] * n_gains
        + [jax.ShapeDtypeStruct((t, d), F32), jax.ShapeDtypeStruct((8, LANES), F32)] * with_loss,
        scratch=[pltpu.VMEM((tm + HALO, d), BF16)], name=f"ffn{layer}_fwd", sem=("arbitrary", "arbitrary"), carry=carry)
    return (outs[0], outs[1], outs[2].reshape(N_SHARDS, t, FF_SHARD), outs[3]) + tuple(outs[4:])


def _ffn_bwd_act(pre, w_out, dxn, layer, tm=1024, carry=None):
    t, d = dxn.shape
    tm = min(tm, t)
    nc = N_SHARDS // 2

    def body(pre_ref, wout_ref, dx_ref, dhu_ref, dw_ref, dcb_ref):
        i = pl.program_id(1)

        @pl.when(i == 0)
        def _():
            dw_ref[...] = jnp.zeros_like(dw_ref)
            dcb_ref[...] = jnp.zeros_like(dcb_ref)

        hg, hu = pre_ref[0], pre_ref[1]
        sg = _sigmoid(hg)
        sl = hg * sg
        dxb = dx_ref[...].astype(BF16)
        dact = _dot(dxb, wout_ref[...], NT)
        dw_ref[...] += _dot((sl * hu).astype(BF16), dxb, TN)
        d_up = dact * sl
        d_gate = dact * hu * (sg * (1.0 + hg * (1.0 - sg)))
        for j, dv in enumerate((d_gate, d_up)):
            dhu_ref[j] = dv.astype(BF16)
            dcb_ref[j, 0:1, :] += jnp.sum(dv, axis=0, keepdims=True)

    return _call(
        body, [pre, w_out, dxn], grid=(nc, t // tm),
        in_specs=[pl.BlockSpec((2, None, tm, FF_SHARD), lambda c, i: (0, c, i, 0)),
                  pl.BlockSpec((FF_SHARD, d), lambda c, i: (c, 0)), pl.BlockSpec((tm, d), lambda c, i: (i, 0))],
        out_specs=[pl.BlockSpec((None, 2, tm, FF_SHARD), lambda c, i: (c, 0, i, 0)),
                   pl.BlockSpec((FF_SHARD, d), lambda c, i: (c, 0)),
                   pl.BlockSpec((None, 2, 8, FF_SHARD), lambda c, i: (c, 0, 0, 0))],
        out_shape=[jax.ShapeDtypeStruct((nc, 2, t, FF_SHARD), BF16), jax.ShapeDtypeStruct((D_FF, d), F32),
                   jax.ShapeDtypeStruct((nc, 2, 8, FF_SHARD), F32)],
        name=f"ffn{layer}_bwd_act", sem=("parallel", "arbitrary"), carry=carry)


def _ffn_bwd_in(dhu, a, cw, w_in, layer, tm=1024, carry=None, norm=None):
    nc, _, t, _ = dhu.shape
    d = D_MODEL
    tm = min(tm, t)
    last_blk = t // 16 - 1
    n_norm = 0 if norm is None else 3

    def body(dh_ref, nx_ref, a_ref, cw_ref, win_ref, *rest):
        norm_refs, (da_ref, o_ref, dcw_ref), dg_refs = rest[:n_norm], rest[n_norm:n_norm + 3], rest[n_norm + 3:]
        i, s = pl.program_id(0), pl.program_id(1)

        @pl.when(s == 0)
        def _():
            o_ref[...] = jnp.zeros_like(o_ref)

        @pl.when((s == 0) & (i == 0))
        def _():
            dcw_ref[...] = jnp.zeros_like(dcw_ref)

        keep = jnp.where(i == t // tm - 1, 0.0, 1.0)
        cw = cw_ref[...]
        sums = [None] * 3
        for r0 in range(0, tm, ROW_CHUNK):
            rows = slice(r0, r0 + ROW_CHUNK)
            if r0 + ROW_CHUNK == tm:
                win = jnp.concatenate([dh_ref[rows, :].astype(F32), nx_ref[...].astype(F32) * keep], axis=0)
            else:
                win = dh_ref[r0:r0 + ROW_CHUNK + HALO, :].astype(F32)
            n = ROW_CHUNK + HALO
            taps = (pltpu.roll(win, n - 2, 0)[:ROW_CHUNK],
                    pltpu.roll(win, n - 1, 0)[:ROW_CHUNK],
                    win[:ROW_CHUNK])
            da = (cw[0:1, :] * taps[0] + cw[1:2, :] * taps[1] + cw[2:3, :] * taps[2]).astype(BF16)
            da_ref[rows, :] = da
            o_ref[rows, :] += _dot(da, win_ref[...], NT)
            af = a_ref[rows, :].astype(F32)
            parts = [jnp.sum(taps[k] * af, axis=0, keepdims=True) for k in range(3)]
            sums = [p if q is None else q + p for q, p in zip(sums, parts)]
        for k in range(3):
            dcw_ref[pl.ds(s, 1), k:k + 1, :] += sums[k][None]

        if norm is not None:
            x_ref, g_ref, dres_ref = norm_refs
            dg_ref = dg_refs[0]

            @pl.when((s == 0) & (i == 0))
            def _():
                dg_ref[...] = jnp.zeros_like(dg_ref)

            @pl.when(s == N_SHARDS - 1)
            def _():
                xf = x_ref[...]
                r = lax.rsqrt(jnp.mean(xf * xf, axis=-1, keepdims=True) + EPS)
                xhat = xf * r
                dh = o_ref[...]
                dg_ref[0:1, :] += jnp.sum(dh * xhat, axis=0, keepdims=True)
                gy = dh * g_ref[...]
                o_ref[...] = dres_ref[...] + r * (gy - xhat * jnp.mean(gy * xhat, axis=-1, keepdims=True))

    row = pl.BlockSpec((tm, d), lambda i, s: (i, 0))
    norm_args = [] if norm is None else list(norm)
    norm_specs = [] if norm is None else [row, pl.BlockSpec((1, d), lambda i, s: (0, 0)), row]
    return _call(
        body, [dhu, dhu, a, cw, w_in] + norm_args, grid=(t // tm, N_SHARDS),
        in_specs=[pl.BlockSpec((None, None, tm, FF_SHARD), lambda i, s: (s % nc, s // nc, i, 0)),
                  pl.BlockSpec((None, None, 16, FF_SHARD),
                               lambda i, s: (s % nc, s // nc, jnp.minimum((i + 1) * (tm // 16), last_blk), 0)),
                  pl.BlockSpec((None, tm, FF_SHARD), lambda i, s: (s, i, 0)),
                  pl.BlockSpec((None, 8, FF_SHARD), lambda i, s: (s, 0, 0)),
                  pl.BlockSpec((None, d, FF_SHARD), lambda i, s: (s, 0, 0))] + norm_specs,
        out_specs=[pl.BlockSpec((None, tm, FF_SHARD), lambda i, s: (s, i, 0)), row,
                   pl.BlockSpec((N_SHARDS, 8, FF_SHARD), lambda i, s: (0, 0, 0))]
        + ([] if norm is None else [pl.BlockSpec((8, d), lambda i, s: (0, 0))]),
        out_shape=[jax.ShapeDtypeStruct((N_SHARDS, t, FF_SHARD), BF16), jax.ShapeDtypeStruct((t, d), F32),
                   jax.ShapeDtypeStruct((N_SHARDS, 8, FF_SHARD), F32)]
        + ([] if norm is None else [jax.ShapeDtypeStruct((8, d), F32)]),
        name=f"ffn{layer}_bwd_in", sem=("arbitrary", "arbitrary"), carry=carry)


def _ffn_wgrad_in(hf, da, layer, carry=None):
    t, d = hf.shape
    return _mm(
        da, hf, pl.BlockSpec((None, t, FF_SHARD), lambda s, j, kk: (s, 0, 0)),
        pl.BlockSpec((t, d), lambda s, j, kk: (0, 0)),
        pl.BlockSpec((None, FF_SHARD, d), lambda s, j, kk: (s, 0, 0)),
        jax.ShapeDtypeStruct((N_SHARDS, FF_SHARD, d), F32), (N_SHARDS, 1, 1), TN, f"ffn{layer}_wgrad_in",
        carry=carry)


Q_PER_KV = N_Q_HEADS // N_KV_HEADS
GROUP_ROWS = Q_PER_KV * CHUNK


def _lane_half():
    return lax.broadcasted_iota(jnp.int32, (CHUNK, LANES), 1) < HEAD_DIM


def _fill_attn_bias(bias_s):
    tq = lax.broadcasted_iota(jnp.int32, (GROUP_ROWS, 2 * CHUNK), 0) & (CHUNK - 1)
    jk = lax.broadcasted_iota(jnp.int32, (GROUP_ROWS, 2 * CHUNK), 1)
    dist = tq + CHUNK - jk
    window = (dist >= 0) & (dist < CHUNK)
    distf = dist.astype(F32)
    for kvh in range(N_KV_HEADS):
        alibi = _per_head_column([-SLOPES[h] for h in range(Q_PER_KV * kvh, Q_PER_KV * (kvh + 1))]) * distf
        bias_s[0, kvh] = jnp.where(window & (jk >= CHUNK), alibi, NEG_BIG)
        bias_s[1, kvh] = jnp.where(window, alibi, NEG_BIG)


def _per_head_column(values):
    r = lax.broadcasted_iota(jnp.int32, (GROUP_ROWS, 1), 0)
    col = jnp.full((GROUP_ROWS, 1), values[Q_PER_KV - 1], F32)
    for j in range(Q_PER_KV - 2, -1, -1):
        col = jnp.where(r < (j + 1) * CHUNK, values[j], col)
    return col


def _half_sum(x, lo):
    s_lo = jnp.sum(jnp.where(lo, x, 0.0), axis=-1, keepdims=True)
    s_hi = jnp.sum(jnp.where(lo, 0.0, x), axis=-1, keepdims=True)
    return jnp.where(lo, s_lo, s_hi)


def _stack_heads(pairs, lo):
    zero = jnp.zeros_like(pairs[0])
    return jnp.concatenate([jnp.where(lo, pairs[0], zero), jnp.where(lo, zero, pairs[0]),
                            jnp.where(lo, pairs[1], zero), jnp.where(lo, zero, pairs[1])], axis=0)


def _unstack_heads(stacked, lo):
    return (jnp.where(lo, stacked[0:CHUNK], stacked[CHUNK:2 * CHUNK]),
            jnp.where(lo, stacked[2 * CHUNK:3 * CHUNK], stacked[3 * CHUNK:]))


def _attn_probs(qs, kn, bias, sink_col):
    s = _dot(qs, kn, NT) * (HEAD_DIM ** -0.5) + bias
    m = jnp.maximum(jnp.max(s, axis=-1, keepdims=True), sink_col)
    e = jnp.exp(s - m)
    den = jnp.sum(e, axis=-1, keepdims=True) + jnp.exp(sink_col - m)
    return e * (1.0 / den), m, den


def _attn_fwd(qraw, kvd, gq, gk, sinks, carry=None):
    t, d = qraw.shape
    nb = t // CHUNK

    def body(sink_ref, q_ref, cur_ref, prev_ref, gq_ref, gk_ref, o_ref, bias_s):
        n = pl.program_id(0)

        @pl.when(n == 0)
        def _():
            _fill_attn_bias(bias_s)

        lo = _lane_half()
        which = jnp.where(n == 0, 0, 1)
        gq_v, gk_v = gq_ref[...], gk_ref[...]
        for kvh in range(N_KV_HEADS):
            ks = slice(kvh * LANES, (kvh + 1) * LANES)
            vs = slice(4 * LANES + kvh * LANES, 4 * LANES + (kvh + 1) * LANES)
            kraw = jnp.concatenate([prev_ref[:, ks], cur_ref[:, ks]], axis=0)
            rk = lax.rsqrt(jnp.mean(kraw * kraw, axis=-1, keepdims=True) + EPS)
            kn = (kraw * rk * gk_v).astype(BF16)
            vv = jnp.concatenate([prev_ref[:, vs], cur_ref[:, vs]], axis=0).astype(BF16)
            qn = []
            for p in range(2):
                qp = q_ref[:, (2 * kvh + p) * LANES:(2 * kvh + p + 1) * LANES]
                r = lax.rsqrt(_half_sum(qp * qp, lo) * (1.0 / HEAD_DIM) + EPS)
                qn.append(qp * r * gq_v)
            heads = range(Q_PER_KV * kvh, Q_PER_KV * (kvh + 1))
            pf, _, _ = _attn_probs(_stack_heads(qn, lo).astype(BF16), kn, bias_s[which, kvh],
                                   _per_head_column([sink_ref[h] for h in heads]))
            for p, o_pair in enumerate(_unstack_heads(_dot(pf.astype(BF16), vv), lo)):
                o_ref[:, (2 * kvh + p) * LANES:(2 * kvh + p + 1) * LANES] = o_pair.astype(BF16)

    blk = lambda f: pl.BlockSpec((CHUNK, d), f)
    vec = pl.BlockSpec((1, LANES), lambda n: (0, 0))
    return _call(
        body, [sinks, qraw, kvd, kvd, gq, gk], grid=(nb,),
        in_specs=[pl.BlockSpec(memory_space=pltpu.SMEM), blk(lambda n: (n, 0)), blk(lambda n: (n, 0)),
                  blk(lambda n: (jnp.maximum(n - 1, 0), 0)), vec, vec],
        out_specs=[blk(lambda n: (n, 0))], out_shape=[jax.ShapeDtypeStruct((t, d), BF16)],
        scratch=[pltpu.VMEM((2, N_KV_HEADS, GROUP_ROWS, 2 * CHUNK), F32)], name="attn_fwd", sem=("arbitrary",),
        carry=carry)[0]


def _attn_bwd(qraw, kvd, d_o, gq, gk, sinks, carry=None):
    t, d = qraw.shape
    nb = t // CHUNK

    def body(sink_ref, q_ref, cur_ref, prev_ref, do_ref, gq_ref, gk_ref,
             dq_ref, dkv_ref, dsink_ref, dgq_ref, dgk_ref, carry_s, pp_s, cp_s, bias_s):
        n = pl.program_id(0)

        @pl.when(n == 0)
        def _():
            carry_s[...] = jnp.zeros_like(carry_s)
            dsink_ref[...] = jnp.zeros_like(dsink_ref)
            dgq_ref[...] = jnp.zeros_like(dgq_ref)
            dgk_ref[...] = jnp.zeros_like(dgk_ref)
            _fill_attn_bias(bias_s)

        @pl.when(n < nb)
        def _():
            lo = _lane_half()
            which = jnp.where(n == 0, 0, 1)
            gq_v, gk_v = gq_ref[...], gk_ref[...]
            for kvh in range(N_KV_HEADS):
                ks = slice(kvh * LANES, (kvh + 1) * LANES)
                vs = slice(4 * LANES + kvh * LANES, 4 * LANES + (kvh + 1) * LANES)
                kraw = jnp.concatenate([prev_ref[:, ks], cur_ref[:, ks]], axis=0)
                rk = lax.rsqrt(jnp.mean(kraw * kraw, axis=-1, keepdims=True) + EPS)
                khat = kraw * rk
                kn = (khat * gk_v).astype(BF16)
                vv = jnp.concatenate([prev_ref[:, vs], cur_ref[:, vs]], axis=0).astype(BF16)
                cols = [slice((2 * kvh + p) * LANES, (2 * kvh + p + 1) * LANES) for p in range(2)]
                rq, qhat = [], []
                for p in range(2):
                    qp = q_ref[:, cols[p]]
                    rq.append(lax.rsqrt(_half_sum(qp * qp, lo) * (1.0 / HEAD_DIM) + EPS))
                    qhat.append(qp * rq[p])
                heads = range(Q_PER_KV * kvh, Q_PER_KV * (kvh + 1))
                qs = _stack_heads([qhat[p] * gq_v for p in range(2)], lo).astype(BF16)
                dos = _stack_heads([do_ref[:, cols[p]] for p in range(2)], lo)
                sink_col = _per_head_column([sink_ref[h] for h in heads])
                pf, m, den = _attn_probs(qs, kn, bias_s[which, kvh], sink_col)
                dp = _dot(dos, vv, NT)
                delta = jnp.sum(pf * dp, axis=-1, keepdims=True)
                sink_delta = jnp.exp(sink_col - m) / den * delta
                for j, h in enumerate(heads):
                    dsink_ref[h:h + 1, :] -= jnp.broadcast_to(
                        jnp.sum(sink_delta[j * CHUNK:(j + 1) * CHUNK], axis=0, keepdims=True), (1, LANES))
                ds = (pf * (dp - delta) * (HEAD_DIM ** -0.5)).astype(BF16)
                dkn = _dot(ds, qs, TN)
                dvb = _dot(pf.astype(BF16), dos, TN)
                for p, dqn in enumerate(_unstack_heads(_dot(ds, kn), lo)):
                    dgq_ref[0:1, :] += jnp.sum(dqn * qhat[p], axis=0, keepdims=True)
                    gy = dqn * gq_v
                    mq = _half_sum(gy * qhat[p], lo) * (1.0 / HEAD_DIM)
                    dq_ref[:, cols[p]] = (rq[p] * (gy - qhat[p] * mq)).astype(BF16)
                dgk_ref[0:1, :] += jnp.sum(dkn * khat, axis=0, keepdims=True)
                gyk = dkn * gk_v
                dkraw = rk * (gyk - khat * jnp.mean(gyk * khat, axis=-1, keepdims=True))
                pp_s[:, ks] = dkraw[:CHUNK]
                cp_s[:, ks] = dkraw[CHUNK:]
                pp_s[:, vs] = dvb[:CHUNK]
                cp_s[:, vs] = dvb[CHUNK:]
            dkv_ref[...] = (carry_s[...] + pp_s[...]).astype(BF16)
            carry_s[...] = cp_s[...]

        @pl.when(n == nb)
        def _():
            dkv_ref[...] = carry_s[...].astype(BF16)

    blk = lambda f: pl.BlockSpec((CHUNK, d), f)
    vec = pl.BlockSpec((1, LANES), lambda n: (0, 0))
    cur = lambda n: (jnp.minimum(n, nb - 1), 0)
    prev = lambda n: (jnp.maximum(jnp.minimum(n, nb - 1) - 1, 0), 0)
    small = lambda r: pl.BlockSpec((r, LANES), lambda n: (0, 0))
    return _call(
        body, [sinks, qraw, kvd, kvd, d_o, gq, gk], grid=(nb + 1,),
        in_specs=[pl.BlockSpec(memory_space=pltpu.SMEM), blk(cur), blk(cur), blk(prev), blk(cur), vec, vec],
        out_specs=[blk(cur), blk(lambda n: (jnp.maximum(n - 1, 0), 0)), small(N_Q_HEADS), small(8), small(8)],
        out_shape=[jax.ShapeDtypeStruct((t, d), BF16), jax.ShapeDtypeStruct((t, d), BF16),
                   jax.ShapeDtypeStruct((N_Q_HEADS, LANES), F32), jax.ShapeDtypeStruct((8, LANES), F32),
                   jax.ShapeDtypeStruct((8, LANES), F32)],
        scratch=[pltpu.VMEM((CHUNK, d), F32)] * 3 + [pltpu.VMEM((2, N_KV_HEADS, GROUP_ROWS, 2 * CHUNK), F32)],
        name="attn_bwd", sem=("arbitrary",), carry=carry)


def _adamw_math(g, w, m, v):
    m = ADAM_B1 * m + (1.0 - ADAM_B1) * g
    v = ADAM_B2 * v + (1.0 - ADAM_B2) * (g * g)
    m_hat = m / (1.0 - ADAM_B1 ** ADAM_STEP)
    v_hat = v / (1.0 - ADAM_B2 ** ADAM_STEP)
    delta = -ADAM_LR * (m_hat / (jnp.sqrt(v_hat) + ADAM_EPS) + ADAM_WD * w)
    return delta, m, v


def _row_tile(r, cap=128):
    for tr in range(min(r, cap), 0, -1):
        if r % tr == 0 and (tr % 8 == 0 or tr == r):
            return tr
    return r


def _chip_sum(grad, recv, place, name, wire_dtype):
    _, r, c = grad.shape
    tr = _row_tile(r, 256)

    def body(pl_ref, g_ref, a_ref, p_ref):
        p_ref[...] = (g_ref[...] + a_ref[...]).astype(p_ref.dtype)

    other = lambda rel, pr: pr[0] ^ (rel + 1)
    return pl.pallas_call(
        body,
        grid_spec=pltpu.PrefetchScalarGridSpec(
            num_scalar_prefetch=1, grid=(3, r // tr),
            in_specs=[pl.BlockSpec((None, None, tr, c), lambda rel, i, pr: (other(rel, pr), pr[1], i, 0)),
                      pl.BlockSpec((None, tr, c), lambda rel, i, pr: (other(rel, pr), i, 0))],
            out_specs=pl.BlockSpec((None, tr, c), lambda rel, i, pr: (other(rel, pr), i, 0))),
        out_shape=jax.ShapeDtypeStruct((4, r, c), wire_dtype), name=name, compiler_params=_params(),
    )(place, grad.reshape(4, 2, r, c), recv)


def _adamw_sharded(grad, recv, others, place, w, m, v, name, layer=None, fill=None):
    r, c = w.shape[-2:]
    tr = _row_tile(r)

    def body(pl_ref, g_ref, a_ref, oth_ref, w_ref, m_ref, v_ref, *rest):
        g_out, d_out, nm_out, nv_out = rest[-4:]
        g = g_ref[...] + a_ref[...]
        for k in range(3):
            g = g + oth_ref[k].astype(F32)
        delta, nm, nv = _adamw_math(g, w_ref[...], m_ref[...], v_ref[...])
        g_out[...] = g
        d_out[...] = delta
        nm_out[...] = nm
        nv_out[...] = nv

    if layer is None:
        row = pl.BlockSpec((tr, c), lambda i, pr: (i, 0))
    else:
        row = pl.BlockSpec((None, tr, c), lambda i, pr: (layer, i, 0))
    n_fill = 0 if fill is None else 4
    in_specs = [pl.BlockSpec((None, None, tr, c), lambda i, pr: (pr[0], pr[1], i, 0)),
                pl.BlockSpec((None, tr, c), lambda i, pr: (pr[0], i, 0)),
                pl.BlockSpec((3, tr, c), lambda i, pr: (0, i, 0)), row, row, row]
    in_specs += [pl.BlockSpec(memory_space=pl.ANY)] * n_fill
    return pl.pallas_call(
        body,
        grid_spec=pltpu.PrefetchScalarGridSpec(
            num_scalar_prefetch=1, grid=(r // tr,), in_specs=in_specs, out_specs=[row] * 4),
        out_shape=[jax.ShapeDtypeStruct(w.shape, F32)] * 4, name=name, compiler_params=_params(),
        input_output_aliases={7 + j: j for j in range(n_fill)},
    )(place, grad.reshape(4, 2, r, c), recv, others, w, m, v, *([] if fill is None else fill))


def _sum_devices(parts, name):
    def body(p_ref, o_ref):
        total = p_ref[0]
        for k in range(1, N_SHARDS):
            total = total + p_ref[k]
        o_ref[...] = total

    return pl.pallas_call(body, out_shape=jax.ShapeDtypeStruct(parts.shape[1:], F32), name=name)(parts)


def _adamw_summed(parts, ws, ms, vs, name):
    n = len(parts)

    def body(*refs):
        p_refs, w_refs, m_refs, v_refs = refs[:n], refs[n:2 * n], refs[2 * n:3 * n], refs[3 * n:4 * n]
        o_refs = refs[4 * n:]
        for i in range(n):
            g = p_refs[i][0]
            for k in range(1, N_SHARDS):
                g = g + p_refs[i][k]
            delta, nm, nv = _adamw_math(g, w_refs[i][...], m_refs[i][...], v_refs[i][...])
            o_refs[4 * i][...] = g
            o_refs[4 * i + 1][...] = delta
            o_refs[4 * i + 2][...] = nm
            o_refs[4 * i + 3][...] = nv

    shapes = [jax.ShapeDtypeStruct(w.shape, F32) for w in ws for _ in range(4)]
    outs = pl.pallas_call(body, out_shape=shapes, name=name, compiler_params=_params())(*parts, *ws, *ms, *vs)
    return [outs[4 * i:4 * i + 4] for i in range(n)]


def _dup_heads(w):
    lead = w.shape[:-1]
    w4 = w.reshape(lead + (N_KV_HEADS, 1, HEAD_DIM))
    return jnp.broadcast_to(w4, lead + (N_KV_HEADS, 2, HEAD_DIM)).reshape(lead + (N_KV_HEADS * LANES,))


def _fold_heads(g):
    lead = g.shape[:-1]
    return g.reshape(lead + (N_KV_HEADS, 2, HEAD_DIM)).sum(axis=-2).reshape(lead + (N_KV_HEADS * HEAD_DIM,))


def kernel(x, a_norm, a_w_in, a_v_norm, a_w_s, a_b_s, a_w_out, f_norm, f_w_in, f_conv_w, f_conv_b, f_w_out, kv_norm, w_kv, k_norm, b_norm, b_w_q, b_q_norm, b_sinks, b_w_o, loss_target, m_a_norm, m_a_w_in, m_a_v_norm, m_a_w_s, m_a_b_s, m_a_w_out, m_f_norm, m_f_w_in, m_f_conv_w, m_f_conv_b, m_f_w_out, m_kv_norm, m_w_kv, m_k_norm, m_b_norm, m_b_w_q, m_b_q_norm, m_b_sinks, m_b_w_o, v_a_norm, v_a_w_in, v_a_v_norm, v_a_w_s, v_a_b_s, v_a_w_out, v_f_norm, v_f_w_in, v_f_conv_w, v_f_conv_b, v_f_w_out, v_kv_norm, v_w_kv, v_k_norm, v_b_norm, v_b_w_q, v_b_q_norm, v_b_sinks, v_b_w_o):
    d = D_MODEL
    xi, yi, ci = _coords()
    place = jnp.stack([2 * xi + yi, ci]).astype(jnp.int32)
    bf = lambda a: a.astype(BF16)
    row = lambda v_: v_.reshape(1, -1)
    x0, target = x[0], loss_target[0]
    t = x0.shape[0]
    res = {}

    red = {}

    def to_sibling(grads, wire=BF16):
        for k, g in grads.items():
            red[k] = dict(grad=g, wire=wire)
        ex = _ToSibling(list(grads.values()))
        ex.names = list(grads)
        return ex

    def to_chips(ex):
        for k, a in zip(ex.names, ex.results):
            red[k]["recv"] = a
            red[k]["psum"] = _chip_sum(red[k]["grad"], a, place, f"chip_sum_{k}", red[k]["wire"])
        nxt = _ToChips([red[k]["psum"] for k in ex.names])
        nxt.names = ex.names
        return nxt

    def landed(ex):
        for k, b in zip(ex.names, ex.results):
            red[k]["others"] = b

    def halves(ex, first_rows):
        parts = []
        for r0, nr in ((0, first_rows), (first_rows, ex.srcs[0].shape[1] - first_rows)):
            part = _ToChips(ex.srcs, rows=(r0, nr))
            part.names = ex.names
            parts.append(part)
        return parts

    def landed_halves(parts):
        for j, k in enumerate(parts[0].names):
            red[k]["others"] = jnp.concatenate([p.results[j] for p in parts], axis=1)

    def update(k, w, m, v, layer=None, fill=None):
        r = red[k]
        return _adamw_sharded(r["grad"], r["recv"], r["others"], place, w, m, v,
                              f"adamw_{k}", layer=layer, fill=fill)

    g_a_in, g_a_out, g_a_norm, g_a_v_norm, g_conv = _exchange_alone(
        _Gather([bf(a_w_in[0]), bf(a_w_out[0]), a_norm, a_v_norm, f_conv_w.reshape(6, FF_SHARD)]), "gather_first")
    a_norm_full, a_v_norm_full = g_a_norm.reshape(1, d), g_a_v_norm.reshape(1, d)
    conv_w = lax.reduce_precision(g_conv.reshape(N_SHARDS, 2, 3, FF_SHARD), 8, 7)
    cw = jnp.pad(jnp.transpose(conv_w, (1, 0, 2, 3)), ((0, 0), (0, 0), (0, 5), (0, 0)))
    w_a_in_flat = jnp.transpose(g_a_in, (1, 0, 2)).reshape(d, 2 * d)
    cb = f_conv_b.reshape(2, N_SHARDS, 1, FF_SHARD)
    tri = jnp.tril(jnp.ones((CHUNK, CHUNK), dtype=bool))
    w_causal = jnp.where(tri[None], a_w_s[0], 0.0).astype(BF16)
    w_causal_t = jnp.transpose(w_causal, (0, 2, 1))
    b_sb = jnp.broadcast_to(a_b_s[0][:, :, None], (N_GROUPS, CHUNK, CHUNK))
    w_a_out = g_a_out.reshape(d, d)
    gq = jnp.tile(b_q_norm.reshape(1, HEAD_DIM), (1, 2))
    gk = jnp.tile(k_norm.reshape(1, HEAD_DIM), (1, 2))
    sinks = b_sinks.reshape(N_Q_HEADS)

    ex = _Gather([bf(f_w_in[0]), bf(f_w_out[0])])
    zpre, x1, h1 = _sgu_fwd(x0, a_norm_full, g_a_in, a_v_norm_full, w_causal, b_sb, w_a_out, carry=ex)
    w_in0, w_out0 = ex.results[0], ex.results[1].reshape(D_FF, d)
    ex = _Gather([bf(w_kv), bf(b_w_q[0]), bf(b_w_o[0]), bf(f_w_in[1])], relay=False, early=True)
    x2, hf0, a0, pre0, hk, hq = _ffn_fwd(x1, f_norm[0:1], w_in0, cw[0], cb[0], w_out0, 0, carry=ex,
                                         next_gains=[row(kv_norm), b_norm])
    kv_full = ex.results[0].reshape(d, 2 * N_KV_HEADS * HEAD_DIM)
    w_q, w_o = ex.results[1].reshape(d, d), ex.results[2].reshape(d, d)
    w_in1 = ex.results[3]
    half = N_KV_HEADS * HEAD_DIM
    w_kv_dup = jnp.concatenate([_dup_heads(kv_full[:, :half]), _dup_heads(kv_full[:, half:])], axis=1)
    kvd = _mm_rows(hk, w_kv_dup, F32, "kv_proj")
    qraw = _mm_rows(hq, w_q, F32, "q_proj")
    ex = _Gather([bf(f_w_out[1])], relay=False, early=True)
    o = _attn_fwd(qraw, kvd, gq, gk, sinks, carry=ex)
    w_out1 = ex.results[0].reshape(D_FF, d)
    x3 = _mm_rows(o, w_o, F32, "o_proj", res=x2)
    _, hf1, a1, pre1, dy, loss_lanes = _ffn_fwd(x3, f_norm[1:2], w_in1, cw[1], cb[1], w_out1, 1, loss_target=target)

    dhu1, dw_out1, dcb1 = _ffn_bwd_act(pre1, w_out1, dy, 1)
    ex = to_sibling({"f_w_out1": dw_out1.reshape(N_SHARDS, D_FF // N_SHARDS, d)})
    da1, dx3, dcw1, dgf1 = _ffn_bwd_in(dhu1, a1, cw[1], w_in1, 1, carry=ex, norm=(x3, f_norm[1:2], dy))
    ex = to_chips(ex)
    dw_in1 = _ffn_wgrad_in(hf1, da1, 1, carry=ex)
    landed(ex)
    ex = to_sibling({"f_w_in1": dw_in1})
    d_o = _mm_rows(dx3, w_o, BF16, "o_proj_bwd", trans_w=True, carry=ex)
    ex = to_chips(ex)
    dw_o = _mm_wgrad(o, dx3, "o_wgrad").reshape(N_SHARDS, d // N_SHARDS, d)
    dq, dkv, dsink, dgq, dgk = _attn_bwd(qraw, kvd, d_o, gq, gk, sinks, carry=ex)
    landed(ex)
    dw_q = _mm_wgrad(hq, dq, "q_wgrad").reshape(N_SHARDS, d // N_SHARDS, d)
    dw_kv_dup = _mm_wgrad(hk, dkv, "kv_wgrad")
    dw_kv = jnp.concatenate(
        [_fold_heads(dw_kv_dup[:, :4 * LANES]), _fold_heads(dw_kv_dup[:, 4 * LANES:])], axis=1
    ).reshape(N_SHARDS, d // N_SHARDS, 2 * N_KV_HEADS * HEAD_DIM)
    ex = to_sibling({"b_w_o": dw_o, "b_w_q": dw_q, "w_kv": dw_kv})
    dx2, dg2 = _rms_bwd(x2, [row(kv_norm), b_norm], [dkv, dq], dx3, "kvq_norm_bwd", tm=512, carry=ex,
                        through=[w_kv_dup, w_q])
    ex = to_chips(ex)
    dhu0, dw_out0, dcb0 = _ffn_bwd_act(pre0, w_out0, dx2, 0, carry=ex)
    landed(ex)
    ex = to_sibling({"f_w_out0": dw_out0.reshape(N_SHARDS, D_FF // N_SHARDS, d)})
    da0, dhf0, dcw0 = _ffn_bwd_in(dhu0, a0, cw[0], w_in0, 0, tm=2048, carry=ex)
    ex = to_chips(ex)
    dw_in0 = _ffn_wgrad_in(hf0, da0, 0, carry=ex)
    landed(ex)
    ex = to_sibling({"f_w_in0": dw_in0})
    dx1, dgf0 = _rms_bwd(x1, [f_norm[0:1]], [dhf0], dx2, "f0_norm_bwd", carry=ex)
    ex_lo, ex_hi = halves(to_chips(ex), 384)
    dz, y, dwc, dbs, dgv = _sgu_bwd(dx1, zpre, w_a_out, a_v_norm_full, w_causal, w_causal_t, b_sb, carry=ex_lo)
    dw_a_out = _mm_wgrad(y, dx1, "a_out_wgrad").reshape(N_SHARDS, d // N_SHARDS, d)
    nsub = g_a_in.shape[2]
    dw_a_in = _mm(
        h1, dz, pl.BlockSpec((t, d), lambda s, j, kk: (0, 0)), pl.BlockSpec((t, nsub), lambda s, j, kk: (0, s)),
        pl.BlockSpec((None, d, nsub), lambda s, j, kk: (s, 0, 0)), jax.ShapeDtypeStruct((N_SHARDS, d, nsub), F32),
        (N_SHARDS, 1, 1), TN, "a_in_wgrad", carry=ex_hi)
    landed_halves([ex_lo, ex_hi])

    def bias_grad(dcb):
        return jnp.transpose(dcb[:, :, 0, :], (1, 0, 2)).reshape(-1)

    g_conv_w = jnp.concatenate([dcw0[:, 0:3, :], dcw1[:, 0:3, :]], axis=1)
    g_a_v_norm = dgv[0].reshape(N_SHARDS, 1, LANES)
    rep = ["a_w_s", "a_b_s", "f_norm", "f_conv_b", "kv_norm", "k_norm", "b_norm", "b_q_norm", "b_sinks"]
    rep_g = dict(
        a_w_s=dwc.reshape(N_GROUPS * CHUNK, CHUNK), a_b_s=dbs[:, :, 0], f_norm=jnp.stack([dgf0[0], dgf1[0]]),
        f_conv_b=jnp.stack([bias_grad(dcb0), bias_grad(dcb1)]), kv_norm=dg2[0:1],
        k_norm=(dgk[0, :HEAD_DIM] + dgk[0, HEAD_DIM:])[None], b_norm=dg2[1:2],
        b_q_norm=(dgq[0, :HEAD_DIM] + dgq[0, HEAD_DIM:])[None], b_sinks=dsink[:, 0][None])
    ex_big = to_sibling({"a_w_out": dw_a_out, "a_w_in": dw_a_in})
    ex_small = to_sibling({"a_v_norm": g_a_v_norm, "f_conv_w": g_conv_w}, wire=F32)
    ex_rep = _Gather([rep_g[k] for k in rep] + [loss_lanes], relay=False)
    together = _Together([ex_big, ex_small, ex_rep])
    dh1 = _mm_rows(dz, w_a_in_flat, F32, "a_in_bwd", trans_w=True, carry=together)
    together.spread()
    ex_big, ex_small = to_chips(ex_big), to_chips(ex_small)
    together = _Together([ex_big, ex_small])
    grad_x, dg0 = _rms_bwd(x0, [a_norm_full], [dh1], dx1, "a_norm_bwd", carry=together)
    together.spread()
    landed(ex_big)
    landed(ex_small)
    (a_norm_parts,) = _exchange_alone(_ToOwners([dg0[0].reshape(N_SHARDS, 1, LANES)]), "a_norm_to_owners")

    res["f_w_out"] = update("f_w_out1", f_w_out, m_f_w_out, v_f_w_out, layer=1)
    w_in_t = [jnp.swapaxes(a_, 1, 2) for a_ in (f_w_in, m_f_w_in, v_f_w_in)]
    res["f_w_in"] = update("f_w_in1", *w_in_t, layer=1)
    res["b_w_o"] = update("b_w_o", b_w_o, m_b_w_o, v_b_w_o, layer=0)
    res["b_w_q"] = update("b_w_q", b_w_q, m_b_w_q, v_b_w_q, layer=0)
    res["w_kv"] = update("w_kv", w_kv, m_w_kv, v_w_kv)
    res["f_w_out"] = update("f_w_out0", f_w_out, m_f_w_out, v_f_w_out, layer=0, fill=res["f_w_out"])
    res["f_w_in"] = [jnp.swapaxes(o_, 1, 2) for o_ in update("f_w_in0", *w_in_t, layer=0, fill=res["f_w_in"])]
    res["a_w_out"] = update("a_w_out", a_w_out, m_a_w_out, v_a_w_out, layer=0)
    res["a_w_in"] = update("a_w_in", a_w_in, m_a_w_in, v_a_w_in, layer=0)
    res["a_v_norm"] = update("a_v_norm", a_v_norm, m_a_v_norm, v_a_v_norm)
    res["f_conv_w"] = [o_.reshape(f_conv_w.shape) for o_ in update(
        "f_conv_w", f_conv_w.reshape(6, FF_SHARD), m_f_conv_w.reshape(6, FF_SHARD), v_f_conv_w.reshape(6, FF_SHARD))]

    rep_w = dict(a_w_s=a_w_s, a_b_s=a_b_s, f_norm=f_norm, f_conv_b=f_conv_b, kv_norm=kv_norm, k_norm=k_norm,
                 b_norm=b_norm, b_q_norm=b_q_norm, b_sinks=b_sinks, a_norm=a_norm)
    rep_m = dict(a_w_s=m_a_w_s, a_b_s=m_a_b_s, f_norm=m_f_norm, f_conv_b=m_f_conv_b, kv_norm=m_kv_norm,
                 k_norm=m_k_norm, b_norm=m_b_norm, b_q_norm=m_b_q_norm, b_sinks=m_b_sinks, a_norm=m_a_norm)
    rep_v = dict(a_w_s=v_a_w_s, a_b_s=v_a_b_s, f_norm=v_f_norm, f_conv_b=v_f_conv_b, kv_norm=v_kv_norm,
                 k_norm=v_k_norm, b_norm=v_b_norm, b_q_norm=v_b_q_norm, b_sinks=v_b_sinks, a_norm=v_a_norm)
    keys = rep + ["a_norm"]
    loss = _sum_devices(ex_rep.results[-1], "loss_sum")[0, 0]
    parts = ex_rep.results[:-1] + [a_norm_parts]
    as2d = lambda a, p: a.reshape(p.shape[1:])
    rep_outs = _adamw_summed(parts, [as2d(rep_w[k], p) for k, p in zip(keys, parts)],
                             [as2d(rep_m[k], p) for k, p in zip(keys, parts)],
                             [as2d(rep_v[k], p) for k, p in zip(keys, parts)], "adamw_replicated")
    for j, key in enumerate(keys):
        res[key] = [o_.reshape(rep_w[key].shape) for o_ in rep_outs[j]]

    order = ["a_norm", "a_w_in", "a_v_norm", "a_w_s", "a_b_s", "a_w_out", "f_norm", "f_w_in", "f_conv_w", "f_conv_b",
             "f_w_out", "kv_norm", "w_kv", "k_norm", "b_norm", "b_w_q", "b_q_norm", "b_sinks", "b_w_o"]
    outs = [loss, grad_x[None]]
    for j in range(4):
        outs += [res[k][j] for k in order]
    return tuple(outs)
```

```python
import jax
import jax.numpy as jnp
from jax import lax
from jax.experimental import pallas as pl
from jax.experimental.pallas import tpu as pltpu

F32 = jnp.float32
BF16 = jnp.bfloat16
EPS = 1e-6
D_MODEL = 1024
CHUNK = 128
N_GROUPS = 8
N_SHARDS = 8
HEAD_DIM = 64
N_Q_HEADS = 16
N_KV_HEADS = 4
D_FF = 2816
FF_SHARD = 2 * D_FF // N_SHARDS
LANES = 128
NEG_BIG = -1e30
ADAM_LR = 0.001
ADAM_B1 = 0.9
ADAM_B2 = 0.999
ADAM_EPS = 1e-08
ADAM_WD = 0.01
ADAM_STEP = 10
VMEM_LIMIT_BYTES = 56 * 1024 * 1024
MESH = pl.DeviceIdType.MESH

NN = (((1,), (0,)), ((), ()))
NT = (((1,), (1,)), ((), ()))
TN = (((0,), (0,)), ((), ()))
SLOPES = tuple(2.0 ** (-8.0 * (h + 1) / N_Q_HEADS) for h in range(N_Q_HEADS))


def _params(sem=None):
    return pltpu.CompilerParams(dimension_semantics=sem, vmem_limit_bytes=VMEM_LIMIT_BYTES)


def _dot(a, b, dims=NN):
    return lax.dot_general(a, b, dims, preferred_element_type=F32)


def _sigmoid(x):
    return 1.0 / (1.0 + jnp.exp(-x))


def _gelu_parts(z):
    cdf = 0.5 * (1.0 + lax.erf(z * (2.0 ** -0.5)))
    pdf = jnp.exp(-0.5 * z * z) * 0.3989422804014327
    return cdf, pdf


def _coords():
    return lax.axis_index("x"), lax.axis_index("y"), lax.axis_index("c")


class _Gather:
    def __init__(self, srcs, relay=True, early=False):
        self.srcs = list(srcs)
        self.early = early
        n = len(self.srcs)
        self.relayed = [relay and s.shape[0] % 32 == 0 for s in self.srcs]
        self.out_shapes = [jax.ShapeDtypeStruct((N_SHARDS,) + s.shape, s.dtype) for s in self.srcs]
        self.sems = [pltpu.SemaphoreType.DMA((n, 9)), pltpu.SemaphoreType.DMA((n, 9)), pltpu.SemaphoreType.DMA((n,))]

    def _plan(self, src, dst, sems):
        send_sems, recv_sems, local_sems = sems
        x, y, c = _coords()
        n = len(src)

        def rows(e, dev, half=None):
            block = dst[e].at[4 * dev[0] + 2 * dev[1] + dev[2]]
            if half is None:
                return block
            nr = self.srcs[e].shape[0] // 2
            return block.at[pl.ds(half * nr, nr)]

        def copy(e, slot, block, to, half=None, from_own=False):
            return pltpu.make_async_remote_copy(
                src_ref=src[e] if from_own else rows(e, block, half), dst_ref=rows(e, block, half),
                send_sem=send_sems.at[e, slot], recv_sem=recv_sems.at[e, slot], device_id=to, device_id_type=MESH)

        return n, x, y, c, rows, copy, local_sems

    def start(self, src, dst, sems):
        n, x, y, c, rows, copy, local_sems = self._plan(src, dst, sems)
        me = (x, y, c)
        for e in range(n):
            pltpu.make_async_copy(src[e], rows(e, me), local_sems.at[e]).start()
            copy(e, 0, me, (x, y, 1 - c), from_own=True).start()
            copy(e, 1, me, (1 - x, y, c), from_own=True).start()
            copy(e, 2, me, (x, 1 - y, c), from_own=True).start()
            if not self.relayed[e]:
                copy(e, 3, me, (1 - x, 1 - y, c), from_own=True).start()

    def pass_on(self, src, dst, sems, wait=True):
        n, x, y, c, rows, copy, local_sems = self._plan(src, dst, sems)
        me, sibling = (x, y, c), (x, y, 1 - c)
        over_x, over_y, diagonal = (1 - x, y, c), (x, 1 - y, c), (1 - x, 1 - y, c)
        sent = []

        def arrived(cp):
            if wait:
                cp.wait_recv()

        def send(cp):
            if wait:
                cp.start()
            sent.append(cp)

        for slot, owner, onward, half in ((1, over_x, over_y, 0), (2, over_y, over_x, 1)):
            for e in range(n):
                arrived(copy(e, slot, owner, me))
                if self.relayed[e]:
                    send(copy(e, 3 + half, owner, onward, half=half))
                send(copy(e, 4 + slot, owner, sibling))
        for e in range(n):
            if self.relayed[e]:
                for half in (0, 1):
                    arrived(copy(e, 3 + half, diagonal, me, half=half))
                    send(copy(e, 7 + half, diagonal, sibling, half=half))
            else:
                arrived(copy(e, 3, diagonal, me))
                send(copy(e, 7, diagonal, sibling))
        return sent

    def finish(self, src, dst, sems, passed_on=False):
        n, x, y, c, rows, copy, local_sems = self._plan(src, dst, sems)
        me, sibling = (x, y, c), (x, y, 1 - c)
        over_x, over_y, diagonal = (1 - x, y, c), (x, 1 - y, c), (1 - x, 1 - y, c)
        sent = self.pass_on(src, dst, sems, wait=not passed_on)
        for e in range(n):
            copy(e, 0, sibling, me).wait_recv()
            copy(e, 5, (1 - x, y, 1 - c), me).wait_recv()
            copy(e, 6, (x, 1 - y, 1 - c), me).wait_recv()
            if self.relayed[e]:
                for half in (0, 1):
                    copy(e, 7 + half, (1 - x, 1 - y, 1 - c), me, half=half).wait_recv()
            else:
                copy(e, 7, (1 - x, 1 - y, 1 - c), me).wait_recv()
        for e in range(n):
            copy(e, 0, me, sibling, from_own=True).wait_send()
            copy(e, 1, me, over_x, from_own=True).wait_send()
            copy(e, 2, me, over_y, from_own=True).wait_send()
            if not self.relayed[e]:
                copy(e, 3, me, diagonal, from_own=True).wait_send()
            pltpu.make_async_copy(src[e], rows(e, me), local_sems.at[e]).wait()
        for cp in sent:
            cp.wait_send()


class _ToSibling:
    def __init__(self, grads):
        self.srcs = list(grads)
        n = len(self.srcs)
        self.out_shapes = [jax.ShapeDtypeStruct((4,) + g.shape[1:], g.dtype) for g in self.srcs]
        self.sems = [pltpu.SemaphoreType.DMA((n, 4)), pltpu.SemaphoreType.DMA((n, 4))]

    def _copies(self, src, dst, sems):
        send_sems, recv_sems = sems
        x, y, c = _coords()
        return [
            pltpu.make_async_remote_copy(
                src_ref=src[i].at[2 * q + (1 - c)], dst_ref=dst[i].at[q], send_sem=send_sems.at[i, q],
                recv_sem=recv_sems.at[i, q], device_id=(x, y, 1 - c), device_id_type=MESH)
            for i in range(len(src)) for q in range(4)]

    def start(self, src, dst, sems):
        for cp in self._copies(src, dst, sems):
            cp.start()

    def finish(self, src, dst, sems):
        for cp in self._copies(src, dst, sems):
            cp.wait()


class _ToChips:
    def __init__(self, psums, rows=None):
        self.srcs = list(psums)
        n = len(self.srcs)
        self.rows = rows
        self.out_shapes = [
            jax.ShapeDtypeStruct((3, p.shape[1] if rows is None else rows[1]) + p.shape[2:], p.dtype)
            for p in self.srcs]
        self.sems = [pltpu.SemaphoreType.DMA((n, 3)), pltpu.SemaphoreType.DMA((n, 3))]

    def _copies(self, src, dst, sems):
        send_sems, recv_sems = sems
        x, y, c = _coords()
        peers = [(x, 1 - y), (1 - x, y), (1 - x, 1 - y)]

        def part(i, q):
            if self.rows is None:
                return src[i].at[q]
            return src[i].at[q, pl.ds(self.rows[0], self.rows[1])]

        return [
            pltpu.make_async_remote_copy(
                src_ref=part(i, 2 * px + py), dst_ref=dst[i].at[r], send_sem=send_sems.at[i, r],
                recv_sem=recv_sems.at[i, r], device_id=(px, py, c), device_id_type=MESH)
            for i in range(len(src)) for r, (px, py) in enumerate(peers)]

    def start(self, src, dst, sems):
        for cp in self._copies(src, dst, sems):
            cp.start()

    def finish(self, src, dst, sems):
        for cp in self._copies(src, dst, sems):
            cp.wait()


class _ToOwners:
    def __init__(self, grads):
        self.srcs = list(grads)
        n = len(self.srcs)
        self.out_shapes = [jax.ShapeDtypeStruct(g.shape, g.dtype) for g in self.srcs]
        self.sems = [pltpu.SemaphoreType.DMA((n, 7)), pltpu.SemaphoreType.DMA((n, 7)), pltpu.SemaphoreType.DMA((n,))]

    def _copies(self, src, dst, sems):
        send_sems, recv_sems, local_sems = sems
        x, y, c = _coords()
        me = 4 * x + 2 * y + c
        copies = [pltpu.make_async_copy(src[i].at[me], dst[i].at[me], local_sems.at[i]) for i in range(len(src))]
        for i in range(len(src)):
            for rel in range(1, N_SHARDS):
                px = x ^ (rel >> 2) if rel >> 2 else x
                py = y ^ ((rel >> 1) & 1) if (rel >> 1) & 1 else y
                pc = c ^ (rel & 1) if rel & 1 else c
                copies.append(pltpu.make_async_remote_copy(
                    src_ref=src[i].at[4 * px + 2 * py + pc], dst_ref=dst[i].at[me], send_sem=send_sems.at[i, rel - 1],
                    recv_sem=recv_sems.at[i, rel - 1], device_id=(px, py, pc), device_id_type=MESH))
        return copies

    def start(self, src, dst, sems):
        for cp in self._copies(src, dst, sems):
            cp.start()

    def finish(self, src, dst, sems):
        for cp in self._copies(src, dst, sems):
            cp.wait()


class _Together:
    def __init__(self, parts):
        self.parts = list(parts)
        self.srcs = [s for p in self.parts for s in p.srcs]
        self.out_shapes = [s for p in self.parts for s in p.out_shapes]
        self.sems = [s for p in self.parts for s in p.sems]

    def _split(self, src, dst, sems):
        a = b = c = 0
        for p in self.parts:
            na, nc = len(p.srcs), len(p.sems)
            yield p, src[a:a + na], dst[b:b + na], sems[c:c + nc]
            a, b, c = a + na, b + na, c + nc

    def start(self, src, dst, sems):
        for p, s, d, m in self._split(src, dst, sems):
            p.start(s, d, m)

    def finish(self, src, dst, sems):
        for p, s, d, m in self._split(src, dst, sems):
            p.finish(s, d, m)

    def spread(self):
        b = 0
        for p in self.parts:
            p.results = self.results[b:b + len(p.srcs)]
            b += len(p.srcs)


def _call(body, args, *, grid, in_specs, out_specs, out_shape, name, scratch=(), sem=None, carry=None):
    out_shape, out_specs = list(out_shape), list(out_specs)
    if carry is None:
        return pl.pallas_call(
            body, grid=grid, in_specs=list(in_specs), out_specs=out_specs, out_shape=out_shape,
            scratch_shapes=list(scratch), name=name, compiler_params=_params(sem))(*args)
    n_in, n_out, n_scr, n_c = len(args), len(out_shape), len(scratch), len(carry.srcs)
    steps = tuple(grid)
    total = 1
    for n_ax in steps:
        total *= n_ax
    early = getattr(carry, "early", False) and total >= 8
    early_step = total - max(2, total // 8)

    def carried(*refs):
        ins, rest = refs[:n_in], refs[n_in:]
        c_src, rest = rest[:n_c], rest[n_c:]
        outs, rest = rest[:n_out], rest[n_out:]
        c_dst, rest = rest[:n_c], rest[n_c:]
        scr, sems = rest[:n_scr], rest[n_scr:]
        step = pl.program_id(0)
        for ax in range(1, len(steps)):
            step = step * steps[ax] + pl.program_id(ax)

        @pl.when(step == 0)
        def _():
            carry.start(c_src, c_dst, sems)

        body(*ins, *outs, *scr)

        if early:
            @pl.when(step == early_step)
            def _():
                carry.pass_on(c_src, c_dst, sems)

        @pl.when(step == total - 1)
        def _():
            if early:
                carry.finish(c_src, c_dst, sems, passed_on=True)
            else:
                carry.finish(c_src, c_dst, sems)

    hbm = pl.BlockSpec(memory_space=pl.ANY)
    res = pl.pallas_call(
        carried, grid=grid, in_specs=list(in_specs) + [hbm] * n_c, out_specs=out_specs + [hbm] * n_c,
        out_shape=out_shape + carry.out_shapes, scratch_shapes=list(scratch) + carry.sems, name=name,
        compiler_params=_params(("arbitrary",) * len(steps)))(*args, *carry.srcs)
    carry.results = list(res[n_out:])
    return list(res[:n_out])


def _exchange_alone(ex, name):
    n = len(ex.srcs)

    def body(*refs):
        src, dst, sems = refs[:n], refs[n:2 * n], refs[2 * n:]
        ex.start(src, dst, sems)
        ex.finish(src, dst, sems)

    hbm = pl.BlockSpec(memory_space=pl.ANY)
    res = pl.pallas_call(body, in_specs=[hbm] * n, out_specs=[hbm] * n, out_shape=ex.out_shapes,
                         scratch_shapes=ex.sems, name=name)(*ex.srcs)
    ex.results = list(res)
    return ex.results


def _rms_bwd(x, gains, dhs, dres, name, tm=512, carry=None, through=None):
    t, d = x.shape
    n = len(gains)
    n_w = 0 if through is None else n

    def body(*refs):
        x_ref, dres_ref = refs[0], refs[1]
        g_refs, dh_refs, w_refs = refs[2:2 + n], refs[2 + n:2 + 2 * n], refs[2 + 2 * n:2 + 2 * n + n_w]
        dx_ref, dg_ref = refs[2 + 2 * n + n_w], refs[3 + 2 * n + n_w]
        i = pl.program_id(0)

        @pl.when(i == 0)
        def _():
            dg_ref[...] = jnp.zeros_like(dg_ref)

        xf = x_ref[...]
        r = lax.rsqrt(jnp.mean(xf * xf, axis=-1, keepdims=True) + EPS)
        xhat = xf * r
        dx = dres_ref[...]
        for j in range(n):
            dh = dh_refs[j][...]
            if n_w:
                dh = _dot(dh.astype(BF16), w_refs[j][...], NT)
            dg_ref[j:j + 1, :] += jnp.sum(dh * xhat, axis=0, keepdims=True)
            gy = dh * g_refs[j][...]
            dx = dx + r * (gy - xhat * jnp.mean(gy * xhat, axis=-1, keepdims=True))
        dx_ref[...] = dx

    row = pl.BlockSpec((tm, d), lambda i: (i, 0))
    vec = pl.BlockSpec((1, d), lambda i: (0, 0))
    dh_rows = [pl.BlockSpec((tm, dh.shape[1]), lambda i: (i, 0)) for dh in dhs]
    w_full = [] if through is None else [pl.BlockSpec(w.shape, lambda i: (0, 0)) for w in through]
    return _call(body, [x, dres, *gains, *dhs, *(through or [])], grid=(t // tm,),
                 in_specs=[row, row] + [vec] * n + dh_rows + w_full,
                 out_specs=[row, pl.BlockSpec((8, d), lambda i: (0, 0))],
                 out_shape=[jax.ShapeDtypeStruct((t, d), F32), jax.ShapeDtypeStruct((8, d), F32)],
                 name=name, sem=("arbitrary",), carry=carry)


def _mm(a, b, a_spec, b_spec, o_spec, out_shape, grid, dims, name, res=None, res_spec=None, carry=None):
    nk = grid[2]
    acc_shape = tuple(s for s in o_spec.block_shape if s is not None)

    def body(*refs):
        a_ref, b_ref = refs[0], refs[1]
        r_ref = refs[2] if res is not None else None
        o_ref = refs[3] if res is not None else refs[2]
        p = _dot(a_ref[...].astype(BF16), b_ref[...].astype(BF16), dims)
        if nk == 1:
            if res is not None:
                p = p + r_ref[...]
            o_ref[...] = p.astype(o_ref.dtype)
            return
        acc_ref = refs[-1]
        k = pl.program_id(2)

        @pl.when(k == 0)
        def _():
            acc_ref[...] = p

        @pl.when(k > 0)
        def _():
            acc_ref[...] += p

        @pl.when(k == nk - 1)
        def _():
            out = acc_ref[...]
            if res is not None:
                out = out + r_ref[...]
            o_ref[...] = out.astype(o_ref.dtype)

    ins = [a, b] + ([res] if res is not None else [])
    specs = [a_spec, b_spec] + ([res_spec] if res is not None else [])
    return _call(body, ins, grid=grid, in_specs=specs, out_specs=[o_spec], out_shape=[out_shape],
                 scratch=[pltpu.VMEM(acc_shape, F32)] if nk > 1 else [], name=name,
                 sem=("parallel", "parallel", "arbitrary"), carry=carry)[0]


def _mm_rows(a, w, out_dtype, name, trans_w=False, res=None, tm=1024, carry=None):
    t, k = a.shape
    tm = min(tm, t)
    n = w.shape[0] if trans_w else w.shape[1]
    return _mm(
        a, w, pl.BlockSpec((tm, k), lambda i, j, kk: (i, 0)), pl.BlockSpec(w.shape, lambda i, j, kk: (0, 0)),
        pl.BlockSpec((tm, n), lambda i, j, kk: (i, 0)), jax.ShapeDtypeStruct((t, n), out_dtype), (t // tm, 1, 1),
        NT if trans_w else NN, name, res=res,
        res_spec=None if res is None else pl.BlockSpec((tm, n), lambda i, j, kk: (i, 0)), carry=carry)


def _mm_wgrad(a, b, name, carry=None):
    t, m = a.shape
    n = b.shape[1]
    tn = n // (2 if b.dtype == F32 else 1)
    return _mm(
        a, b, pl.BlockSpec((t, m), lambda i, j, kk: (0, 0)), pl.BlockSpec((t, tn), lambda i, j, kk: (0, j)),
        pl.BlockSpec((m, tn), lambda i, j, kk: (0, j)), jax.ShapeDtypeStruct((m, n), F32), (1, n // tn, 1), TN, name,
        carry=carry)


def _sgu_fwd(x0, g, w_in, g_v, w_c, b_sb, w_out, tm=256, carry=None):
    t, d = x0.shape
    nsub = w_in.shape[2]

    def body(x_ref, g_ref, win_ref, gv_ref, wc_ref, bsb_ref, wout_ref, zpre_ref, x1_ref, h_ref, u_s, v_s, vn_s, y_s):
        xf = x_ref[...]
        h = (xf * lax.rsqrt(jnp.mean(xf * xf, axis=-1, keepdims=True) + EPS) * g_ref[...]).astype(BF16)
        h_ref[...] = h
        for k in range(N_SHARDS):
            zk = _dot(h, win_ref[k])
            zpre_ref[:, k * nsub:(k + 1) * nsub] = zk
            cdf, _ = _gelu_parts(zk)
            if k < N_SHARDS // 2:
                u_s[:, k * nsub:(k + 1) * nsub] = zk * cdf
            else:
                v_s[:, (k - 4) * nsub:(k - 3) * nsub] = zk * cdf
        v = v_s[...]
        rv = lax.rsqrt(jnp.mean(v * v, axis=-1, keepdims=True) + EPS)
        vn_s[...] = (v * rv * gv_ref[...]).astype(BF16)
        for ci in range(tm // CHUNK):
            rows = slice(ci * CHUNK, (ci + 1) * CHUNK)
            for g in range(N_GROUPS):
                cols = slice(g * LANES, (g + 1) * LANES)
                sv = _dot(wc_ref[g], vn_s[rows, cols]) + bsb_ref[g]
                y_s[rows, cols] = (u_s[rows, cols] * sv).astype(BF16)
        x1_ref[...] = x_ref[...] + _dot(y_s[...], wout_ref[...])

    row = pl.BlockSpec((tm, d), lambda i: (i, 0))
    full = lambda a: pl.BlockSpec(a.shape, lambda i: (0,) * a.ndim)
    return _call(
        body, [x0, g, w_in, g_v, w_c, b_sb, w_out], grid=(t // tm,),
        in_specs=[row, full(g), full(w_in), full(g_v), full(w_c), full(b_sb), full(w_out)],
        out_specs=[pl.BlockSpec((tm, 2 * d), lambda i: (i, 0)), row, row],
        out_shape=[jax.ShapeDtypeStruct((t, 2 * d), F32), jax.ShapeDtypeStruct((t, d), F32),
                   jax.ShapeDtypeStruct((t, d), BF16)],
        scratch=[pltpu.VMEM((tm, d), F32), pltpu.VMEM((tm, d), F32), pltpu.VMEM((tm, d), BF16),
                 pltpu.VMEM((tm, d), BF16)],
        name="sgu_fwd", carry=carry)


def _sgu_bwd(dx1, zpre, w_out, g_v, w_c, w_ct, b_sb, tm=512, carry=None):
    t, d = dx1.shape

    def body(dx_ref, zpre_ref, wout_ref, gv_ref, wc_ref, wct_ref, bsb_ref,
             dz_ref, y_ref, dwc_ref, dbs_ref, dgv_ref, u_s, vn_s, dy_s, du_s, dvn_s):
        i = pl.program_id(0)

        @pl.when(i == 0)
        def _():
            dwc_ref[...] = jnp.zeros_like(dwc_ref)
            dbs_ref[...] = jnp.zeros_like(dbs_ref)
            dgv_ref[...] = jnp.zeros_like(dgv_ref)

        dy_s[...] = _dot(dx_ref[...].astype(BF16), wout_ref[...], NT)
        zu = zpre_ref[:, :d]
        zv = zpre_ref[:, d:]
        cdf_u, pdf_u = _gelu_parts(zu)
        cdf_v, pdf_v = _gelu_parts(zv)
        u_s[...] = zu * cdf_u
        v = zv * cdf_v
        rv = lax.rsqrt(jnp.mean(v * v, axis=-1, keepdims=True) + EPS)
        vhat = v * rv
        gv = gv_ref[...]
        vn_s[...] = (vhat * gv).astype(BF16)
        for ci in range(tm // CHUNK):
            rows = slice(ci * CHUNK, (ci + 1) * CHUNK)
            for g in range(N_GROUPS):
                cols = slice(g * LANES, (g + 1) * LANES)
                vnb = vn_s[rows, cols]
                sv = _dot(wc_ref[g], vnb) + bsb_ref[g]
                dyb = dy_s[rows, cols]
                ub = u_s[rows, cols]
                dsv = dyb * ub
                du_s[rows, cols] = dyb * sv
                y_ref[rows, cols] = (ub * sv).astype(BF16)
                dsvb = dsv.astype(BF16)
                dbs_ref[g] += dsv
                dwc_ref[g] += _dot(dsvb, vnb, NT)
                dvn_s[rows, cols] = _dot(wct_ref[g], dsvb)
        dvn = dvn_s[...]
        dgv_ref[0:1, :] += jnp.sum(dvn * vhat, axis=0, keepdims=True)
        gy = dvn * gv
        dv = rv * (gy - vhat * jnp.mean(gy * vhat, axis=-1, keepdims=True))
        dz_ref[:, :d] = (du_s[...] * (cdf_u + zu * pdf_u)).astype(BF16)
        dz_ref[:, d:] = (dv * (cdf_v + zv * pdf_v)).astype(BF16)

        @pl.when(i == t // tm - 1)
        def _():
            tri = (lax.broadcasted_iota(jnp.int32, (CHUNK, CHUNK), 0)
                   >= lax.broadcasted_iota(jnp.int32, (CHUNK, CHUNK), 1))
            for g in range(N_GROUPS):
                dwc_ref[g] = jnp.where(tri, dwc_ref[g], 0.0)
                dbs_ref[g] = jnp.broadcast_to(jnp.sum(dbs_ref[g], axis=1, keepdims=True), (CHUNK, CHUNK))

    row = pl.BlockSpec((tm, d), lambda i: (i, 0))
    row2 = pl.BlockSpec((tm, 2 * d), lambda i: (i, 0))
    full = lambda a: pl.BlockSpec(a.shape, lambda i: (0,) * a.ndim)
    grp = pl.BlockSpec((N_GROUPS, CHUNK, CHUNK), lambda i: (0, 0, 0))
    return _call(
        body, [dx1, zpre, w_out, g_v, w_c, w_ct, b_sb], grid=(t // tm,),
        in_specs=[row, row2, full(w_out), full(g_v), full(w_c), full(w_ct), full(b_sb)],
        out_specs=[row2, row, grp, grp, pl.BlockSpec((8, d), lambda i: (0, 0))],
        out_shape=[jax.ShapeDtypeStruct((t, 2 * d), BF16), jax.ShapeDtypeStruct((t, d), BF16),
                   jax.ShapeDtypeStruct((N_GROUPS, CHUNK, CHUNK), F32),
                   jax.ShapeDtypeStruct((N_GROUPS, CHUNK, CHUNK), F32), jax.ShapeDtypeStruct((8, d), F32)],
        scratch=[pltpu.VMEM((tm, d), F32), pltpu.VMEM((tm, d), BF16), pltpu.VMEM((tm, d), F32),
                 pltpu.VMEM((tm, d), F32), pltpu.VMEM((tm, d), F32)],
        name="sgu_bwd", sem=("arbitrary",), carry=carry)


ROW_CHUNK = 256
HALO = 16


def _ffn_fwd(x, g, w_in, cw, cb, w_out, layer, tm=512, carry=None, next_gains=(), loss_target=None):
    t, d = x.shape
    nc = N_SHARDS // 2
    n_gains = len(next_gains)
    with_loss = loss_target is not None

    def body(x_ref, xp_ref, g_ref, wg_ref, wu_ref, cwg_ref, cbg_ref, cwu_ref, cbu_ref, wout_ref, *rest):
        extra_in, rest = rest[:n_gains + with_loss], rest[n_gains + with_loss:]
        o_ref, hf_ref, a_ref, pre_ref = rest[:4]
        extra_out, hw_s = rest[4:-1], rest[-1]
        i, c = pl.program_id(0), pl.program_id(1)

        @pl.when(c == 0)
        def _():
            keep = jnp.where(i == 0, 0.0, 1.0)
            xw = jnp.concatenate([xp_ref[...] * keep, x_ref[...]], axis=0)
            xhat = xw * lax.rsqrt(jnp.mean(xw * xw, axis=-1, keepdims=True) + EPS)
            hw_s[...] = (xhat * g_ref[...]).astype(BF16)
            hf_ref[...] = hw_s[HALO:, :]
            o_ref[...] = x_ref[...]

        hw = hw_s[...]
        pre = []
        for j, (w_ref, cw_ref, cb_ref) in enumerate(((wg_ref, cwg_ref, cbg_ref), (wu_ref, cwu_ref, cbu_ref))):
            ab = _dot(hw, w_ref[...]).astype(BF16)
            a_ref[j] = ab[HALO:]
            win = ab.astype(F32)
            cw_v = cw_ref[...]
            pre.append(cw_v[2:3, :] * win[HALO:] + cw_v[1:2, :] * pltpu.roll(win, 1, 0)[HALO:]
                       + cw_v[0:1, :] * pltpu.roll(win, 2, 0)[HALO:] + cb_ref[...])
            pre_ref[j] = pre[j]
        act = (pre[0] * _sigmoid(pre[0]) * pre[1]).astype(BF16)
        o_ref[...] += _dot(act, wout_ref[...])

        if with_loss:
            @pl.when((i == 0) & (c == 0))
            def _():
                extra_out[-1][...] = jnp.zeros_like(extra_out[-1])

        @pl.when(c == nc - 1)
        def _():
            xn = o_ref[...]
            if n_gains:
                xhat = xn * lax.rsqrt(jnp.mean(xn * xn, axis=-1, keepdims=True) + EPS)
                for k in range(n_gains):
                    extra_out[k][...] = (xhat * extra_in[k][...]).astype(BF16)
            if with_loss:
                err = xn - extra_in[-1][...]
                extra_out[-2][...] = err * (1.0 / d)
                part = jnp.sum(jnp.sum(err * err, axis=0, keepdims=True), axis=1, keepdims=True)
                extra_out[-1][...] += jnp.broadcast_to(0.5 / d * part, extra_out[-1].shape)

    row = pl.BlockSpec((tm, d), lambda i, c: (i, 0))
    vec = pl.BlockSpec((1, d), lambda i, c: (0, 0))
    shard = lambda rows, up: pl.BlockSpec((None, rows, FF_SHARD), lambda i, c: (c + up * nc, 0, 0))
    pair = pl.BlockSpec((2, None, tm, FF_SHARD), lambda i, c: (0, c, i, 0))
    lanes = pl.BlockSpec((8, LANES), lambda i, c: (0, 0))
    outs = _call(
        body, [x, x, g, w_in, w_in, cw, cb, cw, cb, w_out, *next_gains] + ([loss_target] if with_loss else []),
        grid=(t // tm, nc),
        in_specs=[row, pl.BlockSpec((HALO, d), lambda i, c: (jnp.maximum(i * (tm // HALO) - 1, 0), 0)),
                  vec, shard(d, 0), shard(d, 1), shard(8, 0), shard(1, 0), shard(8, 1), shard(1, 1),
                  pl.BlockSpec((FF_SHARD, d), lambda i, c: (c, 0))] + [vec] * n_gains + [row] * with_loss,
        out_specs=[row, row, pair, pair] + [row] * n_gains + [row, lanes] * with_loss,
        out_shape=[jax.ShapeDtypeStruct((t, d), F32), jax.ShapeDtypeStruct((t, d), BF16),
                   jax.ShapeDtypeStruct((2, nc, t, FF_SHARD), BF16), jax.ShapeDtypeStruct((2, nc, t, FF_SHARD), F32)]
        + [jax.ShapeDtypeStruct((t, d), BF16)] * n_gains
        + [jax.ShapeDtypeStruct((t, d), F32), jax.ShapeDtypeStruct((8, LANES), F32)] * with_loss,
        scratch=[pltpu.VMEM((tm + HALO, d), BF16)], name=f"ffn{layer}_fwd", sem=("arbitrary", "arbitrary"), carry=carry)
    return (outs[0], outs[1], outs[2].reshape(N_SHARDS, t, FF_SHARD), outs[3]) + tuple(outs[4:])


def _ffn_bwd_act(pre, w_out, dxn, layer, tm=1024, carry=None):
    t, d = dxn.shape
    tm = min(tm, t)
    nc = N_SHARDS // 2

    def body(pre_ref, wout_ref, dx_ref, dhu_ref, dw_ref, dcb_ref):
        i = pl.program_id(1)

        @pl.when(i == 0)
        def _():
            dw_ref[...] = jnp.zeros_like(dw_ref)
            dcb_ref[...] = jnp.zeros_like(dcb_ref)

        hg, hu = pre_ref[0], pre_ref[1]
        sg = _sigmoid(hg)
        sl = hg * sg
        dxb = dx_ref[...].astype(BF16)
        dact = _dot(dxb, wout_ref[...], NT)
        dw_ref[...] += _dot((sl * hu).astype(BF16), dxb, TN)
        d_up = dact * sl
        d_gate = dact * hu * (sg * (1.0 + hg * (1.0 - sg)))
        for j, dv in enumerate((d_gate, d_up)):
            dhu_ref[j] = dv.astype(BF16)
            dcb_ref[j, 0:1, :] += jnp.sum(dv, axis=0, keepdims=True)

    return _call(
        body, [pre, w_out, dxn], grid=(nc, t // tm),
        in_specs=[pl.BlockSpec((2, None, tm, FF_SHARD), lambda c, i: (0, c, i, 0)),
                  pl.BlockSpec((FF_SHARD, d), lambda c, i: (c, 0)), pl.BlockSpec((tm, d), lambda c, i: (i, 0))],
        out_specs=[pl.BlockSpec((None, 2, tm, FF_SHARD), lambda c, i: (c, 0, i, 0)),
                   pl.BlockSpec((FF_SHARD, d), lambda c, i: (c, 0)),
                   pl.BlockSpec((None, 2, 8, FF_SHARD), lambda c, i: (c, 0, 0, 0))],
        out_shape=[jax.ShapeDtypeStruct((nc, 2, t, FF_SHARD), BF16), jax.ShapeDtypeStruct((D_FF, d), F32),
                   jax.ShapeDtypeStruct((nc, 2, 8, FF_SHARD), F32)],
        name=f"ffn{layer}_bwd_act", sem=("parallel", "arbitrary"), carry=carry)


def _ffn_bwd_in(dhu, a, cw, w_in, layer, tm=1024, carry=None, norm=None):
    nc, _, t, _ = dhu.shape
    d = D_MODEL
    tm = min(tm, t)
    last_blk = t // 16 - 1
    n_norm = 0 if norm is None else 3

    def body(dh_ref, nx_ref, a_ref, cw_ref, win_ref, *rest):
        norm_refs, (da_ref, o_ref, dcw_ref), dg_refs = rest[:n_norm], rest[n_norm:n_norm + 3], rest[n_norm + 3:]
        i, s = pl.program_id(0), pl.program_id(1)

        @pl.when(s == 0)
        def _():
            o_ref[...] = jnp.zeros_like(o_ref)

        @pl.when((s == 0) & (i == 0))
        def _():
            dcw_ref[...] = jnp.zeros_like(dcw_ref)

        keep = jnp.where(i == t // tm - 1, 0.0, 1.0)
        cw = cw_ref[...]
        sums = [None] * 3
        for r0 in range(0, tm, ROW_CHUNK):
            rows = slice(r0, r0 + ROW_CHUNK)
            if r0 + ROW_CHUNK == tm:
                win = jnp.concatenate([dh_ref[rows, :].astype(F32), nx_ref[...].astype(F32) * keep], axis=0)
            else:
                win = dh_ref[r0:r0 + ROW_CHUNK + HALO, :].astype(F32)
            n = ROW_CHUNK + HALO
            taps = (pltpu.roll(win, n - 2, 0)[:ROW_CHUNK],
                    pltpu.roll(win, n - 1, 0)[:ROW_CHUNK],
                    win[:ROW_CHUNK])
            da = (cw[0:1, :] * taps[0] + cw[1:2, :] * taps[1] + cw[2:3, :] * taps[2]).astype(BF16)
            da_ref[rows, :] = da
            o_ref[rows, :] += _dot(da, win_ref[...], NT)
            af = a_ref[rows, :].astype(F32)
            parts = [jnp.sum(taps[k] * af, axis=0, keepdims=True) for k in range(3)]
            sums = [p if q is None else q + p for q, p in zip(sums, parts)]
        for k in range(3):
            dcw_ref[pl.ds(s, 1), k:k + 1, :] += sums[k][None]

        if norm is not None:
            x_ref, g_ref, dres_ref = norm_refs
            dg_ref = dg_refs[0]

            @pl.when((s == 0) & (i == 0))
            def _():
                dg_ref[...] = jnp.zeros_like(dg_ref)

            @pl.when(s == N_SHARDS - 1)
            def _():
                xf = x_ref[...]
                r = lax.rsqrt(jnp.mean(xf * xf, axis=-1, keepdims=True) + EPS)
                xhat = xf * r
                dh = o_ref[...]
                dg_ref[0:1, :] += jnp.sum(dh * xhat, axis=0, keepdims=True)
                gy = dh * g_ref[...]
                o_ref[...] = dres_ref[...] + r * (gy - xhat * jnp.mean(gy * xhat, axis=-1, keepdims=True))

    row = pl.BlockSpec((tm, d), lambda i, s: (i, 0))
    norm_args = [] if norm is None else list(norm)
    norm_specs = [] if norm is None else [row, pl.BlockSpec((1, d), lambda i, s: (0, 0)), row]
    return _call(
        body, [dhu, dhu, a, cw, w_in] + norm_args, grid=(t // tm, N_SHARDS),
        in_specs=[pl.BlockSpec((None, None, tm, FF_SHARD), lambda i, s: (s % nc, s // nc, i, 0)),
                  pl.BlockSpec((None, None, 16, FF_SHARD),
                               lambda i, s: (s % nc, s // nc, jnp.minimum((i + 1) * (tm // 16), last_blk), 0)),
                  pl.BlockSpec((None, tm, FF_SHARD), lambda i, s: (s, i, 0)),
                  pl.BlockSpec((None, 8, FF_SHARD), lambda i, s: (s, 0, 0)),
                  pl.BlockSpec((None, d, FF_SHARD), lambda i, s: (s, 0, 0))] + norm_specs,
        out_specs=[pl.BlockSpec((None, tm, FF_SHARD), lambda i, s: (s, i, 0)), row,
                   pl.BlockSpec((N_SHARDS, 8, FF_SHARD), lambda i, s: (0, 0, 0))]
        + ([] if norm is None else [pl.BlockSpec((8, d), lambda i, s: (0, 0))]),
        out_shape=[jax.ShapeDtypeStruct((N_SHARDS, t, FF_SHARD), BF16), jax.ShapeDtypeStruct((t, d), F32),
                   jax.ShapeDtypeStruct((N_SHARDS, 8, FF_SHARD), F32)]
        + ([] if norm is None else [jax.ShapeDtypeStruct((8, d), F32)]),
        name=f"ffn{layer}_bwd_in", sem=("arbitrary", "arbitrary"), carry=carry)


def _ffn_wgrad_in(hf, da, layer, carry=None):
    t, d = hf.shape
    return _mm(
        da, hf, pl.BlockSpec((None, t, FF_SHARD), lambda s, j, kk: (s, 0, 0)),
        pl.BlockSpec((t, d), lambda s, j, kk: (0, 0)),
        pl.BlockSpec((None, FF_SHARD, d), lambda s, j, kk: (s, 0, 0)),
        jax.ShapeDtypeStruct((N_SHARDS, FF_SHARD, d), F32), (N_SHARDS, 1, 1), TN, f"ffn{layer}_wgrad_in",
        carry=carry)


Q_PER_KV = N_Q_HEADS // N_KV_HEADS
GROUP_ROWS = Q_PER_KV * CHUNK


def _lane_half():
    return lax.broadcasted_iota(jnp.int32, (CHUNK, LANES), 1) < HEAD_DIM


def _fill_attn_bias(bias_s):
    tq = lax.broadcasted_iota(jnp.int32, (GROUP_ROWS, 2 * CHUNK), 0) & (CHUNK - 1)
    jk = lax.broadcasted_iota(jnp.int32, (GROUP_ROWS, 2 * CHUNK), 1)
    dist = tq + CHUNK - jk
    window = (dist >= 0) & (dist < CHUNK)
    distf = dist.astype(F32)
    for kvh in range(N_KV_HEADS):
        alibi = _per_head_column([-SLOPES[h] for h in range(Q_PER_KV * kvh, Q_PER_KV * (kvh + 1))]) * distf
        bias_s[0, kvh] = jnp.where(window & (jk >= CHUNK), alibi, NEG_BIG)
        bias_s[1, kvh] = jnp.where(window, alibi, NEG_BIG)


def _per_head_column(values):
    r = lax.broadcasted_iota(jnp.int32, (GROUP_ROWS, 1), 0)
    col = jnp.full((GROUP_ROWS, 1), values[Q_PER_KV - 1], F32)
    for j in range(Q_PER_KV - 2, -1, -1):
        col = jnp.where(r < (j + 1) * CHUNK, values[j], col)
    return col


def _half_sum(x, lo):
    s_lo = jnp.sum(jnp.where(lo, x, 0.0), axis=-1, keepdims=True)
    s_hi = jnp.sum(jnp.where(lo, 0.0, x), axis=-1, keepdims=True)
    return jnp.where(lo, s_lo, s_hi)


def _stack_heads(pairs, lo):
    zero = jnp.zeros_like(pairs[0])
    return jnp.concatenate([jnp.where(lo, pairs[0], zero), jnp.where(lo, zero, pairs[0]),
                            jnp.where(lo, pairs[1], zero), jnp.where(lo, zero, pairs[1])], axis=0)


def _unstack_heads(stacked, lo):
    return (jnp.where(lo, stacked[0:CHUNK], stacked[CHUNK:2 * CHUNK]),
            jnp.where(lo, stacked[2 * CHUNK:3 * CHUNK], stacked[3 * CHUNK:]))


def _attn_probs(qs, kn, bias, sink_col):
    s = _dot(qs, kn, NT) * (HEAD_DIM ** -0.5) + bias
    m = jnp.maximum(jnp.max(s, axis=-1, keepdims=True), sink_col)
    e = jnp.exp(s - m)
    den = jnp.sum(e, axis=-1, keepdims=True) + jnp.exp(sink_col - m)
    return e * (1.0 / den), m, den


def _attn_fwd(qraw, kvd, gq, gk, sinks, carry=None):
    t, d = qraw.shape
    nb = t // CHUNK

    def body(sink_ref, q_ref, cur_ref, prev_ref, gq_ref, gk_ref, o_ref, bias_s):
        n = pl.program_id(0)

        @pl.when(n == 0)
        def _():
            _fill_attn_bias(bias_s)

        lo = _lane_half()
        which = jnp.where(n == 0, 0, 1)
        gq_v, gk_v = gq_ref[...], gk_ref[...]
        for kvh in range(N_KV_HEADS):
            ks = slice(kvh * LANES, (kvh + 1) * LANES)
            vs = slice(4 * LANES + kvh * LANES, 4 * LANES + (kvh + 1) * LANES)
            kraw = jnp.concatenate([prev_ref[:, ks], cur_ref[:, ks]], axis=0)
            rk = lax.rsqrt(jnp.mean(kraw * kraw, axis=-1, keepdims=True) + EPS)
            kn = (kraw * rk * gk_v).astype(BF16)
            vv = jnp.concatenate([prev_ref[:, vs], cur_ref[:, vs]], axis=0).astype(BF16)
            qn = []
            for p in range(2):
                qp = q_ref[:, (2 * kvh + p) * LANES:(2 * kvh + p + 1) * LANES]
                r = lax.rsqrt(_half_sum(qp * qp, lo) * (1.0 / HEAD_DIM) + EPS)
                qn.append(qp * r * gq_v)
            heads = range(Q_PER_KV * kvh, Q_PER_KV * (kvh + 1))
            pf, _, _ = _attn_probs(_stack_heads(qn, lo).astype(BF16), kn, bias_s[which, kvh],
                                   _per_head_column([sink_ref[h] for h in heads]))
            for p, o_pair in enumerate(_unstack_heads(_dot(pf.astype(BF16), vv), lo)):
                o_ref[:, (2 * kvh + p) * LANES:(2 * kvh + p + 1) * LANES] = o_pair.astype(BF16)

    blk = lambda f: pl.BlockSpec((CHUNK, d), f)
    vec = pl.BlockSpec((1, LANES), lambda n: (0, 0))
    return _call(
        body, [sinks, qraw, kvd, kvd, gq, gk], grid=(nb,),
        in_specs=[pl.BlockSpec(memory_space=pltpu.SMEM), blk(lambda n: (n, 0)), blk(lambda n: (n, 0)),
                  blk(lambda n: (jnp.maximum(n - 1, 0), 0)), vec, vec],
        out_specs=[blk(lambda n: (n, 0))], out_shape=[jax.ShapeDtypeStruct((t, d), BF16)],
        scratch=[pltpu.VMEM((2, N_KV_HEADS, GROUP_ROWS, 2 * CHUNK), F32)], name="attn_fwd", sem=("arbitrary",),
        carry=carry)[0]


def _attn_bwd(qraw, kvd, d_o, gq, gk, sinks, carry=None):
    t, d = qraw.shape
    nb = t // CHUNK

    def body(sink_ref, q_ref, cur_ref, prev_ref, do_ref, gq_ref, gk_ref,
             dq_ref, dkv_ref, dsink_ref, dgq_ref, dgk_ref, carry_s, pp_s, cp_s, bias_s):
        n = pl.program_id(0)

        @pl.when(n == 0)
        def _():
            carry_s[...] = jnp.zeros_like(carry_s)
            dsink_ref[...] = jnp.zeros_like(dsink_ref)
            dgq_ref[...] = jnp.zeros_like(dgq_ref)
            dgk_ref[...] = jnp.zeros_like(dgk_ref)
            _fill_attn_bias(bias_s)

        @pl.when(n < nb)
        def _():
            lo = _lane_half()
            which = jnp.where(n == 0, 0, 1)
            gq_v, gk_v = gq_ref[...], gk_ref[...]
            for kvh in range(N_KV_HEADS):
                ks = slice(kvh * LANES, (kvh + 1) * LANES)
                vs = slice(4 * LANES + kvh * LANES, 4 * LANES + (kvh + 1) * LANES)
                kraw = jnp.concatenate([prev_ref[:, ks], cur_ref[:, ks]], axis=0)
                rk = lax.rsqrt(jnp.mean(kraw * kraw, axis=-1, keepdims=True) + EPS)
                khat = kraw * rk
                kn = (khat * gk_v).astype(BF16)
                vv = jnp.concatenate([prev_ref[:, vs], cur_ref[:, vs]], axis=0).astype(BF16)
                cols = [slice((2 * kvh + p) * LANES, (2 * kvh + p + 1) * LANES) for p in range(2)]
                rq, qhat = [], []
                for p in range(2):
                    qp = q_ref[:, cols[p]]
                    rq.append(lax.rsqrt(_half_sum(qp * qp, lo) * (1.0 / HEAD_DIM) + EPS))
                    qhat.append(qp * rq[p])
                heads = range(Q_PER_KV * kvh, Q_PER_KV * (kvh + 1))
                qs = _stack_heads([qhat[p] * gq_v for p in range(2)], lo).astype(BF16)
                dos = _stack_heads([do_ref[:, cols[p]] for p in range(2)], lo)
                sink_col = _per_head_column([sink_ref[h] for h in heads])
                pf, m, den = _attn_probs(qs, kn, bias_s[which, kvh], sink_col)
                dp = _dot(dos, vv, NT)
                delta = jnp.sum(pf * dp, axis=-1, keepdims=True)
                sink_delta = jnp.exp(sink_col - m) / den * delta
                for j, h in enumerate(heads):
                    dsink_ref[h:h + 1, :] -= jnp.broadcast_to(
                        jnp.sum(sink_delta[j * CHUNK:(j + 1) * CHUNK], axis=0, keepdims=True), (1, LANES))
                ds = (pf * (dp - delta) * (HEAD_DIM ** -0.5)).astype(BF16)
                dkn = _dot(ds, qs, TN)
                dvb = _dot(pf.astype(BF16), dos, TN)
                for p, dqn in enumerate(_unstack_heads(_dot(ds, kn), lo)):
                    dgq_ref[0:1, :] += jnp.sum(dqn * qhat[p], axis=0, keepdims=True)
                    gy = dqn * gq_v
                    mq = _half_sum(gy * qhat[p], lo) * (1.0 / HEAD_DIM)
                    dq_ref[:, cols[p]] = (rq[p] * (gy - qhat[p] * mq)).astype(BF16)
                dgk_ref[0:1, :] += jnp.sum(dkn * khat, axis=0, keepdims=True)
                gyk = dkn * gk_v
                dkraw = rk * (gyk - khat * jnp.mean(gyk * khat, axis=-1, keepdims=True))
                pp_s[:, ks] = dkraw[:CHUNK]
                cp_s[:, ks] = dkraw[CHUNK:]
                pp_s[:, vs] = dvb[:CHUNK]
                cp_s[:, vs] = dvb[CHUNK:]
            dkv_ref[...] = (carry_s[...] + pp_s[...]).astype(BF16)
            carry_s[...] = cp_s[...]

        @pl.when(n == nb)
        def _():
            dkv_ref[...] = carry_s[...].astype(BF16)

    blk = lambda f: pl.BlockSpec((CHUNK, d), f)
    vec = pl.BlockSpec((1, LANES), lambda n: (0, 0))
    cur = lambda n: (jnp.minimum(n, nb - 1), 0)
    prev = lambda n: (jnp.maximum(jnp.minimum(n, nb - 1) - 1, 0), 0)
    small = lambda r: pl.BlockSpec((r, LANES), lambda n: (0, 0))
    return _call(
        body, [sinks, qraw, kvd, kvd, d_o, gq, gk], grid=(nb + 1,),
        in_specs=[pl.BlockSpec(memory_space=pltpu.SMEM), blk(cur), blk(cur), blk(prev), blk(cur), vec, vec],
        out_specs=[blk(cur), blk(lambda n: (jnp.maximum(n - 1, 0), 0)), small(N_Q_HEADS), small(8), small(8)],
        out_shape=[jax.ShapeDtypeStruct((t, d), BF16), jax.ShapeDtypeStruct((t, d), BF16),
                   jax.ShapeDtypeStruct((N_Q_HEADS, LANES), F32), jax.ShapeDtypeStruct((8, LANES), F32),
                   jax.ShapeDtypeStruct((8, LANES), F32)],
        scratch=[pltpu.VMEM((CHUNK, d), F32)] * 3 + [pltpu.VMEM((2, N_KV_HEADS, GROUP_ROWS, 2 * CHUNK), F32)],
        name="attn_bwd", sem=("arbitrary",), carry=carry)


def _adamw_math(g, w, m, v):
    m = ADAM_B1 * m + (1.0 - ADAM_B1) * g
    v = ADAM_B2 * v + (1.0 - ADAM_B2) * (g * g)
    m_hat = m / (1.0 - ADAM_B1 ** ADAM_STEP)
    v_hat = v / (1.0 - ADAM_B2 ** ADAM_STEP)
    delta = -ADAM_LR * (m_hat / (jnp.sqrt(v_hat) + ADAM_EPS) + ADAM_WD * w)
    return delta, m, v


def _row_tile(r, cap=128):
    for tr in range(min(r, cap), 0, -1):
        if r % tr == 0 and (tr % 8 == 0 or tr == r):
            return tr
    return r


def _chip_sum(grad, recv, place, name, wire_dtype):
    _, r, c = grad.shape
    tr = _row_tile(r, 256)

    def body(pl_ref, g_ref, a_ref, p_ref):
        p_ref[...] = (g_ref[...] + a_ref[...]).astype(p_ref.dtype)

    other = lambda rel, pr: pr[0] ^ (rel + 1)
    return pl.pallas_call(
        body,
        grid_spec=pltpu.PrefetchScalarGridSpec(
            num_scalar_prefetch=1, grid=(3, r // tr),
            in_specs=[pl.BlockSpec((None, None, tr, c), lambda rel, i, pr: (other(rel, pr), pr[1], i, 0)),
                      pl.BlockSpec((None, tr, c), lambda rel, i, pr: (other(rel, pr), i, 0))],
            out_specs=pl.BlockSpec((None, tr, c), lambda rel, i, pr: (other(rel, pr), i, 0))),
        out_shape=jax.ShapeDtypeStruct((4, r, c), wire_dtype), name=name, compiler_params=_params(),
    )(place, grad.reshape(4, 2, r, c), recv)


def _adamw_sharded(grad, recv, others, place, w, m, v, name, layer=None, fill=None):
    r, c = w.shape[-2:]
    tr = _row_tile(r)

    def body(pl_ref, g_ref, a_ref, oth_ref, w_ref, m_ref, v_ref, *rest):
        g_out, d_out, nm_out, nv_out = rest[-4:]
        g = g_ref[...] + a_ref[...]
        for k in range(3):
            g = g + oth_ref[k].astype(F32)
        delta, nm, nv = _adamw_math(g, w_ref[...], m_ref[...], v_ref[...])
        g_out[...] = g
        d_out[...] = delta
        nm_out[...] = nm
        nv_out[...] = nv

    if layer is None:
        row = pl.BlockSpec((tr, c), lambda i, pr: (i, 0))
    else:
        row = pl.BlockSpec((None, tr, c), lambda i, pr: (layer, i, 0))
    n_fill = 0 if fill is None else 4
    in_specs = [pl.BlockSpec((None, None, tr, c), lambda i, pr: (pr[0], pr[1], i, 0)),
                pl.BlockSpec((None, tr, c), lambda i, pr: (pr[0], i, 0)),
                pl.BlockSpec((3, tr, c), lambda i, pr: (0, i, 0)), row, row, row]
    in_specs += [pl.BlockSpec(memory_space=pl.ANY)] * n_fill
    return pl.pallas_call(
        body,
        grid_spec=pltpu.PrefetchScalarGridSpec(
            num_scalar_prefetch=1, grid=(r // tr,), in_specs=in_specs, out_specs=[row] * 4),
        out_shape=[jax.ShapeDtypeStruct(w.shape, F32)] * 4, name=name, compiler_params=_params(),
        input_output_aliases={7 + j: j for j in range(n_fill)},
    )(place, grad.reshape(4, 2, r, c), recv, others, w, m, v, *([] if fill is None else fill))


def _sum_devices(parts, name):
    def body(p_ref, o_ref):
        total = p_ref[0]
        for k in range(1, N_SHARDS):
            total = total + p_ref[k]
        o_ref[...] = total

    return pl.pallas_call(body, out_shape=jax.ShapeDtypeStruct(parts.shape[1:], F32), name=name)(parts)


def _adamw_summed(parts, ws, ms, vs, name):
    n = len(parts)

    def body(*refs):
        p_refs, w_refs, m_refs, v_refs = refs[:n], refs[n:2 * n], refs[2 * n:3 * n], refs[3 * n:4 * n]
        o_refs = refs[4 * n:]
        for i in range(n):
            g = p_refs[i][0]
            for k in range(1, N_SHARDS):
                g = g + p_refs[i][k]
            delta, nm, nv = _adamw_math(g, w_refs[i][...], m_refs[i][...], v_refs[i][...])
            o_refs[4 * i][...] = g
            o_refs[4 * i + 1][...] = delta
            o_refs[4 * i + 2][...] = nm
            o_refs[4 * i + 3][...] = nv

    shapes = [jax.ShapeDtypeStruct(w.shape, F32) for w in ws for _ in range(4)]
    outs = pl.pallas_call(body, out_shape=shapes, name=name, compiler_params=_params())(*parts, *ws, *ms, *vs)
    return [outs[4 * i:4 * i + 4] for i in range(n)]


def _dup_heads(w):
    lead = w.shape[:-1]
    w4 = w.reshape(lead + (N_KV_HEADS, 1, HEAD_DIM))
    return jnp.broadcast_to(w4, lead + (N_KV_HEADS, 2, HEAD_DIM)).reshape(lead + (N_KV_HEADS * LANES,))


def _fold_heads(g):
    lead = g.shape[:-1]
    return g.reshape(lead + (N_KV_HEADS, 2, HEAD_DIM)).sum(axis=-2).reshape(lead + (N_KV_HEADS * HEAD_DIM,))


def kernel(x, a_norm, a_w_in, a_v_norm, a_w_s, a_b_s, a_w_out, f_norm, f_w_in, f_conv_w, f_conv_b, f_w_out, kv_norm, w_kv, k_norm, b_norm, b_w_q, b_q_norm, b_sinks, b_w_o, loss_target, m_a_norm, m_a_w_in, m_a_v_norm, m_a_w_s, m_a_b_s, m_a_w_out, m_f_norm, m_f_w_in, m_f_conv_w, m_f_conv_b, m_f_w_out, m_kv_norm, m_w_kv, m_k_norm, m_b_norm, m_b_w_q, m_b_q_norm, m_b_sinks, m_b_w_o, v_a_norm, v_a_w_in, v_a_v_norm, v_a_w_s, v_a_b_s, v_a_w_out, v_f_norm, v_f_w_in, v_f_conv_w, v_f_conv_b, v_f_w_out, v_kv_norm, v_w_kv, v_k_norm, v_b_norm, v_b_w_q, v_b_q_norm, v_b_sinks, v_b_w_o):
    d = D_MODEL
    xi, yi, ci = _coords()
    place = jnp.stack([2 * xi + yi, ci]).astype(jnp.int32)
    bf = lambda a: a.astype(BF16)
    row = lambda v_: v_.reshape(1, -1)
    x0, target = x[0], loss_target[0]
    t = x0.shape[0]
    res = {}

    red = {}

    def to_sibling(grads, wire=BF16):
        for k, g in grads.items():
            red[k] = dict(grad=g, wire=wire)
        ex = _ToSibling(list(grads.values()))
        ex.names = list(grads)
        return ex

    def to_chips(ex):
        for k, a in zip(ex.names, ex.results):
            red[k]["recv"] = a
            red[k]["psum"] = _chip_sum(red[k]["grad"], a, place, f"chip_sum_{k}", red[k]["wire"])
        nxt = _ToChips([red[k]["psum"] for k in ex.names])
        nxt.names = ex.names
        return nxt

    def landed(ex):
        for k, b in zip(ex.names, ex.results):
            red[k]["others"] = b

    def halves(ex, first_rows):
        parts = []
        for r0, nr in ((0, first_rows), (first_rows, ex.srcs[0].shape[1] - first_rows)):
            part = _ToChips(ex.srcs, rows=(r0, nr))
            part.names = ex.names
            parts.append(part)
        return parts

    def landed_halves(parts):
        for j, k in enumerate(parts[0].names):
            red[k]["others"] = jnp.concatenate([p.results[j] for p in parts], axis=1)

    def update(k, w, m, v, layer=None, fill=None):
        r = red[k]
        return _adamw_sharded(r["grad"], r["recv"], r["others"], place, w, m, v,
                              f"adamw_{k}", layer=layer, fill=fill)

    g_a_in, g_a_out, g_a_norm, g_a_v_norm, g_conv = _exchange_alone(
        _Gather([bf(a_w_in[0]), bf(a_w_out[0]), a_norm, a_v_norm, f_conv_w.reshape(6, FF_SHARD)]), "gather_first")
    a_norm_full, a_v_norm_full = g_a_norm.reshape(1, d), g_a_v_norm.reshape(1, d)
    conv_w = lax.reduce_precision(g_conv.reshape(N_SHARDS, 2, 3, FF_SHARD), 8, 7)
    cw = jnp.pad(jnp.transpose(conv_w, (1, 0, 2, 3)), ((0, 0), (0, 0), (0, 5), (0, 0)))
    w_a_in_flat = jnp.transpose(g_a_in, (1, 0, 2)).reshape(d, 2 * d)
    cb = f_conv_b.reshape(2, N_SHARDS, 1, FF_SHARD)
    tri = jnp.tril(jnp.ones((CHUNK, CHUNK), dtype=bool))
    w_causal = jnp.where(tri[None], a_w_s[0], 0.0).astype(BF16)
    w_causal_t = jnp.transpose(w_causal, (0, 2, 1))
    b_sb = jnp.broadcast_to(a_b_s[0][:, :, None], (N_GROUPS, CHUNK, CHUNK))
    w_a_out = g_a_out.reshape(d, d)
    gq = jnp.tile(b_q_norm.reshape(1, HEAD_DIM), (1, 2))
    gk = jnp.tile(k_norm.reshape(1, HEAD_DIM), (1, 2))
    sinks = b_sinks.reshape(N_Q_HEADS)

    ex = _Gather([bf(f_w_in[0]), bf(f_w_out[0])])
    zpre, x1, h1 = _sgu_fwd(x0, a_norm_full, g_a_in, a_v_norm_full, w_causal, b_sb, w_a_out, carry=ex)
    w_in0, w_out0 = ex.results[0], ex.results[1].reshape(D_FF, d)
    ex = _Gather([bf(w_kv), bf(b_w_q[0]), bf(b_w_o[0]), bf(f_w_in[1])], relay=False, early=True)
    x2, hf0, a0, pre0, hk, hq = _ffn_fwd(x1, f_norm[0:1], w_in0, cw[0], cb[0], w_out0, 0, carry=ex,
                                         next_gains=[row(kv_norm), b_norm])
    kv_full = ex.results[0].reshape(d, 2 * N_KV_HEADS * HEAD_DIM)
    w_q, w_o = ex.results[1].reshape(d, d), ex.results[2].reshape(d, d)
    w_in1 = ex.results[3]
    half = N_KV_HEADS * HEAD_DIM
    w_kv_dup = jnp.concatenate([_dup_heads(kv_full[:, :half]), _dup_heads(kv_full[:, half:])], axis=1)
    kvd = _mm_rows(hk, w_kv_dup, F32, "kv_proj")
    qraw = _mm_rows(hq, w_q, F32, "q_proj")
    ex = _Gather([bf(f_w_out[1])], relay=False, early=True)
    o = _attn_fwd(qraw, kvd, gq, gk, sinks, carry=ex)
    w_out1 = ex.results[0].reshape(D_FF, d)
    x3 = _mm_rows(o, w_o, F32, "o_proj", res=x2)
    _, hf1, a1, pre1, dy, loss_lanes = _ffn_fwd(x3, f_norm[1:2], w_in1, cw[1], cb[1], w_out1, 1, loss_target=target)

    dhu1, dw_out1, dcb1 = _ffn_bwd_act(pre1, w_out1, dy, 1)
    ex = to_sibling({"f_w_out1": dw_out1.reshape(N_SHARDS, D_FF // N_SHARDS, d)})
    da1, dx3, dcw1, dgf1 = _ffn_bwd_in(dhu1, a1, cw[1], w_in1, 1, carry=ex, norm=(x3, f_norm[1:2], dy))
    ex = to_chips(ex)
    dw_in1 = _ffn_wgrad_in(hf1, da1, 1, carry=ex)
    landed(ex)
    ex = to_sibling({"f_w_in1": dw_in1})
    d_o = _mm_rows(dx3, w_o, BF16, "o_proj_bwd", trans_w=True, carry=ex)
    ex = to_chips(ex)
    dw_o = _mm_wgrad(o, dx3, "o_wgrad").reshape(N_SHARDS, d // N_SHARDS, d)
    dq, dkv, dsink, dgq, dgk = _attn_bwd(qraw, kvd, d_o, gq, gk, sinks, carry=ex)
    landed(ex)
    dw_q = _mm_wgrad(hq, dq, "q_wgrad").reshape(N_SHARDS, d // N_SHARDS, d)
    dw_kv_dup = _mm_wgrad(hk, dkv, "kv_wgrad")
    dw_kv = jnp.concatenate(
        [_fold_heads(dw_kv_dup[:, :4 * LANES]), _fold_heads(dw_kv_dup[:, 4 * LANES:])], axis=1
    ).reshape(N_SHARDS, d // N_SHARDS, 2 * N_KV_HEADS * HEAD_DIM)
    ex = to_sibling({"b_w_o": dw_o, "b_w_q": dw_q, "w_kv": dw_kv})
    dx2, dg2 = _rms_bwd(x2, [row(kv_norm), b_norm], [dkv, dq], dx3, "kvq_norm_bwd", tm=512, carry=ex,
                        through=[w_kv_dup, w_q])
    ex = to_chips(ex)
    dhu0, dw_out0, dcb0 = _ffn_bwd_act(pre0, w_out0, dx2, 0, carry=ex)
    landed(ex)
    ex = to_sibling({"f_w_out0": dw_out0.reshape(N_SHARDS, D_FF // N_SHARDS, d)})
    da0, dhf0, dcw0 = _ffn_bwd_in(dhu0, a0, cw[0], w_in0, 0, tm=2048, carry=ex)
    ex = to_chips(ex)
    dw_in0 = _ffn_wgrad_in(hf0, da0, 0, carry=ex)
    landed(ex)
    ex = to_sibling({"f_w_in0": dw_in0})
    dx1, dgf0 = _rms_bwd(x1, [f_norm[0:1]], [dhf0], dx2, "f0_norm_bwd", carry=ex)
    ex_lo, ex_hi = halves(to_chips(ex), 384)
    dz, y, dwc, dbs, dgv = _sgu_bwd(dx1, zpre, w_a_out, a_v_norm_full, w_causal, w_causal_t, b_sb, carry=ex_lo)
    dw_a_out = _mm_wgrad(y, dx1, "a_out_wgrad").reshape(N_SHARDS, d // N_SHARDS, d)
    nsub = g_a_in.shape[2]
    dw_a_in = _mm(
        h1, dz, pl.BlockSpec((t, d), lambda s, j, kk: (0, 0)), pl.BlockSpec((t, nsub), lambda s, j, kk: (0, s)),
        pl.BlockSpec((None, d, nsub), lambda s, j, kk: (s, 0, 0)), jax.ShapeDtypeStruct((N_SHARDS, d, nsub), F32),
        (N_SHARDS, 1, 1), TN, "a_in_wgrad", carry=ex_hi)
    landed_halves([ex_lo, ex_hi])

    def bias_grad(dcb):
        return jnp.transpose(dcb[:, :, 0, :], (1, 0, 2)).reshape(-1)

    g_conv_w = jnp.concatenate([dcw0[:, 0:3, :], dcw1[:, 0:3, :]], axis=1)
    g_a_v_norm = dgv[0].reshape(N_SHARDS, 1, LANES)
    rep = ["a_w_s", "a_b_s", "f_norm", "f_conv_b", "kv_norm", "k_norm", "b_norm", "b_q_norm", "b_sinks"]
    rep_g = dict(
        a_w_s=dwc.reshape(N_GROUPS * CHUNK, CHUNK), a_b_s=dbs[:, :, 0], f_norm=jnp.stack([dgf0[0], dgf1[0]]),
        f_conv_b=jnp.stack([bias_grad(dcb0), bias_grad(dcb1)]), kv_norm=dg2[0:1],
        k_norm=(dgk[0, :HEAD_DIM] + dgk[0, HEAD_DIM:])[None], b_norm=dg2[1:2],
        b_q_norm=(dgq[0, :HEAD_DIM] + dgq[0, HEAD_DIM:])[None], b_sinks=dsink[:, 0][None])
    ex_big = to_sibling({"a_w_out": dw_a_out, "a_w_in": dw_a_in})
    ex_small = to_sibling({"a_v_norm": g_a_v_norm, "f_conv_w": g_conv_w}, wire=F32)
    ex_rep = _Gather([rep_g[k] for k in rep] + [loss_lanes], relay=False)
    together = _Together([ex_big, ex_small, ex_rep])
    dh1 = _mm_rows(dz, w_a_in_flat, F32, "a_in_bwd", trans_w=True, carry=together)
    together.spread()
    ex_big, ex_small = to_chips(ex_big), to_chips(ex_small)
    together = _Together([ex_big, ex_small])
    grad_x, dg0 = _rms_bwd(x0, [a_norm_full], [dh1], dx1, "a_norm_bwd", carry=together)
    together.spread()
    landed(ex_big)
    landed(ex_small)
    (a_norm_parts,) = _exchange_alone(_ToOwners([dg0[0].reshape(N_SHARDS, 1, LANES)]), "a_norm_to_owners")

    res["f_w_out"] = update("f_w_out1", f_w_out, m_f_w_out, v_f_w_out, layer=1)
    w_in_t = [jnp.swapaxes(a_, 1, 2) for a_ in (f_w_in, m_f_w_in, v_f_w_in)]
    res["f_w_in"] = update("f_w_in1", *w_in_t, layer=1)
    res["b_w_o"] = update("b_w_o", b_w_o, m_b_w_o, v_b_w_o, layer=0)
    res["b_w_q"] = update("b_w_q", b_w_q, m_b_w_q, v_b_w_q, layer=0)
    res["w_kv"] = update("w_kv", w_kv, m_w_kv, v_w_kv)
    res["f_w_out"] = update("f_w_out0", f_w_out, m_f_w_out, v_f_w_out, layer=0, fill=res["f_w_out"])
    res["f_w_in"] = [jnp.swapaxes(o_, 1, 2) for o_ in update("f_w_in0", *w_in_t, layer=0, fill=res["f_w_in"])]
    res["a_w_out"] = update("a_w_out", a_w_out, m_a_w_out, v_a_w_out, layer=0)
    res["a_w_in"] = update("a_w_in", a_w_in, m_a_w_in, v_a_w_in, layer=0)
    res["a_v_norm"] = update("a_v_norm", a_v_norm, m_a_v_norm, v_a_v_norm)
    res["f_conv_w"] = [o_.reshape(f_conv_w.shape) for o_ in update(
        "f_conv_w", f_conv_w.reshape(6, FF_SHARD), m_f_conv_w.reshape(6, FF_SHARD), v_f_conv_w.reshape(6, FF_SHARD))]

    rep_w = dict(a_w_s=a_w_s, a_b_s=a_b_s, f_norm=f_norm, f_conv_b=f_conv_b, kv_norm=kv_norm, k_norm=k_norm,
                 b_norm=b_norm, b_q_norm=b_q_norm, b_sinks=b_sinks, a_norm=a_norm)
    rep_m = dict(a_w_s=m_a_w_s, a_b_s=m_a_b_s, f_norm=m_f_norm, f_conv_b=m_f_conv_b, kv_norm=m_kv_norm,
                 k_norm=m_k_norm, b_norm=m_b_norm, b_q_norm=m_b_q_norm, b_sinks=m_b_sinks, a_norm=m_a_norm)
    rep_v = dict(a_w_s=v_a_w_s, a_b_s=v_a_b_s, f_norm=v_f_norm, f_conv_b=v_f_conv_b, kv_norm=v_kv_norm,
                 k_norm=v_k_norm, b_norm=v_b_norm, b_q_norm=v_b_q_norm, b_sinks=v_b_sinks, a_norm=v_a_norm)
    keys = rep + ["a_norm"]
    loss = _sum_devices(ex_rep.results[-1], "loss_sum")[0, 0]
    parts = ex_rep.results[:-1] + [a_norm_parts]
    as2d = lambda a, p: a.reshape(p.shape[1:])
    rep_outs = _adamw_summed(parts, [as2d(rep_w[k], p) for k, p in zip(keys, parts)],
                             [as2d(rep_m[k], p) for k, p in zip(keys, parts)],
                             [as2d(rep_v[k], p) for k, p in zip(keys, parts)], "adamw_replicated")
    for j, key in enumerate(keys):
        res[key] = [o_.reshape(rep_w[key].shape) for o_ in rep_outs[j]]

    order = ["a_norm", "a_w_in", "a_v_norm", "a_w_s", "a_b_s", "a_w_out", "f_norm", "f_w_in", "f_conv_w", "f_conv_b",
             "f_w_out", "kv_norm", "w_kv", "k_norm", "b_norm", "b_w_q", "b_q_norm", "b_sinks", "b_w_o"]
    outs = [loss, grad_x[None]]
    for j in range(4):
        outs += [res[k][j] for k in order]
    return tuple(outs)
```

```python
import jax
import jax.numpy as jnp
from jax import lax
from jax.experimental import pallas as pl
from jax.experimental.pallas import tpu as pltpu

F32 = jnp.float32
BF16 = jnp.bfloat16
EPS = 1e-6
D_MODEL = 1024
CHUNK = 128
N_GROUPS = 8
N_SHARDS = 8
HEAD_DIM = 64
N_Q_HEADS = 16
N_KV_HEADS = 4
D_FF = 2816
FF_SHARD = 2 * D_FF // N_SHARDS
LANES = 128
NEG_BIG = -1e30
ADAM_LR = 0.001
ADAM_B1 = 0.9
ADAM_B2 = 0.999
ADAM_EPS = 1e-08
ADAM_WD = 0.01
ADAM_STEP = 10
VMEM_LIMIT_BYTES = 56 * 1024 * 1024
MESH = pl.DeviceIdType.MESH

NN = (((1,), (0,)), ((), ()))
NT = (((1,), (1,)), ((), ()))
TN = (((0,), (0,)), ((), ()))
SLOPES = tuple(2.0 ** (-8.0 * (h + 1) / N_Q_HEADS) for h in range(N_Q_HEADS))


def _params(sem=None):
    return pltpu.CompilerParams(dimension_semantics=sem, vmem_limit_bytes=VMEM_LIMIT_BYTES)


def _dot(a, b, dims=NN):
    return lax.dot_general(a, b, dims, preferred_element_type=F32)


def _sigmoid(x):
    return 1.0 / (1.0 + jnp.exp(-x))


def _gelu_parts(z):
    cdf = 0.5 * (1.0 + lax.erf(z * (2.0 ** -0.5)))
    pdf = jnp.exp(-0.5 * z * z) * 0.3989422804014327
    return cdf, pdf


def _coords():
    return lax.axis_index("x"), lax.axis_index("y"), lax.axis_index("c")


class _Gather:
    def __init__(self, srcs, relay=True, early=False):
        self.srcs = list(srcs)
        self.early = early
        n = len(self.srcs)
        self.relayed = [relay and s.shape[0] % 32 == 0 for s in self.srcs]
        self.out_shapes = [jax.ShapeDtypeStruct((N_SHARDS,) + s.shape, s.dtype) for s in self.srcs]
        self.sems = [pltpu.SemaphoreType.DMA((n, 9)), pltpu.SemaphoreType.DMA((n, 9)), pltpu.SemaphoreType.DMA((n,))]

    def _plan(self, src, dst, sems):
        send_sems, recv_sems, local_sems = sems
        x, y, c = _coords()
        n = len(src)

        def rows(e, dev, half=None):
            block = dst[e].at[4 * dev[0] + 2 * dev[1] + dev[2]]
            if half is None:
                return block
            nr = self.srcs[e].shape[0] // 2
            return block.at[pl.ds(half * nr, nr)]

        def copy(e, slot, block, to, half=None, from_own=False):
            return pltpu.make_async_remote_copy(
                src_ref=src[e] if from_own else rows(e, block, half), dst_ref=rows(e, block, half),
                send_sem=send_sems.at[e, slot], recv_sem=recv_sems.at[e, slot], device_id=to, device_id_type=MESH)

        return n, x, y, c, rows, copy, local_sems

    def start(self, src, dst, sems):
        n, x, y, c, rows, copy, local_sems = self._plan(src, dst, sems)
        me = (x, y, c)
        for e in range(n):
            pltpu.make_async_copy(src[e], rows(e, me), local_sems.at[e]).start()
            copy(e, 0, me, (x, y, 1 - c), from_own=True).start()
            copy(e, 1, me, (1 - x, y, c), from_own=True).start()
            copy(e, 2, me, (x, 1 - y, c), from_own=True).start()
            if not self.relayed[e]:
                copy(e, 3, me, (1 - x, 1 - y, c), from_own=True).start()

    def pass_on(self, src, dst, sems, wait=True):
        n, x, y, c, rows, copy, local_sems = self._plan(src, dst, sems)
        me, sibling = (x, y, c), (x, y, 1 - c)
        over_x, over_y, diagonal = (1 - x, y, c), (x, 1 - y, c), (1 - x, 1 - y, c)
        sent = []

        def arrived(cp):
            if wait:
                cp.wait_recv()

        def send(cp):
            if wait:
                cp.start()
            sent.append(cp)

        for slot, owner, onward, half in ((1, over_x, over_y, 0), (2, over_y, over_x, 1)):
            for e in range(n):
                arrived(copy(e, slot, owner, me))
                if self.relayed[e]:
                    send(copy(e, 3 + half, owner, onward, half=half))
                send(copy(e, 4 + slot, owner, sibling))
        for e in range(n):
            if self.relayed[e]:
                for half in (0, 1):
                    arrived(copy(e, 3 + half, diagonal, me, half=half))
                    send(copy(e, 7 + half, diagonal, sibling, half=half))
            else:
                arrived(copy(e, 3, diagonal, me))
                send(copy(e, 7, diagonal, sibling))
        return sent

    def finish(self, src, dst, sems, passed_on=False):
        n, x, y, c, rows, copy, local_sems = self._plan(src, dst, sems)
        me, sibling = (x, y, c), (x, y, 1 - c)
        over_x, over_y, diagonal = (1 - x, y, c), (x, 1 - y, c), (1 - x, 1 - y, c)
        sent = self.pass_on(src, dst, sems, wait=not passed_on)
        for e in range(n):
            copy(e, 0, sibling, me).wait_recv()
            copy(e, 5, (1 - x, y, 1 - c), me).wait_recv()
            copy(e, 6, (x, 1 - y, 1 - c), me).wait_recv()
            if self.relayed[e]:
                for half in (0, 1):
                    copy(e, 7 + half, (1 - x, 1 - y, 1 - c), me, half=half).wait_recv()
            else:
                copy(e, 7, (1 - x, 1 - y, 1 - c), me).wait_recv()
        for e in range(n):
            copy(e, 0, me, sibling, from_own=True).wait_send()
            copy(e, 1, me, over_x, from_own=True).wait_send()
            copy(e, 2, me, over_y, from_own=True).wait_send()
            if not self.relayed[e]:
                copy(e, 3, me, diagonal, from_own=True).wait_send()
            pltpu.make_async_copy(src[e], rows(e, me), local_sems.at[e]).wait()
        for cp in sent:
            cp.wait_send()


class _ToSibling:
    def __init__(self, grads):
        self.srcs = list(grads)
        n = len(self.srcs)
        self.out_shapes = [jax.ShapeDtypeStruct((4,) + g.shape[1:], g.dtype) for g in self.srcs]
        self.sems = [pltpu.SemaphoreType.DMA((n, 4)), pltpu.SemaphoreType.DMA((n, 4))]

    def _copies(self, src, dst, sems):
        send_sems, recv_sems = sems
        x, y, c = _coords()
        return [
            pltpu.make_async_remote_copy(
                src_ref=src[i].at[2 * q + (1 - c)], dst_ref=dst[i].at[q], send_sem=send_sems.at[i, q],
                recv_sem=recv_sems.at[i, q], device_id=(x, y, 1 - c), device_id_type=MESH)
            for i in range(len(src)) for q in range(4)]

    def start(self, src, dst, sems):
        for cp in self._copies(src, dst, sems):
            cp.start()

    def finish(self, src, dst, sems):
        for cp in self._copies(src, dst, sems):
            cp.wait()


class _ToChips:
    def __init__(self, psums, rows=None):
        self.srcs = list(psums)
        n = len(self.srcs)
        self.rows = rows
        self.out_shapes = [
            jax.ShapeDtypeStruct((3, p.shape[1] if rows is None else rows[1]) + p.shape[2:], p.dtype)
            for p in self.srcs]
        self.sems = [pltpu.SemaphoreType.DMA((n, 3)), pltpu.SemaphoreType.DMA((n, 3))]

    def _copies(self, src, dst, sems):
        send_sems, recv_sems = sems
        x, y, c = _coords()
        peers = [(x, 1 - y), (1 - x, y), (1 - x, 1 - y)]

        def part(i, q):
            if self.rows is None:
                return src[i].at[q]
            return src[i].at[q, pl.ds(self.rows[0], self.rows[1])]

        return [
            pltpu.make_async_remote_copy(
                src_ref=part(i, 2 * px + py), dst_ref=dst[i].at[r], send_sem=send_sems.at[i, r],
                recv_sem=recv_sems.at[i, r], device_id=(px, py, c), device_id_type=MESH)
            for i in range(len(src)) for r, (px, py) in enumerate(peers)]

    def start(self, src, dst, sems):
        for cp in self._copies(src, dst, sems):
            cp.start()

    def finish(self, src, dst, sems):
        for cp in self._copies(src, dst, sems):
            cp.wait()


class _ToOwners:
    def __init__(self, grads):
        self.srcs = list(grads)
        n = len(self.srcs)
        self.out_shapes = [jax.ShapeDtypeStruct(g.shape, g.dtype) for g in self.srcs]
        self.sems = [pltpu.SemaphoreType.DMA((n, 7)), pltpu.SemaphoreType.DMA((n, 7)), pltpu.SemaphoreType.DMA((n,))]

    def _copies(self, src, dst, sems):
        send_sems, recv_sems, local_sems = sems
        x, y, c = _coords()
        me = 4 * x + 2 * y + c
        copies = [pltpu.make_async_copy(src[i].at[me], dst[i].at[me], local_sems.at[i]) for i in range(len(src))]
        for i in range(len(src)):
            for rel in range(1, N_SHARDS):
                px = x ^ (rel >> 2) if rel >> 2 else x
                py = y ^ ((rel >> 1) & 1) if (rel >> 1) & 1 else y
                pc = c ^ (rel & 1) if rel & 1 else c
                copies.append(pltpu.make_async_remote_copy(
                    src_ref=src[i].at[4 * px + 2 * py + pc], dst_ref=dst[i].at[me], send_sem=send_sems.at[i, rel - 1],
                    recv_sem=recv_sems.at[i, rel - 1], device_id=(px, py, pc), device_id_type=MESH))
        return copies

    def start(self, src, dst, sems):
        for cp in self._copies(src, dst, sems):
            cp.start()

    def finish(self, src, dst, sems):
        for cp in self._copies(src, dst, sems):
            cp.wait()


class _Together:
    def __init__(self, parts):
        self.parts = list(parts)
        self.srcs = [s for p in self.parts for s in p.srcs]
        self.out_shapes = [s for p in self.parts for s in p.out_shapes]
        self.sems = [s for p in self.parts for s in p.sems]

    def _split(self, src, dst, sems):
        a = b = c = 0
        for p in self.parts:
            na, nc = len(p.srcs), len(p.sems)
            yield p, src[a:a + na], dst[b:b + na], sems[c:c + nc]
            a, b, c = a + na, b + na, c + nc

    def start(self, src, dst, sems):
        for p, s, d, m in self._split(src, dst, sems):
            p.start(s, d, m)

    def finish(self, src, dst, sems):
        for p, s, d, m in self._split(src, dst, sems):
            p.finish(s, d, m)

    def spread(self):
        b = 0
        for p in self.parts:
            p.results = self.results[b:b + len(p.srcs)]
            b += len(p.srcs)


def _call(body, args, *, grid, in_specs, out_specs, out_shape, name, scratch=(), sem=None, carry=None):
    out_shape, out_specs = list(out_shape), list(out_specs)
    if carry is None:
        return pl.pallas_call(
            body, grid=grid, in_specs=list(in_specs), out_specs=out_specs, out_shape=out_shape,
            scratch_shapes=list(scratch), name=name, compiler_params=_params(sem))(*args)
    n_in, n_out, n_scr, n_c = len(args), len(out_shape), len(scratch), len(carry.srcs)
    steps = tuple(grid)
    total = 1
    for n_ax in steps:
        total *= n_ax
    early = getattr(carry, "early", False) and total >= 8
    early_step = total - max(2, total // 8)

    def carried(*refs):
        ins, rest = refs[:n_in], refs[n_in:]
        c_src, rest = rest[:n_c], rest[n_c:]
        outs, rest = rest[:n_out], rest[n_out:]
        c_dst, rest = rest[:n_c], rest[n_c:]
        scr, sems = rest[:n_scr], rest[n_scr:]
        step = pl.program_id(0)
        for ax in range(1, len(steps)):
            step = step * steps[ax] + pl.program_id(ax)

        @pl.when(step == 0)
        def _():
            carry.start(c_src, c_dst, sems)

        body(*ins, *outs, *scr)

        if early:
            @pl.when(step == early_step)
            def _():
                carry.pass_on(c_src, c_dst, sems)

        @pl.when(step == total - 1)
        def _():
            if early:
                carry.finish(c_src, c_dst, sems, passed_on=True)
            else:
                carry.finish(c_src, c_dst, sems)

    hbm = pl.BlockSpec(memory_space=pl.ANY)
    res = pl.pallas_call(
        carried, grid=grid, in_specs=list(in_specs) + [hbm] * n_c, out_specs=out_specs + [hbm] * n_c,
        out_shape=out_shape + carry.out_shapes, scratch_shapes=list(scratch) + carry.sems, name=name,
        compiler_params=_params(("arbitrary",) * len(steps)))(*args, *carry.srcs)
    carry.results = list(res[n_out:])
    return list(res[:n_out])


def _exchange_alone(ex, name):
    n = len(ex.srcs)

    def body(*refs):
        src, dst, sems = refs[:n], refs[n:2 * n], refs[2 * n:]
        ex.start(src, dst, sems)
        ex.finish(src, dst, sems)

    hbm = pl.BlockSpec(memory_space=pl.ANY)
    res = pl.pallas_call(body, in_specs=[hbm] * n, out_specs=[hbm] * n, out_shape=ex.out_shapes,
                         scratch_shapes=ex.sems, name=name)(*ex.srcs)
    ex.results = list(res)
    return ex.results


def _rms_bwd(x, gains, dhs, dres, name, tm=512, carry=None, through=None):
    t, d = x.shape
    n = len(gains)
    n_w = 0 if through is None else n

    def body(*refs):
        x_ref, dres_ref = refs[0], refs[1]
        g_refs, dh_refs, w_refs = refs[2:2 + n], refs[2 + n:2 + 2 * n], refs[2 + 2 * n:2 + 2 * n + n_w]
        dx_ref, dg_ref = refs[2 + 2 * n + n_w], refs[3 + 2 * n + n_w]
        i = pl.program_id(0)

        @pl.when(i == 0)
        def _():
            dg_ref[...] = jnp.zeros_like(dg_ref)

        xf = x_ref[...]
        r = lax.rsqrt(jnp.mean(xf * xf, axis=-1, keepdims=True) + EPS)
        xhat = xf * r
        dx = dres_ref[...]
        for j in range(n):
            dh = dh_refs[j][...]
            if n_w:
                dh = _dot(dh.astype(BF16), w_refs[j][...], NT)
            dg_ref[j:j + 1, :] += jnp.sum(dh * xhat, axis=0, keepdims=True)
            gy = dh * g_refs[j][...]
            dx = dx + r * (gy - xhat * jnp.mean(gy * xhat, axis=-1, keepdims=True))
        dx_ref[...] = dx

    row = pl.BlockSpec((tm, d), lambda i: (i, 0))
    vec = pl.BlockSpec((1, d), lambda i: (0, 0))
    dh_rows = [pl.BlockSpec((tm, dh.shape[1]), lambda i: (i, 0)) for dh in dhs]
    w_full = [] if through is None else [pl.BlockSpec(w.shape, lambda i: (0, 0)) for w in through]
    return _call(body, [x, dres, *gains, *dhs, *(through or [])], grid=(t // tm,),
                 in_specs=[row, row] + [vec] * n + dh_rows + w_full,
                 out_specs=[row, pl.BlockSpec((8, d), lambda i: (0, 0))],
                 out_shape=[jax.ShapeDtypeStruct((t, d), F32), jax.ShapeDtypeStruct((8, d), F32)],
                 name=name, sem=("arbitrary",), carry=carry)


def _mm(a, b, a_spec, b_spec, o_spec, out_shape, grid, dims, name, res=None, res_spec=None, carry=None):
    nk = grid[2]
    acc_shape = tuple(s for s in o_spec.block_shape if s is not None)

    def body(*refs):
        a_ref, b_ref = refs[0], refs[1]
        r_ref = refs[2] if res is not None else None
        o_ref = refs[3] if res is not None else refs[2]
        p = _dot(a_ref[...].astype(BF16), b_ref[...].astype(BF16), dims)
        if nk == 1:
            if res is not None:
                p = p + r_ref[...]
            o_ref[...] = p.astype(o_ref.dtype)
            return
        acc_ref = refs[-1]
        k = pl.program_id(2)

        @pl.when(k == 0)
        def _():
            acc_ref[...] = p

        @pl.when(k > 0)
        def _():
            acc_ref[...] += p

        @pl.when(k == nk - 1)
        def _():
            out = acc_ref[...]
            if res is not None:
                out = out + r_ref[...]
            o_ref[...] = out.astype(o_ref.dtype)

    ins = [a, b] + ([res] if res is not None else [])
    specs = [a_spec, b_spec] + ([res_spec] if res is not None else [])
    return _call(body, ins, grid=grid, in_specs=specs, out_specs=[o_spec], out_shape=[out_shape],
                 scratch=[pltpu.VMEM(acc_shape, F32)] if nk > 1 else [], name=name,
                 sem=("parallel", "parallel", "arbitrary"), carry=carry)[0]


def _mm_rows(a, w, out_dtype, name, trans_w=False, res=None, tm=1024, carry=None):
    t, k = a.shape
    tm = min(tm, t)
    n = w.shape[0] if trans_w else w.shape[1]
    return _mm(
        a, w, pl.BlockSpec((tm, k), lambda i, j, kk: (i, 0)), pl.BlockSpec(w.shape, lambda i, j, kk: (0, 0)),
        pl.BlockSpec((tm, n), lambda i, j, kk: (i, 0)), jax.ShapeDtypeStruct((t, n), out_dtype), (t // tm, 1, 1),
        NT if trans_w else NN, name, res=res,
        res_spec=None if res is None else pl.BlockSpec((tm, n), lambda i, j, kk: (i, 0)), carry=carry)


def _mm_wgrad(a, b, name, carry=None):
    t, m = a.shape
    n = b.shape[1]
    tn = n // (4 if b.dtype == F32 else 2)
    return _mm(
        a, b, pl.BlockSpec((t, m), lambda i, j, kk: (0, 0)), pl.BlockSpec((t, tn), lambda i, j, kk: (0, j)),
        pl.BlockSpec((m, tn), lambda i, j, kk: (0, j)), jax.ShapeDtypeStruct((m, n), F32), (1, n // tn, 1), TN, name,
        carry=carry)


def _sgu_fwd(x0, g, w_in, g_v, w_c, b_sb, w_out, tm=256, carry=None):
    t, d = x0.shape
    nsub = w_in.shape[2]

    def body(x_ref, g_ref, win_ref, gv_ref, wc_ref, bsb_ref, wout_ref, zpre_ref, x1_ref, h_ref, u_s, v_s, vn_s, y_s):
        xf = x_ref[...]
        h = (xf * lax.rsqrt(jnp.mean(xf * xf, axis=-1, keepdims=True) + EPS) * g_ref[...]).astype(BF16)
        h_ref[...] = h
        for k in range(N_SHARDS):
            zk = _dot(h, win_ref[k])
            zpre_ref[:, k * nsub:(k + 1) * nsub] = zk
            cdf, _ = _gelu_parts(zk)
            if k < N_SHARDS // 2:
                u_s[:, k * nsub:(k + 1) * nsub] = zk * cdf
            else:
                v_s[:, (k - 4) * nsub:(k - 3) * nsub] = zk * cdf
        v = v_s[...]
        rv = lax.rsqrt(jnp.mean(v * v, axis=-1, keepdims=True) + EPS)
        vn_s[...] = (v * rv * gv_ref[...]).astype(BF16)
        for ci in range(tm // CHUNK):
            rows = slice(ci * CHUNK, (ci + 1) * CHUNK)
            for g in range(N_GROUPS):
                cols = slice(g * LANES, (g + 1) * LANES)
                sv = _dot(wc_ref[g], vn_s[rows, cols]) + bsb_ref[g]
                y_s[rows, cols] = (u_s[rows, cols] * sv).astype(BF16)
        x1_ref[...] = x_ref[...] + _dot(y_s[...], wout_ref[...])

    row = pl.BlockSpec((tm, d), lambda i: (i, 0))
    full = lambda a: pl.BlockSpec(a.shape, lambda i: (0,) * a.ndim)
    return _call(
        body, [x0, g, w_in, g_v, w_c, b_sb, w_out], grid=(t // tm,),
        in_specs=[row, full(g), full(w_in), full(g_v), full(w_c), full(b_sb), full(w_out)],
        out_specs=[pl.BlockSpec((tm, 2 * d), lambda i: (i, 0)), row, row],
        out_shape=[jax.ShapeDtypeStruct((t, 2 * d), F32), jax.ShapeDtypeStruct((t, d), F32),
                   jax.ShapeDtypeStruct((t, d), BF16)],
        scratch=[pltpu.VMEM((tm, d), F32), pltpu.VMEM((tm, d), F32), pltpu.VMEM((tm, d), BF16),
                 pltpu.VMEM((tm, d), BF16)],
        name="sgu_fwd", carry=carry)


def _sgu_bwd(dx1, zpre, w_out, g_v, w_c, w_ct, b_sb, tm=512, carry=None):
    t, d = dx1.shape

    def body(dx_ref, zpre_ref, wout_ref, gv_ref, wc_ref, wct_ref, bsb_ref,
             dz_ref, y_ref, dwc_ref, dbs_ref, dgv_ref, u_s, vn_s, dy_s, du_s, dvn_s):
        i = pl.program_id(0)

        @pl.when(i == 0)
        def _():
            dwc_ref[...] = jnp.zeros_like(dwc_ref)
            dbs_ref[...] = jnp.zeros_like(dbs_ref)
            dgv_ref[...] = jnp.zeros_like(dgv_ref)

        dy_s[...] = _dot(dx_ref[...].astype(BF16), wout_ref[...], NT)
        zu = zpre_ref[:, :d]
        zv = zpre_ref[:, d:]
        cdf_u, pdf_u = _gelu_parts(zu)
        cdf_v, pdf_v = _gelu_parts(zv)
        u_s[...] = zu * cdf_u
        v = zv * cdf_v
        rv = lax.rsqrt(jnp.mean(v * v, axis=-1, keepdims=True) + EPS)
        vhat = v * rv
        gv = gv_ref[...]
        vn_s[...] = (vhat * gv).astype(BF16)
        for ci in range(tm // CHUNK):
            rows = slice(ci * CHUNK, (ci + 1) * CHUNK)
            for g in range(N_GROUPS):
                cols = slice(g * LANES, (g + 1) * LANES)
                vnb = vn_s[rows, cols]
                sv = _dot(wc_ref[g], vnb) + bsb_ref[g]
                dyb = dy_s[rows, cols]
                ub = u_s[rows, cols]
                dsv = dyb * ub
                du_s[rows, cols] = dyb * sv
                y_ref[rows, cols] = (ub * sv).astype(BF16)
                dsvb = dsv.astype(BF16)
                dbs_ref[g] += dsv
                dwc_ref[g] += _dot(dsvb, vnb, NT)
                dvn_s[rows, cols] = _dot(wct_ref[g], dsvb)
        dvn = dvn_s[...]
        dgv_ref[0:1, :] += jnp.sum(dvn * vhat, axis=0, keepdims=True)
        gy = dvn * gv
        dv = rv * (gy - vhat * jnp.mean(gy * vhat, axis=-1, keepdims=True))
        dz_ref[:, :d] = (du_s[...] * (cdf_u + zu * pdf_u)).astype(BF16)
        dz_ref[:, d:] = (dv * (cdf_v + zv * pdf_v)).astype(BF16)

        @pl.when(i == t // tm - 1)
        def _():
            tri = (lax.broadcasted_iota(jnp.int32, (CHUNK, CHUNK), 0)
                   >= lax.broadcasted_iota(jnp.int32, (CHUNK, CHUNK), 1))
            for g in range(N_GROUPS):
                dwc_ref[g] = jnp.where(tri, dwc_ref[g], 0.0)
                dbs_ref[g] = jnp.broadcast_to(jnp.sum(dbs_ref[g], axis=1, keepdims=True), (CHUNK, CHUNK))

    row = pl.BlockSpec((tm, d), lambda i: (i, 0))
    row2 = pl.BlockSpec((tm, 2 * d), lambda i: (i, 0))
    full = lambda a: pl.BlockSpec(a.shape, lambda i: (0,) * a.ndim)
    grp = pl.BlockSpec((N_GROUPS, CHUNK, CHUNK), lambda i: (0, 0, 0))
    return _call(
        body, [dx1, zpre, w_out, g_v, w_c, w_ct, b_sb], grid=(t // tm,),
        in_specs=[row, row2, full(w_out), full(g_v), full(w_c), full(w_ct), full(b_sb)],
        out_specs=[row2, row, grp, grp, pl.BlockSpec((8, d), lambda i: (0, 0))],
        out_shape=[jax.ShapeDtypeStruct((t, 2 * d), BF16), jax.ShapeDtypeStruct((t, d), BF16),
                   jax.ShapeDtypeStruct((N_GROUPS, CHUNK, CHUNK), F32),
                   jax.ShapeDtypeStruct((N_GROUPS, CHUNK, CHUNK), F32), jax.ShapeDtypeStruct((8, d), F32)],
        scratch=[pltpu.VMEM((tm, d), F32), pltpu.VMEM((tm, d), BF16), pltpu.VMEM((tm, d), F32),
                 pltpu.VMEM((tm, d), F32), pltpu.VMEM((tm, d), F32)],
        name="sgu_bwd", sem=("arbitrary",), carry=carry)


ROW_CHUNK = 256
HALO = 16


def _ffn_fwd(x, g, w_in, cw, cb, w_out, layer, tm=512, carry=None, next_gains=(), loss_target=None):
    t, d = x.shape
    nc = N_SHARDS // 2
    n_gains = len(next_gains)
    with_loss = loss_target is not None

    def body(x_ref, xp_ref, g_ref, wg_ref, wu_ref, cwg_ref, cbg_ref, cwu_ref, cbu_ref, wout_ref, *rest):
        extra_in, rest = rest[:n_gains + with_loss], rest[n_gains + with_loss:]
        o_ref, hf_ref, a_ref, pre_ref = rest[:4]
        extra_out, hw_s = rest[4:-1], rest[-1]
        i, c = pl.program_id(0), pl.program_id(1)

        @pl.when(c == 0)
        def _():
            keep = jnp.where(i == 0, 0.0, 1.0)
            xw = jnp.concatenate([xp_ref[...] * keep, x_ref[...]], axis=0)
            xhat = xw * lax.rsqrt(jnp.mean(xw * xw, axis=-1, keepdims=True) + EPS)
            hw_s[...] = (xhat * g_ref[...]).astype(BF16)
            hf_ref[...] = hw_s[HALO:, :]
            o_ref[...] = x_ref[...]

        hw = hw_s[...]
        pre = []
        for j, (w_ref, cw_ref, cb_ref) in enumerate(((wg_ref, cwg_ref, cbg_ref), (wu_ref, cwu_ref, cbu_ref))):
            ab = _dot(hw, w_ref[...]).astype(BF16)
            a_ref[j] = ab[HALO:]
            win = ab.astype(F32)
            cw_v = cw_ref[...]
            pre.append(cw_v[2:3, :] * win[HALO:] + cw_v[1:2, :] * pltpu.roll(win, 1, 0)[HALO:]
                       + cw_v[0:1, :] * pltpu.roll(win, 2, 0)[HALO:] + cb_ref[...])
            pre_ref[j] = pre[j]
        act = (pre[0] * _sigmoid(pre[0]) * pre[1]).astype(BF16)
        o_ref[...] += _dot(act, wout_ref[...])

        if with_loss:
            @pl.when((i == 0) & (c == 0))
            def _():
                extra_out[-1][...] = jnp.zeros_like(extra_out[-1])

        @pl.when(c == nc - 1)
        def _():
            xn = o_ref[...]
            if n_gains:
                xhat = xn * lax.rsqrt(jnp.mean(xn * xn, axis=-1, keepdims=True) + EPS)
                for k in range(n_gains):
                    extra_out[k][...] = (xhat * extra_in[k][...]).astype(BF16)
            if with_loss:
                err = xn - extra_in[-1][...]
                extra_out[-2][...] = err * (1.0 / d)
                part = jnp.sum(jnp.sum(err * err, axis=0, keepdims=True), axis=1, keepdims=True)
                extra_out[-1][...] += jnp.broadcast_to(0.5 / d * part, extra_out[-1].shape)

    row = pl.BlockSpec((tm, d), lambda i, c: (i, 0))
    vec = pl.BlockSpec((1, d), lambda i, c: (0, 0))
    shard = lambda rows, up: pl.BlockSpec((None, rows, FF_SHARD), lambda i, c: (c + up * nc, 0, 0))
    pair = pl.BlockSpec((2, None, tm, FF_SHARD), lambda i, c: (0, c, i, 0))
    lanes = pl.BlockSpec((8, LANES), lambda i, c: (0, 0))
    outs = _call(
        body, [x, x, g, w_in, w_in, cw, cb, cw, cb, w_out, *next_gains] + ([loss_target] if with_loss else []),
        grid=(t // tm, nc),
        in_specs=[row, pl.BlockSpec((HALO, d), lambda i, c: (jnp.maximum(i * (tm // HALO) - 1, 0), 0)),
                  vec, shard(d, 0), shard(d, 1), shard(8, 0), shard(1, 0), shard(8, 1), shard(1, 1),
                  pl.BlockSpec((FF_SHARD, d), lambda i, c: (c, 0))] + [vec] * n_gains + [row] * with_loss,
        out_specs=[row, row, pair, pair] + [row] * n_gains + [row, lanes] * with_loss,
        out_shape=[jax.ShapeDtypeStruct((t, d), F32), jax.ShapeDtypeStruct((t, d), BF16),
                   jax.ShapeDtypeStruct((2, nc, t, FF_SHARD), BF16), jax.ShapeDtypeStruct((2, nc, t, FF_SHARD), F32)]
        + [jax.ShapeDtypeStruct((t, d), BF16)] * n_gains
        + [jax.ShapeDtypeStruct((t, d), F32), jax.ShapeDtypeStruct((8, LANES), F32)] * with_loss,
        scratch=[pltpu.VMEM((tm + HALO, d), BF16)], name=f"ffn{layer}_fwd", sem=("arbitrary", "arbitrary"), carry=carry)
    return (outs[0], outs[1], outs[2].reshape(N_SHARDS, t, FF_SHARD), outs[3]) + tuple(outs[4:])


def _ffn_bwd_act(pre, w_out, dxn, layer, tm=1024, carry=None):
    t, d = dxn.shape
    tm = min(tm, t)
    nc = N_SHARDS // 2

    def body(pre_ref, wout_ref, dx_ref, dhu_ref, dw_ref, dcb_ref):
        i = pl.program_id(1)

        @pl.when(i == 0)
        def _():
            dw_ref[...] = jnp.zeros_like(dw_ref)
            dcb_ref[...] = jnp.zeros_like(dcb_ref)

        hg, hu = pre_ref[0], pre_ref[1]
        sg = _sigmoid(hg)
        sl = hg * sg
        dxb = dx_ref[...].astype(BF16)
        dact = _dot(dxb, wout_ref[...], NT)
        dw_ref[...] += _dot((sl * hu).astype(BF16), dxb, TN)
        d_up = dact * sl
        d_gate = dact * hu * (sg * (1.0 + hg * (1.0 - sg)))
        for j, dv in enumerate((d_gate, d_up)):
            dhu_ref[j] = dv.astype(BF16)
            dcb_ref[j, 0:1, :] += jnp.sum(dv, axis=0, keepdims=True)

    return _call(
        body, [pre, w_out, dxn], grid=(nc, t // tm),
        in_specs=[pl.BlockSpec((2, None, tm, FF_SHARD), lambda c, i: (0, c, i, 0)),
                  pl.BlockSpec((FF_SHARD, d), lambda c, i: (c, 0)), pl.BlockSpec((tm, d), lambda c, i: (i, 0))],
        out_specs=[pl.BlockSpec((None, 2, tm, FF_SHARD), lambda c, i: (c, 0, i, 0)),
                   pl.BlockSpec((FF_SHARD, d), lambda c, i: (c, 0)),
                   pl.BlockSpec((None, 2, 8, FF_SHARD), lambda c, i: (c, 0, 0, 0))],
        out_shape=[jax.ShapeDtypeStruct((nc, 2, t, FF_SHARD), BF16), jax.ShapeDtypeStruct((D_FF, d), F32),
                   jax.ShapeDtypeStruct((nc, 2, 8, FF_SHARD), F32)],
        name=f"ffn{layer}_bwd_act", sem=("parallel", "arbitrary"), carry=carry)


def _ffn_bwd_in(dhu, a, cw, w_in, layer, tm=1024, carry=None, norm=None):
    nc, _, t, _ = dhu.shape
    d = D_MODEL
    tm = min(tm, t)
    last_blk = t // 16 - 1
    n_norm = 0 if norm is None else 3

    def body(dh_ref, nx_ref, a_ref, cw_ref, win_ref, *rest):
        norm_refs, (da_ref, o_ref, dcw_ref), dg_refs = rest[:n_norm], rest[n_norm:n_norm + 3], rest[n_norm + 3:]
        i, s = pl.program_id(0), pl.program_id(1)

        @pl.when(s == 0)
        def _():
            o_ref[...] = jnp.zeros_like(o_ref)

        @pl.when((s == 0) & (i == 0))
        def _():
            dcw_ref[...] = jnp.zeros_like(dcw_ref)

        keep = jnp.where(i == t // tm - 1, 0.0, 1.0)
        cw = cw_ref[...]
        sums = [None] * 3
        for r0 in range(0, tm, ROW_CHUNK):
            rows = slice(r0, r0 + ROW_CHUNK)
            if r0 + ROW_CHUNK == tm:
                win = jnp.concatenate([dh_ref[rows, :].astype(F32), nx_ref[...].astype(F32) * keep], axis=0)
            else:
                win = dh_ref[r0:r0 + ROW_CHUNK + HALO, :].astype(F32)
            n = ROW_CHUNK + HALO
            taps = (pltpu.roll(win, n - 2, 0)[:ROW_CHUNK],
                    pltpu.roll(win, n - 1, 0)[:ROW_CHUNK],
                    win[:ROW_CHUNK])
            da = (cw[0:1, :] * taps[0] + cw[1:2, :] * taps[1] + cw[2:3, :] * taps[2]).astype(BF16)
            da_ref[rows, :] = da
            o_ref[rows, :] += _dot(da, win_ref[...], NT)
            af = a_ref[rows, :].astype(F32)
            parts = [jnp.sum(taps[k] * af, axis=0, keepdims=True) for k in range(3)]
            sums = [p if q is None else q + p for q, p in zip(sums, parts)]
        for k in range(3):
            dcw_ref[pl.ds(s, 1), k:k + 1, :] += sums[k][None]

        if norm is not None:
            x_ref, g_ref, dres_ref = norm_refs
            dg_ref = dg_refs[0]

            @pl.when((s == 0) & (i == 0))
            def _():
                dg_ref[...] = jnp.zeros_like(dg_ref)

            @pl.when(s == N_SHARDS - 1)
            def _():
                xf = x_ref[...]
                r = lax.rsqrt(jnp.mean(xf * xf, axis=-1, keepdims=True) + EPS)
                xhat = xf * r
                dh = o_ref[...]
                dg_ref[0:1, :] += jnp.sum(dh * xhat, axis=0, keepdims=True)
                gy = dh * g_ref[...]
                o_ref[...] = dres_ref[...] + r * (gy - xhat * jnp.mean(gy * xhat, axis=-1, keepdims=True))

    row = pl.BlockSpec((tm, d), lambda i, s: (i, 0))
    norm_args = [] if norm is None else list(norm)
    norm_specs = [] if norm is None else [row, pl.BlockSpec((1, d), lambda i, s: (0, 0)), row]
    return _call(
        body, [dhu, dhu, a, cw, w_in] + norm_args, grid=(t // tm, N_SHARDS),
        in_specs=[pl.BlockSpec((None, None, tm, FF_SHARD), lambda i, s: (s % nc, s // nc, i, 0)),
                  pl.BlockSpec((None, None, 16, FF_SHARD),
                               lambda i, s: (s % nc, s // nc, jnp.minimum((i + 1) * (tm // 16), last_blk), 0)),
                  pl.BlockSpec((None, tm, FF_SHARD), lambda i, s: (s, i, 0)),
                  pl.BlockSpec((None, 8, FF_SHARD), lambda i, s: (s, 0, 0)),
                  pl.BlockSpec((None, d, FF_SHARD), lambda i, s: (s, 0, 0))] + norm_specs,
        out_specs=[pl.BlockSpec((None, tm, FF_SHARD), lambda i, s: (s, i, 0)), row,
                   pl.BlockSpec((N_SHARDS, 8, FF_SHARD), lambda i, s: (0, 0, 0))]
        + ([] if norm is None else [pl.BlockSpec((8, d), lambda i, s: (0, 0))]),
        out_shape=[jax.ShapeDtypeStruct((N_SHARDS, t, FF_SHARD), BF16), jax.ShapeDtypeStruct((t, d), F32),
                   jax.ShapeDtypeStruct((N_SHARDS, 8, FF_SHARD), F32)]
        + ([] if norm is None else [jax.ShapeDtypeStruct((8, d), F32)]),
        name=f"ffn{layer}_bwd_in", sem=("arbitrary", "arbitrary"), carry=carry)


def _ffn_wgrad_in(hf, da, layer, carry=None):
    t, d = hf.shape
    return _mm(
        da, hf, pl.BlockSpec((None, t, FF_SHARD), lambda s, j, kk: (s, 0, 0)),
        pl.BlockSpec((t, d), lambda s, j, kk: (0, 0)),
        pl.BlockSpec((None, FF_SHARD, d), lambda s, j, kk: (s, 0, 0)),
        jax.ShapeDtypeStruct((N_SHARDS, FF_SHARD, d), F32), (N_SHARDS, 1, 1), TN, f"ffn{layer}_wgrad_in",
        carry=carry)


Q_PER_KV = N_Q_HEADS // N_KV_HEADS
GROUP_ROWS = Q_PER_KV * CHUNK


def _lane_half():
    return lax.broadcasted_iota(jnp.int32, (CHUNK, LANES), 1) < HEAD_DIM


def _fill_attn_bias(bias_s):
    tq = lax.broadcasted_iota(jnp.int32, (GROUP_ROWS, 2 * CHUNK), 0) & (CHUNK - 1)
    jk = lax.broadcasted_iota(jnp.int32, (GROUP_ROWS, 2 * CHUNK), 1)
    dist = tq + CHUNK - jk
    window = (dist >= 0) & (dist < CHUNK)
    distf = dist.astype(F32)
    for kvh in range(N_KV_HEADS):
        alibi = _per_head_column([-SLOPES[h] for h in range(Q_PER_KV * kvh, Q_PER_KV * (kvh + 1))]) * distf
        bias_s[0, kvh] = jnp.where(window & (jk >= CHUNK), alibi, NEG_BIG)
        bias_s[1, kvh] = jnp.where(window, alibi, NEG_BIG)


def _per_head_column(values):
    r = lax.broadcasted_iota(jnp.int32, (GROUP_ROWS, 1), 0)
    col = jnp.full((GROUP_ROWS, 1), values[Q_PER_KV - 1], F32)
    for j in range(Q_PER_KV - 2, -1, -1):
        col = jnp.where(r < (j + 1) * CHUNK, values[j], col)
    return col


def _half_sum(x, lo):
    s_lo = jnp.sum(jnp.where(lo, x, 0.0), axis=-1, keepdims=True)
    s_hi = jnp.sum(jnp.where(lo, 0.0, x), axis=-1, keepdims=True)
    return jnp.where(lo, s_lo, s_hi)


def _stack_heads(pairs, lo):
    zero = jnp.zeros_like(pairs[0])
    return jnp.concatenate([jnp.where(lo, pairs[0], zero), jnp.where(lo, zero, pairs[0]),
                            jnp.where(lo, pairs[1], zero), jnp.where(lo, zero, pairs[1])], axis=0)


def _unstack_heads(stacked, lo):
    return (jnp.where(lo, stacked[0:CHUNK], stacked[CHUNK:2 * CHUNK]),
            jnp.where(lo, stacked[2 * CHUNK:3 * CHUNK], stacked[3 * CHUNK:]))


def _attn_probs(qs, kn, bias, sink_col):
    s = _dot(qs, kn, NT) * (HEAD_DIM ** -0.5) + bias
    m = jnp.maximum(jnp.max(s, axis=-1, keepdims=True), sink_col)
    e = jnp.exp(s - m)
    den = jnp.sum(e, axis=-1, keepdims=True) + jnp.exp(sink_col - m)
    return e * (1.0 / den), m, den


def _attn_fwd(qraw, kvd, gq, gk, sinks, carry=None):
    t, d = qraw.shape
    nb = t // CHUNK

    def body(sink_ref, q_ref, cur_ref, prev_ref, gq_ref, gk_ref, o_ref, bias_s):
        n = pl.program_id(0)

        @pl.when(n == 0)
        def _():
            _fill_attn_bias(bias_s)

        lo = _lane_half()
        which = jnp.where(n == 0, 0, 1)
        gq_v, gk_v = gq_ref[...], gk_ref[...]
        for kvh in range(N_KV_HEADS):
            ks = slice(kvh * LANES, (kvh + 1) * LANES)
            vs = slice(4 * LANES + kvh * LANES, 4 * LANES + (kvh + 1) * LANES)
            kraw = jnp.concatenate([prev_ref[:, ks], cur_ref[:, ks]], axis=0)
            rk = lax.rsqrt(jnp.mean(kraw * kraw, axis=-1, keepdims=True) + EPS)
            kn = (kraw * rk * gk_v).astype(BF16)
            vv = jnp.concatenate([prev_ref[:, vs], cur_ref[:, vs]], axis=0).astype(BF16)
            qn = []
            for p in range(2):
                qp = q_ref[:, (2 * kvh + p) * LANES:(2 * kvh + p + 1) * LANES]
                r = lax.rsqrt(_half_sum(qp * qp, lo) * (1.0 / HEAD_DIM) + EPS)
                qn.append(qp * r * gq_v)
            heads = range(Q_PER_KV * kvh, Q_PER_KV * (kvh + 1))
            pf, _, _ = _attn_probs(_stack_heads(qn, lo).astype(BF16), kn, bias_s[which, kvh],
                                   _per_head_column([sink_ref[h] for h in heads]))
            for p, o_pair in enumerate(_unstack_heads(_dot(pf.astype(BF16), vv), lo)):
                o_ref[:, (2 * kvh + p) * LANES:(2 * kvh + p + 1) * LANES] = o_pair.astype(BF16)

    blk = lambda f: pl.BlockSpec((CHUNK, d), f)
    vec = pl.BlockSpec((1, LANES), lambda n: (0, 0))
    return _call(
        body, [sinks, qraw, kvd, kvd, gq, gk], grid=(nb,),
        in_specs=[pl.BlockSpec(memory_space=pltpu.SMEM), blk(lambda n: (n, 0)), blk(lambda n: (n, 0)),
                  blk(lambda n: (jnp.maximum(n - 1, 0), 0)), vec, vec],
        out_specs=[blk(lambda n: (n, 0))], out_shape=[jax.ShapeDtypeStruct((t, d), BF16)],
        scratch=[pltpu.VMEM((2, N_KV_HEADS, GROUP_ROWS, 2 * CHUNK), F32)], name="attn_fwd", sem=("arbitrary",),
        carry=carry)[0]


def _attn_bwd(qraw, kvd, d_o, gq, gk, sinks, carry=None):
    t, d = qraw.shape
    nb = t // CHUNK

    def body(sink_ref, q_ref, cur_ref, prev_ref, do_ref, gq_ref, gk_ref,
             dq_ref, dkv_ref, dsink_ref, dgq_ref, dgk_ref, carry_s, pp_s, cp_s, bias_s):
        n = pl.program_id(0)

        @pl.when(n == 0)
        def _():
            carry_s[...] = jnp.zeros_like(carry_s)
            dsink_ref[...] = jnp.zeros_like(dsink_ref)
            dgq_ref[...] = jnp.zeros_like(dgq_ref)
            dgk_ref[...] = jnp.zeros_like(dgk_ref)
            _fill_attn_bias(bias_s)

        @pl.when(n < nb)
        def _():
            lo = _lane_half()
            which = jnp.where(n == 0, 0, 1)
            gq_v, gk_v = gq_ref[...], gk_ref[...]
            for kvh in range(N_KV_HEADS):
                ks = slice(kvh * LANES, (kvh + 1) * LANES)
                vs = slice(4 * LANES + kvh * LANES, 4 * LANES + (kvh + 1) * LANES)
                kraw = jnp.concatenate([prev_ref[:, ks], cur_ref[:, ks]], axis=0)
                rk = lax.rsqrt(jnp.mean(kraw * kraw, axis=-1, keepdims=True) + EPS)
                khat = kraw * rk
                kn = (khat * gk_v).astype(BF16)
                vv = jnp.concatenate([prev_ref[:, vs], cur_ref[:, vs]], axis=0).astype(BF16)
                cols = [slice((2 * kvh + p) * LANES, (2 * kvh + p + 1) * LANES) for p in range(2)]
                rq, qhat = [], []
                for p in range(2):
                    qp = q_ref[:, cols[p]]
                    rq.append(lax.rsqrt(_half_sum(qp * qp, lo) * (1.0 / HEAD_DIM) + EPS))
                    qhat.append(qp * rq[p])
                heads = range(Q_PER_KV * kvh, Q_PER_KV * (kvh + 1))
                qs = _stack_heads([qhat[p] * gq_v for p in range(2)], lo).astype(BF16)
                dos = _stack_heads([do_ref[:, cols[p]] for p in range(2)], lo)
                sink_col = _per_head_column([sink_ref[h] for h in heads])
                pf, m, den = _attn_probs(qs, kn, bias_s[which, kvh], sink_col)
                dp = _dot(dos, vv, NT)
                delta = jnp.sum(pf * dp, axis=-1, keepdims=True)
                sink_delta = jnp.exp(sink_col - m) / den * delta
                for j, h in enumerate(heads):
                    dsink_ref[h:h + 1, :] -= jnp.broadcast_to(
                        jnp.sum(sink_delta[j * CHUNK:(j + 1) * CHUNK], axis=0, keepdims=True), (1, LANES))
                ds = (pf * (dp - delta) * (HEAD_DIM ** -0.5)).astype(BF16)
                dkn = _dot(ds, qs, TN)
                dvb = _dot(pf.astype(BF16), dos, TN)
                for p, dqn in enumerate(_unstack_heads(_dot(ds, kn), lo)):
                    dgq_ref[0:1, :] += jnp.sum(dqn * qhat[p], axis=0, keepdims=True)
                    gy = dqn * gq_v
                    mq = _half_sum(gy * qhat[p], lo) * (1.0 / HEAD_DIM)
                    dq_ref[:, cols[p]] = (rq[p] * (gy - qhat[p] * mq)).astype(BF16)
                dgk_ref[0:1, :] += jnp.sum(dkn * khat, axis=0, keepdims=True)
                gyk = dkn * gk_v
                dkraw = rk * (gyk - khat * jnp.mean(gyk * khat, axis=-1, keepdims=True))
                pp_s[:, ks] = dkraw[:CHUNK]
                cp_s[:, ks] = dkraw[CHUNK:]
                pp_s[:, vs] = dvb[:CHUNK]
                cp_s[:, vs] = dvb[CHUNK:]
            dkv_ref[...] = (carry_s[...] + pp_s[...]).astype(BF16)
            carry_s[...] = cp_s[...]

        @pl.when(n == nb)
        def _():
            dkv_ref[...] = carry_s[...].astype(BF16)

    blk = lambda f: pl.BlockSpec((CHUNK, d), f)
    vec = pl.BlockSpec((1, LANES), lambda n: (0, 0))
    cur = lambda n: (jnp.minimum(n, nb - 1), 0)
    prev = lambda n: (jnp.maximum(jnp.minimum(n, nb - 1) - 1, 0), 0)
    small = lambda r: pl.BlockSpec((r, LANES), lambda n: (0, 0))
    return _call(
        body, [sinks, qraw, kvd, kvd, d_o, gq, gk], grid=(nb + 1,),
        in_specs=[pl.BlockSpec(memory_space=pltpu.SMEM), blk(cur), blk(cur), blk(prev), blk(cur), vec, vec],
        out_specs=[blk(cur), blk(lambda n: (jnp.maximum(n - 1, 0), 0)), small(N_Q_HEADS), small(8), small(8)],
        out_shape=[jax.ShapeDtypeStruct((t, d), BF16), jax.ShapeDtypeStruct((t, d), BF16),
                   jax.ShapeDtypeStruct((N_Q_HEADS, LANES), F32), jax.ShapeDtypeStruct((8, LANES), F32),
                   jax.ShapeDtypeStruct((8, LANES), F32)],
        scratch=[pltpu.VMEM((CHUNK, d), F32)] * 3 + [pltpu.VMEM((2, N_KV_HEADS, GROUP_ROWS, 2 * CHUNK), F32)],
        name="attn_bwd", sem=("arbitrary",), carry=carry)


def _adamw_math(g, w, m, v):
    m = ADAM_B1 * m + (1.0 - ADAM_B1) * g
    v = ADAM_B2 * v + (1.0 - ADAM_B2) * (g * g)
    m_hat = m / (1.0 - ADAM_B1 ** ADAM_STEP)
    v_hat = v / (1.0 - ADAM_B2 ** ADAM_STEP)
    delta = -ADAM_LR * (m_hat / (jnp.sqrt(v_hat) + ADAM_EPS) + ADAM_WD * w)
    return delta, m, v


def _row_tile(r, cap=128):
    for tr in range(min(r, cap), 0, -1):
        if r % tr == 0 and (tr % 8 == 0 or tr == r):
            return tr
    return r


def _chip_sum(grad, recv, place, name, wire_dtype):
    _, r, c = grad.shape
    tr = _row_tile(r, 256)

    def body(pl_ref, g_ref, a_ref, p_ref):
        p_ref[...] = (g_ref[...] + a_ref[...]).astype(p_ref.dtype)

    other = lambda rel, pr: pr[0] ^ (rel + 1)
    return pl.pallas_call(
        body,
        grid_spec=pltpu.PrefetchScalarGridSpec(
            num_scalar_prefetch=1, grid=(3, r // tr),
            in_specs=[pl.BlockSpec((None, None, tr, c), lambda rel, i, pr: (other(rel, pr), pr[1], i, 0)),
                      pl.BlockSpec((None, tr, c), lambda rel, i, pr: (other(rel, pr), i, 0))],
            out_specs=pl.BlockSpec((None, tr, c), lambda rel, i, pr: (other(rel, pr), i, 0))),
        out_shape=jax.ShapeDtypeStruct((4, r, c), wire_dtype), name=name, compiler_params=_params(),
    )(place, grad.reshape(4, 2, r, c), recv)


def _adamw_sharded(grad, recv, others, place, w, m, v, name, layer=None, fill=None):
    r, c = w.shape[-2:]
    tr = _row_tile(r)

    def body(pl_ref, g_ref, a_ref, oth_ref, w_ref, m_ref, v_ref, *rest):
        g_out, d_out, nm_out, nv_out = rest[-4:]
        g = g_ref[...] + a_ref[...]
        for k in range(3):
            g = g + oth_ref[k].astype(F32)
        delta, nm, nv = _adamw_math(g, w_ref[...], m_ref[...], v_ref[...])
        g_out[...] = g
        d_out[...] = delta
        nm_out[...] = nm
        nv_out[...] = nv

    if layer is None:
        row = pl.BlockSpec((tr, c), lambda i, pr: (i, 0))
    else:
        row = pl.BlockSpec((None, tr, c), lambda i, pr: (layer, i, 0))
    n_fill = 0 if fill is None else 4
    in_specs = [pl.BlockSpec((None, None, tr, c), lambda i, pr: (pr[0], pr[1], i, 0)),
                pl.BlockSpec((None, tr, c), lambda i, pr: (pr[0], i, 0)),
                pl.BlockSpec((3, tr, c), lambda i, pr: (0, i, 0)), row, row, row]
    in_specs += [pl.BlockSpec(memory_space=pl.ANY)] * n_fill
    return pl.pallas_call(
        body,
        grid_spec=pltpu.PrefetchScalarGridSpec(
            num_scalar_prefetch=1, grid=(r // tr,), in_specs=in_specs, out_specs=[row] * 4),
        out_shape=[jax.ShapeDtypeStruct(w.shape, F32)] * 4, name=name, compiler_params=_params(),
        input_output_aliases={7 + j: j for j in range(n_fill)},
    )(place, grad.reshape(4, 2, r, c), recv, others, w, m, v, *([] if fill is None else fill))


def _sum_devices(parts, name):
    def body(p_ref, o_ref):
        total = p_ref[0]
        for k in range(1, N_SHARDS):
            total = total + p_ref[k]
        o_ref[...] = total

    return pl.pallas_call(body, out_shape=jax.ShapeDtypeStruct(parts.shape[1:], F32), name=name)(parts)


def _adamw_summed(parts, ws, ms, vs, name):
    n = len(parts)

    def body(*refs):
        p_refs, w_refs, m_refs, v_refs = refs[:n], refs[n:2 * n], refs[2 * n:3 * n], refs[3 * n:4 * n]
        o_refs = refs[4 * n:]
        for i in range(n):
            g = p_refs[i][0]
            for k in range(1, N_SHARDS):
                g = g + p_refs[i][k]
            delta, nm, nv = _adamw_math(g, w_refs[i][...], m_refs[i][...], v_refs[i][...])
            o_refs[4 * i][...] = g
            o_refs[4 * i + 1][...] = delta
            o_refs[4 * i + 2][...] = nm
            o_refs[4 * i + 3][...] = nv

    shapes = [jax.ShapeDtypeStruct(w.shape, F32) for w in ws for _ in range(4)]
    outs = pl.pallas_call(body, out_shape=shapes, name=name, compiler_params=_params())(*parts, *ws, *ms, *vs)
    return [outs[4 * i:4 * i + 4] for i in range(n)]


def _dup_heads(w):
    lead = w.shape[:-1]
    w4 = w.reshape(lead + (N_KV_HEADS, 1, HEAD_DIM))
    return jnp.broadcast_to(w4, lead + (N_KV_HEADS, 2, HEAD_DIM)).reshape(lead + (N_KV_HEADS * LANES,))


def _fold_heads(g):
    lead = g.shape[:-1]
    return g.reshape(lead + (N_KV_HEADS, 2, HEAD_DIM)).sum(axis=-2).reshape(lead + (N_KV_HEADS * HEAD_DIM,))


def kernel(x, a_norm, a_w_in, a_v_norm, a_w_s, a_b_s, a_w_out, f_norm, f_w_in, f_conv_w, f_conv_b, f_w_out, kv_norm, w_kv, k_norm, b_norm, b_w_q, b_q_norm, b_sinks, b_w_o, loss_target, m_a_norm, m_a_w_in, m_a_v_norm, m_a_w_s, m_a_b_s, m_a_w_out, m_f_norm, m_f_w_in, m_f_conv_w, m_f_conv_b, m_f_w_out, m_kv_norm, m_w_kv, m_k_norm, m_b_norm, m_b_w_q, m_b_q_norm, m_b_sinks, m_b_w_o, v_a_norm, v_a_w_in, v_a_v_norm, v_a_w_s, v_a_b_s, v_a_w_out, v_f_norm, v_f_w_in, v_f_conv_w, v_f_conv_b, v_f_w_out, v_kv_norm, v_w_kv, v_k_norm, v_b_norm, v_b_w_q, v_b_q_norm, v_b_sinks, v_b_w_o):
    d = D_MODEL
    xi, yi, ci = _coords()
    place = jnp.stack([2 * xi + yi, ci]).astype(jnp.int32)
    bf = lambda a: a.astype(BF16)
    row = lambda v_: v_.reshape(1, -1)
    x0, target = x[0], loss_target[0]
    t = x0.shape[0]
    res = {}

    red = {}

    def to_sibling(grads, wire=BF16):
        for k, g in grads.items():
            red[k] = dict(grad=g, wire=wire)
        ex = _ToSibling(list(grads.values()))
        ex.names = list(grads)
        return ex

    def to_chips(ex):
        for k, a in zip(ex.names, ex.results):
            red[k]["recv"] = a
            red[k]["psum"] = _chip_sum(red[k]["grad"], a, place, f"chip_sum_{k}", red[k]["wire"])
        nxt = _ToChips([red[k]["psum"] for k in ex.names])
        nxt.names = ex.names
        return nxt

    def landed(ex):
        for k, b in zip(ex.names, ex.results):
            red[k]["others"] = b

    def halves(ex, *row_counts):
        parts, r0 = [], 0
        for nr in row_counts + (ex.srcs[0].shape[1] - sum(row_counts),):
            part = _ToChips(ex.srcs, rows=(r0, nr))
            part.names = ex.names
            parts.append(part)
            r0 += nr
        return parts

    def landed_halves(parts):
        for j, k in enumerate(parts[0].names):
            red[k]["others"] = jnp.concatenate([p.results[j] for p in parts], axis=1)

    def update(k, w, m, v, layer=None, fill=None):
        r = red[k]
        return _adamw_sharded(r["grad"], r["recv"], r["others"], place, w, m, v,
                              f"adamw_{k}", layer=layer, fill=fill)

    g_a_in, g_a_out, g_a_norm, g_a_v_norm, g_conv = _exchange_alone(
        _Gather([bf(a_w_in[0]), bf(a_w_out[0]), a_norm, a_v_norm, f_conv_w.reshape(6, FF_SHARD)]), "gather_first")
    a_norm_full, a_v_norm_full = g_a_norm.reshape(1, d), g_a_v_norm.reshape(1, d)
    conv_w = lax.reduce_precision(g_conv.reshape(N_SHARDS, 2, 3, FF_SHARD), 8, 7)
    cw = jnp.pad(jnp.transpose(conv_w, (1, 0, 2, 3)), ((0, 0), (0, 0), (0, 5), (0, 0)))
    w_a_in_flat = jnp.transpose(g_a_in, (1, 0, 2)).reshape(d, 2 * d)
    cb = f_conv_b.reshape(2, N_SHARDS, 1, FF_SHARD)
    tri = jnp.tril(jnp.ones((CHUNK, CHUNK), dtype=bool))
    w_causal = jnp.where(tri[None], a_w_s[0], 0.0).astype(BF16)
    w_causal_t = jnp.transpose(w_causal, (0, 2, 1))
    b_sb = jnp.broadcast_to(a_b_s[0][:, :, None], (N_GROUPS, CHUNK, CHUNK))
    w_a_out = g_a_out.reshape(d, d)
    gq = jnp.tile(b_q_norm.reshape(1, HEAD_DIM), (1, 2))
    gk = jnp.tile(k_norm.reshape(1, HEAD_DIM), (1, 2))
    sinks = b_sinks.reshape(N_Q_HEADS)

    ex = _Gather([bf(f_w_in[0]), bf(f_w_out[0])])
    zpre, x1, h1 = _sgu_fwd(x0, a_norm_full, g_a_in, a_v_norm_full, w_causal, b_sb, w_a_out, carry=ex)
    w_in0, w_out0 = ex.results[0], ex.results[1].reshape(D_FF, d)
    ex = _Gather([bf(w_kv), bf(b_w_q[0]), bf(b_w_o[0]), bf(f_w_in[1])], relay=False, early=True)
    x2, hf0, a0, pre0, hk, hq = _ffn_fwd(x1, f_norm[0:1], w_in0, cw[0], cb[0], w_out0, 0, carry=ex,
                                         next_gains=[row(kv_norm), b_norm])
    kv_full = ex.results[0].reshape(d, 2 * N_KV_HEADS * HEAD_DIM)
    w_q, w_o = ex.results[1].reshape(d, d), ex.results[2].reshape(d, d)
    w_in1 = ex.results[3]
    half = N_KV_HEADS * HEAD_DIM
    w_kv_dup = jnp.concatenate([_dup_heads(kv_full[:, :half]), _dup_heads(kv_full[:, half:])], axis=1)
    kvd = _mm_rows(hk, w_kv_dup, F32, "kv_proj")
    qraw = _mm_rows(hq, w_q, F32, "q_proj")
    ex = _Gather([bf(f_w_out[1])], relay=False, early=True)
    o = _attn_fwd(qraw, kvd, gq, gk, sinks, carry=ex)
    w_out1 = ex.results[0].reshape(D_FF, d)
    x3 = _mm_rows(o, w_o, F32, "o_proj", res=x2)
    _, hf1, a1, pre1, dy, loss_lanes = _ffn_fwd(x3, f_norm[1:2], w_in1, cw[1], cb[1], w_out1, 1, loss_target=target)

    dhu1, dw_out1, dcb1 = _ffn_bwd_act(pre1, w_out1, dy, 1)
    ex = to_sibling({"f_w_out1": dw_out1.reshape(N_SHARDS, D_FF // N_SHARDS, d)})
    da1, dx3, dcw1, dgf1 = _ffn_bwd_in(dhu1, a1, cw[1], w_in1, 1, carry=ex, norm=(x3, f_norm[1:2], dy))
    ex = to_chips(ex)
    dw_in1 = _ffn_wgrad_in(hf1, da1, 1, carry=ex)
    landed(ex)
    ex = to_sibling({"f_w_in1": dw_in1})
    d_o = _mm_rows(dx3, w_o, BF16, "o_proj_bwd", trans_w=True, carry=ex)
    ex = to_chips(ex)
    dw_o = _mm_wgrad(o, dx3, "o_wgrad").reshape(N_SHARDS, d // N_SHARDS, d)
    dq, dkv, dsink, dgq, dgk = _attn_bwd(qraw, kvd, d_o, gq, gk, sinks, carry=ex)
    landed(ex)
    dw_q = _mm_wgrad(hq, dq, "q_wgrad").reshape(N_SHARDS, d // N_SHARDS, d)
    dw_kv_dup = _mm_wgrad(hk, dkv, "kv_wgrad")
    dw_kv = jnp.concatenate(
        [_fold_heads(dw_kv_dup[:, :4 * LANES]), _fold_heads(dw_kv_dup[:, 4 * LANES:])], axis=1
    ).reshape(N_SHARDS, d // N_SHARDS, 2 * N_KV_HEADS * HEAD_DIM)
    ex = to_sibling({"b_w_o": dw_o, "b_w_q": dw_q, "w_kv": dw_kv})
    dx2, dg2 = _rms_bwd(x2, [row(kv_norm), b_norm], [dkv, dq], dx3, "kvq_norm_bwd", tm=512, carry=ex,
                        through=[w_kv_dup, w_q])
    ex = to_chips(ex)
    dhu0, dw_out0, dcb0 = _ffn_bwd_act(pre0, w_out0, dx2, 0, carry=ex)
    landed(ex)
    ex = to_sibling({"f_w_out0": dw_out0.reshape(N_SHARDS, D_FF // N_SHARDS, d)})
    da0, dhf0, dcw0 = _ffn_bwd_in(dhu0, a0, cw[0], w_in0, 0, tm=2048, carry=ex)
    ex = to_chips(ex)
    dw_in0 = _ffn_wgrad_in(hf0, da0, 0, carry=ex)
    landed(ex)
    ex = to_sibling({"f_w_in0": dw_in0})
    dx1, dgf0 = _rms_bwd(x1, [f_norm[0:1]], [dhf0], dx2, "f0_norm_bwd", carry=ex)
    ex_lo, ex_mid, ex_hi = halves(to_chips(ex), 320, 128)
    dz, y, dwc, dbs, dgv = _sgu_bwd(dx1, zpre, w_a_out, a_v_norm_full, w_causal, w_causal_t, b_sb, carry=ex_lo)
    dw_a_out = _mm_wgrad(y, dx1, "a_out_wgrad", carry=ex_mid).reshape(N_SHARDS, d // N_SHARDS, d)
    nsub = g_a_in.shape[2]
    dw_a_in = _mm(
        h1, dz, pl.BlockSpec((t, d), lambda s, j, kk: (0, 0)), pl.BlockSpec((t, nsub), lambda s, j, kk: (0, s)),
        pl.BlockSpec((None, d, nsub), lambda s, j, kk: (s, 0, 0)), jax.ShapeDtypeStruct((N_SHARDS, d, nsub), F32),
        (N_SHARDS, 1, 1), TN, "a_in_wgrad", carry=ex_hi)
    landed_halves([ex_lo, ex_mid, ex_hi])

    def bias_grad(dcb):
        return jnp.transpose(dcb[:, :, 0, :], (1, 0, 2)).reshape(-1)

    g_conv_w = jnp.concatenate([dcw0[:, 0:3, :], dcw1[:, 0:3, :]], axis=1)
    g_a_v_norm = dgv[0].reshape(N_SHARDS, 1, LANES)
    rep = ["a_w_s", "a_b_s", "f_norm", "f_conv_b", "kv_norm", "k_norm", "b_norm", "b_q_norm", "b_sinks"]
    rep_g = dict(
        a_w_s=dwc.reshape(N_GROUPS * CHUNK, CHUNK), a_b_s=dbs[:, :, 0], f_norm=jnp.stack([dgf0[0], dgf1[0]]),
        f_conv_b=jnp.stack([bias_grad(dcb0), bias_grad(dcb1)]), kv_norm=dg2[0:1],
        k_norm=(dgk[0, :HEAD_DIM] + dgk[0, HEAD_DIM:])[None], b_norm=dg2[1:2],
        b_q_norm=(dgq[0, :HEAD_DIM] + dgq[0, HEAD_DIM:])[None], b_sinks=dsink[:, 0][None])
    ex_big = to_sibling({"a_w_out": dw_a_out, "a_w_in": dw_a_in})
    ex_small = to_sibling({"a_v_norm": g_a_v_norm, "f_conv_w": g_conv_w}, wire=F32)
    ex_rep = _Gather([rep_g[k] for k in rep] + [loss_lanes], relay=False)
    together = _Together([ex_big, ex_small, ex_rep])
    dh1 = _mm_rows(dz, w_a_in_flat, F32, "a_in_bwd", trans_w=True, carry=together)
    together.spread()
    ex_big, ex_small = to_chips(ex_big), to_chips(ex_small)
    together = _Together([ex_big, ex_small])
    grad_x, dg0 = _rms_bwd(x0, [a_norm_full], [dh1], dx1, "a_norm_bwd", carry=together)
    together.spread()
    landed(ex_big)
    landed(ex_small)
    (a_norm_parts,) = _exchange_alone(_ToOwners([dg0[0].reshape(N_SHARDS, 1, LANES)]), "a_norm_to_owners")

    res["f_w_out"] = update("f_w_out1", f_w_out, m_f_w_out, v_f_w_out, layer=1)
    w_in_t = [jnp.swapaxes(a_, 1, 2) for a_ in (f_w_in, m_f_w_in, v_f_w_in)]
    res["f_w_in"] = update("f_w_in1", *w_in_t, layer=1)
    res["b_w_o"] = update("b_w_o", b_w_o, m_b_w_o, v_b_w_o, layer=0)
    res["b_w_q"] = update("b_w_q", b_w_q, m_b_w_q, v_b_w_q, layer=0)
    res["w_kv"] = update("w_kv", w_kv, m_w_kv, v_w_kv)
    res["f_w_out"] = update("f_w_out0", f_w_out, m_f_w_out, v_f_w_out, layer=0, fill=res["f_w_out"])
    res["f_w_in"] = [jnp.swapaxes(o_, 1, 2) for o_ in update("f_w_in0", *w_in_t, layer=0, fill=res["f_w_in"])]
    res["a_w_out"] = update("a_w_out", a_w_out, m_a_w_out, v_a_w_out, layer=0)
    res["a_w_in"] = update("a_w_in", a_w_in, m_a_w_in, v_a_w_in, layer=0)
    res["a_v_norm"] = update("a_v_norm", a_v_norm, m_a_v_norm, v_a_v_norm)
    res["f_conv_w"] = [o_.reshape(f_conv_w.shape) for o_ in update(
        "f_conv_w", f_conv_w.reshape(6, FF_SHARD), m_f_conv_w.reshape(6, FF_SHARD), v_f_conv_w.reshape(6, FF_SHARD))]

    rep_w = dict(a_w_s=a_w_s, a_b_s=a_b_s, f_norm=f_norm, f_conv_b=f_conv_b, kv_norm=kv_norm, k_norm=k_norm,
                 b_norm=b_norm, b_q_norm=b_q_norm, b_sinks=b_sinks, a_norm=a_norm)
    rep_m = dict(a_w_s=m_a_w_s, a_b_s=m_a_b_s, f_norm=m_f_norm, f_conv_b=m_f_conv_b, kv_norm=m_kv_norm,
                 k_norm=m_k_norm, b_norm=m_b_norm, b_q_norm=m_b_q_norm, b_sinks=m_b_sinks, a_norm=m_a_norm)
    rep_v = dict(a_w_s=v_a_w_s, a_b_s=v_a_b_s, f_norm=v_f_norm, f_conv_b=v_f_conv_b, kv_norm=v_kv_norm,
                 k_norm=v_k_norm, b_norm=v_b_norm, b_q_norm=v_b_q_norm, b_sinks=v_b_sinks, a_norm=v_a_norm)
    keys = rep + ["a_norm"]
    loss = _sum_devices(ex_rep.results[-1], "loss_sum")[0, 0]
    parts = ex_rep.results[:-1] + [a_norm_parts]
    as2d = lambda a, p: a.reshape(p.shape[1:])
    rep_outs = _adamw_summed(parts, [as2d(rep_w[k], p) for k, p in zip(keys, parts)],
                             [as2d(rep_m[k], p) for k, p in zip(keys, parts)],
                             [as2d(rep_v[k], p) for k, p in zip(keys, parts)], "adamw_replicated")
    for j, key in enumerate(keys):
        res[key] = [o_.reshape(rep_w[key].shape) for o_ in rep_outs[j]]

    order = ["a_norm", "a_w_in", "a_v_norm", "a_w_s", "a_b_s", "a_w_out", "f_norm", "f_w_in", "f_conv_w", "f_conv_b",
             "f_w_out", "kv_norm", "w_kv", "k_norm", "b_norm", "b_w_q", "b_q_norm", "b_sinks", "b_w_o"]
    outs = [loss, grad_x[None]]
    for j in range(4):
        outs += [res[k][j] for k in order]
    return tuple(outs)
```

```python
import jax
import jax.numpy as jnp
from jax import lax
from jax.experimental import pallas as pl
from jax.experimental.pallas import tpu as pltpu

F32 = jnp.float32
BF16 = jnp.bfloat16
EPS = 1e-6
D_MODEL = 1024
CHUNK = 128
N_GROUPS = 8
N_SHARDS = 8
HEAD_DIM = 64
N_Q_HEADS = 16
N_KV_HEADS = 4
D_FF = 2816
FF_SHARD = 2 * D_FF // N_SHARDS
LANES = 128
NEG_BIG = -1e30
ADAM_LR = 0.001
ADAM_B1 = 0.9
ADAM_B2 = 0.999
ADAM_EPS = 1e-08
ADAM_WD = 0.01
ADAM_STEP = 10
VMEM_LIMIT_BYTES = 56 * 1024 * 1024
MESH = pl.DeviceIdType.MESH

NN = (((1,), (0,)), ((), ()))
NT = (((1,), (1,)), ((), ()))
TN = (((0,), (0,)), ((), ()))
SLOPES = tuple(2.0 ** (-8.0 * (h + 1) / N_Q_HEADS) for h in range(N_Q_HEADS))


def _params(sem=None):
    return pltpu.CompilerParams(dimension_semantics=sem, vmem_limit_bytes=VMEM_LIMIT_BYTES)


def _dot(a, b, dims=NN):
    return lax.dot_general(a, b, dims, preferred_element_type=F32)


def _sigmoid(x):
    return 1.0 / (1.0 + jnp.exp(-x))


def _gelu_parts(z):
    cdf = 0.5 * (1.0 + lax.erf(z * (2.0 ** -0.5)))
    pdf = jnp.exp(-0.5 * z * z) * 0.3989422804014327
    return cdf, pdf


def _coords():
    return lax.axis_index("x"), lax.axis_index("y"), lax.axis_index("c")


class _Gather:
    def __init__(self, srcs, relay=True, early=False):
        self.srcs = list(srcs)
        self.early = early
        n = len(self.srcs)
        self.relayed = [relay and s.shape[0] % 32 == 0 for s in self.srcs]
        self.out_shapes = [jax.ShapeDtypeStruct((N_SHARDS,) + s.shape, s.dtype) for s in self.srcs]
        self.sems = [pltpu.SemaphoreType.DMA((n, 9)), pltpu.SemaphoreType.DMA((n, 9)), pltpu.SemaphoreType.DMA((n,))]

    def _plan(self, src, dst, sems):
        send_sems, recv_sems, local_sems = sems
        x, y, c = _coords()
        n = len(src)

        def rows(e, dev, half=None):
            block = dst[e].at[4 * dev[0] + 2 * dev[1] + dev[2]]
            if half is None:
                return block
            nr = self.srcs[e].shape[0] // 2
            return block.at[pl.ds(half * nr, nr)]

        def copy(e, slot, block, to, half=None, from_own=False):
            return pltpu.make_async_remote_copy(
                src_ref=src[e] if from_own else rows(e, block, half), dst_ref=rows(e, block, half),
                send_sem=send_sems.at[e, slot], recv_sem=recv_sems.at[e, slot], device_id=to, device_id_type=MESH)

        return n, x, y, c, rows, copy, local_sems

    def start(self, src, dst, sems):
        n, x, y, c, rows, copy, local_sems = self._plan(src, dst, sems)
        me = (x, y, c)
        for e in range(n):
            pltpu.make_async_copy(src[e], rows(e, me), local_sems.at[e]).start()
            copy(e, 0, me, (x, y, 1 - c), from_own=True).start()
            copy(e, 1, me, (1 - x, y, c), from_own=True).start()
            copy(e, 2, me, (x, 1 - y, c), from_own=True).start()
            if not self.relayed[e]:
                copy(e, 3, me, (1 - x, 1 - y, c), from_own=True).start()

    def pass_on(self, src, dst, sems, wait=True):
        n, x, y, c, rows, copy, local_sems = self._plan(src, dst, sems)
        me, sibling = (x, y, c), (x, y, 1 - c)
        over_x, over_y, diagonal = (1 - x, y, c), (x, 1 - y, c), (1 - x, 1 - y, c)
        sent = []

        def arrived(cp):
            if wait:
                cp.wait_recv()

        def send(cp):
            if wait:
                cp.start()
            sent.append(cp)

        for slot, owner, onward, half in ((1, over_x, over_y, 0), (2, over_y, over_x, 1)):
            for e in range(n):
                arrived(copy(e, slot, owner, me))
                if self.relayed[e]:
                    send(copy(e, 3 + half, owner, onward, half=half))
                send(copy(e, 4 + slot, owner, sibling))
        for e in range(n):
            if self.relayed[e]:
                for half in (0, 1):
                    arrived(copy(e, 3 + half, diagonal, me, half=half))
                    send(copy(e, 7 + half, diagonal, sibling, half=half))
            else:
                arrived(copy(e, 3, diagonal, me))
                send(copy(e, 7, diagonal, sibling))
        return sent

    def finish(self, src, dst, sems, passed_on=False):
        n, x, y, c, rows, copy, local_sems = self._plan(src, dst, sems)
        me, sibling = (x, y, c), (x, y, 1 - c)
        over_x, over_y, diagonal = (1 - x, y, c), (x, 1 - y, c), (1 - x, 1 - y, c)
        sent = self.pass_on(src, dst, sems, wait=not passed_on)
        for e in range(n):
            copy(e, 0, sibling, me).wait_recv()
            copy(e, 5, (1 - x, y, 1 - c), me).wait_recv()
            copy(e, 6, (x, 1 - y, 1 - c), me).wait_recv()
            if self.relayed[e]:
                for half in (0, 1):
                    copy(e, 7 + half, (1 - x, 1 - y, 1 - c), me, half=half).wait_recv()
            else:
                copy(e, 7, (1 - x, 1 - y, 1 - c), me).wait_recv()
        for e in range(n):
            copy(e, 0, me, sibling, from_own=True).wait_send()
            copy(e, 1, me, over_x, from_own=True).wait_send()
            copy(e, 2, me, over_y, from_own=True).wait_send()
            if not self.relayed[e]:
                copy(e, 3, me, diagonal, from_own=True).wait_send()
            pltpu.make_async_copy(src[e], rows(e, me), local_sems.at[e]).wait()
        for cp in sent:
            cp.wait_send()


class _ToSibling:
    def __init__(self, grads):
        self.srcs = list(grads)
        n = len(self.srcs)
        self.out_shapes = [jax.ShapeDtypeStruct((4,) + g.shape[1:], g.dtype) for g in self.srcs]
        self.sems = [pltpu.SemaphoreType.DMA((n, 4)), pltpu.SemaphoreType.DMA((n, 4))]

    def _copies(self, src, dst, sems):
        send_sems, recv_sems = sems
        x, y, c = _coords()
        return [
            pltpu.make_async_remote_copy(
                src_ref=src[i].at[2 * q + (1 - c)], dst_ref=dst[i].at[q], send_sem=send_sems.at[i, q],
                recv_sem=recv_sems.at[i, q], device_id=(x, y, 1 - c), device_id_type=MESH)
            for i in range(len(src)) for q in range(4)]

    def start(self, src, dst, sems):
        for cp in self._copies(src, dst, sems):
            cp.start()

    def finish(self, src, dst, sems):
        for cp in self._copies(src, dst, sems):
            cp.wait()


class _ToChips:
    def __init__(self, psums, rows=None):
        self.srcs = list(psums)
        n = len(self.srcs)
        self.rows = rows
        self.out_shapes = [
            jax.ShapeDtypeStruct((3, p.shape[1] if rows is None else rows[1]) + p.shape[2:], p.dtype)
            for p in self.srcs]
        self.sems = [pltpu.SemaphoreType.DMA((n, 3)), pltpu.SemaphoreType.DMA((n, 3))]

    def _copies(self, src, dst, sems):
        send_sems, recv_sems = sems
        x, y, c = _coords()
        peers = [(x, 1 - y), (1 - x, y), (1 - x, 1 - y)]

        def part(i, q):
            if self.rows is None:
                return src[i].at[q]
            return src[i].at[q, pl.ds(self.rows[0], self.rows[1])]

        return [
            pltpu.make_async_remote_copy(
                src_ref=part(i, 2 * px + py), dst_ref=dst[i].at[r], send_sem=send_sems.at[i, r],
                recv_sem=recv_sems.at[i, r], device_id=(px, py, c), device_id_type=MESH)
            for i in range(len(src)) for r, (px, py) in enumerate(peers)]

    def start(self, src, dst, sems):
        for cp in self._copies(src, dst, sems):
            cp.start()

    def finish(self, src, dst, sems):
        for cp in self._copies(src, dst, sems):
            cp.wait()


class _ToOwners:
    def __init__(self, grads):
        self.srcs = list(grads)
        n = len(self.srcs)
        self.out_shapes = [jax.ShapeDtypeStruct(g.shape, g.dtype) for g in self.srcs]
        self.sems = [pltpu.SemaphoreType.DMA((n, 7)), pltpu.SemaphoreType.DMA((n, 7)), pltpu.SemaphoreType.DMA((n,))]

    def _copies(self, src, dst, sems):
        send_sems, recv_sems, local_sems = sems
        x, y, c = _coords()
        me = 4 * x + 2 * y + c
        copies = [pltpu.make_async_copy(src[i].at[me], dst[i].at[me], local_sems.at[i]) for i in range(len(src))]
        for i in range(len(src)):
            for rel in range(1, N_SHARDS):
                px = x ^ (rel >> 2) if rel >> 2 else x
                py = y ^ ((rel >> 1) & 1) if (rel >> 1) & 1 else y
                pc = c ^ (rel & 1) if rel & 1 else c
                copies.append(pltpu.make_async_remote_copy(
                    src_ref=src[i].at[4 * px + 2 * py + pc], dst_ref=dst[i].at[me], send_sem=send_sems.at[i, rel - 1],
                    recv_sem=recv_sems.at[i, rel - 1], device_id=(px, py, pc), device_id_type=MESH))
        return copies

    def start(self, src, dst, sems):
        for cp in self._copies(src, dst, sems):
            cp.start()

    def finish(self, src, dst, sems):
        for cp in self._copies(src, dst, sems):
            cp.wait()


class _Together:
    def __init__(self, parts):
        self.parts = list(parts)
        self.srcs = [s for p in self.parts for s in p.srcs]
        self.out_shapes = [s for p in self.parts for s in p.out_shapes]
        self.sems = [s for p in self.parts for s in p.sems]

    def _split(self, src, dst, sems):
        a = b = c = 0
        for p in self.parts:
            na, nc = len(p.srcs), len(p.sems)
            yield p, src[a:a + na], dst[b:b + na], sems[c:c + nc]
            a, b, c = a + na, b + na, c + nc

    def start(self, src, dst, sems):
        for p, s, d, m in self._split(src, dst, sems):
            p.start(s, d, m)

    def finish(self, src, dst, sems):
        for p, s, d, m in self._split(src, dst, sems):
            p.finish(s, d, m)

    def spread(self):
        b = 0
        for p in self.parts:
            p.results = self.results[b:b + len(p.srcs)]
            b += len(p.srcs)


def _call(body, args, *, grid, in_specs, out_specs, out_shape, name, scratch=(), sem=None, carry=None):
    out_shape, out_specs = list(out_shape), list(out_specs)
    if carry is None:
        return pl.pallas_call(
            body, grid=grid, in_specs=list(in_specs), out_specs=out_specs, out_shape=out_shape,
            scratch_shapes=list(scratch), name=name, compiler_params=_params(sem))(*args)
    n_in, n_out, n_scr, n_c = len(args), len(out_shape), len(scratch), len(carry.srcs)
    steps = tuple(grid)
    total = 1
    for n_ax in steps:
        total *= n_ax
    early = getattr(carry, "early", False) and total >= 8
    early_step = total - max(2, total // 8)

    def carried(*refs):
        ins, rest = refs[:n_in], refs[n_in:]
        c_src, rest = rest[:n_c], rest[n_c:]
        outs, rest = rest[:n_out], rest[n_out:]
        c_dst, rest = rest[:n_c], rest[n_c:]
        scr, sems = rest[:n_scr], rest[n_scr:]
        step = pl.program_id(0)
        for ax in range(1, len(steps)):
            step = step * steps[ax] + pl.program_id(ax)

        @pl.when(step == 0)
        def _():
            carry.start(c_src, c_dst, sems)

        body(*ins, *outs, *scr)

        if early:
            @pl.when(step == early_step)
            def _():
                carry.pass_on(c_src, c_dst, sems)

        @pl.when(step == total - 1)
        def _():
            if early:
                carry.finish(c_src, c_dst, sems, passed_on=True)
            else:
                carry.finish(c_src, c_dst, sems)

    hbm = pl.BlockSpec(memory_space=pl.ANY)
    res = pl.pallas_call(
        carried, grid=grid, in_specs=list(in_specs) + [hbm] * n_c, out_specs=out_specs + [hbm] * n_c,
        out_shape=out_shape + carry.out_shapes, scratch_shapes=list(scratch) + carry.sems, name=name,
        compiler_params=_params(("arbitrary",) * len(steps)))(*args, *carry.srcs)
    carry.results = list(res[n_out:])
    return list(res[:n_out])


def _exchange_alone(ex, name):
    n = len(ex.srcs)

    def body(*refs):
        src, dst, sems = refs[:n], refs[n:2 * n], refs[2 * n:]
        ex.start(src, dst, sems)
        ex.finish(src, dst, sems)

    hbm = pl.BlockSpec(memory_space=pl.ANY)
    res = pl.pallas_call(body, in_specs=[hbm] * n, out_specs=[hbm] * n, out_shape=ex.out_shapes,
                         scratch_shapes=ex.sems, name=name)(*ex.srcs)
    ex.results = list(res)
    return ex.results


def _rms_bwd(x, gains, dhs, dres, name, tm=512, carry=None, through=None):
    t, d = x.shape
    n = len(gains)
    n_w = 0 if through is None else n

    def body(*refs):
        x_ref, dres_ref = refs[0], refs[1]
        g_refs, dh_refs, w_refs = refs[2:2 + n], refs[2 + n:2 + 2 * n], refs[2 + 2 * n:2 + 2 * n + n_w]
        dx_ref, dg_ref = refs[2 + 2 * n + n_w], refs[3 + 2 * n + n_w]
        i = pl.program_id(0)

        @pl.when(i == 0)
        def _():
            dg_ref[...] = jnp.zeros_like(dg_ref)

        xf = x_ref[...]
        r = lax.rsqrt(jnp.mean(xf * xf, axis=-1, keepdims=True) + EPS)
        xhat = xf * r
        dx = dres_ref[...]
        for j in range(n):
            dh = dh_refs[j][...]
            if n_w:
                dh = _dot(dh.astype(BF16), w_refs[j][...], NT)
            dg_ref[j:j + 1, :] += jnp.sum(dh * xhat, axis=0, keepdims=True)
            gy = dh * g_refs[j][...]
            dx = dx + r * (gy - xhat * jnp.mean(gy * xhat, axis=-1, keepdims=True))
        dx_ref[...] = dx

    row = pl.BlockSpec((tm, d), lambda i: (i, 0))
    vec = pl.BlockSpec((1, d), lambda i: (0, 0))
    dh_rows = [pl.BlockSpec((tm, dh.shape[1]), lambda i: (i, 0)) for dh in dhs]
    w_full = [] if through is None else [pl.BlockSpec(w.shape, lambda i: (0, 0)) for w in through]
    return _call(body, [x, dres, *gains, *dhs, *(through or [])], grid=(t // tm,),
                 in_specs=[row, row] + [vec] * n + dh_rows + w_full,
                 out_specs=[row, pl.BlockSpec((8, d), lambda i: (0, 0))],
                 out_shape=[jax.ShapeDtypeStruct((t, d), F32), jax.ShapeDtypeStruct((8, d), F32)],
                 name=name, sem=("arbitrary",), carry=carry)


def _mm(a, b, a_spec, b_spec, o_spec, out_shape, grid, dims, name, res=None, res_spec=None, carry=None):
    nk = grid[2]
    acc_shape = tuple(s for s in o_spec.block_shape if s is not None)

    def body(*refs):
        a_ref, b_ref = refs[0], refs[1]
        r_ref = refs[2] if res is not None else None
        o_ref = refs[3] if res is not None else refs[2]
        p = _dot(a_ref[...].astype(BF16), b_ref[...].astype(BF16), dims)
        if nk == 1:
            if res is not None:
                p = p + r_ref[...]
            o_ref[...] = p.astype(o_ref.dtype)
            return
        acc_ref = refs[-1]
        k = pl.program_id(2)

        @pl.when(k == 0)
        def _():
            acc_ref[...] = p

        @pl.when(k > 0)
        def _():
            acc_ref[...] += p

        @pl.when(k == nk - 1)
        def _():
            out = acc_ref[...]
            if res is not None:
                out = out + r_ref[...]
            o_ref[...] = out.astype(o_ref.dtype)

    ins = [a, b] + ([res] if res is not None else [])
    specs = [a_spec, b_spec] + ([res_spec] if res is not None else [])
    return _call(body, ins, grid=grid, in_specs=specs, out_specs=[o_spec], out_shape=[out_shape],
                 scratch=[pltpu.VMEM(acc_shape, F32)] if nk > 1 else [], name=name,
                 sem=("parallel", "parallel", "arbitrary"), carry=carry)[0]


def _mm_rows(a, w, out_dtype, name, trans_w=False, res=None, tm=1024, carry=None):
    t, k = a.shape
    tm = min(tm, t)
    n = w.shape[0] if trans_w else w.shape[1]
    return _mm(
        a, w, pl.BlockSpec((tm, k), lambda i, j, kk: (i, 0)), pl.BlockSpec(w.shape, lambda i, j, kk: (0, 0)),
        pl.BlockSpec((tm, n), lambda i, j, kk: (i, 0)), jax.ShapeDtypeStruct((t, n), out_dtype), (t // tm, 1, 1),
        NT if trans_w else NN, name, res=res,
        res_spec=None if res is None else pl.BlockSpec((tm, n), lambda i, j, kk: (i, 0)), carry=carry)


def _mm_wgrad(a, b, name, carry=None):
    t, m = a.shape
    n = b.shape[1]
    tn = n // (4 if b.dtype == F32 else 2)
    return _mm(
        a, b, pl.BlockSpec((t, m), lambda i, j, kk: (0, 0)), pl.BlockSpec((t, tn), lambda i, j, kk: (0, j)),
        pl.BlockSpec((m, tn), lambda i, j, kk: (0, j)), jax.ShapeDtypeStruct((m, n), F32), (1, n // tn, 1), TN, name,
        carry=carry)


def _sgu_fwd(x0, g, w_in, g_v, w_c, b_sb, w_out, tm=256, carry=None):
    t, d = x0.shape
    nsub = w_in.shape[2]

    def body(x_ref, g_ref, win_ref, gv_ref, wc_ref, bsb_ref, wout_ref, zpre_ref, x1_ref, h_ref, u_s, v_s, vn_s, y_s):
        xf = x_ref[...]
        h = (xf * lax.rsqrt(jnp.mean(xf * xf, axis=-1, keepdims=True) + EPS) * g_ref[...]).astype(BF16)
        h_ref[...] = h
        for k in range(N_SHARDS):
            zk = _dot(h, win_ref[k])
            zpre_ref[:, k * nsub:(k + 1) * nsub] = zk
            cdf, _ = _gelu_parts(zk)
            if k < N_SHARDS // 2:
                u_s[:, k * nsub:(k + 1) * nsub] = zk * cdf
            else:
                v_s[:, (k - 4) * nsub:(k - 3) * nsub] = zk * cdf
        v = v_s[...]
        rv = lax.rsqrt(jnp.mean(v * v, axis=-1, keepdims=True) + EPS)
        vn_s[...] = (v * rv * gv_ref[...]).astype(BF16)
        for ci in range(tm // CHUNK):
            rows = slice(ci * CHUNK, (ci + 1) * CHUNK)
            for g in range(N_GROUPS):
                cols = slice(g * LANES, (g + 1) * LANES)
                sv = _dot(wc_ref[g], vn_s[rows, cols]) + bsb_ref[g]
                y_s[rows, cols] = (u_s[rows, cols] * sv).astype(BF16)
        x1_ref[...] = x_ref[...] + _dot(y_s[...], wout_ref[...])

    row = pl.BlockSpec((tm, d), lambda i: (i, 0))
    full = lambda a: pl.BlockSpec(a.shape, lambda i: (0,) * a.ndim)
    return _call(
        body, [x0, g, w_in, g_v, w_c, b_sb, w_out], grid=(t // tm,),
        in_specs=[row, full(g), full(w_in), full(g_v), full(w_c), full(b_sb), full(w_out)],
        out_specs=[pl.BlockSpec((tm, 2 * d), lambda i: (i, 0)), row, row],
        out_shape=[jax.ShapeDtypeStruct((t, 2 * d), F32), jax.ShapeDtypeStruct((t, d), F32),
                   jax.ShapeDtypeStruct((t, d), BF16)],
        scratch=[pltpu.VMEM((tm, d), F32), pltpu.VMEM((tm, d), F32), pltpu.VMEM((tm, d), BF16),
                 pltpu.VMEM((tm, d), BF16)],
        name="sgu_fwd", carry=carry)


def _sgu_bwd(dx1, zpre, w_out, g_v, w_c, w_ct, b_sb, tm=512, carry=None):
    t, d = dx1.shape

    def body(dx_ref, zpre_ref, wout_ref, gv_ref, wc_ref, wct_ref, bsb_ref,
             dz_ref, y_ref, dwc_ref, dbs_ref, dgv_ref, u_s, vn_s, dy_s, du_s, dvn_s):
        i = pl.program_id(0)

        @pl.when(i == 0)
        def _():
            dwc_ref[...] = jnp.zeros_like(dwc_ref)
            dbs_ref[...] = jnp.zeros_like(dbs_ref)
            dgv_ref[...] = jnp.zeros_like(dgv_ref)

        dy_s[...] = _dot(dx_ref[...].astype(BF16), wout_ref[...], NT)
        zu = zpre_ref[:, :d]
        zv = zpre_ref[:, d:]
        cdf_u, pdf_u = _gelu_parts(zu)
        cdf_v, pdf_v = _gelu_parts(zv)
        u_s[...] = zu * cdf_u
        v = zv * cdf_v
        rv = lax.rsqrt(jnp.mean(v * v, axis=-1, keepdims=True) + EPS)
        vhat = v * rv
        gv = gv_ref[...]
        vn_s[...] = (vhat * gv).astype(BF16)
        for ci in range(tm // CHUNK):
            rows = slice(ci * CHUNK, (ci + 1) * CHUNK)
            for g in range(N_GROUPS):
                cols = slice(g * LANES, (g + 1) * LANES)
                vnb = vn_s[rows, cols]
                sv = _dot(wc_ref[g], vnb) + bsb_ref[g]
                dyb = dy_s[rows, cols]
                ub = u_s[rows, cols]
                dsv = dyb * ub
                du_s[rows, cols] = dyb * sv
                y_ref[rows, cols] = (ub * sv).astype(BF16)
                dsvb = dsv.astype(BF16)
                dbs_ref[g] += dsv
                dwc_ref[g] += _dot(dsvb, vnb, NT)
                dvn_s[rows, cols] = _dot(wct_ref[g], dsvb)
        dvn = dvn_s[...]
        dgv_ref[0:1, :] += jnp.sum(dvn * vhat, axis=0, keepdims=True)
        gy = dvn * gv
        dv = rv * (gy - vhat * jnp.mean(gy * vhat, axis=-1, keepdims=True))
        dz_ref[:, :d] = (du_s[...] * (cdf_u + zu * pdf_u)).astype(BF16)
        dz_ref[:, d:] = (dv * (cdf_v + zv * pdf_v)).astype(BF16)

        @pl.when(i == t // tm - 1)
        def _():
            tri = (lax.broadcasted_iota(jnp.int32, (CHUNK, CHUNK), 0)
                   >= lax.broadcasted_iota(jnp.int32, (CHUNK, CHUNK), 1))
            for g in range(N_GROUPS):
                dwc_ref[g] = jnp.where(tri, dwc_ref[g], 0.0)
                dbs_ref[g] = jnp.broadcast_to(jnp.sum(dbs_ref[g], axis=1, keepdims=True), (CHUNK, CHUNK))

    row = pl.BlockSpec((tm, d), lambda i: (i, 0))
    row2 = pl.BlockSpec((tm, 2 * d), lambda i: (i, 0))
    full = lambda a: pl.BlockSpec(a.shape, lambda i: (0,) * a.ndim)
    grp = pl.BlockSpec((N_GROUPS, CHUNK, CHUNK), lambda i: (0, 0, 0))
    return _call(
        body, [dx1, zpre, w_out, g_v, w_c, w_ct, b_sb], grid=(t // tm,),
        in_specs=[row, row2, full(w_out), full(g_v), full(w_c), full(w_ct), full(b_sb)],
        out_specs=[row2, row, grp, grp, pl.BlockSpec((8, d), lambda i: (0, 0))],
        out_shape=[jax.ShapeDtypeStruct((t, 2 * d), BF16), jax.ShapeDtypeStruct((t, d), BF16),
                   jax.ShapeDtypeStruct((N_GROUPS, CHUNK, CHUNK), F32),
                   jax.ShapeDtypeStruct((N_GROUPS, CHUNK, CHUNK), F32), jax.ShapeDtypeStruct((8, d), F32)],
        scratch=[pltpu.VMEM((tm, d), F32), pltpu.VMEM((tm, d), BF16), pltpu.VMEM((tm, d), F32),
                 pltpu.VMEM((tm, d), F32), pltpu.VMEM((tm, d), F32)],
        name="sgu_bwd", sem=("arbitrary",), carry=carry)


ROW_CHUNK = 256
HALO = 16


def _ffn_fwd(x, g, w_in, cw, cb, w_out, layer, tm=512, carry=None, next_gains=(), loss_target=None):
    t, d = x.shape
    nc = N_SHARDS // 2
    n_gains = len(next_gains)
    with_loss = loss_target is not None

    def body(x_ref, xp_ref, g_ref, wg_ref, wu_ref, cwg_ref, cbg_ref, cwu_ref, cbu_ref, wout_ref, *rest):
        extra_in, rest = rest[:n_gains + with_loss], rest[n_gains + with_loss:]
        o_ref, hf_ref, a_ref, pre_ref = rest[:4]
        extra_out, hw_s = rest[4:-1], rest[-1]
        i, c = pl.program_id(0), pl.program_id(1)

        @pl.when(c == 0)
        def _():
            keep = jnp.where(i == 0, 0.0, 1.0)
            xw = jnp.concatenate([xp_ref[...] * keep, x_ref[...]], axis=0)
            xhat = xw * lax.rsqrt(jnp.mean(xw * xw, axis=-1, keepdims=True) + EPS)
            hw_s[...] = (xhat * g_ref[...]).astype(BF16)
            hf_ref[...] = hw_s[HALO:, :]
            o_ref[...] = x_ref[...]

        hw = hw_s[...]
        pre = []
        for j, (w_ref, cw_ref, cb_ref) in enumerate(((wg_ref, cwg_ref, cbg_ref), (wu_ref, cwu_ref, cbu_ref))):
            ab = _dot(hw, w_ref[...]).astype(BF16)
            a_ref[j] = ab[HALO:]
            win = ab.astype(F32)
            cw_v = cw_ref[...]
            pre.append(cw_v[2:3, :] * win[HALO:] + cw_v[1:2, :] * pltpu.roll(win, 1, 0)[HALO:]
                       + cw_v[0:1, :] * pltpu.roll(win, 2, 0)[HALO:] + cb_ref[...])
            pre_ref[j] = pre[j]
        act = (pre[0] * _sigmoid(pre[0]) * pre[1]).astype(BF16)
        o_ref[...] += _dot(act, wout_ref[...])

        if with_loss:
            @pl.when((i == 0) & (c == 0))
            def _():
                extra_out[-1][...] = jnp.zeros_like(extra_out[-1])

        @pl.when(c == nc - 1)
        def _():
            xn = o_ref[...]
            if n_gains:
                xhat = xn * lax.rsqrt(jnp.mean(xn * xn, axis=-1, keepdims=True) + EPS)
                for k in range(n_gains):
                    extra_out[k][...] = (xhat * extra_in[k][...]).astype(BF16)
            if with_loss:
                err = xn - extra_in[-1][...]
                extra_out[-2][...] = err * (1.0 / d)
                part = jnp.sum(jnp.sum(err * err, axis=0, keepdims=True), axis=1, keepdims=True)
                extra_out[-1][...] += jnp.broadcast_to(0.5 / d * part, extra_out[-1].shape)

    row = pl.BlockSpec((tm, d), lambda i, c: (i, 0))
    vec = pl.BlockSpec((1, d), lambda i, c: (0, 0))
    shard = lambda rows, up: pl.BlockSpec((None, rows, FF_SHARD), lambda i, c: (c + up * nc, 0, 0))
    pair = pl.BlockSpec((2, None, tm, FF_SHARD), lambda i, c: (0, c, i, 0))
    lanes = pl.BlockSpec((8, LANES), lambda i, c: (0, 0))
    outs = _call(
        body, [x, x, g, w_in, w_in, cw, cb, cw, cb, w_out, *next_gains] + ([loss_target] if with_loss else []),
        grid=(t // tm, nc),
        in_specs=[row, pl.BlockSpec((HALO, d), lambda i, c: (jnp.maximum(i * (tm // HALO) - 1, 0), 0)),
                  vec, shard(d, 0), shard(d, 1), shard(8, 0), shard(1, 0), shard(8, 1), shard(1, 1),
                  pl.BlockSpec((FF_SHARD, d), lambda i, c: (c, 0))] + [vec] * n_gains + [row] * with_loss,
        out_specs=[row, row, pair, pair] + [row] * n_gains + [row, lanes] * with_loss,
        out_shape=[jax.ShapeDtypeStruct((t, d), F32), jax.ShapeDtypeStruct((t, d), BF16),
                   jax.ShapeDtypeStruct((2, nc, t, FF_SHARD), BF16), jax.ShapeDtypeStruct((2, nc, t, FF_SHARD), F32)]
        + [jax.ShapeDtypeStruct((t, d), BF16)] * n_gains
        + [jax.ShapeDtypeStruct((t, d), F32), jax.ShapeDtypeStruct((8, LANES), F32)] * with_loss,
        scratch=[pltpu.VMEM((tm + HALO, d), BF16)], name=f"ffn{layer}_fwd", sem=("arbitrary", "arbitrary"), carry=carry)
    return (outs[0], outs[1], outs[2].reshape(N_SHARDS, t, FF_SHARD), outs[3]) + tuple(outs[4:])


def _ffn_bwd_act(pre, w_out, dxn, layer, tm=1024, carry=None):
    t, d = dxn.shape
    tm = min(tm, t)
    nc = N_SHARDS // 2

    def body(pre_ref, wout_ref, dx_ref, dhu_ref, dw_ref, dcb_ref):
        i = pl.program_id(1)

        @pl.when(i == 0)
        def _():
            dw_ref[...] = jnp.zeros_like(dw_ref)
            dcb_ref[...] = jnp.zeros_like(dcb_ref)

        hg, hu = pre_ref[0], pre_ref[1]
        sg = _sigmoid(hg)
        sl = hg * sg
        dxb = dx_ref[...].astype(BF16)
        dact = _dot(dxb, wout_ref[...], NT)
        dw_ref[...] += _dot((sl * hu).astype(BF16), dxb, TN)
        d_up = dact * sl
        d_gate = dact * hu * (sg * (1.0 + hg * (1.0 - sg)))
        for j, dv in enumerate((d_gate, d_up)):
            dhu_ref[j] = dv.astype(BF16)
            dcb_ref[j, 0:1, :] += jnp.sum(dv, axis=0, keepdims=True)

    return _call(
        body, [pre, w_out, dxn], grid=(nc, t // tm),
        in_specs=[pl.BlockSpec((2, None, tm, FF_SHARD), lambda c, i: (0, c, i, 0)),
                  pl.BlockSpec((FF_SHARD, d), lambda c, i: (c, 0)), pl.BlockSpec((tm, d), lambda c, i: (i, 0))],
        out_specs=[pl.BlockSpec((None, 2, tm, FF_SHARD), lambda c, i: (c, 0, i, 0)),
                   pl.BlockSpec((FF_SHARD, d), lambda c, i: (c, 0)),
                   pl.BlockSpec((None, 2, 8, FF_SHARD), lambda c, i: (c, 0, 0, 0))],
        out_shape=[jax.ShapeDtypeStruct((nc, 2, t, FF_SHARD), BF16), jax.ShapeDtypeStruct((D_FF, d), F32),
                   jax.ShapeDtypeStruct((nc, 2, 8, FF_SHARD), F32)],
        name=f"ffn{layer}_bwd_act", sem=("parallel", "arbitrary"), carry=carry)


def _ffn_bwd_in(dhu, a, cw, w_in, layer, tm=1024, carry=None, norm=None):
    nc, _, t, _ = dhu.shape
    d = D_MODEL
    tm = min(tm, t)
    last_blk = t // 16 - 1
    n_norm = 0 if norm is None else 3

    n_steps = (t // tm) * N_SHARDS

    def body(dhu_hbm, nx_ref, a_hbm, cw_ref, win_hbm, *rest):
        norm_refs, (da_ref, o_ref, dcw_ref) = rest[:n_norm], rest[n_norm:n_norm + 3]
        dg_refs, (dh_ring, a_ring, w_ring, ring_sems) = rest[n_norm + 3:-4], rest[-4:]
        i, s = pl.program_id(0), pl.program_id(1)
        step = i * N_SHARDS + s

        def fetches(g):
            gi, gs, slot = g // N_SHARDS, g % N_SHARDS, g % 3
            rows = pl.ds(pl.multiple_of(gi * tm, tm), tm)
            return (pltpu.make_async_copy(dhu_hbm.at[gs % nc, gs // nc, rows], dh_ring.at[slot], ring_sems.at[0, slot]),
                    pltpu.make_async_copy(a_hbm.at[gs, rows], a_ring.at[slot], ring_sems.at[1, slot]),
                    pltpu.make_async_copy(win_hbm.at[gs], w_ring.at[slot], ring_sems.at[2, slot]))

        @pl.when(step == 0)
        def _():
            for g in range(min(2, n_steps)):
                for cp in fetches(g):
                    cp.start()

        @pl.when(step + 2 < n_steps)
        def _():
            for cp in fetches(step + 2):
                cp.start()

        for cp in fetches(step):
            cp.wait()
        dh_ref, a_ref, win_ref = dh_ring.at[step % 3], a_ring.at[step % 3], w_ring.at[step % 3]

        @pl.when(s == 0)
        def _():
            o_ref[...] = jnp.zeros_like(o_ref)

        @pl.when((s == 0) & (i == 0))
        def _():
            dcw_ref[...] = jnp.zeros_like(dcw_ref)

        keep = jnp.where(i == t // tm - 1, 0.0, 1.0)
        cw = cw_ref[...]
        sums = [None] * 3
        for r0 in range(0, tm, ROW_CHUNK):
            rows = slice(r0, r0 + ROW_CHUNK)
            if r0 + ROW_CHUNK == tm:
                win = jnp.concatenate([dh_ref[rows, :].astype(F32), nx_ref[...].astype(F32) * keep], axis=0)
            else:
                win = dh_ref[r0:r0 + ROW_CHUNK + HALO, :].astype(F32)
            n = ROW_CHUNK + HALO
            taps = (pltpu.roll(win, n - 2, 0)[:ROW_CHUNK],
                    pltpu.roll(win, n - 1, 0)[:ROW_CHUNK],
                    win[:ROW_CHUNK])
            da = (cw[0:1, :] * taps[0] + cw[1:2, :] * taps[1] + cw[2:3, :] * taps[2]).astype(BF16)
            da_ref[rows, :] = da
            o_ref[rows, :] += _dot(da, win_ref[...], NT)
            af = a_ref[rows, :].astype(F32)
            parts = [jnp.sum(taps[k] * af, axis=0, keepdims=True) for k in range(3)]
            sums = [p if q is None else q + p for q, p in zip(sums, parts)]
        for k in range(3):
            dcw_ref[pl.ds(s, 1), k:k + 1, :] += sums[k][None]

        if norm is not None:
            x_ref, g_ref, dres_ref = norm_refs
            dg_ref = dg_refs[0]

            @pl.when((s == 0) & (i == 0))
            def _():
                dg_ref[...] = jnp.zeros_like(dg_ref)

            @pl.when(s == N_SHARDS - 1)
            def _():
                xf = x_ref[...]
                r = lax.rsqrt(jnp.mean(xf * xf, axis=-1, keepdims=True) + EPS)
                xhat = xf * r
                dh = o_ref[...]
                dg_ref[0:1, :] += jnp.sum(dh * xhat, axis=0, keepdims=True)
                gy = dh * g_ref[...]
                o_ref[...] = dres_ref[...] + r * (gy - xhat * jnp.mean(gy * xhat, axis=-1, keepdims=True))

    row = pl.BlockSpec((tm, d), lambda i, s: (i, 0))
    norm_args = [] if norm is None else list(norm)
    norm_specs = [] if norm is None else [row, pl.BlockSpec((1, d), lambda i, s: (0, 0)), row]
    return _call(
        body, [dhu, dhu, a, cw, w_in] + norm_args, grid=(t // tm, N_SHARDS),
        in_specs=[pl.BlockSpec(memory_space=pl.ANY),
                  pl.BlockSpec((None, None, 16, FF_SHARD),
                               lambda i, s: (s % nc, s // nc, jnp.minimum((i + 1) * (tm // 16), last_blk), 0)),
                  pl.BlockSpec(memory_space=pl.ANY),
                  pl.BlockSpec((None, 8, FF_SHARD), lambda i, s: (s, 0, 0)),
                  pl.BlockSpec(memory_space=pl.ANY)] + norm_specs,
        out_specs=[pl.BlockSpec((None, tm, FF_SHARD), lambda i, s: (s, i, 0)), row,
                   pl.BlockSpec((N_SHARDS, 8, FF_SHARD), lambda i, s: (0, 0, 0))]
        + ([] if norm is None else [pl.BlockSpec((8, d), lambda i, s: (0, 0))]),
        out_shape=[jax.ShapeDtypeStruct((N_SHARDS, t, FF_SHARD), BF16), jax.ShapeDtypeStruct((t, d), F32),
                   jax.ShapeDtypeStruct((N_SHARDS, 8, FF_SHARD), F32)]
        + ([] if norm is None else [jax.ShapeDtypeStruct((8, d), F32)]),
        scratch=[pltpu.VMEM((3, tm, FF_SHARD), BF16), pltpu.VMEM((3, tm, FF_SHARD), BF16),
                 pltpu.VMEM((3, d, FF_SHARD), BF16), pltpu.SemaphoreType.DMA((3, 3))],
        name=f"ffn{layer}_bwd_in", sem=("arbitrary", "arbitrary"), carry=carry)


def _ffn_wgrad_in(hf, da, layer, carry=None):
    t, d = hf.shape
    return _mm(
        da, hf, pl.BlockSpec((None, t, FF_SHARD), lambda s, j, kk: (s, 0, 0)),
        pl.BlockSpec((t, d), lambda s, j, kk: (0, 0)),
        pl.BlockSpec((None, FF_SHARD, d), lambda s, j, kk: (s, 0, 0)),
        jax.ShapeDtypeStruct((N_SHARDS, FF_SHARD, d), F32), (N_SHARDS, 1, 1), TN, f"ffn{layer}_wgrad_in",
        carry=carry)


Q_PER_KV = N_Q_HEADS // N_KV_HEADS
GROUP_ROWS = Q_PER_KV * CHUNK


def _lane_half():
    return lax.broadcasted_iota(jnp.int32, (CHUNK, LANES), 1) < HEAD_DIM


def _fill_attn_bias(bias_s):
    tq = lax.broadcasted_iota(jnp.int32, (GROUP_ROWS, 2 * CHUNK), 0) & (CHUNK - 1)
    jk = lax.broadcasted_iota(jnp.int32, (GROUP_ROWS, 2 * CHUNK), 1)
    dist = tq + CHUNK - jk
    window = (dist >= 0) & (dist < CHUNK)
    distf = dist.astype(F32)
    for kvh in range(N_KV_HEADS):
        alibi = _per_head_column([-SLOPES[h] for h in range(Q_PER_KV * kvh, Q_PER_KV * (kvh + 1))]) * distf
        bias_s[0, kvh] = jnp.where(window & (jk >= CHUNK), alibi, NEG_BIG)
        bias_s[1, kvh] = jnp.where(window, alibi, NEG_BIG)


def _per_head_column(values):
    r = lax.broadcasted_iota(jnp.int32, (GROUP_ROWS, 1), 0)
    col = jnp.full((GROUP_ROWS, 1), values[Q_PER_KV - 1], F32)
    for j in range(Q_PER_KV - 2, -1, -1):
        col = jnp.where(r < (j + 1) * CHUNK, values[j], col)
    return col


def _half_sum(x, lo):
    s_lo = jnp.sum(jnp.where(lo, x, 0.0), axis=-1, keepdims=True)
    s_hi = jnp.sum(jnp.where(lo, 0.0, x), axis=-1, keepdims=True)
    return jnp.where(lo, s_lo, s_hi)


def _stack_heads(pairs, lo):
    zero = jnp.zeros_like(pairs[0])
    return jnp.concatenate([jnp.where(lo, pairs[0], zero), jnp.where(lo, zero, pairs[0]),
                            jnp.where(lo, pairs[1], zero), jnp.where(lo, zero, pairs[1])], axis=0)


def _unstack_heads(stacked, lo):
    return (jnp.where(lo, stacked[0:CHUNK], stacked[CHUNK:2 * CHUNK]),
            jnp.where(lo, stacked[2 * CHUNK:3 * CHUNK], stacked[3 * CHUNK:]))


def _attn_probs(qs, kn, bias, sink_col):
    s = _dot(qs, kn, NT) * (HEAD_DIM ** -0.5) + bias
    m = jnp.maximum(jnp.max(s, axis=-1, keepdims=True), sink_col)
    e = jnp.exp(s - m)
    den = jnp.sum(e, axis=-1, keepdims=True) + jnp.exp(sink_col - m)
    return e * (1.0 / den), m, den


def _attn_fwd(qraw, kvd, gq, gk, sinks, carry=None):
    t, d = qraw.shape
    nb = t // CHUNK

    def body(sink_ref, q_ref, cur_ref, prev_ref, gq_ref, gk_ref, o_ref, bias_s):
        n = pl.program_id(0)

        @pl.when(n == 0)
        def _():
            _fill_attn_bias(bias_s)

        lo = _lane_half()
        which = jnp.where(n == 0, 0, 1)
        gq_v, gk_v = gq_ref[...], gk_ref[...]
        for kvh in range(N_KV_HEADS):
            ks = slice(kvh * LANES, (kvh + 1) * LANES)
            vs = slice(4 * LANES + kvh * LANES, 4 * LANES + (kvh + 1) * LANES)
            kraw = jnp.concatenate([prev_ref[:, ks], cur_ref[:, ks]], axis=0)
            rk = lax.rsqrt(jnp.mean(kraw * kraw, axis=-1, keepdims=True) + EPS)
            kn = (kraw * rk * gk_v).astype(BF16)
            vv = jnp.concatenate([prev_ref[:, vs], cur_ref[:, vs]], axis=0).astype(BF16)
            qn = []
            for p in range(2):
                qp = q_ref[:, (2 * kvh + p) * LANES:(2 * kvh + p + 1) * LANES]
                r = lax.rsqrt(_half_sum(qp * qp, lo) * (1.0 / HEAD_DIM) + EPS)
                qn.append(qp * r * gq_v)
            heads = range(Q_PER_KV * kvh, Q_PER_KV * (kvh + 1))
            pf, _, _ = _attn_probs(_stack_heads(qn, lo).astype(BF16), kn, bias_s[which, kvh],
                                   _per_head_column([sink_ref[h] for h in heads]))
            for p, o_pair in enumerate(_unstack_heads(_dot(pf.astype(BF16), vv), lo)):
                o_ref[:, (2 * kvh + p) * LANES:(2 * kvh + p + 1) * LANES] = o_pair.astype(BF16)

    blk = lambda f: pl.BlockSpec((CHUNK, d), f)
    vec = pl.BlockSpec((1, LANES), lambda n: (0, 0))
    return _call(
        body, [sinks, qraw, kvd, kvd, gq, gk], grid=(nb,),
        in_specs=[pl.BlockSpec(memory_space=pltpu.SMEM), blk(lambda n: (n, 0)), blk(lambda n: (n, 0)),
                  blk(lambda n: (jnp.maximum(n - 1, 0), 0)), vec, vec],
        out_specs=[blk(lambda n: (n, 0))], out_shape=[jax.ShapeDtypeStruct((t, d), BF16)],
        scratch=[pltpu.VMEM((2, N_KV_HEADS, GROUP_ROWS, 2 * CHUNK), F32)], name="attn_fwd", sem=("arbitrary",),
        carry=carry)[0]


def _attn_bwd(qraw, kvd, d_o, gq, gk, sinks, carry=None):
    t, d = qraw.shape
    nb = t // CHUNK

    def body(sink_ref, q_ref, cur_ref, prev_ref, do_ref, gq_ref, gk_ref,
             dq_ref, dkv_ref, dsink_ref, dgq_ref, dgk_ref, carry_s, pp_s, cp_s, bias_s):
        n = pl.program_id(0)

        @pl.when(n == 0)
        def _():
            carry_s[...] = jnp.zeros_like(carry_s)
            dsink_ref[...] = jnp.zeros_like(dsink_ref)
            dgq_ref[...] = jnp.zeros_like(dgq_ref)
            dgk_ref[...] = jnp.zeros_like(dgk_ref)
            _fill_attn_bias(bias_s)

        @pl.when(n < nb)
        def _():
            lo = _lane_half()
            which = jnp.where(n == 0, 0, 1)
            gq_v, gk_v = gq_ref[...], gk_ref[...]
            for kvh in range(N_KV_HEADS):
                ks = slice(kvh * LANES, (kvh + 1) * LANES)
                vs = slice(4 * LANES + kvh * LANES, 4 * LANES + (kvh + 1) * LANES)
                kraw = jnp.concatenate([prev_ref[:, ks], cur_ref[:, ks]], axis=0)
                rk = lax.rsqrt(jnp.mean(kraw * kraw, axis=-1, keepdims=True) + EPS)
                khat = kraw * rk
                kn = (khat * gk_v).astype(BF16)
                vv = jnp.concatenate([prev_ref[:, vs], cur_ref[:, vs]], axis=0).astype(BF16)
                cols = [slice((2 * kvh + p) * LANES, (2 * kvh + p + 1) * LANES) for p in range(2)]
                rq, qhat = [], []
                for p in range(2):
                    qp = q_ref[:, cols[p]]
                    rq.append(lax.rsqrt(_half_sum(qp * qp, lo) * (1.0 / HEAD_DIM) + EPS))
                    qhat.append(qp * rq[p])
                heads = range(Q_PER_KV * kvh, Q_PER_KV * (kvh + 1))
                qs = _stack_heads([qhat[p] * gq_v for p in range(2)], lo).astype(BF16)
                dos = _stack_heads([do_ref[:, cols[p]] for p in range(2)], lo)
                sink_col = _per_head_column([sink_ref[h] for h in heads])
                pf, m, den = _attn_probs(qs, kn, bias_s[which, kvh], sink_col)
                dp = _dot(dos, vv, NT)
                delta = jnp.sum(pf * dp, axis=-1, keepdims=True)
                sink_delta = jnp.exp(sink_col - m) / den * delta
                for j, h in enumerate(heads):
                    dsink_ref[h:h + 1, :] -= jnp.broadcast_to(
                        jnp.sum(sink_delta[j * CHUNK:(j + 1) * CHUNK], axis=0, keepdims=True), (1, LANES))
                ds = (pf * (dp - delta) * (HEAD_DIM ** -0.5)).astype(BF16)
                dkn = _dot(ds, qs, TN)
                dvb = _dot(pf.astype(BF16), dos, TN)
                for p, dqn in enumerate(_unstack_heads(_dot(ds, kn), lo)):
                    dgq_ref[0:1, :] += jnp.sum(dqn * qhat[p], axis=0, keepdims=True)
                    gy = dqn * gq_v
                    mq = _half_sum(gy * qhat[p], lo) * (1.0 / HEAD_DIM)
                    dq_ref[:, cols[p]] = (rq[p] * (gy - qhat[p] * mq)).astype(BF16)
                dgk_ref[0:1, :] += jnp.sum(dkn * khat, axis=0, keepdims=True)
                gyk = dkn * gk_v
                dkraw = rk * (gyk - khat * jnp.mean(gyk * khat, axis=-1, keepdims=True))
                pp_s[:, ks] = dkraw[:CHUNK]
                cp_s[:, ks] = dkraw[CHUNK:]
                pp_s[:, vs] = dvb[:CHUNK]
                cp_s[:, vs] = dvb[CHUNK:]
            dkv_ref[...] = (carry_s[...] + pp_s[...]).astype(BF16)
            carry_s[...] = cp_s[...]

        @pl.when(n == nb)
        def _():
            dkv_ref[...] = carry_s[...].astype(BF16)

    blk = lambda f: pl.BlockSpec((CHUNK, d), f)
    vec = pl.BlockSpec((1, LANES), lambda n: (0, 0))
    cur = lambda n: (jnp.minimum(n, nb - 1), 0)
    prev = lambda n: (jnp.maximum(jnp.minimum(n, nb - 1) - 1, 0), 0)
    small = lambda r: pl.BlockSpec((r, LANES), lambda n: (0, 0))
    return _call(
        body, [sinks, qraw, kvd, kvd, d_o, gq, gk], grid=(nb + 1,),
        in_specs=[pl.BlockSpec(memory_space=pltpu.SMEM), blk(cur), blk(cur), blk(prev), blk(cur), vec, vec],
        out_specs=[blk(cur), blk(lambda n: (jnp.maximum(n - 1, 0), 0)), small(N_Q_HEADS), small(8), small(8)],
        out_shape=[jax.ShapeDtypeStruct((t, d), BF16), jax.ShapeDtypeStruct((t, d), BF16),
                   jax.ShapeDtypeStruct((N_Q_HEADS, LANES), F32), jax.ShapeDtypeStruct((8, LANES), F32),
                   jax.ShapeDtypeStruct((8, LANES), F32)],
        scratch=[pltpu.VMEM((CHUNK, d), F32)] * 3 + [pltpu.VMEM((2, N_KV_HEADS, GROUP_ROWS, 2 * CHUNK), F32)],
        name="attn_bwd", sem=("arbitrary",), carry=carry)


def _adamw_math(g, w, m, v):
    m = ADAM_B1 * m + (1.0 - ADAM_B1) * g
    v = ADAM_B2 * v + (1.0 - ADAM_B2) * (g * g)
    m_hat = m / (1.0 - ADAM_B1 ** ADAM_STEP)
    v_hat = v / (1.0 - ADAM_B2 ** ADAM_STEP)
    delta = -ADAM_LR * (m_hat / (jnp.sqrt(v_hat) + ADAM_EPS) + ADAM_WD * w)
    return delta, m, v


def _row_tile(r, cap=128):
    for tr in range(min(r, cap), 0, -1):
        if r % tr == 0 and (tr % 8 == 0 or tr == r):
            return tr
    return r


def _chip_sum(grad, recv, place, name, wire_dtype):
    _, r, c = grad.shape
    tr = _row_tile(r, 256)

    def body(pl_ref, g_ref, a_ref, p_ref):
        p_ref[...] = (g_ref[...] + a_ref[...]).astype(p_ref.dtype)

    other = lambda rel, pr: pr[0] ^ (rel + 1)
    return pl.pallas_call(
        body,
        grid_spec=pltpu.PrefetchScalarGridSpec(
            num_scalar_prefetch=1, grid=(3, r // tr),
            in_specs=[pl.BlockSpec((None, None, tr, c), lambda rel, i, pr: (other(rel, pr), pr[1], i, 0)),
                      pl.BlockSpec((None, tr, c), lambda rel, i, pr: (other(rel, pr), i, 0))],
            out_specs=pl.BlockSpec((None, tr, c), lambda rel, i, pr: (other(rel, pr), i, 0))),
        out_shape=jax.ShapeDtypeStruct((4, r, c), wire_dtype), name=name, compiler_params=_params(),
    )(place, grad.reshape(4, 2, r, c), recv)


def _adamw_sharded(grad, recv, others, place, w, m, v, name, layer=None, fill=None):
    r, c = w.shape[-2:]
    tr = _row_tile(r)

    def body(pl_ref, g_ref, a_ref, oth_ref, w_ref, m_ref, v_ref, *rest):
        g_out, d_out, nm_out, nv_out = rest[-4:]
        g = g_ref[...] + a_ref[...]
        for k in range(3):
            g = g + oth_ref[k].astype(F32)
        delta, nm, nv = _adamw_math(g, w_ref[...], m_ref[...], v_ref[...])
        g_out[...] = g
        d_out[...] = delta
        nm_out[...] = nm
        nv_out[...] = nv

    if layer is None:
        row = pl.BlockSpec((tr, c), lambda i, pr: (i, 0))
    else:
        row = pl.BlockSpec((None, tr, c), lambda i, pr: (layer, i, 0))
    n_fill = 0 if fill is None else 4
    in_specs = [pl.BlockSpec((None, None, tr, c), lambda i, pr: (pr[0], pr[1], i, 0)),
                pl.BlockSpec((None, tr, c), lambda i, pr: (pr[0], i, 0)),
                pl.BlockSpec((3, tr, c), lambda i, pr: (0, i, 0)), row, row, row]
    in_specs += [pl.BlockSpec(memory_space=pl.ANY)] * n_fill
    return pl.pallas_call(
        body,
        grid_spec=pltpu.PrefetchScalarGridSpec(
            num_scalar_prefetch=1, grid=(r // tr,), in_specs=in_specs, out_specs=[row] * 4),
        out_shape=[jax.ShapeDtypeStruct(w.shape, F32)] * 4, name=name, compiler_params=_params(),
        input_output_aliases={7 + j: j for j in range(n_fill)},
    )(place, grad.reshape(4, 2, r, c), recv, others, w, m, v, *([] if fill is None else fill))


def _sum_devices(parts, name):
    def body(p_ref, o_ref):
        total = p_ref[0]
        for k in range(1, N_SHARDS):
            total = total + p_ref[k]
        o_ref[...] = total

    return pl.pallas_call(body, out_shape=jax.ShapeDtypeStruct(parts.shape[1:], F32), name=name)(parts)


def _adamw_summed(parts, ws, ms, vs, name):
    n = len(parts)

    def body(*refs):
        p_refs, w_refs, m_refs, v_refs = refs[:n], refs[n:2 * n], refs[2 * n:3 * n], refs[3 * n:4 * n]
        o_refs = refs[4 * n:]
        for i in range(n):
            g = p_refs[i][0]
            for k in range(1, N_SHARDS):
                g = g + p_refs[i][k]
            delta, nm, nv = _adamw_math(g, w_refs[i][...], m_refs[i][...], v_refs[i][...])
            o_refs[4 * i][...] = g
            o_refs[4 * i + 1][...] = delta
            o_refs[4 * i + 2][...] = nm
            o_refs[4 * i + 3][...] = nv

    shapes = [jax.ShapeDtypeStruct(w.shape, F32) for w in ws for _ in range(4)]
    outs = pl.pallas_call(body, out_shape=shapes, name=name, compiler_params=_params())(*parts, *ws, *ms, *vs)
    return [outs[4 * i:4 * i + 4] for i in range(n)]


def _dup_heads(w):
    lead = w.shape[:-1]
    w4 = w.reshape(lead + (N_KV_HEADS, 1, HEAD_DIM))
    return jnp.broadcast_to(w4, lead + (N_KV_HEADS, 2, HEAD_DIM)).reshape(lead + (N_KV_HEADS * LANES,))


def _fold_heads(g):
    lead = g.shape[:-1]
    return g.reshape(lead + (N_KV_HEADS, 2, HEAD_DIM)).sum(axis=-2).reshape(lead + (N_KV_HEADS * HEAD_DIM,))


def kernel(x, a_norm, a_w_in, a_v_norm, a_w_s, a_b_s, a_w_out, f_norm, f_w_in, f_conv_w, f_conv_b, f_w_out, kv_norm, w_kv, k_norm, b_norm, b_w_q, b_q_norm, b_sinks, b_w_o, loss_target, m_a_norm, m_a_w_in, m_a_v_norm, m_a_w_s, m_a_b_s, m_a_w_out, m_f_norm, m_f_w_in, m_f_conv_w, m_f_conv_b, m_f_w_out, m_kv_norm, m_w_kv, m_k_norm, m_b_norm, m_b_w_q, m_b_q_norm, m_b_sinks, m_b_w_o, v_a_norm, v_a_w_in, v_a_v_norm, v_a_w_s, v_a_b_s, v_a_w_out, v_f_norm, v_f_w_in, v_f_conv_w, v_f_conv_b, v_f_w_out, v_kv_norm, v_w_kv, v_k_norm, v_b_norm, v_b_w_q, v_b_q_norm, v_b_sinks, v_b_w_o):
    d = D_MODEL
    xi, yi, ci = _coords()
    place = jnp.stack([2 * xi + yi, ci]).astype(jnp.int32)
    bf = lambda a: a.astype(BF16)
    row = lambda v_: v_.reshape(1, -1)
    x0, target = x[0], loss_target[0]
    t = x0.shape[0]
    res = {}

    red = {}

    def to_sibling(grads, wire=BF16):
        for k, g in grads.items():
            red[k] = dict(grad=g, wire=wire)
        ex = _ToSibling(list(grads.values()))
        ex.names = list(grads)
        return ex

    def to_chips(ex):
        for k, a in zip(ex.names, ex.results):
            red[k]["recv"] = a
            red[k]["psum"] = _chip_sum(red[k]["grad"], a, place, f"chip_sum_{k}", red[k]["wire"])
        nxt = _ToChips([red[k]["psum"] for k in ex.names])
        nxt.names = ex.names
        return nxt

    def landed(ex):
        for k, b in zip(ex.names, ex.results):
            red[k]["others"] = b

    def halves(ex, first_rows):
        parts = []
        for r0, nr in ((0, first_rows), (first_rows, ex.srcs[0].shape[1] - first_rows)):
            part = _ToChips(ex.srcs, rows=(r0, nr))
            part.names = ex.names
            parts.append(part)
        return parts

    def landed_halves(parts):
        for j, k in enumerate(parts[0].names):
            red[k]["others"] = jnp.concatenate([p.results[j] for p in parts], axis=1)

    def update(k, w, m, v, layer=None, fill=None):
        r = red[k]
        return _adamw_sharded(r["grad"], r["recv"], r["others"], place, w, m, v,
                              f"adamw_{k}", layer=layer, fill=fill)

    g_a_in, g_a_out, g_a_norm, g_a_v_norm, g_conv = _exchange_alone(
        _Gather([bf(a_w_in[0]), bf(a_w_out[0]), a_norm, a_v_norm, f_conv_w.reshape(6, FF_SHARD)]), "gather_first")
    a_norm_full, a_v_norm_full = g_a_norm.reshape(1, d), g_a_v_norm.reshape(1, d)
    conv_w = lax.reduce_precision(g_conv.reshape(N_SHARDS, 2, 3, FF_SHARD), 8, 7)
    cw = jnp.pad(jnp.transpose(conv_w, (1, 0, 2, 3)), ((0, 0), (0, 0), (0, 5), (0, 0)))
    w_a_in_flat = jnp.transpose(g_a_in, (1, 0, 2)).reshape(d, 2 * d)
    cb = f_conv_b.reshape(2, N_SHARDS, 1, FF_SHARD)
    tri = jnp.tril(jnp.ones((CHUNK, CHUNK), dtype=bool))
    w_causal = jnp.where(tri[None], a_w_s[0], 0.0).astype(BF16)
    w_causal_t = jnp.transpose(w_causal, (0, 2, 1))
    b_sb = jnp.broadcast_to(a_b_s[0][:, :, None], (N_GROUPS, CHUNK, CHUNK))
    w_a_out = g_a_out.reshape(d, d)
    gq = jnp.tile(b_q_norm.reshape(1, HEAD_DIM), (1, 2))
    gk = jnp.tile(k_norm.reshape(1, HEAD_DIM), (1, 2))
    sinks = b_sinks.reshape(N_Q_HEADS)

    ex = _Gather([bf(f_w_in[0]), bf(f_w_out[0])])
    zpre, x1, h1 = _sgu_fwd(x0, a_norm_full, g_a_in, a_v_norm_full, w_causal, b_sb, w_a_out, carry=ex)
    w_in0, w_out0 = ex.results[0], ex.results[1].reshape(D_FF, d)
    ex = _Gather([bf(w_kv), bf(b_w_q[0]), bf(b_w_o[0]), bf(f_w_in[1])], relay=False, early=True)
    x2, hf0, a0, pre0, hk, hq = _ffn_fwd(x1, f_norm[0:1], w_in0, cw[0], cb[0], w_out0, 0, carry=ex,
                                         next_gains=[row(kv_norm), b_norm])
    kv_full = ex.results[0].reshape(d, 2 * N_KV_HEADS * HEAD_DIM)
    w_q, w_o = ex.results[1].reshape(d, d), ex.results[2].reshape(d, d)
    w_in1 = ex.results[3]
    half = N_KV_HEADS * HEAD_DIM
    w_kv_dup = jnp.concatenate([_dup_heads(kv_full[:, :half]), _dup_heads(kv_full[:, half:])], axis=1)
    kvd = _mm_rows(hk, w_kv_dup, F32, "kv_proj")
    qraw = _mm_rows(hq, w_q, F32, "q_proj")
    ex = _Gather([bf(f_w_out[1])], relay=False, early=True)
    o = _attn_fwd(qraw, kvd, gq, gk, sinks, carry=ex)
    w_out1 = ex.results[0].reshape(D_FF, d)
    x3 = _mm_rows(o, w_o, F32, "o_proj", res=x2)
    _, hf1, a1, pre1, dy, loss_lanes = _ffn_fwd(x3, f_norm[1:2], w_in1, cw[1], cb[1], w_out1, 1, loss_target=target)

    dhu1, dw_out1, dcb1 = _ffn_bwd_act(pre1, w_out1, dy, 1)
    ex = to_sibling({"f_w_out1": dw_out1.reshape(N_SHARDS, D_FF // N_SHARDS, d)})
    da1, dx3, dcw1, dgf1 = _ffn_bwd_in(dhu1, a1, cw[1], w_in1, 1, carry=ex, norm=(x3, f_norm[1:2], dy))
    ex = to_chips(ex)
    dw_in1 = _ffn_wgrad_in(hf1, da1, 1, carry=ex)
    landed(ex)
    ex = to_sibling({"f_w_in1": dw_in1})
    d_o = _mm_rows(dx3, w_o, BF16, "o_proj_bwd", trans_w=True, carry=ex)
    ex = to_chips(ex)
    dw_o = _mm_wgrad(o, dx3, "o_wgrad").reshape(N_SHARDS, d // N_SHARDS, d)
    dq, dkv, dsink, dgq, dgk = _attn_bwd(qraw, kvd, d_o, gq, gk, sinks, carry=ex)
    landed(ex)
    dw_q = _mm_wgrad(hq, dq, "q_wgrad").reshape(N_SHARDS, d // N_SHARDS, d)
    dw_kv_dup = _mm_wgrad(hk, dkv, "kv_wgrad")
    dw_kv = jnp.concatenate(
        [_fold_heads(dw_kv_dup[:, :4 * LANES]), _fold_heads(dw_kv_dup[:, 4 * LANES:])], axis=1
    ).reshape(N_SHARDS, d // N_SHARDS, 2 * N_KV_HEADS * HEAD_DIM)
    ex = to_sibling({"b_w_o": dw_o, "b_w_q": dw_q, "w_kv": dw_kv})
    dx2, dg2 = _rms_bwd(x2, [row(kv_norm), b_norm], [dkv, dq], dx3, "kvq_norm_bwd", tm=512, carry=ex,
                        through=[w_kv_dup, w_q])
    ex = to_chips(ex)
    dhu0, dw_out0, dcb0 = _ffn_bwd_act(pre0, w_out0, dx2, 0, carry=ex)
    landed(ex)
    ex = to_sibling({"f_w_out0": dw_out0.reshape(N_SHARDS, D_FF // N_SHARDS, d)})
    da0, dhf0, dcw0 = _ffn_bwd_in(dhu0, a0, cw[0], w_in0, 0, tm=2048, carry=ex)
    ex = to_chips(ex)
    dw_in0 = _ffn_wgrad_in(hf0, da0, 0, carry=ex)
    landed(ex)
    ex = to_sibling({"f_w_in0": dw_in0})
    dx1, dgf0 = _rms_bwd(x1, [f_norm[0:1]], [dhf0], dx2, "f0_norm_bwd", carry=ex)
    ex_lo, ex_hi = halves(to_chips(ex), 384)
    dz, y, dwc, dbs, dgv = _sgu_bwd(dx1, zpre, w_a_out, a_v_norm_full, w_causal, w_causal_t, b_sb, carry=ex_lo)
    dw_a_out = _mm_wgrad(y, dx1, "a_out_wgrad").reshape(N_SHARDS, d // N_SHARDS, d)
    nsub = g_a_in.shape[2]
    dw_a_in = _mm(
        h1, dz, pl.BlockSpec((t, d), lambda s, j, kk: (0, 0)), pl.BlockSpec((t, nsub), lambda s, j, kk: (0, s)),
        pl.BlockSpec((None, d, nsub), lambda s, j, kk: (s, 0, 0)), jax.ShapeDtypeStruct((N_SHARDS, d, nsub), F32),
        (N_SHARDS, 1, 1), TN, "a_in_wgrad", carry=ex_hi)
    landed_halves([ex_lo, ex_hi])

    def bias_grad(dcb):
        return jnp.transpose(dcb[:, :, 0, :], (1, 0, 2)).reshape(-1)

    g_conv_w = jnp.concatenate([dcw0[:, 0:3, :], dcw1[:, 0:3, :]], axis=1)
    g_a_v_norm = dgv[0].reshape(N_SHARDS, 1, LANES)
    rep = ["a_w_s", "a_b_s", "f_norm", "f_conv_b", "kv_norm", "k_norm", "b_norm", "b_q_norm", "b_sinks"]
    rep_g = dict(
        a_w_s=dwc.reshape(N_GROUPS * CHUNK, CHUNK), a_b_s=dbs[:, :, 0], f_norm=jnp.stack([dgf0[0], dgf1[0]]),
        f_conv_b=jnp.stack([bias_grad(dcb0), bias_grad(dcb1)]), kv_norm=dg2[0:1],
        k_norm=(dgk[0, :HEAD_DIM] + dgk[0, HEAD_DIM:])[None], b_norm=dg2[1:2],
        b_q_norm=(dgq[0, :HEAD_DIM] + dgq[0, HEAD_DIM:])[None], b_sinks=dsink[:, 0][None])
    ex_big = to_sibling({"a_w_out": dw_a_out, "a_w_in": dw_a_in})
    ex_small = to_sibling({"a_v_norm": g_a_v_norm, "f_conv_w": g_conv_w}, wire=F32)
    ex_rep = _Gather([rep_g[k] for k in rep] + [loss_lanes], relay=False)
    together = _Together([ex_big, ex_small, ex_rep])
    dh1 = _mm_rows(dz, w_a_in_flat, F32, "a_in_bwd", trans_w=True, carry=together)
    together.spread()
    ex_big, ex_small = to_chips(ex_big), to_chips(ex_small)
    together = _Together([ex_big, ex_small])
    grad_x, dg0 = _rms_bwd(x0, [a_norm_full], [dh1], dx1, "a_norm_bwd", carry=together)
    together.spread()
    landed(ex_big)
    landed(ex_small)
    (a_norm_parts,) = _exchange_alone(_ToOwners([dg0[0].reshape(N_SHARDS, 1, LANES)]), "a_norm_to_owners")

    res["f_w_out"] = update("f_w_out1", f_w_out, m_f_w_out, v_f_w_out, layer=1)
    w_in_t = [jnp.swapaxes(a_, 1, 2) for a_ in (f_w_in, m_f_w_in, v_f_w_in)]
    res["f_w_in"] = update("f_w_in1", *w_in_t, layer=1)
    res["b_w_o"] = update("b_w_o", b_w_o, m_b_w_o, v_b_w_o, layer=0)
    res["b_w_q"] = update("b_w_q", b_w_q, m_b_w_q, v_b_w_q, layer=0)
    res["w_kv"] = update("w_kv", w_kv, m_w_kv, v_w_kv)
    res["f_w_out"] = update("f_w_out0", f_w_out, m_f_w_out, v_f_w_out, layer=0, fill=res["f_w_out"])
    res["f_w_in"] = [jnp.swapaxes(o_, 1, 2) for o_ in update("f_w_in0", *w_in_t, layer=0, fill=res["f_w_in"])]
    res["a_w_out"] = update("a_w_out", a_w_out, m_a_w_out, v_a_w_out, layer=0)
    res["a_w_in"] = update("a_w_in", a_w_in, m_a_w_in, v_a_w_in, layer=0)
    res["a_v_norm"] = update("a_v_norm", a_v_norm, m_a_v_norm, v_a_v_norm)
    res["f_conv_w"] = [o_.reshape(f_conv_w.shape) for o_ in update(
        "f_conv_w", f_conv_w.reshape(6, FF_SHARD), m_f_conv_w.reshape(6, FF_SHARD), v_f_conv_w.reshape(6, FF_SHARD))]

    rep_w = dict(a_w_s=a_w_s, a_b_s=a_b_s, f_norm=f_norm, f_conv_b=f_conv_b, kv_norm=kv_norm, k_norm=k_norm,
                 b_norm=b_norm, b_q_norm=b_q_norm, b_sinks=b_sinks, a_norm=a_norm)
    rep_m = dict(a_w_s=m_a_w_s, a_b_s=m_a_b_s, f_norm=m_f_norm, f_conv_b=m_f_conv_b, kv_norm=m_kv_norm,
                 k_norm=m_k_norm, b_norm=m_b_norm, b_q_norm=m_b_q_norm, b_sinks=m_b_sinks, a_norm=m_a_norm)
    rep_v = dict(a_w_s=v_a_w_s, a_b_s=v_a_b_s, f_norm=v_f_norm, f_conv_b=v_f_conv_b, kv_norm=v_kv_norm,
                 k_norm=v_k_norm, b_norm=v_b_norm, b_q_norm=v_b_q_norm, b_sinks=v_b_sinks, a_norm=v_a_norm)
    keys = rep + ["a_norm"]
    loss = _sum_devices(ex_rep.results[-1], "loss_sum")[0, 0]
    parts = ex_rep.results[:-1] + [a_norm_parts]
    as2d = lambda a, p: a.reshape(p.shape[1:])
    rep_outs = _adamw_summed(parts, [as2d(rep_w[k], p) for k, p in zip(keys, parts)],
                             [as2d(rep_m[k], p) for k, p in zip(keys, parts)],
                             [as2d(rep_v[k], p) for k, p in zip(keys, parts)], "adamw_replicated")
    for j, key in enumerate(keys):
        res[key] = [o_.reshape(rep_w[key].shape) for o_ in rep_outs[j]]

    order = ["a_norm", "a_w_in", "a_v_norm", "a_w_s", "a_b_s", "a_w_out", "f_norm", "f_w_in", "f_conv_w", "f_conv_b",
             "f_w_out", "kv_norm", "w_kv", "k_norm", "b_norm", "b_w_q", "b_q_norm", "b_sinks", "b_w_o"]
    outs = [loss, grad_x[None]]
    for j in range(4):
        outs += [res[k][j] for k in order]
    return tuple(outs)
```

```python
import jax
import jax.numpy as jnp
from jax import lax
from jax.experimental import pallas as pl
from jax.experimental.pallas import tpu as pltpu

F32 = jnp.float32
BF16 = jnp.bfloat16
EPS = 1e-6
D_MODEL = 1024
CHUNK = 128
N_GROUPS = 8
N_SHARDS = 8
HEAD_DIM = 64
N_Q_HEADS = 16
N_KV_HEADS = 4
D_FF = 2816
FF_SHARD = 2 * D_FF // N_SHARDS
LANES = 128
NEG_BIG = -1e30
ADAM_LR = 0.001
ADAM_B1 = 0.9
ADAM_B2 = 0.999
ADAM_EPS = 1e-08
ADAM_WD = 0.01
ADAM_STEP = 10
VMEM_LIMIT_BYTES = 56 * 1024 * 1024
MESH = pl.DeviceIdType.MESH

NN = (((1,), (0,)), ((), ()))
NT = (((1,), (1,)), ((), ()))
TN = (((0,), (0,)), ((), ()))
SLOPES = tuple(2.0 ** (-8.0 * (h + 1) / N_Q_HEADS) for h in range(N_Q_HEADS))


def _params(sem=None):
    return pltpu.CompilerParams(dimension_semantics=sem, vmem_limit_bytes=VMEM_LIMIT_BYTES)


def _dot(a, b, dims=NN):
    return lax.dot_general(a, b, dims, preferred_element_type=F32)


def _sigmoid(x):
    return 1.0 / (1.0 + jnp.exp(-x))


def _gelu_parts(z):
    cdf = 0.5 * (1.0 + lax.erf(z * (2.0 ** -0.5)))
    pdf = jnp.exp(-0.5 * z * z) * 0.3989422804014327
    return cdf, pdf


def _coords():
    return lax.axis_index("x"), lax.axis_index("y"), lax.axis_index("c")


class _Gather:
    def __init__(self, srcs, relay=True, early=False):
        self.srcs = list(srcs)
        self.early = early
        n = len(self.srcs)
        self.relayed = [relay and s.shape[0] % 32 == 0 for s in self.srcs]
        self.out_shapes = [jax.ShapeDtypeStruct((N_SHARDS,) + s.shape, s.dtype) for s in self.srcs]
        self.sems = [pltpu.SemaphoreType.DMA((n, 9)), pltpu.SemaphoreType.DMA((n, 9)), pltpu.SemaphoreType.DMA((n,))]

    def _plan(self, src, dst, sems):
        send_sems, recv_sems, local_sems = sems
        x, y, c = _coords()
        n = len(src)

        def rows(e, dev, half=None):
            block = dst[e].at[4 * dev[0] + 2 * dev[1] + dev[2]]
            if half is None:
                return block
            nr = self.srcs[e].shape[0] // 2
            return block.at[pl.ds(half * nr, nr)]

        def copy(e, slot, block, to, half=None, from_own=False):
            return pltpu.make_async_remote_copy(
                src_ref=src[e] if from_own else rows(e, block, half), dst_ref=rows(e, block, half),
                send_sem=send_sems.at[e, slot], recv_sem=recv_sems.at[e, slot], device_id=to, device_id_type=MESH)

        return n, x, y, c, rows, copy, local_sems

    def start(self, src, dst, sems):
        n, x, y, c, rows, copy, local_sems = self._plan(src, dst, sems)
        me = (x, y, c)
        for e in range(n):
            pltpu.make_async_copy(src[e], rows(e, me), local_sems.at[e]).start()
            copy(e, 0, me, (x, y, 1 - c), from_own=True).start()
            copy(e, 1, me, (1 - x, y, c), from_own=True).start()
            copy(e, 2, me, (x, 1 - y, c), from_own=True).start()
            if not self.relayed[e]:
                copy(e, 3, me, (1 - x, 1 - y, c), from_own=True).start()

    def pass_on(self, src, dst, sems, wait=True):
        n, x, y, c, rows, copy, local_sems = self._plan(src, dst, sems)
        me, sibling = (x, y, c), (x, y, 1 - c)
        over_x, over_y, diagonal = (1 - x, y, c), (x, 1 - y, c), (1 - x, 1 - y, c)
        sent = []

        def arrived(cp):
            if wait:
                cp.wait_recv()

        def send(cp):
            if wait:
                cp.start()
            sent.append(cp)

        for slot, owner, onward, half in ((1, over_x, over_y, 0), (2, over_y, over_x, 1)):
            for e in range(n):
                arrived(copy(e, slot, owner, me))
                if self.relayed[e]:
                    send(copy(e, 3 + half, owner, onward, half=half))
                send(copy(e, 4 + slot, owner, sibling))
        for e in range(n):
            if self.relayed[e]:
                for half in (0, 1):
                    arrived(copy(e, 3 + half, diagonal, me, half=half))
                    send(copy(e, 7 + half, diagonal, sibling, half=half))
            else:
                arrived(copy(e, 3, diagonal, me))
                send(copy(e, 7, diagonal, sibling))
        return sent

    def finish(self, src, dst, sems, passed_on=False):
        n, x, y, c, rows, copy, local_sems = self._plan(src, dst, sems)
        me, sibling = (x, y, c), (x, y, 1 - c)
        over_x, over_y, diagonal = (1 - x, y, c), (x, 1 - y, c), (1 - x, 1 - y, c)
        sent = self.pass_on(src, dst, sems, wait=not passed_on)
        for e in range(n):
            copy(e, 0, sibling, me).wait_recv()
            copy(e, 5, (1 - x, y, 1 - c), me).wait_recv()
            copy(e, 6, (x, 1 - y, 1 - c), me).wait_recv()
            if self.relayed[e]:
                for half in (0, 1):
                    copy(e, 7 + half, (1 - x, 1 - y, 1 - c), me, half=half).wait_recv()
            else:
                copy(e, 7, (1 - x, 1 - y, 1 - c), me).wait_recv()
        for e in range(n):
            copy(e, 0, me, sibling, from_own=True).wait_send()
            copy(e, 1, me, over_x, from_own=True).wait_send()
            copy(e, 2, me, over_y, from_own=True).wait_send()
            if not self.relayed[e]:
                copy(e, 3, me, diagonal, from_own=True).wait_send()
            pltpu.make_async_copy(src[e], rows(e, me), local_sems.at[e]).wait()
        for cp in sent:
            cp.wait_send()


class _ToSibling:
    def __init__(self, grads):
        self.srcs = list(grads)
        n = len(self.srcs)
        self.out_shapes = [jax.ShapeDtypeStruct((4,) + g.shape[1:], g.dtype) for g in self.srcs]
        self.sems = [pltpu.SemaphoreType.DMA((n, 4)), pltpu.SemaphoreType.DMA((n, 4))]

    def _copies(self, src, dst, sems):
        send_sems, recv_sems = sems
        x, y, c = _coords()
        return [
            pltpu.make_async_remote_copy(
                src_ref=src[i].at[2 * q + (1 - c)], dst_ref=dst[i].at[q], send_sem=send_sems.at[i, q],
                recv_sem=recv_sems.at[i, q], device_id=(x, y, 1 - c), device_id_type=MESH)
            for i in range(len(src)) for q in range(4)]

    def start(self, src, dst, sems):
        for cp in self._copies(src, dst, sems):
            cp.start()

    def finish(self, src, dst, sems):
        for cp in self._copies(src, dst, sems):
            cp.wait()


class _ToChips:
    def __init__(self, psums, rows=None):
        self.srcs = list(psums)
        n = len(self.srcs)
        self.rows = rows
        self.out_shapes = [
            jax.ShapeDtypeStruct((3, p.shape[1] if rows is None else rows[1]) + p.shape[2:], p.dtype)
            for p in self.srcs]
        self.sems = [pltpu.SemaphoreType.DMA((n, 3)), pltpu.SemaphoreType.DMA((n, 3))]

    def _copies(self, src, dst, sems):
        send_sems, recv_sems = sems
        x, y, c = _coords()
        peers = [(x, 1 - y), (1 - x, y), (1 - x, 1 - y)]

        def part(i, q):
            if self.rows is None:
                return src[i].at[q]
            return src[i].at[q, pl.ds(self.rows[0], self.rows[1])]

        return [
            pltpu.make_async_remote_copy(
                src_ref=part(i, 2 * px + py), dst_ref=dst[i].at[r], send_sem=send_sems.at[i, r],
                recv_sem=recv_sems.at[i, r], device_id=(px, py, c), device_id_type=MESH)
            for i in range(len(src)) for r, (px, py) in enumerate(peers)]

    def start(self, src, dst, sems):
        for cp in self._copies(src, dst, sems):
            cp.start()

    def finish(self, src, dst, sems):
        for cp in self._copies(src, dst, sems):
            cp.wait()


class _ToOwners:
    def __init__(self, grads):
        self.srcs = list(grads)
        n = len(self.srcs)
        self.out_shapes = [jax.ShapeDtypeStruct(g.shape, g.dtype) for g in self.srcs]
        self.sems = [pltpu.SemaphoreType.DMA((n, 7)), pltpu.SemaphoreType.DMA((n, 7)), pltpu.SemaphoreType.DMA((n,))]

    def _copies(self, src, dst, sems):
        send_sems, recv_sems, local_sems = sems
        x, y, c = _coords()
        me = 4 * x + 2 * y + c
        copies = [pltpu.make_async_copy(src[i].at[me], dst[i].at[me], local_sems.at[i]) for i in range(len(src))]
        for i in range(len(src)):
            for rel in range(1, N_SHARDS):
                px = x ^ (rel >> 2) if rel >> 2 else x
                py = y ^ ((rel >> 1) & 1) if (rel >> 1) & 1 else y
                pc = c ^ (rel & 1) if rel & 1 else c
                copies.append(pltpu.make_async_remote_copy(
                    src_ref=src[i].at[4 * px + 2 * py + pc], dst_ref=dst[i].at[me], send_sem=send_sems.at[i, rel - 1],
                    recv_sem=recv_sems.at[i, rel - 1], device_id=(px, py, pc), device_id_type=MESH))
        return copies

    def start(self, src, dst, sems):
        for cp in self._copies(src, dst, sems):
            cp.start()

    def finish(self, src, dst, sems):
        for cp in self._copies(src, dst, sems):
            cp.wait()


class _Together:
    def __init__(self, parts):
        self.parts = list(parts)
        self.srcs = [s for p in self.parts for s in p.srcs]
        self.out_shapes = [s for p in self.parts for s in p.out_shapes]
        self.sems = [s for p in self.parts for s in p.sems]

    def _split(self, src, dst, sems):
        a = b = c = 0
        for p in self.parts:
            na, nc = len(p.srcs), len(p.sems)
            yield p, src[a:a + na], dst[b:b + na], sems[c:c + nc]
            a, b, c = a + na, b + na, c + nc

    def start(self, src, dst, sems):
        for p, s, d, m in self._split(src, dst, sems):
            p.start(s, d, m)

    def finish(self, src, dst, sems):
        for p, s, d, m in self._split(src, dst, sems):
            p.finish(s, d, m)

    def spread(self):
        b = 0
        for p in self.parts:
            p.results = self.results[b:b + len(p.srcs)]
            b += len(p.srcs)


def _call(body, args, *, grid, in_specs, out_specs, out_shape, name, scratch=(), sem=None, carry=None):
    out_shape, out_specs = list(out_shape), list(out_specs)
    if carry is None:
        return pl.pallas_call(
            body, grid=grid, in_specs=list(in_specs), out_specs=out_specs, out_shape=out_shape,
            scratch_shapes=list(scratch), name=name, compiler_params=_params(sem))(*args)
    n_in, n_out, n_scr, n_c = len(args), len(out_shape), len(scratch), len(carry.srcs)
    steps = tuple(grid)
    total = 1
    for n_ax in steps:
        total *= n_ax
    early = getattr(carry, "early", False) and total >= 8
    early_step = total - max(2, total // 8)

    def carried(*refs):
        ins, rest = refs[:n_in], refs[n_in:]
        c_src, rest = rest[:n_c], rest[n_c:]
        outs, rest = rest[:n_out], rest[n_out:]
        c_dst, rest = rest[:n_c], rest[n_c:]
        scr, sems = rest[:n_scr], rest[n_scr:]
        step = pl.program_id(0)
        for ax in range(1, len(steps)):
            step = step * steps[ax] + pl.program_id(ax)

        @pl.when(step == 0)
        def _():
            carry.start(c_src, c_dst, sems)

        body(*ins, *outs, *scr)

        if early:
            @pl.when(step == early_step)
            def _():
                carry.pass_on(c_src, c_dst, sems)

        @pl.when(step == total - 1)
        def _():
            if early:
                carry.finish(c_src, c_dst, sems, passed_on=True)
            else:
                carry.finish(c_src, c_dst, sems)

    hbm = pl.BlockSpec(memory_space=pl.ANY)
    res = pl.pallas_call(
        carried, grid=grid, in_specs=list(in_specs) + [hbm] * n_c, out_specs=out_specs + [hbm] * n_c,
        out_shape=out_shape + carry.out_shapes, scratch_shapes=list(scratch) + carry.sems, name=name,
        compiler_params=_params(("arbitrary",) * len(steps)))(*args, *carry.srcs)
    carry.results = list(res[n_out:])
    return list(res[:n_out])


def _exchange_alone(ex, name):
    n = len(ex.srcs)

    def body(*refs):
        src, dst, sems = refs[:n], refs[n:2 * n], refs[2 * n:]
        ex.start(src, dst, sems)
        ex.finish(src, dst, sems)

    hbm = pl.BlockSpec(memory_space=pl.ANY)
    res = pl.pallas_call(body, in_specs=[hbm] * n, out_specs=[hbm] * n, out_shape=ex.out_shapes,
                         scratch_shapes=ex.sems, name=name)(*ex.srcs)
    ex.results = list(res)
    return ex.results


def _rms_bwd(x, gains, dhs, dres, name, tm=512, carry=None, through=None):
    t, d = x.shape
    n = len(gains)
    n_w = 0 if through is None else n

    def body(*refs):
        x_ref, dres_ref = refs[0], refs[1]
        g_refs, dh_refs, w_refs = refs[2:2 + n], refs[2 + n:2 + 2 * n], refs[2 + 2 * n:2 + 2 * n + n_w]
        dx_ref, dg_ref = refs[2 + 2 * n + n_w], refs[3 + 2 * n + n_w]
        i = pl.program_id(0)

        @pl.when(i == 0)
        def _():
            dg_ref[...] = jnp.zeros_like(dg_ref)

        xf = x_ref[...]
        r = lax.rsqrt(jnp.mean(xf * xf, axis=-1, keepdims=True) + EPS)
        xhat = xf * r
        dx = dres_ref[...]
        for j in range(n):
            dh = dh_refs[j][...]
            if n_w:
                dh = _dot(dh.astype(BF16), w_refs[j][...], NT)
            dg_ref[j:j + 1, :] += jnp.sum(dh * xhat, axis=0, keepdims=True)
            gy = dh * g_refs[j][...]
            dx = dx + r * (gy - xhat * jnp.mean(gy * xhat, axis=-1, keepdims=True))
        dx_ref[...] = dx

    row = pl.BlockSpec((tm, d), lambda i: (i, 0))
    vec = pl.BlockSpec((1, d), lambda i: (0, 0))
    dh_rows = [pl.BlockSpec((tm, dh.shape[1]), lambda i: (i, 0)) for dh in dhs]
    w_full = [] if through is None else [pl.BlockSpec(w.shape, lambda i: (0, 0)) for w in through]
    return _call(body, [x, dres, *gains, *dhs, *(through or [])], grid=(t // tm,),
                 in_specs=[row, row] + [vec] * n + dh_rows + w_full,
                 out_specs=[row, pl.BlockSpec((8, d), lambda i: (0, 0))],
                 out_shape=[jax.ShapeDtypeStruct((t, d), F32), jax.ShapeDtypeStruct((8, d), F32)],
                 name=name, sem=("arbitrary",), carry=carry)


def _mm(a, b, a_spec, b_spec, o_spec, out_shape, grid, dims, name, res=None, res_spec=None, carry=None):
    nk = grid[2]
    acc_shape = tuple(s for s in o_spec.block_shape if s is not None)

    def body(*refs):
        a_ref, b_ref = refs[0], refs[1]
        r_ref = refs[2] if res is not None else None
        o_ref = refs[3] if res is not None else refs[2]
        p = _dot(a_ref[...].astype(BF16), b_ref[...].astype(BF16), dims)
        if nk == 1:
            if res is not None:
                p = p + r_ref[...]
            o_ref[...] = p.astype(o_ref.dtype)
            return
        acc_ref = refs[-1]
        k = pl.program_id(2)

        @pl.when(k == 0)
        def _():
            acc_ref[...] = p

        @pl.when(k > 0)
        def _():
            acc_ref[...] += p

        @pl.when(k == nk - 1)
        def _():
            out = acc_ref[...]
            if res is not None:
                out = out + r_ref[...]
            o_ref[...] = out.astype(o_ref.dtype)

    ins = [a, b] + ([res] if res is not None else [])
    specs = [a_spec, b_spec] + ([res_spec] if res is not None else [])
    return _call(body, ins, grid=grid, in_specs=specs, out_specs=[o_spec], out_shape=[out_shape],
                 scratch=[pltpu.VMEM(acc_shape, F32)] if nk > 1 else [], name=name,
                 sem=("parallel", "parallel", "arbitrary"), carry=carry)[0]


def _mm_rows(a, w, out_dtype, name, trans_w=False, res=None, tm=1024, carry=None):
    t, k = a.shape
    tm = min(tm, t)
    n = w.shape[0] if trans_w else w.shape[1]
    return _mm(
        a, w, pl.BlockSpec((tm, k), lambda i, j, kk: (i, 0)), pl.BlockSpec(w.shape, lambda i, j, kk: (0, 0)),
        pl.BlockSpec((tm, n), lambda i, j, kk: (i, 0)), jax.ShapeDtypeStruct((t, n), out_dtype), (t // tm, 1, 1),
        NT if trans_w else NN, name, res=res,
        res_spec=None if res is None else pl.BlockSpec((tm, n), lambda i, j, kk: (i, 0)), carry=carry)


def _mm_rows_pair(a1, w1, a2, w2, name, tm=1024):
    t, k = a1.shape
    n = w1.shape[1]
    tm = min(tm, t)

    def body(a1_ref, w1_ref, a2_ref, w2_ref, o1_ref, o2_ref):
        o1_ref[...] = _dot(a1_ref[...], w1_ref[...])
        o2_ref[...] = _dot(a2_ref[...], w2_ref[...])

    rows_in = pl.BlockSpec((tm, k), lambda i: (i, 0))
    rows_out = pl.BlockSpec((tm, n), lambda i: (i, 0))
    full = pl.BlockSpec((k, n), lambda i: (0, 0))
    return _call(body, [a1, w1, a2, w2], grid=(t // tm,), in_specs=[rows_in, full, rows_in, full],
                 out_specs=[rows_out, rows_out], out_shape=[jax.ShapeDtypeStruct((t, n), F32)] * 2, name=name)


def _mm_wgrad(a, b, name, carry=None):
    t, m = a.shape
    n = b.shape[1]
    tn = n // (4 if b.dtype == F32 else 2)
    return _mm(
        a, b, pl.BlockSpec((t, m), lambda i, j, kk: (0, 0)), pl.BlockSpec((t, tn), lambda i, j, kk: (0, j)),
        pl.BlockSpec((m, tn), lambda i, j, kk: (0, j)), jax.ShapeDtypeStruct((m, n), F32), (1, n // tn, 1), TN, name,
        carry=carry)


def _sgu_fwd(x0, g, w_in, g_v, w_c, b_sb, w_out, tm=256, carry=None):
    t, d = x0.shape
    nsub = w_in.shape[2]

    def body(x_ref, g_ref, win_ref, gv_ref, wc_ref, bsb_ref, wout_ref, zpre_ref, x1_ref, h_ref, u_s, v_s, vn_s, y_s):
        xf = x_ref[...]
        h = (xf * lax.rsqrt(jnp.mean(xf * xf, axis=-1, keepdims=True) + EPS) * g_ref[...]).astype(BF16)
        h_ref[...] = h
        for k in range(N_SHARDS):
            zk = _dot(h, win_ref[k])
            zpre_ref[:, k * nsub:(k + 1) * nsub] = zk
            cdf, _ = _gelu_parts(zk)
            if k < N_SHARDS // 2:
                u_s[:, k * nsub:(k + 1) * nsub] = zk * cdf
            else:
                v_s[:, (k - 4) * nsub:(k - 3) * nsub] = zk * cdf
        v = v_s[...]
        rv = lax.rsqrt(jnp.mean(v * v, axis=-1, keepdims=True) + EPS)
        vn_s[...] = (v * rv * gv_ref[...]).astype(BF16)
        for ci in range(tm // CHUNK):
            rows = slice(ci * CHUNK, (ci + 1) * CHUNK)
            for g in range(N_GROUPS):
                cols = slice(g * LANES, (g + 1) * LANES)
                sv = _dot(wc_ref[g], vn_s[rows, cols]) + bsb_ref[g]
                y_s[rows, cols] = (u_s[rows, cols] * sv).astype(BF16)
        x1_ref[...] = x_ref[...] + _dot(y_s[...], wout_ref[...])

    row = pl.BlockSpec((tm, d), lambda i: (i, 0))
    full = lambda a: pl.BlockSpec(a.shape, lambda i: (0,) * a.ndim)
    return _call(
        body, [x0, g, w_in, g_v, w_c, b_sb, w_out], grid=(t // tm,),
        in_specs=[row, full(g), full(w_in), full(g_v), full(w_c), full(b_sb), full(w_out)],
        out_specs=[pl.BlockSpec((tm, 2 * d), lambda i: (i, 0)), row, row],
        out_shape=[jax.ShapeDtypeStruct((t, 2 * d), F32), jax.ShapeDtypeStruct((t, d), F32),
                   jax.ShapeDtypeStruct((t, d), BF16)],
        scratch=[pltpu.VMEM((tm, d), F32), pltpu.VMEM((tm, d), F32), pltpu.VMEM((tm, d), BF16),
                 pltpu.VMEM((tm, d), BF16)],
        name="sgu_fwd", carry=carry)


def _sgu_bwd(dx1, zpre, w_out, g_v, w_c, w_ct, b_sb, tm=512, carry=None):
    t, d = dx1.shape

    def body(dx_ref, zpre_ref, wout_ref, gv_ref, wc_ref, wct_ref, bsb_ref,
             dz_ref, y_ref, dwc_ref, dbs_ref, dgv_ref, u_s, vn_s, dy_s, du_s, dvn_s):
        i = pl.program_id(0)

        @pl.when(i == 0)
        def _():
            dwc_ref[...] = jnp.zeros_like(dwc_ref)
            dbs_ref[...] = jnp.zeros_like(dbs_ref)
            dgv_ref[...] = jnp.zeros_like(dgv_ref)

        dy_s[...] = _dot(dx_ref[...].astype(BF16), wout_ref[...], NT)
        zu = zpre_ref[:, :d]
        zv = zpre_ref[:, d:]
        cdf_u, pdf_u = _gelu_parts(zu)
        cdf_v, pdf_v = _gelu_parts(zv)
        u_s[...] = zu * cdf_u
        v = zv * cdf_v
        rv = lax.rsqrt(jnp.mean(v * v, axis=-1, keepdims=True) + EPS)
        vhat = v * rv
        gv = gv_ref[...]
        vn_s[...] = (vhat * gv).astype(BF16)
        for ci in range(tm // CHUNK):
            rows = slice(ci * CHUNK, (ci + 1) * CHUNK)
            for g in range(N_GROUPS):
                cols = slice(g * LANES, (g + 1) * LANES)
                vnb = vn_s[rows, cols]
                sv = _dot(wc_ref[g], vnb) + bsb_ref[g]
                dyb = dy_s[rows, cols]
                ub = u_s[rows, cols]
                dsv = dyb * ub
                du_s[rows, cols] = dyb * sv
                y_ref[rows, cols] = (ub * sv).astype(BF16)
                dsvb = dsv.astype(BF16)
                dbs_ref[g] += dsv
                dwc_ref[g] += _dot(dsvb, vnb, NT)
                dvn_s[rows, cols] = _dot(wct_ref[g], dsvb)
        dvn = dvn_s[...]
        dgv_ref[0:1, :] += jnp.sum(dvn * vhat, axis=0, keepdims=True)
        gy = dvn * gv
        dv = rv * (gy - vhat * jnp.mean(gy * vhat, axis=-1, keepdims=True))
        dz_ref[:, :d] = (du_s[...] * (cdf_u + zu * pdf_u)).astype(BF16)
        dz_ref[:, d:] = (dv * (cdf_v + zv * pdf_v)).astype(BF16)

        @pl.when(i == t // tm - 1)
        def _():
            tri = (lax.broadcasted_iota(jnp.int32, (CHUNK, CHUNK), 0)
                   >= lax.broadcasted_iota(jnp.int32, (CHUNK, CHUNK), 1))
            for g in range(N_GROUPS):
                dwc_ref[g] = jnp.where(tri, dwc_ref[g], 0.0)
                dbs_ref[g] = jnp.broadcast_to(jnp.sum(dbs_ref[g], axis=1, keepdims=True), (CHUNK, CHUNK))

    row = pl.BlockSpec((tm, d), lambda i: (i, 0))
    row2 = pl.BlockSpec((tm, 2 * d), lambda i: (i, 0))
    full = lambda a: pl.BlockSpec(a.shape, lambda i: (0,) * a.ndim)
    grp = pl.BlockSpec((N_GROUPS, CHUNK, CHUNK), lambda i: (0, 0, 0))
    return _call(
        body, [dx1, zpre, w_out, g_v, w_c, w_ct, b_sb], grid=(t // tm,),
        in_specs=[row, row2, full(w_out), full(g_v), full(w_c), full(w_ct), full(b_sb)],
        out_specs=[row2, row, grp, grp, pl.BlockSpec((8, d), lambda i: (0, 0))],
        out_shape=[jax.ShapeDtypeStruct((t, 2 * d), BF16), jax.ShapeDtypeStruct((t, d), BF16),
                   jax.ShapeDtypeStruct((N_GROUPS, CHUNK, CHUNK), F32),
                   jax.ShapeDtypeStruct((N_GROUPS, CHUNK, CHUNK), F32), jax.ShapeDtypeStruct((8, d), F32)],
        scratch=[pltpu.VMEM((tm, d), F32), pltpu.VMEM((tm, d), BF16), pltpu.VMEM((tm, d), F32),
                 pltpu.VMEM((tm, d), F32), pltpu.VMEM((tm, d), F32)],
        name="sgu_bwd", sem=("arbitrary",), carry=carry)


ROW_CHUNK = 256
HALO = 16


def _ffn_fwd(x, g, w_in, cw, cb, w_out, layer, tm=512, carry=None, next_gains=(), loss_target=None):
    t, d = x.shape
    nc = N_SHARDS // 2
    n_gains = len(next_gains)
    with_loss = loss_target is not None

    def body(x_ref, xp_ref, g_ref, wg_ref, wu_ref, cwg_ref, cbg_ref, cwu_ref, cbu_ref, wout_ref, *rest):
        extra_in, rest = rest[:n_gains + with_loss], rest[n_gains + with_loss:]
        o_ref, hf_ref, a_ref, pre_ref = rest[:4]
        extra_out, hw_s = rest[4:-1], rest[-1]
        i, c = pl.program_id(0), pl.program_id(1)

        @pl.when(c == 0)
        def _():
            keep = jnp.where(i == 0, 0.0, 1.0)
            xw = jnp.concatenate([xp_ref[...] * keep, x_ref[...]], axis=0)
            xhat = xw * lax.rsqrt(jnp.mean(xw * xw, axis=-1, keepdims=True) + EPS)
            hw_s[...] = (xhat * g_ref[...]).astype(BF16)
            hf_ref[...] = hw_s[HALO:, :]
            o_ref[...] = x_ref[...]

        hw = hw_s[...]
        pre = []
        for j, (w_ref, cw_ref, cb_ref) in enumerate(((wg_ref, cwg_ref, cbg_ref), (wu_ref, cwu_ref, cbu_ref))):
            ab = _dot(hw, w_ref[...]).astype(BF16)
            a_ref[j] = ab[HALO:]
            win = ab.astype(F32)
            cw_v = cw_ref[...]
            pre.append(cw_v[2:3, :] * win[HALO:] + cw_v[1:2, :] * pltpu.roll(win, 1, 0)[HALO:]
                       + cw_v[0:1, :] * pltpu.roll(win, 2, 0)[HALO:] + cb_ref[...])
            pre_ref[j] = pre[j]
        act = (pre[0] * _sigmoid(pre[0]) * pre[1]).astype(BF16)
        o_ref[...] += _dot(act, wout_ref[...])

        if with_loss:
            @pl.when((i == 0) & (c == 0))
            def _():
                extra_out[-1][...] = jnp.zeros_like(extra_out[-1])

        @pl.when(c == nc - 1)
        def _():
            xn = o_ref[...]
            if n_gains:
                xhat = xn * lax.rsqrt(jnp.mean(xn * xn, axis=-1, keepdims=True) + EPS)
                for k in range(n_gains):
                    extra_out[k][...] = (xhat * extra_in[k][...]).astype(BF16)
            if with_loss:
                err = xn - extra_in[-1][...]
                extra_out[-2][...] = err * (1.0 / d)
                part = jnp.sum(jnp.sum(err * err, axis=0, keepdims=True), axis=1, keepdims=True)
                extra_out[-1][...] += jnp.broadcast_to(0.5 / d * part, extra_out[-1].shape)

    row = pl.BlockSpec((tm, d), lambda i, c: (i, 0))
    vec = pl.BlockSpec((1, d), lambda i, c: (0, 0))
    shard = lambda rows, up: pl.BlockSpec((None, rows, FF_SHARD), lambda i, c: (c + up * nc, 0, 0))
    pair = pl.BlockSpec((2, None, tm, FF_SHARD), lambda i, c: (0, c, i, 0))
    lanes = pl.BlockSpec((8, LANES), lambda i, c: (0, 0))
    outs = _call(
        body, [x, x, g, w_in, w_in, cw, cb, cw, cb, w_out, *next_gains] + ([loss_target] if with_loss else []),
        grid=(t // tm, nc),
        in_specs=[row, pl.BlockSpec((HALO, d), lambda i, c: (jnp.maximum(i * (tm // HALO) - 1, 0), 0)),
                  vec, shard(d, 0), shard(d, 1), shard(8, 0), shard(1, 0), shard(8, 1), shard(1, 1),
                  pl.BlockSpec((FF_SHARD, d), lambda i, c: (c, 0))] + [vec] * n_gains + [row] * with_loss,
        out_specs=[row, row, pair, pair] + [row] * n_gains + [row, lanes] * with_loss,
        out_shape=[jax.ShapeDtypeStruct((t, d), F32), jax.ShapeDtypeStruct((t, d), BF16),
                   jax.ShapeDtypeStruct((2, nc, t, FF_SHARD), BF16), jax.ShapeDtypeStruct((2, nc, t, FF_SHARD), F32)]
        + [jax.ShapeDtypeStruct((t, d), BF16)] * n_gains
        + [jax.ShapeDtypeStruct((t, d), F32), jax.ShapeDtypeStruct((8, LANES), F32)] * with_loss,
        scratch=[pltpu.VMEM((tm + HALO, d), BF16)], name=f"ffn{layer}_fwd", sem=("arbitrary", "arbitrary"), carry=carry)
    return (outs[0], outs[1], outs[2].reshape(N_SHARDS, t, FF_SHARD), outs[3]) + tuple(outs[4:])


def _ffn_bwd_act(pre, w_out, dxn, layer, tm=1024, carry=None):
    t, d = dxn.shape
    tm = min(tm, t)
    nc = N_SHARDS // 2

    def body(pre_ref, wout_ref, dx_ref, dhu_ref, dw_ref, dcb_ref):
        i = pl.program_id(1)

        @pl.when(i == 0)
        def _():
            dw_ref[...] = jnp.zeros_like(dw_ref)
            dcb_ref[...] = jnp.zeros_like(dcb_ref)

        hg, hu = pre_ref[0], pre_ref[1]
        sg = _sigmoid(hg)
        sl = hg * sg
        dxb = dx_ref[...].astype(BF16)
        dact = _dot(dxb, wout_ref[...], NT)
        dw_ref[...] += _dot((sl * hu).astype(BF16), dxb, TN)
        d_up = dact * sl
        d_gate = dact * hu * (sg * (1.0 + hg * (1.0 - sg)))
        for j, dv in enumerate((d_gate, d_up)):
            dhu_ref[j] = dv.astype(BF16)
            dcb_ref[j, 0:1, :] += jnp.sum(dv, axis=0, keepdims=True)

    return _call(
        body, [pre, w_out, dxn], grid=(nc, t // tm),
        in_specs=[pl.BlockSpec((2, None, tm, FF_SHARD), lambda c, i: (0, c, i, 0)),
                  pl.BlockSpec((FF_SHARD, d), lambda c, i: (c, 0)), pl.BlockSpec((tm, d), lambda c, i: (i, 0))],
        out_specs=[pl.BlockSpec((None, 2, tm, FF_SHARD), lambda c, i: (c, 0, i, 0)),
                   pl.BlockSpec((FF_SHARD, d), lambda c, i: (c, 0)),
                   pl.BlockSpec((None, 2, 8, FF_SHARD), lambda c, i: (c, 0, 0, 0))],
        out_shape=[jax.ShapeDtypeStruct((nc, 2, t, FF_SHARD), BF16), jax.ShapeDtypeStruct((D_FF, d), F32),
                   jax.ShapeDtypeStruct((nc, 2, 8, FF_SHARD), F32)],
        name=f"ffn{layer}_bwd_act", sem=("parallel", "arbitrary"), carry=carry)


def _ffn_bwd_in(dhu, a, cw, w_in, layer, tm=1024, carry=None, norm=None):
    nc, _, t, _ = dhu.shape
    d = D_MODEL
    tm = min(tm, t)
    last_blk = t // 16 - 1
    n_norm = 0 if norm is None else 3

    def body(dh_ref, nx_ref, a_ref, cw_ref, win_ref, *rest):
        norm_refs, (da_ref, o_ref, dcw_ref), dg_refs = rest[:n_norm], rest[n_norm:n_norm + 3], rest[n_norm + 3:]
        i, s = pl.program_id(0), pl.program_id(1)

        @pl.when(s == 0)
        def _():
            o_ref[...] = jnp.zeros_like(o_ref)

        @pl.when((s == 0) & (i == 0))
        def _():
            dcw_ref[...] = jnp.zeros_like(dcw_ref)

        keep = jnp.where(i == t // tm - 1, 0.0, 1.0)
        cw = cw_ref[...]
        sums = [None] * 3
        for r0 in range(0, tm, ROW_CHUNK):
            rows = slice(r0, r0 + ROW_CHUNK)
            if r0 + ROW_CHUNK == tm:
                win = jnp.concatenate([dh_ref[rows, :].astype(F32), nx_ref[...].astype(F32) * keep], axis=0)
            else:
                win = dh_ref[r0:r0 + ROW_CHUNK + HALO, :].astype(F32)
            n = ROW_CHUNK + HALO
            taps = (pltpu.roll(win, n - 2, 0)[:ROW_CHUNK],
                    pltpu.roll(win, n - 1, 0)[:ROW_CHUNK],
                    win[:ROW_CHUNK])
            da = (cw[0:1, :] * taps[0] + cw[1:2, :] * taps[1] + cw[2:3, :] * taps[2]).astype(BF16)
            da_ref[rows, :] = da
            o_ref[rows, :] += _dot(da, win_ref[...], NT)
            af = a_ref[rows, :].astype(F32)
            parts = [jnp.sum(taps[k] * af, axis=0, keepdims=True) for k in range(3)]
            sums = [p if q is None else q + p for q, p in zip(sums, parts)]
        for k in range(3):
            dcw_ref[pl.ds(s, 1), k:k + 1, :] += sums[k][None]

        if norm is not None:
            x_ref, g_ref, dres_ref = norm_refs
            dg_ref = dg_refs[0]

            @pl.when((s == 0) & (i == 0))
            def _():
                dg_ref[...] = jnp.zeros_like(dg_ref)

            @pl.when(s == N_SHARDS - 1)
            def _():
                xf = x_ref[...]
                r = lax.rsqrt(jnp.mean(xf * xf, axis=-1, keepdims=True) + EPS)
                xhat = xf * r
                dh = o_ref[...]
                dg_ref[0:1, :] += jnp.sum(dh * xhat, axis=0, keepdims=True)
                gy = dh * g_ref[...]
                o_ref[...] = dres_ref[...] + r * (gy - xhat * jnp.mean(gy * xhat, axis=-1, keepdims=True))

    row = pl.BlockSpec((tm, d), lambda i, s: (i, 0))
    norm_args = [] if norm is None else list(norm)
    norm_specs = [] if norm is None else [row, pl.BlockSpec((1, d), lambda i, s: (0, 0)), row]
    return _call(
        body, [dhu, dhu, a, cw, w_in] + norm_args, grid=(t // tm, N_SHARDS),
        in_specs=[pl.BlockSpec((None, None, tm, FF_SHARD), lambda i, s: (s % nc, s // nc, i, 0)),
                  pl.BlockSpec((None, None, 16, FF_SHARD),
                               lambda i, s: (s % nc, s // nc, jnp.minimum((i + 1) * (tm // 16), last_blk), 0)),
                  pl.BlockSpec((None, tm, FF_SHARD), lambda i, s: (s, i, 0)),
                  pl.BlockSpec((None, 8, FF_SHARD), lambda i, s: (s, 0, 0)),
                  pl.BlockSpec((None, d, FF_SHARD), lambda i, s: (s, 0, 0))] + norm_specs,
        out_specs=[pl.BlockSpec((None, tm, FF_SHARD), lambda i, s: (s, i, 0)), row,
                   pl.BlockSpec((N_SHARDS, 8, FF_SHARD), lambda i, s: (0, 0, 0))]
        + ([] if norm is None else [pl.BlockSpec((8, d), lambda i, s: (0, 0))]),
        out_shape=[jax.ShapeDtypeStruct((N_SHARDS, t, FF_SHARD), BF16), jax.ShapeDtypeStruct((t, d), F32),
                   jax.ShapeDtypeStruct((N_SHARDS, 8, FF_SHARD), F32)]
        + ([] if norm is None else [jax.ShapeDtypeStruct((8, d), F32)]),
        name=f"ffn{layer}_bwd_in", sem=("arbitrary", "arbitrary"), carry=carry)


def _ffn_wgrad_in(hf, da, layer, carry=None):
    t, d = hf.shape
    return _mm(
        da, hf, pl.BlockSpec((None, t, FF_SHARD), lambda s, j, kk: (s, 0, 0)),
        pl.BlockSpec((t, d), lambda s, j, kk: (0, 0)),
        pl.BlockSpec((None, FF_SHARD, d), lambda s, j, kk: (s, 0, 0)),
        jax.ShapeDtypeStruct((N_SHARDS, FF_SHARD, d), F32), (N_SHARDS, 1, 1), TN, f"ffn{layer}_wgrad_in",
        carry=carry)


Q_PER_KV = N_Q_HEADS // N_KV_HEADS
GROUP_ROWS = Q_PER_KV * CHUNK


def _lane_half():
    return lax.broadcasted_iota(jnp.int32, (CHUNK, LANES), 1) < HEAD_DIM


def _fill_attn_bias(bias_s):
    tq = lax.broadcasted_iota(jnp.int32, (GROUP_ROWS, 2 * CHUNK), 0) & (CHUNK - 1)
    jk = lax.broadcasted_iota(jnp.int32, (GROUP_ROWS, 2 * CHUNK), 1)
    dist = tq + CHUNK - jk
    window = (dist >= 0) & (dist < CHUNK)
    distf = dist.astype(F32)
    for kvh in range(N_KV_HEADS):
        alibi = _per_head_column([-SLOPES[h] for h in range(Q_PER_KV * kvh, Q_PER_KV * (kvh + 1))]) * distf
        bias_s[0, kvh] = jnp.where(window & (jk >= CHUNK), alibi, NEG_BIG)
        bias_s[1, kvh] = jnp.where(window, alibi, NEG_BIG)


def _per_head_column(values):
    r = lax.broadcasted_iota(jnp.int32, (GROUP_ROWS, 1), 0)
    col = jnp.full((GROUP_ROWS, 1), values[Q_PER_KV - 1], F32)
    for j in range(Q_PER_KV - 2, -1, -1):
        col = jnp.where(r < (j + 1) * CHUNK, values[j], col)
    return col


def _half_sum(x, lo):
    s_lo = jnp.sum(jnp.where(lo, x, 0.0), axis=-1, keepdims=True)
    s_hi = jnp.sum(jnp.where(lo, 0.0, x), axis=-1, keepdims=True)
    return jnp.where(lo, s_lo, s_hi)


def _stack_heads(pairs, lo):
    zero = jnp.zeros_like(pairs[0])
    return jnp.concatenate([jnp.where(lo, pairs[0], zero), jnp.where(lo, zero, pairs[0]),
                            jnp.where(lo, pairs[1], zero), jnp.where(lo, zero, pairs[1])], axis=0)


def _unstack_heads(stacked, lo):
    return (jnp.where(lo, stacked[0:CHUNK], stacked[CHUNK:2 * CHUNK]),
            jnp.where(lo, stacked[2 * CHUNK:3 * CHUNK], stacked[3 * CHUNK:]))


def _attn_probs(qs, kn, bias, sink_col):
    s = _dot(qs, kn, NT) * (HEAD_DIM ** -0.5) + bias
    m = jnp.maximum(jnp.max(s, axis=-1, keepdims=True), sink_col)
    e = jnp.exp(s - m)
    den = jnp.sum(e, axis=-1, keepdims=True) + jnp.exp(sink_col - m)
    return e * (1.0 / den), m, den


def _attn_fwd(qraw, kvd, gq, gk, sinks, carry=None):
    t, d = qraw.shape
    nb = t // CHUNK

    def body(sink_ref, q_ref, cur_ref, prev_ref, gq_ref, gk_ref, o_ref, bias_s):
        n = pl.program_id(0)

        @pl.when(n == 0)
        def _():
            _fill_attn_bias(bias_s)

        lo = _lane_half()
        which = jnp.where(n == 0, 0, 1)
        gq_v, gk_v = gq_ref[...], gk_ref[...]
        for kvh in range(N_KV_HEADS):
            ks = slice(kvh * LANES, (kvh + 1) * LANES)
            vs = slice(4 * LANES + kvh * LANES, 4 * LANES + (kvh + 1) * LANES)
            kraw = jnp.concatenate([prev_ref[:, ks], cur_ref[:, ks]], axis=0)
            rk = lax.rsqrt(jnp.mean(kraw * kraw, axis=-1, keepdims=True) + EPS)
            kn = (kraw * rk * gk_v).astype(BF16)
            vv = jnp.concatenate([prev_ref[:, vs], cur_ref[:, vs]], axis=0).astype(BF16)
            qn = []
            for p in range(2):
                qp = q_ref[:, (2 * kvh + p) * LANES:(2 * kvh + p + 1) * LANES]
                r = lax.rsqrt(_half_sum(qp * qp, lo) * (1.0 / HEAD_DIM) + EPS)
                qn.append(qp * r * gq_v)
            heads = range(Q_PER_KV * kvh, Q_PER_KV * (kvh + 1))
            pf, _, _ = _attn_probs(_stack_heads(qn, lo).astype(BF16), kn, bias_s[which, kvh],
                                   _per_head_column([sink_ref[h] for h in heads]))
            for p, o_pair in enumerate(_unstack_heads(_dot(pf.astype(BF16), vv), lo)):
                o_ref[:, (2 * kvh + p) * LANES:(2 * kvh + p + 1) * LANES] = o_pair.astype(BF16)

    blk = lambda f: pl.BlockSpec((CHUNK, d), f)
    vec = pl.BlockSpec((1, LANES), lambda n: (0, 0))
    return _call(
        body, [sinks, qraw, kvd, kvd, gq, gk], grid=(nb,),
        in_specs=[pl.BlockSpec(memory_space=pltpu.SMEM), blk(lambda n: (n, 0)), blk(lambda n: (n, 0)),
                  blk(lambda n: (jnp.maximum(n - 1, 0), 0)), vec, vec],
        out_specs=[blk(lambda n: (n, 0))], out_shape=[jax.ShapeDtypeStruct((t, d), BF16)],
        scratch=[pltpu.VMEM((2, N_KV_HEADS, GROUP_ROWS, 2 * CHUNK), F32)], name="attn_fwd", sem=("arbitrary",),
        carry=carry)[0]


def _attn_bwd(qraw, kvd, d_o, gq, gk, sinks, carry=None):
    t, d = qraw.shape
    nb = t // CHUNK

    def body(sink_ref, q_ref, cur_ref, prev_ref, do_ref, gq_ref, gk_ref,
             dq_ref, dkv_ref, dsink_ref, dgq_ref, dgk_ref, carry_s, pp_s, cp_s, bias_s):
        n = pl.program_id(0)

        @pl.when(n == 0)
        def _():
            carry_s[...] = jnp.zeros_like(carry_s)
            dsink_ref[...] = jnp.zeros_like(dsink_ref)
            dgq_ref[...] = jnp.zeros_like(dgq_ref)
            dgk_ref[...] = jnp.zeros_like(dgk_ref)
            _fill_attn_bias(bias_s)

        @pl.when(n < nb)
        def _():
            lo = _lane_half()
            which = jnp.where(n == 0, 0, 1)
            gq_v, gk_v = gq_ref[...], gk_ref[...]
            for kvh in range(N_KV_HEADS):
                ks = slice(kvh * LANES, (kvh + 1) * LANES)
                vs = slice(4 * LANES + kvh * LANES, 4 * LANES + (kvh + 1) * LANES)
                kraw = jnp.concatenate([prev_ref[:, ks], cur_ref[:, ks]], axis=0)
                rk = lax.rsqrt(jnp.mean(kraw * kraw, axis=-1, keepdims=True) + EPS)
                khat = kraw * rk
                kn = (khat * gk_v).astype(BF16)
                vv = jnp.concatenate([prev_ref[:, vs], cur_ref[:, vs]], axis=0).astype(BF16)
                cols = [slice((2 * kvh + p) * LANES, (2 * kvh + p + 1) * LANES) for p in range(2)]
                rq, qhat = [], []
                for p in range(2):
                    qp = q_ref[:, cols[p]]
                    rq.append(lax.rsqrt(_half_sum(qp * qp, lo) * (1.0 / HEAD_DIM) + EPS))
                    qhat.append(qp * rq[p])
                heads = range(Q_PER_KV * kvh, Q_PER_KV * (kvh + 1))
                qs = _stack_heads([qhat[p] * gq_v for p in range(2)], lo).astype(BF16)
                dos = _stack_heads([do_ref[:, cols[p]] for p in range(2)], lo)
                sink_col = _per_head_column([sink_ref[h] for h in heads])
                pf, m, den = _attn_probs(qs, kn, bias_s[which, kvh], sink_col)
                dp = _dot(dos, vv, NT)
                delta = jnp.sum(pf * dp, axis=-1, keepdims=True)
                sink_delta = jnp.exp(sink_col - m) / den * delta
                for j, h in enumerate(heads):
                    dsink_ref[h:h + 1, :] -= jnp.broadcast_to(
                        jnp.sum(sink_delta[j * CHUNK:(j + 1) * CHUNK], axis=0, keepdims=True), (1, LANES))
                ds = (pf * (dp - delta) * (HEAD_DIM ** -0.5)).astype(BF16)
                dkn = _dot(ds, qs, TN)
                dvb = _dot(pf.astype(BF16), dos, TN)
                for p, dqn in enumerate(_unstack_heads(_dot(ds, kn), lo)):
                    dgq_ref[0:1, :] += jnp.sum(dqn * qhat[p], axis=0, keepdims=True)
                    gy = dqn * gq_v
                    mq = _half_sum(gy * qhat[p], lo) * (1.0 / HEAD_DIM)
                    dq_ref[:, cols[p]] = (rq[p] * (gy - qhat[p] * mq)).astype(BF16)
                dgk_ref[0:1, :] += jnp.sum(dkn * khat, axis=0, keepdims=True)
                gyk = dkn * gk_v
                dkraw = rk * (gyk - khat * jnp.mean(gyk * khat, axis=-1, keepdims=True))
                pp_s[:, ks] = dkraw[:CHUNK]
                cp_s[:, ks] = dkraw[CHUNK:]
                pp_s[:, vs] = dvb[:CHUNK]
                cp_s[:, vs] = dvb[CHUNK:]
            dkv_ref[...] = (carry_s[...] + pp_s[...]).astype(BF16)
            carry_s[...] = cp_s[...]

        @pl.when(n == nb)
        def _():
            dkv_ref[...] = carry_s[...].astype(BF16)

    blk = lambda f: pl.BlockSpec((CHUNK, d), f)
    vec = pl.BlockSpec((1, LANES), lambda n: (0, 0))
    cur = lambda n: (jnp.minimum(n, nb - 1), 0)
    prev = lambda n: (jnp.maximum(jnp.minimum(n, nb - 1) - 1, 0), 0)
    small = lambda r: pl.BlockSpec((r, LANES), lambda n: (0, 0))
    return _call(
        body, [sinks, qraw, kvd, kvd, d_o, gq, gk], grid=(nb + 1,),
        in_specs=[pl.BlockSpec(memory_space=pltpu.SMEM), blk(cur), blk(cur), blk(prev), blk(cur), vec, vec],
        out_specs=[blk(cur), blk(lambda n: (jnp.maximum(n - 1, 0), 0)), small(N_Q_HEADS), small(8), small(8)],
        out_shape=[jax.ShapeDtypeStruct((t, d), BF16), jax.ShapeDtypeStruct((t, d), BF16),
                   jax.ShapeDtypeStruct((N_Q_HEADS, LANES), F32), jax.ShapeDtypeStruct((8, LANES), F32),
                   jax.ShapeDtypeStruct((8, LANES), F32)],
        scratch=[pltpu.VMEM((CHUNK, d), F32)] * 3 + [pltpu.VMEM((2, N_KV_HEADS, GROUP_ROWS, 2 * CHUNK), F32)],
        name="attn_bwd", sem=("arbitrary",), carry=carry)


def _adamw_math(g, w, m, v):
    m = ADAM_B1 * m + (1.0 - ADAM_B1) * g
    v = ADAM_B2 * v + (1.0 - ADAM_B2) * (g * g)
    m_hat = m / (1.0 - ADAM_B1 ** ADAM_STEP)
    v_hat = v / (1.0 - ADAM_B2 ** ADAM_STEP)
    delta = -ADAM_LR * (m_hat / (jnp.sqrt(v_hat) + ADAM_EPS) + ADAM_WD * w)
    return delta, m, v


def _row_tile(r, cap=128):
    for tr in range(min(r, cap), 0, -1):
        if r % tr == 0 and (tr % 8 == 0 or tr == r):
            return tr
    return r


def _chip_sum(grad, recv, place, name, wire_dtype):
    _, r, c = grad.shape
    tr = _row_tile(r, 256)

    def body(pl_ref, g_ref, a_ref, p_ref):
        p_ref[...] = (g_ref[...] + a_ref[...]).astype(p_ref.dtype)

    other = lambda rel, pr: pr[0] ^ (rel + 1)
    return pl.pallas_call(
        body,
        grid_spec=pltpu.PrefetchScalarGridSpec(
            num_scalar_prefetch=1, grid=(3, r // tr),
            in_specs=[pl.BlockSpec((None, None, tr, c), lambda rel, i, pr: (other(rel, pr), pr[1], i, 0)),
                      pl.BlockSpec((None, tr, c), lambda rel, i, pr: (other(rel, pr), i, 0))],
            out_specs=pl.BlockSpec((None, tr, c), lambda rel, i, pr: (other(rel, pr), i, 0))),
        out_shape=jax.ShapeDtypeStruct((4, r, c), wire_dtype), name=name, compiler_params=_params(),
    )(place, grad.reshape(4, 2, r, c), recv)


def _adamw_sharded(grad, recv, others, place, w, m, v, name, layer=None, fill=None):
    r, c = w.shape[-2:]
    tr = _row_tile(r)

    def body(pl_ref, g_ref, a_ref, oth_ref, w_ref, m_ref, v_ref, *rest):
        g_out, d_out, nm_out, nv_out = rest[-4:]
        g = g_ref[...] + a_ref[...]
        for k in range(3):
            g = g + oth_ref[k].astype(F32)
        delta, nm, nv = _adamw_math(g, w_ref[...], m_ref[...], v_ref[...])
        g_out[...] = g
        d_out[...] = delta
        nm_out[...] = nm
        nv_out[...] = nv

    if layer is None:
        row = pl.BlockSpec((tr, c), lambda i, pr: (i, 0))
    else:
        row = pl.BlockSpec((None, tr, c), lambda i, pr: (layer, i, 0))
    n_fill = 0 if fill is None else 4
    in_specs = [pl.BlockSpec((None, None, tr, c), lambda i, pr: (pr[0], pr[1], i, 0)),
                pl.BlockSpec((None, tr, c), lambda i, pr: (pr[0], i, 0)),
                pl.BlockSpec((3, tr, c), lambda i, pr: (0, i, 0)), row, row, row]
    in_specs += [pl.BlockSpec(memory_space=pl.ANY)] * n_fill
    return pl.pallas_call(
        body,
        grid_spec=pltpu.PrefetchScalarGridSpec(
            num_scalar_prefetch=1, grid=(r // tr,), in_specs=in_specs, out_specs=[row] * 4),
        out_shape=[jax.ShapeDtypeStruct(w.shape, F32)] * 4, name=name, compiler_params=_params(),
        input_output_aliases={7 + j: j for j in range(n_fill)},
    )(place, grad.reshape(4, 2, r, c), recv, others, w, m, v, *([] if fill is None else fill))


def _sum_devices(parts, name):
    def body(p_ref, o_ref):
        total = p_ref[0]
        for k in range(1, N_SHARDS):
            total = total + p_ref[k]
        o_ref[...] = total

    return pl.pallas_call(body, out_shape=jax.ShapeDtypeStruct(parts.shape[1:], F32), name=name)(parts)


def _adamw_summed(parts, ws, ms, vs, name):
    n = len(parts)

    def body(*refs):
        p_refs, w_refs, m_refs, v_refs = refs[:n], refs[n:2 * n], refs[2 * n:3 * n], refs[3 * n:4 * n]
        o_refs = refs[4 * n:]
        for i in range(n):
            g = p_refs[i][0]
            for k in range(1, N_SHARDS):
                g = g + p_refs[i][k]
            delta, nm, nv = _adamw_math(g, w_refs[i][...], m_refs[i][...], v_refs[i][...])
            o_refs[4 * i][...] = g
            o_refs[4 * i + 1][...] = delta
            o_refs[4 * i + 2][...] = nm
            o_refs[4 * i + 3][...] = nv

    shapes = [jax.ShapeDtypeStruct(w.shape, F32) for w in ws for _ in range(4)]
    outs = pl.pallas_call(body, out_shape=shapes, name=name, compiler_params=_params())(*parts, *ws, *ms, *vs)
    return [outs[4 * i:4 * i + 4] for i in range(n)]


def _dup_heads(w):
    lead = w.shape[:-1]
    w4 = w.reshape(lead + (N_KV_HEADS, 1, HEAD_DIM))
    return jnp.broadcast_to(w4, lead + (N_KV_HEADS, 2, HEAD_DIM)).reshape(lead + (N_KV_HEADS * LANES,))


def _fold_heads(g):
    lead = g.shape[:-1]
    return g.reshape(lead + (N_KV_HEADS, 2, HEAD_DIM)).sum(axis=-2).reshape(lead + (N_KV_HEADS * HEAD_DIM,))


def kernel(x, a_norm, a_w_in, a_v_norm, a_w_s, a_b_s, a_w_out, f_norm, f_w_in, f_conv_w, f_conv_b, f_w_out, kv_norm, w_kv, k_norm, b_norm, b_w_q, b_q_norm, b_sinks, b_w_o, loss_target, m_a_norm, m_a_w_in, m_a_v_norm, m_a_w_s, m_a_b_s, m_a_w_out, m_f_norm, m_f_w_in, m_f_conv_w, m_f_conv_b, m_f_w_out, m_kv_norm, m_w_kv, m_k_norm, m_b_norm, m_b_w_q, m_b_q_norm, m_b_sinks, m_b_w_o, v_a_norm, v_a_w_in, v_a_v_norm, v_a_w_s, v_a_b_s, v_a_w_out, v_f_norm, v_f_w_in, v_f_conv_w, v_f_conv_b, v_f_w_out, v_kv_norm, v_w_kv, v_k_norm, v_b_norm, v_b_w_q, v_b_q_norm, v_b_sinks, v_b_w_o):
    d = D_MODEL
    xi, yi, ci = _coords()
    place = jnp.stack([2 * xi + yi, ci]).astype(jnp.int32)
    bf = lambda a: a.astype(BF16)
    row = lambda v_: v_.reshape(1, -1)
    x0, target = x[0], loss_target[0]
    t = x0.shape[0]
    res = {}

    red = {}

    def to_sibling(grads, wire=BF16):
        for k, g in grads.items():
            red[k] = dict(grad=g, wire=wire)
        ex = _ToSibling(list(grads.values()))
        ex.names = list(grads)
        return ex

    def to_chips(ex):
        for k, a in zip(ex.names, ex.results):
            red[k]["recv"] = a
            red[k]["psum"] = _chip_sum(red[k]["grad"], a, place, f"chip_sum_{k}", red[k]["wire"])
        nxt = _ToChips([red[k]["psum"] for k in ex.names])
        nxt.names = ex.names
        return nxt

    def landed(ex):
        for k, b in zip(ex.names, ex.results):
            red[k]["others"] = b

    def halves(ex, first_rows):
        parts = []
        for r0, nr in ((0, first_rows), (first_rows, ex.srcs[0].shape[1] - first_rows)):
            part = _ToChips(ex.srcs, rows=(r0, nr))
            part.names = ex.names
            parts.append(part)
        return parts

    def landed_halves(parts):
        for j, k in enumerate(parts[0].names):
            red[k]["others"] = jnp.concatenate([p.results[j] for p in parts], axis=1)

    def update(k, w, m, v, layer=None, fill=None):
        r = red[k]
        return _adamw_sharded(r["grad"], r["recv"], r["others"], place, w, m, v,
                              f"adamw_{k}", layer=layer, fill=fill)

    g_a_in, g_a_out, g_a_norm, g_a_v_norm, g_conv = _exchange_alone(
        _Gather([bf(a_w_in[0]), bf(a_w_out[0]), a_norm, a_v_norm, f_conv_w.reshape(6, FF_SHARD)]), "gather_first")
    a_norm_full, a_v_norm_full = g_a_norm.reshape(1, d), g_a_v_norm.reshape(1, d)
    conv_w = lax.reduce_precision(g_conv.reshape(N_SHARDS, 2, 3, FF_SHARD), 8, 7)
    cw = jnp.pad(jnp.transpose(conv_w, (1, 0, 2, 3)), ((0, 0), (0, 0), (0, 5), (0, 0)))
    w_a_in_flat = jnp.transpose(g_a_in, (1, 0, 2)).reshape(d, 2 * d)
    cb = f_conv_b.reshape(2, N_SHARDS, 1, FF_SHARD)
    tri = jnp.tril(jnp.ones((CHUNK, CHUNK), dtype=bool))
    w_causal = jnp.where(tri[None], a_w_s[0], 0.0).astype(BF16)
    w_causal_t = jnp.transpose(w_causal, (0, 2, 1))
    b_sb = jnp.broadcast_to(a_b_s[0][:, :, None], (N_GROUPS, CHUNK, CHUNK))
    w_a_out = g_a_out.reshape(d, d)
    gq = jnp.tile(b_q_norm.reshape(1, HEAD_DIM), (1, 2))
    gk = jnp.tile(k_norm.reshape(1, HEAD_DIM), (1, 2))
    sinks = b_sinks.reshape(N_Q_HEADS)

    ex = _Gather([bf(f_w_in[0]), bf(f_w_out[0])])
    zpre, x1, h1 = _sgu_fwd(x0, a_norm_full, g_a_in, a_v_norm_full, w_causal, b_sb, w_a_out, carry=ex)
    w_in0, w_out0 = ex.results[0], ex.results[1].reshape(D_FF, d)
    ex = _Gather([bf(w_kv), bf(b_w_q[0]), bf(b_w_o[0]), bf(f_w_in[1])], relay=False, early=True)
    x2, hf0, a0, pre0, hk, hq = _ffn_fwd(x1, f_norm[0:1], w_in0, cw[0], cb[0], w_out0, 0, carry=ex,
                                         next_gains=[row(kv_norm), b_norm])
    kv_full = ex.results[0].reshape(d, 2 * N_KV_HEADS * HEAD_DIM)
    w_q, w_o = ex.results[1].reshape(d, d), ex.results[2].reshape(d, d)
    w_in1 = ex.results[3]
    half = N_KV_HEADS * HEAD_DIM
    w_kv_dup = jnp.concatenate([_dup_heads(kv_full[:, :half]), _dup_heads(kv_full[:, half:])], axis=1)
    kvd, qraw = _mm_rows_pair(hk, w_kv_dup, hq, w_q, "kvq_proj")
    ex = _Gather([bf(f_w_out[1])], relay=False, early=True)
    o = _attn_fwd(qraw, kvd, gq, gk, sinks, carry=ex)
    w_out1 = ex.results[0].reshape(D_FF, d)
    x3 = _mm_rows(o, w_o, F32, "o_proj", res=x2)
    _, hf1, a1, pre1, dy, loss_lanes = _ffn_fwd(x3, f_norm[1:2], w_in1, cw[1], cb[1], w_out1, 1, loss_target=target)

    dhu1, dw_out1, dcb1 = _ffn_bwd_act(pre1, w_out1, dy, 1)
    ex = to_sibling({"f_w_out1": dw_out1.reshape(N_SHARDS, D_FF // N_SHARDS, d)})
    da1, dx3, dcw1, dgf1 = _ffn_bwd_in(dhu1, a1, cw[1], w_in1, 1, carry=ex, norm=(x3, f_norm[1:2], dy))
    ex = to_chips(ex)
    dw_in1 = _ffn_wgrad_in(hf1, da1, 1, carry=ex)
    landed(ex)
    ex = to_sibling({"f_w_in1": dw_in1})
    d_o = _mm_rows(dx3, w_o, BF16, "o_proj_bwd", trans_w=True, carry=ex)
    ex = to_chips(ex)
    dw_o = _mm_wgrad(o, dx3, "o_wgrad").reshape(N_SHARDS, d // N_SHARDS, d)
    dq, dkv, dsink, dgq, dgk = _attn_bwd(qraw, kvd, d_o, gq, gk, sinks, carry=ex)
    landed(ex)
    dw_q = _mm_wgrad(hq, dq, "q_wgrad").reshape(N_SHARDS, d // N_SHARDS, d)
    dw_kv_dup = _mm_wgrad(hk, dkv, "kv_wgrad")
    dw_kv = jnp.concatenate(
        [_fold_heads(dw_kv_dup[:, :4 * LANES]), _fold_heads(dw_kv_dup[:, 4 * LANES:])], axis=1
    ).reshape(N_SHARDS, d // N_SHARDS, 2 * N_KV_HEADS * HEAD_DIM)
    ex = to_sibling({"b_w_o": dw_o, "b_w_q": dw_q, "w_kv": dw_kv})
    dx2, dg2 = _rms_bwd(x2, [row(kv_norm), b_norm], [dkv, dq], dx3, "kvq_norm_bwd", tm=512, carry=ex,
                        through=[w_kv_dup, w_q])
    ex = to_chips(ex)
    dhu0, dw_out0, dcb0 = _ffn_bwd_act(pre0, w_out0, dx2, 0, carry=ex)
    landed(ex)
    ex = to_sibling({"f_w_out0": dw_out0.reshape(N_SHARDS, D_FF // N_SHARDS, d)})
    da0, dhf0, dcw0 = _ffn_bwd_in(dhu0, a0, cw[0], w_in0, 0, tm=2048, carry=ex)
    ex = to_chips(ex)
    dw_in0 = _ffn_wgrad_in(hf0, da0, 0, carry=ex)
    landed(ex)
    ex = to_sibling({"f_w_in0": dw_in0})
    dx1, dgf0 = _rms_bwd(x1, [f_norm[0:1]], [dhf0], dx2, "f0_norm_bwd", carry=ex)
    ex_lo, ex_hi = halves(to_chips(ex), 384)
    dz, y, dwc, dbs, dgv = _sgu_bwd(dx1, zpre, w_a_out, a_v_norm_full, w_causal, w_causal_t, b_sb, carry=ex_lo)
    dw_a_out = _mm_wgrad(y, dx1, "a_out_wgrad").reshape(N_SHARDS, d // N_SHARDS, d)
    nsub = g_a_in.shape[2]
    dw_a_in = _mm(
        h1, dz, pl.BlockSpec((t, d), lambda s, j, kk: (0, 0)), pl.BlockSpec((t, nsub), lambda s, j, kk: (0, s)),
        pl.BlockSpec((None, d, nsub), lambda s, j, kk: (s, 0, 0)), jax.ShapeDtypeStruct((N_SHARDS, d, nsub), F32),
        (N_SHARDS, 1, 1), TN, "a_in_wgrad", carry=ex_hi)
    landed_halves([ex_lo, ex_hi])

    def bias_grad(dcb):
        return jnp.transpose(dcb[:, :, 0, :], (1, 0, 2)).reshape(-1)

    g_conv_w = jnp.concatenate([dcw0[:, 0:3, :], dcw1[:, 0:3, :]], axis=1)
    g_a_v_norm = dgv[0].reshape(N_SHARDS, 1, LANES)
    rep = ["a_w_s", "a_b_s", "f_norm", "f_conv_b", "kv_norm", "k_norm", "b_norm", "b_q_norm", "b_sinks"]
    rep_g = dict(
        a_w_s=dwc.reshape(N_GROUPS * CHUNK, CHUNK), a_b_s=dbs[:, :, 0], f_norm=jnp.stack([dgf0[0], dgf1[0]]),
        f_conv_b=jnp.stack([bias_grad(dcb0), bias_grad(dcb1)]), kv_norm=dg2[0:1],
        k_norm=(dgk[0, :HEAD_DIM] + dgk[0, HEAD_DIM:])[None], b_norm=dg2[1:2],
        b_q_norm=(dgq[0, :HEAD_DIM] + dgq[0, HEAD_DIM:])[None], b_sinks=dsink[:, 0][None])
    ex_big = to_sibling({"a_w_out": dw_a_out, "a_w_in": dw_a_in})
    ex_small = to_sibling({"a_v_norm": g_a_v_norm, "f_conv_w": g_conv_w}, wire=F32)
    ex_rep = _Gather([rep_g[k] for k in rep] + [loss_lanes], relay=False)
    together = _Together([ex_big, ex_small, ex_rep])
    dh1 = _mm_rows(dz, w_a_in_flat, F32, "a_in_bwd", trans_w=True, carry=together)
    together.spread()
    ex_big, ex_small = to_chips(ex_big), to_chips(ex_small)
    together = _Together([ex_big, ex_small])
    grad_x, dg0 = _rms_bwd(x0, [a_norm_full], [dh1], dx1, "a_norm_bwd", carry=together)
    together.spread()
    landed(ex_big)
    landed(ex_small)
    (a_norm_parts,) = _exchange_alone(_ToOwners([dg0[0].reshape(N_SHARDS, 1, LANES)]), "a_norm_to_owners")

    res["f_w_out"] = update("f_w_out1", f_w_out, m_f_w_out, v_f_w_out, layer=1)
    w_in_t = [jnp.swapaxes(a_, 1, 2) for a_ in (f_w_in, m_f_w_in, v_f_w_in)]
    res["f_w_in"] = update("f_w_in1", *w_in_t, layer=1)
    res["b_w_o"] = update("b_w_o", b_w_o, m_b_w_o, v_b_w_o, layer=0)
    res["b_w_q"] = update("b_w_q", b_w_q, m_b_w_q, v_b_w_q, layer=0)
    res["w_kv"] = update("w_kv", w_kv, m_w_kv, v_w_kv)
    res["f_w_out"] = update("f_w_out0", f_w_out, m_f_w_out, v_f_w_out, layer=0, fill=res["f_w_out"])
    res["f_w_in"] = [jnp.swapaxes(o_, 1, 2) for o_ in update("f_w_in0", *w_in_t, layer=0, fill=res["f_w_in"])]
    res["a_w_out"] = update("a_w_out", a_w_out, m_a_w_out, v_a_w_out, layer=0)
    res["a_w_in"] = update("a_w_in", a_w_in, m_a_w_in, v_a_w_in, layer=0)
    res["a_v_norm"] = update("a_v_norm", a_v_norm, m_a_v_norm, v_a_v_norm)
    res["f_conv_w"] = [o_.reshape(f_conv_w.shape) for o_ in update(
        "f_conv_w", f_conv_w.reshape(6, FF_SHARD), m_f_conv_w.reshape(6, FF_SHARD), v_f_conv_w.reshape(6, FF_SHARD))]

    rep_w = dict(a_w_s=a_w_s, a_b_s=a_b_s, f_norm=f_norm, f_conv_b=f_conv_b, kv_norm=kv_norm, k_norm=k_norm,
                 b_norm=b_norm, b_q_norm=b_q_norm, b_sinks=b_sinks, a_norm=a_norm)
    rep_m = dict(a_w_s=m_a_w_s, a_b_s=m_a_b_s, f_norm=m_f_norm, f_conv_b=m_f_conv_b, kv_norm=m_kv_norm,
                 k_norm=m_k_norm, b_norm=m_b_norm, b_q_norm=m_b_q_norm, b_sinks=m_b_sinks, a_norm=m_a_norm)
    rep_v = dict(a_w_s=v_a_w_s, a_b_s=v_a_b_s, f_norm=v_f_norm, f_conv_b=v_f_conv_b, kv_norm=v_kv_norm,
                 k_norm=v_k_norm, b_norm=v_b_norm, b_q_norm=v_b_q_norm, b_sinks=v_b_sinks, a_norm=v_a_norm)
    keys = rep + ["a_norm"]
    loss = _sum_devices(ex_rep.results[-1], "loss_sum")[0, 0]
    parts = ex_rep.results[:-1] + [a_norm_parts]
    as2d = lambda a, p: a.reshape(p.shape[1:])
    rep_outs = _adamw_summed(parts, [as2d(rep_w[k], p) for k, p in zip(keys, parts)],
                             [as2d(rep_m[k], p) for k, p in zip(keys, parts)],
                             [as2d(rep_v[k], p) for k, p in zip(keys, parts)], "adamw_replicated")
    for j, key in enumerate(keys):
        res[key] = [o_.reshape(rep_w[key].shape) for o_ in rep_outs[j]]

    order = ["a_norm", "a_w_in", "a_v_norm", "a_w_s", "a_b_s", "a_w_out", "f_norm", "f_w_in", "f_conv_w", "f_conv_b",
             "f_w_out", "kv_norm", "w_kv", "k_norm", "b_norm", "b_w_q", "b_q_norm", "b_sinks", "b_w_o"]
    outs = [loss, grad_x[None]]
    for j in range(4):
        outs += [res[k][j] for k in order]
    return tuple(outs)
```
